```python
import jax, jax.numpy as jnp
from jax import lax
import numpy as np

D_MODEL = 2048
BATCH = 8
SEQ = 4096
DEPTH = 1

D_MIX = D_MODEL
D_A = D_MIX // 2
CHUNK = 128
A_GROUPS = 8
A_GROUP_W = D_A // A_GROUPS
D_B = D_MIX - D_A
HEAD_DIM = 64
N_Q_HEADS = D_B // HEAD_DIM
N_KV_HEADS = 4
Q_PER_KV = N_Q_HEADS // N_KV_HEADS
D_KV = N_KV_HEADS * HEAD_DIM
WINDOW = 128
BLOCK = WINDOW
ROPE_THETA = 10000.0
NORM_EPS = 1e-5
SPLIT_SIZES = (D_A, D_A, D_A, D_B, D_KV, D_KV, D_B)
D_IN = sum(SPLIT_SIZES)

kernel_name = "hymba_gmlp_swa_sink_adaln"


def rms_norm(x, g):
    xf = x.astype(jnp.float32)
    y = xf * lax.rsqrt(jnp.mean(xf * xf, axis=-1, keepdims=True) + NORM_EPS)
    return (y * g.astype(jnp.float32)).astype(x.dtype)


def layer_norm(x, g, b):
    xf = x.astype(jnp.float32)
    mu = jnp.mean(xf, axis=-1, keepdims=True)
    xc = xf - mu
    var = jnp.mean(xc * xc, axis=-1, keepdims=True)
    y = xc * lax.rsqrt(var + NORM_EPS)
    return (y * g.astype(jnp.float32) + b.astype(jnp.float32)).astype(x.dtype)


def modulate(h, shift, scale):
    return h * (1.0 + scale[:, None, :]) + shift[:, None, :]


def rope_tables(seq, dtype):
    inv_freq = ROPE_THETA ** (-jnp.arange(0, HEAD_DIM, 2, dtype=jnp.float32) / HEAD_DIM)
    ang = jnp.arange(seq, dtype=jnp.float32)[:, None] * inv_freq[None, :]
    return jnp.cos(ang).astype(dtype), jnp.sin(ang).astype(dtype)


def apply_rope(x, cos, sin):
    x1, x2 = jnp.split(x, 2, axis=-1)
    c = cos[None, :, None, :]
    s = sin[None, :, None, :]
    return jnp.concatenate([x1 * c - x2 * s, x2 * c + x1 * s], axis=-1)


def chunked_spatial_gating(u, v, ln_g, ln_b, w_s, b_s):
    bsz, seq, _ = v.shape
    n_chunks = seq // CHUNK
    v = layer_norm(v, ln_g, ln_b)
    vg = v.reshape(bsz, n_chunks, CHUNK, A_GROUPS, A_GROUP_W)
    w = w_s * jnp.tril(jnp.ones((CHUNK, CHUNK), w_s.dtype))[None]
    s = jnp.einsum('gts,bcsgd->bctgd', w, vg) + b_s.T[None, None, :, :, None]
    return u * s.reshape(bsz, seq, D_A)


def sliding_window_sink_attention(q, k, v, sinks):
    bsz, seq = q.shape[0], q.shape[1]
    nb = seq // BLOCK
    qb = q.reshape(bsz, nb, BLOCK, N_KV_HEADS, Q_PER_KV, HEAD_DIM)
    kb = k.reshape(bsz, nb, BLOCK, N_KV_HEADS, HEAD_DIM)
    vb = v.reshape(bsz, nb, BLOCK, N_KV_HEADS, HEAD_DIM)
    pad = ((0, 0), (1, 0), (0, 0), (0, 0), (0, 0))
    k_band = jnp.concatenate([jnp.pad(kb, pad)[:, :-1], kb], axis=2)
    v_band = jnp.concatenate([jnp.pad(vb, pad)[:, :-1], vb], axis=2)
    scale = HEAD_DIM ** -0.5
    scores = jnp.einsum('bnqkgd,bnjkd->bnkgqj', qb, k_band).astype(jnp.float32) * scale
    blk = jnp.arange(nb)[:, None]
    qpos = blk * BLOCK + jnp.arange(BLOCK)[None, :]
    kpos = (blk - 1) * BLOCK + jnp.arange(2 * BLOCK)[None, :]
    rel = qpos[:, :, None] - kpos[:, None, :]
    valid = (rel >= 0) & (rel < WINDOW) & (kpos[:, None, :] >= 0)
    scores = jnp.where(valid[None, :, None, None, :, :], scores, -jnp.inf)
    sink = sinks.astype(jnp.float32).reshape(N_KV_HEADS, Q_PER_KV)[None, None, :, :, None, None]
    m = jnp.maximum(jnp.max(scores, axis=-1, keepdims=True), sink)
    p = jnp.exp(scores - m)
    denom = jnp.sum(p, axis=-1, keepdims=True) + jnp.exp(sink - m)
    probs = (p / denom).astype(v.dtype)
    out = jnp.einsum('bnkgqj,bnjkd->bnqkgd', probs, v_band)
    return out.reshape(bsz, seq, N_Q_HEADS * HEAD_DIM)


def _fwd_setup_inputs(seed: int = 0) -> dict:
    key = jax.random.key(seed)
    ks = jax.random.split(key, 16)
    f32 = jnp.float32
    ada_std = 0.2 * D_MODEL ** -0.5
    return {
        "x": jax.random.normal(ks[0], (BATCH, SEQ, D_MODEL), f32),
        "c": jax.random.normal(ks[1], (BATCH, D_MODEL), f32),
        "w_ada": jax.random.normal(ks[2], (DEPTH, D_MODEL, 3 * D_MODEL), f32) * ada_std,
        "b_ada": jax.random.normal(ks[3], (DEPTH, 3 * D_MODEL), f32) * 0.02,
        "norm_g": 1.0 + 0.02 * jax.random.normal(ks[4], (DEPTH, D_MODEL), f32),
        "w_in": jax.random.normal(ks[5], (DEPTH, D_MODEL, D_IN), f32) * D_MODEL ** -0.5,
        "ln_v_g": 1.0 + 0.02 * jax.random.normal(ks[6], (DEPTH, D_A), f32),
        "ln_v_b": 0.02 * jax.random.normal(ks[7], (DEPTH, D_A), f32),
        "w_spatial": jax.random.normal(ks[8], (DEPTH, A_GROUPS, CHUNK, CHUNK), f32) * (0.5 * CHUNK ** -0.5),
        "b_spatial": 1.0 + 0.02 * jax.random.normal(ks[9], (DEPTH, A_GROUPS, CHUNK), f32),
        "sinks": 0.5 * jax.random.normal(ks[10], (DEPTH, N_Q_HEADS), f32),
        "w_out": jax.random.normal(ks[11], (DEPTH, D_MIX, D_MODEL), f32) * D_MIX ** -0.5,
        "w_ada_final": jax.random.normal(ks[12], (D_MODEL, 2 * D_MODEL), f32) * ada_std,
        "b_ada_final": jax.random.normal(ks[13], (2 * D_MODEL,), f32) * 0.02,
        "final_norm_g": 1.0 + 0.02 * jax.random.normal(ks[14], (D_MODEL,), f32),
    }


def _fwd_reference(x, c, w_ada, b_ada, norm_g, w_in, ln_v_g, ln_v_b, w_spatial, b_spatial,
              sinks, w_out, w_ada_final, b_ada_final, final_norm_g):
    bsz, seq, _ = x.shape
    c_act = jax.nn.silu(c)
    cos, sin = rope_tables(seq, x.dtype)
    offs = np.cumsum((0,) + SPLIT_SIZES)[:-1].tolist()[1:]
    for l in range(DEPTH):
        mod = c_act @ w_ada[l] + b_ada[l]
        shift, scale, gate = jnp.split(mod, 3, axis=-1)
        h = modulate(rms_norm(x, norm_g[l]), shift, scale)
        proj = h @ w_in[l]
        u_a, v_a, z_a, q, k, v, z_b = jnp.split(proj, offs, axis=-1)
        y_a = chunked_spatial_gating(u_a, v_a, ln_v_g[l], ln_v_b[l], w_spatial[l], b_spatial[l])
        q = apply_rope(q.reshape(bsz, seq, N_Q_HEADS, HEAD_DIM), cos, sin)
        k = apply_rope(k.reshape(bsz, seq, N_KV_HEADS, HEAD_DIM), cos, sin)
        v = v.reshape(bsz, seq, N_KV_HEADS, HEAD_DIM)
        y_b = sliding_window_sink_attention(q, k, v, sinks[l])
        y = jnp.concatenate([y_a * jax.nn.silu(z_a), y_b * jax.nn.silu(z_b)], axis=-1)
        x = x + gate[:, None, :] * (y @ w_out[l])
    mod_f = c_act @ w_ada_final + b_ada_final
    shift_f, scale_f = jnp.split(mod_f, 2, axis=-1)
    return modulate(rms_norm(x, final_norm_g), shift_f, scale_f)


import jax as _jax
import jax.numpy as _jnp

TWIN_FORMAT = 'train_step'
FWD_PARAMS = ['x', 'c', 'w_ada', 'b_ada', 'norm_g', 'w_in', 'ln_v_g', 'ln_v_b', 'w_spatial', 'b_spatial', 'sinks', 'w_out', 'w_ada_final', 'b_ada_final', 'final_norm_g']
TWIN_WEIGHTS = ['w_ada', 'b_ada', 'norm_g', 'w_in', 'ln_v_g', 'ln_v_b', 'w_spatial', 'b_spatial', 'sinks', 'w_out', 'w_ada_final', 'b_ada_final', 'final_norm_g']
TWIN_DIFF_INPUT = 'x'
TWIN_INPUTS = ['x', 'c', 'w_ada', 'b_ada', 'norm_g', 'w_in', 'ln_v_g', 'ln_v_b', 'w_spatial', 'b_spatial', 'sinks', 'w_out', 'w_ada_final', 'b_ada_final', 'final_norm_g', 'loss_target', 'm_w_ada', 'm_b_ada', 'm_norm_g', 'm_w_in', 'm_ln_v_g', 'm_ln_v_b', 'm_w_spatial', 'm_b_spatial', 'm_sinks', 'm_w_out', 'm_w_ada_final', 'm_b_ada_final', 'm_final_norm_g', 'v_w_ada', 'v_b_ada', 'v_norm_g', 'v_w_in', 'v_ln_v_g', 'v_ln_v_b', 'v_w_spatial', 'v_b_spatial', 'v_sinks', 'v_w_out', 'v_w_ada_final', 'v_b_ada_final', 'v_final_norm_g']
TWIN_OUTPUTS = ['loss', 'grad_x', 'grad_w_ada', 'grad_b_ada', 'grad_norm_g', 'grad_w_in', 'grad_ln_v_g', 'grad_ln_v_b', 'grad_w_spatial', 'grad_b_spatial', 'grad_sinks', 'grad_w_out', 'grad_w_ada_final', 'grad_b_ada_final', 'grad_final_norm_g', 'delta_w_ada', 'delta_b_ada', 'delta_norm_g', 'delta_w_in', 'delta_ln_v_g', 'delta_ln_v_b', 'delta_w_spatial', 'delta_b_spatial', 'delta_sinks', 'delta_w_out', 'delta_w_ada_final', 'delta_b_ada_final', 'delta_final_norm_g', 'new_m_w_ada', 'new_m_b_ada', 'new_m_norm_g', 'new_m_w_in', 'new_m_ln_v_g', 'new_m_ln_v_b', 'new_m_w_spatial', 'new_m_b_spatial', 'new_m_sinks', 'new_m_w_out', 'new_m_w_ada_final', 'new_m_b_ada_final', 'new_m_final_norm_g', 'new_v_w_ada', 'new_v_b_ada', 'new_v_norm_g', 'new_v_w_in', 'new_v_ln_v_g', 'new_v_ln_v_b', 'new_v_w_spatial', 'new_v_b_spatial', 'new_v_sinks', 'new_v_w_out', 'new_v_w_ada_final', 'new_v_b_ada_final', 'new_v_final_norm_g']
TWIN_LEAF_KINDS = {'loss': 'loss', 'grad_x': 'grad_x', 'grad_w_ada': 'grad_w', 'grad_b_ada': 'grad_w', 'grad_norm_g': 'grad_w', 'grad_w_in': 'grad_w', 'grad_ln_v_g': 'grad_w', 'grad_ln_v_b': 'grad_w', 'grad_w_spatial': 'grad_w', 'grad_b_spatial': 'grad_w', 'grad_sinks': 'grad_w', 'grad_w_out': 'grad_w', 'grad_w_ada_final': 'grad_w', 'grad_b_ada_final': 'grad_w', 'grad_final_norm_g': 'grad_w', 'delta_w_ada': 'delta_w', 'delta_b_ada': 'delta_w', 'delta_norm_g': 'delta_w', 'delta_w_in': 'delta_w', 'delta_ln_v_g': 'delta_w', 'delta_ln_v_b': 'delta_w', 'delta_w_spatial': 'delta_w', 'delta_b_spatial': 'delta_w', 'delta_sinks': 'delta_w', 'delta_w_out': 'delta_w', 'delta_w_ada_final': 'delta_w', 'delta_b_ada_final': 'delta_w', 'delta_final_norm_g': 'delta_w', 'new_m_w_ada': 'new_m', 'new_m_b_ada': 'new_m', 'new_m_norm_g': 'new_m', 'new_m_w_in': 'new_m', 'new_m_ln_v_g': 'new_m', 'new_m_ln_v_b': 'new_m', 'new_m_w_spatial': 'new_m', 'new_m_b_spatial': 'new_m', 'new_m_sinks': 'new_m', 'new_m_w_out': 'new_m', 'new_m_w_ada_final': 'new_m', 'new_m_b_ada_final': 'new_m', 'new_m_final_norm_g': 'new_m', 'new_v_w_ada': 'new_v', 'new_v_b_ada': 'new_v', 'new_v_norm_g': 'new_v', 'new_v_w_in': 'new_v', 'new_v_ln_v_g': 'new_v', 'new_v_ln_v_b': 'new_v', 'new_v_w_spatial': 'new_v', 'new_v_b_spatial': 'new_v', 'new_v_sinks': 'new_v', 'new_v_w_out': 'new_v', 'new_v_w_ada_final': 'new_v', 'new_v_b_ada_final': 'new_v', 'new_v_final_norm_g': 'new_v'}


def _forward(args):
    return _fwd_reference(*[args[k] for k in FWD_PARAMS])


def _output_shape():
    def fwd():
        inp = _fwd_setup_inputs(0)
        return _fwd_reference(*[inp[k] for k in FWD_PARAMS])
    out = _jax.eval_shape(fwd)
    return out.shape, out.dtype

N_MICROBATCH = 1
ADAM_LR = 0.001
ADAM_B1 = 0.9
ADAM_B2 = 0.999
ADAM_EPS = 1e-08
ADAM_WD = 0.01
ADAM_STEP = 10
PER_EXAMPLE_BATCH_AXIS = {'x': 0, 'c': 0, 'loss_target': 0}
SHARED_INPUTS = []
_WEIGHT_DTYPES = {'w_ada': _jnp.float32, 'b_ada': _jnp.float32, 'norm_g': _jnp.float32, 'w_in': _jnp.float32, 'ln_v_g': _jnp.float32, 'ln_v_b': _jnp.float32, 'w_spatial': _jnp.float32, 'b_spatial': _jnp.float32, 'sinks': _jnp.float32, 'w_out': _jnp.float32, 'w_ada_final': _jnp.float32, 'b_ada_final': _jnp.float32, 'final_norm_g': _jnp.float32}
MOMENT_SCALE = {'w_ada': 1.907238e-02, 'b_ada': 3.265742e-02, 'norm_g': 8.311569e-03, 'w_in': 4.808122e-03, 'ln_v_g': 2.354448e-03, 'ln_v_b': 2.451191e-03, 'w_spatial': 4.939088e-03, 'b_spatial': 7.335506e-03, 'sinks': 1.035193e-03, 'w_out': 5.501334e-03, 'w_ada_final': 3.231550e+00, 'b_ada_final': 1.135009e+01, 'final_norm_g': 1.634372e+01}


def _to_microbatches(a, axis):
    t = _jnp.moveaxis(a, axis, 0)
    t = t.reshape((N_MICROBATCH, t.shape[0] // N_MICROBATCH) + t.shape[1:])
    return _jnp.moveaxis(t, 1, axis + 1)


def setup_inputs(seed: int = 0) -> dict:
    inp = _fwd_setup_inputs(seed)
    key = _jax.random.fold_in(_jax.random.key(seed), 7919)
    shape, _ = _output_shape()
    out = dict(inp)
    out["loss_target"] = _jax.random.normal(_jax.random.fold_in(key, 0), shape, _jnp.float32)
    for i, name in enumerate(TWIN_WEIGHTS):
        w = inp[name].astype(_jnp.float32)
        if MOMENT_SCALE is None:
            s = _jnp.sqrt(_jnp.mean(_jnp.square(w)) + 1e-30)
        else:
            s = MOMENT_SCALE[name]
        km, kv = _jax.random.split(_jax.random.fold_in(key, i + 1))
        out[name] = w
        out["m_" + name] = s * _jax.random.normal(km, w.shape, _jnp.float32)
        out["v_" + name] = (s * s) * _jax.random.uniform(kv, w.shape, _jnp.float32, 0.5, 1.5)
    if N_MICROBATCH > 1:
        for name, axis in PER_EXAMPLE_BATCH_AXIS.items():
            out[name] = _to_microbatches(out[name], axis)
    return {'x': out['x'], 'c': out['c'], 'w_ada': out['w_ada'], 'b_ada': out['b_ada'], 'norm_g': out['norm_g'], 'w_in': out['w_in'], 'ln_v_g': out['ln_v_g'], 'ln_v_b': out['ln_v_b'], 'w_spatial': out['w_spatial'], 'b_spatial': out['b_spatial'], 'sinks': out['sinks'], 'w_out': out['w_out'], 'w_ada_final': out['w_ada_final'], 'b_ada_final': out['b_ada_final'], 'final_norm_g': out['final_norm_g'], 'loss_target': out['loss_target'], 'm_w_ada': out['m_w_ada'], 'm_b_ada': out['m_b_ada'], 'm_norm_g': out['m_norm_g'], 'm_w_in': out['m_w_in'], 'm_ln_v_g': out['m_ln_v_g'], 'm_ln_v_b': out['m_ln_v_b'], 'm_w_spatial': out['m_w_spatial'], 'm_b_spatial': out['m_b_spatial'], 'm_sinks': out['m_sinks'], 'm_w_out': out['m_w_out'], 'm_w_ada_final': out['m_w_ada_final'], 'm_b_ada_final': out['m_b_ada_final'], 'm_final_norm_g': out['m_final_norm_g'], 'v_w_ada': out['v_w_ada'], 'v_b_ada': out['v_b_ada'], 'v_norm_g': out['v_norm_g'], 'v_w_in': out['v_w_in'], 'v_ln_v_g': out['v_ln_v_g'], 'v_ln_v_b': out['v_ln_v_b'], 'v_w_spatial': out['v_w_spatial'], 'v_b_spatial': out['v_b_spatial'], 'v_sinks': out['v_sinks'], 'v_w_out': out['v_w_out'], 'v_w_ada_final': out['v_w_ada_final'], 'v_b_ada_final': out['v_b_ada_final'], 'v_final_norm_g': out['v_final_norm_g']}


def _loss(weights, diff, rest, loss_target):
    with _jax.named_scope("forward"):
        args = {**rest, TWIN_DIFF_INPUT: diff, **{k: w.astype(_WEIGHT_DTYPES[k]) for k, w in weights.items()}}
        y = _forward(args)
    with _jax.named_scope("loss_head"):
        err = _jnp.square(y.astype(_jnp.float32) - loss_target)
        return 0.5 * _jnp.sum(_jnp.mean(err, axis=-1)) if err.ndim else 0.5 * err


def _adamw(w, g, m, v):
    m = ADAM_B1 * m + (1.0 - ADAM_B1) * g
    v = ADAM_B2 * v + (1.0 - ADAM_B2) * _jnp.square(g)
    m_hat = m / (1.0 - ADAM_B1 ** ADAM_STEP)
    v_hat = v / (1.0 - ADAM_B2 ** ADAM_STEP)
    delta = -ADAM_LR * (m_hat / (_jnp.sqrt(v_hat) + ADAM_EPS) + ADAM_WD * w)
    return delta, m, v


def reference(x, c, w_ada, b_ada, norm_g, w_in, ln_v_g, ln_v_b, w_spatial, b_spatial, sinks, w_out, w_ada_final, b_ada_final, final_norm_g, loss_target, m_w_ada, m_b_ada, m_norm_g, m_w_in, m_ln_v_g, m_ln_v_b, m_w_spatial, m_b_spatial, m_sinks, m_w_out, m_w_ada_final, m_b_ada_final, m_final_norm_g, v_w_ada, v_b_ada, v_norm_g, v_w_in, v_ln_v_g, v_ln_v_b, v_w_spatial, v_b_spatial, v_sinks, v_w_out, v_w_ada_final, v_b_ada_final, v_final_norm_g):
    given = dict(x=x, c=c, w_ada=w_ada, b_ada=b_ada, norm_g=norm_g, w_in=w_in, ln_v_g=ln_v_g, ln_v_b=ln_v_b, w_spatial=w_spatial, b_spatial=b_spatial, sinks=sinks, w_out=w_out, w_ada_final=w_ada_final, b_ada_final=b_ada_final, final_norm_g=final_norm_g, loss_target=loss_target, m_w_ada=m_w_ada, m_b_ada=m_b_ada, m_norm_g=m_norm_g, m_w_in=m_w_in, m_ln_v_g=m_ln_v_g, m_ln_v_b=m_ln_v_b, m_w_spatial=m_w_spatial, m_b_spatial=m_b_spatial, m_sinks=m_sinks, m_w_out=m_w_out, m_w_ada_final=m_w_ada_final, m_b_ada_final=m_b_ada_final, m_final_norm_g=m_final_norm_g, v_w_ada=v_w_ada, v_b_ada=v_b_ada, v_norm_g=v_norm_g, v_w_in=v_w_in, v_ln_v_g=v_ln_v_g, v_ln_v_b=v_ln_v_b, v_w_spatial=v_w_spatial, v_b_spatial=v_b_spatial, v_sinks=v_sinks, v_w_out=v_w_out, v_w_ada_final=v_w_ada_final, v_b_ada_final=v_b_ada_final, v_final_norm_g=v_final_norm_g)
    weights = {n: given[n] for n in TWIN_WEIGHTS}
    shared = {n: given[n] for n in SHARED_INPUTS}
    per_example = {n: given[n] for n in ['x', 'c']}
    grad_fn = _jax.value_and_grad(_loss, argnums=(0, 1))

    def one_microbatch(ex, loss_target):
        ex = dict(ex)
        diff = ex.pop(TWIN_DIFF_INPUT)
        return grad_fn(weights, diff, {**shared, **ex}, loss_target)

    if N_MICROBATCH == 1:
        loss, (grad_w, grad_x) = one_microbatch(per_example, given["loss_target"])
    else:
        def body(carry, xs):
            loss_sum, grad_sum = carry
            l_k, (gw_k, gx_k) = one_microbatch(xs[0], xs[1])
            with _jax.named_scope("update"):
                return (loss_sum + l_k, _jax.tree.map(_jnp.add, grad_sum, gw_k)), gx_k

        init = (_jnp.zeros((), _jnp.float32), _jax.tree.map(_jnp.zeros_like, weights))
        (loss, grad_w), grad_x = _jax.lax.scan(body, init, (per_example, given["loss_target"]))
    with _jax.named_scope("update"):
        delta_w, new_m, new_v = {}, {}, {}
        for n in TWIN_WEIGHTS:
            delta_w[n], new_m[n], new_v[n] = _adamw(weights[n], grad_w[n], given["m_" + n], given["v_" + n])
    return (loss, grad_x, *[grad_w[n] for n in TWIN_WEIGHTS], *[delta_w[n] for n in TWIN_WEIGHTS],
            *[new_m[n] for n in TWIN_WEIGHTS], *[new_v[n] for n in TWIN_WEIGHTS])
```

```python
import jax
import jax.numpy as jnp
from jax import lax
from jax.experimental import pallas as pl
from jax.experimental.pallas import tpu as pltpu

F32 = jnp.float32
BF16 = jnp.bfloat16
MESH = pl.DeviceIdType.MESH

D = 2048
D_A = 1024
D_B = 1024
D_KV = 256
HEAD = 64
N_Q = 16
BLK = 128
GROUPS = 8
D_IN = 5632
OFF_Q, OFF_K, OFF_V, OFF_ZB = 3072, 4096, 4352, 4608
N_CHIP = 4
N_DEV = 8
W_IN_SHARD = D_IN // N_CHIP
W_OUT_SHARD = D // N_CHIP
EPS = 1e-5
SCALE = HEAD ** -0.5
NEG = -1e30
LANE = 128
VMEM_LIMIT = 56 * 1024 * 1024

ADAM_LR, ADAM_B1, ADAM_B2, ADAM_EPS, ADAM_WD, ADAM_STEP = 0.001, 0.9, 0.999, 1e-08, 0.01, 10
ADAM_C1 = 1.0 - ADAM_B1 ** ADAM_STEP
ADAM_C2 = 1.0 - ADAM_B2 ** ADAM_STEP

NT = (((1,), (1,)), ((), ()))
TN = (((0,), (0,)), ((), ()))


def _params(*sem):
    return pltpu.CompilerParams(dimension_semantics=sem, vmem_limit_bytes=VMEM_LIMIT)


def _silu_parts(z):
    sig = 1.0 / (1.0 + jnp.exp(-z))
    return z * sig, sig


def _rot_half(v, first_half):
    return jnp.where(first_half, -pltpu.roll(v, 96, 1), pltpu.roll(v, 32, 1))


def _lane_masks():
    lane = lax.broadcasted_iota(jnp.int32, (BLK, LANE), 1)
    return (lane % HEAD) < (HEAD // 2), lane < HEAD


def _band_valid(first_block_bound):
    rr = lax.broadcasted_iota(jnp.int32, (BLK, 2 * BLK), 0)
    jj = lax.broadcasted_iota(jnp.int32, (BLK, 2 * BLK), 1)
    return (jj > rr) & (jj <= rr + BLK) & (jj >= first_block_bound)


def _tril():
    t = lax.broadcasted_iota(jnp.int32, (BLK, BLK), 0)
    s = lax.broadcasted_iota(jnp.int32, (BLK, BLK), 1)
    return s <= t


def _expand_kv(slab, lo):
    rolled = pltpu.roll(slab, HEAD, 1)
    zero = jnp.zeros_like(slab)
    return (jnp.where(lo, slab, zero).astype(BF16), jnp.where(lo, zero, rolled).astype(BF16),
            jnp.where(lo, rolled, zero).astype(BF16), jnp.where(lo, zero, slab).astype(BF16))


def _layer_norm_fwd(va, lg, lb):
    mu = jnp.mean(va, axis=-1, keepdims=True)
    xc = va - mu
    rstd = lax.rsqrt(jnp.mean(xc * xc, axis=-1, keepdims=True) + EPS)
    vhat = xc * rstd
    return vhat, rstd, vhat * lg + lb


def _softmax_sink(qm, kexp, valid, sink):
    s = lax.dot_general(qm, kexp, NT, preferred_element_type=F32) * SCALE
    s = jnp.where(valid, s, NEG)
    m = jnp.maximum(jnp.max(s, axis=-1, keepdims=True), sink)
    p = jnp.exp(s - m)
    esink = jnp.exp(sink - m)
    den = jnp.sum(p, axis=-1, keepdims=True) + esink
    return p / den, esink / den


def _rowmat_call(c_all, w, b, name):
    n = w.shape[1]
    tn = 512

    def body(c_ref, w_ref, b_ref, o_ref, ca_ref):
        ca, _ = _silu_parts(c_ref[...])
        ca_ref[...] = ca
        o_ref[...] = jnp.dot(ca.astype(BF16), w_ref[...].astype(BF16), preferred_element_type=F32) + b_ref[...]

    return pl.pallas_call(
        body, name=name, grid=(n // tn,),
        in_specs=[pl.BlockSpec((N_DEV, D), lambda j: (0, 0)), pl.BlockSpec((D, tn), lambda j: (0, j)),
                  pl.BlockSpec((1, tn), lambda j: (0, j))],
        out_specs=[pl.BlockSpec((N_DEV, tn), lambda j: (0, j)), pl.BlockSpec((N_DEV, D), lambda j: (0, 0))],
        out_shape=[jax.ShapeDtypeStruct((N_DEV, n), F32), jax.ShapeDtypeStruct((N_DEV, D), F32)],
        compiler_params=_params("arbitrary"),
    )(c_all, w, b)


def _cast_call(w, name):
    r, n = w.shape
    tr = min(r, 512)

    def body(w_ref, o_ref):
        o_ref[...] = w_ref[...].astype(BF16)

    return pl.pallas_call(
        body, name=name, grid=(r // tr,),
        in_specs=[pl.BlockSpec((tr, n), lambda i: (i, 0))], out_specs=pl.BlockSpec((tr, n), lambda i: (i, 0)),
        out_shape=jax.ShapeDtypeStruct((r, n), BF16), compiler_params=_params("parallel"),
    )(w)


def _proj_call(x, shift, scale, norm_g, w_bf):
    s = x.shape[0]
    tm = min(s, 1024)
    tn = 512

    def body(x_ref, sh_ref, sc_ref, g_ref, w_ref, proj_ref, h_ref):
        @pl.when(pl.program_id(1) == 0)
        def _():
            xv = x_ref[...]
            r = lax.rsqrt(jnp.mean(xv * xv, axis=-1, keepdims=True) + EPS)
            h_ref[...] = ((xv * r * g_ref[...]) * (1.0 + sc_ref[...]) + sh_ref[...]).astype(BF16)

        proj_ref[...] = jnp.dot(h_ref[...], w_ref[...], preferred_element_type=F32)

    vec = pl.BlockSpec((1, D), lambda i, j: (0, 0))
    return pl.pallas_call(
        body, name="proj", grid=(s // tm, D_IN // tn),
        in_specs=[pl.BlockSpec((tm, D), lambda i, j: (i, 0)), vec, vec, vec, pl.BlockSpec((D, tn), lambda i, j: (0, j))],
        out_specs=[pl.BlockSpec((tm, tn), lambda i, j: (i, j)), pl.BlockSpec((tm, D), lambda i, j: (i, 0))],
        out_shape=[jax.ShapeDtypeStruct((s, D_IN), F32), jax.ShapeDtypeStruct((s, D), BF16)],
        compiler_params=_params("parallel", "arbitrary"),
    )(x, shift, scale, norm_g, w_bf)


def _proj_specs(rev_nb=None):
    if rev_nb is None:
        row = lambda i: i
    else:
        row = lambda i: rev_nb - 1 - i
    wide = lambda col: pl.BlockSpec((BLK, D_A), lambda i: (row(i), col))
    kv = lambda col: pl.BlockSpec((BLK, D_KV), lambda i: (row(i), col))
    half = lambda col: pl.BlockSpec((BLK, 512), lambda i: (row(i), col))
    return [wide(0), wide(1), wide(2), wide(3), kv(OFF_K // D_KV), kv(OFF_V // D_KV), half(OFF_ZB // 512), half(OFF_ZB // 512 + 1)]


def _mix_fwd_call(proj, cos, sin, ln_g, ln_b, w_sp, b_sp_t, sinks):
    s = proj.shape[0]
    nb = s // BLK

    def body(ua_ref, va_ref, za_ref, q_ref, k_ref, v_ref, zb0_ref, zb1_ref, cos_ref, sin_ref, lg_ref, lb_ref,
             w_ref, bt_ref, sinks_ref, y_ref, kexp_ref, vexp_ref):
        i = pl.program_id(0)
        first_half, lo = _lane_masks()
        cos_t = cos_ref[...]
        sin_t = sin_ref[...]

        _, _, vln = _layer_norm_fwd(va_ref[...], lg_ref[...], lb_ref[...])
        tril = _tril()
        for g in range(GROUPS):
            cols = slice(g * BLK, (g + 1) * BLK)
            wg = jnp.where(tril, w_ref[g], 0.0).astype(BF16)
            sg = jnp.dot(wg, vln[:, cols].astype(BF16), preferred_element_type=F32) + bt_ref[:, g:g + 1]
            gate, _ = _silu_parts(za_ref[:, cols])
            y_ref[:, cols] = (ua_ref[:, cols] * sg * gate).astype(BF16)

        @pl.when(i == 0)
        def _():
            kexp_ref[:, 0:BLK, :] = jnp.zeros((8, BLK, LANE), BF16)
            vexp_ref[:, 0:BLK, :] = jnp.zeros((8, BLK, LANE), BF16)

        @pl.when(i > 0)
        def _():
            kexp_ref[:, 0:BLK, :] = kexp_ref[:, BLK:2 * BLK, :]
            vexp_ref[:, 0:BLK, :] = vexp_ref[:, BLK:2 * BLK, :]

        for ks in range(2):
            cols = slice(ks * LANE, (ks + 1) * LANE)
            kslab = k_ref[:, cols]
            kr = kslab * cos_t + _rot_half(kslab, first_half) * sin_t
            for n, (ke, ve) in enumerate(zip(_expand_kv(kr, lo), _expand_kv(v_ref[:, cols], lo))):
                kexp_ref[4 * ks + n, BLK:2 * BLK, :] = ke
                vexp_ref[4 * ks + n, BLK:2 * BLK, :] = ve

        valid = _band_valid(jnp.where(i > 0, 0, BLK))
        for sb in range(8):
            cols = slice(sb * LANE, (sb + 1) * LANE)
            kh = sb // 2
            qslab = q_ref[:, cols]
            qr = qslab * cos_t + _rot_half(qslab, first_half) * sin_t
            acc = jnp.zeros((BLK, LANE), F32)
            for par in range(2):
                half = lo if par == 0 else jnp.logical_not(lo)
                qm = jnp.where(half, qr, 0.0).astype(BF16)
                probs, _ = _softmax_sink(qm, kexp_ref[2 * kh + par], valid, sinks_ref[2 * sb + par])
                acc = acc + jnp.dot(probs.astype(BF16), vexp_ref[2 * kh + par], preferred_element_type=F32)
            zb = zb0_ref[:, cols] if sb < 4 else zb1_ref[:, (sb - 4) * LANE:(sb - 3) * LANE]
            gate, _ = _silu_parts(zb)
            y_ref[:, D_A + sb * LANE:D_A + (sb + 1) * LANE] = (acc * gate).astype(BF16)

    tab = pl.BlockSpec((BLK, LANE), lambda i: (i, 0))
    return pl.pallas_call(
        body, name="mix_fwd", grid=(nb,),
        in_specs=_proj_specs() + [
            tab, tab, pl.BlockSpec((1, D_A), lambda i: (0, 0)), pl.BlockSpec((1, D_A), lambda i: (0, 0)),
            pl.BlockSpec((GROUPS, BLK, BLK), lambda i: (0, 0, 0)), pl.BlockSpec((BLK, GROUPS), lambda i: (0, 0)),
            pl.BlockSpec(memory_space=pltpu.SMEM)],
        out_specs=pl.BlockSpec((BLK, 2 * D_A), lambda i: (i, 0)),
        out_shape=jax.ShapeDtypeStruct((s, 2 * D_A), BF16),
        scratch_shapes=[pltpu.VMEM((8, 2 * BLK, LANE), BF16), pltpu.VMEM((8, 2 * BLK, LANE), BF16)],
        compiler_params=_params("arbitrary"),
    )(proj, proj, proj, proj, proj, proj, proj, proj, cos, sin, ln_g, ln_b, w_sp, b_sp_t, sinks)


def _tail_call(y, w_out_bf, x, target, gate, shift_f, scale_f, gf):
    s = x.shape[0]
    tm = min(s, 256)
    nsteps = s // tm

    def body(y_ref, w_ref, x_ref, t_ref, gate_ref, shf_ref, scf_ref, gf_ref, dx2_ref, do_ref, st_ref):
        i = pl.program_id(0)

        @pl.when(i == 0)
        def _():
            st_ref[...] = jnp.zeros((8, D), F32)

        o = jnp.dot(y_ref[...], w_ref[...], preferred_element_type=F32)
        gate_v = gate_ref[...]
        x2 = x_ref[...] + gate_v * o
        r2 = lax.rsqrt(jnp.mean(x2 * x2, axis=-1, keepdims=True) + EPS)
        xn2 = x2 * r2
        hn2 = xn2 * gf_ref[...]
        one_sc = 1.0 + scf_ref[...]
        err = hn2 * one_sc + shf_ref[...] - t_ref[...]
        dout = err * (1.0 / D)
        dhn2 = dout * one_sc
        dxn2 = dhn2 * gf_ref[...]
        dx2 = r2 * (dxn2 - xn2 * jnp.mean(dxn2 * xn2, axis=-1, keepdims=True))
        dx2_ref[...] = dx2
        do_ref[...] = (dx2 * gate_v).astype(BF16)
        st_ref[0:1, :] += jnp.sum(dout, axis=0, keepdims=True)
        st_ref[1:2, :] += jnp.sum(dout * hn2, axis=0, keepdims=True)
        st_ref[2:3, :] += jnp.sum(dhn2 * xn2, axis=0, keepdims=True)
        st_ref[3:4, :] += jnp.sum(dx2 * o, axis=0, keepdims=True)
        st_ref[4:5, :] += jnp.sum(err * err, axis=0, keepdims=True)

        @pl.when(i == nsteps - 1)
        def _():
            st_ref[5:6, :] = jnp.full((1, D), 0.5 / D, F32) * jnp.sum(st_ref[4:5, :])

    vec = pl.BlockSpec((1, D), lambda i: (0, 0))
    rows = lambda: pl.BlockSpec((tm, D), lambda i: (i, 0))
    return pl.pallas_call(
        body, name="tail", grid=(nsteps,),
        in_specs=[rows(), pl.BlockSpec((D, D), lambda i: (0, 0)), rows(), rows(), vec, vec, vec, vec],
        out_specs=[rows(), rows(), pl.BlockSpec((8, D), lambda i: (0, 0))],
        out_shape=[jax.ShapeDtypeStruct((s, D), F32), jax.ShapeDtypeStruct((s, D), BF16), jax.ShapeDtypeStruct((8, D), F32)],
        compiler_params=_params("arbitrary"),
    )(y, w_out_bf, x, target, gate, shift_f, scale_f, gf)


def _dy_call(do, w_out_bf):
    s = do.shape[0]
    tm = min(s, 512)

    def body(do_ref, w_ref, dy_ref):
        dy_ref[...] = lax.dot_general(do_ref[...], w_ref[...], NT, preferred_element_type=F32)

    return pl.pallas_call(
        body, name="dy", grid=(s // tm,),
        in_specs=[pl.BlockSpec((tm, D), lambda i: (i, 0)), pl.BlockSpec((D, D), lambda i: (0, 0))],
        out_specs=pl.BlockSpec((tm, D), lambda i: (i, 0)),
        out_shape=jax.ShapeDtypeStruct((s, D), F32), compiler_params=_params("parallel"),
    )(do, w_out_bf)


def _tn_call(a, b, name):
    s, m = a.shape
    n = b.shape[1]
    tn = 512
    ts = min(s, 1024)
    nk = s // ts

    def body(a_ref, b_ref, o_ref, acc_ref):
        k = pl.program_id(1)

        @pl.when(k == 0)
        def _():
            acc_ref[...] = jnp.zeros((m, tn), F32)

        acc_ref[...] += lax.dot_general(a_ref[...], b_ref[...], TN, preferred_element_type=F32)

        @pl.when(k == nk - 1)
        def _():
            o_ref[...] = acc_ref[...].astype(BF16)

    return pl.pallas_call(
        body, name=name, grid=(n // tn, nk),
        in_specs=[pl.BlockSpec((ts, m), lambda j, k: (k, 0)), pl.BlockSpec((ts, tn), lambda j, k: (k, j))],
        out_specs=pl.BlockSpec((m, tn), lambda j, k: (0, j)),
        out_shape=jax.ShapeDtypeStruct((m, n), BF16),
        scratch_shapes=[pltpu.VMEM((m, tn), F32)],
        compiler_params=_params("parallel", "arbitrary"),
    )(a, b)


def _mix_bwd_call(proj, dy, cos, sin, ln_g, ln_b, w_sp, w_sp_t, b_sp_t, sinks):
    s = proj.shape[0]
    nb = s // BLK
    rev = lambda i: nb - 1 - i
    prev = lambda i: jnp.maximum(nb - 2 - i, 0)

    def body(ua_ref, va_ref, za_ref, q_ref, k_ref, v_ref, zb0_ref, zb1_ref, kp_ref, vp_ref, dy_ref,
             cos_ref, sin_ref, cosp_ref, sinp_ref, lg_ref, lb_ref, w_ref, wt_ref, bt_ref, sinks_ref,
             dp_ref, lnst_ref, dw_ref, dbt_ref, dsink_ref,
             kexp_ref, vexp_ref, dvln_ref, dkacc_ref, dvacc_ref, kcar_ref, vcar_ref):
        i = pl.program_id(0)
        first_half, lo = _lane_masks()
        lane8 = lax.broadcasted_iota(jnp.int32, (8, LANE), 1)
        cos_t = cos_ref[...]
        sin_t = sin_ref[...]

        @pl.when(i == 0)
        def _():
            lnst_ref[...] = jnp.zeros((8, D_A), F32)
            dw_ref[...] = jnp.zeros((GROUPS, BLK, BLK), F32)
            dbt_ref[...] = jnp.zeros((BLK, LANE), F32)
            dsink_ref[...] = jnp.zeros((8, LANE), F32)
            kcar_ref[...] = jnp.zeros((BLK, D_KV), F32)
            vcar_ref[...] = jnp.zeros((BLK, D_KV), F32)

        vhat, rstd, vln = _layer_norm_fwd(va_ref[...], lg_ref[...], lb_ref[...])
        tril = _tril()
        triu = jnp.logical_not(tril) | (lax.broadcasted_iota(jnp.int32, (BLK, BLK), 0) == lax.broadcasted_iota(jnp.int32, (BLK, BLK), 1))
        lane_b = lax.broadcasted_iota(jnp.int32, (BLK, LANE), 1)
        db_acc = jnp.zeros((BLK, LANE), F32)
        for g in range(GROUPS):
            cols = slice(g * BLK, (g + 1) * BLK)
            vln_g = vln[:, cols].astype(BF16)
            wg = jnp.where(tril, w_ref[g], 0.0).astype(BF16)
            sg = jnp.dot(wg, vln_g, preferred_element_type=F32) + bt_ref[:, g:g + 1]
            za = za_ref[:, cols]
            gate, sig = _silu_parts(za)
            ua = ua_ref[:, cols]
            dya_g = dy_ref[:, cols]
            dya = dya_g * gate
            dp_ref[:, cols] = (dya * sg).astype(BF16)
            dp_ref[:, 2 * D_A + g * BLK:2 * D_A + (g + 1) * BLK] = (
                dya_g * (ua * sg) * (sig * (1.0 + za * (1.0 - sig)))).astype(BF16)
            ds = dya * ua
            ds_b = ds.astype(BF16)
            wtg = jnp.where(triu, wt_ref[g], 0.0).astype(BF16)
            dvln_ref[:, cols] = jnp.dot(wtg, ds_b, preferred_element_type=F32)
            dw_ref[g] += jnp.where(tril, lax.dot_general(ds_b, vln_g, NT, preferred_element_type=F32), 0.0)
            db_acc = db_acc + jnp.where(lane_b == g, jnp.sum(ds, axis=-1, keepdims=True), 0.0)
        dbt_ref[...] += db_acc
        dvln = dvln_ref[...]
        lnst_ref[0:1, :] += jnp.sum(dvln * vhat, axis=0, keepdims=True)
        lnst_ref[1:2, :] += jnp.sum(dvln, axis=0, keepdims=True)
        dvhat = dvln * lg_ref[...]
        m1 = jnp.mean(dvhat, axis=-1, keepdims=True)
        m2 = jnp.mean(dvhat * vhat, axis=-1, keepdims=True)
        dp_ref[:, D_A:2 * D_A] = (rstd * (dvhat - m1 - vhat * m2)).astype(BF16)

        cosp = cosp_ref[...]
        sinp = sinp_ref[...]
        for ks in range(2):
            cols = slice(ks * LANE, (ks + 1) * LANE)
            kslab = k_ref[:, cols]
            kr = kslab * cos_t + _rot_half(kslab, first_half) * sin_t
            kpslab = kp_ref[:, cols]
            kpr = kpslab * cosp + _rot_half(kpslab, first_half) * sinp
            for n, (kc, vc, kp, vp) in enumerate(zip(_expand_kv(kr, lo), _expand_kv(v_ref[:, cols], lo),
                                                     _expand_kv(kpr, lo), _expand_kv(vp_ref[:, cols], lo))):
                kexp_ref[4 * ks + n, BLK:2 * BLK, :] = kc
                vexp_ref[4 * ks + n, BLK:2 * BLK, :] = vc
                kexp_ref[4 * ks + n, 0:BLK, :] = kp
                vexp_ref[4 * ks + n, 0:BLK, :] = vp
        dkacc_ref[...] = jnp.zeros((4, 2 * BLK, LANE), F32)
        dvacc_ref[...] = jnp.zeros((4, 2 * BLK, LANE), F32)

        valid = _band_valid(jnp.where(i < nb - 1, 0, BLK))
        dsink_acc = jnp.zeros((8, LANE), F32)
        for sb in range(8):
            cols = slice(sb * LANE, (sb + 1) * LANE)
            kh = sb // 2
            qslab = q_ref[:, cols]
            qr = qslab * cos_t + _rot_half(qslab, first_half) * sin_t
            zb = zb0_ref[:, cols] if sb < 4 else zb1_ref[:, (sb - 4) * LANE:(sb - 3) * LANE]
            gate, sig = _silu_parts(zb)
            dyb = dy_ref[:, D_A + sb * LANE:D_A + (sb + 1) * LANE]
            d_o = dyb * gate
            o_acc = jnp.zeros((BLK, LANE), F32)
            dq_acc = jnp.zeros((BLK, LANE), F32)
            for par in range(2):
                h = 2 * sb + par
                half = lo if par == 0 else jnp.logical_not(lo)
                qm = jnp.where(half, qr, 0.0).astype(BF16)
                probs, psink = _softmax_sink(qm, kexp_ref[2 * kh + par], valid, sinks_ref[h])
                probs_b = probs.astype(BF16)
                o_h = jnp.dot(probs_b, vexp_ref[2 * kh + par], preferred_element_type=F32)
                o_acc = o_acc + o_h
                dom = jnp.where(half, d_o, 0.0)
                dom_b = dom.astype(BF16)
                delta = jnp.sum(dom * o_h, axis=-1, keepdims=True)
                dpr = lax.dot_general(dom_b, vexp_ref[2 * kh + par], NT, preferred_element_type=F32)
                dss = (probs * (dpr - delta) * SCALE).astype(BF16)
                dsink_acc = dsink_acc + jnp.where((lane8 == h), -jnp.sum(psink * delta), 0.0)
                dq_acc = dq_acc + jnp.dot(dss, kexp_ref[2 * kh + par], preferred_element_type=F32)
                dkacc_ref[kh] += lax.dot_general(dss, qm, TN, preferred_element_type=F32)
                dvacc_ref[kh] += lax.dot_general(probs_b, dom_b, TN, preferred_element_type=F32)
            dp_ref[:, OFF_ZB + sb * LANE:OFF_ZB + (sb + 1) * LANE] = (
                dyb * o_acc * (sig * (1.0 + zb * (1.0 - sig)))).astype(BF16)
            dp_ref[:, OFF_Q + sb * LANE:OFF_Q + (sb + 1) * LANE] = (
                dq_acc * cos_t - _rot_half(dq_acc * sin_t, first_half)).astype(BF16)
        row0 = lax.broadcasted_iota(jnp.int32, (8, LANE), 0) == 0
        dsink_ref[...] += jnp.where(row0, dsink_acc, 0.0)

        lo2 = lax.broadcasted_iota(jnp.int32, (2 * BLK, LANE), 1) < HEAD
        for ks in range(2):
            cols = slice(ks * LANE, (ks + 1) * LANE)
            ka = dkacc_ref[2 * ks]
            kb = dkacc_ref[2 * ks + 1]
            dk_band = jnp.where(lo2, ka + pltpu.roll(ka, HEAD, 1), kb + pltpu.roll(kb, HEAD, 1))
            va_ = dvacc_ref[2 * ks]
            vb_ = dvacc_ref[2 * ks + 1]
            dv_band = jnp.where(lo2, va_ + pltpu.roll(va_, HEAD, 1), vb_ + pltpu.roll(vb_, HEAD, 1))
            dkr = dk_band[BLK:2 * BLK, :] + kcar_ref[:, cols]
            dp_ref[:, OFF_K + ks * LANE:OFF_K + (ks + 1) * LANE] = (
                dkr * cos_t - _rot_half(dkr * sin_t, first_half)).astype(BF16)
            dp_ref[:, OFF_V + ks * LANE:OFF_V + (ks + 1) * LANE] = (
                dv_band[BLK:2 * BLK, :] + vcar_ref[:, cols]).astype(BF16)
            kcar_ref[:, cols] = dk_band[0:BLK, :]
            vcar_ref[:, cols] = dv_band[0:BLK, :]

    tab = pl.BlockSpec((BLK, LANE), lambda i: (rev(i), 0))
    tabp = pl.BlockSpec((BLK, LANE), lambda i: (prev(i), 0))
    kvp = lambda col: pl.BlockSpec((BLK, D_KV), lambda i: (prev(i), col))
    vec = pl.BlockSpec((1, D_A), lambda i: (0, 0))
    w3 = pl.BlockSpec((GROUPS, BLK, BLK), lambda i: (0, 0, 0))
    return pl.pallas_call(
        body, name="mix_bwd", grid=(nb,),
        in_specs=_proj_specs(nb) + [
            kvp(OFF_K // D_KV), kvp(OFF_V // D_KV), pl.BlockSpec((BLK, 2 * D_A), lambda i: (rev(i), 0)),
            tab, tab, tabp, tabp, vec, vec, w3, w3, pl.BlockSpec((BLK, GROUPS), lambda i: (0, 0)),
            pl.BlockSpec(memory_space=pltpu.SMEM)],
        out_specs=[pl.BlockSpec((BLK, D_IN), lambda i: (rev(i), 0)), pl.BlockSpec((8, D_A), lambda i: (0, 0)), w3,
                   pl.BlockSpec((BLK, LANE), lambda i: (0, 0)), pl.BlockSpec((8, LANE), lambda i: (0, 0))],
        out_shape=[jax.ShapeDtypeStruct((s, D_IN), BF16), jax.ShapeDtypeStruct((8, D_A), F32),
                   jax.ShapeDtypeStruct((GROUPS, BLK, BLK), F32), jax.ShapeDtypeStruct((BLK, LANE), F32),
                   jax.ShapeDtypeStruct((8, LANE), F32)],
        scratch_shapes=[pltpu.VMEM((8, 2 * BLK, LANE), BF16), pltpu.VMEM((8, 2 * BLK, LANE), BF16),
                        pltpu.VMEM((BLK, D_A), F32), pltpu.VMEM((4, 2 * BLK, LANE), F32),
                        pltpu.VMEM((4, 2 * BLK, LANE), F32), pltpu.VMEM((BLK, D_KV), F32), pltpu.VMEM((BLK, D_KV), F32)],
        compiler_params=_params("arbitrary"),
    )(proj, proj, proj, proj, proj, proj, proj, proj, proj, proj, dy, cos, sin, cos, sin, ln_g, ln_b, w_sp, w_sp_t,
      b_sp_t, sinks)


def _dh_call(dproj, w_bf, x, dx2, scale, norm_g):
    s = x.shape[0]
    tm = min(s, 512)
    tk = 512
    nk = D_IN // tk

    def body(dp_ref, w_ref, x_ref, dx2_ref, sc_ref, g_ref, gx_ref, st_ref, acc_ref):
        i = pl.program_id(0)
        k = pl.program_id(1)

        @pl.when((i == 0) & (k == 0))
        def _():
            st_ref[...] = jnp.zeros((8, D), F32)

        @pl.when(k == 0)
        def _():
            acc_ref[...] = jnp.zeros((tm, D), F32)

        acc_ref[...] += lax.dot_general(dp_ref[...], w_ref[...], NT, preferred_element_type=F32)

        @pl.when(k == nk - 1)
        def _():
            dh = acc_ref[...]
            xv = x_ref[...]
            r = lax.rsqrt(jnp.mean(xv * xv, axis=-1, keepdims=True) + EPS)
            xn = xv * r
            g = g_ref[...]
            dhn = dh * (1.0 + sc_ref[...])
            dxn = dhn * g
            gx_ref[...] = dx2_ref[...] + r * (dxn - xn * jnp.mean(dxn * xn, axis=-1, keepdims=True))
            st_ref[0:1, :] += jnp.sum(dh, axis=0, keepdims=True)
            st_ref[1:2, :] += jnp.sum(dh * (xn * g), axis=0, keepdims=True)
            st_ref[2:3, :] += jnp.sum(dhn * xn, axis=0, keepdims=True)

    vec = pl.BlockSpec((1, D), lambda i, k: (0, 0))
    rows = lambda: pl.BlockSpec((tm, D), lambda i, k: (i, 0))
    return pl.pallas_call(
        body, name="dh", grid=(s // tm, nk),
        in_specs=[pl.BlockSpec((tm, tk), lambda i, k: (i, k)), pl.BlockSpec((D, tk), lambda i, k: (0, k)), rows(), rows(), vec, vec],
        out_specs=[rows(), pl.BlockSpec((8, D), lambda i, k: (0, 0))],
        out_shape=[jax.ShapeDtypeStruct((s, D), F32), jax.ShapeDtypeStruct((8, D), F32)],
        scratch_shapes=[pltpu.VMEM((tm, D), F32)],
        compiler_params=_params("arbitrary", "arbitrary"),
    )(dproj, w_bf, x, dx2, scale, norm_g)


def _adam_math(w, g, m, v):
    m_new = ADAM_B1 * m + (1.0 - ADAM_B1) * g
    v_new = ADAM_B2 * v + (1.0 - ADAM_B2) * (g * g)
    m_hat = m_new / ADAM_C1
    v_hat = v_new / ADAM_C2
    delta = -ADAM_LR * (m_hat / (jnp.sqrt(v_hat) + ADAM_EPS) + ADAM_WD * w)
    return delta, m_new, v_new


def _adam_call(w, g, m, v, name):
    r, n = w.shape
    tr = r if r * n * 4 <= (1 << 20) else max(8, (1 << 20) // (n * 4) // 8 * 8)
    while r % tr:
        tr -= 8

    def body(w_ref, g_ref, m_ref, v_ref, d_ref, mo_ref, vo_ref):
        d_ref[...], mo_ref[...], vo_ref[...] = _adam_math(w_ref[...], g_ref[...], m_ref[...], v_ref[...])

    spec = lambda: pl.BlockSpec((tr, n), lambda i: (i, 0))
    return pl.pallas_call(
        body, name=name, grid=(r // tr,), in_specs=[spec() for _ in range(4)], out_specs=[spec() for _ in range(3)],
        out_shape=[jax.ShapeDtypeStruct((r, n), F32)] * 3, compiler_params=_params("parallel"),
    )(w, g, m, v)


def _adam_outer_call(w, ct, dm, m, v, name):
    r, n = w.shape
    tr = 128

    def body(w_ref, ct_ref, dm_ref, m_ref, v_ref, g_ref, d_ref, mo_ref, vo_ref):
        g = ct_ref[:, 0:1] * dm_ref[0:1, :]
        for b in range(1, N_DEV):
            g = g + ct_ref[:, b:b + 1] * dm_ref[b:b + 1, :]
        g_ref[...] = g
        d_ref[...], mo_ref[...], vo_ref[...] = _adam_math(w_ref[...], g, m_ref[...], v_ref[...])

    spec = lambda: pl.BlockSpec((tr, n), lambda i: (i, 0))
    return pl.pallas_call(
        body, name=name, grid=(r // tr,),
        in_specs=[spec(), pl.BlockSpec((tr, N_DEV), lambda i: (i, 0)), pl.BlockSpec((N_DEV, n), lambda i: (0, 0)), spec(), spec()],
        out_specs=[spec() for _ in range(4)],
        out_shape=[jax.ShapeDtypeStruct((r, n), F32)] * 4, compiler_params=_params("parallel"),
    )(w, ct, dm, m, v)


def _sum_slots_call(recv, name):
    _, r, n = recv.shape
    tr = min(r, 256)

    def body(r_ref, o_ref):
        acc = r_ref[0].astype(F32)
        for d in range(1, N_DEV):
            acc = acc + r_ref[d].astype(F32)
        o_ref[...] = acc

    return pl.pallas_call(
        body, name=name, grid=(r // tr,),
        in_specs=[pl.BlockSpec((N_DEV, tr, n), lambda i: (0, i, 0))], out_specs=pl.BlockSpec((tr, n), lambda i: (i, 0)),
        out_shape=jax.ShapeDtypeStruct((r, n), F32), compiler_params=_params("parallel"),
    )(recv)


def _coords():
    return lax.axis_index("x"), lax.axis_index("y"), lax.axis_index("c")


def _allgather_sum_call(blk, name, with_sum):
    m_per, n = blk.shape

    def body(x_ref, out_ref, *rest):
        if with_sum:
            sum_ref, send_sems, recv_sems, local_sem = rest
        else:
            send_sems, recv_sems, local_sem = rest
        x, y, c = _coords()
        me, sibling = (x, y, c), (x, y, 1 - c)
        chips = [(1 - x, y), (x, 1 - y), (1 - x, 1 - y)]

        def rows(px, py, pc):
            return out_ref.at[pl.ds((4 * px + 2 * py + pc) * m_per, m_per), :]

        def copy(k, block, to, src=None):
            return pltpu.make_async_remote_copy(
                src_ref=rows(*block) if src is None else src, dst_ref=rows(*block),
                send_sem=send_sems.at[k], recv_sem=recv_sems.at[k], device_id=to, device_id_type=MESH)

        mine = pltpu.make_async_copy(x_ref, rows(*me), local_sem)
        mine.start()
        first = [copy(0, me, sibling, src=x_ref)]
        first += [copy(1 + j, me, (*chip, c), src=x_ref) for j, chip in enumerate(chips)]
        for cp in first:
            cp.start()
        passed = [copy(4 + j, (*chip, c), sibling) for j, chip in enumerate(chips)]
        for j, chip in enumerate(chips):
            copy(1 + j, (*chip, c), me).wait_recv()
            passed[j].start()
        copy(0, sibling, me).wait_recv()
        for j, chip in enumerate(chips):
            copy(4 + j, (*chip, 1 - c), me).wait_recv()
        for cp in first + passed:
            cp.wait_send()
        mine.wait()
        if with_sum:
            acc = out_ref[0:m_per, :]
            for d in range(1, N_DEV):
                acc = acc + out_ref[d * m_per:(d + 1) * m_per, :]
            sum_ref[...] = acc

    vm = pl.BlockSpec(memory_space=pltpu.VMEM)
    out_shape = [jax.ShapeDtypeStruct((N_DEV * m_per, n), F32)]
    if with_sum:
        out_shape.append(jax.ShapeDtypeStruct((m_per, n), F32))
    return pl.pallas_call(
        body, name=name, out_shape=out_shape, in_specs=[vm], out_specs=[vm] * len(out_shape),
        scratch_shapes=[pltpu.SemaphoreType.DMA((7,)), pltpu.SemaphoreType.DMA((7,)), pltpu.SemaphoreType.DMA],
        compiler_params=pltpu.CompilerParams(vmem_limit_bytes=VMEM_LIMIT),
    )(blk)


def _weights_gather_call(wi_bf, wo_bf):
    hi = D // 2
    ho = W_OUT_SHARD // 2

    def body(wi_ref, wo_ref, fi_ref, fo_ref, send_sems, recv_sems, local_sems):
        x, y, c = _coords()
        sibling = (x, y, 1 - c)
        chips = [(1 - x, y), (x, 1 - y), (1 - x, 1 - y)]

        def wi_half(ref, px, py, pc, full):
            col = (2 * px + py) * W_IN_SHARD if full else 0
            return ref.at[pl.ds(pc * hi, hi), pl.ds(col, W_IN_SHARD)]

        def wo_half(ref, px, py, pc, full):
            row = (2 * px + py) * W_OUT_SHARD if full else 0
            return ref.at[pl.ds(row + pc * ho, ho), :]

        def copy(k, which, block, to, from_shard):
            half = wi_half if which == 0 else wo_half
            shard, full = (wi_ref, fi_ref) if which == 0 else (wo_ref, fo_ref)
            src = half(shard, *block, False) if from_shard else half(full, *block, True)
            return pltpu.make_async_remote_copy(
                src_ref=src, dst_ref=half(full, *block, True), send_sem=send_sems.at[k], recv_sem=recv_sems.at[k],
                device_id=to, device_id_type=MESH)

        local = [pltpu.make_async_copy(wi_ref, fi_ref.at[:, pl.ds((2 * x + y) * W_IN_SHARD, W_IN_SHARD)], local_sems.at[0]),
                 pltpu.make_async_copy(wo_ref, fo_ref.at[pl.ds((2 * x + y) * W_OUT_SHARD, W_OUT_SHARD), :], local_sems.at[1])]
        for cp in local:
            cp.start()
        first = [copy(6 * w + j, w, (x, y, c), (*chip, c), True) for w in range(2) for j, chip in enumerate(chips)]
        for cp in first:
            cp.start()
        passed = []
        for w in range(2):
            for j, chip in enumerate(chips):
                copy(6 * w + j, w, (*chip, c), (x, y, c), False).wait_recv()
                cp = copy(6 * w + 3 + j, w, (*chip, c), sibling, False)
                cp.start()
                passed.append(cp)
        for w in range(2):
            for j, chip in enumerate(chips):
                copy(6 * w + 3 + j, w, (*chip, 1 - c), (x, y, c), False).wait_recv()
        for cp in first + passed:
            cp.wait_send()
        for cp in local:
            cp.wait()

    anyspec = pl.BlockSpec(memory_space=pl.ANY)
    return pl.pallas_call(
        body, name="weights_gather",
        out_shape=[jax.ShapeDtypeStruct((D, D_IN), BF16), jax.ShapeDtypeStruct((D, D), BF16)],
        in_specs=[anyspec, anyspec], out_specs=[anyspec, anyspec],
        scratch_shapes=[pltpu.SemaphoreType.DMA((12,)), pltpu.SemaphoreType.DMA((12,)), pltpu.SemaphoreType.DMA((2,))],
    )(wi_bf, wo_bf)


def _grad_exchange_call(gwi, gwo):
    hi = D // 2
    ho = W_OUT_SHARD // 2

    def body(gwi_ref, gwo_ref, ri_ref, ro_ref, send_sems, recv_sems, local_sems):
        x, y, c = _coords()
        me = 4 * x + 2 * y + c

        def pieces(px, py, pc):
            j = 2 * px + py
            return (gwi_ref.at[pl.ds(pc * hi, hi), pl.ds(j * W_IN_SHARD, W_IN_SHARD)],
                    gwo_ref.at[pl.ds(j * W_OUT_SHARD + pc * ho, ho), :])

        def flip(v, bit):
            return 1 - v if bit else v

        sends = []
        for r in range(1, N_DEV):
            to = (flip(x, r & 4), flip(y, r & 2), flip(c, r & 1))
            pi, po = pieces(*to)
            for w, (src, dst) in enumerate(((pi, ri_ref.at[me]), (po, ro_ref.at[me]))):
                k = 2 * (r - 1) + w
                cp = pltpu.make_async_remote_copy(src_ref=src, dst_ref=dst, send_sem=send_sems.at[k],
                                                  recv_sem=recv_sems.at[k], device_id=to, device_id_type=MESH)
                cp.start()
                sends.append(cp)
        pi, po = pieces(x, y, c)
        local = [pltpu.make_async_copy(pi, ri_ref.at[me], local_sems.at[0]),
                 pltpu.make_async_copy(po, ro_ref.at[me], local_sems.at[1])]
        for cp in local:
            cp.start()
        for r in range(1, N_DEV):
            frm = 4 * flip(x, r & 4) + 2 * flip(y, r & 2) + flip(c, r & 1)
            for w, dst in enumerate((ri_ref, ro_ref)):
                k = 2 * (r - 1) + w
                pltpu.make_async_remote_copy(src_ref=dst.at[frm], dst_ref=dst.at[frm], send_sem=send_sems.at[k],
                                             recv_sem=recv_sems.at[k], device_id=(x, y, c), device_id_type=MESH).wait_recv()
        for cp in sends:
            cp.wait_send()
        for cp in local:
            cp.wait()

    anyspec = pl.BlockSpec(memory_space=pl.ANY)
    return pl.pallas_call(
        body, name="grad_exchange",
        out_shape=[jax.ShapeDtypeStruct((N_DEV, hi, W_IN_SHARD), BF16), jax.ShapeDtypeStruct((N_DEV, ho, D), BF16)],
        in_specs=[anyspec, anyspec], out_specs=[anyspec, anyspec],
        scratch_shapes=[pltpu.SemaphoreType.DMA((14,)), pltpu.SemaphoreType.DMA((14,)), pltpu.SemaphoreType.DMA((2,))],
    )(gwi, gwo)


def _pair_exchange_call(hwi, hwo):
    hi = D // 2
    ho = W_OUT_SHARD // 2

    def body(hwi_ref, hwo_ref, fi_ref, fo_ref, send_sems, recv_sems, local_sems):
        x, y, c = _coords()
        sibling = (x, y, 1 - c)
        mine_i, mine_o = fi_ref.at[pl.ds(c * hi, hi), :], fo_ref.at[pl.ds(c * ho, ho), :]
        theirs_i, theirs_o = fi_ref.at[pl.ds((1 - c) * hi, hi), :], fo_ref.at[pl.ds((1 - c) * ho, ho), :]
        local = [pltpu.make_async_copy(hwi_ref, mine_i, local_sems.at[0]), pltpu.make_async_copy(hwo_ref, mine_o, local_sems.at[1])]
        for cp in local:
            cp.start()
        sends = [pltpu.make_async_remote_copy(src_ref=src, dst_ref=dst, send_sem=send_sems.at[k], recv_sem=recv_sems.at[k],
                                              device_id=sibling, device_id_type=MESH)
                 for k, (src, dst) in enumerate(((hwi_ref, mine_i), (hwo_ref, mine_o)))]
        for cp in sends:
            cp.start()
        for k, dst in enumerate((theirs_i, theirs_o)):
            pltpu.make_async_remote_copy(src_ref=dst, dst_ref=dst, send_sem=send_sems.at[k], recv_sem=recv_sems.at[k],
                                         device_id=sibling, device_id_type=MESH).wait_recv()
        for cp in sends:
            cp.wait_send()
        for cp in local:
            cp.wait()

    anyspec = pl.BlockSpec(memory_space=pl.ANY)
    return pl.pallas_call(
        body, name="pair_exchange",
        out_shape=[jax.ShapeDtypeStruct((D, W_IN_SHARD), F32), jax.ShapeDtypeStruct((W_OUT_SHARD, D), F32)],
        in_specs=[anyspec, anyspec], out_specs=[anyspec, anyspec],
        scratch_shapes=[pltpu.SemaphoreType.DMA((2,)), pltpu.SemaphoreType.DMA((2,)), pltpu.SemaphoreType.DMA((2,))],
    )(hwi, hwo)


def _rope_tables(s):
    inv_freq = 10000.0 ** (-jnp.arange(0, HEAD, 2, dtype=F32) / HEAD)
    ang = jnp.arange(s, dtype=F32)[:, None] * inv_freq[None, :]
    return jnp.tile(jnp.cos(ang), (1, LANE // (HEAD // 2))), jnp.tile(jnp.sin(ang), (1, LANE // (HEAD // 2)))


def _pad_cols(a, n):
    return jnp.pad(a, ((0, 0), (0, n - a.shape[1])))


def kernel(x, c, w_ada, b_ada, norm_g, w_in, ln_v_g, ln_v_b, w_spatial, b_spatial, sinks, w_out, w_ada_final, b_ada_final, final_norm_g, loss_target, m_w_ada, m_b_ada, m_norm_g, m_w_in, m_ln_v_g, m_ln_v_b, m_w_spatial, m_b_spatial, m_sinks, m_w_out, m_w_ada_final, m_b_ada_final, m_final_norm_g, v_w_ada, v_b_ada, v_norm_g, v_w_in, v_ln_v_g, v_ln_v_b, v_w_spatial, v_b_spatial, v_sinks, v_w_out, v_w_ada_final, v_b_ada_final, v_final_norm_g):
    s = x.shape[1]
    ax, ay, ac = _coords()
    chip = 2 * ax + ay
    me = 4 * ax + 2 * ay + ac
    n_ada = w_ada.shape[2]
    n_adaf = w_ada_final.shape[1]

    x2d = x.reshape(s, D)
    tgt = loss_target.reshape(s, D)
    w_ada2, w_in2, w_out2 = w_ada[0], w_in[0], w_out[0]
    b_ada_f2 = b_ada_final.reshape(1, 2 * D)
    gf = final_norm_g.reshape(1, D)

    c_all = _allgather_sum_call(jnp.pad(c, ((0, 7), (0, 0))), "gather_c", False)[0][::8]
    mod_p, c_act = _rowmat_call(c_all, w_ada2, lax.dynamic_slice(b_ada, (0, chip * n_ada), (1, n_ada)), "mod")
    modf_p, _ = _rowmat_call(c_all, w_ada_final, lax.dynamic_slice(b_ada_f2, (0, chip * n_adaf), (1, n_adaf)), "mod_final")
    mods = _allgather_sum_call(jnp.concatenate([mod_p, modf_p], axis=1), "gather_mod", False)[0]
    my_rows = [lax.dynamic_slice(mods, (16 * j + me, 0), (1, n_ada + n_adaf)) for j in range(N_CHIP)]
    mod = jnp.concatenate([r[:, :n_ada] for r in my_rows], axis=1)
    mod_f = jnp.concatenate([r[:, n_ada:] for r in my_rows], axis=1)
    shift, scale, gate = mod[:, :D], mod[:, D:2 * D], mod[:, 2 * D:]
    shift_f, scale_f = mod_f[:, :D], mod_f[:, D:]

    w_in_bf, w_out_bf = _weights_gather_call(_cast_call(w_in2, "cast_w_in"), _cast_call(w_out2, "cast_w_out"))

    cos, sin = _rope_tables(s)
    b_sp_t = b_spatial[0].T
    sinks1 = sinks.reshape(N_Q)
    proj, h = _proj_call(x2d, shift, scale, norm_g, w_in_bf)
    y = _mix_fwd_call(proj, cos, sin, ln_v_g, ln_v_b, w_spatial[0], b_sp_t, sinks1)
    dx2, do, st_tail = _tail_call(y, w_out_bf, x2d, tgt, gate, shift_f, scale_f, gf)

    g_w_out_p = _tn_call(y, do, "grad_w_out")
    dy = _dy_call(do, w_out_bf)
    dproj, st_ln, d_wsp, d_bsp_t, d_sink = _mix_bwd_call(
        proj, dy, cos, sin, ln_v_g, ln_v_b, w_spatial[0], jnp.swapaxes(w_spatial[0], 1, 2), b_sp_t, sinks1)
    g_w_in_p = _tn_call(h, dproj, "grad_w_in")
    grad_x, st_dh = _dh_call(dproj, w_in_bf, x2d, dx2, scale, norm_g)

    recv_i, recv_o = _grad_exchange_call(g_w_in_p, g_w_out_p)
    g_w_in, g_w_out = _pair_exchange_call(_sum_slots_call(recv_i, "sum_w_in"), _sum_slots_call(recv_o, "sum_w_out"))

    pack = jnp.concatenate([
        d_wsp.reshape(64, D), st_tail, st_dh, _pad_cols(st_ln, D),
        _pad_cols(d_bsp_t[:, :GROUPS].T, D), _pad_cols(d_sink, D)], axis=0)
    rows = pack.shape[0]
    packs, tot = _allgather_sum_call(pack, "gather_small", True)
    packs = packs.reshape(N_DEV, rows, D)
    dmod_all = jnp.concatenate([packs[:, 72, :], packs[:, 73, :], packs[:, 67, :]], axis=1)
    dmodf_all = jnp.concatenate([packs[:, 64, :], packs[:, 65, :]], axis=1)
    loss = tot[69, 0]
    grads = {
        "b_ada": jnp.concatenate([tot[72:73], tot[73:74], tot[67:68]], axis=1),
        "norm_g": tot[74:75],
        "ln_v_g": tot[80:81, :D_A],
        "ln_v_b": tot[81:82, :D_A],
        "w_spatial": tot[0:64].reshape(GROUPS * BLK, BLK),
        "b_spatial": tot[88:96, :BLK],
        "sinks": tot[96:97, :N_Q],
        "b_ada_final": jnp.concatenate([tot[64:65], tot[65:66]], axis=1),
        "final_norm_g": tot[66:67],
        "w_in": g_w_in,
        "w_out": g_w_out,
    }

    weights = dict(w_ada=w_ada, b_ada=b_ada, norm_g=norm_g, w_in=w_in, ln_v_g=ln_v_g, ln_v_b=ln_v_b, w_spatial=w_spatial,
                   b_spatial=b_spatial, sinks=sinks, w_out=w_out, w_ada_final=w_ada_final, b_ada_final=b_ada_final,
                   final_norm_g=final_norm_g)
    m_in = dict(w_ada=m_w_ada, b_ada=m_b_ada, norm_g=m_norm_g, w_in=m_w_in, ln_v_g=m_ln_v_g, ln_v_b=m_ln_v_b,
                w_spatial=m_w_spatial, b_spatial=m_b_spatial, sinks=m_sinks, w_out=m_w_out, w_ada_final=m_w_ada_final,
                b_ada_final=m_b_ada_final, final_norm_g=m_final_norm_g)
    v_in = dict(w_ada=v_w_ada, b_ada=v_b_ada, norm_g=v_norm_g, w_in=v_w_in, ln_v_g=v_ln_v_g, ln_v_b=v_ln_v_b,
                w_spatial=v_w_spatial, b_spatial=v_b_spatial, sinks=v_sinks, w_out=v_w_out, w_ada_final=v_w_ada_final,
                b_ada_final=v_b_ada_final, final_norm_g=v_final_norm_g)
    c_act_t = c_act.T
    outer = {"w_ada": lax.dynamic_slice(dmod_all, (0, chip * n_ada), (N_DEV, n_ada)),
             "w_ada_final": lax.dynamic_slice(dmodf_all, (0, chip * n_adaf), (N_DEV, n_adaf))}
    out_g, out_d, out_m, out_v = [], [], [], []
    for name, w in weights.items():
        shape = w.shape
        if name in outer:
            shape2 = (D, outer[name].shape[1])
            g, dl, mn, vn = _adam_outer_call(w.reshape(shape2), c_act_t, outer[name], m_in[name].reshape(shape2),
                                             v_in[name].reshape(shape2), "adam_" + name)
        else:
            g = grads[name]
            shape2 = g.shape
            dl, mn, vn = _adam_call(w.reshape(shape2), g, m_in[name].reshape(shape2), v_in[name].reshape(shape2), "adam_" + name)
        out_g.append(g.reshape(shape))
        out_d.append(dl.reshape(shape))
        out_m.append(mn.reshape(shape))
        out_v.append(vn.reshape(shape))
    return (loss, grad_x.reshape(x.shape), *out_g, *out_d, *out_m, *out_v)
```

```python
import jax
import jax.numpy as jnp
from jax import lax
from jax.experimental import pallas as pl
from jax.experimental.pallas import tpu as pltpu

F32 = jnp.float32
BF16 = jnp.bfloat16
MESH = pl.DeviceIdType.MESH

D = 2048
D_A = 1024
D_B = 1024
D_KV = 256
HEAD = 64
N_Q = 16
BLK = 128
GROUPS = 8
D_IN = 5632
OFF_Q, OFF_K, OFF_V, OFF_ZB = 3072, 4096, 4352, 4608
N_CHIP = 4
N_DEV = 8
W_IN_SHARD = D_IN // N_CHIP
W_OUT_SHARD = D // N_CHIP
EPS = 1e-5
SCALE = HEAD ** -0.5
NEG = -1e30
LANE = 128
VMEM_LIMIT = 56 * 1024 * 1024

ADAM_LR, ADAM_B1, ADAM_B2, ADAM_EPS, ADAM_WD, ADAM_STEP = 0.001, 0.9, 0.999, 1e-08, 0.01, 10
ADAM_C1 = 1.0 - ADAM_B1 ** ADAM_STEP
ADAM_C2 = 1.0 - ADAM_B2 ** ADAM_STEP

NT = (((1,), (1,)), ((), ()))
TN = (((0,), (0,)), ((), ()))


def _params(*sem):
    return pltpu.CompilerParams(dimension_semantics=sem, vmem_limit_bytes=VMEM_LIMIT)


def _silu_parts(z):
    sig = 1.0 / (1.0 + jnp.exp(-z))
    return z * sig, sig


def _rot_half(v, first_half):
    return jnp.where(first_half, -pltpu.roll(v, 96, 1), pltpu.roll(v, 32, 1))


def _lane_masks():
    lane = lax.broadcasted_iota(jnp.int32, (BLK, LANE), 1)
    return (lane % HEAD) < (HEAD // 2), lane < HEAD


def _band_valid(first_block_bound):
    rr = lax.broadcasted_iota(jnp.int32, (BLK, 2 * BLK), 0)
    jj = lax.broadcasted_iota(jnp.int32, (BLK, 2 * BLK), 1)
    return (jj > rr) & (jj <= rr + BLK) & (jj >= first_block_bound)


def _tril():
    t = lax.broadcasted_iota(jnp.int32, (BLK, BLK), 0)
    s = lax.broadcasted_iota(jnp.int32, (BLK, BLK), 1)
    return s <= t


def _expand_kv(slab, lo):
    rolled = pltpu.roll(slab, HEAD, 1)
    zero = jnp.zeros_like(slab)
    return (jnp.where(lo, slab, zero).astype(BF16), jnp.where(lo, zero, rolled).astype(BF16),
            jnp.where(lo, rolled, zero).astype(BF16), jnp.where(lo, zero, slab).astype(BF16))


def _layer_norm_fwd(va, lg, lb):
    mu = jnp.mean(va, axis=-1, keepdims=True)
    xc = va - mu
    rstd = lax.rsqrt(jnp.mean(xc * xc, axis=-1, keepdims=True) + EPS)
    vhat = xc * rstd
    return vhat, rstd, vhat * lg + lb


def _softmax_sink(qm, kexp, valid, sink):
    s = lax.dot_general(qm, kexp, NT, preferred_element_type=F32) * SCALE
    s = jnp.where(valid, s, NEG)
    m = jnp.maximum(jnp.max(s, axis=-1, keepdims=True), sink)
    p = jnp.exp(s - m)
    esink = jnp.exp(sink - m)
    den = jnp.sum(p, axis=-1, keepdims=True) + esink
    return p / den, esink / den


def _rowmat_call(c_all, w, b, name):
    n = w.shape[1]
    tn = 512

    def body(c_ref, w_ref, b_ref, o_ref, ca_ref):
        ca, _ = _silu_parts(c_ref[...])
        ca_ref[...] = ca
        o_ref[...] = jnp.dot(ca.astype(BF16), w_ref[...].astype(BF16), preferred_element_type=F32) + b_ref[...]

    return pl.pallas_call(
        body, name=name, grid=(n // tn,),
        in_specs=[pl.BlockSpec((N_DEV, D), lambda j: (0, 0)), pl.BlockSpec((D, tn), lambda j: (0, j)),
                  pl.BlockSpec((1, tn), lambda j: (0, j))],
        out_specs=[pl.BlockSpec((N_DEV, tn), lambda j: (0, j)), pl.BlockSpec((N_DEV, D), lambda j: (0, 0))],
        out_shape=[jax.ShapeDtypeStruct((N_DEV, n), F32), jax.ShapeDtypeStruct((N_DEV, D), F32)],
        compiler_params=_params("arbitrary"),
    )(c_all, w, b)


def _cast_into_call(pos, w, full_shape, name):
    r, n = w.shape
    tr = min(r, 512)
    by_cols = full_shape[0] == r
    nrb = r // tr

    def body(pos_ref, w_ref, o_ref):
        o_ref[...] = w_ref[...].astype(BF16)

    out_map = (lambda i, pos: (i, pos[0])) if by_cols else (lambda i, pos: (pos[0] * nrb + i, 0))
    return pl.pallas_call(
        body, name=name,
        grid_spec=pltpu.PrefetchScalarGridSpec(
            num_scalar_prefetch=1, grid=(nrb,),
            in_specs=[pl.BlockSpec((tr, n), lambda i, pos: (i, 0))], out_specs=pl.BlockSpec((tr, n), out_map)),
        out_shape=jax.ShapeDtypeStruct(full_shape, BF16), compiler_params=_params("parallel"),
    )(pos, w)


def _proj_call(x, shift, scale, norm_g, w_bf):
    s = x.shape[0]
    tm = min(s, 1024)
    tn = 512

    def body(x_ref, sh_ref, sc_ref, g_ref, w_ref, proj_ref, h_ref):
        @pl.when(pl.program_id(1) == 0)
        def _():
            xv = x_ref[...]
            r = lax.rsqrt(jnp.mean(xv * xv, axis=-1, keepdims=True) + EPS)
            h_ref[...] = ((xv * r * g_ref[...]) * (1.0 + sc_ref[...]) + sh_ref[...]).astype(BF16)

        proj_ref[...] = jnp.dot(h_ref[...], w_ref[...], preferred_element_type=F32)

    vec = pl.BlockSpec((1, D), lambda i, j: (0, 0))
    return pl.pallas_call(
        body, name="proj", grid=(s // tm, D_IN // tn),
        in_specs=[pl.BlockSpec((tm, D), lambda i, j: (i, 0)), vec, vec, vec, pl.BlockSpec((D, tn), lambda i, j: (0, j))],
        out_specs=[pl.BlockSpec((tm, tn), lambda i, j: (i, j)), pl.BlockSpec((tm, D), lambda i, j: (i, 0))],
        out_shape=[jax.ShapeDtypeStruct((s, D_IN), F32), jax.ShapeDtypeStruct((s, D), BF16)],
        compiler_params=_params("parallel", "arbitrary"),
    )(x, shift, scale, norm_g, w_bf)


def _proj_specs(rev_nb=None):
    if rev_nb is None:
        row = lambda i: i
    else:
        row = lambda i: rev_nb - 1 - i
    wide = lambda col: pl.BlockSpec((BLK, D_A), lambda i: (row(i), col))
    kv = lambda col: pl.BlockSpec((BLK, D_KV), lambda i: (row(i), col))
    half = lambda col: pl.BlockSpec((BLK, 512), lambda i: (row(i), col))
    return [wide(0), wide(1), wide(2), wide(3), kv(OFF_K // D_KV), kv(OFF_V // D_KV), half(OFF_ZB // 512), half(OFF_ZB // 512 + 1)]


def _mix_fwd_call(proj, cos, sin, ln_g, ln_b, w_sp, b_sp_t, sinks):
    s = proj.shape[0]
    nb = s // BLK

    def body(ua_ref, va_ref, za_ref, q_ref, k_ref, v_ref, zb0_ref, zb1_ref, cos_ref, sin_ref, lg_ref, lb_ref,
             w_ref, bt_ref, sinks_ref, y_ref, kexp_ref, vexp_ref):
        i = pl.program_id(0)
        first_half, lo = _lane_masks()
        cos_t = cos_ref[...]
        sin_t = sin_ref[...]

        _, _, vln = _layer_norm_fwd(va_ref[...], lg_ref[...], lb_ref[...])
        tril = _tril()
        for g in range(GROUPS):
            cols = slice(g * BLK, (g + 1) * BLK)
            wg = jnp.where(tril, w_ref[g], 0.0).astype(BF16)
            sg = jnp.dot(wg, vln[:, cols].astype(BF16), preferred_element_type=F32) + bt_ref[:, g:g + 1]
            gate, _ = _silu_parts(za_ref[:, cols])
            y_ref[:, cols] = (ua_ref[:, cols] * sg * gate).astype(BF16)

        @pl.when(i == 0)
        def _():
            kexp_ref[:, 0:BLK, :] = jnp.zeros((8, BLK, LANE), BF16)
            vexp_ref[:, 0:BLK, :] = jnp.zeros((8, BLK, LANE), BF16)

        @pl.when(i > 0)
        def _():
            kexp_ref[:, 0:BLK, :] = kexp_ref[:, BLK:2 * BLK, :]
            vexp_ref[:, 0:BLK, :] = vexp_ref[:, BLK:2 * BLK, :]

        for ks in range(2):
            cols = slice(ks * LANE, (ks + 1) * LANE)
            kslab = k_ref[:, cols]
            kr = kslab * cos_t + _rot_half(kslab, first_half) * sin_t
            for n, (ke, ve) in enumerate(zip(_expand_kv(kr, lo), _expand_kv(v_ref[:, cols], lo))):
                kexp_ref[4 * ks + n, BLK:2 * BLK, :] = ke
                vexp_ref[4 * ks + n, BLK:2 * BLK, :] = ve

        valid = _band_valid(jnp.where(i > 0, 0, BLK))
        for sb in range(8):
            cols = slice(sb * LANE, (sb + 1) * LANE)
            kh = sb // 2
            qslab = q_ref[:, cols]
            qr = qslab * cos_t + _rot_half(qslab, first_half) * sin_t
            acc = jnp.zeros((BLK, LANE), F32)
            for par in range(2):
                half = lo if par == 0 else jnp.logical_not(lo)
                qm = jnp.where(half, qr, 0.0).astype(BF16)
                probs, _ = _softmax_sink(qm, kexp_ref[2 * kh + par], valid, sinks_ref[2 * sb + par])
                acc = acc + jnp.dot(probs.astype(BF16), vexp_ref[2 * kh + par], preferred_element_type=F32)
            zb = zb0_ref[:, cols] if sb < 4 else zb1_ref[:, (sb - 4) * LANE:(sb - 3) * LANE]
            gate, _ = _silu_parts(zb)
            y_ref[:, D_A + sb * LANE:D_A + (sb + 1) * LANE] = (acc * gate).astype(BF16)

    tab = pl.BlockSpec((BLK, LANE), lambda i: (i, 0))
    return pl.pallas_call(
        body, name="mix_fwd", grid=(nb,),
        in_specs=_proj_specs() + [
            tab, tab, pl.BlockSpec((1, D_A), lambda i: (0, 0)), pl.BlockSpec((1, D_A), lambda i: (0, 0)),
            pl.BlockSpec((GROUPS, BLK, BLK), lambda i: (0, 0, 0)), pl.BlockSpec((BLK, GROUPS), lambda i: (0, 0)),
            pl.BlockSpec(memory_space=pltpu.SMEM)],
        out_specs=pl.BlockSpec((BLK, 2 * D_A), lambda i: (i, 0)),
        out_shape=jax.ShapeDtypeStruct((s, 2 * D_A), BF16),
        scratch_shapes=[pltpu.VMEM((8, 2 * BLK, LANE), BF16), pltpu.VMEM((8, 2 * BLK, LANE), BF16)],
        compiler_params=_params("arbitrary"),
    )(proj, proj, proj, proj, proj, proj, proj, proj, cos, sin, ln_g, ln_b, w_sp, b_sp_t, sinks)


def _tail_call(y, w_out_bf, x, target, gate, shift_f, scale_f, gf):
    s = x.shape[0]
    tm = min(s, 256)
    nsteps = s // tm

    def body(y_ref, w_ref, x_ref, t_ref, gate_ref, shf_ref, scf_ref, gf_ref, dx2_ref, do_ref, st_ref):
        i = pl.program_id(0)

        @pl.when(i == 0)
        def _():
            st_ref[...] = jnp.zeros((8, D), F32)

        o = jnp.dot(y_ref[...], w_ref[...], preferred_element_type=F32)
        gate_v = gate_ref[...]
        x2 = x_ref[...] + gate_v * o
        r2 = lax.rsqrt(jnp.mean(x2 * x2, axis=-1, keepdims=True) + EPS)
        xn2 = x2 * r2
        hn2 = xn2 * gf_ref[...]
        one_sc = 1.0 + scf_ref[...]
        err = hn2 * one_sc + shf_ref[...] - t_ref[...]
        dout = err * (1.0 / D)
        dhn2 = dout * one_sc
        dxn2 = dhn2 * gf_ref[...]
        dx2 = r2 * (dxn2 - xn2 * jnp.mean(dxn2 * xn2, axis=-1, keepdims=True))
        dx2_ref[...] = dx2
        do_ref[...] = (dx2 * gate_v).astype(BF16)
        st_ref[0:1, :] += jnp.sum(dout, axis=0, keepdims=True)
        st_ref[1:2, :] += jnp.sum(dout * hn2, axis=0, keepdims=True)
        st_ref[2:3, :] += jnp.sum(dhn2 * xn2, axis=0, keepdims=True)
        st_ref[3:4, :] += jnp.sum(dx2 * o, axis=0, keepdims=True)
        st_ref[4:5, :] += jnp.sum(err * err, axis=0, keepdims=True)

        @pl.when(i == nsteps - 1)
        def _():
            st_ref[5:6, :] = jnp.full((1, D), 0.5 / D, F32) * jnp.sum(st_ref[4:5, :])

    vec = pl.BlockSpec((1, D), lambda i: (0, 0))
    rows = lambda: pl.BlockSpec((tm, D), lambda i: (i, 0))
    return pl.pallas_call(
        body, name="tail", grid=(nsteps,),
        in_specs=[rows(), pl.BlockSpec((D, D), lambda i: (0, 0)), rows(), rows(), vec, vec, vec, vec],
        out_specs=[rows(), rows(), pl.BlockSpec((8, D), lambda i: (0, 0))],
        out_shape=[jax.ShapeDtypeStruct((s, D), F32), jax.ShapeDtypeStruct((s, D), BF16), jax.ShapeDtypeStruct((8, D), F32)],
        compiler_params=_params("arbitrary"),
    )(y, w_out_bf, x, target, gate, shift_f, scale_f, gf)


def _dy_call(do, w_out_bf):
    s = do.shape[0]
    tm = min(s, 512)

    def body(do_ref, w_ref, dy_ref):
        dy_ref[...] = lax.dot_general(do_ref[...], w_ref[...], NT, preferred_element_type=F32)

    return pl.pallas_call(
        body, name="dy", grid=(s // tm,),
        in_specs=[pl.BlockSpec((tm, D), lambda i: (i, 0)), pl.BlockSpec((D, D), lambda i: (0, 0))],
        out_specs=pl.BlockSpec((tm, D), lambda i: (i, 0)),
        out_shape=jax.ShapeDtypeStruct((s, D), F32), compiler_params=_params("parallel"),
    )(do, w_out_bf)


def _tn_call(a, b, name):
    s, m = a.shape
    n = b.shape[1]
    tn = 512
    ts = min(s, 1024)
    nk = s // ts

    def body(a_ref, b_ref, o_ref, acc_ref):
        k = pl.program_id(1)

        @pl.when(k == 0)
        def _():
            acc_ref[...] = jnp.zeros((m, tn), F32)

        acc_ref[...] += lax.dot_general(a_ref[...], b_ref[...], TN, preferred_element_type=F32)

        @pl.when(k == nk - 1)
        def _():
            o_ref[...] = acc_ref[...].astype(BF16)

    return pl.pallas_call(
        body, name=name, grid=(n // tn, nk),
        in_specs=[pl.BlockSpec((ts, m), lambda j, k: (k, 0)), pl.BlockSpec((ts, tn), lambda j, k: (k, j))],
        out_specs=pl.BlockSpec((m, tn), lambda j, k: (0, j)),
        out_shape=jax.ShapeDtypeStruct((m, n), BF16),
        scratch_shapes=[pltpu.VMEM((m, tn), F32)],
        compiler_params=_params("parallel", "arbitrary"),
    )(a, b)


def _mix_bwd_call(proj, dy, cos, sin, ln_g, ln_b, w_sp, w_sp_t, b_sp_t, sinks):
    s = proj.shape[0]
    nb = s // BLK
    rev = lambda i: nb - 1 - i
    prev = lambda i: jnp.maximum(nb - 2 - i, 0)

    def body(ua_ref, va_ref, za_ref, q_ref, k_ref, v_ref, zb0_ref, zb1_ref, kp_ref, vp_ref, dy_ref,
             cos_ref, sin_ref, cosp_ref, sinp_ref, lg_ref, lb_ref, w_ref, wt_ref, bt_ref, sinks_ref,
             dp_ref, lnst_ref, dw_ref, dbt_ref, dsink_ref,
             kexp_ref, vexp_ref, dvln_ref, dkacc_ref, dvacc_ref, kcar_ref, vcar_ref):
        i = pl.program_id(0)
        first_half, lo = _lane_masks()
        lane8 = lax.broadcasted_iota(jnp.int32, (8, LANE), 1)
        cos_t = cos_ref[...]
        sin_t = sin_ref[...]

        @pl.when(i == 0)
        def _():
            lnst_ref[...] = jnp.zeros((8, D_A), F32)
            dw_ref[...] = jnp.zeros((GROUPS, BLK, BLK), F32)
            dbt_ref[...] = jnp.zeros((BLK, LANE), F32)
            dsink_ref[...] = jnp.zeros((8, LANE), F32)
            kcar_ref[...] = jnp.zeros((BLK, D_KV), F32)
            vcar_ref[...] = jnp.zeros((BLK, D_KV), F32)

        vhat, rstd, vln = _layer_norm_fwd(va_ref[...], lg_ref[...], lb_ref[...])
        tril = _tril()
        triu = jnp.logical_not(tril) | (lax.broadcasted_iota(jnp.int32, (BLK, BLK), 0) == lax.broadcasted_iota(jnp.int32, (BLK, BLK), 1))
        lane_b = lax.broadcasted_iota(jnp.int32, (BLK, LANE), 1)
        db_acc = jnp.zeros((BLK, LANE), F32)
        for g in range(GROUPS):
            cols = slice(g * BLK, (g + 1) * BLK)
            vln_g = vln[:, cols].astype(BF16)
            wg = jnp.where(tril, w_ref[g], 0.0).astype(BF16)
            sg = jnp.dot(wg, vln_g, preferred_element_type=F32) + bt_ref[:, g:g + 1]
            za = za_ref[:, cols]
            gate, sig = _silu_parts(za)
            ua = ua_ref[:, cols]
            dya_g = dy_ref[:, cols]
            dya = dya_g * gate
            dp_ref[:, cols] = (dya * sg).astype(BF16)
            dp_ref[:, 2 * D_A + g * BLK:2 * D_A + (g + 1) * BLK] = (
                dya_g * (ua * sg) * (sig * (1.0 + za * (1.0 - sig)))).astype(BF16)
            ds = dya * ua
            ds_b = ds.astype(BF16)
            wtg = jnp.where(triu, wt_ref[g], 0.0).astype(BF16)
            dvln_ref[:, cols] = jnp.dot(wtg, ds_b, preferred_element_type=F32)
            dw_ref[g] += jnp.where(tril, lax.dot_general(ds_b, vln_g, NT, preferred_element_type=F32), 0.0)
            db_acc = db_acc + jnp.where(lane_b == g, jnp.sum(ds, axis=-1, keepdims=True), 0.0)
        dbt_ref[...] += db_acc
        dvln = dvln_ref[...]
        lnst_ref[0:1, :] += jnp.sum(dvln * vhat, axis=0, keepdims=True)
        lnst_ref[1:2, :] += jnp.sum(dvln, axis=0, keepdims=True)
        dvhat = dvln * lg_ref[...]
        m1 = jnp.mean(dvhat, axis=-1, keepdims=True)
        m2 = jnp.mean(dvhat * vhat, axis=-1, keepdims=True)
        dp_ref[:, D_A:2 * D_A] = (rstd * (dvhat - m1 - vhat * m2)).astype(BF16)

        cosp = cosp_ref[...]
        sinp = sinp_ref[...]
        for ks in range(2):
            cols = slice(ks * LANE, (ks + 1) * LANE)
            kslab = k_ref[:, cols]
            kr = kslab * cos_t + _rot_half(kslab, first_half) * sin_t
            kpslab = kp_ref[:, cols]
            kpr = kpslab * cosp + _rot_half(kpslab, first_half) * sinp
            for n, (kc, vc, kp, vp) in enumerate(zip(_expand_kv(kr, lo), _expand_kv(v_ref[:, cols], lo),
                                                     _expand_kv(kpr, lo), _expand_kv(vp_ref[:, cols], lo))):
                kexp_ref[4 * ks + n, BLK:2 * BLK, :] = kc
                vexp_ref[4 * ks + n, BLK:2 * BLK, :] = vc
                kexp_ref[4 * ks + n, 0:BLK, :] = kp
                vexp_ref[4 * ks + n, 0:BLK, :] = vp
        dkacc_ref[...] = jnp.zeros((4, 2 * BLK, LANE), F32)
        dvacc_ref[...] = jnp.zeros((4, 2 * BLK, LANE), F32)

        valid = _band_valid(jnp.where(i < nb - 1, 0, BLK))
        dsink_acc = jnp.zeros((8, LANE), F32)
        for sb in range(8):
            cols = slice(sb * LANE, (sb + 1) * LANE)
            kh = sb // 2
            qslab = q_ref[:, cols]
            qr = qslab * cos_t + _rot_half(qslab, first_half) * sin_t
            zb = zb0_ref[:, cols] if sb < 4 else zb1_ref[:, (sb - 4) * LANE:(sb - 3) * LANE]
            gate, sig = _silu_parts(zb)
            dyb = dy_ref[:, D_A + sb * LANE:D_A + (sb + 1) * LANE]
            d_o = dyb * gate
            o_acc = jnp.zeros((BLK, LANE), F32)
            dq_acc = jnp.zeros((BLK, LANE), F32)
            for par in range(2):
                h = 2 * sb + par
                half = lo if par == 0 else jnp.logical_not(lo)
                qm = jnp.where(half, qr, 0.0).astype(BF16)
                probs, psink = _softmax_sink(qm, kexp_ref[2 * kh + par], valid, sinks_ref[h])
                probs_b = probs.astype(BF16)
                o_h = jnp.dot(probs_b, vexp_ref[2 * kh + par], preferred_element_type=F32)
                o_acc = o_acc + o_h
                dom = jnp.where(half, d_o, 0.0)
                dom_b = dom.astype(BF16)
                delta = jnp.sum(dom * o_h, axis=-1, keepdims=True)
                dpr = lax.dot_general(dom_b, vexp_ref[2 * kh + par], NT, preferred_element_type=F32)
                dss = (probs * (dpr - delta) * SCALE).astype(BF16)
                dsink_acc = dsink_acc + jnp.where((lane8 == h), -jnp.sum(psink * delta), 0.0)
                dq_acc = dq_acc + jnp.dot(dss, kexp_ref[2 * kh + par], preferred_element_type=F32)
                dkacc_ref[kh] += lax.dot_general(dss, qm, TN, preferred_element_type=F32)
                dvacc_ref[kh] += lax.dot_general(probs_b, dom_b, TN, preferred_element_type=F32)
            dp_ref[:, OFF_ZB + sb * LANE:OFF_ZB + (sb + 1) * LANE] = (
                dyb * o_acc * (sig * (1.0 + zb * (1.0 - sig)))).astype(BF16)
            dp_ref[:, OFF_Q + sb * LANE:OFF_Q + (sb + 1) * LANE] = (
                dq_acc * cos_t - _rot_half(dq_acc * sin_t, first_half)).astype(BF16)
        row0 = lax.broadcasted_iota(jnp.int32, (8, LANE), 0) == 0
        dsink_ref[...] += jnp.where(row0, dsink_acc, 0.0)

        lo2 = lax.broadcasted_iota(jnp.int32, (2 * BLK, LANE), 1) < HEAD
        for ks in range(2):
            cols = slice(ks * LANE, (ks + 1) * LANE)
            ka = dkacc_ref[2 * ks]
            kb = dkacc_ref[2 * ks + 1]
            dk_band = jnp.where(lo2, ka + pltpu.roll(ka, HEAD, 1), kb + pltpu.roll(kb, HEAD, 1))
            va_ = dvacc_ref[2 * ks]
            vb_ = dvacc_ref[2 * ks + 1]
            dv_band = jnp.where(lo2, va_ + pltpu.roll(va_, HEAD, 1), vb_ + pltpu.roll(vb_, HEAD, 1))
            dkr = dk_band[BLK:2 * BLK, :] + kcar_ref[:, cols]
            dp_ref[:, OFF_K + ks * LANE:OFF_K + (ks + 1) * LANE] = (
                dkr * cos_t - _rot_half(dkr * sin_t, first_half)).astype(BF16)
            dp_ref[:, OFF_V + ks * LANE:OFF_V + (ks + 1) * LANE] = (
                dv_band[BLK:2 * BLK, :] + vcar_ref[:, cols]).astype(BF16)
            kcar_ref[:, cols] = dk_band[0:BLK, :]
            vcar_ref[:, cols] = dv_band[0:BLK, :]

    tab = pl.BlockSpec((BLK, LANE), lambda i: (rev(i), 0))
    tabp = pl.BlockSpec((BLK, LANE), lambda i: (prev(i), 0))
    kvp = lambda col: pl.BlockSpec((BLK, D_KV), lambda i: (prev(i), col))
    vec = pl.BlockSpec((1, D_A), lambda i: (0, 0))
    w3 = pl.BlockSpec((GROUPS, BLK, BLK), lambda i: (0, 0, 0))
    return pl.pallas_call(
        body, name="mix_bwd", grid=(nb,),
        in_specs=_proj_specs(nb) + [
            kvp(OFF_K // D_KV), kvp(OFF_V // D_KV), pl.BlockSpec((BLK, 2 * D_A), lambda i: (rev(i), 0)),
            tab, tab, tabp, tabp, vec, vec, w3, w3, pl.BlockSpec((BLK, GROUPS), lambda i: (0, 0)),
            pl.BlockSpec(memory_space=pltpu.SMEM)],
        out_specs=[pl.BlockSpec((BLK, D_IN), lambda i: (rev(i), 0)), pl.BlockSpec((8, D_A), lambda i: (0, 0)), w3,
                   pl.BlockSpec((BLK, LANE), lambda i: (0, 0)), pl.BlockSpec((8, LANE), lambda i: (0, 0))],
        out_shape=[jax.ShapeDtypeStruct((s, D_IN), BF16), jax.ShapeDtypeStruct((8, D_A), F32),
                   jax.ShapeDtypeStruct((GROUPS, BLK, BLK), F32), jax.ShapeDtypeStruct((BLK, LANE), F32),
                   jax.ShapeDtypeStruct((8, LANE), F32)],
        scratch_shapes=[pltpu.VMEM((8, 2 * BLK, LANE), BF16), pltpu.VMEM((8, 2 * BLK, LANE), BF16),
                        pltpu.VMEM((BLK, D_A), F32), pltpu.VMEM((4, 2 * BLK, LANE), F32),
                        pltpu.VMEM((4, 2 * BLK, LANE), F32), pltpu.VMEM((BLK, D_KV), F32), pltpu.VMEM((BLK, D_KV), F32)],
        compiler_params=_params("arbitrary"),
    )(proj, proj, proj, proj, proj, proj, proj, proj, proj, proj, dy, cos, sin, cos, sin, ln_g, ln_b, w_sp, w_sp_t,
      b_sp_t, sinks)


def _dh_call(dproj, w_bf, x, dx2, scale, norm_g):
    s = x.shape[0]
    tm = min(s, 512)
    tk = 512
    nk = D_IN // tk

    def body(dp_ref, w_ref, x_ref, dx2_ref, sc_ref, g_ref, gx_ref, st_ref, acc_ref):
        i = pl.program_id(0)
        k = pl.program_id(1)

        @pl.when((i == 0) & (k == 0))
        def _():
            st_ref[...] = jnp.zeros((8, D), F32)

        @pl.when(k == 0)
        def _():
            acc_ref[...] = jnp.zeros((tm, D), F32)

        acc_ref[...] += lax.dot_general(dp_ref[...], w_ref[...], NT, preferred_element_type=F32)

        @pl.when(k == nk - 1)
        def _():
            dh = acc_ref[...]
            xv = x_ref[...]
            r = lax.rsqrt(jnp.mean(xv * xv, axis=-1, keepdims=True) + EPS)
            xn = xv * r
            g = g_ref[...]
            dhn = dh * (1.0 + sc_ref[...])
            dxn = dhn * g
            gx_ref[...] = dx2_ref[...] + r * (dxn - xn * jnp.mean(dxn * xn, axis=-1, keepdims=True))
            st_ref[0:1, :] += jnp.sum(dh, axis=0, keepdims=True)
            st_ref[1:2, :] += jnp.sum(dh * (xn * g), axis=0, keepdims=True)
            st_ref[2:3, :] += jnp.sum(dhn * xn, axis=0, keepdims=True)

    vec = pl.BlockSpec((1, D), lambda i, k: (0, 0))
    rows = lambda: pl.BlockSpec((tm, D), lambda i, k: (i, 0))
    return pl.pallas_call(
        body, name="dh", grid=(s // tm, nk),
        in_specs=[pl.BlockSpec((tm, tk), lambda i, k: (i, k)), pl.BlockSpec((D, tk), lambda i, k: (0, k)), rows(), rows(), vec, vec],
        out_specs=[rows(), pl.BlockSpec((8, D), lambda i, k: (0, 0))],
        out_shape=[jax.ShapeDtypeStruct((s, D), F32), jax.ShapeDtypeStruct((8, D), F32)],
        scratch_shapes=[pltpu.VMEM((tm, D), F32)],
        compiler_params=_params("arbitrary", "arbitrary"),
    )(dproj, w_bf, x, dx2, scale, norm_g)


def _adam_math(w, g, m, v):
    m_new = ADAM_B1 * m + (1.0 - ADAM_B1) * g
    v_new = ADAM_B2 * v + (1.0 - ADAM_B2) * (g * g)
    m_hat = m_new / ADAM_C1
    v_hat = v_new / ADAM_C2
    delta = -ADAM_LR * (m_hat / (jnp.sqrt(v_hat) + ADAM_EPS) + ADAM_WD * w)
    return delta, m_new, v_new


def _adam_call(w, g, m, v, name):
    r, n = w.shape
    tr = r if r * n * 4 <= (1 << 20) else max(8, (1 << 20) // (n * 4) // 8 * 8)
    while r % tr:
        tr -= 8

    def body(w_ref, g_ref, m_ref, v_ref, d_ref, mo_ref, vo_ref):
        d_ref[...], mo_ref[...], vo_ref[...] = _adam_math(w_ref[...], g_ref[...], m_ref[...], v_ref[...])

    spec = lambda: pl.BlockSpec((tr, n), lambda i: (i, 0))
    return pl.pallas_call(
        body, name=name, grid=(r // tr,), in_specs=[spec() for _ in range(4)], out_specs=[spec() for _ in range(3)],
        out_shape=[jax.ShapeDtypeStruct((r, n), F32)] * 3, compiler_params=_params("parallel"),
    )(w, g, m, v)


def _adam_outer_call(w, ct, dm, m, v, name):
    r, n = w.shape
    tr = 128

    def body(w_ref, ct_ref, dm_ref, m_ref, v_ref, g_ref, d_ref, mo_ref, vo_ref):
        g = ct_ref[:, 0:1] * dm_ref[0:1, :]
        for b in range(1, N_DEV):
            g = g + ct_ref[:, b:b + 1] * dm_ref[b:b + 1, :]
        g_ref[...] = g
        d_ref[...], mo_ref[...], vo_ref[...] = _adam_math(w_ref[...], g, m_ref[...], v_ref[...])

    spec = lambda: pl.BlockSpec((tr, n), lambda i: (i, 0))
    return pl.pallas_call(
        body, name=name, grid=(r // tr,),
        in_specs=[spec(), pl.BlockSpec((tr, N_DEV), lambda i: (i, 0)), pl.BlockSpec((N_DEV, n), lambda i: (0, 0)), spec(), spec()],
        out_specs=[spec() for _ in range(4)],
        out_shape=[jax.ShapeDtypeStruct((r, n), F32)] * 4, compiler_params=_params("parallel"),
    )(w, ct, dm, m, v)


def _sum_pieces_call(pos, part, recv, shard_shape, name):
    _, r, n = recv.shape
    tr = min(r, 256)
    nrb = r // tr
    by_cols = part.shape[1] != n

    def body(pos_ref, p_ref, r_ref, o_ref):
        acc = p_ref[...].astype(F32)
        for d in range(N_DEV - 1):
            acc = acc + r_ref[d].astype(F32)
        o_ref[...] = acc

    if by_cols:
        part_map = lambda i, pos: (pos[1] * nrb + i, pos[0])
    else:
        part_map = lambda i, pos: ((2 * pos[0] + pos[1]) * nrb + i, 0)
    return pl.pallas_call(
        body, name=name,
        grid_spec=pltpu.PrefetchScalarGridSpec(
            num_scalar_prefetch=1, grid=(nrb,),
            in_specs=[pl.BlockSpec((tr, n), part_map), pl.BlockSpec((N_DEV - 1, tr, n), lambda i, pos: (0, i, 0))],
            out_specs=pl.BlockSpec((tr, n), lambda i, pos: (pos[1] * nrb + i, 0))),
        out_shape=jax.ShapeDtypeStruct(shard_shape, F32), compiler_params=_params("parallel"),
    )(pos, part, recv)


def _coords():
    return lax.axis_index("x"), lax.axis_index("y"), lax.axis_index("c")


def _allgather_sum_call(blk, name, with_sum):
    m_per, n = blk.shape

    def body(x_ref, out_ref, *rest):
        if with_sum:
            sum_ref, send_sems, recv_sems, local_sem = rest
        else:
            send_sems, recv_sems, local_sem = rest
        x, y, c = _coords()
        me, sibling = (x, y, c), (x, y, 1 - c)
        chips = [(1 - x, y), (x, 1 - y), (1 - x, 1 - y)]

        def rows(px, py, pc):
            return out_ref.at[pl.ds((4 * px + 2 * py + pc) * m_per, m_per), :]

        def copy(k, block, to, src=None):
            return pltpu.make_async_remote_copy(
                src_ref=rows(*block) if src is None else src, dst_ref=rows(*block),
                send_sem=send_sems.at[k], recv_sem=recv_sems.at[k], device_id=to, device_id_type=MESH)

        mine = pltpu.make_async_copy(x_ref, rows(*me), local_sem)
        mine.start()
        first = [copy(0, me, sibling, src=x_ref)]
        first += [copy(1 + j, me, (*chip, c), src=x_ref) for j, chip in enumerate(chips)]
        for cp in first:
            cp.start()
        passed = [copy(4 + j, (*chip, c), sibling) for j, chip in enumerate(chips)]
        for j, chip in enumerate(chips):
            copy(1 + j, (*chip, c), me).wait_recv()
            passed[j].start()
        copy(0, sibling, me).wait_recv()
        for j, chip in enumerate(chips):
            copy(4 + j, (*chip, 1 - c), me).wait_recv()
        for cp in first + passed:
            cp.wait_send()
        mine.wait()
        if with_sum:
            acc = out_ref[0:m_per, :]
            for d in range(1, N_DEV):
                acc = acc + out_ref[d * m_per:(d + 1) * m_per, :]
            sum_ref[...] = acc

    vm = pl.BlockSpec(memory_space=pltpu.VMEM)
    out_shape = [jax.ShapeDtypeStruct((N_DEV * m_per, n), F32)]
    if with_sum:
        out_shape.append(jax.ShapeDtypeStruct((m_per, n), F32))
    return pl.pallas_call(
        body, name=name, out_shape=out_shape, in_specs=[vm], out_specs=[vm] * len(out_shape),
        scratch_shapes=[pltpu.SemaphoreType.DMA((7,)), pltpu.SemaphoreType.DMA((7,)), pltpu.SemaphoreType.DMA],
        compiler_params=pltpu.CompilerParams(vmem_limit_bytes=VMEM_LIMIT),
    )(blk)


def _weights_gather_call(wi_full, wo_full):
    hi = D // 2
    ho = W_OUT_SHARD // 2

    def body(wi_in, wo_in, fi_ref, fo_ref, send_sems, recv_sems):
        del wi_in, wo_in
        x, y, c = _coords()
        sibling = (x, y, 1 - c)
        chips = [(1 - x, y), (x, 1 - y), (1 - x, 1 - y)]

        def half(which, px, py, pc):
            j = 2 * px + py
            if which == 0:
                return fi_ref.at[pl.ds(pc * hi, hi), pl.ds(j * W_IN_SHARD, W_IN_SHARD)]
            return fo_ref.at[pl.ds(j * W_OUT_SHARD + pc * ho, ho), :]

        def copy(k, which, block, to):
            return pltpu.make_async_remote_copy(
                src_ref=half(which, *block), dst_ref=half(which, *block), send_sem=send_sems.at[k],
                recv_sem=recv_sems.at[k], device_id=to, device_id_type=MESH)

        first = [copy(6 * w + j, w, (x, y, c), (*chip, c)) for w in range(2) for j, chip in enumerate(chips)]
        for cp in first:
            cp.start()
        passed = []
        for w in range(2):
            for j, chip in enumerate(chips):
                copy(6 * w + j, w, (*chip, c), (x, y, c)).wait_recv()
                cp = copy(6 * w + 3 + j, w, (*chip, c), sibling)
                cp.start()
                passed.append(cp)
        for w in range(2):
            for j, chip in enumerate(chips):
                copy(6 * w + 3 + j, w, (*chip, 1 - c), (x, y, c)).wait_recv()
        for cp in first + passed:
            cp.wait_send()

    anyspec = pl.BlockSpec(memory_space=pl.ANY)
    return pl.pallas_call(
        body, name="weights_gather",
        out_shape=[jax.ShapeDtypeStruct((D, D_IN), BF16), jax.ShapeDtypeStruct((D, D), BF16)],
        in_specs=[anyspec, anyspec], out_specs=[anyspec, anyspec], input_output_aliases={0: 0, 1: 1},
        scratch_shapes=[pltpu.SemaphoreType.DMA((12,)), pltpu.SemaphoreType.DMA((12,))],
    )(wi_full, wo_full)


def _grad_exchange_call(gwi, gwo):
    hi = D // 2
    ho = W_OUT_SHARD // 2

    def body(gwi_ref, gwo_ref, ri_ref, ro_ref, send_sems, recv_sems):
        x, y, c = _coords()

        def pieces(px, py, pc):
            j = 2 * px + py
            return (gwi_ref.at[pl.ds(pc * hi, hi), pl.ds(j * W_IN_SHARD, W_IN_SHARD)],
                    gwo_ref.at[pl.ds(j * W_OUT_SHARD + pc * ho, ho), :])

        def flip(v, bit):
            return 1 - v if bit else v

        sends = []
        for r in range(1, N_DEV):
            to = (flip(x, r & 4), flip(y, r & 2), flip(c, r & 1))
            pi, po = pieces(*to)
            for w, (src, dst) in enumerate(((pi, ri_ref.at[r - 1]), (po, ro_ref.at[r - 1]))):
                k = 2 * (r - 1) + w
                cp = pltpu.make_async_remote_copy(src_ref=src, dst_ref=dst, send_sem=send_sems.at[k],
                                                  recv_sem=recv_sems.at[k], device_id=to, device_id_type=MESH)
                cp.start()
                sends.append(cp)
        for r in range(1, N_DEV):
            for w, dst in enumerate((ri_ref, ro_ref)):
                k = 2 * (r - 1) + w
                pltpu.make_async_remote_copy(src_ref=dst.at[r - 1], dst_ref=dst.at[r - 1], send_sem=send_sems.at[k],
                                             recv_sem=recv_sems.at[k], device_id=(x, y, c), device_id_type=MESH).wait_recv()
        for cp in sends:
            cp.wait_send()

    anyspec = pl.BlockSpec(memory_space=pl.ANY)
    return pl.pallas_call(
        body, name="grad_exchange",
        out_shape=[jax.ShapeDtypeStruct((N_DEV - 1, hi, W_IN_SHARD), BF16), jax.ShapeDtypeStruct((N_DEV - 1, ho, D), BF16)],
        in_specs=[anyspec, anyspec], out_specs=[anyspec, anyspec],
        scratch_shapes=[pltpu.SemaphoreType.DMA((14,)), pltpu.SemaphoreType.DMA((14,))],
    )(gwi, gwo)


def _pair_exchange_call(gi, go):
    hi = D // 2
    ho = W_OUT_SHARD // 2

    def body(gi_in, go_in, fi_ref, fo_ref, send_sems, recv_sems):
        del gi_in, go_in
        x, y, c = _coords()
        sibling = (x, y, 1 - c)
        mine = (fi_ref.at[pl.ds(c * hi, hi), :], fo_ref.at[pl.ds(c * ho, ho), :])
        theirs = (fi_ref.at[pl.ds((1 - c) * hi, hi), :], fo_ref.at[pl.ds((1 - c) * ho, ho), :])
        sends = [pltpu.make_async_remote_copy(src_ref=ref, dst_ref=ref, send_sem=send_sems.at[k], recv_sem=recv_sems.at[k],
                                              device_id=sibling, device_id_type=MESH) for k, ref in enumerate(mine)]
        for cp in sends:
            cp.start()
        for k, ref in enumerate(theirs):
            pltpu.make_async_remote_copy(src_ref=ref, dst_ref=ref, send_sem=send_sems.at[k], recv_sem=recv_sems.at[k],
                                         device_id=sibling, device_id_type=MESH).wait_recv()
        for cp in sends:
            cp.wait_send()

    anyspec = pl.BlockSpec(memory_space=pl.ANY)
    return pl.pallas_call(
        body, name="pair_exchange",
        out_shape=[jax.ShapeDtypeStruct((D, W_IN_SHARD), F32), jax.ShapeDtypeStruct((W_OUT_SHARD, D), F32)],
        in_specs=[anyspec, anyspec], out_specs=[anyspec, anyspec], input_output_aliases={0: 0, 1: 1},
        scratch_shapes=[pltpu.SemaphoreType.DMA((2,)), pltpu.SemaphoreType.DMA((2,))],
    )(gi, go)


def _rope_tables(s):
    inv_freq = 10000.0 ** (-jnp.arange(0, HEAD, 2, dtype=F32) / HEAD)
    ang = jnp.arange(s, dtype=F32)[:, None] * inv_freq[None, :]
    return jnp.tile(jnp.cos(ang), (1, LANE // (HEAD // 2))), jnp.tile(jnp.sin(ang), (1, LANE // (HEAD // 2)))


def _pad_cols(a, n):
    return jnp.pad(a, ((0, 0), (0, n - a.shape[1])))


def kernel(x, c, w_ada, b_ada, norm_g, w_in, ln_v_g, ln_v_b, w_spatial, b_spatial, sinks, w_out, w_ada_final, b_ada_final, final_norm_g, loss_target, m_w_ada, m_b_ada, m_norm_g, m_w_in, m_ln_v_g, m_ln_v_b, m_w_spatial, m_b_spatial, m_sinks, m_w_out, m_w_ada_final, m_b_ada_final, m_final_norm_g, v_w_ada, v_b_ada, v_norm_g, v_w_in, v_ln_v_g, v_ln_v_b, v_w_spatial, v_b_spatial, v_sinks, v_w_out, v_w_ada_final, v_b_ada_final, v_final_norm_g):
    s = x.shape[1]
    ax, ay, ac = _coords()
    chip = 2 * ax + ay
    me = 4 * ax + 2 * ay + ac
    n_ada = w_ada.shape[2]
    n_adaf = w_ada_final.shape[1]

    x2d = x.reshape(s, D)
    tgt = loss_target.reshape(s, D)
    w_ada2, w_in2, w_out2 = w_ada[0], w_in[0], w_out[0]
    b_ada_f2 = b_ada_final.reshape(1, 2 * D)
    gf = final_norm_g.reshape(1, D)

    c_all = _allgather_sum_call(jnp.pad(c, ((0, 7), (0, 0))), "gather_c", False)[0][::8]
    mod_p, c_act = _rowmat_call(c_all, w_ada2, lax.dynamic_slice(b_ada, (0, chip * n_ada), (1, n_ada)), "mod")
    modf_p, _ = _rowmat_call(c_all, w_ada_final, lax.dynamic_slice(b_ada_f2, (0, chip * n_adaf), (1, n_adaf)), "mod_final")
    mods = _allgather_sum_call(jnp.concatenate([mod_p, modf_p], axis=1), "gather_mod", False)[0]
    my_rows = [lax.dynamic_slice(mods, (16 * j + me, 0), (1, n_ada + n_adaf)) for j in range(N_CHIP)]
    mod = jnp.concatenate([r[:, :n_ada] for r in my_rows], axis=1)
    mod_f = jnp.concatenate([r[:, n_ada:] for r in my_rows], axis=1)
    shift, scale, gate = mod[:, :D], mod[:, D:2 * D], mod[:, 2 * D:]
    shift_f, scale_f = mod_f[:, :D], mod_f[:, D:]

    pos = jnp.stack([chip, ac]).astype(jnp.int32)
    w_in_bf, w_out_bf = _weights_gather_call(_cast_into_call(pos, w_in2, (D, D_IN), "cast_w_in"),
                                             _cast_into_call(pos, w_out2, (D, D), "cast_w_out"))

    cos, sin = _rope_tables(s)
    b_sp_t = b_spatial[0].T
    sinks1 = sinks.reshape(N_Q)
    proj, h = _proj_call(x2d, shift, scale, norm_g, w_in_bf)
    y = _mix_fwd_call(proj, cos, sin, ln_v_g, ln_v_b, w_spatial[0], b_sp_t, sinks1)
    dx2, do, st_tail = _tail_call(y, w_out_bf, x2d, tgt, gate, shift_f, scale_f, gf)

    g_w_out_p = _tn_call(y, do, "grad_w_out")
    dy = _dy_call(do, w_out_bf)
    dproj, st_ln, d_wsp, d_bsp_t, d_sink = _mix_bwd_call(
        proj, dy, cos, sin, ln_v_g, ln_v_b, w_spatial[0], jnp.swapaxes(w_spatial[0], 1, 2), b_sp_t, sinks1)
    g_w_in_p = _tn_call(h, dproj, "grad_w_in")
    grad_x, st_dh = _dh_call(dproj, w_in_bf, x2d, dx2, scale, norm_g)

    recv_i, recv_o = _grad_exchange_call(g_w_in_p, g_w_out_p)
    g_w_in, g_w_out = _pair_exchange_call(_sum_pieces_call(pos, g_w_in_p, recv_i, (D, W_IN_SHARD), "sum_w_in"),
                                          _sum_pieces_call(pos, g_w_out_p, recv_o, (W_OUT_SHARD, D), "sum_w_out"))

    pack = jnp.concatenate([
        d_wsp.reshape(64, D), st_tail, st_dh, _pad_cols(st_ln, D),
        _pad_cols(d_bsp_t[:, :GROUPS].T, D), _pad_cols(d_sink, D)], axis=0)
    rows = pack.shape[0]
    packs, tot = _allgather_sum_call(pack, "gather_small", True)
    packs = packs.reshape(N_DEV, rows, D)
    dmod_all = jnp.concatenate([packs[:, 72, :], packs[:, 73, :], packs[:, 67, :]], axis=1)
    dmodf_all = jnp.concatenate([packs[:, 64, :], packs[:, 65, :]], axis=1)
    loss = tot[69, 0]
    grads = {
        "b_ada": jnp.concatenate([tot[72:73], tot[73:74], tot[67:68]], axis=1),
        "norm_g": tot[74:75],
        "ln_v_g": tot[80:81, :D_A],
        "ln_v_b": tot[81:82, :D_A],
        "w_spatial": tot[0:64].reshape(GROUPS * BLK, BLK),
        "b_spatial": tot[88:96, :BLK],
        "sinks": tot[96:97, :N_Q],
        "b_ada_final": jnp.concatenate([tot[64:65], tot[65:66]], axis=1),
        "final_norm_g": tot[66:67],
        "w_in": g_w_in,
        "w_out": g_w_out,
    }

    weights = dict(w_ada=w_ada, b_ada=b_ada, norm_g=norm_g, w_in=w_in, ln_v_g=ln_v_g, ln_v_b=ln_v_b, w_spatial=w_spatial,
                   b_spatial=b_spatial, sinks=sinks, w_out=w_out, w_ada_final=w_ada_final, b_ada_final=b_ada_final,
                   final_norm_g=final_norm_g)
    m_in = dict(w_ada=m_w_ada, b_ada=m_b_ada, norm_g=m_norm_g, w_in=m_w_in, ln_v_g=m_ln_v_g, ln_v_b=m_ln_v_b,
                w_spatial=m_w_spatial, b_spatial=m_b_spatial, sinks=m_sinks, w_out=m_w_out, w_ada_final=m_w_ada_final,
                b_ada_final=m_b_ada_final, final_norm_g=m_final_norm_g)
    v_in = dict(w_ada=v_w_ada, b_ada=v_b_ada, norm_g=v_norm_g, w_in=v_w_in, ln_v_g=v_ln_v_g, ln_v_b=v_ln_v_b,
                w_spatial=v_w_spatial, b_spatial=v_b_spatial, sinks=v_sinks, w_out=v_w_out, w_ada_final=v_w_ada_final,
                b_ada_final=v_b_ada_final, final_norm_g=v_final_norm_g)
    c_act_t = c_act.T
    outer = {"w_ada": lax.dynamic_slice(dmod_all, (0, chip * n_ada), (N_DEV, n_ada)),
             "w_ada_final": lax.dynamic_slice(dmodf_all, (0, chip * n_adaf), (N_DEV, n_adaf))}
    out_g, out_d, out_m, out_v = [], [], [], []
    for name, w in weights.items():
        shape = w.shape
        if name in outer:
            shape2 = (D, outer[name].shape[1])
            g, dl, mn, vn = _adam_outer_call(w.reshape(shape2), c_act_t, outer[name], m_in[name].reshape(shape2),
                                             v_in[name].reshape(shape2), "adam_" + name)
        else:
            g = grads[name]
            shape2 = g.shape
            dl, mn, vn = _adam_call(w.reshape(shape2), g, m_in[name].reshape(shape2), v_in[name].reshape(shape2), "adam_" + name)
        out_g.append(g.reshape(shape))
        out_d.append(dl.reshape(shape))
        out_m.append(mn.reshape(shape))
        out_v.append(vn.reshape(shape))
    return (loss, grad_x.reshape(x.shape), *out_g, *out_d, *out_m, *out_v)
```

```python
import jax
import jax.numpy as jnp
from jax import lax
from jax.experimental import pallas as pl
from jax.experimental.pallas import tpu as pltpu

F32 = jnp.float32
BF16 = jnp.bfloat16
MESH = pl.DeviceIdType.MESH

D = 2048
D_A = 1024
D_B = 1024
D_KV = 256
HEAD = 64
N_Q = 16
BLK = 128
GROUPS = 8
D_IN = 5632
OFF_Q, OFF_K, OFF_V, OFF_ZB = 3072, 4096, 4352, 4608
N_CHIP = 4
N_DEV = 8
W_IN_SHARD = D_IN // N_CHIP
W_OUT_SHARD = D // N_CHIP
EPS = 1e-5
SCALE = HEAD ** -0.5
NEG = -1e30
LANE = 128
VMEM_LIMIT = 56 * 1024 * 1024

ADAM_LR, ADAM_B1, ADAM_B2, ADAM_EPS, ADAM_WD, ADAM_STEP = 0.001, 0.9, 0.999, 1e-08, 0.01, 10
ADAM_C1 = 1.0 - ADAM_B1 ** ADAM_STEP
ADAM_C2 = 1.0 - ADAM_B2 ** ADAM_STEP

NT = (((1,), (1,)), ((), ()))
TN = (((0,), (0,)), ((), ()))


def _params(*sem):
    return pltpu.CompilerParams(dimension_semantics=sem, vmem_limit_bytes=VMEM_LIMIT)


def _silu_parts(z):
    sig = 1.0 / (1.0 + jnp.exp(-z))
    return z * sig, sig


def _rot_half(v, first_half):
    return jnp.where(first_half, -pltpu.roll(v, 96, 1), pltpu.roll(v, 32, 1))


def _lane_masks():
    lane = lax.broadcasted_iota(jnp.int32, (BLK, LANE), 1)
    return (lane % HEAD) < (HEAD // 2), lane < HEAD


def _band_valid(first_block_bound):
    rr = lax.broadcasted_iota(jnp.int32, (BLK, 2 * BLK), 0)
    jj = lax.broadcasted_iota(jnp.int32, (BLK, 2 * BLK), 1)
    return (jj > rr) & (jj <= rr + BLK) & (jj >= first_block_bound)


def _tril():
    t = lax.broadcasted_iota(jnp.int32, (BLK, BLK), 0)
    s = lax.broadcasted_iota(jnp.int32, (BLK, BLK), 1)
    return s <= t


def _expand_kv(slab, lo):
    rolled = pltpu.roll(slab, HEAD, 1)
    zero = jnp.zeros_like(slab)
    return (jnp.where(lo, slab, zero).astype(BF16), jnp.where(lo, zero, rolled).astype(BF16),
            jnp.where(lo, rolled, zero).astype(BF16), jnp.where(lo, zero, slab).astype(BF16))


def _layer_norm_fwd(va, lg, lb):
    mu = jnp.mean(va, axis=-1, keepdims=True)
    xc = va - mu
    rstd = lax.rsqrt(jnp.mean(xc * xc, axis=-1, keepdims=True) + EPS)
    vhat = xc * rstd
    return vhat, rstd, vhat * lg + lb


def _softmax_sink(qm, kexp, valid, sink):
    s = lax.dot_general(qm, kexp, NT, preferred_element_type=F32) * SCALE
    s = jnp.where(valid, s, NEG)
    m = jnp.maximum(jnp.max(s, axis=-1, keepdims=True), sink)
    p = jnp.exp(s - m)
    esink = jnp.exp(sink - m)
    den = jnp.sum(p, axis=-1, keepdims=True) + esink
    return p / den, esink / den


def _rowmat_call(c_all, w, b, name):
    n = w.shape[1]
    tn = 512

    def body(c_ref, w_ref, b_ref, o_ref, ca_ref):
        ca, _ = _silu_parts(c_ref[...])
        ca_ref[...] = ca
        o_ref[...] = jnp.dot(ca.astype(BF16), w_ref[...].astype(BF16), preferred_element_type=F32) + b_ref[...]

    return pl.pallas_call(
        body, name=name, grid=(n // tn,),
        in_specs=[pl.BlockSpec((N_DEV, D), lambda j: (0, 0)), pl.BlockSpec((D, tn), lambda j: (0, j)),
                  pl.BlockSpec((1, tn), lambda j: (0, j))],
        out_specs=[pl.BlockSpec((N_DEV, tn), lambda j: (0, j)), pl.BlockSpec((N_DEV, D), lambda j: (0, 0))],
        out_shape=[jax.ShapeDtypeStruct((N_DEV, n), F32), jax.ShapeDtypeStruct((N_DEV, D), F32)],
        compiler_params=_params("arbitrary"),
    )(c_all, w, b)


def _cast_into_call(pos, w, full_shape, name):
    r, n = w.shape
    tr = min(r, 512)
    by_cols = full_shape[0] == r
    nrb = r // tr

    def body(pos_ref, w_ref, o_ref):
        o_ref[...] = w_ref[...].astype(BF16)

    out_map = (lambda i, pos: (i, pos[0])) if by_cols else (lambda i, pos: (pos[0] * nrb + i, 0))
    return pl.pallas_call(
        body, name=name,
        grid_spec=pltpu.PrefetchScalarGridSpec(
            num_scalar_prefetch=1, grid=(nrb,),
            in_specs=[pl.BlockSpec((tr, n), lambda i, pos: (i, 0))], out_specs=pl.BlockSpec((tr, n), out_map)),
        out_shape=jax.ShapeDtypeStruct(full_shape, BF16), compiler_params=_params("parallel"),
    )(pos, w)


def _proj_call(x, shift, scale, norm_g, w_bf):
    s = x.shape[0]
    tm = min(s, 1024)
    tn = 512

    def body(x_ref, sh_ref, sc_ref, g_ref, w_ref, proj_ref, h_ref):
        @pl.when(pl.program_id(1) == 0)
        def _():
            xv = x_ref[...]
            r = lax.rsqrt(jnp.mean(xv * xv, axis=-1, keepdims=True) + EPS)
            h_ref[...] = ((xv * r * g_ref[...]) * (1.0 + sc_ref[...]) + sh_ref[...]).astype(BF16)

        proj_ref[...] = jnp.dot(h_ref[...], w_ref[...], preferred_element_type=F32)

    vec = pl.BlockSpec((1, D), lambda i, j: (0, 0))
    return pl.pallas_call(
        body, name="proj", grid=(s // tm, D_IN // tn),
        in_specs=[pl.BlockSpec((tm, D), lambda i, j: (i, 0)), vec, vec, vec, pl.BlockSpec((D, tn), lambda i, j: (0, j))],
        out_specs=[pl.BlockSpec((tm, tn), lambda i, j: (i, j)), pl.BlockSpec((tm, D), lambda i, j: (i, 0))],
        out_shape=[jax.ShapeDtypeStruct((s, D_IN), F32), jax.ShapeDtypeStruct((s, D), BF16)],
        compiler_params=_params("parallel", "arbitrary"),
    )(x, shift, scale, norm_g, w_bf)


def _proj_specs(rev_nb=None):
    if rev_nb is None:
        row = lambda i: i
    else:
        row = lambda i: rev_nb - 1 - i
    wide = lambda col: pl.BlockSpec((BLK, D_A), lambda i: (row(i), col))
    kv = lambda col: pl.BlockSpec((BLK, D_KV), lambda i: (row(i), col))
    half = lambda col: pl.BlockSpec((BLK, 512), lambda i: (row(i), col))
    return [wide(0), wide(1), wide(2), wide(3), kv(OFF_K // D_KV), kv(OFF_V // D_KV), half(OFF_ZB // 512), half(OFF_ZB // 512 + 1)]


def _mix_fwd_call(proj, cos, sin, ln_g, ln_b, w_sp, b_sp_t, sinks):
    s = proj.shape[0]
    nb = s // BLK

    def body(ua_ref, va_ref, za_ref, q_ref, k_ref, v_ref, zb0_ref, zb1_ref, cos_ref, sin_ref, lg_ref, lb_ref,
             w_ref, bt_ref, sinks_ref, y_ref, kexp_ref, vexp_ref):
        i = pl.program_id(0)
        first_half, lo = _lane_masks()
        cos_t = cos_ref[...]
        sin_t = sin_ref[...]

        _, _, vln = _layer_norm_fwd(va_ref[...], lg_ref[...], lb_ref[...])
        tril = _tril()
        for g in range(GROUPS):
            cols = slice(g * BLK, (g + 1) * BLK)
            wg = jnp.where(tril, w_ref[g], 0.0).astype(BF16)
            sg = jnp.dot(wg, vln[:, cols].astype(BF16), preferred_element_type=F32) + bt_ref[:, g:g + 1]
            gate, _ = _silu_parts(za_ref[:, cols])
            y_ref[:, cols] = (ua_ref[:, cols] * sg * gate).astype(BF16)

        @pl.when(i == 0)
        def _():
            kexp_ref[:, 0:BLK, :] = jnp.zeros((8, BLK, LANE), BF16)
            vexp_ref[:, 0:BLK, :] = jnp.zeros((8, BLK, LANE), BF16)

        @pl.when(i > 0)
        def _():
            kexp_ref[:, 0:BLK, :] = kexp_ref[:, BLK:2 * BLK, :]
            vexp_ref[:, 0:BLK, :] = vexp_ref[:, BLK:2 * BLK, :]

        for ks in range(2):
            cols = slice(ks * LANE, (ks + 1) * LANE)
            kslab = k_ref[:, cols]
            kr = kslab * cos_t + _rot_half(kslab, first_half) * sin_t
            for n, (ke, ve) in enumerate(zip(_expand_kv(kr, lo), _expand_kv(v_ref[:, cols], lo))):
                kexp_ref[4 * ks + n, BLK:2 * BLK, :] = ke
                vexp_ref[4 * ks + n, BLK:2 * BLK, :] = ve

        valid = _band_valid(jnp.where(i > 0, 0, BLK))
        for sb in range(8):
            cols = slice(sb * LANE, (sb + 1) * LANE)
            kh = sb // 2
            qslab = q_ref[:, cols]
            qr = qslab * cos_t + _rot_half(qslab, first_half) * sin_t
            acc = jnp.zeros((BLK, LANE), F32)
            for par in range(2):
                half = lo if par == 0 else jnp.logical_not(lo)
                qm = jnp.where(half, qr, 0.0).astype(BF16)
                probs, _ = _softmax_sink(qm, kexp_ref[2 * kh + par], valid, sinks_ref[2 * sb + par])
                acc = acc + jnp.dot(probs.astype(BF16), vexp_ref[2 * kh + par], preferred_element_type=F32)
            zb = zb0_ref[:, cols] if sb < 4 else zb1_ref[:, (sb - 4) * LANE:(sb - 3) * LANE]
            gate, _ = _silu_parts(zb)
            y_ref[:, D_A + sb * LANE:D_A + (sb + 1) * LANE] = (acc * gate).astype(BF16)

    tab = pl.BlockSpec((BLK, LANE), lambda i: (i, 0))
    return pl.pallas_call(
        body, name="mix_fwd", grid=(nb,),
        in_specs=_proj_specs() + [
            tab, tab, pl.BlockSpec((1, D_A), lambda i: (0, 0)), pl.BlockSpec((1, D_A), lambda i: (0, 0)),
            pl.BlockSpec((GROUPS, BLK, BLK), lambda i: (0, 0, 0)), pl.BlockSpec((BLK, GROUPS), lambda i: (0, 0)),
            pl.BlockSpec(memory_space=pltpu.SMEM)],
        out_specs=pl.BlockSpec((BLK, 2 * D_A), lambda i: (i, 0)),
        out_shape=jax.ShapeDtypeStruct((s, 2 * D_A), BF16),
        scratch_shapes=[pltpu.VMEM((8, 2 * BLK, LANE), BF16), pltpu.VMEM((8, 2 * BLK, LANE), BF16)],
        compiler_params=_params("arbitrary"),
    )(proj, proj, proj, proj, proj, proj, proj, proj, cos, sin, ln_g, ln_b, w_sp, b_sp_t, sinks)


def _tail_call(y, w_out_bf, x, target, gate, shift_f, scale_f, gf):
    s = x.shape[0]
    tm = min(s, 256)
    nsteps = s // tm

    def body(y_ref, w_ref, x_ref, t_ref, gate_ref, shf_ref, scf_ref, gf_ref, dx2_ref, do_ref, st_ref):
        i = pl.program_id(0)

        @pl.when(i == 0)
        def _():
            st_ref[...] = jnp.zeros((8, D), F32)

        o = jnp.dot(y_ref[...], w_ref[...], preferred_element_type=F32)
        gate_v = gate_ref[...]
        x2 = x_ref[...] + gate_v * o
        r2 = lax.rsqrt(jnp.mean(x2 * x2, axis=-1, keepdims=True) + EPS)
        xn2 = x2 * r2
        hn2 = xn2 * gf_ref[...]
        one_sc = 1.0 + scf_ref[...]
        err = hn2 * one_sc + shf_ref[...] - t_ref[...]
        dout = err * (1.0 / D)
        dhn2 = dout * one_sc
        dxn2 = dhn2 * gf_ref[...]
        dx2 = r2 * (dxn2 - xn2 * jnp.mean(dxn2 * xn2, axis=-1, keepdims=True))
        dx2_ref[...] = dx2
        do_ref[...] = (dx2 * gate_v).astype(BF16)
        st_ref[0:1, :] += jnp.sum(dout, axis=0, keepdims=True)
        st_ref[1:2, :] += jnp.sum(dout * hn2, axis=0, keepdims=True)
        st_ref[2:3, :] += jnp.sum(dhn2 * xn2, axis=0, keepdims=True)
        st_ref[3:4, :] += jnp.sum(dx2 * o, axis=0, keepdims=True)
        st_ref[4:5, :] += jnp.sum(err * err, axis=0, keepdims=True)

        @pl.when(i == nsteps - 1)
        def _():
            st_ref[5:6, :] = jnp.full((1, D), 0.5 / D, F32) * jnp.sum(st_ref[4:5, :])

    vec = pl.BlockSpec((1, D), lambda i: (0, 0))
    rows = lambda: pl.BlockSpec((tm, D), lambda i: (i, 0))
    return pl.pallas_call(
        body, name="tail", grid=(nsteps,),
        in_specs=[rows(), pl.BlockSpec((D, D), lambda i: (0, 0)), rows(), rows(), vec, vec, vec, vec],
        out_specs=[rows(), rows(), pl.BlockSpec((8, D), lambda i: (0, 0))],
        out_shape=[jax.ShapeDtypeStruct((s, D), F32), jax.ShapeDtypeStruct((s, D), BF16), jax.ShapeDtypeStruct((8, D), F32)],
        compiler_params=_params("arbitrary"),
    )(y, w_out_bf, x, target, gate, shift_f, scale_f, gf)


def _dy_call(do, w_out_bf):
    s = do.shape[0]
    tm = min(s, 512)

    def body(do_ref, w_ref, dy_ref):
        dy_ref[...] = lax.dot_general(do_ref[...], w_ref[...], NT, preferred_element_type=F32)

    return pl.pallas_call(
        body, name="dy", grid=(s // tm,),
        in_specs=[pl.BlockSpec((tm, D), lambda i: (i, 0)), pl.BlockSpec((D, D), lambda i: (0, 0))],
        out_specs=pl.BlockSpec((tm, D), lambda i: (i, 0)),
        out_shape=jax.ShapeDtypeStruct((s, D), F32), compiler_params=_params("parallel"),
    )(do, w_out_bf)


def _tn_call(a, b, name):
    s, m = a.shape
    n = b.shape[1]
    tn = 512
    ts = min(s, 1024)
    nk = s // ts

    def body(a_ref, b_ref, o_ref, acc_ref):
        k = pl.program_id(1)

        @pl.when(k == 0)
        def _():
            acc_ref[...] = jnp.zeros((m, tn), F32)

        acc_ref[...] += lax.dot_general(a_ref[...], b_ref[...], TN, preferred_element_type=F32)

        @pl.when(k == nk - 1)
        def _():
            o_ref[...] = acc_ref[...].astype(BF16)

    return pl.pallas_call(
        body, name=name, grid=(n // tn, nk),
        in_specs=[pl.BlockSpec((ts, m), lambda j, k: (k, 0)), pl.BlockSpec((ts, tn), lambda j, k: (k, j))],
        out_specs=pl.BlockSpec((m, tn), lambda j, k: (0, j)),
        out_shape=jax.ShapeDtypeStruct((m, n), BF16),
        scratch_shapes=[pltpu.VMEM((m, tn), F32)],
        compiler_params=_params("parallel", "arbitrary"),
    )(a, b)


def _tn_shards_call(pos, a, b, qs, name):
    s, m = a.shape
    ts = min(s, 512)
    nk = s // ts

    def body(pos_ref, a_ref, b_ref, o_ref, acc_ref):
        k = pl.program_id(1)

        @pl.when(k == 0)
        def _():
            acc_ref[...] = jnp.zeros((m, W_IN_SHARD), F32)

        acc_ref[...] += lax.dot_general(a_ref[...], b_ref[...], TN, preferred_element_type=F32)

        @pl.when(k == nk - 1)
        def _():
            o_ref[...] = acc_ref[...].astype(BF16)

    def shard(j, pos):
        q = qs[0]
        for n in range(1, len(qs)):
            q = jnp.where(j == n, qs[n], q)
        return jnp.bitwise_xor(pos[0], q)

    return pl.pallas_call(
        body, name=name,
        grid_spec=pltpu.PrefetchScalarGridSpec(
            num_scalar_prefetch=1, grid=(len(qs), nk),
            in_specs=[pl.BlockSpec((ts, m), lambda j, k, pos: (k, 0)),
                      pl.BlockSpec((ts, W_IN_SHARD), lambda j, k, pos: (k, shard(j, pos)))],
            out_specs=pl.BlockSpec((m, W_IN_SHARD), lambda j, k, pos: (0, j)),
            scratch_shapes=[pltpu.VMEM((m, W_IN_SHARD), F32)]),
        out_shape=jax.ShapeDtypeStruct((m, len(qs) * W_IN_SHARD), BF16),
        compiler_params=_params("parallel", "arbitrary"),
    )(pos, a, b)


def _mix_bwd_call(proj, dy, cos, sin, ln_g, ln_b, w_sp, w_sp_t, b_sp_t, sinks):
    s = proj.shape[0]
    nb = s // BLK
    rev = lambda i: nb - 1 - i
    prev = lambda i: jnp.maximum(nb - 2 - i, 0)

    def body(ua_ref, va_ref, za_ref, q_ref, k_ref, v_ref, zb0_ref, zb1_ref, kp_ref, vp_ref, dy_ref,
             cos_ref, sin_ref, cosp_ref, sinp_ref, lg_ref, lb_ref, w_ref, wt_ref, bt_ref, sinks_ref,
             dp_ref, lnst_ref, dw_ref, dbt_ref, dsink_ref,
             kexp_ref, vexp_ref, dvln_ref, dkacc_ref, dvacc_ref, kcar_ref, vcar_ref):
        i = pl.program_id(0)
        first_half, lo = _lane_masks()
        lane8 = lax.broadcasted_iota(jnp.int32, (8, LANE), 1)
        cos_t = cos_ref[...]
        sin_t = sin_ref[...]

        @pl.when(i == 0)
        def _():
            lnst_ref[...] = jnp.zeros((8, D_A), F32)
            dw_ref[...] = jnp.zeros((GROUPS, BLK, BLK), F32)
            dbt_ref[...] = jnp.zeros((BLK, LANE), F32)
            dsink_ref[...] = jnp.zeros((8, LANE), F32)
            kcar_ref[...] = jnp.zeros((BLK, D_KV), F32)
            vcar_ref[...] = jnp.zeros((BLK, D_KV), F32)

        vhat, rstd, vln = _layer_norm_fwd(va_ref[...], lg_ref[...], lb_ref[...])
        tril = _tril()
        triu = jnp.logical_not(tril) | (lax.broadcasted_iota(jnp.int32, (BLK, BLK), 0) == lax.broadcasted_iota(jnp.int32, (BLK, BLK), 1))
        lane_b = lax.broadcasted_iota(jnp.int32, (BLK, LANE), 1)
        db_acc = jnp.zeros((BLK, LANE), F32)
        for g in range(GROUPS):
            cols = slice(g * BLK, (g + 1) * BLK)
            vln_g = vln[:, cols].astype(BF16)
            wg = jnp.where(tril, w_ref[g], 0.0).astype(BF16)
            sg = jnp.dot(wg, vln_g, preferred_element_type=F32) + bt_ref[:, g:g + 1]
            za = za_ref[:, cols]
            gate, sig = _silu_parts(za)
            ua = ua_ref[:, cols]
            dya_g = dy_ref[:, cols]
            dya = dya_g * gate
            dp_ref[:, cols] = (dya * sg).astype(BF16)
            dp_ref[:, 2 * D_A + g * BLK:2 * D_A + (g + 1) * BLK] = (
                dya_g * (ua * sg) * (sig * (1.0 + za * (1.0 - sig)))).astype(BF16)
            ds = dya * ua
            ds_b = ds.astype(BF16)
            wtg = jnp.where(triu, wt_ref[g], 0.0).astype(BF16)
            dvln_ref[:, cols] = jnp.dot(wtg, ds_b, preferred_element_type=F32)
            dw_ref[g] += jnp.where(tril, lax.dot_general(ds_b, vln_g, NT, preferred_element_type=F32), 0.0)
            db_acc = db_acc + jnp.where(lane_b == g, jnp.sum(ds, axis=-1, keepdims=True), 0.0)
        dbt_ref[...] += db_acc
        dvln = dvln_ref[...]
        lnst_ref[0:1, :] += jnp.sum(dvln * vhat, axis=0, keepdims=True)
        lnst_ref[1:2, :] += jnp.sum(dvln, axis=0, keepdims=True)
        dvhat = dvln * lg_ref[...]
        m1 = jnp.mean(dvhat, axis=-1, keepdims=True)
        m2 = jnp.mean(dvhat * vhat, axis=-1, keepdims=True)
        dp_ref[:, D_A:2 * D_A] = (rstd * (dvhat - m1 - vhat * m2)).astype(BF16)

        cosp = cosp_ref[...]
        sinp = sinp_ref[...]
        for ks in range(2):
            cols = slice(ks * LANE, (ks + 1) * LANE)
            kslab = k_ref[:, cols]
            kr = kslab * cos_t + _rot_half(kslab, first_half) * sin_t
            kpslab = kp_ref[:, cols]
            kpr = kpslab * cosp + _rot_half(kpslab, first_half) * sinp
            for n, (kc, vc, kp, vp) in enumerate(zip(_expand_kv(kr, lo), _expand_kv(v_ref[:, cols], lo),
                                                     _expand_kv(kpr, lo), _expand_kv(vp_ref[:, cols], lo))):
                kexp_ref[4 * ks + n, BLK:2 * BLK, :] = kc
                vexp_ref[4 * ks + n, BLK:2 * BLK, :] = vc
                kexp_ref[4 * ks + n, 0:BLK, :] = kp
                vexp_ref[4 * ks + n, 0:BLK, :] = vp
        dkacc_ref[...] = jnp.zeros((4, 2 * BLK, LANE), F32)
        dvacc_ref[...] = jnp.zeros((4, 2 * BLK, LANE), F32)

        valid = _band_valid(jnp.where(i < nb - 1, 0, BLK))
        dsink_acc = jnp.zeros((8, LANE), F32)
        for sb in range(8):
            cols = slice(sb * LANE, (sb + 1) * LANE)
            kh = sb // 2
            qslab = q_ref[:, cols]
            qr = qslab * cos_t + _rot_half(qslab, first_half) * sin_t
            zb = zb0_ref[:, cols] if sb < 4 else zb1_ref[:, (sb - 4) * LANE:(sb - 3) * LANE]
            gate, sig = _silu_parts(zb)
            dyb = dy_ref[:, D_A + sb * LANE:D_A + (sb + 1) * LANE]
            d_o = dyb * gate
            o_acc = jnp.zeros((BLK, LANE), F32)
            dq_acc = jnp.zeros((BLK, LANE), F32)
            for par in range(2):
                h = 2 * sb + par
                half = lo if par == 0 else jnp.logical_not(lo)
                qm = jnp.where(half, qr, 0.0).astype(BF16)
                probs, psink = _softmax_sink(qm, kexp_ref[2 * kh + par], valid, sinks_ref[h])
                probs_b = probs.astype(BF16)
                o_h = jnp.dot(probs_b, vexp_ref[2 * kh + par], preferred_element_type=F32)
                o_acc = o_acc + o_h
                dom = jnp.where(half, d_o, 0.0)
                dom_b = dom.astype(BF16)
                delta = jnp.sum(dom * o_h, axis=-1, keepdims=True)
                dpr = lax.dot_general(dom_b, vexp_ref[2 * kh + par], NT, preferred_element_type=F32)
                dss = (probs * (dpr - delta) * SCALE).astype(BF16)
                dsink_acc = dsink_acc + jnp.where((lane8 == h), -jnp.sum(psink * delta), 0.0)
                dq_acc = dq_acc + jnp.dot(dss, kexp_ref[2 * kh + par], preferred_element_type=F32)
                dkacc_ref[kh] += lax.dot_general(dss, qm, TN, preferred_element_type=F32)
                dvacc_ref[kh] += lax.dot_general(probs_b, dom_b, TN, preferred_element_type=F32)
            dp_ref[:, OFF_ZB + sb * LANE:OFF_ZB + (sb + 1) * LANE] = (
                dyb * o_acc * (sig * (1.0 + zb * (1.0 - sig)))).astype(BF16)
            dp_ref[:, OFF_Q + sb * LANE:OFF_Q + (sb + 1) * LANE] = (
                dq_acc * cos_t - _rot_half(dq_acc * sin_t, first_half)).astype(BF16)
        row0 = lax.broadcasted_iota(jnp.int32, (8, LANE), 0) == 0
        dsink_ref[...] += jnp.where(row0, dsink_acc, 0.0)

        lo2 = lax.broadcasted_iota(jnp.int32, (2 * BLK, LANE), 1) < HEAD
        for ks in range(2):
            cols = slice(ks * LANE, (ks + 1) * LANE)
            ka = dkacc_ref[2 * ks]
            kb = dkacc_ref[2 * ks + 1]
            dk_band = jnp.where(lo2, ka + pltpu.roll(ka, HEAD, 1), kb + pltpu.roll(kb, HEAD, 1))
            va_ = dvacc_ref[2 * ks]
            vb_ = dvacc_ref[2 * ks + 1]
            dv_band = jnp.where(lo2, va_ + pltpu.roll(va_, HEAD, 1), vb_ + pltpu.roll(vb_, HEAD, 1))
            dkr = dk_band[BLK:2 * BLK, :] + kcar_ref[:, cols]
            dp_ref[:, OFF_K + ks * LANE:OFF_K + (ks + 1) * LANE] = (
                dkr * cos_t - _rot_half(dkr * sin_t, first_half)).astype(BF16)
            dp_ref[:, OFF_V + ks * LANE:OFF_V + (ks + 1) * LANE] = (
                dv_band[BLK:2 * BLK, :] + vcar_ref[:, cols]).astype(BF16)
            kcar_ref[:, cols] = dk_band[0:BLK, :]
            vcar_ref[:, cols] = dv_band[0:BLK, :]

    tab = pl.BlockSpec((BLK, LANE), lambda i: (rev(i), 0))
    tabp = pl.BlockSpec((BLK, LANE), lambda i: (prev(i), 0))
    kvp = lambda col: pl.BlockSpec((BLK, D_KV), lambda i: (prev(i), col))
    vec = pl.BlockSpec((1, D_A), lambda i: (0, 0))
    w3 = pl.BlockSpec((GROUPS, BLK, BLK), lambda i: (0, 0, 0))
    return pl.pallas_call(
        body, name="mix_bwd", grid=(nb,),
        in_specs=_proj_specs(nb) + [
            kvp(OFF_K // D_KV), kvp(OFF_V // D_KV), pl.BlockSpec((BLK, 2 * D_A), lambda i: (rev(i), 0)),
            tab, tab, tabp, tabp, vec, vec, w3, w3, pl.BlockSpec((BLK, GROUPS), lambda i: (0, 0)),
            pl.BlockSpec(memory_space=pltpu.SMEM)],
        out_specs=[pl.BlockSpec((BLK, D_IN), lambda i: (rev(i), 0)), pl.BlockSpec((8, D_A), lambda i: (0, 0)), w3,
                   pl.BlockSpec((BLK, LANE), lambda i: (0, 0)), pl.BlockSpec((8, LANE), lambda i: (0, 0))],
        out_shape=[jax.ShapeDtypeStruct((s, D_IN), BF16), jax.ShapeDtypeStruct((8, D_A), F32),
                   jax.ShapeDtypeStruct((GROUPS, BLK, BLK), F32), jax.ShapeDtypeStruct((BLK, LANE), F32),
                   jax.ShapeDtypeStruct((8, LANE), F32)],
        scratch_shapes=[pltpu.VMEM((8, 2 * BLK, LANE), BF16), pltpu.VMEM((8, 2 * BLK, LANE), BF16),
                        pltpu.VMEM((BLK, D_A), F32), pltpu.VMEM((4, 2 * BLK, LANE), F32),
                        pltpu.VMEM((4, 2 * BLK, LANE), F32), pltpu.VMEM((BLK, D_KV), F32), pltpu.VMEM((BLK, D_KV), F32)],
        compiler_params=_params("arbitrary"),
    )(proj, proj, proj, proj, proj, proj, proj, proj, proj, proj, dy, cos, sin, cos, sin, ln_g, ln_b, w_sp, w_sp_t,
      b_sp_t, sinks)


def _dh_call(dproj, w_bf, x, dx2, scale, norm_g):
    s = x.shape[0]
    tm = min(s, 512)
    tk = 512
    nk = D_IN // tk

    def body(dp_ref, w_ref, x_ref, dx2_ref, sc_ref, g_ref, gx_ref, st_ref, acc_ref):
        i = pl.program_id(0)
        k = pl.program_id(1)

        @pl.when((i == 0) & (k == 0))
        def _():
            st_ref[...] = jnp.zeros((8, D), F32)

        @pl.when(k == 0)
        def _():
            acc_ref[...] = jnp.zeros((tm, D), F32)

        acc_ref[...] += lax.dot_general(dp_ref[...], w_ref[...], NT, preferred_element_type=F32)

        @pl.when(k == nk - 1)
        def _():
            dh = acc_ref[...]
            xv = x_ref[...]
            r = lax.rsqrt(jnp.mean(xv * xv, axis=-1, keepdims=True) + EPS)
            xn = xv * r
            g = g_ref[...]
            dhn = dh * (1.0 + sc_ref[...])
            dxn = dhn * g
            gx_ref[...] = dx2_ref[...] + r * (dxn - xn * jnp.mean(dxn * xn, axis=-1, keepdims=True))
            st_ref[0:1, :] += jnp.sum(dh, axis=0, keepdims=True)
            st_ref[1:2, :] += jnp.sum(dh * (xn * g), axis=0, keepdims=True)
            st_ref[2:3, :] += jnp.sum(dhn * xn, axis=0, keepdims=True)

    vec = pl.BlockSpec((1, D), lambda i, k: (0, 0))
    rows = lambda: pl.BlockSpec((tm, D), lambda i, k: (i, 0))
    return pl.pallas_call(
        body, name="dh", grid=(s // tm, nk),
        in_specs=[pl.BlockSpec((tm, tk), lambda i, k: (i, k)), pl.BlockSpec((D, tk), lambda i, k: (0, k)), rows(), rows(), vec, vec],
        out_specs=[rows(), pl.BlockSpec((8, D), lambda i, k: (0, 0))],
        out_shape=[jax.ShapeDtypeStruct((s, D), F32), jax.ShapeDtypeStruct((8, D), F32)],
        scratch_shapes=[pltpu.VMEM((tm, D), F32)],
        compiler_params=_params("arbitrary", "arbitrary"),
    )(dproj, w_bf, x, dx2, scale, norm_g)


def _adam_math(w, g, m, v):
    m_new = ADAM_B1 * m + (1.0 - ADAM_B1) * g
    v_new = ADAM_B2 * v + (1.0 - ADAM_B2) * (g * g)
    m_hat = m_new / ADAM_C1
    v_hat = v_new / ADAM_C2
    delta = -ADAM_LR * (m_hat / (jnp.sqrt(v_hat) + ADAM_EPS) + ADAM_WD * w)
    return delta, m_new, v_new


def _adam_call(w, g, m, v, name):
    r, n = w.shape
    tr = r if r * n * 4 <= (1 << 20) else max(8, (1 << 20) // (n * 4) // 8 * 8)
    while r % tr:
        tr -= 8

    def body(w_ref, g_ref, m_ref, v_ref, d_ref, mo_ref, vo_ref):
        d_ref[...], mo_ref[...], vo_ref[...] = _adam_math(w_ref[...], g_ref[...], m_ref[...], v_ref[...])

    spec = lambda: pl.BlockSpec((tr, n), lambda i: (i, 0))
    return pl.pallas_call(
        body, name=name, grid=(r // tr,), in_specs=[spec() for _ in range(4)], out_specs=[spec() for _ in range(3)],
        out_shape=[jax.ShapeDtypeStruct((r, n), F32)] * 3, compiler_params=_params("parallel"),
    )(w, g, m, v)


def _adam_outer_call(w, ct, dm, m, v, name):
    r, n = w.shape
    tr = 128

    def body(w_ref, ct_ref, dm_ref, m_ref, v_ref, g_ref, d_ref, mo_ref, vo_ref):
        g = ct_ref[:, 0:1] * dm_ref[0:1, :]
        for b in range(1, N_DEV):
            g = g + ct_ref[:, b:b + 1] * dm_ref[b:b + 1, :]
        g_ref[...] = g
        d_ref[...], mo_ref[...], vo_ref[...] = _adam_math(w_ref[...], g, m_ref[...], v_ref[...])

    spec = lambda: pl.BlockSpec((tr, n), lambda i: (i, 0))
    return pl.pallas_call(
        body, name=name, grid=(r // tr,),
        in_specs=[spec(), pl.BlockSpec((tr, N_DEV), lambda i: (i, 0)), pl.BlockSpec((N_DEV, n), lambda i: (0, 0)), spec(), spec()],
        out_specs=[spec() for _ in range(4)],
        out_shape=[jax.ShapeDtypeStruct((r, n), F32)] * 4, compiler_params=_params("parallel"),
    )(w, ct, dm, m, v)


def _sum_pieces_call(pos, part, part_block, recvs, shard_shape, name):
    r, n = recvs[0].shape[1:]
    tr = min(r, 256)
    nrb = r // tr

    def body(pos_ref, p_ref, *refs):
        acc = p_ref[...].astype(F32)
        for r_ref in refs[:-1]:
            for d in range(r_ref.shape[0]):
                acc = acc + r_ref[d].astype(F32)
        refs[-1][...] = acc

    return pl.pallas_call(
        body, name=name,
        grid_spec=pltpu.PrefetchScalarGridSpec(
            num_scalar_prefetch=1, grid=(nrb,),
            in_specs=[pl.BlockSpec((tr, n), lambda i, pos: part_block(i, pos, nrb))] + [
                pl.BlockSpec((rv.shape[0], tr, n), lambda i, pos: (0, i, 0)) for rv in recvs],
            out_specs=pl.BlockSpec((tr, n), lambda i, pos: (pos[1] * nrb + i, 0))),
        out_shape=jax.ShapeDtypeStruct(shard_shape, F32), compiler_params=_params("parallel"),
    )(pos, part, *recvs)


def _coords():
    return lax.axis_index("x"), lax.axis_index("y"), lax.axis_index("c")


def _allgather_sum_call(blk, name, with_sum):
    m_per, n = blk.shape

    def body(x_ref, out_ref, *rest):
        if with_sum:
            sum_ref, send_sems, recv_sems, local_sem = rest
        else:
            send_sems, recv_sems, local_sem = rest
        x, y, c = _coords()
        me, sibling = (x, y, c), (x, y, 1 - c)
        chips = [(1 - x, y), (x, 1 - y), (1 - x, 1 - y)]

        def rows(px, py, pc):
            return out_ref.at[pl.ds((4 * px + 2 * py + pc) * m_per, m_per), :]

        def copy(k, block, to, src=None):
            return pltpu.make_async_remote_copy(
                src_ref=rows(*block) if src is None else src, dst_ref=rows(*block),
                send_sem=send_sems.at[k], recv_sem=recv_sems.at[k], device_id=to, device_id_type=MESH)

        mine = pltpu.make_async_copy(x_ref, rows(*me), local_sem)
        mine.start()
        first = [copy(0, me, sibling, src=x_ref)]
        first += [copy(1 + j, me, (*chip, c), src=x_ref) for j, chip in enumerate(chips)]
        for cp in first:
            cp.start()
        passed = [copy(4 + j, (*chip, c), sibling) for j, chip in enumerate(chips)]
        for j, chip in enumerate(chips):
            copy(1 + j, (*chip, c), me).wait_recv()
            passed[j].start()
        copy(0, sibling, me).wait_recv()
        for j, chip in enumerate(chips):
            copy(4 + j, (*chip, 1 - c), me).wait_recv()
        for cp in first + passed:
            cp.wait_send()
        mine.wait()
        if with_sum:
            acc = out_ref[0:m_per, :]
            for d in range(1, N_DEV):
                acc = acc + out_ref[d * m_per:(d + 1) * m_per, :]
            sum_ref[...] = acc

    vm = pl.BlockSpec(memory_space=pltpu.VMEM)
    out_shape = [jax.ShapeDtypeStruct((N_DEV * m_per, n), F32)]
    if with_sum:
        out_shape.append(jax.ShapeDtypeStruct((m_per, n), F32))
    return pl.pallas_call(
        body, name=name, out_shape=out_shape, in_specs=[vm], out_specs=[vm] * len(out_shape),
        scratch_shapes=[pltpu.SemaphoreType.DMA((7,)), pltpu.SemaphoreType.DMA((7,)), pltpu.SemaphoreType.DMA],
        compiler_params=pltpu.CompilerParams(vmem_limit_bytes=VMEM_LIMIT),
    )(blk)


def _weights_gather_call(wi_full, wo_full):
    hi = D // 2
    ho = W_OUT_SHARD // 2

    def body(wi_in, wo_in, fi_ref, fo_ref, send_sems, recv_sems):
        del wi_in, wo_in
        x, y, c = _coords()
        sibling = (x, y, 1 - c)
        chips = [(1 - x, y), (x, 1 - y), (1 - x, 1 - y)]

        def half(which, px, py, pc):
            j = 2 * px + py
            if which == 0:
                return fi_ref.at[pl.ds(pc * hi, hi), pl.ds(j * W_IN_SHARD, W_IN_SHARD)]
            return fo_ref.at[pl.ds(j * W_OUT_SHARD + pc * ho, ho), :]

        def copy(k, which, block, to):
            return pltpu.make_async_remote_copy(
                src_ref=half(which, *block), dst_ref=half(which, *block), send_sem=send_sems.at[k],
                recv_sem=recv_sems.at[k], device_id=to, device_id_type=MESH)

        first = [copy(6 * w + j, w, (x, y, c), (*chip, c)) for w in range(2) for j, chip in enumerate(chips)]
        for cp in first:
            cp.start()
        passed = []
        for w in range(2):
            for j, chip in enumerate(chips):
                copy(6 * w + j, w, (*chip, c), (x, y, c)).wait_recv()
                cp = copy(6 * w + 3 + j, w, (*chip, c), sibling)
                cp.start()
                passed.append(cp)
        for w in range(2):
            for j, chip in enumerate(chips):
                copy(6 * w + 3 + j, w, (*chip, 1 - c), (x, y, c)).wait_recv()
        for cp in first + passed:
            cp.wait_send()

    anyspec = pl.BlockSpec(memory_space=pl.ANY)
    return pl.pallas_call(
        body, name="weights_gather",
        out_shape=[jax.ShapeDtypeStruct((D, D_IN), BF16), jax.ShapeDtypeStruct((D, D), BF16)],
        in_specs=[anyspec, anyspec], out_specs=[anyspec, anyspec], input_output_aliases={0: 0, 1: 1},
        scratch_shapes=[pltpu.SemaphoreType.DMA((12,)), pltpu.SemaphoreType.DMA((12,))],
    )(wi_full, wo_full)


HBM_SPEC = pl.BlockSpec(memory_space=pltpu.HBM)
SEM_SPEC = pl.BlockSpec(memory_space=pltpu.SEMAPHORE)
SIDE_EFFECT = pltpu.SideEffectType.DATAFLOW_SIDE_EFFECTING


def _peer(x, y, c, q, cb):
    return (1 - x if q & 2 else x, 1 - y if q & 1 else y, 1 - c if cb else c)


def _w_in_piece(slots):
    def piece(part_ref, k, to):
        return part_ref.at[pl.ds(to[2] * (D // 2), D // 2), pl.ds(slots[k] * W_IN_SHARD, W_IN_SHARD)]
    return piece


def _w_out_piece(part_ref, k, to):
    ho = W_OUT_SHARD // 2
    return part_ref.at[pl.ds((2 * to[0] + to[1]) * W_OUT_SHARD + to[2] * ho, ho), :]


def _exchange_start_call(part, rels, piece, slot_shape, name):
    n = len(rels)
    land = lax.empty((n,) + slot_shape, BF16)

    def body(part_ref, land_ref, send_sems, recv_sems, part_thru, land_thru, token):
        x, y, c = _coords()
        for k, (q, cb) in enumerate(rels):
            to = _peer(x, y, c, q, cb)
            pltpu.make_async_remote_copy(src_ref=piece(part_ref, k, to), dst_ref=land_ref.at[k], send_sem=send_sems.at[k],
                                         recv_sem=recv_sems.at[k], device_id=to, device_id_type=MESH).start()
        token[...] = jnp.zeros_like(token)

    return pl.pallas_call(
        body, name=name,
        out_shape=(pltpu.SemaphoreType.DMA((n,)), pltpu.SemaphoreType.DMA((n,)), pltpu.HBM(part.shape, part.dtype),
                   pltpu.HBM(land.shape, land.dtype), jax.ShapeDtypeStruct((8, LANE), F32)),
        in_specs=(HBM_SPEC, HBM_SPEC), out_specs=(SEM_SPEC, SEM_SPEC, HBM_SPEC, HBM_SPEC, pl.BlockSpec(memory_space=pltpu.VMEM)),
        input_output_aliases={0: 2, 1: 3},
        compiler_params=pltpu.CompilerParams(has_side_effects=SIDE_EFFECT),
    )(pltpu.with_memory_space_constraint(part, pltpu.HBM), pltpu.with_memory_space_constraint(land, pltpu.HBM))


def _exchange_wait_call(started, rels, piece, after, name):
    send_sems, recv_sems, part_thru, land_thru, _ = started

    def body(part_ref, land_ref, send_sems, recv_sems, after_ref, part_out, land_out):
        x, y, c = _coords()
        for k, (q, cb) in enumerate(rels):
            to = _peer(x, y, c, q, cb)
            cp = pltpu.make_async_remote_copy(src_ref=piece(part_ref, k, to), dst_ref=land_ref.at[k], send_sem=send_sems.at[k],
                                              recv_sem=recv_sems.at[k], device_id=to, device_id_type=MESH)
            cp.wait_send()
            cp.wait_recv()

    return pl.pallas_call(
        body, name=name,
        out_shape=(pltpu.HBM(part_thru.shape, part_thru.dtype), pltpu.HBM(land_thru.shape, land_thru.dtype)),
        in_specs=(HBM_SPEC, HBM_SPEC, SEM_SPEC, SEM_SPEC, pl.BlockSpec(memory_space=pl.ANY)), out_specs=(HBM_SPEC, HBM_SPEC),
        input_output_aliases={0: 0, 1: 1},
        compiler_params=pltpu.CompilerParams(has_side_effects=SIDE_EFFECT),
    )(part_thru, land_thru, send_sems, recv_sems, after)


def _pair_exchange_call(gi, go):
    hi = D // 2
    ho = W_OUT_SHARD // 2

    def body(gi_in, go_in, fi_ref, fo_ref, send_sems, recv_sems):
        del gi_in, go_in
        x, y, c = _coords()
        sibling = (x, y, 1 - c)
        mine = (fi_ref.at[pl.ds(c * hi, hi), :], fo_ref.at[pl.ds(c * ho, ho), :])
        theirs = (fi_ref.at[pl.ds((1 - c) * hi, hi), :], fo_ref.at[pl.ds((1 - c) * ho, ho), :])
        sends = [pltpu.make_async_remote_copy(src_ref=ref, dst_ref=ref, send_sem=send_sems.at[k], recv_sem=recv_sems.at[k],
                                              device_id=sibling, device_id_type=MESH) for k, ref in enumerate(mine)]
        for cp in sends:
            cp.start()
        for k, ref in enumerate(theirs):
            pltpu.make_async_remote_copy(src_ref=ref, dst_ref=ref, send_sem=send_sems.at[k], recv_sem=recv_sems.at[k],
                                         device_id=sibling, device_id_type=MESH).wait_recv()
        for cp in sends:
            cp.wait_send()

    anyspec = pl.BlockSpec(memory_space=pl.ANY)
    return pl.pallas_call(
        body, name="pair_exchange",
        out_shape=[jax.ShapeDtypeStruct((D, W_IN_SHARD), F32), jax.ShapeDtypeStruct((W_OUT_SHARD, D), F32)],
        in_specs=[anyspec, anyspec], out_specs=[anyspec, anyspec], input_output_aliases={0: 0, 1: 1},
        scratch_shapes=[pltpu.SemaphoreType.DMA((2,)), pltpu.SemaphoreType.DMA((2,))],
    )(gi, go)


def _rope_tables(s):
    inv_freq = 10000.0 ** (-jnp.arange(0, HEAD, 2, dtype=F32) / HEAD)
    ang = jnp.arange(s, dtype=F32)[:, None] * inv_freq[None, :]
    return jnp.tile(jnp.cos(ang), (1, LANE // (HEAD // 2))), jnp.tile(jnp.sin(ang), (1, LANE // (HEAD // 2)))


def _pad_cols(a, n):
    return jnp.pad(a, ((0, 0), (0, n - a.shape[1])))


def kernel(x, c, w_ada, b_ada, norm_g, w_in, ln_v_g, ln_v_b, w_spatial, b_spatial, sinks, w_out, w_ada_final, b_ada_final, final_norm_g, loss_target, m_w_ada, m_b_ada, m_norm_g, m_w_in, m_ln_v_g, m_ln_v_b, m_w_spatial, m_b_spatial, m_sinks, m_w_out, m_w_ada_final, m_b_ada_final, m_final_norm_g, v_w_ada, v_b_ada, v_norm_g, v_w_in, v_ln_v_g, v_ln_v_b, v_w_spatial, v_b_spatial, v_sinks, v_w_out, v_w_ada_final, v_b_ada_final, v_final_norm_g):
    s = x.shape[1]
    ax, ay, ac = _coords()
    chip = 2 * ax + ay
    me = 4 * ax + 2 * ay + ac
    n_ada = w_ada.shape[2]
    n_adaf = w_ada_final.shape[1]

    x2d = x.reshape(s, D)
    tgt = loss_target.reshape(s, D)
    w_ada2, w_in2, w_out2 = w_ada[0], w_in[0], w_out[0]
    b_ada_f2 = b_ada_final.reshape(1, 2 * D)
    gf = final_norm_g.reshape(1, D)

    c_all = _allgather_sum_call(jnp.pad(c, ((0, 7), (0, 0))), "gather_c", False)[0][::8]
    mod_p, c_act = _rowmat_call(c_all, w_ada2, lax.dynamic_slice(b_ada, (0, chip * n_ada), (1, n_ada)), "mod")
    modf_p, _ = _rowmat_call(c_all, w_ada_final, lax.dynamic_slice(b_ada_f2, (0, chip * n_adaf), (1, n_adaf)), "mod_final")
    mods = _allgather_sum_call(jnp.concatenate([mod_p, modf_p], axis=1), "gather_mod", False)[0]
    my_rows = [lax.dynamic_slice(mods, (16 * j + me, 0), (1, n_ada + n_adaf)) for j in range(N_CHIP)]
    mod = jnp.concatenate([r[:, :n_ada] for r in my_rows], axis=1)
    mod_f = jnp.concatenate([r[:, n_ada:] for r in my_rows], axis=1)
    shift, scale, gate = mod[:, :D], mod[:, D:2 * D], mod[:, 2 * D:]
    shift_f, scale_f = mod_f[:, :D], mod_f[:, D:]

    pos = jnp.stack([chip, ac]).astype(jnp.int32)
    w_in_bf, w_out_bf = _weights_gather_call(_cast_into_call(pos, w_in2, (D, D_IN), "cast_w_in"),
                                             _cast_into_call(pos, w_out2, (D, D), "cast_w_out"))

    cos, sin = _rope_tables(s)
    b_sp_t = b_spatial[0].T
    sinks1 = sinks.reshape(N_Q)
    proj, h = _proj_call(x2d, shift, scale, norm_g, w_in_bf)
    y = _mix_fwd_call(proj, cos, sin, ln_v_g, ln_v_b, w_spatial[0], b_sp_t, sinks1)
    dx2, do, st_tail = _tail_call(y, w_out_bf, x2d, tgt, gate, shift_f, scale_f, gf)

    rel_o = [(0, 1), (1, 0), (1, 1), (2, 0), (2, 1), (3, 0), (3, 1)]
    rel_a = [(1, 0), (1, 1), (2, 0), (2, 1)]
    rel_b = [(3, 0), (3, 1), (0, 1)]
    piece_a, piece_b = _w_in_piece([0, 0, 1, 1]), _w_in_piece([0, 0, 1])
    half_in, half_out = (D // 2, W_IN_SHARD), (W_OUT_SHARD // 2, D)

    g_w_out_p = _tn_call(y, do, "grad_w_out")
    st_o = _exchange_start_call(g_w_out_p, rel_o, _w_out_piece, half_out, "send_w_out")
    dy = _dy_call(do, w_out_bf)
    dproj, st_ln, d_wsp, d_bsp_t, d_sink = _mix_bwd_call(
        proj, dy, cos, sin, ln_v_g + st_o[4][0:1, 0:1], ln_v_b, w_spatial[0], jnp.swapaxes(w_spatial[0], 1, 2), b_sp_t, sinks1)
    g_w_in_a = _tn_shards_call(pos, h, dproj, (1, 2), "grad_w_in_a")
    st_a = _exchange_start_call(g_w_in_a, rel_a, piece_a, half_in, "send_w_in_a")
    g_w_in_b = _tn_shards_call(pos, h, dproj, (3, 0), "grad_w_in_b")
    st_b = _exchange_start_call(g_w_in_b, rel_b, piece_b, half_in, "send_w_in_b")
    grad_x, st_dh = _dh_call(dproj, w_in_bf, x2d, dx2, scale + (st_a[4][0:1, 0:1] + st_b[4][0:1, 0:1]), norm_g)

    g_w_out_p, recv_o = _exchange_wait_call(st_o, rel_o, _w_out_piece, st_dh, "wait_w_out")
    _, recv_a = _exchange_wait_call(st_a, rel_a, piece_a, st_dh, "wait_w_in_a")
    g_w_in_b, recv_b = _exchange_wait_call(st_b, rel_b, piece_b, st_dh, "wait_w_in_b")
    g_w_in, g_w_out = _pair_exchange_call(
        _sum_pieces_call(pos, g_w_in_b, lambda i, p, nrb: (p[1] * nrb + i, 1), [recv_a, recv_b], (D, W_IN_SHARD), "sum_w_in"),
        _sum_pieces_call(pos, g_w_out_p, lambda i, p, nrb: ((2 * p[0] + p[1]) * nrb + i, 0), [recv_o], (W_OUT_SHARD, D), "sum_w_out"))

    pack = jnp.concatenate([
        d_wsp.reshape(64, D), st_tail, st_dh, _pad_cols(st_ln, D),
        _pad_cols(d_bsp_t[:, :GROUPS].T, D), _pad_cols(d_sink, D)], axis=0)
    rows = pack.shape[0]
    packs, tot = _allgather_sum_call(pack, "gather_small", True)
    packs = packs.reshape(N_DEV, rows, D)
    dmod_all = jnp.concatenate([packs[:, 72, :], packs[:, 73, :], packs[:, 67, :]], axis=1)
    dmodf_all = jnp.concatenate([packs[:, 64, :], packs[:, 65, :]], axis=1)
    loss = tot[69, 0]
    grads = {
        "b_ada": jnp.concatenate([tot[72:73], tot[73:74], tot[67:68]], axis=1),
        "norm_g": tot[74:75],
        "ln_v_g": tot[80:81, :D_A],
        "ln_v_b": tot[81:82, :D_A],
        "w_spatial": tot[0:64].reshape(GROUPS * BLK, BLK),
        "b_spatial": tot[88:96, :BLK],
        "sinks": tot[96:97, :N_Q],
        "b_ada_final": jnp.concatenate([tot[64:65], tot[65:66]], axis=1),
        "final_norm_g": tot[66:67],
        "w_in": g_w_in,
        "w_out": g_w_out,
    }

    weights = dict(w_ada=w_ada, b_ada=b_ada, norm_g=norm_g, w_in=w_in, ln_v_g=ln_v_g, ln_v_b=ln_v_b, w_spatial=w_spatial,
                   b_spatial=b_spatial, sinks=sinks, w_out=w_out, w_ada_final=w_ada_final, b_ada_final=b_ada_final,
                   final_norm_g=final_norm_g)
    m_in = dict(w_ada=m_w_ada, b_ada=m_b_ada, norm_g=m_norm_g, w_in=m_w_in, ln_v_g=m_ln_v_g, ln_v_b=m_ln_v_b,
                w_spatial=m_w_spatial, b_spatial=m_b_spatial, sinks=m_sinks, w_out=m_w_out, w_ada_final=m_w_ada_final,
                b_ada_final=m_b_ada_final, final_norm_g=m_final_norm_g)
    v_in = dict(w_ada=v_w_ada, b_ada=v_b_ada, norm_g=v_norm_g, w_in=v_w_in, ln_v_g=v_ln_v_g, ln_v_b=v_ln_v_b,
                w_spatial=v_w_spatial, b_spatial=v_b_spatial, sinks=v_sinks, w_out=v_w_out, w_ada_final=v_w_ada_final,
                b_ada_final=v_b_ada_final, final_norm_g=v_final_norm_g)
    c_act_t = c_act.T
    outer = {"w_ada": lax.dynamic_slice(dmod_all, (0, chip * n_ada), (N_DEV, n_ada)),
             "w_ada_final": lax.dynamic_slice(dmodf_all, (0, chip * n_adaf), (N_DEV, n_adaf))}
    out_g, out_d, out_m, out_v = [], [], [], []
    for name, w in weights.items():
        shape = w.shape
        if name in outer:
            shape2 = (D, outer[name].shape[1])
            g, dl, mn, vn = _adam_outer_call(w.reshape(shape2), c_act_t, outer[name], m_in[name].reshape(shape2),
                                             v_in[name].reshape(shape2), "adam_" + name)
        else:
            g = grads[name]
            shape2 = g.shape
            dl, mn, vn = _adam_call(w.reshape(shape2), g, m_in[name].reshape(shape2), v_in[name].reshape(shape2), "adam_" + name)
        out_g.append(g.reshape(shape))
        out_d.append(dl.reshape(shape))
        out_m.append(mn.reshape(shape))
        out_v.append(vn.reshape(shape))
    return (loss, grad_x.reshape(x.shape), *out_g, *out_d, *out_m, *out_v)
```

```python
import jax
import jax.numpy as jnp
from jax import lax
from jax.experimental import pallas as pl
from jax.experimental.pallas import tpu as pltpu

F32 = jnp.float32
BF16 = jnp.bfloat16
MESH = pl.DeviceIdType.MESH

D = 2048
D_A = 1024
D_B = 1024
D_KV = 256
HEAD = 64
N_Q = 16
N_KV = 4
Q_PER_KV = N_Q // N_KV
BLK = 128
GROUPS = 8
D_IN = 5632
OFF_Q, OFF_K, OFF_V, OFF_ZB = 3072, 4096, 4352, 4608
N_CHIP = 4
N_DEV = 8
W_IN_SHARD = D_IN // N_CHIP
W_OUT_SHARD = D // N_CHIP
EPS = 1e-5
SCALE = HEAD ** -0.5
NEG = -1e30
LANE = 128
VMEM_LIMIT = 56 * 1024 * 1024

ADAM_LR, ADAM_B1, ADAM_B2, ADAM_EPS, ADAM_WD, ADAM_STEP = 0.001, 0.9, 0.999, 1e-08, 0.01, 10
ADAM_C1 = 1.0 - ADAM_B1 ** ADAM_STEP
ADAM_C2 = 1.0 - ADAM_B2 ** ADAM_STEP

NT = (((1,), (1,)), ((), ()))
TN = (((0,), (0,)), ((), ()))


def _params(*sem):
    return pltpu.CompilerParams(dimension_semantics=sem, vmem_limit_bytes=VMEM_LIMIT)


def _silu_parts(z):
    sig = 1.0 / (1.0 + jnp.exp(-z))
    return z * sig, sig


def _rot_half(v, first_half):
    return jnp.where(first_half, -pltpu.roll(v, 96, 1), pltpu.roll(v, 32, 1))


def _lane_masks():
    lane = lax.broadcasted_iota(jnp.int32, (BLK, LANE), 1)
    return (lane % HEAD) < (HEAD // 2), lane < HEAD


def _band_valid(first_block_bound, rows=BLK):
    rr = lax.broadcasted_iota(jnp.int32, (rows, 2 * BLK), 0) & (BLK - 1)
    jj = lax.broadcasted_iota(jnp.int32, (rows, 2 * BLK), 1)
    return (jj > rr) & (jj <= rr + BLK) & (jj >= first_block_bound)


def _dup_kv(slab, lo):
    rolled = pltpu.roll(slab, HEAD, 1)
    return jnp.where(lo, slab, rolled).astype(BF16), jnp.where(lo, rolled, slab).astype(BF16)


def _stack_heads(ref, sb, slab, lo, dtype):
    kh, base = sb // 2, 2 * (sb % 2) * BLK
    zero = jnp.zeros_like(slab)
    ref[kh, base:base + BLK, :] = jnp.where(lo, slab, zero).astype(dtype)
    ref[kh, base + BLK:base + 2 * BLK, :] = jnp.where(lo, zero, slab).astype(dtype)


def _unstack_heads(ref, sb, lo):
    kh, base = sb // 2, 2 * (sb % 2) * BLK
    return jnp.where(lo, ref[kh, base:base + BLK, :], ref[kh, base + BLK:base + 2 * BLK, :])


def _sink_column(sinks_ref, kh):
    row = lax.broadcasted_iota(jnp.int32, (Q_PER_KV * BLK, 1), 0)
    col = jnp.full(row.shape, sinks_ref[Q_PER_KV * kh + Q_PER_KV - 1], F32)
    for n in range(Q_PER_KV - 2, -1, -1):
        col = jnp.where(row < (n + 1) * BLK, sinks_ref[Q_PER_KV * kh + n], col)
    return col


def _tril():
    t = lax.broadcasted_iota(jnp.int32, (BLK, BLK), 0)
    s = lax.broadcasted_iota(jnp.int32, (BLK, BLK), 1)
    return s <= t


def _layer_norm_fwd(va, lg, lb):
    mu = jnp.mean(va, axis=-1, keepdims=True)
    xc = va - mu
    rstd = lax.rsqrt(jnp.mean(xc * xc, axis=-1, keepdims=True) + EPS)
    vhat = xc * rstd
    return vhat, rstd, vhat * lg + lb


def _softmax_sink(qm, kexp, valid, sink):
    s = lax.dot_general(qm, kexp, NT, preferred_element_type=F32) * SCALE
    s = jnp.where(valid, s, NEG)
    m = jnp.maximum(jnp.max(s, axis=-1, keepdims=True), sink)
    p = jnp.exp(s - m)
    esink = jnp.exp(sink - m)
    den = jnp.sum(p, axis=-1, keepdims=True) + esink
    return p / den, esink / den


def _rowmat_call(c_all, w, b, name):
    n = w.shape[1]
    tn = 512

    def body(c_ref, w_ref, b_ref, o_ref, ca_ref):
        ca, _ = _silu_parts(c_ref[...])
        ca_ref[...] = ca
        o_ref[...] = jnp.dot(ca.astype(BF16), w_ref[...].astype(BF16), preferred_element_type=F32) + b_ref[...]

    return pl.pallas_call(
        body, name=name, grid=(n // tn,),
        in_specs=[pl.BlockSpec((N_DEV, D), lambda j: (0, 0)), pl.BlockSpec((D, tn), lambda j: (0, j)),
                  pl.BlockSpec((1, tn), lambda j: (0, j))],
        out_specs=[pl.BlockSpec((N_DEV, tn), lambda j: (0, j)), pl.BlockSpec((N_DEV, D), lambda j: (0, 0))],
        out_shape=[jax.ShapeDtypeStruct((N_DEV, n), F32), jax.ShapeDtypeStruct((N_DEV, D), F32)],
        compiler_params=_params("arbitrary"),
    )(c_all, w, b)


def _cast_into_call(pos, w, full_shape, name):
    r, n = w.shape
    tr = min(r, 512)
    by_cols = full_shape[0] == r
    nrb = r // tr

    def body(pos_ref, w_ref, o_ref):
        o_ref[...] = w_ref[...].astype(BF16)

    out_map = (lambda i, pos: (i, pos[0])) if by_cols else (lambda i, pos: (pos[0] * nrb + i, 0))
    return pl.pallas_call(
        body, name=name,
        grid_spec=pltpu.PrefetchScalarGridSpec(
            num_scalar_prefetch=1, grid=(nrb,),
            in_specs=[pl.BlockSpec((tr, n), lambda i, pos: (i, 0))], out_specs=pl.BlockSpec((tr, n), out_map)),
        out_shape=jax.ShapeDtypeStruct(full_shape, BF16), compiler_params=_params("parallel"),
    )(pos, w)


def _proj_call(x, shift, scale, norm_g, w_bf):
    s = x.shape[0]
    tm = min(s, 1024)
    tn = 512

    def body(x_ref, sh_ref, sc_ref, g_ref, w_ref, proj_ref, h_ref):
        @pl.when(pl.program_id(1) == 0)
        def _():
            xv = x_ref[...]
            r = lax.rsqrt(jnp.mean(xv * xv, axis=-1, keepdims=True) + EPS)
            h_ref[...] = ((xv * r * g_ref[...]) * (1.0 + sc_ref[...]) + sh_ref[...]).astype(BF16)

        proj_ref[...] = jnp.dot(h_ref[...], w_ref[...], preferred_element_type=F32)

    vec = pl.BlockSpec((1, D), lambda i, j: (0, 0))
    return pl.pallas_call(
        body, name="proj", grid=(s // tm, D_IN // tn),
        in_specs=[pl.BlockSpec((tm, D), lambda i, j: (i, 0)), vec, vec, vec, pl.BlockSpec((D, tn), lambda i, j: (0, j))],
        out_specs=[pl.BlockSpec((tm, tn), lambda i, j: (i, j)), pl.BlockSpec((tm, D), lambda i, j: (i, 0))],
        out_shape=[jax.ShapeDtypeStruct((s, D_IN), F32), jax.ShapeDtypeStruct((s, D), BF16)],
        compiler_params=_params("parallel", "arbitrary"),
    )(x, shift, scale, norm_g, w_bf)


def _proj_specs(rev_nb=None):
    if rev_nb is None:
        row = lambda i: i
    else:
        row = lambda i: rev_nb - 1 - i
    wide = lambda col: pl.BlockSpec((BLK, D_A), lambda i: (row(i), col))
    kv = lambda col: pl.BlockSpec((BLK, D_KV), lambda i: (row(i), col))
    half = lambda col: pl.BlockSpec((BLK, 512), lambda i: (row(i), col))
    return [wide(0), wide(1), wide(2), wide(3), kv(OFF_K // D_KV), kv(OFF_V // D_KV), half(OFF_ZB // 512), half(OFF_ZB // 512 + 1)]


def _mix_fwd_call(proj, cos, sin, ln_g, ln_b, w_sp, b_sp_t, sinks):
    s = proj.shape[0]
    nb = s // BLK

    def body(ua_ref, va_ref, za_ref, q_ref, k_ref, v_ref, zb0_ref, zb1_ref, cos_ref, sin_ref, lg_ref, lb_ref,
             w_ref, bt_ref, sinks_ref, y_ref, kdup_ref, vdup_ref, qm_ref, ost_ref):
        i = pl.program_id(0)
        first_half, lo = _lane_masks()
        cos_t = cos_ref[...]
        sin_t = sin_ref[...]

        _, _, vln = _layer_norm_fwd(va_ref[...], lg_ref[...], lb_ref[...])
        tril = _tril()
        for g in range(GROUPS):
            cols = slice(g * BLK, (g + 1) * BLK)
            wg = jnp.where(tril, w_ref[g], 0.0).astype(BF16)
            sg = jnp.dot(wg, vln[:, cols].astype(BF16), preferred_element_type=F32) + bt_ref[:, g:g + 1]
            gate, _ = _silu_parts(za_ref[:, cols])
            y_ref[:, cols] = (ua_ref[:, cols] * sg * gate).astype(BF16)

        @pl.when(i == 0)
        def _():
            kdup_ref[:, 0:BLK, :] = jnp.zeros((N_KV, BLK, LANE), BF16)
            vdup_ref[:, 0:BLK, :] = jnp.zeros((N_KV, BLK, LANE), BF16)

        @pl.when(i > 0)
        def _():
            kdup_ref[:, 0:BLK, :] = kdup_ref[:, BLK:2 * BLK, :]
            vdup_ref[:, 0:BLK, :] = vdup_ref[:, BLK:2 * BLK, :]

        for ks in range(2):
            cols = slice(ks * LANE, (ks + 1) * LANE)
            kslab = k_ref[:, cols]
            kr = kslab * cos_t + _rot_half(kslab, first_half) * sin_t
            for n, (kd, vd) in enumerate(zip(_dup_kv(kr, lo), _dup_kv(v_ref[:, cols], lo))):
                kdup_ref[2 * ks + n, BLK:2 * BLK, :] = kd
                vdup_ref[2 * ks + n, BLK:2 * BLK, :] = vd
        for sb in range(8):
            qslab = q_ref[:, sb * LANE:(sb + 1) * LANE]
            _stack_heads(qm_ref, sb, qslab * cos_t + _rot_half(qslab, first_half) * sin_t, lo, BF16)

        valid = _band_valid(jnp.where(i > 0, 0, BLK), Q_PER_KV * BLK)

        def kv_head(kh, carry):
            probs, _ = _softmax_sink(qm_ref[kh], kdup_ref[kh], valid, _sink_column(sinks_ref, kh))
            ost_ref[kh] = jnp.dot(probs.astype(BF16), vdup_ref[kh], preferred_element_type=F32)
            return carry

        lax.fori_loop(0, N_KV, kv_head, 0)
        for sb in range(8):
            cols = slice(sb * LANE, (sb + 1) * LANE)
            zb = zb0_ref[:, cols] if sb < 4 else zb1_ref[:, (sb - 4) * LANE:(sb - 3) * LANE]
            gate, _ = _silu_parts(zb)
            y_ref[:, D_A + sb * LANE:D_A + (sb + 1) * LANE] = (_unstack_heads(ost_ref, sb, lo) * gate).astype(BF16)

    tab = pl.BlockSpec((BLK, LANE), lambda i: (i, 0))
    return pl.pallas_call(
        body, name="mix_fwd", grid=(nb,),
        in_specs=_proj_specs() + [
            tab, tab, pl.BlockSpec((1, D_A), lambda i: (0, 0)), pl.BlockSpec((1, D_A), lambda i: (0, 0)),
            pl.BlockSpec((GROUPS, BLK, BLK), lambda i: (0, 0, 0)), pl.BlockSpec((BLK, GROUPS), lambda i: (0, 0)),
            pl.BlockSpec(memory_space=pltpu.SMEM)],
        out_specs=pl.BlockSpec((BLK, 2 * D_A), lambda i: (i, 0)),
        out_shape=jax.ShapeDtypeStruct((s, 2 * D_A), BF16),
        scratch_shapes=[pltpu.VMEM((N_KV, 2 * BLK, LANE), BF16), pltpu.VMEM((N_KV, 2 * BLK, LANE), BF16),
                        pltpu.VMEM((N_KV, Q_PER_KV * BLK, LANE), BF16), pltpu.VMEM((N_KV, Q_PER_KV * BLK, LANE), F32)],
        compiler_params=_params("arbitrary"),
    )(proj, proj, proj, proj, proj, proj, proj, proj, cos, sin, ln_g, ln_b, w_sp, b_sp_t, sinks)


def _tail_call(y, w_out_bf, x, target, gate, shift_f, scale_f, gf):
    s = x.shape[0]
    tm = min(s, 256)
    nsteps = s // tm

    def body(y_ref, w_ref, x_ref, t_ref, gate_ref, shf_ref, scf_ref, gf_ref, dx2_ref, do_ref, st_ref):
        i = pl.program_id(0)

        @pl.when(i == 0)
        def _():
            st_ref[...] = jnp.zeros((8, D), F32)

        o = jnp.dot(y_ref[...], w_ref[...], preferred_element_type=F32)
        gate_v = gate_ref[...]
        x2 = x_ref[...] + gate_v * o
        r2 = lax.rsqrt(jnp.mean(x2 * x2, axis=-1, keepdims=True) + EPS)
        xn2 = x2 * r2
        hn2 = xn2 * gf_ref[...]
        one_sc = 1.0 + scf_ref[...]
        err = hn2 * one_sc + shf_ref[...] - t_ref[...]
        dout = err * (1.0 / D)
        dhn2 = dout * one_sc
        dxn2 = dhn2 * gf_ref[...]
        dx2 = r2 * (dxn2 - xn2 * jnp.mean(dxn2 * xn2, axis=-1, keepdims=True))
        dx2_ref[...] = dx2
        do_ref[...] = (dx2 * gate_v).astype(BF16)
        st_ref[0:1, :] += jnp.sum(dout, axis=0, keepdims=True)
        st_ref[1:2, :] += jnp.sum(dout * hn2, axis=0, keepdims=True)
        st_ref[2:3, :] += jnp.sum(dhn2 * xn2, axis=0, keepdims=True)
        st_ref[3:4, :] += jnp.sum(dx2 * o, axis=0, keepdims=True)
        st_ref[4:5, :] += jnp.sum(err * err, axis=0, keepdims=True)

        @pl.when(i == nsteps - 1)
        def _():
            st_ref[5:6, :] = jnp.full((1, D), 0.5 / D, F32) * jnp.sum(st_ref[4:5, :])

    vec = pl.BlockSpec((1, D), lambda i: (0, 0))
    rows = lambda: pl.BlockSpec((tm, D), lambda i: (i, 0))
    return pl.pallas_call(
        body, name="tail", grid=(nsteps,),
        in_specs=[rows(), pl.BlockSpec((D, D), lambda i: (0, 0)), rows(), rows(), vec, vec, vec, vec],
        out_specs=[rows(), rows(), pl.BlockSpec((8, D), lambda i: (0, 0))],
        out_shape=[jax.ShapeDtypeStruct((s, D), F32), jax.ShapeDtypeStruct((s, D), BF16), jax.ShapeDtypeStruct((8, D), F32)],
        compiler_params=_params("arbitrary"),
    )(y, w_out_bf, x, target, gate, shift_f, scale_f, gf)


def _dy_call(do, w_out_bf):
    s = do.shape[0]
    tm = min(s, 512)

    def body(do_ref, w_ref, dy_ref):
        dy_ref[...] = lax.dot_general(do_ref[...], w_ref[...], NT, preferred_element_type=F32)

    return pl.pallas_call(
        body, name="dy", grid=(s // tm,),
        in_specs=[pl.BlockSpec((tm, D), lambda i: (i, 0)), pl.BlockSpec((D, D), lambda i: (0, 0))],
        out_specs=pl.BlockSpec((tm, D), lambda i: (i, 0)),
        out_shape=jax.ShapeDtypeStruct((s, D), F32), compiler_params=_params("parallel"),
    )(do, w_out_bf)


def _tn_call(a, b, name):
    s, m = a.shape
    n = b.shape[1]
    tn = 512
    ts = min(s, 1024)
    nk = s // ts

    def body(a_ref, b_ref, o_ref, acc_ref):
        k = pl.program_id(1)

        @pl.when(k == 0)
        def _():
            acc_ref[...] = jnp.zeros((m, tn), F32)

        acc_ref[...] += lax.dot_general(a_ref[...], b_ref[...], TN, preferred_element_type=F32)

        @pl.when(k == nk - 1)
        def _():
            o_ref[...] = acc_ref[...].astype(BF16)

    return pl.pallas_call(
        body, name=name, grid=(n // tn, nk),
        in_specs=[pl.BlockSpec((ts, m), lambda j, k: (k, 0)), pl.BlockSpec((ts, tn), lambda j, k: (k, j))],
        out_specs=pl.BlockSpec((m, tn), lambda j, k: (0, j)),
        out_shape=jax.ShapeDtypeStruct((m, n), BF16),
        scratch_shapes=[pltpu.VMEM((m, tn), F32)],
        compiler_params=_params("parallel", "arbitrary"),
    )(a, b)


def _tn_shards_call(pos, a, b, qs, name):
    s, m = a.shape
    ts = min(s, 512)
    nk = s // ts

    def body(pos_ref, a_ref, b_ref, o_ref, acc_ref):
        k = pl.program_id(1)

        @pl.when(k == 0)
        def _():
            acc_ref[...] = jnp.zeros((m, W_IN_SHARD), F32)

        acc_ref[...] += lax.dot_general(a_ref[...], b_ref[...], TN, preferred_element_type=F32)

        @pl.when(k == nk - 1)
        def _():
            o_ref[...] = acc_ref[...].astype(BF16)

    def shard(j, pos):
        q = qs[0]
        for n in range(1, len(qs)):
            q = jnp.where(j == n, qs[n], q)
        return jnp.bitwise_xor(pos[0], q)

    return pl.pallas_call(
        body, name=name,
        grid_spec=pltpu.PrefetchScalarGridSpec(
            num_scalar_prefetch=1, grid=(len(qs), nk),
            in_specs=[pl.BlockSpec((ts, m), lambda j, k, pos: (k, 0)),
                      pl.BlockSpec((ts, W_IN_SHARD), lambda j, k, pos: (k, shard(j, pos)))],
            out_specs=pl.BlockSpec((m, W_IN_SHARD), lambda j, k, pos: (0, j)),
            scratch_shapes=[pltpu.VMEM((m, W_IN_SHARD), F32)]),
        out_shape=jax.ShapeDtypeStruct((m, len(qs) * W_IN_SHARD), BF16),
        compiler_params=_params("parallel", "arbitrary"),
    )(pos, a, b)


def _mix_bwd_call(proj, dy, cos, sin, ln_g, ln_b, w_sp, w_sp_t, b_sp_t, sinks):
    s = proj.shape[0]
    nb = s // BLK
    rev = lambda i: nb - 1 - i
    prev = lambda i: jnp.maximum(nb - 2 - i, 0)

    def body(ua_ref, va_ref, za_ref, q_ref, k_ref, v_ref, zb0_ref, zb1_ref, kp_ref, vp_ref, dy_ref,
             cos_ref, sin_ref, cosp_ref, sinp_ref, lg_ref, lb_ref, w_ref, wt_ref, bt_ref, sinks_ref,
             dp_ref, lnst_ref, dw_ref, dbt_ref, dsink_ref,
             kdup_ref, vdup_ref, dvln_ref, qm_ref, dom_ref, ost_ref, dqst_ref, dkdup_ref, dvdup_ref, kcar_ref, vcar_ref):
        i = pl.program_id(0)
        first_half, lo = _lane_masks()
        lane8 = lax.broadcasted_iota(jnp.int32, (8, LANE), 1)
        cos_t = cos_ref[...]
        sin_t = sin_ref[...]

        @pl.when(i == 0)
        def _():
            lnst_ref[...] = jnp.zeros((8, D_A), F32)
            dw_ref[...] = jnp.zeros((GROUPS, BLK, BLK), F32)
            dbt_ref[...] = jnp.zeros((BLK, LANE), F32)
            dsink_ref[...] = jnp.zeros((8, LANE), F32)
            kcar_ref[...] = jnp.zeros((BLK, D_KV), F32)
            vcar_ref[...] = jnp.zeros((BLK, D_KV), F32)

        vhat, rstd, vln = _layer_norm_fwd(va_ref[...], lg_ref[...], lb_ref[...])
        tril = _tril()
        triu = jnp.logical_not(tril) | (lax.broadcasted_iota(jnp.int32, (BLK, BLK), 0) == lax.broadcasted_iota(jnp.int32, (BLK, BLK), 1))
        lane_b = lax.broadcasted_iota(jnp.int32, (BLK, LANE), 1)
        db_acc = jnp.zeros((BLK, LANE), F32)
        for g in range(GROUPS):
            cols = slice(g * BLK, (g + 1) * BLK)
            vln_g = vln[:, cols].astype(BF16)
            wg = jnp.where(tril, w_ref[g], 0.0).astype(BF16)
            sg = jnp.dot(wg, vln_g, preferred_element_type=F32) + bt_ref[:, g:g + 1]
            za = za_ref[:, cols]
            gate, sig = _silu_parts(za)
            ua = ua_ref[:, cols]
            dya_g = dy_ref[:, cols]
            dya = dya_g * gate
            dp_ref[:, cols] = (dya * sg).astype(BF16)
            dp_ref[:, 2 * D_A + g * BLK:2 * D_A + (g + 1) * BLK] = (
                dya_g * (ua * sg) * (sig * (1.0 + za * (1.0 - sig)))).astype(BF16)
            ds = dya * ua
            ds_b = ds.astype(BF16)
            wtg = jnp.where(triu, wt_ref[g], 0.0).astype(BF16)
            dvln_ref[:, cols] = jnp.dot(wtg, ds_b, preferred_element_type=F32)
            dw_ref[g] += jnp.where(tril, lax.dot_general(ds_b, vln_g, NT, preferred_element_type=F32), 0.0)
            db_acc = db_acc + jnp.where(lane_b == g, jnp.sum(ds, axis=-1, keepdims=True), 0.0)
        dbt_ref[...] += db_acc
        dvln = dvln_ref[...]
        lnst_ref[0:1, :] += jnp.sum(dvln * vhat, axis=0, keepdims=True)
        lnst_ref[1:2, :] += jnp.sum(dvln, axis=0, keepdims=True)
        dvhat = dvln * lg_ref[...]
        m1 = jnp.mean(dvhat, axis=-1, keepdims=True)
        m2 = jnp.mean(dvhat * vhat, axis=-1, keepdims=True)
        dp_ref[:, D_A:2 * D_A] = (rstd * (dvhat - m1 - vhat * m2)).astype(BF16)

        cosp = cosp_ref[...]
        sinp = sinp_ref[...]
        for ks in range(2):
            cols = slice(ks * LANE, (ks + 1) * LANE)
            kslab = k_ref[:, cols]
            kr = kslab * cos_t + _rot_half(kslab, first_half) * sin_t
            kpslab = kp_ref[:, cols]
            kpr = kpslab * cosp + _rot_half(kpslab, first_half) * sinp
            for n, (kc, vc, kp, vp) in enumerate(zip(_dup_kv(kr, lo), _dup_kv(v_ref[:, cols], lo),
                                                     _dup_kv(kpr, lo), _dup_kv(vp_ref[:, cols], lo))):
                kdup_ref[2 * ks + n, BLK:2 * BLK, :] = kc
                vdup_ref[2 * ks + n, BLK:2 * BLK, :] = vc
                kdup_ref[2 * ks + n, 0:BLK, :] = kp
                vdup_ref[2 * ks + n, 0:BLK, :] = vp
        for sb in range(8):
            cols = slice(sb * LANE, (sb + 1) * LANE)
            qslab = q_ref[:, cols]
            _stack_heads(qm_ref, sb, qslab * cos_t + _rot_half(qslab, first_half) * sin_t, lo, BF16)
            zb = zb0_ref[:, cols] if sb < 4 else zb1_ref[:, (sb - 4) * LANE:(sb - 3) * LANE]
            gate, _ = _silu_parts(zb)
            _stack_heads(dom_ref, sb, dy_ref[:, D_A + sb * LANE:D_A + (sb + 1) * LANE] * gate, lo, F32)

        valid = _band_valid(jnp.where(i < nb - 1, 0, BLK), Q_PER_KV * BLK)

        def kv_head(kh, dsink_acc):
            qm = qm_ref[kh]
            kd = kdup_ref[kh]
            vd = vdup_ref[kh]
            probs, psink = _softmax_sink(qm, kd, valid, _sink_column(sinks_ref, kh))
            probs_b = probs.astype(BF16)
            o = jnp.dot(probs_b, vd, preferred_element_type=F32)
            ost_ref[kh] = o
            dom = dom_ref[kh]
            dom_b = dom.astype(BF16)
            delta = jnp.sum(dom * o, axis=-1, keepdims=True)
            dpr = lax.dot_general(dom_b, vd, NT, preferred_element_type=F32)
            dss = (probs * (dpr - delta) * SCALE).astype(BF16)
            sd = psink * delta
            for n in range(Q_PER_KV):
                dsink_acc = dsink_acc + jnp.where(lane8 == Q_PER_KV * kh + n, -jnp.sum(sd[n * BLK:(n + 1) * BLK]), 0.0)
            dqst_ref[kh] = jnp.dot(dss, kd, preferred_element_type=F32)
            dkdup_ref[kh] = lax.dot_general(dss, qm, TN, preferred_element_type=F32)
            dvdup_ref[kh] = lax.dot_general(probs_b, dom_b, TN, preferred_element_type=F32)
            return dsink_acc

        dsink_acc = lax.fori_loop(0, N_KV, kv_head, jnp.zeros((8, LANE), F32))
        row0 = lax.broadcasted_iota(jnp.int32, (8, LANE), 0) == 0
        dsink_ref[...] += jnp.where(row0, dsink_acc, 0.0)

        for sb in range(8):
            cols = slice(sb * LANE, (sb + 1) * LANE)
            zb = zb0_ref[:, cols] if sb < 4 else zb1_ref[:, (sb - 4) * LANE:(sb - 3) * LANE]
            _, sig = _silu_parts(zb)
            dyb = dy_ref[:, D_A + sb * LANE:D_A + (sb + 1) * LANE]
            dp_ref[:, OFF_ZB + sb * LANE:OFF_ZB + (sb + 1) * LANE] = (
                dyb * _unstack_heads(ost_ref, sb, lo) * (sig * (1.0 + zb * (1.0 - sig)))).astype(BF16)
            dq_r = _unstack_heads(dqst_ref, sb, lo)
            dp_ref[:, OFF_Q + sb * LANE:OFF_Q + (sb + 1) * LANE] = (
                dq_r * cos_t - _rot_half(dq_r * sin_t, first_half)).astype(BF16)

        lo2 = lax.broadcasted_iota(jnp.int32, (2 * BLK, LANE), 1) < HEAD
        for ks in range(2):
            cols = slice(ks * LANE, (ks + 1) * LANE)
            ka = dkdup_ref[2 * ks]
            kb = dkdup_ref[2 * ks + 1]
            dk_band = jnp.where(lo2, ka + pltpu.roll(ka, HEAD, 1), kb + pltpu.roll(kb, HEAD, 1))
            va_ = dvdup_ref[2 * ks]
            vb_ = dvdup_ref[2 * ks + 1]
            dv_band = jnp.where(lo2, va_ + pltpu.roll(va_, HEAD, 1), vb_ + pltpu.roll(vb_, HEAD, 1))
            dkr = dk_band[BLK:2 * BLK, :] + kcar_ref[:, cols]
            dp_ref[:, OFF_K + ks * LANE:OFF_K + (ks + 1) * LANE] = (
                dkr * cos_t - _rot_half(dkr * sin_t, first_half)).astype(BF16)
            dp_ref[:, OFF_V + ks * LANE:OFF_V + (ks + 1) * LANE] = (
                dv_band[BLK:2 * BLK, :] + vcar_ref[:, cols]).astype(BF16)
            kcar_ref[:, cols] = dk_band[0:BLK, :]
            vcar_ref[:, cols] = dv_band[0:BLK, :]

    tab = pl.BlockSpec((BLK, LANE), lambda i: (rev(i), 0))
    tabp = pl.BlockSpec((BLK, LANE), lambda i: (prev(i), 0))
    kvp = lambda col: pl.BlockSpec((BLK, D_KV), lambda i: (prev(i), col))
    vec = pl.BlockSpec((1, D_A), lambda i: (0, 0))
    w3 = pl.BlockSpec((GROUPS, BLK, BLK), lambda i: (0, 0, 0))
    return pl.pallas_call(
        body, name="mix_bwd", grid=(nb,),
        in_specs=_proj_specs(nb) + [
            kvp(OFF_K // D_KV), kvp(OFF_V // D_KV), pl.BlockSpec((BLK, 2 * D_A), lambda i: (rev(i), 0)),
            tab, tab, tabp, tabp, vec, vec, w3, w3, pl.BlockSpec((BLK, GROUPS), lambda i: (0, 0)),
            pl.BlockSpec(memory_space=pltpu.SMEM)],
        out_specs=[pl.BlockSpec((BLK, D_IN), lambda i: (rev(i), 0)), pl.BlockSpec((8, D_A), lambda i: (0, 0)), w3,
                   pl.BlockSpec((BLK, LANE), lambda i: (0, 0)), pl.BlockSpec((8, LANE), lambda i: (0, 0))],
        out_shape=[jax.ShapeDtypeStruct((s, D_IN), BF16), jax.ShapeDtypeStruct((8, D_A), F32),
                   jax.ShapeDtypeStruct((GROUPS, BLK, BLK), F32), jax.ShapeDtypeStruct((BLK, LANE), F32),
                   jax.ShapeDtypeStruct((8, LANE), F32)],
        scratch_shapes=[pltpu.VMEM((N_KV, 2 * BLK, LANE), BF16), pltpu.VMEM((N_KV, 2 * BLK, LANE), BF16),
                        pltpu.VMEM((BLK, D_A), F32), pltpu.VMEM((N_KV, Q_PER_KV * BLK, LANE), BF16),
                        pltpu.VMEM((N_KV, Q_PER_KV * BLK, LANE), F32), pltpu.VMEM((N_KV, Q_PER_KV * BLK, LANE), F32),
                        pltpu.VMEM((N_KV, Q_PER_KV * BLK, LANE), F32), pltpu.VMEM((N_KV, 2 * BLK, LANE), F32),
                        pltpu.VMEM((N_KV, 2 * BLK, LANE), F32), pltpu.VMEM((BLK, D_KV), F32), pltpu.VMEM((BLK, D_KV), F32)],
        compiler_params=_params("arbitrary"),
    )(proj, proj, proj, proj, proj, proj, proj, proj, proj, proj, dy, cos, sin, cos, sin, ln_g, ln_b, w_sp, w_sp_t,
      b_sp_t, sinks)


def _dh_call(dproj, w_bf, x, dx2, scale, norm_g):
    s = x.shape[0]
    tm = min(s, 512)
    tk = 512
    nk = D_IN // tk

    def body(dp_ref, w_ref, x_ref, dx2_ref, sc_ref, g_ref, gx_ref, st_ref, acc_ref):
        i = pl.program_id(0)
        k = pl.program_id(1)

        @pl.when((i == 0) & (k == 0))
        def _():
            st_ref[...] = jnp.zeros((8, D), F32)

        @pl.when(k == 0)
        def _():
            acc_ref[...] = jnp.zeros((tm, D), F32)

        acc_ref[...] += lax.dot_general(dp_ref[...], w_ref[...], NT, preferred_element_type=F32)

        @pl.when(k == nk - 1)
        def _():
            dh = acc_ref[...]
            xv = x_ref[...]
            r = lax.rsqrt(jnp.mean(xv * xv, axis=-1, keepdims=True) + EPS)
            xn = xv * r
            g = g_ref[...]
            dhn = dh * (1.0 + sc_ref[...])
            dxn = dhn * g
            gx_ref[...] = dx2_ref[...] + r * (dxn - xn * jnp.mean(dxn * xn, axis=-1, keepdims=True))
            st_ref[0:1, :] += jnp.sum(dh, axis=0, keepdims=True)
            st_ref[1:2, :] += jnp.sum(dh * (xn * g), axis=0, keepdims=True)
            st_ref[2:3, :] += jnp.sum(dhn * xn, axis=0, keepdims=True)

    vec = pl.BlockSpec((1, D), lambda i, k: (0, 0))
    rows = lambda: pl.BlockSpec((tm, D), lambda i, k: (i, 0))
    return pl.pallas_call(
        body, name="dh", grid=(s // tm, nk),
        in_specs=[pl.BlockSpec((tm, tk), lambda i, k: (i, k)), pl.BlockSpec((D, tk), lambda i, k: (0, k)), rows(), rows(), vec, vec],
        out_specs=[rows(), pl.BlockSpec((8, D), lambda i, k: (0, 0))],
        out_shape=[jax.ShapeDtypeStruct((s, D), F32), jax.ShapeDtypeStruct((8, D), F32)],
        scratch_shapes=[pltpu.VMEM((tm, D), F32)],
        compiler_params=_params("arbitrary", "arbitrary"),
    )(dproj, w_bf, x, dx2, scale, norm_g)


def _adam_math(w, g, m, v):
    m_new = ADAM_B1 * m + (1.0 - ADAM_B1) * g
    v_new = ADAM_B2 * v + (1.0 - ADAM_B2) * (g * g)
    m_hat = m_new / ADAM_C1
    v_hat = v_new / ADAM_C2
    delta = -ADAM_LR * (m_hat / (jnp.sqrt(v_hat) + ADAM_EPS) + ADAM_WD * w)
    return delta, m_new, v_new


def _adam_call(w, g, m, v, name):
    r, n = w.shape
    tr = r if r * n * 4 <= (1 << 20) else max(8, (1 << 20) // (n * 4) // 8 * 8)
    while r % tr:
        tr -= 8

    def body(w_ref, g_ref, m_ref, v_ref, d_ref, mo_ref, vo_ref):
        d_ref[...], mo_ref[...], vo_ref[...] = _adam_math(w_ref[...], g_ref[...], m_ref[...], v_ref[...])

    spec = lambda: pl.BlockSpec((tr, n), lambda i: (i, 0))
    return pl.pallas_call(
        body, name=name, grid=(r // tr,), in_specs=[spec() for _ in range(4)], out_specs=[spec() for _ in range(3)],
        out_shape=[jax.ShapeDtypeStruct((r, n), F32)] * 3, compiler_params=_params("parallel"),
    )(w, g, m, v)


def _adam_outer_call(w, ct, dm, m, v, name):
    r, n = w.shape
    tr = 128

    def body(w_ref, ct_ref, dm_ref, m_ref, v_ref, g_ref, d_ref, mo_ref, vo_ref):
        g = ct_ref[:, 0:1] * dm_ref[0:1, :]
        for b in range(1, N_DEV):
            g = g + ct_ref[:, b:b + 1] * dm_ref[b:b + 1, :]
        g_ref[...] = g
        d_ref[...], mo_ref[...], vo_ref[...] = _adam_math(w_ref[...], g, m_ref[...], v_ref[...])

    spec = lambda: pl.BlockSpec((tr, n), lambda i: (i, 0))
    return pl.pallas_call(
        body, name=name, grid=(r // tr,),
        in_specs=[spec(), pl.BlockSpec((tr, N_DEV), lambda i: (i, 0)), pl.BlockSpec((N_DEV, n), lambda i: (0, 0)), spec(), spec()],
        out_specs=[spec() for _ in range(4)],
        out_shape=[jax.ShapeDtypeStruct((r, n), F32)] * 4, compiler_params=_params("parallel"),
    )(w, ct, dm, m, v)


def _sum_pieces_call(pos, part, part_block, recvs, shard_shape, name):
    r, n = recvs[0].shape[1:]
    tr = min(r, 256)
    nrb = r // tr

    def body(pos_ref, p_ref, *refs):
        acc = p_ref[...].astype(F32)
        for r_ref in refs[:-1]:
            for d in range(r_ref.shape[0]):
                acc = acc + r_ref[d].astype(F32)
        refs[-1][...] = acc

    return pl.pallas_call(
        body, name=name,
        grid_spec=pltpu.PrefetchScalarGridSpec(
            num_scalar_prefetch=1, grid=(nrb,),
            in_specs=[pl.BlockSpec((tr, n), lambda i, pos: part_block(i, pos, nrb))] + [
                pl.BlockSpec((rv.shape[0], tr, n), lambda i, pos: (0, i, 0)) for rv in recvs],
            out_specs=pl.BlockSpec((tr, n), lambda i, pos: (pos[1] * nrb + i, 0))),
        out_shape=jax.ShapeDtypeStruct(shard_shape, F32), compiler_params=_params("parallel"),
    )(pos, part, *recvs)


def _coords():
    return lax.axis_index("x"), lax.axis_index("y"), lax.axis_index("c")


def _allgather_sum_call(blk, name, with_sum):
    m_per, n = blk.shape

    def body(x_ref, out_ref, *rest):
        if with_sum:
            sum_ref, send_sems, recv_sems, local_sem = rest
        else:
            send_sems, recv_sems, local_sem = rest
        x, y, c = _coords()
        me, sibling = (x, y, c), (x, y, 1 - c)
        chips = [(1 - x, y), (x, 1 - y), (1 - x, 1 - y)]

        def rows(px, py, pc):
            return out_ref.at[pl.ds((4 * px + 2 * py + pc) * m_per, m_per), :]

        def copy(k, block, to, src=None):
            return pltpu.make_async_remote_copy(
                src_ref=rows(*block) if src is None else src, dst_ref=rows(*block),
                send_sem=send_sems.at[k], recv_sem=recv_sems.at[k], device_id=to, device_id_type=MESH)

        mine = pltpu.make_async_copy(x_ref, rows(*me), local_sem)
        mine.start()
        first = [copy(0, me, sibling, src=x_ref)]
        first += [copy(1 + j, me, (*chip, c), src=x_ref) for j, chip in enumerate(chips)]
        for cp in first:
            cp.start()
        passed = [copy(4 + j, (*chip, c), sibling) for j, chip in enumerate(chips)]
        for j, chip in enumerate(chips):
            copy(1 + j, (*chip, c), me).wait_recv()
            passed[j].start()
        copy(0, sibling, me).wait_recv()
        for j, chip in enumerate(chips):
            copy(4 + j, (*chip, 1 - c), me).wait_recv()
        for cp in first + passed:
            cp.wait_send()
        mine.wait()
        if with_sum:
            acc = out_ref[0:m_per, :]
            for d in range(1, N_DEV):
                acc = acc + out_ref[d * m_per:(d + 1) * m_per, :]
            sum_ref[...] = acc

    vm = pl.BlockSpec(memory_space=pltpu.VMEM)
    out_shape = [jax.ShapeDtypeStruct((N_DEV * m_per, n), F32)]
    if with_sum:
        out_shape.append(jax.ShapeDtypeStruct((m_per, n), F32))
    return pl.pallas_call(
        body, name=name, out_shape=out_shape, in_specs=[vm], out_specs=[vm] * len(out_shape),
        scratch_shapes=[pltpu.SemaphoreType.DMA((7,)), pltpu.SemaphoreType.DMA((7,)), pltpu.SemaphoreType.DMA],
        compiler_params=pltpu.CompilerParams(vmem_limit_bytes=VMEM_LIMIT),
    )(blk)


def _weights_gather_call(wi_full, wo_full):
    hi = D // 2
    ho = W_OUT_SHARD // 2

    def body(wi_in, wo_in, fi_ref, fo_ref, send_sems, recv_sems):
        del wi_in, wo_in
        x, y, c = _coords()
        sibling = (x, y, 1 - c)
        chips = [(1 - x, y), (x, 1 - y), (1 - x, 1 - y)]

        def half(which, px, py, pc):
            j = 2 * px + py
            if which == 0:
                return fi_ref.at[pl.ds(pc * hi, hi), pl.ds(j * W_IN_SHARD, W_IN_SHARD)]
            return fo_ref.at[pl.ds(j * W_OUT_SHARD + pc * ho, ho), :]

        def copy(k, which, block, to):
            return pltpu.make_async_remote_copy(
                src_ref=half(which, *block), dst_ref=half(which, *block), send_sem=send_sems.at[k],
                recv_sem=recv_sems.at[k], device_id=to, device_id_type=MESH)

        first = [copy(6 * w + j, w, (x, y, c), (*chip, c)) for w in range(2) for j, chip in enumerate(chips)]
        for cp in first:
            cp.start()
        passed = []
        for w in range(2):
            for j, chip in enumerate(chips):
                copy(6 * w + j, w, (*chip, c), (x, y, c)).wait_recv()
                cp = copy(6 * w + 3 + j, w, (*chip, c), sibling)
                cp.start()
                passed.append(cp)
        for w in range(2):
            for j, chip in enumerate(chips):
                copy(6 * w + 3 + j, w, (*chip, 1 - c), (x, y, c)).wait_recv()
        for cp in first + passed:
            cp.wait_send()

    anyspec = pl.BlockSpec(memory_space=pl.ANY)
    return pl.pallas_call(
        body, name="weights_gather",
        out_shape=[jax.ShapeDtypeStruct((D, D_IN), BF16), jax.ShapeDtypeStruct((D, D), BF16)],
        in_specs=[anyspec, anyspec], out_specs=[anyspec, anyspec], input_output_aliases={0: 0, 1: 1},
        scratch_shapes=[pltpu.SemaphoreType.DMA((12,)), pltpu.SemaphoreType.DMA((12,))],
    )(wi_full, wo_full)


HBM_SPEC = pl.BlockSpec(memory_space=pltpu.HBM)
SEM_SPEC = pl.BlockSpec(memory_space=pltpu.SEMAPHORE)
SIDE_EFFECT = pltpu.SideEffectType.DATAFLOW_SIDE_EFFECTING


def _peer(x, y, c, q, cb):
    return (1 - x if q & 2 else x, 1 - y if q & 1 else y, 1 - c if cb else c)


def _w_in_piece(slots):
    def piece(part_ref, k, to):
        return part_ref.at[pl.ds(to[2] * (D // 2), D // 2), pl.ds(slots[k] * W_IN_SHARD, W_IN_SHARD)]
    return piece


def _w_out_piece(part_ref, k, to):
    ho = W_OUT_SHARD // 2
    return part_ref.at[pl.ds((2 * to[0] + to[1]) * W_OUT_SHARD + to[2] * ho, ho), :]


def _exchange_start_call(part, rels, piece, slot_shape, name):
    n = len(rels)
    land = lax.empty((n,) + slot_shape, BF16)

    def body(part_ref, land_ref, send_sems, recv_sems, part_thru, land_thru, token):
        x, y, c = _coords()
        for k, (q, cb) in enumerate(rels):
            to = _peer(x, y, c, q, cb)
            pltpu.make_async_remote_copy(src_ref=piece(part_ref, k, to), dst_ref=land_ref.at[k], send_sem=send_sems.at[k],
                                         recv_sem=recv_sems.at[k], device_id=to, device_id_type=MESH).start()
        token[...] = jnp.zeros_like(token)

    return pl.pallas_call(
        body, name=name,
        out_shape=(pltpu.SemaphoreType.DMA((n,)), pltpu.SemaphoreType.DMA((n,)), pltpu.HBM(part.shape, part.dtype),
                   pltpu.HBM(land.shape, land.dtype), jax.ShapeDtypeStruct((8, LANE), F32)),
        in_specs=(HBM_SPEC, HBM_SPEC), out_specs=(SEM_SPEC, SEM_SPEC, HBM_SPEC, HBM_SPEC, pl.BlockSpec(memory_space=pltpu.VMEM)),
        input_output_aliases={0: 2, 1: 3},
        compiler_params=pltpu.CompilerParams(has_side_effects=SIDE_EFFECT),
    )(pltpu.with_memory_space_constraint(part, pltpu.HBM), pltpu.with_memory_space_constraint(land, pltpu.HBM))


def _exchange_wait_call(started, rels, piece, after, name):
    send_sems, recv_sems, part_thru, land_thru, _ = started

    def body(part_ref, land_ref, send_sems, recv_sems, after_ref, part_out, land_out):
        x, y, c = _coords()
        for k, (q, cb) in enumerate(rels):
            to = _peer(x, y, c, q, cb)
            cp = pltpu.make_async_remote_copy(src_ref=piece(part_ref, k, to), dst_ref=land_ref.at[k], send_sem=send_sems.at[k],
                                              recv_sem=recv_sems.at[k], device_id=to, device_id_type=MESH)
            cp.wait_send()
            cp.wait_recv()

    return pl.pallas_call(
        body, name=name,
        out_shape=(pltpu.HBM(part_thru.shape, part_thru.dtype), pltpu.HBM(land_thru.shape, land_thru.dtype)),
        in_specs=(HBM_SPEC, HBM_SPEC, SEM_SPEC, SEM_SPEC, pl.BlockSpec(memory_space=pl.ANY)), out_specs=(HBM_SPEC, HBM_SPEC),
        input_output_aliases={0: 0, 1: 1},
        compiler_params=pltpu.CompilerParams(has_side_effects=SIDE_EFFECT),
    )(part_thru, land_thru, send_sems, recv_sems, after)


def _pair_exchange_call(gi, go):
    hi = D // 2
    ho = W_OUT_SHARD // 2

    def body(gi_in, go_in, fi_ref, fo_ref, send_sems, recv_sems):
        del gi_in, go_in
        x, y, c = _coords()
        sibling = (x, y, 1 - c)
        mine = (fi_ref.at[pl.ds(c * hi, hi), :], fo_ref.at[pl.ds(c * ho, ho), :])
        theirs = (fi_ref.at[pl.ds((1 - c) * hi, hi), :], fo_ref.at[pl.ds((1 - c) * ho, ho), :])
        sends = [pltpu.make_async_remote_copy(src_ref=ref, dst_ref=ref, send_sem=send_sems.at[k], recv_sem=recv_sems.at[k],
                                              device_id=sibling, device_id_type=MESH) for k, ref in enumerate(mine)]
        for cp in sends:
            cp.start()
        for k, ref in enumerate(theirs):
            pltpu.make_async_remote_copy(src_ref=ref, dst_ref=ref, send_sem=send_sems.at[k], recv_sem=recv_sems.at[k],
                                         device_id=sibling, device_id_type=MESH).wait_recv()
        for cp in sends:
            cp.wait_send()

    anyspec = pl.BlockSpec(memory_space=pl.ANY)
    return pl.pallas_call(
        body, name="pair_exchange",
        out_shape=[jax.ShapeDtypeStruct((D, W_IN_SHARD), F32), jax.ShapeDtypeStruct((W_OUT_SHARD, D), F32)],
        in_specs=[anyspec, anyspec], out_specs=[anyspec, anyspec], input_output_aliases={0: 0, 1: 1},
        scratch_shapes=[pltpu.SemaphoreType.DMA((2,)), pltpu.SemaphoreType.DMA((2,))],
    )(gi, go)


def _rope_tables(s):
    inv_freq = 10000.0 ** (-jnp.arange(0, HEAD, 2, dtype=F32) / HEAD)
    ang = jnp.arange(s, dtype=F32)[:, None] * inv_freq[None, :]
    return jnp.tile(jnp.cos(ang), (1, LANE // (HEAD // 2))), jnp.tile(jnp.sin(ang), (1, LANE // (HEAD // 2)))


def _pad_cols(a, n):
    return jnp.pad(a, ((0, 0), (0, n - a.shape[1])))


def kernel(x, c, w_ada, b_ada, norm_g, w_in, ln_v_g, ln_v_b, w_spatial, b_spatial, sinks, w_out, w_ada_final, b_ada_final, final_norm_g, loss_target, m_w_ada, m_b_ada, m_norm_g, m_w_in, m_ln_v_g, m_ln_v_b, m_w_spatial, m_b_spatial, m_sinks, m_w_out, m_w_ada_final, m_b_ada_final, m_final_norm_g, v_w_ada, v_b_ada, v_norm_g, v_w_in, v_ln_v_g, v_ln_v_b, v_w_spatial, v_b_spatial, v_sinks, v_w_out, v_w_ada_final, v_b_ada_final, v_final_norm_g):
    s = x.shape[1]
    ax, ay, ac = _coords()
    chip = 2 * ax + ay
    me = 4 * ax + 2 * ay + ac
    n_ada = w_ada.shape[2]
    n_adaf = w_ada_final.shape[1]

    x2d = x.reshape(s, D)
    tgt = loss_target.reshape(s, D)
    w_ada2, w_in2, w_out2 = w_ada[0], w_in[0], w_out[0]
    b_ada_f2 = b_ada_final.reshape(1, 2 * D)
    gf = final_norm_g.reshape(1, D)

    c_all = _allgather_sum_call(jnp.pad(c, ((0, 7), (0, 0))), "gather_c", False)[0][::8]
    mod_p, c_act = _rowmat_call(c_all, w_ada2, lax.dynamic_slice(b_ada, (0, chip * n_ada), (1, n_ada)), "mod")
    modf_p, _ = _rowmat_call(c_all, w_ada_final, lax.dynamic_slice(b_ada_f2, (0, chip * n_adaf), (1, n_adaf)), "mod_final")
    mods = _allgather_sum_call(jnp.concatenate([mod_p, modf_p], axis=1), "gather_mod", False)[0]
    my_rows = [lax.dynamic_slice(mods, (16 * j + me, 0), (1, n_ada + n_adaf)) for j in range(N_CHIP)]
    mod = jnp.concatenate([r[:, :n_ada] for r in my_rows], axis=1)
    mod_f = jnp.concatenate([r[:, n_ada:] for r in my_rows], axis=1)
    shift, scale, gate = mod[:, :D], mod[:, D:2 * D], mod[:, 2 * D:]
    shift_f, scale_f = mod_f[:, :D], mod_f[:, D:]

    pos = jnp.stack([chip, ac]).astype(jnp.int32)
    w_in_bf, w_out_bf = _weights_gather_call(_cast_into_call(pos, w_in2, (D, D_IN), "cast_w_in"),
                                             _cast_into_call(pos, w_out2, (D, D), "cast_w_out"))

    cos, sin = _rope_tables(s)
    b_sp_t = b_spatial[0].T
    sinks1 = sinks.reshape(N_Q)
    proj, h = _proj_call(x2d, shift, scale, norm_g, w_in_bf)
    y = _mix_fwd_call(proj, cos, sin, ln_v_g, ln_v_b, w_spatial[0], b_sp_t, sinks1)
    dx2, do, st_tail = _tail_call(y, w_out_bf, x2d, tgt, gate, shift_f, scale_f, gf)

    rel_o = [(0, 1), (1, 0), (1, 1), (2, 0), (2, 1), (3, 0), (3, 1)]
    rel_a = [(1, 0), (1, 1), (2, 0), (2, 1)]
    rel_b = [(3, 0), (3, 1), (0, 1)]
    piece_a, piece_b = _w_in_piece([0, 0, 1, 1]), _w_in_piece([0, 0, 1])
    half_in, half_out = (D // 2, W_IN_SHARD), (W_OUT_SHARD // 2, D)

    g_w_out_p = _tn_call(y, do, "grad_w_out")
    st_o = _exchange_start_call(g_w_out_p, rel_o, _w_out_piece, half_out, "send_w_out")
    dy = _dy_call(do, w_out_bf)
    dproj, st_ln, d_wsp, d_bsp_t, d_sink = _mix_bwd_call(
        proj, dy, cos, sin, ln_v_g + st_o[4][0:1, 0:1], ln_v_b, w_spatial[0], jnp.swapaxes(w_spatial[0], 1, 2), b_sp_t, sinks1)
    g_w_in_a = _tn_shards_call(pos, h, dproj, (1, 2), "grad_w_in_a")
    st_a = _exchange_start_call(g_w_in_a, rel_a, piece_a, half_in, "send_w_in_a")
    g_w_in_b = _tn_shards_call(pos, h, dproj, (3, 0), "grad_w_in_b")
    st_b = _exchange_start_call(g_w_in_b, rel_b, piece_b, half_in, "send_w_in_b")
    grad_x, st_dh = _dh_call(dproj, w_in_bf, x2d, dx2, scale + (st_a[4][0:1, 0:1] + st_b[4][0:1, 0:1]), norm_g)

    g_w_out_p, recv_o = _exchange_wait_call(st_o, rel_o, _w_out_piece, st_dh, "wait_w_out")
    _, recv_a = _exchange_wait_call(st_a, rel_a, piece_a, st_dh, "wait_w_in_a")
    g_w_in_b, recv_b = _exchange_wait_call(st_b, rel_b, piece_b, st_dh, "wait_w_in_b")
    g_w_in, g_w_out = _pair_exchange_call(
        _sum_pieces_call(pos, g_w_in_b, lambda i, p, nrb: (p[1] * nrb + i, 1), [recv_a, recv_b], (D, W_IN_SHARD), "sum_w_in"),
        _sum_pieces_call(pos, g_w_out_p, lambda i, p, nrb: ((2 * p[0] + p[1]) * nrb + i, 0), [recv_o], (W_OUT_SHARD, D), "sum_w_out"))

    pack = jnp.concatenate([
        d_wsp.reshape(64, D), st_tail, st_dh, _pad_cols(st_ln, D),
        _pad_cols(d_bsp_t[:, :GROUPS].T, D), _pad_cols(d_sink, D)], axis=0)
    rows = pack.shape[0]
    packs, tot = _allgather_sum_call(pack, "gather_small", True)
    packs = packs.reshape(N_DEV, rows, D)
    dmod_all = jnp.concatenate([packs[:, 72, :], packs[:, 73, :], packs[:, 67, :]], axis=1)
    dmodf_all = jnp.concatenate([packs[:, 64, :], packs[:, 65, :]], axis=1)
    loss = tot[69, 0]
    grads = {
        "b_ada": jnp.concatenate([tot[72:73], tot[73:74], tot[67:68]], axis=1),
        "norm_g": tot[74:75],
        "ln_v_g": tot[80:81, :D_A],
        "ln_v_b": tot[81:82, :D_A],
        "w_spatial": tot[0:64].reshape(GROUPS * BLK, BLK),
        "b_spatial": tot[88:96, :BLK],
        "sinks": tot[96:97, :N_Q],
        "b_ada_final": jnp.concatenate([tot[64:65], tot[65:66]], axis=1),
        "final_norm_g": tot[66:67],
        "w_in": g_w_in,
        "w_out": g_w_out,
    }

    weights = dict(w_ada=w_ada, b_ada=b_ada, norm_g=norm_g, w_in=w_in, ln_v_g=ln_v_g, ln_v_b=ln_v_b, w_spatial=w_spatial,
                   b_spatial=b_spatial, sinks=sinks, w_out=w_out, w_ada_final=w_ada_final, b_ada_final=b_ada_final,
                   final_norm_g=final_norm_g)
    m_in = dict(w_ada=m_w_ada, b_ada=m_b_ada, norm_g=m_norm_g, w_in=m_w_in, ln_v_g=m_ln_v_g, ln_v_b=m_ln_v_b,
                w_spatial=m_w_spatial, b_spatial=m_b_spatial, sinks=m_sinks, w_out=m_w_out, w_ada_final=m_w_ada_final,
                b_ada_final=m_b_ada_final, final_norm_g=m_final_norm_g)
    v_in = dict(w_ada=v_w_ada, b_ada=v_b_ada, norm_g=v_norm_g, w_in=v_w_in, ln_v_g=v_ln_v_g, ln_v_b=v_ln_v_b,
                w_spatial=v_w_spatial, b_spatial=v_b_spatial, sinks=v_sinks, w_out=v_w_out, w_ada_final=v_w_ada_final,
                b_ada_final=v_b_ada_final, final_norm_g=v_final_norm_g)
    c_act_t = c_act.T
    outer = {"w_ada": lax.dynamic_slice(dmod_all, (0, chip * n_ada), (N_DEV, n_ada)),
             "w_ada_final": lax.dynamic_slice(dmodf_all, (0, chip * n_adaf), (N_DEV, n_adaf))}
    out_g, out_d, out_m, out_v = [], [], [], []
    for name, w in weights.items():
        shape = w.shape
        if name in outer:
            shape2 = (D, outer[name].shape[1])
            g, dl, mn, vn = _adam_outer_call(w.reshape(shape2), c_act_t, outer[name], m_in[name].reshape(shape2),
                                             v_in[name].reshape(shape2), "adam_" + name)
        else:
            g = grads[name]
            shape2 = g.shape
            dl, mn, vn = _adam_call(w.reshape(shape2), g, m_in[name].reshape(shape2), v_in[name].reshape(shape2), "adam_" + name)
        out_g.append(g.reshape(shape))
        out_d.append(dl.reshape(shape))
        out_m.append(mn.reshape(shape))
        out_v.append(vn.reshape(shape))
    return (loss, grad_x.reshape(x.shape), *out_g, *out_d, *out_m, *out_v)
```

```python
import jax
import jax.numpy as jnp
from jax import lax
from jax.experimental import pallas as pl
from jax.experimental.pallas import tpu as pltpu

F32 = jnp.float32
BF16 = jnp.bfloat16
MESH = pl.DeviceIdType.MESH

D = 2048
D_A = 1024
D_B = 1024
D_KV = 256
HEAD = 64
N_Q = 16
N_KV = 4
Q_PER_KV = N_Q // N_KV
BLK = 128
GROUPS = 8
D_IN = 5632
OFF_Q, OFF_K, OFF_V, OFF_ZB = 3072, 4096, 4352, 4608
N_CHIP = 4
N_DEV = 8
W_IN_SHARD = D_IN // N_CHIP
W_OUT_SHARD = D // N_CHIP
EPS = 1e-5
SCALE = HEAD ** -0.5
NEG = -1e30
LANE = 128
VMEM_LIMIT = 56 * 1024 * 1024

ADAM_LR, ADAM_B1, ADAM_B2, ADAM_EPS, ADAM_WD, ADAM_STEP = 0.001, 0.9, 0.999, 1e-08, 0.01, 10
ADAM_C1 = 1.0 - ADAM_B1 ** ADAM_STEP
ADAM_C2 = 1.0 - ADAM_B2 ** ADAM_STEP

NT = (((1,), (1,)), ((), ()))
TN = (((0,), (0,)), ((), ()))


def _params(*sem):
    return pltpu.CompilerParams(dimension_semantics=sem, vmem_limit_bytes=VMEM_LIMIT)


def _silu_parts(z):
    sig = 1.0 / (1.0 + jnp.exp(-z))
    return z * sig, sig


def _rot_half(v, first_half):
    return jnp.where(first_half, -pltpu.roll(v, 96, 1), pltpu.roll(v, 32, 1))


def _lane_masks():
    lane = lax.broadcasted_iota(jnp.int32, (BLK, LANE), 1)
    return (lane % HEAD) < (HEAD // 2), lane < HEAD


def _band_valid(first_block_bound, rows=BLK):
    rr = lax.broadcasted_iota(jnp.int32, (rows, 2 * BLK), 0) & (BLK - 1)
    jj = lax.broadcasted_iota(jnp.int32, (rows, 2 * BLK), 1)
    return (jj > rr) & (jj <= rr + BLK) & (jj >= first_block_bound)


def _dup_kv(slab, lo):
    rolled = pltpu.roll(slab, HEAD, 1)
    return jnp.where(lo, slab, rolled).astype(BF16), jnp.where(lo, rolled, slab).astype(BF16)


def _stack_heads(ref, sb, slab, lo, dtype):
    kh, base = sb // 2, 2 * (sb % 2) * BLK
    zero = jnp.zeros_like(slab)
    ref[kh, base:base + BLK, :] = jnp.where(lo, slab, zero).astype(dtype)
    ref[kh, base + BLK:base + 2 * BLK, :] = jnp.where(lo, zero, slab).astype(dtype)


def _unstack_heads(ref, sb, lo):
    kh, base = sb // 2, 2 * (sb % 2) * BLK
    return jnp.where(lo, ref[kh, base:base + BLK, :], ref[kh, base + BLK:base + 2 * BLK, :])


def _sink_column(sinks_ref, kh):
    row = lax.broadcasted_iota(jnp.int32, (Q_PER_KV * BLK, 1), 0)
    col = jnp.full(row.shape, sinks_ref[Q_PER_KV * kh + Q_PER_KV - 1], F32)
    for n in range(Q_PER_KV - 2, -1, -1):
        col = jnp.where(row < (n + 1) * BLK, sinks_ref[Q_PER_KV * kh + n], col)
    return col


def _tril():
    t = lax.broadcasted_iota(jnp.int32, (BLK, BLK), 0)
    s = lax.broadcasted_iota(jnp.int32, (BLK, BLK), 1)
    return s <= t


def _layer_norm_fwd(va, lg, lb):
    mu = jnp.mean(va, axis=-1, keepdims=True)
    xc = va - mu
    rstd = lax.rsqrt(jnp.mean(xc * xc, axis=-1, keepdims=True) + EPS)
    vhat = xc * rstd
    return vhat, rstd, vhat * lg + lb


def _softmax_sink(qm, kexp, valid, sink):
    s = lax.dot_general(qm, kexp, NT, preferred_element_type=F32) * SCALE
    s = jnp.where(valid, s, NEG)
    m = jnp.maximum(jnp.max(s, axis=-1, keepdims=True), sink)
    p = jnp.exp(s - m)
    esink = jnp.exp(sink - m)
    den = jnp.sum(p, axis=-1, keepdims=True) + esink
    return p / den, esink / den


def _rowmat_call(c_all, w, b, name):
    n = w.shape[1]
    tn = 512

    def body(c_ref, w_ref, b_ref, o_ref, ca_ref):
        ca, _ = _silu_parts(c_ref[...])
        ca_ref[...] = ca
        o_ref[...] = jnp.dot(ca.astype(BF16), w_ref[...].astype(BF16), preferred_element_type=F32) + b_ref[...]

    return pl.pallas_call(
        body, name=name, grid=(n // tn,),
        in_specs=[pl.BlockSpec((N_DEV, D), lambda j: (0, 0)), pl.BlockSpec((D, tn), lambda j: (0, j)),
                  pl.BlockSpec((1, tn), lambda j: (0, j))],
        out_specs=[pl.BlockSpec((N_DEV, tn), lambda j: (0, j)), pl.BlockSpec((N_DEV, D), lambda j: (0, 0))],
        out_shape=[jax.ShapeDtypeStruct((N_DEV, n), F32), jax.ShapeDtypeStruct((N_DEV, D), F32)],
        compiler_params=_params("arbitrary"),
    )(c_all, w, b)


def _cast_into_call(pos, w, full_shape, name):
    r, n = w.shape
    tr = min(r, 512)
    by_cols = full_shape[0] == r
    nrb = r // tr

    def body(pos_ref, w_ref, o_ref):
        o_ref[...] = w_ref[...].astype(BF16)

    out_map = (lambda i, pos: (i, pos[0])) if by_cols else (lambda i, pos: (pos[0] * nrb + i, 0))
    return pl.pallas_call(
        body, name=name,
        grid_spec=pltpu.PrefetchScalarGridSpec(
            num_scalar_prefetch=1, grid=(nrb,),
            in_specs=[pl.BlockSpec((tr, n), lambda i, pos: (i, 0))], out_specs=pl.BlockSpec((tr, n), out_map)),
        out_shape=jax.ShapeDtypeStruct(full_shape, BF16), compiler_params=_params("parallel"),
    )(pos, w)


def _proj_call(x, shift, scale, norm_g, w_bf):
    s = x.shape[0]
    tm = min(s, 1024)
    tn = 512

    def body(x_ref, sh_ref, sc_ref, g_ref, w_ref, proj_ref, h_ref):
        @pl.when(pl.program_id(1) == 0)
        def _():
            xv = x_ref[...]
            r = lax.rsqrt(jnp.mean(xv * xv, axis=-1, keepdims=True) + EPS)
            h_ref[...] = ((xv * r * g_ref[...]) * (1.0 + sc_ref[...]) + sh_ref[...]).astype(BF16)

        proj_ref[...] = jnp.dot(h_ref[...], w_ref[...], preferred_element_type=F32)

    vec = pl.BlockSpec((1, D), lambda i, j: (0, 0))
    return pl.pallas_call(
        body, name="proj", grid=(s // tm, D_IN // tn),
        in_specs=[pl.BlockSpec((tm, D), lambda i, j: (i, 0)), vec, vec, vec, pl.BlockSpec((D, tn), lambda i, j: (0, j))],
        out_specs=[pl.BlockSpec((tm, tn), lambda i, j: (i, j)), pl.BlockSpec((tm, D), lambda i, j: (i, 0))],
        out_shape=[jax.ShapeDtypeStruct((s, D_IN), F32), jax.ShapeDtypeStruct((s, D), BF16)],
        compiler_params=_params("parallel", "arbitrary"),
    )(x, shift, scale, norm_g, w_bf)


def _proj_gather_call(pos, x, shift, scale, norm_g, wi_full, wo_full):
    s = x.shape[0]
    tm = min(s, 512)
    nrow = s // tm
    hi = D // 2
    ho = W_OUT_SHARD // 2

    def body(pos_ref, x_ref, sh_ref, sc_ref, g_ref, wi_in, wo_in, proj_ref, h_ref, fi_ref, fo_ref,
             h_all, wbuf, send_sems, recv_sems, load_sem):
        del wi_in, wo_in
        p = pl.program_id(0)
        i = pl.program_id(1)
        x_, y_, c_ = _coords()
        me, sibling = (x_, y_, c_), (x_, y_, 1 - c_)

        def shard_of(q):
            px, py, _ = _peer(x_, y_, c_, q, 0)
            return 2 * px + py

        def half(which, q, pc):
            if which == 0:
                return fi_ref.at[pl.ds(pc * hi, hi), pl.ds(shard_of(q) * W_IN_SHARD, W_IN_SHARD)]
            return fo_ref.at[pl.ds(shard_of(q) * W_OUT_SHARD + pc * ho, ho), :]

        def copy(k, which, q, pc, to):
            return pltpu.make_async_remote_copy(src_ref=half(which, q, pc), dst_ref=half(which, q, pc),
                                                send_sem=send_sems.at[k], recv_sem=recv_sems.at[k],
                                                device_id=to, device_id_type=MESH)

        def sent(which, q):
            return copy(6 * which + q - 1, which, 0, c_, _peer(x_, y_, c_, q, 0))

        def landed(which, q):
            return copy(6 * which + q - 1, which, q, c_, me)

        def passed(which, q):
            return copy(6 * which + 3 + q - 1, which, q, c_, sibling)

        def from_sibling(which, q):
            return copy(6 * which + 3 + q - 1, which, q, 1 - c_, me)

        def load_shard(q):
            cp = pltpu.make_async_copy(fi_ref.at[:, pl.ds(shard_of(q) * W_IN_SHARD, W_IN_SHARD)], wbuf, load_sem)
            cp.start()
            cp.wait()

        @pl.when((p == 0) & (i == 0))
        def _():
            for which in range(2):
                for q in range(1, N_CHIP):
                    sent(which, q).start()
            load_shard(0)

        for q in range(1, N_CHIP):
            @pl.when((p == q) & (i == 0))
            def _(q=q):
                landed(0, q).wait_recv()
                passed(0, q).start()
                from_sibling(0, q).wait_recv()
                load_shard(q)

        rows = pl.ds(pl.multiple_of(i * tm, tm), tm)

        @pl.when(p == 0)
        def _():
            xv = x_ref[...]
            r = lax.rsqrt(jnp.mean(xv * xv, axis=-1, keepdims=True) + EPS)
            hv = ((xv * r * g_ref[...]) * (1.0 + sc_ref[...]) + sh_ref[...]).astype(BF16)
            h_ref[...] = hv
            h_all[rows, :] = hv

        proj_ref[...] = jnp.dot(h_all[rows, :], wbuf[...], preferred_element_type=F32)

        @pl.when((p == N_CHIP - 1) & (i == nrow - 1))
        def _():
            for q in range(1, N_CHIP):
                landed(1, q).wait_recv()
                passed(1, q).start()
            for q in range(1, N_CHIP):
                from_sibling(1, q).wait_recv()
            for which in range(2):
                for q in range(1, N_CHIP):
                    sent(which, q).wait_send()
                    passed(which, q).wait_send()

    vec = pl.BlockSpec((1, D), lambda p, i, pos: (0, 0))
    first_phase_rows = lambda p, i, pos: (jnp.where(p == 0, i, nrow - 1), 0)
    anyspec = pl.BlockSpec(memory_space=pl.ANY)
    return pl.pallas_call(
        body, name="proj_gather",
        grid_spec=pltpu.PrefetchScalarGridSpec(
            num_scalar_prefetch=1, grid=(N_CHIP, nrow),
            in_specs=[pl.BlockSpec((tm, D), first_phase_rows), vec, vec, vec, anyspec, anyspec],
            out_specs=[pl.BlockSpec((tm, W_IN_SHARD), lambda p, i, pos: (i, jnp.bitwise_xor(pos[0], p))),
                       pl.BlockSpec((tm, D), first_phase_rows), anyspec, anyspec],
            scratch_shapes=[pltpu.VMEM((s, D), BF16), pltpu.VMEM((D, W_IN_SHARD), BF16),
                            pltpu.SemaphoreType.DMA((12,)), pltpu.SemaphoreType.DMA((12,)), pltpu.SemaphoreType.DMA]),
        out_shape=[jax.ShapeDtypeStruct((s, D_IN), F32), jax.ShapeDtypeStruct((s, D), BF16),
                   jax.ShapeDtypeStruct((D, D_IN), BF16), jax.ShapeDtypeStruct((D, D), BF16)],
        input_output_aliases={5: 2, 6: 3},
        compiler_params=_params("arbitrary", "arbitrary"),
    )(pos, x, shift, scale, norm_g, wi_full, wo_full)


def _proj_specs(rev_nb=None):
    if rev_nb is None:
        row = lambda i: i
    else:
        row = lambda i: rev_nb - 1 - i
    wide = lambda col: pl.BlockSpec((BLK, D_A), lambda i: (row(i), col))
    kv = lambda col: pl.BlockSpec((BLK, D_KV), lambda i: (row(i), col))
    half = lambda col: pl.BlockSpec((BLK, 512), lambda i: (row(i), col))
    return [wide(0), wide(1), wide(2), wide(3), kv(OFF_K // D_KV), kv(OFF_V // D_KV), half(OFF_ZB // 512), half(OFF_ZB // 512 + 1)]


def _mix_fwd_call(proj, cos, sin, ln_g, ln_b, w_sp, b_sp_t, sinks):
    s = proj.shape[0]
    nb = s // BLK

    def body(ua_ref, va_ref, za_ref, q_ref, k_ref, v_ref, zb0_ref, zb1_ref, cos_ref, sin_ref, lg_ref, lb_ref,
             w_ref, bt_ref, sinks_ref, y_ref, kdup_ref, vdup_ref, qm_ref, ost_ref):
        i = pl.program_id(0)
        first_half, lo = _lane_masks()
        cos_t = cos_ref[...]
        sin_t = sin_ref[...]

        _, _, vln = _layer_norm_fwd(va_ref[...], lg_ref[...], lb_ref[...])
        tril = _tril()
        for g in range(GROUPS):
            cols = slice(g * BLK, (g + 1) * BLK)
            wg = jnp.where(tril, w_ref[g], 0.0).astype(BF16)
            sg = jnp.dot(wg, vln[:, cols].astype(BF16), preferred_element_type=F32) + bt_ref[:, g:g + 1]
            gate, _ = _silu_parts(za_ref[:, cols])
            y_ref[:, cols] = (ua_ref[:, cols] * sg * gate).astype(BF16)

        @pl.when(i == 0)
        def _():
            kdup_ref[:, 0:BLK, :] = jnp.zeros((N_KV, BLK, LANE), BF16)
            vdup_ref[:, 0:BLK, :] = jnp.zeros((N_KV, BLK, LANE), BF16)

        @pl.when(i > 0)
        def _():
            kdup_ref[:, 0:BLK, :] = kdup_ref[:, BLK:2 * BLK, :]
            vdup_ref[:, 0:BLK, :] = vdup_ref[:, BLK:2 * BLK, :]

        for ks in range(2):
            cols = slice(ks * LANE, (ks + 1) * LANE)
            kslab = k_ref[:, cols]
            kr = kslab * cos_t + _rot_half(kslab, first_half) * sin_t
            for n, (kd, vd) in enumerate(zip(_dup_kv(kr, lo), _dup_kv(v_ref[:, cols], lo))):
                kdup_ref[2 * ks + n, BLK:2 * BLK, :] = kd
                vdup_ref[2 * ks + n, BLK:2 * BLK, :] = vd
        for sb in range(8):
            qslab = q_ref[:, sb * LANE:(sb + 1) * LANE]
            _stack_heads(qm_ref, sb, qslab * cos_t + _rot_half(qslab, first_half) * sin_t, lo, BF16)

        valid = _band_valid(jnp.where(i > 0, 0, BLK), Q_PER_KV * BLK)

        def kv_head(kh, carry):
            probs, _ = _softmax_sink(qm_ref[kh], kdup_ref[kh], valid, _sink_column(sinks_ref, kh))
            ost_ref[kh] = jnp.dot(probs.astype(BF16), vdup_ref[kh], preferred_element_type=F32)
            return carry

        lax.fori_loop(0, N_KV, kv_head, 0)
        for sb in range(8):
            cols = slice(sb * LANE, (sb + 1) * LANE)
            zb = zb0_ref[:, cols] if sb < 4 else zb1_ref[:, (sb - 4) * LANE:(sb - 3) * LANE]
            gate, _ = _silu_parts(zb)
            y_ref[:, D_A + sb * LANE:D_A + (sb + 1) * LANE] = (_unstack_heads(ost_ref, sb, lo) * gate).astype(BF16)

    tab = pl.BlockSpec((BLK, LANE), lambda i: (i, 0))
    return pl.pallas_call(
        body, name="mix_fwd", grid=(nb,),
        in_specs=_proj_specs() + [
            tab, tab, pl.BlockSpec((1, D_A), lambda i: (0, 0)), pl.BlockSpec((1, D_A), lambda i: (0, 0)),
            pl.BlockSpec((GROUPS, BLK, BLK), lambda i: (0, 0, 0)), pl.BlockSpec((BLK, GROUPS), lambda i: (0, 0)),
            pl.BlockSpec(memory_space=pltpu.SMEM)],
        out_specs=pl.BlockSpec((BLK, 2 * D_A), lambda i: (i, 0)),
        out_shape=jax.ShapeDtypeStruct((s, 2 * D_A), BF16),
        scratch_shapes=[pltpu.VMEM((N_KV, 2 * BLK, LANE), BF16), pltpu.VMEM((N_KV, 2 * BLK, LANE), BF16),
                        pltpu.VMEM((N_KV, Q_PER_KV * BLK, LANE), BF16), pltpu.VMEM((N_KV, Q_PER_KV * BLK, LANE), F32)],
        compiler_params=_params("arbitrary"),
    )(proj, proj, proj, proj, proj, proj, proj, proj, cos, sin, ln_g, ln_b, w_sp, b_sp_t, sinks)


def _tail_call(y, w_out_bf, x, target, gate, shift_f, scale_f, gf):
    s = x.shape[0]
    tm = min(s, 256)
    nsteps = s // tm

    def body(y_ref, w_ref, x_ref, t_ref, gate_ref, shf_ref, scf_ref, gf_ref, dx2_ref, do_ref, st_ref):
        i = pl.program_id(0)

        @pl.when(i == 0)
        def _():
            st_ref[...] = jnp.zeros((8, D), F32)

        o = jnp.dot(y_ref[...], w_ref[...], preferred_element_type=F32)
        gate_v = gate_ref[...]
        x2 = x_ref[...] + gate_v * o
        r2 = lax.rsqrt(jnp.mean(x2 * x2, axis=-1, keepdims=True) + EPS)
        xn2 = x2 * r2
        hn2 = xn2 * gf_ref[...]
        one_sc = 1.0 + scf_ref[...]
        err = hn2 * one_sc + shf_ref[...] - t_ref[...]
        dout = err * (1.0 / D)
        dhn2 = dout * one_sc
        dxn2 = dhn2 * gf_ref[...]
        dx2 = r2 * (dxn2 - xn2 * jnp.mean(dxn2 * xn2, axis=-1, keepdims=True))
        dx2_ref[...] = dx2
        do_ref[...] = (dx2 * gate_v).astype(BF16)
        st_ref[0:1, :] += jnp.sum(dout, axis=0, keepdims=True)
        st_ref[1:2, :] += jnp.sum(dout * hn2, axis=0, keepdims=True)
        st_ref[2:3, :] += jnp.sum(dhn2 * xn2, axis=0, keepdims=True)
        st_ref[3:4, :] += jnp.sum(dx2 * o, axis=0, keepdims=True)
        st_ref[4:5, :] += jnp.sum(err * err, axis=0, keepdims=True)

        @pl.when(i == nsteps - 1)
        def _():
            st_ref[5:6, :] = jnp.full((1, D), 0.5 / D, F32) * jnp.sum(st_ref[4:5, :])

    vec = pl.BlockSpec((1, D), lambda i: (0, 0))
    rows = lambda: pl.BlockSpec((tm, D), lambda i: (i, 0))
    return pl.pallas_call(
        body, name="tail", grid=(nsteps,),
        in_specs=[rows(), pl.BlockSpec((D, D), lambda i: (0, 0)), rows(), rows(), vec, vec, vec, vec],
        out_specs=[rows(), rows(), pl.BlockSpec((8, D), lambda i: (0, 0))],
        out_shape=[jax.ShapeDtypeStruct((s, D), F32), jax.ShapeDtypeStruct((s, D), BF16), jax.ShapeDtypeStruct((8, D), F32)],
        compiler_params=_params("arbitrary"),
    )(y, w_out_bf, x, target, gate, shift_f, scale_f, gf)


def _dy_call(do, w_out_bf):
    s = do.shape[0]
    tm = min(s, 512)

    def body(do_ref, w_ref, dy_ref):
        dy_ref[...] = lax.dot_general(do_ref[...], w_ref[...], NT, preferred_element_type=F32)

    return pl.pallas_call(
        body, name="dy", grid=(s // tm,),
        in_specs=[pl.BlockSpec((tm, D), lambda i: (i, 0)), pl.BlockSpec((D, D), lambda i: (0, 0))],
        out_specs=pl.BlockSpec((tm, D), lambda i: (i, 0)),
        out_shape=jax.ShapeDtypeStruct((s, D), F32), compiler_params=_params("parallel"),
    )(do, w_out_bf)


def _tn_call(a, b, name):
    s, m = a.shape
    n = b.shape[1]
    tn = 512
    ts = min(s, 1024)
    nk = s // ts

    def body(a_ref, b_ref, o_ref, acc_ref):
        k = pl.program_id(1)

        @pl.when(k == 0)
        def _():
            acc_ref[...] = jnp.zeros((m, tn), F32)

        acc_ref[...] += lax.dot_general(a_ref[...], b_ref[...], TN, preferred_element_type=F32)

        @pl.when(k == nk - 1)
        def _():
            o_ref[...] = acc_ref[...].astype(BF16)

    return pl.pallas_call(
        body, name=name, grid=(n // tn, nk),
        in_specs=[pl.BlockSpec((ts, m), lambda j, k: (k, 0)), pl.BlockSpec((ts, tn), lambda j, k: (k, j))],
        out_specs=pl.BlockSpec((m, tn), lambda j, k: (0, j)),
        out_shape=jax.ShapeDtypeStruct((m, n), BF16),
        scratch_shapes=[pltpu.VMEM((m, tn), F32)],
        compiler_params=_params("parallel", "arbitrary"),
    )(a, b)


def _tn_shards_call(pos, a, b, qs, name):
    s, m = a.shape
    ts = min(s, 512)
    nk = s // ts

    def body(pos_ref, a_ref, b_ref, o_ref, acc_ref):
        k = pl.program_id(1)

        @pl.when(k == 0)
        def _():
            acc_ref[...] = jnp.zeros((m, W_IN_SHARD), F32)

        acc_ref[...] += lax.dot_general(a_ref[...], b_ref[...], TN, preferred_element_type=F32)

        @pl.when(k == nk - 1)
        def _():
            o_ref[...] = acc_ref[...].astype(BF16)

    def shard(j, pos):
        q = qs[0]
        for n in range(1, len(qs)):
            q = jnp.where(j == n, qs[n], q)
        return jnp.bitwise_xor(pos[0], q)

    return pl.pallas_call(
        body, name=name,
        grid_spec=pltpu.PrefetchScalarGridSpec(
            num_scalar_prefetch=1, grid=(len(qs), nk),
            in_specs=[pl.BlockSpec((ts, m), lambda j, k, pos: (k, 0)),
                      pl.BlockSpec((ts, W_IN_SHARD), lambda j, k, pos: (k, shard(j, pos)))],
            out_specs=pl.BlockSpec((m, W_IN_SHARD), lambda j, k, pos: (0, j)),
            scratch_shapes=[pltpu.VMEM((m, W_IN_SHARD), F32)]),
        out_shape=jax.ShapeDtypeStruct((m, len(qs) * W_IN_SHARD), BF16),
        compiler_params=_params("parallel", "arbitrary"),
    )(pos, a, b)


def _mix_bwd_call(proj, dy, cos, sin, ln_g, ln_b, w_sp, w_sp_t, b_sp_t, sinks):
    s = proj.shape[0]
    nb = s // BLK
    rev = lambda i: nb - 1 - i
    prev = lambda i: jnp.maximum(nb - 2 - i, 0)

    def body(ua_ref, va_ref, za_ref, q_ref, k_ref, v_ref, zb0_ref, zb1_ref, kp_ref, vp_ref, dy_ref,
             cos_ref, sin_ref, cosp_ref, sinp_ref, lg_ref, lb_ref, w_ref, wt_ref, bt_ref, sinks_ref,
             dp_ref, lnst_ref, dw_ref, dbt_ref, dsink_ref,
             kdup_ref, vdup_ref, dvln_ref, qm_ref, dom_ref, ost_ref, dqst_ref, dkdup_ref, dvdup_ref, kcar_ref, vcar_ref):
        i = pl.program_id(0)
        first_half, lo = _lane_masks()
        lane8 = lax.broadcasted_iota(jnp.int32, (8, LANE), 1)
        cos_t = cos_ref[...]
        sin_t = sin_ref[...]

        @pl.when(i == 0)
        def _():
            lnst_ref[...] = jnp.zeros((8, D_A), F32)
            dw_ref[...] = jnp.zeros((GROUPS, BLK, BLK), F32)
            dbt_ref[...] = jnp.zeros((BLK, LANE), F32)
            dsink_ref[...] = jnp.zeros((8, LANE), F32)
            kcar_ref[...] = jnp.zeros((BLK, D_KV), F32)
            vcar_ref[...] = jnp.zeros((BLK, D_KV), F32)

        vhat, rstd, vln = _layer_norm_fwd(va_ref[...], lg_ref[...], lb_ref[...])
        tril = _tril()
        triu = jnp.logical_not(tril) | (lax.broadcasted_iota(jnp.int32, (BLK, BLK), 0) == lax.broadcasted_iota(jnp.int32, (BLK, BLK), 1))
        lane_b = lax.broadcasted_iota(jnp.int32, (BLK, LANE), 1)
        db_acc = jnp.zeros((BLK, LANE), F32)
        for g in range(GROUPS):
            cols = slice(g * BLK, (g + 1) * BLK)
            vln_g = vln[:, cols].astype(BF16)
            wg = jnp.where(tril, w_ref[g], 0.0).astype(BF16)
            sg = jnp.dot(wg, vln_g, preferred_element_type=F32) + bt_ref[:, g:g + 1]
            za = za_ref[:, cols]
            gate, sig = _silu_parts(za)
            ua = ua_ref[:, cols]
            dya_g = dy_ref[:, cols]
            dya = dya_g * gate
            dp_ref[:, cols] = (dya * sg).astype(BF16)
            dp_ref[:, 2 * D_A + g * BLK:2 * D_A + (g + 1) * BLK] = (
                dya_g * (ua * sg) * (sig * (1.0 + za * (1.0 - sig)))).astype(BF16)
            ds = dya * ua
            ds_b = ds.astype(BF16)
            wtg = jnp.where(triu, wt_ref[g], 0.0).astype(BF16)
            dvln_ref[:, cols] = jnp.dot(wtg, ds_b, preferred_element_type=F32)
            dw_ref[g] += jnp.where(tril, lax.dot_general(ds_b, vln_g, NT, preferred_element_type=F32), 0.0)
            db_acc = db_acc + jnp.where(lane_b == g, jnp.sum(ds, axis=-1, keepdims=True), 0.0)
        dbt_ref[...] += db_acc
        dvln = dvln_ref[...]
        lnst_ref[0:1, :] += jnp.sum(dvln * vhat, axis=0, keepdims=True)
        lnst_ref[1:2, :] += jnp.sum(dvln, axis=0, keepdims=True)
        dvhat = dvln * lg_ref[...]
        m1 = jnp.mean(dvhat, axis=-1, keepdims=True)
        m2 = jnp.mean(dvhat * vhat, axis=-1, keepdims=True)
        dp_ref[:, D_A:2 * D_A] = (rstd * (dvhat - m1 - vhat * m2)).astype(BF16)

        cosp = cosp_ref[...]
        sinp = sinp_ref[...]
        for ks in range(2):
            cols = slice(ks * LANE, (ks + 1) * LANE)
            kslab = k_ref[:, cols]
            kr = kslab * cos_t + _rot_half(kslab, first_half) * sin_t
            kpslab = kp_ref[:, cols]
            kpr = kpslab * cosp + _rot_half(kpslab, first_half) * sinp
            for n, (kc, vc, kp, vp) in enumerate(zip(_dup_kv(kr, lo), _dup_kv(v_ref[:, cols], lo),
                                                     _dup_kv(kpr, lo), _dup_kv(vp_ref[:, cols], lo))):
                kdup_ref[2 * ks + n, BLK:2 * BLK, :] = kc
                vdup_ref[2 * ks + n, BLK:2 * BLK, :] = vc
                kdup_ref[2 * ks + n, 0:BLK, :] = kp
                vdup_ref[2 * ks + n, 0:BLK, :] = vp
        for sb in range(8):
            cols = slice(sb * LANE, (sb + 1) * LANE)
            qslab = q_ref[:, cols]
            _stack_heads(qm_ref, sb, qslab * cos_t + _rot_half(qslab, first_half) * sin_t, lo, BF16)
            zb = zb0_ref[:, cols] if sb < 4 else zb1_ref[:, (sb - 4) * LANE:(sb - 3) * LANE]
            gate, _ = _silu_parts(zb)
            _stack_heads(dom_ref, sb, dy_ref[:, D_A + sb * LANE:D_A + (sb + 1) * LANE] * gate, lo, F32)

        valid = _band_valid(jnp.where(i < nb - 1, 0, BLK), Q_PER_KV * BLK)

        def kv_head(kh, dsink_acc):
            qm = qm_ref[kh]
            kd = kdup_ref[kh]
            vd = vdup_ref[kh]
            probs, psink = _softmax_sink(qm, kd, valid, _sink_column(sinks_ref, kh))
            probs_b = probs.astype(BF16)
            o = jnp.dot(probs_b, vd, preferred_element_type=F32)
            ost_ref[kh] = o
            dom = dom_ref[kh]
            dom_b = dom.astype(BF16)
            delta = jnp.sum(dom * o, axis=-1, keepdims=True)
            dpr = lax.dot_general(dom_b, vd, NT, preferred_element_type=F32)
            dss = (probs * (dpr - delta) * SCALE).astype(BF16)
            sd = psink * delta
            for n in range(Q_PER_KV):
                dsink_acc = dsink_acc + jnp.where(lane8 == Q_PER_KV * kh + n, -jnp.sum(sd[n * BLK:(n + 1) * BLK]), 0.0)
            dqst_ref[kh] = jnp.dot(dss, kd, preferred_element_type=F32)
            dkdup_ref[kh] = lax.dot_general(dss, qm, TN, preferred_element_type=F32)
            dvdup_ref[kh] = lax.dot_general(probs_b, dom_b, TN, preferred_element_type=F32)
            return dsink_acc

        dsink_acc = lax.fori_loop(0, N_KV, kv_head, jnp.zeros((8, LANE), F32))
        row0 = lax.broadcasted_iota(jnp.int32, (8, LANE), 0) == 0
        dsink_ref[...] += jnp.where(row0, dsink_acc, 0.0)

        for sb in range(8):
            cols = slice(sb * LANE, (sb + 1) * LANE)
            zb = zb0_ref[:, cols] if sb < 4 else zb1_ref[:, (sb - 4) * LANE:(sb - 3) * LANE]
            _, sig = _silu_parts(zb)
            dyb = dy_ref[:, D_A + sb * LANE:D_A + (sb + 1) * LANE]
            dp_ref[:, OFF_ZB + sb * LANE:OFF_ZB + (sb + 1) * LANE] = (
                dyb * _unstack_heads(ost_ref, sb, lo) * (sig * (1.0 + zb * (1.0 - sig)))).astype(BF16)
            dq_r = _unstack_heads(dqst_ref, sb, lo)
            dp_ref[:, OFF_Q + sb * LANE:OFF_Q + (sb + 1) * LANE] = (
                dq_r * cos_t - _rot_half(dq_r * sin_t, first_half)).astype(BF16)

        lo2 = lax.broadcasted_iota(jnp.int32, (2 * BLK, LANE), 1) < HEAD
        for ks in range(2):
            cols = slice(ks * LANE, (ks + 1) * LANE)
            ka = dkdup_ref[2 * ks]
            kb = dkdup_ref[2 * ks + 1]
            dk_band = jnp.where(lo2, ka + pltpu.roll(ka, HEAD, 1), kb + pltpu.roll(kb, HEAD, 1))
            va_ = dvdup_ref[2 * ks]
            vb_ = dvdup_ref[2 * ks + 1]
            dv_band = jnp.where(lo2, va_ + pltpu.roll(va_, HEAD, 1), vb_ + pltpu.roll(vb_, HEAD, 1))
            dkr = dk_band[BLK:2 * BLK, :] + kcar_ref[:, cols]
            dp_ref[:, OFF_K + ks * LANE:OFF_K + (ks + 1) * LANE] = (
                dkr * cos_t - _rot_half(dkr * sin_t, first_half)).astype(BF16)
            dp_ref[:, OFF_V + ks * LANE:OFF_V + (ks + 1) * LANE] = (
                dv_band[BLK:2 * BLK, :] + vcar_ref[:, cols]).astype(BF16)
            kcar_ref[:, cols] = dk_band[0:BLK, :]
            vcar_ref[:, cols] = dv_band[0:BLK, :]

    tab = pl.BlockSpec((BLK, LANE), lambda i: (rev(i), 0))
    tabp = pl.BlockSpec((BLK, LANE), lambda i: (prev(i), 0))
    kvp = lambda col: pl.BlockSpec((BLK, D_KV), lambda i: (prev(i), col))
    vec = pl.BlockSpec((1, D_A), lambda i: (0, 0))
    w3 = pl.BlockSpec((GROUPS, BLK, BLK), lambda i: (0, 0, 0))
    return pl.pallas_call(
        body, name="mix_bwd", grid=(nb,),
        in_specs=_proj_specs(nb) + [
            kvp(OFF_K // D_KV), kvp(OFF_V // D_KV), pl.BlockSpec((BLK, 2 * D_A), lambda i: (rev(i), 0)),
            tab, tab, tabp, tabp, vec, vec, w3, w3, pl.BlockSpec((BLK, GROUPS), lambda i: (0, 0)),
            pl.BlockSpec(memory_space=pltpu.SMEM)],
        out_specs=[pl.BlockSpec((BLK, D_IN), lambda i: (rev(i), 0)), pl.BlockSpec((8, D_A), lambda i: (0, 0)), w3,
                   pl.BlockSpec((BLK, LANE), lambda i: (0, 0)), pl.BlockSpec((8, LANE), lambda i: (0, 0))],
        out_shape=[jax.ShapeDtypeStruct((s, D_IN), BF16), jax.ShapeDtypeStruct((8, D_A), F32),
                   jax.ShapeDtypeStruct((GROUPS, BLK, BLK), F32), jax.ShapeDtypeStruct((BLK, LANE), F32),
                   jax.ShapeDtypeStruct((8, LANE), F32)],
        scratch_shapes=[pltpu.VMEM((N_KV, 2 * BLK, LANE), BF16), pltpu.VMEM((N_KV, 2 * BLK, LANE), BF16),
                        pltpu.VMEM((BLK, D_A), F32), pltpu.VMEM((N_KV, Q_PER_KV * BLK, LANE), BF16),
                        pltpu.VMEM((N_KV, Q_PER_KV * BLK, LANE), F32), pltpu.VMEM((N_KV, Q_PER_KV * BLK, LANE), F32),
                        pltpu.VMEM((N_KV, Q_PER_KV * BLK, LANE), F32), pltpu.VMEM((N_KV, 2 * BLK, LANE), F32),
                        pltpu.VMEM((N_KV, 2 * BLK, LANE), F32), pltpu.VMEM((BLK, D_KV), F32), pltpu.VMEM((BLK, D_KV), F32)],
        compiler_params=_params("arbitrary"),
    )(proj, proj, proj, proj, proj, proj, proj, proj, proj, proj, dy, cos, sin, cos, sin, ln_g, ln_b, w_sp, w_sp_t,
      b_sp_t, sinks)


def _dh_call(dproj, w_bf, x, dx2, scale, norm_g):
    s = x.shape[0]
    tm = min(s, 512)
    tk = W_IN_SHARD
    nk = D_IN // tk

    def body(dp_ref, w_ref, x_ref, dx2_ref, sc_ref, g_ref, gx_ref, st_ref, acc_ref):
        i = pl.program_id(0)
        k = pl.program_id(1)

        @pl.when((i == 0) & (k == 0))
        def _():
            st_ref[...] = jnp.zeros((8, D), F32)

        @pl.when(k == 0)
        def _():
            acc_ref[...] = jnp.zeros((tm, D), F32)

        acc_ref[...] += lax.dot_general(dp_ref[...], w_ref[...], NT, preferred_element_type=F32)

        @pl.when(k == nk - 1)
        def _():
            g = g_ref[...]
            one_sc = 1.0 + sc_ref[...]

            def chunk(n, carry):
                rows = pl.ds(pl.multiple_of(n * BLK, BLK), BLK)
                dh = acc_ref[rows, :]
                xv = x_ref[rows, :]
                r = lax.rsqrt(jnp.mean(xv * xv, axis=-1, keepdims=True) + EPS)
                xn = xv * r
                dhn = dh * one_sc
                dxn = dhn * g
                gx_ref[rows, :] = dx2_ref[rows, :] + r * (dxn - xn * jnp.mean(dxn * xn, axis=-1, keepdims=True))
                st_ref[0:1, :] += jnp.sum(dh, axis=0, keepdims=True)
                st_ref[1:2, :] += jnp.sum(dh * (xn * g), axis=0, keepdims=True)
                st_ref[2:3, :] += jnp.sum(dhn * xn, axis=0, keepdims=True)
                return carry

            lax.fori_loop(0, tm // BLK, chunk, 0)

    vec = pl.BlockSpec((1, D), lambda i, k: (0, 0))
    rows = lambda: pl.BlockSpec((tm, D), lambda i, k: (i, 0))
    return pl.pallas_call(
        body, name="dh", grid=(s // tm, nk),
        in_specs=[pl.BlockSpec((tm, tk), lambda i, k: (i, k)), pl.BlockSpec((D, tk), lambda i, k: (0, k)), rows(), rows(), vec, vec],
        out_specs=[rows(), pl.BlockSpec((8, D), lambda i, k: (0, 0))],
        out_shape=[jax.ShapeDtypeStruct((s, D), F32), jax.ShapeDtypeStruct((8, D), F32)],
        scratch_shapes=[pltpu.VMEM((tm, D), F32)],
        compiler_params=_params("arbitrary", "arbitrary"),
    )(dproj, w_bf, x, dx2, scale, norm_g)


def _adam_math(w, g, m, v):
    m_new = ADAM_B1 * m + (1.0 - ADAM_B1) * g
    v_new = ADAM_B2 * v + (1.0 - ADAM_B2) * (g * g)
    m_hat = m_new / ADAM_C1
    v_hat = v_new / ADAM_C2
    delta = -ADAM_LR * (m_hat / (jnp.sqrt(v_hat) + ADAM_EPS) + ADAM_WD * w)
    return delta, m_new, v_new


def _adam_call(w, g, m, v, name):
    r, n = w.shape
    tr = r if r * n * 4 <= (1 << 20) else max(8, (1 << 20) // (n * 4) // 8 * 8)
    while r % tr:
        tr -= 8

    def body(w_ref, g_ref, m_ref, v_ref, d_ref, mo_ref, vo_ref):
        d_ref[...], mo_ref[...], vo_ref[...] = _adam_math(w_ref[...], g_ref[...], m_ref[...], v_ref[...])

    spec = lambda: pl.BlockSpec((tr, n), lambda i: (i, 0))
    return pl.pallas_call(
        body, name=name, grid=(r // tr,), in_specs=[spec() for _ in range(4)], out_specs=[spec() for _ in range(3)],
        out_shape=[jax.ShapeDtypeStruct((r, n), F32)] * 3, compiler_params=_params("parallel"),
    )(w, g, m, v)


def _adam_outer_call(w, ct, dm, m, v, name):
    r, n = w.shape
    tr = 128

    def body(w_ref, ct_ref, dm_ref, m_ref, v_ref, g_ref, d_ref, mo_ref, vo_ref):
        g = ct_ref[:, 0:1] * dm_ref[0:1, :]
        for b in range(1, N_DEV):
            g = g + ct_ref[:, b:b + 1] * dm_ref[b:b + 1, :]
        g_ref[...] = g
        d_ref[...], mo_ref[...], vo_ref[...] = _adam_math(w_ref[...], g, m_ref[...], v_ref[...])

    spec = lambda: pl.BlockSpec((tr, n), lambda i: (i, 0))
    return pl.pallas_call(
        body, name=name, grid=(r // tr,),
        in_specs=[spec(), pl.BlockSpec((tr, N_DEV), lambda i: (i, 0)), pl.BlockSpec((N_DEV, n), lambda i: (0, 0)), spec(), spec()],
        out_specs=[spec() for _ in range(4)],
        out_shape=[jax.ShapeDtypeStruct((r, n), F32)] * 4, compiler_params=_params("parallel"),
    )(w, ct, dm, m, v)


def _sum_pieces_call(pos, part, part_block, recvs, shard_shape, name):
    r, n = recvs[0].shape[1:]
    tr = min(r, 256)
    nrb = r // tr

    def body(pos_ref, p_ref, *refs):
        acc = p_ref[...].astype(F32)
        for r_ref in refs[:-1]:
            for d in range(r_ref.shape[0]):
                acc = acc + r_ref[d].astype(F32)
        refs[-1][...] = acc

    return pl.pallas_call(
        body, name=name,
        grid_spec=pltpu.PrefetchScalarGridSpec(
            num_scalar_prefetch=1, grid=(nrb,),
            in_specs=[pl.BlockSpec((tr, n), lambda i, pos: part_block(i, pos, nrb))] + [
                pl.BlockSpec((rv.shape[0], tr, n), lambda i, pos: (0, i, 0)) for rv in recvs],
            out_specs=pl.BlockSpec((tr, n), lambda i, pos: (pos[1] * nrb + i, 0))),
        out_shape=jax.ShapeDtypeStruct(shard_shape, F32), compiler_params=_params("parallel"),
    )(pos, part, *recvs)


def _coords():
    return lax.axis_index("x"), lax.axis_index("y"), lax.axis_index("c")


def _allgather_sum_call(blk, name, with_sum):
    m_per, n = blk.shape

    def body(x_ref, out_ref, *rest):
        if with_sum:
            sum_ref, send_sems, recv_sems, local_sem = rest
        else:
            send_sems, recv_sems, local_sem = rest
        x, y, c = _coords()
        me, sibling = (x, y, c), (x, y, 1 - c)
        chips = [(1 - x, y), (x, 1 - y), (1 - x, 1 - y)]

        def rows(px, py, pc):
            return out_ref.at[pl.ds((4 * px + 2 * py + pc) * m_per, m_per), :]

        def copy(k, block, to, src=None):
            return pltpu.make_async_remote_copy(
                src_ref=rows(*block) if src is None else src, dst_ref=rows(*block),
                send_sem=send_sems.at[k], recv_sem=recv_sems.at[k], device_id=to, device_id_type=MESH)

        mine = pltpu.make_async_copy(x_ref, rows(*me), local_sem)
        mine.start()
        first = [copy(0, me, sibling, src=x_ref)]
        first += [copy(1 + j, me, (*chip, c), src=x_ref) for j, chip in enumerate(chips)]
        for cp in first:
            cp.start()
        passed = [copy(4 + j, (*chip, c), sibling) for j, chip in enumerate(chips)]
        for j, chip in enumerate(chips):
            copy(1 + j, (*chip, c), me).wait_recv()
            passed[j].start()
        copy(0, sibling, me).wait_recv()
        for j, chip in enumerate(chips):
            copy(4 + j, (*chip, 1 - c), me).wait_recv()
        for cp in first + passed:
            cp.wait_send()
        mine.wait()
        if with_sum:
            acc = out_ref[0:m_per, :]
            for d in range(1, N_DEV):
                acc = acc + out_ref[d * m_per:(d + 1) * m_per, :]
            sum_ref[...] = acc

    vm = pl.BlockSpec(memory_space=pltpu.VMEM)
    out_shape = [jax.ShapeDtypeStruct((N_DEV * m_per, n), F32)]
    if with_sum:
        out_shape.append(jax.ShapeDtypeStruct((m_per, n), F32))
    return pl.pallas_call(
        body, name=name, out_shape=out_shape, in_specs=[vm], out_specs=[vm] * len(out_shape),
        scratch_shapes=[pltpu.SemaphoreType.DMA((7,)), pltpu.SemaphoreType.DMA((7,)), pltpu.SemaphoreType.DMA],
        compiler_params=pltpu.CompilerParams(vmem_limit_bytes=VMEM_LIMIT),
    )(blk)


def _weights_gather_call(wi_full, wo_full):
    hi = D // 2
    ho = W_OUT_SHARD // 2

    def body(wi_in, wo_in, fi_ref, fo_ref, send_sems, recv_sems):
        del wi_in, wo_in
        x, y, c = _coords()
        sibling = (x, y, 1 - c)
        chips = [(1 - x, y), (x, 1 - y), (1 - x, 1 - y)]

        def half(which, px, py, pc):
            j = 2 * px + py
            if which == 0:
                return fi_ref.at[pl.ds(pc * hi, hi), pl.ds(j * W_IN_SHARD, W_IN_SHARD)]
            return fo_ref.at[pl.ds(j * W_OUT_SHARD + pc * ho, ho), :]

        def copy(k, which, block, to):
            return pltpu.make_async_remote_copy(
                src_ref=half(which, *block), dst_ref=half(which, *block), send_sem=send_sems.at[k],
                recv_sem=recv_sems.at[k], device_id=to, device_id_type=MESH)

        first = [copy(6 * w + j, w, (x, y, c), (*chip, c)) for w in range(2) for j, chip in enumerate(chips)]
        for cp in first:
            cp.start()
        passed = []
        for w in range(2):
            for j, chip in enumerate(chips):
                copy(6 * w + j, w, (*chip, c), (x, y, c)).wait_recv()
                cp = copy(6 * w + 3 + j, w, (*chip, c), sibling)
                cp.start()
                passed.append(cp)
        for w in range(2):
            for j, chip in enumerate(chips):
                copy(6 * w + 3 + j, w, (*chip, 1 - c), (x, y, c)).wait_recv()
        for cp in first + passed:
            cp.wait_send()

    anyspec = pl.BlockSpec(memory_space=pl.ANY)
    return pl.pallas_call(
        body, name="weights_gather",
        out_shape=[jax.ShapeDtypeStruct((D, D_IN), BF16), jax.ShapeDtypeStruct((D, D), BF16)],
        in_specs=[anyspec, anyspec], out_specs=[anyspec, anyspec], input_output_aliases={0: 0, 1: 1},
        scratch_shapes=[pltpu.SemaphoreType.DMA((12,)), pltpu.SemaphoreType.DMA((12,))],
    )(wi_full, wo_full)


HBM_SPEC = pl.BlockSpec(memory_space=pltpu.HBM)
SEM_SPEC = pl.BlockSpec(memory_space=pltpu.SEMAPHORE)
SIDE_EFFECT = pltpu.SideEffectType.DATAFLOW_SIDE_EFFECTING


def _peer(x, y, c, q, cb):
    return (1 - x if q & 2 else x, 1 - y if q & 1 else y, 1 - c if cb else c)


def _w_in_piece(slots):
    def piece(part_ref, k, to):
        return part_ref.at[pl.ds(to[2] * (D // 2), D // 2), pl.ds(slots[k] * W_IN_SHARD, W_IN_SHARD)]
    return piece


def _w_out_piece(part_ref, k, to):
    ho = W_OUT_SHARD // 2
    return part_ref.at[pl.ds((2 * to[0] + to[1]) * W_OUT_SHARD + to[2] * ho, ho), :]


def _exchange_start_call(part, rels, piece, slot_shape, name):
    n = len(rels)
    land = lax.empty((n,) + slot_shape, BF16)

    def body(part_ref, land_ref, send_sems, recv_sems, part_thru, land_thru, token):
        x, y, c = _coords()
        for k, (q, cb) in enumerate(rels):
            to = _peer(x, y, c, q, cb)
            pltpu.make_async_remote_copy(src_ref=piece(part_ref, k, to), dst_ref=land_ref.at[k], send_sem=send_sems.at[k],
                                         recv_sem=recv_sems.at[k], device_id=to, device_id_type=MESH).start()
        token[...] = jnp.zeros_like(token)

    return pl.pallas_call(
        body, name=name,
        out_shape=(pltpu.SemaphoreType.DMA((n,)), pltpu.SemaphoreType.DMA((n,)), pltpu.HBM(part.shape, part.dtype),
                   pltpu.HBM(land.shape, land.dtype), jax.ShapeDtypeStruct((8, LANE), F32)),
        in_specs=(HBM_SPEC, HBM_SPEC), out_specs=(SEM_SPEC, SEM_SPEC, HBM_SPEC, HBM_SPEC, pl.BlockSpec(memory_space=pltpu.VMEM)),
        input_output_aliases={0: 2, 1: 3},
        compiler_params=pltpu.CompilerParams(has_side_effects=SIDE_EFFECT),
    )(pltpu.with_memory_space_constraint(part, pltpu.HBM), pltpu.with_memory_space_constraint(land, pltpu.HBM))


def _exchange_wait_call(started, rels, piece, after, name):
    send_sems, recv_sems, part_thru, land_thru, _ = started

    def body(part_ref, land_ref, send_sems, recv_sems, after_ref, part_out, land_out):
        x, y, c = _coords()
        for k, (q, cb) in enumerate(rels):
            to = _peer(x, y, c, q, cb)
            cp = pltpu.make_async_remote_copy(src_ref=piece(part_ref, k, to), dst_ref=land_ref.at[k], send_sem=send_sems.at[k],
                                              recv_sem=recv_sems.at[k], device_id=to, device_id_type=MESH)
            cp.wait_send()
            cp.wait_recv()

    return pl.pallas_call(
        body, name=name,
        out_shape=(pltpu.HBM(part_thru.shape, part_thru.dtype), pltpu.HBM(land_thru.shape, land_thru.dtype)),
        in_specs=(HBM_SPEC, HBM_SPEC, SEM_SPEC, SEM_SPEC, pl.BlockSpec(memory_space=pl.ANY)), out_specs=(HBM_SPEC, HBM_SPEC),
        input_output_aliases={0: 0, 1: 1},
        compiler_params=pltpu.CompilerParams(has_side_effects=SIDE_EFFECT),
    )(part_thru, land_thru, send_sems, recv_sems, after)


def _pair_exchange_call(gi, go):
    hi = D // 2
    ho = W_OUT_SHARD // 2

    def body(gi_in, go_in, fi_ref, fo_ref, send_sems, recv_sems):
        del gi_in, go_in
        x, y, c = _coords()
        sibling = (x, y, 1 - c)
        mine = (fi_ref.at[pl.ds(c * hi, hi), :], fo_ref.at[pl.ds(c * ho, ho), :])
        theirs = (fi_ref.at[pl.ds((1 - c) * hi, hi), :], fo_ref.at[pl.ds((1 - c) * ho, ho), :])
        sends = [pltpu.make_async_remote_copy(src_ref=ref, dst_ref=ref, send_sem=send_sems.at[k], recv_sem=recv_sems.at[k],
                                              device_id=sibling, device_id_type=MESH) for k, ref in enumerate(mine)]
        for cp in sends:
            cp.start()
        for k, ref in enumerate(theirs):
            pltpu.make_async_remote_copy(src_ref=ref, dst_ref=ref, send_sem=send_sems.at[k], recv_sem=recv_sems.at[k],
                                         device_id=sibling, device_id_type=MESH).wait_recv()
        for cp in sends:
            cp.wait_send()

    anyspec = pl.BlockSpec(memory_space=pl.ANY)
    return pl.pallas_call(
        body, name="pair_exchange",
        out_shape=[jax.ShapeDtypeStruct((D, W_IN_SHARD), F32), jax.ShapeDtypeStruct((W_OUT_SHARD, D), F32)],
        in_specs=[anyspec, anyspec], out_specs=[anyspec, anyspec], input_output_aliases={0: 0, 1: 1},
        scratch_shapes=[pltpu.SemaphoreType.DMA((2,)), pltpu.SemaphoreType.DMA((2,))],
    )(gi, go)


def _rope_tables(s):
    inv_freq = 10000.0 ** (-jnp.arange(0, HEAD, 2, dtype=F32) / HEAD)
    ang = jnp.arange(s, dtype=F32)[:, None] * inv_freq[None, :]
    return jnp.tile(jnp.cos(ang), (1, LANE // (HEAD // 2))), jnp.tile(jnp.sin(ang), (1, LANE // (HEAD // 2)))


def _pad_cols(a, n):
    return jnp.pad(a, ((0, 0), (0, n - a.shape[1])))


def kernel(x, c, w_ada, b_ada, norm_g, w_in, ln_v_g, ln_v_b, w_spatial, b_spatial, sinks, w_out, w_ada_final, b_ada_final, final_norm_g, loss_target, m_w_ada, m_b_ada, m_norm_g, m_w_in, m_ln_v_g, m_ln_v_b, m_w_spatial, m_b_spatial, m_sinks, m_w_out, m_w_ada_final, m_b_ada_final, m_final_norm_g, v_w_ada, v_b_ada, v_norm_g, v_w_in, v_ln_v_g, v_ln_v_b, v_w_spatial, v_b_spatial, v_sinks, v_w_out, v_w_ada_final, v_b_ada_final, v_final_norm_g):
    s = x.shape[1]
    ax, ay, ac = _coords()
    chip = 2 * ax + ay
    me = 4 * ax + 2 * ay + ac
    n_ada = w_ada.shape[2]
    n_adaf = w_ada_final.shape[1]

    x2d = x.reshape(s, D)
    tgt = loss_target.reshape(s, D)
    w_ada2, w_in2, w_out2 = w_ada[0], w_in[0], w_out[0]
    b_ada_f2 = b_ada_final.reshape(1, 2 * D)
    gf = final_norm_g.reshape(1, D)

    c_all = _allgather_sum_call(jnp.pad(c, ((0, 7), (0, 0))), "gather_c", False)[0][::8]
    mod_p, c_act = _rowmat_call(c_all, w_ada2, lax.dynamic_slice(b_ada, (0, chip * n_ada), (1, n_ada)), "mod")
    modf_p, _ = _rowmat_call(c_all, w_ada_final, lax.dynamic_slice(b_ada_f2, (0, chip * n_adaf), (1, n_adaf)), "mod_final")
    mods = _allgather_sum_call(jnp.concatenate([mod_p, modf_p], axis=1), "gather_mod", False)[0]
    my_rows = [lax.dynamic_slice(mods, (16 * j + me, 0), (1, n_ada + n_adaf)) for j in range(N_CHIP)]
    mod = jnp.concatenate([r[:, :n_ada] for r in my_rows], axis=1)
    mod_f = jnp.concatenate([r[:, n_ada:] for r in my_rows], axis=1)
    shift, scale, gate = mod[:, :D], mod[:, D:2 * D], mod[:, 2 * D:]
    shift_f, scale_f = mod_f[:, :D], mod_f[:, D:]

    pos = jnp.stack([chip, ac]).astype(jnp.int32)
    w_in_own = _cast_into_call(pos, w_in2, (D, D_IN), "cast_w_in")
    w_out_own = _cast_into_call(pos, w_out2, (D, D), "cast_w_out")

    cos, sin = _rope_tables(s)
    b_sp_t = b_spatial[0].T
    sinks1 = sinks.reshape(N_Q)
    proj, h, w_in_bf, w_out_bf = _proj_gather_call(pos, x2d, shift, scale, norm_g, w_in_own, w_out_own)
    y = _mix_fwd_call(proj, cos, sin, ln_v_g, ln_v_b, w_spatial[0], b_sp_t, sinks1)
    dx2, do, st_tail = _tail_call(y, w_out_bf, x2d, tgt, gate, shift_f, scale_f, gf)

    rel_o = [(0, 1), (1, 0), (1, 1), (2, 0), (2, 1), (3, 0), (3, 1)]
    rel_a = [(1, 0), (1, 1), (2, 0), (2, 1)]
    rel_b = [(3, 0), (3, 1), (0, 1)]
    piece_a, piece_b = _w_in_piece([0, 0, 1, 1]), _w_in_piece([0, 0, 1])
    half_in, half_out = (D // 2, W_IN_SHARD), (W_OUT_SHARD // 2, D)

    g_w_out_p = _tn_call(y, do, "grad_w_out")
    st_o = _exchange_start_call(g_w_out_p, rel_o, _w_out_piece, half_out, "send_w_out")
    dy = _dy_call(do, w_out_bf)
    dproj, st_ln, d_wsp, d_bsp_t, d_sink = _mix_bwd_call(
        proj, dy, cos, sin, ln_v_g + st_o[4][0:1, 0:1], ln_v_b, w_spatial[0], jnp.swapaxes(w_spatial[0], 1, 2), b_sp_t, sinks1)
    g_w_in_a = _tn_shards_call(pos, h, dproj, (1, 2), "grad_w_in_a")
    st_a = _exchange_start_call(g_w_in_a, rel_a, piece_a, half_in, "send_w_in_a")
    g_w_in_b = _tn_shards_call(pos, h, dproj, (3, 0), "grad_w_in_b")
    st_b = _exchange_start_call(g_w_in_b, rel_b, piece_b, half_in, "send_w_in_b")
    grad_x, st_dh = _dh_call(dproj, w_in_bf, x2d, dx2, scale + (st_a[4][0:1, 0:1] + st_b[4][0:1, 0:1]), norm_g)

    g_w_out_p, recv_o = _exchange_wait_call(st_o, rel_o, _w_out_piece, st_dh, "wait_w_out")
    _, recv_a = _exchange_wait_call(st_a, rel_a, piece_a, st_dh, "wait_w_in_a")
    g_w_in_b, recv_b = _exchange_wait_call(st_b, rel_b, piece_b, st_dh, "wait_w_in_b")
    g_w_in, g_w_out = _pair_exchange_call(
        _sum_pieces_call(pos, g_w_in_b, lambda i, p, nrb: (p[1] * nrb + i, 1), [recv_a, recv_b], (D, W_IN_SHARD), "sum_w_in"),
        _sum_pieces_call(pos, g_w_out_p, lambda i, p, nrb: ((2 * p[0] + p[1]) * nrb + i, 0), [recv_o], (W_OUT_SHARD, D), "sum_w_out"))

    pack = jnp.concatenate([
        d_wsp.reshape(64, D), st_tail, st_dh, _pad_cols(st_ln, D),
        _pad_cols(d_bsp_t[:, :GROUPS].T, D), _pad_cols(d_sink, D)], axis=0)
    rows = pack.shape[0]
    packs, tot = _allgather_sum_call(pack, "gather_small", True)
    packs = packs.reshape(N_DEV, rows, D)
    dmod_all = jnp.concatenate([packs[:, 72, :], packs[:, 73, :], packs[:, 67, :]], axis=1)
    dmodf_all = jnp.concatenate([packs[:, 64, :], packs[:, 65, :]], axis=1)
    loss = tot[69, 0]
    grads = {
        "b_ada": jnp.concatenate([tot[72:73], tot[73:74], tot[67:68]], axis=1),
        "norm_g": tot[74:75],
        "ln_v_g": tot[80:81, :D_A],
        "ln_v_b": tot[81:82, :D_A],
        "w_spatial": tot[0:64].reshape(GROUPS * BLK, BLK),
        "b_spatial": tot[88:96, :BLK],
        "sinks": tot[96:97, :N_Q],
        "b_ada_final": jnp.concatenate([tot[64:65], tot[65:66]], axis=1),
        "final_norm_g": tot[66:67],
        "w_in": g_w_in,
        "w_out": g_w_out,
    }

    weights = dict(w_ada=w_ada, b_ada=b_ada, norm_g=norm_g, w_in=w_in, ln_v_g=ln_v_g, ln_v_b=ln_v_b, w_spatial=w_spatial,
                   b_spatial=b_spatial, sinks=sinks, w_out=w_out, w_ada_final=w_ada_final, b_ada_final=b_ada_final,
                   final_norm_g=final_norm_g)
    m_in = dict(w_ada=m_w_ada, b_ada=m_b_ada, norm_g=m_norm_g, w_in=m_w_in, ln_v_g=m_ln_v_g, ln_v_b=m_ln_v_b,
                w_spatial=m_w_spatial, b_spatial=m_b_spatial, sinks=m_sinks, w_out=m_w_out, w_ada_final=m_w_ada_final,
                b_ada_final=m_b_ada_final, final_norm_g=m_final_norm_g)
    v_in = dict(w_ada=v_w_ada, b_ada=v_b_ada, norm_g=v_norm_g, w_in=v_w_in, ln_v_g=v_ln_v_g, ln_v_b=v_ln_v_b,
                w_spatial=v_w_spatial, b_spatial=v_b_spatial, sinks=v_sinks, w_out=v_w_out, w_ada_final=v_w_ada_final,
                b_ada_final=v_b_ada_final, final_norm_g=v_final_norm_g)
    c_act_t = c_act.T
    outer = {"w_ada": lax.dynamic_slice(dmod_all, (0, chip * n_ada), (N_DEV, n_ada)),
             "w_ada_final": lax.dynamic_slice(dmodf_all, (0, chip * n_adaf), (N_DEV, n_adaf))}
    out_g, out_d, out_m, out_v = [], [], [], []
    for name, w in weights.items():
        shape = w.shape
        if name in outer:
            shape2 = (D, outer[name].shape[1])
            g, dl, mn, vn = _adam_outer_call(w.reshape(shape2), c_act_t, outer[name], m_in[name].reshape(shape2),
                                             v_in[name].reshape(shape2), "adam_" + name)
        else:
            g = grads[name]
            shape2 = g.shape
            dl, mn, vn = _adam_call(w.reshape(shape2), g, m_in[name].reshape(shape2), v_in[name].reshape(shape2), "adam_" + name)
        out_g.append(g.reshape(shape))
        out_d.append(dl.reshape(shape))
        out_m.append(mn.reshape(shape))
        out_v.append(vn.reshape(shape))
    return (loss, grad_x.reshape(x.shape), *out_g, *out_d, *out_m, *out_v)
```

```python
import jax
import jax.numpy as jnp
from jax import lax
from jax.experimental import pallas as pl
from jax.experimental.pallas import tpu as pltpu

F32 = jnp.float32
BF16 = jnp.bfloat16
MESH = pl.DeviceIdType.MESH

D = 2048
D_A = 1024
D_B = 1024
D_KV = 256
HEAD = 64
N_Q = 16
N_KV = 4
Q_PER_KV = N_Q // N_KV
BLK = 128
GROUPS = 8
D_IN = 5632
OFF_Q, OFF_K, OFF_V, OFF_ZB = 3072, 4096, 4352, 4608
N_CHIP = 4
N_DEV = 8
W_IN_SHARD = D_IN // N_CHIP
W_OUT_SHARD = D // N_CHIP
EPS = 1e-5
SCALE = HEAD ** -0.5
NEG = -1e30
LANE = 128
VMEM_LIMIT = 56 * 1024 * 1024

ADAM_LR, ADAM_B1, ADAM_B2, ADAM_EPS, ADAM_WD, ADAM_STEP = 0.001, 0.9, 0.999, 1e-08, 0.01, 10
ADAM_C1 = 1.0 - ADAM_B1 ** ADAM_STEP
ADAM_C2 = 1.0 - ADAM_B2 ** ADAM_STEP

NT = (((1,), (1,)), ((), ()))
TN = (((0,), (0,)), ((), ()))


def _params(*sem):
    return pltpu.CompilerParams(dimension_semantics=sem, vmem_limit_bytes=VMEM_LIMIT)


def _silu_parts(z):
    sig = 1.0 / (1.0 + jnp.exp(-z))
    return z * sig, sig


def _rot_half(v, first_half):
    return jnp.where(first_half, -pltpu.roll(v, 96, 1), pltpu.roll(v, 32, 1))


def _lane_masks():
    lane = lax.broadcasted_iota(jnp.int32, (BLK, LANE), 1)
    return (lane % HEAD) < (HEAD // 2), lane < HEAD


def _band_valid(first_block_bound, rows=BLK):
    rr = lax.broadcasted_iota(jnp.int32, (rows, 2 * BLK), 0) & (BLK - 1)
    jj = lax.broadcasted_iota(jnp.int32, (rows, 2 * BLK), 1)
    return (jj > rr) & (jj <= rr + BLK) & (jj >= first_block_bound)


def _dup_kv(slab, lo):
    rolled = pltpu.roll(slab, HEAD, 1)
    return jnp.where(lo, slab, rolled).astype(BF16), jnp.where(lo, rolled, slab).astype(BF16)


def _stack_heads(ref, sb, slab, lo, dtype):
    kh, base = sb // 2, 2 * (sb % 2) * BLK
    zero = jnp.zeros_like(slab)
    ref[kh, base:base + BLK, :] = jnp.where(lo, slab, zero).astype(dtype)
    ref[kh, base + BLK:base + 2 * BLK, :] = jnp.where(lo, zero, slab).astype(dtype)


def _unstack_heads(ref, sb, lo):
    kh, base = sb // 2, 2 * (sb % 2) * BLK
    return jnp.where(lo, ref[kh, base:base + BLK, :], ref[kh, base + BLK:base + 2 * BLK, :])


def _sink_column(sinks_ref, kh):
    row = lax.broadcasted_iota(jnp.int32, (Q_PER_KV * BLK, 1), 0)
    col = jnp.full(row.shape, sinks_ref[Q_PER_KV * kh + Q_PER_KV - 1], F32)
    for n in range(Q_PER_KV - 2, -1, -1):
        col = jnp.where(row < (n + 1) * BLK, sinks_ref[Q_PER_KV * kh + n], col)
    return col


def _tril():
    t = lax.broadcasted_iota(jnp.int32, (BLK, BLK), 0)
    s = lax.broadcasted_iota(jnp.int32, (BLK, BLK), 1)
    return s <= t


def _layer_norm_fwd(va, lg, lb):
    mu = jnp.mean(va, axis=-1, keepdims=True)
    xc = va - mu
    rstd = lax.rsqrt(jnp.mean(xc * xc, axis=-1, keepdims=True) + EPS)
    vhat = xc * rstd
    return vhat, rstd, vhat * lg + lb


def _softmax_sink(qm, kexp, valid, sink):
    s = lax.dot_general(qm, kexp, NT, preferred_element_type=F32) * SCALE
    s = jnp.where(valid, s, NEG)
    m = jnp.maximum(jnp.max(s, axis=-1, keepdims=True), sink)
    p = jnp.exp(s - m)
    esink = jnp.exp(sink - m)
    den = jnp.sum(p, axis=-1, keepdims=True) + esink
    return p / den, esink / den


def _rowmat_call(c_all, w, b, name):
    n = w.shape[1]
    tn = 512

    def body(c_ref, w_ref, b_ref, o_ref, ca_ref):
        ca, _ = _silu_parts(c_ref[...])
        ca_ref[...] = ca
        o_ref[...] = jnp.dot(ca.astype(BF16), w_ref[...].astype(BF16), preferred_element_type=F32) + b_ref[...]

    return pl.pallas_call(
        body, name=name, grid=(n // tn,),
        in_specs=[pl.BlockSpec((N_DEV, D), lambda j: (0, 0)), pl.BlockSpec((D, tn), lambda j: (0, j)),
                  pl.BlockSpec((1, tn), lambda j: (0, j))],
        out_specs=[pl.BlockSpec((N_DEV, tn), lambda j: (0, j)), pl.BlockSpec((N_DEV, D), lambda j: (0, 0))],
        out_shape=[jax.ShapeDtypeStruct((N_DEV, n), F32), jax.ShapeDtypeStruct((N_DEV, D), F32)],
        compiler_params=_params("arbitrary"),
    )(c_all, w, b)


def _cast_into_call(pos, w, full_shape, name):
    r, n = w.shape
    tr = min(r, 512)
    by_cols = full_shape[0] == r
    nrb = r // tr

    def body(pos_ref, w_ref, o_ref):
        o_ref[...] = w_ref[...].astype(BF16)

    out_map = (lambda i, pos: (i, pos[0])) if by_cols else (lambda i, pos: (pos[0] * nrb + i, 0))
    return pl.pallas_call(
        body, name=name,
        grid_spec=pltpu.PrefetchScalarGridSpec(
            num_scalar_prefetch=1, grid=(nrb,),
            in_specs=[pl.BlockSpec((tr, n), lambda i, pos: (i, 0))], out_specs=pl.BlockSpec((tr, n), out_map)),
        out_shape=jax.ShapeDtypeStruct(full_shape, BF16), compiler_params=_params("parallel"),
    )(pos, w)


def _proj_call(x, shift, scale, norm_g, w_bf):
    s = x.shape[0]
    tm = min(s, 1024)
    tn = 512

    def body(x_ref, sh_ref, sc_ref, g_ref, w_ref, proj_ref, h_ref):
        @pl.when(pl.program_id(1) == 0)
        def _():
            xv = x_ref[...]
            r = lax.rsqrt(jnp.mean(xv * xv, axis=-1, keepdims=True) + EPS)
            h_ref[...] = ((xv * r * g_ref[...]) * (1.0 + sc_ref[...]) + sh_ref[...]).astype(BF16)

        proj_ref[...] = jnp.dot(h_ref[...], w_ref[...], preferred_element_type=F32)

    vec = pl.BlockSpec((1, D), lambda i, j: (0, 0))
    return pl.pallas_call(
        body, name="proj", grid=(s // tm, D_IN // tn),
        in_specs=[pl.BlockSpec((tm, D), lambda i, j: (i, 0)), vec, vec, vec, pl.BlockSpec((D, tn), lambda i, j: (0, j))],
        out_specs=[pl.BlockSpec((tm, tn), lambda i, j: (i, j)), pl.BlockSpec((tm, D), lambda i, j: (i, 0))],
        out_shape=[jax.ShapeDtypeStruct((s, D_IN), F32), jax.ShapeDtypeStruct((s, D), BF16)],
        compiler_params=_params("parallel", "arbitrary"),
    )(x, shift, scale, norm_g, w_bf)


def _proj_gather_call(pos, x, shift, scale, norm_g, wi_full, wo_full):
    s = x.shape[0]
    tm = min(s, 512)
    nrow = s // tm
    hi = D // 2
    ho = W_OUT_SHARD // 2

    def body(pos_ref, x_ref, sh_ref, sc_ref, g_ref, wi_in, wo_in, proj_ref, h_ref, fi_ref, fo_ref,
             h_all, wbuf, send_sems, recv_sems, load_sem):
        del wi_in, wo_in
        p = pl.program_id(0)
        i = pl.program_id(1)
        x_, y_, c_ = _coords()
        me, sibling = (x_, y_, c_), (x_, y_, 1 - c_)

        def shard_of(q):
            px, py, _ = _peer(x_, y_, c_, q, 0)
            return 2 * px + py

        def part(which, q, pc, sub=None):
            n = hi if which == 0 else ho
            base = pc * n
            if sub is not None:
                n //= 2
                base = base + sub * n
            if which == 0:
                return fi_ref.at[pl.ds(base, n), pl.ds(shard_of(q) * W_IN_SHARD, W_IN_SHARD)]
            return fo_ref.at[pl.ds(shard_of(q) * W_OUT_SHARD + base, n), :]

        def copy(k, which, q, pc, to, sub=None):
            ref = part(which, q, pc, sub)
            return pltpu.make_async_remote_copy(src_ref=ref, dst_ref=ref, send_sem=send_sems.at[k], recv_sem=recv_sems.at[k],
                                                device_id=to, device_id_type=MESH)

        def to_neighbour(which, q):
            return copy(8 * which + q - 1, which, 0, c_, _peer(x_, y_, c_, q, 0))

        def from_neighbour(which, q):
            return copy(8 * which + q - 1, which, q, c_, me)

        def relay(which, q):
            return copy(8 * which + 2 + q - 1, which, q, c_, _peer(x_, y_, c_, 3 - q, 0), q - 1)

        def relayed(which, sub):
            return copy(8 * which + 2 + sub, which, 3, c_, me, sub)

        def to_sibling(which, q):
            return copy(8 * which + 4 + q - 1, which, q, c_, sibling)

        def from_sibling(which, q):
            return copy(8 * which + 4 + q - 1, which, q, 1 - c_, me)

        def relayed_to_sibling(which, sub):
            return copy(8 * which + 6 + sub, which, 3, c_, sibling, sub)

        def relayed_from_sibling(which, sub):
            return copy(8 * which + 6 + sub, which, 3, 1 - c_, me, sub)

        def pass_on_neighbours(which):
            for q in (1, 2):
                from_neighbour(which, q).wait_recv()
                to_sibling(which, q).start()
                relay(which, q).start()

        def pass_on_relayed(which):
            for sub in range(2):
                relayed(which, sub).wait_recv()
                relayed_to_sibling(which, sub).start()

        def load_shard(q):
            cp = pltpu.make_async_copy(fi_ref.at[:, pl.ds(shard_of(q) * W_IN_SHARD, W_IN_SHARD)], wbuf, load_sem)
            cp.start()
            cp.wait()

        @pl.when((p == 0) & (i == 0))
        def _():
            for q in (1, 2):
                to_neighbour(0, q).start()
            load_shard(0)

        @pl.when((p == 1) & (i == 0))
        def _():
            pass_on_neighbours(0)
            for q in (1, 2):
                to_neighbour(1, q).start()
            from_sibling(0, 1).wait_recv()
            load_shard(1)

        @pl.when((p == 2) & (i == 0))
        def _():
            from_sibling(0, 2).wait_recv()
            load_shard(2)

        @pl.when((p == 3) & (i == 0))
        def _():
            pass_on_relayed(0)
            pass_on_neighbours(1)
            for sub in range(2):
                relayed_from_sibling(0, sub).wait_recv()
            load_shard(3)

        rows = pl.ds(pl.multiple_of(i * tm, tm), tm)

        @pl.when(p == 0)
        def _():
            xv = x_ref[...]
            r = lax.rsqrt(jnp.mean(xv * xv, axis=-1, keepdims=True) + EPS)
            hv = ((xv * r * g_ref[...]) * (1.0 + sc_ref[...]) + sh_ref[...]).astype(BF16)
            h_ref[...] = hv
            h_all[rows, :] = hv

        proj_ref[...] = jnp.dot(h_all[rows, :], wbuf[...], preferred_element_type=F32)

        @pl.when((p == N_CHIP - 1) & (i == nrow - 1))
        def _():
            pass_on_relayed(1)
            for q in (1, 2):
                from_sibling(1, q).wait_recv()
            for sub in range(2):
                relayed_from_sibling(1, sub).wait_recv()
            for which in range(2):
                for q in (1, 2):
                    to_neighbour(which, q).wait_send()
                    relay(which, q).wait_send()
                    to_sibling(which, q).wait_send()
                    relayed_to_sibling(which, q - 1).wait_send()

    vec = pl.BlockSpec((1, D), lambda p, i, pos: (0, 0))
    first_phase_rows = lambda p, i, pos: (jnp.where(p == 0, i, nrow - 1), 0)
    anyspec = pl.BlockSpec(memory_space=pl.ANY)
    return pl.pallas_call(
        body, name="proj_gather",
        grid_spec=pltpu.PrefetchScalarGridSpec(
            num_scalar_prefetch=1, grid=(N_CHIP, nrow),
            in_specs=[pl.BlockSpec((tm, D), first_phase_rows), vec, vec, vec, anyspec, anyspec],
            out_specs=[pl.BlockSpec((tm, W_IN_SHARD), lambda p, i, pos: (i, jnp.bitwise_xor(pos[0], p))),
                       pl.BlockSpec((tm, D), first_phase_rows), anyspec, anyspec],
            scratch_shapes=[pltpu.VMEM((s, D), BF16), pltpu.VMEM((D, W_IN_SHARD), BF16),
                            pltpu.SemaphoreType.DMA((16,)), pltpu.SemaphoreType.DMA((16,)), pltpu.SemaphoreType.DMA]),
        out_shape=[jax.ShapeDtypeStruct((s, D_IN), F32), jax.ShapeDtypeStruct((s, D), BF16),
                   jax.ShapeDtypeStruct((D, D_IN), BF16), jax.ShapeDtypeStruct((D, D), BF16)],
        input_output_aliases={5: 2, 6: 3},
        compiler_params=_params("arbitrary", "arbitrary"),
    )(pos, x, shift, scale, norm_g, wi_full, wo_full)


def _proj_specs(rev_nb=None):
    if rev_nb is None:
        row = lambda i: i
    else:
        row = lambda i: rev_nb - 1 - i
    wide = lambda col: pl.BlockSpec((BLK, D_A), lambda i: (row(i), col))
    kv = lambda col: pl.BlockSpec((BLK, D_KV), lambda i: (row(i), col))
    half = lambda col: pl.BlockSpec((BLK, 512), lambda i: (row(i), col))
    return [wide(0), wide(1), wide(2), wide(3), kv(OFF_K // D_KV), kv(OFF_V // D_KV), half(OFF_ZB // 512), half(OFF_ZB // 512 + 1)]


def _mix_fwd_call(proj, cos, sin, ln_g, ln_b, w_sp, b_sp_t, sinks):
    s = proj.shape[0]
    nb = s // BLK

    def body(ua_ref, va_ref, za_ref, q_ref, k_ref, v_ref, zb0_ref, zb1_ref, cos_ref, sin_ref, lg_ref, lb_ref,
             w_ref, bt_ref, sinks_ref, y_ref, kdup_ref, vdup_ref, qm_ref, ost_ref):
        i = pl.program_id(0)
        first_half, lo = _lane_masks()
        cos_t = cos_ref[...]
        sin_t = sin_ref[...]

        _, _, vln = _layer_norm_fwd(va_ref[...], lg_ref[...], lb_ref[...])
        tril = _tril()
        for g in range(GROUPS):
            cols = slice(g * BLK, (g + 1) * BLK)
            wg = jnp.where(tril, w_ref[g], 0.0).astype(BF16)
            sg = jnp.dot(wg, vln[:, cols].astype(BF16), preferred_element_type=F32) + bt_ref[:, g:g + 1]
            gate, _ = _silu_parts(za_ref[:, cols])
            y_ref[:, cols] = (ua_ref[:, cols] * sg * gate).astype(BF16)

        @pl.when(i == 0)
        def _():
            kdup_ref[:, 0:BLK, :] = jnp.zeros((N_KV, BLK, LANE), BF16)
            vdup_ref[:, 0:BLK, :] = jnp.zeros((N_KV, BLK, LANE), BF16)

        @pl.when(i > 0)
        def _():
            kdup_ref[:, 0:BLK, :] = kdup_ref[:, BLK:2 * BLK, :]
            vdup_ref[:, 0:BLK, :] = vdup_ref[:, BLK:2 * BLK, :]

        for ks in range(2):
            cols = slice(ks * LANE, (ks + 1) * LANE)
            kslab = k_ref[:, cols]
            kr = kslab * cos_t + _rot_half(kslab, first_half) * sin_t
            for n, (kd, vd) in enumerate(zip(_dup_kv(kr, lo), _dup_kv(v_ref[:, cols], lo))):
                kdup_ref[2 * ks + n, BLK:2 * BLK, :] = kd
                vdup_ref[2 * ks + n, BLK:2 * BLK, :] = vd
        for sb in range(8):
            qslab = q_ref[:, sb * LANE:(sb + 1) * LANE]
            _stack_heads(qm_ref, sb, qslab * cos_t + _rot_half(qslab, first_half) * sin_t, lo, BF16)

        valid = _band_valid(jnp.where(i > 0, 0, BLK), Q_PER_KV * BLK)

        def kv_head(kh, carry):
            probs, _ = _softmax_sink(qm_ref[kh], kdup_ref[kh], valid, _sink_column(sinks_ref, kh))
            ost_ref[kh] = jnp.dot(probs.astype(BF16), vdup_ref[kh], preferred_element_type=F32)
            return carry

        lax.fori_loop(0, N_KV, kv_head, 0)
        for sb in range(8):
            cols = slice(sb * LANE, (sb + 1) * LANE)
            zb = zb0_ref[:, cols] if sb < 4 else zb1_ref[:, (sb - 4) * LANE:(sb - 3) * LANE]
            gate, _ = _silu_parts(zb)
            y_ref[:, D_A + sb * LANE:D_A + (sb + 1) * LANE] = (_unstack_heads(ost_ref, sb, lo) * gate).astype(BF16)

    tab = pl.BlockSpec((BLK, LANE), lambda i: (i, 0))
    return pl.pallas_call(
        body, name="mix_fwd", grid=(nb,),
        in_specs=_proj_specs() + [
            tab, tab, pl.BlockSpec((1, D_A), lambda i: (0, 0)), pl.BlockSpec((1, D_A), lambda i: (0, 0)),
            pl.BlockSpec((GROUPS, BLK, BLK), lambda i: (0, 0, 0)), pl.BlockSpec((BLK, GROUPS), lambda i: (0, 0)),
            pl.BlockSpec(memory_space=pltpu.SMEM)],
        out_specs=pl.BlockSpec((BLK, 2 * D_A), lambda i: (i, 0)),
        out_shape=jax.ShapeDtypeStruct((s, 2 * D_A), BF16),
        scratch_shapes=[pltpu.VMEM((N_KV, 2 * BLK, LANE), BF16), pltpu.VMEM((N_KV, 2 * BLK, LANE), BF16),
                        pltpu.VMEM((N_KV, Q_PER_KV * BLK, LANE), BF16), pltpu.VMEM((N_KV, Q_PER_KV * BLK, LANE), F32)],
        compiler_params=_params("arbitrary"),
    )(proj, proj, proj, proj, proj, proj, proj, proj, cos, sin, ln_g, ln_b, w_sp, b_sp_t, sinks)


def _tail_call(y, w_out_bf, x, target, gate, shift_f, scale_f, gf):
    s = x.shape[0]
    tm = min(s, 256)
    nsteps = s // tm

    def body(y_ref, w_ref, x_ref, t_ref, gate_ref, shf_ref, scf_ref, gf_ref, dx2_ref, do_ref, st_ref):
        i = pl.program_id(0)

        @pl.when(i == 0)
        def _():
            st_ref[...] = jnp.zeros((8, D), F32)

        o = jnp.dot(y_ref[...], w_ref[...], preferred_element_type=F32)
        gate_v = gate_ref[...]
        x2 = x_ref[...] + gate_v * o
        r2 = lax.rsqrt(jnp.mean(x2 * x2, axis=-1, keepdims=True) + EPS)
        xn2 = x2 * r2
        hn2 = xn2 * gf_ref[...]
        one_sc = 1.0 + scf_ref[...]
        err = hn2 * one_sc + shf_ref[...] - t_ref[...]
        dout = err * (1.0 / D)
        dhn2 = dout * one_sc
        dxn2 = dhn2 * gf_ref[...]
        dx2 = r2 * (dxn2 - xn2 * jnp.mean(dxn2 * xn2, axis=-1, keepdims=True))
        dx2_ref[...] = dx2
        do_ref[...] = (dx2 * gate_v).astype(BF16)
        st_ref[0:1, :] += jnp.sum(dout, axis=0, keepdims=True)
        st_ref[1:2, :] += jnp.sum(dout * hn2, axis=0, keepdims=True)
        st_ref[2:3, :] += jnp.sum(dhn2 * xn2, axis=0, keepdims=True)
        st_ref[3:4, :] += jnp.sum(dx2 * o, axis=0, keepdims=True)
        st_ref[4:5, :] += jnp.sum(err * err, axis=0, keepdims=True)

        @pl.when(i == nsteps - 1)
        def _():
            st_ref[5:6, :] = jnp.full((1, D), 0.5 / D, F32) * jnp.sum(st_ref[4:5, :])

    vec = pl.BlockSpec((1, D), lambda i: (0, 0))
    rows = lambda: pl.BlockSpec((tm, D), lambda i: (i, 0))
    return pl.pallas_call(
        body, name="tail", grid=(nsteps,),
        in_specs=[rows(), pl.BlockSpec((D, D), lambda i: (0, 0)), rows(), rows(), vec, vec, vec, vec],
        out_specs=[rows(), rows(), pl.BlockSpec((8, D), lambda i: (0, 0))],
        out_shape=[jax.ShapeDtypeStruct((s, D), F32), jax.ShapeDtypeStruct((s, D), BF16), jax.ShapeDtypeStruct((8, D), F32)],
        compiler_params=_params("arbitrary"),
    )(y, w_out_bf, x, target, gate, shift_f, scale_f, gf)


def _dy_call(do, w_out_bf):
    s = do.shape[0]
    tm = min(s, 512)

    def body(do_ref, w_ref, dy_ref):
        dy_ref[...] = lax.dot_general(do_ref[...], w_ref[...], NT, preferred_element_type=F32)

    return pl.pallas_call(
        body, name="dy", grid=(s // tm,),
        in_specs=[pl.BlockSpec((tm, D), lambda i: (i, 0)), pl.BlockSpec((D, D), lambda i: (0, 0))],
        out_specs=pl.BlockSpec((tm, D), lambda i: (i, 0)),
        out_shape=jax.ShapeDtypeStruct((s, D), F32), compiler_params=_params("parallel"),
    )(do, w_out_bf)


def _tn_call(a, b, name):
    s, m = a.shape
    n = b.shape[1]
    tn = 512
    ts = min(s, 1024)
    nk = s // ts

    def body(a_ref, b_ref, o_ref, acc_ref):
        k = pl.program_id(1)

        @pl.when(k == 0)
        def _():
            acc_ref[...] = jnp.zeros((m, tn), F32)

        acc_ref[...] += lax.dot_general(a_ref[...], b_ref[...], TN, preferred_element_type=F32)

        @pl.when(k == nk - 1)
        def _():
            o_ref[...] = acc_ref[...].astype(BF16)

    return pl.pallas_call(
        body, name=name, grid=(n // tn, nk),
        in_specs=[pl.BlockSpec((ts, m), lambda j, k: (k, 0)), pl.BlockSpec((ts, tn), lambda j, k: (k, j))],
        out_specs=pl.BlockSpec((m, tn), lambda j, k: (0, j)),
        out_shape=jax.ShapeDtypeStruct((m, n), BF16),
        scratch_shapes=[pltpu.VMEM((m, tn), F32)],
        compiler_params=_params("parallel", "arbitrary"),
    )(a, b)


def _tn_shards_call(pos, a, b, qs, name):
    s, m = a.shape
    ts = min(s, 512)
    nk = s // ts

    def body(pos_ref, a_ref, b_ref, o_ref, acc_ref):
        k = pl.program_id(1)

        @pl.when(k == 0)
        def _():
            acc_ref[...] = jnp.zeros((m, W_IN_SHARD), F32)

        acc_ref[...] += lax.dot_general(a_ref[...], b_ref[...], TN, preferred_element_type=F32)

        @pl.when(k == nk - 1)
        def _():
            o_ref[...] = acc_ref[...].astype(BF16)

    def shard(j, pos):
        q = qs[0]
        for n in range(1, len(qs)):
            q = jnp.where(j == n, qs[n], q)
        return jnp.bitwise_xor(pos[0], q)

    return pl.pallas_call(
        body, name=name,
        grid_spec=pltpu.PrefetchScalarGridSpec(
            num_scalar_prefetch=1, grid=(len(qs), nk),
            in_specs=[pl.BlockSpec((ts, m), lambda j, k, pos: (k, 0)),
                      pl.BlockSpec((ts, W_IN_SHARD), lambda j, k, pos: (k, shard(j, pos)))],
            out_specs=pl.BlockSpec((m, W_IN_SHARD), lambda j, k, pos: (0, j)),
            scratch_shapes=[pltpu.VMEM((m, W_IN_SHARD), F32)]),
        out_shape=jax.ShapeDtypeStruct((m, len(qs) * W_IN_SHARD), BF16),
        compiler_params=_params("parallel", "arbitrary"),
    )(pos, a, b)


def _mix_bwd_call(proj, dy, cos, sin, ln_g, ln_b, w_sp, w_sp_t, b_sp_t, sinks):
    s = proj.shape[0]
    nb = s // BLK
    rev = lambda i: nb - 1 - i
    prev = lambda i: jnp.maximum(nb - 2 - i, 0)

    def body(ua_ref, va_ref, za_ref, q_ref, k_ref, v_ref, zb0_ref, zb1_ref, kp_ref, vp_ref, dy_ref,
             cos_ref, sin_ref, cosp_ref, sinp_ref, lg_ref, lb_ref, w_ref, wt_ref, bt_ref, sinks_ref,
             dp_ref, lnst_ref, dw_ref, dbt_ref, dsink_ref,
             kdup_ref, vdup_ref, dvln_ref, qm_ref, dom_ref, ost_ref, dqst_ref, dkdup_ref, dvdup_ref, kcar_ref, vcar_ref):
        i = pl.program_id(0)
        first_half, lo = _lane_masks()
        lane8 = lax.broadcasted_iota(jnp.int32, (8, LANE), 1)
        cos_t = cos_ref[...]
        sin_t = sin_ref[...]

        @pl.when(i == 0)
        def _():
            lnst_ref[...] = jnp.zeros((8, D_A), F32)
            dw_ref[...] = jnp.zeros((GROUPS, BLK, BLK), F32)
            dbt_ref[...] = jnp.zeros((BLK, LANE), F32)
            dsink_ref[...] = jnp.zeros((8, LANE), F32)
            kcar_ref[...] = jnp.zeros((BLK, D_KV), F32)
            vcar_ref[...] = jnp.zeros((BLK, D_KV), F32)

        vhat, rstd, vln = _layer_norm_fwd(va_ref[...], lg_ref[...], lb_ref[...])
        tril = _tril()
        triu = jnp.logical_not(tril) | (lax.broadcasted_iota(jnp.int32, (BLK, BLK), 0) == lax.broadcasted_iota(jnp.int32, (BLK, BLK), 1))
        lane_b = lax.broadcasted_iota(jnp.int32, (BLK, LANE), 1)
        db_acc = jnp.zeros((BLK, LANE), F32)
        for g in range(GROUPS):
            cols = slice(g * BLK, (g + 1) * BLK)
            vln_g = vln[:, cols].astype(BF16)
            wg = jnp.where(tril, w_ref[g], 0.0).astype(BF16)
            sg = jnp.dot(wg, vln_g, preferred_element_type=F32) + bt_ref[:, g:g + 1]
            za = za_ref[:, cols]
            gate, sig = _silu_parts(za)
            ua = ua_ref[:, cols]
            dya_g = dy_ref[:, cols]
            dya = dya_g * gate
            dp_ref[:, cols] = (dya * sg).astype(BF16)
            dp_ref[:, 2 * D_A + g * BLK:2 * D_A + (g + 1) * BLK] = (
                dya_g * (ua * sg) * (sig * (1.0 + za * (1.0 - sig)))).astype(BF16)
            ds = dya * ua
            ds_b = ds.astype(BF16)
            wtg = jnp.where(triu, wt_ref[g], 0.0).astype(BF16)
            dvln_ref[:, cols] = jnp.dot(wtg, ds_b, preferred_element_type=F32)
            dw_ref[g] += jnp.where(tril, lax.dot_general(ds_b, vln_g, NT, preferred_element_type=F32), 0.0)
            db_acc = db_acc + jnp.where(lane_b == g, jnp.sum(ds, axis=-1, keepdims=True), 0.0)
        dbt_ref[...] += db_acc
        dvln = dvln_ref[...]
        lnst_ref[0:1, :] += jnp.sum(dvln * vhat, axis=0, keepdims=True)
        lnst_ref[1:2, :] += jnp.sum(dvln, axis=0, keepdims=True)
        dvhat = dvln * lg_ref[...]
        m1 = jnp.mean(dvhat, axis=-1, keepdims=True)
        m2 = jnp.mean(dvhat * vhat, axis=-1, keepdims=True)
        dp_ref[:, D_A:2 * D_A] = (rstd * (dvhat - m1 - vhat * m2)).astype(BF16)

        cosp = cosp_ref[...]
        sinp = sinp_ref[...]
        for ks in range(2):
            cols = slice(ks * LANE, (ks + 1) * LANE)
            kslab = k_ref[:, cols]
            kr = kslab * cos_t + _rot_half(kslab, first_half) * sin_t
            kpslab = kp_ref[:, cols]
            kpr = kpslab * cosp + _rot_half(kpslab, first_half) * sinp
            for n, (kc, vc, kp, vp) in enumerate(zip(_dup_kv(kr, lo), _dup_kv(v_ref[:, cols], lo),
                                                     _dup_kv(kpr, lo), _dup_kv(vp_ref[:, cols], lo))):
                kdup_ref[2 * ks + n, BLK:2 * BLK, :] = kc
                vdup_ref[2 * ks + n, BLK:2 * BLK, :] = vc
                kdup_ref[2 * ks + n, 0:BLK, :] = kp
                vdup_ref[2 * ks + n, 0:BLK, :] = vp
        for sb in range(8):
            cols = slice(sb * LANE, (sb + 1) * LANE)
            qslab = q_ref[:, cols]
            _stack_heads(qm_ref, sb, qslab * cos_t + _rot_half(qslab, first_half) * sin_t, lo, BF16)
            zb = zb0_ref[:, cols] if sb < 4 else zb1_ref[:, (sb - 4) * LANE:(sb - 3) * LANE]
            gate, _ = _silu_parts(zb)
            _stack_heads(dom_ref, sb, dy_ref[:, D_A + sb * LANE:D_A + (sb + 1) * LANE] * gate, lo, F32)

        valid = _band_valid(jnp.where(i < nb - 1, 0, BLK), Q_PER_KV * BLK)

        def kv_head(kh, dsink_acc):
            qm = qm_ref[kh]
            kd = kdup_ref[kh]
            vd = vdup_ref[kh]
            probs, psink = _softmax_sink(qm, kd, valid, _sink_column(sinks_ref, kh))
            probs_b = probs.astype(BF16)
            o = jnp.dot(probs_b, vd, preferred_element_type=F32)
            ost_ref[kh] = o
            dom = dom_ref[kh]
            dom_b = dom.astype(BF16)
            delta = jnp.sum(dom * o, axis=-1, keepdims=True)
            dpr = lax.dot_general(dom_b, vd, NT, preferred_element_type=F32)
            dss = (probs * (dpr - delta) * SCALE).astype(BF16)
            sd = psink * delta
            for n in range(Q_PER_KV):
                dsink_acc = dsink_acc + jnp.where(lane8 == Q_PER_KV * kh + n, -jnp.sum(sd[n * BLK:(n + 1) * BLK]), 0.0)
            dqst_ref[kh] = jnp.dot(dss, kd, preferred_element_type=F32)
            dkdup_ref[kh] = lax.dot_general(dss, qm, TN, preferred_element_type=F32)
            dvdup_ref[kh] = lax.dot_general(probs_b, dom_b, TN, preferred_element_type=F32)
            return dsink_acc

        dsink_acc = lax.fori_loop(0, N_KV, kv_head, jnp.zeros((8, LANE), F32))
        row0 = lax.broadcasted_iota(jnp.int32, (8, LANE), 0) == 0
        dsink_ref[...] += jnp.where(row0, dsink_acc, 0.0)

        for sb in range(8):
            cols = slice(sb * LANE, (sb + 1) * LANE)
            zb = zb0_ref[:, cols] if sb < 4 else zb1_ref[:, (sb - 4) * LANE:(sb - 3) * LANE]
            _, sig = _silu_parts(zb)
            dyb = dy_ref[:, D_A + sb * LANE:D_A + (sb + 1) * LANE]
            dp_ref[:, OFF_ZB + sb * LANE:OFF_ZB + (sb + 1) * LANE] = (
                dyb * _unstack_heads(ost_ref, sb, lo) * (sig * (1.0 + zb * (1.0 - sig)))).astype(BF16)
            dq_r = _unstack_heads(dqst_ref, sb, lo)
            dp_ref[:, OFF_Q + sb * LANE:OFF_Q + (sb + 1) * LANE] = (
                dq_r * cos_t - _rot_half(dq_r * sin_t, first_half)).astype(BF16)

        lo2 = lax.broadcasted_iota(jnp.int32, (2 * BLK, LANE), 1) < HEAD
        for ks in range(2):
            cols = slice(ks * LANE, (ks + 1) * LANE)
            ka = dkdup_ref[2 * ks]
            kb = dkdup_ref[2 * ks + 1]
            dk_band = jnp.where(lo2, ka + pltpu.roll(ka, HEAD, 1), kb + pltpu.roll(kb, HEAD, 1))
            va_ = dvdup_ref[2 * ks]
            vb_ = dvdup_ref[2 * ks + 1]
            dv_band = jnp.where(lo2, va_ + pltpu.roll(va_, HEAD, 1), vb_ + pltpu.roll(vb_, HEAD, 1))
            dkr = dk_band[BLK:2 * BLK, :] + kcar_ref[:, cols]
            dp_ref[:, OFF_K + ks * LANE:OFF_K + (ks + 1) * LANE] = (
                dkr * cos_t - _rot_half(dkr * sin_t, first_half)).astype(BF16)
            dp_ref[:, OFF_V + ks * LANE:OFF_V + (ks + 1) * LANE] = (
                dv_band[BLK:2 * BLK, :] + vcar_ref[:, cols]).astype(BF16)
            kcar_ref[:, cols] = dk_band[0:BLK, :]
            vcar_ref[:, cols] = dv_band[0:BLK, :]

    tab = pl.BlockSpec((BLK, LANE), lambda i: (rev(i), 0))
    tabp = pl.BlockSpec((BLK, LANE), lambda i: (prev(i), 0))
    kvp = lambda col: pl.BlockSpec((BLK, D_KV), lambda i: (prev(i), col))
    vec = pl.BlockSpec((1, D_A), lambda i: (0, 0))
    w3 = pl.BlockSpec((GROUPS, BLK, BLK), lambda i: (0, 0, 0))
    return pl.pallas_call(
        body, name="mix_bwd", grid=(nb,),
        in_specs=_proj_specs(nb) + [
            kvp(OFF_K // D_KV), kvp(OFF_V // D_KV), pl.BlockSpec((BLK, 2 * D_A), lambda i: (rev(i), 0)),
            tab, tab, tabp, tabp, vec, vec, w3, w3, pl.BlockSpec((BLK, GROUPS), lambda i: (0, 0)),
            pl.BlockSpec(memory_space=pltpu.SMEM)],
        out_specs=[pl.BlockSpec((BLK, D_IN), lambda i: (rev(i), 0)), pl.BlockSpec((8, D_A), lambda i: (0, 0)), w3,
                   pl.BlockSpec((BLK, LANE), lambda i: (0, 0)), pl.BlockSpec((8, LANE), lambda i: (0, 0))],
        out_shape=[jax.ShapeDtypeStruct((s, D_IN), BF16), jax.ShapeDtypeStruct((8, D_A), F32),
                   jax.ShapeDtypeStruct((GROUPS, BLK, BLK), F32), jax.ShapeDtypeStruct((BLK, LANE), F32),
                   jax.ShapeDtypeStruct((8, LANE), F32)],
        scratch_shapes=[pltpu.VMEM((N_KV, 2 * BLK, LANE), BF16), pltpu.VMEM((N_KV, 2 * BLK, LANE), BF16),
                        pltpu.VMEM((BLK, D_A), F32), pltpu.VMEM((N_KV, Q_PER_KV * BLK, LANE), BF16),
                        pltpu.VMEM((N_KV, Q_PER_KV * BLK, LANE), F32), pltpu.VMEM((N_KV, Q_PER_KV * BLK, LANE), F32),
                        pltpu.VMEM((N_KV, Q_PER_KV * BLK, LANE), F32), pltpu.VMEM((N_KV, 2 * BLK, LANE), F32),
                        pltpu.VMEM((N_KV, 2 * BLK, LANE), F32), pltpu.VMEM((BLK, D_KV), F32), pltpu.VMEM((BLK, D_KV), F32)],
        compiler_params=_params("arbitrary"),
    )(proj, proj, proj, proj, proj, proj, proj, proj, proj, proj, dy, cos, sin, cos, sin, ln_g, ln_b, w_sp, w_sp_t,
      b_sp_t, sinks)


def _dh_call(dproj, w_bf, x, dx2, scale, norm_g):
    s = x.shape[0]
    tm = min(s, 512)
    tk = W_IN_SHARD
    nk = D_IN // tk

    def body(dp_ref, w_ref, x_ref, dx2_ref, sc_ref, g_ref, gx_ref, st_ref, acc_ref):
        i = pl.program_id(0)
        k = pl.program_id(1)

        @pl.when((i == 0) & (k == 0))
        def _():
            st_ref[...] = jnp.zeros((8, D), F32)

        @pl.when(k == 0)
        def _():
            acc_ref[...] = jnp.zeros((tm, D), F32)

        acc_ref[...] += lax.dot_general(dp_ref[...], w_ref[...], NT, preferred_element_type=F32)

        @pl.when(k == nk - 1)
        def _():
            g = g_ref[...]
            one_sc = 1.0 + sc_ref[...]

            def chunk(n, carry):
                rows = pl.ds(pl.multiple_of(n * BLK, BLK), BLK)
                dh = acc_ref[rows, :]
                xv = x_ref[rows, :]
                r = lax.rsqrt(jnp.mean(xv * xv, axis=-1, keepdims=True) + EPS)
                xn = xv * r
                dhn = dh * one_sc
                dxn = dhn * g
                gx_ref[rows, :] = dx2_ref[rows, :] + r * (dxn - xn * jnp.mean(dxn * xn, axis=-1, keepdims=True))
                st_ref[0:1, :] += jnp.sum(dh, axis=0, keepdims=True)
                st_ref[1:2, :] += jnp.sum(dh * (xn * g), axis=0, keepdims=True)
                st_ref[2:3, :] += jnp.sum(dhn * xn, axis=0, keepdims=True)
                return carry

            lax.fori_loop(0, tm // BLK, chunk, 0)

    vec = pl.BlockSpec((1, D), lambda i, k: (0, 0))
    rows = lambda: pl.BlockSpec((tm, D), lambda i, k: (i, 0))
    return pl.pallas_call(
        body, name="dh", grid=(s // tm, nk),
        in_specs=[pl.BlockSpec((tm, tk), lambda i, k: (i, k)), pl.BlockSpec((D, tk), lambda i, k: (0, k)), rows(), rows(), vec, vec],
        out_specs=[rows(), pl.BlockSpec((8, D), lambda i, k: (0, 0))],
        out_shape=[jax.ShapeDtypeStruct((s, D), F32), jax.ShapeDtypeStruct((8, D), F32)],
        scratch_shapes=[pltpu.VMEM((tm, D), F32)],
        compiler_params=_params("arbitrary", "arbitrary"),
    )(dproj, w_bf, x, dx2, scale, norm_g)


def _adam_math(w, g, m, v):
    m_new = ADAM_B1 * m + (1.0 - ADAM_B1) * g
    v_new = ADAM_B2 * v + (1.0 - ADAM_B2) * (g * g)
    m_hat = m_new / ADAM_C1
    v_hat = v_new / ADAM_C2
    delta = -ADAM_LR * (m_hat / (jnp.sqrt(v_hat) + ADAM_EPS) + ADAM_WD * w)
    return delta, m_new, v_new


def _adam_call(w, g, m, v, name):
    r, n = w.shape
    tr = r if r * n * 4 <= (1 << 20) else max(8, (1 << 20) // (n * 4) // 8 * 8)
    while r % tr:
        tr -= 8

    def body(w_ref, g_ref, m_ref, v_ref, d_ref, mo_ref, vo_ref):
        d_ref[...], mo_ref[...], vo_ref[...] = _adam_math(w_ref[...], g_ref[...], m_ref[...], v_ref[...])

    spec = lambda: pl.BlockSpec((tr, n), lambda i: (i, 0))
    return pl.pallas_call(
        body, name=name, grid=(r // tr,), in_specs=[spec() for _ in range(4)], out_specs=[spec() for _ in range(3)],
        out_shape=[jax.ShapeDtypeStruct((r, n), F32)] * 3, compiler_params=_params("parallel"),
    )(w, g, m, v)


def _adam_outer_call(w, ct, dm, m, v, name):
    r, n = w.shape
    tr = 128

    def body(w_ref, ct_ref, dm_ref, m_ref, v_ref, g_ref, d_ref, mo_ref, vo_ref):
        g = ct_ref[:, 0:1] * dm_ref[0:1, :]
        for b in range(1, N_DEV):
            g = g + ct_ref[:, b:b + 1] * dm_ref[b:b + 1, :]
        g_ref[...] = g
        d_ref[...], mo_ref[...], vo_ref[...] = _adam_math(w_ref[...], g, m_ref[...], v_ref[...])

    spec = lambda: pl.BlockSpec((tr, n), lambda i: (i, 0))
    return pl.pallas_call(
        body, name=name, grid=(r // tr,),
        in_specs=[spec(), pl.BlockSpec((tr, N_DEV), lambda i: (i, 0)), pl.BlockSpec((N_DEV, n), lambda i: (0, 0)), spec(), spec()],
        out_specs=[spec() for _ in range(4)],
        out_shape=[jax.ShapeDtypeStruct((r, n), F32)] * 4, compiler_params=_params("parallel"),
    )(w, ct, dm, m, v)


def _sum_pieces_call(pos, part, part_block, recvs, shard_shape, name):
    r, n = recvs[0].shape[1:]
    tr = min(r, 256)
    nrb = r // tr

    def body(pos_ref, p_ref, *refs):
        acc = p_ref[...].astype(F32)
        for r_ref in refs[:-1]:
            for d in range(r_ref.shape[0]):
                acc = acc + r_ref[d].astype(F32)
        refs[-1][...] = acc

    return pl.pallas_call(
        body, name=name,
        grid_spec=pltpu.PrefetchScalarGridSpec(
            num_scalar_prefetch=1, grid=(nrb,),
            in_specs=[pl.BlockSpec((tr, n), lambda i, pos: part_block(i, pos, nrb))] + [
                pl.BlockSpec((rv.shape[0], tr, n), lambda i, pos: (0, i, 0)) for rv in recvs],
            out_specs=pl.BlockSpec((tr, n), lambda i, pos: (pos[1] * nrb + i, 0))),
        out_shape=jax.ShapeDtypeStruct(shard_shape, F32), compiler_params=_params("parallel"),
    )(pos, part, *recvs)


def _coords():
    return lax.axis_index("x"), lax.axis_index("y"), lax.axis_index("c")


def _allgather_sum_call(blk, name, with_sum):
    m_per, n = blk.shape

    def body(x_ref, out_ref, *rest):
        if with_sum:
            sum_ref, send_sems, recv_sems, local_sem = rest
        else:
            send_sems, recv_sems, local_sem = rest
        x, y, c = _coords()
        me, sibling = (x, y, c), (x, y, 1 - c)
        chips = [(1 - x, y), (x, 1 - y), (1 - x, 1 - y)]

        def rows(px, py, pc):
            return out_ref.at[pl.ds((4 * px + 2 * py + pc) * m_per, m_per), :]

        def copy(k, block, to, src=None):
            return pltpu.make_async_remote_copy(
                src_ref=rows(*block) if src is None else src, dst_ref=rows(*block),
                send_sem=send_sems.at[k], recv_sem=recv_sems.at[k], device_id=to, device_id_type=MESH)

        mine = pltpu.make_async_copy(x_ref, rows(*me), local_sem)
        mine.start()
        first = [copy(0, me, sibling, src=x_ref)]
        first += [copy(1 + j, me, (*chip, c), src=x_ref) for j, chip in enumerate(chips)]
        for cp in first:
            cp.start()
        passed = [copy(4 + j, (*chip, c), sibling) for j, chip in enumerate(chips)]
        for j, chip in enumerate(chips):
            copy(1 + j, (*chip, c), me).wait_recv()
            passed[j].start()
        copy(0, sibling, me).wait_recv()
        for j, chip in enumerate(chips):
            copy(4 + j, (*chip, 1 - c), me).wait_recv()
        for cp in first + passed:
            cp.wait_send()
        mine.wait()
        if with_sum:
            acc = out_ref[0:m_per, :]
            for d in range(1, N_DEV):
                acc = acc + out_ref[d * m_per:(d + 1) * m_per, :]
            sum_ref[...] = acc

    vm = pl.BlockSpec(memory_space=pltpu.VMEM)
    out_shape = [jax.ShapeDtypeStruct((N_DEV * m_per, n), F32)]
    if with_sum:
        out_shape.append(jax.ShapeDtypeStruct((m_per, n), F32))
    return pl.pallas_call(
        body, name=name, out_shape=out_shape, in_specs=[vm], out_specs=[vm] * len(out_shape),
        scratch_shapes=[pltpu.SemaphoreType.DMA((7,)), pltpu.SemaphoreType.DMA((7,)), pltpu.SemaphoreType.DMA],
        compiler_params=pltpu.CompilerParams(vmem_limit_bytes=VMEM_LIMIT),
    )(blk)


def _weights_gather_call(wi_full, wo_full):
    hi = D // 2
    ho = W_OUT_SHARD // 2

    def body(wi_in, wo_in, fi_ref, fo_ref, send_sems, recv_sems):
        del wi_in, wo_in
        x, y, c = _coords()
        sibling = (x, y, 1 - c)
        chips = [(1 - x, y), (x, 1 - y), (1 - x, 1 - y)]

        def half(which, px, py, pc):
            j = 2 * px + py
            if which == 0:
                return fi_ref.at[pl.ds(pc * hi, hi), pl.ds(j * W_IN_SHARD, W_IN_SHARD)]
            return fo_ref.at[pl.ds(j * W_OUT_SHARD + pc * ho, ho), :]

        def copy(k, which, block, to):
            return pltpu.make_async_remote_copy(
                src_ref=half(which, *block), dst_ref=half(which, *block), send_sem=send_sems.at[k],
                recv_sem=recv_sems.at[k], device_id=to, device_id_type=MESH)

        first = [copy(6 * w + j, w, (x, y, c), (*chip, c)) for w in range(2) for j, chip in enumerate(chips)]
        for cp in first:
            cp.start()
        passed = []
        for w in range(2):
            for j, chip in enumerate(chips):
                copy(6 * w + j, w, (*chip, c), (x, y, c)).wait_recv()
                cp = copy(6 * w + 3 + j, w, (*chip, c), sibling)
                cp.start()
                passed.append(cp)
        for w in range(2):
            for j, chip in enumerate(chips):
                copy(6 * w + 3 + j, w, (*chip, 1 - c), (x, y, c)).wait_recv()
        for cp in first + passed:
            cp.wait_send()

    anyspec = pl.BlockSpec(memory_space=pl.ANY)
    return pl.pallas_call(
        body, name="weights_gather",
        out_shape=[jax.ShapeDtypeStruct((D, D_IN), BF16), jax.ShapeDtypeStruct((D, D), BF16)],
        in_specs=[anyspec, anyspec], out_specs=[anyspec, anyspec], input_output_aliases={0: 0, 1: 1},
        scratch_shapes=[pltpu.SemaphoreType.DMA((12,)), pltpu.SemaphoreType.DMA((12,))],
    )(wi_full, wo_full)


HBM_SPEC = pl.BlockSpec(memory_space=pltpu.HBM)
SEM_SPEC = pl.BlockSpec(memory_space=pltpu.SEMAPHORE)
SIDE_EFFECT = pltpu.SideEffectType.DATAFLOW_SIDE_EFFECTING


def _peer(x, y, c, q, cb):
    return (1 - x if q & 2 else x, 1 - y if q & 1 else y, 1 - c if cb else c)


def _w_in_piece(slots):
    def piece(part_ref, k, to):
        return part_ref.at[pl.ds(to[2] * (D // 2), D // 2), pl.ds(slots[k] * W_IN_SHARD, W_IN_SHARD)]
    return piece


def _w_out_piece(part_ref, k, to):
    ho = W_OUT_SHARD // 2
    return part_ref.at[pl.ds((2 * to[0] + to[1]) * W_OUT_SHARD + to[2] * ho, ho), :]


def _exchange_start_call(part, rels, piece, slot_shape, name):
    n = len(rels)
    land = lax.empty((n,) + slot_shape, BF16)

    def body(part_ref, land_ref, send_sems, recv_sems, part_thru, land_thru, token):
        x, y, c = _coords()
        for k, (q, cb) in enumerate(rels):
            to = _peer(x, y, c, q, cb)
            pltpu.make_async_remote_copy(src_ref=piece(part_ref, k, to), dst_ref=land_ref.at[k], send_sem=send_sems.at[k],
                                         recv_sem=recv_sems.at[k], device_id=to, device_id_type=MESH).start()
        token[...] = jnp.zeros_like(token)

    return pl.pallas_call(
        body, name=name,
        out_shape=(pltpu.SemaphoreType.DMA((n,)), pltpu.SemaphoreType.DMA((n,)), pltpu.HBM(part.shape, part.dtype),
                   pltpu.HBM(land.shape, land.dtype), jax.ShapeDtypeStruct((8, LANE), F32)),
        in_specs=(HBM_SPEC, HBM_SPEC), out_specs=(SEM_SPEC, SEM_SPEC, HBM_SPEC, HBM_SPEC, pl.BlockSpec(memory_space=pltpu.VMEM)),
        input_output_aliases={0: 2, 1: 3},
        compiler_params=pltpu.CompilerParams(has_side_effects=SIDE_EFFECT),
    )(pltpu.with_memory_space_constraint(part, pltpu.HBM), pltpu.with_memory_space_constraint(land, pltpu.HBM))


def _exchange_wait_call(started, rels, piece, after, name):
    send_sems, recv_sems, part_thru, land_thru, _ = started

    def body(part_ref, land_ref, send_sems, recv_sems, after_ref, part_out, land_out):
        x, y, c = _coords()
        for k, (q, cb) in enumerate(rels):
            to = _peer(x, y, c, q, cb)
            cp = pltpu.make_async_remote_copy(src_ref=piece(part_ref, k, to), dst_ref=land_ref.at[k], send_sem=send_sems.at[k],
                                              recv_sem=recv_sems.at[k], device_id=to, device_id_type=MESH)
            cp.wait_send()
            cp.wait_recv()

    return pl.pallas_call(
        body, name=name,
        out_shape=(pltpu.HBM(part_thru.shape, part_thru.dtype), pltpu.HBM(land_thru.shape, land_thru.dtype)),
        in_specs=(HBM_SPEC, HBM_SPEC, SEM_SPEC, SEM_SPEC, pl.BlockSpec(memory_space=pl.ANY)), out_specs=(HBM_SPEC, HBM_SPEC),
        input_output_aliases={0: 0, 1: 1},
        compiler_params=pltpu.CompilerParams(has_side_effects=SIDE_EFFECT),
    )(part_thru, land_thru, send_sems, recv_sems, after)


def _pair_exchange_call(gi, go):
    hi = D // 2
    ho = W_OUT_SHARD // 2

    def body(gi_in, go_in, fi_ref, fo_ref, send_sems, recv_sems):
        del gi_in, go_in
        x, y, c = _coords()
        sibling = (x, y, 1 - c)
        mine = (fi_ref.at[pl.ds(c * hi, hi), :], fo_ref.at[pl.ds(c * ho, ho), :])
        theirs = (fi_ref.at[pl.ds((1 - c) * hi, hi), :], fo_ref.at[pl.ds((1 - c) * ho, ho), :])
        sends = [pltpu.make_async_remote_copy(src_ref=ref, dst_ref=ref, send_sem=send_sems.at[k], recv_sem=recv_sems.at[k],
                                              device_id=sibling, device_id_type=MESH) for k, ref in enumerate(mine)]
        for cp in sends:
            cp.start()
        for k, ref in enumerate(theirs):
            pltpu.make_async_remote_copy(src_ref=ref, dst_ref=ref, send_sem=send_sems.at[k], recv_sem=recv_sems.at[k],
                                         device_id=sibling, device_id_type=MESH).wait_recv()
        for cp in sends:
            cp.wait_send()

    anyspec = pl.BlockSpec(memory_space=pl.ANY)
    return pl.pallas_call(
        body, name="pair_exchange",
        out_shape=[jax.ShapeDtypeStruct((D, W_IN_SHARD), F32), jax.ShapeDtypeStruct((W_OUT_SHARD, D), F32)],
        in_specs=[anyspec, anyspec], out_specs=[anyspec, anyspec], input_output_aliases={0: 0, 1: 1},
        scratch_shapes=[pltpu.SemaphoreType.DMA((2,)), pltpu.SemaphoreType.DMA((2,))],
    )(gi, go)


def _rope_tables(s):
    inv_freq = 10000.0 ** (-jnp.arange(0, HEAD, 2, dtype=F32) / HEAD)
    ang = jnp.arange(s, dtype=F32)[:, None] * inv_freq[None, :]
    return jnp.tile(jnp.cos(ang), (1, LANE // (HEAD // 2))), jnp.tile(jnp.sin(ang), (1, LANE // (HEAD // 2)))


def _pad_cols(a, n):
    return jnp.pad(a, ((0, 0), (0, n - a.shape[1])))


def kernel(x, c, w_ada, b_ada, norm_g, w_in, ln_v_g, ln_v_b, w_spatial, b_spatial, sinks, w_out, w_ada_final, b_ada_final, final_norm_g, loss_target, m_w_ada, m_b_ada, m_norm_g, m_w_in, m_ln_v_g, m_ln_v_b, m_w_spatial, m_b_spatial, m_sinks, m_w_out, m_w_ada_final, m_b_ada_final, m_final_norm_g, v_w_ada, v_b_ada, v_norm_g, v_w_in, v_ln_v_g, v_ln_v_b, v_w_spatial, v_b_spatial, v_sinks, v_w_out, v_w_ada_final, v_b_ada_final, v_final_norm_g):
    s = x.shape[1]
    ax, ay, ac = _coords()
    chip = 2 * ax + ay
    me = 4 * ax + 2 * ay + ac
    n_ada = w_ada.shape[2]
    n_adaf = w_ada_final.shape[1]

    x2d = x.reshape(s, D)
    tgt = loss_target.reshape(s, D)
    w_ada2, w_in2, w_out2 = w_ada[0], w_in[0], w_out[0]
    b_ada_f2 = b_ada_final.reshape(1, 2 * D)
    gf = final_norm_g.reshape(1, D)

    c_all = _allgather_sum_call(jnp.pad(c, ((0, 7), (0, 0))), "gather_c", False)[0][::8]
    mod_p, c_act = _rowmat_call(c_all, w_ada2, lax.dynamic_slice(b_ada, (0, chip * n_ada), (1, n_ada)), "mod")
    modf_p, _ = _rowmat_call(c_all, w_ada_final, lax.dynamic_slice(b_ada_f2, (0, chip * n_adaf), (1, n_adaf)), "mod_final")
    mods = _allgather_sum_call(jnp.concatenate([mod_p, modf_p], axis=1), "gather_mod", False)[0]
    my_rows = [lax.dynamic_slice(mods, (16 * j + me, 0), (1, n_ada + n_adaf)) for j in range(N_CHIP)]
    mod = jnp.concatenate([r[:, :n_ada] for r in my_rows], axis=1)
    mod_f = jnp.concatenate([r[:, n_ada:] for r in my_rows], axis=1)
    shift, scale, gate = mod[:, :D], mod[:, D:2 * D], mod[:, 2 * D:]
    shift_f, scale_f = mod_f[:, :D], mod_f[:, D:]

    pos = jnp.stack([chip, ac]).astype(jnp.int32)
    w_in_own = _cast_into_call(pos, w_in2, (D, D_IN), "cast_w_in")
    w_out_own = _cast_into_call(pos, w_out2, (D, D), "cast_w_out")

    cos, sin = _rope_tables(s)
    b_sp_t = b_spatial[0].T
    sinks1 = sinks.reshape(N_Q)
    proj, h, w_in_bf, w_out_bf = _proj_gather_call(pos, x2d, shift, scale, norm_g, w_in_own, w_out_own)
    y = _mix_fwd_call(proj, cos, sin, ln_v_g, ln_v_b, w_spatial[0], b_sp_t, sinks1)
    dx2, do, st_tail = _tail_call(y, w_out_bf, x2d, tgt, gate, shift_f, scale_f, gf)

    rel_o = [(0, 1), (1, 0), (1, 1), (2, 0), (2, 1), (3, 0), (3, 1)]
    rel_a = [(1, 0), (1, 1), (2, 0), (2, 1)]
    rel_b = [(3, 0), (3, 1), (0, 1)]
    piece_a, piece_b = _w_in_piece([0, 0, 1, 1]), _w_in_piece([0, 0, 1])
    half_in, half_out = (D // 2, W_IN_SHARD), (W_OUT_SHARD // 2, D)

    g_w_out_p = _tn_call(y, do, "grad_w_out")
    st_o = _exchange_start_call(g_w_out_p, rel_o, _w_out_piece, half_out, "send_w_out")
    dy = _dy_call(do, w_out_bf)
    dproj, st_ln, d_wsp, d_bsp_t, d_sink = _mix_bwd_call(
        proj, dy, cos, sin, ln_v_g + st_o[4][0:1, 0:1], ln_v_b, w_spatial[0], jnp.swapaxes(w_spatial[0], 1, 2), b_sp_t, sinks1)
    g_w_in_a = _tn_shards_call(pos, h, dproj, (1, 2), "grad_w_in_a")
    st_a = _exchange_start_call(g_w_in_a, rel_a, piece_a, half_in, "send_w_in_a")
    g_w_in_b = _tn_shards_call(pos, h, dproj, (3, 0), "grad_w_in_b")
    st_b = _exchange_start_call(g_w_in_b, rel_b, piece_b, half_in, "send_w_in_b")
    grad_x, st_dh = _dh_call(dproj, w_in_bf, x2d, dx2, scale + (st_a[4][0:1, 0:1] + st_b[4][0:1, 0:1]), norm_g)

    g_w_out_p, recv_o = _exchange_wait_call(st_o, rel_o, _w_out_piece, st_dh, "wait_w_out")
    _, recv_a = _exchange_wait_call(st_a, rel_a, piece_a, st_dh, "wait_w_in_a")
    g_w_in_b, recv_b = _exchange_wait_call(st_b, rel_b, piece_b, st_dh, "wait_w_in_b")
    g_w_in, g_w_out = _pair_exchange_call(
        _sum_pieces_call(pos, g_w_in_b, lambda i, p, nrb: (p[1] * nrb + i, 1), [recv_a, recv_b], (D, W_IN_SHARD), "sum_w_in"),
        _sum_pieces_call(pos, g_w_out_p, lambda i, p, nrb: ((2 * p[0] + p[1]) * nrb + i, 0), [recv_o], (W_OUT_SHARD, D), "sum_w_out"))

    pack = jnp.concatenate([
        d_wsp.reshape(64, D), st_tail, st_dh, _pad_cols(st_ln, D),
        _pad_cols(d_bsp_t[:, :GROUPS].T, D), _pad_cols(d_sink, D)], axis=0)
    rows = pack.shape[0]
    packs, tot = _allgather_sum_call(pack, "gather_small", True)
    packs = packs.reshape(N_DEV, rows, D)
    dmod_all = jnp.concatenate([packs[:, 72, :], packs[:, 73, :], packs[:, 67, :]], axis=1)
    dmodf_all = jnp.concatenate([packs[:, 64, :], packs[:, 65, :]], axis=1)
    loss = tot[69, 0]
    grads = {
        "b_ada": jnp.concatenate([tot[72:73], tot[73:74], tot[67:68]], axis=1),
        "norm_g": tot[74:75],
        "ln_v_g": tot[80:81, :D_A],
        "ln_v_b": tot[81:82, :D_A],
        "w_spatial": tot[0:64].reshape(GROUPS * BLK, BLK),
        "b_spatial": tot[88:96, :BLK],
        "sinks": tot[96:97, :N_Q],
        "b_ada_final": jnp.concatenate([tot[64:65], tot[65:66]], axis=1),
        "final_norm_g": tot[66:67],
        "w_in": g_w_in,
        "w_out": g_w_out,
    }

    weights = dict(w_ada=w_ada, b_ada=b_ada, norm_g=norm_g, w_in=w_in, ln_v_g=ln_v_g, ln_v_b=ln_v_b, w_spatial=w_spatial,
                   b_spatial=b_spatial, sinks=sinks, w_out=w_out, w_ada_final=w_ada_final, b_ada_final=b_ada_final,
                   final_norm_g=final_norm_g)
    m_in = dict(w_ada=m_w_ada, b_ada=m_b_ada, norm_g=m_norm_g, w_in=m_w_in, ln_v_g=m_ln_v_g, ln_v_b=m_ln_v_b,
                w_spatial=m_w_spatial, b_spatial=m_b_spatial, sinks=m_sinks, w_out=m_w_out, w_ada_final=m_w_ada_final,
                b_ada_final=m_b_ada_final, final_norm_g=m_final_norm_g)
    v_in = dict(w_ada=v_w_ada, b_ada=v_b_ada, norm_g=v_norm_g, w_in=v_w_in, ln_v_g=v_ln_v_g, ln_v_b=v_ln_v_b,
                w_spatial=v_w_spatial, b_spatial=v_b_spatial, sinks=v_sinks, w_out=v_w_out, w_ada_final=v_w_ada_final,
                b_ada_final=v_b_ada_final, final_norm_g=v_final_norm_g)
    c_act_t = c_act.T
    outer = {"w_ada": lax.dynamic_slice(dmod_all, (0, chip * n_ada), (N_DEV, n_ada)),
             "w_ada_final": lax.dynamic_slice(dmodf_all, (0, chip * n_adaf), (N_DEV, n_adaf))}
    out_g, out_d, out_m, out_v = [], [], [], []
    for name, w in weights.items():
        shape = w.shape
        if name in outer:
            shape2 = (D, outer[name].shape[1])
            g, dl, mn, vn = _adam_outer_call(w.reshape(shape2), c_act_t, outer[name], m_in[name].reshape(shape2),
                                             v_in[name].reshape(shape2), "adam_" + name)
        else:
            g = grads[name]
            shape2 = g.shape
            dl, mn, vn = _adam_call(w.reshape(shape2), g, m_in[name].reshape(shape2), v_in[name].reshape(shape2), "adam_" + name)
        out_g.append(g.reshape(shape))
        out_d.append(dl.reshape(shape))
        out_m.append(mn.reshape(shape))
        out_v.append(vn.reshape(shape))
    return (loss, grad_x.reshape(x.shape), *out_g, *out_d, *out_m, *out_v)
```

```python
import jax
import jax.numpy as jnp
from jax import lax
from jax.experimental import pallas as pl
from jax.experimental.pallas import tpu as pltpu

F32 = jnp.float32
BF16 = jnp.bfloat16
MESH = pl.DeviceIdType.MESH

D = 2048
D_A = 1024
D_B = 1024
D_KV = 256
HEAD = 64
N_Q = 16
N_KV = 4
Q_PER_KV = N_Q // N_KV
BLK = 128
GROUPS = 8
D_IN = 5632
OFF_Q, OFF_K, OFF_V, OFF_ZB = 3072, 4096, 4352, 4608
N_CHIP = 4
N_DEV = 8
W_IN_SHARD = D_IN // N_CHIP
W_OUT_SHARD = D // N_CHIP
EPS = 1e-5
SCALE = HEAD ** -0.5
NEG = -1e30
LANE = 128
ROW_CHUNK = 32
VMEM_LIMIT = 56 * 1024 * 1024

ADAM_LR, ADAM_B1, ADAM_B2, ADAM_EPS, ADAM_WD, ADAM_STEP = 0.001, 0.9, 0.999, 1e-08, 0.01, 10
ADAM_C1 = 1.0 - ADAM_B1 ** ADAM_STEP
ADAM_C2 = 1.0 - ADAM_B2 ** ADAM_STEP

NT = (((1,), (1,)), ((), ()))
TN = (((0,), (0,)), ((), ()))


def _params(*sem):
    return pltpu.CompilerParams(dimension_semantics=sem, vmem_limit_bytes=VMEM_LIMIT)


def _silu_parts(z):
    sig = 1.0 / (1.0 + jnp.exp(-z))
    return z * sig, sig


def _rot_half(v, first_half):
    return jnp.where(first_half, -pltpu.roll(v, 96, 1), pltpu.roll(v, 32, 1))


def _lane_masks():
    lane = lax.broadcasted_iota(jnp.int32, (BLK, LANE), 1)
    return (lane % HEAD) < (HEAD // 2), lane < HEAD


def _band_valid(first_block_bound, rows=BLK):
    rr = lax.broadcasted_iota(jnp.int32, (rows, 2 * BLK), 0) & (BLK - 1)
    jj = lax.broadcasted_iota(jnp.int32, (rows, 2 * BLK), 1)
    return (jj > rr) & (jj <= rr + BLK) & (jj >= first_block_bound)


def _dup_kv(slab, lo):
    rolled = pltpu.roll(slab, HEAD, 1)
    return jnp.where(lo, slab, rolled).astype(BF16), jnp.where(lo, rolled, slab).astype(BF16)


def _stack_heads(ref, sb, slab, lo, dtype):
    kh, base = sb // 2, 2 * (sb % 2) * BLK
    zero = jnp.zeros_like(slab)
    ref[kh, base:base + BLK, :] = jnp.where(lo, slab, zero).astype(dtype)
    ref[kh, base + BLK:base + 2 * BLK, :] = jnp.where(lo, zero, slab).astype(dtype)


def _unstack_heads(ref, sb, lo):
    kh, base = sb // 2, 2 * (sb % 2) * BLK
    return jnp.where(lo, ref[kh, base:base + BLK, :], ref[kh, base + BLK:base + 2 * BLK, :])


def _sink_column(sinks_ref, kh):
    row = lax.broadcasted_iota(jnp.int32, (Q_PER_KV * BLK, 1), 0)
    col = jnp.full(row.shape, sinks_ref[Q_PER_KV * kh + Q_PER_KV - 1], F32)
    for n in range(Q_PER_KV - 2, -1, -1):
        col = jnp.where(row < (n + 1) * BLK, sinks_ref[Q_PER_KV * kh + n], col)
    return col


def _tril():
    t = lax.broadcasted_iota(jnp.int32, (BLK, BLK), 0)
    s = lax.broadcasted_iota(jnp.int32, (BLK, BLK), 1)
    return s <= t


def _layer_norm_fwd(va, lg, lb):
    mu = jnp.mean(va, axis=-1, keepdims=True)
    xc = va - mu
    rstd = lax.rsqrt(jnp.mean(xc * xc, axis=-1, keepdims=True) + EPS)
    vhat = xc * rstd
    return vhat, rstd, vhat * lg + lb


def _softmax_sink(qm, kexp, valid, sink):
    s = lax.dot_general(qm, kexp, NT, preferred_element_type=F32) * SCALE
    s = jnp.where(valid, s, NEG)
    m = jnp.maximum(jnp.max(s, axis=-1, keepdims=True), sink)
    p = jnp.exp(s - m)
    esink = jnp.exp(sink - m)
    den = jnp.sum(p, axis=-1, keepdims=True) + esink
    return p / den, esink / den


def _rowmat_call(c_all, w, b, name):
    n = w.shape[1]
    tn = 512

    def body(c_ref, w_ref, b_ref, o_ref, ca_ref):
        ca, _ = _silu_parts(c_ref[...])
        ca_ref[...] = ca
        o_ref[...] = jnp.dot(ca.astype(BF16), w_ref[...].astype(BF16), preferred_element_type=F32) + b_ref[...]

    return pl.pallas_call(
        body, name=name, grid=(n // tn,),
        in_specs=[pl.BlockSpec((N_DEV, D), lambda j: (0, 0)), pl.BlockSpec((D, tn), lambda j: (0, j)),
                  pl.BlockSpec((1, tn), lambda j: (0, j))],
        out_specs=[pl.BlockSpec((N_DEV, tn), lambda j: (0, j)), pl.BlockSpec((N_DEV, D), lambda j: (0, 0))],
        out_shape=[jax.ShapeDtypeStruct((N_DEV, n), F32), jax.ShapeDtypeStruct((N_DEV, D), F32)],
        compiler_params=_params("arbitrary"),
    )(c_all, w, b)


def _cast_into_call(pos, w, full_shape, name):
    r, n = w.shape
    tr = min(r, 512)
    by_cols = full_shape[0] == r
    nrb = r // tr

    def body(pos_ref, w_ref, o_ref):
        o_ref[...] = w_ref[...].astype(BF16)

    out_map = (lambda i, pos: (i, pos[0])) if by_cols else (lambda i, pos: (pos[0] * nrb + i, 0))
    return pl.pallas_call(
        body, name=name,
        grid_spec=pltpu.PrefetchScalarGridSpec(
            num_scalar_prefetch=1, grid=(nrb,),
            in_specs=[pl.BlockSpec((tr, n), lambda i, pos: (i, 0))], out_specs=pl.BlockSpec((tr, n), out_map)),
        out_shape=jax.ShapeDtypeStruct(full_shape, BF16), compiler_params=_params("parallel"),
    )(pos, w)


def _proj_call(x, shift, scale, norm_g, w_bf):
    s = x.shape[0]
    tm = min(s, 1024)
    tn = 512

    def body(x_ref, sh_ref, sc_ref, g_ref, w_ref, proj_ref, h_ref):
        @pl.when(pl.program_id(1) == 0)
        def _():
            xv = x_ref[...]
            r = lax.rsqrt(jnp.mean(xv * xv, axis=-1, keepdims=True) + EPS)
            h_ref[...] = ((xv * r * g_ref[...]) * (1.0 + sc_ref[...]) + sh_ref[...]).astype(BF16)

        proj_ref[...] = jnp.dot(h_ref[...], w_ref[...], preferred_element_type=F32)

    vec = pl.BlockSpec((1, D), lambda i, j: (0, 0))
    return pl.pallas_call(
        body, name="proj", grid=(s // tm, D_IN // tn),
        in_specs=[pl.BlockSpec((tm, D), lambda i, j: (i, 0)), vec, vec, vec, pl.BlockSpec((D, tn), lambda i, j: (0, j))],
        out_specs=[pl.BlockSpec((tm, tn), lambda i, j: (i, j)), pl.BlockSpec((tm, D), lambda i, j: (i, 0))],
        out_shape=[jax.ShapeDtypeStruct((s, D_IN), F32), jax.ShapeDtypeStruct((s, D), BF16)],
        compiler_params=_params("parallel", "arbitrary"),
    )(x, shift, scale, norm_g, w_bf)


def _proj_gather_call(pos, x, shift, scale, norm_g, wi_full, wo_full):
    s = x.shape[0]
    tm = min(s, 512)
    nrow = s // tm
    hi = D // 2
    ho = W_OUT_SHARD // 2

    def body(pos_ref, x_ref, sh_ref, sc_ref, g_ref, wi_in, wo_in, proj_ref, h_ref, fi_ref, fo_ref,
             h_all, wbuf, send_sems, recv_sems, load_sem):
        del wi_in, wo_in
        p = pl.program_id(0)
        i = pl.program_id(1)
        x_, y_, c_ = _coords()
        me, sibling = (x_, y_, c_), (x_, y_, 1 - c_)

        def shard_of(q):
            px, py, _ = _peer(x_, y_, c_, q, 0)
            return 2 * px + py

        def part(which, q, pc, sub=None):
            n = hi if which == 0 else ho
            base = pc * n
            if sub is not None:
                n //= 2
                base = base + sub * n
            if which == 0:
                return fi_ref.at[pl.ds(base, n), pl.ds(shard_of(q) * W_IN_SHARD, W_IN_SHARD)]
            return fo_ref.at[pl.ds(shard_of(q) * W_OUT_SHARD + base, n), :]

        def copy(k, which, q, pc, to, sub=None):
            ref = part(which, q, pc, sub)
            return pltpu.make_async_remote_copy(src_ref=ref, dst_ref=ref, send_sem=send_sems.at[k], recv_sem=recv_sems.at[k],
                                                device_id=to, device_id_type=MESH)

        def to_neighbour(which, q):
            return copy(8 * which + q - 1, which, 0, c_, _peer(x_, y_, c_, q, 0))

        def from_neighbour(which, q):
            return copy(8 * which + q - 1, which, q, c_, me)

        def relay(which, q):
            return copy(8 * which + 2 + q - 1, which, q, c_, _peer(x_, y_, c_, 3 - q, 0), q - 1)

        def relayed(which, sub):
            return copy(8 * which + 2 + sub, which, 3, c_, me, sub)

        def to_sibling(which, q):
            return copy(8 * which + 4 + q - 1, which, q, c_, sibling)

        def from_sibling(which, q):
            return copy(8 * which + 4 + q - 1, which, q, 1 - c_, me)

        def relayed_to_sibling(which, sub):
            return copy(8 * which + 6 + sub, which, 3, c_, sibling, sub)

        def relayed_from_sibling(which, sub):
            return copy(8 * which + 6 + sub, which, 3, 1 - c_, me, sub)

        def pass_on_neighbours(which):
            for q in (1, 2):
                from_neighbour(which, q).wait_recv()
                to_sibling(which, q).start()
                relay(which, q).start()

        def pass_on_relayed(which):
            for sub in range(2):
                relayed(which, sub).wait_recv()
                relayed_to_sibling(which, sub).start()

        def load_shard(q):
            cp = pltpu.make_async_copy(fi_ref.at[:, pl.ds(shard_of(q) * W_IN_SHARD, W_IN_SHARD)], wbuf, load_sem)
            cp.start()
            cp.wait()

        @pl.when((p == 0) & (i == 0))
        def _():
            for q in (1, 2):
                to_neighbour(0, q).start()
            load_shard(0)

        @pl.when((p == 1) & (i == 0))
        def _():
            pass_on_neighbours(0)
            for q in (1, 2):
                to_neighbour(1, q).start()
            from_sibling(0, 1).wait_recv()
            load_shard(1)

        @pl.when((p == 2) & (i == 0))
        def _():
            from_sibling(0, 2).wait_recv()
            load_shard(2)

        @pl.when((p == 3) & (i == 0))
        def _():
            pass_on_relayed(0)
            pass_on_neighbours(1)
            for sub in range(2):
                relayed_from_sibling(0, sub).wait_recv()
            load_shard(3)

        rows = pl.ds(pl.multiple_of(i * tm, tm), tm)

        @pl.when(p == 0)
        def _():
            g = g_ref[...]
            one_sc = 1.0 + sc_ref[...]
            sh = sh_ref[...]

            def chunk(n, carry):
                sub = pl.ds(pl.multiple_of(n * ROW_CHUNK, ROW_CHUNK), ROW_CHUNK)
                xv = x_ref[sub, :]
                r = lax.rsqrt(jnp.mean(xv * xv, axis=-1, keepdims=True) + EPS)
                hv = ((xv * r * g) * one_sc + sh).astype(BF16)
                h_ref[sub, :] = hv
                h_all[pl.ds(pl.multiple_of(i * tm + n * ROW_CHUNK, ROW_CHUNK), ROW_CHUNK), :] = hv
                return carry

            lax.fori_loop(0, tm // ROW_CHUNK, chunk, 0)

        proj_ref[...] = jnp.dot(h_all[rows, :], wbuf[...], preferred_element_type=F32)

        @pl.when((p == N_CHIP - 1) & (i == nrow - 1))
        def _():
            pass_on_relayed(1)
            for q in (1, 2):
                from_sibling(1, q).wait_recv()
            for sub in range(2):
                relayed_from_sibling(1, sub).wait_recv()
            for which in range(2):
                for q in (1, 2):
                    to_neighbour(which, q).wait_send()
                    relay(which, q).wait_send()
                    to_sibling(which, q).wait_send()
                    relayed_to_sibling(which, q - 1).wait_send()

    vec = pl.BlockSpec((1, D), lambda p, i, pos: (0, 0))
    first_phase_rows = lambda p, i, pos: (jnp.where(p == 0, i, nrow - 1), 0)
    anyspec = pl.BlockSpec(memory_space=pl.ANY)
    return pl.pallas_call(
        body, name="proj_gather",
        grid_spec=pltpu.PrefetchScalarGridSpec(
            num_scalar_prefetch=1, grid=(N_CHIP, nrow),
            in_specs=[pl.BlockSpec((tm, D), first_phase_rows), vec, vec, vec, anyspec, anyspec],
            out_specs=[pl.BlockSpec((tm, W_IN_SHARD), lambda p, i, pos: (i, jnp.bitwise_xor(pos[0], p))),
                       pl.BlockSpec((tm, D), first_phase_rows), anyspec, anyspec],
            scratch_shapes=[pltpu.VMEM((s, D), BF16), pltpu.VMEM((D, W_IN_SHARD), BF16),
                            pltpu.SemaphoreType.DMA((16,)), pltpu.SemaphoreType.DMA((16,)), pltpu.SemaphoreType.DMA]),
        out_shape=[jax.ShapeDtypeStruct((s, D_IN), F32), jax.ShapeDtypeStruct((s, D), BF16),
                   jax.ShapeDtypeStruct((D, D_IN), BF16), jax.ShapeDtypeStruct((D, D), BF16)],
        input_output_aliases={5: 2, 6: 3},
        compiler_params=_params("arbitrary", "arbitrary"),
    )(pos, x, shift, scale, norm_g, wi_full, wo_full)


def _proj_specs(rev_nb=None):
    if rev_nb is None:
        row = lambda i: i
    else:
        row = lambda i: rev_nb - 1 - i
    wide = lambda col: pl.BlockSpec((BLK, D_A), lambda i: (row(i), col))
    kv = lambda col: pl.BlockSpec((BLK, D_KV), lambda i: (row(i), col))
    half = lambda col: pl.BlockSpec((BLK, 512), lambda i: (row(i), col))
    return [wide(0), wide(1), wide(2), wide(3), kv(OFF_K // D_KV), kv(OFF_V // D_KV), half(OFF_ZB // 512), half(OFF_ZB // 512 + 1)]


def _mix_fwd_call(proj, cos, sin, ln_g, ln_b, w_sp, b_sp_t, sinks):
    s = proj.shape[0]
    nb = s // BLK

    def body(ua_ref, va_ref, za_ref, q_ref, k_ref, v_ref, zb0_ref, zb1_ref, cos_ref, sin_ref, lg_ref, lb_ref,
             w_ref, bt_ref, sinks_ref, y_ref, kdup_ref, vdup_ref, qm_ref, ost_ref):
        i = pl.program_id(0)
        first_half, lo = _lane_masks()
        cos_t = cos_ref[...]
        sin_t = sin_ref[...]

        _, _, vln = _layer_norm_fwd(va_ref[...], lg_ref[...], lb_ref[...])
        tril = _tril()
        for g in range(GROUPS):
            cols = slice(g * BLK, (g + 1) * BLK)
            wg = jnp.where(tril, w_ref[g], 0.0).astype(BF16)
            sg = jnp.dot(wg, vln[:, cols].astype(BF16), preferred_element_type=F32) + bt_ref[:, g:g + 1]
            gate, _ = _silu_parts(za_ref[:, cols])
            y_ref[:, cols] = (ua_ref[:, cols] * sg * gate).astype(BF16)

        @pl.when(i == 0)
        def _():
            kdup_ref[:, 0:BLK, :] = jnp.zeros((N_KV, BLK, LANE), BF16)
            vdup_ref[:, 0:BLK, :] = jnp.zeros((N_KV, BLK, LANE), BF16)

        @pl.when(i > 0)
        def _():
            kdup_ref[:, 0:BLK, :] = kdup_ref[:, BLK:2 * BLK, :]
            vdup_ref[:, 0:BLK, :] = vdup_ref[:, BLK:2 * BLK, :]

        for ks in range(2):
            cols = slice(ks * LANE, (ks + 1) * LANE)
            kslab = k_ref[:, cols]
            kr = kslab * cos_t + _rot_half(kslab, first_half) * sin_t
            for n, (kd, vd) in enumerate(zip(_dup_kv(kr, lo), _dup_kv(v_ref[:, cols], lo))):
                kdup_ref[2 * ks + n, BLK:2 * BLK, :] = kd
                vdup_ref[2 * ks + n, BLK:2 * BLK, :] = vd
        for sb in range(8):
            qslab = q_ref[:, sb * LANE:(sb + 1) * LANE]
            _stack_heads(qm_ref, sb, qslab * cos_t + _rot_half(qslab, first_half) * sin_t, lo, BF16)

        valid = _band_valid(jnp.where(i > 0, 0, BLK), Q_PER_KV * BLK)

        def kv_head(kh, carry):
            probs, _ = _softmax_sink(qm_ref[kh], kdup_ref[kh], valid, _sink_column(sinks_ref, kh))
            ost_ref[kh] = jnp.dot(probs.astype(BF16), vdup_ref[kh], preferred_element_type=F32)
            return carry

        lax.fori_loop(0, N_KV, kv_head, 0)
        for sb in range(8):
            cols = slice(sb * LANE, (sb + 1) * LANE)
            zb = zb0_ref[:, cols] if sb < 4 else zb1_ref[:, (sb - 4) * LANE:(sb - 3) * LANE]
            gate, _ = _silu_parts(zb)
            y_ref[:, D_A + sb * LANE:D_A + (sb + 1) * LANE] = (_unstack_heads(ost_ref, sb, lo) * gate).astype(BF16)

    tab = pl.BlockSpec((BLK, LANE), lambda i: (i, 0))
    return pl.pallas_call(
        body, name="mix_fwd", grid=(nb,),
        in_specs=_proj_specs() + [
            tab, tab, pl.BlockSpec((1, D_A), lambda i: (0, 0)), pl.BlockSpec((1, D_A), lambda i: (0, 0)),
            pl.BlockSpec((GROUPS, BLK, BLK), lambda i: (0, 0, 0)), pl.BlockSpec((BLK, GROUPS), lambda i: (0, 0)),
            pl.BlockSpec(memory_space=pltpu.SMEM)],
        out_specs=pl.BlockSpec((BLK, 2 * D_A), lambda i: (i, 0)),
        out_shape=jax.ShapeDtypeStruct((s, 2 * D_A), BF16),
        scratch_shapes=[pltpu.VMEM((N_KV, 2 * BLK, LANE), BF16), pltpu.VMEM((N_KV, 2 * BLK, LANE), BF16),
                        pltpu.VMEM((N_KV, Q_PER_KV * BLK, LANE), BF16), pltpu.VMEM((N_KV, Q_PER_KV * BLK, LANE), F32)],
        compiler_params=_params("arbitrary"),
    )(proj, proj, proj, proj, proj, proj, proj, proj, cos, sin, ln_g, ln_b, w_sp, b_sp_t, sinks)


def _tail_call(y, w_out_bf, x, target, gate, shift_f, scale_f, gf):
    s = x.shape[0]
    tm = min(s, 256)
    nsteps = s // tm

    def body(y_ref, w_ref, x_ref, t_ref, gate_ref, shf_ref, scf_ref, gf_ref, dx2_ref, do_ref, st_ref, o_ref):
        i = pl.program_id(0)

        @pl.when(i == 0)
        def _():
            st_ref[...] = jnp.zeros((8, D), F32)

        o_ref[...] = jnp.dot(y_ref[...], w_ref[...], preferred_element_type=F32)
        gate_v = gate_ref[...]
        gf_v = gf_ref[...]
        shf_v = shf_ref[...]
        one_sc = 1.0 + scf_ref[...]

        def chunk(n, carry):
            rows = pl.ds(pl.multiple_of(n * ROW_CHUNK, ROW_CHUNK), ROW_CHUNK)
            o = o_ref[rows, :]
            x2 = x_ref[rows, :] + gate_v * o
            r2 = lax.rsqrt(jnp.mean(x2 * x2, axis=-1, keepdims=True) + EPS)
            xn2 = x2 * r2
            hn2 = xn2 * gf_v
            err = hn2 * one_sc + shf_v - t_ref[rows, :]
            dout = err * (1.0 / D)
            dhn2 = dout * one_sc
            dxn2 = dhn2 * gf_v
            dx2 = r2 * (dxn2 - xn2 * jnp.mean(dxn2 * xn2, axis=-1, keepdims=True))
            dx2_ref[rows, :] = dx2
            do_ref[rows, :] = (dx2 * gate_v).astype(BF16)
            st_ref[0:1, :] += jnp.sum(dout, axis=0, keepdims=True)
            st_ref[1:2, :] += jnp.sum(dout * hn2, axis=0, keepdims=True)
            st_ref[2:3, :] += jnp.sum(dhn2 * xn2, axis=0, keepdims=True)
            st_ref[3:4, :] += jnp.sum(dx2 * o, axis=0, keepdims=True)
            st_ref[4:5, :] += jnp.sum(err * err, axis=0, keepdims=True)
            return carry

        lax.fori_loop(0, tm // ROW_CHUNK, chunk, 0)

        @pl.when(i == nsteps - 1)
        def _():
            st_ref[5:6, :] = jnp.full((1, D), 0.5 / D, F32) * jnp.sum(st_ref[4:5, :])

    vec = pl.BlockSpec((1, D), lambda i: (0, 0))
    rows = lambda: pl.BlockSpec((tm, D), lambda i: (i, 0))
    return pl.pallas_call(
        body, name="tail", grid=(nsteps,),
        in_specs=[rows(), pl.BlockSpec((D, D), lambda i: (0, 0)), rows(), rows(), vec, vec, vec, vec],
        out_specs=[rows(), rows(), pl.BlockSpec((8, D), lambda i: (0, 0))],
        out_shape=[jax.ShapeDtypeStruct((s, D), F32), jax.ShapeDtypeStruct((s, D), BF16), jax.ShapeDtypeStruct((8, D), F32)],
        scratch_shapes=[pltpu.VMEM((tm, D), F32)],
        compiler_params=_params("arbitrary"),
    )(y, w_out_bf, x, target, gate, shift_f, scale_f, gf)


def _dy_call(do, w_out_bf):
    s = do.shape[0]
    tm = min(s, 512)

    def body(do_ref, w_ref, dy_ref):
        dy_ref[...] = lax.dot_general(do_ref[...], w_ref[...], NT, preferred_element_type=F32)

    return pl.pallas_call(
        body, name="dy", grid=(s // tm,),
        in_specs=[pl.BlockSpec((tm, D), lambda i: (i, 0)), pl.BlockSpec((D, D), lambda i: (0, 0))],
        out_specs=pl.BlockSpec((tm, D), lambda i: (i, 0)),
        out_shape=jax.ShapeDtypeStruct((s, D), F32), compiler_params=_params("parallel"),
    )(do, w_out_bf)


def _tn_call(a, b, name):
    s, m = a.shape
    n = b.shape[1]
    tn = 512
    ts = min(s, 1024)
    nk = s // ts

    def body(a_ref, b_ref, o_ref, acc_ref):
        k = pl.program_id(1)

        @pl.when(k == 0)
        def _():
            acc_ref[...] = jnp.zeros((m, tn), F32)

        acc_ref[...] += lax.dot_general(a_ref[...], b_ref[...], TN, preferred_element_type=F32)

        @pl.when(k == nk - 1)
        def _():
            o_ref[...] = acc_ref[...].astype(BF16)

    return pl.pallas_call(
        body, name=name, grid=(n // tn, nk),
        in_specs=[pl.BlockSpec((ts, m), lambda j, k: (k, 0)), pl.BlockSpec((ts, tn), lambda j, k: (k, j))],
        out_specs=pl.BlockSpec((m, tn), lambda j, k: (0, j)),
        out_shape=jax.ShapeDtypeStruct((m, n), BF16),
        scratch_shapes=[pltpu.VMEM((m, tn), F32)],
        compiler_params=_params("parallel", "arbitrary"),
    )(a, b)


def _tn_shards_call(pos, a, b, qs, name):
    s, m = a.shape
    ts = min(s, 1024)
    nk = s // ts

    def body(pos_ref, a_ref, b_ref, o_ref, acc_ref):
        k = pl.program_id(1)

        @pl.when(k == 0)
        def _():
            acc_ref[...] = jnp.zeros((m, W_IN_SHARD), F32)

        acc_ref[...] += lax.dot_general(a_ref[...], b_ref[...], TN, preferred_element_type=F32)

        @pl.when(k == nk - 1)
        def _():
            o_ref[...] = acc_ref[...].astype(BF16)

    def shard(j, pos):
        q = qs[0]
        for n in range(1, len(qs)):
            q = jnp.where(j == n, qs[n], q)
        return jnp.bitwise_xor(pos[0], q)

    return pl.pallas_call(
        body, name=name,
        grid_spec=pltpu.PrefetchScalarGridSpec(
            num_scalar_prefetch=1, grid=(len(qs), nk),
            in_specs=[pl.BlockSpec((ts, m), lambda j, k, pos: (k, 0)),
                      pl.BlockSpec((ts, W_IN_SHARD), lambda j, k, pos: (k, shard(j, pos)))],
            out_specs=pl.BlockSpec((m, W_IN_SHARD), lambda j, k, pos: (0, j)),
            scratch_shapes=[pltpu.VMEM((m, W_IN_SHARD), F32)]),
        out_shape=jax.ShapeDtypeStruct((m, len(qs) * W_IN_SHARD), BF16),
        compiler_params=_params("parallel", "arbitrary"),
    )(pos, a, b)


def _mix_bwd_call(proj, dy, cos, sin, ln_g, ln_b, w_sp, w_sp_t, b_sp_t, sinks):
    s = proj.shape[0]
    nb = s // BLK
    rev = lambda i: nb - 1 - i
    prev = lambda i: jnp.maximum(nb - 2 - i, 0)

    def body(ua_ref, va_ref, za_ref, q_ref, k_ref, v_ref, zb0_ref, zb1_ref, kp_ref, vp_ref, dy_ref,
             cos_ref, sin_ref, cosp_ref, sinp_ref, lg_ref, lb_ref, w_ref, wt_ref, bt_ref, sinks_ref,
             dp_ref, lnst_ref, dw_ref, dbt_ref, dsink_ref,
             kdup_ref, vdup_ref, dvln_ref, qm_ref, dom_ref, ost_ref, dqst_ref, dkdup_ref, dvdup_ref, kcar_ref, vcar_ref):
        i = pl.program_id(0)
        first_half, lo = _lane_masks()
        lane8 = lax.broadcasted_iota(jnp.int32, (8, LANE), 1)
        cos_t = cos_ref[...]
        sin_t = sin_ref[...]

        @pl.when(i == 0)
        def _():
            lnst_ref[...] = jnp.zeros((8, D_A), F32)
            dw_ref[...] = jnp.zeros((GROUPS, BLK, BLK), F32)
            dbt_ref[...] = jnp.zeros((BLK, LANE), F32)
            dsink_ref[...] = jnp.zeros((8, LANE), F32)
            kcar_ref[...] = jnp.zeros((BLK, D_KV), F32)
            vcar_ref[...] = jnp.zeros((BLK, D_KV), F32)

        vhat, rstd, vln = _layer_norm_fwd(va_ref[...], lg_ref[...], lb_ref[...])
        tril = _tril()
        triu = jnp.logical_not(tril) | (lax.broadcasted_iota(jnp.int32, (BLK, BLK), 0) == lax.broadcasted_iota(jnp.int32, (BLK, BLK), 1))
        lane_b = lax.broadcasted_iota(jnp.int32, (BLK, LANE), 1)
        db_acc = jnp.zeros((BLK, LANE), F32)
        for g in range(GROUPS):
            cols = slice(g * BLK, (g + 1) * BLK)
            vln_g = vln[:, cols].astype(BF16)
            wg = jnp.where(tril, w_ref[g], 0.0).astype(BF16)
            sg = jnp.dot(wg, vln_g, preferred_element_type=F32) + bt_ref[:, g:g + 1]
            za = za_ref[:, cols]
            gate, sig = _silu_parts(za)
            ua = ua_ref[:, cols]
            dya_g = dy_ref[:, cols]
            dya = dya_g * gate
            dp_ref[:, cols] = (dya * sg).astype(BF16)
            dp_ref[:, 2 * D_A + g * BLK:2 * D_A + (g + 1) * BLK] = (
                dya_g * (ua * sg) * (sig * (1.0 + za * (1.0 - sig)))).astype(BF16)
            ds = dya * ua
            ds_b = ds.astype(BF16)
            wtg = jnp.where(triu, wt_ref[g], 0.0).astype(BF16)
            dvln_ref[:, cols] = jnp.dot(wtg, ds_b, preferred_element_type=F32)
            dw_ref[g] += jnp.where(tril, lax.dot_general(ds_b, vln_g, NT, preferred_element_type=F32), 0.0)
            db_acc = db_acc + jnp.where(lane_b == g, jnp.sum(ds, axis=-1, keepdims=True), 0.0)
        dbt_ref[...] += db_acc
        dvln = dvln_ref[...]
        lnst_ref[0:1, :] += jnp.sum(dvln * vhat, axis=0, keepdims=True)
        lnst_ref[1:2, :] += jnp.sum(dvln, axis=0, keepdims=True)
        dvhat = dvln * lg_ref[...]
        m1 = jnp.mean(dvhat, axis=-1, keepdims=True)
        m2 = jnp.mean(dvhat * vhat, axis=-1, keepdims=True)
        dp_ref[:, D_A:2 * D_A] = (rstd * (dvhat - m1 - vhat * m2)).astype(BF16)

        cosp = cosp_ref[...]
        sinp = sinp_ref[...]
        for ks in range(2):
            cols = slice(ks * LANE, (ks + 1) * LANE)
            kslab = k_ref[:, cols]
            kr = kslab * cos_t + _rot_half(kslab, first_half) * sin_t
            kpslab = kp_ref[:, cols]
            kpr = kpslab * cosp + _rot_half(kpslab, first_half) * sinp
            for n, (kc, vc, kp, vp) in enumerate(zip(_dup_kv(kr, lo), _dup_kv(v_ref[:, cols], lo),
                                                     _dup_kv(kpr, lo), _dup_kv(vp_ref[:, cols], lo))):
                kdup_ref[2 * ks + n, BLK:2 * BLK, :] = kc
                vdup_ref[2 * ks + n, BLK:2 * BLK, :] = vc
                kdup_ref[2 * ks + n, 0:BLK, :] = kp
                vdup_ref[2 * ks + n, 0:BLK, :] = vp
        for sb in range(8):
            cols = slice(sb * LANE, (sb + 1) * LANE)
            qslab = q_ref[:, cols]
            _stack_heads(qm_ref, sb, qslab * cos_t + _rot_half(qslab, first_half) * sin_t, lo, BF16)
            zb = zb0_ref[:, cols] if sb < 4 else zb1_ref[:, (sb - 4) * LANE:(sb - 3) * LANE]
            gate, _ = _silu_parts(zb)
            _stack_heads(dom_ref, sb, dy_ref[:, D_A + sb * LANE:D_A + (sb + 1) * LANE] * gate, lo, F32)

        valid = _band_valid(jnp.where(i < nb - 1, 0, BLK), Q_PER_KV * BLK)

        def kv_head(kh, dsink_acc):
            qm = qm_ref[kh]
            kd = kdup_ref[kh]
            vd = vdup_ref[kh]
            probs, psink = _softmax_sink(qm, kd, valid, _sink_column(sinks_ref, kh))
            probs_b = probs.astype(BF16)
            o = jnp.dot(probs_b, vd, preferred_element_type=F32)
            ost_ref[kh] = o
            dom = dom_ref[kh]
            dom_b = dom.astype(BF16)
            delta = jnp.sum(dom * o, axis=-1, keepdims=True)
            dpr = lax.dot_general(dom_b, vd, NT, preferred_element_type=F32)
            dss = (probs * (dpr - delta) * SCALE).astype(BF16)
            sd = psink * delta
            for n in range(Q_PER_KV):
                dsink_acc = dsink_acc + jnp.where(lane8 == Q_PER_KV * kh + n, -jnp.sum(sd[n * BLK:(n + 1) * BLK]), 0.0)
            dqst_ref[kh] = jnp.dot(dss, kd, preferred_element_type=F32)
            dkdup_ref[kh] = lax.dot_general(dss, qm, TN, preferred_element_type=F32)
            dvdup_ref[kh] = lax.dot_general(probs_b, dom_b, TN, preferred_element_type=F32)
            return dsink_acc

        dsink_acc = lax.fori_loop(0, N_KV, kv_head, jnp.zeros((8, LANE), F32))
        row0 = lax.broadcasted_iota(jnp.int32, (8, LANE), 0) == 0
        dsink_ref[...] += jnp.where(row0, dsink_acc, 0.0)

        for sb in range(8):
            cols = slice(sb * LANE, (sb + 1) * LANE)
            zb = zb0_ref[:, cols] if sb < 4 else zb1_ref[:, (sb - 4) * LANE:(sb - 3) * LANE]
            _, sig = _silu_parts(zb)
            dyb = dy_ref[:, D_A + sb * LANE:D_A + (sb + 1) * LANE]
            dp_ref[:, OFF_ZB + sb * LANE:OFF_ZB + (sb + 1) * LANE] = (
                dyb * _unstack_heads(ost_ref, sb, lo) * (sig * (1.0 + zb * (1.0 - sig)))).astype(BF16)
            dq_r = _unstack_heads(dqst_ref, sb, lo)
            dp_ref[:, OFF_Q + sb * LANE:OFF_Q + (sb + 1) * LANE] = (
                dq_r * cos_t - _rot_half(dq_r * sin_t, first_half)).astype(BF16)

        lo2 = lax.broadcasted_iota(jnp.int32, (2 * BLK, LANE), 1) < HEAD
        for ks in range(2):
            cols = slice(ks * LANE, (ks + 1) * LANE)
            ka = dkdup_ref[2 * ks]
            kb = dkdup_ref[2 * ks + 1]
            dk_band = jnp.where(lo2, ka + pltpu.roll(ka, HEAD, 1), kb + pltpu.roll(kb, HEAD, 1))
            va_ = dvdup_ref[2 * ks]
            vb_ = dvdup_ref[2 * ks + 1]
            dv_band = jnp.where(lo2, va_ + pltpu.roll(va_, HEAD, 1), vb_ + pltpu.roll(vb_, HEAD, 1))
            dkr = dk_band[BLK:2 * BLK, :] + kcar_ref[:, cols]
            dp_ref[:, OFF_K + ks * LANE:OFF_K + (ks + 1) * LANE] = (
                dkr * cos_t - _rot_half(dkr * sin_t, first_half)).astype(BF16)
            dp_ref[:, OFF_V + ks * LANE:OFF_V + (ks + 1) * LANE] = (
                dv_band[BLK:2 * BLK, :] + vcar_ref[:, cols]).astype(BF16)
            kcar_ref[:, cols] = dk_band[0:BLK, :]
            vcar_ref[:, cols] = dv_band[0:BLK, :]

    tab = pl.BlockSpec((BLK, LANE), lambda i: (rev(i), 0))
    tabp = pl.BlockSpec((BLK, LANE), lambda i: (prev(i), 0))
    kvp = lambda col: pl.BlockSpec((BLK, D_KV), lambda i: (prev(i), col))
    vec = pl.BlockSpec((1, D_A), lambda i: (0, 0))
    w3 = pl.BlockSpec((GROUPS, BLK, BLK), lambda i: (0, 0, 0))
    return pl.pallas_call(
        body, name="mix_bwd", grid=(nb,),
        in_specs=_proj_specs(nb) + [
            kvp(OFF_K // D_KV), kvp(OFF_V // D_KV), pl.BlockSpec((BLK, 2 * D_A), lambda i: (rev(i), 0)),
            tab, tab, tabp, tabp, vec, vec, w3, w3, pl.BlockSpec((BLK, GROUPS), lambda i: (0, 0)),
            pl.BlockSpec(memory_space=pltpu.SMEM)],
        out_specs=[pl.BlockSpec((BLK, D_IN), lambda i: (rev(i), 0)), pl.BlockSpec((8, D_A), lambda i: (0, 0)), w3,
                   pl.BlockSpec((BLK, LANE), lambda i: (0, 0)), pl.BlockSpec((8, LANE), lambda i: (0, 0))],
        out_shape=[jax.ShapeDtypeStruct((s, D_IN), BF16), jax.ShapeDtypeStruct((8, D_A), F32),
                   jax.ShapeDtypeStruct((GROUPS, BLK, BLK), F32), jax.ShapeDtypeStruct((BLK, LANE), F32),
                   jax.ShapeDtypeStruct((8, LANE), F32)],
        scratch_shapes=[pltpu.VMEM((N_KV, 2 * BLK, LANE), BF16), pltpu.VMEM((N_KV, 2 * BLK, LANE), BF16),
                        pltpu.VMEM((BLK, D_A), F32), pltpu.VMEM((N_KV, Q_PER_KV * BLK, LANE), BF16),
                        pltpu.VMEM((N_KV, Q_PER_KV * BLK, LANE), F32), pltpu.VMEM((N_KV, Q_PER_KV * BLK, LANE), F32),
                        pltpu.VMEM((N_KV, Q_PER_KV * BLK, LANE), F32), pltpu.VMEM((N_KV, 2 * BLK, LANE), F32),
                        pltpu.VMEM((N_KV, 2 * BLK, LANE), F32), pltpu.VMEM((BLK, D_KV), F32), pltpu.VMEM((BLK, D_KV), F32)],
        compiler_params=_params("arbitrary"),
    )(proj, proj, proj, proj, proj, proj, proj, proj, proj, proj, dy, cos, sin, cos, sin, ln_g, ln_b, w_sp, w_sp_t,
      b_sp_t, sinks)


def _dh_call(dproj, w_bf, x, dx2, scale, norm_g):
    s = x.shape[0]
    tm = min(s, 512)
    tk = W_IN_SHARD
    nk = D_IN // tk

    def body(dp_ref, w_ref, x_ref, dx2_ref, sc_ref, g_ref, gx_ref, st_ref, acc_ref):
        i = pl.program_id(0)
        k = pl.program_id(1)

        @pl.when((i == 0) & (k == 0))
        def _():
            st_ref[...] = jnp.zeros((8, D), F32)

        @pl.when(k == 0)
        def _():
            acc_ref[...] = jnp.zeros((tm, D), F32)

        acc_ref[...] += lax.dot_general(dp_ref[...], w_ref[...], NT, preferred_element_type=F32)

        @pl.when(k == nk - 1)
        def _():
            g = g_ref[...]
            one_sc = 1.0 + sc_ref[...]

            def chunk(n, carry):
                rows = pl.ds(pl.multiple_of(n * ROW_CHUNK, ROW_CHUNK), ROW_CHUNK)
                dh = acc_ref[rows, :]
                xv = x_ref[rows, :]
                r = lax.rsqrt(jnp.mean(xv * xv, axis=-1, keepdims=True) + EPS)
                xn = xv * r
                dhn = dh * one_sc
                dxn = dhn * g
                gx_ref[rows, :] = dx2_ref[rows, :] + r * (dxn - xn * jnp.mean(dxn * xn, axis=-1, keepdims=True))
                st_ref[0:1, :] += jnp.sum(dh, axis=0, keepdims=True)
                st_ref[1:2, :] += jnp.sum(dh * (xn * g), axis=0, keepdims=True)
                st_ref[2:3, :] += jnp.sum(dhn * xn, axis=0, keepdims=True)
                return carry

            lax.fori_loop(0, tm // ROW_CHUNK, chunk, 0)

    vec = pl.BlockSpec((1, D), lambda i, k: (0, 0))
    rows = lambda: pl.BlockSpec((tm, D), lambda i, k: (i, 0))
    return pl.pallas_call(
        body, name="dh", grid=(s // tm, nk),
        in_specs=[pl.BlockSpec((tm, tk), lambda i, k: (i, k)), pl.BlockSpec((D, tk), lambda i, k: (0, k)), rows(), rows(), vec, vec],
        out_specs=[rows(), pl.BlockSpec((8, D), lambda i, k: (0, 0))],
        out_shape=[jax.ShapeDtypeStruct((s, D), F32), jax.ShapeDtypeStruct((8, D), F32)],
        scratch_shapes=[pltpu.VMEM((tm, D), F32)],
        compiler_params=_params("arbitrary", "arbitrary"),
    )(dproj, w_bf, x, dx2, scale, norm_g)


def _adam_math(w, g, m, v):
    m_new = ADAM_B1 * m + (1.0 - ADAM_B1) * g
    v_new = ADAM_B2 * v + (1.0 - ADAM_B2) * (g * g)
    m_hat = m_new / ADAM_C1
    v_hat = v_new / ADAM_C2
    delta = -ADAM_LR * (m_hat / (jnp.sqrt(v_hat) + ADAM_EPS) + ADAM_WD * w)
    return delta, m_new, v_new


def _adam_call(w, g, m, v, name):
    r, n = w.shape
    tr = r if r * n * 4 <= (1 << 20) else max(8, (1 << 20) // (n * 4) // 8 * 8)
    while r % tr:
        tr -= 8

    def body(w_ref, g_ref, m_ref, v_ref, d_ref, mo_ref, vo_ref):
        d_ref[...], mo_ref[...], vo_ref[...] = _adam_math(w_ref[...], g_ref[...], m_ref[...], v_ref[...])

    spec = lambda: pl.BlockSpec((tr, n), lambda i: (i, 0))
    return pl.pallas_call(
        body, name=name, grid=(r // tr,), in_specs=[spec() for _ in range(4)], out_specs=[spec() for _ in range(3)],
        out_shape=[jax.ShapeDtypeStruct((r, n), F32)] * 3, compiler_params=_params("parallel"),
    )(w, g, m, v)


def _adam_outer_call(w, ct, dm, m, v, name):
    r, n = w.shape
    tr = 128

    def body(w_ref, ct_ref, dm_ref, m_ref, v_ref, g_ref, d_ref, mo_ref, vo_ref):
        g = ct_ref[:, 0:1] * dm_ref[0:1, :]
        for b in range(1, N_DEV):
            g = g + ct_ref[:, b:b + 1] * dm_ref[b:b + 1, :]
        g_ref[...] = g
        d_ref[...], mo_ref[...], vo_ref[...] = _adam_math(w_ref[...], g, m_ref[...], v_ref[...])

    spec = lambda: pl.BlockSpec((tr, n), lambda i: (i, 0))
    return pl.pallas_call(
        body, name=name, grid=(r // tr,),
        in_specs=[spec(), pl.BlockSpec((tr, N_DEV), lambda i: (i, 0)), pl.BlockSpec((N_DEV, n), lambda i: (0, 0)), spec(), spec()],
        out_specs=[spec() for _ in range(4)],
        out_shape=[jax.ShapeDtypeStruct((r, n), F32)] * 4, compiler_params=_params("parallel"),
    )(w, ct, dm, m, v)


def _sum_pieces_call(pos, part, part_block, recvs, shard_shape, name):
    r, n = recvs[0].shape[1:]
    tr = min(r, 256)
    nrb = r // tr

    def body(pos_ref, p_ref, *refs):
        acc = p_ref[...].astype(F32)
        for r_ref in refs[:-1]:
            for d in range(r_ref.shape[0]):
                acc = acc + r_ref[d].astype(F32)
        refs[-1][...] = acc

    return pl.pallas_call(
        body, name=name,
        grid_spec=pltpu.PrefetchScalarGridSpec(
            num_scalar_prefetch=1, grid=(nrb,),
            in_specs=[pl.BlockSpec((tr, n), lambda i, pos: part_block(i, pos, nrb))] + [
                pl.BlockSpec((rv.shape[0], tr, n), lambda i, pos: (0, i, 0)) for rv in recvs],
            out_specs=pl.BlockSpec((tr, n), lambda i, pos: (pos[1] * nrb + i, 0))),
        out_shape=jax.ShapeDtypeStruct(shard_shape, F32), compiler_params=_params("parallel"),
    )(pos, part, *recvs)


def _coords():
    return lax.axis_index("x"), lax.axis_index("y"), lax.axis_index("c")


def _allgather_sum_call(blk, name, with_sum):
    m_per, n = blk.shape

    def body(x_ref, out_ref, *rest):
        if with_sum:
            sum_ref, send_sems, recv_sems, local_sem = rest
        else:
            send_sems, recv_sems, local_sem = rest
        x, y, c = _coords()
        me, sibling = (x, y, c), (x, y, 1 - c)
        chips = [(1 - x, y), (x, 1 - y), (1 - x, 1 - y)]

        def rows(px, py, pc):
            return out_ref.at[pl.ds((4 * px + 2 * py + pc) * m_per, m_per), :]

        def copy(k, block, to, src=None):
            return pltpu.make_async_remote_copy(
                src_ref=rows(*block) if src is None else src, dst_ref=rows(*block),
                send_sem=send_sems.at[k], recv_sem=recv_sems.at[k], device_id=to, device_id_type=MESH)

        mine = pltpu.make_async_copy(x_ref, rows(*me), local_sem)
        mine.start()
        first = [copy(0, me, sibling, src=x_ref)]
        first += [copy(1 + j, me, (*chip, c), src=x_ref) for j, chip in enumerate(chips)]
        for cp in first:
            cp.start()
        passed = [copy(4 + j, (*chip, c), sibling) for j, chip in enumerate(chips)]
        for j, chip in enumerate(chips):
            copy(1 + j, (*chip, c), me).wait_recv()
            passed[j].start()
        copy(0, sibling, me).wait_recv()
        for j, chip in enumerate(chips):
            copy(4 + j, (*chip, 1 - c), me).wait_recv()
        for cp in first + passed:
            cp.wait_send()
        mine.wait()
        if with_sum:
            acc = out_ref[0:m_per, :]
            for d in range(1, N_DEV):
                acc = acc + out_ref[d * m_per:(d + 1) * m_per, :]
            sum_ref[...] = acc

    vm = pl.BlockSpec(memory_space=pltpu.VMEM)
    out_shape = [jax.ShapeDtypeStruct((N_DEV * m_per, n), F32)]
    if with_sum:
        out_shape.append(jax.ShapeDtypeStruct((m_per, n), F32))
    return pl.pallas_call(
        body, name=name, out_shape=out_shape, in_specs=[vm], out_specs=[vm] * len(out_shape),
        scratch_shapes=[pltpu.SemaphoreType.DMA((7,)), pltpu.SemaphoreType.DMA((7,)), pltpu.SemaphoreType.DMA],
        compiler_params=pltpu.CompilerParams(vmem_limit_bytes=VMEM_LIMIT),
    )(blk)


def _weights_gather_call(wi_full, wo_full):
    hi = D // 2
    ho = W_OUT_SHARD // 2

    def body(wi_in, wo_in, fi_ref, fo_ref, send_sems, recv_sems):
        del wi_in, wo_in
        x, y, c = _coords()
        sibling = (x, y, 1 - c)
        chips = [(1 - x, y), (x, 1 - y), (1 - x, 1 - y)]

        def half(which, px, py, pc):
            j = 2 * px + py
            if which == 0:
                return fi_ref.at[pl.ds(pc * hi, hi), pl.ds(j * W_IN_SHARD, W_IN_SHARD)]
            return fo_ref.at[pl.ds(j * W_OUT_SHARD + pc * ho, ho), :]

        def copy(k, which, block, to):
            return pltpu.make_async_remote_copy(
                src_ref=half(which, *block), dst_ref=half(which, *block), send_sem=send_sems.at[k],
                recv_sem=recv_sems.at[k], device_id=to, device_id_type=MESH)

        first = [copy(6 * w + j, w, (x, y, c), (*chip, c)) for w in range(2) for j, chip in enumerate(chips)]
        for cp in first:
            cp.start()
        passed = []
        for w in range(2):
            for j, chip in enumerate(chips):
                copy(6 * w + j, w, (*chip, c), (x, y, c)).wait_recv()
                cp = copy(6 * w + 3 + j, w, (*chip, c), sibling)
                cp.start()
                passed.append(cp)
        for w in range(2):
            for j, chip in enumerate(chips):
                copy(6 * w + 3 + j, w, (*chip, 1 - c), (x, y, c)).wait_recv()
        for cp in first + passed:
            cp.wait_send()

    anyspec = pl.BlockSpec(memory_space=pl.ANY)
    return pl.pallas_call(
        body, name="weights_gather",
        out_shape=[jax.ShapeDtypeStruct((D, D_IN), BF16), jax.ShapeDtypeStruct((D, D), BF16)],
        in_specs=[anyspec, anyspec], out_specs=[anyspec, anyspec], input_output_aliases={0: 0, 1: 1},
        scratch_shapes=[pltpu.SemaphoreType.DMA((12,)), pltpu.SemaphoreType.DMA((12,))],
    )(wi_full, wo_full)


HBM_SPEC = pl.BlockSpec(memory_space=pltpu.HBM)
SEM_SPEC = pl.BlockSpec(memory_space=pltpu.SEMAPHORE)
SIDE_EFFECT = pltpu.SideEffectType.DATAFLOW_SIDE_EFFECTING


def _peer(x, y, c, q, cb):
    return (1 - x if q & 2 else x, 1 - y if q & 1 else y, 1 - c if cb else c)


def _w_in_piece(slots):
    def piece(part_ref, k, to):
        return part_ref.at[pl.ds(to[2] * (D // 2), D // 2), pl.ds(slots[k] * W_IN_SHARD, W_IN_SHARD)]
    return piece


def _w_out_piece(part_ref, k, to):
    ho = W_OUT_SHARD // 2
    return part_ref.at[pl.ds((2 * to[0] + to[1]) * W_OUT_SHARD + to[2] * ho, ho), :]


def _exchange_start_call(part, rels, piece, slot_shape, name):
    n = len(rels)
    land = lax.empty((n,) + slot_shape, BF16)

    def body(part_ref, land_ref, send_sems, recv_sems, part_thru, land_thru, token):
        x, y, c = _coords()
        for k, (q, cb) in enumerate(rels):
            to = _peer(x, y, c, q, cb)
            pltpu.make_async_remote_copy(src_ref=piece(part_ref, k, to), dst_ref=land_ref.at[k], send_sem=send_sems.at[k],
                                         recv_sem=recv_sems.at[k], device_id=to, device_id_type=MESH).start()
        token[...] = jnp.zeros_like(token)

    return pl.pallas_call(
        body, name=name,
        out_shape=(pltpu.SemaphoreType.DMA((n,)), pltpu.SemaphoreType.DMA((n,)), pltpu.HBM(part.shape, part.dtype),
                   pltpu.HBM(land.shape, land.dtype), jax.ShapeDtypeStruct((8, LANE), F32)),
        in_specs=(HBM_SPEC, HBM_SPEC), out_specs=(SEM_SPEC, SEM_SPEC, HBM_SPEC, HBM_SPEC, pl.BlockSpec(memory_space=pltpu.VMEM)),
        input_output_aliases={0: 2, 1: 3},
        compiler_params=pltpu.CompilerParams(has_side_effects=SIDE_EFFECT),
    )(pltpu.with_memory_space_constraint(part, pltpu.HBM), pltpu.with_memory_space_constraint(land, pltpu.HBM))


def _exchange_wait_call(started, rels, piece, after, name):
    send_sems, recv_sems, part_thru, land_thru, _ = started

    def body(part_ref, land_ref, send_sems, recv_sems, after_ref, part_out, land_out):
        x, y, c = _coords()
        for k, (q, cb) in enumerate(rels):
            to = _peer(x, y, c, q, cb)
            cp = pltpu.make_async_remote_copy(src_ref=piece(part_ref, k, to), dst_ref=land_ref.at[k], send_sem=send_sems.at[k],
                                              recv_sem=recv_sems.at[k], device_id=to, device_id_type=MESH)
            cp.wait_send()
            cp.wait_recv()

    return pl.pallas_call(
        body, name=name,
        out_shape=(pltpu.HBM(part_thru.shape, part_thru.dtype), pltpu.HBM(land_thru.shape, land_thru.dtype)),
        in_specs=(HBM_SPEC, HBM_SPEC, SEM_SPEC, SEM_SPEC, pl.BlockSpec(memory_space=pl.ANY)), out_specs=(HBM_SPEC, HBM_SPEC),
        input_output_aliases={0: 0, 1: 1},
        compiler_params=pltpu.CompilerParams(has_side_effects=SIDE_EFFECT),
    )(part_thru, land_thru, send_sems, recv_sems, after)


def _pair_exchange_call(gi, go):
    hi = D // 2
    ho = W_OUT_SHARD // 2

    def body(gi_in, go_in, fi_ref, fo_ref, send_sems, recv_sems):
        del gi_in, go_in
        x, y, c = _coords()
        sibling = (x, y, 1 - c)
        mine = (fi_ref.at[pl.ds(c * hi, hi), :], fo_ref.at[pl.ds(c * ho, ho), :])
        theirs = (fi_ref.at[pl.ds((1 - c) * hi, hi), :], fo_ref.at[pl.ds((1 - c) * ho, ho), :])
        sends = [pltpu.make_async_remote_copy(src_ref=ref, dst_ref=ref, send_sem=send_sems.at[k], recv_sem=recv_sems.at[k],
                                              device_id=sibling, device_id_type=MESH) for k, ref in enumerate(mine)]
        for cp in sends:
            cp.start()
        for k, ref in enumerate(theirs):
            pltpu.make_async_remote_copy(src_ref=ref, dst_ref=ref, send_sem=send_sems.at[k], recv_sem=recv_sems.at[k],
                                         device_id=sibling, device_id_type=MESH).wait_recv()
        for cp in sends:
            cp.wait_send()

    anyspec = pl.BlockSpec(memory_space=pl.ANY)
    return pl.pallas_call(
        body, name="pair_exchange",
        out_shape=[jax.ShapeDtypeStruct((D, W_IN_SHARD), F32), jax.ShapeDtypeStruct((W_OUT_SHARD, D), F32)],
        in_specs=[anyspec, anyspec], out_specs=[anyspec, anyspec], input_output_aliases={0: 0, 1: 1},
        scratch_shapes=[pltpu.SemaphoreType.DMA((2,)), pltpu.SemaphoreType.DMA((2,))],
    )(gi, go)


def _rope_tables(s):
    inv_freq = 10000.0 ** (-jnp.arange(0, HEAD, 2, dtype=F32) / HEAD)
    ang = jnp.arange(s, dtype=F32)[:, None] * inv_freq[None, :]
    return jnp.tile(jnp.cos(ang), (1, LANE // (HEAD // 2))), jnp.tile(jnp.sin(ang), (1, LANE // (HEAD // 2)))


def _pad_cols(a, n):
    return jnp.pad(a, ((0, 0), (0, n - a.shape[1])))


def kernel(x, c, w_ada, b_ada, norm_g, w_in, ln_v_g, ln_v_b, w_spatial, b_spatial, sinks, w_out, w_ada_final, b_ada_final, final_norm_g, loss_target, m_w_ada, m_b_ada, m_norm_g, m_w_in, m_ln_v_g, m_ln_v_b, m_w_spatial, m_b_spatial, m_sinks, m_w_out, m_w_ada_final, m_b_ada_final, m_final_norm_g, v_w_ada, v_b_ada, v_norm_g, v_w_in, v_ln_v_g, v_ln_v_b, v_w_spatial, v_b_spatial, v_sinks, v_w_out, v_w_ada_final, v_b_ada_final, v_final_norm_g):
    s = x.shape[1]
    ax, ay, ac = _coords()
    chip = 2 * ax + ay
    me = 4 * ax + 2 * ay + ac
    n_ada = w_ada.shape[2]
    n_adaf = w_ada_final.shape[1]

    x2d = x.reshape(s, D)
    tgt = loss_target.reshape(s, D)
    w_ada2, w_in2, w_out2 = w_ada[0], w_in[0], w_out[0]
    b_ada_f2 = b_ada_final.reshape(1, 2 * D)
    gf = final_norm_g.reshape(1, D)

    c_all = _allgather_sum_call(jnp.pad(c, ((0, 7), (0, 0))), "gather_c", False)[0][::8]
    mod_p, c_act = _rowmat_call(c_all, w_ada2, lax.dynamic_slice(b_ada, (0, chip * n_ada), (1, n_ada)), "mod")
    modf_p, _ = _rowmat_call(c_all, w_ada_final, lax.dynamic_slice(b_ada_f2, (0, chip * n_adaf), (1, n_adaf)), "mod_final")
    mods = _allgather_sum_call(jnp.concatenate([mod_p, modf_p], axis=1), "gather_mod", False)[0]
    my_rows = [lax.dynamic_slice(mods, (16 * j + me, 0), (1, n_ada + n_adaf)) for j in range(N_CHIP)]
    mod = jnp.concatenate([r[:, :n_ada] for r in my_rows], axis=1)
    mod_f = jnp.concatenate([r[:, n_ada:] for r in my_rows], axis=1)
    shift, scale, gate = mod[:, :D], mod[:, D:2 * D], mod[:, 2 * D:]
    shift_f, scale_f = mod_f[:, :D], mod_f[:, D:]

    pos = jnp.stack([chip, ac]).astype(jnp.int32)
    w_in_own = _cast_into_call(pos, w_in2, (D, D_IN), "cast_w_in")
    w_out_own = _cast_into_call(pos, w_out2, (D, D), "cast_w_out")

    cos, sin = _rope_tables(s)
    b_sp_t = b_spatial[0].T
    sinks1 = sinks.reshape(N_Q)
    proj, h, w_in_bf, w_out_bf = _proj_gather_call(pos, x2d, shift, scale, norm_g, w_in_own, w_out_own)
    y = _mix_fwd_call(proj, cos, sin, ln_v_g, ln_v_b, w_spatial[0], b_sp_t, sinks1)
    dx2, do, st_tail = _tail_call(y, w_out_bf, x2d, tgt, gate, shift_f, scale_f, gf)

    rel_o = [(0, 1), (1, 0), (1, 1), (2, 0), (2, 1), (3, 0), (3, 1)]
    rel_a = [(1, 0), (1, 1), (2, 0), (2, 1)]
    rel_b = [(3, 0), (3, 1), (0, 1)]
    piece_a, piece_b = _w_in_piece([0, 0, 1, 1]), _w_in_piece([0, 0, 1])
    half_in, half_out = (D // 2, W_IN_SHARD), (W_OUT_SHARD // 2, D)

    g_w_out_p = _tn_call(y, do, "grad_w_out")
    st_o = _exchange_start_call(g_w_out_p, rel_o, _w_out_piece, half_out, "send_w_out")
    dy = _dy_call(do, w_out_bf)
    dproj, st_ln, d_wsp, d_bsp_t, d_sink = _mix_bwd_call(
        proj, dy, cos, sin, ln_v_g + st_o[4][0:1, 0:1], ln_v_b, w_spatial[0], jnp.swapaxes(w_spatial[0], 1, 2), b_sp_t, sinks1)
    g_w_in_a = _tn_shards_call(pos, h, dproj, (1, 2), "grad_w_in_a")
    st_a = _exchange_start_call(g_w_in_a, rel_a, piece_a, half_in, "send_w_in_a")
    g_w_in_b = _tn_shards_call(pos, h, dproj, (3, 0), "grad_w_in_b")
    st_b = _exchange_start_call(g_w_in_b, rel_b, piece_b, half_in, "send_w_in_b")
    grad_x, st_dh = _dh_call(dproj, w_in_bf, x2d, dx2, scale + (st_a[4][0:1, 0:1] + st_b[4][0:1, 0:1]), norm_g)

    g_w_out_p, recv_o = _exchange_wait_call(st_o, rel_o, _w_out_piece, st_dh, "wait_w_out")
    _, recv_a = _exchange_wait_call(st_a, rel_a, piece_a, st_dh, "wait_w_in_a")
    g_w_in_b, recv_b = _exchange_wait_call(st_b, rel_b, piece_b, st_dh, "wait_w_in_b")
    g_w_in, g_w_out = _pair_exchange_call(
        _sum_pieces_call(pos, g_w_in_b, lambda i, p, nrb: (p[1] * nrb + i, 1), [recv_a, recv_b], (D, W_IN_SHARD), "sum_w_in"),
        _sum_pieces_call(pos, g_w_out_p, lambda i, p, nrb: ((2 * p[0] + p[1]) * nrb + i, 0), [recv_o], (W_OUT_SHARD, D), "sum_w_out"))

    pack = jnp.concatenate([
        d_wsp.reshape(64, D), st_tail, st_dh, _pad_cols(st_ln, D),
        _pad_cols(d_bsp_t[:, :GROUPS].T, D), _pad_cols(d_sink, D)], axis=0)
    rows = pack.shape[0]
    packs, tot = _allgather_sum_call(pack, "gather_small", True)
    packs = packs.reshape(N_DEV, rows, D)
    dmod_all = jnp.concatenate([packs[:, 72, :], packs[:, 73, :], packs[:, 67, :]], axis=1)
    dmodf_all = jnp.concatenate([packs[:, 64, :], packs[:, 65, :]], axis=1)
    loss = tot[69, 0]
    grads = {
        "b_ada": jnp.concatenate([tot[72:73], tot[73:74], tot[67:68]], axis=1),
        "norm_g": tot[74:75],
        "ln_v_g": tot[80:81, :D_A],
        "ln_v_b": tot[81:82, :D_A],
        "w_spatial": tot[0:64].reshape(GROUPS * BLK, BLK),
        "b_spatial": tot[88:96, :BLK],
        "sinks": tot[96:97, :N_Q],
        "b_ada_final": jnp.concatenate([tot[64:65], tot[65:66]], axis=1),
        "final_norm_g": tot[66:67],
        "w_in": g_w_in,
        "w_out": g_w_out,
    }

    weights = dict(w_ada=w_ada, b_ada=b_ada, norm_g=norm_g, w_in=w_in, ln_v_g=ln_v_g, ln_v_b=ln_v_b, w_spatial=w_spatial,
                   b_spatial=b_spatial, sinks=sinks, w_out=w_out, w_ada_final=w_ada_final, b_ada_final=b_ada_final,
                   final_norm_g=final_norm_g)
    m_in = dict(w_ada=m_w_ada, b_ada=m_b_ada, norm_g=m_norm_g, w_in=m_w_in, ln_v_g=m_ln_v_g, ln_v_b=m_ln_v_b,
                w_spatial=m_w_spatial, b_spatial=m_b_spatial, sinks=m_sinks, w_out=m_w_out, w_ada_final=m_w_ada_final,
                b_ada_final=m_b_ada_final, final_norm_g=m_final_norm_g)
    v_in = dict(w_ada=v_w_ada, b_ada=v_b_ada, norm_g=v_norm_g, w_in=v_w_in, ln_v_g=v_ln_v_g, ln_v_b=v_ln_v_b,
                w_spatial=v_w_spatial, b_spatial=v_b_spatial, sinks=v_sinks, w_out=v_w_out, w_ada_final=v_w_ada_final,
                b_ada_final=v_b_ada_final, final_norm_g=v_final_norm_g)
    c_act_t = c_act.T
    outer = {"w_ada": lax.dynamic_slice(dmod_all, (0, chip * n_ada), (N_DEV, n_ada)),
             "w_ada_final": lax.dynamic_slice(dmodf_all, (0, chip * n_adaf), (N_DEV, n_adaf))}
    out_g, out_d, out_m, out_v = [], [], [], []
    for name, w in weights.items():
        shape = w.shape
        if name in outer:
            shape2 = (D, outer[name].shape[1])
            g, dl, mn, vn = _adam_outer_call(w.reshape(shape2), c_act_t, outer[name], m_in[name].reshape(shape2),
                                             v_in[name].reshape(shape2), "adam_" + name)
        else:
            g = grads[name]
            shape2 = g.shape
            dl, mn, vn = _adam_call(w.reshape(shape2), g, m_in[name].reshape(shape2), v_in[name].reshape(shape2), "adam_" + name)
        out_g.append(g.reshape(shape))
        out_d.append(dl.reshape(shape))
        out_m.append(mn.reshape(shape))
        out_v.append(vn.reshape(shape))
    return (loss, grad_x.reshape(x.shape), *out_g, *out_d, *out_m, *out_v)
```

```python
import jax
import jax.numpy as jnp
from jax import lax
from jax.experimental import pallas as pl
from jax.experimental.pallas import tpu as pltpu

F32 = jnp.float32
BF16 = jnp.bfloat16
MESH = pl.DeviceIdType.MESH

D = 2048
D_A = 1024
D_B = 1024
D_KV = 256
HEAD = 64
N_Q = 16
N_KV = 4
Q_PER_KV = N_Q // N_KV
BLK = 128
GROUPS = 8
D_IN = 5632
OFF_Q, OFF_K, OFF_V, OFF_ZB = 3072, 4096, 4352, 4608
N_CHIP = 4
N_DEV = 8
W_IN_SHARD = D_IN // N_CHIP
W_OUT_SHARD = D // N_CHIP
EPS = 1e-5
SCALE = HEAD ** -0.5
NEG = -1e30
LANE = 128
VMEM_LIMIT = 56 * 1024 * 1024

ADAM_LR, ADAM_B1, ADAM_B2, ADAM_EPS, ADAM_WD, ADAM_STEP = 0.001, 0.9, 0.999, 1e-08, 0.01, 10
ADAM_C1 = 1.0 - ADAM_B1 ** ADAM_STEP
ADAM_C2 = 1.0 - ADAM_B2 ** ADAM_STEP

NT = (((1,), (1,)), ((), ()))
TN = (((0,), (0,)), ((), ()))


def _params(*sem):
    return pltpu.CompilerParams(dimension_semantics=sem, vmem_limit_bytes=VMEM_LIMIT)


def _silu_parts(z):
    sig = 1.0 / (1.0 + jnp.exp(-z))
    return z * sig, sig


def _rot_half(v, first_half):
    return jnp.where(first_half, -pltpu.roll(v, 96, 1), pltpu.roll(v, 32, 1))


def _lane_masks():
    lane = lax.broadcasted_iota(jnp.int32, (BLK, LANE), 1)
    return (lane % HEAD) < (HEAD // 2), lane < HEAD


def _band_valid(first_block_bound, rows=BLK):
    rr = lax.broadcasted_iota(jnp.int32, (rows, 2 * BLK), 0) & (BLK - 1)
    jj = lax.broadcasted_iota(jnp.int32, (rows, 2 * BLK), 1)
    return (jj > rr) & (jj <= rr + BLK) & (jj >= first_block_bound)


def _dup_kv(slab, lo):
    rolled = pltpu.roll(slab, HEAD, 1)
    return jnp.where(lo, slab, rolled).astype(BF16), jnp.where(lo, rolled, slab).astype(BF16)


def _stack_heads(ref, sb, slab, lo, dtype):
    kh, base = sb // 2, 2 * (sb % 2) * BLK
    zero = jnp.zeros_like(slab)
    ref[kh, base:base + BLK, :] = jnp.where(lo, slab, zero).astype(dtype)
    ref[kh, base + BLK:base + 2 * BLK, :] = jnp.where(lo, zero, slab).astype(dtype)


def _unstack_heads(ref, sb, lo):
    kh, base = sb // 2, 2 * (sb % 2) * BLK
    return jnp.where(lo, ref[kh, base:base + BLK, :], ref[kh, base + BLK:base + 2 * BLK, :])


def _sink_column(sinks_ref, kh):
    row = lax.broadcasted_iota(jnp.int32, (Q_PER_KV * BLK, 1), 0)
    col = jnp.full(row.shape, sinks_ref[Q_PER_KV * kh + Q_PER_KV - 1], F32)
    for n in range(Q_PER_KV - 2, -1, -1):
        col = jnp.where(row < (n + 1) * BLK, sinks_ref[Q_PER_KV * kh + n], col)
    return col


def _tril():
    t = lax.broadcasted_iota(jnp.int32, (BLK, BLK), 0)
    s = lax.broadcasted_iota(jnp.int32, (BLK, BLK), 1)
    return s <= t


def _layer_norm_fwd(va, lg, lb):
    mu = jnp.mean(va, axis=-1, keepdims=True)
    xc = va - mu
    rstd = lax.rsqrt(jnp.mean(xc * xc, axis=-1, keepdims=True) + EPS)
    vhat = xc * rstd
    return vhat, rstd, vhat * lg + lb


def _softmax_sink(qm, kexp, valid, sink):
    s = lax.dot_general(qm, kexp, NT, preferred_element_type=F32) * SCALE
    s = jnp.where(valid, s, NEG)
    m = jnp.maximum(jnp.max(s, axis=-1, keepdims=True), sink)
    p = jnp.exp(s - m)
    esink = jnp.exp(sink - m)
    den = jnp.sum(p, axis=-1, keepdims=True) + esink
    return p / den, esink / den


def _rowmat_call(c_all, w, b, name):
    n = w.shape[1]
    tn = 512

    def body(c_ref, w_ref, b_ref, o_ref, ca_ref):
        ca, _ = _silu_parts(c_ref[...])
        ca_ref[...] = ca
        o_ref[...] = jnp.dot(ca.astype(BF16), w_ref[...].astype(BF16), preferred_element_type=F32) + b_ref[...]

    return pl.pallas_call(
        body, name=name, grid=(n // tn,),
        in_specs=[pl.BlockSpec((N_DEV, D), lambda j: (0, 0)), pl.BlockSpec((D, tn), lambda j: (0, j)),
                  pl.BlockSpec((1, tn), lambda j: (0, j))],
        out_specs=[pl.BlockSpec((N_DEV, tn), lambda j: (0, j)), pl.BlockSpec((N_DEV, D), lambda j: (0, 0))],
        out_shape=[jax.ShapeDtypeStruct((N_DEV, n), F32), jax.ShapeDtypeStruct((N_DEV, D), F32)],
        compiler_params=_params("arbitrary"),
    )(c_all, w, b)


def _cast_into_call(pos, w, full_shape, name):
    r, n = w.shape
    tr = min(r, 512)
    by_cols = full_shape[0] == r
    nrb = r // tr

    def body(pos_ref, w_ref, o_ref):
        o_ref[...] = w_ref[...].astype(BF16)

    out_map = (lambda i, pos: (i, pos[0])) if by_cols else (lambda i, pos: (pos[0] * nrb + i, 0))
    return pl.pallas_call(
        body, name=name,
        grid_spec=pltpu.PrefetchScalarGridSpec(
            num_scalar_prefetch=1, grid=(nrb,),
            in_specs=[pl.BlockSpec((tr, n), lambda i, pos: (i, 0))], out_specs=pl.BlockSpec((tr, n), out_map)),
        out_shape=jax.ShapeDtypeStruct(full_shape, BF16), compiler_params=_params("parallel"),
    )(pos, w)


def _proj_call(x, shift, scale, norm_g, w_bf):
    s = x.shape[0]
    tm = min(s, 1024)
    tn = 512

    def body(x_ref, sh_ref, sc_ref, g_ref, w_ref, proj_ref, h_ref):
        @pl.when(pl.program_id(1) == 0)
        def _():
            xv = x_ref[...]
            r = lax.rsqrt(jnp.mean(xv * xv, axis=-1, keepdims=True) + EPS)
            h_ref[...] = ((xv * r * g_ref[...]) * (1.0 + sc_ref[...]) + sh_ref[...]).astype(BF16)

        proj_ref[...] = jnp.dot(h_ref[...], w_ref[...], preferred_element_type=F32)

    vec = pl.BlockSpec((1, D), lambda i, j: (0, 0))
    return pl.pallas_call(
        body, name="proj", grid=(s // tm, D_IN // tn),
        in_specs=[pl.BlockSpec((tm, D), lambda i, j: (i, 0)), vec, vec, vec, pl.BlockSpec((D, tn), lambda i, j: (0, j))],
        out_specs=[pl.BlockSpec((tm, tn), lambda i, j: (i, j)), pl.BlockSpec((tm, D), lambda i, j: (i, 0))],
        out_shape=[jax.ShapeDtypeStruct((s, D_IN), F32), jax.ShapeDtypeStruct((s, D), BF16)],
        compiler_params=_params("parallel", "arbitrary"),
    )(x, shift, scale, norm_g, w_bf)


def _proj_gather_call(pos, x, shift, scale, norm_g, wi_full, wo_full):
    s = x.shape[0]
    tm = min(s, 512)
    nrow = s // tm
    hi = D // 2
    ho = W_OUT_SHARD // 2

    def body(pos_ref, x_ref, sh_ref, sc_ref, g_ref, wi_in, wo_in, proj_ref, h_ref, fi_ref, fo_ref,
             h_all, wbuf, send_sems, recv_sems, load_sem):
        del wi_in, wo_in
        p = pl.program_id(0)
        i = pl.program_id(1)
        x_, y_, c_ = _coords()
        me, sibling = (x_, y_, c_), (x_, y_, 1 - c_)

        def shard_of(q):
            px, py, _ = _peer(x_, y_, c_, q, 0)
            return 2 * px + py

        def part(which, q, pc, sub=None):
            n = hi if which == 0 else ho
            base = pc * n
            if sub is not None:
                n //= 2
                base = base + sub * n
            if which == 0:
                return fi_ref.at[pl.ds(base, n), pl.ds(shard_of(q) * W_IN_SHARD, W_IN_SHARD)]
            return fo_ref.at[pl.ds(shard_of(q) * W_OUT_SHARD + base, n), :]

        def copy(k, which, q, pc, to, sub=None):
            ref = part(which, q, pc, sub)
            return pltpu.make_async_remote_copy(src_ref=ref, dst_ref=ref, send_sem=send_sems.at[k], recv_sem=recv_sems.at[k],
                                                device_id=to, device_id_type=MESH)

        def to_neighbour(which, q):
            return copy(8 * which + q - 1, which, 0, c_, _peer(x_, y_, c_, q, 0))

        def from_neighbour(which, q):
            return copy(8 * which + q - 1, which, q, c_, me)

        def relay(which, q):
            return copy(8 * which + 2 + q - 1, which, q, c_, _peer(x_, y_, c_, 3 - q, 0), q - 1)

        def relayed(which, sub):
            return copy(8 * which + 2 + sub, which, 3, c_, me, sub)

        def to_sibling(which, q):
            return copy(8 * which + 4 + q - 1, which, q, c_, sibling)

        def from_sibling(which, q):
            return copy(8 * which + 4 + q - 1, which, q, 1 - c_, me)

        def relayed_to_sibling(which, sub):
            return copy(8 * which + 6 + sub, which, 3, c_, sibling, sub)

        def relayed_from_sibling(which, sub):
            return copy(8 * which + 6 + sub, which, 3, 1 - c_, me, sub)

        def pass_on_neighbours(which):
            for q in (1, 2):
                from_neighbour(which, q).wait_recv()
                to_sibling(which, q).start()
                relay(which, q).start()

        def pass_on_relayed(which):
            for sub in range(2):
                relayed(which, sub).wait_recv()
                relayed_to_sibling(which, sub).start()

        def load_shard(q):
            cp = pltpu.make_async_copy(fi_ref.at[:, pl.ds(shard_of(q) * W_IN_SHARD, W_IN_SHARD)], wbuf, load_sem)
            cp.start()
            cp.wait()

        @pl.when((p == 0) & (i == 0))
        def _():
            for q in (1, 2):
                to_neighbour(0, q).start()
            load_shard(0)

        @pl.when((p == 1) & (i == 0))
        def _():
            pass_on_neighbours(0)
            for q in (1, 2):
                to_neighbour(1, q).start()
            from_sibling(0, 1).wait_recv()
            load_shard(1)

        @pl.when((p == 2) & (i == 0))
        def _():
            from_sibling(0, 2).wait_recv()
            load_shard(2)

        @pl.when((p == 3) & (i == 0))
        def _():
            pass_on_relayed(0)
            pass_on_neighbours(1)
            for sub in range(2):
                relayed_from_sibling(0, sub).wait_recv()
            load_shard(3)

        rows = pl.ds(pl.multiple_of(i * tm, tm), tm)

        @pl.when(p == 0)
        def _():
            xv = x_ref[...]
            r = lax.rsqrt(jnp.mean(xv * xv, axis=-1, keepdims=True) + EPS)
            hv = ((xv * r * g_ref[...]) * (1.0 + sc_ref[...]) + sh_ref[...]).astype(BF16)
            h_ref[...] = hv
            h_all[rows, :] = hv

        proj_ref[...] = jnp.dot(h_all[rows, :], wbuf[...], preferred_element_type=F32)

        @pl.when((p == N_CHIP - 1) & (i == nrow - 1))
        def _():
            pass_on_relayed(1)
            for q in (1, 2):
                from_sibling(1, q).wait_recv()
            for sub in range(2):
                relayed_from_sibling(1, sub).wait_recv()
            for which in range(2):
                for q in (1, 2):
                    to_neighbour(which, q).wait_send()
                    relay(which, q).wait_send()
                    to_sibling(which, q).wait_send()
                    relayed_to_sibling(which, q - 1).wait_send()

    vec = pl.BlockSpec((1, D), lambda p, i, pos: (0, 0))
    first_phase_rows = lambda p, i, pos: (jnp.where(p == 0, i, nrow - 1), 0)
    anyspec = pl.BlockSpec(memory_space=pl.ANY)
    return pl.pallas_call(
        body, name="proj_gather",
        grid_spec=pltpu.PrefetchScalarGridSpec(
            num_scalar_prefetch=1, grid=(N_CHIP, nrow),
            in_specs=[pl.BlockSpec((tm, D), first_phase_rows), vec, vec, vec, anyspec, anyspec],
            out_specs=[pl.BlockSpec((tm, W_IN_SHARD), lambda p, i, pos: (i, jnp.bitwise_xor(pos[0], p))),
                       pl.BlockSpec((tm, D), first_phase_rows), anyspec, anyspec],
            scratch_shapes=[pltpu.VMEM((s, D), BF16), pltpu.VMEM((D, W_IN_SHARD), BF16),
                            pltpu.SemaphoreType.DMA((16,)), pltpu.SemaphoreType.DMA((16,)), pltpu.SemaphoreType.DMA]),
        out_shape=[jax.ShapeDtypeStruct((s, D_IN), F32), jax.ShapeDtypeStruct((s, D), BF16),
                   jax.ShapeDtypeStruct((D, D_IN), BF16), jax.ShapeDtypeStruct((D, D), BF16)],
        input_output_aliases={5: 2, 6: 3},
        compiler_params=_params("arbitrary", "arbitrary"),
    )(pos, x, shift, scale, norm_g, wi_full, wo_full)


def _proj_specs(rev_nb=None):
    if rev_nb is None:
        row = lambda i: i
    else:
        row = lambda i: rev_nb - 1 - i
    wide = lambda col: pl.BlockSpec((BLK, D_A), lambda i: (row(i), col))
    kv = lambda col: pl.BlockSpec((BLK, D_KV), lambda i: (row(i), col))
    half = lambda col: pl.BlockSpec((BLK, 512), lambda i: (row(i), col))
    return [wide(0), wide(1), wide(2), wide(3), kv(OFF_K // D_KV), kv(OFF_V // D_KV), half(OFF_ZB // 512), half(OFF_ZB // 512 + 1)]


def _mix_fwd_call(proj, cos, sin, ln_g, ln_b, w_sp, b_sp_t, sinks):
    s = proj.shape[0]
    nb = s // BLK

    def body(ua_ref, va_ref, za_ref, q_ref, k_ref, v_ref, zb0_ref, zb1_ref, cos_ref, sin_ref, lg_ref, lb_ref,
             w_ref, bt_ref, sinks_ref, y_ref, kdup_ref, vdup_ref, qm_ref, ost_ref):
        i = pl.program_id(0)
        first_half, lo = _lane_masks()
        cos_t = cos_ref[...]
        sin_t = sin_ref[...]

        _, _, vln = _layer_norm_fwd(va_ref[...], lg_ref[...], lb_ref[...])
        tril = _tril()
        for g in range(GROUPS):
            cols = slice(g * BLK, (g + 1) * BLK)
            wg = jnp.where(tril, w_ref[g], 0.0).astype(BF16)
            sg = jnp.dot(wg, vln[:, cols].astype(BF16), preferred_element_type=F32) + bt_ref[:, g:g + 1]
            gate, _ = _silu_parts(za_ref[:, cols])
            y_ref[:, cols] = (ua_ref[:, cols] * sg * gate).astype(BF16)

        @pl.when(i == 0)
        def _():
            kdup_ref[:, 0:BLK, :] = jnp.zeros((N_KV, BLK, LANE), BF16)
            vdup_ref[:, 0:BLK, :] = jnp.zeros((N_KV, BLK, LANE), BF16)

        @pl.when(i > 0)
        def _():
            kdup_ref[:, 0:BLK, :] = kdup_ref[:, BLK:2 * BLK, :]
            vdup_ref[:, 0:BLK, :] = vdup_ref[:, BLK:2 * BLK, :]

        for ks in range(2):
            cols = slice(ks * LANE, (ks + 1) * LANE)
            kslab = k_ref[:, cols]
            kr = kslab * cos_t + _rot_half(kslab, first_half) * sin_t
            for n, (kd, vd) in enumerate(zip(_dup_kv(kr, lo), _dup_kv(v_ref[:, cols], lo))):
                kdup_ref[2 * ks + n, BLK:2 * BLK, :] = kd
                vdup_ref[2 * ks + n, BLK:2 * BLK, :] = vd
        for sb in range(8):
            qslab = q_ref[:, sb * LANE:(sb + 1) * LANE]
            _stack_heads(qm_ref, sb, qslab * cos_t + _rot_half(qslab, first_half) * sin_t, lo, BF16)

        valid = _band_valid(jnp.where(i > 0, 0, BLK), Q_PER_KV * BLK)

        def kv_head(kh, carry):
            probs, _ = _softmax_sink(qm_ref[kh], kdup_ref[kh], valid, _sink_column(sinks_ref, kh))
            ost_ref[kh] = jnp.dot(probs.astype(BF16), vdup_ref[kh], preferred_element_type=F32)
            return carry

        lax.fori_loop(0, N_KV, kv_head, 0, unroll=2)
        for sb in range(8):
            cols = slice(sb * LANE, (sb + 1) * LANE)
            zb = zb0_ref[:, cols] if sb < 4 else zb1_ref[:, (sb - 4) * LANE:(sb - 3) * LANE]
            gate, _ = _silu_parts(zb)
            y_ref[:, D_A + sb * LANE:D_A + (sb + 1) * LANE] = (_unstack_heads(ost_ref, sb, lo) * gate).astype(BF16)

    tab = pl.BlockSpec((BLK, LANE), lambda i: (i, 0))
    return pl.pallas_call(
        body, name="mix_fwd", grid=(nb,),
        in_specs=_proj_specs() + [
            tab, tab, pl.BlockSpec((1, D_A), lambda i: (0, 0)), pl.BlockSpec((1, D_A), lambda i: (0, 0)),
            pl.BlockSpec((GROUPS, BLK, BLK), lambda i: (0, 0, 0)), pl.BlockSpec((BLK, GROUPS), lambda i: (0, 0)),
            pl.BlockSpec(memory_space=pltpu.SMEM)],
        out_specs=pl.BlockSpec((BLK, 2 * D_A), lambda i: (i, 0)),
        out_shape=jax.ShapeDtypeStruct((s, 2 * D_A), BF16),
        scratch_shapes=[pltpu.VMEM((N_KV, 2 * BLK, LANE), BF16), pltpu.VMEM((N_KV, 2 * BLK, LANE), BF16),
                        pltpu.VMEM((N_KV, Q_PER_KV * BLK, LANE), BF16), pltpu.VMEM((N_KV, Q_PER_KV * BLK, LANE), F32)],
        compiler_params=_params("arbitrary"),
    )(proj, proj, proj, proj, proj, proj, proj, proj, cos, sin, ln_g, ln_b, w_sp, b_sp_t, sinks)


def _tail_call(y, w_out_bf, x, target, gate, shift_f, scale_f, gf):
    s = x.shape[0]
    tm = min(s, 256)
    nsteps = s // tm

    def body(y_ref, w_ref, x_ref, t_ref, gate_ref, shf_ref, scf_ref, gf_ref, dx2_ref, do_ref, st_ref):
        i = pl.program_id(0)

        @pl.when(i == 0)
        def _():
            st_ref[...] = jnp.zeros((8, D), F32)

        o = jnp.dot(y_ref[...], w_ref[...], preferred_element_type=F32)
        gate_v = gate_ref[...]
        x2 = x_ref[...] + gate_v * o
        r2 = lax.rsqrt(jnp.mean(x2 * x2, axis=-1, keepdims=True) + EPS)
        xn2 = x2 * r2
        hn2 = xn2 * gf_ref[...]
        one_sc = 1.0 + scf_ref[...]
        err = hn2 * one_sc + shf_ref[...] - t_ref[...]
        dout = err * (1.0 / D)
        dhn2 = dout * one_sc
        dxn2 = dhn2 * gf_ref[...]
        dx2 = r2 * (dxn2 - xn2 * jnp.mean(dxn2 * xn2, axis=-1, keepdims=True))
        dx2_ref[...] = dx2
        do_ref[...] = (dx2 * gate_v).astype(BF16)
        st_ref[0:1, :] += jnp.sum(dout, axis=0, keepdims=True)
        st_ref[1:2, :] += jnp.sum(dout * hn2, axis=0, keepdims=True)
        st_ref[2:3, :] += jnp.sum(dhn2 * xn2, axis=0, keepdims=True)
        st_ref[3:4, :] += jnp.sum(dx2 * o, axis=0, keepdims=True)
        st_ref[4:5, :] += jnp.sum(err * err, axis=0, keepdims=True)

        @pl.when(i == nsteps - 1)
        def _():
            st_ref[5:6, :] = jnp.full((1, D), 0.5 / D, F32) * jnp.sum(st_ref[4:5, :])

    vec = pl.BlockSpec((1, D), lambda i: (0, 0))
    rows = lambda: pl.BlockSpec((tm, D), lambda i: (i, 0))
    return pl.pallas_call(
        body, name="tail", grid=(nsteps,),
        in_specs=[rows(), pl.BlockSpec((D, D), lambda i: (0, 0)), rows(), rows(), vec, vec, vec, vec],
        out_specs=[rows(), rows(), pl.BlockSpec((8, D), lambda i: (0, 0))],
        out_shape=[jax.ShapeDtypeStruct((s, D), F32), jax.ShapeDtypeStruct((s, D), BF16), jax.ShapeDtypeStruct((8, D), F32)],
        compiler_params=_params("arbitrary"),
    )(y, w_out_bf, x, target, gate, shift_f, scale_f, gf)


def _dy_call(do, w_out_bf):
    s = do.shape[0]
    tm = min(s, 512)

    def body(do_ref, w_ref, dy_ref):
        dy_ref[...] = lax.dot_general(do_ref[...], w_ref[...], NT, preferred_element_type=F32)

    return pl.pallas_call(
        body, name="dy", grid=(s // tm,),
        in_specs=[pl.BlockSpec((tm, D), lambda i: (i, 0)), pl.BlockSpec((D, D), lambda i: (0, 0))],
        out_specs=pl.BlockSpec((tm, D), lambda i: (i, 0)),
        out_shape=jax.ShapeDtypeStruct((s, D), F32), compiler_params=_params("parallel"),
    )(do, w_out_bf)


def _tn_call(a, b, name):
    s, m = a.shape
    n = b.shape[1]
    tn = 512
    ts = min(s, 1024)
    nk = s // ts

    def body(a_ref, b_ref, o_ref, acc_ref):
        k = pl.program_id(1)

        @pl.when(k == 0)
        def _():
            acc_ref[...] = jnp.zeros((m, tn), F32)

        acc_ref[...] += lax.dot_general(a_ref[...], b_ref[...], TN, preferred_element_type=F32)

        @pl.when(k == nk - 1)
        def _():
            o_ref[...] = acc_ref[...].astype(BF16)

    return pl.pallas_call(
        body, name=name, grid=(n // tn, nk),
        in_specs=[pl.BlockSpec((ts, m), lambda j, k: (k, 0)), pl.BlockSpec((ts, tn), lambda j, k: (k, j))],
        out_specs=pl.BlockSpec((m, tn), lambda j, k: (0, j)),
        out_shape=jax.ShapeDtypeStruct((m, n), BF16),
        scratch_shapes=[pltpu.VMEM((m, tn), F32)],
        compiler_params=_params("parallel", "arbitrary"),
    )(a, b)


def _tn_shards_call(pos, a, b, qs, name):
    s, m = a.shape
    ts = min(s, 1024)
    nk = s // ts

    def body(pos_ref, a_ref, b_ref, o_ref, acc_ref):
        k = pl.program_id(1)

        @pl.when(k == 0)
        def _():
            acc_ref[...] = jnp.zeros((m, W_IN_SHARD), F32)

        acc_ref[...] += lax.dot_general(a_ref[...], b_ref[...], TN, preferred_element_type=F32)

        @pl.when(k == nk - 1)
        def _():
            o_ref[...] = acc_ref[...].astype(BF16)

    def shard(j, pos):
        q = qs[0]
        for n in range(1, len(qs)):
            q = jnp.where(j == n, qs[n], q)
        return jnp.bitwise_xor(pos[0], q)

    return pl.pallas_call(
        body, name=name,
        grid_spec=pltpu.PrefetchScalarGridSpec(
            num_scalar_prefetch=1, grid=(len(qs), nk),
            in_specs=[pl.BlockSpec((ts, m), lambda j, k, pos: (k, 0)),
                      pl.BlockSpec((ts, W_IN_SHARD), lambda j, k, pos: (k, shard(j, pos)))],
            out_specs=pl.BlockSpec((m, W_IN_SHARD), lambda j, k, pos: (0, j)),
            scratch_shapes=[pltpu.VMEM((m, W_IN_SHARD), F32)]),
        out_shape=jax.ShapeDtypeStruct((m, len(qs) * W_IN_SHARD), BF16),
        compiler_params=_params("parallel", "arbitrary"),
    )(pos, a, b)


def _mix_bwd_call(proj, dy, cos, sin, ln_g, ln_b, w_sp, w_sp_t, b_sp_t, sinks):
    s = proj.shape[0]
    nb = s // BLK
    rev = lambda i: nb - 1 - i
    prev = lambda i: jnp.maximum(nb - 2 - i, 0)

    def body(ua_ref, va_ref, za_ref, q_ref, k_ref, v_ref, zb0_ref, zb1_ref, kp_ref, vp_ref, dy_ref,
             cos_ref, sin_ref, cosp_ref, sinp_ref, lg_ref, lb_ref, w_ref, wt_ref, bt_ref, sinks_ref,
             dp_ref, lnst_ref, dw_ref, dbt_ref, dsink_ref,
             kdup_ref, vdup_ref, dvln_ref, qm_ref, dom_ref, ost_ref, dqst_ref, dkdup_ref, dvdup_ref, kcar_ref, vcar_ref):
        i = pl.program_id(0)
        first_half, lo = _lane_masks()
        lane8 = lax.broadcasted_iota(jnp.int32, (8, LANE), 1)
        cos_t = cos_ref[...]
        sin_t = sin_ref[...]

        @pl.when(i == 0)
        def _():
            lnst_ref[...] = jnp.zeros((8, D_A), F32)
            dw_ref[...] = jnp.zeros((GROUPS, BLK, BLK), F32)
            dbt_ref[...] = jnp.zeros((BLK, LANE), F32)
            dsink_ref[...] = jnp.zeros((8, LANE), F32)
            kcar_ref[...] = jnp.zeros((BLK, D_KV), F32)
            vcar_ref[...] = jnp.zeros((BLK, D_KV), F32)

        vhat, rstd, vln = _layer_norm_fwd(va_ref[...], lg_ref[...], lb_ref[...])
        tril = _tril()
        triu = jnp.logical_not(tril) | (lax.broadcasted_iota(jnp.int32, (BLK, BLK), 0) == lax.broadcasted_iota(jnp.int32, (BLK, BLK), 1))
        lane_b = lax.broadcasted_iota(jnp.int32, (BLK, LANE), 1)
        db_acc = jnp.zeros((BLK, LANE), F32)
        for g in range(GROUPS):
            cols = slice(g * BLK, (g + 1) * BLK)
            vln_g = vln[:, cols].astype(BF16)
            wg = jnp.where(tril, w_ref[g], 0.0).astype(BF16)
            sg = jnp.dot(wg, vln_g, preferred_element_type=F32) + bt_ref[:, g:g + 1]
            za = za_ref[:, cols]
            gate, sig = _silu_parts(za)
            ua = ua_ref[:, cols]
            dya_g = dy_ref[:, cols]
            dya = dya_g * gate
            dp_ref[:, cols] = (dya * sg).astype(BF16)
            dp_ref[:, 2 * D_A + g * BLK:2 * D_A + (g + 1) * BLK] = (
                dya_g * (ua * sg) * (sig * (1.0 + za * (1.0 - sig)))).astype(BF16)
            ds = dya * ua
            ds_b = ds.astype(BF16)
            wtg = jnp.where(triu, wt_ref[g], 0.0).astype(BF16)
            dvln_ref[:, cols] = jnp.dot(wtg, ds_b, preferred_element_type=F32)
            dw_ref[g] += jnp.where(tril, lax.dot_general(ds_b, vln_g, NT, preferred_element_type=F32), 0.0)
            db_acc = db_acc + jnp.where(lane_b == g, jnp.sum(ds, axis=-1, keepdims=True), 0.0)
        dbt_ref[...] += db_acc
        dvln = dvln_ref[...]
        lnst_ref[0:1, :] += jnp.sum(dvln * vhat, axis=0, keepdims=True)
        lnst_ref[1:2, :] += jnp.sum(dvln, axis=0, keepdims=True)
        dvhat = dvln * lg_ref[...]
        m1 = jnp.mean(dvhat, axis=-1, keepdims=True)
        m2 = jnp.mean(dvhat * vhat, axis=-1, keepdims=True)
        dp_ref[:, D_A:2 * D_A] = (rstd * (dvhat - m1 - vhat * m2)).astype(BF16)

        cosp = cosp_ref[...]
        sinp = sinp_ref[...]
        for ks in range(2):
            cols = slice(ks * LANE, (ks + 1) * LANE)
            kslab = k_ref[:, cols]
            kr = kslab * cos_t + _rot_half(kslab, first_half) * sin_t
            kpslab = kp_ref[:, cols]
            kpr = kpslab * cosp + _rot_half(kpslab, first_half) * sinp
            for n, (kc, vc, kp, vp) in enumerate(zip(_dup_kv(kr, lo), _dup_kv(v_ref[:, cols], lo),
                                                     _dup_kv(kpr, lo), _dup_kv(vp_ref[:, cols], lo))):
                kdup_ref[2 * ks + n, BLK:2 * BLK, :] = kc
                vdup_ref[2 * ks + n, BLK:2 * BLK, :] = vc
                kdup_ref[2 * ks + n, 0:BLK, :] = kp
                vdup_ref[2 * ks + n, 0:BLK, :] = vp
        for sb in range(8):
            cols = slice(sb * LANE, (sb + 1) * LANE)
            qslab = q_ref[:, cols]
            _stack_heads(qm_ref, sb, qslab * cos_t + _rot_half(qslab, first_half) * sin_t, lo, BF16)
            zb = zb0_ref[:, cols] if sb < 4 else zb1_ref[:, (sb - 4) * LANE:(sb - 3) * LANE]
            gate, _ = _silu_parts(zb)
            _stack_heads(dom_ref, sb, dy_ref[:, D_A + sb * LANE:D_A + (sb + 1) * LANE] * gate, lo, F32)

        valid = _band_valid(jnp.where(i < nb - 1, 0, BLK), Q_PER_KV * BLK)

        def kv_head(kh, dsink_acc):
            qm = qm_ref[kh]
            kd = kdup_ref[kh]
            vd = vdup_ref[kh]
            probs, psink = _softmax_sink(qm, kd, valid, _sink_column(sinks_ref, kh))
            probs_b = probs.astype(BF16)
            o = jnp.dot(probs_b, vd, preferred_element_type=F32)
            ost_ref[kh] = o
            dom = dom_ref[kh]
            dom_b = dom.astype(BF16)
            delta = jnp.sum(dom * o, axis=-1, keepdims=True)
            dpr = lax.dot_general(dom_b, vd, NT, preferred_element_type=F32)
            dss = (probs * (dpr - delta) * SCALE).astype(BF16)
            sd = psink * delta
            for n in range(Q_PER_KV):
                dsink_acc = dsink_acc + jnp.where(lane8 == Q_PER_KV * kh + n, -jnp.sum(sd[n * BLK:(n + 1) * BLK]), 0.0)
            dqst_ref[kh] = jnp.dot(dss, kd, preferred_element_type=F32)
            dkdup_ref[kh] = lax.dot_general(dss, qm, TN, preferred_element_type=F32)
            dvdup_ref[kh] = lax.dot_general(probs_b, dom_b, TN, preferred_element_type=F32)
            return dsink_acc

        dsink_acc = lax.fori_loop(0, N_KV // 2, lambda j, acc: kv_head(2 * j + 1, kv_head(2 * j, acc)), jnp.zeros((8, LANE), F32))
        row0 = lax.broadcasted_iota(jnp.int32, (8, LANE), 0) == 0
        dsink_ref[...] += jnp.where(row0, dsink_acc, 0.0)

        for sb in range(8):
            cols = slice(sb * LANE, (sb + 1) * LANE)
            zb = zb0_ref[:, cols] if sb < 4 else zb1_ref[:, (sb - 4) * LANE:(sb - 3) * LANE]
            _, sig = _silu_parts(zb)
            dyb = dy_ref[:, D_A + sb * LANE:D_A + (sb + 1) * LANE]
            dp_ref[:, OFF_ZB + sb * LANE:OFF_ZB + (sb + 1) * LANE] = (
                dyb * _unstack_heads(ost_ref, sb, lo) * (sig * (1.0 + zb * (1.0 - sig)))).astype(BF16)
            dq_r = _unstack_heads(dqst_ref, sb, lo)
            dp_ref[:, OFF_Q + sb * LANE:OFF_Q + (sb + 1) * LANE] = (
                dq_r * cos_t - _rot_half(dq_r * sin_t, first_half)).astype(BF16)

        lo2 = lax.broadcasted_iota(jnp.int32, (2 * BLK, LANE), 1) < HEAD
        for ks in range(2):
            cols = slice(ks * LANE, (ks + 1) * LANE)
            ka = dkdup_ref[2 * ks]
            kb = dkdup_ref[2 * ks + 1]
            dk_band = jnp.where(lo2, ka + pltpu.roll(ka, HEAD, 1), kb + pltpu.roll(kb, HEAD, 1))
            va_ = dvdup_ref[2 * ks]
            vb_ = dvdup_ref[2 * ks + 1]
            dv_band = jnp.where(lo2, va_ + pltpu.roll(va_, HEAD, 1), vb_ + pltpu.roll(vb_, HEAD, 1))
            dkr = dk_band[BLK:2 * BLK, :] + kcar_ref[:, cols]
            dp_ref[:, OFF_K + ks * LANE:OFF_K + (ks + 1) * LANE] = (
                dkr * cos_t - _rot_half(dkr * sin_t, first_half)).astype(BF16)
            dp_ref[:, OFF_V + ks * LANE:OFF_V + (ks + 1) * LANE] = (
                dv_band[BLK:2 * BLK, :] + vcar_ref[:, cols]).astype(BF16)
            kcar_ref[:, cols] = dk_band[0:BLK, :]
            vcar_ref[:, cols] = dv_band[0:BLK, :]

    tab = pl.BlockSpec((BLK, LANE), lambda i: (rev(i), 0))
    tabp = pl.BlockSpec((BLK, LANE), lambda i: (prev(i), 0))
    kvp = lambda col: pl.BlockSpec((BLK, D_KV), lambda i: (prev(i), col))
    vec = pl.BlockSpec((1, D_A), lambda i: (0, 0))
    w3 = pl.BlockSpec((GROUPS, BLK, BLK), lambda i: (0, 0, 0))
    return pl.pallas_call(
        body, name="mix_bwd", grid=(nb,),
        in_specs=_proj_specs(nb) + [
            kvp(OFF_K // D_KV), kvp(OFF_V // D_KV), pl.BlockSpec((BLK, 2 * D_A), lambda i: (rev(i), 0)),
            tab, tab, tabp, tabp, vec, vec, w3, w3, pl.BlockSpec((BLK, GROUPS), lambda i: (0, 0)),
            pl.BlockSpec(memory_space=pltpu.SMEM)],
        out_specs=[pl.BlockSpec((BLK, D_IN), lambda i: (rev(i), 0)), pl.BlockSpec((8, D_A), lambda i: (0, 0)), w3,
                   pl.BlockSpec((BLK, LANE), lambda i: (0, 0)), pl.BlockSpec((8, LANE), lambda i: (0, 0))],
        out_shape=[jax.ShapeDtypeStruct((s, D_IN), BF16), jax.ShapeDtypeStruct((8, D_A), F32),
                   jax.ShapeDtypeStruct((GROUPS, BLK, BLK), F32), jax.ShapeDtypeStruct((BLK, LANE), F32),
                   jax.ShapeDtypeStruct((8, LANE), F32)],
        scratch_shapes=[pltpu.VMEM((N_KV, 2 * BLK, LANE), BF16), pltpu.VMEM((N_KV, 2 * BLK, LANE), BF16),
                        pltpu.VMEM((BLK, D_A), F32), pltpu.VMEM((N_KV, Q_PER_KV * BLK, LANE), BF16),
                        pltpu.VMEM((N_KV, Q_PER_KV * BLK, LANE), F32), pltpu.VMEM((N_KV, Q_PER_KV * BLK, LANE), F32),
                        pltpu.VMEM((N_KV, Q_PER_KV * BLK, LANE), F32), pltpu.VMEM((N_KV, 2 * BLK, LANE), F32),
                        pltpu.VMEM((N_KV, 2 * BLK, LANE), F32), pltpu.VMEM((BLK, D_KV), F32), pltpu.VMEM((BLK, D_KV), F32)],
        compiler_params=_params("arbitrary"),
    )(proj, proj, proj, proj, proj, proj, proj, proj, proj, proj, dy, cos, sin, cos, sin, ln_g, ln_b, w_sp, w_sp_t,
      b_sp_t, sinks)


def _dh_call(dproj, w_bf, x, dx2, scale, norm_g):
    s = x.shape[0]
    tm = min(s, 512)
    tk = W_IN_SHARD
    nk = D_IN // tk

    def body(dp_ref, w_ref, x_ref, dx2_ref, sc_ref, g_ref, gx_ref, st_ref, acc_ref):
        i = pl.program_id(0)
        k = pl.program_id(1)

        @pl.when((i == 0) & (k == 0))
        def _():
            st_ref[...] = jnp.zeros((8, D), F32)

        @pl.when(k == 0)
        def _():
            acc_ref[...] = jnp.zeros((tm, D), F32)

        acc_ref[...] += lax.dot_general(dp_ref[...], w_ref[...], NT, preferred_element_type=F32)

        @pl.when(k == nk - 1)
        def _():
            g = g_ref[...]
            one_sc = 1.0 + sc_ref[...]

            def chunk(n, carry):
                rows = pl.ds(pl.multiple_of(n * BLK, BLK), BLK)
                dh = acc_ref[rows, :]
                xv = x_ref[rows, :]
                r = lax.rsqrt(jnp.mean(xv * xv, axis=-1, keepdims=True) + EPS)
                xn = xv * r
                dhn = dh * one_sc
                dxn = dhn * g
                gx_ref[rows, :] = dx2_ref[rows, :] + r * (dxn - xn * jnp.mean(dxn * xn, axis=-1, keepdims=True))
                st_ref[0:1, :] += jnp.sum(dh, axis=0, keepdims=True)
                st_ref[1:2, :] += jnp.sum(dh * (xn * g), axis=0, keepdims=True)
                st_ref[2:3, :] += jnp.sum(dhn * xn, axis=0, keepdims=True)
                return carry

            lax.fori_loop(0, tm // BLK, chunk, 0)

    vec = pl.BlockSpec((1, D), lambda i, k: (0, 0))
    rows = lambda: pl.BlockSpec((tm, D), lambda i, k: (i, 0))
    return pl.pallas_call(
        body, name="dh", grid=(s // tm, nk),
        in_specs=[pl.BlockSpec((tm, tk), lambda i, k: (i, k)), pl.BlockSpec((D, tk), lambda i, k: (0, k)), rows(), rows(), vec, vec],
        out_specs=[rows(), pl.BlockSpec((8, D), lambda i, k: (0, 0))],
        out_shape=[jax.ShapeDtypeStruct((s, D), F32), jax.ShapeDtypeStruct((8, D), F32)],
        scratch_shapes=[pltpu.VMEM((tm, D), F32)],
        compiler_params=_params("arbitrary", "arbitrary"),
    )(dproj, w_bf, x, dx2, scale, norm_g)


def _adam_math(w, g, m, v):
    m_new = ADAM_B1 * m + (1.0 - ADAM_B1) * g
    v_new = ADAM_B2 * v + (1.0 - ADAM_B2) * (g * g)
    m_hat = m_new / ADAM_C1
    v_hat = v_new / ADAM_C2
    delta = -ADAM_LR * (m_hat / (jnp.sqrt(v_hat) + ADAM_EPS) + ADAM_WD * w)
    return delta, m_new, v_new


def _adam_small_call(tensors):
    n = len(tensors)

    def body(*refs):
        ins, outs = refs[:4 * n], refs[4 * n:]
        for t in range(n):
            w_ref, g_ref, m_ref, v_ref = ins[4 * t:4 * t + 4]
            d, mo, vo = _adam_math(w_ref[...], g_ref[...], m_ref[...], v_ref[...])
            outs[3 * t][...], outs[3 * t + 1][...], outs[3 * t + 2][...] = d, mo, vo

    vm = pl.BlockSpec(memory_space=pltpu.VMEM)
    flat = [a for t in tensors for a in t]
    out = pl.pallas_call(
        body, name="adam_small", in_specs=[vm] * (4 * n), out_specs=[vm] * (3 * n),
        out_shape=[jax.ShapeDtypeStruct(t[0].shape, F32) for t in tensors for _ in range(3)],
        compiler_params=pltpu.CompilerParams(vmem_limit_bytes=VMEM_LIMIT),
    )(*flat)
    return [tuple(out[3 * t:3 * t + 3]) for t in range(n)]


def _adam_halves_call(pos, w, mine, theirs, m, v, name):
    r, n = w.shape
    half = r // 2
    tr = 128
    nh = half // tr

    def body(pos_ref, w_ref, mine_ref, theirs_ref, m_ref, v_ref, g_ref, d_ref, mo_ref, vo_ref):
        is_mine = (pl.program_id(0) // nh) == pos_ref[1]
        g = jnp.where(is_mine, mine_ref[...], theirs_ref[...])
        g_ref[...] = g
        d_ref[...], mo_ref[...], vo_ref[...] = _adam_math(w_ref[...], g, m_ref[...], v_ref[...])

    spec = lambda: pl.BlockSpec((tr, n), lambda i, pos: (i, 0))
    hspec = lambda: pl.BlockSpec((tr, n), lambda i, pos: (i % nh, 0))
    return pl.pallas_call(
        body, name=name,
        grid_spec=pltpu.PrefetchScalarGridSpec(
            num_scalar_prefetch=1, grid=(r // tr,), in_specs=[spec(), hspec(), hspec(), spec(), spec()],
            out_specs=[spec() for _ in range(4)]),
        out_shape=[jax.ShapeDtypeStruct((r, n), F32)] * 4, compiler_params=_params("parallel"),
    )(pos, w, mine, theirs, m, v)


def _adam_outer_call(w, ct, dm, m, v, name):
    r, n = w.shape
    tr = 128

    def body(w_ref, ct_ref, dm_ref, m_ref, v_ref, g_ref, d_ref, mo_ref, vo_ref):
        g = ct_ref[:, 0:1] * dm_ref[0:1, :]
        for b in range(1, N_DEV):
            g = g + ct_ref[:, b:b + 1] * dm_ref[b:b + 1, :]
        g_ref[...] = g
        d_ref[...], mo_ref[...], vo_ref[...] = _adam_math(w_ref[...], g, m_ref[...], v_ref[...])

    spec = lambda: pl.BlockSpec((tr, n), lambda i: (i, 0))
    return pl.pallas_call(
        body, name=name, grid=(r // tr,),
        in_specs=[spec(), pl.BlockSpec((tr, N_DEV), lambda i: (i, 0)), pl.BlockSpec((N_DEV, n), lambda i: (0, 0)), spec(), spec()],
        out_specs=[spec() for _ in range(4)],
        out_shape=[jax.ShapeDtypeStruct((r, n), F32)] * 4, compiler_params=_params("parallel"),
    )(w, ct, dm, m, v)


def _sum_pieces_call(pos, part, part_block, recvs, name):
    r, n = recvs[0].shape[1:]
    tr = min(r, 256)
    nrb = r // tr

    def body(pos_ref, p_ref, *refs):
        acc = p_ref[...].astype(F32)
        for r_ref in refs[:-1]:
            for d in range(r_ref.shape[0]):
                acc = acc + r_ref[d].astype(F32)
        refs[-1][...] = acc

    return pl.pallas_call(
        body, name=name,
        grid_spec=pltpu.PrefetchScalarGridSpec(
            num_scalar_prefetch=1, grid=(nrb,),
            in_specs=[pl.BlockSpec((tr, n), lambda i, pos: part_block(i, pos, nrb))] + [
                pl.BlockSpec((rv.shape[0], tr, n), lambda i, pos: (0, i, 0)) for rv in recvs],
            out_specs=pl.BlockSpec((tr, n), lambda i, pos: (i, 0))),
        out_shape=jax.ShapeDtypeStruct((r, n), F32), compiler_params=_params("parallel"),
    )(pos, part, *recvs)


def _coords():
    return lax.axis_index("x"), lax.axis_index("y"), lax.axis_index("c")


def _allgather_sum_call(blk, name, with_sum):
    m_per, n = blk.shape

    def body(x_ref, out_ref, *rest):
        if with_sum:
            sum_ref, send_sems, recv_sems, local_sem = rest
        else:
            send_sems, recv_sems, local_sem = rest
        x, y, c = _coords()
        me, sibling = (x, y, c), (x, y, 1 - c)
        chips = [(1 - x, y), (x, 1 - y), (1 - x, 1 - y)]

        def rows(px, py, pc):
            return out_ref.at[pl.ds((4 * px + 2 * py + pc) * m_per, m_per), :]

        def copy(k, block, to, src=None):
            return pltpu.make_async_remote_copy(
                src_ref=rows(*block) if src is None else src, dst_ref=rows(*block),
                send_sem=send_sems.at[k], recv_sem=recv_sems.at[k], device_id=to, device_id_type=MESH)

        mine = pltpu.make_async_copy(x_ref, rows(*me), local_sem)
        mine.start()
        first = [copy(0, me, sibling, src=x_ref)]
        first += [copy(1 + j, me, (*chip, c), src=x_ref) for j, chip in enumerate(chips)]
        for cp in first:
            cp.start()
        passed = [copy(4 + j, (*chip, c), sibling) for j, chip in enumerate(chips)]
        for j, chip in enumerate(chips):
            copy(1 + j, (*chip, c), me).wait_recv()
            passed[j].start()
        copy(0, sibling, me).wait_recv()
        for j, chip in enumerate(chips):
            copy(4 + j, (*chip, 1 - c), me).wait_recv()
        for cp in first + passed:
            cp.wait_send()
        mine.wait()
        if with_sum:
            acc = out_ref[0:m_per, :]
            for d in range(1, N_DEV):
                acc = acc + out_ref[d * m_per:(d + 1) * m_per, :]
            sum_ref[...] = acc

    vm = pl.BlockSpec(memory_space=pltpu.VMEM)
    out_shape = [jax.ShapeDtypeStruct((N_DEV * m_per, n), F32)]
    if with_sum:
        out_shape.append(jax.ShapeDtypeStruct((m_per, n), F32))
    return pl.pallas_call(
        body, name=name, out_shape=out_shape, in_specs=[vm], out_specs=[vm] * len(out_shape),
        scratch_shapes=[pltpu.SemaphoreType.DMA((7,)), pltpu.SemaphoreType.DMA((7,)), pltpu.SemaphoreType.DMA],
        compiler_params=pltpu.CompilerParams(vmem_limit_bytes=VMEM_LIMIT),
    )(blk)


def _weights_gather_call(wi_full, wo_full):
    hi = D // 2
    ho = W_OUT_SHARD // 2

    def body(wi_in, wo_in, fi_ref, fo_ref, send_sems, recv_sems):
        del wi_in, wo_in
        x, y, c = _coords()
        sibling = (x, y, 1 - c)
        chips = [(1 - x, y), (x, 1 - y), (1 - x, 1 - y)]

        def half(which, px, py, pc):
            j = 2 * px + py
            if which == 0:
                return fi_ref.at[pl.ds(pc * hi, hi), pl.ds(j * W_IN_SHARD, W_IN_SHARD)]
            return fo_ref.at[pl.ds(j * W_OUT_SHARD + pc * ho, ho), :]

        def copy(k, which, block, to):
            return pltpu.make_async_remote_copy(
                src_ref=half(which, *block), dst_ref=half(which, *block), send_sem=send_sems.at[k],
                recv_sem=recv_sems.at[k], device_id=to, device_id_type=MESH)

        first = [copy(6 * w + j, w, (x, y, c), (*chip, c)) for w in range(2) for j, chip in enumerate(chips)]
        for cp in first:
            cp.start()
        passed = []
        for w in range(2):
            for j, chip in enumerate(chips):
                copy(6 * w + j, w, (*chip, c), (x, y, c)).wait_recv()
                cp = copy(6 * w + 3 + j, w, (*chip, c), sibling)
                cp.start()
                passed.append(cp)
        for w in range(2):
            for j, chip in enumerate(chips):
                copy(6 * w + 3 + j, w, (*chip, 1 - c), (x, y, c)).wait_recv()
        for cp in first + passed:
            cp.wait_send()

    anyspec = pl.BlockSpec(memory_space=pl.ANY)
    return pl.pallas_call(
        body, name="weights_gather",
        out_shape=[jax.ShapeDtypeStruct((D, D_IN), BF16), jax.ShapeDtypeStruct((D, D), BF16)],
        in_specs=[anyspec, anyspec], out_specs=[anyspec, anyspec], input_output_aliases={0: 0, 1: 1},
        scratch_shapes=[pltpu.SemaphoreType.DMA((12,)), pltpu.SemaphoreType.DMA((12,))],
    )(wi_full, wo_full)


HBM_SPEC = pl.BlockSpec(memory_space=pltpu.HBM)
SEM_SPEC = pl.BlockSpec(memory_space=pltpu.SEMAPHORE)
SIDE_EFFECT = pltpu.SideEffectType.DATAFLOW_SIDE_EFFECTING


def _peer(x, y, c, q, cb):
    return (1 - x if q & 2 else x, 1 - y if q & 1 else y, 1 - c if cb else c)


def _w_in_piece(slots):
    def piece(part_ref, k, to):
        return part_ref.at[pl.ds(to[2] * (D // 2), D // 2), pl.ds(slots[k] * W_IN_SHARD, W_IN_SHARD)]
    return piece


def _w_out_piece(part_ref, k, to):
    ho = W_OUT_SHARD // 2
    return part_ref.at[pl.ds((2 * to[0] + to[1]) * W_OUT_SHARD + to[2] * ho, ho), :]


def _group_piece(part_ref, k, to):
    return part_ref.at[4 * to[0] + 2 * to[1] + to[2]]


def _whole_piece(part_ref, k, to):
    return part_ref


def _exchange_start_call(part, rels, piece, slot_shape, name):
    n = len(rels)
    land = lax.empty((n,) + slot_shape, part.dtype)

    def body(part_ref, land_ref, send_sems, recv_sems, part_thru, land_thru, token):
        x, y, c = _coords()
        for k, (q, cb) in enumerate(rels):
            to = _peer(x, y, c, q, cb)
            pltpu.make_async_remote_copy(src_ref=piece(part_ref, k, to), dst_ref=land_ref.at[k], send_sem=send_sems.at[k],
                                         recv_sem=recv_sems.at[k], device_id=to, device_id_type=MESH).start()
        token[...] = jnp.zeros_like(token)

    return pl.pallas_call(
        body, name=name,
        out_shape=(pltpu.SemaphoreType.DMA((n,)), pltpu.SemaphoreType.DMA((n,)), pltpu.HBM(part.shape, part.dtype),
                   pltpu.HBM(land.shape, land.dtype), jax.ShapeDtypeStruct((8, LANE), F32)),
        in_specs=(HBM_SPEC, HBM_SPEC), out_specs=(SEM_SPEC, SEM_SPEC, HBM_SPEC, HBM_SPEC, pl.BlockSpec(memory_space=pltpu.VMEM)),
        input_output_aliases={0: 2, 1: 3},
        compiler_params=pltpu.CompilerParams(has_side_effects=SIDE_EFFECT),
    )(pltpu.with_memory_space_constraint(part, pltpu.HBM), pltpu.with_memory_space_constraint(land, pltpu.HBM))


def _exchange_wait_call(started, rels, piece, after, name):
    send_sems, recv_sems, part_thru, land_thru, _ = started

    def body(part_ref, land_ref, send_sems, recv_sems, after_ref, part_out, land_out):
        x, y, c = _coords()
        for k, (q, cb) in enumerate(rels):
            to = _peer(x, y, c, q, cb)
            cp = pltpu.make_async_remote_copy(src_ref=piece(part_ref, k, to), dst_ref=land_ref.at[k], send_sem=send_sems.at[k],
                                              recv_sem=recv_sems.at[k], device_id=to, device_id_type=MESH)
            cp.wait_send()
            cp.wait_recv()

    return pl.pallas_call(
        body, name=name,
        out_shape=(pltpu.HBM(part_thru.shape, part_thru.dtype), pltpu.HBM(land_thru.shape, land_thru.dtype)),
        in_specs=(HBM_SPEC, HBM_SPEC, SEM_SPEC, SEM_SPEC, pl.BlockSpec(memory_space=pl.ANY)), out_specs=(HBM_SPEC, HBM_SPEC),
        input_output_aliases={0: 0, 1: 1},
        compiler_params=pltpu.CompilerParams(has_side_effects=SIDE_EFFECT),
    )(part_thru, land_thru, send_sems, recv_sems, after)


def _pair_exchange_call(gi, go):
    hi = D // 2
    ho = W_OUT_SHARD // 2

    def body(gi_in, go_in, fi_ref, fo_ref, send_sems, recv_sems):
        del gi_in, go_in
        x, y, c = _coords()
        sibling = (x, y, 1 - c)
        mine = (fi_ref.at[pl.ds(c * hi, hi), :], fo_ref.at[pl.ds(c * ho, ho), :])
        theirs = (fi_ref.at[pl.ds((1 - c) * hi, hi), :], fo_ref.at[pl.ds((1 - c) * ho, ho), :])
        sends = [pltpu.make_async_remote_copy(src_ref=ref, dst_ref=ref, send_sem=send_sems.at[k], recv_sem=recv_sems.at[k],
                                              device_id=sibling, device_id_type=MESH) for k, ref in enumerate(mine)]
        for cp in sends:
            cp.start()
        for k, ref in enumerate(theirs):
            pltpu.make_async_remote_copy(src_ref=ref, dst_ref=ref, send_sem=send_sems.at[k], recv_sem=recv_sems.at[k],
                                         device_id=sibling, device_id_type=MESH).wait_recv()
        for cp in sends:
            cp.wait_send()

    anyspec = pl.BlockSpec(memory_space=pl.ANY)
    return pl.pallas_call(
        body, name="pair_exchange",
        out_shape=[jax.ShapeDtypeStruct((D, W_IN_SHARD), F32), jax.ShapeDtypeStruct((W_OUT_SHARD, D), F32)],
        in_specs=[anyspec, anyspec], out_specs=[anyspec, anyspec], input_output_aliases={0: 0, 1: 1},
        scratch_shapes=[pltpu.SemaphoreType.DMA((2,)), pltpu.SemaphoreType.DMA((2,))],
    )(gi, go)


def _rope_tables(s):
    inv_freq = 10000.0 ** (-jnp.arange(0, HEAD, 2, dtype=F32) / HEAD)
    ang = jnp.arange(s, dtype=F32)[:, None] * inv_freq[None, :]
    return jnp.tile(jnp.cos(ang), (1, LANE // (HEAD // 2))), jnp.tile(jnp.sin(ang), (1, LANE // (HEAD // 2)))


def _pad_cols(a, n):
    return jnp.pad(a, ((0, 0), (0, n - a.shape[1])))


def kernel(x, c, w_ada, b_ada, norm_g, w_in, ln_v_g, ln_v_b, w_spatial, b_spatial, sinks, w_out, w_ada_final, b_ada_final, final_norm_g, loss_target, m_w_ada, m_b_ada, m_norm_g, m_w_in, m_ln_v_g, m_ln_v_b, m_w_spatial, m_b_spatial, m_sinks, m_w_out, m_w_ada_final, m_b_ada_final, m_final_norm_g, v_w_ada, v_b_ada, v_norm_g, v_w_in, v_ln_v_g, v_ln_v_b, v_w_spatial, v_b_spatial, v_sinks, v_w_out, v_w_ada_final, v_b_ada_final, v_final_norm_g):
    s = x.shape[1]
    ax, ay, ac = _coords()
    chip = 2 * ax + ay
    me = 4 * ax + 2 * ay + ac
    n_ada = w_ada.shape[2]
    n_adaf = w_ada_final.shape[1]

    x2d = x.reshape(s, D)
    tgt = loss_target.reshape(s, D)
    w_ada2, w_in2, w_out2 = w_ada[0], w_in[0], w_out[0]
    b_ada_f2 = b_ada_final.reshape(1, 2 * D)
    gf = final_norm_g.reshape(1, D)

    c_all = _allgather_sum_call(jnp.pad(c, ((0, 7), (0, 0))), "gather_c", False)[0][::8]
    mod_p, c_act = _rowmat_call(c_all, w_ada2, lax.dynamic_slice(b_ada, (0, chip * n_ada), (1, n_ada)), "mod")
    modf_p, _ = _rowmat_call(c_all, w_ada_final, lax.dynamic_slice(b_ada_f2, (0, chip * n_adaf), (1, n_adaf)), "mod_final")
    mods = _allgather_sum_call(jnp.concatenate([mod_p, modf_p], axis=1), "gather_mod", False)[0]
    my_rows = [lax.dynamic_slice(mods, (16 * j + me, 0), (1, n_ada + n_adaf)) for j in range(N_CHIP)]
    mod = jnp.concatenate([r[:, :n_ada] for r in my_rows], axis=1)
    mod_f = jnp.concatenate([r[:, n_ada:] for r in my_rows], axis=1)
    shift, scale, gate = mod[:, :D], mod[:, D:2 * D], mod[:, 2 * D:]
    shift_f, scale_f = mod_f[:, :D], mod_f[:, D:]

    pos = jnp.stack([chip, ac]).astype(jnp.int32)
    w_in_own = _cast_into_call(pos, w_in2, (D, D_IN), "cast_w_in")
    w_out_own = _cast_into_call(pos, w_out2, (D, D), "cast_w_out")

    cos, sin = _rope_tables(s)
    b_sp_t = b_spatial[0].T
    sinks1 = sinks.reshape(N_Q)
    proj, h, w_in_bf, w_out_bf = _proj_gather_call(pos, x2d, shift, scale, norm_g, w_in_own, w_out_own)
    y = _mix_fwd_call(proj, cos, sin, ln_v_g, ln_v_b, w_spatial[0], b_sp_t, sinks1)
    dx2, do, st_tail = _tail_call(y, w_out_bf, x2d, tgt, gate, shift_f, scale_f, gf)

    rel_o = [(0, 1), (1, 0), (1, 1), (2, 0), (2, 1), (3, 0), (3, 1)]
    rel_a = [(1, 0), (1, 1), (2, 0), (2, 1)]
    rel_b = [(3, 0), (3, 1), (0, 1)]
    piece_a, piece_b = _w_in_piece([0, 0, 1, 1]), _w_in_piece([0, 0, 1])
    half_in, half_out = (D // 2, W_IN_SHARD), (W_OUT_SHARD // 2, D)

    g_w_out_p = _tn_call(y, do, "grad_w_out")
    st_o = _exchange_start_call(g_w_out_p, rel_o, _w_out_piece, half_out, "send_w_out")
    dy = _dy_call(do, w_out_bf)
    dproj, st_ln, d_wsp, d_bsp_t, d_sink = _mix_bwd_call(
        proj, dy, cos, sin, ln_v_g + st_o[4][0:1, 0:1], ln_v_b, w_spatial[0], jnp.swapaxes(w_spatial[0], 1, 2), b_sp_t, sinks1)
    g_w_in_a = _tn_shards_call(pos, h, dproj, (1, 2), "grad_w_in_a")
    st_a = _exchange_start_call(g_w_in_a, rel_a, piece_a, half_in, "send_w_in_a")
    g_w_in_b = _tn_shards_call(pos, h, dproj, (3, 0), "grad_w_in_b")
    st_b = _exchange_start_call(g_w_in_b, rel_b, piece_b, half_in, "send_w_in_b")
    rel_all = rel_o
    st_s = _exchange_start_call(d_wsp, rel_all, _group_piece, (BLK, BLK), "send_w_spatial")
    grad_x, st_dh = _dh_call(dproj, w_in_bf, x2d, dx2, scale + (st_a[4][0:1, 0:1] + st_s[4][0:1, 0:1]), norm_g)

    g_w_out_p, recv_o = _exchange_wait_call(st_o, rel_o, _w_out_piece, st_dh, "wait_w_out")
    _, recv_a = _exchange_wait_call(st_a, rel_a, piece_a, st_dh, "wait_w_in_a")
    g_w_in_b, recv_b = _exchange_wait_call(st_b, rel_b, piece_b, st_dh, "wait_w_in_b")
    d_wsp, recv_s = _exchange_wait_call(st_s, rel_all, _group_piece, st_dh, "wait_w_spatial")
    mine_in = _sum_pieces_call(pos, g_w_in_b, lambda i, p, nrb: (p[1] * nrb + i, 1), [recv_a, recv_b], "sum_w_in")
    mine_out = _sum_pieces_call(pos, g_w_out_p, lambda i, p, nrb: ((2 * p[0] + p[1]) * nrb + i, 0), [recv_o], "sum_w_out")
    wsp_group = _sum_pieces_call(pos, d_wsp.reshape(GROUPS * BLK, BLK), lambda i, p, nrb: (2 * p[0] + p[1], 0), [recv_s],
                                 "sum_w_spatial")
    to_sibling = [(0, 1)]
    st_pi = _exchange_start_call(mine_in, to_sibling, _whole_piece, half_in, "swap_w_in")
    st_po = _exchange_start_call(mine_out, to_sibling, _whole_piece, half_out, "swap_w_out")

    misc = jnp.concatenate([st_ln, d_bsp_t[:, :GROUPS].T, d_sink, jnp.zeros((8, D - D_A - 2 * LANE), F32)], axis=1)
    pack = jnp.concatenate([wsp_group.reshape(8, D) + (st_pi[4][0:1, 0:1] + st_po[4][0:1, 0:1]), st_tail, st_dh, misc], axis=0)
    rows = pack.shape[0]
    packs, tot = _allgather_sum_call(pack, "gather_small", True)
    packs = packs.reshape(N_DEV, rows, D)
    dmod_all = jnp.concatenate([packs[:, 16, :], packs[:, 17, :], packs[:, 11, :]], axis=1)
    dmodf_all = jnp.concatenate([packs[:, 8, :], packs[:, 9, :]], axis=1)
    loss = tot[13, 0]
    mine_in, theirs_in = _exchange_wait_call(st_pi, to_sibling, _whole_piece, tot, "swapped_w_in")
    mine_out, theirs_out = _exchange_wait_call(st_po, to_sibling, _whole_piece, tot, "swapped_w_out")
    small = {
        "b_ada": jnp.concatenate([tot[16:17], tot[17:18], tot[11:12]], axis=1),
        "norm_g": tot[18:19],
        "ln_v_g": tot[24:25, :D_A],
        "ln_v_b": tot[25:26, :D_A],
        "w_spatial": packs[:, 0:8, :].reshape(GROUPS * BLK, BLK),
        "b_spatial": tot[24:32, D_A:D_A + BLK],
        "sinks": tot[24:25, D_A + LANE:D_A + LANE + N_Q],
        "b_ada_final": jnp.concatenate([tot[8:9], tot[9:10]], axis=1),
        "final_norm_g": tot[10:11],
    }

    weights = dict(w_ada=w_ada, b_ada=b_ada, norm_g=norm_g, w_in=w_in, ln_v_g=ln_v_g, ln_v_b=ln_v_b, w_spatial=w_spatial,
                   b_spatial=b_spatial, sinks=sinks, w_out=w_out, w_ada_final=w_ada_final, b_ada_final=b_ada_final,
                   final_norm_g=final_norm_g)
    m_in = dict(w_ada=m_w_ada, b_ada=m_b_ada, norm_g=m_norm_g, w_in=m_w_in, ln_v_g=m_ln_v_g, ln_v_b=m_ln_v_b,
                w_spatial=m_w_spatial, b_spatial=m_b_spatial, sinks=m_sinks, w_out=m_w_out, w_ada_final=m_w_ada_final,
                b_ada_final=m_b_ada_final, final_norm_g=m_final_norm_g)
    v_in = dict(w_ada=v_w_ada, b_ada=v_b_ada, norm_g=v_norm_g, w_in=v_w_in, ln_v_g=v_ln_v_g, ln_v_b=v_ln_v_b,
                w_spatial=v_w_spatial, b_spatial=v_b_spatial, sinks=v_sinks, w_out=v_w_out, w_ada_final=v_w_ada_final,
                b_ada_final=v_b_ada_final, final_norm_g=v_final_norm_g)
    c_act_t = c_act.T
    outer = {"w_ada": lax.dynamic_slice(dmod_all, (0, chip * n_ada), (N_DEV, n_ada)),
             "w_ada_final": lax.dynamic_slice(dmodf_all, (0, chip * n_adaf), (N_DEV, n_adaf))}
    halves = {"w_in": (mine_in, theirs_in[0]), "w_out": (mine_out, theirs_out[0])}
    done = {}
    for name, (mine, theirs) in halves.items():
        shape2 = (2 * mine.shape[0], mine.shape[1])
        done[name] = _adam_halves_call(pos, weights[name].reshape(shape2), mine, theirs, m_in[name].reshape(shape2),
                                       v_in[name].reshape(shape2), "adam_" + name)
    for name, dm in outer.items():
        shape2 = (D, dm.shape[1])
        done[name] = _adam_outer_call(weights[name].reshape(shape2), c_act_t, dm, m_in[name].reshape(shape2),
                                      v_in[name].reshape(shape2), "adam_" + name)
    updates = _adam_small_call([(weights[name].reshape(g.shape), g, m_in[name].reshape(g.shape), v_in[name].reshape(g.shape))
                                for name, g in small.items()])
    for (name, g), upd in zip(small.items(), updates):
        done[name] = (g, *upd)
    outs = [[done[name][k].reshape(w.shape) for name, w in weights.items()] for k in range(4)]
    return (loss, grad_x.reshape(x.shape), *outs[0], *outs[1], *outs[2], *outs[3])
```

```python
import jax
import jax.numpy as jnp
from jax import lax
from jax.experimental import pallas as pl
from jax.experimental.pallas import tpu as pltpu

F32 = jnp.float32
BF16 = jnp.bfloat16
MESH = pl.DeviceIdType.MESH

D = 2048
D_A = 1024
D_B = 1024
D_KV = 256
HEAD = 64
N_Q = 16
N_KV = 4
Q_PER_KV = N_Q // N_KV
BLK = 128
GROUPS = 8
D_IN = 5632
OFF_Q, OFF_K, OFF_V, OFF_ZB = 3072, 4096, 4352, 4608
N_CHIP = 4
N_DEV = 8
W_IN_SHARD = D_IN // N_CHIP
W_OUT_SHARD = D // N_CHIP
EPS = 1e-5
SCALE = HEAD ** -0.5
NEG = -1e30
LANE = 128
VMEM_LIMIT = 56 * 1024 * 1024

ADAM_LR, ADAM_B1, ADAM_B2, ADAM_EPS, ADAM_WD, ADAM_STEP = 0.001, 0.9, 0.999, 1e-08, 0.01, 10
ADAM_C1 = 1.0 - ADAM_B1 ** ADAM_STEP
ADAM_C2 = 1.0 - ADAM_B2 ** ADAM_STEP

NT = (((1,), (1,)), ((), ()))
TN = (((0,), (0,)), ((), ()))


def _params(*sem):
    return pltpu.CompilerParams(dimension_semantics=sem, vmem_limit_bytes=VMEM_LIMIT)


def _silu_parts(z):
    sig = 1.0 / (1.0 + jnp.exp(-z))
    return z * sig, sig


def _rot_half(v, first_half):
    return jnp.where(first_half, -pltpu.roll(v, 96, 1), pltpu.roll(v, 32, 1))


def _lane_masks():
    lane = lax.broadcasted_iota(jnp.int32, (BLK, LANE), 1)
    return (lane % HEAD) < (HEAD // 2), lane < HEAD


def _band_valid(first_block_bound, rows=BLK):
    rr = lax.broadcasted_iota(jnp.int32, (rows, 2 * BLK), 0) & (BLK - 1)
    jj = lax.broadcasted_iota(jnp.int32, (rows, 2 * BLK), 1)
    return (jj > rr) & (jj <= rr + BLK) & (jj >= first_block_bound)


def _dup_kv(slab, lo):
    rolled = pltpu.roll(slab, HEAD, 1)
    return jnp.where(lo, slab, rolled).astype(BF16), jnp.where(lo, rolled, slab).astype(BF16)


def _stack_heads(ref, sb, slab, lo, dtype):
    kh, base = sb // 2, 2 * (sb % 2) * BLK
    zero = jnp.zeros_like(slab)
    ref[kh, base:base + BLK, :] = jnp.where(lo, slab, zero).astype(dtype)
    ref[kh, base + BLK:base + 2 * BLK, :] = jnp.where(lo, zero, slab).astype(dtype)


def _unstack_heads(ref, sb, lo):
    kh, base = sb // 2, 2 * (sb % 2) * BLK
    return jnp.where(lo, ref[kh, base:base + BLK, :], ref[kh, base + BLK:base + 2 * BLK, :])


def _sink_column(sinks_ref, kh):
    row = lax.broadcasted_iota(jnp.int32, (Q_PER_KV * BLK, 1), 0)
    col = jnp.full(row.shape, sinks_ref[Q_PER_KV * kh + Q_PER_KV - 1], F32)
    for n in range(Q_PER_KV - 2, -1, -1):
        col = jnp.where(row < (n + 1) * BLK, sinks_ref[Q_PER_KV * kh + n], col)
    return col


def _tril():
    t = lax.broadcasted_iota(jnp.int32, (BLK, BLK), 0)
    s = lax.broadcasted_iota(jnp.int32, (BLK, BLK), 1)
    return s <= t


def _layer_norm_fwd(va, lg, lb):
    mu = jnp.mean(va, axis=-1, keepdims=True)
    xc = va - mu
    rstd = lax.rsqrt(jnp.mean(xc * xc, axis=-1, keepdims=True) + EPS)
    vhat = xc * rstd
    return vhat, rstd, vhat * lg + lb


def _softmax_sink(qm, kexp, valid, sink):
    s = lax.dot_general(qm, kexp, NT, preferred_element_type=F32) * SCALE
    s = jnp.where(valid, s, NEG)
    m = jnp.maximum(jnp.max(s, axis=-1, keepdims=True), sink)
    p = jnp.exp(s - m)
    esink = jnp.exp(sink - m)
    den = jnp.sum(p, axis=-1, keepdims=True) + esink
    return p / den, esink / den


def _rowmat_call(c_all, w, b, name):
    n = w.shape[1]
    tn = 512

    def body(c_ref, w_ref, b_ref, o_ref, ca_ref):
        ca, _ = _silu_parts(c_ref[...])
        ca_ref[...] = ca
        o_ref[...] = jnp.dot(ca.astype(BF16), w_ref[...].astype(BF16), preferred_element_type=F32) + b_ref[...]

    return pl.pallas_call(
        body, name=name, grid=(n // tn,),
        in_specs=[pl.BlockSpec((N_DEV, D), lambda j: (0, 0)), pl.BlockSpec((D, tn), lambda j: (0, j)),
                  pl.BlockSpec((1, tn), lambda j: (0, j))],
        out_specs=[pl.BlockSpec((N_DEV, tn), lambda j: (0, j)), pl.BlockSpec((N_DEV, D), lambda j: (0, 0))],
        out_shape=[jax.ShapeDtypeStruct((N_DEV, n), F32), jax.ShapeDtypeStruct((N_DEV, D), F32)],
        compiler_params=_params("arbitrary"),
    )(c_all, w, b)


def _cast_into_call(pos, w, full_shape, name):
    r, n = w.shape
    tr = min(r, 512)
    by_cols = full_shape[0] == r
    nrb = r // tr

    def body(pos_ref, w_ref, o_ref):
        o_ref[...] = w_ref[...].astype(BF16)

    out_map = (lambda i, pos: (i, pos[0])) if by_cols else (lambda i, pos: (pos[0] * nrb + i, 0))
    return pl.pallas_call(
        body, name=name,
        grid_spec=pltpu.PrefetchScalarGridSpec(
            num_scalar_prefetch=1, grid=(nrb,),
            in_specs=[pl.BlockSpec((tr, n), lambda i, pos: (i, 0))], out_specs=pl.BlockSpec((tr, n), out_map)),
        out_shape=jax.ShapeDtypeStruct(full_shape, BF16), compiler_params=_params("parallel"),
    )(pos, w)


def _proj_call(x, shift, scale, norm_g, w_bf):
    s = x.shape[0]
    tm = min(s, 1024)
    tn = 512

    def body(x_ref, sh_ref, sc_ref, g_ref, w_ref, proj_ref, h_ref):
        @pl.when(pl.program_id(1) == 0)
        def _():
            xv = x_ref[...]
            r = lax.rsqrt(jnp.mean(xv * xv, axis=-1, keepdims=True) + EPS)
            h_ref[...] = ((xv * r * g_ref[...]) * (1.0 + sc_ref[...]) + sh_ref[...]).astype(BF16)

        proj_ref[...] = jnp.dot(h_ref[...], w_ref[...], preferred_element_type=F32)

    vec = pl.BlockSpec((1, D), lambda i, j: (0, 0))
    return pl.pallas_call(
        body, name="proj", grid=(s // tm, D_IN // tn),
        in_specs=[pl.BlockSpec((tm, D), lambda i, j: (i, 0)), vec, vec, vec, pl.BlockSpec((D, tn), lambda i, j: (0, j))],
        out_specs=[pl.BlockSpec((tm, tn), lambda i, j: (i, j)), pl.BlockSpec((tm, D), lambda i, j: (i, 0))],
        out_shape=[jax.ShapeDtypeStruct((s, D_IN), F32), jax.ShapeDtypeStruct((s, D), BF16)],
        compiler_params=_params("parallel", "arbitrary"),
    )(x, shift, scale, norm_g, w_bf)


def _proj_gather_call(pos, x, shift, scale, norm_g, wi_full, wo_full):
    s = x.shape[0]
    tm = min(s, 512)
    nrow = s // tm
    hi = D // 2
    ho = W_OUT_SHARD // 2

    def body(pos_ref, x_ref, sh_ref, sc_ref, g_ref, wi_in, wo_in, proj_ref, h_ref, fi_ref, fo_ref,
             h_all, wbuf, send_sems, recv_sems, load_sem):
        del wi_in, wo_in
        p = pl.program_id(0)
        i = pl.program_id(1)
        x_, y_, c_ = _coords()
        me, sibling = (x_, y_, c_), (x_, y_, 1 - c_)

        def shard_of(q):
            px, py, _ = _peer(x_, y_, c_, q, 0)
            return 2 * px + py

        def part(which, q, pc, sub=None):
            n = hi if which == 0 else ho
            base = pc * n
            if sub is not None:
                n //= 2
                base = base + sub * n
            if which == 0:
                return fi_ref.at[pl.ds(base, n), pl.ds(shard_of(q) * W_IN_SHARD, W_IN_SHARD)]
            return fo_ref.at[pl.ds(shard_of(q) * W_OUT_SHARD + base, n), :]

        def copy(k, which, q, pc, to, sub=None):
            ref = part(which, q, pc, sub)
            return pltpu.make_async_remote_copy(src_ref=ref, dst_ref=ref, send_sem=send_sems.at[k], recv_sem=recv_sems.at[k],
                                                device_id=to, device_id_type=MESH)

        def to_neighbour(which, q):
            return copy(8 * which + q - 1, which, 0, c_, _peer(x_, y_, c_, q, 0))

        def from_neighbour(which, q):
            return copy(8 * which + q - 1, which, q, c_, me)

        def relay(which, q):
            return copy(8 * which + 2 + q - 1, which, q, c_, _peer(x_, y_, c_, 3 - q, 0), q - 1)

        def relayed(which, sub):
            return copy(8 * which + 2 + sub, which, 3, c_, me, sub)

        def to_sibling(which, q):
            return copy(8 * which + 4 + q - 1, which, q, c_, sibling)

        def from_sibling(which, q):
            return copy(8 * which + 4 + q - 1, which, q, 1 - c_, me)

        def relayed_to_sibling(which, sub):
            return copy(8 * which + 6 + sub, which, 3, c_, sibling, sub)

        def relayed_from_sibling(which, sub):
            return copy(8 * which + 6 + sub, which, 3, 1 - c_, me, sub)

        def pass_on_neighbours(which):
            for q in (1, 2):
                from_neighbour(which, q).wait_recv()
                to_sibling(which, q).start()
                relay(which, q).start()

        def pass_on_relayed(which):
            for sub in range(2):
                relayed(which, sub).wait_recv()
                relayed_to_sibling(which, sub).start()

        def load_shard(q):
            cp = pltpu.make_async_copy(fi_ref.at[:, pl.ds(shard_of(q) * W_IN_SHARD, W_IN_SHARD)], wbuf, load_sem)
            cp.start()
            cp.wait()

        @pl.when((p == 0) & (i == 0))
        def _():
            for q in (1, 2):
                to_neighbour(0, q).start()
            load_shard(0)

        @pl.when((p == 1) & (i == 0))
        def _():
            pass_on_neighbours(0)
            for q in (1, 2):
                to_neighbour(1, q).start()
            from_sibling(0, 1).wait_recv()
            load_shard(1)

        @pl.when((p == 2) & (i == 0))
        def _():
            from_sibling(0, 2).wait_recv()
            load_shard(2)

        @pl.when((p == 3) & (i == 0))
        def _():
            pass_on_relayed(0)
            pass_on_neighbours(1)
            for sub in range(2):
                relayed_from_sibling(0, sub).wait_recv()
            load_shard(3)

        rows = pl.ds(pl.multiple_of(i * tm, tm), tm)

        @pl.when(p == 0)
        def _():
            xv = x_ref[...]
            r = lax.rsqrt(jnp.mean(xv * xv, axis=-1, keepdims=True) + EPS)
            hv = ((xv * r * g_ref[...]) * (1.0 + sc_ref[...]) + sh_ref[...]).astype(BF16)
            h_ref[...] = hv
            h_all[rows, :] = hv

        proj_ref[...] = jnp.dot(h_all[rows, :], wbuf[...], preferred_element_type=F32)

        @pl.when((p == N_CHIP - 1) & (i == nrow - 1))
        def _():
            pass_on_relayed(1)
            for q in (1, 2):
                from_sibling(1, q).wait_recv()
            for sub in range(2):
                relayed_from_sibling(1, sub).wait_recv()
            for which in range(2):
                for q in (1, 2):
                    to_neighbour(which, q).wait_send()
                    relay(which, q).wait_send()
                    to_sibling(which, q).wait_send()
                    relayed_to_sibling(which, q - 1).wait_send()

    vec = pl.BlockSpec((1, D), lambda p, i, pos: (0, 0))
    first_phase_rows = lambda p, i, pos: (jnp.where(p == 0, i, nrow - 1), 0)
    anyspec = pl.BlockSpec(memory_space=pl.ANY)
    return pl.pallas_call(
        body, name="proj_gather",
        grid_spec=pltpu.PrefetchScalarGridSpec(
            num_scalar_prefetch=1, grid=(N_CHIP, nrow),
            in_specs=[pl.BlockSpec((tm, D), first_phase_rows), vec, vec, vec, anyspec, anyspec],
            out_specs=[pl.BlockSpec((tm, W_IN_SHARD), lambda p, i, pos: (i, jnp.bitwise_xor(pos[0], p))),
                       pl.BlockSpec((tm, D), first_phase_rows), anyspec, anyspec],
            scratch_shapes=[pltpu.VMEM((s, D), BF16), pltpu.VMEM((D, W_IN_SHARD), BF16),
                            pltpu.SemaphoreType.DMA((16,)), pltpu.SemaphoreType.DMA((16,)), pltpu.SemaphoreType.DMA]),
        out_shape=[jax.ShapeDtypeStruct((s, D_IN), F32), jax.ShapeDtypeStruct((s, D), BF16),
                   jax.ShapeDtypeStruct((D, D_IN), BF16), jax.ShapeDtypeStruct((D, D), BF16)],
        input_output_aliases={5: 2, 6: 3},
        compiler_params=_params("arbitrary", "arbitrary"),
    )(pos, x, shift, scale, norm_g, wi_full, wo_full)


def _proj_specs(rev_nb=None):
    if rev_nb is None:
        row = lambda i: i
    else:
        row = lambda i: rev_nb - 1 - i
    wide = lambda col: pl.BlockSpec((BLK, D_A), lambda i: (row(i), col))
    kv = lambda col: pl.BlockSpec((BLK, D_KV), lambda i: (row(i), col))
    half = lambda col: pl.BlockSpec((BLK, 512), lambda i: (row(i), col))
    return [wide(0), wide(1), wide(2), wide(3), kv(OFF_K // D_KV), kv(OFF_V // D_KV), half(OFF_ZB // 512), half(OFF_ZB // 512 + 1)]


def _mix_fwd_call(proj, cos, sin, ln_g, ln_b, w_sp, b_sp_t, sinks):
    s = proj.shape[0]
    nb = s // BLK

    def body(ua_ref, va_ref, za_ref, q_ref, k_ref, v_ref, zb0_ref, zb1_ref, cos_ref, sin_ref, lg_ref, lb_ref,
             w_ref, bt_ref, sinks_ref, y_ref, kdup_ref, vdup_ref, qm_ref, ost_ref):
        i = pl.program_id(0)
        first_half, lo = _lane_masks()
        cos_t = cos_ref[...]
        sin_t = sin_ref[...]

        _, _, vln = _layer_norm_fwd(va_ref[...], lg_ref[...], lb_ref[...])
        tril = _tril()
        for g in range(GROUPS):
            cols = slice(g * BLK, (g + 1) * BLK)
            wg = jnp.where(tril, w_ref[g], 0.0).astype(BF16)
            sg = jnp.dot(wg, vln[:, cols].astype(BF16), preferred_element_type=F32) + bt_ref[:, g:g + 1]
            gate, _ = _silu_parts(za_ref[:, cols])
            y_ref[:, cols] = (ua_ref[:, cols] * sg * gate).astype(BF16)

        @pl.when(i == 0)
        def _():
            kdup_ref[:, 0:BLK, :] = jnp.zeros((N_KV, BLK, LANE), BF16)
            vdup_ref[:, 0:BLK, :] = jnp.zeros((N_KV, BLK, LANE), BF16)

        @pl.when(i > 0)
        def _():
            kdup_ref[:, 0:BLK, :] = kdup_ref[:, BLK:2 * BLK, :]
            vdup_ref[:, 0:BLK, :] = vdup_ref[:, BLK:2 * BLK, :]

        for ks in range(2):
            cols = slice(ks * LANE, (ks + 1) * LANE)
            kslab = k_ref[:, cols]
            kr = kslab * cos_t + _rot_half(kslab, first_half) * sin_t
            for n, (kd, vd) in enumerate(zip(_dup_kv(kr, lo), _dup_kv(v_ref[:, cols], lo))):
                kdup_ref[2 * ks + n, BLK:2 * BLK, :] = kd
                vdup_ref[2 * ks + n, BLK:2 * BLK, :] = vd
        for sb in range(8):
            qslab = q_ref[:, sb * LANE:(sb + 1) * LANE]
            _stack_heads(qm_ref, sb, qslab * cos_t + _rot_half(qslab, first_half) * sin_t, lo, BF16)

        valid = _band_valid(jnp.where(i > 0, 0, BLK), Q_PER_KV * BLK)

        def kv_head(kh, carry):
            probs, _ = _softmax_sink(qm_ref[kh], kdup_ref[kh], valid, _sink_column(sinks_ref, kh))
            ost_ref[kh] = jnp.dot(probs.astype(BF16), vdup_ref[kh], preferred_element_type=F32)
            return carry

        lax.fori_loop(0, N_KV, kv_head, 0, unroll=2)
        for sb in range(8):
            cols = slice(sb * LANE, (sb + 1) * LANE)
            zb = zb0_ref[:, cols] if sb < 4 else zb1_ref[:, (sb - 4) * LANE:(sb - 3) * LANE]
            gate, _ = _silu_parts(zb)
            y_ref[:, D_A + sb * LANE:D_A + (sb + 1) * LANE] = (_unstack_heads(ost_ref, sb, lo) * gate).astype(BF16)

    tab = pl.BlockSpec((BLK, LANE), lambda i: (i, 0))
    return pl.pallas_call(
        body, name="mix_fwd", grid=(nb,),
        in_specs=_proj_specs() + [
            tab, tab, pl.BlockSpec((1, D_A), lambda i: (0, 0)), pl.BlockSpec((1, D_A), lambda i: (0, 0)),
            pl.BlockSpec((GROUPS, BLK, BLK), lambda i: (0, 0, 0)), pl.BlockSpec((BLK, GROUPS), lambda i: (0, 0)),
            pl.BlockSpec(memory_space=pltpu.SMEM)],
        out_specs=pl.BlockSpec((BLK, 2 * D_A), lambda i: (i, 0)),
        out_shape=jax.ShapeDtypeStruct((s, 2 * D_A), BF16),
        scratch_shapes=[pltpu.VMEM((N_KV, 2 * BLK, LANE), BF16), pltpu.VMEM((N_KV, 2 * BLK, LANE), BF16),
                        pltpu.VMEM((N_KV, Q_PER_KV * BLK, LANE), BF16), pltpu.VMEM((N_KV, Q_PER_KV * BLK, LANE), F32)],
        compiler_params=_params("arbitrary"),
    )(proj, proj, proj, proj, proj, proj, proj, proj, cos, sin, ln_g, ln_b, w_sp, b_sp_t, sinks)


def _tail_call(y, w_out_bf, x, target, gate, shift_f, scale_f, gf):
    s = x.shape[0]
    tm = min(s, 256)
    nsteps = s // tm

    def body(y_ref, w_ref, x_ref, t_ref, gate_ref, shf_ref, scf_ref, gf_ref, dx2_ref, do_ref, st_ref):
        i = pl.program_id(0)

        @pl.when(i == 0)
        def _():
            st_ref[...] = jnp.zeros((8, D), F32)

        o = jnp.dot(y_ref[...], w_ref[...], preferred_element_type=F32)
        gate_v = gate_ref[...]
        x2 = x_ref[...] + gate_v * o
        r2 = lax.rsqrt(jnp.mean(x2 * x2, axis=-1, keepdims=True) + EPS)
        xn2 = x2 * r2
        hn2 = xn2 * gf_ref[...]
        one_sc = 1.0 + scf_ref[...]
        err = hn2 * one_sc + shf_ref[...] - t_ref[...]
        dout = err * (1.0 / D)
        dhn2 = dout * one_sc
        dxn2 = dhn2 * gf_ref[...]
        dx2 = r2 * (dxn2 - xn2 * jnp.mean(dxn2 * xn2, axis=-1, keepdims=True))
        dx2_ref[...] = dx2
        do_ref[...] = (dx2 * gate_v).astype(BF16)
        st_ref[0:1, :] += jnp.sum(dout, axis=0, keepdims=True)
        st_ref[1:2, :] += jnp.sum(dout * hn2, axis=0, keepdims=True)
        st_ref[2:3, :] += jnp.sum(dhn2 * xn2, axis=0, keepdims=True)
        st_ref[3:4, :] += jnp.sum(dx2 * o, axis=0, keepdims=True)
        st_ref[4:5, :] += jnp.sum(err * err, axis=0, keepdims=True)

        @pl.when(i == nsteps - 1)
        def _():
            st_ref[5:6, :] = jnp.full((1, D), 0.5 / D, F32) * jnp.sum(st_ref[4:5, :])

    vec = pl.BlockSpec((1, D), lambda i: (0, 0))
    rows = lambda: pl.BlockSpec((tm, D), lambda i: (i, 0))
    return pl.pallas_call(
        body, name="tail", grid=(nsteps,),
        in_specs=[rows(), pl.BlockSpec((D, D), lambda i: (0, 0)), rows(), rows(), vec, vec, vec, vec],
        out_specs=[rows(), rows(), pl.BlockSpec((8, D), lambda i: (0, 0))],
        out_shape=[jax.ShapeDtypeStruct((s, D), F32), jax.ShapeDtypeStruct((s, D), BF16), jax.ShapeDtypeStruct((8, D), F32)],
        compiler_params=_params("arbitrary"),
    )(y, w_out_bf, x, target, gate, shift_f, scale_f, gf)


def _dy_call(do, w_out_bf):
    s = do.shape[0]
    tm = min(s, 512)

    def body(do_ref, w_ref, dy_ref):
        dy_ref[...] = lax.dot_general(do_ref[...], w_ref[...], NT, preferred_element_type=F32)

    return pl.pallas_call(
        body, name="dy", grid=(s // tm,),
        in_specs=[pl.BlockSpec((tm, D), lambda i: (i, 0)), pl.BlockSpec((D, D), lambda i: (0, 0))],
        out_specs=pl.BlockSpec((tm, D), lambda i: (i, 0)),
        out_shape=jax.ShapeDtypeStruct((s, D), F32), compiler_params=_params("parallel"),
    )(do, w_out_bf)


def _tn_call(a, b, name):
    s, m = a.shape
    n = b.shape[1]
    tn = 512
    ts = min(s, 1024)
    nk = s // ts

    def body(a_ref, b_ref, o_ref, acc_ref):
        k = pl.program_id(1)

        @pl.when(k == 0)
        def _():
            acc_ref[...] = jnp.zeros((m, tn), F32)

        acc_ref[...] += lax.dot_general(a_ref[...], b_ref[...], TN, preferred_element_type=F32)

        @pl.when(k == nk - 1)
        def _():
            o_ref[...] = acc_ref[...].astype(BF16)

    return pl.pallas_call(
        body, name=name, grid=(n // tn, nk),
        in_specs=[pl.BlockSpec((ts, m), lambda j, k: (k, 0)), pl.BlockSpec((ts, tn), lambda j, k: (k, j))],
        out_specs=pl.BlockSpec((m, tn), lambda j, k: (0, j)),
        out_shape=jax.ShapeDtypeStruct((m, n), BF16),
        scratch_shapes=[pltpu.VMEM((m, tn), F32)],
        compiler_params=_params("parallel", "arbitrary"),
    )(a, b)


def _tn_shards_call(pos, a, b, qs, name):
    s, m = a.shape
    ts = min(s, 1024)
    nk = s // ts

    def body(pos_ref, a_ref, b_ref, o_ref, acc_ref):
        k = pl.program_id(1)

        @pl.when(k == 0)
        def _():
            acc_ref[...] = jnp.zeros((m, W_IN_SHARD), F32)

        acc_ref[...] += lax.dot_general(a_ref[...], b_ref[...], TN, preferred_element_type=F32)

        @pl.when(k == nk - 1)
        def _():
            o_ref[...] = acc_ref[...].astype(BF16)

    def shard(j, pos):
        q = qs[0]
        for n in range(1, len(qs)):
            q = jnp.where(j == n, qs[n], q)
        return jnp.bitwise_xor(pos[0], q)

    return pl.pallas_call(
        body, name=name,
        grid_spec=pltpu.PrefetchScalarGridSpec(
            num_scalar_prefetch=1, grid=(len(qs), nk),
            in_specs=[pl.BlockSpec((ts, m), lambda j, k, pos: (k, 0)),
                      pl.BlockSpec((ts, W_IN_SHARD), lambda j, k, pos: (k, shard(j, pos)))],
            out_specs=pl.BlockSpec((m, W_IN_SHARD), lambda j, k, pos: (0, j)),
            scratch_shapes=[pltpu.VMEM((m, W_IN_SHARD), F32)]),
        out_shape=jax.ShapeDtypeStruct((m, len(qs) * W_IN_SHARD), BF16),
        compiler_params=_params("parallel", "arbitrary"),
    )(pos, a, b)


def _mix_bwd_call(proj, dy, cos, sin, ln_g, ln_b, w_sp, w_sp_t, b_sp_t, sinks):
    s = proj.shape[0]
    nb = s // BLK
    rev = lambda i: nb - 1 - i
    prev = lambda i: jnp.maximum(nb - 2 - i, 0)

    def body(ua_ref, va_ref, za_ref, q_ref, k_ref, v_ref, zb0_ref, zb1_ref, kp_ref, vp_ref, dy_ref,
             cos_ref, sin_ref, cosp_ref, sinp_ref, lg_ref, lb_ref, w_ref, wt_ref, bt_ref, sinks_ref,
             dp_ref, lnst_ref, dw_ref, dbt_ref, dsink_ref,
             kdup_ref, vdup_ref, dvln_ref, qm_ref, dom_ref, ost_ref, dqst_ref, dkdup_ref, dvdup_ref, kcar_ref, vcar_ref):
        i = pl.program_id(0)
        first_half, lo = _lane_masks()
        lane8 = lax.broadcasted_iota(jnp.int32, (8, LANE), 1)
        cos_t = cos_ref[...]
        sin_t = sin_ref[...]

        @pl.when(i == 0)
        def _():
            lnst_ref[...] = jnp.zeros((8, D_A), F32)
            dw_ref[...] = jnp.zeros((GROUPS, BLK, BLK), F32)
            dbt_ref[...] = jnp.zeros((BLK, LANE), F32)
            dsink_ref[...] = jnp.zeros((8, LANE), F32)
            kcar_ref[...] = jnp.zeros((BLK, D_KV), F32)
            vcar_ref[...] = jnp.zeros((BLK, D_KV), F32)

        vhat, rstd, vln = _layer_norm_fwd(va_ref[...], lg_ref[...], lb_ref[...])
        tril = _tril()
        triu = jnp.logical_not(tril) | (lax.broadcasted_iota(jnp.int32, (BLK, BLK), 0) == lax.broadcasted_iota(jnp.int32, (BLK, BLK), 1))
        lane_b = lax.broadcasted_iota(jnp.int32, (BLK, LANE), 1)
        db_acc = jnp.zeros((BLK, LANE), F32)
        for g in range(GROUPS):
            cols = slice(g * BLK, (g + 1) * BLK)
            vln_g = vln[:, cols].astype(BF16)
            wg = jnp.where(tril, w_ref[g], 0.0).astype(BF16)
            sg = jnp.dot(wg, vln_g, preferred_element_type=F32) + bt_ref[:, g:g + 1]
            za = za_ref[:, cols]
            gate, sig = _silu_parts(za)
            ua = ua_ref[:, cols]
            dya_g = dy_ref[:, cols]
            dya = dya_g * gate
            dp_ref[:, cols] = (dya * sg).astype(BF16)
            dp_ref[:, 2 * D_A + g * BLK:2 * D_A + (g + 1) * BLK] = (
                dya_g * (ua * sg) * (sig * (1.0 + za * (1.0 - sig)))).astype(BF16)
            ds = dya * ua
            ds_b = ds.astype(BF16)
            wtg = jnp.where(triu, wt_ref[g], 0.0).astype(BF16)
            dvln_ref[:, cols] = jnp.dot(wtg, ds_b, preferred_element_type=F32)
            dw_ref[g] += jnp.where(tril, lax.dot_general(ds_b, vln_g, NT, preferred_element_type=F32), 0.0)
            db_acc = db_acc + jnp.where(lane_b == g, jnp.sum(ds, axis=-1, keepdims=True), 0.0)
        dbt_ref[...] += db_acc
        dvln = dvln_ref[...]
        lnst_ref[0:1, :] += jnp.sum(dvln * vhat, axis=0, keepdims=True)
        lnst_ref[1:2, :] += jnp.sum(dvln, axis=0, keepdims=True)
        dvhat = dvln * lg_ref[...]
        m1 = jnp.mean(dvhat, axis=-1, keepdims=True)
        m2 = jnp.mean(dvhat * vhat, axis=-1, keepdims=True)
        dp_ref[:, D_A:2 * D_A] = (rstd * (dvhat - m1 - vhat * m2)).astype(BF16)

        cosp = cosp_ref[...]
        sinp = sinp_ref[...]
        for ks in range(2):
            cols = slice(ks * LANE, (ks + 1) * LANE)
            kslab = k_ref[:, cols]
            kr = kslab * cos_t + _rot_half(kslab, first_half) * sin_t
            kpslab = kp_ref[:, cols]
            kpr = kpslab * cosp + _rot_half(kpslab, first_half) * sinp
            for n, (kc, vc, kp, vp) in enumerate(zip(_dup_kv(kr, lo), _dup_kv(v_ref[:, cols], lo),
                                                     _dup_kv(kpr, lo), _dup_kv(vp_ref[:, cols], lo))):
                kdup_ref[2 * ks + n, BLK:2 * BLK, :] = kc
                vdup_ref[2 * ks + n, BLK:2 * BLK, :] = vc
                kdup_ref[2 * ks + n, 0:BLK, :] = kp
                vdup_ref[2 * ks + n, 0:BLK, :] = vp
        for sb in range(8):
            cols = slice(sb * LANE, (sb + 1) * LANE)
            qslab = q_ref[:, cols]
            _stack_heads(qm_ref, sb, qslab * cos_t + _rot_half(qslab, first_half) * sin_t, lo, BF16)
            zb = zb0_ref[:, cols] if sb < 4 else zb1_ref[:, (sb - 4) * LANE:(sb - 3) * LANE]
            gate, _ = _silu_parts(zb)
            _stack_heads(dom_ref, sb, dy_ref[:, D_A + sb * LANE:D_A + (sb + 1) * LANE] * gate, lo, F32)

        valid = _band_valid(jnp.where(i < nb - 1, 0, BLK), Q_PER_KV * BLK)

        def kv_head(kh, dsink_acc):
            qm = qm_ref[kh]
            kd = kdup_ref[kh]
            vd = vdup_ref[kh]
            probs, psink = _softmax_sink(qm, kd, valid, _sink_column(sinks_ref, kh))
            probs_b = probs.astype(BF16)
            o = jnp.dot(probs_b, vd, preferred_element_type=F32)
            ost_ref[kh] = o
            dom = dom_ref[kh]
            dom_b = dom.astype(BF16)
            delta = jnp.sum(dom * o, axis=-1, keepdims=True)
            dpr = lax.dot_general(dom_b, vd, NT, preferred_element_type=F32)
            dss = (probs * (dpr - delta) * SCALE).astype(BF16)
            sd = psink * delta
            for n in range(Q_PER_KV):
                dsink_acc = dsink_acc + jnp.where(lane8 == Q_PER_KV * kh + n, -jnp.sum(sd[n * BLK:(n + 1) * BLK]), 0.0)
            dqst_ref[kh] = jnp.dot(dss, kd, preferred_element_type=F32)
            dkdup_ref[kh] = lax.dot_general(dss, qm, TN, preferred_element_type=F32)
            dvdup_ref[kh] = lax.dot_general(probs_b, dom_b, TN, preferred_element_type=F32)
            return dsink_acc

        dsink_acc = lax.fori_loop(0, N_KV // 2, lambda j, acc: kv_head(2 * j + 1, kv_head(2 * j, acc)), jnp.zeros((8, LANE), F32))
        row0 = lax.broadcasted_iota(jnp.int32, (8, LANE), 0) == 0
        dsink_ref[...] += jnp.where(row0, dsink_acc, 0.0)

        for sb in range(8):
            cols = slice(sb * LANE, (sb + 1) * LANE)
            zb = zb0_ref[:, cols] if sb < 4 else zb1_ref[:, (sb - 4) * LANE:(sb - 3) * LANE]
            _, sig = _silu_parts(zb)
            dyb = dy_ref[:, D_A + sb * LANE:D_A + (sb + 1) * LANE]
            dp_ref[:, OFF_ZB + sb * LANE:OFF_ZB + (sb + 1) * LANE] = (
                dyb * _unstack_heads(ost_ref, sb, lo) * (sig * (1.0 + zb * (1.0 - sig)))).astype(BF16)
            dq_r = _unstack_heads(dqst_ref, sb, lo)
            dp_ref[:, OFF_Q + sb * LANE:OFF_Q + (sb + 1) * LANE] = (
                dq_r * cos_t - _rot_half(dq_r * sin_t, first_half)).astype(BF16)

        lo2 = lax.broadcasted_iota(jnp.int32, (2 * BLK, LANE), 1) < HEAD
        for ks in range(2):
            cols = slice(ks * LANE, (ks + 1) * LANE)
            ka = dkdup_ref[2 * ks]
            kb = dkdup_ref[2 * ks + 1]
            dk_band = jnp.where(lo2, ka + pltpu.roll(ka, HEAD, 1), kb + pltpu.roll(kb, HEAD, 1))
            va_ = dvdup_ref[2 * ks]
            vb_ = dvdup_ref[2 * ks + 1]
            dv_band = jnp.where(lo2, va_ + pltpu.roll(va_, HEAD, 1), vb_ + pltpu.roll(vb_, HEAD, 1))
            dkr = dk_band[BLK:2 * BLK, :] + kcar_ref[:, cols]
            dp_ref[:, OFF_K + ks * LANE:OFF_K + (ks + 1) * LANE] = (
                dkr * cos_t - _rot_half(dkr * sin_t, first_half)).astype(BF16)
            dp_ref[:, OFF_V + ks * LANE:OFF_V + (ks + 1) * LANE] = (
                dv_band[BLK:2 * BLK, :] + vcar_ref[:, cols]).astype(BF16)
            kcar_ref[:, cols] = dk_band[0:BLK, :]
            vcar_ref[:, cols] = dv_band[0:BLK, :]

    tab = pl.BlockSpec((BLK, LANE), lambda i: (rev(i), 0))
    tabp = pl.BlockSpec((BLK, LANE), lambda i: (prev(i), 0))
    kvp = lambda col: pl.BlockSpec((BLK, D_KV), lambda i: (prev(i), col))
    vec = pl.BlockSpec((1, D_A), lambda i: (0, 0))
    w3 = pl.BlockSpec((GROUPS, BLK, BLK), lambda i: (0, 0, 0))
    return pl.pallas_call(
        body, name="mix_bwd", grid=(nb,),
        in_specs=_proj_specs(nb) + [
            kvp(OFF_K // D_KV), kvp(OFF_V // D_KV), pl.BlockSpec((BLK, 2 * D_A), lambda i: (rev(i), 0)),
            tab, tab, tabp, tabp, vec, vec, w3, w3, pl.BlockSpec((BLK, GROUPS), lambda i: (0, 0)),
            pl.BlockSpec(memory_space=pltpu.SMEM)],
        out_specs=[pl.BlockSpec((BLK, D_IN), lambda i: (rev(i), 0)), pl.BlockSpec((8, D_A), lambda i: (0, 0)), w3,
                   pl.BlockSpec((BLK, LANE), lambda i: (0, 0)), pl.BlockSpec((8, LANE), lambda i: (0, 0))],
        out_shape=[jax.ShapeDtypeStruct((s, D_IN), BF16), jax.ShapeDtypeStruct((8, D_A), F32),
                   jax.ShapeDtypeStruct((GROUPS, BLK, BLK), F32), jax.ShapeDtypeStruct((BLK, LANE), F32),
                   jax.ShapeDtypeStruct((8, LANE), F32)],
        scratch_shapes=[pltpu.VMEM((N_KV, 2 * BLK, LANE), BF16), pltpu.VMEM((N_KV, 2 * BLK, LANE), BF16),
                        pltpu.VMEM((BLK, D_A), F32), pltpu.VMEM((N_KV, Q_PER_KV * BLK, LANE), BF16),
                        pltpu.VMEM((N_KV, Q_PER_KV * BLK, LANE), F32), pltpu.VMEM((N_KV, Q_PER_KV * BLK, LANE), F32),
                        pltpu.VMEM((N_KV, Q_PER_KV * BLK, LANE), F32), pltpu.VMEM((N_KV, 2 * BLK, LANE), F32),
                        pltpu.VMEM((N_KV, 2 * BLK, LANE), F32), pltpu.VMEM((BLK, D_KV), F32), pltpu.VMEM((BLK, D_KV), F32)],
        compiler_params=_params("arbitrary"),
    )(proj, proj, proj, proj, proj, proj, proj, proj, proj, proj, dy, cos, sin, cos, sin, ln_g, ln_b, w_sp, w_sp_t,
      b_sp_t, sinks)


def _dh_call(dproj, w_bf, x, dx2, scale, norm_g):
    s = x.shape[0]
    tm = min(s, 512)
    tk = W_IN_SHARD
    nk = D_IN // tk

    def body(dp_ref, w_ref, x_ref, dx2_ref, sc_ref, g_ref, gx_ref, st_ref, acc_ref):
        i = pl.program_id(0)
        k = pl.program_id(1)

        @pl.when((i == 0) & (k == 0))
        def _():
            st_ref[...] = jnp.zeros((8, D), F32)

        @pl.when(k == 0)
        def _():
            acc_ref[...] = jnp.zeros((tm, D), F32)

        acc_ref[...] += lax.dot_general(dp_ref[...], w_ref[...], NT, preferred_element_type=F32)

        @pl.when(k == nk - 1)
        def _():
            g = g_ref[...]
            one_sc = 1.0 + sc_ref[...]

            def chunk(n, carry):
                rows = pl.ds(pl.multiple_of(n * BLK, BLK), BLK)
                dh = acc_ref[rows, :]
                xv = x_ref[rows, :]
                r = lax.rsqrt(jnp.mean(xv * xv, axis=-1, keepdims=True) + EPS)
                xn = xv * r
                dhn = dh * one_sc
                dxn = dhn * g
                gx_ref[rows, :] = dx2_ref[rows, :] + r * (dxn - xn * jnp.mean(dxn * xn, axis=-1, keepdims=True))
                st_ref[0:1, :] += jnp.sum(dh, axis=0, keepdims=True)
                st_ref[1:2, :] += jnp.sum(dh * (xn * g), axis=0, keepdims=True)
                st_ref[2:3, :] += jnp.sum(dhn * xn, axis=0, keepdims=True)
                return carry

            lax.fori_loop(0, tm // BLK, chunk, 0)

    vec = pl.BlockSpec((1, D), lambda i, k: (0, 0))
    rows = lambda: pl.BlockSpec((tm, D), lambda i, k: (i, 0))
    return pl.pallas_call(
        body, name="dh", grid=(s // tm, nk),
        in_specs=[pl.BlockSpec((tm, tk), lambda i, k: (i, k)), pl.BlockSpec((D, tk), lambda i, k: (0, k)), rows(), rows(), vec, vec],
        out_specs=[rows(), pl.BlockSpec((8, D), lambda i, k: (0, 0))],
        out_shape=[jax.ShapeDtypeStruct((s, D), F32), jax.ShapeDtypeStruct((8, D), F32)],
        scratch_shapes=[pltpu.VMEM((tm, D), F32)],
        compiler_params=_params("arbitrary", "arbitrary"),
    )(dproj, w_bf, x, dx2, scale, norm_g)


def _adam_math(w, g, m, v):
    m_new = ADAM_B1 * m + (1.0 - ADAM_B1) * g
    v_new = ADAM_B2 * v + (1.0 - ADAM_B2) * (g * g)
    m_hat = m_new / ADAM_C1
    v_hat = v_new / ADAM_C2
    delta = -ADAM_LR * (m_hat / (jnp.sqrt(v_hat) + ADAM_EPS) + ADAM_WD * w)
    return delta, m_new, v_new


def _adam_small_call(tensors):
    n = len(tensors)

    def body(*refs):
        ins, outs = refs[:4 * n], refs[4 * n:]
        for t in range(n):
            w_ref, g_ref, m_ref, v_ref = ins[4 * t:4 * t + 4]
            d, mo, vo = _adam_math(w_ref[...], g_ref[...], m_ref[...], v_ref[...])
            outs[3 * t][...], outs[3 * t + 1][...], outs[3 * t + 2][...] = d, mo, vo

    vm = pl.BlockSpec(memory_space=pltpu.VMEM)
    flat = [a for t in tensors for a in t]
    out = pl.pallas_call(
        body, name="adam_small", in_specs=[vm] * (4 * n), out_specs=[vm] * (3 * n),
        out_shape=[jax.ShapeDtypeStruct(t[0].shape, F32) for t in tensors for _ in range(3)],
        compiler_params=pltpu.CompilerParams(vmem_limit_bytes=VMEM_LIMIT),
    )(*flat)
    return [tuple(out[3 * t:3 * t + 3]) for t in range(n)]


def _adam_halves_call(pos, w, mine, theirs, m, v, name):
    r, n = w.shape
    half = r // 2
    tr = 128
    nh = half // tr

    def body(pos_ref, w_ref, mine_ref, theirs_ref, m_ref, v_ref, g_ref, d_ref, mo_ref, vo_ref):
        is_mine = (pl.program_id(0) // nh) == pos_ref[1]
        g = jnp.where(is_mine, mine_ref[...], theirs_ref[...])
        g_ref[...] = g
        d_ref[...], mo_ref[...], vo_ref[...] = _adam_math(w_ref[...], g, m_ref[...], v_ref[...])

    spec = lambda: pl.BlockSpec((tr, n), lambda i, pos: (i, 0))
    hspec = lambda: pl.BlockSpec((tr, n), lambda i, pos: (i % nh, 0))
    return pl.pallas_call(
        body, name=name,
        grid_spec=pltpu.PrefetchScalarGridSpec(
            num_scalar_prefetch=1, grid=(r // tr,), in_specs=[spec(), hspec(), hspec(), spec(), spec()],
            out_specs=[spec() for _ in range(4)]),
        out_shape=[jax.ShapeDtypeStruct((r, n), F32)] * 4, compiler_params=_params("parallel"),
    )(pos, w, mine, theirs, m, v)


def _adam_outer_call(w, ct, dm, m, v, name):
    r, n = w.shape
    tr = 128

    def body(w_ref, ct_ref, dm_ref, m_ref, v_ref, g_ref, d_ref, mo_ref, vo_ref):
        g = ct_ref[:, 0:1] * dm_ref[0:1, :]
        for b in range(1, N_DEV):
            g = g + ct_ref[:, b:b + 1] * dm_ref[b:b + 1, :]
        g_ref[...] = g
        d_ref[...], mo_ref[...], vo_ref[...] = _adam_math(w_ref[...], g, m_ref[...], v_ref[...])

    spec = lambda: pl.BlockSpec((tr, n), lambda i: (i, 0))
    return pl.pallas_call(
        body, name=name, grid=(r // tr,),
        in_specs=[spec(), pl.BlockSpec((tr, N_DEV), lambda i: (i, 0)), pl.BlockSpec((N_DEV, n), lambda i: (0, 0)), spec(), spec()],
        out_specs=[spec() for _ in range(4)],
        out_shape=[jax.ShapeDtypeStruct((r, n), F32)] * 4, compiler_params=_params("parallel"),
    )(w, ct, dm, m, v)


def _sum_pieces_call(pos, part, part_block, recvs, name):
    r, n = recvs[0].shape[1:]
    tr = min(r, 256)
    nrb = r // tr

    def body(pos_ref, p_ref, *refs):
        acc = p_ref[...].astype(F32)
        for r_ref in refs[:-1]:
            for d in range(r_ref.shape[0]):
                acc = acc + r_ref[d].astype(F32)
        refs[-1][...] = acc

    return pl.pallas_call(
        body, name=name,
        grid_spec=pltpu.PrefetchScalarGridSpec(
            num_scalar_prefetch=1, grid=(nrb,),
            in_specs=[pl.BlockSpec((tr, n), lambda i, pos: part_block(i, pos, nrb))] + [
                pl.BlockSpec((rv.shape[0], tr, n), lambda i, pos: (0, i, 0)) for rv in recvs],
            out_specs=pl.BlockSpec((tr, n), lambda i, pos: (i, 0))),
        out_shape=jax.ShapeDtypeStruct((r, n), F32), compiler_params=_params("parallel"),
    )(pos, part, *recvs)


def _coords():
    return lax.axis_index("x"), lax.axis_index("y"), lax.axis_index("c")


def _allgather_sum_call(blk, name, with_sum):
    m_per, n = blk.shape

    def body(x_ref, out_ref, *rest):
        if with_sum:
            sum_ref, send_sems, recv_sems, local_sem = rest
        else:
            send_sems, recv_sems, local_sem = rest
        x, y, c = _coords()
        me, sibling = (x, y, c), (x, y, 1 - c)
        chips = [(1 - x, y), (x, 1 - y), (1 - x, 1 - y)]

        def rows(px, py, pc):
            return out_ref.at[pl.ds((4 * px + 2 * py + pc) * m_per, m_per), :]

        def copy(k, block, to, src=None):
            return pltpu.make_async_remote_copy(
                src_ref=rows(*block) if src is None else src, dst_ref=rows(*block),
                send_sem=send_sems.at[k], recv_sem=recv_sems.at[k], device_id=to, device_id_type=MESH)

        mine = pltpu.make_async_copy(x_ref, rows(*me), local_sem)
        mine.start()
        first = [copy(0, me, sibling, src=x_ref)]
        first += [copy(1 + j, me, (*chip, c), src=x_ref) for j, chip in enumerate(chips)]
        for cp in first:
            cp.start()
        passed = [copy(4 + j, (*chip, c), sibling) for j, chip in enumerate(chips)]
        for j, chip in enumerate(chips):
            copy(1 + j, (*chip, c), me).wait_recv()
            passed[j].start()
        copy(0, sibling, me).wait_recv()
        for j, chip in enumerate(chips):
            copy(4 + j, (*chip, 1 - c), me).wait_recv()
        for cp in first + passed:
            cp.wait_send()
        mine.wait()
        if with_sum:
            acc = out_ref[0:m_per, :]
            for d in range(1, N_DEV):
                acc = acc + out_ref[d * m_per:(d + 1) * m_per, :]
            sum_ref[...] = acc

    vm = pl.BlockSpec(memory_space=pltpu.VMEM)
    out_shape = [jax.ShapeDtypeStruct((N_DEV * m_per, n), F32)]
    if with_sum:
        out_shape.append(jax.ShapeDtypeStruct((m_per, n), F32))
    return pl.pallas_call(
        body, name=name, out_shape=out_shape, in_specs=[vm], out_specs=[vm] * len(out_shape),
        scratch_shapes=[pltpu.SemaphoreType.DMA((7,)), pltpu.SemaphoreType.DMA((7,)), pltpu.SemaphoreType.DMA],
        compiler_params=pltpu.CompilerParams(vmem_limit_bytes=VMEM_LIMIT),
    )(blk)


def _weights_gather_call(wi_full, wo_full):
    hi = D // 2
    ho = W_OUT_SHARD // 2

    def body(wi_in, wo_in, fi_ref, fo_ref, send_sems, recv_sems):
        del wi_in, wo_in
        x, y, c = _coords()
        sibling = (x, y, 1 - c)
        chips = [(1 - x, y), (x, 1 - y), (1 - x, 1 - y)]

        def half(which, px, py, pc):
            j = 2 * px + py
            if which == 0:
                return fi_ref.at[pl.ds(pc * hi, hi), pl.ds(j * W_IN_SHARD, W_IN_SHARD)]
            return fo_ref.at[pl.ds(j * W_OUT_SHARD + pc * ho, ho), :]

        def copy(k, which, block, to):
            return pltpu.make_async_remote_copy(
                src_ref=half(which, *block), dst_ref=half(which, *block), send_sem=send_sems.at[k],
                recv_sem=recv_sems.at[k], device_id=to, device_id_type=MESH)

        first = [copy(6 * w + j, w, (x, y, c), (*chip, c)) for w in range(2) for j, chip in enumerate(chips)]
        for cp in first:
            cp.start()
        passed = []
        for w in range(2):
            for j, chip in enumerate(chips):
                copy(6 * w + j, w, (*chip, c), (x, y, c)).wait_recv()
                cp = copy(6 * w + 3 + j, w, (*chip, c), sibling)
                cp.start()
                passed.append(cp)
        for w in range(2):
            for j, chip in enumerate(chips):
                copy(6 * w + 3 + j, w, (*chip, 1 - c), (x, y, c)).wait_recv()
        for cp in first + passed:
            cp.wait_send()

    anyspec = pl.BlockSpec(memory_space=pl.ANY)
    return pl.pallas_call(
        body, name="weights_gather",
        out_shape=[jax.ShapeDtypeStruct((D, D_IN), BF16), jax.ShapeDtypeStruct((D, D), BF16)],
        in_specs=[anyspec, anyspec], out_specs=[anyspec, anyspec], input_output_aliases={0: 0, 1: 1},
        scratch_shapes=[pltpu.SemaphoreType.DMA((12,)), pltpu.SemaphoreType.DMA((12,))],
    )(wi_full, wo_full)


HBM_SPEC = pl.BlockSpec(memory_space=pltpu.HBM)
SEM_SPEC = pl.BlockSpec(memory_space=pltpu.SEMAPHORE)
SIDE_EFFECT = pltpu.SideEffectType.DATAFLOW_SIDE_EFFECTING


def _peer(x, y, c, q, cb):
    return (1 - x if q & 2 else x, 1 - y if q & 1 else y, 1 - c if cb else c)


def _w_in_piece(slots):
    def piece(part_ref, k, to):
        return part_ref.at[pl.ds(to[2] * (D // 2), D // 2), pl.ds(slots[k] * W_IN_SHARD, W_IN_SHARD)]
    return piece


def _w_out_piece(part_ref, k, to):
    ho = W_OUT_SHARD // 2
    return part_ref.at[pl.ds((2 * to[0] + to[1]) * W_OUT_SHARD + to[2] * ho, ho), :]


def _group_piece(part_ref, k, to):
    return part_ref.at[4 * to[0] + 2 * to[1] + to[2]]


def _whole_piece(part_ref, k, to):
    return part_ref


def _exchange_start_call(part, rels, piece, slot_shape, name):
    n = len(rels)
    land = lax.empty((n,) + slot_shape, part.dtype)

    def body(part_ref, land_ref, send_sems, recv_sems, part_thru, land_thru, token):
        x, y, c = _coords()
        for k, (q, cb) in enumerate(rels):
            to = _peer(x, y, c, q, cb)
            pltpu.make_async_remote_copy(src_ref=piece(part_ref, k, to), dst_ref=land_ref.at[k], send_sem=send_sems.at[k],
                                         recv_sem=recv_sems.at[k], device_id=to, device_id_type=MESH).start()
        token[...] = jnp.zeros_like(token)

    return pl.pallas_call(
        body, name=name,
        out_shape=(pltpu.SemaphoreType.DMA((n,)), pltpu.SemaphoreType.DMA((n,)), pltpu.HBM(part.shape, part.dtype),
                   pltpu.HBM(land.shape, land.dtype), jax.ShapeDtypeStruct((8, LANE), F32)),
        in_specs=(HBM_SPEC, HBM_SPEC), out_specs=(SEM_SPEC, SEM_SPEC, HBM_SPEC, HBM_SPEC, pl.BlockSpec(memory_space=pltpu.VMEM)),
        input_output_aliases={0: 2, 1: 3},
        compiler_params=pltpu.CompilerParams(has_side_effects=SIDE_EFFECT),
    )(pltpu.with_memory_space_constraint(part, pltpu.HBM), pltpu.with_memory_space_constraint(land, pltpu.HBM))


def _exchange_wait_call(started, rels, piece, after, name):
    send_sems, recv_sems, part_thru, land_thru, _ = started

    def body(part_ref, land_ref, send_sems, recv_sems, after_ref, part_out, land_out):
        x, y, c = _coords()
        for k, (q, cb) in enumerate(rels):
            to = _peer(x, y, c, q, cb)
            cp = pltpu.make_async_remote_copy(src_ref=piece(part_ref, k, to), dst_ref=land_ref.at[k], send_sem=send_sems.at[k],
                                              recv_sem=recv_sems.at[k], device_id=to, device_id_type=MESH)
            cp.wait_send()
            cp.wait_recv()

    return pl.pallas_call(
        body, name=name,
        out_shape=(pltpu.HBM(part_thru.shape, part_thru.dtype), pltpu.HBM(land_thru.shape, land_thru.dtype)),
        in_specs=(HBM_SPEC, HBM_SPEC, SEM_SPEC, SEM_SPEC, pl.BlockSpec(memory_space=pl.ANY)), out_specs=(HBM_SPEC, HBM_SPEC),
        input_output_aliases={0: 0, 1: 1},
        compiler_params=pltpu.CompilerParams(has_side_effects=SIDE_EFFECT),
    )(part_thru, land_thru, send_sems, recv_sems, after)


def _pair_exchange_call(gi, go):
    hi = D // 2
    ho = W_OUT_SHARD // 2

    def body(gi_in, go_in, fi_ref, fo_ref, send_sems, recv_sems):
        del gi_in, go_in
        x, y, c = _coords()
        sibling = (x, y, 1 - c)
        mine = (fi_ref.at[pl.ds(c * hi, hi), :], fo_ref.at[pl.ds(c * ho, ho), :])
        theirs = (fi_ref.at[pl.ds((1 - c) * hi, hi), :], fo_ref.at[pl.ds((1 - c) * ho, ho), :])
        sends = [pltpu.make_async_remote_copy(src_ref=ref, dst_ref=ref, send_sem=send_sems.at[k], recv_sem=recv_sems.at[k],
                                              device_id=sibling, device_id_type=MESH) for k, ref in enumerate(mine)]
        for cp in sends:
            cp.start()
        for k, ref in enumerate(theirs):
            pltpu.make_async_remote_copy(src_ref=ref, dst_ref=ref, send_sem=send_sems.at[k], recv_sem=recv_sems.at[k],
                                         device_id=sibling, device_id_type=MESH).wait_recv()
        for cp in sends:
            cp.wait_send()

    anyspec = pl.BlockSpec(memory_space=pl.ANY)
    return pl.pallas_call(
        body, name="pair_exchange",
        out_shape=[jax.ShapeDtypeStruct((D, W_IN_SHARD), F32), jax.ShapeDtypeStruct((W_OUT_SHARD, D), F32)],
        in_specs=[anyspec, anyspec], out_specs=[anyspec, anyspec], input_output_aliases={0: 0, 1: 1},
        scratch_shapes=[pltpu.SemaphoreType.DMA((2,)), pltpu.SemaphoreType.DMA((2,))],
    )(gi, go)


def _rope_tables(s):
    inv_freq = 10000.0 ** (-jnp.arange(0, HEAD, 2, dtype=F32) / HEAD)
    ang = jnp.arange(s, dtype=F32)[:, None] * inv_freq[None, :]
    return jnp.tile(jnp.cos(ang), (1, LANE // (HEAD // 2))), jnp.tile(jnp.sin(ang), (1, LANE // (HEAD // 2)))


def _pad_cols(a, n):
    return jnp.pad(a, ((0, 0), (0, n - a.shape[1])))


def kernel(x, c, w_ada, b_ada, norm_g, w_in, ln_v_g, ln_v_b, w_spatial, b_spatial, sinks, w_out, w_ada_final, b_ada_final, final_norm_g, loss_target, m_w_ada, m_b_ada, m_norm_g, m_w_in, m_ln_v_g, m_ln_v_b, m_w_spatial, m_b_spatial, m_sinks, m_w_out, m_w_ada_final, m_b_ada_final, m_final_norm_g, v_w_ada, v_b_ada, v_norm_g, v_w_in, v_ln_v_g, v_ln_v_b, v_w_spatial, v_b_spatial, v_sinks, v_w_out, v_w_ada_final, v_b_ada_final, v_final_norm_g):
    s = x.shape[1]
    ax, ay, ac = _coords()
    chip = 2 * ax + ay
    me = 4 * ax + 2 * ay + ac
    n_ada = w_ada.shape[2]
    n_adaf = w_ada_final.shape[1]

    x2d = x.reshape(s, D)
    tgt = loss_target.reshape(s, D)
    w_ada2, w_in2, w_out2 = w_ada[0], w_in[0], w_out[0]
    b_ada_f2 = b_ada_final.reshape(1, 2 * D)
    gf = final_norm_g.reshape(1, D)

    c_all = _allgather_sum_call(jnp.pad(c, ((0, 7), (0, 0))), "gather_c", False)[0][::8]
    mod_p, c_act = _rowmat_call(c_all, w_ada2, lax.dynamic_slice(b_ada, (0, chip * n_ada), (1, n_ada)), "mod")
    modf_p, _ = _rowmat_call(c_all, w_ada_final, lax.dynamic_slice(b_ada_f2, (0, chip * n_adaf), (1, n_adaf)), "mod_final")
    mods = _allgather_sum_call(jnp.concatenate([mod_p, modf_p], axis=1), "gather_mod", False)[0]
    my_rows = [lax.dynamic_slice(mods, (16 * j + me, 0), (1, n_ada + n_adaf)) for j in range(N_CHIP)]
    mod = jnp.concatenate([r[:, :n_ada] for r in my_rows], axis=1)
    mod_f = jnp.concatenate([r[:, n_ada:] for r in my_rows], axis=1)
    shift, scale, gate = mod[:, :D], mod[:, D:2 * D], mod[:, 2 * D:]
    shift_f, scale_f = mod_f[:, :D], mod_f[:, D:]

    pos = jnp.stack([chip, ac]).astype(jnp.int32)
    w_in_own = _cast_into_call(pos, w_in2, (D, D_IN), "cast_w_in")
    w_out_own = _cast_into_call(pos, w_out2, (D, D), "cast_w_out")

    cos, sin = _rope_tables(s)
    b_sp_t = b_spatial[0].T
    sinks1 = sinks.reshape(N_Q)
    proj, h, w_in_bf, w_out_bf = _proj_gather_call(pos, x2d, shift, scale, norm_g, w_in_own, w_out_own)
    y = _mix_fwd_call(proj, cos, sin, ln_v_g, ln_v_b, w_spatial[0], b_sp_t, sinks1)
    dx2, do, st_tail = _tail_call(y, w_out_bf, x2d, tgt, gate, shift_f, scale_f, gf)

    rel_o = [(0, 1), (1, 0), (1, 1), (2, 0), (2, 1), (3, 0), (3, 1)]
    rel_a = [(1, 0), (1, 1), (2, 0), (2, 1)]
    rel_b = [(3, 0), (3, 1), (0, 1)]
    piece_a, piece_b = _w_in_piece([0, 0, 1, 1]), _w_in_piece([0, 0, 1])
    half_in, half_out = (D // 2, W_IN_SHARD), (W_OUT_SHARD // 2, D)

    g_w_out_p = _tn_call(y, do, "grad_w_out")
    st_o = _exchange_start_call(g_w_out_p, rel_o, _w_out_piece, half_out, "send_w_out")
    dy = _dy_call(do, w_out_bf)
    dproj, st_ln, d_wsp, d_bsp_t, d_sink = _mix_bwd_call(
        proj, dy, cos, sin, ln_v_g + st_o[4][0:1, 0:1], ln_v_b, w_spatial[0], jnp.swapaxes(w_spatial[0], 1, 2), b_sp_t, sinks1)
    g_w_in_a = _tn_shards_call(pos, h, dproj, (1, 2), "grad_w_in_a")
    st_a = _exchange_start_call(g_w_in_a, rel_a, piece_a, half_in, "send_w_in_a")
    g_w_in_b = _tn_shards_call(pos, h, dproj, (3, 0), "grad_w_in_b")
    st_b = _exchange_start_call(g_w_in_b, rel_b, piece_b, half_in, "send_w_in_b")
    rel_all = rel_o
    st_s = _exchange_start_call(d_wsp, rel_all, _group_piece, (BLK, BLK), "send_w_spatial")
    sent = st_a[4][0:1, 0:1] + st_b[4][0:1, 0:1] + st_s[4][0:1, 0:1]
    grad_x, st_dh = _dh_call(dproj, w_in_bf, x2d, dx2, scale + sent, norm_g)

    g_w_out_p, recv_o = _exchange_wait_call(st_o, rel_o, _w_out_piece, st_dh, "wait_w_out")
    _, recv_a = _exchange_wait_call(st_a, rel_a, piece_a, st_dh, "wait_w_in_a")
    g_w_in_b, recv_b = _exchange_wait_call(st_b, rel_b, piece_b, st_dh, "wait_w_in_b")
    d_wsp, recv_s = _exchange_wait_call(st_s, rel_all, _group_piece, st_dh, "wait_w_spatial")
    mine_in = _sum_pieces_call(pos, g_w_in_b, lambda i, p, nrb: (p[1] * nrb + i, 1), [recv_a, recv_b], "sum_w_in")
    mine_out = _sum_pieces_call(pos, g_w_out_p, lambda i, p, nrb: ((2 * p[0] + p[1]) * nrb + i, 0), [recv_o], "sum_w_out")
    wsp_group = _sum_pieces_call(pos, d_wsp.reshape(GROUPS * BLK, BLK), lambda i, p, nrb: (2 * p[0] + p[1], 0), [recv_s],
                                 "sum_w_spatial")
    to_sibling = [(0, 1)]
    st_pi = _exchange_start_call(mine_in, to_sibling, _whole_piece, half_in, "swap_w_in")
    st_po = _exchange_start_call(mine_out, to_sibling, _whole_piece, half_out, "swap_w_out")

    misc = jnp.concatenate([st_ln, d_bsp_t[:, :GROUPS].T, d_sink, jnp.zeros((8, D - D_A - 2 * LANE), F32)], axis=1)
    pack = jnp.concatenate([wsp_group.reshape(8, D) + (st_pi[4][0:1, 0:1] + st_po[4][0:1, 0:1]), st_tail, st_dh, misc], axis=0)
    rows = pack.shape[0]
    packs, tot = _allgather_sum_call(pack, "gather_small", True)
    packs = packs.reshape(N_DEV, rows, D)
    dmod_all = jnp.concatenate([packs[:, 16, :], packs[:, 17, :], packs[:, 11, :]], axis=1)
    dmodf_all = jnp.concatenate([packs[:, 8, :], packs[:, 9, :]], axis=1)
    loss = tot[13, 0]
    mine_in, theirs_in = _exchange_wait_call(st_pi, to_sibling, _whole_piece, tot, "swapped_w_in")
    mine_out, theirs_out = _exchange_wait_call(st_po, to_sibling, _whole_piece, tot, "swapped_w_out")
    small = {
        "b_ada": jnp.concatenate([tot[16:17], tot[17:18], tot[11:12]], axis=1),
        "norm_g": tot[18:19],
        "ln_v_g": tot[24:25, :D_A],
        "ln_v_b": tot[25:26, :D_A],
        "w_spatial": packs[:, 0:8, :].reshape(GROUPS * BLK, BLK),
        "b_spatial": tot[24:32, D_A:D_A + BLK],
        "sinks": tot[24:25, D_A + LANE:D_A + LANE + N_Q],
        "b_ada_final": jnp.concatenate([tot[8:9], tot[9:10]], axis=1),
        "final_norm_g": tot[10:11],
    }

    weights = dict(w_ada=w_ada, b_ada=b_ada, norm_g=norm_g, w_in=w_in, ln_v_g=ln_v_g, ln_v_b=ln_v_b, w_spatial=w_spatial,
                   b_spatial=b_spatial, sinks=sinks, w_out=w_out, w_ada_final=w_ada_final, b_ada_final=b_ada_final,
                   final_norm_g=final_norm_g)
    m_in = dict(w_ada=m_w_ada, b_ada=m_b_ada, norm_g=m_norm_g, w_in=m_w_in, ln_v_g=m_ln_v_g, ln_v_b=m_ln_v_b,
                w_spatial=m_w_spatial, b_spatial=m_b_spatial, sinks=m_sinks, w_out=m_w_out, w_ada_final=m_w_ada_final,
                b_ada_final=m_b_ada_final, final_norm_g=m_final_norm_g)
    v_in = dict(w_ada=v_w_ada, b_ada=v_b_ada, norm_g=v_norm_g, w_in=v_w_in, ln_v_g=v_ln_v_g, ln_v_b=v_ln_v_b,
                w_spatial=v_w_spatial, b_spatial=v_b_spatial, sinks=v_sinks, w_out=v_w_out, w_ada_final=v_w_ada_final,
                b_ada_final=v_b_ada_final, final_norm_g=v_final_norm_g)
    c_act_t = c_act.T
    outer = {"w_ada": lax.dynamic_slice(dmod_all, (0, chip * n_ada), (N_DEV, n_ada)),
             "w_ada_final": lax.dynamic_slice(dmodf_all, (0, chip * n_adaf), (N_DEV, n_adaf))}
    halves = {"w_in": (mine_in, theirs_in[0]), "w_out": (mine_out, theirs_out[0])}
    done = {}
    for name, (mine, theirs) in halves.items():
        shape2 = (2 * mine.shape[0], mine.shape[1])
        done[name] = _adam_halves_call(pos, weights[name].reshape(shape2), mine, theirs, m_in[name].reshape(shape2),
                                       v_in[name].reshape(shape2), "adam_" + name)
    for name, dm in outer.items():
        shape2 = (D, dm.shape[1])
        done[name] = _adam_outer_call(weights[name].reshape(shape2), c_act_t, dm, m_in[name].reshape(shape2),
                                      v_in[name].reshape(shape2), "adam_" + name)
    updates = _adam_small_call([(weights[name].reshape(g.shape), g, m_in[name].reshape(g.shape), v_in[name].reshape(g.shape))
                                for name, g in small.items()])
    for (name, g), upd in zip(small.items(), updates):
        done[name] = (g, *upd)
    outs = [[done[name][k].reshape(w.shape) for name, w in weights.items()] for k in range(4)]
    return (loss, grad_x.reshape(x.shape), *outs[0], *outs[1], *outs[2], *outs[3])
```

```python
import jax
import jax.numpy as jnp
from jax import lax
from jax.experimental import pallas as pl
from jax.experimental.pallas import tpu as pltpu

F32 = jnp.float32
BF16 = jnp.bfloat16
MESH = pl.DeviceIdType.MESH

D = 2048
D_A = 1024
D_B = 1024
D_KV = 256
HEAD = 64
N_Q = 16
N_KV = 4
Q_PER_KV = N_Q // N_KV
BLK = 128
GROUPS = 8
D_IN = 5632
OFF_Q, OFF_K, OFF_V, OFF_ZB = 3072, 4096, 4352, 4608
N_CHIP = 4
N_DEV = 8
W_IN_SHARD = D_IN // N_CHIP
W_OUT_SHARD = D // N_CHIP
EPS = 1e-5
SCALE = HEAD ** -0.5
NEG = -1e30
LANE = 128
VMEM_LIMIT = 56 * 1024 * 1024

ADAM_LR, ADAM_B1, ADAM_B2, ADAM_EPS, ADAM_WD, ADAM_STEP = 0.001, 0.9, 0.999, 1e-08, 0.01, 10
ADAM_C1 = 1.0 - ADAM_B1 ** ADAM_STEP
ADAM_C2 = 1.0 - ADAM_B2 ** ADAM_STEP

NT = (((1,), (1,)), ((), ()))
TN = (((0,), (0,)), ((), ()))


def _params(*sem):
    return pltpu.CompilerParams(dimension_semantics=sem, vmem_limit_bytes=VMEM_LIMIT)


def _silu_parts(z):
    sig = 1.0 / (1.0 + jnp.exp(-z))
    return z * sig, sig


def _swap_halves(v, first_half):
    return jnp.where(first_half, pltpu.roll(v, 96, 1), pltpu.roll(v, 32, 1))


def _rope(v, cos_t, sin_s, first_half):
    return v * cos_t + _swap_halves(v, first_half) * sin_s


def _unrope(dv, cos_t, sin_s, first_half):
    return dv * cos_t - _swap_halves(dv, first_half) * sin_s


def _lane_masks():
    lane = lax.broadcasted_iota(jnp.int32, (BLK, LANE), 1)
    return (lane % HEAD) < (HEAD // 2), lane < HEAD


def _band_valid(first_block_bound, rows=BLK):
    rr = lax.broadcasted_iota(jnp.int32, (rows, 2 * BLK), 0) & (BLK - 1)
    jj = lax.broadcasted_iota(jnp.int32, (rows, 2 * BLK), 1)
    return (jj > rr) & (jj <= rr + BLK) & (jj >= first_block_bound)


def _dup_kv(slab, lo):
    rolled = pltpu.roll(slab, HEAD, 1)
    return jnp.where(lo, slab, rolled).astype(BF16), jnp.where(lo, rolled, slab).astype(BF16)


def _stack_heads(ref, sb, slab, lo, dtype):
    kh, base = sb // 2, 2 * (sb % 2) * BLK
    zero = jnp.zeros_like(slab)
    ref[kh, base:base + BLK, :] = jnp.where(lo, slab, zero).astype(dtype)
    ref[kh, base + BLK:base + 2 * BLK, :] = jnp.where(lo, zero, slab).astype(dtype)


def _unstack_heads(ref, sb, lo):
    kh, base = sb // 2, 2 * (sb % 2) * BLK
    return jnp.where(lo, ref[kh, base:base + BLK, :], ref[kh, base + BLK:base + 2 * BLK, :])


def _sink_column(sinks_ref, kh):
    row = lax.broadcasted_iota(jnp.int32, (Q_PER_KV * BLK, 1), 0)
    col = jnp.full(row.shape, sinks_ref[Q_PER_KV * kh + Q_PER_KV - 1], F32)
    for n in range(Q_PER_KV - 2, -1, -1):
        col = jnp.where(row < (n + 1) * BLK, sinks_ref[Q_PER_KV * kh + n], col)
    return col


def _tril():
    t = lax.broadcasted_iota(jnp.int32, (BLK, BLK), 0)
    s = lax.broadcasted_iota(jnp.int32, (BLK, BLK), 1)
    return s <= t


def _layer_norm_fwd(va, lg, lb):
    mu = jnp.mean(va, axis=-1, keepdims=True)
    xc = va - mu
    rstd = lax.rsqrt(jnp.mean(xc * xc, axis=-1, keepdims=True) + EPS)
    vhat = xc * rstd
    return vhat, rstd, vhat * lg + lb


def _softmax_sink(qm, kdup, bias, sink):
    s = lax.dot_general(qm, kdup, NT, preferred_element_type=F32) + bias
    m = jnp.maximum(jnp.max(s, axis=-1, keepdims=True), sink)
    p = jnp.exp(s - m)
    esink = jnp.exp(sink - m)
    inv = 1.0 / (jnp.sum(p, axis=-1, keepdims=True) + esink)
    return p * inv, esink * inv


def _band_bias(bias_ref):
    rows = bias_ref.shape[1]
    bias_ref[0] = jnp.where(_band_valid(BLK, rows), 0.0, NEG)
    bias_ref[1] = jnp.where(_band_valid(0, rows), 0.0, NEG)


def _rowmat_call(c_all, w, b, name):
    n = w.shape[1]
    tn = 512

    def body(c_ref, w_ref, b_ref, o_ref, ca_ref):
        ca, _ = _silu_parts(c_ref[...])
        ca_ref[...] = ca
        o_ref[...] = jnp.dot(ca.astype(BF16), w_ref[...].astype(BF16), preferred_element_type=F32) + b_ref[...]

    return pl.pallas_call(
        body, name=name, grid=(n // tn,),
        in_specs=[pl.BlockSpec((N_DEV, D), lambda j: (0, 0)), pl.BlockSpec((D, tn), lambda j: (0, j)),
                  pl.BlockSpec((1, tn), lambda j: (0, j))],
        out_specs=[pl.BlockSpec((N_DEV, tn), lambda j: (0, j)), pl.BlockSpec((N_DEV, D), lambda j: (0, 0))],
        out_shape=[jax.ShapeDtypeStruct((N_DEV, n), F32), jax.ShapeDtypeStruct((N_DEV, D), F32)],
        compiler_params=_params("arbitrary"),
    )(c_all, w, b)


def _cast_into_call(pos, w, full_shape, name):
    r, n = w.shape
    tr = min(r, 512)
    by_cols = full_shape[0] == r
    nrb = r // tr

    def body(pos_ref, w_ref, o_ref):
        o_ref[...] = w_ref[...].astype(BF16)

    out_map = (lambda i, pos: (i, pos[0])) if by_cols else (lambda i, pos: (pos[0] * nrb + i, 0))
    return pl.pallas_call(
        body, name=name,
        grid_spec=pltpu.PrefetchScalarGridSpec(
            num_scalar_prefetch=1, grid=(nrb,),
            in_specs=[pl.BlockSpec((tr, n), lambda i, pos: (i, 0))], out_specs=pl.BlockSpec((tr, n), out_map)),
        out_shape=jax.ShapeDtypeStruct(full_shape, BF16), compiler_params=_params("parallel"),
    )(pos, w)


def _proj_call(x, shift, scale, norm_g, w_bf):
    s = x.shape[0]
    tm = min(s, 1024)
    tn = 512

    def body(x_ref, sh_ref, sc_ref, g_ref, w_ref, proj_ref, h_ref):
        @pl.when(pl.program_id(1) == 0)
        def _():
            xv = x_ref[...]
            r = lax.rsqrt(jnp.mean(xv * xv, axis=-1, keepdims=True) + EPS)
            h_ref[...] = ((xv * r * g_ref[...]) * (1.0 + sc_ref[...]) + sh_ref[...]).astype(BF16)

        proj_ref[...] = jnp.dot(h_ref[...], w_ref[...], preferred_element_type=F32)

    vec = pl.BlockSpec((1, D), lambda i, j: (0, 0))
    return pl.pallas_call(
        body, name="proj", grid=(s // tm, D_IN // tn),
        in_specs=[pl.BlockSpec((tm, D), lambda i, j: (i, 0)), vec, vec, vec, pl.BlockSpec((D, tn), lambda i, j: (0, j))],
        out_specs=[pl.BlockSpec((tm, tn), lambda i, j: (i, j)), pl.BlockSpec((tm, D), lambda i, j: (i, 0))],
        out_shape=[jax.ShapeDtypeStruct((s, D_IN), F32), jax.ShapeDtypeStruct((s, D), BF16)],
        compiler_params=_params("parallel", "arbitrary"),
    )(x, shift, scale, norm_g, w_bf)


def _proj_gather_call(pos, x, shift, scale, norm_g, wi_full, wo_full):
    s = x.shape[0]
    tm = min(s, 512)
    nrow = s // tm
    hi = D // 2
    ho = W_OUT_SHARD // 2

    def body(pos_ref, x_ref, sh_ref, sc_ref, g_ref, wi_in, wo_in, proj_ref, h_ref, fi_ref, fo_ref,
             h_all, wbuf, send_sems, recv_sems, load_sem):
        del wi_in, wo_in
        p = pl.program_id(0)
        i = pl.program_id(1)
        x_, y_, c_ = _coords()
        me, sibling = (x_, y_, c_), (x_, y_, 1 - c_)

        def shard_of(q):
            px, py, _ = _peer(x_, y_, c_, q, 0)
            return 2 * px + py

        def part(which, q, pc, sub=None):
            n = hi if which == 0 else ho
            base = pc * n
            if sub is not None:
                n //= 2
                base = base + sub * n
            if which == 0:
                return fi_ref.at[pl.ds(base, n), pl.ds(shard_of(q) * W_IN_SHARD, W_IN_SHARD)]
            return fo_ref.at[pl.ds(shard_of(q) * W_OUT_SHARD + base, n), :]

        def copy(k, which, q, pc, to, sub=None):
            ref = part(which, q, pc, sub)
            return pltpu.make_async_remote_copy(src_ref=ref, dst_ref=ref, send_sem=send_sems.at[k], recv_sem=recv_sems.at[k],
                                                device_id=to, device_id_type=MESH)

        def to_neighbour(which, q):
            return copy(8 * which + q - 1, which, 0, c_, _peer(x_, y_, c_, q, 0))

        def from_neighbour(which, q):
            return copy(8 * which + q - 1, which, q, c_, me)

        def relay(which, q):
            return copy(8 * which + 2 + q - 1, which, q, c_, _peer(x_, y_, c_, 3 - q, 0), q - 1)

        def relayed(which, sub):
            return copy(8 * which + 2 + sub, which, 3, c_, me, sub)

        def to_sibling(which, q):
            return copy(8 * which + 4 + q - 1, which, q, c_, sibling)

        def from_sibling(which, q):
            return copy(8 * which + 4 + q - 1, which, q, 1 - c_, me)

        def relayed_to_sibling(which, sub):
            return copy(8 * which + 6 + sub, which, 3, c_, sibling, sub)

        def relayed_from_sibling(which, sub):
            return copy(8 * which + 6 + sub, which, 3, 1 - c_, me, sub)

        def pass_on_neighbours(which):
            for q in (1, 2):
                from_neighbour(which, q).wait_recv()
                to_sibling(which, q).start()
                relay(which, q).start()

        def pass_on_relayed(which):
            for sub in range(2):
                relayed(which, sub).wait_recv()
                relayed_to_sibling(which, sub).start()

        def load_shard(q):
            cp = pltpu.make_async_copy(fi_ref.at[:, pl.ds(shard_of(q) * W_IN_SHARD, W_IN_SHARD)], wbuf, load_sem)
            cp.start()
            cp.wait()

        @pl.when((p == 0) & (i == 0))
        def _():
            for q in (1, 2):
                to_neighbour(0, q).start()
            load_shard(0)

        @pl.when((p == 1) & (i == 0))
        def _():
            pass_on_neighbours(0)
            for q in (1, 2):
                to_neighbour(1, q).start()
            from_sibling(0, 1).wait_recv()
            load_shard(1)

        @pl.when((p == 2) & (i == 0))
        def _():
            from_sibling(0, 2).wait_recv()
            load_shard(2)

        @pl.when((p == 3) & (i == 0))
        def _():
            pass_on_relayed(0)
            pass_on_neighbours(1)
            for sub in range(2):
                relayed_from_sibling(0, sub).wait_recv()
            load_shard(3)

        rows = pl.ds(pl.multiple_of(i * tm, tm), tm)

        @pl.when(p == 0)
        def _():
            xv = x_ref[...]
            r = lax.rsqrt(jnp.mean(xv * xv, axis=-1, keepdims=True) + EPS)
            hv = ((xv * r * g_ref[...]) * (1.0 + sc_ref[...]) + sh_ref[...]).astype(BF16)
            h_ref[...] = hv
            h_all[rows, :] = hv

        proj_ref[...] = jnp.dot(h_all[rows, :], wbuf[...], preferred_element_type=F32)

        @pl.when((p == N_CHIP - 1) & (i == nrow - 1))
        def _():
            pass_on_relayed(1)
            for q in (1, 2):
                from_sibling(1, q).wait_recv()
            for sub in range(2):
                relayed_from_sibling(1, sub).wait_recv()
            for which in range(2):
                for q in (1, 2):
                    to_neighbour(which, q).wait_send()
                    relay(which, q).wait_send()
                    to_sibling(which, q).wait_send()
                    relayed_to_sibling(which, q - 1).wait_send()

    vec = pl.BlockSpec((1, D), lambda p, i, pos: (0, 0))
    first_phase_rows = lambda p, i, pos: (jnp.where(p == 0, i, nrow - 1), 0)
    anyspec = pl.BlockSpec(memory_space=pl.ANY)
    return pl.pallas_call(
        body, name="proj_gather",
        grid_spec=pltpu.PrefetchScalarGridSpec(
            num_scalar_prefetch=1, grid=(N_CHIP, nrow),
            in_specs=[pl.BlockSpec((tm, D), first_phase_rows), vec, vec, vec, anyspec, anyspec],
            out_specs=[pl.BlockSpec((tm, W_IN_SHARD), lambda p, i, pos: (i, jnp.bitwise_xor(pos[0], p))),
                       pl.BlockSpec((tm, D), first_phase_rows), anyspec, anyspec],
            scratch_shapes=[pltpu.VMEM((s, D), BF16), pltpu.VMEM((D, W_IN_SHARD), BF16),
                            pltpu.SemaphoreType.DMA((16,)), pltpu.SemaphoreType.DMA((16,)), pltpu.SemaphoreType.DMA]),
        out_shape=[jax.ShapeDtypeStruct((s, D_IN), F32), jax.ShapeDtypeStruct((s, D), BF16),
                   jax.ShapeDtypeStruct((D, D_IN), BF16), jax.ShapeDtypeStruct((D, D), BF16)],
        input_output_aliases={5: 2, 6: 3},
        compiler_params=_params("arbitrary", "arbitrary"),
    )(pos, x, shift, scale, norm_g, wi_full, wo_full)


def _proj_specs(rev_nb=None):
    if rev_nb is None:
        row = lambda i: i
    else:
        row = lambda i: rev_nb - 1 - i
    wide = lambda col: pl.BlockSpec((BLK, D_A), lambda i: (row(i), col))
    kv = lambda col: pl.BlockSpec((BLK, D_KV), lambda i: (row(i), col))
    half = lambda col: pl.BlockSpec((BLK, 512), lambda i: (row(i), col))
    return [wide(0), wide(1), wide(2), wide(3), kv(OFF_K // D_KV), kv(OFF_V // D_KV), half(OFF_ZB // 512), half(OFF_ZB // 512 + 1)]


def _mix_fwd_call(proj, cos, sin, ln_g, ln_b, w_sp, b_sp_t, sinks):
    s = proj.shape[0]
    nb = s // BLK

    def body(ua_ref, va_ref, za_ref, q_ref, k_ref, v_ref, zb0_ref, zb1_ref, cos_ref, sin_ref, lg_ref, lb_ref,
             w_ref, bt_ref, sinks_ref, y_ref, kdup_ref, vdup_ref, qm_ref, ost_ref, bias_ref):
        i = pl.program_id(0)
        first_half, lo = _lane_masks()
        cos_t = cos_ref[...]
        sin_t = sin_ref[...]

        _, _, vln = _layer_norm_fwd(va_ref[...], lg_ref[...], lb_ref[...])
        tril = _tril()
        for g in range(GROUPS):
            cols = slice(g * BLK, (g + 1) * BLK)
            wg = jnp.where(tril, w_ref[g], 0.0).astype(BF16)
            sg = jnp.dot(wg, vln[:, cols].astype(BF16), preferred_element_type=F32) + bt_ref[:, g:g + 1]
            gate, _ = _silu_parts(za_ref[:, cols])
            y_ref[:, cols] = (ua_ref[:, cols] * sg * gate).astype(BF16)

        @pl.when(i == 0)
        def _():
            kdup_ref[:, 0:BLK, :] = jnp.zeros((N_KV, BLK, LANE), BF16)
            vdup_ref[:, 0:BLK, :] = jnp.zeros((N_KV, BLK, LANE), BF16)
            _band_bias(bias_ref)

        @pl.when(i > 0)
        def _():
            kdup_ref[:, 0:BLK, :] = kdup_ref[:, BLK:2 * BLK, :]
            vdup_ref[:, 0:BLK, :] = vdup_ref[:, BLK:2 * BLK, :]

        for ks in range(2):
            cols = slice(ks * LANE, (ks + 1) * LANE)
            kr = _rope(k_ref[:, cols], cos_t, sin_t, first_half)
            for n, (kd, vd) in enumerate(zip(_dup_kv(kr, lo), _dup_kv(v_ref[:, cols], lo))):
                kdup_ref[2 * ks + n, BLK:2 * BLK, :] = kd
                vdup_ref[2 * ks + n, BLK:2 * BLK, :] = vd
        for sb in range(8):
            _stack_heads(qm_ref, sb, _rope(q_ref[:, sb * LANE:(sb + 1) * LANE], cos_t, sin_t, first_half) * SCALE, lo, BF16)

        block_kind = jnp.where(i > 0, 1, 0)

        def kv_head(kh, carry):
            probs, _ = _softmax_sink(qm_ref[kh], kdup_ref[kh], bias_ref[block_kind], _sink_column(sinks_ref, kh))
            ost_ref[kh] = jnp.dot(probs.astype(BF16), vdup_ref[kh], preferred_element_type=F32)
            return carry

        lax.fori_loop(0, N_KV, kv_head, 0, unroll=2)
        for sb in range(8):
            cols = slice(sb * LANE, (sb + 1) * LANE)
            zb = zb0_ref[:, cols] if sb < 4 else zb1_ref[:, (sb - 4) * LANE:(sb - 3) * LANE]
            gate, _ = _silu_parts(zb)
            y_ref[:, D_A + sb * LANE:D_A + (sb + 1) * LANE] = (_unstack_heads(ost_ref, sb, lo) * gate).astype(BF16)

    tab = pl.BlockSpec((BLK, LANE), lambda i: (i, 0))
    return pl.pallas_call(
        body, name="mix_fwd", grid=(nb,),
        in_specs=_proj_specs() + [
            tab, tab, pl.BlockSpec((1, D_A), lambda i: (0, 0)), pl.BlockSpec((1, D_A), lambda i: (0, 0)),
            pl.BlockSpec((GROUPS, BLK, BLK), lambda i: (0, 0, 0)), pl.BlockSpec((BLK, GROUPS), lambda i: (0, 0)),
            pl.BlockSpec(memory_space=pltpu.SMEM)],
        out_specs=pl.BlockSpec((BLK, 2 * D_A), lambda i: (i, 0)),
        out_shape=jax.ShapeDtypeStruct((s, 2 * D_A), BF16),
        scratch_shapes=[pltpu.VMEM((N_KV, 2 * BLK, LANE), BF16), pltpu.VMEM((N_KV, 2 * BLK, LANE), BF16),
                        pltpu.VMEM((N_KV, Q_PER_KV * BLK, LANE), BF16), pltpu.VMEM((N_KV, Q_PER_KV * BLK, LANE), F32),
                        pltpu.VMEM((2, Q_PER_KV * BLK, 2 * BLK), F32)],
        compiler_params=_params("arbitrary"),
    )(proj, proj, proj, proj, proj, proj, proj, proj, cos, sin, ln_g, ln_b, w_sp, b_sp_t, sinks)


def _tail_call(y, w_out_bf, x, target, gate, shift_f, scale_f, gf):
    s = x.shape[0]
    tm = min(s, 256)
    nsteps = s // tm

    def body(y_ref, w_ref, x_ref, t_ref, gate_ref, shf_ref, scf_ref, gf_ref, dx2_ref, do_ref, st_ref):
        i = pl.program_id(0)

        @pl.when(i == 0)
        def _():
            st_ref[...] = jnp.zeros((8, D), F32)

        o = jnp.dot(y_ref[...], w_ref[...], preferred_element_type=F32)
        gate_v = gate_ref[...]
        x2 = x_ref[...] + gate_v * o
        r2 = lax.rsqrt(jnp.mean(x2 * x2, axis=-1, keepdims=True) + EPS)
        xn2 = x2 * r2
        hn2 = xn2 * gf_ref[...]
        one_sc = 1.0 + scf_ref[...]
        err = hn2 * one_sc + shf_ref[...] - t_ref[...]
        dout = err * (1.0 / D)
        dhn2 = dout * one_sc
        dxn2 = dhn2 * gf_ref[...]
        dx2 = r2 * (dxn2 - xn2 * jnp.mean(dxn2 * xn2, axis=-1, keepdims=True))
        dx2_ref[...] = dx2
        do_ref[...] = (dx2 * gate_v).astype(BF16)
        st_ref[0:1, :] += jnp.sum(dout, axis=0, keepdims=True)
        st_ref[1:2, :] += jnp.sum(dout * hn2, axis=0, keepdims=True)
        st_ref[2:3, :] += jnp.sum(dhn2 * xn2, axis=0, keepdims=True)
        st_ref[3:4, :] += jnp.sum(dx2 * o, axis=0, keepdims=True)
        st_ref[4:5, :] += jnp.sum(err * err, axis=0, keepdims=True)

        @pl.when(i == nsteps - 1)
        def _():
            st_ref[5:6, :] = jnp.full((1, D), 0.5 / D, F32) * jnp.sum(st_ref[4:5, :])

    vec = pl.BlockSpec((1, D), lambda i: (0, 0))
    rows = lambda: pl.BlockSpec((tm, D), lambda i: (i, 0))
    return pl.pallas_call(
        body, name="tail", grid=(nsteps,),
        in_specs=[rows(), pl.BlockSpec((D, D), lambda i: (0, 0)), rows(), rows(), vec, vec, vec, vec],
        out_specs=[rows(), rows(), pl.BlockSpec((8, D), lambda i: (0, 0))],
        out_shape=[jax.ShapeDtypeStruct((s, D), F32), jax.ShapeDtypeStruct((s, D), BF16), jax.ShapeDtypeStruct((8, D), F32)],
        compiler_params=_params("arbitrary"),
    )(y, w_out_bf, x, target, gate, shift_f, scale_f, gf)


def _dy_call(do, w_out_bf):
    s = do.shape[0]
    tm = min(s, 512)

    def body(do_ref, w_ref, dy_ref):
        dy_ref[...] = lax.dot_general(do_ref[...], w_ref[...], NT, preferred_element_type=F32)

    return pl.pallas_call(
        body, name="dy", grid=(s // tm,),
        in_specs=[pl.BlockSpec((tm, D), lambda i: (i, 0)), pl.BlockSpec((D, D), lambda i: (0, 0))],
        out_specs=pl.BlockSpec((tm, D), lambda i: (i, 0)),
        out_shape=jax.ShapeDtypeStruct((s, D), F32), compiler_params=_params("parallel"),
    )(do, w_out_bf)


def _tn_call(a, b, name):
    s, m = a.shape
    n = b.shape[1]
    tn = 512
    ts = min(s, 1024)
    nk = s // ts

    def body(a_ref, b_ref, o_ref, acc_ref):
        k = pl.program_id(1)

        @pl.when(k == 0)
        def _():
            acc_ref[...] = jnp.zeros((m, tn), F32)

        acc_ref[...] += lax.dot_general(a_ref[...], b_ref[...], TN, preferred_element_type=F32)

        @pl.when(k == nk - 1)
        def _():
            o_ref[...] = acc_ref[...].astype(BF16)

    return pl.pallas_call(
        body, name=name, grid=(n // tn, nk),
        in_specs=[pl.BlockSpec((ts, m), lambda j, k: (k, 0)), pl.BlockSpec((ts, tn), lambda j, k: (k, j))],
        out_specs=pl.BlockSpec((m, tn), lambda j, k: (0, j)),
        out_shape=jax.ShapeDtypeStruct((m, n), BF16),
        scratch_shapes=[pltpu.VMEM((m, tn), F32)],
        compiler_params=_params("parallel", "arbitrary"),
    )(a, b)


def _tn_shards_call(pos, a, b, qs, name):
    s, m = a.shape
    ts = min(s, 1024)
    nk = s // ts

    def body(pos_ref, a_ref, b_ref, o_ref, acc_ref):
        k = pl.program_id(1)

        @pl.when(k == 0)
        def _():
            acc_ref[...] = jnp.zeros((m, W_IN_SHARD), F32)

        acc_ref[...] += lax.dot_general(a_ref[...], b_ref[...], TN, preferred_element_type=F32)

        @pl.when(k == nk - 1)
        def _():
            o_ref[...] = acc_ref[...].astype(BF16)

    def shard(j, pos):
        q = qs[0]
        for n in range(1, len(qs)):
            q = jnp.where(j == n, qs[n], q)
        return jnp.bitwise_xor(pos[0], q)

    return pl.pallas_call(
        body, name=name,
        grid_spec=pltpu.PrefetchScalarGridSpec(
            num_scalar_prefetch=1, grid=(len(qs), nk),
            in_specs=[pl.BlockSpec((ts, m), lambda j, k, pos: (k, 0)),
                      pl.BlockSpec((ts, W_IN_SHARD), lambda j, k, pos: (k, shard(j, pos)))],
            out_specs=pl.BlockSpec((m, W_IN_SHARD), lambda j, k, pos: (0, j)),
            scratch_shapes=[pltpu.VMEM((m, W_IN_SHARD), F32)]),
        out_shape=jax.ShapeDtypeStruct((m, len(qs) * W_IN_SHARD), BF16),
        compiler_params=_params("parallel", "arbitrary"),
    )(pos, a, b)


def _mix_bwd_call(proj, dy, cos, sin, ln_g, ln_b, w_sp, w_sp_t, b_sp_t, sinks):
    s = proj.shape[0]
    nb = s // BLK
    rev = lambda i: nb - 1 - i
    prev = lambda i: jnp.maximum(nb - 2 - i, 0)

    def body(ua_ref, va_ref, za_ref, q_ref, k_ref, v_ref, zb0_ref, zb1_ref, kp_ref, vp_ref, dy_ref,
             cos_ref, sin_ref, cosp_ref, sinp_ref, lg_ref, lb_ref, w_ref, wt_ref, bt_ref, sinks_ref,
             dp_ref, lnst_ref, dw_ref, dbt_ref, dsink_ref,
             kdup_ref, vdup_ref, dvln_ref, qm_ref, dom_ref, ost_ref, dqst_ref, dkdup_ref, dvdup_ref, kcar_ref, vcar_ref,
             sigb_ref, bias_ref):
        i = pl.program_id(0)
        first_half, lo = _lane_masks()
        lane8 = lax.broadcasted_iota(jnp.int32, (8, LANE), 1)
        cos_t = cos_ref[...]
        sin_t = sin_ref[...]

        @pl.when(i == 0)
        def _():
            lnst_ref[...] = jnp.zeros((8, D_A), F32)
            dw_ref[...] = jnp.zeros((GROUPS, BLK, BLK), F32)
            dbt_ref[...] = jnp.zeros((BLK, LANE), F32)
            dsink_ref[...] = jnp.zeros((8, LANE), F32)
            kcar_ref[...] = jnp.zeros((BLK, D_KV), F32)
            vcar_ref[...] = jnp.zeros((BLK, D_KV), F32)
            _band_bias(bias_ref)

        vhat, rstd, vln = _layer_norm_fwd(va_ref[...], lg_ref[...], lb_ref[...])
        tril = _tril()
        triu = jnp.logical_not(tril) | (lax.broadcasted_iota(jnp.int32, (BLK, BLK), 0) == lax.broadcasted_iota(jnp.int32, (BLK, BLK), 1))
        lane_b = lax.broadcasted_iota(jnp.int32, (BLK, LANE), 1)
        db_acc = jnp.zeros((BLK, LANE), F32)
        for g in range(GROUPS):
            cols = slice(g * BLK, (g + 1) * BLK)
            vln_g = vln[:, cols].astype(BF16)
            wg = jnp.where(tril, w_ref[g], 0.0).astype(BF16)
            sg = jnp.dot(wg, vln_g, preferred_element_type=F32) + bt_ref[:, g:g + 1]
            za = za_ref[:, cols]
            gate, sig = _silu_parts(za)
            ua = ua_ref[:, cols]
            dya_g = dy_ref[:, cols]
            dya = dya_g * gate
            dp_ref[:, cols] = (dya * sg).astype(BF16)
            dp_ref[:, 2 * D_A + g * BLK:2 * D_A + (g + 1) * BLK] = (
                dya_g * (ua * sg) * (sig * (1.0 + za * (1.0 - sig)))).astype(BF16)
            ds = dya * ua
            ds_b = ds.astype(BF16)
            wtg = jnp.where(triu, wt_ref[g], 0.0).astype(BF16)
            dvln_ref[:, cols] = jnp.dot(wtg, ds_b, preferred_element_type=F32)
            dw_ref[g] += jnp.where(tril, lax.dot_general(ds_b, vln_g, NT, preferred_element_type=F32), 0.0)
            db_acc = db_acc + jnp.where(lane_b == g, jnp.sum(ds, axis=-1, keepdims=True), 0.0)
        dbt_ref[...] += db_acc
        dvln = dvln_ref[...]
        lnst_ref[0:1, :] += jnp.sum(dvln * vhat, axis=0, keepdims=True)
        lnst_ref[1:2, :] += jnp.sum(dvln, axis=0, keepdims=True)
        dvhat = dvln * lg_ref[...]
        m1 = jnp.mean(dvhat, axis=-1, keepdims=True)
        m2 = jnp.mean(dvhat * vhat, axis=-1, keepdims=True)
        dp_ref[:, D_A:2 * D_A] = (rstd * (dvhat - m1 - vhat * m2)).astype(BF16)

        cosp = cosp_ref[...]
        sinp = sinp_ref[...]
        for ks in range(2):
            cols = slice(ks * LANE, (ks + 1) * LANE)
            kr = _rope(k_ref[:, cols], cos_t, sin_t, first_half)
            kpr = _rope(kp_ref[:, cols], cosp, sinp, first_half)
            for n, (kc, vc, kp, vp) in enumerate(zip(_dup_kv(kr, lo), _dup_kv(v_ref[:, cols], lo),
                                                     _dup_kv(kpr, lo), _dup_kv(vp_ref[:, cols], lo))):
                kdup_ref[2 * ks + n, BLK:2 * BLK, :] = kc
                vdup_ref[2 * ks + n, BLK:2 * BLK, :] = vc
                kdup_ref[2 * ks + n, 0:BLK, :] = kp
                vdup_ref[2 * ks + n, 0:BLK, :] = vp
        for sb in range(8):
            cols = slice(sb * LANE, (sb + 1) * LANE)
            _stack_heads(qm_ref, sb, _rope(q_ref[:, cols], cos_t, sin_t, first_half) * SCALE, lo, BF16)
            zb = zb0_ref[:, cols] if sb < 4 else zb1_ref[:, (sb - 4) * LANE:(sb - 3) * LANE]
            gate, sig = _silu_parts(zb)
            sigb_ref[:, cols] = sig
            _stack_heads(dom_ref, sb, dy_ref[:, D_A + sb * LANE:D_A + (sb + 1) * LANE] * gate, lo, F32)

        block_kind = jnp.where(i < nb - 1, 1, 0)

        def kv_head(kh, dsink_acc):
            qm = qm_ref[kh]
            kd = kdup_ref[kh]
            vd = vdup_ref[kh]
            probs, psink = _softmax_sink(qm, kd, bias_ref[block_kind], _sink_column(sinks_ref, kh))
            probs_b = probs.astype(BF16)
            o = jnp.dot(probs_b, vd, preferred_element_type=F32)
            ost_ref[kh] = o
            dom = dom_ref[kh]
            dom_b = dom.astype(BF16)
            delta = jnp.sum(dom * o, axis=-1, keepdims=True)
            dpr = lax.dot_general(dom_b, vd, NT, preferred_element_type=F32)
            dss = (probs * (dpr - delta)).astype(BF16)
            sd = psink * delta
            for n in range(Q_PER_KV):
                dsink_acc = dsink_acc + jnp.where(lane8 == Q_PER_KV * kh + n, -jnp.sum(sd[n * BLK:(n + 1) * BLK]), 0.0)
            dqst_ref[kh] = jnp.dot(dss, kd, preferred_element_type=F32)
            dkdup_ref[kh] = lax.dot_general(dss, qm, TN, preferred_element_type=F32)
            dvdup_ref[kh] = lax.dot_general(probs_b, dom_b, TN, preferred_element_type=F32)
            return dsink_acc

        dsink_acc = lax.fori_loop(0, N_KV // 2, lambda j, acc: kv_head(2 * j + 1, kv_head(2 * j, acc)), jnp.zeros((8, LANE), F32))
        row0 = lax.broadcasted_iota(jnp.int32, (8, LANE), 0) == 0
        dsink_ref[...] += jnp.where(row0, dsink_acc, 0.0)

        for sb in range(8):
            cols = slice(sb * LANE, (sb + 1) * LANE)
            zb = zb0_ref[:, cols] if sb < 4 else zb1_ref[:, (sb - 4) * LANE:(sb - 3) * LANE]
            sig = sigb_ref[:, cols]
            dyb = dy_ref[:, D_A + sb * LANE:D_A + (sb + 1) * LANE]
            dp_ref[:, OFF_ZB + sb * LANE:OFF_ZB + (sb + 1) * LANE] = (
                dyb * _unstack_heads(ost_ref, sb, lo) * (sig * (1.0 + zb * (1.0 - sig)))).astype(BF16)
            dq_r = _unstack_heads(dqst_ref, sb, lo) * SCALE
            dp_ref[:, OFF_Q + sb * LANE:OFF_Q + (sb + 1) * LANE] = _unrope(dq_r, cos_t, sin_t, first_half).astype(BF16)

        lo2 = lax.broadcasted_iota(jnp.int32, (2 * BLK, LANE), 1) < HEAD
        for ks in range(2):
            cols = slice(ks * LANE, (ks + 1) * LANE)
            ka = dkdup_ref[2 * ks]
            kb = dkdup_ref[2 * ks + 1]
            dk_band = jnp.where(lo2, ka + pltpu.roll(ka, HEAD, 1), kb + pltpu.roll(kb, HEAD, 1))
            va_ = dvdup_ref[2 * ks]
            vb_ = dvdup_ref[2 * ks + 1]
            dv_band = jnp.where(lo2, va_ + pltpu.roll(va_, HEAD, 1), vb_ + pltpu.roll(vb_, HEAD, 1))
            dkr = dk_band[BLK:2 * BLK, :] + kcar_ref[:, cols]
            dp_ref[:, OFF_K + ks * LANE:OFF_K + (ks + 1) * LANE] = _unrope(dkr, cos_t, sin_t, first_half).astype(BF16)
            dp_ref[:, OFF_V + ks * LANE:OFF_V + (ks + 1) * LANE] = (
                dv_band[BLK:2 * BLK, :] + vcar_ref[:, cols]).astype(BF16)
            kcar_ref[:, cols] = dk_band[0:BLK, :]
            vcar_ref[:, cols] = dv_band[0:BLK, :]

    tab = pl.BlockSpec((BLK, LANE), lambda i: (rev(i), 0))
    tabp = pl.BlockSpec((BLK, LANE), lambda i: (prev(i), 0))
    kvp = lambda col: pl.BlockSpec((BLK, D_KV), lambda i: (prev(i), col))
    vec = pl.BlockSpec((1, D_A), lambda i: (0, 0))
    w3 = pl.BlockSpec((GROUPS, BLK, BLK), lambda i: (0, 0, 0))
    return pl.pallas_call(
        body, name="mix_bwd", grid=(nb,),
        in_specs=_proj_specs(nb) + [
            kvp(OFF_K // D_KV), kvp(OFF_V // D_KV), pl.BlockSpec((BLK, 2 * D_A), lambda i: (rev(i), 0)),
            tab, tab, tabp, tabp, vec, vec, w3, w3, pl.BlockSpec((BLK, GROUPS), lambda i: (0, 0)),
            pl.BlockSpec(memory_space=pltpu.SMEM)],
        out_specs=[pl.BlockSpec((BLK, D_IN), lambda i: (rev(i), 0)), pl.BlockSpec((8, D_A), lambda i: (0, 0)), w3,
                   pl.BlockSpec((BLK, LANE), lambda i: (0, 0)), pl.BlockSpec((8, LANE), lambda i: (0, 0))],
        out_shape=[jax.ShapeDtypeStruct((s, D_IN), BF16), jax.ShapeDtypeStruct((8, D_A), F32),
                   jax.ShapeDtypeStruct((GROUPS, BLK, BLK), F32), jax.ShapeDtypeStruct((BLK, LANE), F32),
                   jax.ShapeDtypeStruct((8, LANE), F32)],
        scratch_shapes=[pltpu.VMEM((N_KV, 2 * BLK, LANE), BF16), pltpu.VMEM((N_KV, 2 * BLK, LANE), BF16),
                        pltpu.VMEM((BLK, D_A), F32), pltpu.VMEM((N_KV, Q_PER_KV * BLK, LANE), BF16),
                        pltpu.VMEM((N_KV, Q_PER_KV * BLK, LANE), F32), pltpu.VMEM((N_KV, Q_PER_KV * BLK, LANE), F32),
                        pltpu.VMEM((N_KV, Q_PER_KV * BLK, LANE), F32), pltpu.VMEM((N_KV, 2 * BLK, LANE), F32),
                        pltpu.VMEM((N_KV, 2 * BLK, LANE), F32), pltpu.VMEM((BLK, D_KV), F32), pltpu.VMEM((BLK, D_KV), F32),
                        pltpu.VMEM((BLK, D_B), F32), pltpu.VMEM((2, Q_PER_KV * BLK, 2 * BLK), F32)],
        compiler_params=_params("arbitrary"),
    )(proj, proj, proj, proj, proj, proj, proj, proj, proj, proj, dy, cos, sin, cos, sin, ln_g, ln_b, w_sp, w_sp_t,
      b_sp_t, sinks)


def _dh_call(dproj, w_bf, x, dx2, scale, norm_g):
    s = x.shape[0]
    tm = min(s, 512)
    tk = W_IN_SHARD
    nk = D_IN // tk

    def body(dp_ref, w_ref, x_ref, dx2_ref, sc_ref, g_ref, gx_ref, st_ref, acc_ref):
        i = pl.program_id(0)
        k = pl.program_id(1)

        @pl.when((i == 0) & (k == 0))
        def _():
            st_ref[...] = jnp.zeros((8, D), F32)

        @pl.when(k == 0)
        def _():
            acc_ref[...] = jnp.zeros((tm, D), F32)

        acc_ref[...] += lax.dot_general(dp_ref[...], w_ref[...], NT, preferred_element_type=F32)

        @pl.when(k == nk - 1)
        def _():
            g = g_ref[...]
            one_sc = 1.0 + sc_ref[...]

            def chunk(n, carry):
                rows = pl.ds(pl.multiple_of(n * BLK, BLK), BLK)
                dh = acc_ref[rows, :]
                xv = x_ref[rows, :]
                r = lax.rsqrt(jnp.mean(xv * xv, axis=-1, keepdims=True) + EPS)
                xn = xv * r
                dhn = dh * one_sc
                dxn = dhn * g
                gx_ref[rows, :] = dx2_ref[rows, :] + r * (dxn - xn * jnp.mean(dxn * xn, axis=-1, keepdims=True))
                st_ref[0:1, :] += jnp.sum(dh, axis=0, keepdims=True)
                st_ref[1:2, :] += jnp.sum(dh * (xn * g), axis=0, keepdims=True)
                st_ref[2:3, :] += jnp.sum(dhn * xn, axis=0, keepdims=True)
                return carry

            lax.fori_loop(0, tm // BLK, chunk, 0)

    vec = pl.BlockSpec((1, D), lambda i, k: (0, 0))
    rows = lambda: pl.BlockSpec((tm, D), lambda i, k: (i, 0))
    return pl.pallas_call(
        body, name="dh", grid=(s // tm, nk),
        in_specs=[pl.BlockSpec((tm, tk), lambda i, k: (i, k)), pl.BlockSpec((D, tk), lambda i, k: (0, k)), rows(), rows(), vec, vec],
        out_specs=[rows(), pl.BlockSpec((8, D), lambda i, k: (0, 0))],
        out_shape=[jax.ShapeDtypeStruct((s, D), F32), jax.ShapeDtypeStruct((8, D), F32)],
        scratch_shapes=[pltpu.VMEM((tm, D), F32)],
        compiler_params=_params("arbitrary", "arbitrary"),
    )(dproj, w_bf, x, dx2, scale, norm_g)


def _adam_math(w, g, m, v):
    m_new = ADAM_B1 * m + (1.0 - ADAM_B1) * g
    v_new = ADAM_B2 * v + (1.0 - ADAM_B2) * (g * g)
    m_hat = m_new / ADAM_C1
    v_hat = v_new / ADAM_C2
    delta = -ADAM_LR * (m_hat / (jnp.sqrt(v_hat) + ADAM_EPS) + ADAM_WD * w)
    return delta, m_new, v_new


def _adam_small_call(tensors):
    n = len(tensors)

    def body(*refs):
        ins, outs = refs[:4 * n], refs[4 * n:]
        for t in range(n):
            w_ref, g_ref, m_ref, v_ref = ins[4 * t:4 * t + 4]
            d, mo, vo = _adam_math(w_ref[...], g_ref[...], m_ref[...], v_ref[...])
            outs[3 * t][...], outs[3 * t + 1][...], outs[3 * t + 2][...] = d, mo, vo

    vm = pl.BlockSpec(memory_space=pltpu.VMEM)
    flat = [a for t in tensors for a in t]
    out = pl.pallas_call(
        body, name="adam_small", in_specs=[vm] * (4 * n), out_specs=[vm] * (3 * n),
        out_shape=[jax.ShapeDtypeStruct(t[0].shape, F32) for t in tensors for _ in range(3)],
        compiler_params=pltpu.CompilerParams(vmem_limit_bytes=VMEM_LIMIT),
    )(*flat)
    return [tuple(out[3 * t:3 * t + 3]) for t in range(n)]


def _adam_halves_call(pos, w, mine, theirs, m, v, name):
    r, n = w.shape
    half = r // 2
    tr = 128
    nh = half // tr

    def body(pos_ref, w_ref, mine_ref, theirs_ref, m_ref, v_ref, g_ref, d_ref, mo_ref, vo_ref):
        is_mine = (pl.program_id(0) // nh) == pos_ref[1]
        g = jnp.where(is_mine, mine_ref[...], theirs_ref[...])
        g_ref[...] = g
        d_ref[...], mo_ref[...], vo_ref[...] = _adam_math(w_ref[...], g, m_ref[...], v_ref[...])

    spec = lambda: pl.BlockSpec((tr, n), lambda i, pos: (i, 0))
    hspec = lambda: pl.BlockSpec((tr, n), lambda i, pos: (i % nh, 0))
    return pl.pallas_call(
        body, name=name,
        grid_spec=pltpu.PrefetchScalarGridSpec(
            num_scalar_prefetch=1, grid=(r // tr,), in_specs=[spec(), hspec(), hspec(), spec(), spec()],
            out_specs=[spec() for _ in range(4)]),
        out_shape=[jax.ShapeDtypeStruct((r, n), F32)] * 4, compiler_params=_params("parallel"),
    )(pos, w, mine, theirs, m, v)


def _adam_outer_call(w, ct, dm, m, v, name):
    r, n = w.shape
    tr = 128

    def body(w_ref, ct_ref, dm_ref, m_ref, v_ref, g_ref, d_ref, mo_ref, vo_ref):
        g = ct_ref[:, 0:1] * dm_ref[0:1, :]
        for b in range(1, N_DEV):
            g = g + ct_ref[:, b:b + 1] * dm_ref[b:b + 1, :]
        g_ref[...] = g
        d_ref[...], mo_ref[...], vo_ref[...] = _adam_math(w_ref[...], g, m_ref[...], v_ref[...])

    spec = lambda: pl.BlockSpec((tr, n), lambda i: (i, 0))
    return pl.pallas_call(
        body, name=name, grid=(r // tr,),
        in_specs=[spec(), pl.BlockSpec((tr, N_DEV), lambda i: (i, 0)), pl.BlockSpec((N_DEV, n), lambda i: (0, 0)), spec(), spec()],
        out_specs=[spec() for _ in range(4)],
        out_shape=[jax.ShapeDtypeStruct((r, n), F32)] * 4, compiler_params=_params("parallel"),
    )(w, ct, dm, m, v)


def _sum_pieces_call(pos, part, part_block, recvs, name):
    r, n = recvs[0].shape[1:]
    tr = min(r, 256)
    nrb = r // tr

    def body(pos_ref, p_ref, *refs):
        acc = p_ref[...].astype(F32)
        for r_ref in refs[:-1]:
            for d in range(r_ref.shape[0]):
                acc = acc + r_ref[d].astype(F32)
        refs[-1][...] = acc

    return pl.pallas_call(
        body, name=name,
        grid_spec=pltpu.PrefetchScalarGridSpec(
            num_scalar_prefetch=1, grid=(nrb,),
            in_specs=[pl.BlockSpec((tr, n), lambda i, pos: part_block(i, pos, nrb))] + [
                pl.BlockSpec((rv.shape[0], tr, n), lambda i, pos: (0, i, 0)) for rv in recvs],
            out_specs=pl.BlockSpec((tr, n), lambda i, pos: (i, 0))),
        out_shape=jax.ShapeDtypeStruct((r, n), F32), compiler_params=_params("parallel"),
    )(pos, part, *recvs)


def _coords():
    return lax.axis_index("x"), lax.axis_index("y"), lax.axis_index("c")


def _allgather_sum_call(blk, name, with_sum):
    m_per, n = blk.shape

    def body(x_ref, out_ref, *rest):
        if with_sum:
            sum_ref, send_sems, recv_sems, local_sem = rest
        else:
            send_sems, recv_sems, local_sem = rest
        x, y, c = _coords()
        me, sibling = (x, y, c), (x, y, 1 - c)
        chips = [(1 - x, y), (x, 1 - y), (1 - x, 1 - y)]

        def rows(px, py, pc):
            return out_ref.at[pl.ds((4 * px + 2 * py + pc) * m_per, m_per), :]

        def copy(k, block, to, src=None):
            return pltpu.make_async_remote_copy(
                src_ref=rows(*block) if src is None else src, dst_ref=rows(*block),
                send_sem=send_sems.at[k], recv_sem=recv_sems.at[k], device_id=to, device_id_type=MESH)

        mine = pltpu.make_async_copy(x_ref, rows(*me), local_sem)
        mine.start()
        first = [copy(0, me, sibling, src=x_ref)]
        first += [copy(1 + j, me, (*chip, c), src=x_ref) for j, chip in enumerate(chips)]
        for cp in first:
            cp.start()
        passed = [copy(4 + j, (*chip, c), sibling) for j, chip in enumerate(chips)]
        for j, chip in enumerate(chips):
            copy(1 + j, (*chip, c), me).wait_recv()
            passed[j].start()
        copy(0, sibling, me).wait_recv()
        for j, chip in enumerate(chips):
            copy(4 + j, (*chip, 1 - c), me).wait_recv()
        for cp in first + passed:
            cp.wait_send()
        mine.wait()
        if with_sum:
            acc = out_ref[0:m_per, :]
            for d in range(1, N_DEV):
                acc = acc + out_ref[d * m_per:(d + 1) * m_per, :]
            sum_ref[...] = acc

    vm = pl.BlockSpec(memory_space=pltpu.VMEM)
    out_shape = [jax.ShapeDtypeStruct((N_DEV * m_per, n), F32)]
    if with_sum:
        out_shape.append(jax.ShapeDtypeStruct((m_per, n), F32))
    return pl.pallas_call(
        body, name=name, out_shape=out_shape, in_specs=[vm], out_specs=[vm] * len(out_shape),
        scratch_shapes=[pltpu.SemaphoreType.DMA((7,)), pltpu.SemaphoreType.DMA((7,)), pltpu.SemaphoreType.DMA],
        compiler_params=pltpu.CompilerParams(vmem_limit_bytes=VMEM_LIMIT),
    )(blk)


def _weights_gather_call(wi_full, wo_full):
    hi = D // 2
    ho = W_OUT_SHARD // 2

    def body(wi_in, wo_in, fi_ref, fo_ref, send_sems, recv_sems):
        del wi_in, wo_in
        x, y, c = _coords()
        sibling = (x, y, 1 - c)
        chips = [(1 - x, y), (x, 1 - y), (1 - x, 1 - y)]

        def half(which, px, py, pc):
            j = 2 * px + py
            if which == 0:
                return fi_ref.at[pl.ds(pc * hi, hi), pl.ds(j * W_IN_SHARD, W_IN_SHARD)]
            return fo_ref.at[pl.ds(j * W_OUT_SHARD + pc * ho, ho), :]

        def copy(k, which, block, to):
            return pltpu.make_async_remote_copy(
                src_ref=half(which, *block), dst_ref=half(which, *block), send_sem=send_sems.at[k],
                recv_sem=recv_sems.at[k], device_id=to, device_id_type=MESH)

        first = [copy(6 * w + j, w, (x, y, c), (*chip, c)) for w in range(2) for j, chip in enumerate(chips)]
        for cp in first:
            cp.start()
        passed = []
        for w in range(2):
            for j, chip in enumerate(chips):
                copy(6 * w + j, w, (*chip, c), (x, y, c)).wait_recv()
                cp = copy(6 * w + 3 + j, w, (*chip, c), sibling)
                cp.start()
                passed.append(cp)
        for w in range(2):
            for j, chip in enumerate(chips):
                copy(6 * w + 3 + j, w, (*chip, 1 - c), (x, y, c)).wait_recv()
        for cp in first + passed:
            cp.wait_send()

    anyspec = pl.BlockSpec(memory_space=pl.ANY)
    return pl.pallas_call(
        body, name="weights_gather",
        out_shape=[jax.ShapeDtypeStruct((D, D_IN), BF16), jax.ShapeDtypeStruct((D, D), BF16)],
        in_specs=[anyspec, anyspec], out_specs=[anyspec, anyspec], input_output_aliases={0: 0, 1: 1},
        scratch_shapes=[pltpu.SemaphoreType.DMA((12,)), pltpu.SemaphoreType.DMA((12,))],
    )(wi_full, wo_full)


HBM_SPEC = pl.BlockSpec(memory_space=pltpu.HBM)
SEM_SPEC = pl.BlockSpec(memory_space=pltpu.SEMAPHORE)
SIDE_EFFECT = pltpu.SideEffectType.DATAFLOW_SIDE_EFFECTING


def _peer(x, y, c, q, cb):
    return (1 - x if q & 2 else x, 1 - y if q & 1 else y, 1 - c if cb else c)


def _w_in_piece(slots):
    def piece(part_ref, k, to):
        return part_ref.at[pl.ds(to[2] * (D // 2), D // 2), pl.ds(slots[k] * W_IN_SHARD, W_IN_SHARD)]
    return piece


def _w_out_piece(part_ref, k, to):
    ho = W_OUT_SHARD // 2
    return part_ref.at[pl.ds((2 * to[0] + to[1]) * W_OUT_SHARD + to[2] * ho, ho), :]


def _group_piece(part_ref, k, to):
    return part_ref.at[4 * to[0] + 2 * to[1] + to[2]]


def _whole_piece(part_ref, k, to):
    return part_ref


def _exchange_start_call(part, rels, piece, slot_shape, name):
    n = len(rels)
    land = lax.empty((n,) + slot_shape, part.dtype)

    def body(part_ref, land_ref, send_sems, recv_sems, part_thru, land_thru, token):
        x, y, c = _coords()
        for k, (q, cb) in enumerate(rels):
            to = _peer(x, y, c, q, cb)
            pltpu.make_async_remote_copy(src_ref=piece(part_ref, k, to), dst_ref=land_ref.at[k], send_sem=send_sems.at[k],
                                         recv_sem=recv_sems.at[k], device_id=to, device_id_type=MESH).start()
        token[...] = jnp.zeros_like(token)

    return pl.pallas_call(
        body, name=name,
        out_shape=(pltpu.SemaphoreType.DMA((n,)), pltpu.SemaphoreType.DMA((n,)), pltpu.HBM(part.shape, part.dtype),
                   pltpu.HBM(land.shape, land.dtype), jax.ShapeDtypeStruct((8, LANE), F32)),
        in_specs=(HBM_SPEC, HBM_SPEC), out_specs=(SEM_SPEC, SEM_SPEC, HBM_SPEC, HBM_SPEC, pl.BlockSpec(memory_space=pltpu.VMEM)),
        input_output_aliases={0: 2, 1: 3},
        compiler_params=pltpu.CompilerParams(has_side_effects=SIDE_EFFECT),
    )(pltpu.with_memory_space_constraint(part, pltpu.HBM), pltpu.with_memory_space_constraint(land, pltpu.HBM))


def _exchange_wait_call(started, rels, piece, after, name):
    send_sems, recv_sems, part_thru, land_thru, _ = started

    def body(part_ref, land_ref, send_sems, recv_sems, after_ref, part_out, land_out):
        x, y, c = _coords()
        for k, (q, cb) in enumerate(rels):
            to = _peer(x, y, c, q, cb)
            cp = pltpu.make_async_remote_copy(src_ref=piece(part_ref, k, to), dst_ref=land_ref.at[k], send_sem=send_sems.at[k],
                                              recv_sem=recv_sems.at[k], device_id=to, device_id_type=MESH)
            cp.wait_send()
            cp.wait_recv()

    return pl.pallas_call(
        body, name=name,
        out_shape=(pltpu.HBM(part_thru.shape, part_thru.dtype), pltpu.HBM(land_thru.shape, land_thru.dtype)),
        in_specs=(HBM_SPEC, HBM_SPEC, SEM_SPEC, SEM_SPEC, pl.BlockSpec(memory_space=pl.ANY)), out_specs=(HBM_SPEC, HBM_SPEC),
        input_output_aliases={0: 0, 1: 1},
        compiler_params=pltpu.CompilerParams(has_side_effects=SIDE_EFFECT),
    )(part_thru, land_thru, send_sems, recv_sems, after)


def _pair_exchange_call(gi, go):
    hi = D // 2
    ho = W_OUT_SHARD // 2

    def body(gi_in, go_in, fi_ref, fo_ref, send_sems, recv_sems):
        del gi_in, go_in
        x, y, c = _coords()
        sibling = (x, y, 1 - c)
        mine = (fi_ref.at[pl.ds(c * hi, hi), :], fo_ref.at[pl.ds(c * ho, ho), :])
        theirs = (fi_ref.at[pl.ds((1 - c) * hi, hi), :], fo_ref.at[pl.ds((1 - c) * ho, ho), :])
        sends = [pltpu.make_async_remote_copy(src_ref=ref, dst_ref=ref, send_sem=send_sems.at[k], recv_sem=recv_sems.at[k],
                                              device_id=sibling, device_id_type=MESH) for k, ref in enumerate(mine)]
        for cp in sends:
            cp.start()
        for k, ref in enumerate(theirs):
            pltpu.make_async_remote_copy(src_ref=ref, dst_ref=ref, send_sem=send_sems.at[k], recv_sem=recv_sems.at[k],
                                         device_id=sibling, device_id_type=MESH).wait_recv()
        for cp in sends:
            cp.wait_send()

    anyspec = pl.BlockSpec(memory_space=pl.ANY)
    return pl.pallas_call(
        body, name="pair_exchange",
        out_shape=[jax.ShapeDtypeStruct((D, W_IN_SHARD), F32), jax.ShapeDtypeStruct((W_OUT_SHARD, D), F32)],
        in_specs=[anyspec, anyspec], out_specs=[anyspec, anyspec], input_output_aliases={0: 0, 1: 1},
        scratch_shapes=[pltpu.SemaphoreType.DMA((2,)), pltpu.SemaphoreType.DMA((2,))],
    )(gi, go)


def _rope_tables(s):
    inv_freq = 10000.0 ** (-jnp.arange(0, HEAD, 2, dtype=F32) / HEAD)
    ang = jnp.arange(s, dtype=F32)[:, None] * inv_freq[None, :]
    cos = jnp.tile(jnp.cos(ang), (1, LANE // (HEAD // 2)))
    sin = jnp.tile(jnp.sin(ang), (1, LANE // (HEAD // 2)))
    first_half = (jnp.arange(LANE) % HEAD) < (HEAD // 2)
    return cos, jnp.where(first_half[None, :], -sin, sin)


def _pad_cols(a, n):
    return jnp.pad(a, ((0, 0), (0, n - a.shape[1])))


def kernel(x, c, w_ada, b_ada, norm_g, w_in, ln_v_g, ln_v_b, w_spatial, b_spatial, sinks, w_out, w_ada_final, b_ada_final, final_norm_g, loss_target, m_w_ada, m_b_ada, m_norm_g, m_w_in, m_ln_v_g, m_ln_v_b, m_w_spatial, m_b_spatial, m_sinks, m_w_out, m_w_ada_final, m_b_ada_final, m_final_norm_g, v_w_ada, v_b_ada, v_norm_g, v_w_in, v_ln_v_g, v_ln_v_b, v_w_spatial, v_b_spatial, v_sinks, v_w_out, v_w_ada_final, v_b_ada_final, v_final_norm_g):
    s = x.shape[1]
    ax, ay, ac = _coords()
    chip = 2 * ax + ay
    me = 4 * ax + 2 * ay + ac
    n_ada = w_ada.shape[2]
    n_adaf = w_ada_final.shape[1]

    x2d = x.reshape(s, D)
    tgt = loss_target.reshape(s, D)
    w_ada2, w_in2, w_out2 = w_ada[0], w_in[0], w_out[0]
    b_ada_f2 = b_ada_final.reshape(1, 2 * D)
    gf = final_norm_g.reshape(1, D)

    c_all = _allgather_sum_call(jnp.pad(c, ((0, 7), (0, 0))), "gather_c", False)[0][::8]
    mod_p, c_act = _rowmat_call(c_all, w_ada2, lax.dynamic_slice(b_ada, (0, chip * n_ada), (1, n_ada)), "mod")
    modf_p, _ = _rowmat_call(c_all, w_ada_final, lax.dynamic_slice(b_ada_f2, (0, chip * n_adaf), (1, n_adaf)), "mod_final")
    mods = _allgather_sum_call(jnp.concatenate([mod_p, modf_p], axis=1), "gather_mod", False)[0]
    my_rows = [lax.dynamic_slice(mods, (16 * j + me, 0), (1, n_ada + n_adaf)) for j in range(N_CHIP)]
    mod = jnp.concatenate([r[:, :n_ada] for r in my_rows], axis=1)
    mod_f = jnp.concatenate([r[:, n_ada:] for r in my_rows], axis=1)
    shift, scale, gate = mod[:, :D], mod[:, D:2 * D], mod[:, 2 * D:]
    shift_f, scale_f = mod_f[:, :D], mod_f[:, D:]

    pos = jnp.stack([chip, ac]).astype(jnp.int32)
    w_in_own = _cast_into_call(pos, w_in2, (D, D_IN), "cast_w_in")
    w_out_own = _cast_into_call(pos, w_out2, (D, D), "cast_w_out")

    cos, sin = _rope_tables(s)
    b_sp_t = b_spatial[0].T
    sinks1 = sinks.reshape(N_Q)
    proj, h, w_in_bf, w_out_bf = _proj_gather_call(pos, x2d, shift, scale, norm_g, w_in_own, w_out_own)
    y = _mix_fwd_call(proj, cos, sin, ln_v_g, ln_v_b, w_spatial[0], b_sp_t, sinks1)
    dx2, do, st_tail = _tail_call(y, w_out_bf, x2d, tgt, gate, shift_f, scale_f, gf)

    rel_o = [(0, 1), (1, 0), (1, 1), (2, 0), (2, 1), (3, 0), (3, 1)]
    rel_a = [(1, 0), (1, 1), (2, 0), (2, 1)]
    rel_b = [(3, 0), (3, 1), (0, 1)]
    piece_a, piece_b = _w_in_piece([0, 0, 1, 1]), _w_in_piece([0, 0, 1])
    half_in, half_out = (D // 2, W_IN_SHARD), (W_OUT_SHARD // 2, D)

    g_w_out_p = _tn_call(y, do, "grad_w_out")
    st_o = _exchange_start_call(g_w_out_p, rel_o, _w_out_piece, half_out, "send_w_out")
    dy = _dy_call(do, w_out_bf)
    dproj, st_ln, d_wsp, d_bsp_t, d_sink = _mix_bwd_call(
        proj, dy, cos, sin, ln_v_g + st_o[4][0:1, 0:1], ln_v_b, w_spatial[0], jnp.swapaxes(w_spatial[0], 1, 2), b_sp_t, sinks1)
    g_w_in_a = _tn_shards_call(pos, h, dproj, (1, 2), "grad_w_in_a")
    st_a = _exchange_start_call(g_w_in_a, rel_a, piece_a, half_in, "send_w_in_a")
    g_w_in_b = _tn_shards_call(pos, h, dproj, (3, 0), "grad_w_in_b")
    st_b = _exchange_start_call(g_w_in_b, rel_b, piece_b, half_in, "send_w_in_b")
    rel_all = rel_o
    st_s = _exchange_start_call(d_wsp, rel_all, _group_piece, (BLK, BLK), "send_w_spatial")
    sent = st_a[4][0:1, 0:1] + st_b[4][0:1, 0:1] + st_s[4][0:1, 0:1]
    grad_x, st_dh = _dh_call(dproj, w_in_bf, x2d, dx2, scale + sent, norm_g)

    g_w_out_p, recv_o = _exchange_wait_call(st_o, rel_o, _w_out_piece, st_dh, "wait_w_out")
    _, recv_a = _exchange_wait_call(st_a, rel_a, piece_a, st_dh, "wait_w_in_a")
    g_w_in_b, recv_b = _exchange_wait_call(st_b, rel_b, piece_b, st_dh, "wait_w_in_b")
    d_wsp, recv_s = _exchange_wait_call(st_s, rel_all, _group_piece, st_dh, "wait_w_spatial")
    mine_in = _sum_pieces_call(pos, g_w_in_b, lambda i, p, nrb: (p[1] * nrb + i, 1), [recv_a, recv_b], "sum_w_in")
    mine_out = _sum_pieces_call(pos, g_w_out_p, lambda i, p, nrb: ((2 * p[0] + p[1]) * nrb + i, 0), [recv_o], "sum_w_out")
    wsp_group = _sum_pieces_call(pos, d_wsp.reshape(GROUPS * BLK, BLK), lambda i, p, nrb: (2 * p[0] + p[1], 0), [recv_s],
                                 "sum_w_spatial")
    to_sibling = [(0, 1)]
    st_pi = _exchange_start_call(mine_in, to_sibling, _whole_piece, half_in, "swap_w_in")
    st_po = _exchange_start_call(mine_out, to_sibling, _whole_piece, half_out, "swap_w_out")

    misc = jnp.concatenate([st_ln, d_bsp_t[:, :GROUPS].T, d_sink, jnp.zeros((8, D - D_A - 2 * LANE), F32)], axis=1)
    pack = jnp.concatenate([wsp_group.reshape(8, D) + (st_pi[4][0:1, 0:1] + st_po[4][0:1, 0:1]), st_tail, st_dh, misc], axis=0)
    rows = pack.shape[0]
    packs, tot = _allgather_sum_call(pack, "gather_small", True)
    packs = packs.reshape(N_DEV, rows, D)
    dmod_all = jnp.concatenate([packs[:, 16, :], packs[:, 17, :], packs[:, 11, :]], axis=1)
    dmodf_all = jnp.concatenate([packs[:, 8, :], packs[:, 9, :]], axis=1)
    loss = tot[13, 0]
    mine_in, theirs_in = _exchange_wait_call(st_pi, to_sibling, _whole_piece, tot, "swapped_w_in")
    mine_out, theirs_out = _exchange_wait_call(st_po, to_sibling, _whole_piece, tot, "swapped_w_out")
    small = {
        "b_ada": jnp.concatenate([tot[16:17], tot[17:18], tot[11:12]], axis=1),
        "norm_g": tot[18:19],
        "ln_v_g": tot[24:25, :D_A],
        "ln_v_b": tot[25:26, :D_A],
        "w_spatial": packs[:, 0:8, :].reshape(GROUPS * BLK, BLK),
        "b_spatial": tot[24:32, D_A:D_A + BLK],
        "sinks": tot[24:25, D_A + LANE:D_A + LANE + N_Q],
        "b_ada_final": jnp.concatenate([tot[8:9], tot[9:10]], axis=1),
        "final_norm_g": tot[10:11],
    }

    weights = dict(w_ada=w_ada, b_ada=b_ada, norm_g=norm_g, w_in=w_in, ln_v_g=ln_v_g, ln_v_b=ln_v_b, w_spatial=w_spatial,
                   b_spatial=b_spatial, sinks=sinks, w_out=w_out, w_ada_final=w_ada_final, b_ada_final=b_ada_final,
                   final_norm_g=final_norm_g)
    m_in = dict(w_ada=m_w_ada, b_ada=m_b_ada, norm_g=m_norm_g, w_in=m_w_in, ln_v_g=m_ln_v_g, ln_v_b=m_ln_v_b,
                w_spatial=m_w_spatial, b_spatial=m_b_spatial, sinks=m_sinks, w_out=m_w_out, w_ada_final=m_w_ada_final,
                b_ada_final=m_b_ada_final, final_norm_g=m_final_norm_g)
    v_in = dict(w_ada=v_w_ada, b_ada=v_b_ada, norm_g=v_norm_g, w_in=v_w_in, ln_v_g=v_ln_v_g, ln_v_b=v_ln_v_b,
                w_spatial=v_w_spatial, b_spatial=v_b_spatial, sinks=v_sinks, w_out=v_w_out, w_ada_final=v_w_ada_final,
                b_ada_final=v_b_ada_final, final_norm_g=v_final_norm_g)
    c_act_t = c_act.T
    outer = {"w_ada": lax.dynamic_slice(dmod_all, (0, chip * n_ada), (N_DEV, n_ada)),
             "w_ada_final": lax.dynamic_slice(dmodf_all, (0, chip * n_adaf), (N_DEV, n_adaf))}
    halves = {"w_in": (mine_in, theirs_in[0]), "w_out": (mine_out, theirs_out[0])}
    done = {}
    for name, (mine, theirs) in halves.items():
        shape2 = (2 * mine.shape[0], mine.shape[1])
        done[name] = _adam_halves_call(pos, weights[name].reshape(shape2), mine, theirs, m_in[name].reshape(shape2),
                                       v_in[name].reshape(shape2), "adam_" + name)
    for name, dm in outer.items():
        shape2 = (D, dm.shape[1])
        done[name] = _adam_outer_call(weights[name].reshape(shape2), c_act_t, dm, m_in[name].reshape(shape2),
                                      v_in[name].reshape(shape2), "adam_" + name)
    updates = _adam_small_call([(weights[name].reshape(g.shape), g, m_in[name].reshape(g.shape), v_in[name].reshape(g.shape))
                                for name, g in small.items()])
    for (name, g), upd in zip(small.items(), updates):
        done[name] = (g, *upd)
    outs = [[done[name][k].reshape(w.shape) for name, w in weights.items()] for k in range(4)]
    return (loss, grad_x.reshape(x.shape), *outs[0], *outs[1], *outs[2], *outs[3])
```

```python
import jax
import jax.numpy as jnp
from jax import lax
from jax.experimental import pallas as pl
from jax.experimental.pallas import tpu as pltpu

F32 = jnp.float32
BF16 = jnp.bfloat16
MESH = pl.DeviceIdType.MESH

D = 2048
D_A = 1024
D_B = 1024
D_KV = 256
HEAD = 64
N_Q = 16
N_KV = 4
Q_PER_KV = N_Q // N_KV
BLK = 128
GROUPS = 8
D_IN = 5632
OFF_Q, OFF_K, OFF_V, OFF_ZB = 3072, 4096, 4352, 4608
N_CHIP = 4
N_DEV = 8
W_IN_SHARD = D_IN // N_CHIP
W_OUT_SHARD = D // N_CHIP
EPS = 1e-5
SCALE = HEAD ** -0.5
NEG = -1e30
LANE = 128
VMEM_LIMIT = 56 * 1024 * 1024

ADAM_LR, ADAM_B1, ADAM_B2, ADAM_EPS, ADAM_WD, ADAM_STEP = 0.001, 0.9, 0.999, 1e-08, 0.01, 10
ADAM_C1 = 1.0 - ADAM_B1 ** ADAM_STEP
ADAM_C2 = 1.0 - ADAM_B2 ** ADAM_STEP
ADAM_ROWS = 256

NT = (((1,), (1,)), ((), ()))
TN = (((0,), (0,)), ((), ()))


def _params(*sem):
    return pltpu.CompilerParams(dimension_semantics=sem, vmem_limit_bytes=VMEM_LIMIT)


def _silu_parts(z):
    sig = 1.0 / (1.0 + jnp.exp(-z))
    return z * sig, sig


def _swap_halves(v, first_half):
    return jnp.where(first_half, pltpu.roll(v, 96, 1), pltpu.roll(v, 32, 1))


def _rope(v, cos_t, sin_s, first_half):
    return v * cos_t + _swap_halves(v, first_half) * sin_s


def _unrope(dv, cos_t, sin_s, first_half):
    return dv * cos_t - _swap_halves(dv, first_half) * sin_s


def _lane_masks():
    lane = lax.broadcasted_iota(jnp.int32, (BLK, LANE), 1)
    return (lane % HEAD) < (HEAD // 2), lane < HEAD


def _band_valid(first_block_bound, rows=BLK):
    rr = lax.broadcasted_iota(jnp.int32, (rows, 2 * BLK), 0) & (BLK - 1)
    jj = lax.broadcasted_iota(jnp.int32, (rows, 2 * BLK), 1)
    return (jj > rr) & (jj <= rr + BLK) & (jj >= first_block_bound)


def _dup_kv(slab, lo):
    rolled = pltpu.roll(slab, HEAD, 1)
    return jnp.where(lo, slab, rolled).astype(BF16), jnp.where(lo, rolled, slab).astype(BF16)


def _stack_heads(ref, sb, slab, lo, dtype):
    kh, base = sb // 2, 2 * (sb % 2) * BLK
    zero = jnp.zeros_like(slab)
    ref[kh, base:base + BLK, :] = jnp.where(lo, slab, zero).astype(dtype)
    ref[kh, base + BLK:base + 2 * BLK, :] = jnp.where(lo, zero, slab).astype(dtype)


def _unstack_heads(ref, sb, lo):
    kh, base = sb // 2, 2 * (sb % 2) * BLK
    return jnp.where(lo, ref[kh, base:base + BLK, :], ref[kh, base + BLK:base + 2 * BLK, :])


def _sink_column(sinks_ref, kh):
    row = lax.broadcasted_iota(jnp.int32, (Q_PER_KV * BLK, 1), 0)
    col = jnp.full(row.shape, sinks_ref[Q_PER_KV * kh + Q_PER_KV - 1], F32)
    for n in range(Q_PER_KV - 2, -1, -1):
        col = jnp.where(row < (n + 1) * BLK, sinks_ref[Q_PER_KV * kh + n], col)
    return col


def _tril():
    t = lax.broadcasted_iota(jnp.int32, (BLK, BLK), 0)
    s = lax.broadcasted_iota(jnp.int32, (BLK, BLK), 1)
    return s <= t


def _layer_norm_fwd(va, lg, lb):
    mu = jnp.mean(va, axis=-1, keepdims=True)
    xc = va - mu
    rstd = lax.rsqrt(jnp.mean(xc * xc, axis=-1, keepdims=True) + EPS)
    vhat = xc * rstd
    return vhat, rstd, vhat * lg + lb


def _softmax_sink(qm, kdup, bias, sink):
    s = lax.dot_general(qm, kdup, NT, preferred_element_type=F32) + bias
    m = jnp.maximum(jnp.max(s, axis=-1, keepdims=True), sink)
    p = jnp.exp(s - m)
    esink = jnp.exp(sink - m)
    inv = 1.0 / (jnp.sum(p, axis=-1, keepdims=True) + esink)
    return p * inv, esink * inv


def _band_bias(bias_ref):
    rows = bias_ref.shape[1]
    bias_ref[0] = jnp.where(_band_valid(BLK, rows), 0.0, NEG)
    bias_ref[1] = jnp.where(_band_valid(0, rows), 0.0, NEG)


def _rowmat_call(c_all, w, b, name):
    n = w.shape[1]
    tn = 512

    def body(c_ref, w_ref, b_ref, o_ref, ca_ref):
        ca, _ = _silu_parts(c_ref[...])
        ca_ref[...] = ca
        o_ref[...] = jnp.dot(ca.astype(BF16), w_ref[...].astype(BF16), preferred_element_type=F32) + b_ref[...]

    return pl.pallas_call(
        body, name=name, grid=(n // tn,),
        in_specs=[pl.BlockSpec((N_DEV, D), lambda j: (0, 0)), pl.BlockSpec((D, tn), lambda j: (0, j)),
                  pl.BlockSpec((1, tn), lambda j: (0, j))],
        out_specs=[pl.BlockSpec((N_DEV, tn), lambda j: (0, j)), pl.BlockSpec((N_DEV, D), lambda j: (0, 0))],
        out_shape=[jax.ShapeDtypeStruct((N_DEV, n), F32), jax.ShapeDtypeStruct((N_DEV, D), F32)],
        compiler_params=_params("arbitrary"),
    )(c_all, w, b)


def _cast_into_call(pos, w, full_shape, name):
    r, n = w.shape
    tr = min(r, 512)
    by_cols = full_shape[0] == r
    nrb = r // tr

    def body(pos_ref, w_ref, o_ref):
        o_ref[...] = w_ref[...].astype(BF16)

    out_map = (lambda i, pos: (i, pos[0])) if by_cols else (lambda i, pos: (pos[0] * nrb + i, 0))
    return pl.pallas_call(
        body, name=name,
        grid_spec=pltpu.PrefetchScalarGridSpec(
            num_scalar_prefetch=1, grid=(nrb,),
            in_specs=[pl.BlockSpec((tr, n), lambda i, pos: (i, 0))], out_specs=pl.BlockSpec((tr, n), out_map)),
        out_shape=jax.ShapeDtypeStruct(full_shape, BF16), compiler_params=_params("parallel"),
    )(pos, w)


def _proj_call(x, shift, scale, norm_g, w_bf):
    s = x.shape[0]
    tm = min(s, 1024)
    tn = 512

    def body(x_ref, sh_ref, sc_ref, g_ref, w_ref, proj_ref, h_ref):
        @pl.when(pl.program_id(1) == 0)
        def _():
            xv = x_ref[...]
            r = lax.rsqrt(jnp.mean(xv * xv, axis=-1, keepdims=True) + EPS)
            h_ref[...] = ((xv * r * g_ref[...]) * (1.0 + sc_ref[...]) + sh_ref[...]).astype(BF16)

        proj_ref[...] = jnp.dot(h_ref[...], w_ref[...], preferred_element_type=F32)

    vec = pl.BlockSpec((1, D), lambda i, j: (0, 0))
    return pl.pallas_call(
        body, name="proj", grid=(s // tm, D_IN // tn),
        in_specs=[pl.BlockSpec((tm, D), lambda i, j: (i, 0)), vec, vec, vec, pl.BlockSpec((D, tn), lambda i, j: (0, j))],
        out_specs=[pl.BlockSpec((tm, tn), lambda i, j: (i, j)), pl.BlockSpec((tm, D), lambda i, j: (i, 0))],
        out_shape=[jax.ShapeDtypeStruct((s, D_IN), F32), jax.ShapeDtypeStruct((s, D), BF16)],
        compiler_params=_params("parallel", "arbitrary"),
    )(x, shift, scale, norm_g, w_bf)


def _proj_gather_call(pos, x, shift, scale, norm_g, wi_full, wo_full):
    s = x.shape[0]
    tm = min(s, 512)
    nrow = s // tm
    hi = D // 2
    ho = W_OUT_SHARD // 2

    def body(pos_ref, x_ref, sh_ref, sc_ref, g_ref, wi_in, wo_in, proj_ref, h_ref, fi_ref, fo_ref,
             h_all, wbuf, send_sems, recv_sems, load_sem):
        del wi_in, wo_in
        p = pl.program_id(0)
        i = pl.program_id(1)
        x_, y_, c_ = _coords()
        me, sibling = (x_, y_, c_), (x_, y_, 1 - c_)

        def shard_of(q):
            px, py, _ = _peer(x_, y_, c_, q, 0)
            return 2 * px + py

        def part(which, q, pc, sub=None):
            n = hi if which == 0 else ho
            base = pc * n
            if sub is not None:
                n //= 2
                base = base + sub * n
            if which == 0:
                return fi_ref.at[pl.ds(base, n), pl.ds(shard_of(q) * W_IN_SHARD, W_IN_SHARD)]
            return fo_ref.at[pl.ds(shard_of(q) * W_OUT_SHARD + base, n), :]

        def copy(k, which, q, pc, to, sub=None):
            ref = part(which, q, pc, sub)
            return pltpu.make_async_remote_copy(src_ref=ref, dst_ref=ref, send_sem=send_sems.at[k], recv_sem=recv_sems.at[k],
                                                device_id=to, device_id_type=MESH)

        def to_neighbour(which, q):
            return copy(8 * which + q - 1, which, 0, c_, _peer(x_, y_, c_, q, 0))

        def from_neighbour(which, q):
            return copy(8 * which + q - 1, which, q, c_, me)

        def relay(which, q):
            return copy(8 * which + 2 + q - 1, which, q, c_, _peer(x_, y_, c_, 3 - q, 0), q - 1)

        def relayed(which, sub):
            return copy(8 * which + 2 + sub, which, 3, c_, me, sub)

        def to_sibling(which, q):
            return copy(8 * which + 4 + q - 1, which, q, c_, sibling)

        def from_sibling(which, q):
            return copy(8 * which + 4 + q - 1, which, q, 1 - c_, me)

        def relayed_to_sibling(which, sub):
            return copy(8 * which + 6 + sub, which, 3, c_, sibling, sub)

        def relayed_from_sibling(which, sub):
            return copy(8 * which + 6 + sub, which, 3, 1 - c_, me, sub)

        def pass_on_neighbours(which):
            for q in (1, 2):
                from_neighbour(which, q).wait_recv()
                to_sibling(which, q).start()
                relay(which, q).start()

        def pass_on_relayed(which):
            for sub in range(2):
                relayed(which, sub).wait_recv()
                relayed_to_sibling(which, sub).start()

        def load_shard(q):
            cp = pltpu.make_async_copy(fi_ref.at[:, pl.ds(shard_of(q) * W_IN_SHARD, W_IN_SHARD)], wbuf, load_sem)
            cp.start()
            cp.wait()

        @pl.when((p == 0) & (i == 0))
        def _():
            for q in (1, 2):
                to_neighbour(0, q).start()
            load_shard(0)

        @pl.when((p == 1) & (i == 0))
        def _():
            pass_on_neighbours(0)
            for q in (1, 2):
                to_neighbour(1, q).start()
            from_sibling(0, 1).wait_recv()
            load_shard(1)

        @pl.when((p == 2) & (i == 0))
        def _():
            from_sibling(0, 2).wait_recv()
            load_shard(2)

        @pl.when((p == 3) & (i == 0))
        def _():
            pass_on_relayed(0)
            pass_on_neighbours(1)
            for sub in range(2):
                relayed_from_sibling(0, sub).wait_recv()
            load_shard(3)

        rows = pl.ds(pl.multiple_of(i * tm, tm), tm)

        @pl.when(p == 0)
        def _():
            xv = x_ref[...]
            r = lax.rsqrt(jnp.mean(xv * xv, axis=-1, keepdims=True) + EPS)
            hv = ((xv * r * g_ref[...]) * (1.0 + sc_ref[...]) + sh_ref[...]).astype(BF16)
            h_ref[...] = hv
            h_all[rows, :] = hv

        proj_ref[...] = jnp.dot(h_all[rows, :], wbuf[...], preferred_element_type=F32)

        @pl.when((p == N_CHIP - 1) & (i == nrow - 1))
        def _():
            pass_on_relayed(1)
            for q in (1, 2):
                from_sibling(1, q).wait_recv()
            for sub in range(2):
                relayed_from_sibling(1, sub).wait_recv()
            for which in range(2):
                for q in (1, 2):
                    to_neighbour(which, q).wait_send()
                    relay(which, q).wait_send()
                    to_sibling(which, q).wait_send()
                    relayed_to_sibling(which, q - 1).wait_send()

    vec = pl.BlockSpec((1, D), lambda p, i, pos: (0, 0))
    first_phase_rows = lambda p, i, pos: (jnp.where(p == 0, i, nrow - 1), 0)
    anyspec = pl.BlockSpec(memory_space=pl.ANY)
    return pl.pallas_call(
        body, name="proj_gather",
        grid_spec=pltpu.PrefetchScalarGridSpec(
            num_scalar_prefetch=1, grid=(N_CHIP, nrow),
            in_specs=[pl.BlockSpec((tm, D), first_phase_rows), vec, vec, vec, anyspec, anyspec],
            out_specs=[pl.BlockSpec((tm, W_IN_SHARD), lambda p, i, pos: (i, jnp.bitwise_xor(pos[0], p))),
                       pl.BlockSpec((tm, D), first_phase_rows), anyspec, anyspec],
            scratch_shapes=[pltpu.VMEM((s, D), BF16), pltpu.VMEM((D, W_IN_SHARD), BF16),
                            pltpu.SemaphoreType.DMA((16,)), pltpu.SemaphoreType.DMA((16,)), pltpu.SemaphoreType.DMA]),
        out_shape=[jax.ShapeDtypeStruct((s, D_IN), F32), jax.ShapeDtypeStruct((s, D), BF16),
                   jax.ShapeDtypeStruct((D, D_IN), BF16), jax.ShapeDtypeStruct((D, D), BF16)],
        input_output_aliases={5: 2, 6: 3},
        compiler_params=_params("arbitrary", "arbitrary"),
    )(pos, x, shift, scale, norm_g, wi_full, wo_full)


def _proj_specs(rev_nb=None):
    if rev_nb is None:
        row = lambda i: i
    else:
        row = lambda i: rev_nb - 1 - i
    wide = lambda col: pl.BlockSpec((BLK, D_A), lambda i: (row(i), col))
    kv = lambda col: pl.BlockSpec((BLK, D_KV), lambda i: (row(i), col))
    half = lambda col: pl.BlockSpec((BLK, 512), lambda i: (row(i), col))
    return [wide(0), wide(1), wide(2), wide(3), kv(OFF_K // D_KV), kv(OFF_V // D_KV), half(OFF_ZB // 512), half(OFF_ZB // 512 + 1)]


def _mix_fwd_call(proj, cos, sin, ln_g, ln_b, w_sp, b_sp_t, sinks):
    s = proj.shape[0]
    nb = s // BLK

    def body(ua_ref, va_ref, za_ref, q_ref, k_ref, v_ref, zb0_ref, zb1_ref, cos_ref, sin_ref, lg_ref, lb_ref,
             w_ref, bt_ref, sinks_ref, y_ref, kdup_ref, vdup_ref, qm_ref, ost_ref, bias_ref):
        i = pl.program_id(0)
        first_half, lo = _lane_masks()
        cos_t = cos_ref[...]
        sin_t = sin_ref[...]

        _, _, vln = _layer_norm_fwd(va_ref[...], lg_ref[...], lb_ref[...])
        tril = _tril()
        for g in range(GROUPS):
            cols = slice(g * BLK, (g + 1) * BLK)
            wg = jnp.where(tril, w_ref[g], 0.0).astype(BF16)
            sg = jnp.dot(wg, vln[:, cols].astype(BF16), preferred_element_type=F32) + bt_ref[:, g:g + 1]
            gate, _ = _silu_parts(za_ref[:, cols])
            y_ref[:, cols] = (ua_ref[:, cols] * sg * gate).astype(BF16)

        @pl.when(i == 0)
        def _():
            kdup_ref[:, 0:BLK, :] = jnp.zeros((N_KV, BLK, LANE), BF16)
            vdup_ref[:, 0:BLK, :] = jnp.zeros((N_KV, BLK, LANE), BF16)
            _band_bias(bias_ref)

        @pl.when(i > 0)
        def _():
            kdup_ref[:, 0:BLK, :] = kdup_ref[:, BLK:2 * BLK, :]
            vdup_ref[:, 0:BLK, :] = vdup_ref[:, BLK:2 * BLK, :]

        for ks in range(2):
            cols = slice(ks * LANE, (ks + 1) * LANE)
            kr = _rope(k_ref[:, cols], cos_t, sin_t, first_half)
            for n, (kd, vd) in enumerate(zip(_dup_kv(kr, lo), _dup_kv(v_ref[:, cols], lo))):
                kdup_ref[2 * ks + n, BLK:2 * BLK, :] = kd
                vdup_ref[2 * ks + n, BLK:2 * BLK, :] = vd
        for sb in range(8):
            _stack_heads(qm_ref, sb, _rope(q_ref[:, sb * LANE:(sb + 1) * LANE], cos_t, sin_t, first_half) * SCALE, lo, BF16)

        block_kind = jnp.where(i > 0, 1, 0)

        def kv_head(kh, carry):
            probs, _ = _softmax_sink(qm_ref[kh], kdup_ref[kh], bias_ref[block_kind], _sink_column(sinks_ref, kh))
            ost_ref[kh] = jnp.dot(probs.astype(BF16), vdup_ref[kh], preferred_element_type=F32)
            return carry

        lax.fori_loop(0, N_KV, kv_head, 0, unroll=2)
        for sb in range(8):
            cols = slice(sb * LANE, (sb + 1) * LANE)
            zb = zb0_ref[:, cols] if sb < 4 else zb1_ref[:, (sb - 4) * LANE:(sb - 3) * LANE]
            gate, _ = _silu_parts(zb)
            y_ref[:, D_A + sb * LANE:D_A + (sb + 1) * LANE] = (_unstack_heads(ost_ref, sb, lo) * gate).astype(BF16)

    tab = pl.BlockSpec((BLK, LANE), lambda i: (i, 0))
    return pl.pallas_call(
        body, name="mix_fwd", grid=(nb,),
        in_specs=_proj_specs() + [
            tab, tab, pl.BlockSpec((1, D_A), lambda i: (0, 0)), pl.BlockSpec((1, D_A), lambda i: (0, 0)),
            pl.BlockSpec((GROUPS, BLK, BLK), lambda i: (0, 0, 0)), pl.BlockSpec((BLK, GROUPS), lambda i: (0, 0)),
            pl.BlockSpec(memory_space=pltpu.SMEM)],
        out_specs=pl.BlockSpec((BLK, 2 * D_A), lambda i: (i, 0)),
        out_shape=jax.ShapeDtypeStruct((s, 2 * D_A), BF16),
        scratch_shapes=[pltpu.VMEM((N_KV, 2 * BLK, LANE), BF16), pltpu.VMEM((N_KV, 2 * BLK, LANE), BF16),
                        pltpu.VMEM((N_KV, Q_PER_KV * BLK, LANE), BF16), pltpu.VMEM((N_KV, Q_PER_KV * BLK, LANE), F32),
                        pltpu.VMEM((2, Q_PER_KV * BLK, 2 * BLK), F32)],
        compiler_params=_params("arbitrary"),
    )(proj, proj, proj, proj, proj, proj, proj, proj, cos, sin, ln_g, ln_b, w_sp, b_sp_t, sinks)


def _tail_call(y, w_out_bf, x, target, gate, shift_f, scale_f, gf):
    s = x.shape[0]
    tm = min(s, 256)
    nsteps = s // tm

    def body(y_ref, w_ref, x_ref, t_ref, gate_ref, shf_ref, scf_ref, gf_ref, dx2_ref, do_ref, dy_ref, st_ref):
        i = pl.program_id(0)

        @pl.when(i == 0)
        def _():
            st_ref[...] = jnp.zeros((8, D), F32)

        o = jnp.dot(y_ref[...], w_ref[...], preferred_element_type=F32)
        gate_v = gate_ref[...]
        x2 = x_ref[...] + gate_v * o
        r2 = lax.rsqrt(jnp.mean(x2 * x2, axis=-1, keepdims=True) + EPS)
        xn2 = x2 * r2
        hn2 = xn2 * gf_ref[...]
        one_sc = 1.0 + scf_ref[...]
        err = hn2 * one_sc + shf_ref[...] - t_ref[...]
        dout = err * (1.0 / D)
        dhn2 = dout * one_sc
        dxn2 = dhn2 * gf_ref[...]
        dx2 = r2 * (dxn2 - xn2 * jnp.mean(dxn2 * xn2, axis=-1, keepdims=True))
        dx2_ref[...] = dx2
        do = (dx2 * gate_v).astype(BF16)
        do_ref[...] = do
        dy_ref[...] = lax.dot_general(do, w_ref[...], NT, preferred_element_type=F32)
        st_ref[0:1, :] += jnp.sum(dout, axis=0, keepdims=True)
        st_ref[1:2, :] += jnp.sum(dout * hn2, axis=0, keepdims=True)
        st_ref[2:3, :] += jnp.sum(dhn2 * xn2, axis=0, keepdims=True)
        st_ref[3:4, :] += jnp.sum(dx2 * o, axis=0, keepdims=True)
        st_ref[4:5, :] += jnp.sum(err * err, axis=0, keepdims=True)

        @pl.when(i == nsteps - 1)
        def _():
            st_ref[5:6, :] = jnp.full((1, D), 0.5 / D, F32) * jnp.sum(st_ref[4:5, :])

    vec = pl.BlockSpec((1, D), lambda i: (0, 0))
    rows = lambda: pl.BlockSpec((tm, D), lambda i: (i, 0))
    return pl.pallas_call(
        body, name="tail", grid=(nsteps,),
        in_specs=[rows(), pl.BlockSpec((D, D), lambda i: (0, 0)), rows(), rows(), vec, vec, vec, vec],
        out_specs=[rows(), rows(), rows(), pl.BlockSpec((8, D), lambda i: (0, 0))],
        out_shape=[jax.ShapeDtypeStruct((s, D), F32), jax.ShapeDtypeStruct((s, D), BF16), jax.ShapeDtypeStruct((s, D), F32),
                   jax.ShapeDtypeStruct((8, D), F32)],
        compiler_params=_params("arbitrary"),
    )(y, w_out_bf, x, target, gate, shift_f, scale_f, gf)


def _tn_call(a, b, name):
    s, m = a.shape
    n = b.shape[1]
    tn = 512
    ts = min(s, 1024)
    nk = s // ts

    def body(a_ref, b_ref, o_ref, acc_ref):
        k = pl.program_id(1)

        @pl.when(k == 0)
        def _():
            acc_ref[...] = jnp.zeros((m, tn), F32)

        acc_ref[...] += lax.dot_general(a_ref[...], b_ref[...], TN, preferred_element_type=F32)

        @pl.when(k == nk - 1)
        def _():
            o_ref[...] = acc_ref[...].astype(BF16)

    return pl.pallas_call(
        body, name=name, grid=(n // tn, nk),
        in_specs=[pl.BlockSpec((ts, m), lambda j, k: (k, 0)), pl.BlockSpec((ts, tn), lambda j, k: (k, j))],
        out_specs=pl.BlockSpec((m, tn), lambda j, k: (0, j)),
        out_shape=jax.ShapeDtypeStruct((m, n), BF16),
        scratch_shapes=[pltpu.VMEM((m, tn), F32)],
        compiler_params=_params("parallel", "arbitrary"),
    )(a, b)


def _tn_shards_call(pos, a, b, qs, name):
    s, m = a.shape
    ts = min(s, 1024)
    nk = s // ts

    def body(pos_ref, a_ref, b_ref, o_ref, acc_ref):
        k = pl.program_id(1)

        @pl.when(k == 0)
        def _():
            acc_ref[...] = jnp.zeros((m, W_IN_SHARD), F32)

        acc_ref[...] += lax.dot_general(a_ref[...], b_ref[...], TN, preferred_element_type=F32)

        @pl.when(k == nk - 1)
        def _():
            o_ref[...] = acc_ref[...].astype(BF16)

    def shard(j, pos):
        q = qs[0]
        for n in range(1, len(qs)):
            q = jnp.where(j == n, qs[n], q)
        return jnp.bitwise_xor(pos[0], q)

    return pl.pallas_call(
        body, name=name,
        grid_spec=pltpu.PrefetchScalarGridSpec(
            num_scalar_prefetch=1, grid=(len(qs), nk),
            in_specs=[pl.BlockSpec((ts, m), lambda j, k, pos: (k, 0)),
                      pl.BlockSpec((ts, W_IN_SHARD), lambda j, k, pos: (k, shard(j, pos)))],
            out_specs=pl.BlockSpec((m, W_IN_SHARD), lambda j, k, pos: (0, j)),
            scratch_shapes=[pltpu.VMEM((m, W_IN_SHARD), F32)]),
        out_shape=jax.ShapeDtypeStruct((m, len(qs) * W_IN_SHARD), BF16),
        compiler_params=_params("parallel", "arbitrary"),
    )(pos, a, b)


def _mix_bwd_call(proj, dy, cos, sin, ln_g, ln_b, w_sp, w_sp_t, b_sp_t, sinks):
    s = proj.shape[0]
    nb = s // BLK
    rev = lambda i: nb - 1 - i
    prev = lambda i: jnp.maximum(nb - 2 - i, 0)

    def body(ua_ref, va_ref, za_ref, q_ref, k_ref, v_ref, zb0_ref, zb1_ref, kp_ref, vp_ref, dy_ref,
             cos_ref, sin_ref, cosp_ref, sinp_ref, lg_ref, lb_ref, w_ref, wt_ref, bt_ref, sinks_ref,
             dp_ref, lnst_ref, dw_ref, dbt_ref, dsink_ref,
             kdup_ref, vdup_ref, dvln_ref, qm_ref, dom_ref, ost_ref, dqst_ref, dkdup_ref, dvdup_ref, kcar_ref, vcar_ref,
             sigb_ref, bias_ref):
        i = pl.program_id(0)
        first_half, lo = _lane_masks()
        lane8 = lax.broadcasted_iota(jnp.int32, (8, LANE), 1)
        cos_t = cos_ref[...]
        sin_t = sin_ref[...]

        @pl.when(i == 0)
        def _():
            lnst_ref[...] = jnp.zeros((8, D_A), F32)
            dw_ref[...] = jnp.zeros((GROUPS, BLK, BLK), F32)
            dbt_ref[...] = jnp.zeros((BLK, LANE), F32)
            dsink_ref[...] = jnp.zeros((8, LANE), F32)
            kcar_ref[...] = jnp.zeros((BLK, D_KV), F32)
            vcar_ref[...] = jnp.zeros((BLK, D_KV), F32)
            _band_bias(bias_ref)

        vhat, rstd, vln = _layer_norm_fwd(va_ref[...], lg_ref[...], lb_ref[...])
        tril = _tril()
        triu = jnp.logical_not(tril) | (lax.broadcasted_iota(jnp.int32, (BLK, BLK), 0) == lax.broadcasted_iota(jnp.int32, (BLK, BLK), 1))
        lane_b = lax.broadcasted_iota(jnp.int32, (BLK, LANE), 1)
        db_acc = jnp.zeros((BLK, LANE), F32)
        for g in range(GROUPS):
            cols = slice(g * BLK, (g + 1) * BLK)
            vln_g = vln[:, cols].astype(BF16)
            wg = jnp.where(tril, w_ref[g], 0.0).astype(BF16)
            sg = jnp.dot(wg, vln_g, preferred_element_type=F32) + bt_ref[:, g:g + 1]
            za = za_ref[:, cols]
            gate, sig = _silu_parts(za)
            ua = ua_ref[:, cols]
            dya_g = dy_ref[:, cols]
            dya = dya_g * gate
            dp_ref[:, cols] = (dya * sg).astype(BF16)
            dp_ref[:, 2 * D_A + g * BLK:2 * D_A + (g + 1) * BLK] = (
                dya_g * (ua * sg) * (sig * (1.0 + za * (1.0 - sig)))).astype(BF16)
            ds = dya * ua
            ds_b = ds.astype(BF16)
            wtg = jnp.where(triu, wt_ref[g], 0.0).astype(BF16)
            dvln_ref[:, cols] = jnp.dot(wtg, ds_b, preferred_element_type=F32)
            dw_ref[g] += jnp.where(tril, lax.dot_general(ds_b, vln_g, NT, preferred_element_type=F32), 0.0)
            db_acc = db_acc + jnp.where(lane_b == g, jnp.sum(ds, axis=-1, keepdims=True), 0.0)
        dbt_ref[...] += db_acc
        dvln = dvln_ref[...]
        lnst_ref[0:1, :] += jnp.sum(dvln * vhat, axis=0, keepdims=True)
        lnst_ref[1:2, :] += jnp.sum(dvln, axis=0, keepdims=True)
        dvhat = dvln * lg_ref[...]
        m1 = jnp.mean(dvhat, axis=-1, keepdims=True)
        m2 = jnp.mean(dvhat * vhat, axis=-1, keepdims=True)
        dp_ref[:, D_A:2 * D_A] = (rstd * (dvhat - m1 - vhat * m2)).astype(BF16)

        cosp = cosp_ref[...]
        sinp = sinp_ref[...]
        for ks in range(2):
            cols = slice(ks * LANE, (ks + 1) * LANE)
            kr = _rope(k_ref[:, cols], cos_t, sin_t, first_half)
            kpr = _rope(kp_ref[:, cols], cosp, sinp, first_half)
            for n, (kc, vc, kp, vp) in enumerate(zip(_dup_kv(kr, lo), _dup_kv(v_ref[:, cols], lo),
                                                     _dup_kv(kpr, lo), _dup_kv(vp_ref[:, cols], lo))):
                kdup_ref[2 * ks + n, BLK:2 * BLK, :] = kc
                vdup_ref[2 * ks + n, BLK:2 * BLK, :] = vc
                kdup_ref[2 * ks + n, 0:BLK, :] = kp
                vdup_ref[2 * ks + n, 0:BLK, :] = vp
        for sb in range(8):
            cols = slice(sb * LANE, (sb + 1) * LANE)
            _stack_heads(qm_ref, sb, _rope(q_ref[:, cols], cos_t, sin_t, first_half) * SCALE, lo, BF16)
            zb = zb0_ref[:, cols] if sb < 4 else zb1_ref[:, (sb - 4) * LANE:(sb - 3) * LANE]
            gate, sig = _silu_parts(zb)
            sigb_ref[:, cols] = sig
            _stack_heads(dom_ref, sb, dy_ref[:, D_A + sb * LANE:D_A + (sb + 1) * LANE] * gate, lo, F32)

        block_kind = jnp.where(i < nb - 1, 1, 0)

        def kv_head(kh, dsink_acc):
            qm = qm_ref[kh]
            kd = kdup_ref[kh]
            vd = vdup_ref[kh]
            probs, psink = _softmax_sink(qm, kd, bias_ref[block_kind], _sink_column(sinks_ref, kh))
            probs_b = probs.astype(BF16)
            o = jnp.dot(probs_b, vd, preferred_element_type=F32)
            ost_ref[kh] = o
            dom = dom_ref[kh]
            dom_b = dom.astype(BF16)
            delta = jnp.sum(dom * o, axis=-1, keepdims=True)
            dpr = lax.dot_general(dom_b, vd, NT, preferred_element_type=F32)
            dss = (probs * (dpr - delta)).astype(BF16)
            sd = psink * delta
            for n in range(Q_PER_KV):
                dsink_acc = dsink_acc + jnp.where(lane8 == Q_PER_KV * kh + n, -jnp.sum(sd[n * BLK:(n + 1) * BLK]), 0.0)
            dqst_ref[kh] = jnp.dot(dss, kd, preferred_element_type=F32)
            dkdup_ref[kh] = lax.dot_general(dss, qm, TN, preferred_element_type=F32)
            dvdup_ref[kh] = lax.dot_general(probs_b, dom_b, TN, preferred_element_type=F32)
            return dsink_acc

        dsink_acc = lax.fori_loop(0, N_KV // 2, lambda j, acc: kv_head(2 * j + 1, kv_head(2 * j, acc)), jnp.zeros((8, LANE), F32))
        row0 = lax.broadcasted_iota(jnp.int32, (8, LANE), 0) == 0
        dsink_ref[...] += jnp.where(row0, dsink_acc, 0.0)

        for sb in range(8):
            cols = slice(sb * LANE, (sb + 1) * LANE)
            zb = zb0_ref[:, cols] if sb < 4 else zb1_ref[:, (sb - 4) * LANE:(sb - 3) * LANE]
            sig = sigb_ref[:, cols]
            dyb = dy_ref[:, D_A + sb * LANE:D_A + (sb + 1) * LANE]
            dp_ref[:, OFF_ZB + sb * LANE:OFF_ZB + (sb + 1) * LANE] = (
                dyb * _unstack_heads(ost_ref, sb, lo) * (sig * (1.0 + zb * (1.0 - sig)))).astype(BF16)
            dq_r = _unstack_heads(dqst_ref, sb, lo) * SCALE
            dp_ref[:, OFF_Q + sb * LANE:OFF_Q + (sb + 1) * LANE] = _unrope(dq_r, cos_t, sin_t, first_half).astype(BF16)

        lo2 = lax.broadcasted_iota(jnp.int32, (2 * BLK, LANE), 1) < HEAD
        for ks in range(2):
            cols = slice(ks * LANE, (ks + 1) * LANE)
            ka = dkdup_ref[2 * ks]
            kb = dkdup_ref[2 * ks + 1]
            dk_band = jnp.where(lo2, ka + pltpu.roll(ka, HEAD, 1), kb + pltpu.roll(kb, HEAD, 1))
            va_ = dvdup_ref[2 * ks]
            vb_ = dvdup_ref[2 * ks + 1]
            dv_band = jnp.where(lo2, va_ + pltpu.roll(va_, HEAD, 1), vb_ + pltpu.roll(vb_, HEAD, 1))
            dkr = dk_band[BLK:2 * BLK, :] + kcar_ref[:, cols]
            dp_ref[:, OFF_K + ks * LANE:OFF_K + (ks + 1) * LANE] = _unrope(dkr, cos_t, sin_t, first_half).astype(BF16)
            dp_ref[:, OFF_V + ks * LANE:OFF_V + (ks + 1) * LANE] = (
                dv_band[BLK:2 * BLK, :] + vcar_ref[:, cols]).astype(BF16)
            kcar_ref[:, cols] = dk_band[0:BLK, :]
            vcar_ref[:, cols] = dv_band[0:BLK, :]

    tab = pl.BlockSpec((BLK, LANE), lambda i: (rev(i), 0))
    tabp = pl.BlockSpec((BLK, LANE), lambda i: (prev(i), 0))
    kvp = lambda col: pl.BlockSpec((BLK, D_KV), lambda i: (prev(i), col))
    vec = pl.BlockSpec((1, D_A), lambda i: (0, 0))
    w3 = pl.BlockSpec((GROUPS, BLK, BLK), lambda i: (0, 0, 0))
    return pl.pallas_call(
        body, name="mix_bwd", grid=(nb,),
        in_specs=_proj_specs(nb) + [
            kvp(OFF_K // D_KV), kvp(OFF_V // D_KV), pl.BlockSpec((BLK, 2 * D_A), lambda i: (rev(i), 0)),
            tab, tab, tabp, tabp, vec, vec, w3, w3, pl.BlockSpec((BLK, GROUPS), lambda i: (0, 0)),
            pl.BlockSpec(memory_space=pltpu.SMEM)],
        out_specs=[pl.BlockSpec((BLK, D_IN), lambda i: (rev(i), 0)), pl.BlockSpec((8, D_A), lambda i: (0, 0)), w3,
                   pl.BlockSpec((BLK, LANE), lambda i: (0, 0)), pl.BlockSpec((8, LANE), lambda i: (0, 0))],
        out_shape=[jax.ShapeDtypeStruct((s, D_IN), BF16), jax.ShapeDtypeStruct((8, D_A), F32),
                   jax.ShapeDtypeStruct((GROUPS, BLK, BLK), F32), jax.ShapeDtypeStruct((BLK, LANE), F32),
                   jax.ShapeDtypeStruct((8, LANE), F32)],
        scratch_shapes=[pltpu.VMEM((N_KV, 2 * BLK, LANE), BF16), pltpu.VMEM((N_KV, 2 * BLK, LANE), BF16),
                        pltpu.VMEM((BLK, D_A), F32), pltpu.VMEM((N_KV, Q_PER_KV * BLK, LANE), BF16),
                        pltpu.VMEM((N_KV, Q_PER_KV * BLK, LANE), F32), pltpu.VMEM((N_KV, Q_PER_KV * BLK, LANE), F32),
                        pltpu.VMEM((N_KV, Q_PER_KV * BLK, LANE), F32), pltpu.VMEM((N_KV, 2 * BLK, LANE), F32),
                        pltpu.VMEM((N_KV, 2 * BLK, LANE), F32), pltpu.VMEM((BLK, D_KV), F32), pltpu.VMEM((BLK, D_KV), F32),
                        pltpu.VMEM((BLK, D_B), F32), pltpu.VMEM((2, Q_PER_KV * BLK, 2 * BLK), F32)],
        compiler_params=_params("arbitrary"),
    )(proj, proj, proj, proj, proj, proj, proj, proj, proj, proj, dy, cos, sin, cos, sin, ln_g, ln_b, w_sp, w_sp_t,
      b_sp_t, sinks)


def _dh_call(dproj, w_bf, x, dx2, scale, norm_g):
    s = x.shape[0]
    tm = min(s, 512)
    tk = W_IN_SHARD
    nk = D_IN // tk

    def body(dp_ref, w_ref, x_ref, dx2_ref, sc_ref, g_ref, gx_ref, st_ref, acc_ref):
        i = pl.program_id(0)
        k = pl.program_id(1)

        @pl.when((i == 0) & (k == 0))
        def _():
            st_ref[...] = jnp.zeros((8, D), F32)

        @pl.when(k == 0)
        def _():
            acc_ref[...] = jnp.zeros((tm, D), F32)

        acc_ref[...] += lax.dot_general(dp_ref[...], w_ref[...], NT, preferred_element_type=F32)

        @pl.when(k == nk - 1)
        def _():
            g = g_ref[...]
            one_sc = 1.0 + sc_ref[...]

            def chunk(n, carry):
                rows = pl.ds(pl.multiple_of(n * BLK, BLK), BLK)
                dh = acc_ref[rows, :]
                xv = x_ref[rows, :]
                r = lax.rsqrt(jnp.mean(xv * xv, axis=-1, keepdims=True) + EPS)
                xn = xv * r
                dhn = dh * one_sc
                dxn = dhn * g
                gx_ref[rows, :] = dx2_ref[rows, :] + r * (dxn - xn * jnp.mean(dxn * xn, axis=-1, keepdims=True))
                st_ref[0:1, :] += jnp.sum(dh, axis=0, keepdims=True)
                st_ref[1:2, :] += jnp.sum(dh * (xn * g), axis=0, keepdims=True)
                st_ref[2:3, :] += jnp.sum(dhn * xn, axis=0, keepdims=True)
                return carry

            lax.fori_loop(0, tm // BLK, chunk, 0)

    vec = pl.BlockSpec((1, D), lambda i, k: (0, 0))
    rows = lambda: pl.BlockSpec((tm, D), lambda i, k: (i, 0))
    return pl.pallas_call(
        body, name="dh", grid=(s // tm, nk),
        in_specs=[pl.BlockSpec((tm, tk), lambda i, k: (i, k)), pl.BlockSpec((D, tk), lambda i, k: (0, k)), rows(), rows(), vec, vec],
        out_specs=[rows(), pl.BlockSpec((8, D), lambda i, k: (0, 0))],
        out_shape=[jax.ShapeDtypeStruct((s, D), F32), jax.ShapeDtypeStruct((8, D), F32)],
        scratch_shapes=[pltpu.VMEM((tm, D), F32)],
        compiler_params=_params("arbitrary", "arbitrary"),
    )(dproj, w_bf, x, dx2, scale, norm_g)


def _adam_math(w, g, m, v):
    m_new = ADAM_B1 * m + (1.0 - ADAM_B1) * g
    v_new = ADAM_B2 * v + (1.0 - ADAM_B2) * (g * g)
    m_hat = m_new / ADAM_C1
    v_hat = v_new / ADAM_C2
    delta = -ADAM_LR * (m_hat / (jnp.sqrt(v_hat) + ADAM_EPS) + ADAM_WD * w)
    return delta, m_new, v_new


def _adam_small_call(tensors):
    n = len(tensors)

    def body(*refs):
        ins, outs = refs[:4 * n], refs[4 * n:]
        for t in range(n):
            w_ref, g_ref, m_ref, v_ref = ins[4 * t:4 * t + 4]
            d, mo, vo = _adam_math(w_ref[...], g_ref[...], m_ref[...], v_ref[...])
            outs[3 * t][...], outs[3 * t + 1][...], outs[3 * t + 2][...] = d, mo, vo

    vm = pl.BlockSpec(memory_space=pltpu.VMEM)
    flat = [a for t in tensors for a in t]
    out = pl.pallas_call(
        body, name="adam_small", in_specs=[vm] * (4 * n), out_specs=[vm] * (3 * n),
        out_shape=[jax.ShapeDtypeStruct(t[0].shape, F32) for t in tensors for _ in range(3)],
        compiler_params=pltpu.CompilerParams(vmem_limit_bytes=VMEM_LIMIT),
    )(*flat)
    return [tuple(out[3 * t:3 * t + 3]) for t in range(n)]


def _adam_halves_call(pos, w, mine, theirs, m, v, name):
    r, n = w.shape
    half = r // 2
    tr = ADAM_ROWS
    nh = half // tr

    def body(pos_ref, w_ref, mine_ref, theirs_ref, m_ref, v_ref, g_ref, d_ref, mo_ref, vo_ref):
        is_mine = (pl.program_id(0) // nh) == pos_ref[1]
        g = jnp.where(is_mine, mine_ref[...], theirs_ref[...])
        g_ref[...] = g
        d_ref[...], mo_ref[...], vo_ref[...] = _adam_math(w_ref[...], g, m_ref[...], v_ref[...])

    spec = lambda: pl.BlockSpec((tr, n), lambda i, pos: (i, 0))
    hspec = lambda: pl.BlockSpec((tr, n), lambda i, pos: (i % nh, 0))
    return pl.pallas_call(
        body, name=name,
        grid_spec=pltpu.PrefetchScalarGridSpec(
            num_scalar_prefetch=1, grid=(r // tr,), in_specs=[spec(), hspec(), hspec(), spec(), spec()],
            out_specs=[spec() for _ in range(4)]),
        out_shape=[jax.ShapeDtypeStruct((r, n), F32)] * 4, compiler_params=_params("parallel"),
    )(pos, w, mine, theirs, m, v)


def _adam_outer_call(w, ct, dm, m, v, name):
    r, n = w.shape
    tr = ADAM_ROWS

    def body(w_ref, ct_ref, dm_ref, m_ref, v_ref, g_ref, d_ref, mo_ref, vo_ref):
        g = ct_ref[:, 0:1] * dm_ref[0:1, :]
        for b in range(1, N_DEV):
            g = g + ct_ref[:, b:b + 1] * dm_ref[b:b + 1, :]
        g_ref[...] = g
        d_ref[...], mo_ref[...], vo_ref[...] = _adam_math(w_ref[...], g, m_ref[...], v_ref[...])

    spec = lambda: pl.BlockSpec((tr, n), lambda i: (i, 0))
    return pl.pallas_call(
        body, name=name, grid=(r // tr,),
        in_specs=[spec(), pl.BlockSpec((tr, N_DEV), lambda i: (i, 0)), pl.BlockSpec((N_DEV, n), lambda i: (0, 0)), spec(), spec()],
        out_specs=[spec() for _ in range(4)],
        out_shape=[jax.ShapeDtypeStruct((r, n), F32)] * 4, compiler_params=_params("parallel"),
    )(w, ct, dm, m, v)


def _sum_pieces_call(pos, part, part_block, recvs, name):
    r, n = recvs[0].shape[1:]
    tr = min(r, 256)
    nrb = r // tr

    def body(pos_ref, p_ref, *refs):
        acc = p_ref[...].astype(F32)
        for r_ref in refs[:-1]:
            for d in range(r_ref.shape[0]):
                acc = acc + r_ref[d].astype(F32)
        refs[-1][...] = acc

    return pl.pallas_call(
        body, name=name,
        grid_spec=pltpu.PrefetchScalarGridSpec(
            num_scalar_prefetch=1, grid=(nrb,),
            in_specs=[pl.BlockSpec((tr, n), lambda i, pos: part_block(i, pos, nrb))] + [
                pl.BlockSpec((rv.shape[0], tr, n), lambda i, pos: (0, i, 0)) for rv in recvs],
            out_specs=pl.BlockSpec((tr, n), lambda i, pos: (i, 0))),
        out_shape=jax.ShapeDtypeStruct((r, n), F32), compiler_params=_params("parallel"),
    )(pos, part, *recvs)


def _coords():
    return lax.axis_index("x"), lax.axis_index("y"), lax.axis_index("c")


def _allgather_sum_call(blk, name, with_sum):
    m_per, n = blk.shape

    def body(x_ref, out_ref, *rest):
        if with_sum:
            sum_ref, send_sems, recv_sems, local_sem = rest
        else:
            send_sems, recv_sems, local_sem = rest
        x, y, c = _coords()
        me, sibling = (x, y, c), (x, y, 1 - c)
        chips = [(1 - x, y), (x, 1 - y), (1 - x, 1 - y)]

        def rows(px, py, pc):
            return out_ref.at[pl.ds((4 * px + 2 * py + pc) * m_per, m_per), :]

        def copy(k, block, to, src=None):
            return pltpu.make_async_remote_copy(
                src_ref=rows(*block) if src is None else src, dst_ref=rows(*block),
                send_sem=send_sems.at[k], recv_sem=recv_sems.at[k], device_id=to, device_id_type=MESH)

        mine = pltpu.make_async_copy(x_ref, rows(*me), local_sem)
        mine.start()
        first = [copy(0, me, sibling, src=x_ref)]
        first += [copy(1 + j, me, (*chip, c), src=x_ref) for j, chip in enumerate(chips)]
        for cp in first:
            cp.start()
        passed = [copy(4 + j, (*chip, c), sibling) for j, chip in enumerate(chips)]
        for j, chip in enumerate(chips):
            copy(1 + j, (*chip, c), me).wait_recv()
            passed[j].start()
        copy(0, sibling, me).wait_recv()
        for j, chip in enumerate(chips):
            copy(4 + j, (*chip, 1 - c), me).wait_recv()
        for cp in first + passed:
            cp.wait_send()
        mine.wait()
        if with_sum:
            acc = out_ref[0:m_per, :]
            for d in range(1, N_DEV):
                acc = acc + out_ref[d * m_per:(d + 1) * m_per, :]
            sum_ref[...] = acc

    vm = pl.BlockSpec(memory_space=pltpu.VMEM)
    out_shape = [jax.ShapeDtypeStruct((N_DEV * m_per, n), F32)]
    if with_sum:
        out_shape.append(jax.ShapeDtypeStruct((m_per, n), F32))
    return pl.pallas_call(
        body, name=name, out_shape=out_shape, in_specs=[vm], out_specs=[vm] * len(out_shape),
        scratch_shapes=[pltpu.SemaphoreType.DMA((7,)), pltpu.SemaphoreType.DMA((7,)), pltpu.SemaphoreType.DMA],
        compiler_params=pltpu.CompilerParams(vmem_limit_bytes=VMEM_LIMIT),
    )(blk)


def _weights_gather_call(wi_full, wo_full):
    hi = D // 2
    ho = W_OUT_SHARD // 2

    def body(wi_in, wo_in, fi_ref, fo_ref, send_sems, recv_sems):
        del wi_in, wo_in
        x, y, c = _coords()
        sibling = (x, y, 1 - c)
        chips = [(1 - x, y), (x, 1 - y), (1 - x, 1 - y)]

        def half(which, px, py, pc):
            j = 2 * px + py
            if which == 0:
                return fi_ref.at[pl.ds(pc * hi, hi), pl.ds(j * W_IN_SHARD, W_IN_SHARD)]
            return fo_ref.at[pl.ds(j * W_OUT_SHARD + pc * ho, ho), :]

        def copy(k, which, block, to):
            return pltpu.make_async_remote_copy(
                src_ref=half(which, *block), dst_ref=half(which, *block), send_sem=send_sems.at[k],
                recv_sem=recv_sems.at[k], device_id=to, device_id_type=MESH)

        first = [copy(6 * w + j, w, (x, y, c), (*chip, c)) for w in range(2) for j, chip in enumerate(chips)]
        for cp in first:
            cp.start()
        passed = []
        for w in range(2):
            for j, chip in enumerate(chips):
                copy(6 * w + j, w, (*chip, c), (x, y, c)).wait_recv()
                cp = copy(6 * w + 3 + j, w, (*chip, c), sibling)
                cp.start()
                passed.append(cp)
        for w in range(2):
            for j, chip in enumerate(chips):
                copy(6 * w + 3 + j, w, (*chip, 1 - c), (x, y, c)).wait_recv()
        for cp in first + passed:
            cp.wait_send()

    anyspec = pl.BlockSpec(memory_space=pl.ANY)
    return pl.pallas_call(
        body, name="weights_gather",
        out_shape=[jax.ShapeDtypeStruct((D, D_IN), BF16), jax.ShapeDtypeStruct((D, D), BF16)],
        in_specs=[anyspec, anyspec], out_specs=[anyspec, anyspec], input_output_aliases={0: 0, 1: 1},
        scratch_shapes=[pltpu.SemaphoreType.DMA((12,)), pltpu.SemaphoreType.DMA((12,))],
    )(wi_full, wo_full)


HBM_SPEC = pl.BlockSpec(memory_space=pltpu.HBM)
SEM_SPEC = pl.BlockSpec(memory_space=pltpu.SEMAPHORE)
SIDE_EFFECT = pltpu.SideEffectType.DATAFLOW_SIDE_EFFECTING


def _peer(x, y, c, q, cb):
    return (1 - x if q & 2 else x, 1 - y if q & 1 else y, 1 - c if cb else c)


def _w_in_piece(slots):
    def piece(part_ref, k, to):
        return part_ref.at[pl.ds(to[2] * (D // 2), D // 2), pl.ds(slots[k] * W_IN_SHARD, W_IN_SHARD)]
    return piece


def _w_out_piece(part_ref, k, to):
    ho = W_OUT_SHARD // 2
    return part_ref.at[pl.ds((2 * to[0] + to[1]) * W_OUT_SHARD + to[2] * ho, ho), :]


def _group_piece(part_ref, k, to):
    return part_ref.at[4 * to[0] + 2 * to[1] + to[2]]


def _whole_piece(part_ref, k, to):
    return part_ref


def _exchange_start_call(part, rels, piece, slot_shape, name):
    n = len(rels)
    land = lax.empty((n,) + slot_shape, part.dtype)

    def body(part_ref, land_ref, send_sems, recv_sems, part_thru, land_thru, token):
        x, y, c = _coords()
        for k, (q, cb) in enumerate(rels):
            to = _peer(x, y, c, q, cb)
            pltpu.make_async_remote_copy(src_ref=piece(part_ref, k, to), dst_ref=land_ref.at[k], send_sem=send_sems.at[k],
                                         recv_sem=recv_sems.at[k], device_id=to, device_id_type=MESH).start()
        token[...] = jnp.zeros_like(token)

    return pl.pallas_call(
        body, name=name,
        out_shape=(pltpu.SemaphoreType.DMA((n,)), pltpu.SemaphoreType.DMA((n,)), pltpu.HBM(part.shape, part.dtype),
                   pltpu.HBM(land.shape, land.dtype), jax.ShapeDtypeStruct((8, LANE), F32)),
        in_specs=(HBM_SPEC, HBM_SPEC), out_specs=(SEM_SPEC, SEM_SPEC, HBM_SPEC, HBM_SPEC, pl.BlockSpec(memory_space=pltpu.VMEM)),
        input_output_aliases={0: 2, 1: 3},
        compiler_params=pltpu.CompilerParams(has_side_effects=SIDE_EFFECT),
    )(pltpu.with_memory_space_constraint(part, pltpu.HBM), pltpu.with_memory_space_constraint(land, pltpu.HBM))


def _exchange_wait_call(started, rels, piece, after, name):
    send_sems, recv_sems, part_thru, land_thru, _ = started

    def body(part_ref, land_ref, send_sems, recv_sems, after_ref, part_out, land_out):
        x, y, c = _coords()
        for k, (q, cb) in enumerate(rels):
            to = _peer(x, y, c, q, cb)
            cp = pltpu.make_async_remote_copy(src_ref=piece(part_ref, k, to), dst_ref=land_ref.at[k], send_sem=send_sems.at[k],
                                              recv_sem=recv_sems.at[k], device_id=to, device_id_type=MESH)
            cp.wait_send()
            cp.wait_recv()

    return pl.pallas_call(
        body, name=name,
        out_shape=(pltpu.HBM(part_thru.shape, part_thru.dtype), pltpu.HBM(land_thru.shape, land_thru.dtype)),
        in_specs=(HBM_SPEC, HBM_SPEC, SEM_SPEC, SEM_SPEC, pl.BlockSpec(memory_space=pl.ANY)), out_specs=(HBM_SPEC, HBM_SPEC),
        input_output_aliases={0: 0, 1: 1},
        compiler_params=pltpu.CompilerParams(has_side_effects=SIDE_EFFECT),
    )(part_thru, land_thru, send_sems, recv_sems, after)


def _pair_exchange_call(gi, go):
    hi = D // 2
    ho = W_OUT_SHARD // 2

    def body(gi_in, go_in, fi_ref, fo_ref, send_sems, recv_sems):
        del gi_in, go_in
        x, y, c = _coords()
        sibling = (x, y, 1 - c)
        mine = (fi_ref.at[pl.ds(c * hi, hi), :], fo_ref.at[pl.ds(c * ho, ho), :])
        theirs = (fi_ref.at[pl.ds((1 - c) * hi, hi), :], fo_ref.at[pl.ds((1 - c) * ho, ho), :])
        sends = [pltpu.make_async_remote_copy(src_ref=ref, dst_ref=ref, send_sem=send_sems.at[k], recv_sem=recv_sems.at[k],
                                              device_id=sibling, device_id_type=MESH) for k, ref in enumerate(mine)]
        for cp in sends:
            cp.start()
        for k, ref in enumerate(theirs):
            pltpu.make_async_remote_copy(src_ref=ref, dst_ref=ref, send_sem=send_sems.at[k], recv_sem=recv_sems.at[k],
                                         device_id=sibling, device_id_type=MESH).wait_recv()
        for cp in sends:
            cp.wait_send()

    anyspec = pl.BlockSpec(memory_space=pl.ANY)
    return pl.pallas_call(
        body, name="pair_exchange",
        out_shape=[jax.ShapeDtypeStruct((D, W_IN_SHARD), F32), jax.ShapeDtypeStruct((W_OUT_SHARD, D), F32)],
        in_specs=[anyspec, anyspec], out_specs=[anyspec, anyspec], input_output_aliases={0: 0, 1: 1},
        scratch_shapes=[pltpu.SemaphoreType.DMA((2,)), pltpu.SemaphoreType.DMA((2,))],
    )(gi, go)


def _rope_tables(s):
    inv_freq = 10000.0 ** (-jnp.arange(0, HEAD, 2, dtype=F32) / HEAD)
    ang = jnp.arange(s, dtype=F32)[:, None] * inv_freq[None, :]
    cos = jnp.tile(jnp.cos(ang), (1, LANE // (HEAD // 2)))
    sin = jnp.tile(jnp.sin(ang), (1, LANE // (HEAD // 2)))
    first_half = (jnp.arange(LANE) % HEAD) < (HEAD // 2)
    return cos, jnp.where(first_half[None, :], -sin, sin)


def _pad_cols(a, n):
    return jnp.pad(a, ((0, 0), (0, n - a.shape[1])))


def kernel(x, c, w_ada, b_ada, norm_g, w_in, ln_v_g, ln_v_b, w_spatial, b_spatial, sinks, w_out, w_ada_final, b_ada_final, final_norm_g, loss_target, m_w_ada, m_b_ada, m_norm_g, m_w_in, m_ln_v_g, m_ln_v_b, m_w_spatial, m_b_spatial, m_sinks, m_w_out, m_w_ada_final, m_b_ada_final, m_final_norm_g, v_w_ada, v_b_ada, v_norm_g, v_w_in, v_ln_v_g, v_ln_v_b, v_w_spatial, v_b_spatial, v_sinks, v_w_out, v_w_ada_final, v_b_ada_final, v_final_norm_g):
    s = x.shape[1]
    ax, ay, ac = _coords()
    chip = 2 * ax + ay
    me = 4 * ax + 2 * ay + ac
    n_ada = w_ada.shape[2]
    n_adaf = w_ada_final.shape[1]

    x2d = x.reshape(s, D)
    tgt = loss_target.reshape(s, D)
    w_ada2, w_in2, w_out2 = w_ada[0], w_in[0], w_out[0]
    b_ada_f2 = b_ada_final.reshape(1, 2 * D)
    gf = final_norm_g.reshape(1, D)

    c_all = _allgather_sum_call(jnp.pad(c, ((0, 7), (0, 0))), "gather_c", False)[0][::8]
    mod_p, c_act = _rowmat_call(c_all, w_ada2, lax.dynamic_slice(b_ada, (0, chip * n_ada), (1, n_ada)), "mod")
    modf_p, _ = _rowmat_call(c_all, w_ada_final, lax.dynamic_slice(b_ada_f2, (0, chip * n_adaf), (1, n_adaf)), "mod_final")
    mods = _allgather_sum_call(jnp.concatenate([mod_p, modf_p], axis=1), "gather_mod", False)[0]
    my_rows = [lax.dynamic_slice(mods, (16 * j + me, 0), (1, n_ada + n_adaf)) for j in range(N_CHIP)]
    mod = jnp.concatenate([r[:, :n_ada] for r in my_rows], axis=1)
    mod_f = jnp.concatenate([r[:, n_ada:] for r in my_rows], axis=1)
    shift, scale, gate = mod[:, :D], mod[:, D:2 * D], mod[:, 2 * D:]
    shift_f, scale_f = mod_f[:, :D], mod_f[:, D:]

    pos = jnp.stack([chip, ac]).astype(jnp.int32)
    w_in_own = _cast_into_call(pos, w_in2, (D, D_IN), "cast_w_in")
    w_out_own = _cast_into_call(pos, w_out2, (D, D), "cast_w_out")

    cos, sin = _rope_tables(s)
    b_sp_t = b_spatial[0].T
    sinks1 = sinks.reshape(N_Q)
    proj, h, w_in_bf, w_out_bf = _proj_gather_call(pos, x2d, shift, scale, norm_g, w_in_own, w_out_own)
    y = _mix_fwd_call(proj, cos, sin, ln_v_g, ln_v_b, w_spatial[0], b_sp_t, sinks1)
    dx2, do, dy, st_tail = _tail_call(y, w_out_bf, x2d, tgt, gate, shift_f, scale_f, gf)

    rel_o = [(0, 1), (1, 0), (1, 1), (2, 0), (2, 1), (3, 0), (3, 1)]
    rel_a = [(1, 0), (1, 1), (2, 0), (2, 1)]
    rel_b = [(3, 0), (3, 1), (0, 1)]
    piece_a, piece_b = _w_in_piece([0, 0, 1, 1]), _w_in_piece([0, 0, 1])
    half_in, half_out = (D // 2, W_IN_SHARD), (W_OUT_SHARD // 2, D)

    g_w_out_p = _tn_call(y, do, "grad_w_out")
    st_o = _exchange_start_call(g_w_out_p, rel_o, _w_out_piece, half_out, "send_w_out")
    dproj, st_ln, d_wsp, d_bsp_t, d_sink = _mix_bwd_call(
        proj, dy, cos, sin, ln_v_g + st_o[4][0:1, 0:1], ln_v_b, w_spatial[0], jnp.swapaxes(w_spatial[0], 1, 2), b_sp_t, sinks1)
    g_w_in_a = _tn_shards_call(pos, h, dproj, (1, 2), "grad_w_in_a")
    st_a = _exchange_start_call(g_w_in_a, rel_a, piece_a, half_in, "send_w_in_a")
    g_w_in_b = _tn_shards_call(pos, h, dproj, (3, 0), "grad_w_in_b")
    st_b = _exchange_start_call(g_w_in_b, rel_b, piece_b, half_in, "send_w_in_b")
    rel_all = rel_o
    st_s = _exchange_start_call(d_wsp, rel_all, _group_piece, (BLK, BLK), "send_w_spatial")
    sent = st_a[4][0:1, 0:1] + st_b[4][0:1, 0:1] + st_s[4][0:1, 0:1]
    grad_x, st_dh = _dh_call(dproj, w_in_bf, x2d, dx2, scale + sent, norm_g)

    g_w_out_p, recv_o = _exchange_wait_call(st_o, rel_o, _w_out_piece, st_dh, "wait_w_out")
    _, recv_a = _exchange_wait_call(st_a, rel_a, piece_a, st_dh, "wait_w_in_a")
    g_w_in_b, recv_b = _exchange_wait_call(st_b, rel_b, piece_b, st_dh, "wait_w_in_b")
    d_wsp, recv_s = _exchange_wait_call(st_s, rel_all, _group_piece, st_dh, "wait_w_spatial")
    mine_in = _sum_pieces_call(pos, g_w_in_b, lambda i, p, nrb: (p[1] * nrb + i, 1), [recv_a, recv_b], "sum_w_in")
    mine_out = _sum_pieces_call(pos, g_w_out_p, lambda i, p, nrb: ((2 * p[0] + p[1]) * nrb + i, 0), [recv_o], "sum_w_out")
    wsp_group = _sum_pieces_call(pos, d_wsp.reshape(GROUPS * BLK, BLK), lambda i, p, nrb: (2 * p[0] + p[1], 0), [recv_s],
                                 "sum_w_spatial")
    to_sibling = [(0, 1)]
    st_pi = _exchange_start_call(mine_in, to_sibling, _whole_piece, half_in, "swap_w_in")
    st_po = _exchange_start_call(mine_out, to_sibling, _whole_piece, half_out, "swap_w_out")

    misc = jnp.concatenate([st_ln, d_bsp_t[:, :GROUPS].T, d_sink, jnp.zeros((8, D - D_A - 2 * LANE), F32)], axis=1)
    pack = jnp.concatenate([wsp_group.reshape(8, D) + (st_pi[4][0:1, 0:1] + st_po[4][0:1, 0:1]), st_tail, st_dh, misc], axis=0)
    rows = pack.shape[0]
    packs, tot = _allgather_sum_call(pack, "gather_small", True)
    packs = packs.reshape(N_DEV, rows, D)
    dmod_all = jnp.concatenate([packs[:, 16, :], packs[:, 17, :], packs[:, 11, :]], axis=1)
    dmodf_all = jnp.concatenate([packs[:, 8, :], packs[:, 9, :]], axis=1)
    loss = tot[13, 0]
    mine_in, theirs_in = _exchange_wait_call(st_pi, to_sibling, _whole_piece, tot, "swapped_w_in")
    mine_out, theirs_out = _exchange_wait_call(st_po, to_sibling, _whole_piece, tot, "swapped_w_out")
    small = {
        "b_ada": jnp.concatenate([tot[16:17], tot[17:18], tot[11:12]], axis=1),
        "norm_g": tot[18:19],
        "ln_v_g": tot[24:25, :D_A],
        "ln_v_b": tot[25:26, :D_A],
        "w_spatial": packs[:, 0:8, :].reshape(GROUPS * BLK, BLK),
        "b_spatial": tot[24:32, D_A:D_A + BLK],
        "sinks": tot[24:25, D_A + LANE:D_A + LANE + N_Q],
        "b_ada_final": jnp.concatenate([tot[8:9], tot[9:10]], axis=1),
        "final_norm_g": tot[10:11],
    }

    weights = dict(w_ada=w_ada, b_ada=b_ada, norm_g=norm_g, w_in=w_in, ln_v_g=ln_v_g, ln_v_b=ln_v_b, w_spatial=w_spatial,
                   b_spatial=b_spatial, sinks=sinks, w_out=w_out, w_ada_final=w_ada_final, b_ada_final=b_ada_final,
                   final_norm_g=final_norm_g)
    m_in = dict(w_ada=m_w_ada, b_ada=m_b_ada, norm_g=m_norm_g, w_in=m_w_in, ln_v_g=m_ln_v_g, ln_v_b=m_ln_v_b,
                w_spatial=m_w_spatial, b_spatial=m_b_spatial, sinks=m_sinks, w_out=m_w_out, w_ada_final=m_w_ada_final,
                b_ada_final=m_b_ada_final, final_norm_g=m_final_norm_g)
    v_in = dict(w_ada=v_w_ada, b_ada=v_b_ada, norm_g=v_norm_g, w_in=v_w_in, ln_v_g=v_ln_v_g, ln_v_b=v_ln_v_b,
                w_spatial=v_w_spatial, b_spatial=v_b_spatial, sinks=v_sinks, w_out=v_w_out, w_ada_final=v_w_ada_final,
                b_ada_final=v_b_ada_final, final_norm_g=v_final_norm_g)
    c_act_t = c_act.T
    outer = {"w_ada": lax.dynamic_slice(dmod_all, (0, chip * n_ada), (N_DEV, n_ada)),
             "w_ada_final": lax.dynamic_slice(dmodf_all, (0, chip * n_adaf), (N_DEV, n_adaf))}
    halves = {"w_in": (mine_in, theirs_in[0]), "w_out": (mine_out, theirs_out[0])}
    done = {}
    for name, (mine, theirs) in halves.items():
        shape2 = (2 * mine.shape[0], mine.shape[1])
        done[name] = _adam_halves_call(pos, weights[name].reshape(shape2), mine, theirs, m_in[name].reshape(shape2),
                                       v_in[name].reshape(shape2), "adam_" + name)
    for name, dm in outer.items():
        shape2 = (D, dm.shape[1])
        done[name] = _adam_outer_call(weights[name].reshape(shape2), c_act_t, dm, m_in[name].reshape(shape2),
                                      v_in[name].reshape(shape2), "adam_" + name)
    updates = _adam_small_call([(weights[name].reshape(g.shape), g, m_in[name].reshape(g.shape), v_in[name].reshape(g.shape))
                                for name, g in small.items()])
    for (name, g), upd in zip(small.items(), updates):
        done[name] = (g, *upd)
    outs = [[done[name][k].reshape(w.shape) for name, w in weights.items()] for k in range(4)]
    return (loss, grad_x.reshape(x.shape), *outs[0], *outs[1], *outs[2], *outs[3])
```

```python
import jax
import jax.numpy as jnp
from jax import lax
from jax.experimental import pallas as pl
from jax.experimental.pallas import tpu as pltpu

F32 = jnp.float32
BF16 = jnp.bfloat16
MESH = pl.DeviceIdType.MESH

D = 2048
D_A = 1024
D_B = 1024
D_KV = 256
HEAD = 64
N_Q = 16
N_KV = 4
Q_PER_KV = N_Q // N_KV
BLK = 128
GROUPS = 8
D_IN = 5632
OFF_Q, OFF_K, OFF_V, OFF_ZB = 3072, 4096, 4352, 4608
N_CHIP = 4
N_DEV = 8
W_IN_SHARD = D_IN // N_CHIP
W_OUT_SHARD = D // N_CHIP
EPS = 1e-5
SCALE = HEAD ** -0.5
NEG = -1e30
LANE = 128
VMEM_LIMIT = 56 * 1024 * 1024

ADAM_LR, ADAM_B1, ADAM_B2, ADAM_EPS, ADAM_WD, ADAM_STEP = 0.001, 0.9, 0.999, 1e-08, 0.01, 10
ADAM_C1 = 1.0 - ADAM_B1 ** ADAM_STEP
ADAM_C2 = 1.0 - ADAM_B2 ** ADAM_STEP
ADAM_ROWS = 256

NT = (((1,), (1,)), ((), ()))
TN = (((0,), (0,)), ((), ()))


def _params(*sem):
    return pltpu.CompilerParams(dimension_semantics=sem, vmem_limit_bytes=VMEM_LIMIT)


def _silu_parts(z):
    sig = 1.0 / (1.0 + jnp.exp(-z))
    return z * sig, sig


def _swap_halves(v, first_half):
    return jnp.where(first_half, pltpu.roll(v, 96, 1), pltpu.roll(v, 32, 1))


def _rope(v, cos_t, sin_s, first_half):
    return v * cos_t + _swap_halves(v, first_half) * sin_s


def _unrope(dv, cos_t, sin_s, first_half):
    return dv * cos_t - _swap_halves(dv, first_half) * sin_s


def _lane_masks():
    lane = lax.broadcasted_iota(jnp.int32, (BLK, LANE), 1)
    return (lane % HEAD) < (HEAD // 2), lane < HEAD


def _band_valid(first_block_bound, rows=BLK):
    rr = lax.broadcasted_iota(jnp.int32, (rows, 2 * BLK), 0) & (BLK - 1)
    jj = lax.broadcasted_iota(jnp.int32, (rows, 2 * BLK), 1)
    return (jj > rr) & (jj <= rr + BLK) & (jj >= first_block_bound)


def _dup_kv(slab, lo):
    rolled = pltpu.roll(slab, HEAD, 1)
    return jnp.where(lo, slab, rolled).astype(BF16), jnp.where(lo, rolled, slab).astype(BF16)


def _stack_heads(ref, sb, slab, lo, dtype):
    kh, base = sb // 2, 2 * (sb % 2) * BLK
    zero = jnp.zeros_like(slab)
    ref[kh, base:base + BLK, :] = jnp.where(lo, slab, zero).astype(dtype)
    ref[kh, base + BLK:base + 2 * BLK, :] = jnp.where(lo, zero, slab).astype(dtype)


def _unstack_heads(ref, sb, lo):
    kh, base = sb // 2, 2 * (sb % 2) * BLK
    return jnp.where(lo, ref[kh, base:base + BLK, :], ref[kh, base + BLK:base + 2 * BLK, :])


def _sink_column(sinks_ref, kh):
    row = lax.broadcasted_iota(jnp.int32, (Q_PER_KV * BLK, 1), 0)
    col = jnp.full(row.shape, sinks_ref[Q_PER_KV * kh + Q_PER_KV - 1], F32)
    for n in range(Q_PER_KV - 2, -1, -1):
        col = jnp.where(row < (n + 1) * BLK, sinks_ref[Q_PER_KV * kh + n], col)
    return col


def _tril():
    t = lax.broadcasted_iota(jnp.int32, (BLK, BLK), 0)
    s = lax.broadcasted_iota(jnp.int32, (BLK, BLK), 1)
    return s <= t


def _layer_norm_fwd(va, lg, lb):
    mu = jnp.mean(va, axis=-1, keepdims=True)
    xc = va - mu
    rstd = lax.rsqrt(jnp.mean(xc * xc, axis=-1, keepdims=True) + EPS)
    vhat = xc * rstd
    return vhat, rstd, vhat * lg + lb


def _softmax_sink(qm, kdup, bias, sink):
    s = lax.dot_general(qm, kdup, NT, preferred_element_type=F32) + bias
    m = jnp.maximum(jnp.max(s, axis=-1, keepdims=True), sink)
    p = jnp.exp(s - m)
    esink = jnp.exp(sink - m)
    inv = 1.0 / (jnp.sum(p, axis=-1, keepdims=True) + esink)
    return p * inv, esink * inv


def _band_bias(bias_ref):
    rows = bias_ref.shape[1]
    bias_ref[0] = jnp.where(_band_valid(BLK, rows), 0.0, NEG)
    bias_ref[1] = jnp.where(_band_valid(0, rows), 0.0, NEG)


def _rowmat_call(c_all, w, b, name):
    n = w.shape[1]
    tn = 512

    def body(c_ref, w_ref, b_ref, o_ref, ca_ref):
        ca, _ = _silu_parts(c_ref[...])
        ca_ref[...] = ca
        o_ref[...] = jnp.dot(ca.astype(BF16), w_ref[...].astype(BF16), preferred_element_type=F32) + b_ref[...]

    return pl.pallas_call(
        body, name=name, grid=(n // tn,),
        in_specs=[pl.BlockSpec((N_DEV, D), lambda j: (0, 0)), pl.BlockSpec((D, tn), lambda j: (0, j)),
                  pl.BlockSpec((1, tn), lambda j: (0, j))],
        out_specs=[pl.BlockSpec((N_DEV, tn), lambda j: (0, j)), pl.BlockSpec((N_DEV, D), lambda j: (0, 0))],
        out_shape=[jax.ShapeDtypeStruct((N_DEV, n), F32), jax.ShapeDtypeStruct((N_DEV, D), F32)],
        compiler_params=_params("arbitrary"),
    )(c_all, w, b)


def _cast_into_call(pos, w, full_shape, name):
    r, n = w.shape
    tr = min(r, 512)
    by_cols = full_shape[0] == r
    nrb = r // tr

    def body(pos_ref, w_ref, o_ref):
        o_ref[...] = w_ref[...].astype(BF16)

    out_map = (lambda i, pos: (i, pos[0])) if by_cols else (lambda i, pos: (pos[0] * nrb + i, 0))
    return pl.pallas_call(
        body, name=name,
        grid_spec=pltpu.PrefetchScalarGridSpec(
            num_scalar_prefetch=1, grid=(nrb,),
            in_specs=[pl.BlockSpec((tr, n), lambda i, pos: (i, 0))], out_specs=pl.BlockSpec((tr, n), out_map)),
        out_shape=jax.ShapeDtypeStruct(full_shape, BF16), compiler_params=_params("parallel"),
    )(pos, w)


def _proj_call(x, shift, scale, norm_g, w_bf):
    s = x.shape[0]
    tm = min(s, 1024)
    tn = 512

    def body(x_ref, sh_ref, sc_ref, g_ref, w_ref, proj_ref, h_ref):
        @pl.when(pl.program_id(1) == 0)
        def _():
            xv = x_ref[...]
            r = lax.rsqrt(jnp.mean(xv * xv, axis=-1, keepdims=True) + EPS)
            h_ref[...] = ((xv * r * g_ref[...]) * (1.0 + sc_ref[...]) + sh_ref[...]).astype(BF16)

        proj_ref[...] = jnp.dot(h_ref[...], w_ref[...], preferred_element_type=F32)

    vec = pl.BlockSpec((1, D), lambda i, j: (0, 0))
    return pl.pallas_call(
        body, name="proj", grid=(s // tm, D_IN // tn),
        in_specs=[pl.BlockSpec((tm, D), lambda i, j: (i, 0)), vec, vec, vec, pl.BlockSpec((D, tn), lambda i, j: (0, j))],
        out_specs=[pl.BlockSpec((tm, tn), lambda i, j: (i, j)), pl.BlockSpec((tm, D), lambda i, j: (i, 0))],
        out_shape=[jax.ShapeDtypeStruct((s, D_IN), F32), jax.ShapeDtypeStruct((s, D), BF16)],
        compiler_params=_params("parallel", "arbitrary"),
    )(x, shift, scale, norm_g, w_bf)


def _proj_gather_whole_shards_call(pos, x, shift, scale, norm_g, wi_full, wo_full):
    s = x.shape[0]
    tm = min(s, 512)
    nrow = s // tm
    hi = D // 2
    ho = W_OUT_SHARD // 2

    def body(pos_ref, x_ref, sh_ref, sc_ref, g_ref, wi_in, wo_in, proj_ref, h_ref, fi_ref, fo_ref,
             h_all, wbuf, send_sems, recv_sems, load_sem):
        del wi_in, wo_in
        p = pl.program_id(0)
        i = pl.program_id(1)
        x_, y_, c_ = _coords()
        me, sibling = (x_, y_, c_), (x_, y_, 1 - c_)

        def shard_of(q):
            px, py, _ = _peer(x_, y_, c_, q, 0)
            return 2 * px + py

        def part(which, q, pc, sub=None):
            n = hi if which == 0 else ho
            base = pc * n
            if sub is not None:
                n //= 2
                base = base + sub * n
            if which == 0:
                return fi_ref.at[pl.ds(base, n), pl.ds(shard_of(q) * W_IN_SHARD, W_IN_SHARD)]
            return fo_ref.at[pl.ds(shard_of(q) * W_OUT_SHARD + base, n), :]

        def copy(k, which, q, pc, to, sub=None):
            ref = part(which, q, pc, sub)
            return pltpu.make_async_remote_copy(src_ref=ref, dst_ref=ref, send_sem=send_sems.at[k], recv_sem=recv_sems.at[k],
                                                device_id=to, device_id_type=MESH)

        def to_neighbour(which, q):
            return copy(8 * which + q - 1, which, 0, c_, _peer(x_, y_, c_, q, 0))

        def from_neighbour(which, q):
            return copy(8 * which + q - 1, which, q, c_, me)

        def relay(which, q):
            return copy(8 * which + 2 + q - 1, which, q, c_, _peer(x_, y_, c_, 3 - q, 0), q - 1)

        def relayed(which, sub):
            return copy(8 * which + 2 + sub, which, 3, c_, me, sub)

        def to_sibling(which, q):
            return copy(8 * which + 4 + q - 1, which, q, c_, sibling)

        def from_sibling(which, q):
            return copy(8 * which + 4 + q - 1, which, q, 1 - c_, me)

        def relayed_to_sibling(which, sub):
            return copy(8 * which + 6 + sub, which, 3, c_, sibling, sub)

        def relayed_from_sibling(which, sub):
            return copy(8 * which + 6 + sub, which, 3, 1 - c_, me, sub)

        def pass_on_neighbours(which):
            for q in (1, 2):
                from_neighbour(which, q).wait_recv()
                to_sibling(which, q).start()
                relay(which, q).start()

        def pass_on_relayed(which):
            for sub in range(2):
                relayed(which, sub).wait_recv()
                relayed_to_sibling(which, sub).start()

        def load_shard(q):
            cp = pltpu.make_async_copy(fi_ref.at[:, pl.ds(shard_of(q) * W_IN_SHARD, W_IN_SHARD)], wbuf, load_sem)
            cp.start()
            cp.wait()

        @pl.when((p == 0) & (i == 0))
        def _():
            for q in (1, 2):
                to_neighbour(0, q).start()
            load_shard(0)

        @pl.when((p == 1) & (i == 0))
        def _():
            pass_on_neighbours(0)
            for q in (1, 2):
                to_neighbour(1, q).start()
            from_sibling(0, 1).wait_recv()
            load_shard(1)

        @pl.when((p == 2) & (i == 0))
        def _():
            from_sibling(0, 2).wait_recv()
            load_shard(2)

        @pl.when((p == 3) & (i == 0))
        def _():
            pass_on_relayed(0)
            pass_on_neighbours(1)
            for sub in range(2):
                relayed_from_sibling(0, sub).wait_recv()
            load_shard(3)

        rows = pl.ds(pl.multiple_of(i * tm, tm), tm)

        @pl.when(p == 0)
        def _():
            xv = x_ref[...]
            r = lax.rsqrt(jnp.mean(xv * xv, axis=-1, keepdims=True) + EPS)
            hv = ((xv * r * g_ref[...]) * (1.0 + sc_ref[...]) + sh_ref[...]).astype(BF16)
            h_ref[...] = hv
            h_all[rows, :] = hv

        proj_ref[...] = jnp.dot(h_all[rows, :], wbuf[...], preferred_element_type=F32)

        @pl.when((p == N_CHIP - 1) & (i == nrow - 1))
        def _():
            pass_on_relayed(1)
            for q in (1, 2):
                from_sibling(1, q).wait_recv()
            for sub in range(2):
                relayed_from_sibling(1, sub).wait_recv()
            for which in range(2):
                for q in (1, 2):
                    to_neighbour(which, q).wait_send()
                    relay(which, q).wait_send()
                    to_sibling(which, q).wait_send()
                    relayed_to_sibling(which, q - 1).wait_send()

    vec = pl.BlockSpec((1, D), lambda p, i, pos: (0, 0))
    first_phase_rows = lambda p, i, pos: (jnp.where(p == 0, i, nrow - 1), 0)
    anyspec = pl.BlockSpec(memory_space=pl.ANY)
    return pl.pallas_call(
        body, name="proj_gather",
        grid_spec=pltpu.PrefetchScalarGridSpec(
            num_scalar_prefetch=1, grid=(N_CHIP, nrow),
            in_specs=[pl.BlockSpec((tm, D), first_phase_rows), vec, vec, vec, anyspec, anyspec],
            out_specs=[pl.BlockSpec((tm, W_IN_SHARD), lambda p, i, pos: (i, jnp.bitwise_xor(pos[0], p))),
                       pl.BlockSpec((tm, D), first_phase_rows), anyspec, anyspec],
            scratch_shapes=[pltpu.VMEM((s, D), BF16), pltpu.VMEM((D, W_IN_SHARD), BF16),
                            pltpu.SemaphoreType.DMA((16,)), pltpu.SemaphoreType.DMA((16,)), pltpu.SemaphoreType.DMA]),
        out_shape=[jax.ShapeDtypeStruct((s, D_IN), F32), jax.ShapeDtypeStruct((s, D), BF16),
                   jax.ShapeDtypeStruct((D, D_IN), BF16), jax.ShapeDtypeStruct((D, D), BF16)],
        input_output_aliases={5: 2, 6: 3},
        compiler_params=_params("arbitrary", "arbitrary"),
    )(pos, x, shift, scale, norm_g, wi_full, wo_full)


W_IN_PARTS = ((0, 768), (768, 640))


def _proj_gather_call(pos, x, shift, scale, norm_g, wi_full, wo_full):
    s = x.shape[0]
    tm = min(s, 512)
    nrow = s // tm
    hi = D // 2
    ho = W_OUT_SHARD // 2
    phases = [(0, None), (1, 0), (2, 0), (1, 1), (2, 1), (3, 0), (3, 1)]

    def body(pos_ref, x_ref, sh_ref, sc_ref, g_ref, wi_in, wo_in, h_ref, proj_ref, fi_ref, fo_ref,
             h_all, wbuf, obuf, send_sems, recv_sems, load_sem, out_sems):
        del wi_in, wo_in
        p = pl.program_id(0)
        i = pl.program_id(1)
        x_, y_, c_ = _coords()
        me, sibling = (x_, y_, c_), (x_, y_, 1 - c_)

        def shard_of(q):
            px, py, _ = _peer(x_, y_, c_, q, 0)
            return 2 * px + py

        def cols_of(q, cp):
            off, w = (0, W_IN_SHARD) if cp is None else W_IN_PARTS[cp]
            return shard_of(q) * W_IN_SHARD + off, w

        def part(which, q, pc, sub, cp):
            n = hi if which == 0 else ho
            base = pc * n
            if sub is not None:
                n //= 2
                base = base + sub * n
            if which == 0:
                c0, w = cols_of(q, cp)
                return fi_ref.at[pl.ds(base, n), pl.ds(c0, w)]
            return fo_ref.at[pl.ds(shard_of(q) * W_OUT_SHARD + base, n), :]

        def copy(k, ref, to):
            return pltpu.make_async_remote_copy(src_ref=ref, dst_ref=ref, send_sem=send_sems.at[k], recv_sem=recv_sems.at[k],
                                                device_id=to, device_id_type=MESH)

        def sem(which, kind, j, cp):
            return 4 * kind + 2 * cp + j if which == 0 else 16 + 2 * kind + j

        def to_neighbour(which, q, cp=None):
            return copy(sem(which, 0, q - 1, cp), part(which, 0, c_, None, cp), _peer(x_, y_, c_, q, 0))

        def from_neighbour(which, q, cp=None):
            return copy(sem(which, 0, q - 1, cp), part(which, q, c_, None, cp), me)

        def relay(which, q, cp=None):
            return copy(sem(which, 1, q - 1, cp), part(which, q, c_, q - 1, cp), _peer(x_, y_, c_, 3 - q, 0))

        def relayed(which, sub, cp=None):
            return copy(sem(which, 1, sub, cp), part(which, 3, c_, sub, cp), me)

        def to_sibling(which, q, cp=None):
            return copy(sem(which, 2, q - 1, cp), part(which, q, c_, None, cp), sibling)

        def from_sibling(which, q, cp=None):
            return copy(sem(which, 2, q - 1, cp), part(which, q, 1 - c_, None, cp), me)

        def relayed_to_sibling(which, sub, cp=None):
            return copy(sem(which, 3, sub, cp), part(which, 3, c_, sub, cp), sibling)

        def relayed_from_sibling(which, sub, cp=None):
            return copy(sem(which, 3, sub, cp), part(which, 3, 1 - c_, sub, cp), me)

        def pass_on_neighbours(which, cp=None):
            for q in (1, 2):
                from_neighbour(which, q, cp).wait_recv()
                to_sibling(which, q, cp).start()
                relay(which, q, cp).start()

        def pass_on_relayed(which, cp=None):
            for sub in range(2):
                relayed(which, sub, cp).wait_recv()
                relayed_to_sibling(which, sub, cp).start()

        def shard_load(q, cp):
            c0, w = cols_of(q, cp)
            return pltpu.make_async_copy(fi_ref.at[:, pl.ds(c0, w)], wbuf.at[:, 0:w], load_sem)

        def out_copy(k, slot, row0):
            c0, w = cols_of(*phases[k])
            return pltpu.make_async_copy(obuf.at[slot, :, 0:w], proj_ref.at[pl.ds(row0, tm), pl.ds(c0, w)], out_sems.at[slot])

        def drain(k):
            for j in range(min(2, nrow)):
                out_copy(k, (nrow - 1 - j) % 2, 0).wait()

        def arrivals(k):
            q, cp = phases[k]
            if k == 0:
                for cp_ in range(2):
                    for q_ in (1, 2):
                        to_neighbour(0, q_, cp_).start()
            elif q < 3 and k in (1, 3):
                pass_on_neighbours(0, cp)
                if k == 1:
                    for q_ in (1, 2):
                        to_neighbour(1, q_).start()
            elif k == 5:
                for cp_ in range(2):
                    pass_on_relayed(0, cp_)
                pass_on_neighbours(1)
            if q in (1, 2):
                from_sibling(0, q, cp).wait_recv()
            elif q == 3:
                for sub in range(2):
                    relayed_from_sibling(0, sub, cp).wait_recv()

        rows = pl.ds(pl.multiple_of(i * tm, tm), tm)
        slot = i % 2
        for k, (q, cp) in enumerate(phases):
            @pl.when(p == k)
            def _(k=k, q=q, cp=cp):
                @pl.when(i == 0)
                def _():
                    arrivals(k)
                    load = shard_load(q, cp)
                    load.start()
                    if k > 0:
                        drain(k - 1)
                    load.wait()

                if k == 0:
                    xv = x_ref[...]
                    r = lax.rsqrt(jnp.mean(xv * xv, axis=-1, keepdims=True) + EPS)
                    hv = ((xv * r * g_ref[...]) * (1.0 + sc_ref[...]) + sh_ref[...]).astype(BF16)
                    h_ref[...] = hv
                    h_all[rows, :] = hv

                @pl.when(i >= 2)
                def _():
                    out_copy(k, slot, 0).wait()

                w = cols_of(q, cp)[1]
                obuf[slot, :, 0:w] = jnp.dot(h_all[rows, :], wbuf[:, 0:w], preferred_element_type=F32)
                out_copy(k, slot, pl.multiple_of(i * tm, tm)).start()

        @pl.when((p == len(phases) - 1) & (i == nrow - 1))
        def _():
            drain(len(phases) - 1)
            pass_on_relayed(1)
            for q in (1, 2):
                from_sibling(1, q).wait_recv()
            for sub in range(2):
                relayed_from_sibling(1, sub).wait_recv()
            for which, cps in ((0, (0, 1)), (1, (None,))):
                for cp in cps:
                    for q in (1, 2):
                        to_neighbour(which, q, cp).wait_send()
                        relay(which, q, cp).wait_send()
                        to_sibling(which, q, cp).wait_send()
                        relayed_to_sibling(which, q - 1, cp).wait_send()

    vec = pl.BlockSpec((1, D), lambda p, i, pos: (0, 0))
    first_phase_rows = lambda p, i, pos: (jnp.where(p == 0, i, nrow - 1), 0)
    anyspec = pl.BlockSpec(memory_space=pl.ANY)
    return pl.pallas_call(
        body, name="proj_gather",
        grid_spec=pltpu.PrefetchScalarGridSpec(
            num_scalar_prefetch=1, grid=(len(phases), nrow),
            in_specs=[pl.BlockSpec((tm, D), first_phase_rows), vec, vec, vec, anyspec, anyspec],
            out_specs=[pl.BlockSpec((tm, D), first_phase_rows), anyspec, anyspec, anyspec],
            scratch_shapes=[pltpu.VMEM((s, D), BF16), pltpu.VMEM((D, W_IN_SHARD), BF16), pltpu.VMEM((2, tm, W_IN_SHARD), F32),
                            pltpu.SemaphoreType.DMA((24,)), pltpu.SemaphoreType.DMA((24,)), pltpu.SemaphoreType.DMA,
                            pltpu.SemaphoreType.DMA((2,))]),
        out_shape=[jax.ShapeDtypeStruct((s, D), BF16), jax.ShapeDtypeStruct((s, D_IN), F32),
                   jax.ShapeDtypeStruct((D, D_IN), BF16), jax.ShapeDtypeStruct((D, D), BF16)],
        input_output_aliases={5: 2, 6: 3},
        compiler_params=_params("arbitrary", "arbitrary"),
    )(pos, x, shift, scale, norm_g, wi_full, wo_full)


def _proj_specs(rev_nb=None):
    if rev_nb is None:
        row = lambda i: i
    else:
        row = lambda i: rev_nb - 1 - i
    wide = lambda col: pl.BlockSpec((BLK, D_A), lambda i: (row(i), col))
    kv = lambda col: pl.BlockSpec((BLK, D_KV), lambda i: (row(i), col))
    half = lambda col: pl.BlockSpec((BLK, 512), lambda i: (row(i), col))
    return [wide(0), wide(1), wide(2), wide(3), kv(OFF_K // D_KV), kv(OFF_V // D_KV), half(OFF_ZB // 512), half(OFF_ZB // 512 + 1)]


def _mix_fwd_call(proj, cos, sin, ln_g, ln_b, w_sp, b_sp_t, sinks):
    s = proj.shape[0]
    nb = s // BLK

    def body(ua_ref, va_ref, za_ref, q_ref, k_ref, v_ref, zb0_ref, zb1_ref, cos_ref, sin_ref, lg_ref, lb_ref,
             w_ref, bt_ref, sinks_ref, y_ref, kdup_ref, vdup_ref, qm_ref, ost_ref, bias_ref):
        i = pl.program_id(0)
        first_half, lo = _lane_masks()
        cos_t = cos_ref[...]
        sin_t = sin_ref[...]

        _, _, vln = _layer_norm_fwd(va_ref[...], lg_ref[...], lb_ref[...])
        tril = _tril()
        for g in range(GROUPS):
            cols = slice(g * BLK, (g + 1) * BLK)
            wg = jnp.where(tril, w_ref[g], 0.0).astype(BF16)
            sg = jnp.dot(wg, vln[:, cols].astype(BF16), preferred_element_type=F32) + bt_ref[:, g:g + 1]
            gate, _ = _silu_parts(za_ref[:, cols])
            y_ref[:, cols] = (ua_ref[:, cols] * sg * gate).astype(BF16)

        @pl.when(i == 0)
        def _():
            kdup_ref[:, 0:BLK, :] = jnp.zeros((N_KV, BLK, LANE), BF16)
            vdup_ref[:, 0:BLK, :] = jnp.zeros((N_KV, BLK, LANE), BF16)
            _band_bias(bias_ref)

        @pl.when(i > 0)
        def _():
            kdup_ref[:, 0:BLK, :] = kdup_ref[:, BLK:2 * BLK, :]
            vdup_ref[:, 0:BLK, :] = vdup_ref[:, BLK:2 * BLK, :]

        for ks in range(2):
            cols = slice(ks * LANE, (ks + 1) * LANE)
            kr = _rope(k_ref[:, cols], cos_t, sin_t, first_half)
            for n, (kd, vd) in enumerate(zip(_dup_kv(kr, lo), _dup_kv(v_ref[:, cols], lo))):
                kdup_ref[2 * ks + n, BLK:2 * BLK, :] = kd
                vdup_ref[2 * ks + n, BLK:2 * BLK, :] = vd
        for sb in range(8):
            _stack_heads(qm_ref, sb, _rope(q_ref[:, sb * LANE:(sb + 1) * LANE], cos_t, sin_t, first_half) * SCALE, lo, BF16)

        block_kind = jnp.where(i > 0, 1, 0)

        def kv_head(kh, carry):
            probs, _ = _softmax_sink(qm_ref[kh], kdup_ref[kh], bias_ref[block_kind], _sink_column(sinks_ref, kh))
            ost_ref[kh] = jnp.dot(probs.astype(BF16), vdup_ref[kh], preferred_element_type=F32)
            return carry

        lax.fori_loop(0, N_KV, kv_head, 0, unroll=2)
        for sb in range(8):
            cols = slice(sb * LANE, (sb + 1) * LANE)
            zb = zb0_ref[:, cols] if sb < 4 else zb1_ref[:, (sb - 4) * LANE:(sb - 3) * LANE]
            gate, _ = _silu_parts(zb)
            y_ref[:, D_A + sb * LANE:D_A + (sb + 1) * LANE] = (_unstack_heads(ost_ref, sb, lo) * gate).astype(BF16)

    tab = pl.BlockSpec((BLK, LANE), lambda i: (i, 0))
    return pl.pallas_call(
        body, name="mix_fwd", grid=(nb,),
        in_specs=_proj_specs() + [
            tab, tab, pl.BlockSpec((1, D_A), lambda i: (0, 0)), pl.BlockSpec((1, D_A), lambda i: (0, 0)),
            pl.BlockSpec((GROUPS, BLK, BLK), lambda i: (0, 0, 0)), pl.BlockSpec((BLK, GROUPS), lambda i: (0, 0)),
            pl.BlockSpec(memory_space=pltpu.SMEM)],
        out_specs=pl.BlockSpec((BLK, 2 * D_A), lambda i: (i, 0)),
        out_shape=jax.ShapeDtypeStruct((s, 2 * D_A), BF16),
        scratch_shapes=[pltpu.VMEM((N_KV, 2 * BLK, LANE), BF16), pltpu.VMEM((N_KV, 2 * BLK, LANE), BF16),
                        pltpu.VMEM((N_KV, Q_PER_KV * BLK, LANE), BF16), pltpu.VMEM((N_KV, Q_PER_KV * BLK, LANE), F32),
                        pltpu.VMEM((2, Q_PER_KV * BLK, 2 * BLK), F32)],
        compiler_params=_params("arbitrary"),
    )(proj, proj, proj, proj, proj, proj, proj, proj, cos, sin, ln_g, ln_b, w_sp, b_sp_t, sinks)


def _tail_call(y, w_out_bf, x, target, gate, shift_f, scale_f, gf):
    s = x.shape[0]
    tm = min(s, 256)
    nsteps = s // tm

    def body(y_ref, w_ref, x_ref, t_ref, gate_ref, shf_ref, scf_ref, gf_ref, dx2_ref, do_ref, dy_ref, st_ref):
        i = pl.program_id(0)

        @pl.when(i == 0)
        def _():
            st_ref[...] = jnp.zeros((8, D), F32)

        o = jnp.dot(y_ref[...], w_ref[...], preferred_element_type=F32)
        gate_v = gate_ref[...]
        x2 = x_ref[...] + gate_v * o
        r2 = lax.rsqrt(jnp.mean(x2 * x2, axis=-1, keepdims=True) + EPS)
        xn2 = x2 * r2
        hn2 = xn2 * gf_ref[...]
        one_sc = 1.0 + scf_ref[...]
        err = hn2 * one_sc + shf_ref[...] - t_ref[...]
        dout = err * (1.0 / D)
        dhn2 = dout * one_sc
        dxn2 = dhn2 * gf_ref[...]
        dx2 = r2 * (dxn2 - xn2 * jnp.mean(dxn2 * xn2, axis=-1, keepdims=True))
        dx2_ref[...] = dx2
        do = (dx2 * gate_v).astype(BF16)
        do_ref[...] = do
        dy_ref[...] = lax.dot_general(do, w_ref[...], NT, preferred_element_type=F32)
        st_ref[0:1, :] += jnp.sum(dout, axis=0, keepdims=True)
        st_ref[1:2, :] += jnp.sum(dout * hn2, axis=0, keepdims=True)
        st_ref[2:3, :] += jnp.sum(dhn2 * xn2, axis=0, keepdims=True)
        st_ref[3:4, :] += jnp.sum(dx2 * o, axis=0, keepdims=True)
        st_ref[4:5, :] += jnp.sum(err * err, axis=0, keepdims=True)

        @pl.when(i == nsteps - 1)
        def _():
            st_ref[5:6, :] = jnp.full((1, D), 0.5 / D, F32) * jnp.sum(st_ref[4:5, :])

    vec = pl.BlockSpec((1, D), lambda i: (0, 0))
    rows = lambda: pl.BlockSpec((tm, D), lambda i: (i, 0))
    return pl.pallas_call(
        body, name="tail", grid=(nsteps,),
        in_specs=[rows(), pl.BlockSpec((D, D), lambda i: (0, 0)), rows(), rows(), vec, vec, vec, vec],
        out_specs=[rows(), rows(), rows(), pl.BlockSpec((8, D), lambda i: (0, 0))],
        out_shape=[jax.ShapeDtypeStruct((s, D), F32), jax.ShapeDtypeStruct((s, D), BF16), jax.ShapeDtypeStruct((s, D), F32),
                   jax.ShapeDtypeStruct((8, D), F32)],
        compiler_params=_params("arbitrary"),
    )(y, w_out_bf, x, target, gate, shift_f, scale_f, gf)


def _tn_call(a, b, name):
    s, m = a.shape
    n = b.shape[1]
    tn = 512
    ts = min(s, 1024)
    nk = s // ts

    def body(a_ref, b_ref, o_ref, acc_ref):
        k = pl.program_id(1)

        @pl.when(k == 0)
        def _():
            acc_ref[...] = jnp.zeros((m, tn), F32)

        acc_ref[...] += lax.dot_general(a_ref[...], b_ref[...], TN, preferred_element_type=F32)

        @pl.when(k == nk - 1)
        def _():
            o_ref[...] = acc_ref[...].astype(BF16)

    return pl.pallas_call(
        body, name=name, grid=(n // tn, nk),
        in_specs=[pl.BlockSpec((ts, m), lambda j, k: (k, 0)), pl.BlockSpec((ts, tn), lambda j, k: (k, j))],
        out_specs=pl.BlockSpec((m, tn), lambda j, k: (0, j)),
        out_shape=jax.ShapeDtypeStruct((m, n), BF16),
        scratch_shapes=[pltpu.VMEM((m, tn), F32)],
        compiler_params=_params("parallel", "arbitrary"),
    )(a, b)


def _tn_shards_call(pos, a, b, qs, name):
    s, m = a.shape
    ts = min(s, 1024)
    nk = s // ts

    def body(pos_ref, a_ref, b_ref, o_ref, acc_ref):
        k = pl.program_id(1)

        @pl.when(k == 0)
        def _():
            acc_ref[...] = jnp.zeros((m, W_IN_SHARD), F32)

        acc_ref[...] += lax.dot_general(a_ref[...], b_ref[...], TN, preferred_element_type=F32)

        @pl.when(k == nk - 1)
        def _():
            o_ref[...] = acc_ref[...].astype(BF16)

    def shard(j, pos):
        q = qs[0]
        for n in range(1, len(qs)):
            q = jnp.where(j == n, qs[n], q)
        return jnp.bitwise_xor(pos[0], q)

    return pl.pallas_call(
        body, name=name,
        grid_spec=pltpu.PrefetchScalarGridSpec(
            num_scalar_prefetch=1, grid=(len(qs), nk),
            in_specs=[pl.BlockSpec((ts, m), lambda j, k, pos: (k, 0)),
                      pl.BlockSpec((ts, W_IN_SHARD), lambda j, k, pos: (k, shard(j, pos)))],
            out_specs=pl.BlockSpec((m, W_IN_SHARD), lambda j, k, pos: (0, j)),
            scratch_shapes=[pltpu.VMEM((m, W_IN_SHARD), F32)]),
        out_shape=jax.ShapeDtypeStruct((m, len(qs) * W_IN_SHARD), BF16),
        compiler_params=_params("parallel", "arbitrary"),
    )(pos, a, b)


def _mix_bwd_call(proj, dy, cos, sin, ln_g, ln_b, w_sp, w_sp_t, b_sp_t, sinks):
    s = proj.shape[0]
    nb = s // BLK
    rev = lambda i: nb - 1 - i
    prev = lambda i: jnp.maximum(nb - 2 - i, 0)

    def body(ua_ref, va_ref, za_ref, q_ref, k_ref, v_ref, zb0_ref, zb1_ref, kp_ref, vp_ref, dy_ref,
             cos_ref, sin_ref, cosp_ref, sinp_ref, lg_ref, lb_ref, w_ref, wt_ref, bt_ref, sinks_ref,
             dp_ref, lnst_ref, dw_ref, dbt_ref, dsink_ref,
             kdup_ref, vdup_ref, dvln_ref, qm_ref, dom_ref, ost_ref, dqst_ref, dkdup_ref, dvdup_ref, kcar_ref, vcar_ref,
             sigb_ref, bias_ref):
        i = pl.program_id(0)
        first_half, lo = _lane_masks()
        lane8 = lax.broadcasted_iota(jnp.int32, (8, LANE), 1)
        cos_t = cos_ref[...]
        sin_t = sin_ref[...]

        @pl.when(i == 0)
        def _():
            lnst_ref[...] = jnp.zeros((8, D_A), F32)
            dw_ref[...] = jnp.zeros((GROUPS, BLK, BLK), F32)
            dbt_ref[...] = jnp.zeros((BLK, LANE), F32)
            dsink_ref[...] = jnp.zeros((8, LANE), F32)
            kcar_ref[...] = jnp.zeros((BLK, D_KV), F32)
            vcar_ref[...] = jnp.zeros((BLK, D_KV), F32)
            _band_bias(bias_ref)

        vhat, rstd, vln = _layer_norm_fwd(va_ref[...], lg_ref[...], lb_ref[...])
        tril = _tril()
        triu = jnp.logical_not(tril) | (lax.broadcasted_iota(jnp.int32, (BLK, BLK), 0) == lax.broadcasted_iota(jnp.int32, (BLK, BLK), 1))
        lane_b = lax.broadcasted_iota(jnp.int32, (BLK, LANE), 1)
        db_acc = jnp.zeros((BLK, LANE), F32)
        for g in range(GROUPS):
            cols = slice(g * BLK, (g + 1) * BLK)
            vln_g = vln[:, cols].astype(BF16)
            wg = jnp.where(tril, w_ref[g], 0.0).astype(BF16)
            sg = jnp.dot(wg, vln_g, preferred_element_type=F32) + bt_ref[:, g:g + 1]
            za = za_ref[:, cols]
            gate, sig = _silu_parts(za)
            ua = ua_ref[:, cols]
            dya_g = dy_ref[:, cols]
            dya = dya_g * gate
            dp_ref[:, cols] = (dya * sg).astype(BF16)
            dp_ref[:, 2 * D_A + g * BLK:2 * D_A + (g + 1) * BLK] = (
                dya_g * (ua * sg) * (sig * (1.0 + za * (1.0 - sig)))).astype(BF16)
            ds = dya * ua
            ds_b = ds.astype(BF16)
            wtg = jnp.where(triu, wt_ref[g], 0.0).astype(BF16)
            dvln_ref[:, cols] = jnp.dot(wtg, ds_b, preferred_element_type=F32)
            dw_ref[g] += jnp.where(tril, lax.dot_general(ds_b, vln_g, NT, preferred_element_type=F32), 0.0)
            db_acc = db_acc + jnp.where(lane_b == g, jnp.sum(ds, axis=-1, keepdims=True), 0.0)
        dbt_ref[...] += db_acc
        dvln = dvln_ref[...]
        lnst_ref[0:1, :] += jnp.sum(dvln * vhat, axis=0, keepdims=True)
        lnst_ref[1:2, :] += jnp.sum(dvln, axis=0, keepdims=True)
        dvhat = dvln * lg_ref[...]
        m1 = jnp.mean(dvhat, axis=-1, keepdims=True)
        m2 = jnp.mean(dvhat * vhat, axis=-1, keepdims=True)
        dp_ref[:, D_A:2 * D_A] = (rstd * (dvhat - m1 - vhat * m2)).astype(BF16)

        cosp = cosp_ref[...]
        sinp = sinp_ref[...]
        for ks in range(2):
            cols = slice(ks * LANE, (ks + 1) * LANE)
            kr = _rope(k_ref[:, cols], cos_t, sin_t, first_half)
            kpr = _rope(kp_ref[:, cols], cosp, sinp, first_half)
            for n, (kc, vc, kp, vp) in enumerate(zip(_dup_kv(kr, lo), _dup_kv(v_ref[:, cols], lo),
                                                     _dup_kv(kpr, lo), _dup_kv(vp_ref[:, cols], lo))):
                kdup_ref[2 * ks + n, BLK:2 * BLK, :] = kc
                vdup_ref[2 * ks + n, BLK:2 * BLK, :] = vc
                kdup_ref[2 * ks + n, 0:BLK, :] = kp
                vdup_ref[2 * ks + n, 0:BLK, :] = vp
        for sb in range(8):
            cols = slice(sb * LANE, (sb + 1) * LANE)
            _stack_heads(qm_ref, sb, _rope(q_ref[:, cols], cos_t, sin_t, first_half) * SCALE, lo, BF16)
            zb = zb0_ref[:, cols] if sb < 4 else zb1_ref[:, (sb - 4) * LANE:(sb - 3) * LANE]
            gate, sig = _silu_parts(zb)
            sigb_ref[:, cols] = sig
            _stack_heads(dom_ref, sb, dy_ref[:, D_A + sb * LANE:D_A + (sb + 1) * LANE] * gate, lo, F32)

        block_kind = jnp.where(i < nb - 1, 1, 0)

        def kv_head(kh, dsink_acc):
            qm = qm_ref[kh]
            kd = kdup_ref[kh]
            vd = vdup_ref[kh]
            probs, psink = _softmax_sink(qm, kd, bias_ref[block_kind], _sink_column(sinks_ref, kh))
            probs_b = probs.astype(BF16)
            o = jnp.dot(probs_b, vd, preferred_element_type=F32)
            ost_ref[kh] = o
            dom = dom_ref[kh]
            dom_b = dom.astype(BF16)
            delta = jnp.sum(dom * o, axis=-1, keepdims=True)
            dpr = lax.dot_general(dom_b, vd, NT, preferred_element_type=F32)
            dss = (probs * (dpr - delta)).astype(BF16)
            sd = psink * delta
            for n in range(Q_PER_KV):
                dsink_acc = dsink_acc + jnp.where(lane8 == Q_PER_KV * kh + n, -jnp.sum(sd[n * BLK:(n + 1) * BLK]), 0.0)
            dqst_ref[kh] = jnp.dot(dss, kd, preferred_element_type=F32)
            dkdup_ref[kh] = lax.dot_general(dss, qm, TN, preferred_element_type=F32)
            dvdup_ref[kh] = lax.dot_general(probs_b, dom_b, TN, preferred_element_type=F32)
            return dsink_acc

        dsink_acc = lax.fori_loop(0, N_KV // 2, lambda j, acc: kv_head(2 * j + 1, kv_head(2 * j, acc)), jnp.zeros((8, LANE), F32))
        row0 = lax.broadcasted_iota(jnp.int32, (8, LANE), 0) == 0
        dsink_ref[...] += jnp.where(row0, dsink_acc, 0.0)

        for sb in range(8):
            cols = slice(sb * LANE, (sb + 1) * LANE)
            zb = zb0_ref[:, cols] if sb < 4 else zb1_ref[:, (sb - 4) * LANE:(sb - 3) * LANE]
            sig = sigb_ref[:, cols]
            dyb = dy_ref[:, D_A + sb * LANE:D_A + (sb + 1) * LANE]
            dp_ref[:, OFF_ZB + sb * LANE:OFF_ZB + (sb + 1) * LANE] = (
                dyb * _unstack_heads(ost_ref, sb, lo) * (sig * (1.0 + zb * (1.0 - sig)))).astype(BF16)
            dq_r = _unstack_heads(dqst_ref, sb, lo) * SCALE
            dp_ref[:, OFF_Q + sb * LANE:OFF_Q + (sb + 1) * LANE] = _unrope(dq_r, cos_t, sin_t, first_half).astype(BF16)

        lo2 = lax.broadcasted_iota(jnp.int32, (2 * BLK, LANE), 1) < HEAD
        for ks in range(2):
            cols = slice(ks * LANE, (ks + 1) * LANE)
            ka = dkdup_ref[2 * ks]
            kb = dkdup_ref[2 * ks + 1]
            dk_band = jnp.where(lo2, ka + pltpu.roll(ka, HEAD, 1), kb + pltpu.roll(kb, HEAD, 1))
            va_ = dvdup_ref[2 * ks]
            vb_ = dvdup_ref[2 * ks + 1]
            dv_band = jnp.where(lo2, va_ + pltpu.roll(va_, HEAD, 1), vb_ + pltpu.roll(vb_, HEAD, 1))
            dkr = dk_band[BLK:2 * BLK, :] + kcar_ref[:, cols]
            dp_ref[:, OFF_K + ks * LANE:OFF_K + (ks + 1) * LANE] = _unrope(dkr, cos_t, sin_t, first_half).astype(BF16)
            dp_ref[:, OFF_V + ks * LANE:OFF_V + (ks + 1) * LANE] = (
                dv_band[BLK:2 * BLK, :] + vcar_ref[:, cols]).astype(BF16)
            kcar_ref[:, cols] = dk_band[0:BLK, :]
            vcar_ref[:, cols] = dv_band[0:BLK, :]

    tab = pl.BlockSpec((BLK, LANE), lambda i: (rev(i), 0))
    tabp = pl.BlockSpec((BLK, LANE), lambda i: (prev(i), 0))
    kvp = lambda col: pl.BlockSpec((BLK, D_KV), lambda i: (prev(i), col))
    vec = pl.BlockSpec((1, D_A), lambda i: (0, 0))
    w3 = pl.BlockSpec((GROUPS, BLK, BLK), lambda i: (0, 0, 0))
    return pl.pallas_call(
        body, name="mix_bwd", grid=(nb,),
        in_specs=_proj_specs(nb) + [
            kvp(OFF_K // D_KV), kvp(OFF_V // D_KV), pl.BlockSpec((BLK, 2 * D_A), lambda i: (rev(i), 0)),
            tab, tab, tabp, tabp, vec, vec, w3, w3, pl.BlockSpec((BLK, GROUPS), lambda i: (0, 0)),
            pl.BlockSpec(memory_space=pltpu.SMEM)],
        out_specs=[pl.BlockSpec((BLK, D_IN), lambda i: (rev(i), 0)), pl.BlockSpec((8, D_A), lambda i: (0, 0)), w3,
                   pl.BlockSpec((BLK, LANE), lambda i: (0, 0)), pl.BlockSpec((8, LANE), lambda i: (0, 0))],
        out_shape=[jax.ShapeDtypeStruct((s, D_IN), BF16), jax.ShapeDtypeStruct((8, D_A), F32),
                   jax.ShapeDtypeStruct((GROUPS, BLK, BLK), F32), jax.ShapeDtypeStruct((BLK, LANE), F32),
                   jax.ShapeDtypeStruct((8, LANE), F32)],
        scratch_shapes=[pltpu.VMEM((N_KV, 2 * BLK, LANE), BF16), pltpu.VMEM((N_KV, 2 * BLK, LANE), BF16),
                        pltpu.VMEM((BLK, D_A), F32), pltpu.VMEM((N_KV, Q_PER_KV * BLK, LANE), BF16),
                        pltpu.VMEM((N_KV, Q_PER_KV * BLK, LANE), F32), pltpu.VMEM((N_KV, Q_PER_KV * BLK, LANE), F32),
                        pltpu.VMEM((N_KV, Q_PER_KV * BLK, LANE), F32), pltpu.VMEM((N_KV, 2 * BLK, LANE), F32),
                        pltpu.VMEM((N_KV, 2 * BLK, LANE), F32), pltpu.VMEM((BLK, D_KV), F32), pltpu.VMEM((BLK, D_KV), F32),
                        pltpu.VMEM((BLK, D_B), F32), pltpu.VMEM((2, Q_PER_KV * BLK, 2 * BLK), F32)],
        compiler_params=_params("arbitrary"),
    )(proj, proj, proj, proj, proj, proj, proj, proj, proj, proj, dy, cos, sin, cos, sin, ln_g, ln_b, w_sp, w_sp_t,
      b_sp_t, sinks)


def _dh_call(dproj, w_bf, x, dx2, scale, norm_g):
    s = x.shape[0]
    tm = min(s, 512)
    tk = W_IN_SHARD
    nk = D_IN // tk

    def body(dp_ref, w_ref, x_ref, dx2_ref, sc_ref, g_ref, gx_ref, st_ref, acc_ref):
        i = pl.program_id(0)
        k = pl.program_id(1)

        @pl.when((i == 0) & (k == 0))
        def _():
            st_ref[...] = jnp.zeros((8, D), F32)

        @pl.when(k == 0)
        def _():
            acc_ref[...] = jnp.zeros((tm, D), F32)

        acc_ref[...] += lax.dot_general(dp_ref[...], w_ref[...], NT, preferred_element_type=F32)

        @pl.when(k == nk - 1)
        def _():
            g = g_ref[...]
            one_sc = 1.0 + sc_ref[...]

            def chunk(n, carry):
                rows = pl.ds(pl.multiple_of(n * BLK, BLK), BLK)
                dh = acc_ref[rows, :]
                xv = x_ref[rows, :]
                r = lax.rsqrt(jnp.mean(xv * xv, axis=-1, keepdims=True) + EPS)
                xn = xv * r
                dhn = dh * one_sc
                dxn = dhn * g
                gx_ref[rows, :] = dx2_ref[rows, :] + r * (dxn - xn * jnp.mean(dxn * xn, axis=-1, keepdims=True))
                st_ref[0:1, :] += jnp.sum(dh, axis=0, keepdims=True)
                st_ref[1:2, :] += jnp.sum(dh * (xn * g), axis=0, keepdims=True)
                st_ref[2:3, :] += jnp.sum(dhn * xn, axis=0, keepdims=True)
                return carry

            lax.fori_loop(0, tm // BLK, chunk, 0)

    vec = pl.BlockSpec((1, D), lambda i, k: (0, 0))
    rows = lambda: pl.BlockSpec((tm, D), lambda i, k: (i, 0))
    return pl.pallas_call(
        body, name="dh", grid=(s // tm, nk),
        in_specs=[pl.BlockSpec((tm, tk), lambda i, k: (i, k)), pl.BlockSpec((D, tk), lambda i, k: (0, k)), rows(), rows(), vec, vec],
        out_specs=[rows(), pl.BlockSpec((8, D), lambda i, k: (0, 0))],
        out_shape=[jax.ShapeDtypeStruct((s, D), F32), jax.ShapeDtypeStruct((8, D), F32)],
        scratch_shapes=[pltpu.VMEM((tm, D), F32)],
        compiler_params=_params("arbitrary", "arbitrary"),
    )(dproj, w_bf, x, dx2, scale, norm_g)


def _adam_math(w, g, m, v):
    m_new = ADAM_B1 * m + (1.0 - ADAM_B1) * g
    v_new = ADAM_B2 * v + (1.0 - ADAM_B2) * (g * g)
    m_hat = m_new / ADAM_C1
    v_hat = v_new / ADAM_C2
    delta = -ADAM_LR * (m_hat / (jnp.sqrt(v_hat) + ADAM_EPS) + ADAM_WD * w)
    return delta, m_new, v_new


def _adam_small_call(tensors):
    n = len(tensors)

    def body(*refs):
        ins, outs = refs[:4 * n], refs[4 * n:]
        for t in range(n):
            w_ref, g_ref, m_ref, v_ref = ins[4 * t:4 * t + 4]
            d, mo, vo = _adam_math(w_ref[...], g_ref[...], m_ref[...], v_ref[...])
            outs[3 * t][...], outs[3 * t + 1][...], outs[3 * t + 2][...] = d, mo, vo

    vm = pl.BlockSpec(memory_space=pltpu.VMEM)
    flat = [a for t in tensors for a in t]
    out = pl.pallas_call(
        body, name="adam_small", in_specs=[vm] * (4 * n), out_specs=[vm] * (3 * n),
        out_shape=[jax.ShapeDtypeStruct(t[0].shape, F32) for t in tensors for _ in range(3)],
        compiler_params=pltpu.CompilerParams(vmem_limit_bytes=VMEM_LIMIT),
    )(*flat)
    return [tuple(out[3 * t:3 * t + 3]) for t in range(n)]


def _adam_halves_call(pos, w, mine, theirs, m, v, name):
    r, n = w.shape
    half = r // 2
    tr = ADAM_ROWS
    nh = half // tr

    def body(pos_ref, w_ref, mine_ref, theirs_ref, m_ref, v_ref, g_ref, d_ref, mo_ref, vo_ref):
        is_mine = (pl.program_id(0) // nh) == pos_ref[1]
        g = jnp.where(is_mine, mine_ref[...], theirs_ref[...])
        g_ref[...] = g
        d_ref[...], mo_ref[...], vo_ref[...] = _adam_math(w_ref[...], g, m_ref[...], v_ref[...])

    spec = lambda: pl.BlockSpec((tr, n), lambda i, pos: (i, 0))
    hspec = lambda: pl.BlockSpec((tr, n), lambda i, pos: (i % nh, 0))
    return pl.pallas_call(
        body, name=name,
        grid_spec=pltpu.PrefetchScalarGridSpec(
            num_scalar_prefetch=1, grid=(r // tr,), in_specs=[spec(), hspec(), hspec(), spec(), spec()],
            out_specs=[spec() for _ in range(4)]),
        out_shape=[jax.ShapeDtypeStruct((r, n), F32)] * 4, compiler_params=_params("parallel"),
    )(pos, w, mine, theirs, m, v)


def _adam_outer_call(w, ct, dm, m, v, name):
    r, n = w.shape
    tr = ADAM_ROWS

    def body(w_ref, ct_ref, dm_ref, m_ref, v_ref, g_ref, d_ref, mo_ref, vo_ref):
        g = ct_ref[:, 0:1] * dm_ref[0:1, :]
        for b in range(1, N_DEV):
            g = g + ct_ref[:, b:b + 1] * dm_ref[b:b + 1, :]
        g_ref[...] = g
        d_ref[...], mo_ref[...], vo_ref[...] = _adam_math(w_ref[...], g, m_ref[...], v_ref[...])

    spec = lambda: pl.BlockSpec((tr, n), lambda i: (i, 0))
    return pl.pallas_call(
        body, name=name, grid=(r // tr,),
        in_specs=[spec(), pl.BlockSpec((tr, N_DEV), lambda i: (i, 0)), pl.BlockSpec((N_DEV, n), lambda i: (0, 0)), spec(), spec()],
        out_specs=[spec() for _ in range(4)],
        out_shape=[jax.ShapeDtypeStruct((r, n), F32)] * 4, compiler_params=_params("parallel"),
    )(w, ct, dm, m, v)


def _sum_pieces_call(pos, part, part_block, recvs, name):
    r, n = recvs[0].shape[1:]
    tr = min(r, 256)
    nrb = r // tr

    def body(pos_ref, p_ref, *refs):
        acc = p_ref[...].astype(F32)
        for r_ref in refs[:-1]:
            for d in range(r_ref.shape[0]):
                acc = acc + r_ref[d].astype(F32)
        refs[-1][...] = acc

    return pl.pallas_call(
        body, name=name,
        grid_spec=pltpu.PrefetchScalarGridSpec(
            num_scalar_prefetch=1, grid=(nrb,),
            in_specs=[pl.BlockSpec((tr, n), lambda i, pos: part_block(i, pos, nrb))] + [
                pl.BlockSpec((rv.shape[0], tr, n), lambda i, pos: (0, i, 0)) for rv in recvs],
            out_specs=pl.BlockSpec((tr, n), lambda i, pos: (i, 0))),
        out_shape=jax.ShapeDtypeStruct((r, n), F32), compiler_params=_params("parallel"),
    )(pos, part, *recvs)


def _coords():
    return lax.axis_index("x"), lax.axis_index("y"), lax.axis_index("c")


def _allgather_sum_call(blk, name, with_sum):
    m_per, n = blk.shape

    def body(x_ref, out_ref, *rest):
        if with_sum:
            sum_ref, send_sems, recv_sems, local_sem = rest
        else:
            send_sems, recv_sems, local_sem = rest
        x, y, c = _coords()
        me, sibling = (x, y, c), (x, y, 1 - c)
        chips = [(1 - x, y), (x, 1 - y), (1 - x, 1 - y)]

        def rows(px, py, pc):
            return out_ref.at[pl.ds((4 * px + 2 * py + pc) * m_per, m_per), :]

        def copy(k, block, to, src=None):
            return pltpu.make_async_remote_copy(
                src_ref=rows(*block) if src is None else src, dst_ref=rows(*block),
                send_sem=send_sems.at[k], recv_sem=recv_sems.at[k], device_id=to, device_id_type=MESH)

        mine = pltpu.make_async_copy(x_ref, rows(*me), local_sem)
        mine.start()
        first = [copy(0, me, sibling, src=x_ref)]
        first += [copy(1 + j, me, (*chip, c), src=x_ref) for j, chip in enumerate(chips)]
        for cp in first:
            cp.start()
        passed = [copy(4 + j, (*chip, c), sibling) for j, chip in enumerate(chips)]
        for j, chip in enumerate(chips):
            copy(1 + j, (*chip, c), me).wait_recv()
            passed[j].start()
        copy(0, sibling, me).wait_recv()
        for j, chip in enumerate(chips):
            copy(4 + j, (*chip, 1 - c), me).wait_recv()
        for cp in first + passed:
            cp.wait_send()
        mine.wait()
        if with_sum:
            acc = out_ref[0:m_per, :]
            for d in range(1, N_DEV):
                acc = acc + out_ref[d * m_per:(d + 1) * m_per, :]
            sum_ref[...] = acc

    vm = pl.BlockSpec(memory_space=pltpu.VMEM)
    out_shape = [jax.ShapeDtypeStruct((N_DEV * m_per, n), F32)]
    if with_sum:
        out_shape.append(jax.ShapeDtypeStruct((m_per, n), F32))
    return pl.pallas_call(
        body, name=name, out_shape=out_shape, in_specs=[vm], out_specs=[vm] * len(out_shape),
        scratch_shapes=[pltpu.SemaphoreType.DMA((7,)), pltpu.SemaphoreType.DMA((7,)), pltpu.SemaphoreType.DMA],
        compiler_params=pltpu.CompilerParams(vmem_limit_bytes=VMEM_LIMIT),
    )(blk)


def _weights_gather_call(wi_full, wo_full):
    hi = D // 2
    ho = W_OUT_SHARD // 2

    def body(wi_in, wo_in, fi_ref, fo_ref, send_sems, recv_sems):
        del wi_in, wo_in
        x, y, c = _coords()
        sibling = (x, y, 1 - c)
        chips = [(1 - x, y), (x, 1 - y), (1 - x, 1 - y)]

        def half(which, px, py, pc):
            j = 2 * px + py
            if which == 0:
                return fi_ref.at[pl.ds(pc * hi, hi), pl.ds(j * W_IN_SHARD, W_IN_SHARD)]
            return fo_ref.at[pl.ds(j * W_OUT_SHARD + pc * ho, ho), :]

        def copy(k, which, block, to):
            return pltpu.make_async_remote_copy(
                src_ref=half(which, *block), dst_ref=half(which, *block), send_sem=send_sems.at[k],
                recv_sem=recv_sems.at[k], device_id=to, device_id_type=MESH)

        first = [copy(6 * w + j, w, (x, y, c), (*chip, c)) for w in range(2) for j, chip in enumerate(chips)]
        for cp in first:
            cp.start()
        passed = []
        for w in range(2):
            for j, chip in enumerate(chips):
                copy(6 * w + j, w, (*chip, c), (x, y, c)).wait_recv()
                cp = copy(6 * w + 3 + j, w, (*chip, c), sibling)
                cp.start()
                passed.append(cp)
        for w in range(2):
            for j, chip in enumerate(chips):
                copy(6 * w + 3 + j, w, (*chip, 1 - c), (x, y, c)).wait_recv()
        for cp in first + passed:
            cp.wait_send()

    anyspec = pl.BlockSpec(memory_space=pl.ANY)
    return pl.pallas_call(
        body, name="weights_gather",
        out_shape=[jax.ShapeDtypeStruct((D, D_IN), BF16), jax.ShapeDtypeStruct((D, D), BF16)],
        in_specs=[anyspec, anyspec], out_specs=[anyspec, anyspec], input_output_aliases={0: 0, 1: 1},
        scratch_shapes=[pltpu.SemaphoreType.DMA((12,)), pltpu.SemaphoreType.DMA((12,))],
    )(wi_full, wo_full)


HBM_SPEC = pl.BlockSpec(memory_space=pltpu.HBM)
SEM_SPEC = pl.BlockSpec(memory_space=pltpu.SEMAPHORE)
SIDE_EFFECT = pltpu.SideEffectType.DATAFLOW_SIDE_EFFECTING


def _peer(x, y, c, q, cb):
    return (1 - x if q & 2 else x, 1 - y if q & 1 else y, 1 - c if cb else c)


def _w_in_piece(slots):
    def piece(part_ref, k, to):
        return part_ref.at[pl.ds(to[2] * (D // 2), D // 2), pl.ds(slots[k] * W_IN_SHARD, W_IN_SHARD)]
    return piece


def _w_out_piece(part_ref, k, to):
    ho = W_OUT_SHARD // 2
    return part_ref.at[pl.ds((2 * to[0] + to[1]) * W_OUT_SHARD + to[2] * ho, ho), :]


def _group_piece(part_ref, k, to):
    return part_ref.at[4 * to[0] + 2 * to[1] + to[2]]


def _whole_piece(part_ref, k, to):
    return part_ref


def _exchange_start_call(part, rels, piece, slot_shape, name):
    n = len(rels)
    land = lax.empty((n,) + slot_shape, part.dtype)

    def body(part_ref, land_ref, send_sems, recv_sems, part_thru, land_thru, token):
        x, y, c = _coords()
        for k, (q, cb) in enumerate(rels):
            to = _peer(x, y, c, q, cb)
            pltpu.make_async_remote_copy(src_ref=piece(part_ref, k, to), dst_ref=land_ref.at[k], send_sem=send_sems.at[k],
                                         recv_sem=recv_sems.at[k], device_id=to, device_id_type=MESH).start()
        token[...] = jnp.zeros_like(token)

    return pl.pallas_call(
        body, name=name,
        out_shape=(pltpu.SemaphoreType.DMA((n,)), pltpu.SemaphoreType.DMA((n,)), pltpu.HBM(part.shape, part.dtype),
                   pltpu.HBM(land.shape, land.dtype), jax.ShapeDtypeStruct((8, LANE), F32)),
        in_specs=(HBM_SPEC, HBM_SPEC), out_specs=(SEM_SPEC, SEM_SPEC, HBM_SPEC, HBM_SPEC, pl.BlockSpec(memory_space=pltpu.VMEM)),
        input_output_aliases={0: 2, 1: 3},
        compiler_params=pltpu.CompilerParams(has_side_effects=SIDE_EFFECT),
    )(pltpu.with_memory_space_constraint(part, pltpu.HBM), pltpu.with_memory_space_constraint(land, pltpu.HBM))


def _exchange_wait_call(started, rels, piece, after, name):
    send_sems, recv_sems, part_thru, land_thru, _ = started

    def body(part_ref, land_ref, send_sems, recv_sems, after_ref, part_out, land_out):
        x, y, c = _coords()
        for k, (q, cb) in enumerate(rels):
            to = _peer(x, y, c, q, cb)
            cp = pltpu.make_async_remote_copy(src_ref=piece(part_ref, k, to), dst_ref=land_ref.at[k], send_sem=send_sems.at[k],
                                              recv_sem=recv_sems.at[k], device_id=to, device_id_type=MESH)
            cp.wait_send()
            cp.wait_recv()

    return pl.pallas_call(
        body, name=name,
        out_shape=(pltpu.HBM(part_thru.shape, part_thru.dtype), pltpu.HBM(land_thru.shape, land_thru.dtype)),
        in_specs=(HBM_SPEC, HBM_SPEC, SEM_SPEC, SEM_SPEC, pl.BlockSpec(memory_space=pl.ANY)), out_specs=(HBM_SPEC, HBM_SPEC),
        input_output_aliases={0: 0, 1: 1},
        compiler_params=pltpu.CompilerParams(has_side_effects=SIDE_EFFECT),
    )(part_thru, land_thru, send_sems, recv_sems, after)


def _pair_exchange_call(gi, go):
    hi = D // 2
    ho = W_OUT_SHARD // 2

    def body(gi_in, go_in, fi_ref, fo_ref, send_sems, recv_sems):
        del gi_in, go_in
        x, y, c = _coords()
        sibling = (x, y, 1 - c)
        mine = (fi_ref.at[pl.ds(c * hi, hi), :], fo_ref.at[pl.ds(c * ho, ho), :])
        theirs = (fi_ref.at[pl.ds((1 - c) * hi, hi), :], fo_ref.at[pl.ds((1 - c) * ho, ho), :])
        sends = [pltpu.make_async_remote_copy(src_ref=ref, dst_ref=ref, send_sem=send_sems.at[k], recv_sem=recv_sems.at[k],
                                              device_id=sibling, device_id_type=MESH) for k, ref in enumerate(mine)]
        for cp in sends:
            cp.start()
        for k, ref in enumerate(theirs):
            pltpu.make_async_remote_copy(src_ref=ref, dst_ref=ref, send_sem=send_sems.at[k], recv_sem=recv_sems.at[k],
                                         device_id=sibling, device_id_type=MESH).wait_recv()
        for cp in sends:
            cp.wait_send()

    anyspec = pl.BlockSpec(memory_space=pl.ANY)
    return pl.pallas_call(
        body, name="pair_exchange",
        out_shape=[jax.ShapeDtypeStruct((D, W_IN_SHARD), F32), jax.ShapeDtypeStruct((W_OUT_SHARD, D), F32)],
        in_specs=[anyspec, anyspec], out_specs=[anyspec, anyspec], input_output_aliases={0: 0, 1: 1},
        scratch_shapes=[pltpu.SemaphoreType.DMA((2,)), pltpu.SemaphoreType.DMA((2,))],
    )(gi, go)


def _rope_tables(s):
    inv_freq = 10000.0 ** (-jnp.arange(0, HEAD, 2, dtype=F32) / HEAD)
    ang = jnp.arange(s, dtype=F32)[:, None] * inv_freq[None, :]
    cos = jnp.tile(jnp.cos(ang), (1, LANE // (HEAD // 2)))
    sin = jnp.tile(jnp.sin(ang), (1, LANE // (HEAD // 2)))
    first_half = (jnp.arange(LANE) % HEAD) < (HEAD // 2)
    return cos, jnp.where(first_half[None, :], -sin, sin)


def _pad_cols(a, n):
    return jnp.pad(a, ((0, 0), (0, n - a.shape[1])))


def kernel(x, c, w_ada, b_ada, norm_g, w_in, ln_v_g, ln_v_b, w_spatial, b_spatial, sinks, w_out, w_ada_final, b_ada_final, final_norm_g, loss_target, m_w_ada, m_b_ada, m_norm_g, m_w_in, m_ln_v_g, m_ln_v_b, m_w_spatial, m_b_spatial, m_sinks, m_w_out, m_w_ada_final, m_b_ada_final, m_final_norm_g, v_w_ada, v_b_ada, v_norm_g, v_w_in, v_ln_v_g, v_ln_v_b, v_w_spatial, v_b_spatial, v_sinks, v_w_out, v_w_ada_final, v_b_ada_final, v_final_norm_g):
    s = x.shape[1]
    ax, ay, ac = _coords()
    chip = 2 * ax + ay
    me = 4 * ax + 2 * ay + ac
    n_ada = w_ada.shape[2]
    n_adaf = w_ada_final.shape[1]

    x2d = x.reshape(s, D)
    tgt = loss_target.reshape(s, D)
    w_ada2, w_in2, w_out2 = w_ada[0], w_in[0], w_out[0]
    b_ada_f2 = b_ada_final.reshape(1, 2 * D)
    gf = final_norm_g.reshape(1, D)

    c_all = _allgather_sum_call(jnp.pad(c, ((0, 7), (0, 0))), "gather_c", False)[0][::8]
    mod_p, c_act = _rowmat_call(c_all, w_ada2, lax.dynamic_slice(b_ada, (0, chip * n_ada), (1, n_ada)), "mod")
    modf_p, _ = _rowmat_call(c_all, w_ada_final, lax.dynamic_slice(b_ada_f2, (0, chip * n_adaf), (1, n_adaf)), "mod_final")
    mods = _allgather_sum_call(jnp.concatenate([mod_p, modf_p], axis=1), "gather_mod", False)[0]
    my_rows = [lax.dynamic_slice(mods, (16 * j + me, 0), (1, n_ada + n_adaf)) for j in range(N_CHIP)]
    mod = jnp.concatenate([r[:, :n_ada] for r in my_rows], axis=1)
    mod_f = jnp.concatenate([r[:, n_ada:] for r in my_rows], axis=1)
    shift, scale, gate = mod[:, :D], mod[:, D:2 * D], mod[:, 2 * D:]
    shift_f, scale_f = mod_f[:, :D], mod_f[:, D:]

    pos = jnp.stack([chip, ac]).astype(jnp.int32)
    w_in_own = _cast_into_call(pos, w_in2, (D, D_IN), "cast_w_in")
    w_out_own = _cast_into_call(pos, w_out2, (D, D), "cast_w_out")

    cos, sin = _rope_tables(s)
    b_sp_t = b_spatial[0].T
    sinks1 = sinks.reshape(N_Q)
    h, proj, w_in_bf, w_out_bf = _proj_gather_call(pos, x2d, shift, scale, norm_g, w_in_own, w_out_own)
    y = _mix_fwd_call(proj, cos, sin, ln_v_g, ln_v_b, w_spatial[0], b_sp_t, sinks1)
    dx2, do, dy, st_tail = _tail_call(y, w_out_bf, x2d, tgt, gate, shift_f, scale_f, gf)

    rel_o = [(0, 1), (1, 0), (1, 1), (2, 0), (2, 1), (3, 0), (3, 1)]
    rel_a = [(1, 0), (1, 1), (2, 0), (2, 1)]
    rel_b = [(3, 0), (3, 1), (0, 1)]
    piece_a, piece_b = _w_in_piece([0, 0, 1, 1]), _w_in_piece([0, 0, 1])
    half_in, half_out = (D // 2, W_IN_SHARD), (W_OUT_SHARD // 2, D)

    g_w_out_p = _tn_call(y, do, "grad_w_out")
    st_o = _exchange_start_call(g_w_out_p, rel_o, _w_out_piece, half_out, "send_w_out")
    dproj, st_ln, d_wsp, d_bsp_t, d_sink = _mix_bwd_call(
        proj, dy, cos, sin, ln_v_g + st_o[4][0:1, 0:1], ln_v_b, w_spatial[0], jnp.swapaxes(w_spatial[0], 1, 2), b_sp_t, sinks1)
    g_w_in_a = _tn_shards_call(pos, h, dproj, (1, 2), "grad_w_in_a")
    st_a = _exchange_start_call(g_w_in_a, rel_a, piece_a, half_in, "send_w_in_a")
    g_w_in_b = _tn_shards_call(pos, h, dproj, (3, 0), "grad_w_in_b")
    st_b = _exchange_start_call(g_w_in_b, rel_b, piece_b, half_in, "send_w_in_b")
    rel_all = rel_o
    st_s = _exchange_start_call(d_wsp, rel_all, _group_piece, (BLK, BLK), "send_w_spatial")
    sent = st_a[4][0:1, 0:1] + st_b[4][0:1, 0:1] + st_s[4][0:1, 0:1]
    grad_x, st_dh = _dh_call(dproj, w_in_bf, x2d, dx2, scale + sent, norm_g)

    g_w_out_p, recv_o = _exchange_wait_call(st_o, rel_o, _w_out_piece, st_dh, "wait_w_out")
    _, recv_a = _exchange_wait_call(st_a, rel_a, piece_a, st_dh, "wait_w_in_a")
    g_w_in_b, recv_b = _exchange_wait_call(st_b, rel_b, piece_b, st_dh, "wait_w_in_b")
    d_wsp, recv_s = _exchange_wait_call(st_s, rel_all, _group_piece, st_dh, "wait_w_spatial")
    mine_in = _sum_pieces_call(pos, g_w_in_b, lambda i, p, nrb: (p[1] * nrb + i, 1), [recv_a, recv_b], "sum_w_in")
    mine_out = _sum_pieces_call(pos, g_w_out_p, lambda i, p, nrb: ((2 * p[0] + p[1]) * nrb + i, 0), [recv_o], "sum_w_out")
    wsp_group = _sum_pieces_call(pos, d_wsp.reshape(GROUPS * BLK, BLK), lambda i, p, nrb: (2 * p[0] + p[1], 0), [recv_s],
                                 "sum_w_spatial")
    to_sibling = [(0, 1)]
    st_pi = _exchange_start_call(mine_in, to_sibling, _whole_piece, half_in, "swap_w_in")
    st_po = _exchange_start_call(mine_out, to_sibling, _whole_piece, half_out, "swap_w_out")

    misc = jnp.concatenate([st_ln, d_bsp_t[:, :GROUPS].T, d_sink, jnp.zeros((8, D - D_A - 2 * LANE), F32)], axis=1)
    pack = jnp.concatenate([wsp_group.reshape(8, D) + (st_pi[4][0:1, 0:1] + st_po[4][0:1, 0:1]), st_tail, st_dh, misc], axis=0)
    rows = pack.shape[0]
    packs, tot = _allgather_sum_call(pack, "gather_small", True)
    packs = packs.reshape(N_DEV, rows, D)
    dmod_all = jnp.concatenate([packs[:, 16, :], packs[:, 17, :], packs[:, 11, :]], axis=1)
    dmodf_all = jnp.concatenate([packs[:, 8, :], packs[:, 9, :]], axis=1)
    loss = tot[13, 0]
    mine_in, theirs_in = _exchange_wait_call(st_pi, to_sibling, _whole_piece, tot, "swapped_w_in")
    mine_out, theirs_out = _exchange_wait_call(st_po, to_sibling, _whole_piece, tot, "swapped_w_out")
    small = {
        "b_ada": jnp.concatenate([tot[16:17], tot[17:18], tot[11:12]], axis=1),
        "norm_g": tot[18:19],
        "ln_v_g": tot[24:25, :D_A],
        "ln_v_b": tot[25:26, :D_A],
        "w_spatial": packs[:, 0:8, :].reshape(GROUPS * BLK, BLK),
        "b_spatial": tot[24:32, D_A:D_A + BLK],
        "sinks": tot[24:25, D_A + LANE:D_A + LANE + N_Q],
        "b_ada_final": jnp.concatenate([tot[8:9], tot[9:10]], axis=1),
        "final_norm_g": tot[10:11],
    }

    weights = dict(w_ada=w_ada, b_ada=b_ada, norm_g=norm_g, w_in=w_in, ln_v_g=ln_v_g, ln_v_b=ln_v_b, w_spatial=w_spatial,
                   b_spatial=b_spatial, sinks=sinks, w_out=w_out, w_ada_final=w_ada_final, b_ada_final=b_ada_final,
                   final_norm_g=final_norm_g)
    m_in = dict(w_ada=m_w_ada, b_ada=m_b_ada, norm_g=m_norm_g, w_in=m_w_in, ln_v_g=m_ln_v_g, ln_v_b=m_ln_v_b,
                w_spatial=m_w_spatial, b_spatial=m_b_spatial, sinks=m_sinks, w_out=m_w_out, w_ada_final=m_w_ada_final,
                b_ada_final=m_b_ada_final, final_norm_g=m_final_norm_g)
    v_in = dict(w_ada=v_w_ada, b_ada=v_b_ada, norm_g=v_norm_g, w_in=v_w_in, ln_v_g=v_ln_v_g, ln_v_b=v_ln_v_b,
                w_spatial=v_w_spatial, b_spatial=v_b_spatial, sinks=v_sinks, w_out=v_w_out, w_ada_final=v_w_ada_final,
                b_ada_final=v_b_ada_final, final_norm_g=v_final_norm_g)
    c_act_t = c_act.T
    outer = {"w_ada": lax.dynamic_slice(dmod_all, (0, chip * n_ada), (N_DEV, n_ada)),
             "w_ada_final": lax.dynamic_slice(dmodf_all, (0, chip * n_adaf), (N_DEV, n_adaf))}
    halves = {"w_in": (mine_in, theirs_in[0]), "w_out": (mine_out, theirs_out[0])}
    done = {}
    for name, (mine, theirs) in halves.items():
        shape2 = (2 * mine.shape[0], mine.shape[1])
        done[name] = _adam_halves_call(pos, weights[name].reshape(shape2), mine, theirs, m_in[name].reshape(shape2),
                                       v_in[name].reshape(shape2), "adam_" + name)
    for name, dm in outer.items():
        shape2 = (D, dm.shape[1])
        done[name] = _adam_outer_call(weights[name].reshape(shape2), c_act_t, dm, m_in[name].reshape(shape2),
                                      v_in[name].reshape(shape2), "adam_" + name)
    updates = _adam_small_call([(weights[name].reshape(g.shape), g, m_in[name].reshape(g.shape), v_in[name].reshape(g.shape))
                                for name, g in small.items()])
    for (name, g), upd in zip(small.items(), updates):
        done[name] = (g, *upd)
    outs = [[done[name][k].reshape(w.shape) for name, w in weights.items()] for k in range(4)]
    return (loss, grad_x.reshape(x.shape), *outs[0], *outs[1], *outs[2], *outs[3])
```

```python
import jax
import jax.numpy as jnp
from jax import lax
from jax.experimental import pallas as pl
from jax.experimental.pallas import tpu as pltpu

F32 = jnp.float32
BF16 = jnp.bfloat16
MESH = pl.DeviceIdType.MESH

D = 2048
D_A = 1024
D_B = 1024
D_KV = 256
HEAD = 64
N_Q = 16
N_KV = 4
Q_PER_KV = N_Q // N_KV
BLK = 128
GROUPS = 8
D_IN = 5632
OFF_Q, OFF_K, OFF_V, OFF_ZB = 3072, 4096, 4352, 4608
N_CHIP = 4
N_DEV = 8
W_IN_SHARD = D_IN // N_CHIP
W_OUT_SHARD = D // N_CHIP
EPS = 1e-5
SCALE = HEAD ** -0.5
NEG = -1e30
LANE = 128
VMEM_LIMIT = 56 * 1024 * 1024

ADAM_LR, ADAM_B1, ADAM_B2, ADAM_EPS, ADAM_WD, ADAM_STEP = 0.001, 0.9, 0.999, 1e-08, 0.01, 10
ADAM_C1 = 1.0 - ADAM_B1 ** ADAM_STEP
ADAM_C2 = 1.0 - ADAM_B2 ** ADAM_STEP
ADAM_ROWS = 256

NT = (((1,), (1,)), ((), ()))
TN = (((0,), (0,)), ((), ()))


def _params(*sem):
    return pltpu.CompilerParams(dimension_semantics=sem, vmem_limit_bytes=VMEM_LIMIT)


def _silu_parts(z):
    sig = 1.0 / (1.0 + jnp.exp(-z))
    return z * sig, sig


def _swap_halves(v, first_half):
    return jnp.where(first_half, pltpu.roll(v, 96, 1), pltpu.roll(v, 32, 1))


def _rope(v, cos_t, sin_s, first_half):
    return v * cos_t + _swap_halves(v, first_half) * sin_s


def _unrope(dv, cos_t, sin_s, first_half):
    return dv * cos_t - _swap_halves(dv, first_half) * sin_s


def _lane_masks():
    lane = lax.broadcasted_iota(jnp.int32, (BLK, LANE), 1)
    return (lane % HEAD) < (HEAD // 2), lane < HEAD


def _band_valid(first_block_bound, rows=BLK):
    rr = lax.broadcasted_iota(jnp.int32, (rows, 2 * BLK), 0) & (BLK - 1)
    jj = lax.broadcasted_iota(jnp.int32, (rows, 2 * BLK), 1)
    return (jj > rr) & (jj <= rr + BLK) & (jj >= first_block_bound)


def _dup_kv(slab, lo):
    rolled = pltpu.roll(slab, HEAD, 1)
    return jnp.where(lo, slab, rolled).astype(BF16), jnp.where(lo, rolled, slab).astype(BF16)


def _stack_heads(ref, sb, slab, lo, dtype):
    kh, base = sb // 2, 2 * (sb % 2) * BLK
    zero = jnp.zeros_like(slab)
    ref[kh, base:base + BLK, :] = jnp.where(lo, slab, zero).astype(dtype)
    ref[kh, base + BLK:base + 2 * BLK, :] = jnp.where(lo, zero, slab).astype(dtype)


def _unstack_heads(ref, sb, lo):
    kh, base = sb // 2, 2 * (sb % 2) * BLK
    return jnp.where(lo, ref[kh, base:base + BLK, :], ref[kh, base + BLK:base + 2 * BLK, :])


def _sink_column(sinks_ref, kh):
    row = lax.broadcasted_iota(jnp.int32, (Q_PER_KV * BLK, 1), 0)
    col = jnp.full(row.shape, sinks_ref[Q_PER_KV * kh + Q_PER_KV - 1], F32)
    for n in range(Q_PER_KV - 2, -1, -1):
        col = jnp.where(row < (n + 1) * BLK, sinks_ref[Q_PER_KV * kh + n], col)
    return col


def _tril():
    t = lax.broadcasted_iota(jnp.int32, (BLK, BLK), 0)
    s = lax.broadcasted_iota(jnp.int32, (BLK, BLK), 1)
    return s <= t


def _layer_norm_fwd(va, lg, lb):
    mu = jnp.mean(va, axis=-1, keepdims=True)
    xc = va - mu
    rstd = lax.rsqrt(jnp.mean(xc * xc, axis=-1, keepdims=True) + EPS)
    vhat = xc * rstd
    return vhat, rstd, vhat * lg + lb


def _softmax_sink(qm, kdup, bias, sink):
    s = lax.dot_general(qm, kdup, NT, preferred_element_type=F32) + bias
    m = jnp.maximum(jnp.max(s, axis=-1, keepdims=True), sink)
    p = jnp.exp(s - m)
    esink = jnp.exp(sink - m)
    inv = 1.0 / (jnp.sum(p, axis=-1, keepdims=True) + esink)
    return p * inv, esink * inv


def _band_bias(bias_ref):
    rows = bias_ref.shape[1]
    bias_ref[0] = jnp.where(_band_valid(BLK, rows), 0.0, NEG)
    bias_ref[1] = jnp.where(_band_valid(0, rows), 0.0, NEG)


def _rowmat_call(c_all, w, b, name):
    n = w.shape[1]
    tn = 512

    def body(c_ref, w_ref, b_ref, o_ref, ca_ref):
        ca, _ = _silu_parts(c_ref[...])
        ca_ref[...] = ca
        o_ref[...] = jnp.dot(ca.astype(BF16), w_ref[...].astype(BF16), preferred_element_type=F32) + b_ref[...]

    return pl.pallas_call(
        body, name=name, grid=(n // tn,),
        in_specs=[pl.BlockSpec((N_DEV, D), lambda j: (0, 0)), pl.BlockSpec((D, tn), lambda j: (0, j)),
                  pl.BlockSpec((1, tn), lambda j: (0, j))],
        out_specs=[pl.BlockSpec((N_DEV, tn), lambda j: (0, j)), pl.BlockSpec((N_DEV, D), lambda j: (0, 0))],
        out_shape=[jax.ShapeDtypeStruct((N_DEV, n), F32), jax.ShapeDtypeStruct((N_DEV, D), F32)],
        compiler_params=_params("arbitrary"),
    )(c_all, w, b)


def _cast_into_call(pos, w, full_shape, name):
    r, n = w.shape
    tr = min(r, 512)
    by_cols = full_shape[0] == r
    nrb = r // tr

    def body(pos_ref, w_ref, o_ref):
        o_ref[...] = w_ref[...].astype(BF16)

    out_map = (lambda i, pos: (i, pos[0])) if by_cols else (lambda i, pos: (pos[0] * nrb + i, 0))
    return pl.pallas_call(
        body, name=name,
        grid_spec=pltpu.PrefetchScalarGridSpec(
            num_scalar_prefetch=1, grid=(nrb,),
            in_specs=[pl.BlockSpec((tr, n), lambda i, pos: (i, 0))], out_specs=pl.BlockSpec((tr, n), out_map)),
        out_shape=jax.ShapeDtypeStruct(full_shape, BF16), compiler_params=_params("parallel"),
    )(pos, w)


def _proj_call(x, shift, scale, norm_g, w_bf):
    s = x.shape[0]
    tm = min(s, 1024)
    tn = 512

    def body(x_ref, sh_ref, sc_ref, g_ref, w_ref, proj_ref, h_ref):
        @pl.when(pl.program_id(1) == 0)
        def _():
            xv = x_ref[...]
            r = lax.rsqrt(jnp.mean(xv * xv, axis=-1, keepdims=True) + EPS)
            h_ref[...] = ((xv * r * g_ref[...]) * (1.0 + sc_ref[...]) + sh_ref[...]).astype(BF16)

        proj_ref[...] = jnp.dot(h_ref[...], w_ref[...], preferred_element_type=F32)

    vec = pl.BlockSpec((1, D), lambda i, j: (0, 0))
    return pl.pallas_call(
        body, name="proj", grid=(s // tm, D_IN // tn),
        in_specs=[pl.BlockSpec((tm, D), lambda i, j: (i, 0)), vec, vec, vec, pl.BlockSpec((D, tn), lambda i, j: (0, j))],
        out_specs=[pl.BlockSpec((tm, tn), lambda i, j: (i, j)), pl.BlockSpec((tm, D), lambda i, j: (i, 0))],
        out_shape=[jax.ShapeDtypeStruct((s, D_IN), F32), jax.ShapeDtypeStruct((s, D), BF16)],
        compiler_params=_params("parallel", "arbitrary"),
    )(x, shift, scale, norm_g, w_bf)


def _proj_gather_whole_shards_call(pos, x, shift, scale, norm_g, wi_full, wo_full):
    s = x.shape[0]
    tm = min(s, 512)
    nrow = s // tm
    hi = D // 2
    ho = W_OUT_SHARD // 2

    def body(pos_ref, x_ref, sh_ref, sc_ref, g_ref, wi_in, wo_in, proj_ref, h_ref, fi_ref, fo_ref,
             h_all, wbuf, send_sems, recv_sems, load_sem):
        del wi_in, wo_in
        p = pl.program_id(0)
        i = pl.program_id(1)
        x_, y_, c_ = _coords()
        me, sibling = (x_, y_, c_), (x_, y_, 1 - c_)

        def shard_of(q):
            px, py, _ = _peer(x_, y_, c_, q, 0)
            return 2 * px + py

        def part(which, q, pc, sub=None):
            n = hi if which == 0 else ho
            base = pc * n
            if sub is not None:
                n //= 2
                base = base + sub * n
            if which == 0:
                return fi_ref.at[pl.ds(base, n), pl.ds(shard_of(q) * W_IN_SHARD, W_IN_SHARD)]
            return fo_ref.at[pl.ds(shard_of(q) * W_OUT_SHARD + base, n), :]

        def copy(k, which, q, pc, to, sub=None):
            ref = part(which, q, pc, sub)
            return pltpu.make_async_remote_copy(src_ref=ref, dst_ref=ref, send_sem=send_sems.at[k], recv_sem=recv_sems.at[k],
                                                device_id=to, device_id_type=MESH)

        def to_neighbour(which, q):
            return copy(8 * which + q - 1, which, 0, c_, _peer(x_, y_, c_, q, 0))

        def from_neighbour(which, q):
            return copy(8 * which + q - 1, which, q, c_, me)

        def relay(which, q):
            return copy(8 * which + 2 + q - 1, which, q, c_, _peer(x_, y_, c_, 3 - q, 0), q - 1)

        def relayed(which, sub):
            return copy(8 * which + 2 + sub, which, 3, c_, me, sub)

        def to_sibling(which, q):
            return copy(8 * which + 4 + q - 1, which, q, c_, sibling)

        def from_sibling(which, q):
            return copy(8 * which + 4 + q - 1, which, q, 1 - c_, me)

        def relayed_to_sibling(which, sub):
            return copy(8 * which + 6 + sub, which, 3, c_, sibling, sub)

        def relayed_from_sibling(which, sub):
            return copy(8 * which + 6 + sub, which, 3, 1 - c_, me, sub)

        def pass_on_neighbours(which):
            for q in (1, 2):
                from_neighbour(which, q).wait_recv()
                to_sibling(which, q).start()
                relay(which, q).start()

        def pass_on_relayed(which):
            for sub in range(2):
                relayed(which, sub).wait_recv()
                relayed_to_sibling(which, sub).start()

        def load_shard(q):
            cp = pltpu.make_async_copy(fi_ref.at[:, pl.ds(shard_of(q) * W_IN_SHARD, W_IN_SHARD)], wbuf, load_sem)
            cp.start()
            cp.wait()

        @pl.when((p == 0) & (i == 0))
        def _():
            for q in (1, 2):
                to_neighbour(0, q).start()
            load_shard(0)

        @pl.when((p == 1) & (i == 0))
        def _():
            pass_on_neighbours(0)
            for q in (1, 2):
                to_neighbour(1, q).start()
            from_sibling(0, 1).wait_recv()
            load_shard(1)

        @pl.when((p == 2) & (i == 0))
        def _():
            from_sibling(0, 2).wait_recv()
            load_shard(2)

        @pl.when((p == 3) & (i == 0))
        def _():
            pass_on_relayed(0)
            pass_on_neighbours(1)
            for sub in range(2):
                relayed_from_sibling(0, sub).wait_recv()
            load_shard(3)

        rows = pl.ds(pl.multiple_of(i * tm, tm), tm)

        @pl.when(p == 0)
        def _():
            xv = x_ref[...]
            r = lax.rsqrt(jnp.mean(xv * xv, axis=-1, keepdims=True) + EPS)
            hv = ((xv * r * g_ref[...]) * (1.0 + sc_ref[...]) + sh_ref[...]).astype(BF16)
            h_ref[...] = hv
            h_all[rows, :] = hv

        proj_ref[...] = jnp.dot(h_all[rows, :], wbuf[...], preferred_element_type=F32)

        @pl.when((p == N_CHIP - 1) & (i == nrow - 1))
        def _():
            pass_on_relayed(1)
            for q in (1, 2):
                from_sibling(1, q).wait_recv()
            for sub in range(2):
                relayed_from_sibling(1, sub).wait_recv()
            for which in range(2):
                for q in (1, 2):
                    to_neighbour(which, q).wait_send()
                    relay(which, q).wait_send()
                    to_sibling(which, q).wait_send()
                    relayed_to_sibling(which, q - 1).wait_send()

    vec = pl.BlockSpec((1, D), lambda p, i, pos: (0, 0))
    first_phase_rows = lambda p, i, pos: (jnp.where(p == 0, i, nrow - 1), 0)
    anyspec = pl.BlockSpec(memory_space=pl.ANY)
    return pl.pallas_call(
        body, name="proj_gather",
        grid_spec=pltpu.PrefetchScalarGridSpec(
            num_scalar_prefetch=1, grid=(N_CHIP, nrow),
            in_specs=[pl.BlockSpec((tm, D), first_phase_rows), vec, vec, vec, anyspec, anyspec],
            out_specs=[pl.BlockSpec((tm, W_IN_SHARD), lambda p, i, pos: (i, jnp.bitwise_xor(pos[0], p))),
                       pl.BlockSpec((tm, D), first_phase_rows), anyspec, anyspec],
            scratch_shapes=[pltpu.VMEM((s, D), BF16), pltpu.VMEM((D, W_IN_SHARD), BF16),
                            pltpu.SemaphoreType.DMA((16,)), pltpu.SemaphoreType.DMA((16,)), pltpu.SemaphoreType.DMA]),
        out_shape=[jax.ShapeDtypeStruct((s, D_IN), F32), jax.ShapeDtypeStruct((s, D), BF16),
                   jax.ShapeDtypeStruct((D, D_IN), BF16), jax.ShapeDtypeStruct((D, D), BF16)],
        input_output_aliases={5: 2, 6: 3},
        compiler_params=_params("arbitrary", "arbitrary"),
    )(pos, x, shift, scale, norm_g, wi_full, wo_full)


W_IN_PARTS = ((0, 768), (768, 640))


def _proj_gather_call(pos, x, shift, scale, norm_g, wi_full, wo_full):
    s = x.shape[0]
    tm = min(s, 512)
    nrow = s // tm
    hi = D // 2
    ho = W_OUT_SHARD // 2
    phases = [(0, None), (1, 0), (2, 0), (1, 1), (2, 1), (3, 0), (3, 1)]

    def body(pos_ref, x_ref, sh_ref, sc_ref, g_ref, wi_in, wo_in, h_ref, proj_ref, fi_ref, fo_ref,
             h_all, wbuf, obuf, send_sems, recv_sems, load_sems, out_sems):
        del wi_in, wo_in
        p = pl.program_id(0)
        i = pl.program_id(1)
        x_, y_, c_ = _coords()
        me, sibling = (x_, y_, c_), (x_, y_, 1 - c_)

        def shard_of(q):
            px, py, _ = _peer(x_, y_, c_, q, 0)
            return 2 * px + py

        def cols_of(q, cp):
            off, w = (0, W_IN_SHARD) if cp is None else W_IN_PARTS[cp]
            return shard_of(q) * W_IN_SHARD + off, w

        def part(which, q, pc, sub, cp):
            n = hi if which == 0 else ho
            base = pc * n
            if sub is not None:
                n //= 2
                base = base + sub * n
            if which == 0:
                c0, w = cols_of(q, cp)
                return fi_ref.at[pl.ds(base, n), pl.ds(c0, w)]
            return fo_ref.at[pl.ds(shard_of(q) * W_OUT_SHARD + base, n), :]

        def copy(k, ref, to):
            return pltpu.make_async_remote_copy(src_ref=ref, dst_ref=ref, send_sem=send_sems.at[k], recv_sem=recv_sems.at[k],
                                                device_id=to, device_id_type=MESH)

        def sem(which, kind, j, cp):
            return 4 * kind + 2 * cp + j if which == 0 else 16 + 2 * kind + j

        def to_neighbour(which, q, cp=None):
            return copy(sem(which, 0, q - 1, cp), part(which, 0, c_, None, cp), _peer(x_, y_, c_, q, 0))

        def from_neighbour(which, q, cp=None):
            return copy(sem(which, 0, q - 1, cp), part(which, q, c_, None, cp), me)

        def relay(which, q, cp=None):
            return copy(sem(which, 1, q - 1, cp), part(which, q, c_, q - 1, cp), _peer(x_, y_, c_, 3 - q, 0))

        def relayed(which, sub, cp=None):
            return copy(sem(which, 1, sub, cp), part(which, 3, c_, sub, cp), me)

        def to_sibling(which, q, cp=None):
            return copy(sem(which, 2, q - 1, cp), part(which, q, c_, None, cp), sibling)

        def from_sibling(which, q, cp=None):
            return copy(sem(which, 2, q - 1, cp), part(which, q, 1 - c_, None, cp), me)

        def relayed_to_sibling(which, sub, cp=None):
            return copy(sem(which, 3, sub, cp), part(which, 3, c_, sub, cp), sibling)

        def relayed_from_sibling(which, sub, cp=None):
            return copy(sem(which, 3, sub, cp), part(which, 3, 1 - c_, sub, cp), me)

        def pass_on_neighbours(which, cp=None):
            for q in (1, 2):
                from_neighbour(which, q, cp).wait_recv()
                to_sibling(which, q, cp).start()
                relay(which, q, cp).start()

        def pass_on_relayed(which, cp=None):
            for sub in range(2):
                relayed(which, sub, cp).wait_recv()
                relayed_to_sibling(which, sub, cp).start()

        def shard_load(k):
            c0, w = cols_of(*phases[k])
            return pltpu.make_async_copy(fi_ref.at[:, pl.ds(c0, w)], wbuf.at[k % 2, :, 0:w], load_sems.at[k % 2])

        def out_copy(k, slot, row0):
            c0, w = cols_of(*phases[k])
            return pltpu.make_async_copy(obuf.at[slot, :, 0:w], proj_ref.at[pl.ds(row0, tm), pl.ds(c0, w)], out_sems.at[slot])

        def drain(k):
            for j in range(min(2, nrow)):
                out_copy(k, (nrow - 1 - j) % 2, 0).wait()

        def arrivals(k):
            q, cp = phases[k]
            if k == 0:
                for cp_ in range(2):
                    for q_ in (1, 2):
                        to_neighbour(0, q_, cp_).start()
            elif q < 3 and k in (1, 3):
                pass_on_neighbours(0, cp)
                if k == 1:
                    for q_ in (1, 2):
                        to_neighbour(1, q_).start()
            elif k == 5:
                for cp_ in range(2):
                    pass_on_relayed(0, cp_)
                pass_on_neighbours(1)
            if q in (1, 2):
                from_sibling(0, q, cp).wait_recv()
            elif q == 3:
                for sub in range(2):
                    relayed_from_sibling(0, sub, cp).wait_recv()

        rows = pl.ds(pl.multiple_of(i * tm, tm), tm)
        slot = i % 2
        for k, (q, cp) in enumerate(phases):
            @pl.when(p == k)
            def _(k=k, q=q, cp=cp):
                @pl.when(i == 0)
                def _():
                    if k == 0:
                        arrivals(0)
                        shard_load(0).start()
                    else:
                        drain(k - 1)
                    shard_load(k).wait()

                if k + 1 < len(phases):
                    @pl.when(i == max(nrow - 2, 0))
                    def _():
                        arrivals(k + 1)
                        shard_load(k + 1).start()

                if k == 0:
                    xv = x_ref[...]
                    r = lax.rsqrt(jnp.mean(xv * xv, axis=-1, keepdims=True) + EPS)
                    hv = ((xv * r * g_ref[...]) * (1.0 + sc_ref[...]) + sh_ref[...]).astype(BF16)
                    h_ref[...] = hv
                    h_all[rows, :] = hv

                @pl.when(i >= 2)
                def _():
                    out_copy(k, slot, 0).wait()

                w = cols_of(q, cp)[1]
                obuf[slot, :, 0:w] = jnp.dot(h_all[rows, :], wbuf[k % 2, :, 0:w], preferred_element_type=F32)
                out_copy(k, slot, pl.multiple_of(i * tm, tm)).start()

        @pl.when((p == len(phases) - 1) & (i == nrow - 1))
        def _():
            drain(len(phases) - 1)
            pass_on_relayed(1)
            for q in (1, 2):
                from_sibling(1, q).wait_recv()
            for sub in range(2):
                relayed_from_sibling(1, sub).wait_recv()
            for which, cps in ((0, (0, 1)), (1, (None,))):
                for cp in cps:
                    for q in (1, 2):
                        to_neighbour(which, q, cp).wait_send()
                        relay(which, q, cp).wait_send()
                        to_sibling(which, q, cp).wait_send()
                        relayed_to_sibling(which, q - 1, cp).wait_send()

    vec = pl.BlockSpec((1, D), lambda p, i, pos: (0, 0))
    first_phase_rows = lambda p, i, pos: (jnp.where(p == 0, i, nrow - 1), 0)
    anyspec = pl.BlockSpec(memory_space=pl.ANY)
    return pl.pallas_call(
        body, name="proj_gather",
        grid_spec=pltpu.PrefetchScalarGridSpec(
            num_scalar_prefetch=1, grid=(len(phases), nrow),
            in_specs=[pl.BlockSpec((tm, D), first_phase_rows), vec, vec, vec, anyspec, anyspec],
            out_specs=[pl.BlockSpec((tm, D), first_phase_rows), anyspec, anyspec, anyspec],
            scratch_shapes=[pltpu.VMEM((s, D), BF16), pltpu.VMEM((2, D, W_IN_SHARD), BF16), pltpu.VMEM((2, tm, W_IN_SHARD), F32),
                            pltpu.SemaphoreType.DMA((24,)), pltpu.SemaphoreType.DMA((24,)), pltpu.SemaphoreType.DMA((2,)),
                            pltpu.SemaphoreType.DMA((2,))]),
        out_shape=[jax.ShapeDtypeStruct((s, D), BF16), jax.ShapeDtypeStruct((s, D_IN), F32),
                   jax.ShapeDtypeStruct((D, D_IN), BF16), jax.ShapeDtypeStruct((D, D), BF16)],
        input_output_aliases={5: 2, 6: 3},
        compiler_params=_params("arbitrary", "arbitrary"),
    )(pos, x, shift, scale, norm_g, wi_full, wo_full)


def _proj_specs(rev_nb=None):
    if rev_nb is None:
        row = lambda i: i
    else:
        row = lambda i: rev_nb - 1 - i
    wide = lambda col: pl.BlockSpec((BLK, D_A), lambda i: (row(i), col))
    kv = lambda col: pl.BlockSpec((BLK, D_KV), lambda i: (row(i), col))
    half = lambda col: pl.BlockSpec((BLK, 512), lambda i: (row(i), col))
    return [wide(0), wide(1), wide(2), wide(3), kv(OFF_K // D_KV), kv(OFF_V // D_KV), half(OFF_ZB // 512), half(OFF_ZB // 512 + 1)]


def _mix_fwd_call(proj, cos, sin, ln_g, ln_b, w_sp, b_sp_t, sinks):
    s = proj.shape[0]
    nb = s // BLK

    def body(ua_ref, va_ref, za_ref, q_ref, k_ref, v_ref, zb0_ref, zb1_ref, cos_ref, sin_ref, lg_ref, lb_ref,
             w_ref, bt_ref, sinks_ref, y_ref, kdup_ref, vdup_ref, qm_ref, ost_ref, bias_ref):
        i = pl.program_id(0)
        first_half, lo = _lane_masks()
        cos_t = cos_ref[...]
        sin_t = sin_ref[...]

        _, _, vln = _layer_norm_fwd(va_ref[...], lg_ref[...], lb_ref[...])
        tril = _tril()
        for g in range(GROUPS):
            cols = slice(g * BLK, (g + 1) * BLK)
            wg = jnp.where(tril, w_ref[g], 0.0).astype(BF16)
            sg = jnp.dot(wg, vln[:, cols].astype(BF16), preferred_element_type=F32) + bt_ref[:, g:g + 1]
            gate, _ = _silu_parts(za_ref[:, cols])
            y_ref[:, cols] = (ua_ref[:, cols] * sg * gate).astype(BF16)

        @pl.when(i == 0)
        def _():
            kdup_ref[:, 0:BLK, :] = jnp.zeros((N_KV, BLK, LANE), BF16)
            vdup_ref[:, 0:BLK, :] = jnp.zeros((N_KV, BLK, LANE), BF16)
            _band_bias(bias_ref)

        @pl.when(i > 0)
        def _():
            kdup_ref[:, 0:BLK, :] = kdup_ref[:, BLK:2 * BLK, :]
            vdup_ref[:, 0:BLK, :] = vdup_ref[:, BLK:2 * BLK, :]

        for ks in range(2):
            cols = slice(ks * LANE, (ks + 1) * LANE)
            kr = _rope(k_ref[:, cols], cos_t, sin_t, first_half)
            for n, (kd, vd) in enumerate(zip(_dup_kv(kr, lo), _dup_kv(v_ref[:, cols], lo))):
                kdup_ref[2 * ks + n, BLK:2 * BLK, :] = kd
                vdup_ref[2 * ks + n, BLK:2 * BLK, :] = vd
        for sb in range(8):
            _stack_heads(qm_ref, sb, _rope(q_ref[:, sb * LANE:(sb + 1) * LANE], cos_t, sin_t, first_half) * SCALE, lo, BF16)

        block_kind = jnp.where(i > 0, 1, 0)

        def kv_head(kh, carry):
            probs, _ = _softmax_sink(qm_ref[kh], kdup_ref[kh], bias_ref[block_kind], _sink_column(sinks_ref, kh))
            ost_ref[kh] = jnp.dot(probs.astype(BF16), vdup_ref[kh], preferred_element_type=F32)
            return carry

        lax.fori_loop(0, N_KV, kv_head, 0, unroll=2)
        for sb in range(8):
            cols = slice(sb * LANE, (sb + 1) * LANE)
            zb = zb0_ref[:, cols] if sb < 4 else zb1_ref[:, (sb - 4) * LANE:(sb - 3) * LANE]
            gate, _ = _silu_parts(zb)
            y_ref[:, D_A + sb * LANE:D_A + (sb + 1) * LANE] = (_unstack_heads(ost_ref, sb, lo) * gate).astype(BF16)

    tab = pl.BlockSpec((BLK, LANE), lambda i: (i, 0))
    return pl.pallas_call(
        body, name="mix_fwd", grid=(nb,),
        in_specs=_proj_specs() + [
            tab, tab, pl.BlockSpec((1, D_A), lambda i: (0, 0)), pl.BlockSpec((1, D_A), lambda i: (0, 0)),
            pl.BlockSpec((GROUPS, BLK, BLK), lambda i: (0, 0, 0)), pl.BlockSpec((BLK, GROUPS), lambda i: (0, 0)),
            pl.BlockSpec(memory_space=pltpu.SMEM)],
        out_specs=pl.BlockSpec((BLK, 2 * D_A), lambda i: (i, 0)),
        out_shape=jax.ShapeDtypeStruct((s, 2 * D_A), BF16),
        scratch_shapes=[pltpu.VMEM((N_KV, 2 * BLK, LANE), BF16), pltpu.VMEM((N_KV, 2 * BLK, LANE), BF16),
                        pltpu.VMEM((N_KV, Q_PER_KV * BLK, LANE), BF16), pltpu.VMEM((N_KV, Q_PER_KV * BLK, LANE), F32),
                        pltpu.VMEM((2, Q_PER_KV * BLK, 2 * BLK), F32)],
        compiler_params=_params("arbitrary"),
    )(proj, proj, proj, proj, proj, proj, proj, proj, cos, sin, ln_g, ln_b, w_sp, b_sp_t, sinks)


def _tail_call(y, w_out_bf, x, target, gate, shift_f, scale_f, gf):
    s = x.shape[0]
    tm = min(s, 256)
    nsteps = s // tm

    def body(y_ref, w_ref, x_ref, t_ref, gate_ref, shf_ref, scf_ref, gf_ref, dx2_ref, do_ref, dy_ref, st_ref):
        i = pl.program_id(0)

        @pl.when(i == 0)
        def _():
            st_ref[...] = jnp.zeros((8, D), F32)

        o = jnp.dot(y_ref[...], w_ref[...], preferred_element_type=F32)
        gate_v = gate_ref[...]
        x2 = x_ref[...] + gate_v * o
        r2 = lax.rsqrt(jnp.mean(x2 * x2, axis=-1, keepdims=True) + EPS)
        xn2 = x2 * r2
        hn2 = xn2 * gf_ref[...]
        one_sc = 1.0 + scf_ref[...]
        err = hn2 * one_sc + shf_ref[...] - t_ref[...]
        dout = err * (1.0 / D)
        dhn2 = dout * one_sc
        dxn2 = dhn2 * gf_ref[...]
        dx2 = r2 * (dxn2 - xn2 * jnp.mean(dxn2 * xn2, axis=-1, keepdims=True))
        dx2_ref[...] = dx2
        do = (dx2 * gate_v).astype(BF16)
        do_ref[...] = do
        dy_ref[...] = lax.dot_general(do, w_ref[...], NT, preferred_element_type=F32)
        st_ref[0:1, :] += jnp.sum(dout, axis=0, keepdims=True)
        st_ref[1:2, :] += jnp.sum(dout * hn2, axis=0, keepdims=True)
        st_ref[2:3, :] += jnp.sum(dhn2 * xn2, axis=0, keepdims=True)
        st_ref[3:4, :] += jnp.sum(dx2 * o, axis=0, keepdims=True)
        st_ref[4:5, :] += jnp.sum(err * err, axis=0, keepdims=True)

        @pl.when(i == nsteps - 1)
        def _():
            st_ref[5:6, :] = jnp.full((1, D), 0.5 / D, F32) * jnp.sum(st_ref[4:5, :])

    vec = pl.BlockSpec((1, D), lambda i: (0, 0))
    rows = lambda: pl.BlockSpec((tm, D), lambda i: (i, 0))
    return pl.pallas_call(
        body, name="tail", grid=(nsteps,),
        in_specs=[rows(), pl.BlockSpec((D, D), lambda i: (0, 0)), rows(), rows(), vec, vec, vec, vec],
        out_specs=[rows(), rows(), rows(), pl.BlockSpec((8, D), lambda i: (0, 0))],
        out_shape=[jax.ShapeDtypeStruct((s, D), F32), jax.ShapeDtypeStruct((s, D), BF16), jax.ShapeDtypeStruct((s, D), F32),
                   jax.ShapeDtypeStruct((8, D), F32)],
        compiler_params=_params("arbitrary"),
    )(y, w_out_bf, x, target, gate, shift_f, scale_f, gf)


def _tn_call(a, b, name):
    s, m = a.shape
    n = b.shape[1]
    tn = 512
    ts = min(s, 1024)
    nk = s // ts

    def body(a_ref, b_ref, o_ref, acc_ref):
        k = pl.program_id(1)

        @pl.when(k == 0)
        def _():
            acc_ref[...] = jnp.zeros((m, tn), F32)

        acc_ref[...] += lax.dot_general(a_ref[...], b_ref[...], TN, preferred_element_type=F32)

        @pl.when(k == nk - 1)
        def _():
            o_ref[...] = acc_ref[...].astype(BF16)

    return pl.pallas_call(
        body, name=name, grid=(n // tn, nk),
        in_specs=[pl.BlockSpec((ts, m), lambda j, k: (k, 0)), pl.BlockSpec((ts, tn), lambda j, k: (k, j))],
        out_specs=pl.BlockSpec((m, tn), lambda j, k: (0, j)),
        out_shape=jax.ShapeDtypeStruct((m, n), BF16),
        scratch_shapes=[pltpu.VMEM((m, tn), F32)],
        compiler_params=_params("parallel", "arbitrary"),
    )(a, b)


def _tn_shards_call(pos, a, b, qs, name):
    s, m = a.shape
    ts = min(s, 1024)
    nk = s // ts

    def body(pos_ref, a_ref, b_ref, o_ref, acc_ref):
        k = pl.program_id(1)

        @pl.when(k == 0)
        def _():
            acc_ref[...] = jnp.zeros((m, W_IN_SHARD), F32)

        acc_ref[...] += lax.dot_general(a_ref[...], b_ref[...], TN, preferred_element_type=F32)

        @pl.when(k == nk - 1)
        def _():
            o_ref[...] = acc_ref[...].astype(BF16)

    def shard(j, pos):
        q = qs[0]
        for n in range(1, len(qs)):
            q = jnp.where(j == n, qs[n], q)
        return jnp.bitwise_xor(pos[0], q)

    return pl.pallas_call(
        body, name=name,
        grid_spec=pltpu.PrefetchScalarGridSpec(
            num_scalar_prefetch=1, grid=(len(qs), nk),
            in_specs=[pl.BlockSpec((ts, m), lambda j, k, pos: (k, 0)),
                      pl.BlockSpec((ts, W_IN_SHARD), lambda j, k, pos: (k, shard(j, pos)))],
            out_specs=pl.BlockSpec((m, W_IN_SHARD), lambda j, k, pos: (0, j)),
            scratch_shapes=[pltpu.VMEM((m, W_IN_SHARD), F32)]),
        out_shape=jax.ShapeDtypeStruct((m, len(qs) * W_IN_SHARD), BF16),
        compiler_params=_params("parallel", "arbitrary"),
    )(pos, a, b)


def _mix_bwd_call(proj, dy, cos, sin, ln_g, ln_b, w_sp, w_sp_t, b_sp_t, sinks):
    s = proj.shape[0]
    nb = s // BLK
    rev = lambda i: nb - 1 - i
    prev = lambda i: jnp.maximum(nb - 2 - i, 0)

    def body(ua_ref, va_ref, za_ref, q_ref, k_ref, v_ref, zb0_ref, zb1_ref, kp_ref, vp_ref, dy_ref,
             cos_ref, sin_ref, cosp_ref, sinp_ref, lg_ref, lb_ref, w_ref, wt_ref, bt_ref, sinks_ref,
             dp_ref, lnst_ref, dw_ref, dbt_ref, dsink_ref,
             kdup_ref, vdup_ref, dvln_ref, qm_ref, dom_ref, ost_ref, dqst_ref, dkdup_ref, dvdup_ref, kcar_ref, vcar_ref,
             sigb_ref, bias_ref):
        i = pl.program_id(0)
        first_half, lo = _lane_masks()
        lane8 = lax.broadcasted_iota(jnp.int32, (8, LANE), 1)
        cos_t = cos_ref[...]
        sin_t = sin_ref[...]

        @pl.when(i == 0)
        def _():
            lnst_ref[...] = jnp.zeros((8, D_A), F32)
            dw_ref[...] = jnp.zeros((GROUPS, BLK, BLK), F32)
            dbt_ref[...] = jnp.zeros((BLK, LANE), F32)
            dsink_ref[...] = jnp.zeros((8, LANE), F32)
            kcar_ref[...] = jnp.zeros((BLK, D_KV), F32)
            vcar_ref[...] = jnp.zeros((BLK, D_KV), F32)
            _band_bias(bias_ref)

        vhat, rstd, vln = _layer_norm_fwd(va_ref[...], lg_ref[...], lb_ref[...])
        tril = _tril()
        triu = jnp.logical_not(tril) | (lax.broadcasted_iota(jnp.int32, (BLK, BLK), 0) == lax.broadcasted_iota(jnp.int32, (BLK, BLK), 1))
        lane_b = lax.broadcasted_iota(jnp.int32, (BLK, LANE), 1)
        db_acc = jnp.zeros((BLK, LANE), F32)
        for g in range(GROUPS):
            cols = slice(g * BLK, (g + 1) * BLK)
            vln_g = vln[:, cols].astype(BF16)
            wg = jnp.where(tril, w_ref[g], 0.0).astype(BF16)
            sg = jnp.dot(wg, vln_g, preferred_element_type=F32) + bt_ref[:, g:g + 1]
            za = za_ref[:, cols]
            gate, sig = _silu_parts(za)
            ua = ua_ref[:, cols]
            dya_g = dy_ref[:, cols]
            dya = dya_g * gate
            dp_ref[:, cols] = (dya * sg).astype(BF16)
            dp_ref[:, 2 * D_A + g * BLK:2 * D_A + (g + 1) * BLK] = (
                dya_g * (ua * sg) * (sig * (1.0 + za * (1.0 - sig)))).astype(BF16)
            ds = dya * ua
            ds_b = ds.astype(BF16)
            wtg = jnp.where(triu, wt_ref[g], 0.0).astype(BF16)
            dvln_ref[:, cols] = jnp.dot(wtg, ds_b, preferred_element_type=F32)
            dw_ref[g] += jnp.where(tril, lax.dot_general(ds_b, vln_g, NT, preferred_element_type=F32), 0.0)
            db_acc = db_acc + jnp.where(lane_b == g, jnp.sum(ds, axis=-1, keepdims=True), 0.0)
        dbt_ref[...] += db_acc
        dvln = dvln_ref[...]
        lnst_ref[0:1, :] += jnp.sum(dvln * vhat, axis=0, keepdims=True)
        lnst_ref[1:2, :] += jnp.sum(dvln, axis=0, keepdims=True)
        dvhat = dvln * lg_ref[...]
        m1 = jnp.mean(dvhat, axis=-1, keepdims=True)
        m2 = jnp.mean(dvhat * vhat, axis=-1, keepdims=True)
        dp_ref[:, D_A:2 * D_A] = (rstd * (dvhat - m1 - vhat * m2)).astype(BF16)

        cosp = cosp_ref[...]
        sinp = sinp_ref[...]
        for ks in range(2):
            cols = slice(ks * LANE, (ks + 1) * LANE)
            kr = _rope(k_ref[:, cols], cos_t, sin_t, first_half)
            kpr = _rope(kp_ref[:, cols], cosp, sinp, first_half)
            for n, (kc, vc, kp, vp) in enumerate(zip(_dup_kv(kr, lo), _dup_kv(v_ref[:, cols], lo),
                                                     _dup_kv(kpr, lo), _dup_kv(vp_ref[:, cols], lo))):
                kdup_ref[2 * ks + n, BLK:2 * BLK, :] = kc
                vdup_ref[2 * ks + n, BLK:2 * BLK, :] = vc
                kdup_ref[2 * ks + n, 0:BLK, :] = kp
                vdup_ref[2 * ks + n, 0:BLK, :] = vp
        for sb in range(8):
            cols = slice(sb * LANE, (sb + 1) * LANE)
            _stack_heads(qm_ref, sb, _rope(q_ref[:, cols], cos_t, sin_t, first_half) * SCALE, lo, BF16)
            zb = zb0_ref[:, cols] if sb < 4 else zb1_ref[:, (sb - 4) * LANE:(sb - 3) * LANE]
            gate, sig = _silu_parts(zb)
            sigb_ref[:, cols] = sig
            _stack_heads(dom_ref, sb, dy_ref[:, D_A + sb * LANE:D_A + (sb + 1) * LANE] * gate, lo, F32)

        block_kind = jnp.where(i < nb - 1, 1, 0)

        def kv_head(kh, dsink_acc):
            qm = qm_ref[kh]
            kd = kdup_ref[kh]
            vd = vdup_ref[kh]
            probs, psink = _softmax_sink(qm, kd, bias_ref[block_kind], _sink_column(sinks_ref, kh))
            probs_b = probs.astype(BF16)
            o = jnp.dot(probs_b, vd, preferred_element_type=F32)
            ost_ref[kh] = o
            dom = dom_ref[kh]
            dom_b = dom.astype(BF16)
            delta = jnp.sum(dom * o, axis=-1, keepdims=True)
            dpr = lax.dot_general(dom_b, vd, NT, preferred_element_type=F32)
            dss = (probs * (dpr - delta)).astype(BF16)
            sd = psink * delta
            for n in range(Q_PER_KV):
                dsink_acc = dsink_acc + jnp.where(lane8 == Q_PER_KV * kh + n, -jnp.sum(sd[n * BLK:(n + 1) * BLK]), 0.0)
            dqst_ref[kh] = jnp.dot(dss, kd, preferred_element_type=F32)
            dkdup_ref[kh] = lax.dot_general(dss, qm, TN, preferred_element_type=F32)
            dvdup_ref[kh] = lax.dot_general(probs_b, dom_b, TN, preferred_element_type=F32)
            return dsink_acc

        dsink_acc = lax.fori_loop(0, N_KV // 2, lambda j, acc: kv_head(2 * j + 1, kv_head(2 * j, acc)), jnp.zeros((8, LANE), F32))
        row0 = lax.broadcasted_iota(jnp.int32, (8, LANE), 0) == 0
        dsink_ref[...] += jnp.where(row0, dsink_acc, 0.0)

        for sb in range(8):
            cols = slice(sb * LANE, (sb + 1) * LANE)
            zb = zb0_ref[:, cols] if sb < 4 else zb1_ref[:, (sb - 4) * LANE:(sb - 3) * LANE]
            sig = sigb_ref[:, cols]
            dyb = dy_ref[:, D_A + sb * LANE:D_A + (sb + 1) * LANE]
            dp_ref[:, OFF_ZB + sb * LANE:OFF_ZB + (sb + 1) * LANE] = (
                dyb * _unstack_heads(ost_ref, sb, lo) * (sig * (1.0 + zb * (1.0 - sig)))).astype(BF16)
            dq_r = _unstack_heads(dqst_ref, sb, lo) * SCALE
            dp_ref[:, OFF_Q + sb * LANE:OFF_Q + (sb + 1) * LANE] = _unrope(dq_r, cos_t, sin_t, first_half).astype(BF16)

        lo2 = lax.broadcasted_iota(jnp.int32, (2 * BLK, LANE), 1) < HEAD
        for ks in range(2):
            cols = slice(ks * LANE, (ks + 1) * LANE)
            ka = dkdup_ref[2 * ks]
            kb = dkdup_ref[2 * ks + 1]
            dk_band = jnp.where(lo2, ka + pltpu.roll(ka, HEAD, 1), kb + pltpu.roll(kb, HEAD, 1))
            va_ = dvdup_ref[2 * ks]
            vb_ = dvdup_ref[2 * ks + 1]
            dv_band = jnp.where(lo2, va_ + pltpu.roll(va_, HEAD, 1), vb_ + pltpu.roll(vb_, HEAD, 1))
            dkr = dk_band[BLK:2 * BLK, :] + kcar_ref[:, cols]
            dp_ref[:, OFF_K + ks * LANE:OFF_K + (ks + 1) * LANE] = _unrope(dkr, cos_t, sin_t, first_half).astype(BF16)
            dp_ref[:, OFF_V + ks * LANE:OFF_V + (ks + 1) * LANE] = (
                dv_band[BLK:2 * BLK, :] + vcar_ref[:, cols]).astype(BF16)
            kcar_ref[:, cols] = dk_band[0:BLK, :]
            vcar_ref[:, cols] = dv_band[0:BLK, :]

    tab = pl.BlockSpec((BLK, LANE), lambda i: (rev(i), 0))
    tabp = pl.BlockSpec((BLK, LANE), lambda i: (prev(i), 0))
    kvp = lambda col: pl.BlockSpec((BLK, D_KV), lambda i: (prev(i), col))
    vec = pl.BlockSpec((1, D_A), lambda i: (0, 0))
    w3 = pl.BlockSpec((GROUPS, BLK, BLK), lambda i: (0, 0, 0))
    return pl.pallas_call(
        body, name="mix_bwd", grid=(nb,),
        in_specs=_proj_specs(nb) + [
            kvp(OFF_K // D_KV), kvp(OFF_V // D_KV), pl.BlockSpec((BLK, 2 * D_A), lambda i: (rev(i), 0)),
            tab, tab, tabp, tabp, vec, vec, w3, w3, pl.BlockSpec((BLK, GROUPS), lambda i: (0, 0)),
            pl.BlockSpec(memory_space=pltpu.SMEM)],
        out_specs=[pl.BlockSpec((BLK, D_IN), lambda i: (rev(i), 0)), pl.BlockSpec((8, D_A), lambda i: (0, 0)), w3,
                   pl.BlockSpec((BLK, LANE), lambda i: (0, 0)), pl.BlockSpec((8, LANE), lambda i: (0, 0))],
        out_shape=[jax.ShapeDtypeStruct((s, D_IN), BF16), jax.ShapeDtypeStruct((8, D_A), F32),
                   jax.ShapeDtypeStruct((GROUPS, BLK, BLK), F32), jax.ShapeDtypeStruct((BLK, LANE), F32),
                   jax.ShapeDtypeStruct((8, LANE), F32)],
        scratch_shapes=[pltpu.VMEM((N_KV, 2 * BLK, LANE), BF16), pltpu.VMEM((N_KV, 2 * BLK, LANE), BF16),
                        pltpu.VMEM((BLK, D_A), F32), pltpu.VMEM((N_KV, Q_PER_KV * BLK, LANE), BF16),
                        pltpu.VMEM((N_KV, Q_PER_KV * BLK, LANE), F32), pltpu.VMEM((N_KV, Q_PER_KV * BLK, LANE), F32),
                        pltpu.VMEM((N_KV, Q_PER_KV * BLK, LANE), F32), pltpu.VMEM((N_KV, 2 * BLK, LANE), F32),
                        pltpu.VMEM((N_KV, 2 * BLK, LANE), F32), pltpu.VMEM((BLK, D_KV), F32), pltpu.VMEM((BLK, D_KV), F32),
                        pltpu.VMEM((BLK, D_B), F32), pltpu.VMEM((2, Q_PER_KV * BLK, 2 * BLK), F32)],
        compiler_params=_params("arbitrary"),
    )(proj, proj, proj, proj, proj, proj, proj, proj, proj, proj, dy, cos, sin, cos, sin, ln_g, ln_b, w_sp, w_sp_t,
      b_sp_t, sinks)


def _dh_call(dproj, w_bf, x, dx2, scale, norm_g):
    s = x.shape[0]
    tm = min(s, 512)
    tk = W_IN_SHARD
    nk = D_IN // tk

    def body(dp_ref, w_ref, x_ref, dx2_ref, sc_ref, g_ref, gx_ref, st_ref, acc_ref):
        i = pl.program_id(0)
        k = pl.program_id(1)

        @pl.when((i == 0) & (k == 0))
        def _():
            st_ref[...] = jnp.zeros((8, D), F32)

        @pl.when(k == 0)
        def _():
            acc_ref[...] = jnp.zeros((tm, D), F32)

        acc_ref[...] += lax.dot_general(dp_ref[...], w_ref[...], NT, preferred_element_type=F32)

        @pl.when(k == nk - 1)
        def _():
            g = g_ref[...]
            one_sc = 1.0 + sc_ref[...]

            def chunk(n, carry):
                rows = pl.ds(pl.multiple_of(n * BLK, BLK), BLK)
                dh = acc_ref[rows, :]
                xv = x_ref[rows, :]
                r = lax.rsqrt(jnp.mean(xv * xv, axis=-1, keepdims=True) + EPS)
                xn = xv * r
                dhn = dh * one_sc
                dxn = dhn * g
                gx_ref[rows, :] = dx2_ref[rows, :] + r * (dxn - xn * jnp.mean(dxn * xn, axis=-1, keepdims=True))
                st_ref[0:1, :] += jnp.sum(dh, axis=0, keepdims=True)
                st_ref[1:2, :] += jnp.sum(dh * (xn * g), axis=0, keepdims=True)
                st_ref[2:3, :] += jnp.sum(dhn * xn, axis=0, keepdims=True)
                return carry

            lax.fori_loop(0, tm // BLK, chunk, 0)

    vec = pl.BlockSpec((1, D), lambda i, k: (0, 0))
    rows = lambda: pl.BlockSpec((tm, D), lambda i, k: (i, 0))
    return pl.pallas_call(
        body, name="dh", grid=(s // tm, nk),
        in_specs=[pl.BlockSpec((tm, tk), lambda i, k: (i, k)), pl.BlockSpec((D, tk), lambda i, k: (0, k)), rows(), rows(), vec, vec],
        out_specs=[rows(), pl.BlockSpec((8, D), lambda i, k: (0, 0))],
        out_shape=[jax.ShapeDtypeStruct((s, D), F32), jax.ShapeDtypeStruct((8, D), F32)],
        scratch_shapes=[pltpu.VMEM((tm, D), F32)],
        compiler_params=_params("arbitrary", "arbitrary"),
    )(dproj, w_bf, x, dx2, scale, norm_g)


def _adam_math(w, g, m, v):
    m_new = ADAM_B1 * m + (1.0 - ADAM_B1) * g
    v_new = ADAM_B2 * v + (1.0 - ADAM_B2) * (g * g)
    m_hat = m_new / ADAM_C1
    v_hat = v_new / ADAM_C2
    delta = -ADAM_LR * (m_hat / (jnp.sqrt(v_hat) + ADAM_EPS) + ADAM_WD * w)
    return delta, m_new, v_new


def _adam_small_call(tensors):
    n = len(tensors)

    def body(*refs):
        ins, outs = refs[:4 * n], refs[4 * n:]
        for t in range(n):
            w_ref, g_ref, m_ref, v_ref = ins[4 * t:4 * t + 4]
            d, mo, vo = _adam_math(w_ref[...], g_ref[...], m_ref[...], v_ref[...])
            outs[3 * t][...], outs[3 * t + 1][...], outs[3 * t + 2][...] = d, mo, vo

    vm = pl.BlockSpec(memory_space=pltpu.VMEM)
    flat = [a for t in tensors for a in t]
    out = pl.pallas_call(
        body, name="adam_small", in_specs=[vm] * (4 * n), out_specs=[vm] * (3 * n),
        out_shape=[jax.ShapeDtypeStruct(t[0].shape, F32) for t in tensors for _ in range(3)],
        compiler_params=pltpu.CompilerParams(vmem_limit_bytes=VMEM_LIMIT),
    )(*flat)
    return [tuple(out[3 * t:3 * t + 3]) for t in range(n)]


def _adam_halves_call(pos, w, mine, theirs, m, v, name):
    r, n = w.shape
    half = r // 2
    tr = ADAM_ROWS
    nh = half // tr

    def body(pos_ref, w_ref, mine_ref, theirs_ref, m_ref, v_ref, g_ref, d_ref, mo_ref, vo_ref):
        is_mine = (pl.program_id(0) // nh) == pos_ref[1]
        g = jnp.where(is_mine, mine_ref[...], theirs_ref[...])
        g_ref[...] = g
        d_ref[...], mo_ref[...], vo_ref[...] = _adam_math(w_ref[...], g, m_ref[...], v_ref[...])

    spec = lambda: pl.BlockSpec((tr, n), lambda i, pos: (i, 0))
    hspec = lambda: pl.BlockSpec((tr, n), lambda i, pos: (i % nh, 0))
    return pl.pallas_call(
        body, name=name,
        grid_spec=pltpu.PrefetchScalarGridSpec(
            num_scalar_prefetch=1, grid=(r // tr,), in_specs=[spec(), hspec(), hspec(), spec(), spec()],
            out_specs=[spec() for _ in range(4)]),
        out_shape=[jax.ShapeDtypeStruct((r, n), F32)] * 4, compiler_params=_params("parallel"),
    )(pos, w, mine, theirs, m, v)


def _adam_outer_call(w, ct, dm, m, v, name):
    r, n = w.shape
    tr = ADAM_ROWS

    def body(w_ref, ct_ref, dm_ref, m_ref, v_ref, g_ref, d_ref, mo_ref, vo_ref):
        g = ct_ref[:, 0:1] * dm_ref[0:1, :]
        for b in range(1, N_DEV):
            g = g + ct_ref[:, b:b + 1] * dm_ref[b:b + 1, :]
        g_ref[...] = g
        d_ref[...], mo_ref[...], vo_ref[...] = _adam_math(w_ref[...], g, m_ref[...], v_ref[...])

    spec = lambda: pl.BlockSpec((tr, n), lambda i: (i, 0))
    return pl.pallas_call(
        body, name=name, grid=(r // tr,),
        in_specs=[spec(), pl.BlockSpec((tr, N_DEV), lambda i: (i, 0)), pl.BlockSpec((N_DEV, n), lambda i: (0, 0)), spec(), spec()],
        out_specs=[spec() for _ in range(4)],
        out_shape=[jax.ShapeDtypeStruct((r, n), F32)] * 4, compiler_params=_params("parallel"),
    )(w, ct, dm, m, v)


def _sum_pieces_call(pos, part, part_block, recvs, name):
    r, n = recvs[0].shape[1:]
    tr = min(r, 256)
    nrb = r // tr

    def body(pos_ref, p_ref, *refs):
        acc = p_ref[...].astype(F32)
        for r_ref in refs[:-1]:
            for d in range(r_ref.shape[0]):
                acc = acc + r_ref[d].astype(F32)
        refs[-1][...] = acc

    return pl.pallas_call(
        body, name=name,
        grid_spec=pltpu.PrefetchScalarGridSpec(
            num_scalar_prefetch=1, grid=(nrb,),
            in_specs=[pl.BlockSpec((tr, n), lambda i, pos: part_block(i, pos, nrb))] + [
                pl.BlockSpec((rv.shape[0], tr, n), lambda i, pos: (0, i, 0)) for rv in recvs],
            out_specs=pl.BlockSpec((tr, n), lambda i, pos: (i, 0))),
        out_shape=jax.ShapeDtypeStruct((r, n), F32), compiler_params=_params("parallel"),
    )(pos, part, *recvs)


def _coords():
    return lax.axis_index("x"), lax.axis_index("y"), lax.axis_index("c")


def _allgather_sum_call(blk, name, with_sum):
    m_per, n = blk.shape

    def body(x_ref, out_ref, *rest):
        if with_sum:
            sum_ref, send_sems, recv_sems, local_sem = rest
        else:
            send_sems, recv_sems, local_sem = rest
        x, y, c = _coords()
        me, sibling = (x, y, c), (x, y, 1 - c)
        chips = [(1 - x, y), (x, 1 - y), (1 - x, 1 - y)]

        def rows(px, py, pc):
            return out_ref.at[pl.ds((4 * px + 2 * py + pc) * m_per, m_per), :]

        def copy(k, block, to, src=None):
            return pltpu.make_async_remote_copy(
                src_ref=rows(*block) if src is None else src, dst_ref=rows(*block),
                send_sem=send_sems.at[k], recv_sem=recv_sems.at[k], device_id=to, device_id_type=MESH)

        mine = pltpu.make_async_copy(x_ref, rows(*me), local_sem)
        mine.start()
        first = [copy(0, me, sibling, src=x_ref)]
        first += [copy(1 + j, me, (*chip, c), src=x_ref) for j, chip in enumerate(chips)]
        for cp in first:
            cp.start()
        passed = [copy(4 + j, (*chip, c), sibling) for j, chip in enumerate(chips)]
        for j, chip in enumerate(chips):
            copy(1 + j, (*chip, c), me).wait_recv()
            passed[j].start()
        copy(0, sibling, me).wait_recv()
        for j, chip in enumerate(chips):
            copy(4 + j, (*chip, 1 - c), me).wait_recv()
        for cp in first + passed:
            cp.wait_send()
        mine.wait()
        if with_sum:
            acc = out_ref[0:m_per, :]
            for d in range(1, N_DEV):
                acc = acc + out_ref[d * m_per:(d + 1) * m_per, :]
            sum_ref[...] = acc

    vm = pl.BlockSpec(memory_space=pltpu.VMEM)
    out_shape = [jax.ShapeDtypeStruct((N_DEV * m_per, n), F32)]
    if with_sum:
        out_shape.append(jax.ShapeDtypeStruct((m_per, n), F32))
    return pl.pallas_call(
        body, name=name, out_shape=out_shape, in_specs=[vm], out_specs=[vm] * len(out_shape),
        scratch_shapes=[pltpu.SemaphoreType.DMA((7,)), pltpu.SemaphoreType.DMA((7,)), pltpu.SemaphoreType.DMA],
        compiler_params=pltpu.CompilerParams(vmem_limit_bytes=VMEM_LIMIT),
    )(blk)


def _weights_gather_call(wi_full, wo_full):
    hi = D // 2
    ho = W_OUT_SHARD // 2

    def body(wi_in, wo_in, fi_ref, fo_ref, send_sems, recv_sems):
        del wi_in, wo_in
        x, y, c = _coords()
        sibling = (x, y, 1 - c)
        chips = [(1 - x, y), (x, 1 - y), (1 - x, 1 - y)]

        def half(which, px, py, pc):
            j = 2 * px + py
            if which == 0:
                return fi_ref.at[pl.ds(pc * hi, hi), pl.ds(j * W_IN_SHARD, W_IN_SHARD)]
            return fo_ref.at[pl.ds(j * W_OUT_SHARD + pc * ho, ho), :]

        def copy(k, which, block, to):
            return pltpu.make_async_remote_copy(
                src_ref=half(which, *block), dst_ref=half(which, *block), send_sem=send_sems.at[k],
                recv_sem=recv_sems.at[k], device_id=to, device_id_type=MESH)

        first = [copy(6 * w + j, w, (x, y, c), (*chip, c)) for w in range(2) for j, chip in enumerate(chips)]
        for cp in first:
            cp.start()
        passed = []
        for w in range(2):
            for j, chip in enumerate(chips):
                copy(6 * w + j, w, (*chip, c), (x, y, c)).wait_recv()
                cp = copy(6 * w + 3 + j, w, (*chip, c), sibling)
                cp.start()
                passed.append(cp)
        for w in range(2):
            for j, chip in enumerate(chips):
                copy(6 * w + 3 + j, w, (*chip, 1 - c), (x, y, c)).wait_recv()
        for cp in first + passed:
            cp.wait_send()

    anyspec = pl.BlockSpec(memory_space=pl.ANY)
    return pl.pallas_call(
        body, name="weights_gather",
        out_shape=[jax.ShapeDtypeStruct((D, D_IN), BF16), jax.ShapeDtypeStruct((D, D), BF16)],
        in_specs=[anyspec, anyspec], out_specs=[anyspec, anyspec], input_output_aliases={0: 0, 1: 1},
        scratch_shapes=[pltpu.SemaphoreType.DMA((12,)), pltpu.SemaphoreType.DMA((12,))],
    )(wi_full, wo_full)


HBM_SPEC = pl.BlockSpec(memory_space=pltpu.HBM)
SEM_SPEC = pl.BlockSpec(memory_space=pltpu.SEMAPHORE)
SIDE_EFFECT = pltpu.SideEffectType.DATAFLOW_SIDE_EFFECTING


def _peer(x, y, c, q, cb):
    return (1 - x if q & 2 else x, 1 - y if q & 1 else y, 1 - c if cb else c)


def _w_in_piece(slots):
    def piece(part_ref, k, to):
        return part_ref.at[pl.ds(to[2] * (D // 2), D // 2), pl.ds(slots[k] * W_IN_SHARD, W_IN_SHARD)]
    return piece


def _w_out_piece(part_ref, k, to):
    ho = W_OUT_SHARD // 2
    return part_ref.at[pl.ds((2 * to[0] + to[1]) * W_OUT_SHARD + to[2] * ho, ho), :]


def _group_piece(part_ref, k, to):
    return part_ref.at[4 * to[0] + 2 * to[1] + to[2]]


def _whole_piece(part_ref, k, to):
    return part_ref


def _exchange_start_call(part, rels, piece, slot_shape, name):
    n = len(rels)
    land = lax.empty((n,) + slot_shape, part.dtype)

    def body(part_ref, land_ref, send_sems, recv_sems, part_thru, land_thru, token):
        x, y, c = _coords()
        for k, (q, cb) in enumerate(rels):
            to = _peer(x, y, c, q, cb)
            pltpu.make_async_remote_copy(src_ref=piece(part_ref, k, to), dst_ref=land_ref.at[k], send_sem=send_sems.at[k],
                                         recv_sem=recv_sems.at[k], device_id=to, device_id_type=MESH).start()
        token[...] = jnp.zeros_like(token)

    return pl.pallas_call(
        body, name=name,
        out_shape=(pltpu.SemaphoreType.DMA((n,)), pltpu.SemaphoreType.DMA((n,)), pltpu.HBM(part.shape, part.dtype),
                   pltpu.HBM(land.shape, land.dtype), jax.ShapeDtypeStruct((8, LANE), F32)),
        in_specs=(HBM_SPEC, HBM_SPEC), out_specs=(SEM_SPEC, SEM_SPEC, HBM_SPEC, HBM_SPEC, pl.BlockSpec(memory_space=pltpu.VMEM)),
        input_output_aliases={0: 2, 1: 3},
        compiler_params=pltpu.CompilerParams(has_side_effects=SIDE_EFFECT),
    )(pltpu.with_memory_space_constraint(part, pltpu.HBM), pltpu.with_memory_space_constraint(land, pltpu.HBM))


def _exchange_wait_call(started, rels, piece, after, name):
    send_sems, recv_sems, part_thru, land_thru, _ = started

    def body(part_ref, land_ref, send_sems, recv_sems, after_ref, part_out, land_out):
        x, y, c = _coords()
        for k, (q, cb) in enumerate(rels):
            to = _peer(x, y, c, q, cb)
            cp = pltpu.make_async_remote_copy(src_ref=piece(part_ref, k, to), dst_ref=land_ref.at[k], send_sem=send_sems.at[k],
                                              recv_sem=recv_sems.at[k], device_id=to, device_id_type=MESH)
            cp.wait_send()
            cp.wait_recv()

    return pl.pallas_call(
        body, name=name,
        out_shape=(pltpu.HBM(part_thru.shape, part_thru.dtype), pltpu.HBM(land_thru.shape, land_thru.dtype)),
        in_specs=(HBM_SPEC, HBM_SPEC, SEM_SPEC, SEM_SPEC, pl.BlockSpec(memory_space=pl.ANY)), out_specs=(HBM_SPEC, HBM_SPEC),
        input_output_aliases={0: 0, 1: 1},
        compiler_params=pltpu.CompilerParams(has_side_effects=SIDE_EFFECT),
    )(part_thru, land_thru, send_sems, recv_sems, after)


def _pair_exchange_call(gi, go):
    hi = D // 2
    ho = W_OUT_SHARD // 2

    def body(gi_in, go_in, fi_ref, fo_ref, send_sems, recv_sems):
        del gi_in, go_in
        x, y, c = _coords()
        sibling = (x, y, 1 - c)
        mine = (fi_ref.at[pl.ds(c * hi, hi), :], fo_ref.at[pl.ds(c * ho, ho), :])
        theirs = (fi_ref.at[pl.ds((1 - c) * hi, hi), :], fo_ref.at[pl.ds((1 - c) * ho, ho), :])
        sends = [pltpu.make_async_remote_copy(src_ref=ref, dst_ref=ref, send_sem=send_sems.at[k], recv_sem=recv_sems.at[k],
                                              device_id=sibling, device_id_type=MESH) for k, ref in enumerate(mine)]
        for cp in sends:
            cp.start()
        for k, ref in enumerate(theirs):
            pltpu.make_async_remote_copy(src_ref=ref, dst_ref=ref, send_sem=send_sems.at[k], recv_sem=recv_sems.at[k],
                                         device_id=sibling, device_id_type=MESH).wait_recv()
        for cp in sends:
            cp.wait_send()

    anyspec = pl.BlockSpec(memory_space=pl.ANY)
    return pl.pallas_call(
        body, name="pair_exchange",
        out_shape=[jax.ShapeDtypeStruct((D, W_IN_SHARD), F32), jax.ShapeDtypeStruct((W_OUT_SHARD, D), F32)],
        in_specs=[anyspec, anyspec], out_specs=[anyspec, anyspec], input_output_aliases={0: 0, 1: 1},
        scratch_shapes=[pltpu.SemaphoreType.DMA((2,)), pltpu.SemaphoreType.DMA((2,))],
    )(gi, go)


def _rope_tables(s):
    inv_freq = 10000.0 ** (-jnp.arange(0, HEAD, 2, dtype=F32) / HEAD)
    ang = jnp.arange(s, dtype=F32)[:, None] * inv_freq[None, :]
    cos = jnp.tile(jnp.cos(ang), (1, LANE // (HEAD // 2)))
    sin = jnp.tile(jnp.sin(ang), (1, LANE // (HEAD // 2)))
    first_half = (jnp.arange(LANE) % HEAD) < (HEAD // 2)
    return cos, jnp.where(first_half[None, :], -sin, sin)


def _pad_cols(a, n):
    return jnp.pad(a, ((0, 0), (0, n - a.shape[1])))


def kernel(x, c, w_ada, b_ada, norm_g, w_in, ln_v_g, ln_v_b, w_spatial, b_spatial, sinks, w_out, w_ada_final, b_ada_final, final_norm_g, loss_target, m_w_ada, m_b_ada, m_norm_g, m_w_in, m_ln_v_g, m_ln_v_b, m_w_spatial, m_b_spatial, m_sinks, m_w_out, m_w_ada_final, m_b_ada_final, m_final_norm_g, v_w_ada, v_b_ada, v_norm_g, v_w_in, v_ln_v_g, v_ln_v_b, v_w_spatial, v_b_spatial, v_sinks, v_w_out, v_w_ada_final, v_b_ada_final, v_final_norm_g):
    s = x.shape[1]
    ax, ay, ac = _coords()
    chip = 2 * ax + ay
    me = 4 * ax + 2 * ay + ac
    n_ada = w_ada.shape[2]
    n_adaf = w_ada_final.shape[1]

    x2d = x.reshape(s, D)
    tgt = loss_target.reshape(s, D)
    w_ada2, w_in2, w_out2 = w_ada[0], w_in[0], w_out[0]
    b_ada_f2 = b_ada_final.reshape(1, 2 * D)
    gf = final_norm_g.reshape(1, D)

    c_all = _allgather_sum_call(jnp.pad(c, ((0, 7), (0, 0))), "gather_c", False)[0][::8]
    mod_p, c_act = _rowmat_call(c_all, w_ada2, lax.dynamic_slice(b_ada, (0, chip * n_ada), (1, n_ada)), "mod")
    modf_p, _ = _rowmat_call(c_all, w_ada_final, lax.dynamic_slice(b_ada_f2, (0, chip * n_adaf), (1, n_adaf)), "mod_final")
    mods = _allgather_sum_call(jnp.concatenate([mod_p, modf_p], axis=1), "gather_mod", False)[0]
    my_rows = [lax.dynamic_slice(mods, (16 * j + me, 0), (1, n_ada + n_adaf)) for j in range(N_CHIP)]
    mod = jnp.concatenate([r[:, :n_ada] for r in my_rows], axis=1)
    mod_f = jnp.concatenate([r[:, n_ada:] for r in my_rows], axis=1)
    shift, scale, gate = mod[:, :D], mod[:, D:2 * D], mod[:, 2 * D:]
    shift_f, scale_f = mod_f[:, :D], mod_f[:, D:]

    pos = jnp.stack([chip, ac]).astype(jnp.int32)
    w_in_own = _cast_into_call(pos, w_in2, (D, D_IN), "cast_w_in")
    w_out_own = _cast_into_call(pos, w_out2, (D, D), "cast_w_out")

    cos, sin = _rope_tables(s)
    b_sp_t = b_spatial[0].T
    sinks1 = sinks.reshape(N_Q)
    h, proj, w_in_bf, w_out_bf = _proj_gather_call(pos, x2d, shift, scale, norm_g, w_in_own, w_out_own)
    y = _mix_fwd_call(proj, cos, sin, ln_v_g, ln_v_b, w_spatial[0], b_sp_t, sinks1)
    dx2, do, dy, st_tail = _tail_call(y, w_out_bf, x2d, tgt, gate, shift_f, scale_f, gf)

    rel_o = [(0, 1), (1, 0), (1, 1), (2, 0), (2, 1), (3, 0), (3, 1)]
    rel_a = [(1, 0), (1, 1), (2, 0), (2, 1)]
    rel_b = [(3, 0), (3, 1), (0, 1)]
    piece_a, piece_b = _w_in_piece([0, 0, 1, 1]), _w_in_piece([0, 0, 1])
    half_in, half_out = (D // 2, W_IN_SHARD), (W_OUT_SHARD // 2, D)

    g_w_out_p = _tn_call(y, do, "grad_w_out")
    st_o = _exchange_start_call(g_w_out_p, rel_o, _w_out_piece, half_out, "send_w_out")
    dproj, st_ln, d_wsp, d_bsp_t, d_sink = _mix_bwd_call(
        proj, dy, cos, sin, ln_v_g + st_o[4][0:1, 0:1], ln_v_b, w_spatial[0], jnp.swapaxes(w_spatial[0], 1, 2), b_sp_t, sinks1)
    g_w_in_a = _tn_shards_call(pos, h, dproj, (1, 2), "grad_w_in_a")
    st_a = _exchange_start_call(g_w_in_a, rel_a, piece_a, half_in, "send_w_in_a")
    g_w_in_b = _tn_shards_call(pos, h, dproj, (3, 0), "grad_w_in_b")
    st_b = _exchange_start_call(g_w_in_b, rel_b, piece_b, half_in, "send_w_in_b")
    rel_all = rel_o
    st_s = _exchange_start_call(d_wsp, rel_all, _group_piece, (BLK, BLK), "send_w_spatial")
    sent = st_a[4][0:1, 0:1] + st_b[4][0:1, 0:1] + st_s[4][0:1, 0:1]
    grad_x, st_dh = _dh_call(dproj, w_in_bf, x2d, dx2, scale + sent, norm_g)

    g_w_out_p, recv_o = _exchange_wait_call(st_o, rel_o, _w_out_piece, st_dh, "wait_w_out")
    _, recv_a = _exchange_wait_call(st_a, rel_a, piece_a, st_dh, "wait_w_in_a")
    g_w_in_b, recv_b = _exchange_wait_call(st_b, rel_b, piece_b, st_dh, "wait_w_in_b")
    d_wsp, recv_s = _exchange_wait_call(st_s, rel_all, _group_piece, st_dh, "wait_w_spatial")
    mine_in = _sum_pieces_call(pos, g_w_in_b, lambda i, p, nrb: (p[1] * nrb + i, 1), [recv_a, recv_b], "sum_w_in")
    mine_out = _sum_pieces_call(pos, g_w_out_p, lambda i, p, nrb: ((2 * p[0] + p[1]) * nrb + i, 0), [recv_o], "sum_w_out")
    wsp_group = _sum_pieces_call(pos, d_wsp.reshape(GROUPS * BLK, BLK), lambda i, p, nrb: (2 * p[0] + p[1], 0), [recv_s],
                                 "sum_w_spatial")
    to_sibling = [(0, 1)]
    st_pi = _exchange_start_call(mine_in, to_sibling, _whole_piece, half_in, "swap_w_in")
    st_po = _exchange_start_call(mine_out, to_sibling, _whole_piece, half_out, "swap_w_out")

    misc = jnp.concatenate([st_ln, d_bsp_t[:, :GROUPS].T, d_sink, jnp.zeros((8, D - D_A - 2 * LANE), F32)], axis=1)
    pack = jnp.concatenate([wsp_group.reshape(8, D) + (st_pi[4][0:1, 0:1] + st_po[4][0:1, 0:1]), st_tail, st_dh, misc], axis=0)
    rows = pack.shape[0]
    packs, tot = _allgather_sum_call(pack, "gather_small", True)
    packs = packs.reshape(N_DEV, rows, D)
    dmod_all = jnp.concatenate([packs[:, 16, :], packs[:, 17, :], packs[:, 11, :]], axis=1)
    dmodf_all = jnp.concatenate([packs[:, 8, :], packs[:, 9, :]], axis=1)
    loss = tot[13, 0]
    mine_in, theirs_in = _exchange_wait_call(st_pi, to_sibling, _whole_piece, tot, "swapped_w_in")
    mine_out, theirs_out = _exchange_wait_call(st_po, to_sibling, _whole_piece, tot, "swapped_w_out")
    small = {
        "b_ada": jnp.concatenate([tot[16:17], tot[17:18], tot[11:12]], axis=1),
        "norm_g": tot[18:19],
        "ln_v_g": tot[24:25, :D_A],
        "ln_v_b": tot[25:26, :D_A],
        "w_spatial": packs[:, 0:8, :].reshape(GROUPS * BLK, BLK),
        "b_spatial": tot[24:32, D_A:D_A + BLK],
        "sinks": tot[24:25, D_A + LANE:D_A + LANE + N_Q],
        "b_ada_final": jnp.concatenate([tot[8:9], tot[9:10]], axis=1),
        "final_norm_g": tot[10:11],
    }

    weights = dict(w_ada=w_ada, b_ada=b_ada, norm_g=norm_g, w_in=w_in, ln_v_g=ln_v_g, ln_v_b=ln_v_b, w_spatial=w_spatial,
                   b_spatial=b_spatial, sinks=sinks, w_out=w_out, w_ada_final=w_ada_final, b_ada_final=b_ada_final,
                   final_norm_g=final_norm_g)
    m_in = dict(w_ada=m_w_ada, b_ada=m_b_ada, norm_g=m_norm_g, w_in=m_w_in, ln_v_g=m_ln_v_g, ln_v_b=m_ln_v_b,
                w_spatial=m_w_spatial, b_spatial=m_b_spatial, sinks=m_sinks, w_out=m_w_out, w_ada_final=m_w_ada_final,
                b_ada_final=m_b_ada_final, final_norm_g=m_final_norm_g)
    v_in = dict(w_ada=v_w_ada, b_ada=v_b_ada, norm_g=v_norm_g, w_in=v_w_in, ln_v_g=v_ln_v_g, ln_v_b=v_ln_v_b,
                w_spatial=v_w_spatial, b_spatial=v_b_spatial, sinks=v_sinks, w_out=v_w_out, w_ada_final=v_w_ada_final,
                b_ada_final=v_b_ada_final, final_norm_g=v_final_norm_g)
    c_act_t = c_act.T
    outer = {"w_ada": lax.dynamic_slice(dmod_all, (0, chip * n_ada), (N_DEV, n_ada)),
             "w_ada_final": lax.dynamic_slice(dmodf_all, (0, chip * n_adaf), (N_DEV, n_adaf))}
    halves = {"w_in": (mine_in, theirs_in[0]), "w_out": (mine_out, theirs_out[0])}
    done = {}
    for name, (mine, theirs) in halves.items():
        shape2 = (2 * mine.shape[0], mine.shape[1])
        done[name] = _adam_halves_call(pos, weights[name].reshape(shape2), mine, theirs, m_in[name].reshape(shape2),
                                       v_in[name].reshape(shape2), "adam_" + name)
    for name, dm in outer.items():
        shape2 = (D, dm.shape[1])
        done[name] = _adam_outer_call(weights[name].reshape(shape2), c_act_t, dm, m_in[name].reshape(shape2),
                                      v_in[name].reshape(shape2), "adam_" + name)
    updates = _adam_small_call([(weights[name].reshape(g.shape), g, m_in[name].reshape(g.shape), v_in[name].reshape(g.shape))
                                for name, g in small.items()])
    for (name, g), upd in zip(small.items(), updates):
        done[name] = (g, *upd)
    outs = [[done[name][k].reshape(w.shape) for name, w in weights.items()] for k in range(4)]
    return (loss, grad_x.reshape(x.shape), *outs[0], *outs[1], *outs[2], *outs[3])
```

```python
import jax
import jax.numpy as jnp
from jax import lax
from jax.experimental import pallas as pl
from jax.experimental.pallas import tpu as pltpu

F32 = jnp.float32
BF16 = jnp.bfloat16
MESH = pl.DeviceIdType.MESH

D = 2048
D_A = 1024
D_B = 1024
D_KV = 256
HEAD = 64
N_Q = 16
N_KV = 4
Q_PER_KV = N_Q // N_KV
BLK = 128
GROUPS = 8
D_IN = 5632
OFF_Q, OFF_K, OFF_V, OFF_ZB = 3072, 4096, 4352, 4608
N_CHIP = 4
N_DEV = 8
W_IN_SHARD = D_IN // N_CHIP
W_OUT_SHARD = D // N_CHIP
EPS = 1e-5
SCALE = HEAD ** -0.5
NEG = -1e30
LANE = 128
VMEM_LIMIT = 56 * 1024 * 1024

ADAM_LR, ADAM_B1, ADAM_B2, ADAM_EPS, ADAM_WD, ADAM_STEP = 0.001, 0.9, 0.999, 1e-08, 0.01, 10
ADAM_C1 = 1.0 - ADAM_B1 ** ADAM_STEP
ADAM_C2 = 1.0 - ADAM_B2 ** ADAM_STEP
ADAM_ROWS = 256

NT = (((1,), (1,)), ((), ()))
TN = (((0,), (0,)), ((), ()))


def _params(*sem):
    return pltpu.CompilerParams(dimension_semantics=sem, vmem_limit_bytes=VMEM_LIMIT)


def _silu_parts(z):
    sig = 1.0 / (1.0 + jnp.exp(-z))
    return z * sig, sig


def _swap_halves(v, first_half):
    return jnp.where(first_half, pltpu.roll(v, 96, 1), pltpu.roll(v, 32, 1))


def _rope(v, cos_t, sin_s, first_half):
    return v * cos_t + _swap_halves(v, first_half) * sin_s


def _unrope(dv, cos_t, sin_s, first_half):
    return dv * cos_t - _swap_halves(dv, first_half) * sin_s


def _lane_masks():
    lane = lax.broadcasted_iota(jnp.int32, (BLK, LANE), 1)
    return (lane % HEAD) < (HEAD // 2), lane < HEAD


def _band_valid(first_block_bound, rows=BLK):
    rr = lax.broadcasted_iota(jnp.int32, (rows, 2 * BLK), 0) & (BLK - 1)
    jj = lax.broadcasted_iota(jnp.int32, (rows, 2 * BLK), 1)
    return (jj > rr) & (jj <= rr + BLK) & (jj >= first_block_bound)


def _dup_kv(slab, lo):
    rolled = pltpu.roll(slab, HEAD, 1)
    return jnp.where(lo, slab, rolled).astype(BF16), jnp.where(lo, rolled, slab).astype(BF16)


def _stack_heads(ref, sb, slab, lo, dtype):
    kh, base = sb // 2, 2 * (sb % 2) * BLK
    zero = jnp.zeros_like(slab)
    ref[kh, base:base + BLK, :] = jnp.where(lo, slab, zero).astype(dtype)
    ref[kh, base + BLK:base + 2 * BLK, :] = jnp.where(lo, zero, slab).astype(dtype)


def _unstack_heads(ref, sb, lo):
    kh, base = sb // 2, 2 * (sb % 2) * BLK
    return jnp.where(lo, ref[kh, base:base + BLK, :], ref[kh, base + BLK:base + 2 * BLK, :])


def _sink_column(sinks_ref, kh):
    row = lax.broadcasted_iota(jnp.int32, (Q_PER_KV * BLK, 1), 0)
    col = jnp.full(row.shape, sinks_ref[Q_PER_KV * kh + Q_PER_KV - 1], F32)
    for n in range(Q_PER_KV - 2, -1, -1):
        col = jnp.where(row < (n + 1) * BLK, sinks_ref[Q_PER_KV * kh + n], col)
    return col


def _tril():
    t = lax.broadcasted_iota(jnp.int32, (BLK, BLK), 0)
    s = lax.broadcasted_iota(jnp.int32, (BLK, BLK), 1)
    return s <= t


def _layer_norm_fwd(va, lg, lb):
    mu = jnp.mean(va, axis=-1, keepdims=True)
    xc = va - mu
    rstd = lax.rsqrt(jnp.mean(xc * xc, axis=-1, keepdims=True) + EPS)
    vhat = xc * rstd
    return vhat, rstd, vhat * lg + lb


def _softmax_sink(qm, kdup, bias, sink):
    s = lax.dot_general(qm, kdup, NT, preferred_element_type=F32) + bias
    m = jnp.maximum(jnp.max(s, axis=-1, keepdims=True), sink)
    p = jnp.exp(s - m)
    esink = jnp.exp(sink - m)
    inv = 1.0 / (jnp.sum(p, axis=-1, keepdims=True) + esink)
    return p * inv, esink * inv


def _band_bias(bias_ref):
    rows = bias_ref.shape[1]
    bias_ref[0] = jnp.where(_band_valid(BLK, rows), 0.0, NEG)
    bias_ref[1] = jnp.where(_band_valid(0, rows), 0.0, NEG)


def _rowmat_call(c_all, w, b, name):
    n = w.shape[1]
    tn = 512

    def body(c_ref, w_ref, b_ref, o_ref, ca_ref):
        ca, _ = _silu_parts(c_ref[...])
        ca_ref[...] = ca
        o_ref[...] = jnp.dot(ca.astype(BF16), w_ref[...].astype(BF16), preferred_element_type=F32) + b_ref[...]

    return pl.pallas_call(
        body, name=name, grid=(n // tn,),
        in_specs=[pl.BlockSpec((N_DEV, D), lambda j: (0, 0)), pl.BlockSpec((D, tn), lambda j: (0, j)),
                  pl.BlockSpec((1, tn), lambda j: (0, j))],
        out_specs=[pl.BlockSpec((N_DEV, tn), lambda j: (0, j)), pl.BlockSpec((N_DEV, D), lambda j: (0, 0))],
        out_shape=[jax.ShapeDtypeStruct((N_DEV, n), F32), jax.ShapeDtypeStruct((N_DEV, D), F32)],
        compiler_params=_params("arbitrary"),
    )(c_all, w, b)


def _cast_into_call(pos, w, full_shape, name):
    r, n = w.shape
    tr = min(r, 512)
    by_cols = full_shape[0] == r
    nrb = r // tr

    def body(pos_ref, w_ref, o_ref):
        o_ref[...] = w_ref[...].astype(BF16)

    out_map = (lambda i, pos: (i, pos[0])) if by_cols else (lambda i, pos: (pos[0] * nrb + i, 0))
    return pl.pallas_call(
        body, name=name,
        grid_spec=pltpu.PrefetchScalarGridSpec(
            num_scalar_prefetch=1, grid=(nrb,),
            in_specs=[pl.BlockSpec((tr, n), lambda i, pos: (i, 0))], out_specs=pl.BlockSpec((tr, n), out_map)),
        out_shape=jax.ShapeDtypeStruct(full_shape, BF16), compiler_params=_params("parallel"),
    )(pos, w)


def _proj_call(x, shift, scale, norm_g, w_bf):
    s = x.shape[0]
    tm = min(s, 1024)
    tn = 512

    def body(x_ref, sh_ref, sc_ref, g_ref, w_ref, proj_ref, h_ref):
        @pl.when(pl.program_id(1) == 0)
        def _():
            xv = x_ref[...]
            r = lax.rsqrt(jnp.mean(xv * xv, axis=-1, keepdims=True) + EPS)
            h_ref[...] = ((xv * r * g_ref[...]) * (1.0 + sc_ref[...]) + sh_ref[...]).astype(BF16)

        proj_ref[...] = jnp.dot(h_ref[...], w_ref[...], preferred_element_type=F32)

    vec = pl.BlockSpec((1, D), lambda i, j: (0, 0))
    return pl.pallas_call(
        body, name="proj", grid=(s // tm, D_IN // tn),
        in_specs=[pl.BlockSpec((tm, D), lambda i, j: (i, 0)), vec, vec, vec, pl.BlockSpec((D, tn), lambda i, j: (0, j))],
        out_specs=[pl.BlockSpec((tm, tn), lambda i, j: (i, j)), pl.BlockSpec((tm, D), lambda i, j: (i, 0))],
        out_shape=[jax.ShapeDtypeStruct((s, D_IN), F32), jax.ShapeDtypeStruct((s, D), BF16)],
        compiler_params=_params("parallel", "arbitrary"),
    )(x, shift, scale, norm_g, w_bf)


def _proj_gather_whole_shards_call(pos, x, shift, scale, norm_g, wi_full, wo_full):
    s = x.shape[0]
    tm = min(s, 512)
    nrow = s // tm
    hi = D // 2
    ho = W_OUT_SHARD // 2

    def body(pos_ref, x_ref, sh_ref, sc_ref, g_ref, wi_in, wo_in, proj_ref, h_ref, fi_ref, fo_ref,
             h_all, wbuf, send_sems, recv_sems, load_sem):
        del wi_in, wo_in
        p = pl.program_id(0)
        i = pl.program_id(1)
        x_, y_, c_ = _coords()
        me, sibling = (x_, y_, c_), (x_, y_, 1 - c_)

        def shard_of(q):
            px, py, _ = _peer(x_, y_, c_, q, 0)
            return 2 * px + py

        def part(which, q, pc, sub=None):
            n = hi if which == 0 else ho
            base = pc * n
            if sub is not None:
                n //= 2
                base = base + sub * n
            if which == 0:
                return fi_ref.at[pl.ds(base, n), pl.ds(shard_of(q) * W_IN_SHARD, W_IN_SHARD)]
            return fo_ref.at[pl.ds(shard_of(q) * W_OUT_SHARD + base, n), :]

        def copy(k, which, q, pc, to, sub=None):
            ref = part(which, q, pc, sub)
            return pltpu.make_async_remote_copy(src_ref=ref, dst_ref=ref, send_sem=send_sems.at[k], recv_sem=recv_sems.at[k],
                                                device_id=to, device_id_type=MESH)

        def to_neighbour(which, q):
            return copy(8 * which + q - 1, which, 0, c_, _peer(x_, y_, c_, q, 0))

        def from_neighbour(which, q):
            return copy(8 * which + q - 1, which, q, c_, me)

        def relay(which, q):
            return copy(8 * which + 2 + q - 1, which, q, c_, _peer(x_, y_, c_, 3 - q, 0), q - 1)

        def relayed(which, sub):
            return copy(8 * which + 2 + sub, which, 3, c_, me, sub)

        def to_sibling(which, q):
            return copy(8 * which + 4 + q - 1, which, q, c_, sibling)

        def from_sibling(which, q):
            return copy(8 * which + 4 + q - 1, which, q, 1 - c_, me)

        def relayed_to_sibling(which, sub):
            return copy(8 * which + 6 + sub, which, 3, c_, sibling, sub)

        def relayed_from_sibling(which, sub):
            return copy(8 * which + 6 + sub, which, 3, 1 - c_, me, sub)

        def pass_on_neighbours(which):
            for q in (1, 2):
                from_neighbour(which, q).wait_recv()
                to_sibling(which, q).start()
                relay(which, q).start()

        def pass_on_relayed(which):
            for sub in range(2):
                relayed(which, sub).wait_recv()
                relayed_to_sibling(which, sub).start()

        def load_shard(q):
            cp = pltpu.make_async_copy(fi_ref.at[:, pl.ds(shard_of(q) * W_IN_SHARD, W_IN_SHARD)], wbuf, load_sem)
            cp.start()
            cp.wait()

        @pl.when((p == 0) & (i == 0))
        def _():
            for q in (1, 2):
                to_neighbour(0, q).start()
            load_shard(0)

        @pl.when((p == 1) & (i == 0))
        def _():
            pass_on_neighbours(0)
            for q in (1, 2):
                to_neighbour(1, q).start()
            from_sibling(0, 1).wait_recv()
            load_shard(1)

        @pl.when((p == 2) & (i == 0))
        def _():
            from_sibling(0, 2).wait_recv()
            load_shard(2)

        @pl.when((p == 3) & (i == 0))
        def _():
            pass_on_relayed(0)
            pass_on_neighbours(1)
            for sub in range(2):
                relayed_from_sibling(0, sub).wait_recv()
            load_shard(3)

        rows = pl.ds(pl.multiple_of(i * tm, tm), tm)

        @pl.when(p == 0)
        def _():
            xv = x_ref[...]
            r = lax.rsqrt(jnp.mean(xv * xv, axis=-1, keepdims=True) + EPS)
            hv = ((xv * r * g_ref[...]) * (1.0 + sc_ref[...]) + sh_ref[...]).astype(BF16)
            h_ref[...] = hv
            h_all[rows, :] = hv

        proj_ref[...] = jnp.dot(h_all[rows, :], wbuf[...], preferred_element_type=F32)

        @pl.when((p == N_CHIP - 1) & (i == nrow - 1))
        def _():
            pass_on_relayed(1)
            for q in (1, 2):
                from_sibling(1, q).wait_recv()
            for sub in range(2):
                relayed_from_sibling(1, sub).wait_recv()
            for which in range(2):
                for q in (1, 2):
                    to_neighbour(which, q).wait_send()
                    relay(which, q).wait_send()
                    to_sibling(which, q).wait_send()
                    relayed_to_sibling(which, q - 1).wait_send()

    vec = pl.BlockSpec((1, D), lambda p, i, pos: (0, 0))
    first_phase_rows = lambda p, i, pos: (jnp.where(p == 0, i, nrow - 1), 0)
    anyspec = pl.BlockSpec(memory_space=pl.ANY)
    return pl.pallas_call(
        body, name="proj_gather",
        grid_spec=pltpu.PrefetchScalarGridSpec(
            num_scalar_prefetch=1, grid=(N_CHIP, nrow),
            in_specs=[pl.BlockSpec((tm, D), first_phase_rows), vec, vec, vec, anyspec, anyspec],
            out_specs=[pl.BlockSpec((tm, W_IN_SHARD), lambda p, i, pos: (i, jnp.bitwise_xor(pos[0], p))),
                       pl.BlockSpec((tm, D), first_phase_rows), anyspec, anyspec],
            scratch_shapes=[pltpu.VMEM((s, D), BF16), pltpu.VMEM((D, W_IN_SHARD), BF16),
                            pltpu.SemaphoreType.DMA((16,)), pltpu.SemaphoreType.DMA((16,)), pltpu.SemaphoreType.DMA]),
        out_shape=[jax.ShapeDtypeStruct((s, D_IN), F32), jax.ShapeDtypeStruct((s, D), BF16),
                   jax.ShapeDtypeStruct((D, D_IN), BF16), jax.ShapeDtypeStruct((D, D), BF16)],
        input_output_aliases={5: 2, 6: 3},
        compiler_params=_params("arbitrary", "arbitrary"),
    )(pos, x, shift, scale, norm_g, wi_full, wo_full)


W_IN_PARTS = ((0, 768), (768, 640))
OUT_STREAMS = 4


def _proj_gather_call(pos, x, shift, scale, norm_g, wi_full, wo_full):
    s = x.shape[0]
    tm = min(s, 512)
    nrow = s // tm
    hi = D // 2
    ho = W_OUT_SHARD // 2
    phases = [(0, None), (1, 0), (2, 0), (1, 1), (2, 1), (3, 0), (3, 1)]

    def body(pos_ref, x_ref, sh_ref, sc_ref, g_ref, wi_in, wo_in, h_ref, proj_ref, fi_ref, fo_ref,
             h_all, wbuf, obuf, send_sems, recv_sems, load_sems, out_sems):
        del wi_in, wo_in
        p = pl.program_id(0)
        i = pl.program_id(1)
        x_, y_, c_ = _coords()
        me, sibling = (x_, y_, c_), (x_, y_, 1 - c_)

        def shard_of(q):
            px, py, _ = _peer(x_, y_, c_, q, 0)
            return 2 * px + py

        def cols_of(q, cp):
            off, w = (0, W_IN_SHARD) if cp is None else W_IN_PARTS[cp]
            return shard_of(q) * W_IN_SHARD + off, w

        def part(which, q, pc, sub, cp):
            n = hi if which == 0 else ho
            base = pc * n
            if sub is not None:
                n //= 2
                base = base + sub * n
            if which == 0:
                c0, w = cols_of(q, cp)
                return fi_ref.at[pl.ds(base, n), pl.ds(c0, w)]
            return fo_ref.at[pl.ds(shard_of(q) * W_OUT_SHARD + base, n), :]

        def copy(k, ref, to):
            return pltpu.make_async_remote_copy(src_ref=ref, dst_ref=ref, send_sem=send_sems.at[k], recv_sem=recv_sems.at[k],
                                                device_id=to, device_id_type=MESH)

        def sem(which, kind, j, cp):
            return 4 * kind + 2 * cp + j if which == 0 else 16 + 2 * kind + j

        def to_neighbour(which, q, cp=None):
            return copy(sem(which, 0, q - 1, cp), part(which, 0, c_, None, cp), _peer(x_, y_, c_, q, 0))

        def from_neighbour(which, q, cp=None):
            return copy(sem(which, 0, q - 1, cp), part(which, q, c_, None, cp), me)

        def relay(which, q, cp=None):
            return copy(sem(which, 1, q - 1, cp), part(which, q, c_, q - 1, cp), _peer(x_, y_, c_, 3 - q, 0))

        def relayed(which, sub, cp=None):
            return copy(sem(which, 1, sub, cp), part(which, 3, c_, sub, cp), me)

        def to_sibling(which, q, cp=None):
            return copy(sem(which, 2, q - 1, cp), part(which, q, c_, None, cp), sibling)

        def from_sibling(which, q, cp=None):
            return copy(sem(which, 2, q - 1, cp), part(which, q, 1 - c_, None, cp), me)

        def relayed_to_sibling(which, sub, cp=None):
            return copy(sem(which, 3, sub, cp), part(which, 3, c_, sub, cp), sibling)

        def relayed_from_sibling(which, sub, cp=None):
            return copy(sem(which, 3, sub, cp), part(which, 3, 1 - c_, sub, cp), me)

        def pass_on_neighbours(which, cp=None):
            for q in (1, 2):
                from_neighbour(which, q, cp).wait_recv()
                to_sibling(which, q, cp).start()
                relay(which, q, cp).start()

        def pass_on_relayed(which, cp=None):
            for sub in range(2):
                relayed(which, sub, cp).wait_recv()
                relayed_to_sibling(which, sub, cp).start()

        def shard_load(k):
            c0, w = cols_of(*phases[k])
            return pltpu.make_async_copy(fi_ref.at[:, pl.ds(c0, w)], wbuf.at[k % 2, :, 0:w], load_sems.at[k % 2])

        class OutCopies:
            def __init__(self, k, slot, row0):
                c0, w = cols_of(*phases[k])
                strip = tm // OUT_STREAMS
                self.copies = [pltpu.make_async_copy(obuf.at[slot, n * strip:(n + 1) * strip, 0:w],
                                                     proj_ref.at[pl.ds(row0 + n * strip, strip), pl.ds(c0, w)],
                                                     out_sems.at[slot, n]) for n in range(OUT_STREAMS)]

            def start(self):
                for cp in self.copies:
                    cp.start()

            def wait(self):
                for cp in self.copies:
                    cp.wait()

        out_copy = OutCopies

        def drain(k):
            for j in range(min(2, nrow)):
                out_copy(k, (nrow - 1 - j) % 2, 0).wait()

        def arrivals(k):
            q, cp = phases[k]
            if k == 0:
                for cp_ in range(2):
                    for q_ in (1, 2):
                        to_neighbour(0, q_, cp_).start()
            elif q < 3 and k in (1, 3):
                pass_on_neighbours(0, cp)
                if k == 1:
                    for q_ in (1, 2):
                        to_neighbour(1, q_).start()
            elif k == 5:
                for cp_ in range(2):
                    pass_on_relayed(0, cp_)
                pass_on_neighbours(1)
            if q in (1, 2):
                from_sibling(0, q, cp).wait_recv()
            elif q == 3:
                for sub in range(2):
                    relayed_from_sibling(0, sub, cp).wait_recv()

        rows = pl.ds(pl.multiple_of(i * tm, tm), tm)
        slot = i % 2
        for k, (q, cp) in enumerate(phases):
            @pl.when(p == k)
            def _(k=k, q=q, cp=cp):
                @pl.when(i == 0)
                def _():
                    if k == 0:
                        arrivals(0)
                        shard_load(0).start()
                    else:
                        drain(k - 1)
                    shard_load(k).wait()

                if k + 1 < len(phases):
                    @pl.when(i == max(nrow - 2, 0))
                    def _():
                        arrivals(k + 1)
                        shard_load(k + 1).start()

                if k == 0:
                    xv = x_ref[...]
                    r = lax.rsqrt(jnp.mean(xv * xv, axis=-1, keepdims=True) + EPS)
                    hv = ((xv * r * g_ref[...]) * (1.0 + sc_ref[...]) + sh_ref[...]).astype(BF16)
                    h_ref[...] = hv
                    h_all[rows, :] = hv

                @pl.when(i >= 2)
                def _():
                    out_copy(k, slot, 0).wait()

                w = cols_of(q, cp)[1]
                obuf[slot, :, 0:w] = jnp.dot(h_all[rows, :], wbuf[k % 2, :, 0:w], preferred_element_type=F32)
                out_copy(k, slot, pl.multiple_of(i * tm, tm)).start()

        @pl.when((p == len(phases) - 1) & (i == nrow - 1))
        def _():
            drain(len(phases) - 1)
            pass_on_relayed(1)
            for q in (1, 2):
                from_sibling(1, q).wait_recv()
            for sub in range(2):
                relayed_from_sibling(1, sub).wait_recv()
            for which, cps in ((0, (0, 1)), (1, (None,))):
                for cp in cps:
                    for q in (1, 2):
                        to_neighbour(which, q, cp).wait_send()
                        relay(which, q, cp).wait_send()
                        to_sibling(which, q, cp).wait_send()
                        relayed_to_sibling(which, q - 1, cp).wait_send()

    vec = pl.BlockSpec((1, D), lambda p, i, pos: (0, 0))
    first_phase_rows = lambda p, i, pos: (jnp.where(p == 0, i, nrow - 1), 0)
    anyspec = pl.BlockSpec(memory_space=pl.ANY)
    return pl.pallas_call(
        body, name="proj_gather",
        grid_spec=pltpu.PrefetchScalarGridSpec(
            num_scalar_prefetch=1, grid=(len(phases), nrow),
            in_specs=[pl.BlockSpec((tm, D), first_phase_rows), vec, vec, vec, anyspec, anyspec],
            out_specs=[pl.BlockSpec((tm, D), first_phase_rows), anyspec, anyspec, anyspec],
            scratch_shapes=[pltpu.VMEM((s, D), BF16), pltpu.VMEM((2, D, W_IN_SHARD), BF16), pltpu.VMEM((2, tm, W_IN_SHARD), F32),
                            pltpu.SemaphoreType.DMA((24,)), pltpu.SemaphoreType.DMA((24,)), pltpu.SemaphoreType.DMA((2,)),
                            pltpu.SemaphoreType.DMA((2, OUT_STREAMS))]),
        out_shape=[jax.ShapeDtypeStruct((s, D), BF16), jax.ShapeDtypeStruct((s, D_IN), F32),
                   jax.ShapeDtypeStruct((D, D_IN), BF16), jax.ShapeDtypeStruct((D, D), BF16)],
        input_output_aliases={5: 2, 6: 3},
        compiler_params=_params("arbitrary", "arbitrary"),
    )(pos, x, shift, scale, norm_g, wi_full, wo_full)


def _proj_specs(rev_nb=None):
    if rev_nb is None:
        row = lambda i: i
    else:
        row = lambda i: rev_nb - 1 - i
    wide = lambda col: pl.BlockSpec((BLK, D_A), lambda i: (row(i), col))
    kv = lambda col: pl.BlockSpec((BLK, D_KV), lambda i: (row(i), col))
    half = lambda col: pl.BlockSpec((BLK, 512), lambda i: (row(i), col))
    return [wide(0), wide(1), wide(2), wide(3), kv(OFF_K // D_KV), kv(OFF_V // D_KV), half(OFF_ZB // 512), half(OFF_ZB // 512 + 1)]


def _mix_fwd_call(proj, cos, sin, ln_g, ln_b, w_sp, b_sp_t, sinks):
    s = proj.shape[0]
    nb = s // BLK

    def body(ua_ref, va_ref, za_ref, q_ref, k_ref, v_ref, zb0_ref, zb1_ref, cos_ref, sin_ref, lg_ref, lb_ref,
             w_ref, bt_ref, sinks_ref, y_ref, kdup_ref, vdup_ref, qm_ref, ost_ref, bias_ref):
        i = pl.program_id(0)
        first_half, lo = _lane_masks()
        cos_t = cos_ref[...]
        sin_t = sin_ref[...]

        _, _, vln = _layer_norm_fwd(va_ref[...], lg_ref[...], lb_ref[...])
        tril = _tril()
        for g in range(GROUPS):
            cols = slice(g * BLK, (g + 1) * BLK)
            wg = jnp.where(tril, w_ref[g], 0.0).astype(BF16)
            sg = jnp.dot(wg, vln[:, cols].astype(BF16), preferred_element_type=F32) + bt_ref[:, g:g + 1]
            gate, _ = _silu_parts(za_ref[:, cols])
            y_ref[:, cols] = (ua_ref[:, cols] * sg * gate).astype(BF16)

        @pl.when(i == 0)
        def _():
            kdup_ref[:, 0:BLK, :] = jnp.zeros((N_KV, BLK, LANE), BF16)
            vdup_ref[:, 0:BLK, :] = jnp.zeros((N_KV, BLK, LANE), BF16)
            _band_bias(bias_ref)

        @pl.when(i > 0)
        def _():
            kdup_ref[:, 0:BLK, :] = kdup_ref[:, BLK:2 * BLK, :]
            vdup_ref[:, 0:BLK, :] = vdup_ref[:, BLK:2 * BLK, :]

        for ks in range(2):
            cols = slice(ks * LANE, (ks + 1) * LANE)
            kr = _rope(k_ref[:, cols], cos_t, sin_t, first_half)
            for n, (kd, vd) in enumerate(zip(_dup_kv(kr, lo), _dup_kv(v_ref[:, cols], lo))):
                kdup_ref[2 * ks + n, BLK:2 * BLK, :] = kd
                vdup_ref[2 * ks + n, BLK:2 * BLK, :] = vd
        for sb in range(8):
            _stack_heads(qm_ref, sb, _rope(q_ref[:, sb * LANE:(sb + 1) * LANE], cos_t, sin_t, first_half) * SCALE, lo, BF16)

        block_kind = jnp.where(i > 0, 1, 0)

        def kv_head(kh, carry):
            probs, _ = _softmax_sink(qm_ref[kh], kdup_ref[kh], bias_ref[block_kind], _sink_column(sinks_ref, kh))
            ost_ref[kh] = jnp.dot(probs.astype(BF16), vdup_ref[kh], preferred_element_type=F32)
            return carry

        lax.fori_loop(0, N_KV, kv_head, 0, unroll=2)
        for sb in range(8):
            cols = slice(sb * LANE, (sb + 1) * LANE)
            zb = zb0_ref[:, cols] if sb < 4 else zb1_ref[:, (sb - 4) * LANE:(sb - 3) * LANE]
            gate, _ = _silu_parts(zb)
            y_ref[:, D_A + sb * LANE:D_A + (sb + 1) * LANE] = (_unstack_heads(ost_ref, sb, lo) * gate).astype(BF16)

    tab = pl.BlockSpec((BLK, LANE), lambda i: (i, 0))
    return pl.pallas_call(
        body, name="mix_fwd", grid=(nb,),
        in_specs=_proj_specs() + [
            tab, tab, pl.BlockSpec((1, D_A), lambda i: (0, 0)), pl.BlockSpec((1, D_A), lambda i: (0, 0)),
            pl.BlockSpec((GROUPS, BLK, BLK), lambda i: (0, 0, 0)), pl.BlockSpec((BLK, GROUPS), lambda i: (0, 0)),
            pl.BlockSpec(memory_space=pltpu.SMEM)],
        out_specs=pl.BlockSpec((BLK, 2 * D_A), lambda i: (i, 0)),
        out_shape=jax.ShapeDtypeStruct((s, 2 * D_A), BF16),
        scratch_shapes=[pltpu.VMEM((N_KV, 2 * BLK, LANE), BF16), pltpu.VMEM((N_KV, 2 * BLK, LANE), BF16),
                        pltpu.VMEM((N_KV, Q_PER_KV * BLK, LANE), BF16), pltpu.VMEM((N_KV, Q_PER_KV * BLK, LANE), F32),
                        pltpu.VMEM((2, Q_PER_KV * BLK, 2 * BLK), F32)],
        compiler_params=_params("arbitrary"),
    )(proj, proj, proj, proj, proj, proj, proj, proj, cos, sin, ln_g, ln_b, w_sp, b_sp_t, sinks)


def _tail_call(y, w_out_bf, x, target, gate, shift_f, scale_f, gf):
    s = x.shape[0]
    tm = min(s, 256)
    nsteps = s // tm

    def body(y_ref, w_ref, x_ref, t_ref, gate_ref, shf_ref, scf_ref, gf_ref, dx2_ref, do_ref, dy_ref, st_ref):
        i = pl.program_id(0)

        @pl.when(i == 0)
        def _():
            st_ref[...] = jnp.zeros((8, D), F32)

        o = jnp.dot(y_ref[...], w_ref[...], preferred_element_type=F32)
        gate_v = gate_ref[...]
        x2 = x_ref[...] + gate_v * o
        r2 = lax.rsqrt(jnp.mean(x2 * x2, axis=-1, keepdims=True) + EPS)
        xn2 = x2 * r2
        hn2 = xn2 * gf_ref[...]
        one_sc = 1.0 + scf_ref[...]
        err = hn2 * one_sc + shf_ref[...] - t_ref[...]
        dout = err * (1.0 / D)
        dhn2 = dout * one_sc
        dxn2 = dhn2 * gf_ref[...]
        dx2 = r2 * (dxn2 - xn2 * jnp.mean(dxn2 * xn2, axis=-1, keepdims=True))
        dx2_ref[...] = dx2
        do = (dx2 * gate_v).astype(BF16)
        do_ref[...] = do
        dy_ref[...] = lax.dot_general(do, w_ref[...], NT, preferred_element_type=F32)
        st_ref[0:1, :] += jnp.sum(dout, axis=0, keepdims=True)
        st_ref[1:2, :] += jnp.sum(dout * hn2, axis=0, keepdims=True)
        st_ref[2:3, :] += jnp.sum(dhn2 * xn2, axis=0, keepdims=True)
        st_ref[3:4, :] += jnp.sum(dx2 * o, axis=0, keepdims=True)
        st_ref[4:5, :] += jnp.sum(err * err, axis=0, keepdims=True)

        @pl.when(i == nsteps - 1)
        def _():
            st_ref[5:6, :] = jnp.full((1, D), 0.5 / D, F32) * jnp.sum(st_ref[4:5, :])

    vec = pl.BlockSpec((1, D), lambda i: (0, 0))
    rows = lambda: pl.BlockSpec((tm, D), lambda i: (i, 0))
    return pl.pallas_call(
        body, name="tail", grid=(nsteps,),
        in_specs=[rows(), pl.BlockSpec((D, D), lambda i: (0, 0)), rows(), rows(), vec, vec, vec, vec],
        out_specs=[rows(), rows(), rows(), pl.BlockSpec((8, D), lambda i: (0, 0))],
        out_shape=[jax.ShapeDtypeStruct((s, D), F32), jax.ShapeDtypeStruct((s, D), BF16), jax.ShapeDtypeStruct((s, D), F32),
                   jax.ShapeDtypeStruct((8, D), F32)],
        compiler_params=_params("arbitrary"),
    )(y, w_out_bf, x, target, gate, shift_f, scale_f, gf)


def _tn_call(a, b, name):
    s, m = a.shape
    n = b.shape[1]
    tn = 512
    ts = min(s, 1024)
    nk = s // ts

    def body(a_ref, b_ref, o_ref, acc_ref):
        k = pl.program_id(1)

        @pl.when(k == 0)
        def _():
            acc_ref[...] = jnp.zeros((m, tn), F32)

        acc_ref[...] += lax.dot_general(a_ref[...], b_ref[...], TN, preferred_element_type=F32)

        @pl.when(k == nk - 1)
        def _():
            o_ref[...] = acc_ref[...].astype(BF16)

    return pl.pallas_call(
        body, name=name, grid=(n // tn, nk),
        in_specs=[pl.BlockSpec((ts, m), lambda j, k: (k, 0)), pl.BlockSpec((ts, tn), lambda j, k: (k, j))],
        out_specs=pl.BlockSpec((m, tn), lambda j, k: (0, j)),
        out_shape=jax.ShapeDtypeStruct((m, n), BF16),
        scratch_shapes=[pltpu.VMEM((m, tn), F32)],
        compiler_params=_params("parallel", "arbitrary"),
    )(a, b)


def _tn_shards_call(pos, a, b, qs, name):
    s, m = a.shape
    ts = min(s, 1024)
    nk = s // ts

    def body(pos_ref, a_ref, b_ref, o_ref, acc_ref):
        k = pl.program_id(1)

        @pl.when(k == 0)
        def _():
            acc_ref[...] = jnp.zeros((m, W_IN_SHARD), F32)

        acc_ref[...] += lax.dot_general(a_ref[...], b_ref[...], TN, preferred_element_type=F32)

        @pl.when(k == nk - 1)
        def _():
            o_ref[...] = acc_ref[...].astype(BF16)

    def shard(j, pos):
        q = qs[0]
        for n in range(1, len(qs)):
            q = jnp.where(j == n, qs[n], q)
        return jnp.bitwise_xor(pos[0], q)

    return pl.pallas_call(
        body, name=name,
        grid_spec=pltpu.PrefetchScalarGridSpec(
            num_scalar_prefetch=1, grid=(len(qs), nk),
            in_specs=[pl.BlockSpec((ts, m), lambda j, k, pos: (k, 0)),
                      pl.BlockSpec((ts, W_IN_SHARD), lambda j, k, pos: (k, shard(j, pos)))],
            out_specs=pl.BlockSpec((m, W_IN_SHARD), lambda j, k, pos: (0, j)),
            scratch_shapes=[pltpu.VMEM((m, W_IN_SHARD), F32)]),
        out_shape=jax.ShapeDtypeStruct((m, len(qs) * W_IN_SHARD), BF16),
        compiler_params=_params("parallel", "arbitrary"),
    )(pos, a, b)


def _mix_bwd_call(proj, dy, cos, sin, ln_g, ln_b, w_sp, w_sp_t, b_sp_t, sinks):
    s = proj.shape[0]
    nb = s // BLK
    rev = lambda i: nb - 1 - i
    prev = lambda i: jnp.maximum(nb - 2 - i, 0)

    def body(ua_ref, va_ref, za_ref, q_ref, k_ref, v_ref, zb0_ref, zb1_ref, kp_ref, vp_ref, dy_ref,
             cos_ref, sin_ref, cosp_ref, sinp_ref, lg_ref, lb_ref, w_ref, wt_ref, bt_ref, sinks_ref,
             dp_ref, lnst_ref, dw_ref, dbt_ref, dsink_ref,
             kdup_ref, vdup_ref, dvln_ref, qm_ref, dom_ref, ost_ref, dqst_ref, dkdup_ref, dvdup_ref, kcar_ref, vcar_ref,
             sigb_ref, bias_ref):
        i = pl.program_id(0)
        first_half, lo = _lane_masks()
        lane8 = lax.broadcasted_iota(jnp.int32, (8, LANE), 1)
        cos_t = cos_ref[...]
        sin_t = sin_ref[...]

        @pl.when(i == 0)
        def _():
            lnst_ref[...] = jnp.zeros((8, D_A), F32)
            dw_ref[...] = jnp.zeros((GROUPS, BLK, BLK), F32)
            dbt_ref[...] = jnp.zeros((BLK, LANE), F32)
            dsink_ref[...] = jnp.zeros((8, LANE), F32)
            kcar_ref[...] = jnp.zeros((BLK, D_KV), F32)
            vcar_ref[...] = jnp.zeros((BLK, D_KV), F32)
            _band_bias(bias_ref)

        vhat, rstd, vln = _layer_norm_fwd(va_ref[...], lg_ref[...], lb_ref[...])
        tril = _tril()
        triu = jnp.logical_not(tril) | (lax.broadcasted_iota(jnp.int32, (BLK, BLK), 0) == lax.broadcasted_iota(jnp.int32, (BLK, BLK), 1))
        lane_b = lax.broadcasted_iota(jnp.int32, (BLK, LANE), 1)
        db_acc = jnp.zeros((BLK, LANE), F32)
        for g in range(GROUPS):
            cols = slice(g * BLK, (g + 1) * BLK)
            vln_g = vln[:, cols].astype(BF16)
            wg = jnp.where(tril, w_ref[g], 0.0).astype(BF16)
            sg = jnp.dot(wg, vln_g, preferred_element_type=F32) + bt_ref[:, g:g + 1]
            za = za_ref[:, cols]
            gate, sig = _silu_parts(za)
            ua = ua_ref[:, cols]
            dya_g = dy_ref[:, cols]
            dya = dya_g * gate
            dp_ref[:, cols] = (dya * sg).astype(BF16)
            dp_ref[:, 2 * D_A + g * BLK:2 * D_A + (g + 1) * BLK] = (
                dya_g * (ua * sg) * (sig * (1.0 + za * (1.0 - sig)))).astype(BF16)
            ds = dya * ua
            ds_b = ds.astype(BF16)
            wtg = jnp.where(triu, wt_ref[g], 0.0).astype(BF16)
            dvln_ref[:, cols] = jnp.dot(wtg, ds_b, preferred_element_type=F32)
            dw_ref[g] += jnp.where(tril, lax.dot_general(ds_b, vln_g, NT, preferred_element_type=F32), 0.0)
            db_acc = db_acc + jnp.where(lane_b == g, jnp.sum(ds, axis=-1, keepdims=True), 0.0)
        dbt_ref[...] += db_acc
        dvln = dvln_ref[...]
        lnst_ref[0:1, :] += jnp.sum(dvln * vhat, axis=0, keepdims=True)
        lnst_ref[1:2, :] += jnp.sum(dvln, axis=0, keepdims=True)
        dvhat = dvln * lg_ref[...]
        m1 = jnp.mean(dvhat, axis=-1, keepdims=True)
        m2 = jnp.mean(dvhat * vhat, axis=-1, keepdims=True)
        dp_ref[:, D_A:2 * D_A] = (rstd * (dvhat - m1 - vhat * m2)).astype(BF16)

        cosp = cosp_ref[...]
        sinp = sinp_ref[...]
        for ks in range(2):
            cols = slice(ks * LANE, (ks + 1) * LANE)
            kr = _rope(k_ref[:, cols], cos_t, sin_t, first_half)
            kpr = _rope(kp_ref[:, cols], cosp, sinp, first_half)
            for n, (kc, vc, kp, vp) in enumerate(zip(_dup_kv(kr, lo), _dup_kv(v_ref[:, cols], lo),
                                                     _dup_kv(kpr, lo), _dup_kv(vp_ref[:, cols], lo))):
                kdup_ref[2 * ks + n, BLK:2 * BLK, :] = kc
                vdup_ref[2 * ks + n, BLK:2 * BLK, :] = vc
                kdup_ref[2 * ks + n, 0:BLK, :] = kp
                vdup_ref[2 * ks + n, 0:BLK, :] = vp
        for sb in range(8):
            cols = slice(sb * LANE, (sb + 1) * LANE)
            _stack_heads(qm_ref, sb, _rope(q_ref[:, cols], cos_t, sin_t, first_half) * SCALE, lo, BF16)
            zb = zb0_ref[:, cols] if sb < 4 else zb1_ref[:, (sb - 4) * LANE:(sb - 3) * LANE]
            gate, sig = _silu_parts(zb)
            sigb_ref[:, cols] = sig
            _stack_heads(dom_ref, sb, dy_ref[:, D_A + sb * LANE:D_A + (sb + 1) * LANE] * gate, lo, F32)

        block_kind = jnp.where(i < nb - 1, 1, 0)

        def kv_head(kh, dsink_acc):
            qm = qm_ref[kh]
            kd = kdup_ref[kh]
            vd = vdup_ref[kh]
            probs, psink = _softmax_sink(qm, kd, bias_ref[block_kind], _sink_column(sinks_ref, kh))
            probs_b = probs.astype(BF16)
            o = jnp.dot(probs_b, vd, preferred_element_type=F32)
            ost_ref[kh] = o
            dom = dom_ref[kh]
            dom_b = dom.astype(BF16)
            delta = jnp.sum(dom * o, axis=-1, keepdims=True)
            dpr = lax.dot_general(dom_b, vd, NT, preferred_element_type=F32)
            dss = (probs * (dpr - delta)).astype(BF16)
            sd = psink * delta
            for n in range(Q_PER_KV):
                dsink_acc = dsink_acc + jnp.where(lane8 == Q_PER_KV * kh + n, -jnp.sum(sd[n * BLK:(n + 1) * BLK]), 0.0)
            dqst_ref[kh] = jnp.dot(dss, kd, preferred_element_type=F32)
            dkdup_ref[kh] = lax.dot_general(dss, qm, TN, preferred_element_type=F32)
            dvdup_ref[kh] = lax.dot_general(probs_b, dom_b, TN, preferred_element_type=F32)
            return dsink_acc

        dsink_acc = lax.fori_loop(0, N_KV // 2, lambda j, acc: kv_head(2 * j + 1, kv_head(2 * j, acc)), jnp.zeros((8, LANE), F32))
        row0 = lax.broadcasted_iota(jnp.int32, (8, LANE), 0) == 0
        dsink_ref[...] += jnp.where(row0, dsink_acc, 0.0)

        for sb in range(8):
            cols = slice(sb * LANE, (sb + 1) * LANE)
            zb = zb0_ref[:, cols] if sb < 4 else zb1_ref[:, (sb - 4) * LANE:(sb - 3) * LANE]
            sig = sigb_ref[:, cols]
            dyb = dy_ref[:, D_A + sb * LANE:D_A + (sb + 1) * LANE]
            dp_ref[:, OFF_ZB + sb * LANE:OFF_ZB + (sb + 1) * LANE] = (
                dyb * _unstack_heads(ost_ref, sb, lo) * (sig * (1.0 + zb * (1.0 - sig)))).astype(BF16)
            dq_r = _unstack_heads(dqst_ref, sb, lo) * SCALE
            dp_ref[:, OFF_Q + sb * LANE:OFF_Q + (sb + 1) * LANE] = _unrope(dq_r, cos_t, sin_t, first_half).astype(BF16)

        lo2 = lax.broadcasted_iota(jnp.int32, (2 * BLK, LANE), 1) < HEAD
        for ks in range(2):
            cols = slice(ks * LANE, (ks + 1) * LANE)
            ka = dkdup_ref[2 * ks]
            kb = dkdup_ref[2 * ks + 1]
            dk_band = jnp.where(lo2, ka + pltpu.roll(ka, HEAD, 1), kb + pltpu.roll(kb, HEAD, 1))
            va_ = dvdup_ref[2 * ks]
            vb_ = dvdup_ref[2 * ks + 1]
            dv_band = jnp.where(lo2, va_ + pltpu.roll(va_, HEAD, 1), vb_ + pltpu.roll(vb_, HEAD, 1))
            dkr = dk_band[BLK:2 * BLK, :] + kcar_ref[:, cols]
            dp_ref[:, OFF_K + ks * LANE:OFF_K + (ks + 1) * LANE] = _unrope(dkr, cos_t, sin_t, first_half).astype(BF16)
            dp_ref[:, OFF_V + ks * LANE:OFF_V + (ks + 1) * LANE] = (
                dv_band[BLK:2 * BLK, :] + vcar_ref[:, cols]).astype(BF16)
            kcar_ref[:, cols] = dk_band[0:BLK, :]
            vcar_ref[:, cols] = dv_band[0:BLK, :]

    tab = pl.BlockSpec((BLK, LANE), lambda i: (rev(i), 0))
    tabp = pl.BlockSpec((BLK, LANE), lambda i: (prev(i), 0))
    kvp = lambda col: pl.BlockSpec((BLK, D_KV), lambda i: (prev(i), col))
    vec = pl.BlockSpec((1, D_A), lambda i: (0, 0))
    w3 = pl.BlockSpec((GROUPS, BLK, BLK), lambda i: (0, 0, 0))
    return pl.pallas_call(
        body, name="mix_bwd", grid=(nb,),
        in_specs=_proj_specs(nb) + [
            kvp(OFF_K // D_KV), kvp(OFF_V // D_KV), pl.BlockSpec((BLK, 2 * D_A), lambda i: (rev(i), 0)),
            tab, tab, tabp, tabp, vec, vec, w3, w3, pl.BlockSpec((BLK, GROUPS), lambda i: (0, 0)),
            pl.BlockSpec(memory_space=pltpu.SMEM)],
        out_specs=[pl.BlockSpec((BLK, D_IN), lambda i: (rev(i), 0)), pl.BlockSpec((8, D_A), lambda i: (0, 0)), w3,
                   pl.BlockSpec((BLK, LANE), lambda i: (0, 0)), pl.BlockSpec((8, LANE), lambda i: (0, 0))],
        out_shape=[jax.ShapeDtypeStruct((s, D_IN), BF16), jax.ShapeDtypeStruct((8, D_A), F32),
                   jax.ShapeDtypeStruct((GROUPS, BLK, BLK), F32), jax.ShapeDtypeStruct((BLK, LANE), F32),
                   jax.ShapeDtypeStruct((8, LANE), F32)],
        scratch_shapes=[pltpu.VMEM((N_KV, 2 * BLK, LANE), BF16), pltpu.VMEM((N_KV, 2 * BLK, LANE), BF16),
                        pltpu.VMEM((BLK, D_A), F32), pltpu.VMEM((N_KV, Q_PER_KV * BLK, LANE), BF16),
                        pltpu.VMEM((N_KV, Q_PER_KV * BLK, LANE), F32), pltpu.VMEM((N_KV, Q_PER_KV * BLK, LANE), F32),
                        pltpu.VMEM((N_KV, Q_PER_KV * BLK, LANE), F32), pltpu.VMEM((N_KV, 2 * BLK, LANE), F32),
                        pltpu.VMEM((N_KV, 2 * BLK, LANE), F32), pltpu.VMEM((BLK, D_KV), F32), pltpu.VMEM((BLK, D_KV), F32),
                        pltpu.VMEM((BLK, D_B), F32), pltpu.VMEM((2, Q_PER_KV * BLK, 2 * BLK), F32)],
        compiler_params=_params("arbitrary"),
    )(proj, proj, proj, proj, proj, proj, proj, proj, proj, proj, dy, cos, sin, cos, sin, ln_g, ln_b, w_sp, w_sp_t,
      b_sp_t, sinks)


def _dh_call(dproj, w_bf, x, dx2, scale, norm_g):
    s = x.shape[0]
    tm = min(s, 512)
    tk = W_IN_SHARD
    nk = D_IN // tk

    def body(dp_ref, w_ref, x_ref, dx2_ref, sc_ref, g_ref, gx_ref, st_ref, acc_ref):
        i = pl.program_id(0)
        k = pl.program_id(1)

        @pl.when((i == 0) & (k == 0))
        def _():
            st_ref[...] = jnp.zeros((8, D), F32)

        @pl.when(k == 0)
        def _():
            acc_ref[...] = jnp.zeros((tm, D), F32)

        acc_ref[...] += lax.dot_general(dp_ref[...], w_ref[...], NT, preferred_element_type=F32)

        @pl.when(k == nk - 1)
        def _():
            g = g_ref[...]
            one_sc = 1.0 + sc_ref[...]

            def chunk(n, carry):
                rows = pl.ds(pl.multiple_of(n * BLK, BLK), BLK)
                dh = acc_ref[rows, :]
                xv = x_ref[rows, :]
                r = lax.rsqrt(jnp.mean(xv * xv, axis=-1, keepdims=True) + EPS)
                xn = xv * r
                dhn = dh * one_sc
                dxn = dhn * g
                gx_ref[rows, :] = dx2_ref[rows, :] + r * (dxn - xn * jnp.mean(dxn * xn, axis=-1, keepdims=True))
                st_ref[0:1, :] += jnp.sum(dh, axis=0, keepdims=True)
                st_ref[1:2, :] += jnp.sum(dh * (xn * g), axis=0, keepdims=True)
                st_ref[2:3, :] += jnp.sum(dhn * xn, axis=0, keepdims=True)
                return carry

            lax.fori_loop(0, tm // BLK, chunk, 0)

    vec = pl.BlockSpec((1, D), lambda i, k: (0, 0))
    rows = lambda: pl.BlockSpec((tm, D), lambda i, k: (i, 0))
    return pl.pallas_call(
        body, name="dh", grid=(s // tm, nk),
        in_specs=[pl.BlockSpec((tm, tk), lambda i, k: (i, k)), pl.BlockSpec((D, tk), lambda i, k: (0, k)), rows(), rows(), vec, vec],
        out_specs=[rows(), pl.BlockSpec((8, D), lambda i, k: (0, 0))],
        out_shape=[jax.ShapeDtypeStruct((s, D), F32), jax.ShapeDtypeStruct((8, D), F32)],
        scratch_shapes=[pltpu.VMEM((tm, D), F32)],
        compiler_params=_params("arbitrary", "arbitrary"),
    )(dproj, w_bf, x, dx2, scale, norm_g)


def _adam_math(w, g, m, v):
    m_new = ADAM_B1 * m + (1.0 - ADAM_B1) * g
    v_new = ADAM_B2 * v + (1.0 - ADAM_B2) * (g * g)
    m_hat = m_new / ADAM_C1
    v_hat = v_new / ADAM_C2
    delta = -ADAM_LR * (m_hat / (jnp.sqrt(v_hat) + ADAM_EPS) + ADAM_WD * w)
    return delta, m_new, v_new


def _adam_small_call(tensors):
    n = len(tensors)

    def body(*refs):
        ins, outs = refs[:4 * n], refs[4 * n:]
        for t in range(n):
            w_ref, g_ref, m_ref, v_ref = ins[4 * t:4 * t + 4]
            d, mo, vo = _adam_math(w_ref[...], g_ref[...], m_ref[...], v_ref[...])
            outs[3 * t][...], outs[3 * t + 1][...], outs[3 * t + 2][...] = d, mo, vo

    vm = pl.BlockSpec(memory_space=pltpu.VMEM)
    flat = [a for t in tensors for a in t]
    out = pl.pallas_call(
        body, name="adam_small", in_specs=[vm] * (4 * n), out_specs=[vm] * (3 * n),
        out_shape=[jax.ShapeDtypeStruct(t[0].shape, F32) for t in tensors for _ in range(3)],
        compiler_params=pltpu.CompilerParams(vmem_limit_bytes=VMEM_LIMIT),
    )(*flat)
    return [tuple(out[3 * t:3 * t + 3]) for t in range(n)]


def _adam_halves_call(pos, w, mine, theirs, m, v, name):
    r, n = w.shape
    half = r // 2
    tr = ADAM_ROWS
    nh = half // tr

    def body(pos_ref, w_ref, mine_ref, theirs_ref, m_ref, v_ref, g_ref, d_ref, mo_ref, vo_ref):
        is_mine = (pl.program_id(0) // nh) == pos_ref[1]
        g = jnp.where(is_mine, mine_ref[...], theirs_ref[...])
        g_ref[...] = g
        d_ref[...], mo_ref[...], vo_ref[...] = _adam_math(w_ref[...], g, m_ref[...], v_ref[...])

    spec = lambda: pl.BlockSpec((tr, n), lambda i, pos: (i, 0))
    hspec = lambda: pl.BlockSpec((tr, n), lambda i, pos: (i % nh, 0))
    return pl.pallas_call(
        body, name=name,
        grid_spec=pltpu.PrefetchScalarGridSpec(
            num_scalar_prefetch=1, grid=(r // tr,), in_specs=[spec(), hspec(), hspec(), spec(), spec()],
            out_specs=[spec() for _ in range(4)]),
        out_shape=[jax.ShapeDtypeStruct((r, n), F32)] * 4, compiler_params=_params("parallel"),
    )(pos, w, mine, theirs, m, v)


def _adam_outer_call(w, ct, dm, m, v, name):
    r, n = w.shape
    tr = ADAM_ROWS

    def body(w_ref, ct_ref, dm_ref, m_ref, v_ref, g_ref, d_ref, mo_ref, vo_ref):
        g = ct_ref[:, 0:1] * dm_ref[0:1, :]
        for b in range(1, N_DEV):
            g = g + ct_ref[:, b:b + 1] * dm_ref[b:b + 1, :]
        g_ref[...] = g
        d_ref[...], mo_ref[...], vo_ref[...] = _adam_math(w_ref[...], g, m_ref[...], v_ref[...])

    spec = lambda: pl.BlockSpec((tr, n), lambda i: (i, 0))
    return pl.pallas_call(
        body, name=name, grid=(r // tr,),
        in_specs=[spec(), pl.BlockSpec((tr, N_DEV), lambda i: (i, 0)), pl.BlockSpec((N_DEV, n), lambda i: (0, 0)), spec(), spec()],
        out_specs=[spec() for _ in range(4)],
        out_shape=[jax.ShapeDtypeStruct((r, n), F32)] * 4, compiler_params=_params("parallel"),
    )(w, ct, dm, m, v)


def _sum_pieces_call(pos, part, part_block, recvs, name):
    r, n = recvs[0].shape[1:]
    tr = min(r, 256)
    nrb = r // tr

    def body(pos_ref, p_ref, *refs):
        acc = p_ref[...].astype(F32)
        for r_ref in refs[:-1]:
            for d in range(r_ref.shape[0]):
                acc = acc + r_ref[d].astype(F32)
        refs[-1][...] = acc

    return pl.pallas_call(
        body, name=name,
        grid_spec=pltpu.PrefetchScalarGridSpec(
            num_scalar_prefetch=1, grid=(nrb,),
            in_specs=[pl.BlockSpec((tr, n), lambda i, pos: part_block(i, pos, nrb))] + [
                pl.BlockSpec((rv.shape[0], tr, n), lambda i, pos: (0, i, 0)) for rv in recvs],
            out_specs=pl.BlockSpec((tr, n), lambda i, pos: (i, 0))),
        out_shape=jax.ShapeDtypeStruct((r, n), F32), compiler_params=_params("parallel"),
    )(pos, part, *recvs)


def _coords():
    return lax.axis_index("x"), lax.axis_index("y"), lax.axis_index("c")


def _allgather_sum_call(blk, name, with_sum):
    m_per, n = blk.shape

    def body(x_ref, out_ref, *rest):
        if with_sum:
            sum_ref, send_sems, recv_sems, local_sem = rest
        else:
            send_sems, recv_sems, local_sem = rest
        x, y, c = _coords()
        me, sibling = (x, y, c), (x, y, 1 - c)
        chips = [(1 - x, y), (x, 1 - y), (1 - x, 1 - y)]

        def rows(px, py, pc):
            return out_ref.at[pl.ds((4 * px + 2 * py + pc) * m_per, m_per), :]

        def copy(k, block, to, src=None):
            return pltpu.make_async_remote_copy(
                src_ref=rows(*block) if src is None else src, dst_ref=rows(*block),
                send_sem=send_sems.at[k], recv_sem=recv_sems.at[k], device_id=to, device_id_type=MESH)

        mine = pltpu.make_async_copy(x_ref, rows(*me), local_sem)
        mine.start()
        first = [copy(0, me, sibling, src=x_ref)]
        first += [copy(1 + j, me, (*chip, c), src=x_ref) for j, chip in enumerate(chips)]
        for cp in first:
            cp.start()
        passed = [copy(4 + j, (*chip, c), sibling) for j, chip in enumerate(chips)]
        for j, chip in enumerate(chips):
            copy(1 + j, (*chip, c), me).wait_recv()
            passed[j].start()
        copy(0, sibling, me).wait_recv()
        for j, chip in enumerate(chips):
            copy(4 + j, (*chip, 1 - c), me).wait_recv()
        for cp in first + passed:
            cp.wait_send()
        mine.wait()
        if with_sum:
            acc = out_ref[0:m_per, :]
            for d in range(1, N_DEV):
                acc = acc + out_ref[d * m_per:(d + 1) * m_per, :]
            sum_ref[...] = acc

    vm = pl.BlockSpec(memory_space=pltpu.VMEM)
    out_shape = [jax.ShapeDtypeStruct((N_DEV * m_per, n), F32)]
    if with_sum:
        out_shape.append(jax.ShapeDtypeStruct((m_per, n), F32))
    return pl.pallas_call(
        body, name=name, out_shape=out_shape, in_specs=[vm], out_specs=[vm] * len(out_shape),
        scratch_shapes=[pltpu.SemaphoreType.DMA((7,)), pltpu.SemaphoreType.DMA((7,)), pltpu.SemaphoreType.DMA],
        compiler_params=pltpu.CompilerParams(vmem_limit_bytes=VMEM_LIMIT),
    )(blk)


def _weights_gather_call(wi_full, wo_full):
    hi = D // 2
    ho = W_OUT_SHARD // 2

    def body(wi_in, wo_in, fi_ref, fo_ref, send_sems, recv_sems):
        del wi_in, wo_in
        x, y, c = _coords()
        sibling = (x, y, 1 - c)
        chips = [(1 - x, y), (x, 1 - y), (1 - x, 1 - y)]

        def half(which, px, py, pc):
            j = 2 * px + py
            if which == 0:
                return fi_ref.at[pl.ds(pc * hi, hi), pl.ds(j * W_IN_SHARD, W_IN_SHARD)]
            return fo_ref.at[pl.ds(j * W_OUT_SHARD + pc * ho, ho), :]

        def copy(k, which, block, to):
            return pltpu.make_async_remote_copy(
                src_ref=half(which, *block), dst_ref=half(which, *block), send_sem=send_sems.at[k],
                recv_sem=recv_sems.at[k], device_id=to, device_id_type=MESH)

        first = [copy(6 * w + j, w, (x, y, c), (*chip, c)) for w in range(2) for j, chip in enumerate(chips)]
        for cp in first:
            cp.start()
        passed = []
        for w in range(2):
            for j, chip in enumerate(chips):
                copy(6 * w + j, w, (*chip, c), (x, y, c)).wait_recv()
                cp = copy(6 * w + 3 + j, w, (*chip, c), sibling)
                cp.start()
                passed.append(cp)
        for w in range(2):
            for j, chip in enumerate(chips):
                copy(6 * w + 3 + j, w, (*chip, 1 - c), (x, y, c)).wait_recv()
        for cp in first + passed:
            cp.wait_send()

    anyspec = pl.BlockSpec(memory_space=pl.ANY)
    return pl.pallas_call(
        body, name="weights_gather",
        out_shape=[jax.ShapeDtypeStruct((D, D_IN), BF16), jax.ShapeDtypeStruct((D, D), BF16)],
        in_specs=[anyspec, anyspec], out_specs=[anyspec, anyspec], input_output_aliases={0: 0, 1: 1},
        scratch_shapes=[pltpu.SemaphoreType.DMA((12,)), pltpu.SemaphoreType.DMA((12,))],
    )(wi_full, wo_full)


HBM_SPEC = pl.BlockSpec(memory_space=pltpu.HBM)
SEM_SPEC = pl.BlockSpec(memory_space=pltpu.SEMAPHORE)
SIDE_EFFECT = pltpu.SideEffectType.DATAFLOW_SIDE_EFFECTING


def _peer(x, y, c, q, cb):
    return (1 - x if q & 2 else x, 1 - y if q & 1 else y, 1 - c if cb else c)


def _w_in_piece(slots):
    def piece(part_ref, k, to):
        return part_ref.at[pl.ds(to[2] * (D // 2), D // 2), pl.ds(slots[k] * W_IN_SHARD, W_IN_SHARD)]
    return piece


def _w_out_piece(part_ref, k, to):
    ho = W_OUT_SHARD // 2
    return part_ref.at[pl.ds((2 * to[0] + to[1]) * W_OUT_SHARD + to[2] * ho, ho), :]


def _group_piece(part_ref, k, to):
    return part_ref.at[4 * to[0] + 2 * to[1] + to[2]]


def _whole_piece(part_ref, k, to):
    return part_ref


def _exchange_start_call(part, rels, piece, slot_shape, name):
    n = len(rels)
    land = lax.empty((n,) + slot_shape, part.dtype)

    def body(part_ref, land_ref, send_sems, recv_sems, part_thru, land_thru, token):
        x, y, c = _coords()
        for k, (q, cb) in enumerate(rels):
            to = _peer(x, y, c, q, cb)
            pltpu.make_async_remote_copy(src_ref=piece(part_ref, k, to), dst_ref=land_ref.at[k], send_sem=send_sems.at[k],
                                         recv_sem=recv_sems.at[k], device_id=to, device_id_type=MESH).start()
        token[...] = jnp.zeros_like(token)

    return pl.pallas_call(
        body, name=name,
        out_shape=(pltpu.SemaphoreType.DMA((n,)), pltpu.SemaphoreType.DMA((n,)), pltpu.HBM(part.shape, part.dtype),
                   pltpu.HBM(land.shape, land.dtype), jax.ShapeDtypeStruct((8, LANE), F32)),
        in_specs=(HBM_SPEC, HBM_SPEC), out_specs=(SEM_SPEC, SEM_SPEC, HBM_SPEC, HBM_SPEC, pl.BlockSpec(memory_space=pltpu.VMEM)),
        input_output_aliases={0: 2, 1: 3},
        compiler_params=pltpu.CompilerParams(has_side_effects=SIDE_EFFECT),
    )(pltpu.with_memory_space_constraint(part, pltpu.HBM), pltpu.with_memory_space_constraint(land, pltpu.HBM))


def _exchange_wait_call(started, rels, piece, after, name):
    send_sems, recv_sems, part_thru, land_thru, _ = started

    def body(part_ref, land_ref, send_sems, recv_sems, after_ref, part_out, land_out):
        x, y, c = _coords()
        for k, (q, cb) in enumerate(rels):
            to = _peer(x, y, c, q, cb)
            cp = pltpu.make_async_remote_copy(src_ref=piece(part_ref, k, to), dst_ref=land_ref.at[k], send_sem=send_sems.at[k],
                                              recv_sem=recv_sems.at[k], device_id=to, device_id_type=MESH)
            cp.wait_send()
            cp.wait_recv()

    return pl.pallas_call(
        body, name=name,
        out_shape=(pltpu.HBM(part_thru.shape, part_thru.dtype), pltpu.HBM(land_thru.shape, land_thru.dtype)),
        in_specs=(HBM_SPEC, HBM_SPEC, SEM_SPEC, SEM_SPEC, pl.BlockSpec(memory_space=pl.ANY)), out_specs=(HBM_SPEC, HBM_SPEC),
        input_output_aliases={0: 0, 1: 1},
        compiler_params=pltpu.CompilerParams(has_side_effects=SIDE_EFFECT),
    )(part_thru, land_thru, send_sems, recv_sems, after)


def _pair_exchange_call(gi, go):
    hi = D // 2
    ho = W_OUT_SHARD // 2

    def body(gi_in, go_in, fi_ref, fo_ref, send_sems, recv_sems):
        del gi_in, go_in
        x, y, c = _coords()
        sibling = (x, y, 1 - c)
        mine = (fi_ref.at[pl.ds(c * hi, hi), :], fo_ref.at[pl.ds(c * ho, ho), :])
        theirs = (fi_ref.at[pl.ds((1 - c) * hi, hi), :], fo_ref.at[pl.ds((1 - c) * ho, ho), :])
        sends = [pltpu.make_async_remote_copy(src_ref=ref, dst_ref=ref, send_sem=send_sems.at[k], recv_sem=recv_sems.at[k],
                                              device_id=sibling, device_id_type=MESH) for k, ref in enumerate(mine)]
        for cp in sends:
            cp.start()
        for k, ref in enumerate(theirs):
            pltpu.make_async_remote_copy(src_ref=ref, dst_ref=ref, send_sem=send_sems.at[k], recv_sem=recv_sems.at[k],
                                         device_id=sibling, device_id_type=MESH).wait_recv()
        for cp in sends:
            cp.wait_send()

    anyspec = pl.BlockSpec(memory_space=pl.ANY)
    return pl.pallas_call(
        body, name="pair_exchange",
        out_shape=[jax.ShapeDtypeStruct((D, W_IN_SHARD), F32), jax.ShapeDtypeStruct((W_OUT_SHARD, D), F32)],
        in_specs=[anyspec, anyspec], out_specs=[anyspec, anyspec], input_output_aliases={0: 0, 1: 1},
        scratch_shapes=[pltpu.SemaphoreType.DMA((2,)), pltpu.SemaphoreType.DMA((2,))],
    )(gi, go)


def _rope_tables(s):
    inv_freq = 10000.0 ** (-jnp.arange(0, HEAD, 2, dtype=F32) / HEAD)
    ang = jnp.arange(s, dtype=F32)[:, None] * inv_freq[None, :]
    cos = jnp.tile(jnp.cos(ang), (1, LANE // (HEAD // 2)))
    sin = jnp.tile(jnp.sin(ang), (1, LANE // (HEAD // 2)))
    first_half = (jnp.arange(LANE) % HEAD) < (HEAD // 2)
    return cos, jnp.where(first_half[None, :], -sin, sin)


def _pad_cols(a, n):
    return jnp.pad(a, ((0, 0), (0, n - a.shape[1])))


def kernel(x, c, w_ada, b_ada, norm_g, w_in, ln_v_g, ln_v_b, w_spatial, b_spatial, sinks, w_out, w_ada_final, b_ada_final, final_norm_g, loss_target, m_w_ada, m_b_ada, m_norm_g, m_w_in, m_ln_v_g, m_ln_v_b, m_w_spatial, m_b_spatial, m_sinks, m_w_out, m_w_ada_final, m_b_ada_final, m_final_norm_g, v_w_ada, v_b_ada, v_norm_g, v_w_in, v_ln_v_g, v_ln_v_b, v_w_spatial, v_b_spatial, v_sinks, v_w_out, v_w_ada_final, v_b_ada_final, v_final_norm_g):
    s = x.shape[1]
    ax, ay, ac = _coords()
    chip = 2 * ax + ay
    me = 4 * ax + 2 * ay + ac
    n_ada = w_ada.shape[2]
    n_adaf = w_ada_final.shape[1]

    x2d = x.reshape(s, D)
    tgt = loss_target.reshape(s, D)
    w_ada2, w_in2, w_out2 = w_ada[0], w_in[0], w_out[0]
    b_ada_f2 = b_ada_final.reshape(1, 2 * D)
    gf = final_norm_g.reshape(1, D)

    c_all = _allgather_sum_call(jnp.pad(c, ((0, 7), (0, 0))), "gather_c", False)[0][::8]
    mod_p, c_act = _rowmat_call(c_all, w_ada2, lax.dynamic_slice(b_ada, (0, chip * n_ada), (1, n_ada)), "mod")
    modf_p, _ = _rowmat_call(c_all, w_ada_final, lax.dynamic_slice(b_ada_f2, (0, chip * n_adaf), (1, n_adaf)), "mod_final")
    mods = _allgather_sum_call(jnp.concatenate([mod_p, modf_p], axis=1), "gather_mod", False)[0]
    my_rows = [lax.dynamic_slice(mods, (16 * j + me, 0), (1, n_ada + n_adaf)) for j in range(N_CHIP)]
    mod = jnp.concatenate([r[:, :n_ada] for r in my_rows], axis=1)
    mod_f = jnp.concatenate([r[:, n_ada:] for r in my_rows], axis=1)
    shift, scale, gate = mod[:, :D], mod[:, D:2 * D], mod[:, 2 * D:]
    shift_f, scale_f = mod_f[:, :D], mod_f[:, D:]

    pos = jnp.stack([chip, ac]).astype(jnp.int32)
    w_in_own = _cast_into_call(pos, w_in2, (D, D_IN), "cast_w_in")
    w_out_own = _cast_into_call(pos, w_out2, (D, D), "cast_w_out")

    cos, sin = _rope_tables(s)
    b_sp_t = b_spatial[0].T
    sinks1 = sinks.reshape(N_Q)
    h, proj, w_in_bf, w_out_bf = _proj_gather_call(pos, x2d, shift, scale, norm_g, w_in_own, w_out_own)
    y = _mix_fwd_call(proj, cos, sin, ln_v_g, ln_v_b, w_spatial[0], b_sp_t, sinks1)
    dx2, do, dy, st_tail = _tail_call(y, w_out_bf, x2d, tgt, gate, shift_f, scale_f, gf)

    rel_o = [(0, 1), (1, 0), (1, 1), (2, 0), (2, 1), (3, 0), (3, 1)]
    rel_a = [(1, 0), (1, 1), (2, 0), (2, 1)]
    rel_b = [(3, 0), (3, 1), (0, 1)]
    piece_a, piece_b = _w_in_piece([0, 0, 1, 1]), _w_in_piece([0, 0, 1])
    half_in, half_out = (D // 2, W_IN_SHARD), (W_OUT_SHARD // 2, D)

    g_w_out_p = _tn_call(y, do, "grad_w_out")
    st_o = _exchange_start_call(g_w_out_p, rel_o, _w_out_piece, half_out, "send_w_out")
    dproj, st_ln, d_wsp, d_bsp_t, d_sink = _mix_bwd_call(
        proj, dy, cos, sin, ln_v_g + st_o[4][0:1, 0:1], ln_v_b, w_spatial[0], jnp.swapaxes(w_spatial[0], 1, 2), b_sp_t, sinks1)
    g_w_in_a = _tn_shards_call(pos, h, dproj, (1, 2), "grad_w_in_a")
    st_a = _exchange_start_call(g_w_in_a, rel_a, piece_a, half_in, "send_w_in_a")
    g_w_in_b = _tn_shards_call(pos, h, dproj, (3, 0), "grad_w_in_b")
    st_b = _exchange_start_call(g_w_in_b, rel_b, piece_b, half_in, "send_w_in_b")
    rel_all = rel_o
    st_s = _exchange_start_call(d_wsp, rel_all, _group_piece, (BLK, BLK), "send_w_spatial")
    sent = st_a[4][0:1, 0:1] + st_b[4][0:1, 0:1] + st_s[4][0:1, 0:1]
    grad_x, st_dh = _dh_call(dproj, w_in_bf, x2d, dx2, scale + sent, norm_g)

    g_w_out_p, recv_o = _exchange_wait_call(st_o, rel_o, _w_out_piece, st_dh, "wait_w_out")
    _, recv_a = _exchange_wait_call(st_a, rel_a, piece_a, st_dh, "wait_w_in_a")
    g_w_in_b, recv_b = _exchange_wait_call(st_b, rel_b, piece_b, st_dh, "wait_w_in_b")
    d_wsp, recv_s = _exchange_wait_call(st_s, rel_all, _group_piece, st_dh, "wait_w_spatial")
    mine_in = _sum_pieces_call(pos, g_w_in_b, lambda i, p, nrb: (p[1] * nrb + i, 1), [recv_a, recv_b], "sum_w_in")
    mine_out = _sum_pieces_call(pos, g_w_out_p, lambda i, p, nrb: ((2 * p[0] + p[1]) * nrb + i, 0), [recv_o], "sum_w_out")
    wsp_group = _sum_pieces_call(pos, d_wsp.reshape(GROUPS * BLK, BLK), lambda i, p, nrb: (2 * p[0] + p[1], 0), [recv_s],
                                 "sum_w_spatial")
    to_sibling = [(0, 1)]
    st_pi = _exchange_start_call(mine_in, to_sibling, _whole_piece, half_in, "swap_w_in")
    st_po = _exchange_start_call(mine_out, to_sibling, _whole_piece, half_out, "swap_w_out")

    misc = jnp.concatenate([st_ln, d_bsp_t[:, :GROUPS].T, d_sink, jnp.zeros((8, D - D_A - 2 * LANE), F32)], axis=1)
    pack = jnp.concatenate([wsp_group.reshape(8, D) + (st_pi[4][0:1, 0:1] + st_po[4][0:1, 0:1]), st_tail, st_dh, misc], axis=0)
    rows = pack.shape[0]
    packs, tot = _allgather_sum_call(pack, "gather_small", True)
    packs = packs.reshape(N_DEV, rows, D)
    dmod_all = jnp.concatenate([packs[:, 16, :], packs[:, 17, :], packs[:, 11, :]], axis=1)
    dmodf_all = jnp.concatenate([packs[:, 8, :], packs[:, 9, :]], axis=1)
    loss = tot[13, 0]
    mine_in, theirs_in = _exchange_wait_call(st_pi, to_sibling, _whole_piece, tot, "swapped_w_in")
    mine_out, theirs_out = _exchange_wait_call(st_po, to_sibling, _whole_piece, tot, "swapped_w_out")
    small = {
        "b_ada": jnp.concatenate([tot[16:17], tot[17:18], tot[11:12]], axis=1),
        "norm_g": tot[18:19],
        "ln_v_g": tot[24:25, :D_A],
        "ln_v_b": tot[25:26, :D_A],
        "w_spatial": packs[:, 0:8, :].reshape(GROUPS * BLK, BLK),
        "b_spatial": tot[24:32, D_A:D_A + BLK],
        "sinks": tot[24:25, D_A + LANE:D_A + LANE + N_Q],
        "b_ada_final": jnp.concatenate([tot[8:9], tot[9:10]], axis=1),
        "final_norm_g": tot[10:11],
    }

    weights = dict(w_ada=w_ada, b_ada=b_ada, norm_g=norm_g, w_in=w_in, ln_v_g=ln_v_g, ln_v_b=ln_v_b, w_spatial=w_spatial,
                   b_spatial=b_spatial, sinks=sinks, w_out=w_out, w_ada_final=w_ada_final, b_ada_final=b_ada_final,
                   final_norm_g=final_norm_g)
    m_in = dict(w_ada=m_w_ada, b_ada=m_b_ada, norm_g=m_norm_g, w_in=m_w_in, ln_v_g=m_ln_v_g, ln_v_b=m_ln_v_b,
                w_spatial=m_w_spatial, b_spatial=m_b_spatial, sinks=m_sinks, w_out=m_w_out, w_ada_final=m_w_ada_final,
                b_ada_final=m_b_ada_final, final_norm_g=m_final_norm_g)
    v_in = dict(w_ada=v_w_ada, b_ada=v_b_ada, norm_g=v_norm_g, w_in=v_w_in, ln_v_g=v_ln_v_g, ln_v_b=v_ln_v_b,
                w_spatial=v_w_spatial, b_spatial=v_b_spatial, sinks=v_sinks, w_out=v_w_out, w_ada_final=v_w_ada_final,
                b_ada_final=v_b_ada_final, final_norm_g=v_final_norm_g)
    c_act_t = c_act.T
    outer = {"w_ada": lax.dynamic_slice(dmod_all, (0, chip * n_ada), (N_DEV, n_ada)),
             "w_ada_final": lax.dynamic_slice(dmodf_all, (0, chip * n_adaf), (N_DEV, n_adaf))}
    halves = {"w_in": (mine_in, theirs_in[0]), "w_out": (mine_out, theirs_out[0])}
    done = {}
    for name, (mine, theirs) in halves.items():
        shape2 = (2 * mine.shape[0], mine.shape[1])
        done[name] = _adam_halves_call(pos, weights[name].reshape(shape2), mine, theirs, m_in[name].reshape(shape2),
                                       v_in[name].reshape(shape2), "adam_" + name)
    for name, dm in outer.items():
        shape2 = (D, dm.shape[1])
        done[name] = _adam_outer_call(weights[name].reshape(shape2), c_act_t, dm, m_in[name].reshape(shape2),
                                      v_in[name].reshape(shape2), "adam_" + name)
    updates = _adam_small_call([(weights[name].reshape(g.shape), g, m_in[name].reshape(g.shape), v_in[name].reshape(g.shape))
                                for name, g in small.items()])
    for (name, g), upd in zip(small.items(), updates):
        done[name] = (g, *upd)
    outs = [[done[name][k].reshape(w.shape) for name, w in weights.items()] for k in range(4)]
    return (loss, grad_x.reshape(x.shape), *outs[0], *outs[1], *outs[2], *outs[3])
```

```python
import jax
import jax.numpy as jnp
from jax import lax
from jax.experimental import pallas as pl
from jax.experimental.pallas import tpu as pltpu

F32 = jnp.float32
BF16 = jnp.bfloat16
MESH = pl.DeviceIdType.MESH

D = 2048
D_A = 1024
D_B = 1024
D_KV = 256
HEAD = 64
N_Q = 16
N_KV = 4
Q_PER_KV = N_Q // N_KV
BLK = 128
GROUPS = 8
D_IN = 5632
OFF_Q, OFF_K, OFF_V, OFF_ZB = 3072, 4096, 4352, 4608
N_CHIP = 4
N_DEV = 8
W_IN_SHARD = D_IN // N_CHIP
W_OUT_SHARD = D // N_CHIP
EPS = 1e-5
SCALE = HEAD ** -0.5
NEG = -1e30
LANE = 128
VMEM_LIMIT = 56 * 1024 * 1024

ADAM_LR, ADAM_B1, ADAM_B2, ADAM_EPS, ADAM_WD, ADAM_STEP = 0.001, 0.9, 0.999, 1e-08, 0.01, 10
ADAM_C1 = 1.0 - ADAM_B1 ** ADAM_STEP
ADAM_C2 = 1.0 - ADAM_B2 ** ADAM_STEP
ADAM_ROWS = 256

NT = (((1,), (1,)), ((), ()))
TN = (((0,), (0,)), ((), ()))


def _params(*sem):
    return pltpu.CompilerParams(dimension_semantics=sem, vmem_limit_bytes=VMEM_LIMIT)


def _silu_parts(z):
    sig = 1.0 / (1.0 + jnp.exp(-z))
    return z * sig, sig


def _swap_halves(v, first_half):
    return jnp.where(first_half, pltpu.roll(v, 96, 1), pltpu.roll(v, 32, 1))


def _rope(v, cos_t, sin_s, first_half):
    return v * cos_t + _swap_halves(v, first_half) * sin_s


def _unrope(dv, cos_t, sin_s, first_half):
    return dv * cos_t - _swap_halves(dv, first_half) * sin_s


def _lane_masks():
    lane = lax.broadcasted_iota(jnp.int32, (BLK, LANE), 1)
    return (lane % HEAD) < (HEAD // 2), lane < HEAD


def _band_valid(first_block_bound, rows=BLK):
    rr = lax.broadcasted_iota(jnp.int32, (rows, 2 * BLK), 0) & (BLK - 1)
    jj = lax.broadcasted_iota(jnp.int32, (rows, 2 * BLK), 1)
    return (jj > rr) & (jj <= rr + BLK) & (jj >= first_block_bound)


def _dup_kv(slab, lo):
    rolled = pltpu.roll(slab, HEAD, 1)
    return jnp.where(lo, slab, rolled).astype(BF16), jnp.where(lo, rolled, slab).astype(BF16)


def _stack_heads(ref, sb, slab, lo, dtype):
    kh, base = sb // 2, 2 * (sb % 2) * BLK
    zero = jnp.zeros_like(slab)
    ref[kh, base:base + BLK, :] = jnp.where(lo, slab, zero).astype(dtype)
    ref[kh, base + BLK:base + 2 * BLK, :] = jnp.where(lo, zero, slab).astype(dtype)


def _unstack_heads(ref, sb, lo):
    kh, base = sb // 2, 2 * (sb % 2) * BLK
    return jnp.where(lo, ref[kh, base:base + BLK, :], ref[kh, base + BLK:base + 2 * BLK, :])


def _sink_column(sinks_ref, kh):
    row = lax.broadcasted_iota(jnp.int32, (Q_PER_KV * BLK, 1), 0)
    col = jnp.full(row.shape, sinks_ref[Q_PER_KV * kh + Q_PER_KV - 1], F32)
    for n in range(Q_PER_KV - 2, -1, -1):
        col = jnp.where(row < (n + 1) * BLK, sinks_ref[Q_PER_KV * kh + n], col)
    return col


def _tril():
    t = lax.broadcasted_iota(jnp.int32, (BLK, BLK), 0)
    s = lax.broadcasted_iota(jnp.int32, (BLK, BLK), 1)
    return s <= t


def _layer_norm_fwd(va, lg, lb):
    mu = jnp.mean(va, axis=-1, keepdims=True)
    xc = va - mu
    rstd = lax.rsqrt(jnp.mean(xc * xc, axis=-1, keepdims=True) + EPS)
    vhat = xc * rstd
    return vhat, rstd, vhat * lg + lb


def _softmax_sink(qm, kdup, bias, sink):
    s = lax.dot_general(qm, kdup, NT, preferred_element_type=F32) + bias
    m = jnp.maximum(jnp.max(s, axis=-1, keepdims=True), sink)
    p = jnp.exp(s - m)
    esink = jnp.exp(sink - m)
    inv = 1.0 / (jnp.sum(p, axis=-1, keepdims=True) + esink)
    return p * inv, esink * inv


def _band_bias(bias_ref):
    rows = bias_ref.shape[1]
    bias_ref[0] = jnp.where(_band_valid(BLK, rows), 0.0, NEG)
    bias_ref[1] = jnp.where(_band_valid(0, rows), 0.0, NEG)


def _rowmat_call(c_all, w, b, name):
    n = w.shape[1]
    tn = 512

    def body(c_ref, w_ref, b_ref, o_ref, ca_ref):
        ca, _ = _silu_parts(c_ref[...])
        ca_ref[...] = ca
        o_ref[...] = jnp.dot(ca.astype(BF16), w_ref[...].astype(BF16), preferred_element_type=F32) + b_ref[...]

    return pl.pallas_call(
        body, name=name, grid=(n // tn,),
        in_specs=[pl.BlockSpec((N_DEV, D), lambda j: (0, 0)), pl.BlockSpec((D, tn), lambda j: (0, j)),
                  pl.BlockSpec((1, tn), lambda j: (0, j))],
        out_specs=[pl.BlockSpec((N_DEV, tn), lambda j: (0, j)), pl.BlockSpec((N_DEV, D), lambda j: (0, 0))],
        out_shape=[jax.ShapeDtypeStruct((N_DEV, n), F32), jax.ShapeDtypeStruct((N_DEV, D), F32)],
        compiler_params=_params("arbitrary"),
    )(c_all, w, b)


def _cast_into_call(pos, w, full_shape, name):
    r, n = w.shape
    tr = min(r, 512)
    by_cols = full_shape[0] == r
    nrb = r // tr

    def body(pos_ref, w_ref, o_ref):
        o_ref[...] = w_ref[...].astype(BF16)

    out_map = (lambda i, pos: (i, pos[0])) if by_cols else (lambda i, pos: (pos[0] * nrb + i, 0))
    return pl.pallas_call(
        body, name=name,
        grid_spec=pltpu.PrefetchScalarGridSpec(
            num_scalar_prefetch=1, grid=(nrb,),
            in_specs=[pl.BlockSpec((tr, n), lambda i, pos: (i, 0))], out_specs=pl.BlockSpec((tr, n), out_map)),
        out_shape=jax.ShapeDtypeStruct(full_shape, BF16), compiler_params=_params("parallel"),
    )(pos, w)


def _proj_call(x, shift, scale, norm_g, w_bf):
    s = x.shape[0]
    tm = min(s, 1024)
    tn = 512

    def body(x_ref, sh_ref, sc_ref, g_ref, w_ref, proj_ref, h_ref):
        @pl.when(pl.program_id(1) == 0)
        def _():
            xv = x_ref[...]
            r = lax.rsqrt(jnp.mean(xv * xv, axis=-1, keepdims=True) + EPS)
            h_ref[...] = ((xv * r * g_ref[...]) * (1.0 + sc_ref[...]) + sh_ref[...]).astype(BF16)

        proj_ref[...] = jnp.dot(h_ref[...], w_ref[...], preferred_element_type=F32)

    vec = pl.BlockSpec((1, D), lambda i, j: (0, 0))
    return pl.pallas_call(
        body, name="proj", grid=(s // tm, D_IN // tn),
        in_specs=[pl.BlockSpec((tm, D), lambda i, j: (i, 0)), vec, vec, vec, pl.BlockSpec((D, tn), lambda i, j: (0, j))],
        out_specs=[pl.BlockSpec((tm, tn), lambda i, j: (i, j)), pl.BlockSpec((tm, D), lambda i, j: (i, 0))],
        out_shape=[jax.ShapeDtypeStruct((s, D_IN), F32), jax.ShapeDtypeStruct((s, D), BF16)],
        compiler_params=_params("parallel", "arbitrary"),
    )(x, shift, scale, norm_g, w_bf)


def _proj_gather_whole_shards_call(pos, x, shift, scale, norm_g, wi_full, wo_full):
    s = x.shape[0]
    tm = min(s, 512)
    nrow = s // tm
    hi = D // 2
    ho = W_OUT_SHARD // 2

    def body(pos_ref, x_ref, sh_ref, sc_ref, g_ref, wi_in, wo_in, proj_ref, h_ref, fi_ref, fo_ref,
             h_all, wbuf, send_sems, recv_sems, load_sem):
        del wi_in, wo_in
        p = pl.program_id(0)
        i = pl.program_id(1)
        x_, y_, c_ = _coords()
        me, sibling = (x_, y_, c_), (x_, y_, 1 - c_)

        def shard_of(q):
            px, py, _ = _peer(x_, y_, c_, q, 0)
            return 2 * px + py

        def part(which, q, pc, sub=None):
            n = hi if which == 0 else ho
            base = pc * n
            if sub is not None:
                n //= 2
                base = base + sub * n
            if which == 0:
                return fi_ref.at[pl.ds(base, n), pl.ds(shard_of(q) * W_IN_SHARD, W_IN_SHARD)]
            return fo_ref.at[pl.ds(shard_of(q) * W_OUT_SHARD + base, n), :]

        def copy(k, which, q, pc, to, sub=None):
            ref = part(which, q, pc, sub)
            return pltpu.make_async_remote_copy(src_ref=ref, dst_ref=ref, send_sem=send_sems.at[k], recv_sem=recv_sems.at[k],
                                                device_id=to, device_id_type=MESH)

        def to_neighbour(which, q):
            return copy(8 * which + q - 1, which, 0, c_, _peer(x_, y_, c_, q, 0))

        def from_neighbour(which, q):
            return copy(8 * which + q - 1, which, q, c_, me)

        def relay(which, q):
            return copy(8 * which + 2 + q - 1, which, q, c_, _peer(x_, y_, c_, 3 - q, 0), q - 1)

        def relayed(which, sub):
            return copy(8 * which + 2 + sub, which, 3, c_, me, sub)

        def to_sibling(which, q):
            return copy(8 * which + 4 + q - 1, which, q, c_, sibling)

        def from_sibling(which, q):
            return copy(8 * which + 4 + q - 1, which, q, 1 - c_, me)

        def relayed_to_sibling(which, sub):
            return copy(8 * which + 6 + sub, which, 3, c_, sibling, sub)

        def relayed_from_sibling(which, sub):
            return copy(8 * which + 6 + sub, which, 3, 1 - c_, me, sub)

        def pass_on_neighbours(which):
            for q in (1, 2):
                from_neighbour(which, q).wait_recv()
                to_sibling(which, q).start()
                relay(which, q).start()

        def pass_on_relayed(which):
            for sub in range(2):
                relayed(which, sub).wait_recv()
                relayed_to_sibling(which, sub).start()

        def load_shard(q):
            cp = pltpu.make_async_copy(fi_ref.at[:, pl.ds(shard_of(q) * W_IN_SHARD, W_IN_SHARD)], wbuf, load_sem)
            cp.start()
            cp.wait()

        @pl.when((p == 0) & (i == 0))
        def _():
            for q in (1, 2):
                to_neighbour(0, q).start()
            load_shard(0)

        @pl.when((p == 1) & (i == 0))
        def _():
            pass_on_neighbours(0)
            for q in (1, 2):
                to_neighbour(1, q).start()
            from_sibling(0, 1).wait_recv()
            load_shard(1)

        @pl.when((p == 2) & (i == 0))
        def _():
            from_sibling(0, 2).wait_recv()
            load_shard(2)

        @pl.when((p == 3) & (i == 0))
        def _():
            pass_on_relayed(0)
            pass_on_neighbours(1)
            for sub in range(2):
                relayed_from_sibling(0, sub).wait_recv()
            load_shard(3)

        rows = pl.ds(pl.multiple_of(i * tm, tm), tm)

        @pl.when(p == 0)
        def _():
            xv = x_ref[...]
            r = lax.rsqrt(jnp.mean(xv * xv, axis=-1, keepdims=True) + EPS)
            hv = ((xv * r * g_ref[...]) * (1.0 + sc_ref[...]) + sh_ref[...]).astype(BF16)
            h_ref[...] = hv
            h_all[rows, :] = hv

        proj_ref[...] = jnp.dot(h_all[rows, :], wbuf[...], preferred_element_type=F32)

        @pl.when((p == N_CHIP - 1) & (i == nrow - 1))
        def _():
            pass_on_relayed(1)
            for q in (1, 2):
                from_sibling(1, q).wait_recv()
            for sub in range(2):
                relayed_from_sibling(1, sub).wait_recv()
            for which in range(2):
                for q in (1, 2):
                    to_neighbour(which, q).wait_send()
                    relay(which, q).wait_send()
                    to_sibling(which, q).wait_send()
                    relayed_to_sibling(which, q - 1).wait_send()

    vec = pl.BlockSpec((1, D), lambda p, i, pos: (0, 0))
    first_phase_rows = lambda p, i, pos: (jnp.where(p == 0, i, nrow - 1), 0)
    anyspec = pl.BlockSpec(memory_space=pl.ANY)
    return pl.pallas_call(
        body, name="proj_gather",
        grid_spec=pltpu.PrefetchScalarGridSpec(
            num_scalar_prefetch=1, grid=(N_CHIP, nrow),
            in_specs=[pl.BlockSpec((tm, D), first_phase_rows), vec, vec, vec, anyspec, anyspec],
            out_specs=[pl.BlockSpec((tm, W_IN_SHARD), lambda p, i, pos: (i, jnp.bitwise_xor(pos[0], p))),
                       pl.BlockSpec((tm, D), first_phase_rows), anyspec, anyspec],
            scratch_shapes=[pltpu.VMEM((s, D), BF16), pltpu.VMEM((D, W_IN_SHARD), BF16),
                            pltpu.SemaphoreType.DMA((16,)), pltpu.SemaphoreType.DMA((16,)), pltpu.SemaphoreType.DMA]),
        out_shape=[jax.ShapeDtypeStruct((s, D_IN), F32), jax.ShapeDtypeStruct((s, D), BF16),
                   jax.ShapeDtypeStruct((D, D_IN), BF16), jax.ShapeDtypeStruct((D, D), BF16)],
        input_output_aliases={5: 2, 6: 3},
        compiler_params=_params("arbitrary", "arbitrary"),
    )(pos, x, shift, scale, norm_g, wi_full, wo_full)


W_IN_PARTS = ((0, 768), (768, 640))
OUT_STREAMS = 4
X_STREAMS = 4


def _proj_gather_call(pos, x, shift, scale, norm_g, wi_full, wo_full):
    s = x.shape[0]
    tm = min(s, 512)
    nrow = s // tm
    hi = D // 2
    ho = W_OUT_SHARD // 2
    phases = [(0, None), (1, 0), (2, 0), (1, 1), (2, 1), (3, 0), (3, 1)]

    def body(pos_ref, *refs):
        x_refs = refs[:X_STREAMS]
        (sh_ref, sc_ref, g_ref, _, _, h_ref, proj_ref, fi_ref, fo_ref,
         h_all, wbuf, obuf, send_sems, recv_sems, load_sems, out_sems) = refs[X_STREAMS:]
        p = pl.program_id(0)
        i = pl.program_id(1)
        x_, y_, c_ = _coords()
        me, sibling = (x_, y_, c_), (x_, y_, 1 - c_)

        def shard_of(q):
            px, py, _ = _peer(x_, y_, c_, q, 0)
            return 2 * px + py

        def cols_of(q, cp):
            off, w = (0, W_IN_SHARD) if cp is None else W_IN_PARTS[cp]
            return shard_of(q) * W_IN_SHARD + off, w

        def part(which, q, pc, sub, cp):
            n = hi if which == 0 else ho
            base = pc * n
            if sub is not None:
                n //= 2
                base = base + sub * n
            if which == 0:
                c0, w = cols_of(q, cp)
                return fi_ref.at[pl.ds(base, n), pl.ds(c0, w)]
            return fo_ref.at[pl.ds(shard_of(q) * W_OUT_SHARD + base, n), :]

        def copy(k, ref, to):
            return pltpu.make_async_remote_copy(src_ref=ref, dst_ref=ref, send_sem=send_sems.at[k], recv_sem=recv_sems.at[k],
                                                device_id=to, device_id_type=MESH)

        def sem(which, kind, j, cp):
            return 4 * kind + 2 * cp + j if which == 0 else 16 + 2 * kind + j

        def to_neighbour(which, q, cp=None):
            return copy(sem(which, 0, q - 1, cp), part(which, 0, c_, None, cp), _peer(x_, y_, c_, q, 0))

        def from_neighbour(which, q, cp=None):
            return copy(sem(which, 0, q - 1, cp), part(which, q, c_, None, cp), me)

        def relay(which, q, cp=None):
            return copy(sem(which, 1, q - 1, cp), part(which, q, c_, q - 1, cp), _peer(x_, y_, c_, 3 - q, 0))

        def relayed(which, sub, cp=None):
            return copy(sem(which, 1, sub, cp), part(which, 3, c_, sub, cp), me)

        def to_sibling(which, q, cp=None):
            return copy(sem(which, 2, q - 1, cp), part(which, q, c_, None, cp), sibling)

        def from_sibling(which, q, cp=None):
            return copy(sem(which, 2, q - 1, cp), part(which, q, 1 - c_, None, cp), me)

        def relayed_to_sibling(which, sub, cp=None):
            return copy(sem(which, 3, sub, cp), part(which, 3, c_, sub, cp), sibling)

        def relayed_from_sibling(which, sub, cp=None):
            return copy(sem(which, 3, sub, cp), part(which, 3, 1 - c_, sub, cp), me)

        def pass_on_neighbours(which, cp=None):
            for q in (1, 2):
                from_neighbour(which, q, cp).wait_recv()
                to_sibling(which, q, cp).start()
                relay(which, q, cp).start()

        def pass_on_relayed(which, cp=None):
            for sub in range(2):
                relayed(which, sub, cp).wait_recv()
                relayed_to_sibling(which, sub, cp).start()

        def shard_load(k):
            c0, w = cols_of(*phases[k])
            return pltpu.make_async_copy(fi_ref.at[:, pl.ds(c0, w)], wbuf.at[k % 2, :, 0:w], load_sems.at[k % 2])

        class OutCopies:
            def __init__(self, k, slot, row0):
                c0, w = cols_of(*phases[k])
                strip = tm // OUT_STREAMS
                self.copies = [pltpu.make_async_copy(obuf.at[slot, n * strip:(n + 1) * strip, 0:w],
                                                     proj_ref.at[pl.ds(row0 + n * strip, strip), pl.ds(c0, w)],
                                                     out_sems.at[slot, n]) for n in range(OUT_STREAMS)]

            def start(self):
                for cp in self.copies:
                    cp.start()

            def wait(self):
                for cp in self.copies:
                    cp.wait()

        out_copy = OutCopies

        def drain(k):
            for j in range(min(2, nrow)):
                out_copy(k, (nrow - 1 - j) % 2, 0).wait()

        def arrivals(k):
            q, cp = phases[k]
            if k == 0:
                for cp_ in range(2):
                    for q_ in (1, 2):
                        to_neighbour(0, q_, cp_).start()
            elif q < 3 and k in (1, 3):
                pass_on_neighbours(0, cp)
                if k == 1:
                    for q_ in (1, 2):
                        to_neighbour(1, q_).start()
            elif k == 5:
                for cp_ in range(2):
                    pass_on_relayed(0, cp_)
                pass_on_neighbours(1)
            if q in (1, 2):
                from_sibling(0, q, cp).wait_recv()
            elif q == 3:
                for sub in range(2):
                    relayed_from_sibling(0, sub, cp).wait_recv()

        rows = pl.ds(pl.multiple_of(i * tm, tm), tm)
        slot = i % 2
        for k, (q, cp) in enumerate(phases):
            @pl.when(p == k)
            def _(k=k, q=q, cp=cp):
                @pl.when(i == 0)
                def _():
                    if k == 0:
                        arrivals(0)
                        shard_load(0).start()
                    else:
                        drain(k - 1)
                    shard_load(k).wait()

                if k + 1 < len(phases):
                    @pl.when(i == max(nrow - 2, 0))
                    def _():
                        arrivals(k + 1)
                        shard_load(k + 1).start()

                if k == 0:
                    wx = D // X_STREAMS
                    ssq = sum(jnp.sum(xr[...] * xr[...], axis=-1, keepdims=True) for xr in x_refs)
                    r = lax.rsqrt(ssq * (1.0 / D) + EPS)
                    for n, xr in enumerate(x_refs):
                        cols = slice(n * wx, (n + 1) * wx)
                        hv = ((xr[...] * r * g_ref[:, cols]) * (1.0 + sc_ref[:, cols]) + sh_ref[:, cols]).astype(BF16)
                        h_ref[:, cols] = hv
                        h_all[rows, cols] = hv

                @pl.when(i >= 2)
                def _():
                    out_copy(k, slot, 0).wait()

                w = cols_of(q, cp)[1]
                obuf[slot, :, 0:w] = jnp.dot(h_all[rows, :], wbuf[k % 2, :, 0:w], preferred_element_type=F32)
                out_copy(k, slot, pl.multiple_of(i * tm, tm)).start()

        @pl.when((p == len(phases) - 1) & (i == nrow - 1))
        def _():
            drain(len(phases) - 1)
            pass_on_relayed(1)
            for q in (1, 2):
                from_sibling(1, q).wait_recv()
            for sub in range(2):
                relayed_from_sibling(1, sub).wait_recv()
            for which, cps in ((0, (0, 1)), (1, (None,))):
                for cp in cps:
                    for q in (1, 2):
                        to_neighbour(which, q, cp).wait_send()
                        relay(which, q, cp).wait_send()
                        to_sibling(which, q, cp).wait_send()
                        relayed_to_sibling(which, q - 1, cp).wait_send()

    vec = pl.BlockSpec((1, D), lambda p, i, pos: (0, 0))
    first_phase_rows = lambda p, i, pos: (jnp.where(p == 0, i, nrow - 1), 0)
    anyspec = pl.BlockSpec(memory_space=pl.ANY)
    x_spec = lambda n: pl.BlockSpec((tm, D // X_STREAMS), lambda p, i, pos: (jnp.where(p == 0, i, nrow - 1), n))
    return pl.pallas_call(
        body, name="proj_gather",
        grid_spec=pltpu.PrefetchScalarGridSpec(
            num_scalar_prefetch=1, grid=(len(phases), nrow),
            in_specs=[x_spec(n) for n in range(X_STREAMS)] + [vec, vec, vec, anyspec, anyspec],
            out_specs=[pl.BlockSpec((tm, D), first_phase_rows), anyspec, anyspec, anyspec],
            scratch_shapes=[pltpu.VMEM((s, D), BF16), pltpu.VMEM((2, D, W_IN_SHARD), BF16), pltpu.VMEM((2, tm, W_IN_SHARD), F32),
                            pltpu.SemaphoreType.DMA((24,)), pltpu.SemaphoreType.DMA((24,)), pltpu.SemaphoreType.DMA((2,)),
                            pltpu.SemaphoreType.DMA((2, OUT_STREAMS))]),
        out_shape=[jax.ShapeDtypeStruct((s, D), BF16), jax.ShapeDtypeStruct((s, D_IN), F32),
                   jax.ShapeDtypeStruct((D, D_IN), BF16), jax.ShapeDtypeStruct((D, D), BF16)],
        input_output_aliases={X_STREAMS + 4: 2, X_STREAMS + 5: 3},
        compiler_params=_params("arbitrary", "arbitrary"),
    )(pos, *([x] * X_STREAMS), shift, scale, norm_g, wi_full, wo_full)


def _proj_specs(rev_nb=None):
    if rev_nb is None:
        row = lambda i: i
    else:
        row = lambda i: rev_nb - 1 - i
    wide = lambda col: pl.BlockSpec((BLK, D_A), lambda i: (row(i), col))
    kv = lambda col: pl.BlockSpec((BLK, D_KV), lambda i: (row(i), col))
    half = lambda col: pl.BlockSpec((BLK, 512), lambda i: (row(i), col))
    return [wide(0), wide(1), wide(2), wide(3), kv(OFF_K // D_KV), kv(OFF_V // D_KV), half(OFF_ZB // 512), half(OFF_ZB // 512 + 1)]


def _mix_fwd_call(proj, cos, sin, ln_g, ln_b, w_sp, b_sp_t, sinks):
    s = proj.shape[0]
    nb = s // BLK

    def body(ua_ref, va_ref, za_ref, q_ref, k_ref, v_ref, zb0_ref, zb1_ref, cos_ref, sin_ref, lg_ref, lb_ref,
             w_ref, bt_ref, sinks_ref, y_ref, kdup_ref, vdup_ref, qm_ref, ost_ref, bias_ref):
        i = pl.program_id(0)
        first_half, lo = _lane_masks()
        cos_t = cos_ref[...]
        sin_t = sin_ref[...]

        _, _, vln = _layer_norm_fwd(va_ref[...], lg_ref[...], lb_ref[...])
        tril = _tril()
        for g in range(GROUPS):
            cols = slice(g * BLK, (g + 1) * BLK)
            wg = jnp.where(tril, w_ref[g], 0.0).astype(BF16)
            sg = jnp.dot(wg, vln[:, cols].astype(BF16), preferred_element_type=F32) + bt_ref[:, g:g + 1]
            gate, _ = _silu_parts(za_ref[:, cols])
            y_ref[:, cols] = (ua_ref[:, cols] * sg * gate).astype(BF16)

        @pl.when(i == 0)
        def _():
            kdup_ref[:, 0:BLK, :] = jnp.zeros((N_KV, BLK, LANE), BF16)
            vdup_ref[:, 0:BLK, :] = jnp.zeros((N_KV, BLK, LANE), BF16)
            _band_bias(bias_ref)

        @pl.when(i > 0)
        def _():
            kdup_ref[:, 0:BLK, :] = kdup_ref[:, BLK:2 * BLK, :]
            vdup_ref[:, 0:BLK, :] = vdup_ref[:, BLK:2 * BLK, :]

        for ks in range(2):
            cols = slice(ks * LANE, (ks + 1) * LANE)
            kr = _rope(k_ref[:, cols], cos_t, sin_t, first_half)
            for n, (kd, vd) in enumerate(zip(_dup_kv(kr, lo), _dup_kv(v_ref[:, cols], lo))):
                kdup_ref[2 * ks + n, BLK:2 * BLK, :] = kd
                vdup_ref[2 * ks + n, BLK:2 * BLK, :] = vd
        for sb in range(8):
            _stack_heads(qm_ref, sb, _rope(q_ref[:, sb * LANE:(sb + 1) * LANE], cos_t, sin_t, first_half) * SCALE, lo, BF16)

        block_kind = jnp.where(i > 0, 1, 0)

        def kv_head(kh, carry):
            probs, _ = _softmax_sink(qm_ref[kh], kdup_ref[kh], bias_ref[block_kind], _sink_column(sinks_ref, kh))
            ost_ref[kh] = jnp.dot(probs.astype(BF16), vdup_ref[kh], preferred_element_type=F32)
            return carry

        lax.fori_loop(0, N_KV, kv_head, 0, unroll=2)
        for sb in range(8):
            cols = slice(sb * LANE, (sb + 1) * LANE)
            zb = zb0_ref[:, cols] if sb < 4 else zb1_ref[:, (sb - 4) * LANE:(sb - 3) * LANE]
            gate, _ = _silu_parts(zb)
            y_ref[:, D_A + sb * LANE:D_A + (sb + 1) * LANE] = (_unstack_heads(ost_ref, sb, lo) * gate).astype(BF16)

    tab = pl.BlockSpec((BLK, LANE), lambda i: (i, 0))
    return pl.pallas_call(
        body, name="mix_fwd", grid=(nb,),
        in_specs=_proj_specs() + [
            tab, tab, pl.BlockSpec((1, D_A), lambda i: (0, 0)), pl.BlockSpec((1, D_A), lambda i: (0, 0)),
            pl.BlockSpec((GROUPS, BLK, BLK), lambda i: (0, 0, 0)), pl.BlockSpec((BLK, GROUPS), lambda i: (0, 0)),
            pl.BlockSpec(memory_space=pltpu.SMEM)],
        out_specs=pl.BlockSpec((BLK, 2 * D_A), lambda i: (i, 0)),
        out_shape=jax.ShapeDtypeStruct((s, 2 * D_A), BF16),
        scratch_shapes=[pltpu.VMEM((N_KV, 2 * BLK, LANE), BF16), pltpu.VMEM((N_KV, 2 * BLK, LANE), BF16),
                        pltpu.VMEM((N_KV, Q_PER_KV * BLK, LANE), BF16), pltpu.VMEM((N_KV, Q_PER_KV * BLK, LANE), F32),
                        pltpu.VMEM((2, Q_PER_KV * BLK, 2 * BLK), F32)],
        compiler_params=_params("arbitrary"),
    )(proj, proj, proj, proj, proj, proj, proj, proj, cos, sin, ln_g, ln_b, w_sp, b_sp_t, sinks)


def _tail_call(y, w_out_bf, x, target, gate, shift_f, scale_f, gf):
    s = x.shape[0]
    tm = min(s, 256)
    nsteps = s // tm

    def body(y_ref, w_ref, x_ref, t_ref, gate_ref, shf_ref, scf_ref, gf_ref, dx2_ref, do_ref, dy_ref, st_ref):
        i = pl.program_id(0)

        @pl.when(i == 0)
        def _():
            st_ref[...] = jnp.zeros((8, D), F32)

        o = jnp.dot(y_ref[...], w_ref[...], preferred_element_type=F32)
        gate_v = gate_ref[...]
        x2 = x_ref[...] + gate_v * o
        r2 = lax.rsqrt(jnp.mean(x2 * x2, axis=-1, keepdims=True) + EPS)
        xn2 = x2 * r2
        hn2 = xn2 * gf_ref[...]
        one_sc = 1.0 + scf_ref[...]
        err = hn2 * one_sc + shf_ref[...] - t_ref[...]
        dout = err * (1.0 / D)
        dhn2 = dout * one_sc
        dxn2 = dhn2 * gf_ref[...]
        dx2 = r2 * (dxn2 - xn2 * jnp.mean(dxn2 * xn2, axis=-1, keepdims=True))
        dx2_ref[...] = dx2
        do = (dx2 * gate_v).astype(BF16)
        do_ref[...] = do
        dy_ref[...] = lax.dot_general(do, w_ref[...], NT, preferred_element_type=F32)
        st_ref[0:1, :] += jnp.sum(dout, axis=0, keepdims=True)
        st_ref[1:2, :] += jnp.sum(dout * hn2, axis=0, keepdims=True)
        st_ref[2:3, :] += jnp.sum(dhn2 * xn2, axis=0, keepdims=True)
        st_ref[3:4, :] += jnp.sum(dx2 * o, axis=0, keepdims=True)
        st_ref[4:5, :] += jnp.sum(err * err, axis=0, keepdims=True)

        @pl.when(i == nsteps - 1)
        def _():
            st_ref[5:6, :] = jnp.full((1, D), 0.5 / D, F32) * jnp.sum(st_ref[4:5, :])

    vec = pl.BlockSpec((1, D), lambda i: (0, 0))
    rows = lambda: pl.BlockSpec((tm, D), lambda i: (i, 0))
    return pl.pallas_call(
        body, name="tail", grid=(nsteps,),
        in_specs=[rows(), pl.BlockSpec((D, D), lambda i: (0, 0)), rows(), rows(), vec, vec, vec, vec],
        out_specs=[rows(), rows(), rows(), pl.BlockSpec((8, D), lambda i: (0, 0))],
        out_shape=[jax.ShapeDtypeStruct((s, D), F32), jax.ShapeDtypeStruct((s, D), BF16), jax.ShapeDtypeStruct((s, D), F32),
                   jax.ShapeDtypeStruct((8, D), F32)],
        compiler_params=_params("arbitrary"),
    )(y, w_out_bf, x, target, gate, shift_f, scale_f, gf)


def _tn_call(a, b, name):
    s, m = a.shape
    n = b.shape[1]
    tn = 512
    ts = min(s, 1024)
    nk = s // ts

    def body(a_ref, b_ref, o_ref, acc_ref):
        k = pl.program_id(1)

        @pl.when(k == 0)
        def _():
            acc_ref[...] = jnp.zeros((m, tn), F32)

        acc_ref[...] += lax.dot_general(a_ref[...], b_ref[...], TN, preferred_element_type=F32)

        @pl.when(k == nk - 1)
        def _():
            o_ref[...] = acc_ref[...].astype(BF16)

    return pl.pallas_call(
        body, name=name, grid=(n // tn, nk),
        in_specs=[pl.BlockSpec((ts, m), lambda j, k: (k, 0)), pl.BlockSpec((ts, tn), lambda j, k: (k, j))],
        out_specs=pl.BlockSpec((m, tn), lambda j, k: (0, j)),
        out_shape=jax.ShapeDtypeStruct((m, n), BF16),
        scratch_shapes=[pltpu.VMEM((m, tn), F32)],
        compiler_params=_params("parallel", "arbitrary"),
    )(a, b)


def _tn_shards_call(pos, a, b, qs, name):
    s, m = a.shape
    ts = min(s, 1024)
    nk = s // ts

    def body(pos_ref, a_ref, b_ref, o_ref, acc_ref):
        k = pl.program_id(1)

        @pl.when(k == 0)
        def _():
            acc_ref[...] = jnp.zeros((m, W_IN_SHARD), F32)

        acc_ref[...] += lax.dot_general(a_ref[...], b_ref[...], TN, preferred_element_type=F32)

        @pl.when(k == nk - 1)
        def _():
            o_ref[...] = acc_ref[...].astype(BF16)

    def shard(j, pos):
        q = qs[0]
        for n in range(1, len(qs)):
            q = jnp.where(j == n, qs[n], q)
        return jnp.bitwise_xor(pos[0], q)

    return pl.pallas_call(
        body, name=name,
        grid_spec=pltpu.PrefetchScalarGridSpec(
            num_scalar_prefetch=1, grid=(len(qs), nk),
            in_specs=[pl.BlockSpec((ts, m), lambda j, k, pos: (k, 0)),
                      pl.BlockSpec((ts, W_IN_SHARD), lambda j, k, pos: (k, shard(j, pos)))],
            out_specs=pl.BlockSpec((m, W_IN_SHARD), lambda j, k, pos: (0, j)),
            scratch_shapes=[pltpu.VMEM((m, W_IN_SHARD), F32)]),
        out_shape=jax.ShapeDtypeStruct((m, len(qs) * W_IN_SHARD), BF16),
        compiler_params=_params("parallel", "arbitrary"),
    )(pos, a, b)


def _mix_bwd_call(proj, dy, cos, sin, ln_g, ln_b, w_sp, w_sp_t, b_sp_t, sinks):
    s = proj.shape[0]
    nb = s // BLK
    rev = lambda i: nb - 1 - i
    prev = lambda i: jnp.maximum(nb - 2 - i, 0)

    def body(ua_ref, va_ref, za_ref, q_ref, k_ref, v_ref, zb0_ref, zb1_ref, kp_ref, vp_ref, dy_ref,
             cos_ref, sin_ref, cosp_ref, sinp_ref, lg_ref, lb_ref, w_ref, wt_ref, bt_ref, sinks_ref,
             dp_ref, lnst_ref, dw_ref, dbt_ref, dsink_ref,
             kdup_ref, vdup_ref, dvln_ref, qm_ref, dom_ref, ost_ref, dqst_ref, dkdup_ref, dvdup_ref, kcar_ref, vcar_ref,
             sigb_ref, bias_ref):
        i = pl.program_id(0)
        first_half, lo = _lane_masks()
        lane8 = lax.broadcasted_iota(jnp.int32, (8, LANE), 1)
        cos_t = cos_ref[...]
        sin_t = sin_ref[...]

        @pl.when(i == 0)
        def _():
            lnst_ref[...] = jnp.zeros((8, D_A), F32)
            dw_ref[...] = jnp.zeros((GROUPS, BLK, BLK), F32)
            dbt_ref[...] = jnp.zeros((BLK, LANE), F32)
            dsink_ref[...] = jnp.zeros((8, LANE), F32)
            kcar_ref[...] = jnp.zeros((BLK, D_KV), F32)
            vcar_ref[...] = jnp.zeros((BLK, D_KV), F32)
            _band_bias(bias_ref)

        vhat, rstd, vln = _layer_norm_fwd(va_ref[...], lg_ref[...], lb_ref[...])
        tril = _tril()
        triu = jnp.logical_not(tril) | (lax.broadcasted_iota(jnp.int32, (BLK, BLK), 0) == lax.broadcasted_iota(jnp.int32, (BLK, BLK), 1))
        lane_b = lax.broadcasted_iota(jnp.int32, (BLK, LANE), 1)
        db_acc = jnp.zeros((BLK, LANE), F32)
        for g in range(GROUPS):
            cols = slice(g * BLK, (g + 1) * BLK)
            vln_g = vln[:, cols].astype(BF16)
            wg = jnp.where(tril, w_ref[g], 0.0).astype(BF16)
            sg = jnp.dot(wg, vln_g, preferred_element_type=F32) + bt_ref[:, g:g + 1]
            za = za_ref[:, cols]
            gate, sig = _silu_parts(za)
            ua = ua_ref[:, cols]
            dya_g = dy_ref[:, cols]
            dya = dya_g * gate
            dp_ref[:, cols] = (dya * sg).astype(BF16)
            dp_ref[:, 2 * D_A + g * BLK:2 * D_A + (g + 1) * BLK] = (
                dya_g * (ua * sg) * (sig * (1.0 + za * (1.0 - sig)))).astype(BF16)
            ds = dya * ua
            ds_b = ds.astype(BF16)
            wtg = jnp.where(triu, wt_ref[g], 0.0).astype(BF16)
            dvln_ref[:, cols] = jnp.dot(wtg, ds_b, preferred_element_type=F32)
            dw_ref[g] += jnp.where(tril, lax.dot_general(ds_b, vln_g, NT, preferred_element_type=F32), 0.0)
            db_acc = db_acc + jnp.where(lane_b == g, jnp.sum(ds, axis=-1, keepdims=True), 0.0)
        dbt_ref[...] += db_acc
        dvln = dvln_ref[...]
        lnst_ref[0:1, :] += jnp.sum(dvln * vhat, axis=0, keepdims=True)
        lnst_ref[1:2, :] += jnp.sum(dvln, axis=0, keepdims=True)
        dvhat = dvln * lg_ref[...]
        m1 = jnp.mean(dvhat, axis=-1, keepdims=True)
        m2 = jnp.mean(dvhat * vhat, axis=-1, keepdims=True)
        dp_ref[:, D_A:2 * D_A] = (rstd * (dvhat - m1 - vhat * m2)).astype(BF16)

        cosp = cosp_ref[...]
        sinp = sinp_ref[...]
        for ks in range(2):
            cols = slice(ks * LANE, (ks + 1) * LANE)
            kr = _rope(k_ref[:, cols], cos_t, sin_t, first_half)
            kpr = _rope(kp_ref[:, cols], cosp, sinp, first_half)
            for n, (kc, vc, kp, vp) in enumerate(zip(_dup_kv(kr, lo), _dup_kv(v_ref[:, cols], lo),
                                                     _dup_kv(kpr, lo), _dup_kv(vp_ref[:, cols], lo))):
                kdup_ref[2 * ks + n, BLK:2 * BLK, :] = kc
                vdup_ref[2 * ks + n, BLK:2 * BLK, :] = vc
                kdup_ref[2 * ks + n, 0:BLK, :] = kp
                vdup_ref[2 * ks + n, 0:BLK, :] = vp
        for sb in range(8):
            cols = slice(sb * LANE, (sb + 1) * LANE)
            _stack_heads(qm_ref, sb, _rope(q_ref[:, cols], cos_t, sin_t, first_half) * SCALE, lo, BF16)
            zb = zb0_ref[:, cols] if sb < 4 else zb1_ref[:, (sb - 4) * LANE:(sb - 3) * LANE]
            gate, sig = _silu_parts(zb)
            sigb_ref[:, cols] = sig
            _stack_heads(dom_ref, sb, dy_ref[:, D_A + sb * LANE:D_A + (sb + 1) * LANE] * gate, lo, F32)

        block_kind = jnp.where(i < nb - 1, 1, 0)

        def kv_head(kh, dsink_acc):
            qm = qm_ref[kh]
            kd = kdup_ref[kh]
            vd = vdup_ref[kh]
            probs, psink = _softmax_sink(qm, kd, bias_ref[block_kind], _sink_column(sinks_ref, kh))
            probs_b = probs.astype(BF16)
            o = jnp.dot(probs_b, vd, preferred_element_type=F32)
            ost_ref[kh] = o
            dom = dom_ref[kh]
            dom_b = dom.astype(BF16)
            delta = jnp.sum(dom * o, axis=-1, keepdims=True)
            dpr = lax.dot_general(dom_b, vd, NT, preferred_element_type=F32)
            dss = (probs * (dpr - delta)).astype(BF16)
            sd = psink * delta
            for n in range(Q_PER_KV):
                dsink_acc = dsink_acc + jnp.where(lane8 == Q_PER_KV * kh + n, -jnp.sum(sd[n * BLK:(n + 1) * BLK]), 0.0)
            dqst_ref[kh] = jnp.dot(dss, kd, preferred_element_type=F32)
            dkdup_ref[kh] = lax.dot_general(dss, qm, TN, preferred_element_type=F32)
            dvdup_ref[kh] = lax.dot_general(probs_b, dom_b, TN, preferred_element_type=F32)
            return dsink_acc

        dsink_acc = lax.fori_loop(0, N_KV // 2, lambda j, acc: kv_head(2 * j + 1, kv_head(2 * j, acc)), jnp.zeros((8, LANE), F32))
        row0 = lax.broadcasted_iota(jnp.int32, (8, LANE), 0) == 0
        dsink_ref[...] += jnp.where(row0, dsink_acc, 0.0)

        for sb in range(8):
            cols = slice(sb * LANE, (sb + 1) * LANE)
            zb = zb0_ref[:, cols] if sb < 4 else zb1_ref[:, (sb - 4) * LANE:(sb - 3) * LANE]
            sig = sigb_ref[:, cols]
            dyb = dy_ref[:, D_A + sb * LANE:D_A + (sb + 1) * LANE]
            dp_ref[:, OFF_ZB + sb * LANE:OFF_ZB + (sb + 1) * LANE] = (
                dyb * _unstack_heads(ost_ref, sb, lo) * (sig * (1.0 + zb * (1.0 - sig)))).astype(BF16)
            dq_r = _unstack_heads(dqst_ref, sb, lo) * SCALE
            dp_ref[:, OFF_Q + sb * LANE:OFF_Q + (sb + 1) * LANE] = _unrope(dq_r, cos_t, sin_t, first_half).astype(BF16)

        lo2 = lax.broadcasted_iota(jnp.int32, (2 * BLK, LANE), 1) < HEAD
        for ks in range(2):
            cols = slice(ks * LANE, (ks + 1) * LANE)
            ka = dkdup_ref[2 * ks]
            kb = dkdup_ref[2 * ks + 1]
            dk_band = jnp.where(lo2, ka + pltpu.roll(ka, HEAD, 1), kb + pltpu.roll(kb, HEAD, 1))
            va_ = dvdup_ref[2 * ks]
            vb_ = dvdup_ref[2 * ks + 1]
            dv_band = jnp.where(lo2, va_ + pltpu.roll(va_, HEAD, 1), vb_ + pltpu.roll(vb_, HEAD, 1))
            dkr = dk_band[BLK:2 * BLK, :] + kcar_ref[:, cols]
            dp_ref[:, OFF_K + ks * LANE:OFF_K + (ks + 1) * LANE] = _unrope(dkr, cos_t, sin_t, first_half).astype(BF16)
            dp_ref[:, OFF_V + ks * LANE:OFF_V + (ks + 1) * LANE] = (
                dv_band[BLK:2 * BLK, :] + vcar_ref[:, cols]).astype(BF16)
            kcar_ref[:, cols] = dk_band[0:BLK, :]
            vcar_ref[:, cols] = dv_band[0:BLK, :]

    tab = pl.BlockSpec((BLK, LANE), lambda i: (rev(i), 0))
    tabp = pl.BlockSpec((BLK, LANE), lambda i: (prev(i), 0))
    kvp = lambda col: pl.BlockSpec((BLK, D_KV), lambda i: (prev(i), col))
    vec = pl.BlockSpec((1, D_A), lambda i: (0, 0))
    w3 = pl.BlockSpec((GROUPS, BLK, BLK), lambda i: (0, 0, 0))
    return pl.pallas_call(
        body, name="mix_bwd", grid=(nb,),
        in_specs=_proj_specs(nb) + [
            kvp(OFF_K // D_KV), kvp(OFF_V // D_KV), pl.BlockSpec((BLK, 2 * D_A), lambda i: (rev(i), 0)),
            tab, tab, tabp, tabp, vec, vec, w3, w3, pl.BlockSpec((BLK, GROUPS), lambda i: (0, 0)),
            pl.BlockSpec(memory_space=pltpu.SMEM)],
        out_specs=[pl.BlockSpec((BLK, D_IN), lambda i: (rev(i), 0)), pl.BlockSpec((8, D_A), lambda i: (0, 0)), w3,
                   pl.BlockSpec((BLK, LANE), lambda i: (0, 0)), pl.BlockSpec((8, LANE), lambda i: (0, 0))],
        out_shape=[jax.ShapeDtypeStruct((s, D_IN), BF16), jax.ShapeDtypeStruct((8, D_A), F32),
                   jax.ShapeDtypeStruct((GROUPS, BLK, BLK), F32), jax.ShapeDtypeStruct((BLK, LANE), F32),
                   jax.ShapeDtypeStruct((8, LANE), F32)],
        scratch_shapes=[pltpu.VMEM((N_KV, 2 * BLK, LANE), BF16), pltpu.VMEM((N_KV, 2 * BLK, LANE), BF16),
                        pltpu.VMEM((BLK, D_A), F32), pltpu.VMEM((N_KV, Q_PER_KV * BLK, LANE), BF16),
                        pltpu.VMEM((N_KV, Q_PER_KV * BLK, LANE), F32), pltpu.VMEM((N_KV, Q_PER_KV * BLK, LANE), F32),
                        pltpu.VMEM((N_KV, Q_PER_KV * BLK, LANE), F32), pltpu.VMEM((N_KV, 2 * BLK, LANE), F32),
                        pltpu.VMEM((N_KV, 2 * BLK, LANE), F32), pltpu.VMEM((BLK, D_KV), F32), pltpu.VMEM((BLK, D_KV), F32),
                        pltpu.VMEM((BLK, D_B), F32), pltpu.VMEM((2, Q_PER_KV * BLK, 2 * BLK), F32)],
        compiler_params=_params("arbitrary"),
    )(proj, proj, proj, proj, proj, proj, proj, proj, proj, proj, dy, cos, sin, cos, sin, ln_g, ln_b, w_sp, w_sp_t,
      b_sp_t, sinks)


def _dh_call(dproj, w_bf, x, dx2, scale, norm_g):
    s = x.shape[0]
    tm = min(s, 512)
    tk = W_IN_SHARD
    nk = D_IN // tk

    def body(dp_ref, w_ref, x_ref, dx2_ref, sc_ref, g_ref, gx_ref, st_ref, acc_ref):
        i = pl.program_id(0)
        k = pl.program_id(1)

        @pl.when((i == 0) & (k == 0))
        def _():
            st_ref[...] = jnp.zeros((8, D), F32)

        @pl.when(k == 0)
        def _():
            acc_ref[...] = jnp.zeros((tm, D), F32)

        acc_ref[...] += lax.dot_general(dp_ref[...], w_ref[...], NT, preferred_element_type=F32)

        @pl.when(k == nk - 1)
        def _():
            g = g_ref[...]
            one_sc = 1.0 + sc_ref[...]

            def chunk(n, carry):
                rows = pl.ds(pl.multiple_of(n * BLK, BLK), BLK)
                dh = acc_ref[rows, :]
                xv = x_ref[rows, :]
                r = lax.rsqrt(jnp.mean(xv * xv, axis=-1, keepdims=True) + EPS)
                xn = xv * r
                dhn = dh * one_sc
                dxn = dhn * g
                gx_ref[rows, :] = dx2_ref[rows, :] + r * (dxn - xn * jnp.mean(dxn * xn, axis=-1, keepdims=True))
                st_ref[0:1, :] += jnp.sum(dh, axis=0, keepdims=True)
                st_ref[1:2, :] += jnp.sum(dh * (xn * g), axis=0, keepdims=True)
                st_ref[2:3, :] += jnp.sum(dhn * xn, axis=0, keepdims=True)
                return carry

            lax.fori_loop(0, tm // BLK, chunk, 0)

    vec = pl.BlockSpec((1, D), lambda i, k: (0, 0))
    rows = lambda: pl.BlockSpec((tm, D), lambda i, k: (i, 0))
    return pl.pallas_call(
        body, name="dh", grid=(s // tm, nk),
        in_specs=[pl.BlockSpec((tm, tk), lambda i, k: (i, k)), pl.BlockSpec((D, tk), lambda i, k: (0, k)), rows(), rows(), vec, vec],
        out_specs=[rows(), pl.BlockSpec((8, D), lambda i, k: (0, 0))],
        out_shape=[jax.ShapeDtypeStruct((s, D), F32), jax.ShapeDtypeStruct((8, D), F32)],
        scratch_shapes=[pltpu.VMEM((tm, D), F32)],
        compiler_params=_params("arbitrary", "arbitrary"),
    )(dproj, w_bf, x, dx2, scale, norm_g)


def _adam_math(w, g, m, v):
    m_new = ADAM_B1 * m + (1.0 - ADAM_B1) * g
    v_new = ADAM_B2 * v + (1.0 - ADAM_B2) * (g * g)
    m_hat = m_new / ADAM_C1
    v_hat = v_new / ADAM_C2
    delta = -ADAM_LR * (m_hat / (jnp.sqrt(v_hat) + ADAM_EPS) + ADAM_WD * w)
    return delta, m_new, v_new


def _adam_small_call(tensors):
    n = len(tensors)

    def body(*refs):
        ins, outs = refs[:4 * n], refs[4 * n:]
        for t in range(n):
            w_ref, g_ref, m_ref, v_ref = ins[4 * t:4 * t + 4]
            d, mo, vo = _adam_math(w_ref[...], g_ref[...], m_ref[...], v_ref[...])
            outs[3 * t][...], outs[3 * t + 1][...], outs[3 * t + 2][...] = d, mo, vo

    vm = pl.BlockSpec(memory_space=pltpu.VMEM)
    flat = [a for t in tensors for a in t]
    out = pl.pallas_call(
        body, name="adam_small", in_specs=[vm] * (4 * n), out_specs=[vm] * (3 * n),
        out_shape=[jax.ShapeDtypeStruct(t[0].shape, F32) for t in tensors for _ in range(3)],
        compiler_params=pltpu.CompilerParams(vmem_limit_bytes=VMEM_LIMIT),
    )(*flat)
    return [tuple(out[3 * t:3 * t + 3]) for t in range(n)]


def _adam_halves_call(pos, w, mine, theirs, m, v, name):
    r, n = w.shape
    half = r // 2
    tr = ADAM_ROWS
    nh = half // tr

    def body(pos_ref, w_ref, mine_ref, theirs_ref, m_ref, v_ref, g_ref, d_ref, mo_ref, vo_ref):
        is_mine = (pl.program_id(0) // nh) == pos_ref[1]
        g = jnp.where(is_mine, mine_ref[...], theirs_ref[...])
        g_ref[...] = g
        d_ref[...], mo_ref[...], vo_ref[...] = _adam_math(w_ref[...], g, m_ref[...], v_ref[...])

    spec = lambda: pl.BlockSpec((tr, n), lambda i, pos: (i, 0))
    hspec = lambda: pl.BlockSpec((tr, n), lambda i, pos: (i % nh, 0))
    return pl.pallas_call(
        body, name=name,
        grid_spec=pltpu.PrefetchScalarGridSpec(
            num_scalar_prefetch=1, grid=(r // tr,), in_specs=[spec(), hspec(), hspec(), spec(), spec()],
            out_specs=[spec() for _ in range(4)]),
        out_shape=[jax.ShapeDtypeStruct((r, n), F32)] * 4, compiler_params=_params("parallel"),
    )(pos, w, mine, theirs, m, v)


def _adam_outer_call(w, ct, dm, m, v, name):
    r, n = w.shape
    tr = ADAM_ROWS

    def body(w_ref, ct_ref, dm_ref, m_ref, v_ref, g_ref, d_ref, mo_ref, vo_ref):
        g = ct_ref[:, 0:1] * dm_ref[0:1, :]
        for b in range(1, N_DEV):
            g = g + ct_ref[:, b:b + 1] * dm_ref[b:b + 1, :]
        g_ref[...] = g
        d_ref[...], mo_ref[...], vo_ref[...] = _adam_math(w_ref[...], g, m_ref[...], v_ref[...])

    spec = lambda: pl.BlockSpec((tr, n), lambda i: (i, 0))
    return pl.pallas_call(
        body, name=name, grid=(r // tr,),
        in_specs=[spec(), pl.BlockSpec((tr, N_DEV), lambda i: (i, 0)), pl.BlockSpec((N_DEV, n), lambda i: (0, 0)), spec(), spec()],
        out_specs=[spec() for _ in range(4)],
        out_shape=[jax.ShapeDtypeStruct((r, n), F32)] * 4, compiler_params=_params("parallel"),
    )(w, ct, dm, m, v)


def _sum_pieces_call(pos, part, part_block, recvs, name):
    r, n = recvs[0].shape[1:]
    tr = min(r, 256)
    nrb = r // tr

    def body(pos_ref, p_ref, *refs):
        acc = p_ref[...].astype(F32)
        for r_ref in refs[:-1]:
            for d in range(r_ref.shape[0]):
                acc = acc + r_ref[d].astype(F32)
        refs[-1][...] = acc

    return pl.pallas_call(
        body, name=name,
        grid_spec=pltpu.PrefetchScalarGridSpec(
            num_scalar_prefetch=1, grid=(nrb,),
            in_specs=[pl.BlockSpec((tr, n), lambda i, pos: part_block(i, pos, nrb))] + [
                pl.BlockSpec((rv.shape[0], tr, n), lambda i, pos: (0, i, 0)) for rv in recvs],
            out_specs=pl.BlockSpec((tr, n), lambda i, pos: (i, 0))),
        out_shape=jax.ShapeDtypeStruct((r, n), F32), compiler_params=_params("parallel"),
    )(pos, part, *recvs)


def _coords():
    return lax.axis_index("x"), lax.axis_index("y"), lax.axis_index("c")


def _allgather_sum_call(blk, name, with_sum):
    m_per, n = blk.shape

    def body(x_ref, out_ref, *rest):
        if with_sum:
            sum_ref, send_sems, recv_sems, local_sem = rest
        else:
            send_sems, recv_sems, local_sem = rest
        x, y, c = _coords()
        me, sibling = (x, y, c), (x, y, 1 - c)
        chips = [(1 - x, y), (x, 1 - y), (1 - x, 1 - y)]

        def rows(px, py, pc):
            return out_ref.at[pl.ds((4 * px + 2 * py + pc) * m_per, m_per), :]

        def copy(k, block, to, src=None):
            return pltpu.make_async_remote_copy(
                src_ref=rows(*block) if src is None else src, dst_ref=rows(*block),
                send_sem=send_sems.at[k], recv_sem=recv_sems.at[k], device_id=to, device_id_type=MESH)

        mine = pltpu.make_async_copy(x_ref, rows(*me), local_sem)
        mine.start()
        first = [copy(0, me, sibling, src=x_ref)]
        first += [copy(1 + j, me, (*chip, c), src=x_ref) for j, chip in enumerate(chips)]
        for cp in first:
            cp.start()
        passed = [copy(4 + j, (*chip, c), sibling) for j, chip in enumerate(chips)]
        for j, chip in enumerate(chips):
            copy(1 + j, (*chip, c), me).wait_recv()
            passed[j].start()
        copy(0, sibling, me).wait_recv()
        for j, chip in enumerate(chips):
            copy(4 + j, (*chip, 1 - c), me).wait_recv()
        for cp in first + passed:
            cp.wait_send()
        mine.wait()
        if with_sum:
            acc = out_ref[0:m_per, :]
            for d in range(1, N_DEV):
                acc = acc + out_ref[d * m_per:(d + 1) * m_per, :]
            sum_ref[...] = acc

    vm = pl.BlockSpec(memory_space=pltpu.VMEM)
    out_shape = [jax.ShapeDtypeStruct((N_DEV * m_per, n), F32)]
    if with_sum:
        out_shape.append(jax.ShapeDtypeStruct((m_per, n), F32))
    return pl.pallas_call(
        body, name=name, out_shape=out_shape, in_specs=[vm], out_specs=[vm] * len(out_shape),
        scratch_shapes=[pltpu.SemaphoreType.DMA((7,)), pltpu.SemaphoreType.DMA((7,)), pltpu.SemaphoreType.DMA],
        compiler_params=pltpu.CompilerParams(vmem_limit_bytes=VMEM_LIMIT),
    )(blk)


def _weights_gather_call(wi_full, wo_full):
    hi = D // 2
    ho = W_OUT_SHARD // 2

    def body(wi_in, wo_in, fi_ref, fo_ref, send_sems, recv_sems):
        del wi_in, wo_in
        x, y, c = _coords()
        sibling = (x, y, 1 - c)
        chips = [(1 - x, y), (x, 1 - y), (1 - x, 1 - y)]

        def half(which, px, py, pc):
            j = 2 * px + py
            if which == 0:
                return fi_ref.at[pl.ds(pc * hi, hi), pl.ds(j * W_IN_SHARD, W_IN_SHARD)]
            return fo_ref.at[pl.ds(j * W_OUT_SHARD + pc * ho, ho), :]

        def copy(k, which, block, to):
            return pltpu.make_async_remote_copy(
                src_ref=half(which, *block), dst_ref=half(which, *block), send_sem=send_sems.at[k],
                recv_sem=recv_sems.at[k], device_id=to, device_id_type=MESH)

        first = [copy(6 * w + j, w, (x, y, c), (*chip, c)) for w in range(2) for j, chip in enumerate(chips)]
        for cp in first:
            cp.start()
        passed = []
        for w in range(2):
            for j, chip in enumerate(chips):
                copy(6 * w + j, w, (*chip, c), (x, y, c)).wait_recv()
                cp = copy(6 * w + 3 + j, w, (*chip, c), sibling)
                cp.start()
                passed.append(cp)
        for w in range(2):
            for j, chip in enumerate(chips):
                copy(6 * w + 3 + j, w, (*chip, 1 - c), (x, y, c)).wait_recv()
        for cp in first + passed:
            cp.wait_send()

    anyspec = pl.BlockSpec(memory_space=pl.ANY)
    return pl.pallas_call(
        body, name="weights_gather",
        out_shape=[jax.ShapeDtypeStruct((D, D_IN), BF16), jax.ShapeDtypeStruct((D, D), BF16)],
        in_specs=[anyspec, anyspec], out_specs=[anyspec, anyspec], input_output_aliases={0: 0, 1: 1},
        scratch_shapes=[pltpu.SemaphoreType.DMA((12,)), pltpu.SemaphoreType.DMA((12,))],
    )(wi_full, wo_full)


HBM_SPEC = pl.BlockSpec(memory_space=pltpu.HBM)
SEM_SPEC = pl.BlockSpec(memory_space=pltpu.SEMAPHORE)
SIDE_EFFECT = pltpu.SideEffectType.DATAFLOW_SIDE_EFFECTING


def _peer(x, y, c, q, cb):
    return (1 - x if q & 2 else x, 1 - y if q & 1 else y, 1 - c if cb else c)


def _w_in_piece(slots):
    def piece(part_ref, k, to):
        return part_ref.at[pl.ds(to[2] * (D // 2), D // 2), pl.ds(slots[k] * W_IN_SHARD, W_IN_SHARD)]
    return piece


def _w_out_piece(part_ref, k, to):
    ho = W_OUT_SHARD // 2
    return part_ref.at[pl.ds((2 * to[0] + to[1]) * W_OUT_SHARD + to[2] * ho, ho), :]


def _group_piece(part_ref, k, to):
    return part_ref.at[4 * to[0] + 2 * to[1] + to[2]]


def _whole_piece(part_ref, k, to):
    return part_ref


def _exchange_start_call(part, rels, piece, slot_shape, name):
    n = len(rels)
    land = lax.empty((n,) + slot_shape, part.dtype)

    def body(part_ref, land_ref, send_sems, recv_sems, part_thru, land_thru, token):
        x, y, c = _coords()
        for k, (q, cb) in enumerate(rels):
            to = _peer(x, y, c, q, cb)
            pltpu.make_async_remote_copy(src_ref=piece(part_ref, k, to), dst_ref=land_ref.at[k], send_sem=send_sems.at[k],
                                         recv_sem=recv_sems.at[k], device_id=to, device_id_type=MESH).start()
        token[...] = jnp.zeros_like(token)

    return pl.pallas_call(
        body, name=name,
        out_shape=(pltpu.SemaphoreType.DMA((n,)), pltpu.SemaphoreType.DMA((n,)), pltpu.HBM(part.shape, part.dtype),
                   pltpu.HBM(land.shape, land.dtype), jax.ShapeDtypeStruct((8, LANE), F32)),
        in_specs=(HBM_SPEC, HBM_SPEC), out_specs=(SEM_SPEC, SEM_SPEC, HBM_SPEC, HBM_SPEC, pl.BlockSpec(memory_space=pltpu.VMEM)),
        input_output_aliases={0: 2, 1: 3},
        compiler_params=pltpu.CompilerParams(has_side_effects=SIDE_EFFECT),
    )(pltpu.with_memory_space_constraint(part, pltpu.HBM), pltpu.with_memory_space_constraint(land, pltpu.HBM))


def _exchange_wait_call(started, rels, piece, after, name):
    send_sems, recv_sems, part_thru, land_thru, _ = started

    def body(part_ref, land_ref, send_sems, recv_sems, after_ref, part_out, land_out):
        x, y, c = _coords()
        for k, (q, cb) in enumerate(rels):
            to = _peer(x, y, c, q, cb)
            cp = pltpu.make_async_remote_copy(src_ref=piece(part_ref, k, to), dst_ref=land_ref.at[k], send_sem=send_sems.at[k],
                                              recv_sem=recv_sems.at[k], device_id=to, device_id_type=MESH)
            cp.wait_send()
            cp.wait_recv()

    return pl.pallas_call(
        body, name=name,
        out_shape=(pltpu.HBM(part_thru.shape, part_thru.dtype), pltpu.HBM(land_thru.shape, land_thru.dtype)),
        in_specs=(HBM_SPEC, HBM_SPEC, SEM_SPEC, SEM_SPEC, pl.BlockSpec(memory_space=pl.ANY)), out_specs=(HBM_SPEC, HBM_SPEC),
        input_output_aliases={0: 0, 1: 1},
        compiler_params=pltpu.CompilerParams(has_side_effects=SIDE_EFFECT),
    )(part_thru, land_thru, send_sems, recv_sems, after)


def _pair_exchange_call(gi, go):
    hi = D // 2
    ho = W_OUT_SHARD // 2

    def body(gi_in, go_in, fi_ref, fo_ref, send_sems, recv_sems):
        del gi_in, go_in
        x, y, c = _coords()
        sibling = (x, y, 1 - c)
        mine = (fi_ref.at[pl.ds(c * hi, hi), :], fo_ref.at[pl.ds(c * ho, ho), :])
        theirs = (fi_ref.at[pl.ds((1 - c) * hi, hi), :], fo_ref.at[pl.ds((1 - c) * ho, ho), :])
        sends = [pltpu.make_async_remote_copy(src_ref=ref, dst_ref=ref, send_sem=send_sems.at[k], recv_sem=recv_sems.at[k],
                                              device_id=sibling, device_id_type=MESH) for k, ref in enumerate(mine)]
        for cp in sends:
            cp.start()
        for k, ref in enumerate(theirs):
            pltpu.make_async_remote_copy(src_ref=ref, dst_ref=ref, send_sem=send_sems.at[k], recv_sem=recv_sems.at[k],
                                         device_id=sibling, device_id_type=MESH).wait_recv()
        for cp in sends:
            cp.wait_send()

    anyspec = pl.BlockSpec(memory_space=pl.ANY)
    return pl.pallas_call(
        body, name="pair_exchange",
        out_shape=[jax.ShapeDtypeStruct((D, W_IN_SHARD), F32), jax.ShapeDtypeStruct((W_OUT_SHARD, D), F32)],
        in_specs=[anyspec, anyspec], out_specs=[anyspec, anyspec], input_output_aliases={0: 0, 1: 1},
        scratch_shapes=[pltpu.SemaphoreType.DMA((2,)), pltpu.SemaphoreType.DMA((2,))],
    )(gi, go)


def _rope_tables(s):
    inv_freq = 10000.0 ** (-jnp.arange(0, HEAD, 2, dtype=F32) / HEAD)
    ang = jnp.arange(s, dtype=F32)[:, None] * inv_freq[None, :]
    cos = jnp.tile(jnp.cos(ang), (1, LANE // (HEAD // 2)))
    sin = jnp.tile(jnp.sin(ang), (1, LANE // (HEAD // 2)))
    first_half = (jnp.arange(LANE) % HEAD) < (HEAD // 2)
    return cos, jnp.where(first_half[None, :], -sin, sin)


def _pad_cols(a, n):
    return jnp.pad(a, ((0, 0), (0, n - a.shape[1])))


def kernel(x, c, w_ada, b_ada, norm_g, w_in, ln_v_g, ln_v_b, w_spatial, b_spatial, sinks, w_out, w_ada_final, b_ada_final, final_norm_g, loss_target, m_w_ada, m_b_ada, m_norm_g, m_w_in, m_ln_v_g, m_ln_v_b, m_w_spatial, m_b_spatial, m_sinks, m_w_out, m_w_ada_final, m_b_ada_final, m_final_norm_g, v_w_ada, v_b_ada, v_norm_g, v_w_in, v_ln_v_g, v_ln_v_b, v_w_spatial, v_b_spatial, v_sinks, v_w_out, v_w_ada_final, v_b_ada_final, v_final_norm_g):
    s = x.shape[1]
    ax, ay, ac = _coords()
    chip = 2 * ax + ay
    me = 4 * ax + 2 * ay + ac
    n_ada = w_ada.shape[2]
    n_adaf = w_ada_final.shape[1]

    x2d = x.reshape(s, D)
    tgt = loss_target.reshape(s, D)
    w_ada2, w_in2, w_out2 = w_ada[0], w_in[0], w_out[0]
    b_ada_f2 = b_ada_final.reshape(1, 2 * D)
    gf = final_norm_g.reshape(1, D)

    c_all = _allgather_sum_call(jnp.pad(c, ((0, 7), (0, 0))), "gather_c", False)[0][::8]
    mod_p, c_act = _rowmat_call(c_all, w_ada2, lax.dynamic_slice(b_ada, (0, chip * n_ada), (1, n_ada)), "mod")
    modf_p, _ = _rowmat_call(c_all, w_ada_final, lax.dynamic_slice(b_ada_f2, (0, chip * n_adaf), (1, n_adaf)), "mod_final")
    mods = _allgather_sum_call(jnp.concatenate([mod_p, modf_p], axis=1), "gather_mod", False)[0]
    my_rows = [lax.dynamic_slice(mods, (16 * j + me, 0), (1, n_ada + n_adaf)) for j in range(N_CHIP)]
    mod = jnp.concatenate([r[:, :n_ada] for r in my_rows], axis=1)
    mod_f = jnp.concatenate([r[:, n_ada:] for r in my_rows], axis=1)
    shift, scale, gate = mod[:, :D], mod[:, D:2 * D], mod[:, 2 * D:]
    shift_f, scale_f = mod_f[:, :D], mod_f[:, D:]

    pos = jnp.stack([chip, ac]).astype(jnp.int32)
    w_in_own = _cast_into_call(pos, w_in2, (D, D_IN), "cast_w_in")
    w_out_own = _cast_into_call(pos, w_out2, (D, D), "cast_w_out")

    cos, sin = _rope_tables(s)
    b_sp_t = b_spatial[0].T
    sinks1 = sinks.reshape(N_Q)
    h, proj, w_in_bf, w_out_bf = _proj_gather_call(pos, x2d, shift, scale, norm_g, w_in_own, w_out_own)
    y = _mix_fwd_call(proj, cos, sin, ln_v_g, ln_v_b, w_spatial[0], b_sp_t, sinks1)
    dx2, do, dy, st_tail = _tail_call(y, w_out_bf, x2d, tgt, gate, shift_f, scale_f, gf)

    rel_o = [(0, 1), (1, 0), (1, 1), (2, 0), (2, 1), (3, 0), (3, 1)]
    rel_a = [(1, 0), (1, 1), (2, 0), (2, 1)]
    rel_b = [(3, 0), (3, 1), (0, 1)]
    piece_a, piece_b = _w_in_piece([0, 0, 1, 1]), _w_in_piece([0, 0, 1])
    half_in, half_out = (D // 2, W_IN_SHARD), (W_OUT_SHARD // 2, D)

    g_w_out_p = _tn_call(y, do, "grad_w_out")
    st_o = _exchange_start_call(g_w_out_p, rel_o, _w_out_piece, half_out, "send_w_out")
    dproj, st_ln, d_wsp, d_bsp_t, d_sink = _mix_bwd_call(
        proj, dy, cos, sin, ln_v_g + st_o[4][0:1, 0:1], ln_v_b, w_spatial[0], jnp.swapaxes(w_spatial[0], 1, 2), b_sp_t, sinks1)
    g_w_in_a = _tn_shards_call(pos, h, dproj, (1, 2), "grad_w_in_a")
    st_a = _exchange_start_call(g_w_in_a, rel_a, piece_a, half_in, "send_w_in_a")
    g_w_in_b = _tn_shards_call(pos, h, dproj, (3, 0), "grad_w_in_b")
    st_b = _exchange_start_call(g_w_in_b, rel_b, piece_b, half_in, "send_w_in_b")
    rel_all = rel_o
    st_s = _exchange_start_call(d_wsp, rel_all, _group_piece, (BLK, BLK), "send_w_spatial")
    sent = st_a[4][0:1, 0:1] + st_b[4][0:1, 0:1] + st_s[4][0:1, 0:1]
    grad_x, st_dh = _dh_call(dproj, w_in_bf, x2d, dx2, scale + sent, norm_g)

    g_w_out_p, recv_o = _exchange_wait_call(st_o, rel_o, _w_out_piece, st_dh, "wait_w_out")
    _, recv_a = _exchange_wait_call(st_a, rel_a, piece_a, st_dh, "wait_w_in_a")
    g_w_in_b, recv_b = _exchange_wait_call(st_b, rel_b, piece_b, st_dh, "wait_w_in_b")
    d_wsp, recv_s = _exchange_wait_call(st_s, rel_all, _group_piece, st_dh, "wait_w_spatial")
    mine_in = _sum_pieces_call(pos, g_w_in_b, lambda i, p, nrb: (p[1] * nrb + i, 1), [recv_a, recv_b], "sum_w_in")
    mine_out = _sum_pieces_call(pos, g_w_out_p, lambda i, p, nrb: ((2 * p[0] + p[1]) * nrb + i, 0), [recv_o], "sum_w_out")
    wsp_group = _sum_pieces_call(pos, d_wsp.reshape(GROUPS * BLK, BLK), lambda i, p, nrb: (2 * p[0] + p[1], 0), [recv_s],
                                 "sum_w_spatial")
    to_sibling = [(0, 1)]
    st_pi = _exchange_start_call(mine_in, to_sibling, _whole_piece, half_in, "swap_w_in")
    st_po = _exchange_start_call(mine_out, to_sibling, _whole_piece, half_out, "swap_w_out")

    misc = jnp.concatenate([st_ln, d_bsp_t[:, :GROUPS].T, d_sink, jnp.zeros((8, D - D_A - 2 * LANE), F32)], axis=1)
    pack = jnp.concatenate([wsp_group.reshape(8, D) + (st_pi[4][0:1, 0:1] + st_po[4][0:1, 0:1]), st_tail, st_dh, misc], axis=0)
    rows = pack.shape[0]
    packs, tot = _allgather_sum_call(pack, "gather_small", True)
    packs = packs.reshape(N_DEV, rows, D)
    dmod_all = jnp.concatenate([packs[:, 16, :], packs[:, 17, :], packs[:, 11, :]], axis=1)
    dmodf_all = jnp.concatenate([packs[:, 8, :], packs[:, 9, :]], axis=1)
    loss = tot[13, 0]
    mine_in, theirs_in = _exchange_wait_call(st_pi, to_sibling, _whole_piece, tot, "swapped_w_in")
    mine_out, theirs_out = _exchange_wait_call(st_po, to_sibling, _whole_piece, tot, "swapped_w_out")
    small = {
        "b_ada": jnp.concatenate([tot[16:17], tot[17:18], tot[11:12]], axis=1),
        "norm_g": tot[18:19],
        "ln_v_g": tot[24:25, :D_A],
        "ln_v_b": tot[25:26, :D_A],
        "w_spatial": packs[:, 0:8, :].reshape(GROUPS * BLK, BLK),
        "b_spatial": tot[24:32, D_A:D_A + BLK],
        "sinks": tot[24:25, D_A + LANE:D_A + LANE + N_Q],
        "b_ada_final": jnp.concatenate([tot[8:9], tot[9:10]], axis=1),
        "final_norm_g": tot[10:11],
    }

    weights = dict(w_ada=w_ada, b_ada=b_ada, norm_g=norm_g, w_in=w_in, ln_v_g=ln_v_g, ln_v_b=ln_v_b, w_spatial=w_spatial,
                   b_spatial=b_spatial, sinks=sinks, w_out=w_out, w_ada_final=w_ada_final, b_ada_final=b_ada_final,
                   final_norm_g=final_norm_g)
    m_in = dict(w_ada=m_w_ada, b_ada=m_b_ada, norm_g=m_norm_g, w_in=m_w_in, ln_v_g=m_ln_v_g, ln_v_b=m_ln_v_b,
                w_spatial=m_w_spatial, b_spatial=m_b_spatial, sinks=m_sinks, w_out=m_w_out, w_ada_final=m_w_ada_final,
                b_ada_final=m_b_ada_final, final_norm_g=m_final_norm_g)
    v_in = dict(w_ada=v_w_ada, b_ada=v_b_ada, norm_g=v_norm_g, w_in=v_w_in, ln_v_g=v_ln_v_g, ln_v_b=v_ln_v_b,
                w_spatial=v_w_spatial, b_spatial=v_b_spatial, sinks=v_sinks, w_out=v_w_out, w_ada_final=v_w_ada_final,
                b_ada_final=v_b_ada_final, final_norm_g=v_final_norm_g)
    c_act_t = c_act.T
    outer = {"w_ada": lax.dynamic_slice(dmod_all, (0, chip * n_ada), (N_DEV, n_ada)),
             "w_ada_final": lax.dynamic_slice(dmodf_all, (0, chip * n_adaf), (N_DEV, n_adaf))}
    halves = {"w_in": (mine_in, theirs_in[0]), "w_out": (mine_out, theirs_out[0])}
    done = {}
    for name, (mine, theirs) in halves.items():
        shape2 = (2 * mine.shape[0], mine.shape[1])
        done[name] = _adam_halves_call(pos, weights[name].reshape(shape2), mine, theirs, m_in[name].reshape(shape2),
                                       v_in[name].reshape(shape2), "adam_" + name)
    for name, dm in outer.items():
        shape2 = (D, dm.shape[1])
        done[name] = _adam_outer_call(weights[name].reshape(shape2), c_act_t, dm, m_in[name].reshape(shape2),
                                      v_in[name].reshape(shape2), "adam_" + name)
    updates = _adam_small_call([(weights[name].reshape(g.shape), g, m_in[name].reshape(g.shape), v_in[name].reshape(g.shape))
                                for name, g in small.items()])
    for (name, g), upd in zip(small.items(), updates):
        done[name] = (g, *upd)
    outs = [[done[name][k].reshape(w.shape) for name, w in weights.items()] for k in range(4)]
    return (loss, grad_x.reshape(x.shape), *outs[0], *outs[1], *outs[2], *outs[3])
```

```python
import jax
import jax.numpy as jnp
from jax import lax
from jax.experimental import pallas as pl
from jax.experimental.pallas import tpu as pltpu

F32 = jnp.float32
BF16 = jnp.bfloat16
MESH = pl.DeviceIdType.MESH

D = 2048
D_A = 1024
D_B = 1024
D_KV = 256
HEAD = 64
N_Q = 16
N_KV = 4
Q_PER_KV = N_Q // N_KV
BLK = 128
GROUPS = 8
D_IN = 5632
OFF_Q, OFF_K, OFF_V, OFF_ZB = 3072, 4096, 4352, 4608
N_CHIP = 4
N_DEV = 8
W_IN_SHARD = D_IN // N_CHIP
W_OUT_SHARD = D // N_CHIP
EPS = 1e-5
SCALE = HEAD ** -0.5
NEG = -1e30
LANE = 128
VMEM_LIMIT = 56 * 1024 * 1024

ADAM_LR, ADAM_B1, ADAM_B2, ADAM_EPS, ADAM_WD, ADAM_STEP = 0.001, 0.9, 0.999, 1e-08, 0.01, 10
ADAM_C1 = 1.0 - ADAM_B1 ** ADAM_STEP
ADAM_C2 = 1.0 - ADAM_B2 ** ADAM_STEP
ADAM_ROWS = 256

NT = (((1,), (1,)), ((), ()))
TN = (((0,), (0,)), ((), ()))


def _params(*sem):
    return pltpu.CompilerParams(dimension_semantics=sem, vmem_limit_bytes=VMEM_LIMIT)


def _silu_parts(z):
    sig = 1.0 / (1.0 + jnp.exp(-z))
    return z * sig, sig


def _swap_halves(v, first_half):
    return jnp.where(first_half, pltpu.roll(v, 96, 1), pltpu.roll(v, 32, 1))


def _rope(v, cos_t, sin_s, first_half):
    return v * cos_t + _swap_halves(v, first_half) * sin_s


def _unrope(dv, cos_t, sin_s, first_half):
    return dv * cos_t - _swap_halves(dv, first_half) * sin_s


def _lane_masks():
    lane = lax.broadcasted_iota(jnp.int32, (BLK, LANE), 1)
    return (lane % HEAD) < (HEAD // 2), lane < HEAD


def _band_valid(first_block_bound, rows=BLK):
    rr = lax.broadcasted_iota(jnp.int32, (rows, 2 * BLK), 0) & (BLK - 1)
    jj = lax.broadcasted_iota(jnp.int32, (rows, 2 * BLK), 1)
    return (jj > rr) & (jj <= rr + BLK) & (jj >= first_block_bound)


def _dup_kv(slab, lo):
    rolled = pltpu.roll(slab, HEAD, 1)
    return jnp.where(lo, slab, rolled).astype(BF16), jnp.where(lo, rolled, slab).astype(BF16)


def _stack_heads(ref, sb, slab, lo, dtype):
    kh, base = sb // 2, 2 * (sb % 2) * BLK
    zero = jnp.zeros_like(slab)
    ref[kh, base:base + BLK, :] = jnp.where(lo, slab, zero).astype(dtype)
    ref[kh, base + BLK:base + 2 * BLK, :] = jnp.where(lo, zero, slab).astype(dtype)


def _unstack_heads(ref, sb, lo):
    kh, base = sb // 2, 2 * (sb % 2) * BLK
    return jnp.where(lo, ref[kh, base:base + BLK, :], ref[kh, base + BLK:base + 2 * BLK, :])


def _sink_column(sinks_ref, kh):
    row = lax.broadcasted_iota(jnp.int32, (Q_PER_KV * BLK, 1), 0)
    col = jnp.full(row.shape, sinks_ref[Q_PER_KV * kh + Q_PER_KV - 1], F32)
    for n in range(Q_PER_KV - 2, -1, -1):
        col = jnp.where(row < (n + 1) * BLK, sinks_ref[Q_PER_KV * kh + n], col)
    return col


def _tril():
    t = lax.broadcasted_iota(jnp.int32, (BLK, BLK), 0)
    s = lax.broadcasted_iota(jnp.int32, (BLK, BLK), 1)
    return s <= t


def _layer_norm_fwd(va, lg, lb):
    mu = jnp.mean(va, axis=-1, keepdims=True)
    xc = va - mu
    rstd = lax.rsqrt(jnp.mean(xc * xc, axis=-1, keepdims=True) + EPS)
    vhat = xc * rstd
    return vhat, rstd, vhat * lg + lb


def _softmax_sink(qm, kdup, bias, sink):
    s = lax.dot_general(qm, kdup, NT, preferred_element_type=F32) + bias
    m = jnp.maximum(jnp.max(s, axis=-1, keepdims=True), sink)
    p = jnp.exp(s - m)
    esink = jnp.exp(sink - m)
    inv = 1.0 / (jnp.sum(p, axis=-1, keepdims=True) + esink)
    return p * inv, esink * inv


def _band_bias(bias_ref):
    rows = bias_ref.shape[1]
    bias_ref[0] = jnp.where(_band_valid(BLK, rows), 0.0, NEG)
    bias_ref[1] = jnp.where(_band_valid(0, rows), 0.0, NEG)


def _rowmat_call(c_all, w, b, name):
    n = w.shape[1]
    tn = 512

    def body(c_ref, w_ref, b_ref, o_ref, ca_ref):
        ca, _ = _silu_parts(c_ref[...])
        ca_ref[...] = ca
        o_ref[...] = jnp.dot(ca.astype(BF16), w_ref[...].astype(BF16), preferred_element_type=F32) + b_ref[...]

    return pl.pallas_call(
        body, name=name, grid=(n // tn,),
        in_specs=[pl.BlockSpec((N_DEV, D), lambda j: (0, 0)), pl.BlockSpec((D, tn), lambda j: (0, j)),
                  pl.BlockSpec((1, tn), lambda j: (0, j))],
        out_specs=[pl.BlockSpec((N_DEV, tn), lambda j: (0, j)), pl.BlockSpec((N_DEV, D), lambda j: (0, 0))],
        out_shape=[jax.ShapeDtypeStruct((N_DEV, n), F32), jax.ShapeDtypeStruct((N_DEV, D), F32)],
        compiler_params=_params("arbitrary"),
    )(c_all, w, b)


def _cast_into_call(pos, w, full_shape, name):
    r, n = w.shape
    tr = min(r, 512)
    by_cols = full_shape[0] == r
    nrb = r // tr

    def body(pos_ref, w_ref, o_ref):
        o_ref[...] = w_ref[...].astype(BF16)

    out_map = (lambda i, pos: (i, pos[0])) if by_cols else (lambda i, pos: (pos[0] * nrb + i, 0))
    return pl.pallas_call(
        body, name=name,
        grid_spec=pltpu.PrefetchScalarGridSpec(
            num_scalar_prefetch=1, grid=(nrb,),
            in_specs=[pl.BlockSpec((tr, n), lambda i, pos: (i, 0))], out_specs=pl.BlockSpec((tr, n), out_map)),
        out_shape=jax.ShapeDtypeStruct(full_shape, BF16), compiler_params=_params("parallel"),
    )(pos, w)


W_IN_PARTS = ((0, 768), (768, 640))
OUT_STREAMS = 4
X_STREAMS = 4


def _proj_gather_call(pos, x, shift, scale, norm_g, wi_full, wo_full):
    s = x.shape[0]
    tm = min(s, 512)
    nrow = s // tm
    hi = D // 2
    ho = W_OUT_SHARD // 2
    phases = [(0, None), (1, 0), (2, 0), (1, 1), (2, 1), (3, 0), (3, 1)]

    def body(pos_ref, *refs):
        x_refs = refs[:X_STREAMS]
        (sh_ref, sc_ref, g_ref, _, _, h_ref, proj_ref, fi_ref, fo_ref,
         h_all, wbuf, obuf, send_sems, recv_sems, load_sems, out_sems) = refs[X_STREAMS:]
        p = pl.program_id(0)
        i = pl.program_id(1)
        x_, y_, c_ = _coords()
        me, sibling = (x_, y_, c_), (x_, y_, 1 - c_)

        def shard_of(q):
            px, py, _ = _peer(x_, y_, c_, q, 0)
            return 2 * px + py

        def cols_of(q, cp):
            off, w = (0, W_IN_SHARD) if cp is None else W_IN_PARTS[cp]
            return shard_of(q) * W_IN_SHARD + off, w

        def part(which, q, pc, sub, cp):
            n = hi if which == 0 else ho
            base = pc * n
            if sub is not None:
                n //= 2
                base = base + sub * n
            if which == 0:
                c0, w = cols_of(q, cp)
                return fi_ref.at[pl.ds(base, n), pl.ds(c0, w)]
            return fo_ref.at[pl.ds(shard_of(q) * W_OUT_SHARD + base, n), :]

        def copy(k, ref, to):
            return pltpu.make_async_remote_copy(src_ref=ref, dst_ref=ref, send_sem=send_sems.at[k], recv_sem=recv_sems.at[k],
                                                device_id=to, device_id_type=MESH)

        def sem(which, kind, j, cp):
            return 4 * kind + 2 * cp + j if which == 0 else 16 + 2 * kind + j

        def to_neighbour(which, q, cp=None):
            return copy(sem(which, 0, q - 1, cp), part(which, 0, c_, None, cp), _peer(x_, y_, c_, q, 0))

        def from_neighbour(which, q, cp=None):
            return copy(sem(which, 0, q - 1, cp), part(which, q, c_, None, cp), me)

        def relay(which, q, cp=None):
            return copy(sem(which, 1, q - 1, cp), part(which, q, c_, q - 1, cp), _peer(x_, y_, c_, 3 - q, 0))

        def relayed(which, sub, cp=None):
            return copy(sem(which, 1, sub, cp), part(which, 3, c_, sub, cp), me)

        def to_sibling(which, q, cp=None):
            return copy(sem(which, 2, q - 1, cp), part(which, q, c_, None, cp), sibling)

        def from_sibling(which, q, cp=None):
            return copy(sem(which, 2, q - 1, cp), part(which, q, 1 - c_, None, cp), me)

        def relayed_to_sibling(which, sub, cp=None):
            return copy(sem(which, 3, sub, cp), part(which, 3, c_, sub, cp), sibling)

        def relayed_from_sibling(which, sub, cp=None):
            return copy(sem(which, 3, sub, cp), part(which, 3, 1 - c_, sub, cp), me)

        def pass_on_neighbours(which, cp=None):
            for q in (1, 2):
                from_neighbour(which, q, cp).wait_recv()
                to_sibling(which, q, cp).start()
                relay(which, q, cp).start()

        def pass_on_relayed(which, cp=None):
            for sub in range(2):
                relayed(which, sub, cp).wait_recv()
                relayed_to_sibling(which, sub, cp).start()

        def shard_load(k):
            c0, w = cols_of(*phases[k])
            return pltpu.make_async_copy(fi_ref.at[:, pl.ds(c0, w)], wbuf.at[k % 2, :, 0:w], load_sems.at[k % 2])

        class OutCopies:
            def __init__(self, k, slot, row0):
                c0, w = cols_of(*phases[k])
                strip = tm // OUT_STREAMS
                self.copies = [pltpu.make_async_copy(obuf.at[slot, n * strip:(n + 1) * strip, 0:w],
                                                     proj_ref.at[pl.ds(row0 + n * strip, strip), pl.ds(c0, w)],
                                                     out_sems.at[slot, n]) for n in range(OUT_STREAMS)]

            def start(self):
                for cp in self.copies:
                    cp.start()

            def wait(self):
                for cp in self.copies:
                    cp.wait()

        out_copy = OutCopies

        def drain(k):
            for j in range(min(2, nrow)):
                out_copy(k, (nrow - 1 - j) % 2, 0).wait()

        def arrivals(k):
            q, cp = phases[k]
            if k == 0:
                for cp_ in range(2):
                    for q_ in (1, 2):
                        to_neighbour(0, q_, cp_).start()
            elif q < 3 and k in (1, 3):
                pass_on_neighbours(0, cp)
                if k == 1:
                    for q_ in (1, 2):
                        to_neighbour(1, q_).start()
            elif k == 5:
                for cp_ in range(2):
                    pass_on_relayed(0, cp_)
                pass_on_neighbours(1)
            if q in (1, 2):
                from_sibling(0, q, cp).wait_recv()
            elif q == 3:
                for sub in range(2):
                    relayed_from_sibling(0, sub, cp).wait_recv()

        rows = pl.ds(pl.multiple_of(i * tm, tm), tm)
        slot = i % 2
        for k, (q, cp) in enumerate(phases):
            @pl.when(p == k)
            def _(k=k, q=q, cp=cp):
                @pl.when(i == 0)
                def _():
                    if k == 0:
                        arrivals(0)
                        shard_load(0).start()
                    else:
                        drain(k - 1)
                    shard_load(k).wait()

                if k + 1 < len(phases):
                    @pl.when(i == max(nrow - 2, 0))
                    def _():
                        arrivals(k + 1)
                        shard_load(k + 1).start()

                if k == 0:
                    wx = D // X_STREAMS
                    ssq = sum(jnp.sum(xr[...] * xr[...], axis=-1, keepdims=True) for xr in x_refs)
                    r = lax.rsqrt(ssq * (1.0 / D) + EPS)
                    for n, xr in enumerate(x_refs):
                        cols = slice(n * wx, (n + 1) * wx)
                        hv = ((xr[...] * r * g_ref[:, cols]) * (1.0 + sc_ref[:, cols]) + sh_ref[:, cols]).astype(BF16)
                        h_ref[:, cols] = hv
                        h_all[rows, cols] = hv

                @pl.when(i >= 2)
                def _():
                    out_copy(k, slot, 0).wait()

                w = cols_of(q, cp)[1]
                obuf[slot, :, 0:w] = jnp.dot(h_all[rows, :], wbuf[k % 2, :, 0:w], preferred_element_type=F32)
                out_copy(k, slot, pl.multiple_of(i * tm, tm)).start()

        @pl.when((p == len(phases) - 1) & (i == nrow - 1))
        def _():
            drain(len(phases) - 1)
            pass_on_relayed(1)
            for q in (1, 2):
                from_sibling(1, q).wait_recv()
            for sub in range(2):
                relayed_from_sibling(1, sub).wait_recv()
            for which, cps in ((0, (0, 1)), (1, (None,))):
                for cp in cps:
                    for q in (1, 2):
                        to_neighbour(which, q, cp).wait_send()
                        relay(which, q, cp).wait_send()
                        to_sibling(which, q, cp).wait_send()
                        relayed_to_sibling(which, q - 1, cp).wait_send()

    vec = pl.BlockSpec((1, D), lambda p, i, pos: (0, 0))
    first_phase_rows = lambda p, i, pos: (jnp.where(p == 0, i, nrow - 1), 0)
    anyspec = pl.BlockSpec(memory_space=pl.ANY)
    x_spec = lambda n: pl.BlockSpec((tm, D // X_STREAMS), lambda p, i, pos: (jnp.where(p == 0, i, nrow - 1), n))
    return pl.pallas_call(
        body, name="proj_gather",
        grid_spec=pltpu.PrefetchScalarGridSpec(
            num_scalar_prefetch=1, grid=(len(phases), nrow),
            in_specs=[x_spec(n) for n in range(X_STREAMS)] + [vec, vec, vec, anyspec, anyspec],
            out_specs=[pl.BlockSpec((tm, D), first_phase_rows), anyspec, anyspec, anyspec],
            scratch_shapes=[pltpu.VMEM((s, D), BF16), pltpu.VMEM((2, D, W_IN_SHARD), BF16), pltpu.VMEM((2, tm, W_IN_SHARD), F32),
                            pltpu.SemaphoreType.DMA((24,)), pltpu.SemaphoreType.DMA((24,)), pltpu.SemaphoreType.DMA((2,)),
                            pltpu.SemaphoreType.DMA((2, OUT_STREAMS))]),
        out_shape=[jax.ShapeDtypeStruct((s, D), BF16), jax.ShapeDtypeStruct((s, D_IN), F32),
                   jax.ShapeDtypeStruct((D, D_IN), BF16), jax.ShapeDtypeStruct((D, D), BF16)],
        input_output_aliases={X_STREAMS + 4: 2, X_STREAMS + 5: 3},
        compiler_params=_params("arbitrary", "arbitrary"),
    )(pos, *([x] * X_STREAMS), shift, scale, norm_g, wi_full, wo_full)


def _proj_specs(rev_nb=None):
    if rev_nb is None:
        row = lambda i: i
    else:
        row = lambda i: rev_nb - 1 - i
    wide = lambda col: pl.BlockSpec((BLK, D_A), lambda i: (row(i), col))
    kv = lambda col: pl.BlockSpec((BLK, D_KV), lambda i: (row(i), col))
    half = lambda col: pl.BlockSpec((BLK, 512), lambda i: (row(i), col))
    return [wide(0), wide(1), wide(2), wide(3), kv(OFF_K // D_KV), kv(OFF_V // D_KV), half(OFF_ZB // 512), half(OFF_ZB // 512 + 1)]


def _mix_fwd_call(proj, cos, sin, ln_g, ln_b, w_sp, b_sp_t, sinks):
    s = proj.shape[0]
    nb = s // BLK

    def body(ua_ref, va_ref, za_ref, q_ref, k_ref, v_ref, zb0_ref, zb1_ref, cos_ref, sin_ref, lg_ref, lb_ref,
             w_ref, bt_ref, sinks_ref, y_ref, kdup_ref, vdup_ref, qm_ref, ost_ref, bias_ref):
        i = pl.program_id(0)
        first_half, lo = _lane_masks()
        cos_t = cos_ref[...]
        sin_t = sin_ref[...]

        _, _, vln = _layer_norm_fwd(va_ref[...], lg_ref[...], lb_ref[...])
        tril = _tril()
        for g in range(GROUPS):
            cols = slice(g * BLK, (g + 1) * BLK)
            wg = jnp.where(tril, w_ref[g], 0.0).astype(BF16)
            sg = jnp.dot(wg, vln[:, cols].astype(BF16), preferred_element_type=F32) + bt_ref[:, g:g + 1]
            gate, _ = _silu_parts(za_ref[:, cols])
            y_ref[:, cols] = (ua_ref[:, cols] * sg * gate).astype(BF16)

        @pl.when(i == 0)
        def _():
            kdup_ref[:, 0:BLK, :] = jnp.zeros((N_KV, BLK, LANE), BF16)
            vdup_ref[:, 0:BLK, :] = jnp.zeros((N_KV, BLK, LANE), BF16)
            _band_bias(bias_ref)

        @pl.when(i > 0)
        def _():
            kdup_ref[:, 0:BLK, :] = kdup_ref[:, BLK:2 * BLK, :]
            vdup_ref[:, 0:BLK, :] = vdup_ref[:, BLK:2 * BLK, :]

        for ks in range(2):
            cols = slice(ks * LANE, (ks + 1) * LANE)
            kr = _rope(k_ref[:, cols], cos_t, sin_t, first_half)
            for n, (kd, vd) in enumerate(zip(_dup_kv(kr, lo), _dup_kv(v_ref[:, cols], lo))):
                kdup_ref[2 * ks + n, BLK:2 * BLK, :] = kd
                vdup_ref[2 * ks + n, BLK:2 * BLK, :] = vd
        for sb in range(8):
            _stack_heads(qm_ref, sb, _rope(q_ref[:, sb * LANE:(sb + 1) * LANE], cos_t, sin_t, first_half) * SCALE, lo, BF16)

        block_kind = jnp.where(i > 0, 1, 0)

        def kv_head(kh, carry):
            probs, _ = _softmax_sink(qm_ref[kh], kdup_ref[kh], bias_ref[block_kind], _sink_column(sinks_ref, kh))
            ost_ref[kh] = jnp.dot(probs.astype(BF16), vdup_ref[kh], preferred_element_type=F32)
            return carry

        lax.fori_loop(0, N_KV, kv_head, 0, unroll=2)
        for sb in range(8):
            cols = slice(sb * LANE, (sb + 1) * LANE)
            zb = zb0_ref[:, cols] if sb < 4 else zb1_ref[:, (sb - 4) * LANE:(sb - 3) * LANE]
            gate, _ = _silu_parts(zb)
            y_ref[:, D_A + sb * LANE:D_A + (sb + 1) * LANE] = (_unstack_heads(ost_ref, sb, lo) * gate).astype(BF16)

    tab = pl.BlockSpec((BLK, LANE), lambda i: (i, 0))
    return pl.pallas_call(
        body, name="mix_fwd", grid=(nb,),
        in_specs=_proj_specs() + [
            tab, tab, pl.BlockSpec((1, D_A), lambda i: (0, 0)), pl.BlockSpec((1, D_A), lambda i: (0, 0)),
            pl.BlockSpec((GROUPS, BLK, BLK), lambda i: (0, 0, 0)), pl.BlockSpec((BLK, GROUPS), lambda i: (0, 0)),
            pl.BlockSpec(memory_space=pltpu.SMEM)],
        out_specs=pl.BlockSpec((BLK, 2 * D_A), lambda i: (i, 0)),
        out_shape=jax.ShapeDtypeStruct((s, 2 * D_A), BF16),
        scratch_shapes=[pltpu.VMEM((N_KV, 2 * BLK, LANE), BF16), pltpu.VMEM((N_KV, 2 * BLK, LANE), BF16),
                        pltpu.VMEM((N_KV, Q_PER_KV * BLK, LANE), BF16), pltpu.VMEM((N_KV, Q_PER_KV * BLK, LANE), F32),
                        pltpu.VMEM((2, Q_PER_KV * BLK, 2 * BLK), F32)],
        compiler_params=_params("arbitrary"),
    )(proj, proj, proj, proj, proj, proj, proj, proj, cos, sin, ln_g, ln_b, w_sp, b_sp_t, sinks)


def _tail_call(y, w_out_bf, x, target, gate, shift_f, scale_f, gf):
    s = x.shape[0]
    tm = min(s, 256)
    nsteps = s // tm

    def body(y_ref, w_ref, x_ref, t_ref, gate_ref, shf_ref, scf_ref, gf_ref, dx2_ref, do_ref, dy_ref, st_ref):
        i = pl.program_id(0)

        @pl.when(i == 0)
        def _():
            st_ref[...] = jnp.zeros((8, D), F32)

        o = jnp.dot(y_ref[...], w_ref[...], preferred_element_type=F32)
        gate_v = gate_ref[...]
        x2 = x_ref[...] + gate_v * o
        r2 = lax.rsqrt(jnp.mean(x2 * x2, axis=-1, keepdims=True) + EPS)
        xn2 = x2 * r2
        hn2 = xn2 * gf_ref[...]
        one_sc = 1.0 + scf_ref[...]
        err = hn2 * one_sc + shf_ref[...] - t_ref[...]
        dout = err * (1.0 / D)
        dhn2 = dout * one_sc
        dxn2 = dhn2 * gf_ref[...]
        dx2 = r2 * (dxn2 - xn2 * jnp.mean(dxn2 * xn2, axis=-1, keepdims=True))
        dx2_ref[...] = dx2
        do = (dx2 * gate_v).astype(BF16)
        do_ref[...] = do
        dy_ref[...] = lax.dot_general(do, w_ref[...], NT, preferred_element_type=F32)
        st_ref[0:1, :] += jnp.sum(dout, axis=0, keepdims=True)
        st_ref[1:2, :] += jnp.sum(dout * hn2, axis=0, keepdims=True)
        st_ref[2:3, :] += jnp.sum(dhn2 * xn2, axis=0, keepdims=True)
        st_ref[3:4, :] += jnp.sum(dx2 * o, axis=0, keepdims=True)
        st_ref[4:5, :] += jnp.sum(err * err, axis=0, keepdims=True)

        @pl.when(i == nsteps - 1)
        def _():
            st_ref[5:6, :] = jnp.full((1, D), 0.5 / D, F32) * jnp.sum(st_ref[4:5, :])

    vec = pl.BlockSpec((1, D), lambda i: (0, 0))
    rows = lambda: pl.BlockSpec((tm, D), lambda i: (i, 0))
    return pl.pallas_call(
        body, name="tail", grid=(nsteps,),
        in_specs=[rows(), pl.BlockSpec((D, D), lambda i: (0, 0)), rows(), rows(), vec, vec, vec, vec],
        out_specs=[rows(), rows(), rows(), pl.BlockSpec((8, D), lambda i: (0, 0))],
        out_shape=[jax.ShapeDtypeStruct((s, D), F32), jax.ShapeDtypeStruct((s, D), BF16), jax.ShapeDtypeStruct((s, D), F32),
                   jax.ShapeDtypeStruct((8, D), F32)],
        compiler_params=_params("arbitrary"),
    )(y, w_out_bf, x, target, gate, shift_f, scale_f, gf)


def _tn_call(a, b, name):
    s, m = a.shape
    n = b.shape[1]
    tn = 512
    ts = min(s, 1024)
    nk = s // ts

    def body(a_ref, b_ref, o_ref, acc_ref):
        k = pl.program_id(1)

        @pl.when(k == 0)
        def _():
            acc_ref[...] = jnp.zeros((m, tn), F32)

        acc_ref[...] += lax.dot_general(a_ref[...], b_ref[...], TN, preferred_element_type=F32)

        @pl.when(k == nk - 1)
        def _():
            o_ref[...] = acc_ref[...].astype(BF16)

    return pl.pallas_call(
        body, name=name, grid=(n // tn, nk),
        in_specs=[pl.BlockSpec((ts, m), lambda j, k: (k, 0)), pl.BlockSpec((ts, tn), lambda j, k: (k, j))],
        out_specs=pl.BlockSpec((m, tn), lambda j, k: (0, j)),
        out_shape=jax.ShapeDtypeStruct((m, n), BF16),
        scratch_shapes=[pltpu.VMEM((m, tn), F32)],
        compiler_params=_params("parallel", "arbitrary"),
    )(a, b)


def _tn_shards_call(pos, a, b, qs, name):
    s, m = a.shape
    ts = min(s, 1024)
    nk = s // ts

    def body(pos_ref, a_ref, b_ref, o_ref, acc_ref):
        k = pl.program_id(1)

        @pl.when(k == 0)
        def _():
            acc_ref[...] = jnp.zeros((m, W_IN_SHARD), F32)

        acc_ref[...] += lax.dot_general(a_ref[...], b_ref[...], TN, preferred_element_type=F32)

        @pl.when(k == nk - 1)
        def _():
            o_ref[...] = acc_ref[...].astype(BF16)

    def shard(j, pos):
        q = qs[0]
        for n in range(1, len(qs)):
            q = jnp.where(j == n, qs[n], q)
        return jnp.bitwise_xor(pos[0], q)

    return pl.pallas_call(
        body, name=name,
        grid_spec=pltpu.PrefetchScalarGridSpec(
            num_scalar_prefetch=1, grid=(len(qs), nk),
            in_specs=[pl.BlockSpec((ts, m), lambda j, k, pos: (k, 0)),
                      pl.BlockSpec((ts, W_IN_SHARD), lambda j, k, pos: (k, shard(j, pos)))],
            out_specs=pl.BlockSpec((m, W_IN_SHARD), lambda j, k, pos: (0, j)),
            scratch_shapes=[pltpu.VMEM((m, W_IN_SHARD), F32)]),
        out_shape=jax.ShapeDtypeStruct((m, len(qs) * W_IN_SHARD), BF16),
        compiler_params=_params("parallel", "arbitrary"),
    )(pos, a, b)


def _mix_bwd_call(proj, dy, cos, sin, ln_g, ln_b, w_sp, w_sp_t, b_sp_t, sinks):
    s = proj.shape[0]
    nb = s // BLK
    rev = lambda i: nb - 1 - i
    prev = lambda i: jnp.maximum(nb - 2 - i, 0)

    def body(ua_ref, va_ref, za_ref, q_ref, k_ref, v_ref, zb0_ref, zb1_ref, kp_ref, vp_ref, dy_ref,
             cos_ref, sin_ref, cosp_ref, sinp_ref, lg_ref, lb_ref, w_ref, wt_ref, bt_ref, sinks_ref,
             dp_ref, lnst_ref, dw_ref, dbt_ref, dsink_ref,
             kdup_ref, vdup_ref, dvln_ref, qm_ref, dom_ref, ost_ref, dqst_ref, dkdup_ref, dvdup_ref, kcar_ref, vcar_ref,
             sigb_ref, bias_ref):
        i = pl.program_id(0)
        first_half, lo = _lane_masks()
        lane8 = lax.broadcasted_iota(jnp.int32, (8, LANE), 1)
        cos_t = cos_ref[...]
        sin_t = sin_ref[...]

        @pl.when(i == 0)
        def _():
            lnst_ref[...] = jnp.zeros((8, D_A), F32)
            dw_ref[...] = jnp.zeros((GROUPS, BLK, BLK), F32)
            dbt_ref[...] = jnp.zeros((BLK, LANE), F32)
            dsink_ref[...] = jnp.zeros((8, LANE), F32)
            kcar_ref[...] = jnp.zeros((BLK, D_KV), F32)
            vcar_ref[...] = jnp.zeros((BLK, D_KV), F32)
            _band_bias(bias_ref)

        vhat, rstd, vln = _layer_norm_fwd(va_ref[...], lg_ref[...], lb_ref[...])
        tril = _tril()
        triu = jnp.logical_not(tril) | (lax.broadcasted_iota(jnp.int32, (BLK, BLK), 0) == lax.broadcasted_iota(jnp.int32, (BLK, BLK), 1))
        lane_b = lax.broadcasted_iota(jnp.int32, (BLK, LANE), 1)
        db_acc = jnp.zeros((BLK, LANE), F32)
        for g in range(GROUPS):
            cols = slice(g * BLK, (g + 1) * BLK)
            vln_g = vln[:, cols].astype(BF16)
            wg = jnp.where(tril, w_ref[g], 0.0).astype(BF16)
            sg = jnp.dot(wg, vln_g, preferred_element_type=F32) + bt_ref[:, g:g + 1]
            za = za_ref[:, cols]
            gate, sig = _silu_parts(za)
            ua = ua_ref[:, cols]
            dya_g = dy_ref[:, cols]
            dya = dya_g * gate
            dp_ref[:, cols] = (dya * sg).astype(BF16)
            dp_ref[:, 2 * D_A + g * BLK:2 * D_A + (g + 1) * BLK] = (
                dya_g * (ua * sg) * (sig * (1.0 + za * (1.0 - sig)))).astype(BF16)
            ds = dya * ua
            ds_b = ds.astype(BF16)
            wtg = jnp.where(triu, wt_ref[g], 0.0).astype(BF16)
            dvln_ref[:, cols] = jnp.dot(wtg, ds_b, preferred_element_type=F32)
            dw_ref[g] += jnp.where(tril, lax.dot_general(ds_b, vln_g, NT, preferred_element_type=F32), 0.0)
            db_acc = db_acc + jnp.where(lane_b == g, jnp.sum(ds, axis=-1, keepdims=True), 0.0)
        dbt_ref[...] += db_acc
        dvln = dvln_ref[...]
        lnst_ref[0:1, :] += jnp.sum(dvln * vhat, axis=0, keepdims=True)
        lnst_ref[1:2, :] += jnp.sum(dvln, axis=0, keepdims=True)
        dvhat = dvln * lg_ref[...]
        m1 = jnp.mean(dvhat, axis=-1, keepdims=True)
        m2 = jnp.mean(dvhat * vhat, axis=-1, keepdims=True)
        dp_ref[:, D_A:2 * D_A] = (rstd * (dvhat - m1 - vhat * m2)).astype(BF16)

        cosp = cosp_ref[...]
        sinp = sinp_ref[...]
        for ks in range(2):
            cols = slice(ks * LANE, (ks + 1) * LANE)
            kr = _rope(k_ref[:, cols], cos_t, sin_t, first_half)
            kpr = _rope(kp_ref[:, cols], cosp, sinp, first_half)
            for n, (kc, vc, kp, vp) in enumerate(zip(_dup_kv(kr, lo), _dup_kv(v_ref[:, cols], lo),
                                                     _dup_kv(kpr, lo), _dup_kv(vp_ref[:, cols], lo))):
                kdup_ref[2 * ks + n, BLK:2 * BLK, :] = kc
                vdup_ref[2 * ks + n, BLK:2 * BLK, :] = vc
                kdup_ref[2 * ks + n, 0:BLK, :] = kp
                vdup_ref[2 * ks + n, 0:BLK, :] = vp
        for sb in range(8):
            cols = slice(sb * LANE, (sb + 1) * LANE)
            _stack_heads(qm_ref, sb, _rope(q_ref[:, cols], cos_t, sin_t, first_half) * SCALE, lo, BF16)
            zb = zb0_ref[:, cols] if sb < 4 else zb1_ref[:, (sb - 4) * LANE:(sb - 3) * LANE]
            gate, sig = _silu_parts(zb)
            sigb_ref[:, cols] = sig
            _stack_heads(dom_ref, sb, dy_ref[:, D_A + sb * LANE:D_A + (sb + 1) * LANE] * gate, lo, F32)

        block_kind = jnp.where(i < nb - 1, 1, 0)

        def kv_head(kh, dsink_acc):
            qm = qm_ref[kh]
            kd = kdup_ref[kh]
            vd = vdup_ref[kh]
            probs, psink = _softmax_sink(qm, kd, bias_ref[block_kind], _sink_column(sinks_ref, kh))
            probs_b = probs.astype(BF16)
            o = jnp.dot(probs_b, vd, preferred_element_type=F32)
            ost_ref[kh] = o
            dom = dom_ref[kh]
            dom_b = dom.astype(BF16)
            delta = jnp.sum(dom * o, axis=-1, keepdims=True)
            dpr = lax.dot_general(dom_b, vd, NT, preferred_element_type=F32)
            dss = (probs * (dpr - delta)).astype(BF16)
            sd = psink * delta
            for n in range(Q_PER_KV):
                dsink_acc = dsink_acc + jnp.where(lane8 == Q_PER_KV * kh + n, -jnp.sum(sd[n * BLK:(n + 1) * BLK]), 0.0)
            dqst_ref[kh] = jnp.dot(dss, kd, preferred_element_type=F32)
            dkdup_ref[kh] = lax.dot_general(dss, qm, TN, preferred_element_type=F32)
            dvdup_ref[kh] = lax.dot_general(probs_b, dom_b, TN, preferred_element_type=F32)
            return dsink_acc

        dsink_acc = lax.fori_loop(0, N_KV // 2, lambda j, acc: kv_head(2 * j + 1, kv_head(2 * j, acc)), jnp.zeros((8, LANE), F32))
        row0 = lax.broadcasted_iota(jnp.int32, (8, LANE), 0) == 0
        dsink_ref[...] += jnp.where(row0, dsink_acc, 0.0)

        for sb in range(8):
            cols = slice(sb * LANE, (sb + 1) * LANE)
            zb = zb0_ref[:, cols] if sb < 4 else zb1_ref[:, (sb - 4) * LANE:(sb - 3) * LANE]
            sig = sigb_ref[:, cols]
            dyb = dy_ref[:, D_A + sb * LANE:D_A + (sb + 1) * LANE]
            dp_ref[:, OFF_ZB + sb * LANE:OFF_ZB + (sb + 1) * LANE] = (
                dyb * _unstack_heads(ost_ref, sb, lo) * (sig * (1.0 + zb * (1.0 - sig)))).astype(BF16)
            dq_r = _unstack_heads(dqst_ref, sb, lo) * SCALE
            dp_ref[:, OFF_Q + sb * LANE:OFF_Q + (sb + 1) * LANE] = _unrope(dq_r, cos_t, sin_t, first_half).astype(BF16)

        lo2 = lax.broadcasted_iota(jnp.int32, (2 * BLK, LANE), 1) < HEAD
        for ks in range(2):
            cols = slice(ks * LANE, (ks + 1) * LANE)
            ka = dkdup_ref[2 * ks]
            kb = dkdup_ref[2 * ks + 1]
            dk_band = jnp.where(lo2, ka + pltpu.roll(ka, HEAD, 1), kb + pltpu.roll(kb, HEAD, 1))
            va_ = dvdup_ref[2 * ks]
            vb_ = dvdup_ref[2 * ks + 1]
            dv_band = jnp.where(lo2, va_ + pltpu.roll(va_, HEAD, 1), vb_ + pltpu.roll(vb_, HEAD, 1))
            dkr = dk_band[BLK:2 * BLK, :] + kcar_ref[:, cols]
            dp_ref[:, OFF_K + ks * LANE:OFF_K + (ks + 1) * LANE] = _unrope(dkr, cos_t, sin_t, first_half).astype(BF16)
            dp_ref[:, OFF_V + ks * LANE:OFF_V + (ks + 1) * LANE] = (
                dv_band[BLK:2 * BLK, :] + vcar_ref[:, cols]).astype(BF16)
            kcar_ref[:, cols] = dk_band[0:BLK, :]
            vcar_ref[:, cols] = dv_band[0:BLK, :]

    tab = pl.BlockSpec((BLK, LANE), lambda i: (rev(i), 0))
    tabp = pl.BlockSpec((BLK, LANE), lambda i: (prev(i), 0))
    kvp = lambda col: pl.BlockSpec((BLK, D_KV), lambda i: (prev(i), col))
    vec = pl.BlockSpec((1, D_A), lambda i: (0, 0))
    w3 = pl.BlockSpec((GROUPS, BLK, BLK), lambda i: (0, 0, 0))
    return pl.pallas_call(
        body, name="mix_bwd", grid=(nb,),
        in_specs=_proj_specs(nb) + [
            kvp(OFF_K // D_KV), kvp(OFF_V // D_KV), pl.BlockSpec((BLK, 2 * D_A), lambda i: (rev(i), 0)),
            tab, tab, tabp, tabp, vec, vec, w3, w3, pl.BlockSpec((BLK, GROUPS), lambda i: (0, 0)),
            pl.BlockSpec(memory_space=pltpu.SMEM)],
        out_specs=[pl.BlockSpec((BLK, D_IN), lambda i: (rev(i), 0)), pl.BlockSpec((8, D_A), lambda i: (0, 0)), w3,
                   pl.BlockSpec((BLK, LANE), lambda i: (0, 0)), pl.BlockSpec((8, LANE), lambda i: (0, 0))],
        out_shape=[jax.ShapeDtypeStruct((s, D_IN), BF16), jax.ShapeDtypeStruct((8, D_A), F32),
                   jax.ShapeDtypeStruct((GROUPS, BLK, BLK), F32), jax.ShapeDtypeStruct((BLK, LANE), F32),
                   jax.ShapeDtypeStruct((8, LANE), F32)],
        scratch_shapes=[pltpu.VMEM((N_KV, 2 * BLK, LANE), BF16), pltpu.VMEM((N_KV, 2 * BLK, LANE), BF16),
                        pltpu.VMEM((BLK, D_A), F32), pltpu.VMEM((N_KV, Q_PER_KV * BLK, LANE), BF16),
                        pltpu.VMEM((N_KV, Q_PER_KV * BLK, LANE), F32), pltpu.VMEM((N_KV, Q_PER_KV * BLK, LANE), F32),
                        pltpu.VMEM((N_KV, Q_PER_KV * BLK, LANE), F32), pltpu.VMEM((N_KV, 2 * BLK, LANE), F32),
                        pltpu.VMEM((N_KV, 2 * BLK, LANE), F32), pltpu.VMEM((BLK, D_KV), F32), pltpu.VMEM((BLK, D_KV), F32),
                        pltpu.VMEM((BLK, D_B), F32), pltpu.VMEM((2, Q_PER_KV * BLK, 2 * BLK), F32)],
        compiler_params=_params("arbitrary"),
    )(proj, proj, proj, proj, proj, proj, proj, proj, proj, proj, dy, cos, sin, cos, sin, ln_g, ln_b, w_sp, w_sp_t,
      b_sp_t, sinks)


def _dh_call(dproj, w_bf, x, dx2, scale, norm_g):
    s = x.shape[0]
    tm = min(s, 512)
    tk = W_IN_SHARD
    nk = D_IN // tk

    def body(dp_ref, w_ref, x_ref, dx2_ref, sc_ref, g_ref, gx_ref, st_ref, acc_ref):
        i = pl.program_id(0)
        k = pl.program_id(1)

        @pl.when((i == 0) & (k == 0))
        def _():
            st_ref[...] = jnp.zeros((8, D), F32)

        @pl.when(k == 0)
        def _():
            acc_ref[...] = jnp.zeros((tm, D), F32)

        acc_ref[...] += lax.dot_general(dp_ref[...], w_ref[...], NT, preferred_element_type=F32)

        @pl.when(k == nk - 1)
        def _():
            g = g_ref[...]
            one_sc = 1.0 + sc_ref[...]

            def chunk(n, carry):
                rows = pl.ds(pl.multiple_of(n * BLK, BLK), BLK)
                dh = acc_ref[rows, :]
                xv = x_ref[rows, :]
                r = lax.rsqrt(jnp.mean(xv * xv, axis=-1, keepdims=True) + EPS)
                xn = xv * r
                dhn = dh * one_sc
                dxn = dhn * g
                gx_ref[rows, :] = dx2_ref[rows, :] + r * (dxn - xn * jnp.mean(dxn * xn, axis=-1, keepdims=True))
                st_ref[0:1, :] += jnp.sum(dh, axis=0, keepdims=True)
                st_ref[1:2, :] += jnp.sum(dh * (xn * g), axis=0, keepdims=True)
                st_ref[2:3, :] += jnp.sum(dhn * xn, axis=0, keepdims=True)
                return carry

            lax.fori_loop(0, tm // BLK, chunk, 0)

    vec = pl.BlockSpec((1, D), lambda i, k: (0, 0))
    rows = lambda: pl.BlockSpec((tm, D), lambda i, k: (i, 0))
    return pl.pallas_call(
        body, name="dh", grid=(s // tm, nk),
        in_specs=[pl.BlockSpec((tm, tk), lambda i, k: (i, k)), pl.BlockSpec((D, tk), lambda i, k: (0, k)), rows(), rows(), vec, vec],
        out_specs=[rows(), pl.BlockSpec((8, D), lambda i, k: (0, 0))],
        out_shape=[jax.ShapeDtypeStruct((s, D), F32), jax.ShapeDtypeStruct((8, D), F32)],
        scratch_shapes=[pltpu.VMEM((tm, D), F32)],
        compiler_params=_params("arbitrary", "arbitrary"),
    )(dproj, w_bf, x, dx2, scale, norm_g)


def _adam_math(w, g, m, v):
    m_new = ADAM_B1 * m + (1.0 - ADAM_B1) * g
    v_new = ADAM_B2 * v + (1.0 - ADAM_B2) * (g * g)
    m_hat = m_new / ADAM_C1
    v_hat = v_new / ADAM_C2
    delta = -ADAM_LR * (m_hat / (jnp.sqrt(v_hat) + ADAM_EPS) + ADAM_WD * w)
    return delta, m_new, v_new


def _adam_small_call(tensors):
    n = len(tensors)

    def body(*refs):
        ins, outs = refs[:4 * n], refs[4 * n:]
        for t in range(n):
            w_ref, g_ref, m_ref, v_ref = ins[4 * t:4 * t + 4]
            d, mo, vo = _adam_math(w_ref[...], g_ref[...], m_ref[...], v_ref[...])
            outs[3 * t][...], outs[3 * t + 1][...], outs[3 * t + 2][...] = d, mo, vo

    vm = pl.BlockSpec(memory_space=pltpu.VMEM)
    flat = [a for t in tensors for a in t]
    out = pl.pallas_call(
        body, name="adam_small", in_specs=[vm] * (4 * n), out_specs=[vm] * (3 * n),
        out_shape=[jax.ShapeDtypeStruct(t[0].shape, F32) for t in tensors for _ in range(3)],
        compiler_params=pltpu.CompilerParams(vmem_limit_bytes=VMEM_LIMIT),
    )(*flat)
    return [tuple(out[3 * t:3 * t + 3]) for t in range(n)]


def _adam_halves_call(pos, w, mine, theirs, m, v, name):
    r, n = w.shape
    half = r // 2
    tr = ADAM_ROWS
    nh = half // tr

    def body(pos_ref, w_ref, mine_ref, theirs_ref, m_ref, v_ref, g_ref, d_ref, mo_ref, vo_ref):
        is_mine = (pl.program_id(0) // nh) == pos_ref[1]
        g = jnp.where(is_mine, mine_ref[...], theirs_ref[...])
        g_ref[...] = g
        d_ref[...], mo_ref[...], vo_ref[...] = _adam_math(w_ref[...], g, m_ref[...], v_ref[...])

    spec = lambda: pl.BlockSpec((tr, n), lambda i, pos: (i, 0))
    hspec = lambda: pl.BlockSpec((tr, n), lambda i, pos: (i % nh, 0))
    return pl.pallas_call(
        body, name=name,
        grid_spec=pltpu.PrefetchScalarGridSpec(
            num_scalar_prefetch=1, grid=(r // tr,), in_specs=[spec(), hspec(), hspec(), spec(), spec()],
            out_specs=[spec() for _ in range(4)]),
        out_shape=[jax.ShapeDtypeStruct((r, n), F32)] * 4, compiler_params=_params("parallel"),
    )(pos, w, mine, theirs, m, v)


def _adam_outer_call(w, ct, dm, m, v, name):
    r, n = w.shape
    tr = ADAM_ROWS

    def body(w_ref, ct_ref, dm_ref, m_ref, v_ref, g_ref, d_ref, mo_ref, vo_ref):
        g = ct_ref[:, 0:1] * dm_ref[0:1, :]
        for b in range(1, N_DEV):
            g = g + ct_ref[:, b:b + 1] * dm_ref[b:b + 1, :]
        g_ref[...] = g
        d_ref[...], mo_ref[...], vo_ref[...] = _adam_math(w_ref[...], g, m_ref[...], v_ref[...])

    spec = lambda: pl.BlockSpec((tr, n), lambda i: (i, 0))
    return pl.pallas_call(
        body, name=name, grid=(r // tr,),
        in_specs=[spec(), pl.BlockSpec((tr, N_DEV), lambda i: (i, 0)), pl.BlockSpec((N_DEV, n), lambda i: (0, 0)), spec(), spec()],
        out_specs=[spec() for _ in range(4)],
        out_shape=[jax.ShapeDtypeStruct((r, n), F32)] * 4, compiler_params=_params("parallel"),
    )(w, ct, dm, m, v)


def _sum_pieces_call(pos, part, part_block, recvs, name):
    r, n = recvs[0].shape[1:]
    tr = min(r, 256)
    nrb = r // tr

    def body(pos_ref, p_ref, *refs):
        acc = p_ref[...].astype(F32)
        for r_ref in refs[:-1]:
            for d in range(r_ref.shape[0]):
                acc = acc + r_ref[d].astype(F32)
        refs[-1][...] = acc

    return pl.pallas_call(
        body, name=name,
        grid_spec=pltpu.PrefetchScalarGridSpec(
            num_scalar_prefetch=1, grid=(nrb,),
            in_specs=[pl.BlockSpec((tr, n), lambda i, pos: part_block(i, pos, nrb))] + [
                pl.BlockSpec((rv.shape[0], tr, n), lambda i, pos: (0, i, 0)) for rv in recvs],
            out_specs=pl.BlockSpec((tr, n), lambda i, pos: (i, 0))),
        out_shape=jax.ShapeDtypeStruct((r, n), F32), compiler_params=_params("parallel"),
    )(pos, part, *recvs)


def _coords():
    return lax.axis_index("x"), lax.axis_index("y"), lax.axis_index("c")


def _allgather_sum_call(blk, name, with_sum):
    m_per, n = blk.shape

    def body(x_ref, out_ref, *rest):
        if with_sum:
            sum_ref, send_sems, recv_sems, local_sem = rest
        else:
            send_sems, recv_sems, local_sem = rest
        x, y, c = _coords()
        me, sibling = (x, y, c), (x, y, 1 - c)
        chips = [(1 - x, y), (x, 1 - y), (1 - x, 1 - y)]

        def rows(px, py, pc):
            return out_ref.at[pl.ds((4 * px + 2 * py + pc) * m_per, m_per), :]

        def copy(k, block, to, src=None):
            return pltpu.make_async_remote_copy(
                src_ref=rows(*block) if src is None else src, dst_ref=rows(*block),
                send_sem=send_sems.at[k], recv_sem=recv_sems.at[k], device_id=to, device_id_type=MESH)

        mine = pltpu.make_async_copy(x_ref, rows(*me), local_sem)
        mine.start()
        first = [copy(0, me, sibling, src=x_ref)]
        first += [copy(1 + j, me, (*chip, c), src=x_ref) for j, chip in enumerate(chips)]
        for cp in first:
            cp.start()
        passed = [copy(4 + j, (*chip, c), sibling) for j, chip in enumerate(chips)]
        for j, chip in enumerate(chips):
            copy(1 + j, (*chip, c), me).wait_recv()
            passed[j].start()
        copy(0, sibling, me).wait_recv()
        for j, chip in enumerate(chips):
            copy(4 + j, (*chip, 1 - c), me).wait_recv()
        for cp in first + passed:
            cp.wait_send()
        mine.wait()
        if with_sum:
            acc = out_ref[0:m_per, :]
            for d in range(1, N_DEV):
                acc = acc + out_ref[d * m_per:(d + 1) * m_per, :]
            sum_ref[...] = acc

    vm = pl.BlockSpec(memory_space=pltpu.VMEM)
    out_shape = [jax.ShapeDtypeStruct((N_DEV * m_per, n), F32)]
    if with_sum:
        out_shape.append(jax.ShapeDtypeStruct((m_per, n), F32))
    return pl.pallas_call(
        body, name=name, out_shape=out_shape, in_specs=[vm], out_specs=[vm] * len(out_shape),
        scratch_shapes=[pltpu.SemaphoreType.DMA((7,)), pltpu.SemaphoreType.DMA((7,)), pltpu.SemaphoreType.DMA],
        compiler_params=pltpu.CompilerParams(vmem_limit_bytes=VMEM_LIMIT),
    )(blk)


HBM_SPEC = pl.BlockSpec(memory_space=pltpu.HBM)
SEM_SPEC = pl.BlockSpec(memory_space=pltpu.SEMAPHORE)
SIDE_EFFECT = pltpu.SideEffectType.DATAFLOW_SIDE_EFFECTING


def _peer(x, y, c, q, cb):
    return (1 - x if q & 2 else x, 1 - y if q & 1 else y, 1 - c if cb else c)


def _w_in_piece(slots):
    def piece(part_ref, k, to):
        return part_ref.at[pl.ds(to[2] * (D // 2), D // 2), pl.ds(slots[k] * W_IN_SHARD, W_IN_SHARD)]
    return piece


def _w_out_piece(part_ref, k, to):
    ho = W_OUT_SHARD // 2
    return part_ref.at[pl.ds((2 * to[0] + to[1]) * W_OUT_SHARD + to[2] * ho, ho), :]


def _group_piece(part_ref, k, to):
    return part_ref.at[4 * to[0] + 2 * to[1] + to[2]]


def _whole_piece(part_ref, k, to):
    return part_ref


def _exchange_start_call(part, rels, piece, slot_shape, name):
    n = len(rels)
    land = lax.empty((n,) + slot_shape, part.dtype)

    def body(part_ref, land_ref, send_sems, recv_sems, part_thru, land_thru, token):
        x, y, c = _coords()
        for k, (q, cb) in enumerate(rels):
            to = _peer(x, y, c, q, cb)
            pltpu.make_async_remote_copy(src_ref=piece(part_ref, k, to), dst_ref=land_ref.at[k], send_sem=send_sems.at[k],
                                         recv_sem=recv_sems.at[k], device_id=to, device_id_type=MESH).start()
        token[...] = jnp.zeros_like(token)

    return pl.pallas_call(
        body, name=name,
        out_shape=(pltpu.SemaphoreType.DMA((n,)), pltpu.SemaphoreType.DMA((n,)), pltpu.HBM(part.shape, part.dtype),
                   pltpu.HBM(land.shape, land.dtype), jax.ShapeDtypeStruct((1, 1), F32)),
        in_specs=(HBM_SPEC, HBM_SPEC), out_specs=(SEM_SPEC, SEM_SPEC, HBM_SPEC, HBM_SPEC, pl.BlockSpec(memory_space=pltpu.VMEM)),
        input_output_aliases={0: 2, 1: 3},
        compiler_params=pltpu.CompilerParams(has_side_effects=SIDE_EFFECT),
    )(pltpu.with_memory_space_constraint(part, pltpu.HBM), pltpu.with_memory_space_constraint(land, pltpu.HBM))


def _exchange_wait_call(started, rels, piece, after, name):
    send_sems, recv_sems, part_thru, land_thru, _ = started

    def body(part_ref, land_ref, send_sems, recv_sems, after_ref, part_out, land_out):
        x, y, c = _coords()
        for k, (q, cb) in enumerate(rels):
            to = _peer(x, y, c, q, cb)
            cp = pltpu.make_async_remote_copy(src_ref=piece(part_ref, k, to), dst_ref=land_ref.at[k], send_sem=send_sems.at[k],
                                              recv_sem=recv_sems.at[k], device_id=to, device_id_type=MESH)
            cp.wait_send()
            cp.wait_recv()

    return pl.pallas_call(
        body, name=name,
        out_shape=(pltpu.HBM(part_thru.shape, part_thru.dtype), pltpu.HBM(land_thru.shape, land_thru.dtype)),
        in_specs=(HBM_SPEC, HBM_SPEC, SEM_SPEC, SEM_SPEC, pl.BlockSpec(memory_space=pl.ANY)), out_specs=(HBM_SPEC, HBM_SPEC),
        input_output_aliases={0: 0, 1: 1},
        compiler_params=pltpu.CompilerParams(has_side_effects=SIDE_EFFECT),
    )(part_thru, land_thru, send_sems, recv_sems, after)


def _rope_tables(s):
    inv_freq = 10000.0 ** (-jnp.arange(0, HEAD, 2, dtype=F32) / HEAD)
    ang = jnp.arange(s, dtype=F32)[:, None] * inv_freq[None, :]
    cos = jnp.tile(jnp.cos(ang), (1, LANE // (HEAD // 2)))
    sin = jnp.tile(jnp.sin(ang), (1, LANE // (HEAD // 2)))
    first_half = (jnp.arange(LANE) % HEAD) < (HEAD // 2)
    return cos, jnp.where(first_half[None, :], -sin, sin)


def kernel(x, c, w_ada, b_ada, norm_g, w_in, ln_v_g, ln_v_b, w_spatial, b_spatial, sinks, w_out, w_ada_final, b_ada_final, final_norm_g, loss_target, m_w_ada, m_b_ada, m_norm_g, m_w_in, m_ln_v_g, m_ln_v_b, m_w_spatial, m_b_spatial, m_sinks, m_w_out, m_w_ada_final, m_b_ada_final, m_final_norm_g, v_w_ada, v_b_ada, v_norm_g, v_w_in, v_ln_v_g, v_ln_v_b, v_w_spatial, v_b_spatial, v_sinks, v_w_out, v_w_ada_final, v_b_ada_final, v_final_norm_g):
    s = x.shape[1]
    ax, ay, ac = _coords()
    chip = 2 * ax + ay
    me = 4 * ax + 2 * ay + ac
    n_ada = w_ada.shape[2]
    n_adaf = w_ada_final.shape[1]

    x2d = x.reshape(s, D)
    tgt = loss_target.reshape(s, D)
    w_ada2, w_in2, w_out2 = w_ada[0], w_in[0], w_out[0]
    b_ada_f2 = b_ada_final.reshape(1, 2 * D)
    gf = final_norm_g.reshape(1, D)

    c_all = _allgather_sum_call(jnp.pad(c, ((0, 7), (0, 0))), "gather_c", False)[0][::8]
    mod_p, c_act = _rowmat_call(c_all, w_ada2, lax.dynamic_slice(b_ada, (0, chip * n_ada), (1, n_ada)), "mod")
    modf_p, _ = _rowmat_call(c_all, w_ada_final, lax.dynamic_slice(b_ada_f2, (0, chip * n_adaf), (1, n_adaf)), "mod_final")
    mods = _allgather_sum_call(jnp.concatenate([mod_p, modf_p], axis=1), "gather_mod", False)[0]
    my_rows = [lax.dynamic_slice(mods, (16 * j + me, 0), (1, n_ada + n_adaf)) for j in range(N_CHIP)]
    mod = jnp.concatenate([r[:, :n_ada] for r in my_rows], axis=1)
    mod_f = jnp.concatenate([r[:, n_ada:] for r in my_rows], axis=1)
    shift, scale, gate = mod[:, :D], mod[:, D:2 * D], mod[:, 2 * D:]
    shift_f, scale_f = mod_f[:, :D], mod_f[:, D:]

    pos = jnp.stack([chip, ac]).astype(jnp.int32)
    w_in_own = _cast_into_call(pos, w_in2, (D, D_IN), "cast_w_in")
    w_out_own = _cast_into_call(pos, w_out2, (D, D), "cast_w_out")

    cos, sin = _rope_tables(s)
    b_sp_t = b_spatial[0].T
    sinks1 = sinks.reshape(N_Q)
    h, proj, w_in_bf, w_out_bf = _proj_gather_call(pos, x2d, shift, scale, norm_g, w_in_own, w_out_own)
    y = _mix_fwd_call(proj, cos, sin, ln_v_g, ln_v_b, w_spatial[0], b_sp_t, sinks1)
    dx2, do, dy, st_tail = _tail_call(y, w_out_bf, x2d, tgt, gate, shift_f, scale_f, gf)

    rel_o = [(0, 1), (1, 0), (1, 1), (2, 0), (2, 1), (3, 0), (3, 1)]
    rel_a = [(1, 0), (1, 1), (2, 0), (2, 1)]
    rel_b = [(3, 0), (3, 1), (0, 1)]
    piece_a, piece_b = _w_in_piece([0, 0, 1, 1]), _w_in_piece([0, 0, 1])
    half_in, half_out = (D // 2, W_IN_SHARD), (W_OUT_SHARD // 2, D)

    g_w_out_p = _tn_call(y, do, "grad_w_out")
    st_o = _exchange_start_call(g_w_out_p, rel_o, _w_out_piece, half_out, "send_w_out")
    dproj, st_ln, d_wsp, d_bsp_t, d_sink = _mix_bwd_call(
        proj, dy, cos, sin, ln_v_g + st_o[4], ln_v_b, w_spatial[0], jnp.swapaxes(w_spatial[0], 1, 2), b_sp_t, sinks1)
    g_w_in_a = _tn_shards_call(pos, h, dproj, (1, 2), "grad_w_in_a")
    st_a = _exchange_start_call(g_w_in_a, rel_a, piece_a, half_in, "send_w_in_a")
    g_w_in_b = _tn_shards_call(pos, h, dproj, (3, 0), "grad_w_in_b")
    st_b = _exchange_start_call(g_w_in_b, rel_b, piece_b, half_in, "send_w_in_b")
    rel_all = rel_o
    st_s = _exchange_start_call(d_wsp, rel_all, _group_piece, (BLK, BLK), "send_w_spatial")
    sent = st_a[4] + st_b[4] + st_s[4]
    grad_x, st_dh = _dh_call(dproj, w_in_bf, x2d, dx2, scale + sent, norm_g)

    g_w_out_p, recv_o = _exchange_wait_call(st_o, rel_o, _w_out_piece, st_dh, "wait_w_out")
    _, recv_a = _exchange_wait_call(st_a, rel_a, piece_a, st_dh, "wait_w_in_a")
    g_w_in_b, recv_b = _exchange_wait_call(st_b, rel_b, piece_b, st_dh, "wait_w_in_b")
    d_wsp, recv_s = _exchange_wait_call(st_s, rel_all, _group_piece, st_dh, "wait_w_spatial")
    mine_in = _sum_pieces_call(pos, g_w_in_b, lambda i, p, nrb: (p[1] * nrb + i, 1), [recv_a, recv_b], "sum_w_in")
    mine_out = _sum_pieces_call(pos, g_w_out_p, lambda i, p, nrb: ((2 * p[0] + p[1]) * nrb + i, 0), [recv_o], "sum_w_out")
    wsp_group = _sum_pieces_call(pos, d_wsp.reshape(GROUPS * BLK, BLK), lambda i, p, nrb: (2 * p[0] + p[1], 0), [recv_s],
                                 "sum_w_spatial")
    to_sibling = [(0, 1)]
    st_pi = _exchange_start_call(mine_in, to_sibling, _whole_piece, half_in, "swap_w_in")
    st_po = _exchange_start_call(mine_out, to_sibling, _whole_piece, half_out, "swap_w_out")

    misc = jnp.concatenate([st_ln, d_bsp_t[:, :GROUPS].T, d_sink, jnp.zeros((8, D - D_A - 2 * LANE), F32)], axis=1)
    pack = jnp.concatenate([wsp_group.reshape(8, D) + (st_pi[4] + st_po[4]), st_tail, st_dh, misc], axis=0)
    rows = pack.shape[0]
    packs, tot = _allgather_sum_call(pack, "gather_small", True)
    packs = packs.reshape(N_DEV, rows, D)
    dmod_all = jnp.concatenate([packs[:, 16, :], packs[:, 17, :], packs[:, 11, :]], axis=1)
    dmodf_all = jnp.concatenate([packs[:, 8, :], packs[:, 9, :]], axis=1)
    loss = tot[13, 0]
    mine_in, theirs_in = _exchange_wait_call(st_pi, to_sibling, _whole_piece, tot, "swapped_w_in")
    mine_out, theirs_out = _exchange_wait_call(st_po, to_sibling, _whole_piece, tot, "swapped_w_out")
    small = {
        "b_ada": jnp.concatenate([tot[16:17], tot[17:18], tot[11:12]], axis=1),
        "norm_g": tot[18:19],
        "ln_v_g": tot[24:25, :D_A],
        "ln_v_b": tot[25:26, :D_A],
        "w_spatial": packs[:, 0:8, :].reshape(GROUPS * BLK, BLK),
        "b_spatial": tot[24:32, D_A:D_A + BLK],
        "sinks": tot[24:25, D_A + LANE:D_A + LANE + N_Q],
        "b_ada_final": jnp.concatenate([tot[8:9], tot[9:10]], axis=1),
        "final_norm_g": tot[10:11],
    }

    weights = dict(w_ada=w_ada, b_ada=b_ada, norm_g=norm_g, w_in=w_in, ln_v_g=ln_v_g, ln_v_b=ln_v_b, w_spatial=w_spatial,
                   b_spatial=b_spatial, sinks=sinks, w_out=w_out, w_ada_final=w_ada_final, b_ada_final=b_ada_final,
                   final_norm_g=final_norm_g)
    m_in = dict(w_ada=m_w_ada, b_ada=m_b_ada, norm_g=m_norm_g, w_in=m_w_in, ln_v_g=m_ln_v_g, ln_v_b=m_ln_v_b,
                w_spatial=m_w_spatial, b_spatial=m_b_spatial, sinks=m_sinks, w_out=m_w_out, w_ada_final=m_w_ada_final,
                b_ada_final=m_b_ada_final, final_norm_g=m_final_norm_g)
    v_in = dict(w_ada=v_w_ada, b_ada=v_b_ada, norm_g=v_norm_g, w_in=v_w_in, ln_v_g=v_ln_v_g, ln_v_b=v_ln_v_b,
                w_spatial=v_w_spatial, b_spatial=v_b_spatial, sinks=v_sinks, w_out=v_w_out, w_ada_final=v_w_ada_final,
                b_ada_final=v_b_ada_final, final_norm_g=v_final_norm_g)
    c_act_t = c_act.T
    outer = {"w_ada": lax.dynamic_slice(dmod_all, (0, chip * n_ada), (N_DEV, n_ada)),
             "w_ada_final": lax.dynamic_slice(dmodf_all, (0, chip * n_adaf), (N_DEV, n_adaf))}
    halves = {"w_in": (mine_in, theirs_in[0]), "w_out": (mine_out, theirs_out[0])}
    done = {}
    for name, (mine, theirs) in halves.items():
        shape2 = (2 * mine.shape[0], mine.shape[1])
        done[name] = _adam_halves_call(pos, weights[name].reshape(shape2), mine, theirs, m_in[name].reshape(shape2),
                                       v_in[name].reshape(shape2), "adam_" + name)
    for name, dm in outer.items():
        shape2 = (D, dm.shape[1])
        done[name] = _adam_outer_call(weights[name].reshape(shape2), c_act_t, dm, m_in[name].reshape(shape2),
                                      v_in[name].reshape(shape2), "adam_" + name)
    updates = _adam_small_call([(weights[name].reshape(g.shape), g, m_in[name].reshape(g.shape), v_in[name].reshape(g.shape))
                                for name, g in small.items()])
    for (name, g), upd in zip(small.items(), updates):
        done[name] = (g, *upd)
    outs = [[done[name][k].reshape(w.shape) for name, w in weights.items()] for k in range(4)]
    return (loss, grad_x.reshape(x.shape), *outs[0], *outs[1], *outs[2], *outs[3])
```

```python
import jax
import jax.numpy as jnp
from jax import lax
from jax.experimental import pallas as pl
from jax.experimental.pallas import tpu as pltpu

F32 = jnp.float32
BF16 = jnp.bfloat16
MESH = pl.DeviceIdType.MESH

D = 2048
D_A = 1024
D_B = 1024
D_KV = 256
HEAD = 64
N_Q = 16
N_KV = 4
Q_PER_KV = N_Q // N_KV
BLK = 128
GROUPS = 8
D_IN = 5632
OFF_Q, OFF_K, OFF_V, OFF_ZB = 3072, 4096, 4352, 4608
N_CHIP = 4
N_DEV = 8
W_IN_SHARD = D_IN // N_CHIP
W_OUT_SHARD = D // N_CHIP
EPS = 1e-5
SCALE = HEAD ** -0.5
NEG = -1e30
LANE = 128
VMEM_LIMIT = 56 * 1024 * 1024

ADAM_LR, ADAM_B1, ADAM_B2, ADAM_EPS, ADAM_WD, ADAM_STEP = 0.001, 0.9, 0.999, 1e-08, 0.01, 10
ADAM_C1 = 1.0 - ADAM_B1 ** ADAM_STEP
ADAM_C2 = 1.0 - ADAM_B2 ** ADAM_STEP
ADAM_ROWS = 256

NT = (((1,), (1,)), ((), ()))
TN = (((0,), (0,)), ((), ()))


def _params(*sem):
    return pltpu.CompilerParams(dimension_semantics=sem, vmem_limit_bytes=VMEM_LIMIT)


def _silu_parts(z):
    z = z.astype(F32)
    sig = 1.0 / (1.0 + jnp.exp(-z))
    return z * sig, sig


def _swap_halves(v, first_half):
    return jnp.where(first_half, pltpu.roll(v, 96, 1), pltpu.roll(v, 32, 1))


def _rope(v, cos_t, sin_s, first_half):
    v = v.astype(F32)
    return v * cos_t + _swap_halves(v, first_half) * sin_s


def _unrope(dv, cos_t, sin_s, first_half):
    return dv * cos_t - _swap_halves(dv, first_half) * sin_s


def _lane_masks():
    lane = lax.broadcasted_iota(jnp.int32, (BLK, LANE), 1)
    return (lane % HEAD) < (HEAD // 2), lane < HEAD


def _band_valid(first_block_bound, rows=BLK):
    rr = lax.broadcasted_iota(jnp.int32, (rows, 2 * BLK), 0) & (BLK - 1)
    jj = lax.broadcasted_iota(jnp.int32, (rows, 2 * BLK), 1)
    return (jj > rr) & (jj <= rr + BLK) & (jj >= first_block_bound)


def _dup_kv(slab, lo):
    slab = slab.astype(F32)
    rolled = pltpu.roll(slab, HEAD, 1)
    return jnp.where(lo, slab, rolled).astype(BF16), jnp.where(lo, rolled, slab).astype(BF16)


def _stack_heads(ref, sb, slab, lo, dtype):
    kh, base = sb // 2, 2 * (sb % 2) * BLK
    zero = jnp.zeros_like(slab)
    ref[kh, base:base + BLK, :] = jnp.where(lo, slab, zero).astype(dtype)
    ref[kh, base + BLK:base + 2 * BLK, :] = jnp.where(lo, zero, slab).astype(dtype)


def _unstack_heads(ref, sb, lo):
    kh, base = sb // 2, 2 * (sb % 2) * BLK
    return jnp.where(lo, ref[kh, base:base + BLK, :], ref[kh, base + BLK:base + 2 * BLK, :])


def _sink_column(sinks_ref, kh):
    row = lax.broadcasted_iota(jnp.int32, (Q_PER_KV * BLK, 1), 0)
    col = jnp.full(row.shape, sinks_ref[Q_PER_KV * kh + Q_PER_KV - 1], F32)
    for n in range(Q_PER_KV - 2, -1, -1):
        col = jnp.where(row < (n + 1) * BLK, sinks_ref[Q_PER_KV * kh + n], col)
    return col


def _tril():
    t = lax.broadcasted_iota(jnp.int32, (BLK, BLK), 0)
    s = lax.broadcasted_iota(jnp.int32, (BLK, BLK), 1)
    return s <= t


def _layer_norm_fwd(va, lg, lb):
    va = va.astype(F32)
    mu =jnp.mean(va, axis=-1, keepdims=True)
    xc = va - mu
    rstd = lax.rsqrt(jnp.mean(xc * xc, axis=-1, keepdims=True) + EPS)
    vhat = xc * rstd
    return vhat, rstd, vhat * lg + lb


def _softmax_sink(qm, kdup, bias, sink):
    s = lax.dot_general(qm, kdup, NT, preferred_element_type=F32) + bias
    m = jnp.maximum(jnp.max(s, axis=-1, keepdims=True), sink)
    p = jnp.exp(s - m)
    esink = jnp.exp(sink - m)
    inv = 1.0 / (jnp.sum(p, axis=-1, keepdims=True) + esink)
    return p * inv, esink * inv


def _band_bias(bias_ref):
    rows = bias_ref.shape[1]
    bias_ref[0] = jnp.where(_band_valid(BLK, rows), 0.0, NEG)
    bias_ref[1] = jnp.where(_band_valid(0, rows), 0.0, NEG)


def _rowmat_call(c_all, w, b, name):
    n = w.shape[1]
    tn = 512

    def body(c_ref, w_ref, b_ref, o_ref, ca_ref):
        ca, _ = _silu_parts(c_ref[...])
        ca_ref[...] = ca
        o_ref[...] = jnp.dot(ca.astype(BF16), w_ref[...].astype(BF16), preferred_element_type=F32) + b_ref[...]

    return pl.pallas_call(
        body, name=name, grid=(n // tn,),
        in_specs=[pl.BlockSpec((N_DEV, D), lambda j: (0, 0)), pl.BlockSpec((D, tn), lambda j: (0, j)),
                  pl.BlockSpec((1, tn), lambda j: (0, j))],
        out_specs=[pl.BlockSpec((N_DEV, tn), lambda j: (0, j)), pl.BlockSpec((N_DEV, D), lambda j: (0, 0))],
        out_shape=[jax.ShapeDtypeStruct((N_DEV, n), F32), jax.ShapeDtypeStruct((N_DEV, D), F32)],
        compiler_params=_params("arbitrary"),
    )(c_all, w, b)


def _cast_into_call(pos, w, full_shape, name):
    r, n = w.shape
    tr = min(r, 512)
    by_cols = full_shape[0] == r
    nrb = r // tr

    def body(pos_ref, w_ref, o_ref):
        o_ref[...] = w_ref[...].astype(BF16)

    out_map = (lambda i, pos: (i, pos[0])) if by_cols else (lambda i, pos: (pos[0] * nrb + i, 0))
    return pl.pallas_call(
        body, name=name,
        grid_spec=pltpu.PrefetchScalarGridSpec(
            num_scalar_prefetch=1, grid=(nrb,),
            in_specs=[pl.BlockSpec((tr, n), lambda i, pos: (i, 0))], out_specs=pl.BlockSpec((tr, n), out_map)),
        out_shape=jax.ShapeDtypeStruct(full_shape, BF16), compiler_params=_params("parallel"),
    )(pos, w)


W_IN_PARTS = ((0, 768), (768, 640))
OUT_STREAMS = 4
X_STREAMS = 4


def _proj_gather_call(pos, x, shift, scale, norm_g, wi_full, wo_full):
    s = x.shape[0]
    tm = min(s, 512)
    nrow = s // tm
    hi = D // 2
    ho = W_OUT_SHARD // 2
    phases = [(0, None), (1, 0), (2, 0), (1, 1), (2, 1), (3, 0), (3, 1)]

    def body(pos_ref, *refs):
        x_refs = refs[:X_STREAMS]
        (sh_ref, sc_ref, g_ref, _, _, h_ref, proj_ref, fi_ref, fo_ref,
         h_all, wbuf, obuf, send_sems, recv_sems, load_sems, out_sems) = refs[X_STREAMS:]
        p = pl.program_id(0)
        i = pl.program_id(1)
        x_, y_, c_ = _coords()
        me, sibling = (x_, y_, c_), (x_, y_, 1 - c_)

        def shard_of(q):
            px, py, _ = _peer(x_, y_, c_, q, 0)
            return 2 * px + py

        def cols_of(q, cp):
            off, w = (0, W_IN_SHARD) if cp is None else W_IN_PARTS[cp]
            return shard_of(q) * W_IN_SHARD + off, w

        def part(which, q, pc, sub, cp):
            n = hi if which == 0 else ho
            base = pc * n
            if sub is not None:
                n //= 2
                base = base + sub * n
            if which == 0:
                c0, w = cols_of(q, cp)
                return fi_ref.at[pl.ds(base, n), pl.ds(c0, w)]
            return fo_ref.at[pl.ds(shard_of(q) * W_OUT_SHARD + base, n), :]

        def copy(k, ref, to):
            return pltpu.make_async_remote_copy(src_ref=ref, dst_ref=ref, send_sem=send_sems.at[k], recv_sem=recv_sems.at[k],
                                                device_id=to, device_id_type=MESH)

        def sem(which, kind, j, cp):
            return 4 * kind + 2 * cp + j if which == 0 else 16 + 2 * kind + j

        def to_neighbour(which, q, cp=None):
            return copy(sem(which, 0, q - 1, cp), part(which, 0, c_, None, cp), _peer(x_, y_, c_, q, 0))

        def from_neighbour(which, q, cp=None):
            return copy(sem(which, 0, q - 1, cp), part(which, q, c_, None, cp), me)

        def relay(which, q, cp=None):
            return copy(sem(which, 1, q - 1, cp), part(which, q, c_, q - 1, cp), _peer(x_, y_, c_, 3 - q, 0))

        def relayed(which, sub, cp=None):
            return copy(sem(which, 1, sub, cp), part(which, 3, c_, sub, cp), me)

        def to_sibling(which, q, cp=None):
            return copy(sem(which, 2, q - 1, cp), part(which, q, c_, None, cp), sibling)

        def from_sibling(which, q, cp=None):
            return copy(sem(which, 2, q - 1, cp), part(which, q, 1 - c_, None, cp), me)

        def relayed_to_sibling(which, sub, cp=None):
            return copy(sem(which, 3, sub, cp), part(which, 3, c_, sub, cp), sibling)

        def relayed_from_sibling(which, sub, cp=None):
            return copy(sem(which, 3, sub, cp), part(which, 3, 1 - c_, sub, cp), me)

        def pass_on_neighbours(which, cp=None):
            for q in (1, 2):
                from_neighbour(which, q, cp).wait_recv()
                to_sibling(which, q, cp).start()
                relay(which, q, cp).start()

        def pass_on_relayed(which, cp=None):
            for sub in range(2):
                relayed(which, sub, cp).wait_recv()
                relayed_to_sibling(which, sub, cp).start()

        def shard_load(k):
            c0, w = cols_of(*phases[k])
            return pltpu.make_async_copy(fi_ref.at[:, pl.ds(c0, w)], wbuf.at[k % 2, :, 0:w], load_sems.at[k % 2])

        class OutCopies:
            def __init__(self, k, slot, row0):
                c0, w = cols_of(*phases[k])
                strip = tm // OUT_STREAMS
                self.copies = [pltpu.make_async_copy(obuf.at[slot, n * strip:(n + 1) * strip, 0:w],
                                                     proj_ref.at[pl.ds(row0 + n * strip, strip), pl.ds(c0, w)],
                                                     out_sems.at[slot, n]) for n in range(OUT_STREAMS)]

            def start(self):
                for cp in self.copies:
                    cp.start()

            def wait(self):
                for cp in self.copies:
                    cp.wait()

        out_copy = OutCopies

        def drain(k):
            for j in range(min(2, nrow)):
                out_copy(k, (nrow - 1 - j) % 2, 0).wait()

        def arrivals(k):
            q, cp = phases[k]
            if k == 0:
                for cp_ in range(2):
                    for q_ in (1, 2):
                        to_neighbour(0, q_, cp_).start()
            elif q < 3 and k in (1, 3):
                pass_on_neighbours(0, cp)
                if k == 1:
                    for q_ in (1, 2):
                        to_neighbour(1, q_).start()
            elif k == 5:
                for cp_ in range(2):
                    pass_on_relayed(0, cp_)
                pass_on_neighbours(1)
            if q in (1, 2):
                from_sibling(0, q, cp).wait_recv()
            elif q == 3:
                for sub in range(2):
                    relayed_from_sibling(0, sub, cp).wait_recv()

        rows = pl.ds(pl.multiple_of(i * tm, tm), tm)
        slot = i % 2
        for k, (q, cp) in enumerate(phases):
            @pl.when(p == k)
            def _(k=k, q=q, cp=cp):
                @pl.when(i == 0)
                def _():
                    if k == 0:
                        arrivals(0)
                        shard_load(0).start()
                    else:
                        drain(k - 1)
                    shard_load(k).wait()

                if k + 1 < len(phases):
                    @pl.when(i == max(nrow - 2, 0))
                    def _():
                        arrivals(k + 1)
                        shard_load(k + 1).start()

                if k == 0:
                    wx = D // X_STREAMS
                    ssq = sum(jnp.sum(xr[...] * xr[...], axis=-1, keepdims=True) for xr in x_refs)
                    r = lax.rsqrt(ssq * (1.0 / D) + EPS)
                    for n, xr in enumerate(x_refs):
                        cols = slice(n * wx, (n + 1) * wx)
                        hv = ((xr[...] * r * g_ref[:, cols]) * (1.0 + sc_ref[:, cols]) + sh_ref[:, cols]).astype(BF16)
                        h_ref[:, cols] = hv
                        h_all[rows, cols] = hv

                @pl.when(i >= 2)
                def _():
                    out_copy(k, slot, 0).wait()

                w = cols_of(q, cp)[1]
                obuf[slot, :, 0:w] = jnp.dot(h_all[rows, :], wbuf[k % 2, :, 0:w], preferred_element_type=F32).astype(BF16)
                out_copy(k, slot, pl.multiple_of(i * tm, tm)).start()

        @pl.when((p == len(phases) - 1) & (i == nrow - 1))
        def _():
            drain(len(phases) - 1)
            pass_on_relayed(1)
            for q in (1, 2):
                from_sibling(1, q).wait_recv()
            for sub in range(2):
                relayed_from_sibling(1, sub).wait_recv()
            for which, cps in ((0, (0, 1)), (1, (None,))):
                for cp in cps:
                    for q in (1, 2):
                        to_neighbour(which, q, cp).wait_send()
                        relay(which, q, cp).wait_send()
                        to_sibling(which, q, cp).wait_send()
                        relayed_to_sibling(which, q - 1, cp).wait_send()

    vec = pl.BlockSpec((1, D), lambda p, i, pos: (0, 0))
    first_phase_rows = lambda p, i, pos: (jnp.where(p == 0, i, nrow - 1), 0)
    anyspec = pl.BlockSpec(memory_space=pl.ANY)
    x_spec = lambda n: pl.BlockSpec((tm, D // X_STREAMS), lambda p, i, pos: (jnp.where(p == 0, i, nrow - 1), n))
    return pl.pallas_call(
        body, name="proj_gather",
        grid_spec=pltpu.PrefetchScalarGridSpec(
            num_scalar_prefetch=1, grid=(len(phases), nrow),
            in_specs=[x_spec(n) for n in range(X_STREAMS)] + [vec, vec, vec, anyspec, anyspec],
            out_specs=[pl.BlockSpec((tm, D), first_phase_rows), anyspec, anyspec, anyspec],
            scratch_shapes=[pltpu.VMEM((s, D), BF16), pltpu.VMEM((2, D, W_IN_SHARD), BF16), pltpu.VMEM((2, tm, W_IN_SHARD), BF16),
                            pltpu.SemaphoreType.DMA((24,)), pltpu.SemaphoreType.DMA((24,)), pltpu.SemaphoreType.DMA((2,)),
                            pltpu.SemaphoreType.DMA((2, OUT_STREAMS))]),
        out_shape=[jax.ShapeDtypeStruct((s, D), BF16), jax.ShapeDtypeStruct((s, D_IN), BF16),
                   jax.ShapeDtypeStruct((D, D_IN), BF16), jax.ShapeDtypeStruct((D, D), BF16)],
        input_output_aliases={X_STREAMS + 4: 2, X_STREAMS + 5: 3},
        compiler_params=_params("arbitrary", "arbitrary"),
    )(pos, *([x] * X_STREAMS), shift, scale, norm_g, wi_full, wo_full)


def _proj_specs(rev_nb=None):
    if rev_nb is None:
        row = lambda i: i
    else:
        row = lambda i: rev_nb - 1 - i
    wide = lambda col: pl.BlockSpec((BLK, D_A), lambda i: (row(i), col))
    kv = lambda col: pl.BlockSpec((BLK, D_KV), lambda i: (row(i), col))
    half = lambda col: pl.BlockSpec((BLK, 512), lambda i: (row(i), col))
    return [wide(0), wide(1), wide(2), wide(3), kv(OFF_K // D_KV), kv(OFF_V // D_KV), half(OFF_ZB // 512), half(OFF_ZB // 512 + 1)]


def _mix_fwd_call(proj, cos, sin, ln_g, ln_b, w_sp, b_sp_t, sinks):
    s = proj.shape[0]
    nb = s // BLK

    def body(ua_ref, va_ref, za_ref, q_ref, k_ref, v_ref, zb0_ref, zb1_ref, cos_ref, sin_ref, lg_ref, lb_ref,
             w_ref, bt_ref, sinks_ref, y_ref, kdup_ref, vdup_ref, qm_ref, ost_ref, bias_ref):
        i = pl.program_id(0)
        first_half, lo = _lane_masks()
        cos_t = cos_ref[...]
        sin_t = sin_ref[...]

        _, _, vln = _layer_norm_fwd(va_ref[...], lg_ref[...], lb_ref[...])
        tril = _tril()
        for g in range(GROUPS):
            cols = slice(g * BLK, (g + 1) * BLK)
            wg = jnp.where(tril, w_ref[g], 0.0).astype(BF16)
            sg = jnp.dot(wg, vln[:, cols].astype(BF16), preferred_element_type=F32) + bt_ref[:, g:g + 1]
            gate, _ = _silu_parts(za_ref[:, cols])
            y_ref[:, cols] = (ua_ref[:, cols].astype(F32) * sg * gate).astype(BF16)

        @pl.when(i == 0)
        def _():
            kdup_ref[:, 0:BLK, :] = jnp.zeros((N_KV, BLK, LANE), BF16)
            vdup_ref[:, 0:BLK, :] = jnp.zeros((N_KV, BLK, LANE), BF16)
            _band_bias(bias_ref)

        @pl.when(i > 0)
        def _():
            kdup_ref[:, 0:BLK, :] = kdup_ref[:, BLK:2 * BLK, :]
            vdup_ref[:, 0:BLK, :] = vdup_ref[:, BLK:2 * BLK, :]

        for ks in range(2):
            cols = slice(ks * LANE, (ks + 1) * LANE)
            kr = _rope(k_ref[:, cols], cos_t, sin_t, first_half)
            for n, (kd, vd) in enumerate(zip(_dup_kv(kr, lo), _dup_kv(v_ref[:, cols], lo))):
                kdup_ref[2 * ks + n, BLK:2 * BLK, :] = kd
                vdup_ref[2 * ks + n, BLK:2 * BLK, :] = vd
        for sb in range(8):
            _stack_heads(qm_ref, sb, _rope(q_ref[:, sb * LANE:(sb + 1) * LANE], cos_t, sin_t, first_half) * SCALE, lo, BF16)

        block_kind = jnp.where(i > 0, 1, 0)

        def kv_head(kh, carry):
            probs, _ = _softmax_sink(qm_ref[kh], kdup_ref[kh], bias_ref[block_kind], _sink_column(sinks_ref, kh))
            ost_ref[kh] = jnp.dot(probs.astype(BF16), vdup_ref[kh], preferred_element_type=F32)
            return carry

        lax.fori_loop(0, N_KV, kv_head, 0, unroll=2)
        for sb in range(8):
            cols = slice(sb * LANE, (sb + 1) * LANE)
            zb = zb0_ref[:, cols] if sb < 4 else zb1_ref[:, (sb - 4) * LANE:(sb - 3) * LANE]
            gate, _ = _silu_parts(zb)
            y_ref[:, D_A + sb * LANE:D_A + (sb + 1) * LANE] = (_unstack_heads(ost_ref, sb, lo) * gate).astype(BF16)

    tab = pl.BlockSpec((BLK, LANE), lambda i: (i, 0))
    return pl.pallas_call(
        body, name="mix_fwd", grid=(nb,),
        in_specs=_proj_specs() + [
            tab, tab, pl.BlockSpec((1, D_A), lambda i: (0, 0)), pl.BlockSpec((1, D_A), lambda i: (0, 0)),
            pl.BlockSpec((GROUPS, BLK, BLK), lambda i: (0, 0, 0)), pl.BlockSpec((BLK, GROUPS), lambda i: (0, 0)),
            pl.BlockSpec(memory_space=pltpu.SMEM)],
        out_specs=pl.BlockSpec((BLK, 2 * D_A), lambda i: (i, 0)),
        out_shape=jax.ShapeDtypeStruct((s, 2 * D_A), BF16),
        scratch_shapes=[pltpu.VMEM((N_KV, 2 * BLK, LANE), BF16), pltpu.VMEM((N_KV, 2 * BLK, LANE), BF16),
                        pltpu.VMEM((N_KV, Q_PER_KV * BLK, LANE), BF16), pltpu.VMEM((N_KV, Q_PER_KV * BLK, LANE), F32),
                        pltpu.VMEM((2, Q_PER_KV * BLK, 2 * BLK), F32)],
        compiler_params=_params("arbitrary"),
    )(proj, proj, proj, proj, proj, proj, proj, proj, cos, sin, ln_g, ln_b, w_sp, b_sp_t, sinks)


def _tail_call(y, w_out_bf, x, target, gate, shift_f, scale_f, gf):
    s = x.shape[0]
    tm = min(s, 256)
    nsteps = s // tm

    def body(y_ref, w_ref, x_ref, t_ref, gate_ref, shf_ref, scf_ref, gf_ref, dx2_ref, do_ref, dy_ref, st_ref):
        i = pl.program_id(0)

        @pl.when(i == 0)
        def _():
            st_ref[...] = jnp.zeros((8, D), F32)

        o = jnp.dot(y_ref[...], w_ref[...], preferred_element_type=F32)
        gate_v = gate_ref[...]
        x2 = x_ref[...] + gate_v * o
        r2 = lax.rsqrt(jnp.mean(x2 * x2, axis=-1, keepdims=True) + EPS)
        xn2 = x2 * r2
        hn2 = xn2 * gf_ref[...]
        one_sc = 1.0 + scf_ref[...]
        err = hn2 * one_sc + shf_ref[...] - t_ref[...]
        dout = err * (1.0 / D)
        dhn2 = dout * one_sc
        dxn2 = dhn2 * gf_ref[...]
        dx2 = r2 * (dxn2 - xn2 * jnp.mean(dxn2 * xn2, axis=-1, keepdims=True))
        dx2_ref[...] = dx2
        do = (dx2 * gate_v).astype(BF16)
        do_ref[...] = do
        dy_ref[...] = lax.dot_general(do, w_ref[...], NT, preferred_element_type=F32)
        st_ref[0:1, :] += jnp.sum(dout, axis=0, keepdims=True)
        st_ref[1:2, :] += jnp.sum(dout * hn2, axis=0, keepdims=True)
        st_ref[2:3, :] += jnp.sum(dhn2 * xn2, axis=0, keepdims=True)
        st_ref[3:4, :] += jnp.sum(dx2 * o, axis=0, keepdims=True)
        st_ref[4:5, :] += jnp.sum(err * err, axis=0, keepdims=True)

        @pl.when(i == nsteps - 1)
        def _():
            st_ref[5:6, :] = jnp.full((1, D), 0.5 / D, F32) * jnp.sum(st_ref[4:5, :])

    vec = pl.BlockSpec((1, D), lambda i: (0, 0))
    rows = lambda: pl.BlockSpec((tm, D), lambda i: (i, 0))
    return pl.pallas_call(
        body, name="tail", grid=(nsteps,),
        in_specs=[rows(), pl.BlockSpec((D, D), lambda i: (0, 0)), rows(), rows(), vec, vec, vec, vec],
        out_specs=[rows(), rows(), rows(), pl.BlockSpec((8, D), lambda i: (0, 0))],
        out_shape=[jax.ShapeDtypeStruct((s, D), F32), jax.ShapeDtypeStruct((s, D), BF16), jax.ShapeDtypeStruct((s, D), F32),
                   jax.ShapeDtypeStruct((8, D), F32)],
        compiler_params=_params("arbitrary"),
    )(y, w_out_bf, x, target, gate, shift_f, scale_f, gf)


def _tn_call(a, b, name):
    s, m = a.shape
    n = b.shape[1]
    tn = 512
    ts = min(s, 1024)
    nk = s // ts

    def body(a_ref, b_ref, o_ref, acc_ref):
        k = pl.program_id(1)

        @pl.when(k == 0)
        def _():
            acc_ref[...] = jnp.zeros((m, tn), F32)

        acc_ref[...] += lax.dot_general(a_ref[...], b_ref[...], TN, preferred_element_type=F32)

        @pl.when(k == nk - 1)
        def _():
            o_ref[...] = acc_ref[...].astype(BF16)

    return pl.pallas_call(
        body, name=name, grid=(n // tn, nk),
        in_specs=[pl.BlockSpec((ts, m), lambda j, k: (k, 0)), pl.BlockSpec((ts, tn), lambda j, k: (k, j))],
        out_specs=pl.BlockSpec((m, tn), lambda j, k: (0, j)),
        out_shape=jax.ShapeDtypeStruct((m, n), BF16),
        scratch_shapes=[pltpu.VMEM((m, tn), F32)],
        compiler_params=_params("parallel", "arbitrary"),
    )(a, b)


def _tn_shards_call(pos, a, b, qs, name):
    s, m = a.shape
    ts = min(s, 1024)
    nk = s // ts

    def body(pos_ref, a_ref, b_ref, o_ref, acc_ref):
        k = pl.program_id(1)

        @pl.when(k == 0)
        def _():
            acc_ref[...] = jnp.zeros((m, W_IN_SHARD), F32)

        acc_ref[...] += lax.dot_general(a_ref[...], b_ref[...], TN, preferred_element_type=F32)

        @pl.when(k == nk - 1)
        def _():
            o_ref[...] = acc_ref[...].astype(BF16)

    def shard(j, pos):
        q = qs[0]
        for n in range(1, len(qs)):
            q = jnp.where(j == n, qs[n], q)
        return jnp.bitwise_xor(pos[0], q)

    return pl.pallas_call(
        body, name=name,
        grid_spec=pltpu.PrefetchScalarGridSpec(
            num_scalar_prefetch=1, grid=(len(qs), nk),
            in_specs=[pl.BlockSpec((ts, m), lambda j, k, pos: (k, 0)),
                      pl.BlockSpec((ts, W_IN_SHARD), lambda j, k, pos: (k, shard(j, pos)))],
            out_specs=pl.BlockSpec((m, W_IN_SHARD), lambda j, k, pos: (0, j)),
            scratch_shapes=[pltpu.VMEM((m, W_IN_SHARD), F32)]),
        out_shape=jax.ShapeDtypeStruct((m, len(qs) * W_IN_SHARD), BF16),
        compiler_params=_params("parallel", "arbitrary"),
    )(pos, a, b)


def _mix_bwd_call(proj, dy, cos, sin, ln_g, ln_b, w_sp, w_sp_t, b_sp_t, sinks):
    s = proj.shape[0]
    nb = s // BLK
    rev = lambda i: nb - 1 - i
    prev = lambda i: jnp.maximum(nb - 2 - i, 0)

    def body(ua_ref, va_ref, za_ref, q_ref, k_ref, v_ref, zb0_ref, zb1_ref, kp_ref, vp_ref, dy_ref,
             cos_ref, sin_ref, cosp_ref, sinp_ref, lg_ref, lb_ref, w_ref, wt_ref, bt_ref, sinks_ref,
             dp_ref, lnst_ref, dw_ref, dbt_ref, dsink_ref,
             kdup_ref, vdup_ref, dvln_ref, qm_ref, dom_ref, ost_ref, dqst_ref, dkdup_ref, dvdup_ref, kcar_ref, vcar_ref,
             sigb_ref, bias_ref):
        i = pl.program_id(0)
        first_half, lo = _lane_masks()
        lane8 = lax.broadcasted_iota(jnp.int32, (8, LANE), 1)
        cos_t = cos_ref[...]
        sin_t = sin_ref[...]

        @pl.when(i == 0)
        def _():
            lnst_ref[...] = jnp.zeros((8, D_A), F32)
            dw_ref[...] = jnp.zeros((GROUPS, BLK, BLK), F32)
            dbt_ref[...] = jnp.zeros((BLK, LANE), F32)
            dsink_ref[...] = jnp.zeros((8, LANE), F32)
            kcar_ref[...] = jnp.zeros((BLK, D_KV), F32)
            vcar_ref[...] = jnp.zeros((BLK, D_KV), F32)
            _band_bias(bias_ref)

        vhat, rstd, vln = _layer_norm_fwd(va_ref[...], lg_ref[...], lb_ref[...])
        tril = _tril()
        triu = jnp.logical_not(tril) | (lax.broadcasted_iota(jnp.int32, (BLK, BLK), 0) == lax.broadcasted_iota(jnp.int32, (BLK, BLK), 1))
        lane_b = lax.broadcasted_iota(jnp.int32, (BLK, LANE), 1)
        db_acc = jnp.zeros((BLK, LANE), F32)
        for g in range(GROUPS):
            cols = slice(g * BLK, (g + 1) * BLK)
            vln_g = vln[:, cols].astype(BF16)
            wg = jnp.where(tril, w_ref[g], 0.0).astype(BF16)
            sg = jnp.dot(wg, vln_g, preferred_element_type=F32) + bt_ref[:, g:g + 1]
            za = za_ref[:, cols].astype(F32)
            gate, sig = _silu_parts(za)
            ua = ua_ref[:, cols].astype(F32)
            dya_g = dy_ref[:, cols]
            dya = dya_g * gate
            dp_ref[:, cols] = (dya * sg).astype(BF16)
            dp_ref[:, 2 * D_A + g * BLK:2 * D_A + (g + 1) * BLK] = (
                dya_g * (ua * sg) * (sig * (1.0 + za * (1.0 - sig)))).astype(BF16)
            ds = dya * ua
            ds_b = ds.astype(BF16)
            wtg = jnp.where(triu, wt_ref[g], 0.0).astype(BF16)
            dvln_ref[:, cols] = jnp.dot(wtg, ds_b, preferred_element_type=F32)
            dw_ref[g] += jnp.where(tril, lax.dot_general(ds_b, vln_g, NT, preferred_element_type=F32), 0.0)
            db_acc = db_acc + jnp.where(lane_b == g, jnp.sum(ds, axis=-1, keepdims=True), 0.0)
        dbt_ref[...] += db_acc
        dvln = dvln_ref[...]
        lnst_ref[0:1, :] += jnp.sum(dvln * vhat, axis=0, keepdims=True)
        lnst_ref[1:2, :] += jnp.sum(dvln, axis=0, keepdims=True)
        dvhat = dvln * lg_ref[...]
        m1 = jnp.mean(dvhat, axis=-1, keepdims=True)
        m2 = jnp.mean(dvhat * vhat, axis=-1, keepdims=True)
        dp_ref[:, D_A:2 * D_A] = (rstd * (dvhat - m1 - vhat * m2)).astype(BF16)

        cosp = cosp_ref[...]
        sinp = sinp_ref[...]
        for ks in range(2):
            cols = slice(ks * LANE, (ks + 1) * LANE)
            kr = _rope(k_ref[:, cols], cos_t, sin_t, first_half)
            kpr = _rope(kp_ref[:, cols], cosp, sinp, first_half)
            for n, (kc, vc, kp, vp) in enumerate(zip(_dup_kv(kr, lo), _dup_kv(v_ref[:, cols], lo),
                                                     _dup_kv(kpr, lo), _dup_kv(vp_ref[:, cols], lo))):
                kdup_ref[2 * ks + n, BLK:2 * BLK, :] = kc
                vdup_ref[2 * ks + n, BLK:2 * BLK, :] = vc
                kdup_ref[2 * ks + n, 0:BLK, :] = kp
                vdup_ref[2 * ks + n, 0:BLK, :] = vp
        for sb in range(8):
            cols = slice(sb * LANE, (sb + 1) * LANE)
            _stack_heads(qm_ref, sb, _rope(q_ref[:, cols], cos_t, sin_t, first_half) * SCALE, lo, BF16)
            zb = zb0_ref[:, cols] if sb < 4 else zb1_ref[:, (sb - 4) * LANE:(sb - 3) * LANE]
            gate, sig = _silu_parts(zb)
            sigb_ref[:, cols] = sig
            _stack_heads(dom_ref, sb, dy_ref[:, D_A + sb * LANE:D_A + (sb + 1) * LANE] * gate, lo, F32)

        block_kind = jnp.where(i < nb - 1, 1, 0)

        def kv_head(kh, dsink_acc):
            qm = qm_ref[kh]
            kd = kdup_ref[kh]
            vd = vdup_ref[kh]
            probs, psink = _softmax_sink(qm, kd, bias_ref[block_kind], _sink_column(sinks_ref, kh))
            probs_b = probs.astype(BF16)
            o = jnp.dot(probs_b, vd, preferred_element_type=F32)
            ost_ref[kh] = o
            dom = dom_ref[kh]
            dom_b = dom.astype(BF16)
            delta = jnp.sum(dom * o, axis=-1, keepdims=True)
            dpr = lax.dot_general(dom_b, vd, NT, preferred_element_type=F32)
            dss = (probs * (dpr - delta)).astype(BF16)
            sd = psink * delta
            for n in range(Q_PER_KV):
                dsink_acc = dsink_acc + jnp.where(lane8 == Q_PER_KV * kh + n, -jnp.sum(sd[n * BLK:(n + 1) * BLK]), 0.0)
            dqst_ref[kh] = jnp.dot(dss, kd, preferred_element_type=F32)
            dkdup_ref[kh] = lax.dot_general(dss, qm, TN, preferred_element_type=F32)
            dvdup_ref[kh] = lax.dot_general(probs_b, dom_b, TN, preferred_element_type=F32)
            return dsink_acc

        dsink_acc = lax.fori_loop(0, N_KV // 2, lambda j, acc: kv_head(2 * j + 1, kv_head(2 * j, acc)), jnp.zeros((8, LANE), F32))
        row0 = lax.broadcasted_iota(jnp.int32, (8, LANE), 0) == 0
        dsink_ref[...] += jnp.where(row0, dsink_acc, 0.0)

        for sb in range(8):
            cols = slice(sb * LANE, (sb + 1) * LANE)
            zb = (zb0_ref[:, cols] if sb < 4 else zb1_ref[:, (sb - 4) * LANE:(sb - 3) * LANE]).astype(F32)
            sig = sigb_ref[:, cols]
            dyb = dy_ref[:, D_A + sb * LANE:D_A + (sb + 1) * LANE]
            dp_ref[:, OFF_ZB + sb * LANE:OFF_ZB + (sb + 1) * LANE] = (
                dyb * _unstack_heads(ost_ref, sb, lo) * (sig * (1.0 + zb * (1.0 - sig)))).astype(BF16)
            dq_r = _unstack_heads(dqst_ref, sb, lo) * SCALE
            dp_ref[:, OFF_Q + sb * LANE:OFF_Q + (sb + 1) * LANE] = _unrope(dq_r, cos_t, sin_t, first_half).astype(BF16)

        lo2 = lax.broadcasted_iota(jnp.int32, (2 * BLK, LANE), 1) < HEAD
        for ks in range(2):
            cols = slice(ks * LANE, (ks + 1) * LANE)
            ka = dkdup_ref[2 * ks]
            kb = dkdup_ref[2 * ks + 1]
            dk_band = jnp.where(lo2, ka + pltpu.roll(ka, HEAD, 1), kb + pltpu.roll(kb, HEAD, 1))
            va_ = dvdup_ref[2 * ks]
            vb_ = dvdup_ref[2 * ks + 1]
            dv_band = jnp.where(lo2, va_ + pltpu.roll(va_, HEAD, 1), vb_ + pltpu.roll(vb_, HEAD, 1))
            dkr = dk_band[BLK:2 * BLK, :] + kcar_ref[:, cols]
            dp_ref[:, OFF_K + ks * LANE:OFF_K + (ks + 1) * LANE] = _unrope(dkr, cos_t, sin_t, first_half).astype(BF16)
            dp_ref[:, OFF_V + ks * LANE:OFF_V + (ks + 1) * LANE] = (
                dv_band[BLK:2 * BLK, :] + vcar_ref[:, cols]).astype(BF16)
            kcar_ref[:, cols] = dk_band[0:BLK, :]
            vcar_ref[:, cols] = dv_band[0:BLK, :]

    tab = pl.BlockSpec((BLK, LANE), lambda i: (rev(i), 0))
    tabp = pl.BlockSpec((BLK, LANE), lambda i: (prev(i), 0))
    kvp = lambda col: pl.BlockSpec((BLK, D_KV), lambda i: (prev(i), col))
    vec = pl.BlockSpec((1, D_A), lambda i: (0, 0))
    w3 = pl.BlockSpec((GROUPS, BLK, BLK), lambda i: (0, 0, 0))
    return pl.pallas_call(
        body, name="mix_bwd", grid=(nb,),
        in_specs=_proj_specs(nb) + [
            kvp(OFF_K // D_KV), kvp(OFF_V // D_KV), pl.BlockSpec((BLK, 2 * D_A), lambda i: (rev(i), 0)),
            tab, tab, tabp, tabp, vec, vec, w3, w3, pl.BlockSpec((BLK, GROUPS), lambda i: (0, 0)),
            pl.BlockSpec(memory_space=pltpu.SMEM)],
        out_specs=[pl.BlockSpec((BLK, D_IN), lambda i: (rev(i), 0)), pl.BlockSpec((8, D_A), lambda i: (0, 0)), w3,
                   pl.BlockSpec((BLK, LANE), lambda i: (0, 0)), pl.BlockSpec((8, LANE), lambda i: (0, 0))],
        out_shape=[jax.ShapeDtypeStruct((s, D_IN), BF16), jax.ShapeDtypeStruct((8, D_A), F32),
                   jax.ShapeDtypeStruct((GROUPS, BLK, BLK), F32), jax.ShapeDtypeStruct((BLK, LANE), F32),
                   jax.ShapeDtypeStruct((8, LANE), F32)],
        scratch_shapes=[pltpu.VMEM((N_KV, 2 * BLK, LANE), BF16), pltpu.VMEM((N_KV, 2 * BLK, LANE), BF16),
                        pltpu.VMEM((BLK, D_A), F32), pltpu.VMEM((N_KV, Q_PER_KV * BLK, LANE), BF16),
                        pltpu.VMEM((N_KV, Q_PER_KV * BLK, LANE), F32), pltpu.VMEM((N_KV, Q_PER_KV * BLK, LANE), F32),
                        pltpu.VMEM((N_KV, Q_PER_KV * BLK, LANE), F32), pltpu.VMEM((N_KV, 2 * BLK, LANE), F32),
                        pltpu.VMEM((N_KV, 2 * BLK, LANE), F32), pltpu.VMEM((BLK, D_KV), F32), pltpu.VMEM((BLK, D_KV), F32),
                        pltpu.VMEM((BLK, D_B), F32), pltpu.VMEM((2, Q_PER_KV * BLK, 2 * BLK), F32)],
        compiler_params=_params("arbitrary"),
    )(proj, proj, proj, proj, proj, proj, proj, proj, proj, proj, dy, cos, sin, cos, sin, ln_g, ln_b, w_sp, w_sp_t,
      b_sp_t, sinks)


def _dh_call(dproj, w_bf, x, dx2, scale, norm_g):
    s = x.shape[0]
    tm = min(s, 512)
    tk = W_IN_SHARD
    nk = D_IN // tk

    def body(dp_ref, w_ref, x_ref, dx2_ref, sc_ref, g_ref, gx_ref, st_ref, acc_ref):
        i = pl.program_id(0)
        k = pl.program_id(1)

        @pl.when((i == 0) & (k == 0))
        def _():
            st_ref[...] = jnp.zeros((8, D), F32)

        @pl.when(k == 0)
        def _():
            acc_ref[...] = jnp.zeros((tm, D), F32)

        acc_ref[...] += lax.dot_general(dp_ref[...], w_ref[...], NT, preferred_element_type=F32)

        @pl.when(k == nk - 1)
        def _():
            g = g_ref[...]
            one_sc = 1.0 + sc_ref[...]

            def chunk(n, carry):
                rows = pl.ds(pl.multiple_of(n * BLK, BLK), BLK)
                dh = acc_ref[rows, :]
                xv = x_ref[rows, :]
                r = lax.rsqrt(jnp.mean(xv * xv, axis=-1, keepdims=True) + EPS)
                xn = xv * r
                dhn = dh * one_sc
                dxn = dhn * g
                gx_ref[rows, :] = dx2_ref[rows, :] + r * (dxn - xn * jnp.mean(dxn * xn, axis=-1, keepdims=True))
                st_ref[0:1, :] += jnp.sum(dh, axis=0, keepdims=True)
                st_ref[1:2, :] += jnp.sum(dh * (xn * g), axis=0, keepdims=True)
                st_ref[2:3, :] += jnp.sum(dhn * xn, axis=0, keepdims=True)
                return carry

            lax.fori_loop(0, tm // BLK, chunk, 0)

    vec = pl.BlockSpec((1, D), lambda i, k: (0, 0))
    rows = lambda: pl.BlockSpec((tm, D), lambda i, k: (i, 0))
    return pl.pallas_call(
        body, name="dh", grid=(s // tm, nk),
        in_specs=[pl.BlockSpec((tm, tk), lambda i, k: (i, k)), pl.BlockSpec((D, tk), lambda i, k: (0, k)), rows(), rows(), vec, vec],
        out_specs=[rows(), pl.BlockSpec((8, D), lambda i, k: (0, 0))],
        out_shape=[jax.ShapeDtypeStruct((s, D), F32), jax.ShapeDtypeStruct((8, D), F32)],
        scratch_shapes=[pltpu.VMEM((tm, D), F32)],
        compiler_params=_params("arbitrary", "arbitrary"),
    )(dproj, w_bf, x, dx2, scale, norm_g)


def _adam_math(w, g, m, v):
    m_new = ADAM_B1 * m + (1.0 - ADAM_B1) * g
    v_new = ADAM_B2 * v + (1.0 - ADAM_B2) * (g * g)
    m_hat = m_new / ADAM_C1
    v_hat = v_new / ADAM_C2
    delta = -ADAM_LR * (m_hat / (jnp.sqrt(v_hat) + ADAM_EPS) + ADAM_WD * w)
    return delta, m_new, v_new


def _adam_small_call(tensors):
    n = len(tensors)

    def body(*refs):
        ins, outs = refs[:4 * n], refs[4 * n:]
        for t in range(n):
            w_ref, g_ref, m_ref, v_ref = ins[4 * t:4 * t + 4]
            d, mo, vo = _adam_math(w_ref[...], g_ref[...], m_ref[...], v_ref[...])
            outs[3 * t][...], outs[3 * t + 1][...], outs[3 * t + 2][...] = d, mo, vo

    vm = pl.BlockSpec(memory_space=pltpu.VMEM)
    flat = [a for t in tensors for a in t]
    out = pl.pallas_call(
        body, name="adam_small", in_specs=[vm] * (4 * n), out_specs=[vm] * (3 * n),
        out_shape=[jax.ShapeDtypeStruct(t[0].shape, F32) for t in tensors for _ in range(3)],
        compiler_params=pltpu.CompilerParams(vmem_limit_bytes=VMEM_LIMIT),
    )(*flat)
    return [tuple(out[3 * t:3 * t + 3]) for t in range(n)]


def _adam_halves_call(pos, w, mine, theirs, m, v, name):
    r, n = w.shape
    half = r // 2
    tr = ADAM_ROWS
    nh = half // tr

    def body(pos_ref, w_ref, mine_ref, theirs_ref, m_ref, v_ref, g_ref, d_ref, mo_ref, vo_ref):
        is_mine = (pl.program_id(0) // nh) == pos_ref[1]
        g = jnp.where(is_mine, mine_ref[...], theirs_ref[...])
        g_ref[...] = g
        d_ref[...], mo_ref[...], vo_ref[...] = _adam_math(w_ref[...], g, m_ref[...], v_ref[...])

    spec = lambda: pl.BlockSpec((tr, n), lambda i, pos: (i, 0))
    hspec = lambda: pl.BlockSpec((tr, n), lambda i, pos: (i % nh, 0))
    return pl.pallas_call(
        body, name=name,
        grid_spec=pltpu.PrefetchScalarGridSpec(
            num_scalar_prefetch=1, grid=(r // tr,), in_specs=[spec(), hspec(), hspec(), spec(), spec()],
            out_specs=[spec() for _ in range(4)]),
        out_shape=[jax.ShapeDtypeStruct((r, n), F32)] * 4, compiler_params=_params("parallel"),
    )(pos, w, mine, theirs, m, v)


def _adam_outer_call(w, ct, dm, m, v, name):
    r, n = w.shape
    tr = ADAM_ROWS

    def body(w_ref, ct_ref, dm_ref, m_ref, v_ref, g_ref, d_ref, mo_ref, vo_ref):
        g = ct_ref[:, 0:1] * dm_ref[0:1, :]
        for b in range(1, N_DEV):
            g = g + ct_ref[:, b:b + 1] * dm_ref[b:b + 1, :]
        g_ref[...] = g
        d_ref[...], mo_ref[...], vo_ref[...] = _adam_math(w_ref[...], g, m_ref[...], v_ref[...])

    spec = lambda: pl.BlockSpec((tr, n), lambda i: (i, 0))
    return pl.pallas_call(
        body, name=name, grid=(r // tr,),
        in_specs=[spec(), pl.BlockSpec((tr, N_DEV), lambda i: (i, 0)), pl.BlockSpec((N_DEV, n), lambda i: (0, 0)), spec(), spec()],
        out_specs=[spec() for _ in range(4)],
        out_shape=[jax.ShapeDtypeStruct((r, n), F32)] * 4, compiler_params=_params("parallel"),
    )(w, ct, dm, m, v)


def _sum_pieces_call(pos, part, part_block, recvs, name):
    r, n = recvs[0].shape[1:]
    tr = min(r, 256)
    nrb = r // tr

    def body(pos_ref, p_ref, *refs):
        acc = p_ref[...].astype(F32)
        for r_ref in refs[:-1]:
            for d in range(r_ref.shape[0]):
                acc = acc + r_ref[d].astype(F32)
        refs[-1][...] = acc

    return pl.pallas_call(
        body, name=name,
        grid_spec=pltpu.PrefetchScalarGridSpec(
            num_scalar_prefetch=1, grid=(nrb,),
            in_specs=[pl.BlockSpec((tr, n), lambda i, pos: part_block(i, pos, nrb))] + [
                pl.BlockSpec((rv.shape[0], tr, n), lambda i, pos: (0, i, 0)) for rv in recvs],
            out_specs=pl.BlockSpec((tr, n), lambda i, pos: (i, 0))),
        out_shape=jax.ShapeDtypeStruct((r, n), F32), compiler_params=_params("parallel"),
    )(pos, part, *recvs)


def _coords():
    return lax.axis_index("x"), lax.axis_index("y"), lax.axis_index("c")


def _allgather_sum_call(blk, name, with_sum):
    m_per, n = blk.shape

    def body(x_ref, out_ref, *rest):
        if with_sum:
            sum_ref, send_sems, recv_sems, local_sem = rest
        else:
            send_sems, recv_sems, local_sem = rest
        x, y, c = _coords()
        me, sibling = (x, y, c), (x, y, 1 - c)
        chips = [(1 - x, y), (x, 1 - y), (1 - x, 1 - y)]

        def rows(px, py, pc):
            return out_ref.at[pl.ds((4 * px + 2 * py + pc) * m_per, m_per), :]

        def copy(k, block, to, src=None):
            return pltpu.make_async_remote_copy(
                src_ref=rows(*block) if src is None else src, dst_ref=rows(*block),
                send_sem=send_sems.at[k], recv_sem=recv_sems.at[k], device_id=to, device_id_type=MESH)

        mine = pltpu.make_async_copy(x_ref, rows(*me), local_sem)
        mine.start()
        first = [copy(0, me, sibling, src=x_ref)]
        first += [copy(1 + j, me, (*chip, c), src=x_ref) for j, chip in enumerate(chips)]
        for cp in first:
            cp.start()
        passed = [copy(4 + j, (*chip, c), sibling) for j, chip in enumerate(chips)]
        for j, chip in enumerate(chips):
            copy(1 + j, (*chip, c), me).wait_recv()
            passed[j].start()
        copy(0, sibling, me).wait_recv()
        for j, chip in enumerate(chips):
            copy(4 + j, (*chip, 1 - c), me).wait_recv()
        for cp in first + passed:
            cp.wait_send()
        mine.wait()
        if with_sum:
            acc = out_ref[0:m_per, :]
            for d in range(1, N_DEV):
                acc = acc + out_ref[d * m_per:(d + 1) * m_per, :]
            sum_ref[...] = acc

    vm = pl.BlockSpec(memory_space=pltpu.VMEM)
    out_shape = [jax.ShapeDtypeStruct((N_DEV * m_per, n), F32)]
    if with_sum:
        out_shape.append(jax.ShapeDtypeStruct((m_per, n), F32))
    return pl.pallas_call(
        body, name=name, out_shape=out_shape, in_specs=[vm], out_specs=[vm] * len(out_shape),
        scratch_shapes=[pltpu.SemaphoreType.DMA((7,)), pltpu.SemaphoreType.DMA((7,)), pltpu.SemaphoreType.DMA],
        compiler_params=pltpu.CompilerParams(vmem_limit_bytes=VMEM_LIMIT),
    )(blk)


HBM_SPEC = pl.BlockSpec(memory_space=pltpu.HBM)
SEM_SPEC = pl.BlockSpec(memory_space=pltpu.SEMAPHORE)
SIDE_EFFECT = pltpu.SideEffectType.DATAFLOW_SIDE_EFFECTING


def _peer(x, y, c, q, cb):
    return (1 - x if q & 2 else x, 1 - y if q & 1 else y, 1 - c if cb else c)


def _w_in_piece(slots):
    def piece(part_ref, k, to):
        return part_ref.at[pl.ds(to[2] * (D // 2), D // 2), pl.ds(slots[k] * W_IN_SHARD, W_IN_SHARD)]
    return piece


def _w_out_piece(part_ref, k, to):
    ho = W_OUT_SHARD // 2
    return part_ref.at[pl.ds((2 * to[0] + to[1]) * W_OUT_SHARD + to[2] * ho, ho), :]


def _group_piece(part_ref, k, to):
    return part_ref.at[4 * to[0] + 2 * to[1] + to[2]]


def _whole_piece(part_ref, k, to):
    return part_ref


def _exchange_start_call(part, rels, piece, slot_shape, name):
    n = len(rels)
    land = lax.empty((n,) + slot_shape, part.dtype)

    def body(part_ref, land_ref, send_sems, recv_sems, part_thru, land_thru, token):
        x, y, c = _coords()
        for k, (q, cb) in enumerate(rels):
            to = _peer(x, y, c, q, cb)
            pltpu.make_async_remote_copy(src_ref=piece(part_ref, k, to), dst_ref=land_ref.at[k], send_sem=send_sems.at[k],
                                         recv_sem=recv_sems.at[k], device_id=to, device_id_type=MESH).start()
        token[...] = jnp.zeros_like(token)

    return pl.pallas_call(
        body, name=name,
        out_shape=(pltpu.SemaphoreType.DMA((n,)), pltpu.SemaphoreType.DMA((n,)), pltpu.HBM(part.shape, part.dtype),
                   pltpu.HBM(land.shape, land.dtype), jax.ShapeDtypeStruct((1, 1), F32)),
        in_specs=(HBM_SPEC, HBM_SPEC), out_specs=(SEM_SPEC, SEM_SPEC, HBM_SPEC, HBM_SPEC, pl.BlockSpec(memory_space=pltpu.VMEM)),
        input_output_aliases={0: 2, 1: 3},
        compiler_params=pltpu.CompilerParams(has_side_effects=SIDE_EFFECT),
    )(pltpu.with_memory_space_constraint(part, pltpu.HBM), pltpu.with_memory_space_constraint(land, pltpu.HBM))


def _exchange_wait_call(started, rels, piece, after, name):
    send_sems, recv_sems, part_thru, land_thru, _ = started

    def body(part_ref, land_ref, send_sems, recv_sems, after_ref, part_out, land_out):
        x, y, c = _coords()
        for k, (q, cb) in enumerate(rels):
            to = _peer(x, y, c, q, cb)
            cp = pltpu.make_async_remote_copy(src_ref=piece(part_ref, k, to), dst_ref=land_ref.at[k], send_sem=send_sems.at[k],
                                              recv_sem=recv_sems.at[k], device_id=to, device_id_type=MESH)
            cp.wait_send()
            cp.wait_recv()

    return pl.pallas_call(
        body, name=name,
        out_shape=(pltpu.HBM(part_thru.shape, part_thru.dtype), pltpu.HBM(land_thru.shape, land_thru.dtype)),
        in_specs=(HBM_SPEC, HBM_SPEC, SEM_SPEC, SEM_SPEC, pl.BlockSpec(memory_space=pl.ANY)), out_specs=(HBM_SPEC, HBM_SPEC),
        input_output_aliases={0: 0, 1: 1},
        compiler_params=pltpu.CompilerParams(has_side_effects=SIDE_EFFECT),
    )(part_thru, land_thru, send_sems, recv_sems, after)


def _rope_tables(s):
    inv_freq = 10000.0 ** (-jnp.arange(0, HEAD, 2, dtype=F32) / HEAD)
    ang = jnp.arange(s, dtype=F32)[:, None] * inv_freq[None, :]
    cos = jnp.tile(jnp.cos(ang), (1, LANE // (HEAD // 2)))
    sin = jnp.tile(jnp.sin(ang), (1, LANE // (HEAD // 2)))
    first_half = (jnp.arange(LANE) % HEAD) < (HEAD // 2)
    return cos, jnp.where(first_half[None, :], -sin, sin)


def kernel(x, c, w_ada, b_ada, norm_g, w_in, ln_v_g, ln_v_b, w_spatial, b_spatial, sinks, w_out, w_ada_final, b_ada_final, final_norm_g, loss_target, m_w_ada, m_b_ada, m_norm_g, m_w_in, m_ln_v_g, m_ln_v_b, m_w_spatial, m_b_spatial, m_sinks, m_w_out, m_w_ada_final, m_b_ada_final, m_final_norm_g, v_w_ada, v_b_ada, v_norm_g, v_w_in, v_ln_v_g, v_ln_v_b, v_w_spatial, v_b_spatial, v_sinks, v_w_out, v_w_ada_final, v_b_ada_final, v_final_norm_g):
    s = x.shape[1]
    ax, ay, ac = _coords()
    chip = 2 * ax + ay
    me = 4 * ax + 2 * ay + ac
    n_ada = w_ada.shape[2]
    n_adaf = w_ada_final.shape[1]

    x2d = x.reshape(s, D)
    tgt = loss_target.reshape(s, D)
    w_ada2, w_in2, w_out2 = w_ada[0], w_in[0], w_out[0]
    b_ada_f2 = b_ada_final.reshape(1, 2 * D)
    gf = final_norm_g.reshape(1, D)

    c_all = _allgather_sum_call(jnp.pad(c, ((0, 7), (0, 0))), "gather_c", False)[0][::8]
    mod_p, c_act = _rowmat_call(c_all, w_ada2, lax.dynamic_slice(b_ada, (0, chip * n_ada), (1, n_ada)), "mod")
    modf_p, _ = _rowmat_call(c_all, w_ada_final, lax.dynamic_slice(b_ada_f2, (0, chip * n_adaf), (1, n_adaf)), "mod_final")
    mods = _allgather_sum_call(jnp.concatenate([mod_p, modf_p], axis=1), "gather_mod", False)[0]
    my_rows = [lax.dynamic_slice(mods, (16 * j + me, 0), (1, n_ada + n_adaf)) for j in range(N_CHIP)]
    mod = jnp.concatenate([r[:, :n_ada] for r in my_rows], axis=1)
    mod_f = jnp.concatenate([r[:, n_ada:] for r in my_rows], axis=1)
    shift, scale, gate = mod[:, :D], mod[:, D:2 * D], mod[:, 2 * D:]
    shift_f, scale_f = mod_f[:, :D], mod_f[:, D:]

    pos = jnp.stack([chip, ac]).astype(jnp.int32)
    w_in_own = _cast_into_call(pos, w_in2, (D, D_IN), "cast_w_in")
    w_out_own = _cast_into_call(pos, w_out2, (D, D), "cast_w_out")

    cos, sin = _rope_tables(s)
    b_sp_t = b_spatial[0].T
    sinks1 = sinks.reshape(N_Q)
    h, proj, w_in_bf, w_out_bf = _proj_gather_call(pos, x2d, shift, scale, norm_g, w_in_own, w_out_own)
    y = _mix_fwd_call(proj, cos, sin, ln_v_g, ln_v_b, w_spatial[0], b_sp_t, sinks1)
    dx2, do, dy, st_tail = _tail_call(y, w_out_bf, x2d, tgt, gate, shift_f, scale_f, gf)

    rel_o = [(0, 1), (1, 0), (1, 1), (2, 0), (2, 1), (3, 0), (3, 1)]
    rel_a = [(1, 0), (1, 1), (2, 0), (2, 1)]
    rel_b = [(3, 0), (3, 1), (0, 1)]
    piece_a, piece_b = _w_in_piece([0, 0, 1, 1]), _w_in_piece([0, 0, 1])
    half_in, half_out = (D // 2, W_IN_SHARD), (W_OUT_SHARD // 2, D)

    g_w_out_p = _tn_call(y, do, "grad_w_out")
    st_o = _exchange_start_call(g_w_out_p, rel_o, _w_out_piece, half_out, "send_w_out")
    dproj, st_ln, d_wsp, d_bsp_t, d_sink = _mix_bwd_call(
        proj, dy, cos, sin, ln_v_g + st_o[4], ln_v_b, w_spatial[0], jnp.swapaxes(w_spatial[0], 1, 2), b_sp_t, sinks1)
    g_w_in_a = _tn_shards_call(pos, h, dproj, (1, 2), "grad_w_in_a")
    st_a = _exchange_start_call(g_w_in_a, rel_a, piece_a, half_in, "send_w_in_a")
    g_w_in_b = _tn_shards_call(pos, h, dproj, (3, 0), "grad_w_in_b")
    st_b = _exchange_start_call(g_w_in_b, rel_b, piece_b, half_in, "send_w_in_b")
    rel_all = rel_o
    st_s = _exchange_start_call(d_wsp, rel_all, _group_piece, (BLK, BLK), "send_w_spatial")
    sent = st_a[4] + st_b[4] + st_s[4]
    grad_x, st_dh = _dh_call(dproj, w_in_bf, x2d, dx2, scale + sent, norm_g)

    g_w_out_p, recv_o = _exchange_wait_call(st_o, rel_o, _w_out_piece, st_dh, "wait_w_out")
    _, recv_a = _exchange_wait_call(st_a, rel_a, piece_a, st_dh, "wait_w_in_a")
    g_w_in_b, recv_b = _exchange_wait_call(st_b, rel_b, piece_b, st_dh, "wait_w_in_b")
    d_wsp, recv_s = _exchange_wait_call(st_s, rel_all, _group_piece, st_dh, "wait_w_spatial")
    mine_in = _sum_pieces_call(pos, g_w_in_b, lambda i, p, nrb: (p[1] * nrb + i, 1), [recv_a, recv_b], "sum_w_in")
    mine_out = _sum_pieces_call(pos, g_w_out_p, lambda i, p, nrb: ((2 * p[0] + p[1]) * nrb + i, 0), [recv_o], "sum_w_out")
    wsp_group = _sum_pieces_call(pos, d_wsp.reshape(GROUPS * BLK, BLK), lambda i, p, nrb: (2 * p[0] + p[1], 0), [recv_s],
                                 "sum_w_spatial")
    to_sibling = [(0, 1)]
    st_pi = _exchange_start_call(mine_in, to_sibling, _whole_piece, half_in, "swap_w_in")
    st_po = _exchange_start_call(mine_out, to_sibling, _whole_piece, half_out, "swap_w_out")

    misc = jnp.concatenate([st_ln, d_bsp_t[:, :GROUPS].T, d_sink, jnp.zeros((8, D - D_A - 2 * LANE), F32)], axis=1)
    pack = jnp.concatenate([wsp_group.reshape(8, D) + (st_pi[4] + st_po[4]), st_tail, st_dh, misc], axis=0)
    rows = pack.shape[0]
    packs, tot = _allgather_sum_call(pack, "gather_small", True)
    packs = packs.reshape(N_DEV, rows, D)
    dmod_all = jnp.concatenate([packs[:, 16, :], packs[:, 17, :], packs[:, 11, :]], axis=1)
    dmodf_all = jnp.concatenate([packs[:, 8, :], packs[:, 9, :]], axis=1)
    loss = tot[13, 0]
    mine_in, theirs_in = _exchange_wait_call(st_pi, to_sibling, _whole_piece, tot, "swapped_w_in")
    mine_out, theirs_out = _exchange_wait_call(st_po, to_sibling, _whole_piece, tot, "swapped_w_out")
    small = {
        "b_ada": jnp.concatenate([tot[16:17], tot[17:18], tot[11:12]], axis=1),
        "norm_g": tot[18:19],
        "ln_v_g": tot[24:25, :D_A],
        "ln_v_b": tot[25:26, :D_A],
        "w_spatial": packs[:, 0:8, :].reshape(GROUPS * BLK, BLK),
        "b_spatial": tot[24:32, D_A:D_A + BLK],
        "sinks": tot[24:25, D_A + LANE:D_A + LANE + N_Q],
        "b_ada_final": jnp.concatenate([tot[8:9], tot[9:10]], axis=1),
        "final_norm_g": tot[10:11],
    }

    weights = dict(w_ada=w_ada, b_ada=b_ada, norm_g=norm_g, w_in=w_in, ln_v_g=ln_v_g, ln_v_b=ln_v_b, w_spatial=w_spatial,
                   b_spatial=b_spatial, sinks=sinks, w_out=w_out, w_ada_final=w_ada_final, b_ada_final=b_ada_final,
                   final_norm_g=final_norm_g)
    m_in = dict(w_ada=m_w_ada, b_ada=m_b_ada, norm_g=m_norm_g, w_in=m_w_in, ln_v_g=m_ln_v_g, ln_v_b=m_ln_v_b,
                w_spatial=m_w_spatial, b_spatial=m_b_spatial, sinks=m_sinks, w_out=m_w_out, w_ada_final=m_w_ada_final,
                b_ada_final=m_b_ada_final, final_norm_g=m_final_norm_g)
    v_in = dict(w_ada=v_w_ada, b_ada=v_b_ada, norm_g=v_norm_g, w_in=v_w_in, ln_v_g=v_ln_v_g, ln_v_b=v_ln_v_b,
                w_spatial=v_w_spatial, b_spatial=v_b_spatial, sinks=v_sinks, w_out=v_w_out, w_ada_final=v_w_ada_final,
                b_ada_final=v_b_ada_final, final_norm_g=v_final_norm_g)
    c_act_t = c_act.T
    outer = {"w_ada": lax.dynamic_slice(dmod_all, (0, chip * n_ada), (N_DEV, n_ada)),
             "w_ada_final": lax.dynamic_slice(dmodf_all, (0, chip * n_adaf), (N_DEV, n_adaf))}
    halves = {"w_in": (mine_in, theirs_in[0]), "w_out": (mine_out, theirs_out[0])}
    done = {}
    for name, (mine, theirs) in halves.items():
        shape2 = (2 * mine.shape[0], mine.shape[1])
        done[name] = _adam_halves_call(pos, weights[name].reshape(shape2), mine, theirs, m_in[name].reshape(shape2),
                                       v_in[name].reshape(shape2), "adam_" + name)
    for name, dm in outer.items():
        shape2 = (D, dm.shape[1])
        done[name] = _adam_outer_call(weights[name].reshape(shape2), c_act_t, dm, m_in[name].reshape(shape2),
                                      v_in[name].reshape(shape2), "adam_" + name)
    updates = _adam_small_call([(weights[name].reshape(g.shape), g, m_in[name].reshape(g.shape), v_in[name].reshape(g.shape))
                                for name, g in small.items()])
    for (name, g), upd in zip(small.items(), updates):
        done[name] = (g, *upd)
    outs = [[done[name][k].reshape(w.shape) for name, w in weights.items()] for k in range(4)]
    return (loss, grad_x.reshape(x.shape), *outs[0], *outs[1], *outs[2], *outs[3])
```

```python
import jax
import jax.numpy as jnp
from jax import lax
from jax.experimental import pallas as pl
from jax.experimental.pallas import tpu as pltpu

F32 = jnp.float32
BF16 = jnp.bfloat16
MESH = pl.DeviceIdType.MESH

D = 2048
D_A = 1024
D_B = 1024
D_KV = 256
HEAD = 64
N_Q = 16
N_KV = 4
Q_PER_KV = N_Q // N_KV
BLK = 128
GROUPS = 8
D_IN = 5632
OFF_Q, OFF_K, OFF_V, OFF_ZB = 3072, 4096, 4352, 4608
N_CHIP = 4
N_DEV = 8
W_IN_SHARD = D_IN // N_CHIP
W_OUT_SHARD = D // N_CHIP
EPS = 1e-5
SCALE = HEAD ** -0.5
NEG = -1e30
LANE = 128
VMEM_LIMIT = 56 * 1024 * 1024

ADAM_LR, ADAM_B1, ADAM_B2, ADAM_EPS, ADAM_WD, ADAM_STEP = 0.001, 0.9, 0.999, 1e-08, 0.01, 10
ADAM_C1 = 1.0 - ADAM_B1 ** ADAM_STEP
ADAM_C2 = 1.0 - ADAM_B2 ** ADAM_STEP
ADAM_ROWS = 256

NT = (((1,), (1,)), ((), ()))
TN = (((0,), (0,)), ((), ()))


def _params(*sem):
    return pltpu.CompilerParams(dimension_semantics=sem, vmem_limit_bytes=VMEM_LIMIT)


def _silu_parts(z):
    sig = 1.0 / (1.0 + jnp.exp(-z))
    return z * sig, sig


def _swap_halves(v, first_half):
    return jnp.where(first_half, pltpu.roll(v, 96, 1), pltpu.roll(v, 32, 1))


def _rope(v, cos_t, sin_s, first_half):
    return v * cos_t + _swap_halves(v, first_half) * sin_s


def _unrope(dv, cos_t, sin_s, first_half):
    return dv * cos_t - _swap_halves(dv, first_half) * sin_s


def _lane_masks():
    lane = lax.broadcasted_iota(jnp.int32, (BLK, LANE), 1)
    return (lane % HEAD) < (HEAD // 2), lane < HEAD


def _band_valid(first_block_bound, rows=BLK):
    rr = lax.broadcasted_iota(jnp.int32, (rows, 2 * BLK), 0) & (BLK - 1)
    jj = lax.broadcasted_iota(jnp.int32, (rows, 2 * BLK), 1)
    return (jj > rr) & (jj <= rr + BLK) & (jj >= first_block_bound)


def _dup_kv(slab, lo):
    rolled = pltpu.roll(slab, HEAD, 1)
    return jnp.where(lo, slab, rolled).astype(BF16), jnp.where(lo, rolled, slab).astype(BF16)


def _stack_heads(ref, sb, slab, lo, dtype):
    kh, base = sb // 2, 2 * (sb % 2) * BLK
    zero = jnp.zeros_like(slab)
    ref[kh, base:base + BLK, :] = jnp.where(lo, slab, zero).astype(dtype)
    ref[kh, base + BLK:base + 2 * BLK, :] = jnp.where(lo, zero, slab).astype(dtype)


def _unstack_heads(ref, sb, lo):
    kh, base = sb // 2, 2 * (sb % 2) * BLK
    return jnp.where(lo, ref[kh, base:base + BLK, :], ref[kh, base + BLK:base + 2 * BLK, :])


def _sink_column(sinks_ref, kh):
    row = lax.broadcasted_iota(jnp.int32, (Q_PER_KV * BLK, 1), 0)
    col = jnp.full(row.shape, sinks_ref[Q_PER_KV * kh + Q_PER_KV - 1], F32)
    for n in range(Q_PER_KV - 2, -1, -1):
        col = jnp.where(row < (n + 1) * BLK, sinks_ref[Q_PER_KV * kh + n], col)
    return col


def _tril():
    t = lax.broadcasted_iota(jnp.int32, (BLK, BLK), 0)
    s = lax.broadcasted_iota(jnp.int32, (BLK, BLK), 1)
    return s <= t


def _layer_norm_fwd(va, lg, lb):
    mu = jnp.mean(va, axis=-1, keepdims=True)
    xc = va - mu
    rstd = lax.rsqrt(jnp.mean(xc * xc, axis=-1, keepdims=True) + EPS)
    vhat = xc * rstd
    return vhat, rstd, vhat * lg + lb


def _softmax_sink(qm, kdup, bias, sink):
    s = lax.dot_general(qm, kdup, NT, preferred_element_type=F32) + bias
    m = jnp.maximum(jnp.max(s, axis=-1, keepdims=True), sink)
    p = jnp.exp(s - m)
    esink = jnp.exp(sink - m)
    inv = 1.0 / (jnp.sum(p, axis=-1, keepdims=True) + esink)
    return p * inv, esink * inv


def _band_bias(bias_ref):
    rows = bias_ref.shape[1]
    bias_ref[0] = jnp.where(_band_valid(BLK, rows), 0.0, NEG)
    bias_ref[1] = jnp.where(_band_valid(0, rows), 0.0, NEG)


def _rowmat_call(c_all, w, b, name):
    n = w.shape[1]
    tn = 512

    def body(c_ref, w_ref, b_ref, o_ref, ca_ref):
        ca, _ = _silu_parts(c_ref[...])
        ca_ref[...] = ca
        o_ref[...] = jnp.dot(ca.astype(BF16), w_ref[...].astype(BF16), preferred_element_type=F32) + b_ref[...]

    return pl.pallas_call(
        body, name=name, grid=(n // tn,),
        in_specs=[pl.BlockSpec((N_DEV, D), lambda j: (0, 0)), pl.BlockSpec((D, tn), lambda j: (0, j)),
                  pl.BlockSpec((1, tn), lambda j: (0, j))],
        out_specs=[pl.BlockSpec((N_DEV, tn), lambda j: (0, j)), pl.BlockSpec((N_DEV, D), lambda j: (0, 0))],
        out_shape=[jax.ShapeDtypeStruct((N_DEV, n), F32), jax.ShapeDtypeStruct((N_DEV, D), F32)],
        compiler_params=_params("arbitrary"),
    )(c_all, w, b)


def _cast_into_call(pos, w, full_shape, name):
    r, n = w.shape
    tr = min(r, 512)
    by_cols = full_shape[0] == r
    nrb = r // tr

    def body(pos_ref, w_ref, o_ref):
        o_ref[...] = w_ref[...].astype(BF16)

    out_map = (lambda i, pos: (i, pos[0])) if by_cols else (lambda i, pos: (pos[0] * nrb + i, 0))
    return pl.pallas_call(
        body, name=name,
        grid_spec=pltpu.PrefetchScalarGridSpec(
            num_scalar_prefetch=1, grid=(nrb,),
            in_specs=[pl.BlockSpec((tr, n), lambda i, pos: (i, 0))], out_specs=pl.BlockSpec((tr, n), out_map)),
        out_shape=jax.ShapeDtypeStruct(full_shape, BF16), compiler_params=_params("parallel"),
    )(pos, w)


W_IN_PARTS = ((0, 768), (768, 640))
OUT_STREAMS = 4
X_STREAMS = 4


def _proj_gather_call(pos, x, shift, scale, norm_g, wi_full, wo_full):
    s = x.shape[0]
    tm = min(s, 512)
    nrow = s // tm
    hi = D // 2
    ho = W_OUT_SHARD // 2
    phases = [(0, None), (1, 0), (2, 0), (1, 1), (2, 1), (3, 0), (3, 1)]

    def body(pos_ref, *refs):
        x_refs = refs[:X_STREAMS]
        (sh_ref, sc_ref, g_ref, _, _, h_ref, proj_ref, fi_ref, fo_ref,
         h_all, wbuf, obuf, send_sems, recv_sems, load_sems, out_sems) = refs[X_STREAMS:]
        p = pl.program_id(0)
        i = pl.program_id(1)
        x_, y_, c_ = _coords()
        me, sibling = (x_, y_, c_), (x_, y_, 1 - c_)

        def shard_of(q):
            px, py, _ = _peer(x_, y_, c_, q, 0)
            return 2 * px + py

        def cols_of(q, cp):
            off, w = (0, W_IN_SHARD) if cp is None else W_IN_PARTS[cp]
            return shard_of(q) * W_IN_SHARD + off, w

        def part(which, q, pc, sub, cp):
            n = hi if which == 0 else ho
            base = pc * n
            if sub is not None:
                n //= 2
                base = base + sub * n
            if which == 0:
                c0, w = cols_of(q, cp)
                return fi_ref.at[pl.ds(base, n), pl.ds(c0, w)]
            return fo_ref.at[pl.ds(shard_of(q) * W_OUT_SHARD + base, n), :]

        def copy(k, ref, to):
            return pltpu.make_async_remote_copy(src_ref=ref, dst_ref=ref, send_sem=send_sems.at[k], recv_sem=recv_sems.at[k],
                                                device_id=to, device_id_type=MESH)

        def sem(which, kind, j, cp):
            return 4 * kind + 2 * cp + j if which == 0 else 16 + 2 * kind + j

        def to_neighbour(which, q, cp=None):
            return copy(sem(which, 0, q - 1, cp), part(which, 0, c_, None, cp), _peer(x_, y_, c_, q, 0))

        def from_neighbour(which, q, cp=None):
            return copy(sem(which, 0, q - 1, cp), part(which, q, c_, None, cp), me)

        def relay(which, q, cp=None):
            return copy(sem(which, 1, q - 1, cp), part(which, q, c_, q - 1, cp), _peer(x_, y_, c_, 3 - q, 0))

        def relayed(which, sub, cp=None):
            return copy(sem(which, 1, sub, cp), part(which, 3, c_, sub, cp), me)

        def to_sibling(which, q, cp=None):
            return copy(sem(which, 2, q - 1, cp), part(which, q, c_, None, cp), sibling)

        def from_sibling(which, q, cp=None):
            return copy(sem(which, 2, q - 1, cp), part(which, q, 1 - c_, None, cp), me)

        def relayed_to_sibling(which, sub, cp=None):
            return copy(sem(which, 3, sub, cp), part(which, 3, c_, sub, cp), sibling)

        def relayed_from_sibling(which, sub, cp=None):
            return copy(sem(which, 3, sub, cp), part(which, 3, 1 - c_, sub, cp), me)

        def pass_on_neighbours(which, cp=None):
            for q in (1, 2):
                from_neighbour(which, q, cp).wait_recv()
                to_sibling(which, q, cp).start()
                relay(which, q, cp).start()

        def pass_on_relayed(which, cp=None):
            for sub in range(2):
                relayed(which, sub, cp).wait_recv()
                relayed_to_sibling(which, sub, cp).start()

        def shard_load(k):
            c0, w = cols_of(*phases[k])
            return pltpu.make_async_copy(fi_ref.at[:, pl.ds(c0, w)], wbuf.at[k % 2, :, 0:w], load_sems.at[k % 2])

        class OutCopies:
            def __init__(self, k, slot, row0):
                c0, w = cols_of(*phases[k])
                strip = tm // OUT_STREAMS
                self.copies = [pltpu.make_async_copy(obuf.at[slot, n * strip:(n + 1) * strip, 0:w],
                                                     proj_ref.at[pl.ds(row0 + n * strip, strip), pl.ds(c0, w)],
                                                     out_sems.at[slot, n]) for n in range(OUT_STREAMS)]

            def start(self):
                for cp in self.copies:
                    cp.start()

            def wait(self):
                for cp in self.copies:
                    cp.wait()

        out_copy = OutCopies

        def drain(k):
            for j in range(min(2, nrow)):
                out_copy(k, (nrow - 1 - j) % 2, 0).wait()

        def arrivals(k):
            q, cp = phases[k]
            if k == 0:
                for cp_ in range(2):
                    for q_ in (1, 2):
                        to_neighbour(0, q_, cp_).start()
            elif q < 3 and k in (1, 3):
                pass_on_neighbours(0, cp)
                if k == 1:
                    for q_ in (1, 2):
                        to_neighbour(1, q_).start()
            elif k == 5:
                for cp_ in range(2):
                    pass_on_relayed(0, cp_)
                pass_on_neighbours(1)
            if q in (1, 2):
                from_sibling(0, q, cp).wait_recv()
            elif q == 3:
                for sub in range(2):
                    relayed_from_sibling(0, sub, cp).wait_recv()

        rows = pl.ds(pl.multiple_of(i * tm, tm), tm)
        slot = i % 2
        for k, (q, cp) in enumerate(phases):
            @pl.when(p == k)
            def _(k=k, q=q, cp=cp):
                @pl.when(i == 0)
                def _():
                    if k == 0:
                        arrivals(0)
                        shard_load(0).start()
                    else:
                        drain(k - 1)
                    shard_load(k).wait()

                if k + 1 < len(phases):
                    @pl.when(i == max(nrow - 2, 0))
                    def _():
                        arrivals(k + 1)
                        shard_load(k + 1).start()

                if k == 0:
                    wx = D // X_STREAMS
                    ssq = sum(jnp.sum(xr[...] * xr[...], axis=-1, keepdims=True) for xr in x_refs)
                    r = lax.rsqrt(ssq * (1.0 / D) + EPS)
                    for n, xr in enumerate(x_refs):
                        cols = slice(n * wx, (n + 1) * wx)
                        hv = ((xr[...] * r * g_ref[:, cols]) * (1.0 + sc_ref[:, cols]) + sh_ref[:, cols]).astype(BF16)
                        h_ref[:, cols] = hv
                        h_all[rows, cols] = hv

                @pl.when(i >= 2)
                def _():
                    out_copy(k, slot, 0).wait()

                w = cols_of(q, cp)[1]
                obuf[slot, :, 0:w] = jnp.dot(h_all[rows, :], wbuf[k % 2, :, 0:w], preferred_element_type=F32)
                out_copy(k, slot, pl.multiple_of(i * tm, tm)).start()

        @pl.when((p == len(phases) - 1) & (i == nrow - 1))
        def _():
            drain(len(phases) - 1)
            pass_on_relayed(1)
            for q in (1, 2):
                from_sibling(1, q).wait_recv()
            for sub in range(2):
                relayed_from_sibling(1, sub).wait_recv()
            for which, cps in ((0, (0, 1)), (1, (None,))):
                for cp in cps:
                    for q in (1, 2):
                        to_neighbour(which, q, cp).wait_send()
                        relay(which, q, cp).wait_send()
                        to_sibling(which, q, cp).wait_send()
                        relayed_to_sibling(which, q - 1, cp).wait_send()

    vec = pl.BlockSpec((1, D), lambda p, i, pos: (0, 0))
    first_phase_rows = lambda p, i, pos: (jnp.where(p == 0, i, nrow - 1), 0)
    anyspec = pl.BlockSpec(memory_space=pl.ANY)
    x_spec = lambda n: pl.BlockSpec((tm, D // X_STREAMS), lambda p, i, pos: (jnp.where(p == 0, i, nrow - 1), n))
    return pl.pallas_call(
        body, name="proj_gather",
        grid_spec=pltpu.PrefetchScalarGridSpec(
            num_scalar_prefetch=1, grid=(len(phases), nrow),
            in_specs=[x_spec(n) for n in range(X_STREAMS)] + [vec, vec, vec, anyspec, anyspec],
            out_specs=[pl.BlockSpec((tm, D), first_phase_rows), anyspec, anyspec, anyspec],
            scratch_shapes=[pltpu.VMEM((s, D), BF16), pltpu.VMEM((2, D, W_IN_SHARD), BF16), pltpu.VMEM((2, tm, W_IN_SHARD), F32),
                            pltpu.SemaphoreType.DMA((24,)), pltpu.SemaphoreType.DMA((24,)), pltpu.SemaphoreType.DMA((2,)),
                            pltpu.SemaphoreType.DMA((2, OUT_STREAMS))]),
        out_shape=[jax.ShapeDtypeStruct((s, D), BF16), jax.ShapeDtypeStruct((s, D_IN), F32),
                   jax.ShapeDtypeStruct((D, D_IN), BF16), jax.ShapeDtypeStruct((D, D), BF16)],
        input_output_aliases={X_STREAMS + 4: 2, X_STREAMS + 5: 3},
        compiler_params=_params("arbitrary", "arbitrary"),
    )(pos, *([x] * X_STREAMS), shift, scale, norm_g, wi_full, wo_full)


def _proj_specs(rev_nb=None):
    if rev_nb is None:
        row = lambda i: i
    else:
        row = lambda i: rev_nb - 1 - i
    wide = lambda col: pl.BlockSpec((BLK, D_A), lambda i: (row(i), col))
    kv = lambda col: pl.BlockSpec((BLK, D_KV), lambda i: (row(i), col))
    half = lambda col: pl.BlockSpec((BLK, 512), lambda i: (row(i), col))
    return [wide(0), wide(1), wide(2), wide(3), kv(OFF_K // D_KV), kv(OFF_V // D_KV), half(OFF_ZB // 512), half(OFF_ZB // 512 + 1)]


def _mix_fwd_call(proj, cos, sin, ln_g, ln_b, w_sp, b_sp_t, sinks):
    s = proj.shape[0]
    nb = s // BLK

    def body(ua_ref, va_ref, za_ref, q_ref, k_ref, v_ref, zb0_ref, zb1_ref, cos_ref, sin_ref, lg_ref, lb_ref,
             w_ref, bt_ref, sinks_ref, y_ref, kdup_ref, vdup_ref, qm_ref, ost_ref, bias_ref):
        i = pl.program_id(0)
        first_half, lo = _lane_masks()
        cos_t = cos_ref[...]
        sin_t = sin_ref[...]

        _, _, vln = _layer_norm_fwd(va_ref[...], lg_ref[...], lb_ref[...])
        tril = _tril()
        for g in range(GROUPS):
            cols = slice(g * BLK, (g + 1) * BLK)
            wg = jnp.where(tril, w_ref[g], 0.0).astype(BF16)
            sg = jnp.dot(wg, vln[:, cols].astype(BF16), preferred_element_type=F32) + bt_ref[:, g:g + 1]
            gate, _ = _silu_parts(za_ref[:, cols])
            y_ref[:, cols] = (ua_ref[:, cols] * sg * gate).astype(BF16)

        @pl.when(i == 0)
        def _():
            kdup_ref[:, 0:BLK, :] = jnp.zeros((N_KV, BLK, LANE), BF16)
            vdup_ref[:, 0:BLK, :] = jnp.zeros((N_KV, BLK, LANE), BF16)
            _band_bias(bias_ref)

        @pl.when(i > 0)
        def _():
            kdup_ref[:, 0:BLK, :] = kdup_ref[:, BLK:2 * BLK, :]
            vdup_ref[:, 0:BLK, :] = vdup_ref[:, BLK:2 * BLK, :]

        for ks in range(2):
            cols = slice(ks * LANE, (ks + 1) * LANE)
            kr = _rope(k_ref[:, cols], cos_t, sin_t, first_half)
            for n, (kd, vd) in enumerate(zip(_dup_kv(kr, lo), _dup_kv(v_ref[:, cols], lo))):
                kdup_ref[2 * ks + n, BLK:2 * BLK, :] = kd
                vdup_ref[2 * ks + n, BLK:2 * BLK, :] = vd
        for sb in range(8):
            _stack_heads(qm_ref, sb, _rope(q_ref[:, sb * LANE:(sb + 1) * LANE], cos_t, sin_t, first_half) * SCALE, lo, BF16)

        block_kind = jnp.where(i > 0, 1, 0)

        def kv_head(kh, carry):
            probs, _ = _softmax_sink(qm_ref[kh], kdup_ref[kh], bias_ref[block_kind], _sink_column(sinks_ref, kh))
            ost_ref[kh] = jnp.dot(probs.astype(BF16), vdup_ref[kh], preferred_element_type=F32)
            return carry

        lax.fori_loop(0, N_KV, kv_head, 0, unroll=2)
        for sb in range(8):
            cols = slice(sb * LANE, (sb + 1) * LANE)
            zb = zb0_ref[:, cols] if sb < 4 else zb1_ref[:, (sb - 4) * LANE:(sb - 3) * LANE]
            gate, _ = _silu_parts(zb)
            y_ref[:, D_A + sb * LANE:D_A + (sb + 1) * LANE] = (_unstack_heads(ost_ref, sb, lo) * gate).astype(BF16)

    tab = pl.BlockSpec((BLK, LANE), lambda i: (i, 0))
    return pl.pallas_call(
        body, name="mix_fwd", grid=(nb,),
        in_specs=_proj_specs() + [
            tab, tab, pl.BlockSpec((1, D_A), lambda i: (0, 0)), pl.BlockSpec((1, D_A), lambda i: (0, 0)),
            pl.BlockSpec((GROUPS, BLK, BLK), lambda i: (0, 0, 0)), pl.BlockSpec((BLK, GROUPS), lambda i: (0, 0)),
            pl.BlockSpec(memory_space=pltpu.SMEM)],
        out_specs=pl.BlockSpec((BLK, 2 * D_A), lambda i: (i, 0)),
        out_shape=jax.ShapeDtypeStruct((s, 2 * D_A), BF16),
        scratch_shapes=[pltpu.VMEM((N_KV, 2 * BLK, LANE), BF16), pltpu.VMEM((N_KV, 2 * BLK, LANE), BF16),
                        pltpu.VMEM((N_KV, Q_PER_KV * BLK, LANE), BF16), pltpu.VMEM((N_KV, Q_PER_KV * BLK, LANE), F32),
                        pltpu.VMEM((2, Q_PER_KV * BLK, 2 * BLK), F32)],
        compiler_params=_params("arbitrary"),
    )(proj, proj, proj, proj, proj, proj, proj, proj, cos, sin, ln_g, ln_b, w_sp, b_sp_t, sinks)


def _tail_call(y, w_out_bf, x, target, gate, shift_f, scale_f, gf):
    s = x.shape[0]
    tm = min(s, 256)
    nsteps = s // tm

    def body(y_ref, w_ref, x_ref, t_ref, gate_ref, shf_ref, scf_ref, gf_ref, dx2_ref, do_ref, dy_ref, st_ref):
        i = pl.program_id(0)

        @pl.when(i == 0)
        def _():
            st_ref[...] = jnp.zeros((8, D), F32)

        o = jnp.dot(y_ref[...], w_ref[...], preferred_element_type=F32)
        gate_v = gate_ref[...]
        x2 = x_ref[...] + gate_v * o
        r2 = lax.rsqrt(jnp.mean(x2 * x2, axis=-1, keepdims=True) + EPS)
        xn2 = x2 * r2
        hn2 = xn2 * gf_ref[...]
        one_sc = 1.0 + scf_ref[...]
        err = hn2 * one_sc + shf_ref[...] - t_ref[...]
        dout = err * (1.0 / D)
        dhn2 = dout * one_sc
        dxn2 = dhn2 * gf_ref[...]
        dx2 = r2 * (dxn2 - xn2 * jnp.mean(dxn2 * xn2, axis=-1, keepdims=True))
        dx2_ref[...] = dx2
        do = (dx2 * gate_v).astype(BF16)
        do_ref[...] = do
        dy_ref[...] = lax.dot_general(do, w_ref[...], NT, preferred_element_type=F32)
        st_ref[0:1, :] += jnp.sum(dout, axis=0, keepdims=True)
        st_ref[1:2, :] += jnp.sum(dout * hn2, axis=0, keepdims=True)
        st_ref[2:3, :] += jnp.sum(dhn2 * xn2, axis=0, keepdims=True)
        st_ref[3:4, :] += jnp.sum(dx2 * o, axis=0, keepdims=True)
        st_ref[4:5, :] += jnp.sum(err * err, axis=0, keepdims=True)

        @pl.when(i == nsteps - 1)
        def _():
            st_ref[5:6, :] = jnp.full((1, D), 0.5 / D, F32) * jnp.sum(st_ref[4:5, :])

    vec = pl.BlockSpec((1, D), lambda i: (0, 0))
    rows = lambda: pl.BlockSpec((tm, D), lambda i: (i, 0))
    return pl.pallas_call(
        body, name="tail", grid=(nsteps,),
        in_specs=[rows(), pl.BlockSpec((D, D), lambda i: (0, 0)), rows(), rows(), vec, vec, vec, vec],
        out_specs=[rows(), rows(), rows(), pl.BlockSpec((8, D), lambda i: (0, 0))],
        out_shape=[jax.ShapeDtypeStruct((s, D), F32), jax.ShapeDtypeStruct((s, D), BF16), jax.ShapeDtypeStruct((s, D), F32),
                   jax.ShapeDtypeStruct((8, D), F32)],
        compiler_params=_params("arbitrary"),
    )(y, w_out_bf, x, target, gate, shift_f, scale_f, gf)


def _tn_call(a, b, name):
    s, m = a.shape
    n = b.shape[1]
    tn = 512
    ts = min(s, 1024)
    nk = s // ts

    def body(a_ref, b_ref, o_ref, acc_ref):
        k = pl.program_id(1)

        @pl.when(k == 0)
        def _():
            acc_ref[...] = jnp.zeros((m, tn), F32)

        acc_ref[...] += lax.dot_general(a_ref[...], b_ref[...], TN, preferred_element_type=F32)

        @pl.when(k == nk - 1)
        def _():
            o_ref[...] = acc_ref[...].astype(BF16)

    return pl.pallas_call(
        body, name=name, grid=(n // tn, nk),
        in_specs=[pl.BlockSpec((ts, m), lambda j, k: (k, 0)), pl.BlockSpec((ts, tn), lambda j, k: (k, j))],
        out_specs=pl.BlockSpec((m, tn), lambda j, k: (0, j)),
        out_shape=jax.ShapeDtypeStruct((m, n), BF16),
        scratch_shapes=[pltpu.VMEM((m, tn), F32)],
        compiler_params=_params("parallel", "arbitrary"),
    )(a, b)


def _tn_shards_call(pos, a, b, qs, name):
    s, m = a.shape
    ts = min(s, 1024)
    nk = s // ts

    def body(pos_ref, a_ref, b_ref, o_ref, acc_ref):
        k = pl.program_id(1)

        @pl.when(k == 0)
        def _():
            acc_ref[...] = jnp.zeros((m, W_IN_SHARD), F32)

        acc_ref[...] += lax.dot_general(a_ref[...], b_ref[...], TN, preferred_element_type=F32)

        @pl.when(k == nk - 1)
        def _():
            o_ref[...] = acc_ref[...].astype(BF16)

    def shard(j, pos):
        q = qs[0]
        for n in range(1, len(qs)):
            q = jnp.where(j == n, qs[n], q)
        return jnp.bitwise_xor(pos[0], q)

    return pl.pallas_call(
        body, name=name,
        grid_spec=pltpu.PrefetchScalarGridSpec(
            num_scalar_prefetch=1, grid=(len(qs), nk),
            in_specs=[pl.BlockSpec((ts, m), lambda j, k, pos: (k, 0)),
                      pl.BlockSpec((ts, W_IN_SHARD), lambda j, k, pos: (k, shard(j, pos)))],
            out_specs=pl.BlockSpec((m, W_IN_SHARD), lambda j, k, pos: (0, j)),
            scratch_shapes=[pltpu.VMEM((m, W_IN_SHARD), F32)]),
        out_shape=jax.ShapeDtypeStruct((m, len(qs) * W_IN_SHARD), BF16),
        compiler_params=_params("parallel", "arbitrary"),
    )(pos, a, b)


def _mix_bwd_call(proj, dy, cos, sin, ln_g, ln_b, w_sp, w_sp_t, b_sp_t, sinks):
    s = proj.shape[0]
    nb = s // BLK
    rev = lambda i: nb - 1 - i
    prev = lambda i: jnp.maximum(nb - 2 - i, 0)

    def body(ua_ref, va_ref, za_ref, q_ref, k_ref, v_ref, zb0_ref, zb1_ref, kp_ref, vp_ref, dy_ref,
             cos_ref, sin_ref, cosp_ref, sinp_ref, lg_ref, lb_ref, w_ref, wt_ref, bt_ref, sinks_ref,
             dp_ref, lnst_ref, dw_ref, dbt_ref, dsink_ref,
             kdup_ref, vdup_ref, dvln_ref, qm_ref, dom_ref, ost_ref, dqst_ref, dkdup_ref, dvdup_ref, kcar_ref, vcar_ref,
             sigb_ref, bias_ref):
        i = pl.program_id(0)
        first_half, lo = _lane_masks()
        lane8 = lax.broadcasted_iota(jnp.int32, (8, LANE), 1)
        cos_t = cos_ref[...]
        sin_t = sin_ref[...]

        @pl.when(i == 0)
        def _():
            lnst_ref[...] = jnp.zeros((8, D_A), F32)
            dw_ref[...] = jnp.zeros((GROUPS, BLK, BLK), F32)
            dbt_ref[...] = jnp.zeros((BLK, LANE), F32)
            dsink_ref[...] = jnp.zeros((8, LANE), F32)
            kcar_ref[...] = jnp.zeros((BLK, D_KV), F32)
            vcar_ref[...] = jnp.zeros((BLK, D_KV), F32)
            _band_bias(bias_ref)

        vhat, rstd, vln = _layer_norm_fwd(va_ref[...], lg_ref[...], lb_ref[...])
        tril = _tril()
        triu = jnp.logical_not(tril) | (lax.broadcasted_iota(jnp.int32, (BLK, BLK), 0) == lax.broadcasted_iota(jnp.int32, (BLK, BLK), 1))
        lane_b = lax.broadcasted_iota(jnp.int32, (BLK, LANE), 1)
        db_acc = jnp.zeros((BLK, LANE), F32)
        for g in range(GROUPS):
            cols = slice(g * BLK, (g + 1) * BLK)
            vln_g = vln[:, cols].astype(BF16)
            wg = jnp.where(tril, w_ref[g], 0.0).astype(BF16)
            sg = jnp.dot(wg, vln_g, preferred_element_type=F32) + bt_ref[:, g:g + 1]
            za = za_ref[:, cols]
            gate, sig = _silu_parts(za)
            ua = ua_ref[:, cols]
            dya_g = dy_ref[:, cols]
            dya = dya_g * gate
            dp_ref[:, cols] = (dya * sg).astype(BF16)
            dp_ref[:, 2 * D_A + g * BLK:2 * D_A + (g + 1) * BLK] = (
                dya_g * (ua * sg) * (sig * (1.0 + za * (1.0 - sig)))).astype(BF16)
            ds = dya * ua
            ds_b = ds.astype(BF16)
            wtg = jnp.where(triu, wt_ref[g], 0.0).astype(BF16)
            dvln_ref[:, cols] = jnp.dot(wtg, ds_b, preferred_element_type=F32)
            dw_ref[g] += jnp.where(tril, lax.dot_general(ds_b, vln_g, NT, preferred_element_type=F32), 0.0)
            db_acc = db_acc + jnp.where(lane_b == g, jnp.sum(ds, axis=-1, keepdims=True), 0.0)
        dbt_ref[...] += db_acc
        dvln = dvln_ref[...]
        lnst_ref[0:1, :] += jnp.sum(dvln * vhat, axis=0, keepdims=True)
        lnst_ref[1:2, :] += jnp.sum(dvln, axis=0, keepdims=True)
        dvhat = dvln * lg_ref[...]
        m1 = jnp.mean(dvhat, axis=-1, keepdims=True)
        m2 = jnp.mean(dvhat * vhat, axis=-1, keepdims=True)
        dp_ref[:, D_A:2 * D_A] = (rstd * (dvhat - m1 - vhat * m2)).astype(BF16)

        cosp = cosp_ref[...]
        sinp = sinp_ref[...]
        for ks in range(2):
            cols = slice(ks * LANE, (ks + 1) * LANE)
            kr = _rope(k_ref[:, cols], cos_t, sin_t, first_half)
            kpr = _rope(kp_ref[:, cols], cosp, sinp, first_half)
            for n, (kc, vc, kp, vp) in enumerate(zip(_dup_kv(kr, lo), _dup_kv(v_ref[:, cols], lo),
                                                     _dup_kv(kpr, lo), _dup_kv(vp_ref[:, cols], lo))):
                kdup_ref[2 * ks + n, BLK:2 * BLK, :] = kc
                vdup_ref[2 * ks + n, BLK:2 * BLK, :] = vc
                kdup_ref[2 * ks + n, 0:BLK, :] = kp
                vdup_ref[2 * ks + n, 0:BLK, :] = vp
        for sb in range(8):
            cols = slice(sb * LANE, (sb + 1) * LANE)
            _stack_heads(qm_ref, sb, _rope(q_ref[:, cols], cos_t, sin_t, first_half) * SCALE, lo, BF16)
            zb = zb0_ref[:, cols] if sb < 4 else zb1_ref[:, (sb - 4) * LANE:(sb - 3) * LANE]
            gate, sig = _silu_parts(zb)
            sigb_ref[:, cols] = sig
            _stack_heads(dom_ref, sb, dy_ref[:, D_A + sb * LANE:D_A + (sb + 1) * LANE] * gate, lo, F32)

        block_kind = jnp.where(i < nb - 1, 1, 0)

        def kv_head(kh, dsink_acc):
            qm = qm_ref[kh]
            kd = kdup_ref[kh]
            vd = vdup_ref[kh]
            probs, psink = _softmax_sink(qm, kd, bias_ref[block_kind], _sink_column(sinks_ref, kh))
            probs_b = probs.astype(BF16)
            o = jnp.dot(probs_b, vd, preferred_element_type=F32)
            ost_ref[kh] = o
            dom = dom_ref[kh]
            dom_b = dom.astype(BF16)
            delta = jnp.sum(dom * o, axis=-1, keepdims=True)
            dpr = lax.dot_general(dom_b, vd, NT, preferred_element_type=F32)
            dss = (probs * (dpr - delta)).astype(BF16)
            sd = psink * delta
            for n in range(Q_PER_KV):
                dsink_acc = dsink_acc + jnp.where(lane8 == Q_PER_KV * kh + n, -jnp.sum(sd[n * BLK:(n + 1) * BLK]), 0.0)
            dqst_ref[kh] = jnp.dot(dss, kd, preferred_element_type=F32)
            dkdup_ref[kh] = lax.dot_general(dss, qm, TN, preferred_element_type=F32)
            dvdup_ref[kh] = lax.dot_general(probs_b, dom_b, TN, preferred_element_type=F32)
            return dsink_acc

        dsink_acc = lax.fori_loop(0, N_KV // 2, lambda j, acc: kv_head(2 * j + 1, kv_head(2 * j, acc)), jnp.zeros((8, LANE), F32))
        row0 = lax.broadcasted_iota(jnp.int32, (8, LANE), 0) == 0
        dsink_ref[...] += jnp.where(row0, dsink_acc, 0.0)

        for sb in range(8):
            cols = slice(sb * LANE, (sb + 1) * LANE)
            zb = zb0_ref[:, cols] if sb < 4 else zb1_ref[:, (sb - 4) * LANE:(sb - 3) * LANE]
            sig = sigb_ref[:, cols]
            dyb = dy_ref[:, D_A + sb * LANE:D_A + (sb + 1) * LANE]
            dp_ref[:, OFF_ZB + sb * LANE:OFF_ZB + (sb + 1) * LANE] = (
                dyb * _unstack_heads(ost_ref, sb, lo) * (sig * (1.0 + zb * (1.0 - sig)))).astype(BF16)
            dq_r = _unstack_heads(dqst_ref, sb, lo) * SCALE
            dp_ref[:, OFF_Q + sb * LANE:OFF_Q + (sb + 1) * LANE] = _unrope(dq_r, cos_t, sin_t, first_half).astype(BF16)

        lo2 = lax.broadcasted_iota(jnp.int32, (2 * BLK, LANE), 1) < HEAD
        for ks in range(2):
            cols = slice(ks * LANE, (ks + 1) * LANE)
            ka = dkdup_ref[2 * ks]
            kb = dkdup_ref[2 * ks + 1]
            dk_band = jnp.where(lo2, ka + pltpu.roll(ka, HEAD, 1), kb + pltpu.roll(kb, HEAD, 1))
            va_ = dvdup_ref[2 * ks]
            vb_ = dvdup_ref[2 * ks + 1]
            dv_band = jnp.where(lo2, va_ + pltpu.roll(va_, HEAD, 1), vb_ + pltpu.roll(vb_, HEAD, 1))
            dkr = dk_band[BLK:2 * BLK, :] + kcar_ref[:, cols]
            dp_ref[:, OFF_K + ks * LANE:OFF_K + (ks + 1) * LANE] = _unrope(dkr, cos_t, sin_t, first_half).astype(BF16)
            dp_ref[:, OFF_V + ks * LANE:OFF_V + (ks + 1) * LANE] = (
                dv_band[BLK:2 * BLK, :] + vcar_ref[:, cols]).astype(BF16)
            kcar_ref[:, cols] = dk_band[0:BLK, :]
            vcar_ref[:, cols] = dv_band[0:BLK, :]

    tab = pl.BlockSpec((BLK, LANE), lambda i: (rev(i), 0))
    tabp = pl.BlockSpec((BLK, LANE), lambda i: (prev(i), 0))
    kvp = lambda col: pl.BlockSpec((BLK, D_KV), lambda i: (prev(i), col))
    vec = pl.BlockSpec((1, D_A), lambda i: (0, 0))
    w3 = pl.BlockSpec((GROUPS, BLK, BLK), lambda i: (0, 0, 0))
    return pl.pallas_call(
        body, name="mix_bwd", grid=(nb,),
        in_specs=_proj_specs(nb) + [
            kvp(OFF_K // D_KV), kvp(OFF_V // D_KV), pl.BlockSpec((BLK, 2 * D_A), lambda i: (rev(i), 0)),
            tab, tab, tabp, tabp, vec, vec, w3, w3, pl.BlockSpec((BLK, GROUPS), lambda i: (0, 0)),
            pl.BlockSpec(memory_space=pltpu.SMEM)],
        out_specs=[pl.BlockSpec((BLK, D_IN), lambda i: (rev(i), 0)), pl.BlockSpec((8, D_A), lambda i: (0, 0)), w3,
                   pl.BlockSpec((BLK, LANE), lambda i: (0, 0)), pl.BlockSpec((8, LANE), lambda i: (0, 0))],
        out_shape=[jax.ShapeDtypeStruct((s, D_IN), BF16), jax.ShapeDtypeStruct((8, D_A), F32),
                   jax.ShapeDtypeStruct((GROUPS, BLK, BLK), F32), jax.ShapeDtypeStruct((BLK, LANE), F32),
                   jax.ShapeDtypeStruct((8, LANE), F32)],
        scratch_shapes=[pltpu.VMEM((N_KV, 2 * BLK, LANE), BF16), pltpu.VMEM((N_KV, 2 * BLK, LANE), BF16),
                        pltpu.VMEM((BLK, D_A), F32), pltpu.VMEM((N_KV, Q_PER_KV * BLK, LANE), BF16),
                        pltpu.VMEM((N_KV, Q_PER_KV * BLK, LANE), F32), pltpu.VMEM((N_KV, Q_PER_KV * BLK, LANE), F32),
                        pltpu.VMEM((N_KV, Q_PER_KV * BLK, LANE), F32), pltpu.VMEM((N_KV, 2 * BLK, LANE), F32),
                        pltpu.VMEM((N_KV, 2 * BLK, LANE), F32), pltpu.VMEM((BLK, D_KV), F32), pltpu.VMEM((BLK, D_KV), F32),
                        pltpu.VMEM((BLK, D_B), F32), pltpu.VMEM((2, Q_PER_KV * BLK, 2 * BLK), F32)],
        compiler_params=_params("arbitrary"),
    )(proj, proj, proj, proj, proj, proj, proj, proj, proj, proj, dy, cos, sin, cos, sin, ln_g, ln_b, w_sp, w_sp_t,
      b_sp_t, sinks)


def _dh_call(dproj, w_bf, x, dx2, scale, norm_g):
    s = x.shape[0]
    tm = min(s, 512)
    tk = W_IN_SHARD
    nk = D_IN // tk

    def body(dp_ref, w_ref, x_ref, dx2_ref, sc_ref, g_ref, gx_ref, st_ref, acc_ref):
        i = pl.program_id(0)
        k = pl.program_id(1)

        @pl.when((i == 0) & (k == 0))
        def _():
            st_ref[...] = jnp.zeros((8, D), F32)

        @pl.when(k == 0)
        def _():
            acc_ref[...] = jnp.zeros((tm, D), F32)

        acc_ref[...] += lax.dot_general(dp_ref[...], w_ref[...], NT, preferred_element_type=F32)

        @pl.when(k == nk - 1)
        def _():
            g = g_ref[...]
            one_sc = 1.0 + sc_ref[...]

            def chunk(n, carry):
                rows = pl.ds(pl.multiple_of(n * BLK, BLK), BLK)
                dh = acc_ref[rows, :]
                xv = x_ref[rows, :]
                r = lax.rsqrt(jnp.mean(xv * xv, axis=-1, keepdims=True) + EPS)
                xn = xv * r
                dhn = dh * one_sc
                dxn = dhn * g
                gx_ref[rows, :] = dx2_ref[rows, :] + r * (dxn - xn * jnp.mean(dxn * xn, axis=-1, keepdims=True))
                st_ref[0:1, :] += jnp.sum(dh, axis=0, keepdims=True)
                st_ref[1:2, :] += jnp.sum(dh * (xn * g), axis=0, keepdims=True)
                st_ref[2:3, :] += jnp.sum(dhn * xn, axis=0, keepdims=True)
                return carry

            lax.fori_loop(0, tm // BLK, chunk, 0)

    vec = pl.BlockSpec((1, D), lambda i, k: (0, 0))
    rows = lambda: pl.BlockSpec((tm, D), lambda i, k: (i, 0))
    return pl.pallas_call(
        body, name="dh", grid=(s // tm, nk),
        in_specs=[pl.BlockSpec((tm, tk), lambda i, k: (i, k)), pl.BlockSpec((D, tk), lambda i, k: (0, k)), rows(), rows(), vec, vec],
        out_specs=[rows(), pl.BlockSpec((8, D), lambda i, k: (0, 0))],
        out_shape=[jax.ShapeDtypeStruct((s, D), F32), jax.ShapeDtypeStruct((8, D), F32)],
        scratch_shapes=[pltpu.VMEM((tm, D), F32)],
        compiler_params=_params("arbitrary", "arbitrary"),
    )(dproj, w_bf, x, dx2, scale, norm_g)


def _adam_math(w, g, m, v):
    m_new = ADAM_B1 * m + (1.0 - ADAM_B1) * g
    v_new = ADAM_B2 * v + (1.0 - ADAM_B2) * (g * g)
    m_hat = m_new / ADAM_C1
    v_hat = v_new / ADAM_C2
    delta = -ADAM_LR * (m_hat / (jnp.sqrt(v_hat) + ADAM_EPS) + ADAM_WD * w)
    return delta, m_new, v_new


def _adam_small_call(tensors):
    n = len(tensors)

    def body(*refs):
        ins, outs = refs[:4 * n], refs[4 * n:]
        for t in range(n):
            w_ref, g_ref, m_ref, v_ref = ins[4 * t:4 * t + 4]
            d, mo, vo = _adam_math(w_ref[...], g_ref[...], m_ref[...], v_ref[...])
            outs[3 * t][...], outs[3 * t + 1][...], outs[3 * t + 2][...] = d, mo, vo

    vm = pl.BlockSpec(memory_space=pltpu.VMEM)
    flat = [a for t in tensors for a in t]
    out = pl.pallas_call(
        body, name="adam_small", in_specs=[vm] * (4 * n), out_specs=[vm] * (3 * n),
        out_shape=[jax.ShapeDtypeStruct(t[0].shape, F32) for t in tensors for _ in range(3)],
        compiler_params=pltpu.CompilerParams(vmem_limit_bytes=VMEM_LIMIT),
    )(*flat)
    return [tuple(out[3 * t:3 * t + 3]) for t in range(n)]


def _adam_halves_call(pos, w, mine, theirs, m, v, name):
    r, n = w.shape
    half = r // 2
    tr = ADAM_ROWS
    nh = half // tr

    def body(pos_ref, w_ref, mine_ref, theirs_ref, m_ref, v_ref, g_ref, d_ref, mo_ref, vo_ref):
        is_mine = (pl.program_id(0) // nh) == pos_ref[1]
        g = jnp.where(is_mine, mine_ref[...], theirs_ref[...])
        g_ref[...] = g
        d_ref[...], mo_ref[...], vo_ref[...] = _adam_math(w_ref[...], g, m_ref[...], v_ref[...])

    spec = lambda: pl.BlockSpec((tr, n), lambda i, pos: (i, 0))
    hspec = lambda: pl.BlockSpec((tr, n), lambda i, pos: (i % nh, 0))
    return pl.pallas_call(
        body, name=name,
        grid_spec=pltpu.PrefetchScalarGridSpec(
            num_scalar_prefetch=1, grid=(r // tr,), in_specs=[spec(), hspec(), hspec(), spec(), spec()],
            out_specs=[spec() for _ in range(4)]),
        out_shape=[jax.ShapeDtypeStruct((r, n), F32)] * 4, compiler_params=_params("parallel"),
    )(pos, w, mine, theirs, m, v)


def _adam_outer_call(w, ct, dm, m, v, name):
    r, n = w.shape
    tr = ADAM_ROWS

    def body(w_ref, ct_ref, dm_ref, m_ref, v_ref, g_ref, d_ref, mo_ref, vo_ref):
        g = ct_ref[:, 0:1] * dm_ref[0:1, :]
        for b in range(1, N_DEV):
            g = g + ct_ref[:, b:b + 1] * dm_ref[b:b + 1, :]
        g_ref[...] = g
        d_ref[...], mo_ref[...], vo_ref[...] = _adam_math(w_ref[...], g, m_ref[...], v_ref[...])

    spec = lambda: pl.BlockSpec((tr, n), lambda i: (i, 0))
    return pl.pallas_call(
        body, name=name, grid=(r // tr,),
        in_specs=[spec(), pl.BlockSpec((tr, N_DEV), lambda i: (i, 0)), pl.BlockSpec((N_DEV, n), lambda i: (0, 0)), spec(), spec()],
        out_specs=[spec() for _ in range(4)],
        out_shape=[jax.ShapeDtypeStruct((r, n), F32)] * 4, compiler_params=_params("parallel"),
    )(w, ct, dm, m, v)


def _sum_pieces_call(pos, part, part_block, recvs, name):
    r, n = recvs[0].shape[1:]
    tr = min(r, 256)
    nrb = r // tr

    def body(pos_ref, p_ref, *refs):
        acc = p_ref[...].astype(F32)
        for r_ref in refs[:-1]:
            for d in range(r_ref.shape[0]):
                acc = acc + r_ref[d].astype(F32)
        refs[-1][...] = acc

    return pl.pallas_call(
        body, name=name,
        grid_spec=pltpu.PrefetchScalarGridSpec(
            num_scalar_prefetch=1, grid=(nrb,),
            in_specs=[pl.BlockSpec((tr, n), lambda i, pos: part_block(i, pos, nrb))] + [
                pl.BlockSpec((rv.shape[0], tr, n), lambda i, pos: (0, i, 0)) for rv in recvs],
            out_specs=pl.BlockSpec((tr, n), lambda i, pos: (i, 0))),
        out_shape=jax.ShapeDtypeStruct((r, n), F32), compiler_params=_params("parallel"),
    )(pos, part, *recvs)


def _coords():
    return lax.axis_index("x"), lax.axis_index("y"), lax.axis_index("c")


def _allgather_sum_call(blk, name, with_sum):
    m_per, n = blk.shape

    def body(x_ref, out_ref, *rest):
        if with_sum:
            sum_ref, send_sems, recv_sems, local_sem = rest
        else:
            send_sems, recv_sems, local_sem = rest
        x, y, c = _coords()
        me, sibling = (x, y, c), (x, y, 1 - c)
        chips = [(1 - x, y), (x, 1 - y), (1 - x, 1 - y)]

        def rows(px, py, pc):
            return out_ref.at[pl.ds((4 * px + 2 * py + pc) * m_per, m_per), :]

        def copy(k, block, to, src=None):
            return pltpu.make_async_remote_copy(
                src_ref=rows(*block) if src is None else src, dst_ref=rows(*block),
                send_sem=send_sems.at[k], recv_sem=recv_sems.at[k], device_id=to, device_id_type=MESH)

        mine = pltpu.make_async_copy(x_ref, rows(*me), local_sem)
        mine.start()
        first = [copy(0, me, sibling, src=x_ref)]
        first += [copy(1 + j, me, (*chip, c), src=x_ref) for j, chip in enumerate(chips)]
        for cp in first:
            cp.start()
        passed = [copy(4 + j, (*chip, c), sibling) for j, chip in enumerate(chips)]
        for j, chip in enumerate(chips):
            copy(1 + j, (*chip, c), me).wait_recv()
            passed[j].start()
        copy(0, sibling, me).wait_recv()
        for j, chip in enumerate(chips):
            copy(4 + j, (*chip, 1 - c), me).wait_recv()
        for cp in first + passed:
            cp.wait_send()
        mine.wait()
        if with_sum:
            acc = out_ref[0:m_per, :]
            for d in range(1, N_DEV):
                acc = acc + out_ref[d * m_per:(d + 1) * m_per, :]
            sum_ref[...] = acc

    vm = pl.BlockSpec(memory_space=pltpu.VMEM)
    out_shape = [jax.ShapeDtypeStruct((N_DEV * m_per, n), F32)]
    if with_sum:
        out_shape.append(jax.ShapeDtypeStruct((m_per, n), F32))
    return pl.pallas_call(
        body, name=name, out_shape=out_shape, in_specs=[vm], out_specs=[vm] * len(out_shape),
        scratch_shapes=[pltpu.SemaphoreType.DMA((7,)), pltpu.SemaphoreType.DMA((7,)), pltpu.SemaphoreType.DMA],
        compiler_params=pltpu.CompilerParams(vmem_limit_bytes=VMEM_LIMIT),
    )(blk)


HBM_SPEC = pl.BlockSpec(memory_space=pltpu.HBM)
SEM_SPEC = pl.BlockSpec(memory_space=pltpu.SEMAPHORE)
SIDE_EFFECT = pltpu.SideEffectType.DATAFLOW_SIDE_EFFECTING


def _peer(x, y, c, q, cb):
    return (1 - x if q & 2 else x, 1 - y if q & 1 else y, 1 - c if cb else c)


def _w_in_piece(slots):
    def piece(part_ref, k, to):
        return part_ref.at[pl.ds(to[2] * (D // 2), D // 2), pl.ds(slots[k] * W_IN_SHARD, W_IN_SHARD)]
    return piece


def _w_out_piece(part_ref, k, to):
    ho = W_OUT_SHARD // 2
    return part_ref.at[pl.ds((2 * to[0] + to[1]) * W_OUT_SHARD + to[2] * ho, ho), :]


def _group_piece(part_ref, k, to):
    return part_ref.at[4 * to[0] + 2 * to[1] + to[2]]


def _whole_piece(part_ref, k, to):
    return part_ref


def _exchange_start_call(groups, name):
    ng = len(groups)
    lands = [lax.empty((len(rels),) + slot_shape, part.dtype) for part, rels, _, slot_shape in groups]

    def body(*refs):
        ins, outs = refs[:2 * ng], refs[2 * ng:]
        x, y, c = _coords()
        for g, (_, rels, piece, _) in enumerate(groups):
            part_ref, land_ref = ins[2 * g], ins[2 * g + 1]
            send_sems, recv_sems = outs[4 * g], outs[4 * g + 1]
            for k, (q, cb) in enumerate(rels):
                to = _peer(x, y, c, q, cb)
                pltpu.make_async_remote_copy(src_ref=piece(part_ref, k, to), dst_ref=land_ref.at[k], send_sem=send_sems.at[k],
                                             recv_sem=recv_sems.at[k], device_id=to, device_id_type=MESH).start()
        outs[-1][...] = jnp.zeros_like(outs[-1])

    out_shape, out_specs, operands = [], [], []
    for (part, rels, _, _), land in zip(groups, lands):
        n = len(rels)
        out_shape += [pltpu.SemaphoreType.DMA((n,)), pltpu.SemaphoreType.DMA((n,)), pltpu.HBM(part.shape, part.dtype),
                      pltpu.HBM(land.shape, land.dtype)]
        out_specs += [SEM_SPEC, SEM_SPEC, HBM_SPEC, HBM_SPEC]
        operands += [pltpu.with_memory_space_constraint(part, pltpu.HBM), pltpu.with_memory_space_constraint(land, pltpu.HBM)]
    out = pl.pallas_call(
        body, name=name,
        out_shape=tuple(out_shape) + (jax.ShapeDtypeStruct((1, 1), F32),),
        in_specs=(HBM_SPEC,) * (2 * ng), out_specs=tuple(out_specs) + (pl.BlockSpec(memory_space=pltpu.VMEM),),
        input_output_aliases={j: 4 * (j // 2) + 2 + j % 2 for j in range(2 * ng)},
        compiler_params=pltpu.CompilerParams(has_side_effects=SIDE_EFFECT),
    )(*operands)
    return [tuple(out[4 * g:4 * g + 4]) for g in range(ng)], out[-1]


def _exchange_wait_call(started, groups, after, name):
    ng = len(groups)

    def body(*refs):
        ins = refs[:4 * ng]
        x, y, c = _coords()
        for g, (_, rels, piece, _) in enumerate(groups):
            part_ref, land_ref, send_sems, recv_sems = ins[4 * g:4 * g + 4]
            for k, (q, cb) in enumerate(rels):
                to = _peer(x, y, c, q, cb)
                cp = pltpu.make_async_remote_copy(src_ref=piece(part_ref, k, to), dst_ref=land_ref.at[k], send_sem=send_sems.at[k],
                                                  recv_sem=recv_sems.at[k], device_id=to, device_id_type=MESH)
                cp.wait_send()
                cp.wait_recv()

    operands, in_specs, out_shape = [], [], []
    for send_sems, recv_sems, part_thru, land_thru in started:
        operands += [part_thru, land_thru, send_sems, recv_sems]
        in_specs += [HBM_SPEC, HBM_SPEC, SEM_SPEC, SEM_SPEC]
        out_shape += [pltpu.HBM(part_thru.shape, part_thru.dtype), pltpu.HBM(land_thru.shape, land_thru.dtype)]
    out = pl.pallas_call(
        body, name=name, out_shape=tuple(out_shape),
        in_specs=tuple(in_specs) + (pl.BlockSpec(memory_space=pl.ANY),), out_specs=(HBM_SPEC,) * (2 * ng),
        input_output_aliases={4 * g + j: 2 * g + j for g in range(ng) for j in range(2)},
        compiler_params=pltpu.CompilerParams(has_side_effects=SIDE_EFFECT),
    )(*operands, after)
    return [tuple(out[2 * g:2 * g + 2]) for g in range(ng)]


def _rope_tables(s):
    inv_freq = 10000.0 ** (-jnp.arange(0, HEAD, 2, dtype=F32) / HEAD)
    ang = jnp.arange(s, dtype=F32)[:, None] * inv_freq[None, :]
    cos = jnp.tile(jnp.cos(ang), (1, LANE // (HEAD // 2)))
    sin = jnp.tile(jnp.sin(ang), (1, LANE // (HEAD // 2)))
    first_half = (jnp.arange(LANE) % HEAD) < (HEAD // 2)
    return cos, jnp.where(first_half[None, :], -sin, sin)


def kernel(x, c, w_ada, b_ada, norm_g, w_in, ln_v_g, ln_v_b, w_spatial, b_spatial, sinks, w_out, w_ada_final, b_ada_final, final_norm_g, loss_target, m_w_ada, m_b_ada, m_norm_g, m_w_in, m_ln_v_g, m_ln_v_b, m_w_spatial, m_b_spatial, m_sinks, m_w_out, m_w_ada_final, m_b_ada_final, m_final_norm_g, v_w_ada, v_b_ada, v_norm_g, v_w_in, v_ln_v_g, v_ln_v_b, v_w_spatial, v_b_spatial, v_sinks, v_w_out, v_w_ada_final, v_b_ada_final, v_final_norm_g):
    s = x.shape[1]
    ax, ay, ac = _coords()
    chip = 2 * ax + ay
    me = 4 * ax + 2 * ay + ac
    n_ada = w_ada.shape[2]
    n_adaf = w_ada_final.shape[1]

    x2d = x.reshape(s, D)
    tgt = loss_target.reshape(s, D)
    w_ada2, w_in2, w_out2 = w_ada[0], w_in[0], w_out[0]
    b_ada_f2 = b_ada_final.reshape(1, 2 * D)
    gf = final_norm_g.reshape(1, D)

    c_all = _allgather_sum_call(jnp.pad(c, ((0, 7), (0, 0))), "gather_c", False)[0][::8]
    mod_p, c_act = _rowmat_call(c_all, w_ada2, lax.dynamic_slice(b_ada, (0, chip * n_ada), (1, n_ada)), "mod")
    modf_p, _ = _rowmat_call(c_all, w_ada_final, lax.dynamic_slice(b_ada_f2, (0, chip * n_adaf), (1, n_adaf)), "mod_final")
    mods = _allgather_sum_call(jnp.concatenate([mod_p, modf_p], axis=1), "gather_mod", False)[0]
    my_rows = [lax.dynamic_slice(mods, (16 * j + me, 0), (1, n_ada + n_adaf)) for j in range(N_CHIP)]
    mod = jnp.concatenate([r[:, :n_ada] for r in my_rows], axis=1)
    mod_f = jnp.concatenate([r[:, n_ada:] for r in my_rows], axis=1)
    shift, scale, gate = mod[:, :D], mod[:, D:2 * D], mod[:, 2 * D:]
    shift_f, scale_f = mod_f[:, :D], mod_f[:, D:]

    pos = jnp.stack([chip, ac]).astype(jnp.int32)
    w_in_own = _cast_into_call(pos, w_in2, (D, D_IN), "cast_w_in")
    w_out_own = _cast_into_call(pos, w_out2, (D, D), "cast_w_out")

    cos, sin = _rope_tables(s)
    b_sp_t = b_spatial[0].T
    sinks1 = sinks.reshape(N_Q)
    h, proj, w_in_bf, w_out_bf = _proj_gather_call(pos, x2d, shift, scale, norm_g, w_in_own, w_out_own)
    y = _mix_fwd_call(proj, cos, sin, ln_v_g, ln_v_b, w_spatial[0], b_sp_t, sinks1)
    dx2, do, dy, st_tail = _tail_call(y, w_out_bf, x2d, tgt, gate, shift_f, scale_f, gf)

    rel_o = [(0, 1), (1, 0), (1, 1), (2, 0), (2, 1), (3, 0), (3, 1)]
    rel_a = [(1, 0), (1, 1), (2, 0), (2, 1)]
    rel_b = [(3, 0), (3, 1), (0, 1)]
    piece_a, piece_b = _w_in_piece([0, 0, 1, 1]), _w_in_piece([0, 0, 1])
    half_in, half_out = (D // 2, W_IN_SHARD), (W_OUT_SHARD // 2, D)

    g_w_out_p = _tn_call(y, do, "grad_w_out")
    grp_o = [(g_w_out_p, rel_o, _w_out_piece, half_out)]
    st_o, tok_o = _exchange_start_call(grp_o, "send_w_out")
    dproj, st_ln, d_wsp, d_bsp_t, d_sink = _mix_bwd_call(
        proj, dy, cos, sin, ln_v_g + tok_o, ln_v_b, w_spatial[0], jnp.swapaxes(w_spatial[0], 1, 2), b_sp_t, sinks1)
    g_w_in_a = _tn_shards_call(pos, h, dproj, (1, 2), "grad_w_in_a")
    grp_a = [(g_w_in_a, rel_a, piece_a, half_in), (d_wsp, rel_o, _group_piece, (BLK, BLK))]
    st_a, tok_a = _exchange_start_call(grp_a, "send_w_in_a")
    g_w_in_b = _tn_shards_call(pos, h, dproj, (3, 0), "grad_w_in_b")
    grp_b = [(g_w_in_b, rel_b, piece_b, half_in)]
    st_b, tok_b = _exchange_start_call(grp_b, "send_w_in_b")
    grad_x, st_dh = _dh_call(dproj, w_in_bf, x2d, dx2, scale + (tok_a + tok_b), norm_g)

    ((g_w_out_p, recv_o),) = _exchange_wait_call(st_o, grp_o, st_dh, "wait_w_out")
    (_, recv_a), (d_wsp, recv_s) = _exchange_wait_call(st_a, grp_a, st_dh, "wait_w_in_a")
    ((g_w_in_b, recv_b),) = _exchange_wait_call(st_b, grp_b, st_dh, "wait_w_in_b")
    mine_in = _sum_pieces_call(pos, g_w_in_b, lambda i, p, nrb: (p[1] * nrb + i, 1), [recv_a, recv_b], "sum_w_in")
    mine_out = _sum_pieces_call(pos, g_w_out_p, lambda i, p, nrb: ((2 * p[0] + p[1]) * nrb + i, 0), [recv_o], "sum_w_out")
    wsp_group = _sum_pieces_call(pos, d_wsp.reshape(GROUPS * BLK, BLK), lambda i, p, nrb: (2 * p[0] + p[1], 0), [recv_s],
                                 "sum_w_spatial")
    to_sibling = [(0, 1)]
    grp_p = [(mine_in, to_sibling, _whole_piece, half_in), (mine_out, to_sibling, _whole_piece, half_out)]
    st_p, tok_p = _exchange_start_call(grp_p, "swap_halves")

    misc = jnp.concatenate([st_ln, d_bsp_t[:, :GROUPS].T, d_sink, jnp.zeros((8, D - D_A - 2 * LANE), F32)], axis=1)
    pack = jnp.concatenate([wsp_group.reshape(8, D) + tok_p, st_tail, st_dh, misc], axis=0)
    rows = pack.shape[0]
    packs, tot = _allgather_sum_call(pack, "gather_small", True)
    packs = packs.reshape(N_DEV, rows, D)
    dmod_all = jnp.concatenate([packs[:, 16, :], packs[:, 17, :], packs[:, 11, :]], axis=1)
    dmodf_all = jnp.concatenate([packs[:, 8, :], packs[:, 9, :]], axis=1)
    loss = tot[13, 0]
    (mine_in, theirs_in), (mine_out, theirs_out) = _exchange_wait_call(st_p, grp_p, tot, "swapped_halves")
    small = {
        "b_ada": jnp.concatenate([tot[16:17], tot[17:18], tot[11:12]], axis=1),
        "norm_g": tot[18:19],
        "ln_v_g": tot[24:25, :D_A],
        "ln_v_b": tot[25:26, :D_A],
        "w_spatial": packs[:, 0:8, :].reshape(GROUPS * BLK, BLK),
        "b_spatial": tot[24:32, D_A:D_A + BLK],
        "sinks": tot[24:25, D_A + LANE:D_A + LANE + N_Q],
        "b_ada_final": jnp.concatenate([tot[8:9], tot[9:10]], axis=1),
        "final_norm_g": tot[10:11],
    }

    weights = dict(w_ada=w_ada, b_ada=b_ada, norm_g=norm_g, w_in=w_in, ln_v_g=ln_v_g, ln_v_b=ln_v_b, w_spatial=w_spatial,
                   b_spatial=b_spatial, sinks=sinks, w_out=w_out, w_ada_final=w_ada_final, b_ada_final=b_ada_final,
                   final_norm_g=final_norm_g)
    m_in = dict(w_ada=m_w_ada, b_ada=m_b_ada, norm_g=m_norm_g, w_in=m_w_in, ln_v_g=m_ln_v_g, ln_v_b=m_ln_v_b,
                w_spatial=m_w_spatial, b_spatial=m_b_spatial, sinks=m_sinks, w_out=m_w_out, w_ada_final=m_w_ada_final,
                b_ada_final=m_b_ada_final, final_norm_g=m_final_norm_g)
    v_in = dict(w_ada=v_w_ada, b_ada=v_b_ada, norm_g=v_norm_g, w_in=v_w_in, ln_v_g=v_ln_v_g, ln_v_b=v_ln_v_b,
                w_spatial=v_w_spatial, b_spatial=v_b_spatial, sinks=v_sinks, w_out=v_w_out, w_ada_final=v_w_ada_final,
                b_ada_final=v_b_ada_final, final_norm_g=v_final_norm_g)
    c_act_t = c_act.T
    outer = {"w_ada": lax.dynamic_slice(dmod_all, (0, chip * n_ada), (N_DEV, n_ada)),
             "w_ada_final": lax.dynamic_slice(dmodf_all, (0, chip * n_adaf), (N_DEV, n_adaf))}
    halves = {"w_in": (mine_in, theirs_in[0]), "w_out": (mine_out, theirs_out[0])}
    done = {}
    for name, (mine, theirs) in halves.items():
        shape2 = (2 * mine.shape[0], mine.shape[1])
        done[name] = _adam_halves_call(pos, weights[name].reshape(shape2), mine, theirs, m_in[name].reshape(shape2),
                                       v_in[name].reshape(shape2), "adam_" + name)
    for name, dm in outer.items():
        shape2 = (D, dm.shape[1])
        done[name] = _adam_outer_call(weights[name].reshape(shape2), c_act_t, dm, m_in[name].reshape(shape2),
                                      v_in[name].reshape(shape2), "adam_" + name)
    updates = _adam_small_call([(weights[name].reshape(g.shape), g, m_in[name].reshape(g.shape), v_in[name].reshape(g.shape))
                                for name, g in small.items()])
    for (name, g), upd in zip(small.items(), updates):
        done[name] = (g, *upd)
    outs = [[done[name][k].reshape(w.shape) for name, w in weights.items()] for k in range(4)]
    return (loss, grad_x.reshape(x.shape), *outs[0], *outs[1], *outs[2], *outs[3])
```

```python
import jax
import jax.numpy as jnp
from jax import lax
from jax.experimental import pallas as pl
from jax.experimental.pallas import tpu as pltpu

F32 = jnp.float32
BF16 = jnp.bfloat16
MESH = pl.DeviceIdType.MESH

D = 2048
D_A = 1024
D_B = 1024
D_KV = 256
HEAD = 64
N_Q = 16
N_KV = 4
Q_PER_KV = N_Q // N_KV
BLK = 128
GROUPS = 8
D_IN = 5632
OFF_Q, OFF_K, OFF_V, OFF_ZB = 3072, 4096, 4352, 4608
N_CHIP = 4
N_DEV = 8
W_IN_SHARD = D_IN // N_CHIP
W_OUT_SHARD = D // N_CHIP
EPS = 1e-5
SCALE = HEAD ** -0.5
NEG = -1e30
LANE = 128
VMEM_LIMIT = 56 * 1024 * 1024

ADAM_LR, ADAM_B1, ADAM_B2, ADAM_EPS, ADAM_WD, ADAM_STEP = 0.001, 0.9, 0.999, 1e-08, 0.01, 10
ADAM_C1 = 1.0 - ADAM_B1 ** ADAM_STEP
ADAM_C2 = 1.0 - ADAM_B2 ** ADAM_STEP
ADAM_ROWS = 256

NT = (((1,), (1,)), ((), ()))
TN = (((0,), (0,)), ((), ()))


def _params(*sem):
    return pltpu.CompilerParams(dimension_semantics=sem, vmem_limit_bytes=VMEM_LIMIT)


def _silu_parts(z):
    sig = 1.0 / (1.0 + jnp.exp(-z))
    return z * sig, sig


def _swap_halves(v, first_half):
    return jnp.where(first_half, pltpu.roll(v, 96, 1), pltpu.roll(v, 32, 1))


def _rope(v, cos_t, sin_s, first_half):
    return v * cos_t + _swap_halves(v, first_half) * sin_s


def _unrope(dv, cos_t, sin_s, first_half):
    return dv * cos_t - _swap_halves(dv, first_half) * sin_s


def _lane_masks():
    lane = lax.broadcasted_iota(jnp.int32, (BLK, LANE), 1)
    return (lane % HEAD) < (HEAD // 2), lane < HEAD


def _band_valid(first_block_bound, rows=BLK):
    rr = lax.broadcasted_iota(jnp.int32, (rows, 2 * BLK), 0) & (BLK - 1)
    jj = lax.broadcasted_iota(jnp.int32, (rows, 2 * BLK), 1)
    return (jj > rr) & (jj <= rr + BLK) & (jj >= first_block_bound)


def _dup_kv(slab, lo):
    rolled = pltpu.roll(slab, HEAD, 1)
    return jnp.where(lo, slab, rolled).astype(BF16), jnp.where(lo, rolled, slab).astype(BF16)


def _stack_heads(ref, sb, slab, lo, dtype):
    kh, base = sb // 2, 2 * (sb % 2) * BLK
    zero = jnp.zeros_like(slab)
    ref[kh, base:base + BLK, :] = jnp.where(lo, slab, zero).astype(dtype)
    ref[kh, base + BLK:base + 2 * BLK, :] = jnp.where(lo, zero, slab).astype(dtype)


def _unstack_heads(ref, sb, lo):
    kh, base = sb // 2, 2 * (sb % 2) * BLK
    return jnp.where(lo, ref[kh, base:base + BLK, :], ref[kh, base + BLK:base + 2 * BLK, :])


def _sink_column(sinks_ref, kh):
    row = lax.broadcasted_iota(jnp.int32, (Q_PER_KV * BLK, 1), 0)
    col = jnp.full(row.shape, sinks_ref[Q_PER_KV * kh + Q_PER_KV - 1], F32)
    for n in range(Q_PER_KV - 2, -1, -1):
        col = jnp.where(row < (n + 1) * BLK, sinks_ref[Q_PER_KV * kh + n], col)
    return col


def _tril():
    t = lax.broadcasted_iota(jnp.int32, (BLK, BLK), 0)
    s = lax.broadcasted_iota(jnp.int32, (BLK, BLK), 1)
    return s <= t


def _layer_norm_fwd(va, lg, lb):
    mu = jnp.mean(va, axis=-1, keepdims=True)
    xc = va - mu
    rstd = lax.rsqrt(jnp.mean(xc * xc, axis=-1, keepdims=True) + EPS)
    vhat = xc * rstd
    return vhat, rstd, vhat * lg + lb


def _softmax_sink(qm, kdup, bias, sink):
    s = lax.dot_general(qm, kdup, NT, preferred_element_type=F32) + bias
    m = jnp.maximum(jnp.max(s, axis=-1, keepdims=True), sink)
    p = jnp.exp(s - m)
    esink = jnp.exp(sink - m)
    inv = 1.0 / (jnp.sum(p, axis=-1, keepdims=True) + esink)
    return p * inv, esink * inv


def _band_bias(bias_ref):
    rows = bias_ref.shape[1]
    bias_ref[0] = jnp.where(_band_valid(BLK, rows), 0.0, NEG)
    bias_ref[1] = jnp.where(_band_valid(0, rows), 0.0, NEG)


def _rowmat_call(c_all, w, b, name):
    n = w.shape[1]
    tn = 512

    def body(c_ref, w_ref, b_ref, o_ref, ca_ref):
        ca, _ = _silu_parts(c_ref[...])
        ca_ref[...] = ca
        o_ref[...] = jnp.dot(ca.astype(BF16), w_ref[...].astype(BF16), preferred_element_type=F32) + b_ref[...]

    return pl.pallas_call(
        body, name=name, grid=(n // tn,),
        in_specs=[pl.BlockSpec((N_DEV, D), lambda j: (0, 0)), pl.BlockSpec((D, tn), lambda j: (0, j)),
                  pl.BlockSpec((1, tn), lambda j: (0, j))],
        out_specs=[pl.BlockSpec((N_DEV, tn), lambda j: (0, j)), pl.BlockSpec((N_DEV, D), lambda j: (0, 0))],
        out_shape=[jax.ShapeDtypeStruct((N_DEV, n), F32), jax.ShapeDtypeStruct((N_DEV, D), F32)],
        compiler_params=_params("arbitrary"),
    )(c_all, w, b)


def _cast_into_call(pos, w, full_shape, name):
    r, n = w.shape
    tr = min(r, 512)
    by_cols = full_shape[0] == r
    nrb = r // tr

    def body(pos_ref, w_ref, o_ref):
        o_ref[...] = w_ref[...].astype(BF16)

    out_map = (lambda i, pos: (i, pos[0])) if by_cols else (lambda i, pos: (pos[0] * nrb + i, 0))
    return pl.pallas_call(
        body, name=name,
        grid_spec=pltpu.PrefetchScalarGridSpec(
            num_scalar_prefetch=1, grid=(nrb,),
            in_specs=[pl.BlockSpec((tr, n), lambda i, pos: (i, 0))], out_specs=pl.BlockSpec((tr, n), out_map)),
        out_shape=jax.ShapeDtypeStruct(full_shape, BF16), compiler_params=_params("parallel"),
    )(pos, w)


W_IN_PARTS = ((0, 768), (768, 640))
OUT_STREAMS = 4
X_STREAMS = 4


def _proj_gather_call(pos, x, shift, scale, norm_g, wi_full, wo_full):
    s = x.shape[0]
    tm = min(s, 512)
    nrow = s // tm
    hi = D // 2
    ho = W_OUT_SHARD // 2
    phases = [(0, None), (1, 0), (2, 0), (1, 1), (2, 1), (3, 0), (3, 1)]

    def body(pos_ref, *refs):
        x_refs = refs[:X_STREAMS]
        (sh_ref, sc_ref, g_ref, _, _, h_ref, proj_ref, fi_ref, fo_ref,
         h_all, wbuf, obuf, send_sems, recv_sems, load_sems, out_sems) = refs[X_STREAMS:]
        p = pl.program_id(0)
        i = pl.program_id(1)
        x_, y_, c_ = _coords()
        me, sibling = (x_, y_, c_), (x_, y_, 1 - c_)

        def shard_of(q):
            px, py, _ = _peer(x_, y_, c_, q, 0)
            return 2 * px + py

        def cols_of(q, cp):
            off, w = (0, W_IN_SHARD) if cp is None else W_IN_PARTS[cp]
            return shard_of(q) * W_IN_SHARD + off, w

        def part(which, q, pc, sub, cp):
            n = hi if which == 0 else ho
            base = pc * n
            if sub is not None:
                n //= 2
                base = base + sub * n
            if which == 0:
                c0, w = cols_of(q, cp)
                return fi_ref.at[pl.ds(base, n), pl.ds(c0, w)]
            return fo_ref.at[pl.ds(shard_of(q) * W_OUT_SHARD + base, n), :]

        def copy(k, ref, to):
            return pltpu.make_async_remote_copy(src_ref=ref, dst_ref=ref, send_sem=send_sems.at[k], recv_sem=recv_sems.at[k],
                                                device_id=to, device_id_type=MESH)

        def sem(which, kind, j, cp):
            return 4 * kind + 2 * cp + j if which == 0 else 16 + 2 * kind + j

        def to_neighbour(which, q, cp=None):
            return copy(sem(which, 0, q - 1, cp), part(which, 0, c_, None, cp), _peer(x_, y_, c_, q, 0))

        def from_neighbour(which, q, cp=None):
            return copy(sem(which, 0, q - 1, cp), part(which, q, c_, None, cp), me)

        def relay(which, q, cp=None):
            return copy(sem(which, 1, q - 1, cp), part(which, q, c_, q - 1, cp), _peer(x_, y_, c_, 3 - q, 0))

        def relayed(which, sub, cp=None):
            return copy(sem(which, 1, sub, cp), part(which, 3, c_, sub, cp), me)

        def to_sibling(which, q, cp=None):
            return copy(sem(which, 2, q - 1, cp), part(which, q, c_, None, cp), sibling)

        def from_sibling(which, q, cp=None):
            return copy(sem(which, 2, q - 1, cp), part(which, q, 1 - c_, None, cp), me)

        def relayed_to_sibling(which, sub, cp=None):
            return copy(sem(which, 3, sub, cp), part(which, 3, c_, sub, cp), sibling)

        def relayed_from_sibling(which, sub, cp=None):
            return copy(sem(which, 3, sub, cp), part(which, 3, 1 - c_, sub, cp), me)

        def pass_on_neighbours(which, cp=None):
            for q in (1, 2):
                from_neighbour(which, q, cp).wait_recv()
                to_sibling(which, q, cp).start()
                relay(which, q, cp).start()

        def pass_on_relayed(which, cp=None):
            for sub in range(2):
                relayed(which, sub, cp).wait_recv()
                relayed_to_sibling(which, sub, cp).start()

        def shard_load(k):
            c0, w = cols_of(*phases[k])
            return pltpu.make_async_copy(fi_ref.at[:, pl.ds(c0, w)], wbuf.at[k % 2, :, 0:w], load_sems.at[k % 2])

        class OutCopies:
            def __init__(self, k, slot, row0):
                c0, w = cols_of(*phases[k])
                strip = tm // OUT_STREAMS
                self.copies = [pltpu.make_async_copy(obuf.at[slot, n * strip:(n + 1) * strip, 0:w],
                                                     proj_ref.at[pl.ds(row0 + n * strip, strip), pl.ds(c0, w)],
                                                     out_sems.at[slot, n]) for n in range(OUT_STREAMS)]

            def start(self):
                for cp in self.copies:
                    cp.start()

            def wait(self):
                for cp in self.copies:
                    cp.wait()

        out_copy = OutCopies

        def drain(k):
            for j in range(min(2, nrow)):
                out_copy(k, (nrow - 1 - j) % 2, 0).wait()

        def arrivals(k):
            q, cp = phases[k]
            if k == 0:
                for cp_ in range(2):
                    for q_ in (1, 2):
                        to_neighbour(0, q_, cp_).start()
            elif q < 3 and k in (1, 3):
                pass_on_neighbours(0, cp)
                if k == 1:
                    for q_ in (1, 2):
                        to_neighbour(1, q_).start()
            elif k == 5:
                for cp_ in range(2):
                    pass_on_relayed(0, cp_)
                pass_on_neighbours(1)
            if q in (1, 2):
                from_sibling(0, q, cp).wait_recv()
            elif q == 3:
                for sub in range(2):
                    relayed_from_sibling(0, sub, cp).wait_recv()

        rows = pl.ds(pl.multiple_of(i * tm, tm), tm)
        slot = i % 2
        for k, (q, cp) in enumerate(phases):
            @pl.when(p == k)
            def _(k=k, q=q, cp=cp):
                @pl.when(i == 0)
                def _():
                    if k == 0:
                        arrivals(0)
                        shard_load(0).start()
                    else:
                        drain(k - 1)
                    shard_load(k).wait()

                if k + 1 < len(phases):
                    @pl.when(i == max(nrow - 2, 0))
                    def _():
                        arrivals(k + 1)
                        shard_load(k + 1).start()

                if k == 0:
                    wx = D // X_STREAMS
                    ssq = sum(jnp.sum(xr[...] * xr[...], axis=-1, keepdims=True) for xr in x_refs)
                    r = lax.rsqrt(ssq * (1.0 / D) + EPS)
                    for n, xr in enumerate(x_refs):
                        cols = slice(n * wx, (n + 1) * wx)
                        hv = ((xr[...] * r * g_ref[:, cols]) * (1.0 + sc_ref[:, cols]) + sh_ref[:, cols]).astype(BF16)
                        h_ref[:, cols] = hv
                        h_all[rows, cols] = hv

                @pl.when(i >= 2)
                def _():
                    out_copy(k, slot, 0).wait()

                w = cols_of(q, cp)[1]
                obuf[slot, :, 0:w] = jnp.dot(h_all[rows, :], wbuf[k % 2, :, 0:w], preferred_element_type=F32)
                out_copy(k, slot, pl.multiple_of(i * tm, tm)).start()

        @pl.when((p == len(phases) - 1) & (i == nrow - 1))
        def _():
            drain(len(phases) - 1)
            pass_on_relayed(1)
            for q in (1, 2):
                from_sibling(1, q).wait_recv()
            for sub in range(2):
                relayed_from_sibling(1, sub).wait_recv()
            for which, cps in ((0, (0, 1)), (1, (None,))):
                for cp in cps:
                    for q in (1, 2):
                        to_neighbour(which, q, cp).wait_send()
                        relay(which, q, cp).wait_send()
                        to_sibling(which, q, cp).wait_send()
                        relayed_to_sibling(which, q - 1, cp).wait_send()

    vec = pl.BlockSpec((1, D), lambda p, i, pos: (0, 0))
    first_phase_rows = lambda p, i, pos: (jnp.where(p == 0, i, nrow - 1), 0)
    anyspec = pl.BlockSpec(memory_space=pl.ANY)
    x_spec = lambda n: pl.BlockSpec((tm, D // X_STREAMS), lambda p, i, pos: (jnp.where(p == 0, i, nrow - 1), n))
    return pl.pallas_call(
        body, name="proj_gather",
        grid_spec=pltpu.PrefetchScalarGridSpec(
            num_scalar_prefetch=1, grid=(len(phases), nrow),
            in_specs=[x_spec(n) for n in range(X_STREAMS)] + [vec, vec, vec, anyspec, anyspec],
            out_specs=[pl.BlockSpec((tm, D), first_phase_rows), anyspec, anyspec, anyspec],
            scratch_shapes=[pltpu.VMEM((s, D), BF16), pltpu.VMEM((2, D, W_IN_SHARD), BF16), pltpu.VMEM((2, tm, W_IN_SHARD), F32),
                            pltpu.SemaphoreType.DMA((24,)), pltpu.SemaphoreType.DMA((24,)), pltpu.SemaphoreType.DMA((2,)),
                            pltpu.SemaphoreType.DMA((2, OUT_STREAMS))]),
        out_shape=[jax.ShapeDtypeStruct((s, D), BF16), jax.ShapeDtypeStruct((s, D_IN), F32),
                   jax.ShapeDtypeStruct((D, D_IN), BF16), jax.ShapeDtypeStruct((D, D), BF16)],
        input_output_aliases={X_STREAMS + 4: 2, X_STREAMS + 5: 3},
        compiler_params=_params("arbitrary", "arbitrary"),
    )(pos, *([x] * X_STREAMS), shift, scale, norm_g, wi_full, wo_full)


def _proj_specs(rev_nb=None):
    if rev_nb is None:
        row = lambda i: i
    else:
        row = lambda i: rev_nb - 1 - i
    wide = lambda col: pl.BlockSpec((BLK, D_A), lambda i: (row(i), col))
    kv = lambda col: pl.BlockSpec((BLK, D_KV), lambda i: (row(i), col))
    half = lambda col: pl.BlockSpec((BLK, 512), lambda i: (row(i), col))
    return [wide(0), wide(1), wide(2), wide(3), kv(OFF_K // D_KV), kv(OFF_V // D_KV), half(OFF_ZB // 512), half(OFF_ZB // 512 + 1)]


def _mix_fwd_call(proj, cos, sin, ln_g, ln_b, w_sp, b_sp_t, sinks):
    s = proj.shape[0]
    nb = s // BLK

    def body(ua_ref, va_ref, za_ref, q_ref, k_ref, v_ref, zb0_ref, zb1_ref, cos_ref, sin_ref, lg_ref, lb_ref,
             w_ref, bt_ref, sinks_ref, y_ref, probs_ref, ost_ref, psink_ref, kdup_ref, vdup_ref, qm_ref, bias_ref):
        i = pl.program_id(0)
        first_half, lo = _lane_masks()
        cos_t = cos_ref[...]
        sin_t = sin_ref[...]

        _, _, vln = _layer_norm_fwd(va_ref[...], lg_ref[...], lb_ref[...])
        tril = _tril()
        for g in range(GROUPS):
            cols = slice(g * BLK, (g + 1) * BLK)
            wg = jnp.where(tril, w_ref[g], 0.0).astype(BF16)
            sg = jnp.dot(wg, vln[:, cols].astype(BF16), preferred_element_type=F32) + bt_ref[:, g:g + 1]
            gate, _ = _silu_parts(za_ref[:, cols])
            y_ref[:, cols] = (ua_ref[:, cols] * sg * gate).astype(BF16)

        @pl.when(i == 0)
        def _():
            kdup_ref[:, 0:BLK, :] = jnp.zeros((N_KV, BLK, LANE), BF16)
            vdup_ref[:, 0:BLK, :] = jnp.zeros((N_KV, BLK, LANE), BF16)
            _band_bias(bias_ref)

        @pl.when(i > 0)
        def _():
            kdup_ref[:, 0:BLK, :] = kdup_ref[:, BLK:2 * BLK, :]
            vdup_ref[:, 0:BLK, :] = vdup_ref[:, BLK:2 * BLK, :]

        for ks in range(2):
            cols = slice(ks * LANE, (ks + 1) * LANE)
            kr = _rope(k_ref[:, cols], cos_t, sin_t, first_half)
            for n, (kd, vd) in enumerate(zip(_dup_kv(kr, lo), _dup_kv(v_ref[:, cols], lo))):
                kdup_ref[2 * ks + n, BLK:2 * BLK, :] = kd
                vdup_ref[2 * ks + n, BLK:2 * BLK, :] = vd
        for sb in range(8):
            _stack_heads(qm_ref, sb, _rope(q_ref[:, sb * LANE:(sb + 1) * LANE], cos_t, sin_t, first_half) * SCALE, lo, BF16)

        block_kind = jnp.where(i > 0, 1, 0)

        psink_ref[...] = jnp.zeros((Q_PER_KV * BLK, LANE), F32)
        lane_q = lax.broadcasted_iota(jnp.int32, (Q_PER_KV * BLK, LANE), 1)

        def kv_head(kh, carry):
            probs, psink = _softmax_sink(qm_ref[kh], kdup_ref[kh], bias_ref[block_kind], _sink_column(sinks_ref, kh))
            probs_ref[kh] = probs
            psink_ref[...] = jnp.where(lane_q == kh, psink, psink_ref[...])
            ost_ref[kh] = jnp.dot(probs.astype(BF16), vdup_ref[kh], preferred_element_type=F32)
            return carry

        lax.fori_loop(0, N_KV, kv_head, 0, unroll=2)
        for sb in range(8):
            cols = slice(sb * LANE, (sb + 1) * LANE)
            zb = zb0_ref[:, cols] if sb < 4 else zb1_ref[:, (sb - 4) * LANE:(sb - 3) * LANE]
            gate, _ = _silu_parts(zb)
            y_ref[:, D_A + sb * LANE:D_A + (sb + 1) * LANE] = (_unstack_heads(ost_ref, sb, lo) * gate).astype(BF16)

    tab = pl.BlockSpec((BLK, LANE), lambda i: (i, 0))
    return pl.pallas_call(
        body, name="mix_fwd", grid=(nb,),
        in_specs=_proj_specs() + [
            tab, tab, pl.BlockSpec((1, D_A), lambda i: (0, 0)), pl.BlockSpec((1, D_A), lambda i: (0, 0)),
            pl.BlockSpec((GROUPS, BLK, BLK), lambda i: (0, 0, 0)), pl.BlockSpec((BLK, GROUPS), lambda i: (0, 0)),
            pl.BlockSpec(memory_space=pltpu.SMEM)],
        out_specs=[pl.BlockSpec((BLK, 2 * D_A), lambda i: (i, 0)),
                   pl.BlockSpec((None, N_KV, Q_PER_KV * BLK, 2 * BLK), lambda i: (i, 0, 0, 0)),
                   pl.BlockSpec((None, N_KV, Q_PER_KV * BLK, LANE), lambda i: (i, 0, 0, 0)),
                   pl.BlockSpec((None, Q_PER_KV * BLK, LANE), lambda i: (i, 0, 0))],
        out_shape=[jax.ShapeDtypeStruct((s, 2 * D_A), BF16), jax.ShapeDtypeStruct((nb, N_KV, Q_PER_KV * BLK, 2 * BLK), F32),
                   jax.ShapeDtypeStruct((nb, N_KV, Q_PER_KV * BLK, LANE), F32), jax.ShapeDtypeStruct((nb, Q_PER_KV * BLK, LANE), F32)],
        scratch_shapes=[pltpu.VMEM((N_KV, 2 * BLK, LANE), BF16), pltpu.VMEM((N_KV, 2 * BLK, LANE), BF16),
                        pltpu.VMEM((N_KV, Q_PER_KV * BLK, LANE), BF16), pltpu.VMEM((2, Q_PER_KV * BLK, 2 * BLK), F32)],
        compiler_params=_params("arbitrary"),
    )(proj, proj, proj, proj, proj, proj, proj, proj, cos, sin, ln_g, ln_b, w_sp, b_sp_t, sinks)


def _tail_call(y, w_out_bf, x, target, gate, shift_f, scale_f, gf):
    s = x.shape[0]
    tm = min(s, 256)
    nsteps = s // tm

    def body(y_ref, w_ref, x_ref, t_ref, gate_ref, shf_ref, scf_ref, gf_ref, dx2_ref, do_ref, dy_ref, st_ref):
        i = pl.program_id(0)

        @pl.when(i == 0)
        def _():
            st_ref[...] = jnp.zeros((8, D), F32)

        o = jnp.dot(y_ref[...], w_ref[...], preferred_element_type=F32)
        gate_v = gate_ref[...]
        x2 = x_ref[...] + gate_v * o
        r2 = lax.rsqrt(jnp.mean(x2 * x2, axis=-1, keepdims=True) + EPS)
        xn2 = x2 * r2
        hn2 = xn2 * gf_ref[...]
        one_sc = 1.0 + scf_ref[...]
        err = hn2 * one_sc + shf_ref[...] - t_ref[...]
        dout = err * (1.0 / D)
        dhn2 = dout * one_sc
        dxn2 = dhn2 * gf_ref[...]
        dx2 = r2 * (dxn2 - xn2 * jnp.mean(dxn2 * xn2, axis=-1, keepdims=True))
        dx2_ref[...] = dx2
        do = (dx2 * gate_v).astype(BF16)
        do_ref[...] = do
        dy_ref[...] = lax.dot_general(do, w_ref[...], NT, preferred_element_type=F32)
        st_ref[0:1, :] += jnp.sum(dout, axis=0, keepdims=True)
        st_ref[1:2, :] += jnp.sum(dout * hn2, axis=0, keepdims=True)
        st_ref[2:3, :] += jnp.sum(dhn2 * xn2, axis=0, keepdims=True)
        st_ref[3:4, :] += jnp.sum(dx2 * o, axis=0, keepdims=True)
        st_ref[4:5, :] += jnp.sum(err * err, axis=0, keepdims=True)

        @pl.when(i == nsteps - 1)
        def _():
            st_ref[5:6, :] = jnp.full((1, D), 0.5 / D, F32) * jnp.sum(st_ref[4:5, :])

    vec = pl.BlockSpec((1, D), lambda i: (0, 0))
    rows = lambda: pl.BlockSpec((tm, D), lambda i: (i, 0))
    return pl.pallas_call(
        body, name="tail", grid=(nsteps,),
        in_specs=[rows(), pl.BlockSpec((D, D), lambda i: (0, 0)), rows(), rows(), vec, vec, vec, vec],
        out_specs=[rows(), rows(), rows(), pl.BlockSpec((8, D), lambda i: (0, 0))],
        out_shape=[jax.ShapeDtypeStruct((s, D), F32), jax.ShapeDtypeStruct((s, D), BF16), jax.ShapeDtypeStruct((s, D), F32),
                   jax.ShapeDtypeStruct((8, D), F32)],
        compiler_params=_params("arbitrary"),
    )(y, w_out_bf, x, target, gate, shift_f, scale_f, gf)


def _tn_call(a, b, name):
    s, m = a.shape
    n = b.shape[1]
    tn = 512
    ts = min(s, 1024)
    nk = s // ts

    def body(a_ref, b_ref, o_ref, acc_ref):
        k = pl.program_id(1)

        @pl.when(k == 0)
        def _():
            acc_ref[...] = jnp.zeros((m, tn), F32)

        acc_ref[...] += lax.dot_general(a_ref[...], b_ref[...], TN, preferred_element_type=F32)

        @pl.when(k == nk - 1)
        def _():
            o_ref[...] = acc_ref[...].astype(BF16)

    return pl.pallas_call(
        body, name=name, grid=(n // tn, nk),
        in_specs=[pl.BlockSpec((ts, m), lambda j, k: (k, 0)), pl.BlockSpec((ts, tn), lambda j, k: (k, j))],
        out_specs=pl.BlockSpec((m, tn), lambda j, k: (0, j)),
        out_shape=jax.ShapeDtypeStruct((m, n), BF16),
        scratch_shapes=[pltpu.VMEM((m, tn), F32)],
        compiler_params=_params("parallel", "arbitrary"),
    )(a, b)


def _tn_shards_call(pos, a, b, qs, name):
    s, m = a.shape
    ts = min(s, 1024)
    nk = s // ts

    def body(pos_ref, a_ref, b_ref, o_ref, acc_ref):
        k = pl.program_id(1)

        @pl.when(k == 0)
        def _():
            acc_ref[...] = jnp.zeros((m, W_IN_SHARD), F32)

        acc_ref[...] += lax.dot_general(a_ref[...], b_ref[...], TN, preferred_element_type=F32)

        @pl.when(k == nk - 1)
        def _():
            o_ref[...] = acc_ref[...].astype(BF16)

    def shard(j, pos):
        q = qs[0]
        for n in range(1, len(qs)):
            q = jnp.where(j == n, qs[n], q)
        return jnp.bitwise_xor(pos[0], q)

    return pl.pallas_call(
        body, name=name,
        grid_spec=pltpu.PrefetchScalarGridSpec(
            num_scalar_prefetch=1, grid=(len(qs), nk),
            in_specs=[pl.BlockSpec((ts, m), lambda j, k, pos: (k, 0)),
                      pl.BlockSpec((ts, W_IN_SHARD), lambda j, k, pos: (k, shard(j, pos)))],
            out_specs=pl.BlockSpec((m, W_IN_SHARD), lambda j, k, pos: (0, j)),
            scratch_shapes=[pltpu.VMEM((m, W_IN_SHARD), F32)]),
        out_shape=jax.ShapeDtypeStruct((m, len(qs) * W_IN_SHARD), BF16),
        compiler_params=_params("parallel", "arbitrary"),
    )(pos, a, b)


def _mix_bwd_call(proj, dy, probs, outs, psinks, cos, sin, ln_g, ln_b, w_sp, w_sp_t, b_sp_t):
    s = proj.shape[0]
    nb = s // BLK
    rev = lambda i: nb - 1 - i
    prev = lambda i: jnp.maximum(nb - 2 - i, 0)

    def body(ua_ref, va_ref, za_ref, q_ref, k_ref, v_ref, zb0_ref, zb1_ref, kp_ref, vp_ref, dy_ref,
             probs_ref, ost_ref, psink_ref, cos_ref, sin_ref, cosp_ref, sinp_ref, lg_ref, lb_ref, w_ref, wt_ref, bt_ref,
             dp_ref, lnst_ref, dw_ref, dbt_ref, dsink_ref,
             kdup_ref, vdup_ref, dvln_ref, qm_ref, dom_ref, dqst_ref, dkdup_ref, dvdup_ref, kcar_ref, vcar_ref, sigb_ref):
        i = pl.program_id(0)
        first_half, lo = _lane_masks()
        lane8 = lax.broadcasted_iota(jnp.int32, (8, LANE), 1)
        cos_t = cos_ref[...]
        sin_t = sin_ref[...]

        @pl.when(i == 0)
        def _():
            lnst_ref[...] = jnp.zeros((8, D_A), F32)
            dw_ref[...] = jnp.zeros((GROUPS, BLK, BLK), F32)
            dbt_ref[...] = jnp.zeros((BLK, LANE), F32)
            dsink_ref[...] = jnp.zeros((8, LANE), F32)
            kcar_ref[...] = jnp.zeros((BLK, D_KV), F32)
            vcar_ref[...] = jnp.zeros((BLK, D_KV), F32)

        vhat, rstd, vln = _layer_norm_fwd(va_ref[...], lg_ref[...], lb_ref[...])
        tril = _tril()
        triu = jnp.logical_not(tril) | (lax.broadcasted_iota(jnp.int32, (BLK, BLK), 0) == lax.broadcasted_iota(jnp.int32, (BLK, BLK), 1))
        lane_b = lax.broadcasted_iota(jnp.int32, (BLK, LANE), 1)
        db_acc = jnp.zeros((BLK, LANE), F32)
        for g in range(GROUPS):
            cols = slice(g * BLK, (g + 1) * BLK)
            vln_g = vln[:, cols].astype(BF16)
            wg = jnp.where(tril, w_ref[g], 0.0).astype(BF16)
            sg = jnp.dot(wg, vln_g, preferred_element_type=F32) + bt_ref[:, g:g + 1]
            za = za_ref[:, cols]
            gate, sig = _silu_parts(za)
            ua = ua_ref[:, cols]
            dya_g = dy_ref[:, cols]
            dya = dya_g * gate
            dp_ref[:, cols] = (dya * sg).astype(BF16)
            dp_ref[:, 2 * D_A + g * BLK:2 * D_A + (g + 1) * BLK] = (
                dya_g * (ua * sg) * (sig * (1.0 + za * (1.0 - sig)))).astype(BF16)
            ds = dya * ua
            ds_b = ds.astype(BF16)
            wtg = jnp.where(triu, wt_ref[g], 0.0).astype(BF16)
            dvln_ref[:, cols] = jnp.dot(wtg, ds_b, preferred_element_type=F32)
            dw_ref[g] += jnp.where(tril, lax.dot_general(ds_b, vln_g, NT, preferred_element_type=F32), 0.0)
            db_acc = db_acc + jnp.where(lane_b == g, jnp.sum(ds, axis=-1, keepdims=True), 0.0)
        dbt_ref[...] += db_acc
        dvln = dvln_ref[...]
        lnst_ref[0:1, :] += jnp.sum(dvln * vhat, axis=0, keepdims=True)
        lnst_ref[1:2, :] += jnp.sum(dvln, axis=0, keepdims=True)
        dvhat = dvln * lg_ref[...]
        m1 = jnp.mean(dvhat, axis=-1, keepdims=True)
        m2 = jnp.mean(dvhat * vhat, axis=-1, keepdims=True)
        dp_ref[:, D_A:2 * D_A] = (rstd * (dvhat - m1 - vhat * m2)).astype(BF16)

        cosp = cosp_ref[...]
        sinp = sinp_ref[...]
        for ks in range(2):
            cols = slice(ks * LANE, (ks + 1) * LANE)
            kr = _rope(k_ref[:, cols], cos_t, sin_t, first_half)
            kpr = _rope(kp_ref[:, cols], cosp, sinp, first_half)
            for n, (kc, vc, kp, vp) in enumerate(zip(_dup_kv(kr, lo), _dup_kv(v_ref[:, cols], lo),
                                                     _dup_kv(kpr, lo), _dup_kv(vp_ref[:, cols], lo))):
                kdup_ref[2 * ks + n, BLK:2 * BLK, :] = kc
                vdup_ref[2 * ks + n, BLK:2 * BLK, :] = vc
                kdup_ref[2 * ks + n, 0:BLK, :] = kp
                vdup_ref[2 * ks + n, 0:BLK, :] = vp
        for sb in range(8):
            cols = slice(sb * LANE, (sb + 1) * LANE)
            _stack_heads(qm_ref, sb, _rope(q_ref[:, cols], cos_t, sin_t, first_half) * SCALE, lo, BF16)
            zb = zb0_ref[:, cols] if sb < 4 else zb1_ref[:, (sb - 4) * LANE:(sb - 3) * LANE]
            gate, sig = _silu_parts(zb)
            sigb_ref[:, cols] = sig
            _stack_heads(dom_ref, sb, dy_ref[:, D_A + sb * LANE:D_A + (sb + 1) * LANE] * gate, lo, F32)

        lane_q = lax.broadcasted_iota(jnp.int32, (Q_PER_KV * BLK, LANE), 1)

        def kv_head(kh, dsink_acc):
            qm = qm_ref[kh]
            kd = kdup_ref[kh]
            vd = vdup_ref[kh]
            probs = probs_ref[kh]
            psink = jnp.sum(jnp.where(lane_q == kh, psink_ref[...], 0.0), axis=-1, keepdims=True)
            probs_b = probs.astype(BF16)
            o = ost_ref[kh]
            dom = dom_ref[kh]
            dom_b = dom.astype(BF16)
            delta = jnp.sum(dom * o, axis=-1, keepdims=True)
            dpr = lax.dot_general(dom_b, vd, NT, preferred_element_type=F32)
            dss = (probs * (dpr - delta)).astype(BF16)
            sd = psink * delta
            for n in range(Q_PER_KV):
                dsink_acc = dsink_acc + jnp.where(lane8 == Q_PER_KV * kh + n, -jnp.sum(sd[n * BLK:(n + 1) * BLK]), 0.0)
            dqst_ref[kh] = jnp.dot(dss, kd, preferred_element_type=F32)
            dkdup_ref[kh] = lax.dot_general(dss, qm, TN, preferred_element_type=F32)
            dvdup_ref[kh] = lax.dot_general(probs_b, dom_b, TN, preferred_element_type=F32)
            return dsink_acc

        dsink_acc = lax.fori_loop(0, N_KV // 2, lambda j, acc: kv_head(2 * j + 1, kv_head(2 * j, acc)), jnp.zeros((8, LANE), F32))
        row0 = lax.broadcasted_iota(jnp.int32, (8, LANE), 0) == 0
        dsink_ref[...] += jnp.where(row0, dsink_acc, 0.0)

        for sb in range(8):
            cols = slice(sb * LANE, (sb + 1) * LANE)
            zb = zb0_ref[:, cols] if sb < 4 else zb1_ref[:, (sb - 4) * LANE:(sb - 3) * LANE]
            sig = sigb_ref[:, cols]
            dyb = dy_ref[:, D_A + sb * LANE:D_A + (sb + 1) * LANE]
            dp_ref[:, OFF_ZB + sb * LANE:OFF_ZB + (sb + 1) * LANE] = (
                dyb * _unstack_heads(ost_ref, sb, lo) * (sig * (1.0 + zb * (1.0 - sig)))).astype(BF16)
            dq_r = _unstack_heads(dqst_ref, sb, lo) * SCALE
            dp_ref[:, OFF_Q + sb * LANE:OFF_Q + (sb + 1) * LANE] = _unrope(dq_r, cos_t, sin_t, first_half).astype(BF16)

        lo2 = lax.broadcasted_iota(jnp.int32, (2 * BLK, LANE), 1) < HEAD
        for ks in range(2):
            cols = slice(ks * LANE, (ks + 1) * LANE)
            ka = dkdup_ref[2 * ks]
            kb = dkdup_ref[2 * ks + 1]
            dk_band = jnp.where(lo2, ka + pltpu.roll(ka, HEAD, 1), kb + pltpu.roll(kb, HEAD, 1))
            va_ = dvdup_ref[2 * ks]
            vb_ = dvdup_ref[2 * ks + 1]
            dv_band = jnp.where(lo2, va_ + pltpu.roll(va_, HEAD, 1), vb_ + pltpu.roll(vb_, HEAD, 1))
            dkr = dk_band[BLK:2 * BLK, :] + kcar_ref[:, cols]
            dp_ref[:, OFF_K + ks * LANE:OFF_K + (ks + 1) * LANE] = _unrope(dkr, cos_t, sin_t, first_half).astype(BF16)
            dp_ref[:, OFF_V + ks * LANE:OFF_V + (ks + 1) * LANE] = (
                dv_band[BLK:2 * BLK, :] + vcar_ref[:, cols]).astype(BF16)
            kcar_ref[:, cols] = dk_band[0:BLK, :]
            vcar_ref[:, cols] = dv_band[0:BLK, :]

    tab = pl.BlockSpec((BLK, LANE), lambda i: (rev(i), 0))
    tabp = pl.BlockSpec((BLK, LANE), lambda i: (prev(i), 0))
    kvp = lambda col: pl.BlockSpec((BLK, D_KV), lambda i: (prev(i), col))
    vec = pl.BlockSpec((1, D_A), lambda i: (0, 0))
    w3 = pl.BlockSpec((GROUPS, BLK, BLK), lambda i: (0, 0, 0))
    return pl.pallas_call(
        body, name="mix_bwd", grid=(nb,),
        in_specs=_proj_specs(nb) + [
            kvp(OFF_K // D_KV), kvp(OFF_V // D_KV), pl.BlockSpec((BLK, 2 * D_A), lambda i: (rev(i), 0)),
            pl.BlockSpec((None, N_KV, Q_PER_KV * BLK, 2 * BLK), lambda i: (rev(i), 0, 0, 0)),
            pl.BlockSpec((None, N_KV, Q_PER_KV * BLK, LANE), lambda i: (rev(i), 0, 0, 0)),
            pl.BlockSpec((None, Q_PER_KV * BLK, LANE), lambda i: (rev(i), 0, 0)),
            tab, tab, tabp, tabp, vec, vec, w3, w3, pl.BlockSpec((BLK, GROUPS), lambda i: (0, 0))],
        out_specs=[pl.BlockSpec((BLK, D_IN), lambda i: (rev(i), 0)), pl.BlockSpec((8, D_A), lambda i: (0, 0)), w3,
                   pl.BlockSpec((BLK, LANE), lambda i: (0, 0)), pl.BlockSpec((8, LANE), lambda i: (0, 0))],
        out_shape=[jax.ShapeDtypeStruct((s, D_IN), BF16), jax.ShapeDtypeStruct((8, D_A), F32),
                   jax.ShapeDtypeStruct((GROUPS, BLK, BLK), F32), jax.ShapeDtypeStruct((BLK, LANE), F32),
                   jax.ShapeDtypeStruct((8, LANE), F32)],
        scratch_shapes=[pltpu.VMEM((N_KV, 2 * BLK, LANE), BF16), pltpu.VMEM((N_KV, 2 * BLK, LANE), BF16),
                        pltpu.VMEM((BLK, D_A), F32), pltpu.VMEM((N_KV, Q_PER_KV * BLK, LANE), BF16),
                        pltpu.VMEM((N_KV, Q_PER_KV * BLK, LANE), F32), pltpu.VMEM((N_KV, Q_PER_KV * BLK, LANE), F32),
                        pltpu.VMEM((N_KV, 2 * BLK, LANE), F32), pltpu.VMEM((N_KV, 2 * BLK, LANE), F32),
                        pltpu.VMEM((BLK, D_KV), F32), pltpu.VMEM((BLK, D_KV), F32), pltpu.VMEM((BLK, D_B), F32)],
        compiler_params=_params("arbitrary"),
    )(proj, proj, proj, proj, proj, proj, proj, proj, proj, proj, dy, probs, outs, psinks, cos, sin, cos, sin, ln_g, ln_b,
      w_sp, w_sp_t, b_sp_t)


def _dh_call(dproj, w_bf, x, dx2, scale, norm_g):
    s = x.shape[0]
    tm = min(s, 512)
    tk = W_IN_SHARD
    nk = D_IN // tk

    def body(dp_ref, w_ref, x_ref, dx2_ref, sc_ref, g_ref, gx_ref, st_ref, acc_ref):
        i = pl.program_id(0)
        k = pl.program_id(1)

        @pl.when((i == 0) & (k == 0))
        def _():
            st_ref[...] = jnp.zeros((8, D), F32)

        @pl.when(k == 0)
        def _():
            acc_ref[...] = jnp.zeros((tm, D), F32)

        acc_ref[...] += lax.dot_general(dp_ref[...], w_ref[...], NT, preferred_element_type=F32)

        @pl.when(k == nk - 1)
        def _():
            g = g_ref[...]
            one_sc = 1.0 + sc_ref[...]

            def chunk(n, carry):
                rows = pl.ds(pl.multiple_of(n * BLK, BLK), BLK)
                dh = acc_ref[rows, :]
                xv = x_ref[rows, :]
                r = lax.rsqrt(jnp.mean(xv * xv, axis=-1, keepdims=True) + EPS)
                xn = xv * r
                dhn = dh * one_sc
                dxn = dhn * g
                gx_ref[rows, :] = dx2_ref[rows, :] + r * (dxn - xn * jnp.mean(dxn * xn, axis=-1, keepdims=True))
                st_ref[0:1, :] += jnp.sum(dh, axis=0, keepdims=True)
                st_ref[1:2, :] += jnp.sum(dh * (xn * g), axis=0, keepdims=True)
                st_ref[2:3, :] += jnp.sum(dhn * xn, axis=0, keepdims=True)
                return carry

            lax.fori_loop(0, tm // BLK, chunk, 0)

    vec = pl.BlockSpec((1, D), lambda i, k: (0, 0))
    rows = lambda: pl.BlockSpec((tm, D), lambda i, k: (i, 0))
    return pl.pallas_call(
        body, name="dh", grid=(s // tm, nk),
        in_specs=[pl.BlockSpec((tm, tk), lambda i, k: (i, k)), pl.BlockSpec((D, tk), lambda i, k: (0, k)), rows(), rows(), vec, vec],
        out_specs=[rows(), pl.BlockSpec((8, D), lambda i, k: (0, 0))],
        out_shape=[jax.ShapeDtypeStruct((s, D), F32), jax.ShapeDtypeStruct((8, D), F32)],
        scratch_shapes=[pltpu.VMEM((tm, D), F32)],
        compiler_params=_params("arbitrary", "arbitrary"),
    )(dproj, w_bf, x, dx2, scale, norm_g)


def _adam_math(w, g, m, v):
    m_new = ADAM_B1 * m + (1.0 - ADAM_B1) * g
    v_new = ADAM_B2 * v + (1.0 - ADAM_B2) * (g * g)
    m_hat = m_new / ADAM_C1
    v_hat = v_new / ADAM_C2
    delta = -ADAM_LR * (m_hat / (jnp.sqrt(v_hat) + ADAM_EPS) + ADAM_WD * w)
    return delta, m_new, v_new


def _adam_small_call(tensors):
    n = len(tensors)

    def body(*refs):
        ins, outs = refs[:4 * n], refs[4 * n:]
        for t in range(n):
            w_ref, g_ref, m_ref, v_ref = ins[4 * t:4 * t + 4]
            d, mo, vo = _adam_math(w_ref[...], g_ref[...], m_ref[...], v_ref[...])
            outs[3 * t][...], outs[3 * t + 1][...], outs[3 * t + 2][...] = d, mo, vo

    vm = pl.BlockSpec(memory_space=pltpu.VMEM)
    flat = [a for t in tensors for a in t]
    out = pl.pallas_call(
        body, name="adam_small", in_specs=[vm] * (4 * n), out_specs=[vm] * (3 * n),
        out_shape=[jax.ShapeDtypeStruct(t[0].shape, F32) for t in tensors for _ in range(3)],
        compiler_params=pltpu.CompilerParams(vmem_limit_bytes=VMEM_LIMIT),
    )(*flat)
    return [tuple(out[3 * t:3 * t + 3]) for t in range(n)]


def _adam_halves_call(pos, w, mine, theirs, m, v, name):
    r, n = w.shape
    half = r // 2
    tr = ADAM_ROWS
    nh = half // tr

    def body(pos_ref, w_ref, mine_ref, theirs_ref, m_ref, v_ref, g_ref, d_ref, mo_ref, vo_ref):
        is_mine = (pl.program_id(0) // nh) == pos_ref[1]
        g = jnp.where(is_mine, mine_ref[...], theirs_ref[...])
        g_ref[...] = g
        d_ref[...], mo_ref[...], vo_ref[...] = _adam_math(w_ref[...], g, m_ref[...], v_ref[...])

    spec = lambda: pl.BlockSpec((tr, n), lambda i, pos: (i, 0))
    hspec = lambda: pl.BlockSpec((tr, n), lambda i, pos: (i % nh, 0))
    return pl.pallas_call(
        body, name=name,
        grid_spec=pltpu.PrefetchScalarGridSpec(
            num_scalar_prefetch=1, grid=(r // tr,), in_specs=[spec(), hspec(), hspec(), spec(), spec()],
            out_specs=[spec() for _ in range(4)]),
        out_shape=[jax.ShapeDtypeStruct((r, n), F32)] * 4, compiler_params=_params("parallel"),
    )(pos, w, mine, theirs, m, v)


def _adam_outer_call(w, ct, dm, m, v, name):
    r, n = w.shape
    tr = ADAM_ROWS

    def body(w_ref, ct_ref, dm_ref, m_ref, v_ref, g_ref, d_ref, mo_ref, vo_ref):
        g = ct_ref[:, 0:1] * dm_ref[0:1, :]
        for b in range(1, N_DEV):
            g = g + ct_ref[:, b:b + 1] * dm_ref[b:b + 1, :]
        g_ref[...] = g
        d_ref[...], mo_ref[...], vo_ref[...] = _adam_math(w_ref[...], g, m_ref[...], v_ref[...])

    spec = lambda: pl.BlockSpec((tr, n), lambda i: (i, 0))
    return pl.pallas_call(
        body, name=name, grid=(r // tr,),
        in_specs=[spec(), pl.BlockSpec((tr, N_DEV), lambda i: (i, 0)), pl.BlockSpec((N_DEV, n), lambda i: (0, 0)), spec(), spec()],
        out_specs=[spec() for _ in range(4)],
        out_shape=[jax.ShapeDtypeStruct((r, n), F32)] * 4, compiler_params=_params("parallel"),
    )(w, ct, dm, m, v)


def _sum_pieces_call(pos, part, part_block, recvs, name):
    r, n = recvs[0].shape[1:]
    tr = min(r, 256)
    nrb = r // tr

    def body(pos_ref, p_ref, *refs):
        acc = p_ref[...].astype(F32)
        for r_ref in refs[:-1]:
            for d in range(r_ref.shape[0]):
                acc = acc + r_ref[d].astype(F32)
        refs[-1][...] = acc

    return pl.pallas_call(
        body, name=name,
        grid_spec=pltpu.PrefetchScalarGridSpec(
            num_scalar_prefetch=1, grid=(nrb,),
            in_specs=[pl.BlockSpec((tr, n), lambda i, pos: part_block(i, pos, nrb))] + [
                pl.BlockSpec((rv.shape[0], tr, n), lambda i, pos: (0, i, 0)) for rv in recvs],
            out_specs=pl.BlockSpec((tr, n), lambda i, pos: (i, 0))),
        out_shape=jax.ShapeDtypeStruct((r, n), F32), compiler_params=_params("parallel"),
    )(pos, part, *recvs)


def _coords():
    return lax.axis_index("x"), lax.axis_index("y"), lax.axis_index("c")


def _allgather_sum_call(blk, name, with_sum):
    m_per, n = blk.shape

    def body(x_ref, out_ref, *rest):
        if with_sum:
            sum_ref, send_sems, recv_sems, local_sem = rest
        else:
            send_sems, recv_sems, local_sem = rest
        x, y, c = _coords()
        me, sibling = (x, y, c), (x, y, 1 - c)
        chips = [(1 - x, y), (x, 1 - y), (1 - x, 1 - y)]

        def rows(px, py, pc):
            return out_ref.at[pl.ds((4 * px + 2 * py + pc) * m_per, m_per), :]

        def copy(k, block, to, src=None):
            return pltpu.make_async_remote_copy(
                src_ref=rows(*block) if src is None else src, dst_ref=rows(*block),
                send_sem=send_sems.at[k], recv_sem=recv_sems.at[k], device_id=to, device_id_type=MESH)

        mine = pltpu.make_async_copy(x_ref, rows(*me), local_sem)
        mine.start()
        first = [copy(0, me, sibling, src=x_ref)]
        first += [copy(1 + j, me, (*chip, c), src=x_ref) for j, chip in enumerate(chips)]
        for cp in first:
            cp.start()
        passed = [copy(4 + j, (*chip, c), sibling) for j, chip in enumerate(chips)]
        for j, chip in enumerate(chips):
            copy(1 + j, (*chip, c), me).wait_recv()
            passed[j].start()
        copy(0, sibling, me).wait_recv()
        for j, chip in enumerate(chips):
            copy(4 + j, (*chip, 1 - c), me).wait_recv()
        for cp in first + passed:
            cp.wait_send()
        mine.wait()
        if with_sum:
            acc = out_ref[0:m_per, :]
            for d in range(1, N_DEV):
                acc = acc + out_ref[d * m_per:(d + 1) * m_per, :]
            sum_ref[...] = acc

    vm = pl.BlockSpec(memory_space=pltpu.VMEM)
    out_shape = [jax.ShapeDtypeStruct((N_DEV * m_per, n), F32)]
    if with_sum:
        out_shape.append(jax.ShapeDtypeStruct((m_per, n), F32))
    return pl.pallas_call(
        body, name=name, out_shape=out_shape, in_specs=[vm], out_specs=[vm] * len(out_shape),
        scratch_shapes=[pltpu.SemaphoreType.DMA((7,)), pltpu.SemaphoreType.DMA((7,)), pltpu.SemaphoreType.DMA],
        compiler_params=pltpu.CompilerParams(vmem_limit_bytes=VMEM_LIMIT),
    )(blk)


HBM_SPEC = pl.BlockSpec(memory_space=pltpu.HBM)
SEM_SPEC = pl.BlockSpec(memory_space=pltpu.SEMAPHORE)
SIDE_EFFECT = pltpu.SideEffectType.DATAFLOW_SIDE_EFFECTING


def _peer(x, y, c, q, cb):
    return (1 - x if q & 2 else x, 1 - y if q & 1 else y, 1 - c if cb else c)


def _w_in_piece(slots):
    def piece(part_ref, k, to):
        return part_ref.at[pl.ds(to[2] * (D // 2), D // 2), pl.ds(slots[k] * W_IN_SHARD, W_IN_SHARD)]
    return piece


def _w_out_piece(part_ref, k, to):
    ho = W_OUT_SHARD // 2
    return part_ref.at[pl.ds((2 * to[0] + to[1]) * W_OUT_SHARD + to[2] * ho, ho), :]


def _group_piece(part_ref, k, to):
    return part_ref.at[4 * to[0] + 2 * to[1] + to[2]]


def _whole_piece(part_ref, k, to):
    return part_ref


def _exchange_start_call(groups, name):
    ng = len(groups)
    lands = [lax.empty((len(rels),) + slot_shape, part.dtype) for part, rels, _, slot_shape in groups]

    def body(*refs):
        ins, outs = refs[:2 * ng], refs[2 * ng:]
        x, y, c = _coords()
        for g, (_, rels, piece, _) in enumerate(groups):
            part_ref, land_ref = ins[2 * g], ins[2 * g + 1]
            send_sems, recv_sems = outs[4 * g], outs[4 * g + 1]
            for k, (q, cb) in enumerate(rels):
                to = _peer(x, y, c, q, cb)
                pltpu.make_async_remote_copy(src_ref=piece(part_ref, k, to), dst_ref=land_ref.at[k], send_sem=send_sems.at[k],
                                             recv_sem=recv_sems.at[k], device_id=to, device_id_type=MESH).start()
        outs[-1][...] = jnp.zeros_like(outs[-1])

    out_shape, out_specs, operands = [], [], []
    for (part, rels, _, _), land in zip(groups, lands):
        n = len(rels)
        out_shape += [pltpu.SemaphoreType.DMA((n,)), pltpu.SemaphoreType.DMA((n,)), pltpu.HBM(part.shape, part.dtype),
                      pltpu.HBM(land.shape, land.dtype)]
        out_specs += [SEM_SPEC, SEM_SPEC, HBM_SPEC, HBM_SPEC]
        operands += [pltpu.with_memory_space_constraint(part, pltpu.HBM), pltpu.with_memory_space_constraint(land, pltpu.HBM)]
    out = pl.pallas_call(
        body, name=name,
        out_shape=tuple(out_shape) + (jax.ShapeDtypeStruct((1, 1), F32),),
        in_specs=(HBM_SPEC,) * (2 * ng), out_specs=tuple(out_specs) + (pl.BlockSpec(memory_space=pltpu.VMEM),),
        input_output_aliases={j: 4 * (j // 2) + 2 + j % 2 for j in range(2 * ng)},
        compiler_params=pltpu.CompilerParams(has_side_effects=SIDE_EFFECT),
    )(*operands)
    return [tuple(out[4 * g:4 * g + 4]) for g in range(ng)], out[-1]


def _exchange_wait_call(started, groups, after, name):
    ng = len(groups)

    def body(*refs):
        ins = refs[:4 * ng]
        x, y, c = _coords()
        for g, (_, rels, piece, _) in enumerate(groups):
            part_ref, land_ref, send_sems, recv_sems = ins[4 * g:4 * g + 4]
            for k, (q, cb) in enumerate(rels):
                to = _peer(x, y, c, q, cb)
                cp = pltpu.make_async_remote_copy(src_ref=piece(part_ref, k, to), dst_ref=land_ref.at[k], send_sem=send_sems.at[k],
                                                  recv_sem=recv_sems.at[k], device_id=to, device_id_type=MESH)
                cp.wait_send()
                cp.wait_recv()

    operands, in_specs, out_shape = [], [], []
    for send_sems, recv_sems, part_thru, land_thru in started:
        operands += [part_thru, land_thru, send_sems, recv_sems]
        in_specs += [HBM_SPEC, HBM_SPEC, SEM_SPEC, SEM_SPEC]
        out_shape += [pltpu.HBM(part_thru.shape, part_thru.dtype), pltpu.HBM(land_thru.shape, land_thru.dtype)]
    out = pl.pallas_call(
        body, name=name, out_shape=tuple(out_shape),
        in_specs=tuple(in_specs) + (pl.BlockSpec(memory_space=pl.ANY),), out_specs=(HBM_SPEC,) * (2 * ng),
        input_output_aliases={4 * g + j: 2 * g + j for g in range(ng) for j in range(2)},
        compiler_params=pltpu.CompilerParams(has_side_effects=SIDE_EFFECT),
    )(*operands, after)
    return [tuple(out[2 * g:2 * g + 2]) for g in range(ng)]


def _rope_tables(s):
    inv_freq = 10000.0 ** (-jnp.arange(0, HEAD, 2, dtype=F32) / HEAD)
    ang = jnp.arange(s, dtype=F32)[:, None] * inv_freq[None, :]
    cos = jnp.tile(jnp.cos(ang), (1, LANE // (HEAD // 2)))
    sin = jnp.tile(jnp.sin(ang), (1, LANE // (HEAD // 2)))
    first_half = (jnp.arange(LANE) % HEAD) < (HEAD // 2)
    return cos, jnp.where(first_half[None, :], -sin, sin)


def kernel(x, c, w_ada, b_ada, norm_g, w_in, ln_v_g, ln_v_b, w_spatial, b_spatial, sinks, w_out, w_ada_final, b_ada_final, final_norm_g, loss_target, m_w_ada, m_b_ada, m_norm_g, m_w_in, m_ln_v_g, m_ln_v_b, m_w_spatial, m_b_spatial, m_sinks, m_w_out, m_w_ada_final, m_b_ada_final, m_final_norm_g, v_w_ada, v_b_ada, v_norm_g, v_w_in, v_ln_v_g, v_ln_v_b, v_w_spatial, v_b_spatial, v_sinks, v_w_out, v_w_ada_final, v_b_ada_final, v_final_norm_g):
    s = x.shape[1]
    ax, ay, ac = _coords()
    chip = 2 * ax + ay
    me = 4 * ax + 2 * ay + ac
    n_ada = w_ada.shape[2]
    n_adaf = w_ada_final.shape[1]

    x2d = x.reshape(s, D)
    tgt = loss_target.reshape(s, D)
    w_ada2, w_in2, w_out2 = w_ada[0], w_in[0], w_out[0]
    b_ada_f2 = b_ada_final.reshape(1, 2 * D)
    gf = final_norm_g.reshape(1, D)

    c_all = _allgather_sum_call(jnp.pad(c, ((0, 7), (0, 0))), "gather_c", False)[0][::8]
    mod_p, c_act = _rowmat_call(c_all, w_ada2, lax.dynamic_slice(b_ada, (0, chip * n_ada), (1, n_ada)), "mod")
    modf_p, _ = _rowmat_call(c_all, w_ada_final, lax.dynamic_slice(b_ada_f2, (0, chip * n_adaf), (1, n_adaf)), "mod_final")
    mods = _allgather_sum_call(jnp.concatenate([mod_p, modf_p], axis=1), "gather_mod", False)[0]
    my_rows = [lax.dynamic_slice(mods, (16 * j + me, 0), (1, n_ada + n_adaf)) for j in range(N_CHIP)]
    mod = jnp.concatenate([r[:, :n_ada] for r in my_rows], axis=1)
    mod_f = jnp.concatenate([r[:, n_ada:] for r in my_rows], axis=1)
    shift, scale, gate = mod[:, :D], mod[:, D:2 * D], mod[:, 2 * D:]
    shift_f, scale_f = mod_f[:, :D], mod_f[:, D:]

    pos = jnp.stack([chip, ac]).astype(jnp.int32)
    w_in_own = _cast_into_call(pos, w_in2, (D, D_IN), "cast_w_in")
    w_out_own = _cast_into_call(pos, w_out2, (D, D), "cast_w_out")

    cos, sin = _rope_tables(s)
    b_sp_t = b_spatial[0].T
    sinks1 = sinks.reshape(N_Q)
    h, proj, w_in_bf, w_out_bf = _proj_gather_call(pos, x2d, shift, scale, norm_g, w_in_own, w_out_own)
    y, probs, attn_out, psinks = _mix_fwd_call(proj, cos, sin, ln_v_g, ln_v_b, w_spatial[0], b_sp_t, sinks1)
    dx2, do, dy, st_tail = _tail_call(y, w_out_bf, x2d, tgt, gate, shift_f, scale_f, gf)

    rel_o = [(0, 1), (1, 0), (1, 1), (2, 0), (2, 1), (3, 0), (3, 1)]
    rel_a = [(1, 0), (1, 1), (2, 0), (2, 1)]
    rel_b = [(3, 0), (3, 1), (0, 1)]
    piece_a, piece_b = _w_in_piece([0, 0, 1, 1]), _w_in_piece([0, 0, 1])
    half_in, half_out = (D // 2, W_IN_SHARD), (W_OUT_SHARD // 2, D)

    g_w_out_p = _tn_call(y, do, "grad_w_out")
    grp_o = [(g_w_out_p, rel_o, _w_out_piece, half_out)]
    st_o, tok_o = _exchange_start_call(grp_o, "send_w_out")
    dproj, st_ln, d_wsp, d_bsp_t, d_sink = _mix_bwd_call(
        proj, dy, probs, attn_out, psinks, cos, sin, ln_v_g + tok_o, ln_v_b, w_spatial[0], jnp.swapaxes(w_spatial[0], 1, 2),
        b_sp_t)
    g_w_in_a = _tn_shards_call(pos, h, dproj, (1, 2), "grad_w_in_a")
    grp_a = [(g_w_in_a, rel_a, piece_a, half_in), (d_wsp, rel_o, _group_piece, (BLK, BLK))]
    st_a, tok_a = _exchange_start_call(grp_a, "send_w_in_a")
    g_w_in_b = _tn_shards_call(pos, h, dproj, (3, 0), "grad_w_in_b")
    grp_b = [(g_w_in_b, rel_b, piece_b, half_in)]
    st_b, tok_b = _exchange_start_call(grp_b, "send_w_in_b")
    grad_x, st_dh = _dh_call(dproj, w_in_bf, x2d, dx2, scale + (tok_a + tok_b), norm_g)

    ((g_w_out_p, recv_o),) = _exchange_wait_call(st_o, grp_o, st_dh, "wait_w_out")
    (_, recv_a), (d_wsp, recv_s) = _exchange_wait_call(st_a, grp_a, st_dh, "wait_w_in_a")
    ((g_w_in_b, recv_b),) = _exchange_wait_call(st_b, grp_b, st_dh, "wait_w_in_b")
    mine_in = _sum_pieces_call(pos, g_w_in_b, lambda i, p, nrb: (p[1] * nrb + i, 1), [recv_a, recv_b], "sum_w_in")
    mine_out = _sum_pieces_call(pos, g_w_out_p, lambda i, p, nrb: ((2 * p[0] + p[1]) * nrb + i, 0), [recv_o], "sum_w_out")
    wsp_group = _sum_pieces_call(pos, d_wsp.reshape(GROUPS * BLK, BLK), lambda i, p, nrb: (2 * p[0] + p[1], 0), [recv_s],
                                 "sum_w_spatial")
    to_sibling = [(0, 1)]
    grp_p = [(mine_in, to_sibling, _whole_piece, half_in), (mine_out, to_sibling, _whole_piece, half_out)]
    st_p, tok_p = _exchange_start_call(grp_p, "swap_halves")

    misc = jnp.concatenate([st_ln, d_bsp_t[:, :GROUPS].T, d_sink, jnp.zeros((8, D - D_A - 2 * LANE), F32)], axis=1)
    pack = jnp.concatenate([wsp_group.reshape(8, D) + tok_p, st_tail, st_dh, misc], axis=0)
    rows = pack.shape[0]
    packs, tot = _allgather_sum_call(pack, "gather_small", True)
    packs = packs.reshape(N_DEV, rows, D)
    dmod_all = jnp.concatenate([packs[:, 16, :], packs[:, 17, :], packs[:, 11, :]], axis=1)
    dmodf_all = jnp.concatenate([packs[:, 8, :], packs[:, 9, :]], axis=1)
    loss = tot[13, 0]
    (mine_in, theirs_in), (mine_out, theirs_out) = _exchange_wait_call(st_p, grp_p, tot, "swapped_halves")
    small = {
        "b_ada": jnp.concatenate([tot[16:17], tot[17:18], tot[11:12]], axis=1),
        "norm_g": tot[18:19],
        "ln_v_g": tot[24:25, :D_A],
        "ln_v_b": tot[25:26, :D_A],
        "w_spatial": packs[:, 0:8, :].reshape(GROUPS * BLK, BLK),
        "b_spatial": tot[24:32, D_A:D_A + BLK],
        "sinks": tot[24:25, D_A + LANE:D_A + LANE + N_Q],
        "b_ada_final": jnp.concatenate([tot[8:9], tot[9:10]], axis=1),
        "final_norm_g": tot[10:11],
    }

    weights = dict(w_ada=w_ada, b_ada=b_ada, norm_g=norm_g, w_in=w_in, ln_v_g=ln_v_g, ln_v_b=ln_v_b, w_spatial=w_spatial,
                   b_spatial=b_spatial, sinks=sinks, w_out=w_out, w_ada_final=w_ada_final, b_ada_final=b_ada_final,
                   final_norm_g=final_norm_g)
    m_in = dict(w_ada=m_w_ada, b_ada=m_b_ada, norm_g=m_norm_g, w_in=m_w_in, ln_v_g=m_ln_v_g, ln_v_b=m_ln_v_b,
                w_spatial=m_w_spatial, b_spatial=m_b_spatial, sinks=m_sinks, w_out=m_w_out, w_ada_final=m_w_ada_final,
                b_ada_final=m_b_ada_final, final_norm_g=m_final_norm_g)
    v_in = dict(w_ada=v_w_ada, b_ada=v_b_ada, norm_g=v_norm_g, w_in=v_w_in, ln_v_g=v_ln_v_g, ln_v_b=v_ln_v_b,
                w_spatial=v_w_spatial, b_spatial=v_b_spatial, sinks=v_sinks, w_out=v_w_out, w_ada_final=v_w_ada_final,
                b_ada_final=v_b_ada_final, final_norm_g=v_final_norm_g)
    c_act_t = c_act.T
    outer = {"w_ada": lax.dynamic_slice(dmod_all, (0, chip * n_ada), (N_DEV, n_ada)),
             "w_ada_final": lax.dynamic_slice(dmodf_all, (0, chip * n_adaf), (N_DEV, n_adaf))}
    halves = {"w_in": (mine_in, theirs_in[0]), "w_out": (mine_out, theirs_out[0])}
    done = {}
    for name, (mine, theirs) in halves.items():
        shape2 = (2 * mine.shape[0], mine.shape[1])
        done[name] = _adam_halves_call(pos, weights[name].reshape(shape2), mine, theirs, m_in[name].reshape(shape2),
                                       v_in[name].reshape(shape2), "adam_" + name)
    for name, dm in outer.items():
        shape2 = (D, dm.shape[1])
        done[name] = _adam_outer_call(weights[name].reshape(shape2), c_act_t, dm, m_in[name].reshape(shape2),
                                      v_in[name].reshape(shape2), "adam_" + name)
    updates = _adam_small_call([(weights[name].reshape(g.shape), g, m_in[name].reshape(g.shape), v_in[name].reshape(g.shape))
                                for name, g in small.items()])
    for (name, g), upd in zip(small.items(), updates):
        done[name] = (g, *upd)
    outs = [[done[name][k].reshape(w.shape) for name, w in weights.items()] for k in range(4)]
    return (loss, grad_x.reshape(x.shape), *outs[0], *outs[1], *outs[2], *outs[3])
```

```python
import numpy as np
import jax
import jax.numpy as jnp
from jax import lax
from jax.experimental import pallas as pl
from jax.experimental.pallas import tpu as pltpu

F32 = jnp.float32
BF16 = jnp.bfloat16
MESH = pl.DeviceIdType.MESH

D = 2048
D_A = 1024
D_B = 1024
D_KV = 256
HEAD = 64
N_Q = 16
N_KV = 4
Q_PER_KV = N_Q // N_KV
BLK = 128
GROUPS = 8
D_IN = 5632
OFF_Q, OFF_K, OFF_V, OFF_ZB = 3072, 4096, 4352, 4608
N_CHIP = 4
N_DEV = 8
W_IN_SHARD = D_IN // N_CHIP
W_OUT_SHARD = D // N_CHIP
EPS = 1e-5
SCALE = HEAD ** -0.5
NEG = -1e30
LANE = 128
VMEM_LIMIT = 56 * 1024 * 1024

ADAM_LR, ADAM_B1, ADAM_B2, ADAM_EPS, ADAM_WD, ADAM_STEP = 0.001, 0.9, 0.999, 1e-08, 0.01, 10
ADAM_C1 = 1.0 - ADAM_B1 ** ADAM_STEP
ADAM_C2 = 1.0 - ADAM_B2 ** ADAM_STEP
ADAM_ROWS = 256

NT = (((1,), (1,)), ((), ()))
TN = (((0,), (0,)), ((), ()))


def _params(*sem):
    return pltpu.CompilerParams(dimension_semantics=sem, vmem_limit_bytes=VMEM_LIMIT)


def _silu_parts(z):
    sig = 1.0 / (1.0 + jnp.exp(-z))
    return z * sig, sig


def _swap_halves(v, first_half):
    return jnp.where(first_half, pltpu.roll(v, 96, 1), pltpu.roll(v, 32, 1))


def _rope(v, cos_t, sin_s, first_half):
    return v * cos_t + _swap_halves(v, first_half) * sin_s


def _unrope(dv, cos_t, sin_s, first_half):
    return dv * cos_t - _swap_halves(dv, first_half) * sin_s


def _lane_masks():
    lane = lax.broadcasted_iota(jnp.int32, (BLK, LANE), 1)
    return (lane % HEAD) < (HEAD // 2), lane < HEAD


def _band_valid(first_block_bound, rows=BLK):
    rr = lax.broadcasted_iota(jnp.int32, (rows, 2 * BLK), 0) & (BLK - 1)
    jj = lax.broadcasted_iota(jnp.int32, (rows, 2 * BLK), 1)
    return (jj > rr) & (jj <= rr + BLK) & (jj >= first_block_bound)


def _dup_kv(slab, lo):
    rolled = pltpu.roll(slab, HEAD, 1)
    return jnp.where(lo, slab, rolled).astype(BF16), jnp.where(lo, rolled, slab).astype(BF16)


def _stack_heads(ref, sb, slab, lo, dtype):
    kh, base = sb // 2, 2 * (sb % 2) * BLK
    zero = jnp.zeros_like(slab)
    ref[kh, base:base + BLK, :] = jnp.where(lo, slab, zero).astype(dtype)
    ref[kh, base + BLK:base + 2 * BLK, :] = jnp.where(lo, zero, slab).astype(dtype)


def _unstack_heads(ref, sb, lo):
    kh, base = sb // 2, 2 * (sb % 2) * BLK
    return jnp.where(lo, ref[kh, base:base + BLK, :], ref[kh, base + BLK:base + 2 * BLK, :])


def _sink_column(sinks_ref, kh):
    row = lax.broadcasted_iota(jnp.int32, (Q_PER_KV * BLK, 1), 0)
    col = jnp.full(row.shape, sinks_ref[Q_PER_KV * kh + Q_PER_KV - 1], F32)
    for n in range(Q_PER_KV - 2, -1, -1):
        col = jnp.where(row < (n + 1) * BLK, sinks_ref[Q_PER_KV * kh + n], col)
    return col


def _tril():
    t = lax.broadcasted_iota(jnp.int32, (BLK, BLK), 0)
    s = lax.broadcasted_iota(jnp.int32, (BLK, BLK), 1)
    return s <= t


def _layer_norm_fwd(va, lg, lb):
    mu = jnp.mean(va, axis=-1, keepdims=True)
    xc = va - mu
    rstd = lax.rsqrt(jnp.mean(xc * xc, axis=-1, keepdims=True) + EPS)
    vhat = xc * rstd
    return vhat, rstd, vhat * lg + lb


def _softmax_sink(qm, kdup, bias, sink):
    s = lax.dot_general(qm, kdup, NT, preferred_element_type=F32) + bias
    m = jnp.maximum(jnp.max(s, axis=-1, keepdims=True), sink)
    p = jnp.exp(s - m)
    esink = jnp.exp(sink - m)
    inv = 1.0 / (jnp.sum(p, axis=-1, keepdims=True) + esink)
    return p * inv, esink * inv


def _band_bias(bias_ref):
    rows = bias_ref.shape[1]
    bias_ref[0] = jnp.where(_band_valid(BLK, rows), 0.0, NEG)
    bias_ref[1] = jnp.where(_band_valid(0, rows), 0.0, NEG)


def _rowmat_call(c_all, w, b, name):
    n = w.shape[1]
    tn = 512

    def body(c_ref, w_ref, b_ref, o_ref, ca_ref):
        ca, _ = _silu_parts(c_ref[...])
        ca_ref[...] = ca
        o_ref[...] = jnp.dot(ca.astype(BF16), w_ref[...].astype(BF16), preferred_element_type=F32) + b_ref[...]

    return pl.pallas_call(
        body, name=name, grid=(n // tn,),
        in_specs=[pl.BlockSpec((N_DEV, D), lambda j: (0, 0)), pl.BlockSpec((D, tn), lambda j: (0, j)),
                  pl.BlockSpec((1, tn), lambda j: (0, j))],
        out_specs=[pl.BlockSpec((N_DEV, tn), lambda j: (0, j)), pl.BlockSpec((N_DEV, D), lambda j: (0, 0))],
        out_shape=[jax.ShapeDtypeStruct((N_DEV, n), F32), jax.ShapeDtypeStruct((N_DEV, D), F32)],
        compiler_params=_params("arbitrary"),
    )(c_all, w, b)


def _cast_into_call(pos, w, full_shape, name):
    r, n = w.shape
    tr = min(r, 512)
    by_cols = full_shape[0] == r
    nrb = r // tr

    def body(pos_ref, w_ref, o_ref):
        o_ref[...] = w_ref[...].astype(BF16)

    out_map = (lambda i, pos: (i, pos[0])) if by_cols else (lambda i, pos: (pos[0] * nrb + i, 0))
    return pl.pallas_call(
        body, name=name,
        grid_spec=pltpu.PrefetchScalarGridSpec(
            num_scalar_prefetch=1, grid=(nrb,),
            in_specs=[pl.BlockSpec((tr, n), lambda i, pos: (i, 0))], out_specs=pl.BlockSpec((tr, n), out_map)),
        out_shape=jax.ShapeDtypeStruct(full_shape, BF16), compiler_params=_params("parallel"),
    )(pos, w)


W_IN_PARTS = ((0, 768), (768, 640))
OUT_STREAMS = 4
X_STREAMS = 4


def _proj_gather_call(pos, x, shift, scale, norm_g, wi_full, wo_full):
    s = x.shape[0]
    tm = min(s, 512)
    nrow = s // tm
    hi = D // 2
    ho = W_OUT_SHARD // 2
    phases = [(0, None), (1, 0), (2, 0), (1, 1), (2, 1), (3, 0), (3, 1)]

    def body(pos_ref, *refs):
        x_refs = refs[:X_STREAMS]
        (sh_ref, sc_ref, g_ref, _, _, h_ref, proj_ref, fi_ref, fo_ref,
         h_all, wbuf, obuf, send_sems, recv_sems, load_sems, out_sems) = refs[X_STREAMS:]
        p = pl.program_id(0)
        i = pl.program_id(1)
        x_, y_, c_ = _coords()
        me, sibling = (x_, y_, c_), (x_, y_, 1 - c_)

        def shard_of(q):
            px, py, _ = _peer(x_, y_, c_, q, 0)
            return 2 * px + py

        def cols_of(q, cp):
            off, w = (0, W_IN_SHARD) if cp is None else W_IN_PARTS[cp]
            return shard_of(q) * W_IN_SHARD + off, w

        def part(which, q, pc, sub, cp):
            n = hi if which == 0 else ho
            base = pc * n
            if sub is not None:
                n //= 2
                base = base + sub * n
            if which == 0:
                c0, w = cols_of(q, cp)
                return fi_ref.at[pl.ds(base, n), pl.ds(c0, w)]
            return fo_ref.at[pl.ds(shard_of(q) * W_OUT_SHARD + base, n), :]

        def copy(k, ref, to):
            return pltpu.make_async_remote_copy(src_ref=ref, dst_ref=ref, send_sem=send_sems.at[k], recv_sem=recv_sems.at[k],
                                                device_id=to, device_id_type=MESH)

        def sem(which, kind, j, cp):
            return 4 * kind + 2 * cp + j if which == 0 else 16 + 2 * kind + j

        def to_neighbour(which, q, cp=None):
            return copy(sem(which, 0, q - 1, cp), part(which, 0, c_, None, cp), _peer(x_, y_, c_, q, 0))

        def from_neighbour(which, q, cp=None):
            return copy(sem(which, 0, q - 1, cp), part(which, q, c_, None, cp), me)

        def relay(which, q, cp=None):
            return copy(sem(which, 1, q - 1, cp), part(which, q, c_, q - 1, cp), _peer(x_, y_, c_, 3 - q, 0))

        def relayed(which, sub, cp=None):
            return copy(sem(which, 1, sub, cp), part(which, 3, c_, sub, cp), me)

        def to_sibling(which, q, cp=None):
            return copy(sem(which, 2, q - 1, cp), part(which, q, c_, None, cp), sibling)

        def from_sibling(which, q, cp=None):
            return copy(sem(which, 2, q - 1, cp), part(which, q, 1 - c_, None, cp), me)

        def relayed_to_sibling(which, sub, cp=None):
            return copy(sem(which, 3, sub, cp), part(which, 3, c_, sub, cp), sibling)

        def relayed_from_sibling(which, sub, cp=None):
            return copy(sem(which, 3, sub, cp), part(which, 3, 1 - c_, sub, cp), me)

        def pass_on_neighbours(which, cp=None):
            for q in (1, 2):
                from_neighbour(which, q, cp).wait_recv()
                to_sibling(which, q, cp).start()
                relay(which, q, cp).start()

        def pass_on_relayed(which, cp=None):
            for sub in range(2):
                relayed(which, sub, cp).wait_recv()
                relayed_to_sibling(which, sub, cp).start()

        def shard_load(k):
            c0, w = cols_of(*phases[k])
            return pltpu.make_async_copy(fi_ref.at[:, pl.ds(c0, w)], wbuf.at[k % 2, :, 0:w], load_sems.at[k % 2])

        class OutCopies:
            def __init__(self, k, slot, row0):
                c0, w = cols_of(*phases[k])
                strip = tm // OUT_STREAMS
                self.copies = [pltpu.make_async_copy(obuf.at[slot, n * strip:(n + 1) * strip, 0:w],
                                                     proj_ref.at[pl.ds(row0 + n * strip, strip), pl.ds(c0, w)],
                                                     out_sems.at[slot, n]) for n in range(OUT_STREAMS)]

            def start(self):
                for cp in self.copies:
                    cp.start()

            def wait(self):
                for cp in self.copies:
                    cp.wait()

        out_copy = OutCopies

        def drain(k):
            for j in range(min(2, nrow)):
                out_copy(k, (nrow - 1 - j) % 2, 0).wait()

        def arrivals(k):
            q, cp = phases[k]
            if k == 0:
                for cp_ in range(2):
                    for q_ in (1, 2):
                        to_neighbour(0, q_, cp_).start()
            elif q < 3 and k in (1, 3):
                pass_on_neighbours(0, cp)
                if k == 1:
                    for q_ in (1, 2):
                        to_neighbour(1, q_).start()
            elif k == 5:
                for cp_ in range(2):
                    pass_on_relayed(0, cp_)
                pass_on_neighbours(1)
            if q in (1, 2):
                from_sibling(0, q, cp).wait_recv()
            elif q == 3:
                for sub in range(2):
                    relayed_from_sibling(0, sub, cp).wait_recv()

        rows = pl.ds(pl.multiple_of(i * tm, tm), tm)
        slot = i % 2
        for k, (q, cp) in enumerate(phases):
            @pl.when(p == k)
            def _(k=k, q=q, cp=cp):
                @pl.when(i == 0)
                def _():
                    if k == 0:
                        arrivals(0)
                        shard_load(0).start()
                    else:
                        drain(k - 1)
                    shard_load(k).wait()

                if k + 1 < len(phases):
                    @pl.when(i == max(nrow - 2, 0))
                    def _():
                        arrivals(k + 1)
                        shard_load(k + 1).start()

                if k == 0:
                    wx = D // X_STREAMS
                    ssq = sum(jnp.sum(xr[...] * xr[...], axis=-1, keepdims=True) for xr in x_refs)
                    r = lax.rsqrt(ssq * (1.0 / D) + EPS)
                    for n, xr in enumerate(x_refs):
                        cols = slice(n * wx, (n + 1) * wx)
                        hv = ((xr[...] * r * g_ref[:, cols]) * (1.0 + sc_ref[:, cols]) + sh_ref[:, cols]).astype(BF16)
                        h_ref[:, cols] = hv
                        h_all[rows, cols] = hv

                @pl.when(i >= 2)
                def _():
                    out_copy(k, slot, 0).wait()

                w = cols_of(q, cp)[1]
                obuf[slot, :, 0:w] = jnp.dot(h_all[rows, :], wbuf[k % 2, :, 0:w], preferred_element_type=F32)
                out_copy(k, slot, pl.multiple_of(i * tm, tm)).start()

        @pl.when((p == len(phases) - 1) & (i == nrow - 1))
        def _():
            drain(len(phases) - 1)
            pass_on_relayed(1)
            for q in (1, 2):
                from_sibling(1, q).wait_recv()
            for sub in range(2):
                relayed_from_sibling(1, sub).wait_recv()
            for which, cps in ((0, (0, 1)), (1, (None,))):
                for cp in cps:
                    for q in (1, 2):
                        to_neighbour(which, q, cp).wait_send()
                        relay(which, q, cp).wait_send()
                        to_sibling(which, q, cp).wait_send()
                        relayed_to_sibling(which, q - 1, cp).wait_send()

    vec = pl.BlockSpec((1, D), lambda p, i, pos: (0, 0))
    first_phase_rows = lambda p, i, pos: (jnp.where(p == 0, i, nrow - 1), 0)
    anyspec = pl.BlockSpec(memory_space=pl.ANY)
    x_spec = lambda n: pl.BlockSpec((tm, D // X_STREAMS), lambda p, i, pos: (jnp.where(p == 0, i, nrow - 1), n))
    return pl.pallas_call(
        body, name="proj_gather",
        grid_spec=pltpu.PrefetchScalarGridSpec(
            num_scalar_prefetch=1, grid=(len(phases), nrow),
            in_specs=[x_spec(n) for n in range(X_STREAMS)] + [vec, vec, vec, anyspec, anyspec],
            out_specs=[pl.BlockSpec((tm, D), first_phase_rows), anyspec, anyspec, anyspec],
            scratch_shapes=[pltpu.VMEM((s, D), BF16), pltpu.VMEM((2, D, W_IN_SHARD), BF16), pltpu.VMEM((2, tm, W_IN_SHARD), F32),
                            pltpu.SemaphoreType.DMA((24,)), pltpu.SemaphoreType.DMA((24,)), pltpu.SemaphoreType.DMA((2,)),
                            pltpu.SemaphoreType.DMA((2, OUT_STREAMS))]),
        out_shape=[jax.ShapeDtypeStruct((s, D), BF16), jax.ShapeDtypeStruct((s, D_IN), F32),
                   jax.ShapeDtypeStruct((D, D_IN), BF16), jax.ShapeDtypeStruct((D, D), BF16)],
        input_output_aliases={X_STREAMS + 4: 2, X_STREAMS + 5: 3},
        compiler_params=_params("arbitrary", "arbitrary"),
    )(pos, *([x] * X_STREAMS), shift, scale, norm_g, wi_full, wo_full)


def _proj_specs(rev_nb=None):
    if rev_nb is None:
        row = lambda i: i
    else:
        row = lambda i: rev_nb - 1 - i
    wide = lambda col: pl.BlockSpec((BLK, D_A), lambda i: (row(i), col))
    kv = lambda col: pl.BlockSpec((BLK, D_KV), lambda i: (row(i), col))
    half = lambda col: pl.BlockSpec((BLK, 512), lambda i: (row(i), col))
    return [wide(0), wide(1), wide(2), wide(3), kv(OFF_K // D_KV), kv(OFF_V // D_KV), half(OFF_ZB // 512), half(OFF_ZB // 512 + 1)]


def _mix_fwd_call(proj, cos, sin, ln_g, ln_b, w_sp, b_sp_t, sinks):
    s = proj.shape[0]
    nb = s // BLK

    def body(ua_ref, va_ref, za_ref, q_ref, k_ref, v_ref, zb0_ref, zb1_ref, cos_ref, sin_ref, lg_ref, lb_ref,
             w_ref, bt_ref, sinks_ref, y_ref, probs_ref, ost_ref, psink_ref, kdup_ref, vdup_ref, qm_ref, bias_ref):
        i = pl.program_id(0)
        first_half, lo = _lane_masks()
        cos_t = cos_ref[...]
        sin_t = sin_ref[...]

        _, _, vln = _layer_norm_fwd(va_ref[...], lg_ref[...], lb_ref[...])
        tril = _tril()
        for g in range(GROUPS):
            cols = slice(g * BLK, (g + 1) * BLK)
            wg = jnp.where(tril, w_ref[g], 0.0).astype(BF16)
            sg = jnp.dot(wg, vln[:, cols].astype(BF16), preferred_element_type=F32) + bt_ref[:, g:g + 1]
            gate, _ = _silu_parts(za_ref[:, cols])
            y_ref[:, cols] = (ua_ref[:, cols] * sg * gate).astype(BF16)

        @pl.when(i == 0)
        def _():
            kdup_ref[:, 0:BLK, :] = jnp.zeros((N_KV, BLK, LANE), BF16)
            vdup_ref[:, 0:BLK, :] = jnp.zeros((N_KV, BLK, LANE), BF16)
            _band_bias(bias_ref)

        @pl.when(i > 0)
        def _():
            kdup_ref[:, 0:BLK, :] = kdup_ref[:, BLK:2 * BLK, :]
            vdup_ref[:, 0:BLK, :] = vdup_ref[:, BLK:2 * BLK, :]

        for ks in range(2):
            cols = slice(ks * LANE, (ks + 1) * LANE)
            kr = _rope(k_ref[:, cols], cos_t, sin_t, first_half)
            for n, (kd, vd) in enumerate(zip(_dup_kv(kr, lo), _dup_kv(v_ref[:, cols], lo))):
                kdup_ref[2 * ks + n, BLK:2 * BLK, :] = kd
                vdup_ref[2 * ks + n, BLK:2 * BLK, :] = vd
        for sb in range(8):
            _stack_heads(qm_ref, sb, _rope(q_ref[:, sb * LANE:(sb + 1) * LANE], cos_t, sin_t, first_half) * SCALE, lo, BF16)

        block_kind = jnp.where(i > 0, 1, 0)

        psink_ref[...] = jnp.zeros((Q_PER_KV * BLK, LANE), F32)
        lane_q = lax.broadcasted_iota(jnp.int32, (Q_PER_KV * BLK, LANE), 1)

        def kv_head(kh, carry):
            probs, psink = _softmax_sink(qm_ref[kh], kdup_ref[kh], bias_ref[block_kind], _sink_column(sinks_ref, kh))
            probs_ref[kh] = probs
            psink_ref[...] = jnp.where(lane_q == kh, psink, psink_ref[...])
            ost_ref[kh] = jnp.dot(probs.astype(BF16), vdup_ref[kh], preferred_element_type=F32)
            return carry

        lax.fori_loop(0, N_KV, kv_head, 0, unroll=2)
        for sb in range(8):
            cols = slice(sb * LANE, (sb + 1) * LANE)
            zb = zb0_ref[:, cols] if sb < 4 else zb1_ref[:, (sb - 4) * LANE:(sb - 3) * LANE]
            gate, _ = _silu_parts(zb)
            y_ref[:, D_A + sb * LANE:D_A + (sb + 1) * LANE] = (_unstack_heads(ost_ref, sb, lo) * gate).astype(BF16)

    tab = pl.BlockSpec((BLK, LANE), lambda i: (i, 0))
    return pl.pallas_call(
        body, name="mix_fwd", grid=(nb,),
        in_specs=_proj_specs() + [
            tab, tab, pl.BlockSpec((1, D_A), lambda i: (0, 0)), pl.BlockSpec((1, D_A), lambda i: (0, 0)),
            pl.BlockSpec((GROUPS, BLK, BLK), lambda i: (0, 0, 0)), pl.BlockSpec((BLK, GROUPS), lambda i: (0, 0)),
            pl.BlockSpec(memory_space=pltpu.SMEM)],
        out_specs=[pl.BlockSpec((BLK, 2 * D_A), lambda i: (i, 0)),
                   pl.BlockSpec((None, N_KV, Q_PER_KV * BLK, 2 * BLK), lambda i: (i, 0, 0, 0)),
                   pl.BlockSpec((None, N_KV, Q_PER_KV * BLK, LANE), lambda i: (i, 0, 0, 0)),
                   pl.BlockSpec((None, Q_PER_KV * BLK, LANE), lambda i: (i, 0, 0))],
        out_shape=[jax.ShapeDtypeStruct((s, 2 * D_A), BF16), jax.ShapeDtypeStruct((nb, N_KV, Q_PER_KV * BLK, 2 * BLK), F32),
                   jax.ShapeDtypeStruct((nb, N_KV, Q_PER_KV * BLK, LANE), F32), jax.ShapeDtypeStruct((nb, Q_PER_KV * BLK, LANE), F32)],
        scratch_shapes=[pltpu.VMEM((N_KV, 2 * BLK, LANE), BF16), pltpu.VMEM((N_KV, 2 * BLK, LANE), BF16),
                        pltpu.VMEM((N_KV, Q_PER_KV * BLK, LANE), BF16), pltpu.VMEM((2, Q_PER_KV * BLK, 2 * BLK), F32)],
        compiler_params=_params("arbitrary"),
    )(proj, proj, proj, proj, proj, proj, proj, proj, cos, sin, ln_g, ln_b, w_sp, b_sp_t, sinks)


def _tail_call(y, w_out_bf, x, target, gate, shift_f, scale_f, gf):
    s = x.shape[0]
    tm = min(s, 256)
    nsteps = s // tm

    def body(y_ref, w_ref, x_ref, t_ref, gate_ref, shf_ref, scf_ref, gf_ref, dx2_ref, do_ref, dy_ref, st_ref):
        i = pl.program_id(0)

        @pl.when(i == 0)
        def _():
            st_ref[...] = jnp.zeros((8, D), F32)

        o = jnp.dot(y_ref[...], w_ref[...], preferred_element_type=F32)
        gate_v = gate_ref[...]
        x2 = x_ref[...] + gate_v * o
        r2 = lax.rsqrt(jnp.mean(x2 * x2, axis=-1, keepdims=True) + EPS)
        xn2 = x2 * r2
        hn2 = xn2 * gf_ref[...]
        one_sc = 1.0 + scf_ref[...]
        err = hn2 * one_sc + shf_ref[...] - t_ref[...]
        dout = err * (1.0 / D)
        dhn2 = dout * one_sc
        dxn2 = dhn2 * gf_ref[...]
        dx2 = r2 * (dxn2 - xn2 * jnp.mean(dxn2 * xn2, axis=-1, keepdims=True))
        dx2_ref[...] = dx2
        do = (dx2 * gate_v).astype(BF16)
        do_ref[...] = do
        dy_ref[...] = lax.dot_general(do, w_ref[...], NT, preferred_element_type=F32)
        st_ref[0:1, :] += jnp.sum(dout, axis=0, keepdims=True)
        st_ref[1:2, :] += jnp.sum(dout * hn2, axis=0, keepdims=True)
        st_ref[2:3, :] += jnp.sum(dhn2 * xn2, axis=0, keepdims=True)
        st_ref[3:4, :] += jnp.sum(dx2 * o, axis=0, keepdims=True)
        st_ref[4:5, :] += jnp.sum(err * err, axis=0, keepdims=True)

        @pl.when(i == nsteps - 1)
        def _():
            st_ref[5:6, :] = jnp.full((1, D), 0.5 / D, F32) * jnp.sum(st_ref[4:5, :])

    vec = pl.BlockSpec((1, D), lambda i: (0, 0))
    rows = lambda: pl.BlockSpec((tm, D), lambda i: (i, 0))
    return pl.pallas_call(
        body, name="tail", grid=(nsteps,),
        in_specs=[rows(), pl.BlockSpec((D, D), lambda i: (0, 0)), rows(), rows(), vec, vec, vec, vec],
        out_specs=[rows(), rows(), rows(), pl.BlockSpec((8, D), lambda i: (0, 0))],
        out_shape=[jax.ShapeDtypeStruct((s, D), F32), jax.ShapeDtypeStruct((s, D), BF16), jax.ShapeDtypeStruct((s, D), F32),
                   jax.ShapeDtypeStruct((8, D), F32)],
        compiler_params=_params("arbitrary"),
    )(y, w_out_bf, x, target, gate, shift_f, scale_f, gf)


def _tn_call(a, b, name):
    s, m = a.shape
    n = b.shape[1]
    tn = 512
    ts = min(s, 1024)
    nk = s // ts

    def body(a_ref, b_ref, o_ref, acc_ref):
        k = pl.program_id(1)

        @pl.when(k == 0)
        def _():
            acc_ref[...] = jnp.zeros((m, tn), F32)

        acc_ref[...] += lax.dot_general(a_ref[...], b_ref[...], TN, preferred_element_type=F32)

        @pl.when(k == nk - 1)
        def _():
            o_ref[...] = acc_ref[...].astype(BF16)

    return pl.pallas_call(
        body, name=name, grid=(n // tn, nk),
        in_specs=[pl.BlockSpec((ts, m), lambda j, k: (k, 0)), pl.BlockSpec((ts, tn), lambda j, k: (k, j))],
        out_specs=pl.BlockSpec((m, tn), lambda j, k: (0, j)),
        out_shape=jax.ShapeDtypeStruct((m, n), BF16),
        scratch_shapes=[pltpu.VMEM((m, tn), F32)],
        compiler_params=_params("parallel", "arbitrary"),
    )(a, b)


def _tn_shards_call(pos, a, b, qs, name):
    s, m = a.shape
    ts = min(s, 1024)
    nk = s // ts

    def body(pos_ref, a_ref, b_ref, o_ref, acc_ref):
        k = pl.program_id(1)

        @pl.when(k == 0)
        def _():
            acc_ref[...] = jnp.zeros((m, W_IN_SHARD), F32)

        acc_ref[...] += lax.dot_general(a_ref[...], b_ref[...], TN, preferred_element_type=F32)

        @pl.when(k == nk - 1)
        def _():
            o_ref[...] = acc_ref[...].astype(BF16)

    def shard(j, pos):
        q = qs[0]
        for n in range(1, len(qs)):
            q = jnp.where(j == n, qs[n], q)
        return jnp.bitwise_xor(pos[0], q)

    return pl.pallas_call(
        body, name=name,
        grid_spec=pltpu.PrefetchScalarGridSpec(
            num_scalar_prefetch=1, grid=(len(qs), nk),
            in_specs=[pl.BlockSpec((ts, m), lambda j, k, pos: (k, 0)),
                      pl.BlockSpec((ts, W_IN_SHARD), lambda j, k, pos: (k, shard(j, pos)))],
            out_specs=pl.BlockSpec((m, W_IN_SHARD), lambda j, k, pos: (0, j)),
            scratch_shapes=[pltpu.VMEM((m, W_IN_SHARD), F32)]),
        out_shape=jax.ShapeDtypeStruct((m, len(qs) * W_IN_SHARD), BF16),
        compiler_params=_params("parallel", "arbitrary"),
    )(pos, a, b)


def _mix_bwd_call(proj, dy, probs, outs, psinks, tables, ln_g, ln_b, w_sp, w_sp_t, b_sp_t):
    s = proj.shape[0]
    nb = s // BLK
    rev = lambda i: nb - 1 - i
    prev = lambda i: jnp.maximum(nb - 2 - i, 0)

    def body(ua_ref, va_ref, za_ref, q_ref, k_ref, v_ref, zb0_ref, zb1_ref, kp_ref, vp_ref, dy_ref,
             probs_ref, ost_ref, psink_ref, cos_ref, sin_ref, cosp_ref, sinp_ref, lg_ref, lb_ref, w_ref, wt_ref, bt_ref,
             dp_ref, lnst_ref, dw_ref, dbt_ref, dsink_ref,
             kdup_ref, vdup_ref, dvln_ref, qm_ref, dom_ref, dqst_ref, dkdup_ref, dvdup_ref, kcar_ref, vcar_ref, sigb_ref):
        i = pl.program_id(0)
        first_half, lo = _lane_masks()
        lane8 = lax.broadcasted_iota(jnp.int32, (8, LANE), 1)
        cos_t = cos_ref[...]
        sin_t = sin_ref[...]

        @pl.when(i == 0)
        def _():
            lnst_ref[...] = jnp.zeros((8, D_A), F32)
            dw_ref[...] = jnp.zeros((GROUPS, BLK, BLK), F32)
            dbt_ref[...] = jnp.zeros((BLK, LANE), F32)
            dsink_ref[...] = jnp.zeros((8, LANE), F32)
            kcar_ref[...] = jnp.zeros((BLK, D_KV), F32)
            vcar_ref[...] = jnp.zeros((BLK, D_KV), F32)

        vhat, rstd, vln = _layer_norm_fwd(va_ref[...], lg_ref[...], lb_ref[...])
        tril = _tril()
        triu = jnp.logical_not(tril) | (lax.broadcasted_iota(jnp.int32, (BLK, BLK), 0) == lax.broadcasted_iota(jnp.int32, (BLK, BLK), 1))
        lane_b = lax.broadcasted_iota(jnp.int32, (BLK, LANE), 1)
        db_acc = jnp.zeros((BLK, LANE), F32)
        for g in range(GROUPS):
            cols = slice(g * BLK, (g + 1) * BLK)
            vln_g = vln[:, cols].astype(BF16)
            wg = jnp.where(tril, w_ref[g], 0.0).astype(BF16)
            sg = jnp.dot(wg, vln_g, preferred_element_type=F32) + bt_ref[:, g:g + 1]
            za = za_ref[:, cols]
            gate, sig = _silu_parts(za)
            ua = ua_ref[:, cols]
            dya_g = dy_ref[:, cols]
            dya = dya_g * gate
            dp_ref[:, cols] = (dya * sg).astype(BF16)
            dp_ref[:, 2 * D_A + g * BLK:2 * D_A + (g + 1) * BLK] = (
                dya_g * (ua * sg) * (sig * (1.0 + za * (1.0 - sig)))).astype(BF16)
            ds = dya * ua
            ds_b = ds.astype(BF16)
            wtg = jnp.where(triu, wt_ref[g], 0.0).astype(BF16)
            dvln_ref[:, cols] = jnp.dot(wtg, ds_b, preferred_element_type=F32)
            dw_ref[g] += jnp.where(tril, lax.dot_general(ds_b, vln_g, NT, preferred_element_type=F32), 0.0)
            db_acc = db_acc + jnp.where(lane_b == g, jnp.sum(ds, axis=-1, keepdims=True), 0.0)
        dbt_ref[...] += db_acc
        dvln = dvln_ref[...]
        lnst_ref[0:1, :] += jnp.sum(dvln * vhat, axis=0, keepdims=True)
        lnst_ref[1:2, :] += jnp.sum(dvln, axis=0, keepdims=True)
        dvhat = dvln * lg_ref[...]
        m1 = jnp.mean(dvhat, axis=-1, keepdims=True)
        m2 = jnp.mean(dvhat * vhat, axis=-1, keepdims=True)
        dp_ref[:, D_A:2 * D_A] = (rstd * (dvhat - m1 - vhat * m2)).astype(BF16)

        cosp = cosp_ref[...]
        sinp = sinp_ref[...]
        for ks in range(2):
            cols = slice(ks * LANE, (ks + 1) * LANE)
            kr = _rope(k_ref[:, cols], cos_t, sin_t, first_half)
            kpr = _rope(kp_ref[:, cols], cosp, sinp, first_half)
            for n, (kc, vc, kp, vp) in enumerate(zip(_dup_kv(kr, lo), _dup_kv(v_ref[:, cols], lo),
                                                     _dup_kv(kpr, lo), _dup_kv(vp_ref[:, cols], lo))):
                kdup_ref[2 * ks + n, BLK:2 * BLK, :] = kc
                vdup_ref[2 * ks + n, BLK:2 * BLK, :] = vc
                kdup_ref[2 * ks + n, 0:BLK, :] = kp
                vdup_ref[2 * ks + n, 0:BLK, :] = vp
        for sb in range(8):
            cols = slice(sb * LANE, (sb + 1) * LANE)
            _stack_heads(qm_ref, sb, _rope(q_ref[:, cols], cos_t, sin_t, first_half) * SCALE, lo, BF16)
            zb = zb0_ref[:, cols] if sb < 4 else zb1_ref[:, (sb - 4) * LANE:(sb - 3) * LANE]
            gate, sig = _silu_parts(zb)
            sigb_ref[:, cols] = sig
            _stack_heads(dom_ref, sb, dy_ref[:, D_A + sb * LANE:D_A + (sb + 1) * LANE] * gate, lo, F32)

        lane_q = lax.broadcasted_iota(jnp.int32, (Q_PER_KV * BLK, LANE), 1)

        def kv_head(kh, dsink_acc):
            qm = qm_ref[kh]
            kd = kdup_ref[kh]
            vd = vdup_ref[kh]
            probs = probs_ref[kh]
            psink = jnp.sum(jnp.where(lane_q == kh, psink_ref[...], 0.0), axis=-1, keepdims=True)
            probs_b = probs.astype(BF16)
            o = ost_ref[kh]
            dom = dom_ref[kh]
            dom_b = dom.astype(BF16)
            delta = jnp.sum(dom * o, axis=-1, keepdims=True)
            dpr = lax.dot_general(dom_b, vd, NT, preferred_element_type=F32)
            dss = (probs * (dpr - delta)).astype(BF16)
            sd = psink * delta
            for n in range(Q_PER_KV):
                dsink_acc = dsink_acc + jnp.where(lane8 == Q_PER_KV * kh + n, -jnp.sum(sd[n * BLK:(n + 1) * BLK]), 0.0)
            dqst_ref[kh] = jnp.dot(dss, kd, preferred_element_type=F32)
            dkdup_ref[kh] = lax.dot_general(dss, qm, TN, preferred_element_type=F32)
            dvdup_ref[kh] = lax.dot_general(probs_b, dom_b, TN, preferred_element_type=F32)
            return dsink_acc

        dsink_acc = lax.fori_loop(0, N_KV // 2, lambda j, acc: kv_head(2 * j + 1, kv_head(2 * j, acc)), jnp.zeros((8, LANE), F32))
        row0 = lax.broadcasted_iota(jnp.int32, (8, LANE), 0) == 0
        dsink_ref[...] += jnp.where(row0, dsink_acc, 0.0)

        for sb in range(8):
            cols = slice(sb * LANE, (sb + 1) * LANE)
            zb = zb0_ref[:, cols] if sb < 4 else zb1_ref[:, (sb - 4) * LANE:(sb - 3) * LANE]
            sig = sigb_ref[:, cols]
            dyb = dy_ref[:, D_A + sb * LANE:D_A + (sb + 1) * LANE]
            dp_ref[:, OFF_ZB + sb * LANE:OFF_ZB + (sb + 1) * LANE] = (
                dyb * _unstack_heads(ost_ref, sb, lo) * (sig * (1.0 + zb * (1.0 - sig)))).astype(BF16)
            dq_r = _unstack_heads(dqst_ref, sb, lo) * SCALE
            dp_ref[:, OFF_Q + sb * LANE:OFF_Q + (sb + 1) * LANE] = _unrope(dq_r, cos_t, sin_t, first_half).astype(BF16)

        lo2 = lax.broadcasted_iota(jnp.int32, (2 * BLK, LANE), 1) < HEAD
        for ks in range(2):
            cols = slice(ks * LANE, (ks + 1) * LANE)
            ka = dkdup_ref[2 * ks]
            kb = dkdup_ref[2 * ks + 1]
            dk_band = jnp.where(lo2, ka + pltpu.roll(ka, HEAD, 1), kb + pltpu.roll(kb, HEAD, 1))
            va_ = dvdup_ref[2 * ks]
            vb_ = dvdup_ref[2 * ks + 1]
            dv_band = jnp.where(lo2, va_ + pltpu.roll(va_, HEAD, 1), vb_ + pltpu.roll(vb_, HEAD, 1))
            dkr = dk_band[BLK:2 * BLK, :] + kcar_ref[:, cols]
            dp_ref[:, OFF_K + ks * LANE:OFF_K + (ks + 1) * LANE] = _unrope(dkr, cos_t, sin_t, first_half).astype(BF16)
            dp_ref[:, OFF_V + ks * LANE:OFF_V + (ks + 1) * LANE] = (
                dv_band[BLK:2 * BLK, :] + vcar_ref[:, cols]).astype(BF16)
            kcar_ref[:, cols] = dk_band[0:BLK, :]
            vcar_ref[:, cols] = dv_band[0:BLK, :]

    tab = pl.BlockSpec((BLK, LANE), lambda i: (rev(i), 0))
    kvp = lambda col: pl.BlockSpec((BLK, D_KV), lambda i: (prev(i), col))
    vec = pl.BlockSpec((1, D_A), lambda i: (0, 0))
    w3 = pl.BlockSpec((GROUPS, BLK, BLK), lambda i: (0, 0, 0))
    return pl.pallas_call(
        body, name="mix_bwd", grid=(nb,),
        in_specs=_proj_specs(nb) + [
            kvp(OFF_K // D_KV), kvp(OFF_V // D_KV), pl.BlockSpec((BLK, 2 * D_A), lambda i: (rev(i), 0)),
            pl.BlockSpec((None, N_KV, Q_PER_KV * BLK, 2 * BLK), lambda i: (rev(i), 0, 0, 0)),
            pl.BlockSpec((None, N_KV, Q_PER_KV * BLK, LANE), lambda i: (rev(i), 0, 0, 0)),
            pl.BlockSpec((None, Q_PER_KV * BLK, LANE), lambda i: (rev(i), 0, 0)),
            tab, tab, tab, tab, vec, vec, w3, w3, pl.BlockSpec((BLK, GROUPS), lambda i: (0, 0))],
        out_specs=[pl.BlockSpec((BLK, D_IN), lambda i: (rev(i), 0)), pl.BlockSpec((8, D_A), lambda i: (0, 0)), w3,
                   pl.BlockSpec((BLK, LANE), lambda i: (0, 0)), pl.BlockSpec((8, LANE), lambda i: (0, 0))],
        out_shape=[jax.ShapeDtypeStruct((s, D_IN), BF16), jax.ShapeDtypeStruct((8, D_A), F32),
                   jax.ShapeDtypeStruct((GROUPS, BLK, BLK), F32), jax.ShapeDtypeStruct((BLK, LANE), F32),
                   jax.ShapeDtypeStruct((8, LANE), F32)],
        scratch_shapes=[pltpu.VMEM((N_KV, 2 * BLK, LANE), BF16), pltpu.VMEM((N_KV, 2 * BLK, LANE), BF16),
                        pltpu.VMEM((BLK, D_A), F32), pltpu.VMEM((N_KV, Q_PER_KV * BLK, LANE), BF16),
                        pltpu.VMEM((N_KV, Q_PER_KV * BLK, LANE), F32), pltpu.VMEM((N_KV, Q_PER_KV * BLK, LANE), F32),
                        pltpu.VMEM((N_KV, 2 * BLK, LANE), F32), pltpu.VMEM((N_KV, 2 * BLK, LANE), F32),
                        pltpu.VMEM((BLK, D_KV), F32), pltpu.VMEM((BLK, D_KV), F32), pltpu.VMEM((BLK, D_B), F32)],
        compiler_params=_params("arbitrary"),
    )(proj, proj, proj, proj, proj, proj, proj, proj, proj, proj, dy, probs, outs, psinks, *tables, ln_g, ln_b,
      w_sp, w_sp_t, b_sp_t)


def _dh_call(dproj, w_bf, x, dx2, scale, norm_g):
    s = x.shape[0]
    tm = min(s, 512)
    tk = W_IN_SHARD
    nk = D_IN // tk

    def body(dp_ref, w_ref, x_ref, dx2_ref, sc_ref, g_ref, gx_ref, st_ref, acc_ref):
        i = pl.program_id(0)
        k = pl.program_id(1)

        @pl.when((i == 0) & (k == 0))
        def _():
            st_ref[...] = jnp.zeros((8, D), F32)

        @pl.when(k == 0)
        def _():
            acc_ref[...] = jnp.zeros((tm, D), F32)

        acc_ref[...] += lax.dot_general(dp_ref[...], w_ref[...], NT, preferred_element_type=F32)

        @pl.when(k == nk - 1)
        def _():
            g = g_ref[...]
            one_sc = 1.0 + sc_ref[...]

            def chunk(n, carry):
                rows = pl.ds(pl.multiple_of(n * BLK, BLK), BLK)
                dh = acc_ref[rows, :]
                xv = x_ref[rows, :]
                r = lax.rsqrt(jnp.mean(xv * xv, axis=-1, keepdims=True) + EPS)
                xn = xv * r
                dhn = dh * one_sc
                dxn = dhn * g
                gx_ref[rows, :] = dx2_ref[rows, :] + r * (dxn - xn * jnp.mean(dxn * xn, axis=-1, keepdims=True))
                st_ref[0:1, :] += jnp.sum(dh, axis=0, keepdims=True)
                st_ref[1:2, :] += jnp.sum(dh * (xn * g), axis=0, keepdims=True)
                st_ref[2:3, :] += jnp.sum(dhn * xn, axis=0, keepdims=True)
                return carry

            lax.fori_loop(0, tm // BLK, chunk, 0)

    vec = pl.BlockSpec((1, D), lambda i, k: (0, 0))
    rows = lambda: pl.BlockSpec((tm, D), lambda i, k: (i, 0))
    return pl.pallas_call(
        body, name="dh", grid=(s // tm, nk),
        in_specs=[pl.BlockSpec((tm, tk), lambda i, k: (i, k)), pl.BlockSpec((D, tk), lambda i, k: (0, k)), rows(), rows(), vec, vec],
        out_specs=[rows(), pl.BlockSpec((8, D), lambda i, k: (0, 0))],
        out_shape=[jax.ShapeDtypeStruct((s, D), F32), jax.ShapeDtypeStruct((8, D), F32)],
        scratch_shapes=[pltpu.VMEM((tm, D), F32)],
        compiler_params=_params("arbitrary", "arbitrary"),
    )(dproj, w_bf, x, dx2, scale, norm_g)


def _adam_math(w, g, m, v):
    m_new = ADAM_B1 * m + (1.0 - ADAM_B1) * g
    v_new = ADAM_B2 * v + (1.0 - ADAM_B2) * (g * g)
    m_hat = m_new / ADAM_C1
    v_hat = v_new / ADAM_C2
    delta = -ADAM_LR * (m_hat / (jnp.sqrt(v_hat) + ADAM_EPS) + ADAM_WD * w)
    return delta, m_new, v_new


def _adam_small_call(tensors):
    n = len(tensors)

    def body(*refs):
        ins, outs = refs[:4 * n], refs[4 * n:]
        for t in range(n):
            w_ref, g_ref, m_ref, v_ref = ins[4 * t:4 * t + 4]
            d, mo, vo = _adam_math(w_ref[...], g_ref[...], m_ref[...], v_ref[...])
            outs[3 * t][...], outs[3 * t + 1][...], outs[3 * t + 2][...] = d, mo, vo

    vm = pl.BlockSpec(memory_space=pltpu.VMEM)
    flat = [a for t in tensors for a in t]
    out = pl.pallas_call(
        body, name="adam_small", in_specs=[vm] * (4 * n), out_specs=[vm] * (3 * n),
        out_shape=[jax.ShapeDtypeStruct(t[0].shape, F32) for t in tensors for _ in range(3)],
        compiler_params=pltpu.CompilerParams(vmem_limit_bytes=VMEM_LIMIT),
    )(*flat)
    return [tuple(out[3 * t:3 * t + 3]) for t in range(n)]


def _adam_halves_call(pos, w, mine, theirs, m, v, name):
    r, n = w.shape
    half = r // 2
    tr = ADAM_ROWS
    nh = half // tr

    def body(pos_ref, w_ref, mine_ref, theirs_ref, m_ref, v_ref, g_ref, d_ref, mo_ref, vo_ref):
        is_mine = (pl.program_id(0) // nh) == pos_ref[1]
        g = jnp.where(is_mine, mine_ref[...], theirs_ref[...])
        g_ref[...] = g
        d_ref[...], mo_ref[...], vo_ref[...] = _adam_math(w_ref[...], g, m_ref[...], v_ref[...])

    spec = lambda: pl.BlockSpec((tr, n), lambda i, pos: (i, 0))
    hspec = lambda: pl.BlockSpec((tr, n), lambda i, pos: (i % nh, 0))
    return pl.pallas_call(
        body, name=name,
        grid_spec=pltpu.PrefetchScalarGridSpec(
            num_scalar_prefetch=1, grid=(r // tr,), in_specs=[spec(), hspec(), hspec(), spec(), spec()],
            out_specs=[spec() for _ in range(4)]),
        out_shape=[jax.ShapeDtypeStruct((r, n), F32)] * 4, compiler_params=_params("parallel"),
    )(pos, w, mine, theirs, m, v)


def _adam_outer_call(w, ct, dm, m, v, name):
    r, n = w.shape
    tr = ADAM_ROWS

    def body(w_ref, ct_ref, dm_ref, m_ref, v_ref, g_ref, d_ref, mo_ref, vo_ref):
        g = ct_ref[:, 0:1] * dm_ref[0:1, :]
        for b in range(1, N_DEV):
            g = g + ct_ref[:, b:b + 1] * dm_ref[b:b + 1, :]
        g_ref[...] = g
        d_ref[...], mo_ref[...], vo_ref[...] = _adam_math(w_ref[...], g, m_ref[...], v_ref[...])

    spec = lambda: pl.BlockSpec((tr, n), lambda i: (i, 0))
    return pl.pallas_call(
        body, name=name, grid=(r // tr,),
        in_specs=[spec(), pl.BlockSpec((tr, N_DEV), lambda i: (i, 0)), pl.BlockSpec((N_DEV, n), lambda i: (0, 0)), spec(), spec()],
        out_specs=[spec() for _ in range(4)],
        out_shape=[jax.ShapeDtypeStruct((r, n), F32)] * 4, compiler_params=_params("parallel"),
    )(w, ct, dm, m, v)


def _sum_pieces_call(pos, part, part_block, recvs, name):
    r, n = recvs[0].shape[1:]
    tr = min(r, 256)
    nrb = r // tr

    def body(pos_ref, p_ref, *refs):
        acc = p_ref[...].astype(F32)
        for r_ref in refs[:-1]:
            for d in range(r_ref.shape[0]):
                acc = acc + r_ref[d].astype(F32)
        refs[-1][...] = acc

    return pl.pallas_call(
        body, name=name,
        grid_spec=pltpu.PrefetchScalarGridSpec(
            num_scalar_prefetch=1, grid=(nrb,),
            in_specs=[pl.BlockSpec((tr, n), lambda i, pos: part_block(i, pos, nrb))] + [
                pl.BlockSpec((rv.shape[0], tr, n), lambda i, pos: (0, i, 0)) for rv in recvs],
            out_specs=pl.BlockSpec((tr, n), lambda i, pos: (i, 0))),
        out_shape=jax.ShapeDtypeStruct((r, n), F32), compiler_params=_params("parallel"),
    )(pos, part, *recvs)


def _coords():
    return lax.axis_index("x"), lax.axis_index("y"), lax.axis_index("c")


def _allgather_sum_call(blk, name, with_sum):
    m_per, n = blk.shape

    def body(x_ref, out_ref, *rest):
        if with_sum:
            sum_ref, send_sems, recv_sems, local_sem = rest
        else:
            send_sems, recv_sems, local_sem = rest
        x, y, c = _coords()
        me, sibling = (x, y, c), (x, y, 1 - c)
        chips = [(1 - x, y), (x, 1 - y), (1 - x, 1 - y)]

        def rows(px, py, pc):
            return out_ref.at[pl.ds((4 * px + 2 * py + pc) * m_per, m_per), :]

        def copy(k, block, to, src=None):
            return pltpu.make_async_remote_copy(
                src_ref=rows(*block) if src is None else src, dst_ref=rows(*block),
                send_sem=send_sems.at[k], recv_sem=recv_sems.at[k], device_id=to, device_id_type=MESH)

        mine = pltpu.make_async_copy(x_ref, rows(*me), local_sem)
        mine.start()
        first = [copy(0, me, sibling, src=x_ref)]
        first += [copy(1 + j, me, (*chip, c), src=x_ref) for j, chip in enumerate(chips)]
        for cp in first:
            cp.start()
        passed = [copy(4 + j, (*chip, c), sibling) for j, chip in enumerate(chips)]
        for j, chip in enumerate(chips):
            copy(1 + j, (*chip, c), me).wait_recv()
            passed[j].start()
        copy(0, sibling, me).wait_recv()
        for j, chip in enumerate(chips):
            copy(4 + j, (*chip, 1 - c), me).wait_recv()
        for cp in first + passed:
            cp.wait_send()
        mine.wait()
        if with_sum:
            acc = out_ref[0:m_per, :]
            for d in range(1, N_DEV):
                acc = acc + out_ref[d * m_per:(d + 1) * m_per, :]
            sum_ref[...] = acc

    vm = pl.BlockSpec(memory_space=pltpu.VMEM)
    out_shape = [jax.ShapeDtypeStruct((N_DEV * m_per, n), F32)]
    if with_sum:
        out_shape.append(jax.ShapeDtypeStruct((m_per, n), F32))
    return pl.pallas_call(
        body, name=name, out_shape=out_shape, in_specs=[vm], out_specs=[vm] * len(out_shape),
        scratch_shapes=[pltpu.SemaphoreType.DMA((7,)), pltpu.SemaphoreType.DMA((7,)), pltpu.SemaphoreType.DMA],
        compiler_params=pltpu.CompilerParams(vmem_limit_bytes=VMEM_LIMIT),
    )(blk)


HBM_SPEC = pl.BlockSpec(memory_space=pltpu.HBM)
SEM_SPEC = pl.BlockSpec(memory_space=pltpu.SEMAPHORE)
SIDE_EFFECT = pltpu.SideEffectType.DATAFLOW_SIDE_EFFECTING


def _peer(x, y, c, q, cb):
    return (1 - x if q & 2 else x, 1 - y if q & 1 else y, 1 - c if cb else c)


def _w_in_piece(slots):
    def piece(part_ref, k, to):
        return part_ref.at[pl.ds(to[2] * (D // 2), D // 2), pl.ds(slots[k] * W_IN_SHARD, W_IN_SHARD)]
    return piece


def _w_out_piece(part_ref, k, to):
    ho = W_OUT_SHARD // 2
    return part_ref.at[pl.ds((2 * to[0] + to[1]) * W_OUT_SHARD + to[2] * ho, ho), :]


def _group_piece(part_ref, k, to):
    return part_ref.at[4 * to[0] + 2 * to[1] + to[2]]


def _whole_piece(part_ref, k, to):
    return part_ref


def _exchange_start_call(groups, name):
    ng = len(groups)
    lands = [lax.empty((len(rels),) + slot_shape, part.dtype) for part, rels, _, slot_shape in groups]

    def body(*refs):
        ins, outs = refs[:2 * ng], refs[2 * ng:]
        x, y, c = _coords()
        for g, (_, rels, piece, _) in enumerate(groups):
            part_ref, land_ref = ins[2 * g], ins[2 * g + 1]
            send_sems, recv_sems = outs[4 * g], outs[4 * g + 1]
            for k, (q, cb) in enumerate(rels):
                to = _peer(x, y, c, q, cb)
                pltpu.make_async_remote_copy(src_ref=piece(part_ref, k, to), dst_ref=land_ref.at[k], send_sem=send_sems.at[k],
                                             recv_sem=recv_sems.at[k], device_id=to, device_id_type=MESH).start()
        outs[-1][...] = jnp.zeros_like(outs[-1])

    out_shape, out_specs, operands = [], [], []
    for (part, rels, _, _), land in zip(groups, lands):
        n = len(rels)
        out_shape += [pltpu.SemaphoreType.DMA((n,)), pltpu.SemaphoreType.DMA((n,)), pltpu.HBM(part.shape, part.dtype),
                      pltpu.HBM(land.shape, land.dtype)]
        out_specs += [SEM_SPEC, SEM_SPEC, HBM_SPEC, HBM_SPEC]
        operands += [pltpu.with_memory_space_constraint(part, pltpu.HBM), pltpu.with_memory_space_constraint(land, pltpu.HBM)]
    out = pl.pallas_call(
        body, name=name,
        out_shape=tuple(out_shape) + (jax.ShapeDtypeStruct((1, 1), F32),),
        in_specs=(HBM_SPEC,) * (2 * ng), out_specs=tuple(out_specs) + (pl.BlockSpec(memory_space=pltpu.VMEM),),
        input_output_aliases={j: 4 * (j // 2) + 2 + j % 2 for j in range(2 * ng)},
        compiler_params=pltpu.CompilerParams(has_side_effects=SIDE_EFFECT),
    )(*operands)
    return [tuple(out[4 * g:4 * g + 4]) for g in range(ng)], out[-1]


def _exchange_wait_call(started, groups, after, name):
    ng = len(groups)

    def body(*refs):
        ins = refs[:4 * ng]
        x, y, c = _coords()
        for g, (_, rels, piece, _) in enumerate(groups):
            part_ref, land_ref, send_sems, recv_sems = ins[4 * g:4 * g + 4]
            for k, (q, cb) in enumerate(rels):
                to = _peer(x, y, c, q, cb)
                cp = pltpu.make_async_remote_copy(src_ref=piece(part_ref, k, to), dst_ref=land_ref.at[k], send_sem=send_sems.at[k],
                                                  recv_sem=recv_sems.at[k], device_id=to, device_id_type=MESH)
                cp.wait_send()
                cp.wait_recv()

    operands, in_specs, out_shape = [], [], []
    for send_sems, recv_sems, part_thru, land_thru in started:
        operands += [part_thru, land_thru, send_sems, recv_sems]
        in_specs += [HBM_SPEC, HBM_SPEC, SEM_SPEC, SEM_SPEC]
        out_shape += [pltpu.HBM(part_thru.shape, part_thru.dtype), pltpu.HBM(land_thru.shape, land_thru.dtype)]
    out = pl.pallas_call(
        body, name=name, out_shape=tuple(out_shape),
        in_specs=tuple(in_specs) + (pl.BlockSpec(memory_space=pl.ANY),), out_specs=(HBM_SPEC,) * (2 * ng),
        input_output_aliases={4 * g + j: 2 * g + j for g in range(ng) for j in range(2)},
        compiler_params=pltpu.CompilerParams(has_side_effects=SIDE_EFFECT),
    )(*operands, after)
    return [tuple(out[2 * g:2 * g + 2]) for g in range(ng)]


def _rope_tables(s):
    inv_freq = np.float32(10000.0) ** (-np.arange(0, HEAD, 2, dtype=np.float32) / np.float32(HEAD))
    ang = np.arange(s, dtype=np.float32)[:, None] * inv_freq[None, :]
    cos = np.tile(np.cos(ang), (1, LANE // (HEAD // 2))).astype(np.float32)
    sin = np.tile(np.sin(ang), (1, LANE // (HEAD // 2))).astype(np.float32)
    first_half = (np.arange(LANE) % HEAD) < (HEAD // 2)
    sin = np.where(first_half[None, :], -sin, sin)
    behind = lambda t: np.concatenate([t[:BLK], t[:-BLK]], axis=0)
    return tuple(jnp.asarray(t) for t in (cos, sin, behind(cos), behind(sin)))


def kernel(x, c, w_ada, b_ada, norm_g, w_in, ln_v_g, ln_v_b, w_spatial, b_spatial, sinks, w_out, w_ada_final, b_ada_final, final_norm_g, loss_target, m_w_ada, m_b_ada, m_norm_g, m_w_in, m_ln_v_g, m_ln_v_b, m_w_spatial, m_b_spatial, m_sinks, m_w_out, m_w_ada_final, m_b_ada_final, m_final_norm_g, v_w_ada, v_b_ada, v_norm_g, v_w_in, v_ln_v_g, v_ln_v_b, v_w_spatial, v_b_spatial, v_sinks, v_w_out, v_w_ada_final, v_b_ada_final, v_final_norm_g):
    s = x.shape[1]
    ax, ay, ac = _coords()
    chip = 2 * ax + ay
    me = 4 * ax + 2 * ay + ac
    n_ada = w_ada.shape[2]
    n_adaf = w_ada_final.shape[1]

    x2d = x.reshape(s, D)
    tgt = loss_target.reshape(s, D)
    w_ada2, w_in2, w_out2 = w_ada[0], w_in[0], w_out[0]
    b_ada_f2 = b_ada_final.reshape(1, 2 * D)
    gf = final_norm_g.reshape(1, D)

    c_all = _allgather_sum_call(jnp.pad(c, ((0, 7), (0, 0))), "gather_c", False)[0][::8]
    mod_p, c_act = _rowmat_call(c_all, w_ada2, lax.dynamic_slice(b_ada, (0, chip * n_ada), (1, n_ada)), "mod")
    modf_p, _ = _rowmat_call(c_all, w_ada_final, lax.dynamic_slice(b_ada_f2, (0, chip * n_adaf), (1, n_adaf)), "mod_final")
    mods = _allgather_sum_call(jnp.concatenate([mod_p, modf_p], axis=1), "gather_mod", False)[0]
    my_rows = [lax.dynamic_slice(mods, (16 * j + me, 0), (1, n_ada + n_adaf)) for j in range(N_CHIP)]
    mod = jnp.concatenate([r[:, :n_ada] for r in my_rows], axis=1)
    mod_f = jnp.concatenate([r[:, n_ada:] for r in my_rows], axis=1)
    shift, scale, gate = mod[:, :D], mod[:, D:2 * D], mod[:, 2 * D:]
    shift_f, scale_f = mod_f[:, :D], mod_f[:, D:]

    pos = jnp.stack([chip, ac]).astype(jnp.int32)
    w_in_own = _cast_into_call(pos, w_in2, (D, D_IN), "cast_w_in")
    w_out_own = _cast_into_call(pos, w_out2, (D, D), "cast_w_out")

    tables = _rope_tables(s)
    cos, sin = tables[:2]
    b_sp_t = b_spatial[0].T
    sinks1 = sinks.reshape(N_Q)
    h, proj, w_in_bf, w_out_bf = _proj_gather_call(pos, x2d, shift, scale, norm_g, w_in_own, w_out_own)
    y, probs, attn_out, psinks = _mix_fwd_call(proj, cos, sin, ln_v_g, ln_v_b, w_spatial[0], b_sp_t, sinks1)
    dx2, do, dy, st_tail = _tail_call(y, w_out_bf, x2d, tgt, gate, shift_f, scale_f, gf)

    rel_o = [(0, 1), (1, 0), (1, 1), (2, 0), (2, 1), (3, 0), (3, 1)]
    rel_a = [(1, 0), (1, 1), (2, 0), (2, 1)]
    rel_b = [(3, 0), (3, 1), (0, 1)]
    piece_a, piece_b = _w_in_piece([0, 0, 1, 1]), _w_in_piece([0, 0, 1])
    half_in, half_out = (D // 2, W_IN_SHARD), (W_OUT_SHARD // 2, D)

    g_w_out_p = _tn_call(y, do, "grad_w_out")
    grp_o = [(g_w_out_p, rel_o, _w_out_piece, half_out)]
    st_o, tok_o = _exchange_start_call(grp_o, "send_w_out")
    dproj, st_ln, d_wsp, d_bsp_t, d_sink = _mix_bwd_call(
        proj, dy, probs, attn_out, psinks, tables, ln_v_g + tok_o, ln_v_b, w_spatial[0], jnp.swapaxes(w_spatial[0], 1, 2),
        b_sp_t)
    g_w_in_a = _tn_shards_call(pos, h, dproj, (1, 2), "grad_w_in_a")
    grp_a = [(g_w_in_a, rel_a, piece_a, half_in), (d_wsp, rel_o, _group_piece, (BLK, BLK))]
    st_a, tok_a = _exchange_start_call(grp_a, "send_w_in_a")
    g_w_in_b = _tn_shards_call(pos, h, dproj, (3, 0), "grad_w_in_b")
    grp_b = [(g_w_in_b, rel_b, piece_b, half_in)]
    st_b, tok_b = _exchange_start_call(grp_b, "send_w_in_b")
    grad_x, st_dh = _dh_call(dproj, w_in_bf, x2d, dx2, scale + (tok_a + tok_b), norm_g)

    ((g_w_out_p, recv_o),) = _exchange_wait_call(st_o, grp_o, st_dh, "wait_w_out")
    (_, recv_a), (d_wsp, recv_s) = _exchange_wait_call(st_a, grp_a, st_dh, "wait_w_in_a")
    ((g_w_in_b, recv_b),) = _exchange_wait_call(st_b, grp_b, st_dh, "wait_w_in_b")
    mine_in = _sum_pieces_call(pos, g_w_in_b, lambda i, p, nrb: (p[1] * nrb + i, 1), [recv_a, recv_b], "sum_w_in")
    mine_out = _sum_pieces_call(pos, g_w_out_p, lambda i, p, nrb: ((2 * p[0] + p[1]) * nrb + i, 0), [recv_o], "sum_w_out")
    wsp_group = _sum_pieces_call(pos, d_wsp.reshape(GROUPS * BLK, BLK), lambda i, p, nrb: (2 * p[0] + p[1], 0), [recv_s],
                                 "sum_w_spatial")
    to_sibling = [(0, 1)]
    grp_p = [(mine_in, to_sibling, _whole_piece, half_in), (mine_out, to_sibling, _whole_piece, half_out)]
    st_p, tok_p = _exchange_start_call(grp_p, "swap_halves")

    misc = jnp.concatenate([st_ln, d_bsp_t[:, :GROUPS].T, d_sink, jnp.zeros((8, D - D_A - 2 * LANE), F32)], axis=1)
    pack = jnp.concatenate([wsp_group.reshape(8, D) + tok_p, st_tail, st_dh, misc], axis=0)
    rows = pack.shape[0]
    packs, tot = _allgather_sum_call(pack, "gather_small", True)
    packs = packs.reshape(N_DEV, rows, D)
    dmod_all = jnp.concatenate([packs[:, 16, :], packs[:, 17, :], packs[:, 11, :]], axis=1)
    dmodf_all = jnp.concatenate([packs[:, 8, :], packs[:, 9, :]], axis=1)
    loss = tot[13, 0]
    (mine_in, theirs_in), (mine_out, theirs_out) = _exchange_wait_call(st_p, grp_p, tot, "swapped_halves")
    small = {
        "b_ada": jnp.concatenate([tot[16:17], tot[17:18], tot[11:12]], axis=1),
        "norm_g": tot[18:19],
        "ln_v_g": tot[24:25, :D_A],
        "ln_v_b": tot[25:26, :D_A],
        "w_spatial": packs[:, 0:8, :].reshape(GROUPS * BLK, BLK),
        "b_spatial": tot[24:32, D_A:D_A + BLK],
        "sinks": tot[24:25, D_A + LANE:D_A + LANE + N_Q],
        "b_ada_final": jnp.concatenate([tot[8:9], tot[9:10]], axis=1),
        "final_norm_g": tot[10:11],
    }

    weights = dict(w_ada=w_ada, b_ada=b_ada, norm_g=norm_g, w_in=w_in, ln_v_g=ln_v_g, ln_v_b=ln_v_b, w_spatial=w_spatial,
                   b_spatial=b_spatial, sinks=sinks, w_out=w_out, w_ada_final=w_ada_final, b_ada_final=b_ada_final,
                   final_norm_g=final_norm_g)
    m_in = dict(w_ada=m_w_ada, b_ada=m_b_ada, norm_g=m_norm_g, w_in=m_w_in, ln_v_g=m_ln_v_g, ln_v_b=m_ln_v_b,
                w_spatial=m_w_spatial, b_spatial=m_b_spatial, sinks=m_sinks, w_out=m_w_out, w_ada_final=m_w_ada_final,
                b_ada_final=m_b_ada_final, final_norm_g=m_final_norm_g)
    v_in = dict(w_ada=v_w_ada, b_ada=v_b_ada, norm_g=v_norm_g, w_in=v_w_in, ln_v_g=v_ln_v_g, ln_v_b=v_ln_v_b,
                w_spatial=v_w_spatial, b_spatial=v_b_spatial, sinks=v_sinks, w_out=v_w_out, w_ada_final=v_w_ada_final,
                b_ada_final=v_b_ada_final, final_norm_g=v_final_norm_g)
    c_act_t = c_act.T
    outer = {"w_ada": lax.dynamic_slice(dmod_all, (0, chip * n_ada), (N_DEV, n_ada)),
             "w_ada_final": lax.dynamic_slice(dmodf_all, (0, chip * n_adaf), (N_DEV, n_adaf))}
    halves = {"w_in": (mine_in, theirs_in[0]), "w_out": (mine_out, theirs_out[0])}
    done = {}
    for name, (mine, theirs) in halves.items():
        shape2 = (2 * mine.shape[0], mine.shape[1])
        done[name] = _adam_halves_call(pos, weights[name].reshape(shape2), mine, theirs, m_in[name].reshape(shape2),
                                       v_in[name].reshape(shape2), "adam_" + name)
    for name, dm in outer.items():
        shape2 = (D, dm.shape[1])
        done[name] = _adam_outer_call(weights[name].reshape(shape2), c_act_t, dm, m_in[name].reshape(shape2),
                                      v_in[name].reshape(shape2), "adam_" + name)
    updates = _adam_small_call([(weights[name].reshape(g.shape), g, m_in[name].reshape(g.shape), v_in[name].reshape(g.shape))
                                for name, g in small.items()])
    for (name, g), upd in zip(small.items(), updates):
        done[name] = (g, *upd)
    outs = [[done[name][k].reshape(w.shape) for name, w in weights.items()] for k in range(4)]
    return (loss, grad_x.reshape(x.shape), *outs[0], *outs[1], *outs[2], *outs[3])
```

```python
import numpy as np
import jax
import jax.numpy as jnp
from jax import lax
from jax.experimental import pallas as pl
from jax.experimental.pallas import tpu as pltpu

F32 = jnp.float32
BF16 = jnp.bfloat16
MESH = pl.DeviceIdType.MESH

D = 2048
D_A = 1024
D_B = 1024
D_KV = 256
HEAD = 64
N_Q = 16
N_KV = 4
Q_PER_KV = N_Q // N_KV
BLK = 128
GROUPS = 8
D_IN = 5632
OFF_Q, OFF_K, OFF_V, OFF_ZB = 3072, 4096, 4352, 4608
N_CHIP = 4
N_DEV = 8
W_IN_SHARD = D_IN // N_CHIP
W_OUT_SHARD = D // N_CHIP
EPS = 1e-5
SCALE = HEAD ** -0.5
NEG = -1e30
LANE = 128
VMEM_LIMIT = 56 * 1024 * 1024

ADAM_LR, ADAM_B1, ADAM_B2, ADAM_EPS, ADAM_WD, ADAM_STEP = 0.001, 0.9, 0.999, 1e-08, 0.01, 10
ADAM_C1 = 1.0 - ADAM_B1 ** ADAM_STEP
ADAM_C2 = 1.0 - ADAM_B2 ** ADAM_STEP
ADAM_ROWS = 256

NT = (((1,), (1,)), ((), ()))
TN = (((0,), (0,)), ((), ()))


def _params(*sem):
    return pltpu.CompilerParams(dimension_semantics=sem, vmem_limit_bytes=VMEM_LIMIT)


def _silu_parts(z):
    sig = 1.0 / (1.0 + jnp.exp(-z))
    return z * sig, sig


def _swap_halves(v, first_half):
    return jnp.where(first_half, pltpu.roll(v, 96, 1), pltpu.roll(v, 32, 1))


def _rope(v, cos_t, sin_s, first_half):
    return v * cos_t + _swap_halves(v, first_half) * sin_s


def _unrope(dv, cos_t, sin_s, first_half):
    return dv * cos_t - _swap_halves(dv, first_half) * sin_s


def _lane_masks():
    lane = lax.broadcasted_iota(jnp.int32, (BLK, LANE), 1)
    return (lane % HEAD) < (HEAD // 2), lane < HEAD


def _band_valid(first_block_bound, rows=BLK):
    rr = lax.broadcasted_iota(jnp.int32, (rows, 2 * BLK), 0) & (BLK - 1)
    jj = lax.broadcasted_iota(jnp.int32, (rows, 2 * BLK), 1)
    return (jj > rr) & (jj <= rr + BLK) & (jj >= first_block_bound)


def _dup_kv(slab, lo):
    rolled = pltpu.roll(slab, HEAD, 1)
    return jnp.where(lo, slab, rolled).astype(BF16), jnp.where(lo, rolled, slab).astype(BF16)


def _stack_heads(ref, sb, slab, lo, dtype):
    kh, base = sb // 2, 2 * (sb % 2) * BLK
    zero = jnp.zeros_like(slab)
    ref[kh, base:base + BLK, :] = jnp.where(lo, slab, zero).astype(dtype)
    ref[kh, base + BLK:base + 2 * BLK, :] = jnp.where(lo, zero, slab).astype(dtype)


def _unstack_heads(ref, sb, lo):
    kh, base = sb // 2, 2 * (sb % 2) * BLK
    return jnp.where(lo, ref[kh, base:base + BLK, :], ref[kh, base + BLK:base + 2 * BLK, :])


def _sink_column(sinks_ref, kh):
    row = lax.broadcasted_iota(jnp.int32, (Q_PER_KV * BLK, 1), 0)
    col = jnp.full(row.shape, sinks_ref[Q_PER_KV * kh + Q_PER_KV - 1], F32)
    for n in range(Q_PER_KV - 2, -1, -1):
        col = jnp.where(row < (n + 1) * BLK, sinks_ref[Q_PER_KV * kh + n], col)
    return col


def _tril():
    t = lax.broadcasted_iota(jnp.int32, (BLK, BLK), 0)
    s = lax.broadcasted_iota(jnp.int32, (BLK, BLK), 1)
    return s <= t


def _layer_norm_fwd(va, lg, lb):
    mu = jnp.mean(va, axis=-1, keepdims=True)
    xc = va - mu
    rstd = lax.rsqrt(jnp.mean(xc * xc, axis=-1, keepdims=True) + EPS)
    vhat = xc * rstd
    return vhat, rstd, vhat * lg + lb


def _softmax_sink(qm, kdup, bias, sink):
    s = lax.dot_general(qm, kdup, NT, preferred_element_type=F32) + bias
    m = jnp.maximum(jnp.max(s, axis=-1, keepdims=True), sink)
    p = jnp.exp(s - m)
    esink = jnp.exp(sink - m)
    inv = 1.0 / (jnp.sum(p, axis=-1, keepdims=True) + esink)
    return p * inv, esink * inv


def _band_bias(bias_ref):
    rows = bias_ref.shape[1]
    bias_ref[0] = jnp.where(_band_valid(BLK, rows), 0.0, NEG)
    bias_ref[1] = jnp.where(_band_valid(0, rows), 0.0, NEG)


def _rowmat_call(c_all, w, b, name):
    n = w.shape[1]
    tn = 512

    def body(c_ref, w_ref, b_ref, o_ref, ca_ref):
        ca, _ = _silu_parts(c_ref[...])
        ca_ref[...] = ca
        o_ref[...] = jnp.dot(ca.astype(BF16), w_ref[...].astype(BF16), preferred_element_type=F32) + b_ref[...]

    return pl.pallas_call(
        body, name=name, grid=(n // tn,),
        in_specs=[pl.BlockSpec((N_DEV, D), lambda j: (0, 0)), pl.BlockSpec((D, tn), lambda j: (0, j)),
                  pl.BlockSpec((1, tn), lambda j: (0, j))],
        out_specs=[pl.BlockSpec((N_DEV, tn), lambda j: (0, j)), pl.BlockSpec((N_DEV, D), lambda j: (0, 0))],
        out_shape=[jax.ShapeDtypeStruct((N_DEV, n), F32), jax.ShapeDtypeStruct((N_DEV, D), F32)],
        compiler_params=_params("arbitrary"),
    )(c_all, w, b)


def _cast_into_call(pos, w, full_shape, name):
    r, n = w.shape
    tr = min(r, 512)
    by_cols = full_shape[0] == r
    nrb = r // tr

    def body(pos_ref, w_ref, o_ref):
        o_ref[...] = w_ref[...].astype(BF16)

    out_map = (lambda i, pos: (i, pos[0])) if by_cols else (lambda i, pos: (pos[0] * nrb + i, 0))
    return pl.pallas_call(
        body, name=name,
        grid_spec=pltpu.PrefetchScalarGridSpec(
            num_scalar_prefetch=1, grid=(nrb,),
            in_specs=[pl.BlockSpec((tr, n), lambda i, pos: (i, 0))], out_specs=pl.BlockSpec((tr, n), out_map)),
        out_shape=jax.ShapeDtypeStruct(full_shape, BF16), compiler_params=_params("parallel"),
    )(pos, w)


W_IN_PARTS = ((0, 768), (768, 640))
OUT_STREAMS = 4
X_STREAMS = 4


def _proj_gather_call(pos, x, shift, scale, norm_g, wi_full, wo_full):
    s = x.shape[0]
    tm = min(s, 512)
    nrow = s // tm
    hi = D // 2
    ho = W_OUT_SHARD // 2
    phases = [(0, None), (1, 0), (2, 0), (1, 1), (2, 1), (3, 0), (3, 1)]

    def body(pos_ref, *refs):
        x_refs = refs[:X_STREAMS]
        (sh_ref, sc_ref, g_ref, _, _, h_ref, proj_ref, fi_ref, fo_ref,
         h_all, wbuf, obuf, send_sems, recv_sems, load_sems, out_sems) = refs[X_STREAMS:]
        p = pl.program_id(0)
        i = pl.program_id(1)
        x_, y_, c_ = _coords()
        me, sibling = (x_, y_, c_), (x_, y_, 1 - c_)

        def shard_of(q):
            px, py, _ = _peer(x_, y_, c_, q, 0)
            return 2 * px + py

        def cols_of(q, cp):
            off, w = (0, W_IN_SHARD) if cp is None else W_IN_PARTS[cp]
            return shard_of(q) * W_IN_SHARD + off, w

        def part(which, q, pc, sub, cp):
            n = hi if which == 0 else ho
            base = pc * n
            if sub is not None:
                n //= 2
                base = base + sub * n
            if which == 0:
                c0, w = cols_of(q, cp)
                return fi_ref.at[pl.ds(base, n), pl.ds(c0, w)]
            return fo_ref.at[pl.ds(shard_of(q) * W_OUT_SHARD + base, n), :]

        def copy(k, ref, to):
            return pltpu.make_async_remote_copy(src_ref=ref, dst_ref=ref, send_sem=send_sems.at[k], recv_sem=recv_sems.at[k],
                                                device_id=to, device_id_type=MESH)

        def sem(which, kind, j, cp):
            return 4 * kind + 2 * cp + j if which == 0 else 16 + 2 * kind + j

        def to_neighbour(which, q, cp=None):
            return copy(sem(which, 0, q - 1, cp), part(which, 0, c_, None, cp), _peer(x_, y_, c_, q, 0))

        def from_neighbour(which, q, cp=None):
            return copy(sem(which, 0, q - 1, cp), part(which, q, c_, None, cp), me)

        def relay(which, q, cp=None):
            return copy(sem(which, 1, q - 1, cp), part(which, q, c_, q - 1, cp), _peer(x_, y_, c_, 3 - q, 0))

        def relayed(which, sub, cp=None):
            return copy(sem(which, 1, sub, cp), part(which, 3, c_, sub, cp), me)

        def to_sibling(which, q, cp=None):
            return copy(sem(which, 2, q - 1, cp), part(which, q, c_, None, cp), sibling)

        def from_sibling(which, q, cp=None):
            return copy(sem(which, 2, q - 1, cp), part(which, q, 1 - c_, None, cp), me)

        def relayed_to_sibling(which, sub, cp=None):
            return copy(sem(which, 3, sub, cp), part(which, 3, c_, sub, cp), sibling)

        def relayed_from_sibling(which, sub, cp=None):
            return copy(sem(which, 3, sub, cp), part(which, 3, 1 - c_, sub, cp), me)

        def pass_on_neighbours(which, cp=None):
            for q in (1, 2):
                from_neighbour(which, q, cp).wait_recv()
                to_sibling(which, q, cp).start()
                relay(which, q, cp).start()

        def pass_on_relayed(which, cp=None):
            for sub in range(2):
                relayed(which, sub, cp).wait_recv()
                relayed_to_sibling(which, sub, cp).start()

        def shard_load(k):
            c0, w = cols_of(*phases[k])
            return pltpu.make_async_copy(fi_ref.at[:, pl.ds(c0, w)], wbuf.at[k % 2, :, 0:w], load_sems.at[k % 2])

        class OutCopies:
            def __init__(self, k, slot, row0):
                c0, w = cols_of(*phases[k])
                strip = tm // OUT_STREAMS
                self.copies = [pltpu.make_async_copy(obuf.at[slot, n * strip:(n + 1) * strip, 0:w],
                                                     proj_ref.at[pl.ds(row0 + n * strip, strip), pl.ds(c0, w)],
                                                     out_sems.at[slot, n]) for n in range(OUT_STREAMS)]

            def start(self):
                for cp in self.copies:
                    cp.start()

            def wait(self):
                for cp in self.copies:
                    cp.wait()

        out_copy = OutCopies

        def drain(k):
            for j in range(min(2, nrow)):
                out_copy(k, (nrow - 1 - j) % 2, 0).wait()

        def arrivals(k):
            q, cp = phases[k]
            if k == 0:
                for cp_ in range(2):
                    for q_ in (1, 2):
                        to_neighbour(0, q_, cp_).start()
            elif q < 3 and k in (1, 3):
                pass_on_neighbours(0, cp)
                if k == 1:
                    for q_ in (1, 2):
                        to_neighbour(1, q_).start()
            elif k == 5:
                for cp_ in range(2):
                    pass_on_relayed(0, cp_)
                pass_on_neighbours(1)
            if q in (1, 2):
                from_sibling(0, q, cp).wait_recv()
            elif q == 3:
                for sub in range(2):
                    relayed_from_sibling(0, sub, cp).wait_recv()

        rows = pl.ds(pl.multiple_of(i * tm, tm), tm)
        slot = i % 2
        for k, (q, cp) in enumerate(phases):
            @pl.when(p == k)
            def _(k=k, q=q, cp=cp):
                @pl.when(i == 0)
                def _():
                    if k == 0:
                        arrivals(0)
                        shard_load(0).start()
                    else:
                        drain(k - 1)
                    shard_load(k).wait()

                if k + 1 < len(phases):
                    @pl.when(i == max(nrow - 2, 0))
                    def _():
                        arrivals(k + 1)
                        shard_load(k + 1).start()

                if k == 0:
                    wx = D // X_STREAMS
                    ssq = sum(jnp.sum(xr[...] * xr[...], axis=-1, keepdims=True) for xr in x_refs)
                    r = lax.rsqrt(ssq * (1.0 / D) + EPS)
                    for n, xr in enumerate(x_refs):
                        cols = slice(n * wx, (n + 1) * wx)
                        hv = ((xr[...] * r * g_ref[:, cols]) * (1.0 + sc_ref[:, cols]) + sh_ref[:, cols]).astype(BF16)
                        h_ref[:, cols] = hv
                        h_all[rows, cols] = hv

                @pl.when(i >= 2)
                def _():
                    out_copy(k, slot, 0).wait()

                w = cols_of(q, cp)[1]
                obuf[slot, :, 0:w] = jnp.dot(h_all[rows, :], wbuf[k % 2, :, 0:w], preferred_element_type=F32)
                out_copy(k, slot, pl.multiple_of(i * tm, tm)).start()

        @pl.when((p == len(phases) - 1) & (i == nrow - 1))
        def _():
            drain(len(phases) - 1)
            pass_on_relayed(1)
            for q in (1, 2):
                from_sibling(1, q).wait_recv()
            for sub in range(2):
                relayed_from_sibling(1, sub).wait_recv()
            for which, cps in ((0, (0, 1)), (1, (None,))):
                for cp in cps:
                    for q in (1, 2):
                        to_neighbour(which, q, cp).wait_send()
                        relay(which, q, cp).wait_send()
                        to_sibling(which, q, cp).wait_send()
                        relayed_to_sibling(which, q - 1, cp).wait_send()

    vec = pl.BlockSpec((1, D), lambda p, i, pos: (0, 0))
    first_phase_rows = lambda p, i, pos: (jnp.where(p == 0, i, nrow - 1), 0)
    anyspec = pl.BlockSpec(memory_space=pl.ANY)
    x_spec = lambda n: pl.BlockSpec((tm, D // X_STREAMS), lambda p, i, pos: (jnp.where(p == 0, i, nrow - 1), n))
    return pl.pallas_call(
        body, name="proj_gather",
        grid_spec=pltpu.PrefetchScalarGridSpec(
            num_scalar_prefetch=1, grid=(len(phases), nrow),
            in_specs=[x_spec(n) for n in range(X_STREAMS)] + [vec, vec, vec, anyspec, anyspec],
            out_specs=[pl.BlockSpec((tm, D), first_phase_rows), anyspec, anyspec, anyspec],
            scratch_shapes=[pltpu.VMEM((s, D), BF16), pltpu.VMEM((2, D, W_IN_SHARD), BF16), pltpu.VMEM((2, tm, W_IN_SHARD), F32),
                            pltpu.SemaphoreType.DMA((24,)), pltpu.SemaphoreType.DMA((24,)), pltpu.SemaphoreType.DMA((2,)),
                            pltpu.SemaphoreType.DMA((2, OUT_STREAMS))]),
        out_shape=[jax.ShapeDtypeStruct((s, D), BF16), jax.ShapeDtypeStruct((s, D_IN), F32),
                   jax.ShapeDtypeStruct((D, D_IN), BF16), jax.ShapeDtypeStruct((D, D), BF16)],
        input_output_aliases={X_STREAMS + 4: 2, X_STREAMS + 5: 3},
        compiler_params=_params("arbitrary", "arbitrary"),
    )(pos, *([x] * X_STREAMS), shift, scale, norm_g, wi_full, wo_full)


def _proj_specs(rev_nb=None):
    if rev_nb is None:
        row = lambda i: i
    else:
        row = lambda i: rev_nb - 1 - i
    wide = lambda col: pl.BlockSpec((BLK, D_A), lambda i: (row(i), col))
    kv = lambda col: pl.BlockSpec((BLK, D_KV), lambda i: (row(i), col))
    half = lambda col: pl.BlockSpec((BLK, 512), lambda i: (row(i), col))
    return [wide(0), wide(1), wide(2), wide(3), kv(OFF_K // D_KV), kv(OFF_V // D_KV), half(OFF_ZB // 512), half(OFF_ZB // 512 + 1)]


def _mix_fwd_call(proj, cos, sin, ln_g, ln_b, w_sp, b_sp_t, sinks):
    s = proj.shape[0]
    nb = s // BLK

    def body(ua_ref, va_ref, za_ref, q_ref, k_ref, v_ref, zb0_ref, zb1_ref, cos_ref, sin_ref, lg_ref, lb_ref,
             w_ref, bt_ref, sinks_ref, y_ref, probs_ref, ost_ref, psink_ref, kdup_ref, vdup_ref, qm_ref, bias_ref):
        i = pl.program_id(0)
        first_half, lo = _lane_masks()
        cos_t = cos_ref[...]
        sin_t = sin_ref[...]

        _, _, vln = _layer_norm_fwd(va_ref[...], lg_ref[...], lb_ref[...])
        tril = _tril()
        for g in range(GROUPS):
            cols = slice(g * BLK, (g + 1) * BLK)
            wg = jnp.where(tril, w_ref[g], 0.0).astype(BF16)
            sg = jnp.dot(wg, vln[:, cols].astype(BF16), preferred_element_type=F32) + bt_ref[:, g:g + 1]
            gate, _ = _silu_parts(za_ref[:, cols])
            y_ref[:, cols] = (ua_ref[:, cols] * sg * gate).astype(BF16)

        @pl.when(i == 0)
        def _():
            kdup_ref[:, 0:BLK, :] = jnp.zeros((N_KV, BLK, LANE), BF16)
            vdup_ref[:, 0:BLK, :] = jnp.zeros((N_KV, BLK, LANE), BF16)
            _band_bias(bias_ref)

        @pl.when(i > 0)
        def _():
            kdup_ref[:, 0:BLK, :] = kdup_ref[:, BLK:2 * BLK, :]
            vdup_ref[:, 0:BLK, :] = vdup_ref[:, BLK:2 * BLK, :]

        for ks in range(2):
            cols = slice(ks * LANE, (ks + 1) * LANE)
            kr = _rope(k_ref[:, cols], cos_t, sin_t, first_half)
            for n, (kd, vd) in enumerate(zip(_dup_kv(kr, lo), _dup_kv(v_ref[:, cols], lo))):
                kdup_ref[2 * ks + n, BLK:2 * BLK, :] = kd
                vdup_ref[2 * ks + n, BLK:2 * BLK, :] = vd
        for sb in range(8):
            _stack_heads(qm_ref, sb, _rope(q_ref[:, sb * LANE:(sb + 1) * LANE], cos_t, sin_t, first_half) * SCALE, lo, BF16)

        block_kind = jnp.where(i > 0, 1, 0)

        psink_ref[...] = jnp.zeros((Q_PER_KV * BLK, LANE), F32)
        lane_q = lax.broadcasted_iota(jnp.int32, (Q_PER_KV * BLK, LANE), 1)

        def kv_head(kh, carry):
            probs, psink = _softmax_sink(qm_ref[kh], kdup_ref[kh], bias_ref[block_kind], _sink_column(sinks_ref, kh))
            probs_ref[kh] = probs
            psink_ref[...] = jnp.where(lane_q == kh, psink, psink_ref[...])
            ost_ref[kh] = jnp.dot(probs.astype(BF16), vdup_ref[kh], preferred_element_type=F32)
            return carry

        lax.fori_loop(0, N_KV, kv_head, 0, unroll=2)
        for sb in range(8):
            cols = slice(sb * LANE, (sb + 1) * LANE)
            zb = zb0_ref[:, cols] if sb < 4 else zb1_ref[:, (sb - 4) * LANE:(sb - 3) * LANE]
            gate, _ = _silu_parts(zb)
            y_ref[:, D_A + sb * LANE:D_A + (sb + 1) * LANE] = (_unstack_heads(ost_ref, sb, lo) * gate).astype(BF16)

    tab = pl.BlockSpec((BLK, LANE), lambda i: (i, 0))
    return pl.pallas_call(
        body, name="mix_fwd", grid=(nb,),
        in_specs=_proj_specs() + [
            tab, tab, pl.BlockSpec((1, D_A), lambda i: (0, 0)), pl.BlockSpec((1, D_A), lambda i: (0, 0)),
            pl.BlockSpec((GROUPS, BLK, BLK), lambda i: (0, 0, 0)), pl.BlockSpec((BLK, GROUPS), lambda i: (0, 0)),
            pl.BlockSpec(memory_space=pltpu.SMEM)],
        out_specs=[pl.BlockSpec((BLK, 2 * D_A), lambda i: (i, 0)),
                   pl.BlockSpec((None, N_KV, Q_PER_KV * BLK, 2 * BLK), lambda i: (i, 0, 0, 0)),
                   pl.BlockSpec((None, N_KV, Q_PER_KV * BLK, LANE), lambda i: (i, 0, 0, 0)),
                   pl.BlockSpec((None, Q_PER_KV * BLK, LANE), lambda i: (i, 0, 0))],
        out_shape=[jax.ShapeDtypeStruct((s, 2 * D_A), BF16), jax.ShapeDtypeStruct((nb, N_KV, Q_PER_KV * BLK, 2 * BLK), F32),
                   jax.ShapeDtypeStruct((nb, N_KV, Q_PER_KV * BLK, LANE), F32), jax.ShapeDtypeStruct((nb, Q_PER_KV * BLK, LANE), F32)],
        scratch_shapes=[pltpu.VMEM((N_KV, 2 * BLK, LANE), BF16), pltpu.VMEM((N_KV, 2 * BLK, LANE), BF16),
                        pltpu.VMEM((N_KV, Q_PER_KV * BLK, LANE), BF16), pltpu.VMEM((2, Q_PER_KV * BLK, 2 * BLK), F32)],
        compiler_params=_params("arbitrary"),
    )(proj, proj, proj, proj, proj, proj, proj, proj, cos, sin, ln_g, ln_b, w_sp, b_sp_t, sinks)


def _tail_call(y, w_out_bf, x, target, gate, shift_f, scale_f, gf):
    s = x.shape[0]
    tm = min(s, 256)
    nsteps = s // tm

    def body(y_ref, w_ref, x_ref, t_ref, gate_ref, shf_ref, scf_ref, gf_ref, dx2_ref, do_ref, dy_ref, st_ref):
        i = pl.program_id(0)

        @pl.when(i == 0)
        def _():
            st_ref[...] = jnp.zeros((8, D), F32)

        o = jnp.dot(y_ref[...], w_ref[...], preferred_element_type=F32)
        gate_v = gate_ref[...]
        x2 = x_ref[...] + gate_v * o
        r2 = lax.rsqrt(jnp.mean(x2 * x2, axis=-1, keepdims=True) + EPS)
        xn2 = x2 * r2
        hn2 = xn2 * gf_ref[...]
        one_sc = 1.0 + scf_ref[...]
        err = hn2 * one_sc + shf_ref[...] - t_ref[...]
        dout = err * (1.0 / D)
        dhn2 = dout * one_sc
        dxn2 = dhn2 * gf_ref[...]
        dx2 = r2 * (dxn2 - xn2 * jnp.mean(dxn2 * xn2, axis=-1, keepdims=True))
        dx2_ref[...] = dx2
        do = (dx2 * gate_v).astype(BF16)
        do_ref[...] = do
        dy_ref[...] = lax.dot_general(do, w_ref[...], NT, preferred_element_type=F32)
        st_ref[0:1, :] += jnp.sum(dout, axis=0, keepdims=True)
        st_ref[1:2, :] += jnp.sum(dout * hn2, axis=0, keepdims=True)
        st_ref[2:3, :] += jnp.sum(dhn2 * xn2, axis=0, keepdims=True)
        st_ref[3:4, :] += jnp.sum(dx2 * o, axis=0, keepdims=True)
        st_ref[4:5, :] += jnp.sum(err * err, axis=0, keepdims=True)

        @pl.when(i == nsteps - 1)
        def _():
            st_ref[5:6, :] = jnp.full((1, D), 0.5 / D, F32) * jnp.sum(st_ref[4:5, :])

    vec = pl.BlockSpec((1, D), lambda i: (0, 0))
    rows = lambda: pl.BlockSpec((tm, D), lambda i: (i, 0))
    return pl.pallas_call(
        body, name="tail", grid=(nsteps,),
        in_specs=[rows(), pl.BlockSpec((D, D), lambda i: (0, 0)), rows(), rows(), vec, vec, vec, vec],
        out_specs=[rows(), rows(), rows(), pl.BlockSpec((8, D), lambda i: (0, 0))],
        out_shape=[jax.ShapeDtypeStruct((s, D), F32), jax.ShapeDtypeStruct((s, D), BF16), jax.ShapeDtypeStruct((s, D), F32),
                   jax.ShapeDtypeStruct((8, D), F32)],
        compiler_params=_params("arbitrary"),
    )(y, w_out_bf, x, target, gate, shift_f, scale_f, gf)


def _tn_call(a, b, name):
    s, m = a.shape
    n = b.shape[1]
    tn = 512
    ts = min(s, 1024)
    nk = s // ts

    def body(a_ref, b_ref, o_ref, acc_ref):
        k = pl.program_id(1)

        @pl.when(k == 0)
        def _():
            acc_ref[...] = jnp.zeros((m, tn), F32)

        acc_ref[...] += lax.dot_general(a_ref[...], b_ref[...], TN, preferred_element_type=F32)

        @pl.when(k == nk - 1)
        def _():
            o_ref[...] = acc_ref[...].astype(BF16)

    return pl.pallas_call(
        body, name=name, grid=(n // tn, nk),
        in_specs=[pl.BlockSpec((ts, m), lambda j, k: (k, 0)), pl.BlockSpec((ts, tn), lambda j, k: (k, j))],
        out_specs=pl.BlockSpec((m, tn), lambda j, k: (0, j)),
        out_shape=jax.ShapeDtypeStruct((m, n), BF16),
        scratch_shapes=[pltpu.VMEM((m, tn), F32)],
        compiler_params=_params("parallel", "arbitrary"),
    )(a, b)


def _tn_shards_call(pos, a, b, qs, name):
    s, m = a.shape
    ts = min(s, 1024)
    nk = s // ts

    def body(pos_ref, a_ref, b_ref, o_ref, acc_ref):
        k = pl.program_id(1)

        @pl.when(k == 0)
        def _():
            acc_ref[...] = jnp.zeros((m, W_IN_SHARD), F32)

        acc_ref[...] += lax.dot_general(a_ref[...], b_ref[...], TN, preferred_element_type=F32)

        @pl.when(k == nk - 1)
        def _():
            o_ref[...] = acc_ref[...].astype(BF16)

    def shard(j, pos):
        q = qs[0]
        for n in range(1, len(qs)):
            q = jnp.where(j == n, qs[n], q)
        return jnp.bitwise_xor(pos[0], q)

    return pl.pallas_call(
        body, name=name,
        grid_spec=pltpu.PrefetchScalarGridSpec(
            num_scalar_prefetch=1, grid=(len(qs), nk),
            in_specs=[pl.BlockSpec((ts, m), lambda j, k, pos: (k, 0)),
                      pl.BlockSpec((ts, W_IN_SHARD), lambda j, k, pos: (k, shard(j, pos)))],
            out_specs=pl.BlockSpec((m, W_IN_SHARD), lambda j, k, pos: (0, j)),
            scratch_shapes=[pltpu.VMEM((m, W_IN_SHARD), F32)]),
        out_shape=jax.ShapeDtypeStruct((m, len(qs) * W_IN_SHARD), BF16),
        compiler_params=_params("parallel", "arbitrary"),
    )(pos, a, b)


def _mix_bwd_call(proj, dy, probs, outs, psinks, tables, ln_g, ln_b, w_sp, w_sp_t, b_sp_t):
    s = proj.shape[0]
    nb = s // BLK
    rev = lambda i: nb - 1 - i
    prev = lambda i: jnp.maximum(nb - 2 - i, 0)

    def body(ua_ref, va_ref, za_ref, q_ref, k_ref, v_ref, zb0_ref, zb1_ref, kp_ref, vp_ref, dy_ref,
             probs_ref, ost_ref, psink_ref, cos_ref, sin_ref, cosp_ref, sinp_ref, lg_ref, lb_ref, w_ref, wt_ref, bt_ref,
             dp_ref, lnst_ref, dw_ref, dbt_ref, dsink_ref,
             kdup_ref, vdup_ref, dvln_ref, qm_ref, dom_ref, dqst_ref, dkdup_ref, dvdup_ref, kcar_ref, vcar_ref, sigb_ref):
        i = pl.program_id(0)
        first_half, lo = _lane_masks()
        lane8 = lax.broadcasted_iota(jnp.int32, (8, LANE), 1)
        cos_t = cos_ref[...]
        sin_t = sin_ref[...]

        @pl.when(i == 0)
        def _():
            lnst_ref[...] = jnp.zeros((8, D_A), F32)
            dw_ref[...] = jnp.zeros((GROUPS, BLK, BLK), F32)
            dbt_ref[...] = jnp.zeros((BLK, LANE), F32)
            dsink_ref[...] = jnp.zeros((8, LANE), F32)
            kcar_ref[...] = jnp.zeros((BLK, D_KV), F32)
            vcar_ref[...] = jnp.zeros((BLK, D_KV), F32)

        vhat, rstd, vln = _layer_norm_fwd(va_ref[...], lg_ref[...], lb_ref[...])
        tril = _tril()
        triu = jnp.logical_not(tril) | (lax.broadcasted_iota(jnp.int32, (BLK, BLK), 0) == lax.broadcasted_iota(jnp.int32, (BLK, BLK), 1))
        lane_b = lax.broadcasted_iota(jnp.int32, (BLK, LANE), 1)
        db_acc = jnp.zeros((BLK, LANE), F32)
        for g in range(GROUPS):
            cols = slice(g * BLK, (g + 1) * BLK)
            vln_g = vln[:, cols].astype(BF16)
            wg = jnp.where(tril, w_ref[g], 0.0).astype(BF16)
            sg = jnp.dot(wg, vln_g, preferred_element_type=F32) + bt_ref[:, g:g + 1]
            za = za_ref[:, cols]
            gate, sig = _silu_parts(za)
            ua = ua_ref[:, cols]
            dya_g = dy_ref[:, cols]
            dya = dya_g * gate
            dp_ref[:, cols] = (dya * sg).astype(BF16)
            dp_ref[:, 2 * D_A + g * BLK:2 * D_A + (g + 1) * BLK] = (
                dya_g * (ua * sg) * (sig * (1.0 + za * (1.0 - sig)))).astype(BF16)
            ds = dya * ua
            ds_b = ds.astype(BF16)
            wtg = jnp.where(triu, wt_ref[g], 0.0).astype(BF16)
            dvln_ref[:, cols] = jnp.dot(wtg, ds_b, preferred_element_type=F32)
            dw_ref[g] += jnp.where(tril, lax.dot_general(ds_b, vln_g, NT, preferred_element_type=F32), 0.0)
            db_acc = db_acc + jnp.where(lane_b == g, jnp.sum(ds, axis=-1, keepdims=True), 0.0)
        dbt_ref[...] += db_acc
        dvln = dvln_ref[...]
        lnst_ref[0:1, :] += jnp.sum(dvln * vhat, axis=0, keepdims=True)
        lnst_ref[1:2, :] += jnp.sum(dvln, axis=0, keepdims=True)
        dvhat = dvln * lg_ref[...]
        m1 = jnp.mean(dvhat, axis=-1, keepdims=True)
        m2 = jnp.mean(dvhat * vhat, axis=-1, keepdims=True)
        dp_ref[:, D_A:2 * D_A] = (rstd * (dvhat - m1 - vhat * m2)).astype(BF16)

        cosp = cosp_ref[...]
        sinp = sinp_ref[...]
        for ks in range(2):
            cols = slice(ks * LANE, (ks + 1) * LANE)
            kr = _rope(k_ref[:, cols], cos_t, sin_t, first_half)
            kpr = _rope(kp_ref[:, cols], cosp, sinp, first_half)
            for n, (kc, vc, kp, vp) in enumerate(zip(_dup_kv(kr, lo), _dup_kv(v_ref[:, cols], lo),
                                                     _dup_kv(kpr, lo), _dup_kv(vp_ref[:, cols], lo))):
                kdup_ref[2 * ks + n, BLK:2 * BLK, :] = kc
                vdup_ref[2 * ks + n, BLK:2 * BLK, :] = vc
                kdup_ref[2 * ks + n, 0:BLK, :] = kp
                vdup_ref[2 * ks + n, 0:BLK, :] = vp
        for sb in range(8):
            cols = slice(sb * LANE, (sb + 1) * LANE)
            _stack_heads(qm_ref, sb, _rope(q_ref[:, cols], cos_t, sin_t, first_half) * SCALE, lo, BF16)
            zb = zb0_ref[:, cols] if sb < 4 else zb1_ref[:, (sb - 4) * LANE:(sb - 3) * LANE]
            gate, sig = _silu_parts(zb)
            sigb_ref[:, cols] = sig
            _stack_heads(dom_ref, sb, dy_ref[:, D_A + sb * LANE:D_A + (sb + 1) * LANE] * gate, lo, F32)

        lane_q = lax.broadcasted_iota(jnp.int32, (Q_PER_KV * BLK, LANE), 1)

        def kv_head(kh, dsink_acc):
            qm = qm_ref[kh]
            kd = kdup_ref[kh]
            vd = vdup_ref[kh]
            probs = probs_ref[kh]
            psink = jnp.sum(jnp.where(lane_q == kh, psink_ref[...], 0.0), axis=-1, keepdims=True)
            probs_b = probs.astype(BF16)
            o = ost_ref[kh]
            dom = dom_ref[kh]
            dom_b = dom.astype(BF16)
            delta = jnp.sum(dom * o, axis=-1, keepdims=True)
            dpr = lax.dot_general(dom_b, vd, NT, preferred_element_type=F32)
            dss = (probs * (dpr - delta)).astype(BF16)
            sd = psink * delta
            for n in range(Q_PER_KV):
                dsink_acc = dsink_acc + jnp.where(lane8 == Q_PER_KV * kh + n, -jnp.sum(sd[n * BLK:(n + 1) * BLK]), 0.0)
            dqst_ref[kh] = jnp.dot(dss, kd, preferred_element_type=F32)
            dkdup_ref[kh] = lax.dot_general(dss, qm, TN, preferred_element_type=F32)
            dvdup_ref[kh] = lax.dot_general(probs_b, dom_b, TN, preferred_element_type=F32)
            return dsink_acc

        dsink_acc = lax.fori_loop(0, N_KV // 2, lambda j, acc: kv_head(2 * j + 1, kv_head(2 * j, acc)), jnp.zeros((8, LANE), F32))
        row0 = lax.broadcasted_iota(jnp.int32, (8, LANE), 0) == 0
        dsink_ref[...] += jnp.where(row0, dsink_acc, 0.0)

        for sb in range(8):
            cols = slice(sb * LANE, (sb + 1) * LANE)
            zb = zb0_ref[:, cols] if sb < 4 else zb1_ref[:, (sb - 4) * LANE:(sb - 3) * LANE]
            sig = sigb_ref[:, cols]
            dyb = dy_ref[:, D_A + sb * LANE:D_A + (sb + 1) * LANE]
            dp_ref[:, OFF_ZB + sb * LANE:OFF_ZB + (sb + 1) * LANE] = (
                dyb * _unstack_heads(ost_ref, sb, lo) * (sig * (1.0 + zb * (1.0 - sig)))).astype(BF16)
            dq_r = _unstack_heads(dqst_ref, sb, lo) * SCALE
            dp_ref[:, OFF_Q + sb * LANE:OFF_Q + (sb + 1) * LANE] = _unrope(dq_r, cos_t, sin_t, first_half).astype(BF16)

        lo2 = lax.broadcasted_iota(jnp.int32, (2 * BLK, LANE), 1) < HEAD
        for ks in range(2):
            cols = slice(ks * LANE, (ks + 1) * LANE)
            ka = dkdup_ref[2 * ks]
            kb = dkdup_ref[2 * ks + 1]
            dk_band = jnp.where(lo2, ka + pltpu.roll(ka, HEAD, 1), kb + pltpu.roll(kb, HEAD, 1))
            va_ = dvdup_ref[2 * ks]
            vb_ = dvdup_ref[2 * ks + 1]
            dv_band = jnp.where(lo2, va_ + pltpu.roll(va_, HEAD, 1), vb_ + pltpu.roll(vb_, HEAD, 1))
            dkr = dk_band[BLK:2 * BLK, :] + kcar_ref[:, cols]
            dp_ref[:, OFF_K + ks * LANE:OFF_K + (ks + 1) * LANE] = _unrope(dkr, cos_t, sin_t, first_half).astype(BF16)
            dp_ref[:, OFF_V + ks * LANE:OFF_V + (ks + 1) * LANE] = (
                dv_band[BLK:2 * BLK, :] + vcar_ref[:, cols]).astype(BF16)
            kcar_ref[:, cols] = dk_band[0:BLK, :]
            vcar_ref[:, cols] = dv_band[0:BLK, :]

    tab = pl.BlockSpec((BLK, LANE), lambda i: (rev(i), 0))
    kvp = lambda col: pl.BlockSpec((BLK, D_KV), lambda i: (prev(i), col))
    vec = pl.BlockSpec((1, D_A), lambda i: (0, 0))
    w3 = pl.BlockSpec((GROUPS, BLK, BLK), lambda i: (0, 0, 0))
    return pl.pallas_call(
        body, name="mix_bwd", grid=(nb,),
        in_specs=_proj_specs(nb) + [
            kvp(OFF_K // D_KV), kvp(OFF_V // D_KV), pl.BlockSpec((BLK, 2 * D_A), lambda i: (rev(i), 0)),
            pl.BlockSpec((None, N_KV, Q_PER_KV * BLK, 2 * BLK), lambda i: (rev(i), 0, 0, 0)),
            pl.BlockSpec((None, N_KV, Q_PER_KV * BLK, LANE), lambda i: (rev(i), 0, 0, 0)),
            pl.BlockSpec((None, Q_PER_KV * BLK, LANE), lambda i: (rev(i), 0, 0)),
            tab, tab, tab, tab, vec, vec, w3, w3, pl.BlockSpec((BLK, GROUPS), lambda i: (0, 0))],
        out_specs=[pl.BlockSpec((BLK, D_IN), lambda i: (rev(i), 0)), pl.BlockSpec((8, D_A), lambda i: (0, 0)), w3,
                   pl.BlockSpec((BLK, LANE), lambda i: (0, 0)), pl.BlockSpec((8, LANE), lambda i: (0, 0))],
        out_shape=[jax.ShapeDtypeStruct((s, D_IN), BF16), jax.ShapeDtypeStruct((8, D_A), F32),
                   jax.ShapeDtypeStruct((GROUPS, BLK, BLK), F32), jax.ShapeDtypeStruct((BLK, LANE), F32),
                   jax.ShapeDtypeStruct((8, LANE), F32)],
        scratch_shapes=[pltpu.VMEM((N_KV, 2 * BLK, LANE), BF16), pltpu.VMEM((N_KV, 2 * BLK, LANE), BF16),
                        pltpu.VMEM((BLK, D_A), F32), pltpu.VMEM((N_KV, Q_PER_KV * BLK, LANE), BF16),
                        pltpu.VMEM((N_KV, Q_PER_KV * BLK, LANE), F32), pltpu.VMEM((N_KV, Q_PER_KV * BLK, LANE), F32),
                        pltpu.VMEM((N_KV, 2 * BLK, LANE), F32), pltpu.VMEM((N_KV, 2 * BLK, LANE), F32),
                        pltpu.VMEM((BLK, D_KV), F32), pltpu.VMEM((BLK, D_KV), F32), pltpu.VMEM((BLK, D_B), F32)],
        compiler_params=_params("arbitrary"),
    )(proj, proj, proj, proj, proj, proj, proj, proj, proj, proj, dy, probs, outs, psinks, *tables, ln_g, ln_b,
      w_sp, w_sp_t, b_sp_t)


def _dh_call(dproj, w_bf, x, dx2, scale, norm_g):
    s = x.shape[0]
    tm = min(s, 512)
    tk = W_IN_SHARD
    nk = D_IN // tk

    def body(dp_ref, w_ref, x_ref, dx2_ref, sc_ref, g_ref, gx_ref, st_ref, acc_ref):
        i = pl.program_id(0)
        k = pl.program_id(1)

        @pl.when((i == 0) & (k == 0))
        def _():
            st_ref[...] = jnp.zeros((8, D), F32)

        @pl.when(k == 0)
        def _():
            acc_ref[...] = jnp.zeros((tm, D), F32)

        acc_ref[...] += lax.dot_general(dp_ref[...], w_ref[...], NT, preferred_element_type=F32)

        @pl.when(k == nk - 1)
        def _():
            g = g_ref[...]
            one_sc = 1.0 + sc_ref[...]

            def chunk(n, carry):
                rows = pl.ds(pl.multiple_of(n * BLK, BLK), BLK)
                dh = acc_ref[rows, :]
                xv = x_ref[rows, :]
                r = lax.rsqrt(jnp.mean(xv * xv, axis=-1, keepdims=True) + EPS)
                xn = xv * r
                dhn = dh * one_sc
                dxn = dhn * g
                gx_ref[rows, :] = dx2_ref[rows, :] + r * (dxn - xn * jnp.mean(dxn * xn, axis=-1, keepdims=True))
                st_ref[0:1, :] += jnp.sum(dh, axis=0, keepdims=True)
                st_ref[1:2, :] += jnp.sum(dh * (xn * g), axis=0, keepdims=True)
                st_ref[2:3, :] += jnp.sum(dhn * xn, axis=0, keepdims=True)
                return carry

            lax.fori_loop(0, tm // BLK, chunk, 0)

    vec = pl.BlockSpec((1, D), lambda i, k: (0, 0))
    rows = lambda: pl.BlockSpec((tm, D), lambda i, k: (i, 0))
    return pl.pallas_call(
        body, name="dh", grid=(s // tm, nk),
        in_specs=[pl.BlockSpec((tm, tk), lambda i, k: (i, k)), pl.BlockSpec((D, tk), lambda i, k: (0, k)), rows(), rows(), vec, vec],
        out_specs=[rows(), pl.BlockSpec((8, D), lambda i, k: (0, 0))],
        out_shape=[jax.ShapeDtypeStruct((s, D), F32), jax.ShapeDtypeStruct((8, D), F32)],
        scratch_shapes=[pltpu.VMEM((tm, D), F32)],
        compiler_params=_params("arbitrary", "arbitrary"),
    )(dproj, w_bf, x, dx2, scale, norm_g)


def _adam_math(w, g, m, v):
    m_new = ADAM_B1 * m + (1.0 - ADAM_B1) * g
    v_new = ADAM_B2 * v + (1.0 - ADAM_B2) * (g * g)
    m_hat = m_new / ADAM_C1
    v_hat = v_new / ADAM_C2
    delta = -ADAM_LR * (m_hat / (jnp.sqrt(v_hat) + ADAM_EPS) + ADAM_WD * w)
    return delta, m_new, v_new


def _adam_small_call(tensors):
    n = len(tensors)

    def body(*refs):
        ins, outs = refs[:4 * n], refs[4 * n:]
        for t in range(n):
            w_ref, g_ref, m_ref, v_ref = ins[4 * t:4 * t + 4]
            d, mo, vo = _adam_math(w_ref[...], g_ref[...], m_ref[...], v_ref[...])
            outs[3 * t][...], outs[3 * t + 1][...], outs[3 * t + 2][...] = d, mo, vo

    vm = pl.BlockSpec(memory_space=pltpu.VMEM)
    flat = [a for t in tensors for a in t]
    out = pl.pallas_call(
        body, name="adam_small", in_specs=[vm] * (4 * n), out_specs=[vm] * (3 * n),
        out_shape=[jax.ShapeDtypeStruct(t[0].shape, F32) for t in tensors for _ in range(3)],
        compiler_params=pltpu.CompilerParams(vmem_limit_bytes=VMEM_LIMIT),
    )(*flat)
    return [tuple(out[3 * t:3 * t + 3]) for t in range(n)]


def _adam_halves_call(pos, w, mine, theirs, m, v, name):
    r, n = w.shape
    half = r // 2
    tr = ADAM_ROWS
    nh = half // tr

    def body(pos_ref, w_ref, mine_ref, theirs_ref, m_ref, v_ref, g_ref, d_ref, mo_ref, vo_ref):
        is_mine = (pl.program_id(0) // nh) == pos_ref[1]
        g = jnp.where(is_mine, mine_ref[...], theirs_ref[...])
        g_ref[...] = g
        d_ref[...], mo_ref[...], vo_ref[...] = _adam_math(w_ref[...], g, m_ref[...], v_ref[...])

    spec = lambda: pl.BlockSpec((tr, n), lambda i, pos: (i, 0))

    def half_spec(core_of_half):
        def index(i, pos):
            first = core_of_half(pos) == 0
            active = (i // nh == 0) == first
            return jnp.where(active, i % nh, jnp.where(first, nh - 1, 0)), 0
        return pl.BlockSpec((tr, n), index)

    return pl.pallas_call(
        body, name=name,
        grid_spec=pltpu.PrefetchScalarGridSpec(
            num_scalar_prefetch=1, grid=(r // tr,),
            in_specs=[spec(), half_spec(lambda pos: pos[1]), half_spec(lambda pos: 1 - pos[1]), spec(), spec()],
            out_specs=[spec() for _ in range(4)]),
        out_shape=[jax.ShapeDtypeStruct((r, n), F32)] * 4, compiler_params=_params("arbitrary"),
    )(pos, w, mine, theirs, m, v)


def _adam_outer_call(w, ct, dm, m, v, name):
    r, n = w.shape
    tr = ADAM_ROWS

    def body(w_ref, ct_ref, dm_ref, m_ref, v_ref, g_ref, d_ref, mo_ref, vo_ref):
        g = ct_ref[:, 0:1] * dm_ref[0:1, :]
        for b in range(1, N_DEV):
            g = g + ct_ref[:, b:b + 1] * dm_ref[b:b + 1, :]
        g_ref[...] = g
        d_ref[...], mo_ref[...], vo_ref[...] = _adam_math(w_ref[...], g, m_ref[...], v_ref[...])

    spec = lambda: pl.BlockSpec((tr, n), lambda i: (i, 0))
    return pl.pallas_call(
        body, name=name, grid=(r // tr,),
        in_specs=[spec(), pl.BlockSpec((tr, N_DEV), lambda i: (i, 0)), pl.BlockSpec((N_DEV, n), lambda i: (0, 0)), spec(), spec()],
        out_specs=[spec() for _ in range(4)],
        out_shape=[jax.ShapeDtypeStruct((r, n), F32)] * 4, compiler_params=_params("parallel"),
    )(w, ct, dm, m, v)


def _sum_pieces_call(pos, part, part_block, recvs, name):
    r, n = recvs[0].shape[1:]
    tr = min(r, 256)
    nrb = r // tr

    def body(pos_ref, p_ref, *refs):
        acc = p_ref[...].astype(F32)
        for r_ref in refs[:-1]:
            for d in range(r_ref.shape[0]):
                acc = acc + r_ref[d].astype(F32)
        refs[-1][...] = acc

    return pl.pallas_call(
        body, name=name,
        grid_spec=pltpu.PrefetchScalarGridSpec(
            num_scalar_prefetch=1, grid=(nrb,),
            in_specs=[pl.BlockSpec((tr, n), lambda i, pos: part_block(i, pos, nrb))] + [
                pl.BlockSpec((rv.shape[0], tr, n), lambda i, pos: (0, i, 0)) for rv in recvs],
            out_specs=pl.BlockSpec((tr, n), lambda i, pos: (i, 0))),
        out_shape=jax.ShapeDtypeStruct((r, n), F32), compiler_params=_params("parallel"),
    )(pos, part, *recvs)


def _coords():
    return lax.axis_index("x"), lax.axis_index("y"), lax.axis_index("c")


def _allgather_sum_call(blk, name, with_sum):
    m_per, n = blk.shape

    def body(x_ref, out_ref, *rest):
        if with_sum:
            sum_ref, send_sems, recv_sems, local_sem = rest
        else:
            send_sems, recv_sems, local_sem = rest
        x, y, c = _coords()
        me, sibling = (x, y, c), (x, y, 1 - c)
        chips = [(1 - x, y), (x, 1 - y), (1 - x, 1 - y)]

        def rows(px, py, pc):
            return out_ref.at[pl.ds((4 * px + 2 * py + pc) * m_per, m_per), :]

        def copy(k, block, to, src=None):
            return pltpu.make_async_remote_copy(
                src_ref=rows(*block) if src is None else src, dst_ref=rows(*block),
                send_sem=send_sems.at[k], recv_sem=recv_sems.at[k], device_id=to, device_id_type=MESH)

        mine = pltpu.make_async_copy(x_ref, rows(*me), local_sem)
        mine.start()
        first = [copy(0, me, sibling, src=x_ref)]
        first += [copy(1 + j, me, (*chip, c), src=x_ref) for j, chip in enumerate(chips)]
        for cp in first:
            cp.start()
        passed = [copy(4 + j, (*chip, c), sibling) for j, chip in enumerate(chips)]
        for j, chip in enumerate(chips):
            copy(1 + j, (*chip, c), me).wait_recv()
            passed[j].start()
        copy(0, sibling, me).wait_recv()
        for j, chip in enumerate(chips):
            copy(4 + j, (*chip, 1 - c), me).wait_recv()
        for cp in first + passed:
            cp.wait_send()
        mine.wait()
        if with_sum:
            acc = out_ref[0:m_per, :]
            for d in range(1, N_DEV):
                acc = acc + out_ref[d * m_per:(d + 1) * m_per, :]
            sum_ref[...] = acc

    vm = pl.BlockSpec(memory_space=pltpu.VMEM)
    out_shape = [jax.ShapeDtypeStruct((N_DEV * m_per, n), F32)]
    if with_sum:
        out_shape.append(jax.ShapeDtypeStruct((m_per, n), F32))
    return pl.pallas_call(
        body, name=name, out_shape=out_shape, in_specs=[vm], out_specs=[vm] * len(out_shape),
        scratch_shapes=[pltpu.SemaphoreType.DMA((7,)), pltpu.SemaphoreType.DMA((7,)), pltpu.SemaphoreType.DMA],
        compiler_params=pltpu.CompilerParams(vmem_limit_bytes=VMEM_LIMIT),
    )(blk)


HBM_SPEC = pl.BlockSpec(memory_space=pltpu.HBM)
SEM_SPEC = pl.BlockSpec(memory_space=pltpu.SEMAPHORE)
SIDE_EFFECT = pltpu.SideEffectType.DATAFLOW_SIDE_EFFECTING


def _peer(x, y, c, q, cb):
    return (1 - x if q & 2 else x, 1 - y if q & 1 else y, 1 - c if cb else c)


def _w_in_piece(slots):
    def piece(part_ref, k, to):
        return part_ref.at[pl.ds(to[2] * (D // 2), D // 2), pl.ds(slots[k] * W_IN_SHARD, W_IN_SHARD)]
    return piece


def _w_out_piece(part_ref, k, to):
    ho = W_OUT_SHARD // 2
    return part_ref.at[pl.ds((2 * to[0] + to[1]) * W_OUT_SHARD + to[2] * ho, ho), :]


def _group_piece(part_ref, k, to):
    return part_ref.at[4 * to[0] + 2 * to[1] + to[2]]


def _whole_piece(part_ref, k, to):
    return part_ref


def _exchange_start_call(groups, name):
    ng = len(groups)
    lands = [lax.empty((len(rels),) + slot_shape, part.dtype) for part, rels, _, slot_shape in groups]

    def body(*refs):
        ins, outs = refs[:2 * ng], refs[2 * ng:]
        x, y, c = _coords()
        for g, (_, rels, piece, _) in enumerate(groups):
            part_ref, land_ref = ins[2 * g], ins[2 * g + 1]
            send_sems, recv_sems = outs[4 * g], outs[4 * g + 1]
            for k, (q, cb) in enumerate(rels):
                to = _peer(x, y, c, q, cb)
                pltpu.make_async_remote_copy(src_ref=piece(part_ref, k, to), dst_ref=land_ref.at[k], send_sem=send_sems.at[k],
                                             recv_sem=recv_sems.at[k], device_id=to, device_id_type=MESH).start()
        outs[-1][...] = jnp.zeros_like(outs[-1])

    out_shape, out_specs, operands = [], [], []
    for (part, rels, _, _), land in zip(groups, lands):
        n = len(rels)
        out_shape += [pltpu.SemaphoreType.DMA((n,)), pltpu.SemaphoreType.DMA((n,)), pltpu.HBM(part.shape, part.dtype),
                      pltpu.HBM(land.shape, land.dtype)]
        out_specs += [SEM_SPEC, SEM_SPEC, HBM_SPEC, HBM_SPEC]
        operands += [pltpu.with_memory_space_constraint(part, pltpu.HBM), pltpu.with_memory_space_constraint(land, pltpu.HBM)]
    out = pl.pallas_call(
        body, name=name,
        out_shape=tuple(out_shape) + (jax.ShapeDtypeStruct((1, 1), F32),),
        in_specs=(HBM_SPEC,) * (2 * ng), out_specs=tuple(out_specs) + (pl.BlockSpec(memory_space=pltpu.VMEM),),
        input_output_aliases={j: 4 * (j // 2) + 2 + j % 2 for j in range(2 * ng)},
        compiler_params=pltpu.CompilerParams(has_side_effects=SIDE_EFFECT),
    )(*operands)
    return [tuple(out[4 * g:4 * g + 4]) for g in range(ng)], out[-1]


def _exchange_wait_call(started, groups, after, name):
    ng = len(groups)

    def body(*refs):
        ins = refs[:4 * ng]
        x, y, c = _coords()
        for g, (_, rels, piece, _) in enumerate(groups):
            part_ref, land_ref, send_sems, recv_sems = ins[4 * g:4 * g + 4]
            for k, (q, cb) in enumerate(rels):
                to = _peer(x, y, c, q, cb)
                cp = pltpu.make_async_remote_copy(src_ref=piece(part_ref, k, to), dst_ref=land_ref.at[k], send_sem=send_sems.at[k],
                                                  recv_sem=recv_sems.at[k], device_id=to, device_id_type=MESH)
                cp.wait_send()
                cp.wait_recv()

    operands, in_specs, out_shape = [], [], []
    for send_sems, recv_sems, part_thru, land_thru in started:
        operands += [part_thru, land_thru, send_sems, recv_sems]
        in_specs += [HBM_SPEC, HBM_SPEC, SEM_SPEC, SEM_SPEC]
        out_shape += [pltpu.HBM(part_thru.shape, part_thru.dtype), pltpu.HBM(land_thru.shape, land_thru.dtype)]
    out = pl.pallas_call(
        body, name=name, out_shape=tuple(out_shape),
        in_specs=tuple(in_specs) + (pl.BlockSpec(memory_space=pl.ANY),), out_specs=(HBM_SPEC,) * (2 * ng),
        input_output_aliases={4 * g + j: 2 * g + j for g in range(ng) for j in range(2)},
        compiler_params=pltpu.CompilerParams(has_side_effects=SIDE_EFFECT),
    )(*operands, after)
    return [tuple(out[2 * g:2 * g + 2]) for g in range(ng)]


def _rope_tables(s):
    inv_freq = np.float32(10000.0) ** (-np.arange(0, HEAD, 2, dtype=np.float32) / np.float32(HEAD))
    ang = np.arange(s, dtype=np.float32)[:, None] * inv_freq[None, :]
    cos = np.tile(np.cos(ang), (1, LANE // (HEAD // 2))).astype(np.float32)
    sin = np.tile(np.sin(ang), (1, LANE // (HEAD // 2))).astype(np.float32)
    first_half = (np.arange(LANE) % HEAD) < (HEAD // 2)
    sin = np.where(first_half[None, :], -sin, sin)
    behind = lambda t: np.concatenate([t[:BLK], t[:-BLK]], axis=0)
    return tuple(jnp.asarray(t) for t in (cos, sin, behind(cos), behind(sin)))


def kernel(x, c, w_ada, b_ada, norm_g, w_in, ln_v_g, ln_v_b, w_spatial, b_spatial, sinks, w_out, w_ada_final, b_ada_final, final_norm_g, loss_target, m_w_ada, m_b_ada, m_norm_g, m_w_in, m_ln_v_g, m_ln_v_b, m_w_spatial, m_b_spatial, m_sinks, m_w_out, m_w_ada_final, m_b_ada_final, m_final_norm_g, v_w_ada, v_b_ada, v_norm_g, v_w_in, v_ln_v_g, v_ln_v_b, v_w_spatial, v_b_spatial, v_sinks, v_w_out, v_w_ada_final, v_b_ada_final, v_final_norm_g):
    s = x.shape[1]
    ax, ay, ac = _coords()
    chip = 2 * ax + ay
    me = 4 * ax + 2 * ay + ac
    n_ada = w_ada.shape[2]
    n_adaf = w_ada_final.shape[1]

    x2d = x.reshape(s, D)
    tgt = loss_target.reshape(s, D)
    w_ada2, w_in2, w_out2 = w_ada[0], w_in[0], w_out[0]
    b_ada_f2 = b_ada_final.reshape(1, 2 * D)
    gf = final_norm_g.reshape(1, D)

    c_all = _allgather_sum_call(jnp.pad(c, ((0, 7), (0, 0))), "gather_c", False)[0][::8]
    mod_p, c_act = _rowmat_call(c_all, w_ada2, lax.dynamic_slice(b_ada, (0, chip * n_ada), (1, n_ada)), "mod")
    modf_p, _ = _rowmat_call(c_all, w_ada_final, lax.dynamic_slice(b_ada_f2, (0, chip * n_adaf), (1, n_adaf)), "mod_final")
    mods = _allgather_sum_call(jnp.concatenate([mod_p, modf_p], axis=1), "gather_mod", False)[0]
    my_rows = [lax.dynamic_slice(mods, (16 * j + me, 0), (1, n_ada + n_adaf)) for j in range(N_CHIP)]
    mod = jnp.concatenate([r[:, :n_ada] for r in my_rows], axis=1)
    mod_f = jnp.concatenate([r[:, n_ada:] for r in my_rows], axis=1)
    shift, scale, gate = mod[:, :D], mod[:, D:2 * D], mod[:, 2 * D:]
    shift_f, scale_f = mod_f[:, :D], mod_f[:, D:]

    pos = jnp.stack([chip, ac]).astype(jnp.int32)
    w_in_own = _cast_into_call(pos, w_in2, (D, D_IN), "cast_w_in")
    w_out_own = _cast_into_call(pos, w_out2, (D, D), "cast_w_out")

    tables = _rope_tables(s)
    cos, sin = tables[:2]
    b_sp_t = b_spatial[0].T
    sinks1 = sinks.reshape(N_Q)
    h, proj, w_in_bf, w_out_bf = _proj_gather_call(pos, x2d, shift, scale, norm_g, w_in_own, w_out_own)
    y, probs, attn_out, psinks = _mix_fwd_call(proj, cos, sin, ln_v_g, ln_v_b, w_spatial[0], b_sp_t, sinks1)
    dx2, do, dy, st_tail = _tail_call(y, w_out_bf, x2d, tgt, gate, shift_f, scale_f, gf)

    rel_o = [(0, 1), (1, 0), (1, 1), (2, 0), (2, 1), (3, 0), (3, 1)]
    rel_a = [(1, 0), (1, 1), (2, 0), (2, 1)]
    rel_b = [(3, 0), (3, 1), (0, 1)]
    piece_a, piece_b = _w_in_piece([0, 0, 1, 1]), _w_in_piece([0, 0, 1])
    half_in, half_out = (D // 2, W_IN_SHARD), (W_OUT_SHARD // 2, D)

    g_w_out_p = _tn_call(y, do, "grad_w_out")
    grp_o = [(g_w_out_p, rel_o, _w_out_piece, half_out)]
    st_o, tok_o = _exchange_start_call(grp_o, "send_w_out")
    dproj, st_ln, d_wsp, d_bsp_t, d_sink = _mix_bwd_call(
        proj, dy, probs, attn_out, psinks, tables, ln_v_g + tok_o, ln_v_b, w_spatial[0], jnp.swapaxes(w_spatial[0], 1, 2),
        b_sp_t)
    g_w_in_a = _tn_shards_call(pos, h, dproj, (1, 2), "grad_w_in_a")
    grp_a = [(g_w_in_a, rel_a, piece_a, half_in), (d_wsp, rel_o, _group_piece, (BLK, BLK))]
    st_a, tok_a = _exchange_start_call(grp_a, "send_w_in_a")
    g_w_in_b = _tn_shards_call(pos, h, dproj, (3, 0), "grad_w_in_b")
    grp_b = [(g_w_in_b, rel_b, piece_b, half_in)]
    st_b, tok_b = _exchange_start_call(grp_b, "send_w_in_b")
    grad_x, st_dh = _dh_call(dproj, w_in_bf, x2d, dx2, scale + (tok_a + tok_b), norm_g)

    ((g_w_out_p, recv_o),) = _exchange_wait_call(st_o, grp_o, st_dh, "wait_w_out")
    (_, recv_a), (d_wsp, recv_s) = _exchange_wait_call(st_a, grp_a, st_dh, "wait_w_in_a")
    ((g_w_in_b, recv_b),) = _exchange_wait_call(st_b, grp_b, st_dh, "wait_w_in_b")
    mine_in = _sum_pieces_call(pos, g_w_in_b, lambda i, p, nrb: (p[1] * nrb + i, 1), [recv_a, recv_b], "sum_w_in")
    mine_out = _sum_pieces_call(pos, g_w_out_p, lambda i, p, nrb: ((2 * p[0] + p[1]) * nrb + i, 0), [recv_o], "sum_w_out")
    wsp_group = _sum_pieces_call(pos, d_wsp.reshape(GROUPS * BLK, BLK), lambda i, p, nrb: (2 * p[0] + p[1], 0), [recv_s],
                                 "sum_w_spatial")
    to_sibling = [(0, 1)]
    grp_p = [(mine_in, to_sibling, _whole_piece, half_in), (mine_out, to_sibling, _whole_piece, half_out)]
    st_p, tok_p = _exchange_start_call(grp_p, "swap_halves")

    misc = jnp.concatenate([st_ln, d_bsp_t[:, :GROUPS].T, d_sink, jnp.zeros((8, D - D_A - 2 * LANE), F32)], axis=1)
    pack = jnp.concatenate([wsp_group.reshape(8, D) + tok_p, st_tail, st_dh, misc], axis=0)
    rows = pack.shape[0]
    packs, tot = _allgather_sum_call(pack, "gather_small", True)
    packs = packs.reshape(N_DEV, rows, D)
    dmod_all = jnp.concatenate([packs[:, 16, :], packs[:, 17, :], packs[:, 11, :]], axis=1)
    dmodf_all = jnp.concatenate([packs[:, 8, :], packs[:, 9, :]], axis=1)
    loss = tot[13, 0]
    (mine_in, theirs_in), (mine_out, theirs_out) = _exchange_wait_call(st_p, grp_p, tot, "swapped_halves")
    small = {
        "b_ada": jnp.concatenate([tot[16:17], tot[17:18], tot[11:12]], axis=1),
        "norm_g": tot[18:19],
        "ln_v_g": tot[24:25, :D_A],
        "ln_v_b": tot[25:26, :D_A],
        "w_spatial": packs[:, 0:8, :].reshape(GROUPS * BLK, BLK),
        "b_spatial": tot[24:32, D_A:D_A + BLK],
        "sinks": tot[24:25, D_A + LANE:D_A + LANE + N_Q],
        "b_ada_final": jnp.concatenate([tot[8:9], tot[9:10]], axis=1),
        "final_norm_g": tot[10:11],
    }

    weights = dict(w_ada=w_ada, b_ada=b_ada, norm_g=norm_g, w_in=w_in, ln_v_g=ln_v_g, ln_v_b=ln_v_b, w_spatial=w_spatial,
                   b_spatial=b_spatial, sinks=sinks, w_out=w_out, w_ada_final=w_ada_final, b_ada_final=b_ada_final,
                   final_norm_g=final_norm_g)
    m_in = dict(w_ada=m_w_ada, b_ada=m_b_ada, norm_g=m_norm_g, w_in=m_w_in, ln_v_g=m_ln_v_g, ln_v_b=m_ln_v_b,
                w_spatial=m_w_spatial, b_spatial=m_b_spatial, sinks=m_sinks, w_out=m_w_out, w_ada_final=m_w_ada_final,
                b_ada_final=m_b_ada_final, final_norm_g=m_final_norm_g)
    v_in = dict(w_ada=v_w_ada, b_ada=v_b_ada, norm_g=v_norm_g, w_in=v_w_in, ln_v_g=v_ln_v_g, ln_v_b=v_ln_v_b,
                w_spatial=v_w_spatial, b_spatial=v_b_spatial, sinks=v_sinks, w_out=v_w_out, w_ada_final=v_w_ada_final,
                b_ada_final=v_b_ada_final, final_norm_g=v_final_norm_g)
    c_act_t = c_act.T
    outer = {"w_ada": lax.dynamic_slice(dmod_all, (0, chip * n_ada), (N_DEV, n_ada)),
             "w_ada_final": lax.dynamic_slice(dmodf_all, (0, chip * n_adaf), (N_DEV, n_adaf))}
    halves = {"w_in": (mine_in, theirs_in[0]), "w_out": (mine_out, theirs_out[0])}
    done = {}
    for name, (mine, theirs) in halves.items():
        shape2 = (2 * mine.shape[0], mine.shape[1])
        done[name] = _adam_halves_call(pos, weights[name].reshape(shape2), mine, theirs, m_in[name].reshape(shape2),
                                       v_in[name].reshape(shape2), "adam_" + name)
    for name, dm in outer.items():
        shape2 = (D, dm.shape[1])
        done[name] = _adam_outer_call(weights[name].reshape(shape2), c_act_t, dm, m_in[name].reshape(shape2),
                                      v_in[name].reshape(shape2), "adam_" + name)
    updates = _adam_small_call([(weights[name].reshape(g.shape), g, m_in[name].reshape(g.shape), v_in[name].reshape(g.shape))
                                for name, g in small.items()])
    for (name, g), upd in zip(small.items(), updates):
        done[name] = (g, *upd)
    outs = [[done[name][k].reshape(w.shape) for name, w in weights.items()] for k in range(4)]
    return (loss, grad_x.reshape(x.shape), *outs[0], *outs[1], *outs[2], *outs[3])
```

```python
import numpy as np
import jax
import jax.numpy as jnp
from jax import lax
from jax.experimental import pallas as pl
from jax.experimental.pallas import tpu as pltpu

F32 = jnp.float32
BF16 = jnp.bfloat16
MESH = pl.DeviceIdType.MESH

D = 2048
D_A = 1024
D_B = 1024
D_KV = 256
HEAD = 64
N_Q = 16
N_KV = 4
Q_PER_KV = N_Q // N_KV
BLK = 128
GROUPS = 8
D_IN = 5632
OFF_Q, OFF_K, OFF_V, OFF_ZB = 3072, 4096, 4352, 4608
N_CHIP = 4
N_DEV = 8
W_IN_SHARD = D_IN // N_CHIP
W_OUT_SHARD = D // N_CHIP
EPS = 1e-5
SCALE = HEAD ** -0.5
NEG = -1e30
LANE = 128
VMEM_LIMIT = 56 * 1024 * 1024

ADAM_LR, ADAM_B1, ADAM_B2, ADAM_EPS, ADAM_WD, ADAM_STEP = 0.001, 0.9, 0.999, 1e-08, 0.01, 10
ADAM_C1 = 1.0 - ADAM_B1 ** ADAM_STEP
ADAM_C2 = 1.0 - ADAM_B2 ** ADAM_STEP
ADAM_ROWS = 256

NT = (((1,), (1,)), ((), ()))
TN = (((0,), (0,)), ((), ()))


def _params(*sem):
    return pltpu.CompilerParams(dimension_semantics=sem, vmem_limit_bytes=VMEM_LIMIT)


def _silu_parts(z):
    sig = 1.0 / (1.0 + jnp.exp(-z))
    return z * sig, sig


def _swap_halves(v, first_half):
    return jnp.where(first_half, pltpu.roll(v, 96, 1), pltpu.roll(v, 32, 1))


def _rope(v, cos_t, sin_s, first_half):
    return v * cos_t + _swap_halves(v, first_half) * sin_s


def _unrope(dv, cos_t, sin_s, first_half):
    return dv * cos_t - _swap_halves(dv, first_half) * sin_s


def _lane_masks():
    lane = lax.broadcasted_iota(jnp.int32, (BLK, LANE), 1)
    return (lane % HEAD) < (HEAD // 2), lane < HEAD


def _band_valid(first_block_bound, rows=BLK):
    rr = lax.broadcasted_iota(jnp.int32, (rows, 2 * BLK), 0) & (BLK - 1)
    jj = lax.broadcasted_iota(jnp.int32, (rows, 2 * BLK), 1)
    return (jj > rr) & (jj <= rr + BLK) & (jj >= first_block_bound)


def _dup_kv(slab, lo):
    rolled = pltpu.roll(slab, HEAD, 1)
    return jnp.where(lo, slab, rolled).astype(BF16), jnp.where(lo, rolled, slab).astype(BF16)


def _stack_heads(ref, sb, slab, lo, dtype):
    kh, base = sb // 2, 2 * (sb % 2) * BLK
    zero = jnp.zeros_like(slab)
    ref[kh, base:base + BLK, :] = jnp.where(lo, slab, zero).astype(dtype)
    ref[kh, base + BLK:base + 2 * BLK, :] = jnp.where(lo, zero, slab).astype(dtype)


def _unstack_heads(ref, sb, lo):
    kh, base = sb // 2, 2 * (sb % 2) * BLK
    return jnp.where(lo, ref[kh, base:base + BLK, :], ref[kh, base + BLK:base + 2 * BLK, :])


def _sink_column(sinks_ref, kh):
    row = lax.broadcasted_iota(jnp.int32, (Q_PER_KV * BLK, 1), 0)
    col = jnp.full(row.shape, sinks_ref[Q_PER_KV * kh + Q_PER_KV - 1], F32)
    for n in range(Q_PER_KV - 2, -1, -1):
        col = jnp.where(row < (n + 1) * BLK, sinks_ref[Q_PER_KV * kh + n], col)
    return col


def _tril():
    t = lax.broadcasted_iota(jnp.int32, (BLK, BLK), 0)
    s = lax.broadcasted_iota(jnp.int32, (BLK, BLK), 1)
    return s <= t


def _layer_norm_fwd(va, lg, lb):
    mu = jnp.mean(va, axis=-1, keepdims=True)
    xc = va - mu
    rstd = lax.rsqrt(jnp.mean(xc * xc, axis=-1, keepdims=True) + EPS)
    vhat = xc * rstd
    return vhat, rstd, vhat * lg + lb


def _softmax_sink(qm, kdup, bias, sink):
    s = lax.dot_general(qm, kdup, NT, preferred_element_type=F32) + bias
    m = jnp.maximum(jnp.max(s, axis=-1, keepdims=True), sink)
    p = jnp.exp(s - m)
    esink = jnp.exp(sink - m)
    inv = 1.0 / (jnp.sum(p, axis=-1, keepdims=True) + esink)
    return p * inv, esink * inv


def _band_bias(bias_ref):
    rows = bias_ref.shape[1]
    bias_ref[0] = jnp.where(_band_valid(BLK, rows), 0.0, NEG)
    bias_ref[1] = jnp.where(_band_valid(0, rows), 0.0, NEG)


def _rowmat_call(c_all, w, b, name):
    n = w.shape[1]
    tn = 512

    def body(c_ref, w_ref, b_ref, o_ref, ca_ref):
        ca, _ = _silu_parts(c_ref[...])
        ca_ref[...] = ca
        o_ref[...] = jnp.dot(ca.astype(BF16), w_ref[...].astype(BF16), preferred_element_type=F32) + b_ref[...]

    return pl.pallas_call(
        body, name=name, grid=(n // tn,),
        in_specs=[pl.BlockSpec((N_DEV, D), lambda j: (0, 0)), pl.BlockSpec((D, tn), lambda j: (0, j)),
                  pl.BlockSpec((1, tn), lambda j: (0, j))],
        out_specs=[pl.BlockSpec((N_DEV, tn), lambda j: (0, j)), pl.BlockSpec((N_DEV, D), lambda j: (0, 0))],
        out_shape=[jax.ShapeDtypeStruct((N_DEV, n), F32), jax.ShapeDtypeStruct((N_DEV, D), F32)],
        compiler_params=_params("arbitrary"),
    )(c_all, w, b)


def _cast_into_call(pos, w, full_shape, name):
    r, n = w.shape
    tr = min(r, 512)
    by_cols = full_shape[0] == r
    nrb = r // tr

    def body(pos_ref, w_ref, o_ref):
        o_ref[...] = w_ref[...].astype(BF16)

    out_map = (lambda i, pos: (i, pos[0])) if by_cols else (lambda i, pos: (pos[0] * nrb + i, 0))
    return pl.pallas_call(
        body, name=name,
        grid_spec=pltpu.PrefetchScalarGridSpec(
            num_scalar_prefetch=1, grid=(nrb,),
            in_specs=[pl.BlockSpec((tr, n), lambda i, pos: (i, 0))], out_specs=pl.BlockSpec((tr, n), out_map)),
        out_shape=jax.ShapeDtypeStruct(full_shape, BF16), compiler_params=_params("parallel"),
    )(pos, w)


W_IN_PARTS = ((0, 768), (768, 640))
OUT_STREAMS = 4
X_STREAMS = 4


def _proj_gather_call(pos, x, shift, scale, norm_g, wi_full, wo_full):
    s = x.shape[0]
    tm = min(s, 512)
    nrow = s // tm
    hi = D // 2
    ho = W_OUT_SHARD // 2
    phases = [(0, None), (1, 0), (2, 0), (1, 1), (2, 1), (3, 0), (3, 1)]

    def body(pos_ref, *refs):
        x_refs = refs[:X_STREAMS]
        (sh_ref, sc_ref, g_ref, _, _, h_ref, proj_ref, fi_ref, fo_ref,
         h_all, wbuf, obuf, send_sems, recv_sems, load_sems, out_sems) = refs[X_STREAMS:]
        p = pl.program_id(0)
        i = pl.program_id(1)
        x_, y_, c_ = _coords()
        me, sibling = (x_, y_, c_), (x_, y_, 1 - c_)

        def shard_of(q):
            px, py, _ = _peer(x_, y_, c_, q, 0)
            return 2 * px + py

        def cols_of(q, cp):
            off, w = (0, W_IN_SHARD) if cp is None else W_IN_PARTS[cp]
            return shard_of(q) * W_IN_SHARD + off, w

        def part(which, q, pc, sub, cp):
            n = hi if which == 0 else ho
            base = pc * n
            if sub is not None:
                n //= 2
                base = base + sub * n
            if which == 0:
                c0, w = cols_of(q, cp)
                return fi_ref.at[pl.ds(base, n), pl.ds(c0, w)]
            return fo_ref.at[pl.ds(shard_of(q) * W_OUT_SHARD + base, n), :]

        def copy(k, ref, to):
            return pltpu.make_async_remote_copy(src_ref=ref, dst_ref=ref, send_sem=send_sems.at[k], recv_sem=recv_sems.at[k],
                                                device_id=to, device_id_type=MESH)

        def sem(which, kind, j, cp):
            return 4 * kind + 2 * cp + j if which == 0 else 16 + 2 * kind + j

        def to_neighbour(which, q, cp=None):
            return copy(sem(which, 0, q - 1, cp), part(which, 0, c_, None, cp), _peer(x_, y_, c_, q, 0))

        def from_neighbour(which, q, cp=None):
            return copy(sem(which, 0, q - 1, cp), part(which, q, c_, None, cp), me)

        def relay(which, q, cp=None):
            return copy(sem(which, 1, q - 1, cp), part(which, q, c_, q - 1, cp), _peer(x_, y_, c_, 3 - q, 0))

        def relayed(which, sub, cp=None):
            return copy(sem(which, 1, sub, cp), part(which, 3, c_, sub, cp), me)

        def to_sibling(which, q, cp=None):
            return copy(sem(which, 2, q - 1, cp), part(which, q, c_, None, cp), sibling)

        def from_sibling(which, q, cp=None):
            return copy(sem(which, 2, q - 1, cp), part(which, q, 1 - c_, None, cp), me)

        def relayed_to_sibling(which, sub, cp=None):
            return copy(sem(which, 3, sub, cp), part(which, 3, c_, sub, cp), sibling)

        def relayed_from_sibling(which, sub, cp=None):
            return copy(sem(which, 3, sub, cp), part(which, 3, 1 - c_, sub, cp), me)

        def pass_on_neighbours(which, cp=None):
            for q in (1, 2):
                from_neighbour(which, q, cp).wait_recv()
                to_sibling(which, q, cp).start()
                relay(which, q, cp).start()

        def pass_on_relayed(which, cp=None):
            for sub in range(2):
                relayed(which, sub, cp).wait_recv()
                relayed_to_sibling(which, sub, cp).start()

        def shard_load(k):
            c0, w = cols_of(*phases[k])
            return pltpu.make_async_copy(fi_ref.at[:, pl.ds(c0, w)], wbuf.at[k % 2, :, 0:w], load_sems.at[k % 2])

        class OutCopies:
            def __init__(self, k, slot, row0):
                c0, w = cols_of(*phases[k])
                strip = tm // OUT_STREAMS
                self.copies = [pltpu.make_async_copy(obuf.at[slot, n * strip:(n + 1) * strip, 0:w],
                                                     proj_ref.at[pl.ds(row0 + n * strip, strip), pl.ds(c0, w)],
                                                     out_sems.at[slot, n]) for n in range(OUT_STREAMS)]

            def start(self):
                for cp in self.copies:
                    cp.start()

            def wait(self):
                for cp in self.copies:
                    cp.wait()

        out_copy = OutCopies

        def drain(k):
            for j in range(min(2, nrow)):
                out_copy(k, (nrow - 1 - j) % 2, 0).wait()

        def arrivals(k):
            q, cp = phases[k]
            if k == 0:
                for cp_ in range(2):
                    for q_ in (1, 2):
                        to_neighbour(0, q_, cp_).start()
            elif q < 3 and k in (1, 3):
                pass_on_neighbours(0, cp)
                if k == 1:
                    for q_ in (1, 2):
                        to_neighbour(1, q_).start()
            elif k == 5:
                for cp_ in range(2):
                    pass_on_relayed(0, cp_)
                pass_on_neighbours(1)
            if q in (1, 2):
                from_sibling(0, q, cp).wait_recv()
            elif q == 3:
                for sub in range(2):
                    relayed_from_sibling(0, sub, cp).wait_recv()

        rows = pl.ds(pl.multiple_of(i * tm, tm), tm)
        slot = i % 2
        for k, (q, cp) in enumerate(phases):
            @pl.when(p == k)
            def _(k=k, q=q, cp=cp):
                @pl.when(i == 0)
                def _():
                    if k == 0:
                        arrivals(0)
                        shard_load(0).start()
                    else:
                        drain(k - 1)
                    shard_load(k).wait()

                if k + 1 < len(phases):
                    @pl.when(i == max(nrow - 2, 0))
                    def _():
                        arrivals(k + 1)
                        shard_load(k + 1).start()

                if k == 0:
                    wx = D // X_STREAMS
                    ssq = sum(jnp.sum(xr[...] * xr[...], axis=-1, keepdims=True) for xr in x_refs)
                    r = lax.rsqrt(ssq * (1.0 / D) + EPS)
                    for n, xr in enumerate(x_refs):
                        cols = slice(n * wx, (n + 1) * wx)
                        hv = ((xr[...] * r * g_ref[:, cols]) * (1.0 + sc_ref[:, cols]) + sh_ref[:, cols]).astype(BF16)
                        h_ref[:, cols] = hv
                        h_all[rows, cols] = hv

                @pl.when(i >= 2)
                def _():
                    out_copy(k, slot, 0).wait()

                w = cols_of(q, cp)[1]
                obuf[slot, :, 0:w] = jnp.dot(h_all[rows, :], wbuf[k % 2, :, 0:w], preferred_element_type=F32)
                out_copy(k, slot, pl.multiple_of(i * tm, tm)).start()

        @pl.when((p == len(phases) - 1) & (i == nrow - 1))
        def _():
            drain(len(phases) - 1)
            pass_on_relayed(1)
            for q in (1, 2):
                from_sibling(1, q).wait_recv()
            for sub in range(2):
                relayed_from_sibling(1, sub).wait_recv()
            for which, cps in ((0, (0, 1)), (1, (None,))):
                for cp in cps:
                    for q in (1, 2):
                        to_neighbour(which, q, cp).wait_send()
                        relay(which, q, cp).wait_send()
                        to_sibling(which, q, cp).wait_send()
                        relayed_to_sibling(which, q - 1, cp).wait_send()

    vec = pl.BlockSpec((1, D), lambda p, i, pos: (0, 0))
    first_phase_rows = lambda p, i, pos: (jnp.where(p == 0, i, nrow - 1), 0)
    anyspec = pl.BlockSpec(memory_space=pl.ANY)
    x_spec = lambda n: pl.BlockSpec((tm, D // X_STREAMS), lambda p, i, pos: (jnp.where(p == 0, i, nrow - 1), n))
    return pl.pallas_call(
        body, name="proj_gather",
        grid_spec=pltpu.PrefetchScalarGridSpec(
            num_scalar_prefetch=1, grid=(len(phases), nrow),
            in_specs=[x_spec(n) for n in range(X_STREAMS)] + [vec, vec, vec, anyspec, anyspec],
            out_specs=[pl.BlockSpec((tm, D), first_phase_rows), anyspec, anyspec, anyspec],
            scratch_shapes=[pltpu.VMEM((s, D), BF16), pltpu.VMEM((2, D, W_IN_SHARD), BF16), pltpu.VMEM((2, tm, W_IN_SHARD), F32),
                            pltpu.SemaphoreType.DMA((24,)), pltpu.SemaphoreType.DMA((24,)), pltpu.SemaphoreType.DMA((2,)),
                            pltpu.SemaphoreType.DMA((2, OUT_STREAMS))]),
        out_shape=[jax.ShapeDtypeStruct((s, D), BF16), jax.ShapeDtypeStruct((s, D_IN), F32),
                   jax.ShapeDtypeStruct((D, D_IN), BF16), jax.ShapeDtypeStruct((D, D), BF16)],
        input_output_aliases={X_STREAMS + 4: 2, X_STREAMS + 5: 3},
        compiler_params=_params("arbitrary", "arbitrary"),
    )(pos, *([x] * X_STREAMS), shift, scale, norm_g, wi_full, wo_full)


def _proj_specs(rev_nb=None):
    if rev_nb is None:
        row = lambda i: i
    else:
        row = lambda i: rev_nb - 1 - i
    wide = lambda col: pl.BlockSpec((BLK, D_A), lambda i: (row(i), col))
    kv = lambda col: pl.BlockSpec((BLK, D_KV), lambda i: (row(i), col))
    half = lambda col: pl.BlockSpec((BLK, 512), lambda i: (row(i), col))
    return [wide(0), wide(1), wide(2), wide(3), kv(OFF_K // D_KV), kv(OFF_V // D_KV), half(OFF_ZB // 512), half(OFF_ZB // 512 + 1)]


def _mix_fwd_call(proj, cos, sin, ln_g, ln_b, w_sp, b_sp_t, sinks):
    s = proj.shape[0]
    nb = s // BLK

    def body(ua_ref, va_ref, za_ref, q_ref, k_ref, v_ref, zb0_ref, zb1_ref, cos_ref, sin_ref, lg_ref, lb_ref,
             w_ref, bt_ref, sinks_ref, y_ref, probs_ref, ost_ref, psink_ref, kdup_ref, vdup_ref, qm_ref, bias_ref):
        i = pl.program_id(0)
        first_half, lo = _lane_masks()
        cos_t = cos_ref[...]
        sin_t = sin_ref[...]

        _, _, vln = _layer_norm_fwd(va_ref[...], lg_ref[...], lb_ref[...])
        tril = _tril()
        for g in range(GROUPS):
            cols = slice(g * BLK, (g + 1) * BLK)
            wg = jnp.where(tril, w_ref[g], 0.0).astype(BF16)
            sg = jnp.dot(wg, vln[:, cols].astype(BF16), preferred_element_type=F32) + bt_ref[:, g:g + 1]
            gate, _ = _silu_parts(za_ref[:, cols])
            y_ref[:, cols] = (ua_ref[:, cols] * sg * gate).astype(BF16)

        @pl.when(i == 0)
        def _():
            kdup_ref[:, 0:BLK, :] = jnp.zeros((N_KV, BLK, LANE), BF16)
            vdup_ref[:, 0:BLK, :] = jnp.zeros((N_KV, BLK, LANE), BF16)
            _band_bias(bias_ref)

        @pl.when(i > 0)
        def _():
            kdup_ref[:, 0:BLK, :] = kdup_ref[:, BLK:2 * BLK, :]
            vdup_ref[:, 0:BLK, :] = vdup_ref[:, BLK:2 * BLK, :]

        for ks in range(2):
            cols = slice(ks * LANE, (ks + 1) * LANE)
            kr = _rope(k_ref[:, cols], cos_t, sin_t, first_half)
            for n, (kd, vd) in enumerate(zip(_dup_kv(kr, lo), _dup_kv(v_ref[:, cols], lo))):
                kdup_ref[2 * ks + n, BLK:2 * BLK, :] = kd
                vdup_ref[2 * ks + n, BLK:2 * BLK, :] = vd
        for sb in range(8):
            _stack_heads(qm_ref, sb, _rope(q_ref[:, sb * LANE:(sb + 1) * LANE], cos_t, sin_t, first_half) * SCALE, lo, BF16)

        block_kind = jnp.where(i > 0, 1, 0)

        psink_ref[...] = jnp.zeros((Q_PER_KV * BLK, LANE), F32)
        lane_q = lax.broadcasted_iota(jnp.int32, (Q_PER_KV * BLK, LANE), 1)

        def kv_head(kh, carry):
            probs, psink = _softmax_sink(qm_ref[kh], kdup_ref[kh], bias_ref[block_kind], _sink_column(sinks_ref, kh))
            probs_ref[kh] = probs
            psink_ref[...] = jnp.where(lane_q == kh, psink, psink_ref[...])
            ost_ref[kh] = jnp.dot(probs.astype(BF16), vdup_ref[kh], preferred_element_type=F32)
            return carry

        lax.fori_loop(0, N_KV, kv_head, 0, unroll=2)
        for sb in range(8):
            cols = slice(sb * LANE, (sb + 1) * LANE)
            zb = zb0_ref[:, cols] if sb < 4 else zb1_ref[:, (sb - 4) * LANE:(sb - 3) * LANE]
            gate, _ = _silu_parts(zb)
            y_ref[:, D_A + sb * LANE:D_A + (sb + 1) * LANE] = (_unstack_heads(ost_ref, sb, lo) * gate).astype(BF16)

    tab = pl.BlockSpec((BLK, LANE), lambda i: (i, 0))
    return pl.pallas_call(
        body, name="mix_fwd", grid=(nb,),
        in_specs=_proj_specs() + [
            tab, tab, pl.BlockSpec((1, D_A), lambda i: (0, 0)), pl.BlockSpec((1, D_A), lambda i: (0, 0)),
            pl.BlockSpec((GROUPS, BLK, BLK), lambda i: (0, 0, 0)), pl.BlockSpec((BLK, GROUPS), lambda i: (0, 0)),
            pl.BlockSpec(memory_space=pltpu.SMEM)],
        out_specs=[pl.BlockSpec((BLK, 2 * D_A), lambda i: (i, 0)),
                   pl.BlockSpec((None, N_KV, Q_PER_KV * BLK, 2 * BLK), lambda i: (i, 0, 0, 0)),
                   pl.BlockSpec((None, N_KV, Q_PER_KV * BLK, LANE), lambda i: (i, 0, 0, 0)),
                   pl.BlockSpec((None, Q_PER_KV * BLK, LANE), lambda i: (i, 0, 0))],
        out_shape=[jax.ShapeDtypeStruct((s, 2 * D_A), BF16), jax.ShapeDtypeStruct((nb, N_KV, Q_PER_KV * BLK, 2 * BLK), F32),
                   jax.ShapeDtypeStruct((nb, N_KV, Q_PER_KV * BLK, LANE), F32), jax.ShapeDtypeStruct((nb, Q_PER_KV * BLK, LANE), F32)],
        scratch_shapes=[pltpu.VMEM((N_KV, 2 * BLK, LANE), BF16), pltpu.VMEM((N_KV, 2 * BLK, LANE), BF16),
                        pltpu.VMEM((N_KV, Q_PER_KV * BLK, LANE), BF16), pltpu.VMEM((2, Q_PER_KV * BLK, 2 * BLK), F32)],
        compiler_params=_params("arbitrary"),
    )(proj, proj, proj, proj, proj, proj, proj, proj, cos, sin, ln_g, ln_b, w_sp, b_sp_t, sinks)


def _tail_call(y, w_out_bf, x, target, gate, shift_f, scale_f, gf):
    s = x.shape[0]
    tm = min(s, 256)
    nsteps = s // tm

    def body(y_ref, w_ref, x_ref, t_ref, gate_ref, shf_ref, scf_ref, gf_ref, dx2_ref, do_ref, dy_ref, st_ref):
        i = pl.program_id(0)

        @pl.when(i == 0)
        def _():
            st_ref[...] = jnp.zeros((8, D), F32)

        o = jnp.dot(y_ref[...], w_ref[...], preferred_element_type=F32)
        gate_v = gate_ref[...]
        x2 = x_ref[...] + gate_v * o
        r2 = lax.rsqrt(jnp.mean(x2 * x2, axis=-1, keepdims=True) + EPS)
        xn2 = x2 * r2
        hn2 = xn2 * gf_ref[...]
        one_sc = 1.0 + scf_ref[...]
        err = hn2 * one_sc + shf_ref[...] - t_ref[...]
        dout = err * (1.0 / D)
        dhn2 = dout * one_sc
        dxn2 = dhn2 * gf_ref[...]
        dx2 = r2 * (dxn2 - xn2 * jnp.mean(dxn2 * xn2, axis=-1, keepdims=True))
        dx2_ref[...] = dx2
        do = (dx2 * gate_v).astype(BF16)
        do_ref[...] = do
        dy_ref[...] = lax.dot_general(do, w_ref[...], NT, preferred_element_type=F32)
        st_ref[0:1, :] += jnp.sum(dout, axis=0, keepdims=True)
        st_ref[1:2, :] += jnp.sum(dout * hn2, axis=0, keepdims=True)
        st_ref[2:3, :] += jnp.sum(dhn2 * xn2, axis=0, keepdims=True)
        st_ref[3:4, :] += jnp.sum(dx2 * o, axis=0, keepdims=True)
        st_ref[4:5, :] += jnp.sum(err * err, axis=0, keepdims=True)

        @pl.when(i == nsteps - 1)
        def _():
            st_ref[5:6, :] = jnp.full((1, D), 0.5 / D, F32) * jnp.sum(st_ref[4:5, :])

    vec = pl.BlockSpec((1, D), lambda i: (0, 0))
    rows = lambda: pl.BlockSpec((tm, D), lambda i: (i, 0))
    return pl.pallas_call(
        body, name="tail", grid=(nsteps,),
        in_specs=[rows(), pl.BlockSpec((D, D), lambda i: (0, 0)), rows(), rows(), vec, vec, vec, vec],
        out_specs=[rows(), rows(), rows(), pl.BlockSpec((8, D), lambda i: (0, 0))],
        out_shape=[jax.ShapeDtypeStruct((s, D), F32), jax.ShapeDtypeStruct((s, D), BF16), jax.ShapeDtypeStruct((s, D), F32),
                   jax.ShapeDtypeStruct((8, D), F32)],
        compiler_params=_params("arbitrary"),
    )(y, w_out_bf, x, target, gate, shift_f, scale_f, gf)


def _tn_call(a, b, name):
    s, m = a.shape
    n = b.shape[1]
    tn = 1024
    ts = min(s, 1024)
    nk = s // ts

    def body(a_ref, b_ref, o_ref, acc_ref):
        k = pl.program_id(1)

        @pl.when(k == 0)
        def _():
            acc_ref[...] = jnp.zeros((m, tn), F32)

        acc_ref[...] += lax.dot_general(a_ref[...], b_ref[...], TN, preferred_element_type=F32)

        @pl.when(k == nk - 1)
        def _():
            o_ref[...] = acc_ref[...].astype(BF16)

    return pl.pallas_call(
        body, name=name, grid=(n // tn, nk),
        in_specs=[pl.BlockSpec((ts, m), lambda j, k: (k, 0)), pl.BlockSpec((ts, tn), lambda j, k: (k, j))],
        out_specs=pl.BlockSpec((m, tn), lambda j, k: (0, j)),
        out_shape=jax.ShapeDtypeStruct((m, n), BF16),
        scratch_shapes=[pltpu.VMEM((m, tn), F32)],
        compiler_params=_params("parallel", "arbitrary"),
    )(a, b)


def _tn_shards_call(pos, a, b, qs, name):
    s, m = a.shape
    ts = min(s, 1024)
    nk = s // ts

    def body(pos_ref, a_ref, b_ref, o_ref, acc_ref):
        k = pl.program_id(1)

        @pl.when(k == 0)
        def _():
            acc_ref[...] = jnp.zeros((m, W_IN_SHARD), F32)

        acc_ref[...] += lax.dot_general(a_ref[...], b_ref[...], TN, preferred_element_type=F32)

        @pl.when(k == nk - 1)
        def _():
            o_ref[...] = acc_ref[...].astype(BF16)

    def shard(j, pos):
        q = qs[0]
        for n in range(1, len(qs)):
            q = jnp.where(j == n, qs[n], q)
        return jnp.bitwise_xor(pos[0], q)

    return pl.pallas_call(
        body, name=name,
        grid_spec=pltpu.PrefetchScalarGridSpec(
            num_scalar_prefetch=1, grid=(len(qs), nk),
            in_specs=[pl.BlockSpec((ts, m), lambda j, k, pos: (k, 0)),
                      pl.BlockSpec((ts, W_IN_SHARD), lambda j, k, pos: (k, shard(j, pos)))],
            out_specs=pl.BlockSpec((m, W_IN_SHARD), lambda j, k, pos: (0, j)),
            scratch_shapes=[pltpu.VMEM((m, W_IN_SHARD), F32)]),
        out_shape=jax.ShapeDtypeStruct((m, len(qs) * W_IN_SHARD), BF16),
        compiler_params=_params("parallel", "arbitrary"),
    )(pos, a, b)


def _mix_bwd_call(proj, dy, probs, outs, psinks, tables, ln_g, ln_b, w_sp, w_sp_t, b_sp_t):
    s = proj.shape[0]
    nb = s // BLK
    rev = lambda i: nb - 1 - i
    prev = lambda i: jnp.maximum(nb - 2 - i, 0)

    def body(ua_ref, va_ref, za_ref, q_ref, k_ref, v_ref, zb0_ref, zb1_ref, kp_ref, vp_ref, dy_ref,
             probs_ref, ost_ref, psink_ref, cos_ref, sin_ref, cosp_ref, sinp_ref, lg_ref, lb_ref, w_ref, wt_ref, bt_ref,
             dp_ref, lnst_ref, dw_ref, dbt_ref, dsink_ref,
             kdup_ref, vdup_ref, dvln_ref, qm_ref, dom_ref, dqst_ref, dkdup_ref, dvdup_ref, kcar_ref, vcar_ref, sigb_ref):
        i = pl.program_id(0)
        first_half, lo = _lane_masks()
        lane8 = lax.broadcasted_iota(jnp.int32, (8, LANE), 1)
        cos_t = cos_ref[...]
        sin_t = sin_ref[...]

        @pl.when(i == 0)
        def _():
            lnst_ref[...] = jnp.zeros((8, D_A), F32)
            dw_ref[...] = jnp.zeros((GROUPS, BLK, BLK), F32)
            dbt_ref[...] = jnp.zeros((BLK, LANE), F32)
            dsink_ref[...] = jnp.zeros((8, LANE), F32)
            kcar_ref[...] = jnp.zeros((BLK, D_KV), F32)
            vcar_ref[...] = jnp.zeros((BLK, D_KV), F32)

        vhat, rstd, vln = _layer_norm_fwd(va_ref[...], lg_ref[...], lb_ref[...])
        tril = _tril()
        triu = jnp.logical_not(tril) | (lax.broadcasted_iota(jnp.int32, (BLK, BLK), 0) == lax.broadcasted_iota(jnp.int32, (BLK, BLK), 1))
        lane_b = lax.broadcasted_iota(jnp.int32, (BLK, LANE), 1)
        db_acc = jnp.zeros((BLK, LANE), F32)
        for g in range(GROUPS):
            cols = slice(g * BLK, (g + 1) * BLK)
            vln_g = vln[:, cols].astype(BF16)
            wg = jnp.where(tril, w_ref[g], 0.0).astype(BF16)
            sg = jnp.dot(wg, vln_g, preferred_element_type=F32) + bt_ref[:, g:g + 1]
            za = za_ref[:, cols]
            gate, sig = _silu_parts(za)
            ua = ua_ref[:, cols]
            dya_g = dy_ref[:, cols]
            dya = dya_g * gate
            dp_ref[:, cols] = (dya * sg).astype(BF16)
            dp_ref[:, 2 * D_A + g * BLK:2 * D_A + (g + 1) * BLK] = (
                dya_g * (ua * sg) * (sig * (1.0 + za * (1.0 - sig)))).astype(BF16)
            ds = dya * ua
            ds_b = ds.astype(BF16)
            wtg = jnp.where(triu, wt_ref[g], 0.0).astype(BF16)
            dvln_ref[:, cols] = jnp.dot(wtg, ds_b, preferred_element_type=F32)
            dw_ref[g] += jnp.where(tril, lax.dot_general(ds_b, vln_g, NT, preferred_element_type=F32), 0.0)
            db_acc = db_acc + jnp.where(lane_b == g, jnp.sum(ds, axis=-1, keepdims=True), 0.0)
        dbt_ref[...] += db_acc
        dvln = dvln_ref[...]
        lnst_ref[0:1, :] += jnp.sum(dvln * vhat, axis=0, keepdims=True)
        lnst_ref[1:2, :] += jnp.sum(dvln, axis=0, keepdims=True)
        dvhat = dvln * lg_ref[...]
        m1 = jnp.mean(dvhat, axis=-1, keepdims=True)
        m2 = jnp.mean(dvhat * vhat, axis=-1, keepdims=True)
        dp_ref[:, D_A:2 * D_A] = (rstd * (dvhat - m1 - vhat * m2)).astype(BF16)

        cosp = cosp_ref[...]
        sinp = sinp_ref[...]
        for ks in range(2):
            cols = slice(ks * LANE, (ks + 1) * LANE)
            kr = _rope(k_ref[:, cols], cos_t, sin_t, first_half)
            kpr = _rope(kp_ref[:, cols], cosp, sinp, first_half)
            for n, (kc, vc, kp, vp) in enumerate(zip(_dup_kv(kr, lo), _dup_kv(v_ref[:, cols], lo),
                                                     _dup_kv(kpr, lo), _dup_kv(vp_ref[:, cols], lo))):
                kdup_ref[2 * ks + n, BLK:2 * BLK, :] = kc
                vdup_ref[2 * ks + n, BLK:2 * BLK, :] = vc
                kdup_ref[2 * ks + n, 0:BLK, :] = kp
                vdup_ref[2 * ks + n, 0:BLK, :] = vp
        for sb in range(8):
            cols = slice(sb * LANE, (sb + 1) * LANE)
            _stack_heads(qm_ref, sb, _rope(q_ref[:, cols], cos_t, sin_t, first_half) * SCALE, lo, BF16)
            zb = zb0_ref[:, cols] if sb < 4 else zb1_ref[:, (sb - 4) * LANE:(sb - 3) * LANE]
            gate, sig = _silu_parts(zb)
            sigb_ref[:, cols] = sig
            _stack_heads(dom_ref, sb, dy_ref[:, D_A + sb * LANE:D_A + (sb + 1) * LANE] * gate, lo, F32)

        lane_q = lax.broadcasted_iota(jnp.int32, (Q_PER_KV * BLK, LANE), 1)

        def kv_head(kh, dsink_acc):
            qm = qm_ref[kh]
            kd = kdup_ref[kh]
            vd = vdup_ref[kh]
            probs = probs_ref[kh]
            psink = jnp.sum(jnp.where(lane_q == kh, psink_ref[...], 0.0), axis=-1, keepdims=True)
            probs_b = probs.astype(BF16)
            o = ost_ref[kh]
            dom = dom_ref[kh]
            dom_b = dom.astype(BF16)
            delta = jnp.sum(dom * o, axis=-1, keepdims=True)
            dpr = lax.dot_general(dom_b, vd, NT, preferred_element_type=F32)
            dss = (probs * (dpr - delta)).astype(BF16)
            sd = psink * delta
            for n in range(Q_PER_KV):
                dsink_acc = dsink_acc + jnp.where(lane8 == Q_PER_KV * kh + n, -jnp.sum(sd[n * BLK:(n + 1) * BLK]), 0.0)
            dqst_ref[kh] = jnp.dot(dss, kd, preferred_element_type=F32)
            dkdup_ref[kh] = lax.dot_general(dss, qm, TN, preferred_element_type=F32)
            dvdup_ref[kh] = lax.dot_general(probs_b, dom_b, TN, preferred_element_type=F32)
            return dsink_acc

        dsink_acc = lax.fori_loop(0, N_KV // 2, lambda j, acc: kv_head(2 * j + 1, kv_head(2 * j, acc)), jnp.zeros((8, LANE), F32))
        row0 = lax.broadcasted_iota(jnp.int32, (8, LANE), 0) == 0
        dsink_ref[...] += jnp.where(row0, dsink_acc, 0.0)

        for sb in range(8):
            cols = slice(sb * LANE, (sb + 1) * LANE)
            zb = zb0_ref[:, cols] if sb < 4 else zb1_ref[:, (sb - 4) * LANE:(sb - 3) * LANE]
            sig = sigb_ref[:, cols]
            dyb = dy_ref[:, D_A + sb * LANE:D_A + (sb + 1) * LANE]
            dp_ref[:, OFF_ZB + sb * LANE:OFF_ZB + (sb + 1) * LANE] = (
                dyb * _unstack_heads(ost_ref, sb, lo) * (sig * (1.0 + zb * (1.0 - sig)))).astype(BF16)
            dq_r = _unstack_heads(dqst_ref, sb, lo) * SCALE
            dp_ref[:, OFF_Q + sb * LANE:OFF_Q + (sb + 1) * LANE] = _unrope(dq_r, cos_t, sin_t, first_half).astype(BF16)

        lo2 = lax.broadcasted_iota(jnp.int32, (2 * BLK, LANE), 1) < HEAD
        for ks in range(2):
            cols = slice(ks * LANE, (ks + 1) * LANE)
            ka = dkdup_ref[2 * ks]
            kb = dkdup_ref[2 * ks + 1]
            dk_band = jnp.where(lo2, ka + pltpu.roll(ka, HEAD, 1), kb + pltpu.roll(kb, HEAD, 1))
            va_ = dvdup_ref[2 * ks]
            vb_ = dvdup_ref[2 * ks + 1]
            dv_band = jnp.where(lo2, va_ + pltpu.roll(va_, HEAD, 1), vb_ + pltpu.roll(vb_, HEAD, 1))
            dkr = dk_band[BLK:2 * BLK, :] + kcar_ref[:, cols]
            dp_ref[:, OFF_K + ks * LANE:OFF_K + (ks + 1) * LANE] = _unrope(dkr, cos_t, sin_t, first_half).astype(BF16)
            dp_ref[:, OFF_V + ks * LANE:OFF_V + (ks + 1) * LANE] = (
                dv_band[BLK:2 * BLK, :] + vcar_ref[:, cols]).astype(BF16)
            kcar_ref[:, cols] = dk_band[0:BLK, :]
            vcar_ref[:, cols] = dv_band[0:BLK, :]

    tab = pl.BlockSpec((BLK, LANE), lambda i: (rev(i), 0))
    kvp = lambda col: pl.BlockSpec((BLK, D_KV), lambda i: (prev(i), col))
    vec = pl.BlockSpec((1, D_A), lambda i: (0, 0))
    w3 = pl.BlockSpec((GROUPS, BLK, BLK), lambda i: (0, 0, 0))
    return pl.pallas_call(
        body, name="mix_bwd", grid=(nb,),
        in_specs=_proj_specs(nb) + [
            kvp(OFF_K // D_KV), kvp(OFF_V // D_KV), pl.BlockSpec((BLK, 2 * D_A), lambda i: (rev(i), 0)),
            pl.BlockSpec((None, N_KV, Q_PER_KV * BLK, 2 * BLK), lambda i: (rev(i), 0, 0, 0)),
            pl.BlockSpec((None, N_KV, Q_PER_KV * BLK, LANE), lambda i: (rev(i), 0, 0, 0)),
            pl.BlockSpec((None, Q_PER_KV * BLK, LANE), lambda i: (rev(i), 0, 0)),
            tab, tab, tab, tab, vec, vec, w3, w3, pl.BlockSpec((BLK, GROUPS), lambda i: (0, 0))],
        out_specs=[pl.BlockSpec((BLK, D_IN), lambda i: (rev(i), 0)), pl.BlockSpec((8, D_A), lambda i: (0, 0)), w3,
                   pl.BlockSpec((BLK, LANE), lambda i: (0, 0)), pl.BlockSpec((8, LANE), lambda i: (0, 0))],
        out_shape=[jax.ShapeDtypeStruct((s, D_IN), BF16), jax.ShapeDtypeStruct((8, D_A), F32),
                   jax.ShapeDtypeStruct((GROUPS, BLK, BLK), F32), jax.ShapeDtypeStruct((BLK, LANE), F32),
                   jax.ShapeDtypeStruct((8, LANE), F32)],
        scratch_shapes=[pltpu.VMEM((N_KV, 2 * BLK, LANE), BF16), pltpu.VMEM((N_KV, 2 * BLK, LANE), BF16),
                        pltpu.VMEM((BLK, D_A), F32), pltpu.VMEM((N_KV, Q_PER_KV * BLK, LANE), BF16),
                        pltpu.VMEM((N_KV, Q_PER_KV * BLK, LANE), F32), pltpu.VMEM((N_KV, Q_PER_KV * BLK, LANE), F32),
                        pltpu.VMEM((N_KV, 2 * BLK, LANE), F32), pltpu.VMEM((N_KV, 2 * BLK, LANE), F32),
                        pltpu.VMEM((BLK, D_KV), F32), pltpu.VMEM((BLK, D_KV), F32), pltpu.VMEM((BLK, D_B), F32)],
        compiler_params=_params("arbitrary"),
    )(proj, proj, proj, proj, proj, proj, proj, proj, proj, proj, dy, probs, outs, psinks, *tables, ln_g, ln_b,
      w_sp, w_sp_t, b_sp_t)


def _dh_call(dproj, w_bf, x, dx2, scale, norm_g):
    s = x.shape[0]
    tm = min(s, 512)
    tk = W_IN_SHARD
    nk = D_IN // tk

    def body(dp_ref, w_ref, x_ref, dx2_ref, sc_ref, g_ref, gx_ref, st_ref, acc_ref):
        i = pl.program_id(0)
        k = pl.program_id(1)

        @pl.when((i == 0) & (k == 0))
        def _():
            st_ref[...] = jnp.zeros((8, D), F32)

        @pl.when(k == 0)
        def _():
            acc_ref[...] = jnp.zeros((tm, D), F32)

        acc_ref[...] += lax.dot_general(dp_ref[...], w_ref[...], NT, preferred_element_type=F32)

        @pl.when(k == nk - 1)
        def _():
            g = g_ref[...]
            one_sc = 1.0 + sc_ref[...]

            def chunk(n, carry):
                rows = pl.ds(pl.multiple_of(n * BLK, BLK), BLK)
                dh = acc_ref[rows, :]
                xv = x_ref[rows, :]
                r = lax.rsqrt(jnp.mean(xv * xv, axis=-1, keepdims=True) + EPS)
                xn = xv * r
                dhn = dh * one_sc
                dxn = dhn * g
                gx_ref[rows, :] = dx2_ref[rows, :] + r * (dxn - xn * jnp.mean(dxn * xn, axis=-1, keepdims=True))
                st_ref[0:1, :] += jnp.sum(dh, axis=0, keepdims=True)
                st_ref[1:2, :] += jnp.sum(dh * (xn * g), axis=0, keepdims=True)
                st_ref[2:3, :] += jnp.sum(dhn * xn, axis=0, keepdims=True)
                return carry

            lax.fori_loop(0, tm // BLK, chunk, 0)

    vec = pl.BlockSpec((1, D), lambda i, k: (0, 0))
    rows = lambda: pl.BlockSpec((tm, D), lambda i, k: (i, 0))
    return pl.pallas_call(
        body, name="dh", grid=(s // tm, nk),
        in_specs=[pl.BlockSpec((tm, tk), lambda i, k: (i, k)), pl.BlockSpec((D, tk), lambda i, k: (0, k)), rows(), rows(), vec, vec],
        out_specs=[rows(), pl.BlockSpec((8, D), lambda i, k: (0, 0))],
        out_shape=[jax.ShapeDtypeStruct((s, D), F32), jax.ShapeDtypeStruct((8, D), F32)],
        scratch_shapes=[pltpu.VMEM((tm, D), F32)],
        compiler_params=_params("arbitrary", "arbitrary"),
    )(dproj, w_bf, x, dx2, scale, norm_g)


def _adam_math(w, g, m, v):
    m_new = ADAM_B1 * m + (1.0 - ADAM_B1) * g
    v_new = ADAM_B2 * v + (1.0 - ADAM_B2) * (g * g)
    m_hat = m_new / ADAM_C1
    v_hat = v_new / ADAM_C2
    delta = -ADAM_LR * (m_hat / (jnp.sqrt(v_hat) + ADAM_EPS) + ADAM_WD * w)
    return delta, m_new, v_new


def _adam_small_call(tensors):
    n = len(tensors)

    def body(*refs):
        ins, outs = refs[:4 * n], refs[4 * n:]
        for t in range(n):
            w_ref, g_ref, m_ref, v_ref = ins[4 * t:4 * t + 4]
            d, mo, vo = _adam_math(w_ref[...], g_ref[...], m_ref[...], v_ref[...])
            outs[3 * t][...], outs[3 * t + 1][...], outs[3 * t + 2][...] = d, mo, vo

    vm = pl.BlockSpec(memory_space=pltpu.VMEM)
    flat = [a for t in tensors for a in t]
    out = pl.pallas_call(
        body, name="adam_small", in_specs=[vm] * (4 * n), out_specs=[vm] * (3 * n),
        out_shape=[jax.ShapeDtypeStruct(t[0].shape, F32) for t in tensors for _ in range(3)],
        compiler_params=pltpu.CompilerParams(vmem_limit_bytes=VMEM_LIMIT),
    )(*flat)
    return [tuple(out[3 * t:3 * t + 3]) for t in range(n)]


def _adam_halves_call(pos, w, mine, theirs, m, v, name):
    r, n = w.shape
    half = r // 2
    tr = ADAM_ROWS
    nh = half // tr

    def body(pos_ref, w_ref, mine_ref, theirs_ref, m_ref, v_ref, g_ref, d_ref, mo_ref, vo_ref):
        is_mine = (pl.program_id(0) // nh) == pos_ref[1]
        g = jnp.where(is_mine, mine_ref[...], theirs_ref[...])
        g_ref[...] = g
        d_ref[...], mo_ref[...], vo_ref[...] = _adam_math(w_ref[...], g, m_ref[...], v_ref[...])

    spec = lambda: pl.BlockSpec((tr, n), lambda i, pos: (i, 0))

    def half_spec(core_of_half):
        def index(i, pos):
            first = core_of_half(pos) == 0
            active = (i // nh == 0) == first
            return jnp.where(active, i % nh, jnp.where(first, nh - 1, 0)), 0
        return pl.BlockSpec((tr, n), index)

    return pl.pallas_call(
        body, name=name,
        grid_spec=pltpu.PrefetchScalarGridSpec(
            num_scalar_prefetch=1, grid=(r // tr,),
            in_specs=[spec(), half_spec(lambda pos: pos[1]), half_spec(lambda pos: 1 - pos[1]), spec(), spec()],
            out_specs=[spec() for _ in range(4)]),
        out_shape=[jax.ShapeDtypeStruct((r, n), F32)] * 4, compiler_params=_params("arbitrary"),
    )(pos, w, mine, theirs, m, v)


def _adam_outer_call(w, ct, dm, m, v, name):
    r, n = w.shape
    tr = ADAM_ROWS

    def body(w_ref, ct_ref, dm_ref, m_ref, v_ref, g_ref, d_ref, mo_ref, vo_ref):
        g = ct_ref[:, 0:1] * dm_ref[0:1, :]
        for b in range(1, N_DEV):
            g = g + ct_ref[:, b:b + 1] * dm_ref[b:b + 1, :]
        g_ref[...] = g
        d_ref[...], mo_ref[...], vo_ref[...] = _adam_math(w_ref[...], g, m_ref[...], v_ref[...])

    spec = lambda: pl.BlockSpec((tr, n), lambda i: (i, 0))
    return pl.pallas_call(
        body, name=name, grid=(r // tr,),
        in_specs=[spec(), pl.BlockSpec((tr, N_DEV), lambda i: (i, 0)), pl.BlockSpec((N_DEV, n), lambda i: (0, 0)), spec(), spec()],
        out_specs=[spec() for _ in range(4)],
        out_shape=[jax.ShapeDtypeStruct((r, n), F32)] * 4, compiler_params=_params("parallel"),
    )(w, ct, dm, m, v)


def _sum_pieces_call(pos, part, part_block, recvs, name):
    r, n = recvs[0].shape[1:]
    tr = min(r, 256)
    nrb = r // tr

    def body(pos_ref, p_ref, *refs):
        acc = p_ref[...].astype(F32)
        for r_ref in refs[:-1]:
            for d in range(r_ref.shape[0]):
                acc = acc + r_ref[d].astype(F32)
        refs[-1][...] = acc

    return pl.pallas_call(
        body, name=name,
        grid_spec=pltpu.PrefetchScalarGridSpec(
            num_scalar_prefetch=1, grid=(nrb,),
            in_specs=[pl.BlockSpec((tr, n), lambda i, pos: part_block(i, pos, nrb))] + [
                pl.BlockSpec((rv.shape[0], tr, n), lambda i, pos: (0, i, 0)) for rv in recvs],
            out_specs=pl.BlockSpec((tr, n), lambda i, pos: (i, 0))),
        out_shape=jax.ShapeDtypeStruct((r, n), F32), compiler_params=_params("parallel"),
    )(pos, part, *recvs)


def _coords():
    return lax.axis_index("x"), lax.axis_index("y"), lax.axis_index("c")


def _allgather_sum_call(blk, name, with_sum):
    m_per, n = blk.shape

    def body(x_ref, out_ref, *rest):
        if with_sum:
            sum_ref, send_sems, recv_sems, local_sem = rest
        else:
            send_sems, recv_sems, local_sem = rest
        x, y, c = _coords()
        me, sibling = (x, y, c), (x, y, 1 - c)
        chips = [(1 - x, y), (x, 1 - y), (1 - x, 1 - y)]

        def rows(px, py, pc):
            return out_ref.at[pl.ds((4 * px + 2 * py + pc) * m_per, m_per), :]

        def copy(k, block, to, src=None):
            return pltpu.make_async_remote_copy(
                src_ref=rows(*block) if src is None else src, dst_ref=rows(*block),
                send_sem=send_sems.at[k], recv_sem=recv_sems.at[k], device_id=to, device_id_type=MESH)

        mine = pltpu.make_async_copy(x_ref, rows(*me), local_sem)
        mine.start()
        first = [copy(0, me, sibling, src=x_ref)]
        first += [copy(1 + j, me, (*chip, c), src=x_ref) for j, chip in enumerate(chips)]
        for cp in first:
            cp.start()
        passed = [copy(4 + j, (*chip, c), sibling) for j, chip in enumerate(chips)]
        for j, chip in enumerate(chips):
            copy(1 + j, (*chip, c), me).wait_recv()
            passed[j].start()
        copy(0, sibling, me).wait_recv()
        for j, chip in enumerate(chips):
            copy(4 + j, (*chip, 1 - c), me).wait_recv()
        for cp in first + passed:
            cp.wait_send()
        mine.wait()
        if with_sum:
            acc = out_ref[0:m_per, :]
            for d in range(1, N_DEV):
                acc = acc + out_ref[d * m_per:(d + 1) * m_per, :]
            sum_ref[...] = acc

    vm = pl.BlockSpec(memory_space=pltpu.VMEM)
    out_shape = [jax.ShapeDtypeStruct((N_DEV * m_per, n), F32)]
    if with_sum:
        out_shape.append(jax.ShapeDtypeStruct((m_per, n), F32))
    return pl.pallas_call(
        body, name=name, out_shape=out_shape, in_specs=[vm], out_specs=[vm] * len(out_shape),
        scratch_shapes=[pltpu.SemaphoreType.DMA((7,)), pltpu.SemaphoreType.DMA((7,)), pltpu.SemaphoreType.DMA],
        compiler_params=pltpu.CompilerParams(vmem_limit_bytes=VMEM_LIMIT),
    )(blk)


HBM_SPEC = pl.BlockSpec(memory_space=pltpu.HBM)
SEM_SPEC = pl.BlockSpec(memory_space=pltpu.SEMAPHORE)
SIDE_EFFECT = pltpu.SideEffectType.DATAFLOW_SIDE_EFFECTING


def _peer(x, y, c, q, cb):
    return (1 - x if q & 2 else x, 1 - y if q & 1 else y, 1 - c if cb else c)


def _w_in_piece(slots):
    def piece(part_ref, k, to):
        return part_ref.at[pl.ds(to[2] * (D // 2), D // 2), pl.ds(slots[k] * W_IN_SHARD, W_IN_SHARD)]
    return piece


def _w_out_piece(part_ref, k, to):
    ho = W_OUT_SHARD // 2
    return part_ref.at[pl.ds((2 * to[0] + to[1]) * W_OUT_SHARD + to[2] * ho, ho), :]


def _group_piece(part_ref, k, to):
    return part_ref.at[4 * to[0] + 2 * to[1] + to[2]]


def _whole_piece(part_ref, k, to):
    return part_ref


def _exchange_start_call(groups, name):
    ng = len(groups)
    lands = [lax.empty((len(rels),) + slot_shape, part.dtype) for part, rels, _, slot_shape in groups]

    def body(*refs):
        ins, outs = refs[:2 * ng], refs[2 * ng:]
        x, y, c = _coords()
        for g, (_, rels, piece, _) in enumerate(groups):
            part_ref, land_ref = ins[2 * g], ins[2 * g + 1]
            send_sems, recv_sems = outs[4 * g], outs[4 * g + 1]
            for k, (q, cb) in enumerate(rels):
                to = _peer(x, y, c, q, cb)
                pltpu.make_async_remote_copy(src_ref=piece(part_ref, k, to), dst_ref=land_ref.at[k], send_sem=send_sems.at[k],
                                             recv_sem=recv_sems.at[k], device_id=to, device_id_type=MESH).start()
        outs[-1][...] = jnp.zeros_like(outs[-1])

    out_shape, out_specs, operands = [], [], []
    for (part, rels, _, _), land in zip(groups, lands):
        n = len(rels)
        out_shape += [pltpu.SemaphoreType.DMA((n,)), pltpu.SemaphoreType.DMA((n,)), pltpu.HBM(part.shape, part.dtype),
                      pltpu.HBM(land.shape, land.dtype)]
        out_specs += [SEM_SPEC, SEM_SPEC, HBM_SPEC, HBM_SPEC]
        operands += [pltpu.with_memory_space_constraint(part, pltpu.HBM), pltpu.with_memory_space_constraint(land, pltpu.HBM)]
    out = pl.pallas_call(
        body, name=name,
        out_shape=tuple(out_shape) + (jax.ShapeDtypeStruct((1, 1), F32),),
        in_specs=(HBM_SPEC,) * (2 * ng), out_specs=tuple(out_specs) + (pl.BlockSpec(memory_space=pltpu.VMEM),),
        input_output_aliases={j: 4 * (j // 2) + 2 + j % 2 for j in range(2 * ng)},
        compiler_params=pltpu.CompilerParams(has_side_effects=SIDE_EFFECT),
    )(*operands)
    return [tuple(out[4 * g:4 * g + 4]) for g in range(ng)], out[-1]


def _exchange_wait_call(started, groups, after, name):
    ng = len(groups)

    def body(*refs):
        ins = refs[:4 * ng]
        x, y, c = _coords()
        for g, (_, rels, piece, _) in enumerate(groups):
            part_ref, land_ref, send_sems, recv_sems = ins[4 * g:4 * g + 4]
            for k, (q, cb) in enumerate(rels):
                to = _peer(x, y, c, q, cb)
                cp = pltpu.make_async_remote_copy(src_ref=piece(part_ref, k, to), dst_ref=land_ref.at[k], send_sem=send_sems.at[k],
                                                  recv_sem=recv_sems.at[k], device_id=to, device_id_type=MESH)
                cp.wait_send()
                cp.wait_recv()

    operands, in_specs, out_shape = [], [], []
    for send_sems, recv_sems, part_thru, land_thru in started:
        operands += [part_thru, land_thru, send_sems, recv_sems]
        in_specs += [HBM_SPEC, HBM_SPEC, SEM_SPEC, SEM_SPEC]
        out_shape += [pltpu.HBM(part_thru.shape, part_thru.dtype), pltpu.HBM(land_thru.shape, land_thru.dtype)]
    out = pl.pallas_call(
        body, name=name, out_shape=tuple(out_shape),
        in_specs=tuple(in_specs) + (pl.BlockSpec(memory_space=pl.ANY),), out_specs=(HBM_SPEC,) * (2 * ng),
        input_output_aliases={4 * g + j: 2 * g + j for g in range(ng) for j in range(2)},
        compiler_params=pltpu.CompilerParams(has_side_effects=SIDE_EFFECT),
    )(*operands, after)
    return [tuple(out[2 * g:2 * g + 2]) for g in range(ng)]


def _rope_tables(s):
    inv_freq = np.float32(10000.0) ** (-np.arange(0, HEAD, 2, dtype=np.float32) / np.float32(HEAD))
    ang = np.arange(s, dtype=np.float32)[:, None] * inv_freq[None, :]
    cos = np.tile(np.cos(ang), (1, LANE // (HEAD // 2))).astype(np.float32)
    sin = np.tile(np.sin(ang), (1, LANE // (HEAD // 2))).astype(np.float32)
    first_half = (np.arange(LANE) % HEAD) < (HEAD // 2)
    sin = np.where(first_half[None, :], -sin, sin)
    behind = lambda t: np.concatenate([t[:BLK], t[:-BLK]], axis=0)
    return tuple(jnp.asarray(t) for t in (cos, sin, behind(cos), behind(sin)))


def kernel(x, c, w_ada, b_ada, norm_g, w_in, ln_v_g, ln_v_b, w_spatial, b_spatial, sinks, w_out, w_ada_final, b_ada_final, final_norm_g, loss_target, m_w_ada, m_b_ada, m_norm_g, m_w_in, m_ln_v_g, m_ln_v_b, m_w_spatial, m_b_spatial, m_sinks, m_w_out, m_w_ada_final, m_b_ada_final, m_final_norm_g, v_w_ada, v_b_ada, v_norm_g, v_w_in, v_ln_v_g, v_ln_v_b, v_w_spatial, v_b_spatial, v_sinks, v_w_out, v_w_ada_final, v_b_ada_final, v_final_norm_g):
    s = x.shape[1]
    ax, ay, ac = _coords()
    chip = 2 * ax + ay
    me = 4 * ax + 2 * ay + ac
    n_ada = w_ada.shape[2]
    n_adaf = w_ada_final.shape[1]

    x2d = x.reshape(s, D)
    tgt = loss_target.reshape(s, D)
    w_ada2, w_in2, w_out2 = w_ada[0], w_in[0], w_out[0]
    b_ada_f2 = b_ada_final.reshape(1, 2 * D)
    gf = final_norm_g.reshape(1, D)

    c_all = _allgather_sum_call(jnp.pad(c, ((0, 7), (0, 0))), "gather_c", False)[0][::8]
    mod_p, c_act = _rowmat_call(c_all, w_ada2, lax.dynamic_slice(b_ada, (0, chip * n_ada), (1, n_ada)), "mod")
    modf_p, _ = _rowmat_call(c_all, w_ada_final, lax.dynamic_slice(b_ada_f2, (0, chip * n_adaf), (1, n_adaf)), "mod_final")
    mods = _allgather_sum_call(jnp.concatenate([mod_p, modf_p], axis=1), "gather_mod", False)[0]
    my_rows = [lax.dynamic_slice(mods, (16 * j + me, 0), (1, n_ada + n_adaf)) for j in range(N_CHIP)]
    mod = jnp.concatenate([r[:, :n_ada] for r in my_rows], axis=1)
    mod_f = jnp.concatenate([r[:, n_ada:] for r in my_rows], axis=1)
    shift, scale, gate = mod[:, :D], mod[:, D:2 * D], mod[:, 2 * D:]
    shift_f, scale_f = mod_f[:, :D], mod_f[:, D:]

    pos = jnp.stack([chip, ac]).astype(jnp.int32)
    w_in_own = _cast_into_call(pos, w_in2, (D, D_IN), "cast_w_in")
    w_out_own = _cast_into_call(pos, w_out2, (D, D), "cast_w_out")

    tables = _rope_tables(s)
    cos, sin = tables[:2]
    b_sp_t = b_spatial[0].T
    sinks1 = sinks.reshape(N_Q)
    h, proj, w_in_bf, w_out_bf = _proj_gather_call(pos, x2d, shift, scale, norm_g, w_in_own, w_out_own)
    y, probs, attn_out, psinks = _mix_fwd_call(proj, cos, sin, ln_v_g, ln_v_b, w_spatial[0], b_sp_t, sinks1)
    dx2, do, dy, st_tail = _tail_call(y, w_out_bf, x2d, tgt, gate, shift_f, scale_f, gf)

    rel_o = [(0, 1), (1, 0), (1, 1), (2, 0), (2, 1), (3, 0), (3, 1)]
    rel_a = [(1, 0), (1, 1), (2, 0), (2, 1)]
    rel_b = [(3, 0), (3, 1), (0, 1)]
    piece_a, piece_b = _w_in_piece([0, 0, 1, 1]), _w_in_piece([0, 0, 1])
    half_in, half_out = (D // 2, W_IN_SHARD), (W_OUT_SHARD // 2, D)

    g_w_out_p = _tn_call(y, do, "grad_w_out")
    grp_o = [(g_w_out_p, rel_o, _w_out_piece, half_out)]
    st_o, tok_o = _exchange_start_call(grp_o, "send_w_out")
    dproj, st_ln, d_wsp, d_bsp_t, d_sink = _mix_bwd_call(
        proj, dy, probs, attn_out, psinks, tables, ln_v_g + tok_o, ln_v_b, w_spatial[0], jnp.swapaxes(w_spatial[0], 1, 2),
        b_sp_t)
    g_w_in_a = _tn_shards_call(pos, h, dproj, (1, 2), "grad_w_in_a")
    grp_a = [(g_w_in_a, rel_a, piece_a, half_in), (d_wsp, rel_o, _group_piece, (BLK, BLK))]
    st_a, tok_a = _exchange_start_call(grp_a, "send_w_in_a")
    g_w_in_b = _tn_shards_call(pos, h, dproj, (3, 0), "grad_w_in_b")
    grp_b = [(g_w_in_b, rel_b, piece_b, half_in)]
    st_b, tok_b = _exchange_start_call(grp_b, "send_w_in_b")
    grad_x, st_dh = _dh_call(dproj, w_in_bf, x2d, dx2, scale + (tok_a + tok_b), norm_g)

    ((g_w_out_p, recv_o),) = _exchange_wait_call(st_o, grp_o, st_dh, "wait_w_out")
    (_, recv_a), (d_wsp, recv_s) = _exchange_wait_call(st_a, grp_a, st_dh, "wait_w_in_a")
    ((g_w_in_b, recv_b),) = _exchange_wait_call(st_b, grp_b, st_dh, "wait_w_in_b")
    mine_in = _sum_pieces_call(pos, g_w_in_b, lambda i, p, nrb: (p[1] * nrb + i, 1), [recv_a, recv_b], "sum_w_in")
    mine_out = _sum_pieces_call(pos, g_w_out_p, lambda i, p, nrb: ((2 * p[0] + p[1]) * nrb + i, 0), [recv_o], "sum_w_out")
    wsp_group = _sum_pieces_call(pos, d_wsp.reshape(GROUPS * BLK, BLK), lambda i, p, nrb: (2 * p[0] + p[1], 0), [recv_s],
                                 "sum_w_spatial")
    to_sibling = [(0, 1)]
    grp_p = [(mine_in, to_sibling, _whole_piece, half_in), (mine_out, to_sibling, _whole_piece, half_out)]
    st_p, tok_p = _exchange_start_call(grp_p, "swap_halves")

    misc = jnp.concatenate([st_ln, d_bsp_t[:, :GROUPS].T, d_sink, jnp.zeros((8, D - D_A - 2 * LANE), F32)], axis=1)
    pack = jnp.concatenate([wsp_group.reshape(8, D) + tok_p, st_tail, st_dh, misc], axis=0)
    rows = pack.shape[0]
    packs, tot = _allgather_sum_call(pack, "gather_small", True)
    packs = packs.reshape(N_DEV, rows, D)
    dmod_all = jnp.concatenate([packs[:, 16, :], packs[:, 17, :], packs[:, 11, :]], axis=1)
    dmodf_all = jnp.concatenate([packs[:, 8, :], packs[:, 9, :]], axis=1)
    loss = tot[13, 0]
    (mine_in, theirs_in), (mine_out, theirs_out) = _exchange_wait_call(st_p, grp_p, tot, "swapped_halves")
    small = {
        "b_ada": jnp.concatenate([tot[16:17], tot[17:18], tot[11:12]], axis=1),
        "norm_g": tot[18:19],
        "ln_v_g": tot[24:25, :D_A],
        "ln_v_b": tot[25:26, :D_A],
        "w_spatial": packs[:, 0:8, :].reshape(GROUPS * BLK, BLK),
        "b_spatial": tot[24:32, D_A:D_A + BLK],
        "sinks": tot[24:25, D_A + LANE:D_A + LANE + N_Q],
        "b_ada_final": jnp.concatenate([tot[8:9], tot[9:10]], axis=1),
        "final_norm_g": tot[10:11],
    }

    weights = dict(w_ada=w_ada, b_ada=b_ada, norm_g=norm_g, w_in=w_in, ln_v_g=ln_v_g, ln_v_b=ln_v_b, w_spatial=w_spatial,
                   b_spatial=b_spatial, sinks=sinks, w_out=w_out, w_ada_final=w_ada_final, b_ada_final=b_ada_final,
                   final_norm_g=final_norm_g)
    m_in = dict(w_ada=m_w_ada, b_ada=m_b_ada, norm_g=m_norm_g, w_in=m_w_in, ln_v_g=m_ln_v_g, ln_v_b=m_ln_v_b,
                w_spatial=m_w_spatial, b_spatial=m_b_spatial, sinks=m_sinks, w_out=m_w_out, w_ada_final=m_w_ada_final,
                b_ada_final=m_b_ada_final, final_norm_g=m_final_norm_g)
    v_in = dict(w_ada=v_w_ada, b_ada=v_b_ada, norm_g=v_norm_g, w_in=v_w_in, ln_v_g=v_ln_v_g, ln_v_b=v_ln_v_b,
                w_spatial=v_w_spatial, b_spatial=v_b_spatial, sinks=v_sinks, w_out=v_w_out, w_ada_final=v_w_ada_final,
                b_ada_final=v_b_ada_final, final_norm_g=v_final_norm_g)
    c_act_t = c_act.T
    outer = {"w_ada": lax.dynamic_slice(dmod_all, (0, chip * n_ada), (N_DEV, n_ada)),
             "w_ada_final": lax.dynamic_slice(dmodf_all, (0, chip * n_adaf), (N_DEV, n_adaf))}
    halves = {"w_in": (mine_in, theirs_in[0]), "w_out": (mine_out, theirs_out[0])}
    done = {}
    for name, (mine, theirs) in halves.items():
        shape2 = (2 * mine.shape[0], mine.shape[1])
        done[name] = _adam_halves_call(pos, weights[name].reshape(shape2), mine, theirs, m_in[name].reshape(shape2),
                                       v_in[name].reshape(shape2), "adam_" + name)
    for name, dm in outer.items():
        shape2 = (D, dm.shape[1])
        done[name] = _adam_outer_call(weights[name].reshape(shape2), c_act_t, dm, m_in[name].reshape(shape2),
                                      v_in[name].reshape(shape2), "adam_" + name)
    updates = _adam_small_call([(weights[name].reshape(g.shape), g, m_in[name].reshape(g.shape), v_in[name].reshape(g.shape))
                                for name, g in small.items()])
    for (name, g), upd in zip(small.items(), updates):
        done[name] = (g, *upd)
    outs = [[done[name][k].reshape(w.shape) for name, w in weights.items()] for k in range(4)]
    return (loss, grad_x.reshape(x.shape), *outs[0], *outs[1], *outs[2], *outs[3])
```

```python
import numpy as np
import jax
import jax.numpy as jnp
from jax import lax
from jax.experimental import pallas as pl
from jax.experimental.pallas import tpu as pltpu

F32 = jnp.float32
BF16 = jnp.bfloat16
MESH = pl.DeviceIdType.MESH

D = 2048
D_A = 1024
D_B = 1024
D_KV = 256
HEAD = 64
N_Q = 16
N_KV = 4
Q_PER_KV = N_Q // N_KV
BLK = 128
GROUPS = 8
D_IN = 5632
OFF_Q, OFF_K, OFF_V, OFF_ZB = 3072, 4096, 4352, 4608
N_CHIP = 4
N_DEV = 8
W_IN_SHARD = D_IN // N_CHIP
W_OUT_SHARD = D // N_CHIP
EPS = 1e-5
SCALE = HEAD ** -0.5
NEG = -1e30
LANE = 128
VMEM_LIMIT = 56 * 1024 * 1024

ADAM_LR, ADAM_B1, ADAM_B2, ADAM_EPS, ADAM_WD, ADAM_STEP = 0.001, 0.9, 0.999, 1e-08, 0.01, 10
ADAM_C1 = 1.0 - ADAM_B1 ** ADAM_STEP
ADAM_C2 = 1.0 - ADAM_B2 ** ADAM_STEP
ADAM_ROWS = 256

NT = (((1,), (1,)), ((), ()))
TN = (((0,), (0,)), ((), ()))


def _params(*sem):
    return pltpu.CompilerParams(dimension_semantics=sem, vmem_limit_bytes=VMEM_LIMIT)


def _silu_parts(z):
    sig = 1.0 / (1.0 + jnp.exp(-z))
    return z * sig, sig


def _swap_halves(v, first_half):
    return jnp.where(first_half, pltpu.roll(v, 96, 1), pltpu.roll(v, 32, 1))


def _rope(v, cos_t, sin_s, first_half):
    return v * cos_t + _swap_halves(v, first_half) * sin_s


def _unrope(dv, cos_t, sin_s, first_half):
    return dv * cos_t - _swap_halves(dv, first_half) * sin_s


def _lane_masks():
    lane = lax.broadcasted_iota(jnp.int32, (BLK, LANE), 1)
    return (lane % HEAD) < (HEAD // 2), lane < HEAD


def _band_valid(first_block_bound, rows=BLK):
    rr = lax.broadcasted_iota(jnp.int32, (rows, 2 * BLK), 0) & (BLK - 1)
    jj = lax.broadcasted_iota(jnp.int32, (rows, 2 * BLK), 1)
    return (jj > rr) & (jj <= rr + BLK) & (jj >= first_block_bound)


def _dup_kv(slab, lo):
    rolled = pltpu.roll(slab, HEAD, 1)
    return jnp.where(lo, slab, rolled).astype(BF16), jnp.where(lo, rolled, slab).astype(BF16)


def _stack_heads(ref, sb, slab, lo, dtype):
    kh, base = sb // 2, 2 * (sb % 2) * BLK
    zero = jnp.zeros_like(slab)
    ref[kh, base:base + BLK, :] = jnp.where(lo, slab, zero).astype(dtype)
    ref[kh, base + BLK:base + 2 * BLK, :] = jnp.where(lo, zero, slab).astype(dtype)


def _unstack_heads(ref, sb, lo):
    kh, base = sb // 2, 2 * (sb % 2) * BLK
    return jnp.where(lo, ref[kh, base:base + BLK, :], ref[kh, base + BLK:base + 2 * BLK, :])


def _sink_column(sinks_ref, kh):
    row = lax.broadcasted_iota(jnp.int32, (Q_PER_KV * BLK, 1), 0)
    col = jnp.full(row.shape, sinks_ref[Q_PER_KV * kh + Q_PER_KV - 1], F32)
    for n in range(Q_PER_KV - 2, -1, -1):
        col = jnp.where(row < (n + 1) * BLK, sinks_ref[Q_PER_KV * kh + n], col)
    return col


def _tril():
    t = lax.broadcasted_iota(jnp.int32, (BLK, BLK), 0)
    s = lax.broadcasted_iota(jnp.int32, (BLK, BLK), 1)
    return s <= t


def _layer_norm_fwd(va, lg, lb):
    mu = jnp.mean(va, axis=-1, keepdims=True)
    xc = va - mu
    rstd = lax.rsqrt(jnp.mean(xc * xc, axis=-1, keepdims=True) + EPS)
    vhat = xc * rstd
    return vhat, rstd, vhat * lg + lb


def _softmax_sink(qm, kdup, bias, sink):
    s = lax.dot_general(qm, kdup, NT, preferred_element_type=F32) + bias
    m = jnp.maximum(jnp.max(s, axis=-1, keepdims=True), sink)
    p = jnp.exp(s - m)
    esink = jnp.exp(sink - m)
    inv = 1.0 / (jnp.sum(p, axis=-1, keepdims=True) + esink)
    return p * inv, esink * inv


def _band_bias(bias_ref):
    rows = bias_ref.shape[1]
    bias_ref[0] = jnp.where(_band_valid(BLK, rows), 0.0, NEG)
    bias_ref[1] = jnp.where(_band_valid(0, rows), 0.0, NEG)


def _rowmat_call(c_all, w, b, name):
    n = w.shape[1]
    tn = 512

    def body(c_ref, w_ref, b_ref, o_ref, ca_ref):
        ca, _ = _silu_parts(c_ref[...])
        ca_ref[...] = ca
        o_ref[...] = jnp.dot(ca.astype(BF16), w_ref[...].astype(BF16), preferred_element_type=F32) + b_ref[...]

    return pl.pallas_call(
        body, name=name, grid=(n // tn,),
        in_specs=[pl.BlockSpec((N_DEV, D), lambda j: (0, 0)), pl.BlockSpec((D, tn), lambda j: (0, j)),
                  pl.BlockSpec((1, tn), lambda j: (0, j))],
        out_specs=[pl.BlockSpec((N_DEV, tn), lambda j: (0, j)), pl.BlockSpec((N_DEV, D), lambda j: (0, 0))],
        out_shape=[jax.ShapeDtypeStruct((N_DEV, n), F32), jax.ShapeDtypeStruct((N_DEV, D), F32)],
        compiler_params=_params("arbitrary"),
    )(c_all, w, b)


def _cast_into_call(pos, w, full_shape, name):
    r, n = w.shape
    tr = min(r, 512)
    by_cols = full_shape[0] == r
    nrb = r // tr

    def body(pos_ref, w_ref, o_ref):
        o_ref[...] = w_ref[...].astype(BF16)

    out_map = (lambda i, pos: (i, pos[0])) if by_cols else (lambda i, pos: (pos[0] * nrb + i, 0))
    return pl.pallas_call(
        body, name=name,
        grid_spec=pltpu.PrefetchScalarGridSpec(
            num_scalar_prefetch=1, grid=(nrb,),
            in_specs=[pl.BlockSpec((tr, n), lambda i, pos: (i, 0))], out_specs=pl.BlockSpec((tr, n), out_map)),
        out_shape=jax.ShapeDtypeStruct(full_shape, BF16), compiler_params=_params("parallel"),
    )(pos, w)


W_IN_PARTS = ((0, 768), (768, 640))
OUT_STREAMS = 4
X_STREAMS = 4


def _proj_gather_call(pos, x, shift, scale, norm_g, wi_full, wo_full):
    s = x.shape[0]
    tm = min(s, 512)
    nrow = s // tm
    hi = D // 2
    ho = W_OUT_SHARD // 2
    phases = [(0, None), (1, 0), (2, 0), (1, 1), (2, 1), (3, 0), (3, 1)]

    def body(pos_ref, *refs):
        x_refs = refs[:X_STREAMS]
        (sh_ref, sc_ref, g_ref, _, _, h_ref, proj_ref, fi_ref, fo_ref,
         h_all, wbuf, obuf, send_sems, recv_sems, load_sems, out_sems) = refs[X_STREAMS:]
        p = pl.program_id(0)
        i = pl.program_id(1)
        x_, y_, c_ = _coords()
        me, sibling = (x_, y_, c_), (x_, y_, 1 - c_)

        def shard_of(q):
            px, py, _ = _peer(x_, y_, c_, q, 0)
            return 2 * px + py

        def cols_of(q, cp):
            off, w = (0, W_IN_SHARD) if cp is None else W_IN_PARTS[cp]
            return shard_of(q) * W_IN_SHARD + off, w

        def part(which, q, pc, sub, cp):
            return _weight_part(fi_ref, fo_ref, (x_, y_, c_), which, q, pc, sub, cp)

        def copy(k, ref, to):
            return pltpu.make_async_remote_copy(src_ref=ref, dst_ref=ref, send_sem=send_sems.at[k], recv_sem=recv_sems.at[k],
                                                device_id=to, device_id_type=MESH)

        def sem(which, kind, j, cp):
            return 4 * kind + 2 * cp + j if which == 0 else 16 + 2 * kind + j

        def relay(which, q, cp=None):
            return copy(sem(which, 1, q - 1, cp), part(which, q, c_, q - 1, cp), _peer(x_, y_, c_, 3 - q, 0))

        def relayed(which, sub, cp=None):
            return copy(sem(which, 1, sub, cp), part(which, 3, c_, sub, cp), me)

        def to_sibling(which, q, cp=None):
            return copy(sem(which, 2, q - 1, cp), part(which, q, c_, None, cp), sibling)

        def from_sibling(which, q, cp=None):
            return copy(sem(which, 2, q - 1, cp), part(which, q, 1 - c_, None, cp), me)

        def relayed_to_sibling(which, sub, cp=None):
            return copy(sem(which, 3, sub, cp), part(which, 3, c_, sub, cp), sibling)

        def relayed_from_sibling(which, sub, cp=None):
            return copy(sem(which, 3, sub, cp), part(which, 3, 1 - c_, sub, cp), me)

        def pass_on_neighbours(which, cp=None):
            for q in (1, 2):
                to_sibling(which, q, cp).start()
                relay(which, q, cp).start()

        def pass_on_relayed(which, cp=None):
            for sub in range(2):
                relayed(which, sub, cp).wait_recv()
                relayed_to_sibling(which, sub, cp).start()

        def shard_load(k):
            c0, w = cols_of(*phases[k])
            return pltpu.make_async_copy(fi_ref.at[:, pl.ds(c0, w)], wbuf.at[k % 2, :, 0:w], load_sems.at[k % 2])

        class OutCopies:
            def __init__(self, k, slot, row0):
                c0, w = cols_of(*phases[k])
                strip = tm // OUT_STREAMS
                self.copies = [pltpu.make_async_copy(obuf.at[slot, n * strip:(n + 1) * strip, 0:w],
                                                     proj_ref.at[pl.ds(row0 + n * strip, strip), pl.ds(c0, w)],
                                                     out_sems.at[slot, n]) for n in range(OUT_STREAMS)]

            def start(self):
                for cp in self.copies:
                    cp.start()

            def wait(self):
                for cp in self.copies:
                    cp.wait()

        out_copy = OutCopies

        def drain(k):
            for j in range(min(2, nrow)):
                out_copy(k, (nrow - 1 - j) % 2, 0).wait()

        def arrivals(k):
            q, cp = phases[k]
            if k in (1, 3):
                pass_on_neighbours(0, cp)
            elif k == 5:
                for cp_ in range(2):
                    pass_on_relayed(0, cp_)
                pass_on_neighbours(1)
            if q in (1, 2):
                from_sibling(0, q, cp).wait_recv()
            elif q == 3:
                for sub in range(2):
                    relayed_from_sibling(0, sub, cp).wait_recv()

        rows = pl.ds(pl.multiple_of(i * tm, tm), tm)
        slot = i % 2
        for k, (q, cp) in enumerate(phases):
            @pl.when(p == k)
            def _(k=k, q=q, cp=cp):
                @pl.when(i == 0)
                def _():
                    if k == 0:
                        arrivals(0)
                        shard_load(0).start()
                    else:
                        drain(k - 1)
                    shard_load(k).wait()

                if k + 1 < len(phases):
                    @pl.when(i == max(nrow - 2, 0))
                    def _():
                        arrivals(k + 1)
                        shard_load(k + 1).start()

                if k == 0:
                    wx = D // X_STREAMS
                    ssq = sum(jnp.sum(xr[...] * xr[...], axis=-1, keepdims=True) for xr in x_refs)
                    r = lax.rsqrt(ssq * (1.0 / D) + EPS)
                    for n, xr in enumerate(x_refs):
                        cols = slice(n * wx, (n + 1) * wx)
                        hv = ((xr[...] * r * g_ref[:, cols]) * (1.0 + sc_ref[:, cols]) + sh_ref[:, cols]).astype(BF16)
                        h_ref[:, cols] = hv
                        h_all[rows, cols] = hv

                @pl.when(i >= 2)
                def _():
                    out_copy(k, slot, 0).wait()

                w = cols_of(q, cp)[1]
                obuf[slot, :, 0:w] = jnp.dot(h_all[rows, :], wbuf[k % 2, :, 0:w], preferred_element_type=F32)
                out_copy(k, slot, pl.multiple_of(i * tm, tm)).start()

        @pl.when((p == len(phases) - 1) & (i == nrow - 1))
        def _():
            drain(len(phases) - 1)
            pass_on_relayed(1)
            for q in (1, 2):
                from_sibling(1, q).wait_recv()
            for sub in range(2):
                relayed_from_sibling(1, sub).wait_recv()
            for which, cps in ((0, (0, 1)), (1, (None,))):
                for cp in cps:
                    for q in (1, 2):
                        relay(which, q, cp).wait_send()
                        to_sibling(which, q, cp).wait_send()
                        relayed_to_sibling(which, q - 1, cp).wait_send()

    vec = pl.BlockSpec((1, D), lambda p, i, pos: (0, 0))
    first_phase_rows = lambda p, i, pos: (jnp.where(p == 0, i, nrow - 1), 0)
    anyspec = pl.BlockSpec(memory_space=pl.ANY)
    x_spec = lambda n: pl.BlockSpec((tm, D // X_STREAMS), lambda p, i, pos: (jnp.where(p == 0, i, nrow - 1), n))
    return pl.pallas_call(
        body, name="proj_gather",
        grid_spec=pltpu.PrefetchScalarGridSpec(
            num_scalar_prefetch=1, grid=(len(phases), nrow),
            in_specs=[x_spec(n) for n in range(X_STREAMS)] + [vec, vec, vec, anyspec, anyspec],
            out_specs=[pl.BlockSpec((tm, D), first_phase_rows), anyspec, anyspec, anyspec],
            scratch_shapes=[pltpu.VMEM((s, D), BF16), pltpu.VMEM((2, D, W_IN_SHARD), BF16), pltpu.VMEM((2, tm, W_IN_SHARD), F32),
                            pltpu.SemaphoreType.DMA((24,)), pltpu.SemaphoreType.DMA((24,)), pltpu.SemaphoreType.DMA((2,)),
                            pltpu.SemaphoreType.DMA((2, OUT_STREAMS))]),
        out_shape=[jax.ShapeDtypeStruct((s, D), BF16), jax.ShapeDtypeStruct((s, D_IN), F32),
                   jax.ShapeDtypeStruct((D, D_IN), BF16), jax.ShapeDtypeStruct((D, D), BF16)],
        input_output_aliases={X_STREAMS + 4: 2, X_STREAMS + 5: 3},
        compiler_params=_params("arbitrary", "arbitrary"),
    )(pos, *([x] * X_STREAMS), shift, scale, norm_g, wi_full, wo_full)


def _proj_specs(rev_nb=None):
    if rev_nb is None:
        row = lambda i: i
    else:
        row = lambda i: rev_nb - 1 - i
    wide = lambda col: pl.BlockSpec((BLK, D_A), lambda i: (row(i), col))
    kv = lambda col: pl.BlockSpec((BLK, D_KV), lambda i: (row(i), col))
    half = lambda col: pl.BlockSpec((BLK, 512), lambda i: (row(i), col))
    return [wide(0), wide(1), wide(2), wide(3), kv(OFF_K // D_KV), kv(OFF_V // D_KV), half(OFF_ZB // 512), half(OFF_ZB // 512 + 1)]


def _mix_fwd_call(proj, cos, sin, ln_g, ln_b, w_sp, b_sp_t, sinks):
    s = proj.shape[0]
    nb = s // BLK

    def body(ua_ref, va_ref, za_ref, q_ref, k_ref, v_ref, zb0_ref, zb1_ref, cos_ref, sin_ref, lg_ref, lb_ref,
             w_ref, bt_ref, sinks_ref, y_ref, probs_ref, ost_ref, psink_ref, kdup_ref, vdup_ref, qm_ref, bias_ref):
        i = pl.program_id(0)
        first_half, lo = _lane_masks()
        cos_t = cos_ref[...]
        sin_t = sin_ref[...]

        _, _, vln = _layer_norm_fwd(va_ref[...], lg_ref[...], lb_ref[...])
        tril = _tril()
        for g in range(GROUPS):
            cols = slice(g * BLK, (g + 1) * BLK)
            wg = jnp.where(tril, w_ref[g], 0.0).astype(BF16)
            sg = jnp.dot(wg, vln[:, cols].astype(BF16), preferred_element_type=F32) + bt_ref[:, g:g + 1]
            gate, _ = _silu_parts(za_ref[:, cols])
            y_ref[:, cols] = (ua_ref[:, cols] * sg * gate).astype(BF16)

        @pl.when(i == 0)
        def _():
            kdup_ref[:, 0:BLK, :] = jnp.zeros((N_KV, BLK, LANE), BF16)
            vdup_ref[:, 0:BLK, :] = jnp.zeros((N_KV, BLK, LANE), BF16)
            _band_bias(bias_ref)

        @pl.when(i > 0)
        def _():
            kdup_ref[:, 0:BLK, :] = kdup_ref[:, BLK:2 * BLK, :]
            vdup_ref[:, 0:BLK, :] = vdup_ref[:, BLK:2 * BLK, :]

        for ks in range(2):
            cols = slice(ks * LANE, (ks + 1) * LANE)
            kr = _rope(k_ref[:, cols], cos_t, sin_t, first_half)
            for n, (kd, vd) in enumerate(zip(_dup_kv(kr, lo), _dup_kv(v_ref[:, cols], lo))):
                kdup_ref[2 * ks + n, BLK:2 * BLK, :] = kd
                vdup_ref[2 * ks + n, BLK:2 * BLK, :] = vd
        for sb in range(8):
            _stack_heads(qm_ref, sb, _rope(q_ref[:, sb * LANE:(sb + 1) * LANE], cos_t, sin_t, first_half) * SCALE, lo, BF16)

        block_kind = jnp.where(i > 0, 1, 0)

        psink_ref[...] = jnp.zeros((Q_PER_KV * BLK, LANE), F32)
        lane_q = lax.broadcasted_iota(jnp.int32, (Q_PER_KV * BLK, LANE), 1)

        def kv_head(kh, carry):
            probs, psink = _softmax_sink(qm_ref[kh], kdup_ref[kh], bias_ref[block_kind], _sink_column(sinks_ref, kh))
            probs_ref[kh] = probs
            psink_ref[...] = jnp.where(lane_q == kh, psink, psink_ref[...])
            ost_ref[kh] = jnp.dot(probs.astype(BF16), vdup_ref[kh], preferred_element_type=F32)
            return carry

        lax.fori_loop(0, N_KV, kv_head, 0, unroll=2)
        for sb in range(8):
            cols = slice(sb * LANE, (sb + 1) * LANE)
            zb = zb0_ref[:, cols] if sb < 4 else zb1_ref[:, (sb - 4) * LANE:(sb - 3) * LANE]
            gate, _ = _silu_parts(zb)
            y_ref[:, D_A + sb * LANE:D_A + (sb + 1) * LANE] = (_unstack_heads(ost_ref, sb, lo) * gate).astype(BF16)

    tab = pl.BlockSpec((BLK, LANE), lambda i: (i, 0))
    return pl.pallas_call(
        body, name="mix_fwd", grid=(nb,),
        in_specs=_proj_specs() + [
            tab, tab, pl.BlockSpec((1, D_A), lambda i: (0, 0)), pl.BlockSpec((1, D_A), lambda i: (0, 0)),
            pl.BlockSpec((GROUPS, BLK, BLK), lambda i: (0, 0, 0)), pl.BlockSpec((BLK, GROUPS), lambda i: (0, 0)),
            pl.BlockSpec(memory_space=pltpu.SMEM)],
        out_specs=[pl.BlockSpec((BLK, 2 * D_A), lambda i: (i, 0)),
                   pl.BlockSpec((None, N_KV, Q_PER_KV * BLK, 2 * BLK), lambda i: (i, 0, 0, 0)),
                   pl.BlockSpec((None, N_KV, Q_PER_KV * BLK, LANE), lambda i: (i, 0, 0, 0)),
                   pl.BlockSpec((None, Q_PER_KV * BLK, LANE), lambda i: (i, 0, 0))],
        out_shape=[jax.ShapeDtypeStruct((s, 2 * D_A), BF16), jax.ShapeDtypeStruct((nb, N_KV, Q_PER_KV * BLK, 2 * BLK), F32),
                   jax.ShapeDtypeStruct((nb, N_KV, Q_PER_KV * BLK, LANE), F32), jax.ShapeDtypeStruct((nb, Q_PER_KV * BLK, LANE), F32)],
        scratch_shapes=[pltpu.VMEM((N_KV, 2 * BLK, LANE), BF16), pltpu.VMEM((N_KV, 2 * BLK, LANE), BF16),
                        pltpu.VMEM((N_KV, Q_PER_KV * BLK, LANE), BF16), pltpu.VMEM((2, Q_PER_KV * BLK, 2 * BLK), F32)],
        compiler_params=_params("arbitrary"),
    )(proj, proj, proj, proj, proj, proj, proj, proj, cos, sin, ln_g, ln_b, w_sp, b_sp_t, sinks)


def _tail_call(y, w_out_bf, x, target, gate, shift_f, scale_f, gf):
    s = x.shape[0]
    tm = min(s, 256)
    nsteps = s // tm

    def body(y_ref, w_ref, x_ref, t_ref, gate_ref, shf_ref, scf_ref, gf_ref, dx2_ref, do_ref, dy_ref, st_ref):
        i = pl.program_id(0)

        @pl.when(i == 0)
        def _():
            st_ref[...] = jnp.zeros((8, D), F32)

        o = jnp.dot(y_ref[...], w_ref[...], preferred_element_type=F32)
        gate_v = gate_ref[...]
        x2 = x_ref[...] + gate_v * o
        r2 = lax.rsqrt(jnp.mean(x2 * x2, axis=-1, keepdims=True) + EPS)
        xn2 = x2 * r2
        hn2 = xn2 * gf_ref[...]
        one_sc = 1.0 + scf_ref[...]
        err = hn2 * one_sc + shf_ref[...] - t_ref[...]
        dout = err * (1.0 / D)
        dhn2 = dout * one_sc
        dxn2 = dhn2 * gf_ref[...]
        dx2 = r2 * (dxn2 - xn2 * jnp.mean(dxn2 * xn2, axis=-1, keepdims=True))
        dx2_ref[...] = dx2
        do = (dx2 * gate_v).astype(BF16)
        do_ref[...] = do
        dy_ref[...] = lax.dot_general(do, w_ref[...], NT, preferred_element_type=F32)
        st_ref[0:1, :] += jnp.sum(dout, axis=0, keepdims=True)
        st_ref[1:2, :] += jnp.sum(dout * hn2, axis=0, keepdims=True)
        st_ref[2:3, :] += jnp.sum(dhn2 * xn2, axis=0, keepdims=True)
        st_ref[3:4, :] += jnp.sum(dx2 * o, axis=0, keepdims=True)
        st_ref[4:5, :] += jnp.sum(err * err, axis=0, keepdims=True)

        @pl.when(i == nsteps - 1)
        def _():
            st_ref[5:6, :] = jnp.full((1, D), 0.5 / D, F32) * jnp.sum(st_ref[4:5, :])

    vec = pl.BlockSpec((1, D), lambda i: (0, 0))
    rows = lambda: pl.BlockSpec((tm, D), lambda i: (i, 0))
    return pl.pallas_call(
        body, name="tail", grid=(nsteps,),
        in_specs=[rows(), pl.BlockSpec((D, D), lambda i: (0, 0)), rows(), rows(), vec, vec, vec, vec],
        out_specs=[rows(), rows(), rows(), pl.BlockSpec((8, D), lambda i: (0, 0))],
        out_shape=[jax.ShapeDtypeStruct((s, D), F32), jax.ShapeDtypeStruct((s, D), BF16), jax.ShapeDtypeStruct((s, D), F32),
                   jax.ShapeDtypeStruct((8, D), F32)],
        compiler_params=_params("arbitrary"),
    )(y, w_out_bf, x, target, gate, shift_f, scale_f, gf)


def _tn_call(a, b, name):
    s, m = a.shape
    n = b.shape[1]
    tn = 1024
    ts = min(s, 1024)
    nk = s // ts

    def body(a_ref, b_ref, o_ref, acc_ref):
        k = pl.program_id(1)

        @pl.when(k == 0)
        def _():
            acc_ref[...] = jnp.zeros((m, tn), F32)

        acc_ref[...] += lax.dot_general(a_ref[...], b_ref[...], TN, preferred_element_type=F32)

        @pl.when(k == nk - 1)
        def _():
            o_ref[...] = acc_ref[...].astype(BF16)

    return pl.pallas_call(
        body, name=name, grid=(n // tn, nk),
        in_specs=[pl.BlockSpec((ts, m), lambda j, k: (k, 0)), pl.BlockSpec((ts, tn), lambda j, k: (k, j))],
        out_specs=pl.BlockSpec((m, tn), lambda j, k: (0, j)),
        out_shape=jax.ShapeDtypeStruct((m, n), BF16),
        scratch_shapes=[pltpu.VMEM((m, tn), F32)],
        compiler_params=_params("parallel", "arbitrary"),
    )(a, b)


def _tn_shards_call(pos, a, b, qs, name):
    s, m = a.shape
    ts = min(s, 1024)
    nk = s // ts

    def body(pos_ref, a_ref, b_ref, o_ref, acc_ref):
        k = pl.program_id(1)

        @pl.when(k == 0)
        def _():
            acc_ref[...] = jnp.zeros((m, W_IN_SHARD), F32)

        acc_ref[...] += lax.dot_general(a_ref[...], b_ref[...], TN, preferred_element_type=F32)

        @pl.when(k == nk - 1)
        def _():
            o_ref[...] = acc_ref[...].astype(BF16)

    def shard(j, pos):
        q = qs[0]
        for n in range(1, len(qs)):
            q = jnp.where(j == n, qs[n], q)
        return jnp.bitwise_xor(pos[0], q)

    return pl.pallas_call(
        body, name=name,
        grid_spec=pltpu.PrefetchScalarGridSpec(
            num_scalar_prefetch=1, grid=(len(qs), nk),
            in_specs=[pl.BlockSpec((ts, m), lambda j, k, pos: (k, 0)),
                      pl.BlockSpec((ts, W_IN_SHARD), lambda j, k, pos: (k, shard(j, pos)))],
            out_specs=pl.BlockSpec((m, W_IN_SHARD), lambda j, k, pos: (0, j)),
            scratch_shapes=[pltpu.VMEM((m, W_IN_SHARD), F32)]),
        out_shape=jax.ShapeDtypeStruct((m, len(qs) * W_IN_SHARD), BF16),
        compiler_params=_params("parallel", "arbitrary"),
    )(pos, a, b)


def _mix_bwd_call(proj, dy, probs, outs, psinks, tables, ln_g, ln_b, w_sp, w_sp_t, b_sp_t):
    s = proj.shape[0]
    nb = s // BLK
    rev = lambda i: nb - 1 - i
    prev = lambda i: jnp.maximum(nb - 2 - i, 0)

    def body(ua_ref, va_ref, za_ref, q_ref, k_ref, v_ref, zb0_ref, zb1_ref, kp_ref, vp_ref, dy_ref,
             probs_ref, ost_ref, psink_ref, cos_ref, sin_ref, cosp_ref, sinp_ref, lg_ref, lb_ref, w_ref, wt_ref, bt_ref,
             dp_ref, lnst_ref, dw_ref, dbt_ref, dsink_ref,
             kdup_ref, vdup_ref, dvln_ref, qm_ref, dom_ref, dqst_ref, dkdup_ref, dvdup_ref, kcar_ref, vcar_ref, sigb_ref):
        i = pl.program_id(0)
        first_half, lo = _lane_masks()
        lane8 = lax.broadcasted_iota(jnp.int32, (8, LANE), 1)
        cos_t = cos_ref[...]
        sin_t = sin_ref[...]

        @pl.when(i == 0)
        def _():
            lnst_ref[...] = jnp.zeros((8, D_A), F32)
            dw_ref[...] = jnp.zeros((GROUPS, BLK, BLK), F32)
            dbt_ref[...] = jnp.zeros((BLK, LANE), F32)
            dsink_ref[...] = jnp.zeros((8, LANE), F32)
            kcar_ref[...] = jnp.zeros((BLK, D_KV), F32)
            vcar_ref[...] = jnp.zeros((BLK, D_KV), F32)

        vhat, rstd, vln = _layer_norm_fwd(va_ref[...], lg_ref[...], lb_ref[...])
        tril = _tril()
        triu = jnp.logical_not(tril) | (lax.broadcasted_iota(jnp.int32, (BLK, BLK), 0) == lax.broadcasted_iota(jnp.int32, (BLK, BLK), 1))
        lane_b = lax.broadcasted_iota(jnp.int32, (BLK, LANE), 1)
        db_acc = jnp.zeros((BLK, LANE), F32)
        for g in range(GROUPS):
            cols = slice(g * BLK, (g + 1) * BLK)
            vln_g = vln[:, cols].astype(BF16)
            wg = jnp.where(tril, w_ref[g], 0.0).astype(BF16)
            sg = jnp.dot(wg, vln_g, preferred_element_type=F32) + bt_ref[:, g:g + 1]
            za = za_ref[:, cols]
            gate, sig = _silu_parts(za)
            ua = ua_ref[:, cols]
            dya_g = dy_ref[:, cols]
            dya = dya_g * gate
            dp_ref[:, cols] = (dya * sg).astype(BF16)
            dp_ref[:, 2 * D_A + g * BLK:2 * D_A + (g + 1) * BLK] = (
                dya_g * (ua * sg) * (sig * (1.0 + za * (1.0 - sig)))).astype(BF16)
            ds = dya * ua
            ds_b = ds.astype(BF16)
            wtg = jnp.where(triu, wt_ref[g], 0.0).astype(BF16)
            dvln_ref[:, cols] = jnp.dot(wtg, ds_b, preferred_element_type=F32)
            dw_ref[g] += jnp.where(tril, lax.dot_general(ds_b, vln_g, NT, preferred_element_type=F32), 0.0)
            db_acc = db_acc + jnp.where(lane_b == g, jnp.sum(ds, axis=-1, keepdims=True), 0.0)
        dbt_ref[...] += db_acc
        dvln = dvln_ref[...]
        lnst_ref[0:1, :] += jnp.sum(dvln * vhat, axis=0, keepdims=True)
        lnst_ref[1:2, :] += jnp.sum(dvln, axis=0, keepdims=True)
        dvhat = dvln * lg_ref[...]
        m1 = jnp.mean(dvhat, axis=-1, keepdims=True)
        m2 = jnp.mean(dvhat * vhat, axis=-1, keepdims=True)
        dp_ref[:, D_A:2 * D_A] = (rstd * (dvhat - m1 - vhat * m2)).astype(BF16)

        cosp = cosp_ref[...]
        sinp = sinp_ref[...]
        for ks in range(2):
            cols = slice(ks * LANE, (ks + 1) * LANE)
            kr = _rope(k_ref[:, cols], cos_t, sin_t, first_half)
            kpr = _rope(kp_ref[:, cols], cosp, sinp, first_half)
            for n, (kc, vc, kp, vp) in enumerate(zip(_dup_kv(kr, lo), _dup_kv(v_ref[:, cols], lo),
                                                     _dup_kv(kpr, lo), _dup_kv(vp_ref[:, cols], lo))):
                kdup_ref[2 * ks + n, BLK:2 * BLK, :] = kc
                vdup_ref[2 * ks + n, BLK:2 * BLK, :] = vc
                kdup_ref[2 * ks + n, 0:BLK, :] = kp
                vdup_ref[2 * ks + n, 0:BLK, :] = vp
        for sb in range(8):
            cols = slice(sb * LANE, (sb + 1) * LANE)
            _stack_heads(qm_ref, sb, _rope(q_ref[:, cols], cos_t, sin_t, first_half) * SCALE, lo, BF16)
            zb = zb0_ref[:, cols] if sb < 4 else zb1_ref[:, (sb - 4) * LANE:(sb - 3) * LANE]
            gate, sig = _silu_parts(zb)
            sigb_ref[:, cols] = sig
            _stack_heads(dom_ref, sb, dy_ref[:, D_A + sb * LANE:D_A + (sb + 1) * LANE] * gate, lo, F32)

        lane_q = lax.broadcasted_iota(jnp.int32, (Q_PER_KV * BLK, LANE), 1)

        def kv_head(kh, dsink_acc):
            qm = qm_ref[kh]
            kd = kdup_ref[kh]
            vd = vdup_ref[kh]
            probs = probs_ref[kh]
            psink = jnp.sum(jnp.where(lane_q == kh, psink_ref[...], 0.0), axis=-1, keepdims=True)
            probs_b = probs.astype(BF16)
            o = ost_ref[kh]
            dom = dom_ref[kh]
            dom_b = dom.astype(BF16)
            delta = jnp.sum(dom * o, axis=-1, keepdims=True)
            dpr = lax.dot_general(dom_b, vd, NT, preferred_element_type=F32)
            dss = (probs * (dpr - delta)).astype(BF16)
            sd = psink * delta
            for n in range(Q_PER_KV):
                dsink_acc = dsink_acc + jnp.where(lane8 == Q_PER_KV * kh + n, -jnp.sum(sd[n * BLK:(n + 1) * BLK]), 0.0)
            dqst_ref[kh] = jnp.dot(dss, kd, preferred_element_type=F32)
            dkdup_ref[kh] = lax.dot_general(dss, qm, TN, preferred_element_type=F32)
            dvdup_ref[kh] = lax.dot_general(probs_b, dom_b, TN, preferred_element_type=F32)
            return dsink_acc

        dsink_acc = lax.fori_loop(0, N_KV // 2, lambda j, acc: kv_head(2 * j + 1, kv_head(2 * j, acc)), jnp.zeros((8, LANE), F32))
        row0 = lax.broadcasted_iota(jnp.int32, (8, LANE), 0) == 0
        dsink_ref[...] += jnp.where(row0, dsink_acc, 0.0)

        for sb in range(8):
            cols = slice(sb * LANE, (sb + 1) * LANE)
            zb = zb0_ref[:, cols] if sb < 4 else zb1_ref[:, (sb - 4) * LANE:(sb - 3) * LANE]
            sig = sigb_ref[:, cols]
            dyb = dy_ref[:, D_A + sb * LANE:D_A + (sb + 1) * LANE]
            dp_ref[:, OFF_ZB + sb * LANE:OFF_ZB + (sb + 1) * LANE] = (
                dyb * _unstack_heads(ost_ref, sb, lo) * (sig * (1.0 + zb * (1.0 - sig)))).astype(BF16)
            dq_r = _unstack_heads(dqst_ref, sb, lo) * SCALE
            dp_ref[:, OFF_Q + sb * LANE:OFF_Q + (sb + 1) * LANE] = _unrope(dq_r, cos_t, sin_t, first_half).astype(BF16)

        lo2 = lax.broadcasted_iota(jnp.int32, (2 * BLK, LANE), 1) < HEAD
        for ks in range(2):
            cols = slice(ks * LANE, (ks + 1) * LANE)
            ka = dkdup_ref[2 * ks]
            kb = dkdup_ref[2 * ks + 1]
            dk_band = jnp.where(lo2, ka + pltpu.roll(ka, HEAD, 1), kb + pltpu.roll(kb, HEAD, 1))
            va_ = dvdup_ref[2 * ks]
            vb_ = dvdup_ref[2 * ks + 1]
            dv_band = jnp.where(lo2, va_ + pltpu.roll(va_, HEAD, 1), vb_ + pltpu.roll(vb_, HEAD, 1))
            dkr = dk_band[BLK:2 * BLK, :] + kcar_ref[:, cols]
            dp_ref[:, OFF_K + ks * LANE:OFF_K + (ks + 1) * LANE] = _unrope(dkr, cos_t, sin_t, first_half).astype(BF16)
            dp_ref[:, OFF_V + ks * LANE:OFF_V + (ks + 1) * LANE] = (
                dv_band[BLK:2 * BLK, :] + vcar_ref[:, cols]).astype(BF16)
            kcar_ref[:, cols] = dk_band[0:BLK, :]
            vcar_ref[:, cols] = dv_band[0:BLK, :]

    tab = pl.BlockSpec((BLK, LANE), lambda i: (rev(i), 0))
    kvp = lambda col: pl.BlockSpec((BLK, D_KV), lambda i: (prev(i), col))
    vec = pl.BlockSpec((1, D_A), lambda i: (0, 0))
    w3 = pl.BlockSpec((GROUPS, BLK, BLK), lambda i: (0, 0, 0))
    return pl.pallas_call(
        body, name="mix_bwd", grid=(nb,),
        in_specs=_proj_specs(nb) + [
            kvp(OFF_K // D_KV), kvp(OFF_V // D_KV), pl.BlockSpec((BLK, 2 * D_A), lambda i: (rev(i), 0)),
            pl.BlockSpec((None, N_KV, Q_PER_KV * BLK, 2 * BLK), lambda i: (rev(i), 0, 0, 0)),
            pl.BlockSpec((None, N_KV, Q_PER_KV * BLK, LANE), lambda i: (rev(i), 0, 0, 0)),
            pl.BlockSpec((None, Q_PER_KV * BLK, LANE), lambda i: (rev(i), 0, 0)),
            tab, tab, tab, tab, vec, vec, w3, w3, pl.BlockSpec((BLK, GROUPS), lambda i: (0, 0))],
        out_specs=[pl.BlockSpec((BLK, D_IN), lambda i: (rev(i), 0)), pl.BlockSpec((8, D_A), lambda i: (0, 0)), w3,
                   pl.BlockSpec((BLK, LANE), lambda i: (0, 0)), pl.BlockSpec((8, LANE), lambda i: (0, 0))],
        out_shape=[jax.ShapeDtypeStruct((s, D_IN), BF16), jax.ShapeDtypeStruct((8, D_A), F32),
                   jax.ShapeDtypeStruct((GROUPS, BLK, BLK), F32), jax.ShapeDtypeStruct((BLK, LANE), F32),
                   jax.ShapeDtypeStruct((8, LANE), F32)],
        scratch_shapes=[pltpu.VMEM((N_KV, 2 * BLK, LANE), BF16), pltpu.VMEM((N_KV, 2 * BLK, LANE), BF16),
                        pltpu.VMEM((BLK, D_A), F32), pltpu.VMEM((N_KV, Q_PER_KV * BLK, LANE), BF16),
                        pltpu.VMEM((N_KV, Q_PER_KV * BLK, LANE), F32), pltpu.VMEM((N_KV, Q_PER_KV * BLK, LANE), F32),
                        pltpu.VMEM((N_KV, 2 * BLK, LANE), F32), pltpu.VMEM((N_KV, 2 * BLK, LANE), F32),
                        pltpu.VMEM((BLK, D_KV), F32), pltpu.VMEM((BLK, D_KV), F32), pltpu.VMEM((BLK, D_B), F32)],
        compiler_params=_params("arbitrary"),
    )(proj, proj, proj, proj, proj, proj, proj, proj, proj, proj, dy, probs, outs, psinks, *tables, ln_g, ln_b,
      w_sp, w_sp_t, b_sp_t)


def _dh_call(dproj, w_bf, x, dx2, scale, norm_g):
    s = x.shape[0]
    tm = min(s, 512)
    tk = W_IN_SHARD
    nk = D_IN // tk

    def body(dp_ref, w_ref, x_ref, dx2_ref, sc_ref, g_ref, gx_ref, st_ref, acc_ref):
        i = pl.program_id(0)
        k = pl.program_id(1)

        @pl.when((i == 0) & (k == 0))
        def _():
            st_ref[...] = jnp.zeros((8, D), F32)

        @pl.when(k == 0)
        def _():
            acc_ref[...] = jnp.zeros((tm, D), F32)

        acc_ref[...] += lax.dot_general(dp_ref[...], w_ref[...], NT, preferred_element_type=F32)

        @pl.when(k == nk - 1)
        def _():
            g = g_ref[...]
            one_sc = 1.0 + sc_ref[...]

            def chunk(n, carry):
                rows = pl.ds(pl.multiple_of(n * BLK, BLK), BLK)
                dh = acc_ref[rows, :]
                xv = x_ref[rows, :]
                r = lax.rsqrt(jnp.mean(xv * xv, axis=-1, keepdims=True) + EPS)
                xn = xv * r
                dhn = dh * one_sc
                dxn = dhn * g
                gx_ref[rows, :] = dx2_ref[rows, :] + r * (dxn - xn * jnp.mean(dxn * xn, axis=-1, keepdims=True))
                st_ref[0:1, :] += jnp.sum(dh, axis=0, keepdims=True)
                st_ref[1:2, :] += jnp.sum(dh * (xn * g), axis=0, keepdims=True)
                st_ref[2:3, :] += jnp.sum(dhn * xn, axis=0, keepdims=True)
                return carry

            lax.fori_loop(0, tm // BLK, chunk, 0)

    vec = pl.BlockSpec((1, D), lambda i, k: (0, 0))
    rows = lambda: pl.BlockSpec((tm, D), lambda i, k: (i, 0))
    return pl.pallas_call(
        body, name="dh", grid=(s // tm, nk),
        in_specs=[pl.BlockSpec((tm, tk), lambda i, k: (i, k)), pl.BlockSpec((D, tk), lambda i, k: (0, k)), rows(), rows(), vec, vec],
        out_specs=[rows(), pl.BlockSpec((8, D), lambda i, k: (0, 0))],
        out_shape=[jax.ShapeDtypeStruct((s, D), F32), jax.ShapeDtypeStruct((8, D), F32)],
        scratch_shapes=[pltpu.VMEM((tm, D), F32)],
        compiler_params=_params("arbitrary", "arbitrary"),
    )(dproj, w_bf, x, dx2, scale, norm_g)


def _adam_math(w, g, m, v):
    m_new = ADAM_B1 * m + (1.0 - ADAM_B1) * g
    v_new = ADAM_B2 * v + (1.0 - ADAM_B2) * (g * g)
    m_hat = m_new / ADAM_C1
    v_hat = v_new / ADAM_C2
    delta = -ADAM_LR * (m_hat / (jnp.sqrt(v_hat) + ADAM_EPS) + ADAM_WD * w)
    return delta, m_new, v_new


def _adam_small_call(tensors):
    n = len(tensors)

    def body(*refs):
        ins, outs = refs[:4 * n], refs[4 * n:]
        for t in range(n):
            w_ref, g_ref, m_ref, v_ref = ins[4 * t:4 * t + 4]
            d, mo, vo = _adam_math(w_ref[...], g_ref[...], m_ref[...], v_ref[...])
            outs[3 * t][...], outs[3 * t + 1][...], outs[3 * t + 2][...] = d, mo, vo

    vm = pl.BlockSpec(memory_space=pltpu.VMEM)
    flat = [a for t in tensors for a in t]
    out = pl.pallas_call(
        body, name="adam_small", in_specs=[vm] * (4 * n), out_specs=[vm] * (3 * n),
        out_shape=[jax.ShapeDtypeStruct(t[0].shape, F32) for t in tensors for _ in range(3)],
        compiler_params=pltpu.CompilerParams(vmem_limit_bytes=VMEM_LIMIT),
    )(*flat)
    return [tuple(out[3 * t:3 * t + 3]) for t in range(n)]


def _adam_halves_call(pos, w, mine, theirs, m, v, name):
    r, n = w.shape
    half = r // 2
    tr = ADAM_ROWS
    nh = half // tr

    def body(pos_ref, w_ref, mine_ref, theirs_ref, m_ref, v_ref, g_ref, d_ref, mo_ref, vo_ref):
        is_mine = (pl.program_id(0) // nh) == pos_ref[1]
        g = jnp.where(is_mine, mine_ref[...], theirs_ref[...])
        g_ref[...] = g
        d_ref[...], mo_ref[...], vo_ref[...] = _adam_math(w_ref[...], g, m_ref[...], v_ref[...])

    spec = lambda: pl.BlockSpec((tr, n), lambda i, pos: (i, 0))

    def half_spec(core_of_half):
        def index(i, pos):
            first = core_of_half(pos) == 0
            active = (i // nh == 0) == first
            return jnp.where(active, i % nh, jnp.where(first, nh - 1, 0)), 0
        return pl.BlockSpec((tr, n), index)

    return pl.pallas_call(
        body, name=name,
        grid_spec=pltpu.PrefetchScalarGridSpec(
            num_scalar_prefetch=1, grid=(r // tr,),
            in_specs=[spec(), half_spec(lambda pos: pos[1]), half_spec(lambda pos: 1 - pos[1]), spec(), spec()],
            out_specs=[spec() for _ in range(4)]),
        out_shape=[jax.ShapeDtypeStruct((r, n), F32)] * 4, compiler_params=_params("arbitrary"),
    )(pos, w, mine, theirs, m, v)


def _adam_outer_call(w, ct, dm, m, v, name):
    r, n = w.shape
    tr = ADAM_ROWS

    def body(w_ref, ct_ref, dm_ref, m_ref, v_ref, g_ref, d_ref, mo_ref, vo_ref):
        g = ct_ref[:, 0:1] * dm_ref[0:1, :]
        for b in range(1, N_DEV):
            g = g + ct_ref[:, b:b + 1] * dm_ref[b:b + 1, :]
        g_ref[...] = g
        d_ref[...], mo_ref[...], vo_ref[...] = _adam_math(w_ref[...], g, m_ref[...], v_ref[...])

    spec = lambda: pl.BlockSpec((tr, n), lambda i: (i, 0))
    return pl.pallas_call(
        body, name=name, grid=(r // tr,),
        in_specs=[spec(), pl.BlockSpec((tr, N_DEV), lambda i: (i, 0)), pl.BlockSpec((N_DEV, n), lambda i: (0, 0)), spec(), spec()],
        out_specs=[spec() for _ in range(4)],
        out_shape=[jax.ShapeDtypeStruct((r, n), F32)] * 4, compiler_params=_params("parallel"),
    )(w, ct, dm, m, v)


def _sum_pieces_call(pos, part, part_block, recvs, name):
    r, n = recvs[0].shape[1:]
    tr = min(r, 256)
    nrb = r // tr

    def body(pos_ref, p_ref, *refs):
        acc = p_ref[...].astype(F32)
        for r_ref in refs[:-1]:
            for d in range(r_ref.shape[0]):
                acc = acc + r_ref[d].astype(F32)
        refs[-1][...] = acc

    return pl.pallas_call(
        body, name=name,
        grid_spec=pltpu.PrefetchScalarGridSpec(
            num_scalar_prefetch=1, grid=(nrb,),
            in_specs=[pl.BlockSpec((tr, n), lambda i, pos: part_block(i, pos, nrb))] + [
                pl.BlockSpec((rv.shape[0], tr, n), lambda i, pos: (0, i, 0)) for rv in recvs],
            out_specs=pl.BlockSpec((tr, n), lambda i, pos: (i, 0))),
        out_shape=jax.ShapeDtypeStruct((r, n), F32), compiler_params=_params("parallel"),
    )(pos, part, *recvs)


def _coords():
    return lax.axis_index("x"), lax.axis_index("y"), lax.axis_index("c")


def _allgather_sum_call(blk, name, with_sum):
    m_per, n = blk.shape

    def body(x_ref, out_ref, *rest):
        if with_sum:
            sum_ref, send_sems, recv_sems, local_sem = rest
        else:
            send_sems, recv_sems, local_sem = rest
        x, y, c = _coords()
        me, sibling = (x, y, c), (x, y, 1 - c)
        chips = [(1 - x, y), (x, 1 - y), (1 - x, 1 - y)]

        def rows(px, py, pc):
            return out_ref.at[pl.ds((4 * px + 2 * py + pc) * m_per, m_per), :]

        def copy(k, block, to, src=None):
            return pltpu.make_async_remote_copy(
                src_ref=rows(*block) if src is None else src, dst_ref=rows(*block),
                send_sem=send_sems.at[k], recv_sem=recv_sems.at[k], device_id=to, device_id_type=MESH)

        mine = pltpu.make_async_copy(x_ref, rows(*me), local_sem)
        mine.start()
        first = [copy(0, me, sibling, src=x_ref)]
        first += [copy(1 + j, me, (*chip, c), src=x_ref) for j, chip in enumerate(chips)]
        for cp in first:
            cp.start()
        passed = [copy(4 + j, (*chip, c), sibling) for j, chip in enumerate(chips)]
        for j, chip in enumerate(chips):
            copy(1 + j, (*chip, c), me).wait_recv()
            passed[j].start()
        copy(0, sibling, me).wait_recv()
        for j, chip in enumerate(chips):
            copy(4 + j, (*chip, 1 - c), me).wait_recv()
        for cp in first + passed:
            cp.wait_send()
        mine.wait()
        if with_sum:
            acc = out_ref[0:m_per, :]
            for d in range(1, N_DEV):
                acc = acc + out_ref[d * m_per:(d + 1) * m_per, :]
            sum_ref[...] = acc

    vm = pl.BlockSpec(memory_space=pltpu.VMEM)
    out_shape = [jax.ShapeDtypeStruct((N_DEV * m_per, n), F32)]
    if with_sum:
        out_shape.append(jax.ShapeDtypeStruct((m_per, n), F32))
    return pl.pallas_call(
        body, name=name, out_shape=out_shape, in_specs=[vm], out_specs=[vm] * len(out_shape),
        scratch_shapes=[pltpu.SemaphoreType.DMA((7,)), pltpu.SemaphoreType.DMA((7,)), pltpu.SemaphoreType.DMA],
        compiler_params=pltpu.CompilerParams(vmem_limit_bytes=VMEM_LIMIT),
    )(blk)


HBM_SPEC = pl.BlockSpec(memory_space=pltpu.HBM)
SEM_SPEC = pl.BlockSpec(memory_space=pltpu.SEMAPHORE)
SIDE_EFFECT = pltpu.SideEffectType.DATAFLOW_SIDE_EFFECTING


def _peer(x, y, c, q, cb):
    return (1 - x if q & 2 else x, 1 - y if q & 1 else y, 1 - c if cb else c)


def _weight_part(fi_ref, fo_ref, xyc, which, q, pc, sub=None, cp=None):
    x_, y_, c_ = xyc
    px, py, _ = _peer(x_, y_, c_, q, 0)
    shard = 2 * px + py
    n = D // 2 if which == 0 else W_OUT_SHARD // 2
    base = pc * n
    if sub is not None:
        n //= 2
        base = base + sub * n
    if which == 0:
        off, w = (0, W_IN_SHARD) if cp is None else W_IN_PARTS[cp]
        return fi_ref.at[pl.ds(base, n), pl.ds(shard * W_IN_SHARD + off, w)]
    return fo_ref.at[pl.ds(shard * W_OUT_SHARD + base, n), :]


DIRECT_HALVES = ((0, 0, 1), (0, 0, 2), (0, 1, 1), (0, 1, 2), (1, None, 1), (1, None, 2))


def _weights_send_call(wi_full, wo_full):
    n = len(DIRECT_HALVES)

    def body(wi_ref, wo_ref, send_sems, recv_sems, wi_thru, wo_thru, token):
        xyc = _coords()
        for k, (which, cp, q) in enumerate(DIRECT_HALVES):
            ref = _weight_part(wi_ref, wo_ref, xyc, which, 0, xyc[2], None, cp)
            pltpu.make_async_remote_copy(src_ref=ref, dst_ref=ref, send_sem=send_sems.at[k], recv_sem=recv_sems.at[k],
                                         device_id=_peer(*xyc, q, 0), device_id_type=MESH).start()
        token[...] = jnp.zeros_like(token)

    return pl.pallas_call(
        body, name="send_weights",
        out_shape=(pltpu.SemaphoreType.DMA((n,)), pltpu.SemaphoreType.DMA((n,)), pltpu.HBM(wi_full.shape, wi_full.dtype),
                   pltpu.HBM(wo_full.shape, wo_full.dtype), jax.ShapeDtypeStruct((1, 1), F32)),
        in_specs=(HBM_SPEC, HBM_SPEC), out_specs=(SEM_SPEC, SEM_SPEC, HBM_SPEC, HBM_SPEC, pl.BlockSpec(memory_space=pltpu.VMEM)),
        input_output_aliases={0: 2, 1: 3},
        compiler_params=pltpu.CompilerParams(has_side_effects=SIDE_EFFECT),
    )(pltpu.with_memory_space_constraint(wi_full, pltpu.HBM), pltpu.with_memory_space_constraint(wo_full, pltpu.HBM))


def _weights_landed_call(started, after):
    send_sems, recv_sems, wi_thru, wo_thru, _ = started

    def body(wi_ref, wo_ref, send_sems, recv_sems, after_ref, wi_out, wo_out):
        xyc = _coords()
        for k, (which, cp, q) in enumerate(DIRECT_HALVES):
            own = _weight_part(wi_ref, wo_ref, xyc, which, 0, xyc[2], None, cp)
            theirs = _weight_part(wi_ref, wo_ref, xyc, which, q, xyc[2], None, cp)
            cp_ = pltpu.make_async_remote_copy(src_ref=own, dst_ref=theirs, send_sem=send_sems.at[k], recv_sem=recv_sems.at[k],
                                               device_id=_peer(*xyc, q, 0), device_id_type=MESH)
            cp_.wait_send()
            cp_.wait_recv()

    return pl.pallas_call(
        body, name="weights_landed",
        out_shape=(pltpu.HBM(wi_thru.shape, wi_thru.dtype), pltpu.HBM(wo_thru.shape, wo_thru.dtype)),
        in_specs=(HBM_SPEC, HBM_SPEC, SEM_SPEC, SEM_SPEC, pl.BlockSpec(memory_space=pl.ANY)), out_specs=(HBM_SPEC, HBM_SPEC),
        input_output_aliases={0: 0, 1: 1},
        compiler_params=pltpu.CompilerParams(has_side_effects=SIDE_EFFECT),
    )(wi_thru, wo_thru, send_sems, recv_sems, after)


def _w_in_piece(slots):
    def piece(part_ref, k, to):
        return part_ref.at[pl.ds(to[2] * (D // 2), D // 2), pl.ds(slots[k] * W_IN_SHARD, W_IN_SHARD)]
    return piece


def _w_out_piece(part_ref, k, to):
    ho = W_OUT_SHARD // 2
    return part_ref.at[pl.ds((2 * to[0] + to[1]) * W_OUT_SHARD + to[2] * ho, ho), :]


def _group_piece(part_ref, k, to):
    return part_ref.at[4 * to[0] + 2 * to[1] + to[2]]


def _whole_piece(part_ref, k, to):
    return part_ref


def _exchange_start_call(groups, name):
    ng = len(groups)
    lands = [lax.empty((len(rels),) + slot_shape, part.dtype) for part, rels, _, slot_shape in groups]

    def body(*refs):
        ins, outs = refs[:2 * ng], refs[2 * ng:]
        x, y, c = _coords()
        for g, (_, rels, piece, _) in enumerate(groups):
            part_ref, land_ref = ins[2 * g], ins[2 * g + 1]
            send_sems, recv_sems = outs[4 * g], outs[4 * g + 1]
            for k, (q, cb) in enumerate(rels):
                to = _peer(x, y, c, q, cb)
                pltpu.make_async_remote_copy(src_ref=piece(part_ref, k, to), dst_ref=land_ref.at[k], send_sem=send_sems.at[k],
                                             recv_sem=recv_sems.at[k], device_id=to, device_id_type=MESH).start()
        outs[-1][...] = jnp.zeros_like(outs[-1])

    out_shape, out_specs, operands = [], [], []
    for (part, rels, _, _), land in zip(groups, lands):
        n = len(rels)
        out_shape += [pltpu.SemaphoreType.DMA((n,)), pltpu.SemaphoreType.DMA((n,)), pltpu.HBM(part.shape, part.dtype),
                      pltpu.HBM(land.shape, land.dtype)]
        out_specs += [SEM_SPEC, SEM_SPEC, HBM_SPEC, HBM_SPEC]
        operands += [pltpu.with_memory_space_constraint(part, pltpu.HBM), pltpu.with_memory_space_constraint(land, pltpu.HBM)]
    out = pl.pallas_call(
        body, name=name,
        out_shape=tuple(out_shape) + (jax.ShapeDtypeStruct((1, 1), F32),),
        in_specs=(HBM_SPEC,) * (2 * ng), out_specs=tuple(out_specs) + (pl.BlockSpec(memory_space=pltpu.VMEM),),
        input_output_aliases={j: 4 * (j // 2) + 2 + j % 2 for j in range(2 * ng)},
        compiler_params=pltpu.CompilerParams(has_side_effects=SIDE_EFFECT),
    )(*operands)
    return [tuple(out[4 * g:4 * g + 4]) for g in range(ng)], out[-1]


def _exchange_wait_call(started, groups, after, name):
    ng = len(groups)

    def body(*refs):
        ins = refs[:4 * ng]
        x, y, c = _coords()
        for g, (_, rels, piece, _) in enumerate(groups):
            part_ref, land_ref, send_sems, recv_sems = ins[4 * g:4 * g + 4]
            for k, (q, cb) in enumerate(rels):
                to = _peer(x, y, c, q, cb)
                cp = pltpu.make_async_remote_copy(src_ref=piece(part_ref, k, to), dst_ref=land_ref.at[k], send_sem=send_sems.at[k],
                                                  recv_sem=recv_sems.at[k], device_id=to, device_id_type=MESH)
                cp.wait_send()
                cp.wait_recv()

    operands, in_specs, out_shape = [], [], []
    for send_sems, recv_sems, part_thru, land_thru in started:
        operands += [part_thru, land_thru, send_sems, recv_sems]
        in_specs += [HBM_SPEC, HBM_SPEC, SEM_SPEC, SEM_SPEC]
        out_shape += [pltpu.HBM(part_thru.shape, part_thru.dtype), pltpu.HBM(land_thru.shape, land_thru.dtype)]
    out = pl.pallas_call(
        body, name=name, out_shape=tuple(out_shape),
        in_specs=tuple(in_specs) + (pl.BlockSpec(memory_space=pl.ANY),), out_specs=(HBM_SPEC,) * (2 * ng),
        input_output_aliases={4 * g + j: 2 * g + j for g in range(ng) for j in range(2)},
        compiler_params=pltpu.CompilerParams(has_side_effects=SIDE_EFFECT),
    )(*operands, after)
    return [tuple(out[2 * g:2 * g + 2]) for g in range(ng)]


def _rope_tables(s):
    inv_freq = np.float32(10000.0) ** (-np.arange(0, HEAD, 2, dtype=np.float32) / np.float32(HEAD))
    ang = np.arange(s, dtype=np.float32)[:, None] * inv_freq[None, :]
    cos = np.tile(np.cos(ang), (1, LANE // (HEAD // 2))).astype(np.float32)
    sin = np.tile(np.sin(ang), (1, LANE // (HEAD // 2))).astype(np.float32)
    first_half = (np.arange(LANE) % HEAD) < (HEAD // 2)
    sin = np.where(first_half[None, :], -sin, sin)
    behind = lambda t: np.concatenate([t[:BLK], t[:-BLK]], axis=0)
    return tuple(jnp.asarray(t) for t in (cos, sin, behind(cos), behind(sin)))


def kernel(x, c, w_ada, b_ada, norm_g, w_in, ln_v_g, ln_v_b, w_spatial, b_spatial, sinks, w_out, w_ada_final, b_ada_final, final_norm_g, loss_target, m_w_ada, m_b_ada, m_norm_g, m_w_in, m_ln_v_g, m_ln_v_b, m_w_spatial, m_b_spatial, m_sinks, m_w_out, m_w_ada_final, m_b_ada_final, m_final_norm_g, v_w_ada, v_b_ada, v_norm_g, v_w_in, v_ln_v_g, v_ln_v_b, v_w_spatial, v_b_spatial, v_sinks, v_w_out, v_w_ada_final, v_b_ada_final, v_final_norm_g):
    s = x.shape[1]
    ax, ay, ac = _coords()
    chip = 2 * ax + ay
    me = 4 * ax + 2 * ay + ac
    n_ada = w_ada.shape[2]
    n_adaf = w_ada_final.shape[1]

    x2d = x.reshape(s, D)
    tgt = loss_target.reshape(s, D)
    w_ada2, w_in2, w_out2 = w_ada[0], w_in[0], w_out[0]
    b_ada_f2 = b_ada_final.reshape(1, 2 * D)
    gf = final_norm_g.reshape(1, D)

    pos = jnp.stack([chip, ac]).astype(jnp.int32)
    sent_w = _weights_send_call(_cast_into_call(pos, w_in2, (D, D_IN), "cast_w_in"),
                                _cast_into_call(pos, w_out2, (D, D), "cast_w_out"))
    c_all = _allgather_sum_call(jnp.pad(c, ((0, 7), (0, 0))) + sent_w[4], "gather_c", False)[0][::8]
    mod_p, c_act = _rowmat_call(c_all, w_ada2, lax.dynamic_slice(b_ada, (0, chip * n_ada), (1, n_ada)), "mod")
    modf_p, _ = _rowmat_call(c_all, w_ada_final, lax.dynamic_slice(b_ada_f2, (0, chip * n_adaf), (1, n_adaf)), "mod_final")
    mods = _allgather_sum_call(jnp.concatenate([mod_p, modf_p], axis=1), "gather_mod", False)[0]
    my_rows = [lax.dynamic_slice(mods, (16 * j + me, 0), (1, n_ada + n_adaf)) for j in range(N_CHIP)]
    mod = jnp.concatenate([r[:, :n_ada] for r in my_rows], axis=1)
    mod_f = jnp.concatenate([r[:, n_ada:] for r in my_rows], axis=1)
    shift, scale, gate = mod[:, :D], mod[:, D:2 * D], mod[:, 2 * D:]
    shift_f, scale_f = mod_f[:, :D], mod_f[:, D:]

    w_in_own, w_out_own = _weights_landed_call(sent_w, mods)

    tables = _rope_tables(s)
    cos, sin = tables[:2]
    b_sp_t = b_spatial[0].T
    sinks1 = sinks.reshape(N_Q)
    h, proj, w_in_bf, w_out_bf = _proj_gather_call(pos, x2d, shift, scale, norm_g, w_in_own, w_out_own)
    y, probs, attn_out, psinks = _mix_fwd_call(proj, cos, sin, ln_v_g, ln_v_b, w_spatial[0], b_sp_t, sinks1)
    dx2, do, dy, st_tail = _tail_call(y, w_out_bf, x2d, tgt, gate, shift_f, scale_f, gf)

    rel_o = [(0, 1), (1, 0), (1, 1), (2, 0), (2, 1), (3, 0), (3, 1)]
    rel_a = [(1, 0), (1, 1), (2, 0), (2, 1)]
    rel_b = [(3, 0), (3, 1), (0, 1)]
    piece_a, piece_b = _w_in_piece([0, 0, 1, 1]), _w_in_piece([0, 0, 1])
    half_in, half_out = (D // 2, W_IN_SHARD), (W_OUT_SHARD // 2, D)

    g_w_out_p = _tn_call(y, do, "grad_w_out")
    grp_o = [(g_w_out_p, rel_o, _w_out_piece, half_out)]
    st_o, tok_o = _exchange_start_call(grp_o, "send_w_out")
    dproj, st_ln, d_wsp, d_bsp_t, d_sink = _mix_bwd_call(
        proj, dy, probs, attn_out, psinks, tables, ln_v_g + tok_o, ln_v_b, w_spatial[0], jnp.swapaxes(w_spatial[0], 1, 2),
        b_sp_t)
    g_w_in_a = _tn_shards_call(pos, h, dproj, (1, 2), "grad_w_in_a")
    grp_a = [(g_w_in_a, rel_a, piece_a, half_in), (d_wsp, rel_o, _group_piece, (BLK, BLK))]
    st_a, tok_a = _exchange_start_call(grp_a, "send_w_in_a")
    g_w_in_b = _tn_shards_call(pos, h, dproj, (3, 0), "grad_w_in_b")
    grp_b = [(g_w_in_b, rel_b, piece_b, half_in)]
    st_b, tok_b = _exchange_start_call(grp_b, "send_w_in_b")
    grad_x, st_dh = _dh_call(dproj, w_in_bf, x2d, dx2, scale + (tok_a + tok_b), norm_g)

    ((g_w_out_p, recv_o),) = _exchange_wait_call(st_o, grp_o, st_dh, "wait_w_out")
    (_, recv_a), (d_wsp, recv_s) = _exchange_wait_call(st_a, grp_a, st_dh, "wait_w_in_a")
    ((g_w_in_b, recv_b),) = _exchange_wait_call(st_b, grp_b, st_dh, "wait_w_in_b")
    mine_in = _sum_pieces_call(pos, g_w_in_b, lambda i, p, nrb: (p[1] * nrb + i, 1), [recv_a, recv_b], "sum_w_in")
    mine_out = _sum_pieces_call(pos, g_w_out_p, lambda i, p, nrb: ((2 * p[0] + p[1]) * nrb + i, 0), [recv_o], "sum_w_out")
    wsp_group = _sum_pieces_call(pos, d_wsp.reshape(GROUPS * BLK, BLK), lambda i, p, nrb: (2 * p[0] + p[1], 0), [recv_s],
                                 "sum_w_spatial")
    to_sibling = [(0, 1)]
    grp_p = [(mine_in, to_sibling, _whole_piece, half_in), (mine_out, to_sibling, _whole_piece, half_out)]
    st_p, tok_p = _exchange_start_call(grp_p, "swap_halves")

    misc = jnp.concatenate([st_ln, d_bsp_t[:, :GROUPS].T, d_sink, jnp.zeros((8, D - D_A - 2 * LANE), F32)], axis=1)
    pack = jnp.concatenate([wsp_group.reshape(8, D) + tok_p, st_tail, st_dh, misc], axis=0)
    rows = pack.shape[0]
    packs, tot = _allgather_sum_call(pack, "gather_small", True)
    packs = packs.reshape(N_DEV, rows, D)
    dmod_all = jnp.concatenate([packs[:, 16, :], packs[:, 17, :], packs[:, 11, :]], axis=1)
    dmodf_all = jnp.concatenate([packs[:, 8, :], packs[:, 9, :]], axis=1)
    loss = tot[13, 0]
    (mine_in, theirs_in), (mine_out, theirs_out) = _exchange_wait_call(st_p, grp_p, tot, "swapped_halves")
    small = {
        "b_ada": jnp.concatenate([tot[16:17], tot[17:18], tot[11:12]], axis=1),
        "norm_g": tot[18:19],
        "ln_v_g": tot[24:25, :D_A],
        "ln_v_b": tot[25:26, :D_A],
        "w_spatial": packs[:, 0:8, :].reshape(GROUPS * BLK, BLK),
        "b_spatial": tot[24:32, D_A:D_A + BLK],
        "sinks": tot[24:25, D_A + LANE:D_A + LANE + N_Q],
        "b_ada_final": jnp.concatenate([tot[8:9], tot[9:10]], axis=1),
        "final_norm_g": tot[10:11],
    }

    weights = dict(w_ada=w_ada, b_ada=b_ada, norm_g=norm_g, w_in=w_in, ln_v_g=ln_v_g, ln_v_b=ln_v_b, w_spatial=w_spatial,
                   b_spatial=b_spatial, sinks=sinks, w_out=w_out, w_ada_final=w_ada_final, b_ada_final=b_ada_final,
                   final_norm_g=final_norm_g)
    m_in = dict(w_ada=m_w_ada, b_ada=m_b_ada, norm_g=m_norm_g, w_in=m_w_in, ln_v_g=m_ln_v_g, ln_v_b=m_ln_v_b,
                w_spatial=m_w_spatial, b_spatial=m_b_spatial, sinks=m_sinks, w_out=m_w_out, w_ada_final=m_w_ada_final,
                b_ada_final=m_b_ada_final, final_norm_g=m_final_norm_g)
    v_in = dict(w_ada=v_w_ada, b_ada=v_b_ada, norm_g=v_norm_g, w_in=v_w_in, ln_v_g=v_ln_v_g, ln_v_b=v_ln_v_b,
                w_spatial=v_w_spatial, b_spatial=v_b_spatial, sinks=v_sinks, w_out=v_w_out, w_ada_final=v_w_ada_final,
                b_ada_final=v_b_ada_final, final_norm_g=v_final_norm_g)
    c_act_t = c_act.T
    outer = {"w_ada": lax.dynamic_slice(dmod_all, (0, chip * n_ada), (N_DEV, n_ada)),
             "w_ada_final": lax.dynamic_slice(dmodf_all, (0, chip * n_adaf), (N_DEV, n_adaf))}
    halves = {"w_in": (mine_in, theirs_in[0]), "w_out": (mine_out, theirs_out[0])}
    done = {}
    for name, (mine, theirs) in halves.items():
        shape2 = (2 * mine.shape[0], mine.shape[1])
        done[name] = _adam_halves_call(pos, weights[name].reshape(shape2), mine, theirs, m_in[name].reshape(shape2),
                                       v_in[name].reshape(shape2), "adam_" + name)
    for name, dm in outer.items():
        shape2 = (D, dm.shape[1])
        done[name] = _adam_outer_call(weights[name].reshape(shape2), c_act_t, dm, m_in[name].reshape(shape2),
                                      v_in[name].reshape(shape2), "adam_" + name)
    updates = _adam_small_call([(weights[name].reshape(g.shape), g, m_in[name].reshape(g.shape), v_in[name].reshape(g.shape))
                                for name, g in small.items()])
    for (name, g), upd in zip(small.items(), updates):
        done[name] = (g, *upd)
    outs = [[done[name][k].reshape(w.shape) for name, w in weights.items()] for k in range(4)]
    return (loss, grad_x.reshape(x.shape), *outs[0], *outs[1], *outs[2], *outs[3])
```

```python
import numpy as np
import jax
import jax.numpy as jnp
from jax import lax
from jax.experimental import pallas as pl
from jax.experimental.pallas import tpu as pltpu

F32 = jnp.float32
BF16 = jnp.bfloat16
MESH = pl.DeviceIdType.MESH

D = 2048
D_A = 1024
D_B = 1024
D_KV = 256
HEAD = 64
N_Q = 16
N_KV = 4
Q_PER_KV = N_Q // N_KV
BLK = 128
GROUPS = 8
D_IN = 5632
OFF_Q, OFF_K, OFF_V, OFF_ZB = 3072, 4096, 4352, 4608
N_CHIP = 4
N_DEV = 8
W_IN_SHARD = D_IN // N_CHIP
W_OUT_SHARD = D // N_CHIP
EPS = 1e-5
SCALE = HEAD ** -0.5
NEG = -1e30
LANE = 128
VMEM_LIMIT = 56 * 1024 * 1024

ADAM_LR, ADAM_B1, ADAM_B2, ADAM_EPS, ADAM_WD, ADAM_STEP = 0.001, 0.9, 0.999, 1e-08, 0.01, 10
ADAM_C1 = 1.0 - ADAM_B1 ** ADAM_STEP
ADAM_C2 = 1.0 - ADAM_B2 ** ADAM_STEP
ADAM_ROWS = 256

NT = (((1,), (1,)), ((), ()))
TN = (((0,), (0,)), ((), ()))


def _params(*sem):
    return pltpu.CompilerParams(dimension_semantics=sem, vmem_limit_bytes=VMEM_LIMIT)


def _silu_parts(z):
    sig = 1.0 / (1.0 + jnp.exp(-z))
    return z * sig, sig


def _swap_halves(v, first_half):
    return jnp.where(first_half, pltpu.roll(v, 96, 1), pltpu.roll(v, 32, 1))


def _rope(v, cos_t, sin_s, first_half):
    return v * cos_t + _swap_halves(v, first_half) * sin_s


def _unrope(dv, cos_t, sin_s, first_half):
    return dv * cos_t - _swap_halves(dv, first_half) * sin_s


def _lane_masks():
    lane = lax.broadcasted_iota(jnp.int32, (BLK, LANE), 1)
    return (lane % HEAD) < (HEAD // 2), lane < HEAD


def _band_valid(first_block_bound, rows=BLK):
    rr = lax.broadcasted_iota(jnp.int32, (rows, 2 * BLK), 0) & (BLK - 1)
    jj = lax.broadcasted_iota(jnp.int32, (rows, 2 * BLK), 1)
    return (jj > rr) & (jj <= rr + BLK) & (jj >= first_block_bound)


def _dup_kv(slab, lo):
    rolled = pltpu.roll(slab, HEAD, 1)
    return jnp.where(lo, slab, rolled).astype(BF16), jnp.where(lo, rolled, slab).astype(BF16)


def _stack_heads(ref, sb, slab, lo, dtype):
    kh, base = sb // 2, 2 * (sb % 2) * BLK
    zero = jnp.zeros_like(slab)
    ref[kh, base:base + BLK, :] = jnp.where(lo, slab, zero).astype(dtype)
    ref[kh, base + BLK:base + 2 * BLK, :] = jnp.where(lo, zero, slab).astype(dtype)


def _unstack_heads(ref, sb, lo):
    kh, base = sb // 2, 2 * (sb % 2) * BLK
    return jnp.where(lo, ref[kh, base:base + BLK, :], ref[kh, base + BLK:base + 2 * BLK, :])


def _sink_column(sinks_ref, kh):
    row = lax.broadcasted_iota(jnp.int32, (Q_PER_KV * BLK, 1), 0)
    col = jnp.full(row.shape, sinks_ref[Q_PER_KV * kh + Q_PER_KV - 1], F32)
    for n in range(Q_PER_KV - 2, -1, -1):
        col = jnp.where(row < (n + 1) * BLK, sinks_ref[Q_PER_KV * kh + n], col)
    return col


def _tril():
    t = lax.broadcasted_iota(jnp.int32, (BLK, BLK), 0)
    s = lax.broadcasted_iota(jnp.int32, (BLK, BLK), 1)
    return s <= t


def _layer_norm_fwd(va, lg, lb):
    mu = jnp.mean(va, axis=-1, keepdims=True)
    xc = va - mu
    rstd = lax.rsqrt(jnp.mean(xc * xc, axis=-1, keepdims=True) + EPS)
    vhat = xc * rstd
    return vhat, rstd, vhat * lg + lb


def _softmax_sink(qm, kdup, bias, sink):
    s = lax.dot_general(qm, kdup, NT, preferred_element_type=F32) + bias
    m = jnp.maximum(jnp.max(s, axis=-1, keepdims=True), sink)
    p = jnp.exp(s - m)
    esink = jnp.exp(sink - m)
    inv = 1.0 / (jnp.sum(p, axis=-1, keepdims=True) + esink)
    return p * inv, esink * inv


def _band_bias(bias_ref):
    rows = bias_ref.shape[1]
    bias_ref[0] = jnp.where(_band_valid(BLK, rows), 0.0, NEG)
    bias_ref[1] = jnp.where(_band_valid(0, rows), 0.0, NEG)


def _rowmat_call(c_all, w, b, name):
    n = w.shape[1]
    tn = 512

    def body(c_ref, w_ref, b_ref, o_ref, ca_ref):
        ca, _ = _silu_parts(c_ref[...])
        ca_ref[...] = ca
        o_ref[...] = jnp.dot(ca.astype(BF16), w_ref[...].astype(BF16), preferred_element_type=F32) + b_ref[...]

    return pl.pallas_call(
        body, name=name, grid=(n // tn,),
        in_specs=[pl.BlockSpec((N_DEV, D), lambda j: (0, 0)), pl.BlockSpec((D, tn), lambda j: (0, j)),
                  pl.BlockSpec((1, tn), lambda j: (0, j))],
        out_specs=[pl.BlockSpec((N_DEV, tn), lambda j: (0, j)), pl.BlockSpec((N_DEV, D), lambda j: (0, 0))],
        out_shape=[jax.ShapeDtypeStruct((N_DEV, n), F32), jax.ShapeDtypeStruct((N_DEV, D), F32)],
        compiler_params=_params("arbitrary"),
    )(c_all, w, b)


W_IN_PARTS = ((0, 768), (768, 640))
OUT_STREAMS = 4
X_STREAMS = 4


def _proj_gather_call(pos, x, shift, scale, norm_g, wi_full, wo_full):
    s = x.shape[0]
    tm = min(s, 512)
    nrow = s // tm
    hi = D // 2
    ho = W_OUT_SHARD // 2
    phases = [(0, None), (1, 0), (2, 0), (1, 1), (2, 1), (3, 0), (3, 1)]

    def body(pos_ref, *refs):
        x_refs = refs[:X_STREAMS]
        (sh_ref, sc_ref, g_ref, _, _, h_ref, proj_ref, fi_ref, fo_ref,
         h_all, wbuf, obuf, send_sems, recv_sems, load_sems, out_sems) = refs[X_STREAMS:]
        p = pl.program_id(0)
        i = pl.program_id(1)
        x_, y_, c_ = _coords()
        me, sibling = (x_, y_, c_), (x_, y_, 1 - c_)

        def shard_of(q):
            px, py, _ = _peer(x_, y_, c_, q, 0)
            return 2 * px + py

        def cols_of(q, cp):
            off, w = (0, W_IN_SHARD) if cp is None else W_IN_PARTS[cp]
            return shard_of(q) * W_IN_SHARD + off, w

        def part(which, q, pc, sub, cp):
            n = hi if which == 0 else ho
            base = pc * n
            if sub is not None:
                n //= 2
                base = base + sub * n
            if which == 0:
                c0, w = cols_of(q, cp)
                return fi_ref.at[pl.ds(base, n), pl.ds(c0, w)]
            return fo_ref.at[pl.ds(shard_of(q) * W_OUT_SHARD + base, n), :]

        def copy(k, ref, to):
            return pltpu.make_async_remote_copy(src_ref=ref, dst_ref=ref, send_sem=send_sems.at[k], recv_sem=recv_sems.at[k],
                                                device_id=to, device_id_type=MESH)

        def sem(which, kind, j, cp):
            return 4 * kind + 2 * cp + j if which == 0 else 16 + 2 * kind + j

        def to_neighbour(which, q, cp=None):
            return copy(sem(which, 0, q - 1, cp), part(which, 0, c_, None, cp), _peer(x_, y_, c_, q, 0))

        def from_neighbour(which, q, cp=None):
            return copy(sem(which, 0, q - 1, cp), part(which, q, c_, None, cp), me)

        def relay(which, q, cp=None):
            return copy(sem(which, 1, q - 1, cp), part(which, q, c_, q - 1, cp), _peer(x_, y_, c_, 3 - q, 0))

        def relayed(which, sub, cp=None):
            return copy(sem(which, 1, sub, cp), part(which, 3, c_, sub, cp), me)

        def to_sibling(which, q, cp=None):
            return copy(sem(which, 2, q - 1, cp), part(which, q, c_, None, cp), sibling)

        def from_sibling(which, q, cp=None):
            return copy(sem(which, 2, q - 1, cp), part(which, q, 1 - c_, None, cp), me)

        def relayed_to_sibling(which, sub, cp=None):
            return copy(sem(which, 3, sub, cp), part(which, 3, c_, sub, cp), sibling)

        def relayed_from_sibling(which, sub, cp=None):
            return copy(sem(which, 3, sub, cp), part(which, 3, 1 - c_, sub, cp), me)

        def pass_on_neighbours(which, cp=None):
            for q in (1, 2):
                from_neighbour(which, q, cp).wait_recv()
                to_sibling(which, q, cp).start()
                relay(which, q, cp).start()

        def pass_on_relayed(which, cp=None):
            for sub in range(2):
                relayed(which, sub, cp).wait_recv()
                relayed_to_sibling(which, sub, cp).start()

        def shard_load(k):
            c0, w = cols_of(*phases[k])
            return pltpu.make_async_copy(fi_ref.at[:, pl.ds(c0, w)], wbuf.at[k % 2, :, 0:w], load_sems.at[k % 2])

        class OutCopies:
            def __init__(self, k, slot, row0):
                c0, w = cols_of(*phases[k])
                strip = tm // OUT_STREAMS
                self.copies = [pltpu.make_async_copy(obuf.at[slot, n * strip:(n + 1) * strip, 0:w],
                                                     proj_ref.at[pl.ds(row0 + n * strip, strip), pl.ds(c0, w)],
                                                     out_sems.at[slot, n]) for n in range(OUT_STREAMS)]

            def start(self):
                for cp in self.copies:
                    cp.start()

            def wait(self):
                for cp in self.copies:
                    cp.wait()

        out_copy = OutCopies

        def drain(k):
            for j in range(min(2, nrow)):
                out_copy(k, (nrow - 1 - j) % 2, 0).wait()

        def arrivals(k):
            q, cp = phases[k]
            if k == 0:
                for cp_ in range(2):
                    for q_ in (1, 2):
                        to_neighbour(0, q_, cp_).start()
            elif q < 3 and k in (1, 3):
                pass_on_neighbours(0, cp)
                if k == 1:
                    for q_ in (1, 2):
                        to_neighbour(1, q_).start()
            elif k == 5:
                for cp_ in range(2):
                    pass_on_relayed(0, cp_)
                pass_on_neighbours(1)
            if q in (1, 2):
                from_sibling(0, q, cp).wait_recv()
            elif q == 3:
                for sub in range(2):
                    relayed_from_sibling(0, sub, cp).wait_recv()

        rows = pl.ds(pl.multiple_of(i * tm, tm), tm)
        slot = i % 2
        for k, (q, cp) in enumerate(phases):
            @pl.when(p == k)
            def _(k=k, q=q, cp=cp):
                @pl.when(i == 0)
                def _():
                    if k == 0:
                        arrivals(0)
                        shard_load(0).start()
                    else:
                        drain(k - 1)
                    shard_load(k).wait()

                if k + 1 < len(phases):
                    @pl.when(i == max(nrow - 2, 0))
                    def _():
                        arrivals(k + 1)
                        shard_load(k + 1).start()

                if k == 0:
                    wx = D // X_STREAMS
                    ssq = sum(jnp.sum(xr[...] * xr[...], axis=-1, keepdims=True) for xr in x_refs)
                    r = lax.rsqrt(ssq * (1.0 / D) + EPS)
                    for n, xr in enumerate(x_refs):
                        cols = slice(n * wx, (n + 1) * wx)
                        hv = ((xr[...] * r * g_ref[:, cols]) * (1.0 + sc_ref[:, cols]) + sh_ref[:, cols]).astype(BF16)
                        h_ref[:, cols] = hv
                        h_all[rows, cols] = hv

                @pl.when(i >= 2)
                def _():
                    out_copy(k, slot, 0).wait()

                w = cols_of(q, cp)[1]
                obuf[slot, :, 0:w] = jnp.dot(h_all[rows, :], wbuf[k % 2, :, 0:w], preferred_element_type=F32)
                out_copy(k, slot, pl.multiple_of(i * tm, tm)).start()

        @pl.when((p == len(phases) - 1) & (i == nrow - 1))
        def _():
            drain(len(phases) - 1)
            pass_on_relayed(1)
            for q in (1, 2):
                from_sibling(1, q).wait_recv()
            for sub in range(2):
                relayed_from_sibling(1, sub).wait_recv()
            for which, cps in ((0, (0, 1)), (1, (None,))):
                for cp in cps:
                    for q in (1, 2):
                        to_neighbour(which, q, cp).wait_send()
                        relay(which, q, cp).wait_send()
                        to_sibling(which, q, cp).wait_send()
                        relayed_to_sibling(which, q - 1, cp).wait_send()

    vec = pl.BlockSpec((1, D), lambda p, i, pos: (0, 0))
    first_phase_rows = lambda p, i, pos: (jnp.where(p == 0, i, nrow - 1), 0)
    anyspec = pl.BlockSpec(memory_space=pl.ANY)
    x_spec = lambda n: pl.BlockSpec((tm, D // X_STREAMS), lambda p, i, pos: (jnp.where(p == 0, i, nrow - 1), n))
    return pl.pallas_call(
        body, name="proj_gather",
        grid_spec=pltpu.PrefetchScalarGridSpec(
            num_scalar_prefetch=1, grid=(len(phases), nrow),
            in_specs=[x_spec(n) for n in range(X_STREAMS)] + [vec, vec, vec, anyspec, anyspec],
            out_specs=[pl.BlockSpec((tm, D), first_phase_rows), anyspec, anyspec, anyspec],
            scratch_shapes=[pltpu.VMEM((s, D), BF16), pltpu.VMEM((2, D, W_IN_SHARD), BF16), pltpu.VMEM((2, tm, W_IN_SHARD), F32),
                            pltpu.SemaphoreType.DMA((24,)), pltpu.SemaphoreType.DMA((24,)), pltpu.SemaphoreType.DMA((2,)),
                            pltpu.SemaphoreType.DMA((2, OUT_STREAMS))]),
        out_shape=[jax.ShapeDtypeStruct((s, D), BF16), jax.ShapeDtypeStruct((s, D_IN), F32),
                   jax.ShapeDtypeStruct((D, D_IN), BF16), jax.ShapeDtypeStruct((D, D), BF16)],
        input_output_aliases={X_STREAMS + 4: 2, X_STREAMS + 5: 3},
        compiler_params=_params("arbitrary", "arbitrary"),
    )(pos, *([x] * X_STREAMS), shift, scale, norm_g, wi_full, wo_full)


def _proj_specs(rev_nb=None):
    if rev_nb is None:
        row = lambda i: i
    else:
        row = lambda i: rev_nb - 1 - i
    wide = lambda col: pl.BlockSpec((BLK, D_A), lambda i: (row(i), col))
    kv = lambda col: pl.BlockSpec((BLK, D_KV), lambda i: (row(i), col))
    half = lambda col: pl.BlockSpec((BLK, 512), lambda i: (row(i), col))
    return [wide(0), wide(1), wide(2), wide(3), kv(OFF_K // D_KV), kv(OFF_V // D_KV), half(OFF_ZB // 512), half(OFF_ZB // 512 + 1)]


def _mix_fwd_call(proj, cos, sin, ln_g, ln_b, w_sp, b_sp_t, sinks):
    s = proj.shape[0]
    nb = s // BLK

    def body(ua_ref, va_ref, za_ref, q_ref, k_ref, v_ref, zb0_ref, zb1_ref, cos_ref, sin_ref, lg_ref, lb_ref,
             w_ref, bt_ref, sinks_ref, y_ref, probs_ref, ost_ref, psink_ref, kdup_ref, vdup_ref, qm_ref, bias_ref):
        i = pl.program_id(0)
        first_half, lo = _lane_masks()
        cos_t = cos_ref[...]
        sin_t = sin_ref[...]

        _, _, vln = _layer_norm_fwd(va_ref[...], lg_ref[...], lb_ref[...])
        tril = _tril()
        for g in range(GROUPS):
            cols = slice(g * BLK, (g + 1) * BLK)
            wg = jnp.where(tril, w_ref[g], 0.0).astype(BF16)
            sg = jnp.dot(wg, vln[:, cols].astype(BF16), preferred_element_type=F32) + bt_ref[:, g:g + 1]
            gate, _ = _silu_parts(za_ref[:, cols])
            y_ref[:, cols] = (ua_ref[:, cols] * sg * gate).astype(BF16)

        @pl.when(i == 0)
        def _():
            kdup_ref[:, 0:BLK, :] = jnp.zeros((N_KV, BLK, LANE), BF16)
            vdup_ref[:, 0:BLK, :] = jnp.zeros((N_KV, BLK, LANE), BF16)
            _band_bias(bias_ref)

        @pl.when(i > 0)
        def _():
            kdup_ref[:, 0:BLK, :] = kdup_ref[:, BLK:2 * BLK, :]
            vdup_ref[:, 0:BLK, :] = vdup_ref[:, BLK:2 * BLK, :]

        for ks in range(2):
            cols = slice(ks * LANE, (ks + 1) * LANE)
            kr = _rope(k_ref[:, cols], cos_t, sin_t, first_half)
            for n, (kd, vd) in enumerate(zip(_dup_kv(kr, lo), _dup_kv(v_ref[:, cols], lo))):
                kdup_ref[2 * ks + n, BLK:2 * BLK, :] = kd
                vdup_ref[2 * ks + n, BLK:2 * BLK, :] = vd
        for sb in range(8):
            _stack_heads(qm_ref, sb, _rope(q_ref[:, sb * LANE:(sb + 1) * LANE], cos_t, sin_t, first_half) * SCALE, lo, BF16)

        block_kind = jnp.where(i > 0, 1, 0)

        psink_ref[...] = jnp.zeros((Q_PER_KV * BLK, LANE), F32)
        lane_q = lax.broadcasted_iota(jnp.int32, (Q_PER_KV * BLK, LANE), 1)

        def kv_head(kh, carry):
            probs, psink = _softmax_sink(qm_ref[kh], kdup_ref[kh], bias_ref[block_kind], _sink_column(sinks_ref, kh))
            probs_ref[kh] = probs
            psink_ref[...] = jnp.where(lane_q == kh, psink, psink_ref[...])
            ost_ref[kh] = jnp.dot(probs.astype(BF16), vdup_ref[kh], preferred_element_type=F32)
            return carry

        lax.fori_loop(0, N_KV, kv_head, 0, unroll=2)
        for sb in range(8):
            cols = slice(sb * LANE, (sb + 1) * LANE)
            zb = zb0_ref[:, cols] if sb < 4 else zb1_ref[:, (sb - 4) * LANE:(sb - 3) * LANE]
            gate, _ = _silu_parts(zb)
            y_ref[:, D_A + sb * LANE:D_A + (sb + 1) * LANE] = (_unstack_heads(ost_ref, sb, lo) * gate).astype(BF16)

    tab = pl.BlockSpec((BLK, LANE), lambda i: (i, 0))
    return pl.pallas_call(
        body, name="mix_fwd", grid=(nb,),
        in_specs=_proj_specs() + [
            tab, tab, pl.BlockSpec((1, D_A), lambda i: (0, 0)), pl.BlockSpec((1, D_A), lambda i: (0, 0)),
            pl.BlockSpec((GROUPS, BLK, BLK), lambda i: (0, 0, 0)), pl.BlockSpec((BLK, GROUPS), lambda i: (0, 0)),
            pl.BlockSpec(memory_space=pltpu.SMEM)],
        out_specs=[pl.BlockSpec((BLK, 2 * D_A), lambda i: (i, 0)),
                   pl.BlockSpec((None, N_KV, Q_PER_KV * BLK, 2 * BLK), lambda i: (i, 0, 0, 0)),
                   pl.BlockSpec((None, N_KV, Q_PER_KV * BLK, LANE), lambda i: (i, 0, 0, 0)),
                   pl.BlockSpec((None, Q_PER_KV * BLK, LANE), lambda i: (i, 0, 0))],
        out_shape=[jax.ShapeDtypeStruct((s, 2 * D_A), BF16), jax.ShapeDtypeStruct((nb, N_KV, Q_PER_KV * BLK, 2 * BLK), F32),
                   jax.ShapeDtypeStruct((nb, N_KV, Q_PER_KV * BLK, LANE), F32), jax.ShapeDtypeStruct((nb, Q_PER_KV * BLK, LANE), F32)],
        scratch_shapes=[pltpu.VMEM((N_KV, 2 * BLK, LANE), BF16), pltpu.VMEM((N_KV, 2 * BLK, LANE), BF16),
                        pltpu.VMEM((N_KV, Q_PER_KV * BLK, LANE), BF16), pltpu.VMEM((2, Q_PER_KV * BLK, 2 * BLK), F32)],
        compiler_params=_params("arbitrary"),
    )(proj, proj, proj, proj, proj, proj, proj, proj, cos, sin, ln_g, ln_b, w_sp, b_sp_t, sinks)


def _tail_call(y, w_out_bf, x, target, gate, shift_f, scale_f, gf):
    s = x.shape[0]
    tm = min(s, 256)
    nsteps = s // tm

    def body(y_ref, w_ref, x_ref, t_ref, gate_ref, shf_ref, scf_ref, gf_ref, dx2_ref, do_ref, dy_ref, st_ref):
        i = pl.program_id(0)

        @pl.when(i == 0)
        def _():
            st_ref[...] = jnp.zeros((8, D), F32)

        o = jnp.dot(y_ref[...], w_ref[...], preferred_element_type=F32)
        gate_v = gate_ref[...]
        x2 = x_ref[...] + gate_v * o
        r2 = lax.rsqrt(jnp.mean(x2 * x2, axis=-1, keepdims=True) + EPS)
        xn2 = x2 * r2
        hn2 = xn2 * gf_ref[...]
        one_sc = 1.0 + scf_ref[...]
        err = hn2 * one_sc + shf_ref[...] - t_ref[...]
        dout = err * (1.0 / D)
        dhn2 = dout * one_sc
        dxn2 = dhn2 * gf_ref[...]
        dx2 = r2 * (dxn2 - xn2 * jnp.mean(dxn2 * xn2, axis=-1, keepdims=True))
        dx2_ref[...] = dx2
        do = (dx2 * gate_v).astype(BF16)
        do_ref[...] = do
        dy_ref[...] = lax.dot_general(do, w_ref[...], NT, preferred_element_type=F32)
        st_ref[0:1, :] += jnp.sum(dout, axis=0, keepdims=True)
        st_ref[1:2, :] += jnp.sum(dout * hn2, axis=0, keepdims=True)
        st_ref[2:3, :] += jnp.sum(dhn2 * xn2, axis=0, keepdims=True)
        st_ref[3:4, :] += jnp.sum(dx2 * o, axis=0, keepdims=True)
        st_ref[4:5, :] += jnp.sum(err * err, axis=0, keepdims=True)

        @pl.when(i == nsteps - 1)
        def _():
            st_ref[5:6, :] = jnp.full((1, D), 0.5 / D, F32) * jnp.sum(st_ref[4:5, :])

    vec = pl.BlockSpec((1, D), lambda i: (0, 0))
    rows = lambda: pl.BlockSpec((tm, D), lambda i: (i, 0))
    return pl.pallas_call(
        body, name="tail", grid=(nsteps,),
        in_specs=[rows(), pl.BlockSpec((D, D), lambda i: (0, 0)), rows(), rows(), vec, vec, vec, vec],
        out_specs=[rows(), rows(), rows(), pl.BlockSpec((8, D), lambda i: (0, 0))],
        out_shape=[jax.ShapeDtypeStruct((s, D), F32), jax.ShapeDtypeStruct((s, D), BF16), jax.ShapeDtypeStruct((s, D), F32),
                   jax.ShapeDtypeStruct((8, D), F32)],
        compiler_params=_params("arbitrary"),
    )(y, w_out_bf, x, target, gate, shift_f, scale_f, gf)


def _tn_call(a, b, name):
    s, m = a.shape
    n = b.shape[1]
    tn = 1024
    ts = min(s, 1024)
    nk = s // ts

    def body(a_ref, b_ref, o_ref, acc_ref):
        k = pl.program_id(1)

        @pl.when(k == 0)
        def _():
            acc_ref[...] = jnp.zeros((m, tn), F32)

        acc_ref[...] += lax.dot_general(a_ref[...], b_ref[...], TN, preferred_element_type=F32)

        @pl.when(k == nk - 1)
        def _():
            o_ref[...] = acc_ref[...].astype(BF16)

    return pl.pallas_call(
        body, name=name, grid=(n // tn, nk),
        in_specs=[pl.BlockSpec((ts, m), lambda j, k: (k, 0)), pl.BlockSpec((ts, tn), lambda j, k: (k, j))],
        out_specs=pl.BlockSpec((m, tn), lambda j, k: (0, j)),
        out_shape=jax.ShapeDtypeStruct((m, n), BF16),
        scratch_shapes=[pltpu.VMEM((m, tn), F32)],
        compiler_params=_params("parallel", "arbitrary"),
    )(a, b)


def _tn_shards_call(pos, a, b, qs, name):
    s, m = a.shape
    ts = min(s, 1024)
    nk = s // ts

    def body(pos_ref, a_ref, b_ref, o_ref, acc_ref):
        k = pl.program_id(1)

        @pl.when(k == 0)
        def _():
            acc_ref[...] = jnp.zeros((m, W_IN_SHARD), F32)

        acc_ref[...] += lax.dot_general(a_ref[...], b_ref[...], TN, preferred_element_type=F32)

        @pl.when(k == nk - 1)
        def _():
            o_ref[...] = acc_ref[...].astype(BF16)

    def shard(j, pos):
        q = qs[0]
        for n in range(1, len(qs)):
            q = jnp.where(j == n, qs[n], q)
        return jnp.bitwise_xor(pos[0], q)

    return pl.pallas_call(
        body, name=name,
        grid_spec=pltpu.PrefetchScalarGridSpec(
            num_scalar_prefetch=1, grid=(len(qs), nk),
            in_specs=[pl.BlockSpec((ts, m), lambda j, k, pos: (k, 0)),
                      pl.BlockSpec((ts, W_IN_SHARD), lambda j, k, pos: (k, shard(j, pos)))],
            out_specs=pl.BlockSpec((m, W_IN_SHARD), lambda j, k, pos: (0, j)),
            scratch_shapes=[pltpu.VMEM((m, W_IN_SHARD), F32)]),
        out_shape=jax.ShapeDtypeStruct((m, len(qs) * W_IN_SHARD), BF16),
        compiler_params=_params("parallel", "arbitrary"),
    )(pos, a, b)


def _mix_bwd_call(proj, dy, probs, outs, psinks, tables, ln_g, ln_b, w_sp, w_sp_t, b_sp_t):
    s = proj.shape[0]
    nb = s // BLK
    rev = lambda i: nb - 1 - i
    prev = lambda i: jnp.maximum(nb - 2 - i, 0)

    def body(ua_ref, va_ref, za_ref, q_ref, k_ref, v_ref, zb0_ref, zb1_ref, kp_ref, vp_ref, dy_ref,
             probs_ref, ost_ref, psink_ref, cos_ref, sin_ref, cosp_ref, sinp_ref, lg_ref, lb_ref, w_ref, wt_ref, bt_ref,
             dp_ref, lnst_ref, dw_ref, dbt_ref, dsink_ref,
             kdup_ref, vdup_ref, dvln_ref, qm_ref, dom_ref, dqst_ref, dkdup_ref, dvdup_ref, kcar_ref, vcar_ref, sigb_ref):
        i = pl.program_id(0)
        first_half, lo = _lane_masks()
        lane8 = lax.broadcasted_iota(jnp.int32, (8, LANE), 1)
        cos_t = cos_ref[...]
        sin_t = sin_ref[...]

        @pl.when(i == 0)
        def _():
            lnst_ref[...] = jnp.zeros((8, D_A), F32)
            dw_ref[...] = jnp.zeros((GROUPS, BLK, BLK), F32)
            dbt_ref[...] = jnp.zeros((BLK, LANE), F32)
            dsink_ref[...] = jnp.zeros((8, LANE), F32)
            kcar_ref[...] = jnp.zeros((BLK, D_KV), F32)
            vcar_ref[...] = jnp.zeros((BLK, D_KV), F32)

        vhat, rstd, vln = _layer_norm_fwd(va_ref[...], lg_ref[...], lb_ref[...])
        tril = _tril()
        triu = jnp.logical_not(tril) | (lax.broadcasted_iota(jnp.int32, (BLK, BLK), 0) == lax.broadcasted_iota(jnp.int32, (BLK, BLK), 1))
        lane_b = lax.broadcasted_iota(jnp.int32, (BLK, LANE), 1)
        db_acc = jnp.zeros((BLK, LANE), F32)
        for g in range(GROUPS):
            cols = slice(g * BLK, (g + 1) * BLK)
            vln_g = vln[:, cols].astype(BF16)
            wg = jnp.where(tril, w_ref[g], 0.0).astype(BF16)
            sg = jnp.dot(wg, vln_g, preferred_element_type=F32) + bt_ref[:, g:g + 1]
            za = za_ref[:, cols]
            gate, sig = _silu_parts(za)
            ua = ua_ref[:, cols]
            dya_g = dy_ref[:, cols]
            dya = dya_g * gate
            dp_ref[:, cols] = (dya * sg).astype(BF16)
            dp_ref[:, 2 * D_A + g * BLK:2 * D_A + (g + 1) * BLK] = (
                dya_g * (ua * sg) * (sig * (1.0 + za * (1.0 - sig)))).astype(BF16)
            ds = dya * ua
            ds_b = ds.astype(BF16)
            wtg = jnp.where(triu, wt_ref[g], 0.0).astype(BF16)
            dvln_ref[:, cols] = jnp.dot(wtg, ds_b, preferred_element_type=F32)
            dw_ref[g] += jnp.where(tril, lax.dot_general(ds_b, vln_g, NT, preferred_element_type=F32), 0.0)
            db_acc = db_acc + jnp.where(lane_b == g, jnp.sum(ds, axis=-1, keepdims=True), 0.0)
        dbt_ref[...] += db_acc
        dvln = dvln_ref[...]
        lnst_ref[0:1, :] += jnp.sum(dvln * vhat, axis=0, keepdims=True)
        lnst_ref[1:2, :] += jnp.sum(dvln, axis=0, keepdims=True)
        dvhat = dvln * lg_ref[...]
        m1 = jnp.mean(dvhat, axis=-1, keepdims=True)
        m2 = jnp.mean(dvhat * vhat, axis=-1, keepdims=True)
        dp_ref[:, D_A:2 * D_A] = (rstd * (dvhat - m1 - vhat * m2)).astype(BF16)

        cosp = cosp_ref[...]
        sinp = sinp_ref[...]
        for ks in range(2):
            cols = slice(ks * LANE, (ks + 1) * LANE)
            kr = _rope(k_ref[:, cols], cos_t, sin_t, first_half)
            kpr = _rope(kp_ref[:, cols], cosp, sinp, first_half)
            for n, (kc, vc, kp, vp) in enumerate(zip(_dup_kv(kr, lo), _dup_kv(v_ref[:, cols], lo),
                                                     _dup_kv(kpr, lo), _dup_kv(vp_ref[:, cols], lo))):
                kdup_ref[2 * ks + n, BLK:2 * BLK, :] = kc
                vdup_ref[2 * ks + n, BLK:2 * BLK, :] = vc
                kdup_ref[2 * ks + n, 0:BLK, :] = kp
                vdup_ref[2 * ks + n, 0:BLK, :] = vp
        for sb in range(8):
            cols = slice(sb * LANE, (sb + 1) * LANE)
            _stack_heads(qm_ref, sb, _rope(q_ref[:, cols], cos_t, sin_t, first_half) * SCALE, lo, BF16)
            zb = zb0_ref[:, cols] if sb < 4 else zb1_ref[:, (sb - 4) * LANE:(sb - 3) * LANE]
            gate, sig = _silu_parts(zb)
            sigb_ref[:, cols] = sig
            _stack_heads(dom_ref, sb, dy_ref[:, D_A + sb * LANE:D_A + (sb + 1) * LANE] * gate, lo, F32)

        lane_q = lax.broadcasted_iota(jnp.int32, (Q_PER_KV * BLK, LANE), 1)

        def kv_head(kh, dsink_acc):
            qm = qm_ref[kh]
            kd = kdup_ref[kh]
            vd = vdup_ref[kh]
            probs = probs_ref[kh]
            psink = jnp.sum(jnp.where(lane_q == kh, psink_ref[...], 0.0), axis=-1, keepdims=True)
            probs_b = probs.astype(BF16)
            o = ost_ref[kh]
            dom = dom_ref[kh]
            dom_b = dom.astype(BF16)
            delta = jnp.sum(dom * o, axis=-1, keepdims=True)
            dpr = lax.dot_general(dom_b, vd, NT, preferred_element_type=F32)
            dss = (probs * (dpr - delta)).astype(BF16)
            sd = psink * delta
            for n in range(Q_PER_KV):
                dsink_acc = dsink_acc + jnp.where(lane8 == Q_PER_KV * kh + n, -jnp.sum(sd[n * BLK:(n + 1) * BLK]), 0.0)
            dqst_ref[kh] = jnp.dot(dss, kd, preferred_element_type=F32)
            dkdup_ref[kh] = lax.dot_general(dss, qm, TN, preferred_element_type=F32)
            dvdup_ref[kh] = lax.dot_general(probs_b, dom_b, TN, preferred_element_type=F32)
            return dsink_acc

        dsink_acc = lax.fori_loop(0, N_KV // 2, lambda j, acc: kv_head(2 * j + 1, kv_head(2 * j, acc)), jnp.zeros((8, LANE), F32))
        row0 = lax.broadcasted_iota(jnp.int32, (8, LANE), 0) == 0
        dsink_ref[...] += jnp.where(row0, dsink_acc, 0.0)

        for sb in range(8):
            cols = slice(sb * LANE, (sb + 1) * LANE)
            zb = zb0_ref[:, cols] if sb < 4 else zb1_ref[:, (sb - 4) * LANE:(sb - 3) * LANE]
            sig = sigb_ref[:, cols]
            dyb = dy_ref[:, D_A + sb * LANE:D_A + (sb + 1) * LANE]
            dp_ref[:, OFF_ZB + sb * LANE:OFF_ZB + (sb + 1) * LANE] = (
                dyb * _unstack_heads(ost_ref, sb, lo) * (sig * (1.0 + zb * (1.0 - sig)))).astype(BF16)
            dq_r = _unstack_heads(dqst_ref, sb, lo) * SCALE
            dp_ref[:, OFF_Q + sb * LANE:OFF_Q + (sb + 1) * LANE] = _unrope(dq_r, cos_t, sin_t, first_half).astype(BF16)

        lo2 = lax.broadcasted_iota(jnp.int32, (2 * BLK, LANE), 1) < HEAD
        for ks in range(2):
            cols = slice(ks * LANE, (ks + 1) * LANE)
            ka = dkdup_ref[2 * ks]
            kb = dkdup_ref[2 * ks + 1]
            dk_band = jnp.where(lo2, ka + pltpu.roll(ka, HEAD, 1), kb + pltpu.roll(kb, HEAD, 1))
            va_ = dvdup_ref[2 * ks]
            vb_ = dvdup_ref[2 * ks + 1]
            dv_band = jnp.where(lo2, va_ + pltpu.roll(va_, HEAD, 1), vb_ + pltpu.roll(vb_, HEAD, 1))
            dkr = dk_band[BLK:2 * BLK, :] + kcar_ref[:, cols]
            dp_ref[:, OFF_K + ks * LANE:OFF_K + (ks + 1) * LANE] = _unrope(dkr, cos_t, sin_t, first_half).astype(BF16)
            dp_ref[:, OFF_V + ks * LANE:OFF_V + (ks + 1) * LANE] = (
                dv_band[BLK:2 * BLK, :] + vcar_ref[:, cols]).astype(BF16)
            kcar_ref[:, cols] = dk_band[0:BLK, :]
            vcar_ref[:, cols] = dv_band[0:BLK, :]

    tab = pl.BlockSpec((BLK, LANE), lambda i: (rev(i), 0))
    kvp = lambda col: pl.BlockSpec((BLK, D_KV), lambda i: (prev(i), col))
    vec = pl.BlockSpec((1, D_A), lambda i: (0, 0))
    w3 = pl.BlockSpec((GROUPS, BLK, BLK), lambda i: (0, 0, 0))
    return pl.pallas_call(
        body, name="mix_bwd", grid=(nb,),
        in_specs=_proj_specs(nb) + [
            kvp(OFF_K // D_KV), kvp(OFF_V // D_KV), pl.BlockSpec((BLK, 2 * D_A), lambda i: (rev(i), 0)),
            pl.BlockSpec((None, N_KV, Q_PER_KV * BLK, 2 * BLK), lambda i: (rev(i), 0, 0, 0)),
            pl.BlockSpec((None, N_KV, Q_PER_KV * BLK, LANE), lambda i: (rev(i), 0, 0, 0)),
            pl.BlockSpec((None, Q_PER_KV * BLK, LANE), lambda i: (rev(i), 0, 0)),
            tab, tab, tab, tab, vec, vec, w3, w3, pl.BlockSpec((BLK, GROUPS), lambda i: (0, 0))],
        out_specs=[pl.BlockSpec((BLK, D_IN), lambda i: (rev(i), 0)), pl.BlockSpec((8, D_A), lambda i: (0, 0)), w3,
                   pl.BlockSpec((BLK, LANE), lambda i: (0, 0)), pl.BlockSpec((8, LANE), lambda i: (0, 0))],
        out_shape=[jax.ShapeDtypeStruct((s, D_IN), BF16), jax.ShapeDtypeStruct((8, D_A), F32),
                   jax.ShapeDtypeStruct((GROUPS, BLK, BLK), F32), jax.ShapeDtypeStruct((BLK, LANE), F32),
                   jax.ShapeDtypeStruct((8, LANE), F32)],
        scratch_shapes=[pltpu.VMEM((N_KV, 2 * BLK, LANE), BF16), pltpu.VMEM((N_KV, 2 * BLK, LANE), BF16),
                        pltpu.VMEM((BLK, D_A), F32), pltpu.VMEM((N_KV, Q_PER_KV * BLK, LANE), BF16),
                        pltpu.VMEM((N_KV, Q_PER_KV * BLK, LANE), F32), pltpu.VMEM((N_KV, Q_PER_KV * BLK, LANE), F32),
                        pltpu.VMEM((N_KV, 2 * BLK, LANE), F32), pltpu.VMEM((N_KV, 2 * BLK, LANE), F32),
                        pltpu.VMEM((BLK, D_KV), F32), pltpu.VMEM((BLK, D_KV), F32), pltpu.VMEM((BLK, D_B), F32)],
        compiler_params=_params("arbitrary"),
    )(proj, proj, proj, proj, proj, proj, proj, proj, proj, proj, dy, probs, outs, psinks, *tables, ln_g, ln_b,
      w_sp, w_sp_t, b_sp_t)


def _dh_call(dproj, w_bf, x, dx2, scale, norm_g):
    s = x.shape[0]
    tm = min(s, 512)
    tk = W_IN_SHARD
    nk = D_IN // tk

    def body(dp_ref, w_ref, x_ref, dx2_ref, sc_ref, g_ref, gx_ref, st_ref, acc_ref):
        i = pl.program_id(0)
        k = pl.program_id(1)

        @pl.when((i == 0) & (k == 0))
        def _():
            st_ref[...] = jnp.zeros((8, D), F32)

        @pl.when(k == 0)
        def _():
            acc_ref[...] = jnp.zeros((tm, D), F32)

        acc_ref[...] += lax.dot_general(dp_ref[...], w_ref[...], NT, preferred_element_type=F32)

        @pl.when(k == nk - 1)
        def _():
            g = g_ref[...]
            one_sc = 1.0 + sc_ref[...]

            def chunk(n, carry):
                rows = pl.ds(pl.multiple_of(n * BLK, BLK), BLK)
                dh = acc_ref[rows, :]
                xv = x_ref[rows, :]
                r = lax.rsqrt(jnp.mean(xv * xv, axis=-1, keepdims=True) + EPS)
                xn = xv * r
                dhn = dh * one_sc
                dxn = dhn * g
                gx_ref[rows, :] = dx2_ref[rows, :] + r * (dxn - xn * jnp.mean(dxn * xn, axis=-1, keepdims=True))
                st_ref[0:1, :] += jnp.sum(dh, axis=0, keepdims=True)
                st_ref[1:2, :] += jnp.sum(dh * (xn * g), axis=0, keepdims=True)
                st_ref[2:3, :] += jnp.sum(dhn * xn, axis=0, keepdims=True)
                return carry

            lax.fori_loop(0, tm // BLK, chunk, 0)

    vec = pl.BlockSpec((1, D), lambda i, k: (0, 0))
    rows = lambda: pl.BlockSpec((tm, D), lambda i, k: (i, 0))
    return pl.pallas_call(
        body, name="dh", grid=(s // tm, nk),
        in_specs=[pl.BlockSpec((tm, tk), lambda i, k: (i, k)), pl.BlockSpec((D, tk), lambda i, k: (0, k)), rows(), rows(), vec, vec],
        out_specs=[rows(), pl.BlockSpec((8, D), lambda i, k: (0, 0))],
        out_shape=[jax.ShapeDtypeStruct((s, D), F32), jax.ShapeDtypeStruct((8, D), F32)],
        scratch_shapes=[pltpu.VMEM((tm, D), F32)],
        compiler_params=_params("arbitrary", "arbitrary"),
    )(dproj, w_bf, x, dx2, scale, norm_g)


def _adam_math(w, g, m, v):
    m_new = ADAM_B1 * m + (1.0 - ADAM_B1) * g
    v_new = ADAM_B2 * v + (1.0 - ADAM_B2) * (g * g)
    m_hat = m_new / ADAM_C1
    v_hat = v_new / ADAM_C2
    delta = -ADAM_LR * (m_hat / (jnp.sqrt(v_hat) + ADAM_EPS) + ADAM_WD * w)
    return delta, m_new, v_new


def _adam_small_call(tensors):
    n = len(tensors)

    def body(*refs):
        ins, outs = refs[:4 * n], refs[4 * n:]
        for t in range(n):
            w_ref, g_ref, m_ref, v_ref = ins[4 * t:4 * t + 4]
            d, mo, vo = _adam_math(w_ref[...], g_ref[...], m_ref[...], v_ref[...])
            outs[3 * t][...], outs[3 * t + 1][...], outs[3 * t + 2][...] = d, mo, vo

    vm = pl.BlockSpec(memory_space=pltpu.VMEM)
    flat = [a for t in tensors for a in t]
    out = pl.pallas_call(
        body, name="adam_small", in_specs=[vm] * (4 * n), out_specs=[vm] * (3 * n),
        out_shape=[jax.ShapeDtypeStruct(t[0].shape, F32) for t in tensors for _ in range(3)],
        compiler_params=pltpu.CompilerParams(vmem_limit_bytes=VMEM_LIMIT),
    )(*flat)
    return [tuple(out[3 * t:3 * t + 3]) for t in range(n)]


def _adam_halves_call(pos, w, mine, theirs, m, v, name):
    r, n = w.shape
    half = r // 2
    tr = ADAM_ROWS
    nh = half // tr

    def body(pos_ref, w_ref, mine_ref, theirs_ref, m_ref, v_ref, g_ref, d_ref, mo_ref, vo_ref):
        is_mine = (pl.program_id(0) // nh) == pos_ref[1]
        g = jnp.where(is_mine, mine_ref[...], theirs_ref[...])
        g_ref[...] = g
        d_ref[...], mo_ref[...], vo_ref[...] = _adam_math(w_ref[...], g, m_ref[...], v_ref[...])

    spec = lambda: pl.BlockSpec((tr, n), lambda i, pos: (i, 0))

    def half_spec(core_of_half):
        def index(i, pos):
            first = core_of_half(pos) == 0
            active = (i // nh == 0) == first
            return jnp.where(active, i % nh, jnp.where(first, nh - 1, 0)), 0
        return pl.BlockSpec((tr, n), index)

    return pl.pallas_call(
        body, name=name,
        grid_spec=pltpu.PrefetchScalarGridSpec(
            num_scalar_prefetch=1, grid=(r // tr,),
            in_specs=[spec(), half_spec(lambda pos: pos[1]), half_spec(lambda pos: 1 - pos[1]), spec(), spec()],
            out_specs=[spec() for _ in range(4)]),
        out_shape=[jax.ShapeDtypeStruct((r, n), F32)] * 4, compiler_params=_params("arbitrary"),
    )(pos, w, mine, theirs, m, v)


def _adam_outer_call(w, ct, dm, m, v, name):
    r, n = w.shape
    tr = ADAM_ROWS

    def body(w_ref, ct_ref, dm_ref, m_ref, v_ref, g_ref, d_ref, mo_ref, vo_ref):
        g = ct_ref[:, 0:1] * dm_ref[0:1, :]
        for b in range(1, N_DEV):
            g = g + ct_ref[:, b:b + 1] * dm_ref[b:b + 1, :]
        g_ref[...] = g
        d_ref[...], mo_ref[...], vo_ref[...] = _adam_math(w_ref[...], g, m_ref[...], v_ref[...])

    spec = lambda: pl.BlockSpec((tr, n), lambda i: (i, 0))
    return pl.pallas_call(
        body, name=name, grid=(r // tr,),
        in_specs=[spec(), pl.BlockSpec((tr, N_DEV), lambda i: (i, 0)), pl.BlockSpec((N_DEV, n), lambda i: (0, 0)), spec(), spec()],
        out_specs=[spec() for _ in range(4)],
        out_shape=[jax.ShapeDtypeStruct((r, n), F32)] * 4, compiler_params=_params("parallel"),
    )(w, ct, dm, m, v)


def _sum_pieces_call(pos, part, part_block, recvs, name):
    r, n = recvs[0].shape[1:]
    tr = min(r, 256)
    nrb = r // tr

    def body(pos_ref, p_ref, *refs):
        acc = p_ref[...].astype(F32)
        for r_ref in refs[:-1]:
            for d in range(r_ref.shape[0]):
                acc = acc + r_ref[d].astype(F32)
        refs[-1][...] = acc

    return pl.pallas_call(
        body, name=name,
        grid_spec=pltpu.PrefetchScalarGridSpec(
            num_scalar_prefetch=1, grid=(nrb,),
            in_specs=[pl.BlockSpec((tr, n), lambda i, pos: part_block(i, pos, nrb))] + [
                pl.BlockSpec((rv.shape[0], tr, n), lambda i, pos: (0, i, 0)) for rv in recvs],
            out_specs=pl.BlockSpec((tr, n), lambda i, pos: (i, 0))),
        out_shape=jax.ShapeDtypeStruct((r, n), F32), compiler_params=_params("parallel"),
    )(pos, part, *recvs)


def _coords():
    return lax.axis_index("x"), lax.axis_index("y"), lax.axis_index("c")


CAST_ROWS = 256


def _allgather_sum_call(blk, name, with_sum, cast=None):
    m_per, n = blk.shape
    n_out = 2 if with_sum else 1
    if cast is not None:
        w, full_shape = cast
        wr, wn = w.shape
        by_cols = full_shape[0] == wr
        tr = min(wr, CAST_ROWS)
        n_chunk = wr // tr

    def body(*refs):
        x_ref = refs[0]
        out_ref = refs[1 + (cast is not None)]
        rest = refs[1 + (cast is not None) + n_out + (cast is not None):]
        send_sems, recv_sems, local_sem = rest[:3]
        x, y, c = _coords()
        me, sibling = (x, y, c), (x, y, 1 - c)
        chips = [(1 - x, y), (x, 1 - y), (1 - x, 1 - y)]

        def rows(px, py, pc):
            return out_ref.at[pl.ds((4 * px + 2 * py + pc) * m_per, m_per), :]

        def copy(k, block, to, src=None):
            return pltpu.make_async_remote_copy(
                src_ref=rows(*block) if src is None else src, dst_ref=rows(*block),
                send_sem=send_sems.at[k], recv_sem=recv_sems.at[k], device_id=to, device_id_type=MESH)

        mine = pltpu.make_async_copy(x_ref, rows(*me), local_sem)
        mine.start()
        first = [copy(0, me, sibling, src=x_ref)]
        first += [copy(1 + j, me, (*chip, c), src=x_ref) for j, chip in enumerate(chips)]
        for cp in first:
            cp.start()

        if cast is not None:
            w_ref, full_ref = refs[1], refs[1 + 1 + n_out]
            f32_buf, bf16_buf, in_sems, out_sems = rest[3:]
            chip_no = 2 * x + y

            def fetch(i):
                return pltpu.make_async_copy(w_ref.at[pl.ds(i * tr, tr), :], f32_buf.at[i % 2], in_sems.at[i % 2])

            def store(i):
                if by_cols:
                    dst = full_ref.at[pl.ds(i * tr, tr), pl.ds(chip_no * wn, wn)]
                else:
                    dst = full_ref.at[pl.ds(chip_no * wr + i * tr, tr), :]
                return pltpu.make_async_copy(bf16_buf.at[i % 2], dst, out_sems.at[i % 2])

            fetch(0).start()
            for i in range(n_chunk):
                if i + 1 < n_chunk:
                    fetch(i + 1).start()
                fetch(i).wait()
                if i >= 2:
                    store(i - 2).wait()
                bf16_buf[i % 2] = f32_buf[i % 2].astype(BF16)
                store(i).start()
            for i in range(max(n_chunk - 2, 0), n_chunk):
                store(i).wait()

        passed = [copy(4 + j, (*chip, c), sibling) for j, chip in enumerate(chips)]
        for j, chip in enumerate(chips):
            copy(1 + j, (*chip, c), me).wait_recv()
            passed[j].start()
        copy(0, sibling, me).wait_recv()
        for j, chip in enumerate(chips):
            copy(4 + j, (*chip, 1 - c), me).wait_recv()
        for cp in first + passed:
            cp.wait_send()
        mine.wait()
        if with_sum:
            sum_ref = refs[1 + (cast is not None) + 1]
            acc = out_ref[0:m_per, :]
            for d in range(1, N_DEV):
                acc = acc + out_ref[d * m_per:(d + 1) * m_per, :]
            sum_ref[...] = acc

    vm = pl.BlockSpec(memory_space=pltpu.VMEM)
    anyspec = pl.BlockSpec(memory_space=pl.ANY)
    out_shape = [jax.ShapeDtypeStruct((N_DEV * m_per, n), F32)]
    if with_sum:
        out_shape.append(jax.ShapeDtypeStruct((m_per, n), F32))
    in_specs, out_specs, operands = [vm], [vm] * n_out, [blk]
    scratch = [pltpu.SemaphoreType.DMA((7,)), pltpu.SemaphoreType.DMA((7,)), pltpu.SemaphoreType.DMA]
    if cast is not None:
        in_specs.append(anyspec)
        operands.append(w)
        out_shape.append(jax.ShapeDtypeStruct(full_shape, BF16))
        out_specs.append(anyspec)
        scratch += [pltpu.VMEM((2, tr, wn), F32), pltpu.VMEM((2, tr, wn), BF16), pltpu.SemaphoreType.DMA((2,)),
                    pltpu.SemaphoreType.DMA((2,))]
    return pl.pallas_call(
        body, name=name, out_shape=out_shape, in_specs=in_specs, out_specs=out_specs, scratch_shapes=scratch,
        compiler_params=pltpu.CompilerParams(vmem_limit_bytes=VMEM_LIMIT),
    )(*operands)


HBM_SPEC = pl.BlockSpec(memory_space=pltpu.HBM)
SEM_SPEC = pl.BlockSpec(memory_space=pltpu.SEMAPHORE)
SIDE_EFFECT = pltpu.SideEffectType.DATAFLOW_SIDE_EFFECTING


def _peer(x, y, c, q, cb):
    return (1 - x if q & 2 else x, 1 - y if q & 1 else y, 1 - c if cb else c)


def _w_in_piece(slots):
    def piece(part_ref, k, to):
        return part_ref.at[pl.ds(to[2] * (D // 2), D // 2), pl.ds(slots[k] * W_IN_SHARD, W_IN_SHARD)]
    return piece


def _w_out_piece(part_ref, k, to):
    ho = W_OUT_SHARD // 2
    return part_ref.at[pl.ds((2 * to[0] + to[1]) * W_OUT_SHARD + to[2] * ho, ho), :]


def _group_piece(part_ref, k, to):
    return part_ref.at[4 * to[0] + 2 * to[1] + to[2]]


def _whole_piece(part_ref, k, to):
    return part_ref


def _exchange_start_call(groups, name):
    ng = len(groups)
    lands = [lax.empty((len(rels),) + slot_shape, part.dtype) for part, rels, _, slot_shape in groups]

    def body(*refs):
        ins, outs = refs[:2 * ng], refs[2 * ng:]
        x, y, c = _coords()
        for g, (_, rels, piece, _) in enumerate(groups):
            part_ref, land_ref = ins[2 * g], ins[2 * g + 1]
            send_sems, recv_sems = outs[4 * g], outs[4 * g + 1]
            for k, (q, cb) in enumerate(rels):
                to = _peer(x, y, c, q, cb)
                pltpu.make_async_remote_copy(src_ref=piece(part_ref, k, to), dst_ref=land_ref.at[k], send_sem=send_sems.at[k],
                                             recv_sem=recv_sems.at[k], device_id=to, device_id_type=MESH).start()
        outs[-1][...] = jnp.zeros_like(outs[-1])

    out_shape, out_specs, operands = [], [], []
    for (part, rels, _, _), land in zip(groups, lands):
        n = len(rels)
        out_shape += [pltpu.SemaphoreType.DMA((n,)), pltpu.SemaphoreType.DMA((n,)), pltpu.HBM(part.shape, part.dtype),
                      pltpu.HBM(land.shape, land.dtype)]
        out_specs += [SEM_SPEC, SEM_SPEC, HBM_SPEC, HBM_SPEC]
        operands += [pltpu.with_memory_space_constraint(part, pltpu.HBM), pltpu.with_memory_space_constraint(land, pltpu.HBM)]
    out = pl.pallas_call(
        body, name=name,
        out_shape=tuple(out_shape) + (jax.ShapeDtypeStruct((1, 1), F32),),
        in_specs=(HBM_SPEC,) * (2 * ng), out_specs=tuple(out_specs) + (pl.BlockSpec(memory_space=pltpu.VMEM),),
        input_output_aliases={j: 4 * (j // 2) + 2 + j % 2 for j in range(2 * ng)},
        compiler_params=pltpu.CompilerParams(has_side_effects=SIDE_EFFECT),
    )(*operands)
    return [tuple(out[4 * g:4 * g + 4]) for g in range(ng)], out[-1]


def _exchange_wait_call(started, groups, after, name):
    ng = len(groups)

    def body(*refs):
        ins = refs[:4 * ng]
        x, y, c = _coords()
        for g, (_, rels, piece, _) in enumerate(groups):
            part_ref, land_ref, send_sems, recv_sems = ins[4 * g:4 * g + 4]
            for k, (q, cb) in enumerate(rels):
                to = _peer(x, y, c, q, cb)
                cp = pltpu.make_async_remote_copy(src_ref=piece(part_ref, k, to), dst_ref=land_ref.at[k], send_sem=send_sems.at[k],
                                                  recv_sem=recv_sems.at[k], device_id=to, device_id_type=MESH)
                cp.wait_send()
                cp.wait_recv()

    operands, in_specs, out_shape = [], [], []
    for send_sems, recv_sems, part_thru, land_thru in started:
        operands += [part_thru, land_thru, send_sems, recv_sems]
        in_specs += [HBM_SPEC, HBM_SPEC, SEM_SPEC, SEM_SPEC]
        out_shape += [pltpu.HBM(part_thru.shape, part_thru.dtype), pltpu.HBM(land_thru.shape, land_thru.dtype)]
    out = pl.pallas_call(
        body, name=name, out_shape=tuple(out_shape),
        in_specs=tuple(in_specs) + (pl.BlockSpec(memory_space=pl.ANY),), out_specs=(HBM_SPEC,) * (2 * ng),
        input_output_aliases={4 * g + j: 2 * g + j for g in range(ng) for j in range(2)},
        compiler_params=pltpu.CompilerParams(has_side_effects=SIDE_EFFECT),
    )(*operands, after)
    return [tuple(out[2 * g:2 * g + 2]) for g in range(ng)]


def _rope_tables(s):
    inv_freq = np.float32(10000.0) ** (-np.arange(0, HEAD, 2, dtype=np.float32) / np.float32(HEAD))
    ang = np.arange(s, dtype=np.float32)[:, None] * inv_freq[None, :]
    cos = np.tile(np.cos(ang), (1, LANE // (HEAD // 2))).astype(np.float32)
    sin = np.tile(np.sin(ang), (1, LANE // (HEAD // 2))).astype(np.float32)
    first_half = (np.arange(LANE) % HEAD) < (HEAD // 2)
    sin = np.where(first_half[None, :], -sin, sin)
    behind = lambda t: np.concatenate([t[:BLK], t[:-BLK]], axis=0)
    return tuple(jnp.asarray(t) for t in (cos, sin, behind(cos), behind(sin)))


def kernel(x, c, w_ada, b_ada, norm_g, w_in, ln_v_g, ln_v_b, w_spatial, b_spatial, sinks, w_out, w_ada_final, b_ada_final, final_norm_g, loss_target, m_w_ada, m_b_ada, m_norm_g, m_w_in, m_ln_v_g, m_ln_v_b, m_w_spatial, m_b_spatial, m_sinks, m_w_out, m_w_ada_final, m_b_ada_final, m_final_norm_g, v_w_ada, v_b_ada, v_norm_g, v_w_in, v_ln_v_g, v_ln_v_b, v_w_spatial, v_b_spatial, v_sinks, v_w_out, v_w_ada_final, v_b_ada_final, v_final_norm_g):
    s = x.shape[1]
    ax, ay, ac = _coords()
    chip = 2 * ax + ay
    me = 4 * ax + 2 * ay + ac
    n_ada = w_ada.shape[2]
    n_adaf = w_ada_final.shape[1]

    x2d = x.reshape(s, D)
    tgt = loss_target.reshape(s, D)
    w_ada2, w_in2, w_out2 = w_ada[0], w_in[0], w_out[0]
    b_ada_f2 = b_ada_final.reshape(1, 2 * D)
    gf = final_norm_g.reshape(1, D)

    c_all, w_in_own = _allgather_sum_call(jnp.pad(c, ((0, 7), (0, 0))), "gather_c", False, cast=(w_in2, (D, D_IN)))
    c_all = c_all[::8]
    mod_p, c_act = _rowmat_call(c_all, w_ada2, lax.dynamic_slice(b_ada, (0, chip * n_ada), (1, n_ada)), "mod")
    modf_p, _ = _rowmat_call(c_all, w_ada_final, lax.dynamic_slice(b_ada_f2, (0, chip * n_adaf), (1, n_adaf)), "mod_final")
    mods, w_out_own = _allgather_sum_call(jnp.concatenate([mod_p, modf_p], axis=1), "gather_mod", False, cast=(w_out2, (D, D)))
    my_rows = [lax.dynamic_slice(mods, (16 * j + me, 0), (1, n_ada + n_adaf)) for j in range(N_CHIP)]
    mod = jnp.concatenate([r[:, :n_ada] for r in my_rows], axis=1)
    mod_f = jnp.concatenate([r[:, n_ada:] for r in my_rows], axis=1)
    shift, scale, gate = mod[:, :D], mod[:, D:2 * D], mod[:, 2 * D:]
    shift_f, scale_f = mod_f[:, :D], mod_f[:, D:]

    pos = jnp.stack([chip, ac]).astype(jnp.int32)

    tables = _rope_tables(s)
    cos, sin = tables[:2]
    b_sp_t = b_spatial[0].T
    sinks1 = sinks.reshape(N_Q)
    h, proj, w_in_bf, w_out_bf = _proj_gather_call(pos, x2d, shift, scale, norm_g, w_in_own, w_out_own)
    y, probs, attn_out, psinks = _mix_fwd_call(proj, cos, sin, ln_v_g, ln_v_b, w_spatial[0], b_sp_t, sinks1)
    dx2, do, dy, st_tail = _tail_call(y, w_out_bf, x2d, tgt, gate, shift_f, scale_f, gf)

    rel_o = [(0, 1), (1, 0), (1, 1), (2, 0), (2, 1), (3, 0), (3, 1)]
    rel_a = [(1, 0), (1, 1), (2, 0), (2, 1)]
    rel_b = [(3, 0), (3, 1), (0, 1)]
    piece_a, piece_b = _w_in_piece([0, 0, 1, 1]), _w_in_piece([0, 0, 1])
    half_in, half_out = (D // 2, W_IN_SHARD), (W_OUT_SHARD // 2, D)

    g_w_out_p = _tn_call(y, do, "grad_w_out")
    grp_o = [(g_w_out_p, rel_o, _w_out_piece, half_out)]
    st_o, tok_o = _exchange_start_call(grp_o, "send_w_out")
    dproj, st_ln, d_wsp, d_bsp_t, d_sink = _mix_bwd_call(
        proj, dy, probs, attn_out, psinks, tables, ln_v_g + tok_o, ln_v_b, w_spatial[0], jnp.swapaxes(w_spatial[0], 1, 2),
        b_sp_t)
    g_w_in_a = _tn_shards_call(pos, h, dproj, (1, 2), "grad_w_in_a")
    grp_a = [(g_w_in_a, rel_a, piece_a, half_in), (d_wsp, rel_o, _group_piece, (BLK, BLK))]
    st_a, tok_a = _exchange_start_call(grp_a, "send_w_in_a")
    g_w_in_b = _tn_shards_call(pos, h, dproj, (3, 0), "grad_w_in_b")
    grp_b = [(g_w_in_b, rel_b, piece_b, half_in)]
    st_b, tok_b = _exchange_start_call(grp_b, "send_w_in_b")
    grad_x, st_dh = _dh_call(dproj, w_in_bf, x2d, dx2, scale + (tok_a + tok_b), norm_g)

    ((g_w_out_p, recv_o),) = _exchange_wait_call(st_o, grp_o, st_dh, "wait_w_out")
    (_, recv_a), (d_wsp, recv_s) = _exchange_wait_call(st_a, grp_a, st_dh, "wait_w_in_a")
    ((g_w_in_b, recv_b),) = _exchange_wait_call(st_b, grp_b, st_dh, "wait_w_in_b")
    mine_in = _sum_pieces_call(pos, g_w_in_b, lambda i, p, nrb: (p[1] * nrb + i, 1), [recv_a, recv_b], "sum_w_in")
    mine_out = _sum_pieces_call(pos, g_w_out_p, lambda i, p, nrb: ((2 * p[0] + p[1]) * nrb + i, 0), [recv_o], "sum_w_out")
    wsp_group = _sum_pieces_call(pos, d_wsp.reshape(GROUPS * BLK, BLK), lambda i, p, nrb: (2 * p[0] + p[1], 0), [recv_s],
                                 "sum_w_spatial")
    to_sibling = [(0, 1)]
    grp_p = [(mine_in, to_sibling, _whole_piece, half_in), (mine_out, to_sibling, _whole_piece, half_out)]
    st_p, tok_p = _exchange_start_call(grp_p, "swap_halves")

    misc = jnp.concatenate([st_ln, d_bsp_t[:, :GROUPS].T, d_sink, jnp.zeros((8, D - D_A - 2 * LANE), F32)], axis=1)
    pack = jnp.concatenate([wsp_group.reshape(8, D) + tok_p, st_tail, st_dh, misc], axis=0)
    rows = pack.shape[0]
    packs, tot = _allgather_sum_call(pack, "gather_small", True)
    packs = packs.reshape(N_DEV, rows, D)
    dmod_all = jnp.concatenate([packs[:, 16, :], packs[:, 17, :], packs[:, 11, :]], axis=1)
    dmodf_all = jnp.concatenate([packs[:, 8, :], packs[:, 9, :]], axis=1)
    loss = tot[13, 0]
    (mine_in, theirs_in), (mine_out, theirs_out) = _exchange_wait_call(st_p, grp_p, tot, "swapped_halves")
    small = {
        "b_ada": jnp.concatenate([tot[16:17], tot[17:18], tot[11:12]], axis=1),
        "norm_g": tot[18:19],
        "ln_v_g": tot[24:25, :D_A],
        "ln_v_b": tot[25:26, :D_A],
        "w_spatial": packs[:, 0:8, :].reshape(GROUPS * BLK, BLK),
        "b_spatial": tot[24:32, D_A:D_A + BLK],
        "sinks": tot[24:25, D_A + LANE:D_A + LANE + N_Q],
        "b_ada_final": jnp.concatenate([tot[8:9], tot[9:10]], axis=1),
        "final_norm_g": tot[10:11],
    }

    weights = dict(w_ada=w_ada, b_ada=b_ada, norm_g=norm_g, w_in=w_in, ln_v_g=ln_v_g, ln_v_b=ln_v_b, w_spatial=w_spatial,
                   b_spatial=b_spatial, sinks=sinks, w_out=w_out, w_ada_final=w_ada_final, b_ada_final=b_ada_final,
                   final_norm_g=final_norm_g)
    m_in = dict(w_ada=m_w_ada, b_ada=m_b_ada, norm_g=m_norm_g, w_in=m_w_in, ln_v_g=m_ln_v_g, ln_v_b=m_ln_v_b,
                w_spatial=m_w_spatial, b_spatial=m_b_spatial, sinks=m_sinks, w_out=m_w_out, w_ada_final=m_w_ada_final,
                b_ada_final=m_b_ada_final, final_norm_g=m_final_norm_g)
    v_in = dict(w_ada=v_w_ada, b_ada=v_b_ada, norm_g=v_norm_g, w_in=v_w_in, ln_v_g=v_ln_v_g, ln_v_b=v_ln_v_b,
                w_spatial=v_w_spatial, b_spatial=v_b_spatial, sinks=v_sinks, w_out=v_w_out, w_ada_final=v_w_ada_final,
                b_ada_final=v_b_ada_final, final_norm_g=v_final_norm_g)
    c_act_t = c_act.T
    outer = {"w_ada": lax.dynamic_slice(dmod_all, (0, chip * n_ada), (N_DEV, n_ada)),
             "w_ada_final": lax.dynamic_slice(dmodf_all, (0, chip * n_adaf), (N_DEV, n_adaf))}
    halves = {"w_in": (mine_in, theirs_in[0]), "w_out": (mine_out, theirs_out[0])}
    done = {}
    for name, (mine, theirs) in halves.items():
        shape2 = (2 * mine.shape[0], mine.shape[1])
        done[name] = _adam_halves_call(pos, weights[name].reshape(shape2), mine, theirs, m_in[name].reshape(shape2),
                                       v_in[name].reshape(shape2), "adam_" + name)
    for name, dm in outer.items():
        shape2 = (D, dm.shape[1])
        done[name] = _adam_outer_call(weights[name].reshape(shape2), c_act_t, dm, m_in[name].reshape(shape2),
                                      v_in[name].reshape(shape2), "adam_" + name)
    updates = _adam_small_call([(weights[name].reshape(g.shape), g, m_in[name].reshape(g.shape), v_in[name].reshape(g.shape))
                                for name, g in small.items()])
    for (name, g), upd in zip(small.items(), updates):
        done[name] = (g, *upd)
    outs = [[done[name][k].reshape(w.shape) for name, w in weights.items()] for k in range(4)]
    return (loss, grad_x.reshape(x.shape), *outs[0], *outs[1], *outs[2], *outs[3])
```

```python
import numpy as np
import jax
import jax.numpy as jnp
from jax import lax
from jax.experimental import pallas as pl
from jax.experimental.pallas import tpu as pltpu

F32 = jnp.float32
BF16 = jnp.bfloat16
MESH = pl.DeviceIdType.MESH

D = 2048
D_A = 1024
D_B = 1024
D_KV = 256
HEAD = 64
N_Q = 16
N_KV = 4
Q_PER_KV = N_Q // N_KV
BLK = 128
GROUPS = 8
D_IN = 5632
OFF_Q, OFF_K, OFF_V, OFF_ZB = 3072, 4096, 4352, 4608
N_CHIP = 4
N_DEV = 8
W_IN_SHARD = D_IN // N_CHIP
W_OUT_SHARD = D // N_CHIP
EPS = 1e-5
SCALE = HEAD ** -0.5
NEG = -1e30
LANE = 128
VMEM_LIMIT = 56 * 1024 * 1024

ADAM_LR, ADAM_B1, ADAM_B2, ADAM_EPS, ADAM_WD, ADAM_STEP = 0.001, 0.9, 0.999, 1e-08, 0.01, 10
ADAM_C1 = 1.0 - ADAM_B1 ** ADAM_STEP
ADAM_C2 = 1.0 - ADAM_B2 ** ADAM_STEP
ADAM_ROWS = 256

NT = (((1,), (1,)), ((), ()))
TN = (((0,), (0,)), ((), ()))


def _params(*sem):
    return pltpu.CompilerParams(dimension_semantics=sem, vmem_limit_bytes=VMEM_LIMIT)


def _silu_parts(z):
    sig = 1.0 / (1.0 + jnp.exp(-z))
    return z * sig, sig


def _swap_halves(v, first_half):
    return jnp.where(first_half, pltpu.roll(v, 96, 1), pltpu.roll(v, 32, 1))


def _rope(v, cos_t, sin_s, first_half):
    return v * cos_t + _swap_halves(v, first_half) * sin_s


def _unrope(dv, cos_t, sin_s, first_half):
    return dv * cos_t - _swap_halves(dv, first_half) * sin_s


def _lane_masks():
    lane = lax.broadcasted_iota(jnp.int32, (BLK, LANE), 1)
    return (lane % HEAD) < (HEAD // 2), lane < HEAD


def _band_valid(first_block_bound, rows=BLK):
    rr = lax.broadcasted_iota(jnp.int32, (rows, 2 * BLK), 0) & (BLK - 1)
    jj = lax.broadcasted_iota(jnp.int32, (rows, 2 * BLK), 1)
    return (jj > rr) & (jj <= rr + BLK) & (jj >= first_block_bound)


def _dup_kv(slab, lo):
    rolled = pltpu.roll(slab, HEAD, 1)
    return jnp.where(lo, slab, rolled).astype(BF16), jnp.where(lo, rolled, slab).astype(BF16)


def _stack_heads(ref, sb, slab, lo, dtype):
    kh, base = sb // 2, 2 * (sb % 2) * BLK
    zero = jnp.zeros_like(slab)
    ref[kh, base:base + BLK, :] = jnp.where(lo, slab, zero).astype(dtype)
    ref[kh, base + BLK:base + 2 * BLK, :] = jnp.where(lo, zero, slab).astype(dtype)


def _unstack_heads(ref, sb, lo):
    kh, base = sb // 2, 2 * (sb % 2) * BLK
    return jnp.where(lo, ref[kh, base:base + BLK, :], ref[kh, base + BLK:base + 2 * BLK, :])


def _sink_column(sinks_ref, kh):
    row = lax.broadcasted_iota(jnp.int32, (Q_PER_KV * BLK, 1), 0)
    col = jnp.full(row.shape, sinks_ref[Q_PER_KV * kh + Q_PER_KV - 1], F32)
    for n in range(Q_PER_KV - 2, -1, -1):
        col = jnp.where(row < (n + 1) * BLK, sinks_ref[Q_PER_KV * kh + n], col)
    return col


def _tril():
    t = lax.broadcasted_iota(jnp.int32, (BLK, BLK), 0)
    s = lax.broadcasted_iota(jnp.int32, (BLK, BLK), 1)
    return s <= t


def _layer_norm_fwd(va, lg, lb):
    mu = jnp.mean(va, axis=-1, keepdims=True)
    xc = va - mu
    rstd = lax.rsqrt(jnp.mean(xc * xc, axis=-1, keepdims=True) + EPS)
    vhat = xc * rstd
    return vhat, rstd, vhat * lg + lb


def _softmax_sink(qm, kdup, bias, sink):
    s = lax.dot_general(qm, kdup, NT, preferred_element_type=F32) + bias
    m = jnp.maximum(jnp.max(s, axis=-1, keepdims=True), sink)
    p = jnp.exp(s - m)
    esink = jnp.exp(sink - m)
    inv = 1.0 / (jnp.sum(p, axis=-1, keepdims=True) + esink)
    return p * inv, esink * inv


def _band_bias(bias_ref):
    rows = bias_ref.shape[1]
    bias_ref[0] = jnp.where(_band_valid(BLK, rows), 0.0, NEG)
    bias_ref[1] = jnp.where(_band_valid(0, rows), 0.0, NEG)


def _rowmat_call(c_all, w, b, name):
    n = w.shape[1]
    tn = 512

    def body(c_ref, w_ref, b_ref, o_ref, ca_ref):
        ca, _ = _silu_parts(c_ref[...])
        ca_ref[...] = ca
        o_ref[...] = jnp.dot(ca.astype(BF16), w_ref[...].astype(BF16), preferred_element_type=F32) + b_ref[...]

    return pl.pallas_call(
        body, name=name, grid=(n // tn,),
        in_specs=[pl.BlockSpec((N_DEV, D), lambda j: (0, 0)), pl.BlockSpec((D, tn), lambda j: (0, j)),
                  pl.BlockSpec((1, tn), lambda j: (0, j))],
        out_specs=[pl.BlockSpec((N_DEV, tn), lambda j: (0, j)), pl.BlockSpec((N_DEV, D), lambda j: (0, 0))],
        out_shape=[jax.ShapeDtypeStruct((N_DEV, n), F32), jax.ShapeDtypeStruct((N_DEV, D), F32)],
        compiler_params=_params("arbitrary"),
    )(c_all, w, b)


W_IN_PARTS = ((0, 768), (768, 640))
OUT_STREAMS = 4
X_STREAMS = 4


def _proj_gather_call(pos, x, shift, scale, norm_g, wi_full, wo_full):
    s = x.shape[0]
    tm = min(s, 512)
    nrow = s // tm
    hi = D // 2
    ho = W_OUT_SHARD // 2
    phases = [(0, None), (1, 0), (2, 0), (1, 1), (2, 1), (3, 0), (3, 1)]

    def body(pos_ref, *refs):
        x_refs = refs[:X_STREAMS]
        (sh_ref, sc_ref, g_ref, _, _, h_ref, proj_ref, fi_ref, fo_ref,
         h_all, wbuf, obuf, send_sems, recv_sems, load_sems, out_sems) = refs[X_STREAMS:]
        p = pl.program_id(0)
        i = pl.program_id(1)
        x_, y_, c_ = _coords()
        me, sibling = (x_, y_, c_), (x_, y_, 1 - c_)

        def shard_of(q):
            px, py, _ = _peer(x_, y_, c_, q, 0)
            return 2 * px + py

        def cols_of(q, cp):
            off, w = (0, W_IN_SHARD) if cp is None else W_IN_PARTS[cp]
            return shard_of(q) * W_IN_SHARD + off, w

        def part(which, q, pc, sub, cp):
            n = hi if which == 0 else ho
            base = pc * n
            if sub is not None:
                n //= 2
                base = base + sub * n
            if which == 0:
                c0, w = cols_of(q, cp)
                return fi_ref.at[pl.ds(base, n), pl.ds(c0, w)]
            return fo_ref.at[pl.ds(shard_of(q) * W_OUT_SHARD + base, n), :]

        def copy(k, ref, to):
            return pltpu.make_async_remote_copy(src_ref=ref, dst_ref=ref, send_sem=send_sems.at[k], recv_sem=recv_sems.at[k],
                                                device_id=to, device_id_type=MESH)

        def sem(which, kind, j, cp):
            return 4 * kind + 2 * cp + j if which == 0 else 16 + 2 * kind + j

        def to_neighbour(which, q, cp=None):
            return copy(sem(which, 0, q - 1, cp), part(which, 0, c_, None, cp), _peer(x_, y_, c_, q, 0))

        def from_neighbour(which, q, cp=None):
            return copy(sem(which, 0, q - 1, cp), part(which, q, c_, None, cp), me)

        def relay(which, q, cp=None):
            return copy(sem(which, 1, q - 1, cp), part(which, q, c_, q - 1, cp), _peer(x_, y_, c_, 3 - q, 0))

        def relayed(which, sub, cp=None):
            return copy(sem(which, 1, sub, cp), part(which, 3, c_, sub, cp), me)

        def to_sibling(which, q, cp=None):
            return copy(sem(which, 2, q - 1, cp), part(which, q, c_, None, cp), sibling)

        def from_sibling(which, q, cp=None):
            return copy(sem(which, 2, q - 1, cp), part(which, q, 1 - c_, None, cp), me)

        def relayed_to_sibling(which, sub, cp=None):
            return copy(sem(which, 3, sub, cp), part(which, 3, c_, sub, cp), sibling)

        def relayed_from_sibling(which, sub, cp=None):
            return copy(sem(which, 3, sub, cp), part(which, 3, 1 - c_, sub, cp), me)

        def pass_on_neighbours(which, cp=None):
            for q in (1, 2):
                from_neighbour(which, q, cp).wait_recv()
                to_sibling(which, q, cp).start()
                relay(which, q, cp).start()

        def pass_on_relayed(which, cp=None):
            for sub in range(2):
                relayed(which, sub, cp).wait_recv()
                relayed_to_sibling(which, sub, cp).start()

        def shard_load(k):
            c0, w = cols_of(*phases[k])
            return pltpu.make_async_copy(fi_ref.at[:, pl.ds(c0, w)], wbuf.at[k % 2, :, 0:w], load_sems.at[k % 2])

        class OutCopies:
            def __init__(self, k, slot, row0):
                c0, w = cols_of(*phases[k])
                strip = tm // OUT_STREAMS
                self.copies = [pltpu.make_async_copy(obuf.at[slot, n * strip:(n + 1) * strip, 0:w],
                                                     proj_ref.at[pl.ds(row0 + n * strip, strip), pl.ds(c0, w)],
                                                     out_sems.at[slot, n]) for n in range(OUT_STREAMS)]

            def start(self):
                for cp in self.copies:
                    cp.start()

            def wait(self):
                for cp in self.copies:
                    cp.wait()

        out_copy = OutCopies

        def drain(k):
            for j in range(min(2, nrow)):
                out_copy(k, (nrow - 1 - j) % 2, 0).wait()

        def arrivals(k):
            q, cp = phases[k]
            if k == 0:
                for cp_ in range(2):
                    for q_ in (1, 2):
                        to_neighbour(0, q_, cp_).start()
            elif q < 3 and k in (1, 3):
                pass_on_neighbours(0, cp)
                if k == 1:
                    for q_ in (1, 2):
                        to_neighbour(1, q_).start()
            elif k == 5:
                for cp_ in range(2):
                    pass_on_relayed(0, cp_)
                pass_on_neighbours(1)
            if q in (1, 2):
                from_sibling(0, q, cp).wait_recv()
            elif q == 3:
                for sub in range(2):
                    relayed_from_sibling(0, sub, cp).wait_recv()

        rows = pl.ds(pl.multiple_of(i * tm, tm), tm)
        slot = i % 2
        for k, (q, cp) in enumerate(phases):
            @pl.when(p == k)
            def _(k=k, q=q, cp=cp):
                @pl.when(i == 0)
                def _():
                    if k == 0:
                        arrivals(0)
                        shard_load(0).start()
                    else:
                        drain(k - 1)
                    shard_load(k).wait()

                if k + 1 < len(phases):
                    @pl.when(i == max(nrow - 2, 0))
                    def _():
                        arrivals(k + 1)
                        shard_load(k + 1).start()

                if k == 0:
                    wx = D // X_STREAMS
                    ssq = sum(jnp.sum(xr[...] * xr[...], axis=-1, keepdims=True) for xr in x_refs)
                    r = lax.rsqrt(ssq * (1.0 / D) + EPS)
                    for n, xr in enumerate(x_refs):
                        cols = slice(n * wx, (n + 1) * wx)
                        hv = ((xr[...] * r * g_ref[:, cols]) * (1.0 + sc_ref[:, cols]) + sh_ref[:, cols]).astype(BF16)
                        h_ref[:, cols] = hv
                        h_all[rows, cols] = hv

                @pl.when(i >= 2)
                def _():
                    out_copy(k, slot, 0).wait()

                w = cols_of(q, cp)[1]
                obuf[slot, :, 0:w] = jnp.dot(h_all[rows, :], wbuf[k % 2, :, 0:w], preferred_element_type=F32)
                out_copy(k, slot, pl.multiple_of(i * tm, tm)).start()

        @pl.when((p == len(phases) - 1) & (i == nrow - 1))
        def _():
            drain(len(phases) - 1)
            pass_on_relayed(1)
            for q in (1, 2):
                from_sibling(1, q).wait_recv()
            for sub in range(2):
                relayed_from_sibling(1, sub).wait_recv()
            for which, cps in ((0, (0, 1)), (1, (None,))):
                for cp in cps:
                    for q in (1, 2):
                        to_neighbour(which, q, cp).wait_send()
                        relay(which, q, cp).wait_send()
                        to_sibling(which, q, cp).wait_send()
                        relayed_to_sibling(which, q - 1, cp).wait_send()

    vec = pl.BlockSpec((1, D), lambda p, i, pos: (0, 0))
    first_phase_rows = lambda p, i, pos: (jnp.where(p == 0, i, nrow - 1), 0)
    anyspec = pl.BlockSpec(memory_space=pl.ANY)
    x_spec = lambda n: pl.BlockSpec((tm, D // X_STREAMS), lambda p, i, pos: (jnp.where(p == 0, i, nrow - 1), n))
    return pl.pallas_call(
        body, name="proj_gather",
        grid_spec=pltpu.PrefetchScalarGridSpec(
            num_scalar_prefetch=1, grid=(len(phases), nrow),
            in_specs=[x_spec(n) for n in range(X_STREAMS)] + [vec, vec, vec, anyspec, anyspec],
            out_specs=[pl.BlockSpec((tm, D), first_phase_rows), anyspec, anyspec, anyspec],
            scratch_shapes=[pltpu.VMEM((s, D), BF16), pltpu.VMEM((2, D, W_IN_SHARD), BF16), pltpu.VMEM((2, tm, W_IN_SHARD), F32),
                            pltpu.SemaphoreType.DMA((24,)), pltpu.SemaphoreType.DMA((24,)), pltpu.SemaphoreType.DMA((2,)),
                            pltpu.SemaphoreType.DMA((2, OUT_STREAMS))]),
        out_shape=[jax.ShapeDtypeStruct((s, D), BF16), jax.ShapeDtypeStruct((s, D_IN), F32),
                   jax.ShapeDtypeStruct((D, D_IN), BF16), jax.ShapeDtypeStruct((D, D), BF16)],
        input_output_aliases={X_STREAMS + 4: 2, X_STREAMS + 5: 3},
        compiler_params=_params("arbitrary", "arbitrary"),
    )(pos, *([x] * X_STREAMS), shift, scale, norm_g, wi_full, wo_full)


def _proj_specs(rev_nb=None):
    if rev_nb is None:
        row = lambda i: i
    else:
        row = lambda i: rev_nb - 1 - i
    wide = lambda col: pl.BlockSpec((BLK, D_A), lambda i: (row(i), col))
    kv = lambda col: pl.BlockSpec((BLK, D_KV), lambda i: (row(i), col))
    half = lambda col: pl.BlockSpec((BLK, 512), lambda i: (row(i), col))
    return [wide(0), wide(1), wide(2), wide(3), kv(OFF_K // D_KV), kv(OFF_V // D_KV), half(OFF_ZB // 512), half(OFF_ZB // 512 + 1)]


def _mix_fwd_call(proj, cos, sin, ln_g, ln_b, w_sp, b_sp_t, sinks):
    s = proj.shape[0]
    nb = s // BLK

    def body(ua_ref, va_ref, za_ref, q_ref, k_ref, v_ref, zb0_ref, zb1_ref, cos_ref, sin_ref, lg_ref, lb_ref,
             w_ref, bt_ref, sinks_ref, y_ref, probs_ref, ost_ref, psink_ref, kdup_ref, vdup_ref, qm_ref, bias_ref):
        i = pl.program_id(0)
        first_half, lo = _lane_masks()
        cos_t = cos_ref[...]
        sin_t = sin_ref[...]

        _, _, vln = _layer_norm_fwd(va_ref[...], lg_ref[...], lb_ref[...])
        tril = _tril()
        for g in range(GROUPS):
            cols = slice(g * BLK, (g + 1) * BLK)
            wg = jnp.where(tril, w_ref[g], 0.0).astype(BF16)
            sg = jnp.dot(wg, vln[:, cols].astype(BF16), preferred_element_type=F32) + bt_ref[:, g:g + 1]
            gate, _ = _silu_parts(za_ref[:, cols])
            y_ref[:, cols] = (ua_ref[:, cols] * sg * gate).astype(BF16)

        @pl.when(i == 0)
        def _():
            kdup_ref[:, 0:BLK, :] = jnp.zeros((N_KV, BLK, LANE), BF16)
            vdup_ref[:, 0:BLK, :] = jnp.zeros((N_KV, BLK, LANE), BF16)
            _band_bias(bias_ref)

        @pl.when(i > 0)
        def _():
            kdup_ref[:, 0:BLK, :] = kdup_ref[:, BLK:2 * BLK, :]
            vdup_ref[:, 0:BLK, :] = vdup_ref[:, BLK:2 * BLK, :]

        for ks in range(2):
            cols = slice(ks * LANE, (ks + 1) * LANE)
            kr = _rope(k_ref[:, cols], cos_t, sin_t, first_half)
            for n, (kd, vd) in enumerate(zip(_dup_kv(kr, lo), _dup_kv(v_ref[:, cols], lo))):
                kdup_ref[2 * ks + n, BLK:2 * BLK, :] = kd
                vdup_ref[2 * ks + n, BLK:2 * BLK, :] = vd
        for sb in range(8):
            _stack_heads(qm_ref, sb, _rope(q_ref[:, sb * LANE:(sb + 1) * LANE], cos_t, sin_t, first_half) * SCALE, lo, BF16)

        block_kind = jnp.where(i > 0, 1, 0)

        psink_ref[...] = jnp.zeros((Q_PER_KV * BLK, LANE), F32)
        lane_q = lax.broadcasted_iota(jnp.int32, (Q_PER_KV * BLK, LANE), 1)

        def kv_head(kh, carry):
            probs, psink = _softmax_sink(qm_ref[kh], kdup_ref[kh], bias_ref[block_kind], _sink_column(sinks_ref, kh))
            probs_ref[kh] = probs
            psink_ref[...] = jnp.where(lane_q == kh, psink, psink_ref[...])
            ost_ref[kh] = jnp.dot(probs.astype(BF16), vdup_ref[kh], preferred_element_type=F32)
            return carry

        lax.fori_loop(0, N_KV, kv_head, 0, unroll=True)
        for sb in range(8):
            cols = slice(sb * LANE, (sb + 1) * LANE)
            zb = zb0_ref[:, cols] if sb < 4 else zb1_ref[:, (sb - 4) * LANE:(sb - 3) * LANE]
            gate, _ = _silu_parts(zb)
            y_ref[:, D_A + sb * LANE:D_A + (sb + 1) * LANE] = (_unstack_heads(ost_ref, sb, lo) * gate).astype(BF16)

    tab = pl.BlockSpec((BLK, LANE), lambda i: (i, 0))
    return pl.pallas_call(
        body, name="mix_fwd", grid=(nb,),
        in_specs=_proj_specs() + [
            tab, tab, pl.BlockSpec((1, D_A), lambda i: (0, 0)), pl.BlockSpec((1, D_A), lambda i: (0, 0)),
            pl.BlockSpec((GROUPS, BLK, BLK), lambda i: (0, 0, 0)), pl.BlockSpec((BLK, GROUPS), lambda i: (0, 0)),
            pl.BlockSpec(memory_space=pltpu.SMEM)],
        out_specs=[pl.BlockSpec((BLK, 2 * D_A), lambda i: (i, 0)),
                   pl.BlockSpec((None, N_KV, Q_PER_KV * BLK, 2 * BLK), lambda i: (i, 0, 0, 0)),
                   pl.BlockSpec((None, N_KV, Q_PER_KV * BLK, LANE), lambda i: (i, 0, 0, 0)),
                   pl.BlockSpec((None, Q_PER_KV * BLK, LANE), lambda i: (i, 0, 0))],
        out_shape=[jax.ShapeDtypeStruct((s, 2 * D_A), BF16), jax.ShapeDtypeStruct((nb, N_KV, Q_PER_KV * BLK, 2 * BLK), F32),
                   jax.ShapeDtypeStruct((nb, N_KV, Q_PER_KV * BLK, LANE), F32), jax.ShapeDtypeStruct((nb, Q_PER_KV * BLK, LANE), F32)],
        scratch_shapes=[pltpu.VMEM((N_KV, 2 * BLK, LANE), BF16), pltpu.VMEM((N_KV, 2 * BLK, LANE), BF16),
                        pltpu.VMEM((N_KV, Q_PER_KV * BLK, LANE), BF16), pltpu.VMEM((2, Q_PER_KV * BLK, 2 * BLK), F32)],
        compiler_params=_params("arbitrary"),
    )(proj, proj, proj, proj, proj, proj, proj, proj, cos, sin, ln_g, ln_b, w_sp, b_sp_t, sinks)


def _tail_call(y, w_out_bf, x, target, gate, shift_f, scale_f, gf):
    s = x.shape[0]
    tm = min(s, 256)
    nsteps = s // tm

    def body(y_ref, w_ref, x_ref, t_ref, gate_ref, shf_ref, scf_ref, gf_ref, dx2_ref, do_ref, dy_ref, st_ref):
        i = pl.program_id(0)

        @pl.when(i == 0)
        def _():
            st_ref[...] = jnp.zeros((8, D), F32)

        o = jnp.dot(y_ref[...], w_ref[...], preferred_element_type=F32)
        gate_v = gate_ref[...]
        x2 = x_ref[...] + gate_v * o
        r2 = lax.rsqrt(jnp.mean(x2 * x2, axis=-1, keepdims=True) + EPS)
        xn2 = x2 * r2
        hn2 = xn2 * gf_ref[...]
        one_sc = 1.0 + scf_ref[...]
        err = hn2 * one_sc + shf_ref[...] - t_ref[...]
        dout = err * (1.0 / D)
        dhn2 = dout * one_sc
        dxn2 = dhn2 * gf_ref[...]
        dx2 = r2 * (dxn2 - xn2 * jnp.mean(dxn2 * xn2, axis=-1, keepdims=True))
        dx2_ref[...] = dx2
        do = (dx2 * gate_v).astype(BF16)
        do_ref[...] = do
        dy_ref[...] = lax.dot_general(do, w_ref[...], NT, preferred_element_type=F32)
        st_ref[0:1, :] += jnp.sum(dout, axis=0, keepdims=True)
        st_ref[1:2, :] += jnp.sum(dout * hn2, axis=0, keepdims=True)
        st_ref[2:3, :] += jnp.sum(dhn2 * xn2, axis=0, keepdims=True)
        st_ref[3:4, :] += jnp.sum(dx2 * o, axis=0, keepdims=True)
        st_ref[4:5, :] += jnp.sum(err * err, axis=0, keepdims=True)

        @pl.when(i == nsteps - 1)
        def _():
            st_ref[5:6, :] = jnp.full((1, D), 0.5 / D, F32) * jnp.sum(st_ref[4:5, :])

    vec = pl.BlockSpec((1, D), lambda i: (0, 0))
    rows = lambda: pl.BlockSpec((tm, D), lambda i: (i, 0))
    return pl.pallas_call(
        body, name="tail", grid=(nsteps,),
        in_specs=[rows(), pl.BlockSpec((D, D), lambda i: (0, 0)), rows(), rows(), vec, vec, vec, vec],
        out_specs=[rows(), rows(), rows(), pl.BlockSpec((8, D), lambda i: (0, 0))],
        out_shape=[jax.ShapeDtypeStruct((s, D), F32), jax.ShapeDtypeStruct((s, D), BF16), jax.ShapeDtypeStruct((s, D), F32),
                   jax.ShapeDtypeStruct((8, D), F32)],
        compiler_params=_params("arbitrary"),
    )(y, w_out_bf, x, target, gate, shift_f, scale_f, gf)


def _tn_call(a, b, name):
    s, m = a.shape
    n = b.shape[1]
    tn = 1024
    ts = min(s, 1024)
    nk = s // ts

    def body(a_ref, b_ref, o_ref, acc_ref):
        k = pl.program_id(1)

        @pl.when(k == 0)
        def _():
            acc_ref[...] = jnp.zeros((m, tn), F32)

        acc_ref[...] += lax.dot_general(a_ref[...], b_ref[...], TN, preferred_element_type=F32)

        @pl.when(k == nk - 1)
        def _():
            o_ref[...] = acc_ref[...].astype(BF16)

    return pl.pallas_call(
        body, name=name, grid=(n // tn, nk),
        in_specs=[pl.BlockSpec((ts, m), lambda j, k: (k, 0)), pl.BlockSpec((ts, tn), lambda j, k: (k, j))],
        out_specs=pl.BlockSpec((m, tn), lambda j, k: (0, j)),
        out_shape=jax.ShapeDtypeStruct((m, n), BF16),
        scratch_shapes=[pltpu.VMEM((m, tn), F32)],
        compiler_params=_params("parallel", "arbitrary"),
    )(a, b)


def _tn_shards_call(pos, a, b, qs, name):
    s, m = a.shape
    ts = min(s, 1024)
    nk = s // ts

    def body(pos_ref, a_ref, b_ref, o_ref, acc_ref):
        k = pl.program_id(1)

        @pl.when(k == 0)
        def _():
            acc_ref[...] = jnp.zeros((m, W_IN_SHARD), F32)

        acc_ref[...] += lax.dot_general(a_ref[...], b_ref[...], TN, preferred_element_type=F32)

        @pl.when(k == nk - 1)
        def _():
            o_ref[...] = acc_ref[...].astype(BF16)

    def shard(j, pos):
        q = qs[0]
        for n in range(1, len(qs)):
            q = jnp.where(j == n, qs[n], q)
        return jnp.bitwise_xor(pos[0], q)

    return pl.pallas_call(
        body, name=name,
        grid_spec=pltpu.PrefetchScalarGridSpec(
            num_scalar_prefetch=1, grid=(len(qs), nk),
            in_specs=[pl.BlockSpec((ts, m), lambda j, k, pos: (k, 0)),
                      pl.BlockSpec((ts, W_IN_SHARD), lambda j, k, pos: (k, shard(j, pos)))],
            out_specs=pl.BlockSpec((m, W_IN_SHARD), lambda j, k, pos: (0, j)),
            scratch_shapes=[pltpu.VMEM((m, W_IN_SHARD), F32)]),
        out_shape=jax.ShapeDtypeStruct((m, len(qs) * W_IN_SHARD), BF16),
        compiler_params=_params("parallel", "arbitrary"),
    )(pos, a, b)


def _mix_bwd_call(proj, dy, probs, outs, psinks, tables, ln_g, ln_b, w_sp, w_sp_t, b_sp_t):
    s = proj.shape[0]
    nb = s // BLK
    rev = lambda i: nb - 1 - i
    prev = lambda i: jnp.maximum(nb - 2 - i, 0)

    def body(ua_ref, va_ref, za_ref, q_ref, k_ref, v_ref, zb0_ref, zb1_ref, kp_ref, vp_ref, dy_ref,
             probs_ref, ost_ref, psink_ref, cos_ref, sin_ref, cosp_ref, sinp_ref, lg_ref, lb_ref, w_ref, wt_ref, bt_ref,
             dp_ref, lnst_ref, dw_ref, dbt_ref, dsink_ref,
             kdup_ref, vdup_ref, dvln_ref, qm_ref, dom_ref, dqst_ref, dkdup_ref, dvdup_ref, kcar_ref, vcar_ref, sigb_ref):
        i = pl.program_id(0)
        first_half, lo = _lane_masks()
        lane8 = lax.broadcasted_iota(jnp.int32, (8, LANE), 1)
        cos_t = cos_ref[...]
        sin_t = sin_ref[...]

        @pl.when(i == 0)
        def _():
            lnst_ref[...] = jnp.zeros((8, D_A), F32)
            dw_ref[...] = jnp.zeros((GROUPS, BLK, BLK), F32)
            dbt_ref[...] = jnp.zeros((BLK, LANE), F32)
            dsink_ref[...] = jnp.zeros((8, LANE), F32)
            kcar_ref[...] = jnp.zeros((BLK, D_KV), F32)
            vcar_ref[...] = jnp.zeros((BLK, D_KV), F32)

        vhat, rstd, vln = _layer_norm_fwd(va_ref[...], lg_ref[...], lb_ref[...])
        tril = _tril()
        triu = jnp.logical_not(tril) | (lax.broadcasted_iota(jnp.int32, (BLK, BLK), 0) == lax.broadcasted_iota(jnp.int32, (BLK, BLK), 1))
        lane_b = lax.broadcasted_iota(jnp.int32, (BLK, LANE), 1)
        db_acc = jnp.zeros((BLK, LANE), F32)
        for g in range(GROUPS):
            cols = slice(g * BLK, (g + 1) * BLK)
            vln_g = vln[:, cols].astype(BF16)
            wg = jnp.where(tril, w_ref[g], 0.0).astype(BF16)
            sg = jnp.dot(wg, vln_g, preferred_element_type=F32) + bt_ref[:, g:g + 1]
            za = za_ref[:, cols]
            gate, sig = _silu_parts(za)
            ua = ua_ref[:, cols]
            dya_g = dy_ref[:, cols]
            dya = dya_g * gate
            dp_ref[:, cols] = (dya * sg).astype(BF16)
            dp_ref[:, 2 * D_A + g * BLK:2 * D_A + (g + 1) * BLK] = (
                dya_g * (ua * sg) * (sig * (1.0 + za * (1.0 - sig)))).astype(BF16)
            ds = dya * ua
            ds_b = ds.astype(BF16)
            wtg = jnp.where(triu, wt_ref[g], 0.0).astype(BF16)
            dvln_ref[:, cols] = jnp.dot(wtg, ds_b, preferred_element_type=F32)
            dw_ref[g] += jnp.where(tril, lax.dot_general(ds_b, vln_g, NT, preferred_element_type=F32), 0.0)
            db_acc = db_acc + jnp.where(lane_b == g, jnp.sum(ds, axis=-1, keepdims=True), 0.0)
        dbt_ref[...] += db_acc
        dvln = dvln_ref[...]
        lnst_ref[0:1, :] += jnp.sum(dvln * vhat, axis=0, keepdims=True)
        lnst_ref[1:2, :] += jnp.sum(dvln, axis=0, keepdims=True)
        dvhat = dvln * lg_ref[...]
        m1 = jnp.mean(dvhat, axis=-1, keepdims=True)
        m2 = jnp.mean(dvhat * vhat, axis=-1, keepdims=True)
        dp_ref[:, D_A:2 * D_A] = (rstd * (dvhat - m1 - vhat * m2)).astype(BF16)

        cosp = cosp_ref[...]
        sinp = sinp_ref[...]
        for ks in range(2):
            cols = slice(ks * LANE, (ks + 1) * LANE)
            kr = _rope(k_ref[:, cols], cos_t, sin_t, first_half)
            kpr = _rope(kp_ref[:, cols], cosp, sinp, first_half)
            for n, (kc, vc, kp, vp) in enumerate(zip(_dup_kv(kr, lo), _dup_kv(v_ref[:, cols], lo),
                                                     _dup_kv(kpr, lo), _dup_kv(vp_ref[:, cols], lo))):
                kdup_ref[2 * ks + n, BLK:2 * BLK, :] = kc
                vdup_ref[2 * ks + n, BLK:2 * BLK, :] = vc
                kdup_ref[2 * ks + n, 0:BLK, :] = kp
                vdup_ref[2 * ks + n, 0:BLK, :] = vp
        for sb in range(8):
            cols = slice(sb * LANE, (sb + 1) * LANE)
            _stack_heads(qm_ref, sb, _rope(q_ref[:, cols], cos_t, sin_t, first_half) * SCALE, lo, BF16)
            zb = zb0_ref[:, cols] if sb < 4 else zb1_ref[:, (sb - 4) * LANE:(sb - 3) * LANE]
            gate, sig = _silu_parts(zb)
            sigb_ref[:, cols] = sig
            _stack_heads(dom_ref, sb, dy_ref[:, D_A + sb * LANE:D_A + (sb + 1) * LANE] * gate, lo, F32)

        lane_q = lax.broadcasted_iota(jnp.int32, (Q_PER_KV * BLK, LANE), 1)

        def kv_head(kh, dsink_acc):
            qm = qm_ref[kh]
            kd = kdup_ref[kh]
            vd = vdup_ref[kh]
            probs = probs_ref[kh]
            psink = jnp.sum(jnp.where(lane_q == kh, psink_ref[...], 0.0), axis=-1, keepdims=True)
            probs_b = probs.astype(BF16)
            o = ost_ref[kh]
            dom = dom_ref[kh]
            dom_b = dom.astype(BF16)
            delta = jnp.sum(dom * o, axis=-1, keepdims=True)
            dpr = lax.dot_general(dom_b, vd, NT, preferred_element_type=F32)
            dss = (probs * (dpr - delta)).astype(BF16)
            sd = psink * delta
            for n in range(Q_PER_KV):
                dsink_acc = dsink_acc + jnp.where(lane8 == Q_PER_KV * kh + n, -jnp.sum(sd[n * BLK:(n + 1) * BLK]), 0.0)
            dqst_ref[kh] = jnp.dot(dss, kd, preferred_element_type=F32)
            dkdup_ref[kh] = lax.dot_general(dss, qm, TN, preferred_element_type=F32)
            dvdup_ref[kh] = lax.dot_general(probs_b, dom_b, TN, preferred_element_type=F32)
            return dsink_acc

        dsink_acc = jnp.zeros((8, LANE), F32)
        for kh in range(N_KV):
            dsink_acc = kv_head(kh, dsink_acc)
        row0 = lax.broadcasted_iota(jnp.int32, (8, LANE), 0) == 0
        dsink_ref[...] += jnp.where(row0, dsink_acc, 0.0)

        for sb in range(8):
            cols = slice(sb * LANE, (sb + 1) * LANE)
            zb = zb0_ref[:, cols] if sb < 4 else zb1_ref[:, (sb - 4) * LANE:(sb - 3) * LANE]
            sig = sigb_ref[:, cols]
            dyb = dy_ref[:, D_A + sb * LANE:D_A + (sb + 1) * LANE]
            dp_ref[:, OFF_ZB + sb * LANE:OFF_ZB + (sb + 1) * LANE] = (
                dyb * _unstack_heads(ost_ref, sb, lo) * (sig * (1.0 + zb * (1.0 - sig)))).astype(BF16)
            dq_r = _unstack_heads(dqst_ref, sb, lo) * SCALE
            dp_ref[:, OFF_Q + sb * LANE:OFF_Q + (sb + 1) * LANE] = _unrope(dq_r, cos_t, sin_t, first_half).astype(BF16)

        lo2 = lax.broadcasted_iota(jnp.int32, (2 * BLK, LANE), 1) < HEAD
        for ks in range(2):
            cols = slice(ks * LANE, (ks + 1) * LANE)
            ka = dkdup_ref[2 * ks]
            kb = dkdup_ref[2 * ks + 1]
            dk_band = jnp.where(lo2, ka + pltpu.roll(ka, HEAD, 1), kb + pltpu.roll(kb, HEAD, 1))
            va_ = dvdup_ref[2 * ks]
            vb_ = dvdup_ref[2 * ks + 1]
            dv_band = jnp.where(lo2, va_ + pltpu.roll(va_, HEAD, 1), vb_ + pltpu.roll(vb_, HEAD, 1))
            dkr = dk_band[BLK:2 * BLK, :] + kcar_ref[:, cols]
            dp_ref[:, OFF_K + ks * LANE:OFF_K + (ks + 1) * LANE] = _unrope(dkr, cos_t, sin_t, first_half).astype(BF16)
            dp_ref[:, OFF_V + ks * LANE:OFF_V + (ks + 1) * LANE] = (
                dv_band[BLK:2 * BLK, :] + vcar_ref[:, cols]).astype(BF16)
            kcar_ref[:, cols] = dk_band[0:BLK, :]
            vcar_ref[:, cols] = dv_band[0:BLK, :]

    tab = pl.BlockSpec((BLK, LANE), lambda i: (rev(i), 0))
    kvp = lambda col: pl.BlockSpec((BLK, D_KV), lambda i: (prev(i), col))
    vec = pl.BlockSpec((1, D_A), lambda i: (0, 0))
    w3 = pl.BlockSpec((GROUPS, BLK, BLK), lambda i: (0, 0, 0))
    return pl.pallas_call(
        body, name="mix_bwd", grid=(nb,),
        in_specs=_proj_specs(nb) + [
            kvp(OFF_K // D_KV), kvp(OFF_V // D_KV), pl.BlockSpec((BLK, 2 * D_A), lambda i: (rev(i), 0)),
            pl.BlockSpec((None, N_KV, Q_PER_KV * BLK, 2 * BLK), lambda i: (rev(i), 0, 0, 0)),
            pl.BlockSpec((None, N_KV, Q_PER_KV * BLK, LANE), lambda i: (rev(i), 0, 0, 0)),
            pl.BlockSpec((None, Q_PER_KV * BLK, LANE), lambda i: (rev(i), 0, 0)),
            tab, tab, tab, tab, vec, vec, w3, w3, pl.BlockSpec((BLK, GROUPS), lambda i: (0, 0))],
        out_specs=[pl.BlockSpec((BLK, D_IN), lambda i: (rev(i), 0)), pl.BlockSpec((8, D_A), lambda i: (0, 0)), w3,
                   pl.BlockSpec((BLK, LANE), lambda i: (0, 0)), pl.BlockSpec((8, LANE), lambda i: (0, 0))],
        out_shape=[jax.ShapeDtypeStruct((s, D_IN), BF16), jax.ShapeDtypeStruct((8, D_A), F32),
                   jax.ShapeDtypeStruct((GROUPS, BLK, BLK), F32), jax.ShapeDtypeStruct((BLK, LANE), F32),
                   jax.ShapeDtypeStruct((8, LANE), F32)],
        scratch_shapes=[pltpu.VMEM((N_KV, 2 * BLK, LANE), BF16), pltpu.VMEM((N_KV, 2 * BLK, LANE), BF16),
                        pltpu.VMEM((BLK, D_A), F32), pltpu.VMEM((N_KV, Q_PER_KV * BLK, LANE), BF16),
                        pltpu.VMEM((N_KV, Q_PER_KV * BLK, LANE), F32), pltpu.VMEM((N_KV, Q_PER_KV * BLK, LANE), F32),
                        pltpu.VMEM((N_KV, 2 * BLK, LANE), F32), pltpu.VMEM((N_KV, 2 * BLK, LANE), F32),
                        pltpu.VMEM((BLK, D_KV), F32), pltpu.VMEM((BLK, D_KV), F32), pltpu.VMEM((BLK, D_B), F32)],
        compiler_params=_params("arbitrary"),
    )(proj, proj, proj, proj, proj, proj, proj, proj, proj, proj, dy, probs, outs, psinks, *tables, ln_g, ln_b,
      w_sp, w_sp_t, b_sp_t)


def _dh_call(dproj, w_bf, x, dx2, scale, norm_g):
    s = x.shape[0]
    tm = min(s, 512)
    tk = W_IN_SHARD
    nk = D_IN // tk

    def body(dp_ref, w_ref, x_ref, dx2_ref, sc_ref, g_ref, gx_ref, st_ref, acc_ref):
        i = pl.program_id(0)
        k = pl.program_id(1)

        @pl.when((i == 0) & (k == 0))
        def _():
            st_ref[...] = jnp.zeros((8, D), F32)

        @pl.when(k == 0)
        def _():
            acc_ref[...] = jnp.zeros((tm, D), F32)

        acc_ref[...] += lax.dot_general(dp_ref[...], w_ref[...], NT, preferred_element_type=F32)

        @pl.when(k == nk - 1)
        def _():
            g = g_ref[...]
            one_sc = 1.0 + sc_ref[...]

            def chunk(n, carry):
                rows = pl.ds(pl.multiple_of(n * BLK, BLK), BLK)
                dh = acc_ref[rows, :]
                xv = x_ref[rows, :]
                r = lax.rsqrt(jnp.mean(xv * xv, axis=-1, keepdims=True) + EPS)
                xn = xv * r
                dhn = dh * one_sc
                dxn = dhn * g
                gx_ref[rows, :] = dx2_ref[rows, :] + r * (dxn - xn * jnp.mean(dxn * xn, axis=-1, keepdims=True))
                st_ref[0:1, :] += jnp.sum(dh, axis=0, keepdims=True)
                st_ref[1:2, :] += jnp.sum(dh * (xn * g), axis=0, keepdims=True)
                st_ref[2:3, :] += jnp.sum(dhn * xn, axis=0, keepdims=True)
                return carry

            lax.fori_loop(0, tm // BLK, chunk, 0)

    vec = pl.BlockSpec((1, D), lambda i, k: (0, 0))
    rows = lambda: pl.BlockSpec((tm, D), lambda i, k: (i, 0))
    return pl.pallas_call(
        body, name="dh", grid=(s // tm, nk),
        in_specs=[pl.BlockSpec((tm, tk), lambda i, k: (i, k)), pl.BlockSpec((D, tk), lambda i, k: (0, k)), rows(), rows(), vec, vec],
        out_specs=[rows(), pl.BlockSpec((8, D), lambda i, k: (0, 0))],
        out_shape=[jax.ShapeDtypeStruct((s, D), F32), jax.ShapeDtypeStruct((8, D), F32)],
        scratch_shapes=[pltpu.VMEM((tm, D), F32)],
        compiler_params=_params("arbitrary", "arbitrary"),
    )(dproj, w_bf, x, dx2, scale, norm_g)


def _adam_math(w, g, m, v):
    m_new = ADAM_B1 * m + (1.0 - ADAM_B1) * g
    v_new = ADAM_B2 * v + (1.0 - ADAM_B2) * (g * g)
    m_hat = m_new / ADAM_C1
    v_hat = v_new / ADAM_C2
    delta = -ADAM_LR * (m_hat / (jnp.sqrt(v_hat) + ADAM_EPS) + ADAM_WD * w)
    return delta, m_new, v_new


def _adam_small_call(tensors):
    n = len(tensors)

    def body(*refs):
        ins, outs = refs[:4 * n], refs[4 * n:]
        for t in range(n):
            w_ref, g_ref, m_ref, v_ref = ins[4 * t:4 * t + 4]
            d, mo, vo = _adam_math(w_ref[...], g_ref[...], m_ref[...], v_ref[...])
            outs[3 * t][...], outs[3 * t + 1][...], outs[3 * t + 2][...] = d, mo, vo

    vm = pl.BlockSpec(memory_space=pltpu.VMEM)
    flat = [a for t in tensors for a in t]
    out = pl.pallas_call(
        body, name="adam_small", in_specs=[vm] * (4 * n), out_specs=[vm] * (3 * n),
        out_shape=[jax.ShapeDtypeStruct(t[0].shape, F32) for t in tensors for _ in range(3)],
        compiler_params=pltpu.CompilerParams(vmem_limit_bytes=VMEM_LIMIT),
    )(*flat)
    return [tuple(out[3 * t:3 * t + 3]) for t in range(n)]


def _adam_halves_call(pos, w, mine, theirs, m, v, name):
    r, n = w.shape
    half = r // 2
    tr = ADAM_ROWS
    nh = half // tr

    def body(pos_ref, w_ref, mine_ref, theirs_ref, m_ref, v_ref, g_ref, d_ref, mo_ref, vo_ref):
        is_mine = (pl.program_id(0) // nh) == pos_ref[1]
        g = jnp.where(is_mine, mine_ref[...], theirs_ref[...])
        g_ref[...] = g
        d_ref[...], mo_ref[...], vo_ref[...] = _adam_math(w_ref[...], g, m_ref[...], v_ref[...])

    spec = lambda: pl.BlockSpec((tr, n), lambda i, pos: (i, 0))

    def half_spec(core_of_half):
        def index(i, pos):
            first = core_of_half(pos) == 0
            active = (i // nh == 0) == first
            return jnp.where(active, i % nh, jnp.where(first, nh - 1, 0)), 0
        return pl.BlockSpec((tr, n), index)

    return pl.pallas_call(
        body, name=name,
        grid_spec=pltpu.PrefetchScalarGridSpec(
            num_scalar_prefetch=1, grid=(r // tr,),
            in_specs=[spec(), half_spec(lambda pos: pos[1]), half_spec(lambda pos: 1 - pos[1]), spec(), spec()],
            out_specs=[spec() for _ in range(4)]),
        out_shape=[jax.ShapeDtypeStruct((r, n), F32)] * 4, compiler_params=_params("arbitrary"),
    )(pos, w, mine, theirs, m, v)


def _adam_outer_call(w, ct, dm, m, v, name):
    r, n = w.shape
    tr = ADAM_ROWS

    def body(w_ref, ct_ref, dm_ref, m_ref, v_ref, g_ref, d_ref, mo_ref, vo_ref):
        g = ct_ref[:, 0:1] * dm_ref[0:1, :]
        for b in range(1, N_DEV):
            g = g + ct_ref[:, b:b + 1] * dm_ref[b:b + 1, :]
        g_ref[...] = g
        d_ref[...], mo_ref[...], vo_ref[...] = _adam_math(w_ref[...], g, m_ref[...], v_ref[...])

    spec = lambda: pl.BlockSpec((tr, n), lambda i: (i, 0))
    return pl.pallas_call(
        body, name=name, grid=(r // tr,),
        in_specs=[spec(), pl.BlockSpec((tr, N_DEV), lambda i: (i, 0)), pl.BlockSpec((N_DEV, n), lambda i: (0, 0)), spec(), spec()],
        out_specs=[spec() for _ in range(4)],
        out_shape=[jax.ShapeDtypeStruct((r, n), F32)] * 4, compiler_params=_params("parallel"),
    )(w, ct, dm, m, v)


def _sum_pieces_call(pos, part, part_block, recvs, name):
    r, n = recvs[0].shape[1:]
    tr = min(r, 256)
    nrb = r // tr

    def body(pos_ref, p_ref, *refs):
        acc = p_ref[...].astype(F32)
        for r_ref in refs[:-1]:
            for d in range(r_ref.shape[0]):
                acc = acc + r_ref[d].astype(F32)
        refs[-1][...] = acc

    return pl.pallas_call(
        body, name=name,
        grid_spec=pltpu.PrefetchScalarGridSpec(
            num_scalar_prefetch=1, grid=(nrb,),
            in_specs=[pl.BlockSpec((tr, n), lambda i, pos: part_block(i, pos, nrb))] + [
                pl.BlockSpec((rv.shape[0], tr, n), lambda i, pos: (0, i, 0)) for rv in recvs],
            out_specs=pl.BlockSpec((tr, n), lambda i, pos: (i, 0))),
        out_shape=jax.ShapeDtypeStruct((r, n), F32), compiler_params=_params("parallel"),
    )(pos, part, *recvs)


def _coords():
    return lax.axis_index("x"), lax.axis_index("y"), lax.axis_index("c")


CAST_ROWS = 256


def _allgather_sum_call(blk, name, with_sum, cast=None):
    m_per, n = blk.shape
    n_out = 2 if with_sum else 1
    if cast is not None:
        w, full_shape = cast
        wr, wn = w.shape
        by_cols = full_shape[0] == wr
        tr = min(wr, CAST_ROWS)
        n_chunk = wr // tr

    def body(*refs):
        x_ref = refs[0]
        out_ref = refs[1 + (cast is not None)]
        rest = refs[1 + (cast is not None) + n_out + (cast is not None):]
        send_sems, recv_sems, local_sem = rest[:3]
        x, y, c = _coords()
        me, sibling = (x, y, c), (x, y, 1 - c)
        chips = [(1 - x, y), (x, 1 - y), (1 - x, 1 - y)]

        def rows(px, py, pc):
            return out_ref.at[pl.ds((4 * px + 2 * py + pc) * m_per, m_per), :]

        def copy(k, block, to, src=None):
            return pltpu.make_async_remote_copy(
                src_ref=rows(*block) if src is None else src, dst_ref=rows(*block),
                send_sem=send_sems.at[k], recv_sem=recv_sems.at[k], device_id=to, device_id_type=MESH)

        mine = pltpu.make_async_copy(x_ref, rows(*me), local_sem)
        mine.start()
        first = [copy(0, me, sibling, src=x_ref)]
        first += [copy(1 + j, me, (*chip, c), src=x_ref) for j, chip in enumerate(chips)]
        for cp in first:
            cp.start()

        if cast is not None:
            w_ref, full_ref = refs[1], refs[1 + 1 + n_out]
            f32_buf, bf16_buf, in_sems, out_sems = rest[3:]
            chip_no = 2 * x + y

            def fetch(i):
                return pltpu.make_async_copy(w_ref.at[pl.ds(i * tr, tr), :], f32_buf.at[i % 2], in_sems.at[i % 2])

            def store(i):
                if by_cols:
                    dst = full_ref.at[pl.ds(i * tr, tr), pl.ds(chip_no * wn, wn)]
                else:
                    dst = full_ref.at[pl.ds(chip_no * wr + i * tr, tr), :]
                return pltpu.make_async_copy(bf16_buf.at[i % 2], dst, out_sems.at[i % 2])

            fetch(0).start()
            for i in range(n_chunk):
                if i + 1 < n_chunk:
                    fetch(i + 1).start()
                fetch(i).wait()
                if i >= 2:
                    store(i - 2).wait()
                bf16_buf[i % 2] = f32_buf[i % 2].astype(BF16)
                store(i).start()
            for i in range(max(n_chunk - 2, 0), n_chunk):
                store(i).wait()

        passed = [copy(4 + j, (*chip, c), sibling) for j, chip in enumerate(chips)]
        for j, chip in enumerate(chips):
            copy(1 + j, (*chip, c), me).wait_recv()
            passed[j].start()
        copy(0, sibling, me).wait_recv()
        for j, chip in enumerate(chips):
            copy(4 + j, (*chip, 1 - c), me).wait_recv()
        for cp in first + passed:
            cp.wait_send()
        mine.wait()
        if with_sum:
            sum_ref = refs[1 + (cast is not None) + 1]
            acc = out_ref[0:m_per, :]
            for d in range(1, N_DEV):
                acc = acc + out_ref[d * m_per:(d + 1) * m_per, :]
            sum_ref[...] = acc

    vm = pl.BlockSpec(memory_space=pltpu.VMEM)
    anyspec = pl.BlockSpec(memory_space=pl.ANY)
    out_shape = [jax.ShapeDtypeStruct((N_DEV * m_per, n), F32)]
    if with_sum:
        out_shape.append(jax.ShapeDtypeStruct((m_per, n), F32))
    in_specs, out_specs, operands = [vm], [vm] * n_out, [blk]
    scratch = [pltpu.SemaphoreType.DMA((7,)), pltpu.SemaphoreType.DMA((7,)), pltpu.SemaphoreType.DMA]
    if cast is not None:
        in_specs.append(anyspec)
        operands.append(w)
        out_shape.append(jax.ShapeDtypeStruct(full_shape, BF16))
        out_specs.append(anyspec)
        scratch += [pltpu.VMEM((2, tr, wn), F32), pltpu.VMEM((2, tr, wn), BF16), pltpu.SemaphoreType.DMA((2,)),
                    pltpu.SemaphoreType.DMA((2,))]
    return pl.pallas_call(
        body, name=name, out_shape=out_shape, in_specs=in_specs, out_specs=out_specs, scratch_shapes=scratch,
        compiler_params=pltpu.CompilerParams(vmem_limit_bytes=VMEM_LIMIT),
    )(*operands)


HBM_SPEC = pl.BlockSpec(memory_space=pltpu.HBM)
SEM_SPEC = pl.BlockSpec(memory_space=pltpu.SEMAPHORE)
SIDE_EFFECT = pltpu.SideEffectType.DATAFLOW_SIDE_EFFECTING


def _peer(x, y, c, q, cb):
    return (1 - x if q & 2 else x, 1 - y if q & 1 else y, 1 - c if cb else c)


def _w_in_piece(slots):
    def piece(part_ref, k, to):
        return part_ref.at[pl.ds(to[2] * (D // 2), D // 2), pl.ds(slots[k] * W_IN_SHARD, W_IN_SHARD)]
    return piece


def _w_out_piece(part_ref, k, to):
    ho = W_OUT_SHARD // 2
    return part_ref.at[pl.ds((2 * to[0] + to[1]) * W_OUT_SHARD + to[2] * ho, ho), :]


def _group_piece(part_ref, k, to):
    return part_ref.at[4 * to[0] + 2 * to[1] + to[2]]


def _whole_piece(part_ref, k, to):
    return part_ref


def _exchange_start_call(groups, name):
    ng = len(groups)
    lands = [lax.empty((len(rels),) + slot_shape, part.dtype) for part, rels, _, slot_shape in groups]

    def body(*refs):
        ins, outs = refs[:2 * ng], refs[2 * ng:]
        x, y, c = _coords()
        for g, (_, rels, piece, _) in enumerate(groups):
            part_ref, land_ref = ins[2 * g], ins[2 * g + 1]
            send_sems, recv_sems = outs[4 * g], outs[4 * g + 1]
            for k, (q, cb) in enumerate(rels):
                to = _peer(x, y, c, q, cb)
                pltpu.make_async_remote_copy(src_ref=piece(part_ref, k, to), dst_ref=land_ref.at[k], send_sem=send_sems.at[k],
                                             recv_sem=recv_sems.at[k], device_id=to, device_id_type=MESH).start()
        outs[-1][...] = jnp.zeros_like(outs[-1])

    out_shape, out_specs, operands = [], [], []
    for (part, rels, _, _), land in zip(groups, lands):
        n = len(rels)
        out_shape += [pltpu.SemaphoreType.DMA((n,)), pltpu.SemaphoreType.DMA((n,)), pltpu.HBM(part.shape, part.dtype),
                      pltpu.HBM(land.shape, land.dtype)]
        out_specs += [SEM_SPEC, SEM_SPEC, HBM_SPEC, HBM_SPEC]
        operands += [pltpu.with_memory_space_constraint(part, pltpu.HBM), pltpu.with_memory_space_constraint(land, pltpu.HBM)]
    out = pl.pallas_call(
        body, name=name,
        out_shape=tuple(out_shape) + (jax.ShapeDtypeStruct((1, 1), F32),),
        in_specs=(HBM_SPEC,) * (2 * ng), out_specs=tuple(out_specs) + (pl.BlockSpec(memory_space=pltpu.VMEM),),
        input_output_aliases={j: 4 * (j // 2) + 2 + j % 2 for j in range(2 * ng)},
        compiler_params=pltpu.CompilerParams(has_side_effects=SIDE_EFFECT),
    )(*operands)
    return [tuple(out[4 * g:4 * g + 4]) for g in range(ng)], out[-1]


def _exchange_wait_call(started, groups, after, name):
    ng = len(groups)

    def body(*refs):
        ins = refs[:4 * ng]
        x, y, c = _coords()
        for g, (_, rels, piece, _) in enumerate(groups):
            part_ref, land_ref, send_sems, recv_sems = ins[4 * g:4 * g + 4]
            for k, (q, cb) in enumerate(rels):
                to = _peer(x, y, c, q, cb)
                cp = pltpu.make_async_remote_copy(src_ref=piece(part_ref, k, to), dst_ref=land_ref.at[k], send_sem=send_sems.at[k],
                                                  recv_sem=recv_sems.at[k], device_id=to, device_id_type=MESH)
                cp.wait_send()
                cp.wait_recv()

    operands, in_specs, out_shape = [], [], []
    for send_sems, recv_sems, part_thru, land_thru in started:
        operands += [part_thru, land_thru, send_sems, recv_sems]
        in_specs += [HBM_SPEC, HBM_SPEC, SEM_SPEC, SEM_SPEC]
        out_shape += [pltpu.HBM(part_thru.shape, part_thru.dtype), pltpu.HBM(land_thru.shape, land_thru.dtype)]
    out = pl.pallas_call(
        body, name=name, out_shape=tuple(out_shape),
        in_specs=tuple(in_specs) + (pl.BlockSpec(memory_space=pl.ANY),), out_specs=(HBM_SPEC,) * (2 * ng),
        input_output_aliases={4 * g + j: 2 * g + j for g in range(ng) for j in range(2)},
        compiler_params=pltpu.CompilerParams(has_side_effects=SIDE_EFFECT),
    )(*operands, after)
    return [tuple(out[2 * g:2 * g + 2]) for g in range(ng)]


def _rope_tables(s):
    inv_freq = np.float32(10000.0) ** (-np.arange(0, HEAD, 2, dtype=np.float32) / np.float32(HEAD))
    ang = np.arange(s, dtype=np.float32)[:, None] * inv_freq[None, :]
    cos = np.tile(np.cos(ang), (1, LANE // (HEAD // 2))).astype(np.float32)
    sin = np.tile(np.sin(ang), (1, LANE // (HEAD // 2))).astype(np.float32)
    first_half = (np.arange(LANE) % HEAD) < (HEAD // 2)
    sin = np.where(first_half[None, :], -sin, sin)
    behind = lambda t: np.concatenate([t[:BLK], t[:-BLK]], axis=0)
    return tuple(jnp.asarray(t) for t in (cos, sin, behind(cos), behind(sin)))


def kernel(x, c, w_ada, b_ada, norm_g, w_in, ln_v_g, ln_v_b, w_spatial, b_spatial, sinks, w_out, w_ada_final, b_ada_final, final_norm_g, loss_target, m_w_ada, m_b_ada, m_norm_g, m_w_in, m_ln_v_g, m_ln_v_b, m_w_spatial, m_b_spatial, m_sinks, m_w_out, m_w_ada_final, m_b_ada_final, m_final_norm_g, v_w_ada, v_b_ada, v_norm_g, v_w_in, v_ln_v_g, v_ln_v_b, v_w_spatial, v_b_spatial, v_sinks, v_w_out, v_w_ada_final, v_b_ada_final, v_final_norm_g):
    s = x.shape[1]
    ax, ay, ac = _coords()
    chip = 2 * ax + ay
    me = 4 * ax + 2 * ay + ac
    n_ada = w_ada.shape[2]
    n_adaf = w_ada_final.shape[1]

    x2d = x.reshape(s, D)
    tgt = loss_target.reshape(s, D)
    w_ada2, w_in2, w_out2 = w_ada[0], w_in[0], w_out[0]
    b_ada_f2 = b_ada_final.reshape(1, 2 * D)
    gf = final_norm_g.reshape(1, D)

    c_all, w_in_own = _allgather_sum_call(jnp.pad(c, ((0, 7), (0, 0))), "gather_c", False, cast=(w_in2, (D, D_IN)))
    c_all = c_all[::8]
    mod_p, c_act = _rowmat_call(c_all, w_ada2, lax.dynamic_slice(b_ada, (0, chip * n_ada), (1, n_ada)), "mod")
    modf_p, _ = _rowmat_call(c_all, w_ada_final, lax.dynamic_slice(b_ada_f2, (0, chip * n_adaf), (1, n_adaf)), "mod_final")
    mods, w_out_own = _allgather_sum_call(jnp.concatenate([mod_p, modf_p], axis=1), "gather_mod", False, cast=(w_out2, (D, D)))
    my_rows = [lax.dynamic_slice(mods, (16 * j + me, 0), (1, n_ada + n_adaf)) for j in range(N_CHIP)]
    mod = jnp.concatenate([r[:, :n_ada] for r in my_rows], axis=1)
    mod_f = jnp.concatenate([r[:, n_ada:] for r in my_rows], axis=1)
    shift, scale, gate = mod[:, :D], mod[:, D:2 * D], mod[:, 2 * D:]
    shift_f, scale_f = mod_f[:, :D], mod_f[:, D:]

    pos = jnp.stack([chip, ac]).astype(jnp.int32)

    tables = _rope_tables(s)
    cos, sin = tables[:2]
    b_sp_t = b_spatial[0].T
    sinks1 = sinks.reshape(N_Q)
    h, proj, w_in_bf, w_out_bf = _proj_gather_call(pos, x2d, shift, scale, norm_g, w_in_own, w_out_own)
    y, probs, attn_out, psinks = _mix_fwd_call(proj, cos, sin, ln_v_g, ln_v_b, w_spatial[0], b_sp_t, sinks1)
    dx2, do, dy, st_tail = _tail_call(y, w_out_bf, x2d, tgt, gate, shift_f, scale_f, gf)

    rel_o = [(0, 1), (1, 0), (1, 1), (2, 0), (2, 1), (3, 0), (3, 1)]
    rel_a = [(1, 0), (1, 1), (2, 0), (2, 1)]
    rel_b = [(3, 0), (3, 1), (0, 1)]
    piece_a, piece_b = _w_in_piece([0, 0, 1, 1]), _w_in_piece([0, 0, 1])
    half_in, half_out = (D // 2, W_IN_SHARD), (W_OUT_SHARD // 2, D)

    g_w_out_p = _tn_call(y, do, "grad_w_out")
    grp_o = [(g_w_out_p, rel_o, _w_out_piece, half_out)]
    st_o, tok_o = _exchange_start_call(grp_o, "send_w_out")
    dproj, st_ln, d_wsp, d_bsp_t, d_sink = _mix_bwd_call(
        proj, dy, probs, attn_out, psinks, tables, ln_v_g + tok_o, ln_v_b, w_spatial[0], jnp.swapaxes(w_spatial[0], 1, 2),
        b_sp_t)
    g_w_in_a = _tn_shards_call(pos, h, dproj, (1, 2), "grad_w_in_a")
    grp_a = [(g_w_in_a, rel_a, piece_a, half_in), (d_wsp, rel_o, _group_piece, (BLK, BLK))]
    st_a, tok_a = _exchange_start_call(grp_a, "send_w_in_a")
    g_w_in_b = _tn_shards_call(pos, h, dproj, (3, 0), "grad_w_in_b")
    grp_b = [(g_w_in_b, rel_b, piece_b, half_in)]
    st_b, tok_b = _exchange_start_call(grp_b, "send_w_in_b")
    grad_x, st_dh = _dh_call(dproj, w_in_bf, x2d, dx2, scale + (tok_a + tok_b), norm_g)

    ((g_w_out_p, recv_o),) = _exchange_wait_call(st_o, grp_o, st_dh, "wait_w_out")
    (_, recv_a), (d_wsp, recv_s) = _exchange_wait_call(st_a, grp_a, st_dh, "wait_w_in_a")
    ((g_w_in_b, recv_b),) = _exchange_wait_call(st_b, grp_b, st_dh, "wait_w_in_b")
    mine_in = _sum_pieces_call(pos, g_w_in_b, lambda i, p, nrb: (p[1] * nrb + i, 1), [recv_a, recv_b], "sum_w_in")
    mine_out = _sum_pieces_call(pos, g_w_out_p, lambda i, p, nrb: ((2 * p[0] + p[1]) * nrb + i, 0), [recv_o], "sum_w_out")
    wsp_group = _sum_pieces_call(pos, d_wsp.reshape(GROUPS * BLK, BLK), lambda i, p, nrb: (2 * p[0] + p[1], 0), [recv_s],
                                 "sum_w_spatial")
    to_sibling = [(0, 1)]
    grp_p = [(mine_in, to_sibling, _whole_piece, half_in), (mine_out, to_sibling, _whole_piece, half_out)]
    st_p, tok_p = _exchange_start_call(grp_p, "swap_halves")

    misc = jnp.concatenate([st_ln, d_bsp_t[:, :GROUPS].T, d_sink, jnp.zeros((8, D - D_A - 2 * LANE), F32)], axis=1)
    pack = jnp.concatenate([wsp_group.reshape(8, D) + tok_p, st_tail, st_dh, misc], axis=0)
    rows = pack.shape[0]
    packs, tot = _allgather_sum_call(pack, "gather_small", True)
    packs = packs.reshape(N_DEV, rows, D)
    dmod_all = jnp.concatenate([packs[:, 16, :], packs[:, 17, :], packs[:, 11, :]], axis=1)
    dmodf_all = jnp.concatenate([packs[:, 8, :], packs[:, 9, :]], axis=1)
    loss = tot[13, 0]
    (mine_in, theirs_in), (mine_out, theirs_out) = _exchange_wait_call(st_p, grp_p, tot, "swapped_halves")
    small = {
        "b_ada": jnp.concatenate([tot[16:17], tot[17:18], tot[11:12]], axis=1),
        "norm_g": tot[18:19],
        "ln_v_g": tot[24:25, :D_A],
        "ln_v_b": tot[25:26, :D_A],
        "w_spatial": packs[:, 0:8, :].reshape(GROUPS * BLK, BLK),
        "b_spatial": tot[24:32, D_A:D_A + BLK],
        "sinks": tot[24:25, D_A + LANE:D_A + LANE + N_Q],
        "b_ada_final": jnp.concatenate([tot[8:9], tot[9:10]], axis=1),
        "final_norm_g": tot[10:11],
    }

    weights = dict(w_ada=w_ada, b_ada=b_ada, norm_g=norm_g, w_in=w_in, ln_v_g=ln_v_g, ln_v_b=ln_v_b, w_spatial=w_spatial,
                   b_spatial=b_spatial, sinks=sinks, w_out=w_out, w_ada_final=w_ada_final, b_ada_final=b_ada_final,
                   final_norm_g=final_norm_g)
    m_in = dict(w_ada=m_w_ada, b_ada=m_b_ada, norm_g=m_norm_g, w_in=m_w_in, ln_v_g=m_ln_v_g, ln_v_b=m_ln_v_b,
                w_spatial=m_w_spatial, b_spatial=m_b_spatial, sinks=m_sinks, w_out=m_w_out, w_ada_final=m_w_ada_final,
                b_ada_final=m_b_ada_final, final_norm_g=m_final_norm_g)
    v_in = dict(w_ada=v_w_ada, b_ada=v_b_ada, norm_g=v_norm_g, w_in=v_w_in, ln_v_g=v_ln_v_g, ln_v_b=v_ln_v_b,
                w_spatial=v_w_spatial, b_spatial=v_b_spatial, sinks=v_sinks, w_out=v_w_out, w_ada_final=v_w_ada_final,
                b_ada_final=v_b_ada_final, final_norm_g=v_final_norm_g)
    c_act_t = c_act.T
    outer = {"w_ada": lax.dynamic_slice(dmod_all, (0, chip * n_ada), (N_DEV, n_ada)),
             "w_ada_final": lax.dynamic_slice(dmodf_all, (0, chip * n_adaf), (N_DEV, n_adaf))}
    halves = {"w_in": (mine_in, theirs_in[0]), "w_out": (mine_out, theirs_out[0])}
    done = {}
    for name, (mine, theirs) in halves.items():
        shape2 = (2 * mine.shape[0], mine.shape[1])
        done[name] = _adam_halves_call(pos, weights[name].reshape(shape2), mine, theirs, m_in[name].reshape(shape2),
                                       v_in[name].reshape(shape2), "adam_" + name)
    for name, dm in outer.items():
        shape2 = (D, dm.shape[1])
        done[name] = _adam_outer_call(weights[name].reshape(shape2), c_act_t, dm, m_in[name].reshape(shape2),
                                      v_in[name].reshape(shape2), "adam_" + name)
    updates = _adam_small_call([(weights[name].reshape(g.shape), g, m_in[name].reshape(g.shape), v_in[name].reshape(g.shape))
                                for name, g in small.items()])
    for (name, g), upd in zip(small.items(), updates):
        done[name] = (g, *upd)
    outs = [[done[name][k].reshape(w.shape) for name, w in weights.items()] for k in range(4)]
    return (loss, grad_x.reshape(x.shape), *outs[0], *outs[1], *outs[2], *outs[3])
```

```python
import numpy as np
import jax
import jax.numpy as jnp
from jax import lax
from jax.experimental import pallas as pl
from jax.experimental.pallas import tpu as pltpu

F32 = jnp.float32
BF16 = jnp.bfloat16
MESH = pl.DeviceIdType.MESH

D = 2048
D_A = 1024
D_B = 1024
D_KV = 256
HEAD = 64
N_Q = 16
N_KV = 4
Q_PER_KV = N_Q // N_KV
BLK = 128
GROUPS = 8
D_IN = 5632
OFF_Q, OFF_K, OFF_V, OFF_ZB = 3072, 4096, 4352, 4608
N_CHIP = 4
N_DEV = 8
W_IN_SHARD = D_IN // N_CHIP
W_OUT_SHARD = D // N_CHIP
EPS = 1e-5
SCALE = HEAD ** -0.5
NEG = -1e30
LANE = 128
VMEM_LIMIT = 56 * 1024 * 1024

ADAM_LR, ADAM_B1, ADAM_B2, ADAM_EPS, ADAM_WD, ADAM_STEP = 0.001, 0.9, 0.999, 1e-08, 0.01, 10
ADAM_C1 = 1.0 - ADAM_B1 ** ADAM_STEP
ADAM_C2 = 1.0 - ADAM_B2 ** ADAM_STEP
ADAM_ROWS = 256

NT = (((1,), (1,)), ((), ()))
TN = (((0,), (0,)), ((), ()))


def _params(*sem):
    return pltpu.CompilerParams(dimension_semantics=sem, vmem_limit_bytes=VMEM_LIMIT)


def _silu_parts(z):
    sig = 1.0 / (1.0 + jnp.exp(-z))
    return z * sig, sig


def _swap_halves(v, first_half):
    return jnp.where(first_half, pltpu.roll(v, 96, 1), pltpu.roll(v, 32, 1))


def _rope(v, cos_t, sin_s, first_half):
    return v * cos_t + _swap_halves(v, first_half) * sin_s


def _unrope(dv, cos_t, sin_s, first_half):
    return dv * cos_t - _swap_halves(dv, first_half) * sin_s


def _lane_masks():
    lane = lax.broadcasted_iota(jnp.int32, (BLK, LANE), 1)
    return (lane % HEAD) < (HEAD // 2), lane < HEAD


def _band_valid(first_block_bound, rows=BLK):
    rr = lax.broadcasted_iota(jnp.int32, (rows, 2 * BLK), 0) & (BLK - 1)
    jj = lax.broadcasted_iota(jnp.int32, (rows, 2 * BLK), 1)
    return (jj > rr) & (jj <= rr + BLK) & (jj >= first_block_bound)


def _dup_kv(slab, lo):
    rolled = pltpu.roll(slab, HEAD, 1)
    return jnp.where(lo, slab, rolled).astype(BF16), jnp.where(lo, rolled, slab).astype(BF16)


def _stack_heads(ref, sb, slab, lo, dtype):
    kh, base = sb // 2, 2 * (sb % 2) * BLK
    zero = jnp.zeros_like(slab)
    ref[kh, base:base + BLK, :] = jnp.where(lo, slab, zero).astype(dtype)
    ref[kh, base + BLK:base + 2 * BLK, :] = jnp.where(lo, zero, slab).astype(dtype)


def _unstack_heads(ref, sb, lo):
    kh, base = sb // 2, 2 * (sb % 2) * BLK
    return jnp.where(lo, ref[kh, base:base + BLK, :], ref[kh, base + BLK:base + 2 * BLK, :])


def _sink_column(sinks_ref, kh):
    row = lax.broadcasted_iota(jnp.int32, (Q_PER_KV * BLK, 1), 0)
    col = jnp.full(row.shape, sinks_ref[Q_PER_KV * kh + Q_PER_KV - 1], F32)
    for n in range(Q_PER_KV - 2, -1, -1):
        col = jnp.where(row < (n + 1) * BLK, sinks_ref[Q_PER_KV * kh + n], col)
    return col


def _tril():
    t = lax.broadcasted_iota(jnp.int32, (BLK, BLK), 0)
    s = lax.broadcasted_iota(jnp.int32, (BLK, BLK), 1)
    return s <= t


def _layer_norm_fwd(va, lg, lb):
    mu = jnp.mean(va, axis=-1, keepdims=True)
    xc = va - mu
    rstd = lax.rsqrt(jnp.mean(xc * xc, axis=-1, keepdims=True) + EPS)
    vhat = xc * rstd
    return vhat, rstd, vhat * lg + lb


def _softmax_sink(qm, kdup, bias, sink):
    s = lax.dot_general(qm, kdup, NT, preferred_element_type=F32) + bias
    m = jnp.maximum(jnp.max(s, axis=-1, keepdims=True), sink)
    p = jnp.exp(s - m)
    esink = jnp.exp(sink - m)
    inv = 1.0 / (jnp.sum(p, axis=-1, keepdims=True) + esink)
    return p * inv, esink * inv


def _band_bias(bias_ref):
    rows = bias_ref.shape[1]
    bias_ref[0] = jnp.where(_band_valid(BLK, rows), 0.0, NEG)
    bias_ref[1] = jnp.where(_band_valid(0, rows), 0.0, NEG)


def _rowmat_call(c_all, w, b, name):
    n = w.shape[1]
    tn = 512

    def body(c_ref, w_ref, b_ref, o_ref, ca_ref):
        ca, _ = _silu_parts(c_ref[...])
        ca_ref[...] = ca
        o_ref[...] = jnp.dot(ca.astype(BF16), w_ref[...].astype(BF16), preferred_element_type=F32) + b_ref[...]

    return pl.pallas_call(
        body, name=name, grid=(n // tn,),
        in_specs=[pl.BlockSpec((N_DEV, D), lambda j: (0, 0)), pl.BlockSpec((D, tn), lambda j: (0, j)),
                  pl.BlockSpec((1, tn), lambda j: (0, j))],
        out_specs=[pl.BlockSpec((N_DEV, tn), lambda j: (0, j)), pl.BlockSpec((N_DEV, D), lambda j: (0, 0))],
        out_shape=[jax.ShapeDtypeStruct((N_DEV, n), F32), jax.ShapeDtypeStruct((N_DEV, D), F32)],
        compiler_params=_params("arbitrary"),
    )(c_all, w, b)


W_IN_PARTS = ((0, 768), (768, 640))
OUT_STREAMS = 4
X_STREAMS = 4


def _proj_gather_call(pos, x, shift, scale, norm_g, wi_full, wo_full):
    s = x.shape[0]
    tm = min(s, 512)
    nrow = s // tm
    hi = D // 2
    ho = W_OUT_SHARD // 2
    phases = [(0, None), (1, 0), (2, 0), (1, 1), (2, 1), (3, 0), (3, 1)]

    def body(pos_ref, *refs):
        x_refs = refs[:X_STREAMS]
        (sh_ref, sc_ref, g_ref, _, _, h_ref, proj_ref, fi_ref, fo_ref,
         h_all, wbuf, obuf, send_sems, recv_sems, load_sems, out_sems) = refs[X_STREAMS:]
        p = pl.program_id(0)
        i = pl.program_id(1)
        x_, y_, c_ = _coords()
        me, sibling = (x_, y_, c_), (x_, y_, 1 - c_)

        def shard_of(q):
            px, py, _ = _peer(x_, y_, c_, q, 0)
            return 2 * px + py

        def cols_of(q, cp):
            off, w = (0, W_IN_SHARD) if cp is None else W_IN_PARTS[cp]
            return shard_of(q) * W_IN_SHARD + off, w

        def part(which, q, pc, sub, cp):
            n = hi if which == 0 else ho
            base = pc * n
            if sub is not None:
                n //= 2
                base = base + sub * n
            if which == 0:
                c0, w = cols_of(q, cp)
                return fi_ref.at[pl.ds(base, n), pl.ds(c0, w)]
            return fo_ref.at[pl.ds(shard_of(q) * W_OUT_SHARD + base, n), :]

        def copy(k, ref, to):
            return pltpu.make_async_remote_copy(src_ref=ref, dst_ref=ref, send_sem=send_sems.at[k], recv_sem=recv_sems.at[k],
                                                device_id=to, device_id_type=MESH)

        def sem(which, kind, j, cp):
            return 4 * kind + 2 * cp + j if which == 0 else 16 + 2 * kind + j

        def to_neighbour(which, q, cp=None):
            return copy(sem(which, 0, q - 1, cp), part(which, 0, c_, None, cp), _peer(x_, y_, c_, q, 0))

        def from_neighbour(which, q, cp=None):
            return copy(sem(which, 0, q - 1, cp), part(which, q, c_, None, cp), me)

        def relay(which, q, cp=None):
            return copy(sem(which, 1, q - 1, cp), part(which, q, c_, q - 1, cp), _peer(x_, y_, c_, 3 - q, 0))

        def relayed(which, sub, cp=None):
            return copy(sem(which, 1, sub, cp), part(which, 3, c_, sub, cp), me)

        def to_sibling(which, q, cp=None):
            return copy(sem(which, 2, q - 1, cp), part(which, q, c_, None, cp), sibling)

        def from_sibling(which, q, cp=None):
            return copy(sem(which, 2, q - 1, cp), part(which, q, 1 - c_, None, cp), me)

        def relayed_to_sibling(which, sub, cp=None):
            return copy(sem(which, 3, sub, cp), part(which, 3, c_, sub, cp), sibling)

        def relayed_from_sibling(which, sub, cp=None):
            return copy(sem(which, 3, sub, cp), part(which, 3, 1 - c_, sub, cp), me)

        def pass_on_neighbours(which, cp=None):
            for q in (1, 2):
                from_neighbour(which, q, cp).wait_recv()
                to_sibling(which, q, cp).start()
                relay(which, q, cp).start()

        def pass_on_relayed(which, cp=None):
            for sub in range(2):
                relayed(which, sub, cp).wait_recv()
                relayed_to_sibling(which, sub, cp).start()

        def shard_load(k):
            c0, w = cols_of(*phases[k])
            return pltpu.make_async_copy(fi_ref.at[:, pl.ds(c0, w)], wbuf.at[k % 2, :, 0:w], load_sems.at[k % 2])

        class OutCopies:
            def __init__(self, k, slot, row0):
                c0, w = cols_of(*phases[k])
                strip = tm // OUT_STREAMS
                self.copies = [pltpu.make_async_copy(obuf.at[slot, n * strip:(n + 1) * strip, 0:w],
                                                     proj_ref.at[pl.ds(row0 + n * strip, strip), pl.ds(c0, w)],
                                                     out_sems.at[slot, n]) for n in range(OUT_STREAMS)]

            def start(self):
                for cp in self.copies:
                    cp.start()

            def wait(self):
                for cp in self.copies:
                    cp.wait()

        out_copy = OutCopies

        def drain(k):
            for j in range(min(2, nrow)):
                out_copy(k, (nrow - 1 - j) % 2, 0).wait()

        def arrivals(k):
            q, cp = phases[k]
            if k == 0:
                for cp_ in range(2):
                    for q_ in (1, 2):
                        to_neighbour(0, q_, cp_).start()
            elif q < 3 and k in (1, 3):
                pass_on_neighbours(0, cp)
                if k == 1:
                    for q_ in (1, 2):
                        to_neighbour(1, q_).start()
            elif k == 5:
                for cp_ in range(2):
                    pass_on_relayed(0, cp_)
                pass_on_neighbours(1)
            if q in (1, 2):
                from_sibling(0, q, cp).wait_recv()
            elif q == 3:
                for sub in range(2):
                    relayed_from_sibling(0, sub, cp).wait_recv()

        rows = pl.ds(pl.multiple_of(i * tm, tm), tm)
        slot = i % 2
        for k, (q, cp) in enumerate(phases):
            @pl.when(p == k)
            def _(k=k, q=q, cp=cp):
                @pl.when(i == 0)
                def _():
                    if k == 0:
                        arrivals(0)
                        shard_load(0).start()
                    else:
                        drain(k - 1)
                    shard_load(k).wait()

                if k + 1 < len(phases):
                    @pl.when(i == max(nrow - 2, 0))
                    def _():
                        arrivals(k + 1)
                        shard_load(k + 1).start()

                if k == 0:
                    wx = D // X_STREAMS
                    ssq = sum(jnp.sum(xr[...] * xr[...], axis=-1, keepdims=True) for xr in x_refs)
                    r = lax.rsqrt(ssq * (1.0 / D) + EPS)
                    for n, xr in enumerate(x_refs):
                        cols = slice(n * wx, (n + 1) * wx)
                        hv = ((xr[...] * r * g_ref[:, cols]) * (1.0 + sc_ref[:, cols]) + sh_ref[:, cols]).astype(BF16)
                        h_ref[:, cols] = hv
                        h_all[rows, cols] = hv

                @pl.when(i >= 2)
                def _():
                    out_copy(k, slot, 0).wait()

                w = cols_of(q, cp)[1]
                obuf[slot, :, 0:w] = jnp.dot(h_all[rows, :], wbuf[k % 2, :, 0:w], preferred_element_type=F32)
                out_copy(k, slot, pl.multiple_of(i * tm, tm)).start()

        @pl.when((p == len(phases) - 1) & (i == nrow - 1))
        def _():
            drain(len(phases) - 1)
            pass_on_relayed(1)
            for q in (1, 2):
                from_sibling(1, q).wait_recv()
            for sub in range(2):
                relayed_from_sibling(1, sub).wait_recv()
            for which, cps in ((0, (0, 1)), (1, (None,))):
                for cp in cps:
                    for q in (1, 2):
                        to_neighbour(which, q, cp).wait_send()
                        relay(which, q, cp).wait_send()
                        to_sibling(which, q, cp).wait_send()
                        relayed_to_sibling(which, q - 1, cp).wait_send()

    vec = pl.BlockSpec((1, D), lambda p, i, pos: (0, 0))
    first_phase_rows = lambda p, i, pos: (jnp.where(p == 0, i, nrow - 1), 0)
    anyspec = pl.BlockSpec(memory_space=pl.ANY)
    x_spec = lambda n: pl.BlockSpec((tm, D // X_STREAMS), lambda p, i, pos: (jnp.where(p == 0, i, nrow - 1), n))
    return pl.pallas_call(
        body, name="proj_gather",
        grid_spec=pltpu.PrefetchScalarGridSpec(
            num_scalar_prefetch=1, grid=(len(phases), nrow),
            in_specs=[x_spec(n) for n in range(X_STREAMS)] + [vec, vec, vec, anyspec, anyspec],
            out_specs=[pl.BlockSpec((tm, D), first_phase_rows), anyspec, anyspec, anyspec],
            scratch_shapes=[pltpu.VMEM((s, D), BF16), pltpu.VMEM((2, D, W_IN_SHARD), BF16), pltpu.VMEM((2, tm, W_IN_SHARD), F32),
                            pltpu.SemaphoreType.DMA((24,)), pltpu.SemaphoreType.DMA((24,)), pltpu.SemaphoreType.DMA((2,)),
                            pltpu.SemaphoreType.DMA((2, OUT_STREAMS))]),
        out_shape=[jax.ShapeDtypeStruct((s, D), BF16), jax.ShapeDtypeStruct((s, D_IN), F32),
                   jax.ShapeDtypeStruct((D, D_IN), BF16), jax.ShapeDtypeStruct((D, D), BF16)],
        input_output_aliases={X_STREAMS + 4: 2, X_STREAMS + 5: 3},
        compiler_params=_params("arbitrary", "arbitrary"),
    )(pos, *([x] * X_STREAMS), shift, scale, norm_g, wi_full, wo_full)


def _proj_specs(rev_nb=None):
    if rev_nb is None:
        row = lambda i: i
    else:
        row = lambda i: rev_nb - 1 - i
    wide = lambda col: pl.BlockSpec((BLK, D_A), lambda i: (row(i), col))
    kv = lambda col: pl.BlockSpec((BLK, D_KV), lambda i: (row(i), col))
    half = lambda col: pl.BlockSpec((BLK, 512), lambda i: (row(i), col))
    return [wide(0), wide(1), wide(2), wide(3), kv(OFF_K // D_KV), kv(OFF_V // D_KV), half(OFF_ZB // 512), half(OFF_ZB // 512 + 1)]


def _mix_fwd_call(proj, cos, sin, ln_g, ln_b, w_sp, b_sp_t, sinks):
    s = proj.shape[0]
    nb = s // BLK

    def body(ua_ref, va_ref, za_ref, q_ref, k_ref, v_ref, zb0_ref, zb1_ref, cos_ref, sin_ref, lg_ref, lb_ref,
             w_ref, bt_ref, sinks_ref, y_ref, probs_ref, ost_ref, psink_ref, kdup_ref, vdup_ref, qm_ref, bias_ref):
        i = pl.program_id(0)
        first_half, lo = _lane_masks()
        cos_t = cos_ref[...]
        sin_t = sin_ref[...]

        _, _, vln = _layer_norm_fwd(va_ref[...], lg_ref[...], lb_ref[...])
        tril = _tril()
        for g in range(GROUPS):
            cols = slice(g * BLK, (g + 1) * BLK)
            wg = jnp.where(tril, w_ref[g], 0.0).astype(BF16)
            sg = jnp.dot(wg, vln[:, cols].astype(BF16), preferred_element_type=F32) + bt_ref[:, g:g + 1]
            gate, _ = _silu_parts(za_ref[:, cols])
            y_ref[:, cols] = (ua_ref[:, cols] * sg * gate).astype(BF16)

        @pl.when(i == 0)
        def _():
            kdup_ref[:, 0:BLK, :] = jnp.zeros((N_KV, BLK, LANE), BF16)
            vdup_ref[:, 0:BLK, :] = jnp.zeros((N_KV, BLK, LANE), BF16)
            _band_bias(bias_ref)

        @pl.when(i > 0)
        def _():
            kdup_ref[:, 0:BLK, :] = kdup_ref[:, BLK:2 * BLK, :]
            vdup_ref[:, 0:BLK, :] = vdup_ref[:, BLK:2 * BLK, :]

        for ks in range(2):
            cols = slice(ks * LANE, (ks + 1) * LANE)
            kr = _rope(k_ref[:, cols], cos_t, sin_t, first_half)
            for n, (kd, vd) in enumerate(zip(_dup_kv(kr, lo), _dup_kv(v_ref[:, cols], lo))):
                kdup_ref[2 * ks + n, BLK:2 * BLK, :] = kd
                vdup_ref[2 * ks + n, BLK:2 * BLK, :] = vd
        for sb in range(8):
            _stack_heads(qm_ref, sb, _rope(q_ref[:, sb * LANE:(sb + 1) * LANE], cos_t, sin_t, first_half) * SCALE, lo, BF16)

        block_kind = jnp.where(i > 0, 1, 0)

        psink_ref[...] = jnp.zeros((Q_PER_KV * BLK, LANE), F32)
        lane_q = lax.broadcasted_iota(jnp.int32, (Q_PER_KV * BLK, LANE), 1)

        def kv_head(kh, carry):
            probs, psink = _softmax_sink(qm_ref[kh], kdup_ref[kh], bias_ref[block_kind], _sink_column(sinks_ref, kh))
            probs_ref[kh] = probs
            psink_ref[...] = jnp.where(lane_q == kh, psink, psink_ref[...])
            ost_ref[kh] = jnp.dot(probs.astype(BF16), vdup_ref[kh], preferred_element_type=F32)
            return carry

        lax.fori_loop(0, N_KV, kv_head, 0, unroll=2)
        for sb in range(8):
            cols = slice(sb * LANE, (sb + 1) * LANE)
            zb = zb0_ref[:, cols] if sb < 4 else zb1_ref[:, (sb - 4) * LANE:(sb - 3) * LANE]
            gate, _ = _silu_parts(zb)
            y_ref[:, D_A + sb * LANE:D_A + (sb + 1) * LANE] = (_unstack_heads(ost_ref, sb, lo) * gate).astype(BF16)

    tab = pl.BlockSpec((BLK, LANE), lambda i: (i, 0))
    return pl.pallas_call(
        body, name="mix_fwd", grid=(nb,),
        in_specs=_proj_specs() + [
            tab, tab, pl.BlockSpec((1, D_A), lambda i: (0, 0)), pl.BlockSpec((1, D_A), lambda i: (0, 0)),
            pl.BlockSpec((GROUPS, BLK, BLK), lambda i: (0, 0, 0)), pl.BlockSpec((BLK, GROUPS), lambda i: (0, 0)),
            pl.BlockSpec(memory_space=pltpu.SMEM)],
        out_specs=[pl.BlockSpec((BLK, 2 * D_A), lambda i: (i, 0)),
                   pl.BlockSpec((None, N_KV, Q_PER_KV * BLK, 2 * BLK), lambda i: (i, 0, 0, 0)),
                   pl.BlockSpec((None, N_KV, Q_PER_KV * BLK, LANE), lambda i: (i, 0, 0, 0)),
                   pl.BlockSpec((None, Q_PER_KV * BLK, LANE), lambda i: (i, 0, 0))],
        out_shape=[jax.ShapeDtypeStruct((s, 2 * D_A), BF16), jax.ShapeDtypeStruct((nb, N_KV, Q_PER_KV * BLK, 2 * BLK), F32),
                   jax.ShapeDtypeStruct((nb, N_KV, Q_PER_KV * BLK, LANE), F32), jax.ShapeDtypeStruct((nb, Q_PER_KV * BLK, LANE), F32)],
        scratch_shapes=[pltpu.VMEM((N_KV, 2 * BLK, LANE), BF16), pltpu.VMEM((N_KV, 2 * BLK, LANE), BF16),
                        pltpu.VMEM((N_KV, Q_PER_KV * BLK, LANE), BF16), pltpu.VMEM((2, Q_PER_KV * BLK, 2 * BLK), F32)],
        compiler_params=_params("arbitrary"),
    )(proj, proj, proj, proj, proj, proj, proj, proj, cos, sin, ln_g, ln_b, w_sp, b_sp_t, sinks)


def _tail_call(y, w_out_bf, x, target, gate, shift_f, scale_f, gf):
    s = x.shape[0]
    tm = min(s, 256)
    nsteps = s // tm

    def body(y_ref, w_ref, x_ref, t_ref, gate_ref, shf_ref, scf_ref, gf_ref, dx2_ref, do_ref, dy_ref, st_ref):
        i = pl.program_id(0)

        @pl.when(i == 0)
        def _():
            st_ref[...] = jnp.zeros((8, D), F32)

        o = jnp.dot(y_ref[...], w_ref[...], preferred_element_type=F32)
        gate_v = gate_ref[...]
        x2 = x_ref[...] + gate_v * o
        r2 = lax.rsqrt(jnp.mean(x2 * x2, axis=-1, keepdims=True) + EPS)
        xn2 = x2 * r2
        hn2 = xn2 * gf_ref[...]
        one_sc = 1.0 + scf_ref[...]
        err = hn2 * one_sc + shf_ref[...] - t_ref[...]
        dout = err * (1.0 / D)
        dhn2 = dout * one_sc
        dxn2 = dhn2 * gf_ref[...]
        dx2 = r2 * (dxn2 - xn2 * jnp.mean(dxn2 * xn2, axis=-1, keepdims=True))
        dx2_ref[...] = dx2
        do = (dx2 * gate_v).astype(BF16)
        do_ref[...] = do
        dy_ref[...] = lax.dot_general(do, w_ref[...], NT, preferred_element_type=F32)
        st_ref[0:1, :] += jnp.sum(dout, axis=0, keepdims=True)
        st_ref[1:2, :] += jnp.sum(dout * hn2, axis=0, keepdims=True)
        st_ref[2:3, :] += jnp.sum(dhn2 * xn2, axis=0, keepdims=True)
        st_ref[3:4, :] += jnp.sum(dx2 * o, axis=0, keepdims=True)
        st_ref[4:5, :] += jnp.sum(err * err, axis=0, keepdims=True)

        @pl.when(i == nsteps - 1)
        def _():
            st_ref[5:6, :] = jnp.full((1, D), 0.5 / D, F32) * jnp.sum(st_ref[4:5, :])

    vec = pl.BlockSpec((1, D), lambda i: (0, 0))
    rows = lambda: pl.BlockSpec((tm, D), lambda i: (i, 0))
    return pl.pallas_call(
        body, name="tail", grid=(nsteps,),
        in_specs=[rows(), pl.BlockSpec((D, D), lambda i: (0, 0)), rows(), rows(), vec, vec, vec, vec],
        out_specs=[rows(), rows(), rows(), pl.BlockSpec((8, D), lambda i: (0, 0))],
        out_shape=[jax.ShapeDtypeStruct((s, D), F32), jax.ShapeDtypeStruct((s, D), BF16), jax.ShapeDtypeStruct((s, D), F32),
                   jax.ShapeDtypeStruct((8, D), F32)],
        compiler_params=_params("arbitrary"),
    )(y, w_out_bf, x, target, gate, shift_f, scale_f, gf)


def _tn_call(a, b, name):
    s, m = a.shape
    n = b.shape[1]
    tn = 1024
    ts = min(s, 1024)
    nk = s // ts

    def body(a_ref, b_ref, o_ref, acc_ref):
        k = pl.program_id(1)

        @pl.when(k == 0)
        def _():
            acc_ref[...] = jnp.zeros((m, tn), F32)

        acc_ref[...] += lax.dot_general(a_ref[...], b_ref[...], TN, preferred_element_type=F32)

        @pl.when(k == nk - 1)
        def _():
            o_ref[...] = acc_ref[...].astype(BF16)

    return pl.pallas_call(
        body, name=name, grid=(n // tn, nk),
        in_specs=[pl.BlockSpec((ts, m), lambda j, k: (k, 0)), pl.BlockSpec((ts, tn), lambda j, k: (k, j))],
        out_specs=pl.BlockSpec((m, tn), lambda j, k: (0, j)),
        out_shape=jax.ShapeDtypeStruct((m, n), BF16),
        scratch_shapes=[pltpu.VMEM((m, tn), F32)],
        compiler_params=_params("parallel", "arbitrary"),
    )(a, b)


def _tn_shards_call(pos, a, b, qs, name):
    s, m = a.shape
    ts = min(s, 1024)
    nk = s // ts

    def body(pos_ref, a_ref, b_ref, o_ref, acc_ref):
        k = pl.program_id(1)

        @pl.when(k == 0)
        def _():
            acc_ref[...] = jnp.zeros((m, W_IN_SHARD), F32)

        acc_ref[...] += lax.dot_general(a_ref[...], b_ref[...], TN, preferred_element_type=F32)

        @pl.when(k == nk - 1)
        def _():
            o_ref[...] = acc_ref[...].astype(BF16)

    def shard(j, pos):
        q = qs[0]
        for n in range(1, len(qs)):
            q = jnp.where(j == n, qs[n], q)
        return jnp.bitwise_xor(pos[0], q)

    return pl.pallas_call(
        body, name=name,
        grid_spec=pltpu.PrefetchScalarGridSpec(
            num_scalar_prefetch=1, grid=(len(qs), nk),
            in_specs=[pl.BlockSpec((ts, m), lambda j, k, pos: (k, 0)),
                      pl.BlockSpec((ts, W_IN_SHARD), lambda j, k, pos: (k, shard(j, pos)))],
            out_specs=pl.BlockSpec((m, W_IN_SHARD), lambda j, k, pos: (0, j)),
            scratch_shapes=[pltpu.VMEM((m, W_IN_SHARD), F32)]),
        out_shape=jax.ShapeDtypeStruct((m, len(qs) * W_IN_SHARD), BF16),
        compiler_params=_params("parallel", "arbitrary"),
    )(pos, a, b)


def _mix_bwd_call(proj, dy, probs, outs, psinks, tables, ln_g, ln_b, w_sp, w_sp_t, b_sp_t):
    s = proj.shape[0]
    nb = s // BLK
    rev = lambda i: nb - 1 - i
    prev = lambda i: jnp.maximum(nb - 2 - i, 0)

    def body(ua_ref, va_ref, za_ref, q_ref, k_ref, v_ref, zb0_ref, zb1_ref, kp_ref, vp_ref, dy_ref,
             probs_ref, ost_ref, psink_ref, cos_ref, sin_ref, cosp_ref, sinp_ref, lg_ref, lb_ref, w_ref, wt_ref, bt_ref,
             dp_ref, lnst_ref, dw_ref, dbt_ref, dsink_ref,
             kdup_ref, vdup_ref, dvln_ref, qm_ref, dom_ref, dqst_ref, dkdup_ref, dvdup_ref, kcar_ref, vcar_ref, sigb_ref):
        i = pl.program_id(0)
        first_half, lo = _lane_masks()
        lane8 = lax.broadcasted_iota(jnp.int32, (8, LANE), 1)
        cos_t = cos_ref[...]
        sin_t = sin_ref[...]

        @pl.when(i == 0)
        def _():
            lnst_ref[...] = jnp.zeros((8, D_A), F32)
            dw_ref[...] = jnp.zeros((GROUPS, BLK, BLK), F32)
            dbt_ref[...] = jnp.zeros((BLK, LANE), F32)
            dsink_ref[...] = jnp.zeros((8, LANE), F32)
            kcar_ref[...] = jnp.zeros((BLK, D_KV), F32)
            vcar_ref[...] = jnp.zeros((BLK, D_KV), F32)

        vhat, rstd, vln = _layer_norm_fwd(va_ref[...], lg_ref[...], lb_ref[...])
        tril = _tril()
        triu = jnp.logical_not(tril) | (lax.broadcasted_iota(jnp.int32, (BLK, BLK), 0) == lax.broadcasted_iota(jnp.int32, (BLK, BLK), 1))
        lane_b = lax.broadcasted_iota(jnp.int32, (BLK, LANE), 1)
        db_acc = jnp.zeros((BLK, LANE), F32)
        for g in range(GROUPS):
            cols = slice(g * BLK, (g + 1) * BLK)
            vln_g = vln[:, cols].astype(BF16)
            wg = jnp.where(tril, w_ref[g], 0.0).astype(BF16)
            sg = jnp.dot(wg, vln_g, preferred_element_type=F32) + bt_ref[:, g:g + 1]
            za = za_ref[:, cols]
            gate, sig = _silu_parts(za)
            ua = ua_ref[:, cols]
            dya_g = dy_ref[:, cols]
            dya = dya_g * gate
            dp_ref[:, cols] = (dya * sg).astype(BF16)
            dp_ref[:, 2 * D_A + g * BLK:2 * D_A + (g + 1) * BLK] = (
                dya_g * (ua * sg) * (sig * (1.0 + za * (1.0 - sig)))).astype(BF16)
            ds = dya * ua
            ds_b = ds.astype(BF16)
            wtg = jnp.where(triu, wt_ref[g], 0.0).astype(BF16)
            dvln_ref[:, cols] = jnp.dot(wtg, ds_b, preferred_element_type=F32)
            dw_ref[g] += jnp.where(tril, lax.dot_general(ds_b, vln_g, NT, preferred_element_type=F32), 0.0)
            db_acc = db_acc + jnp.where(lane_b == g, jnp.sum(ds, axis=-1, keepdims=True), 0.0)
        dbt_ref[...] += db_acc
        dvln = dvln_ref[...]
        lnst_ref[0:1, :] += jnp.sum(dvln * vhat, axis=0, keepdims=True)
        lnst_ref[1:2, :] += jnp.sum(dvln, axis=0, keepdims=True)
        dvhat = dvln * lg_ref[...]
        m1 = jnp.mean(dvhat, axis=-1, keepdims=True)
        m2 = jnp.mean(dvhat * vhat, axis=-1, keepdims=True)
        dp_ref[:, D_A:2 * D_A] = (rstd * (dvhat - m1 - vhat * m2)).astype(BF16)

        cosp = cosp_ref[...]
        sinp = sinp_ref[...]
        for ks in range(2):
            cols = slice(ks * LANE, (ks + 1) * LANE)
            kr = _rope(k_ref[:, cols], cos_t, sin_t, first_half)
            kpr = _rope(kp_ref[:, cols], cosp, sinp, first_half)
            for n, (kc, vc, kp, vp) in enumerate(zip(_dup_kv(kr, lo), _dup_kv(v_ref[:, cols], lo),
                                                     _dup_kv(kpr, lo), _dup_kv(vp_ref[:, cols], lo))):
                kdup_ref[2 * ks + n, BLK:2 * BLK, :] = kc
                vdup_ref[2 * ks + n, BLK:2 * BLK, :] = vc
                kdup_ref[2 * ks + n, 0:BLK, :] = kp
                vdup_ref[2 * ks + n, 0:BLK, :] = vp
        for sb in range(8):
            cols = slice(sb * LANE, (sb + 1) * LANE)
            _stack_heads(qm_ref, sb, _rope(q_ref[:, cols], cos_t, sin_t, first_half) * SCALE, lo, BF16)
            zb = zb0_ref[:, cols] if sb < 4 else zb1_ref[:, (sb - 4) * LANE:(sb - 3) * LANE]
            gate, sig = _silu_parts(zb)
            sigb_ref[:, cols] = sig
            _stack_heads(dom_ref, sb, dy_ref[:, D_A + sb * LANE:D_A + (sb + 1) * LANE] * gate, lo, F32)

        lane_q = lax.broadcasted_iota(jnp.int32, (Q_PER_KV * BLK, LANE), 1)

        def kv_head(kh, dsink_acc):
            qm = qm_ref[kh]
            kd = kdup_ref[kh]
            vd = vdup_ref[kh]
            probs = probs_ref[kh]
            psink = jnp.sum(jnp.where(lane_q == kh, psink_ref[...], 0.0), axis=-1, keepdims=True)
            probs_b = probs.astype(BF16)
            o = ost_ref[kh]
            dom = dom_ref[kh]
            dom_b = dom.astype(BF16)
            delta = jnp.sum(dom * o, axis=-1, keepdims=True)
            dpr = lax.dot_general(dom_b, vd, NT, preferred_element_type=F32)
            dss = (probs * (dpr - delta)).astype(BF16)
            sd = psink * delta
            for n in range(Q_PER_KV):
                dsink_acc = dsink_acc + jnp.where(lane8 == Q_PER_KV * kh + n, -jnp.sum(sd[n * BLK:(n + 1) * BLK]), 0.0)
            dqst_ref[kh] = jnp.dot(dss, kd, preferred_element_type=F32)
            dkdup_ref[kh] = lax.dot_general(dss, qm, TN, preferred_element_type=F32)
            dvdup_ref[kh] = lax.dot_general(probs_b, dom_b, TN, preferred_element_type=F32)
            return dsink_acc

        dsink_acc = jnp.zeros((8, LANE), F32)
        for kh in range(N_KV):
            dsink_acc = kv_head(kh, dsink_acc)
        row0 = lax.broadcasted_iota(jnp.int32, (8, LANE), 0) == 0
        dsink_ref[...] += jnp.where(row0, dsink_acc, 0.0)

        for sb in range(8):
            cols = slice(sb * LANE, (sb + 1) * LANE)
            zb = zb0_ref[:, cols] if sb < 4 else zb1_ref[:, (sb - 4) * LANE:(sb - 3) * LANE]
            sig = sigb_ref[:, cols]
            dyb = dy_ref[:, D_A + sb * LANE:D_A + (sb + 1) * LANE]
            dp_ref[:, OFF_ZB + sb * LANE:OFF_ZB + (sb + 1) * LANE] = (
                dyb * _unstack_heads(ost_ref, sb, lo) * (sig * (1.0 + zb * (1.0 - sig)))).astype(BF16)
            dq_r = _unstack_heads(dqst_ref, sb, lo) * SCALE
            dp_ref[:, OFF_Q + sb * LANE:OFF_Q + (sb + 1) * LANE] = _unrope(dq_r, cos_t, sin_t, first_half).astype(BF16)

        lo2 = lax.broadcasted_iota(jnp.int32, (2 * BLK, LANE), 1) < HEAD
        for ks in range(2):
            cols = slice(ks * LANE, (ks + 1) * LANE)
            ka = dkdup_ref[2 * ks]
            kb = dkdup_ref[2 * ks + 1]
            dk_band = jnp.where(lo2, ka + pltpu.roll(ka, HEAD, 1), kb + pltpu.roll(kb, HEAD, 1))
            va_ = dvdup_ref[2 * ks]
            vb_ = dvdup_ref[2 * ks + 1]
            dv_band = jnp.where(lo2, va_ + pltpu.roll(va_, HEAD, 1), vb_ + pltpu.roll(vb_, HEAD, 1))
            dkr = dk_band[BLK:2 * BLK, :] + kcar_ref[:, cols]
            dp_ref[:, OFF_K + ks * LANE:OFF_K + (ks + 1) * LANE] = _unrope(dkr, cos_t, sin_t, first_half).astype(BF16)
            dp_ref[:, OFF_V + ks * LANE:OFF_V + (ks + 1) * LANE] = (
                dv_band[BLK:2 * BLK, :] + vcar_ref[:, cols]).astype(BF16)
            kcar_ref[:, cols] = dk_band[0:BLK, :]
            vcar_ref[:, cols] = dv_band[0:BLK, :]

    tab = pl.BlockSpec((BLK, LANE), lambda i: (rev(i), 0))
    kvp = lambda col: pl.BlockSpec((BLK, D_KV), lambda i: (prev(i), col))
    vec = pl.BlockSpec((1, D_A), lambda i: (0, 0))
    w3 = pl.BlockSpec((GROUPS, BLK, BLK), lambda i: (0, 0, 0))
    return pl.pallas_call(
        body, name="mix_bwd", grid=(nb,),
        in_specs=_proj_specs(nb) + [
            kvp(OFF_K // D_KV), kvp(OFF_V // D_KV), pl.BlockSpec((BLK, 2 * D_A), lambda i: (rev(i), 0)),
            pl.BlockSpec((None, N_KV, Q_PER_KV * BLK, 2 * BLK), lambda i: (rev(i), 0, 0, 0)),
            pl.BlockSpec((None, N_KV, Q_PER_KV * BLK, LANE), lambda i: (rev(i), 0, 0, 0)),
            pl.BlockSpec((None, Q_PER_KV * BLK, LANE), lambda i: (rev(i), 0, 0)),
            tab, tab, tab, tab, vec, vec, w3, w3, pl.BlockSpec((BLK, GROUPS), lambda i: (0, 0))],
        out_specs=[pl.BlockSpec((BLK, D_IN), lambda i: (rev(i), 0)), pl.BlockSpec((8, D_A), lambda i: (0, 0)), w3,
                   pl.BlockSpec((BLK, LANE), lambda i: (0, 0)), pl.BlockSpec((8, LANE), lambda i: (0, 0))],
        out_shape=[jax.ShapeDtypeStruct((s, D_IN), BF16), jax.ShapeDtypeStruct((8, D_A), F32),
                   jax.ShapeDtypeStruct((GROUPS, BLK, BLK), F32), jax.ShapeDtypeStruct((BLK, LANE), F32),
                   jax.ShapeDtypeStruct((8, LANE), F32)],
        scratch_shapes=[pltpu.VMEM((N_KV, 2 * BLK, LANE), BF16), pltpu.VMEM((N_KV, 2 * BLK, LANE), BF16),
                        pltpu.VMEM((BLK, D_A), F32), pltpu.VMEM((N_KV, Q_PER_KV * BLK, LANE), BF16),
                        pltpu.VMEM((N_KV, Q_PER_KV * BLK, LANE), F32), pltpu.VMEM((N_KV, Q_PER_KV * BLK, LANE), F32),
                        pltpu.VMEM((N_KV, 2 * BLK, LANE), F32), pltpu.VMEM((N_KV, 2 * BLK, LANE), F32),
                        pltpu.VMEM((BLK, D_KV), F32), pltpu.VMEM((BLK, D_KV), F32), pltpu.VMEM((BLK, D_B), F32)],
        compiler_params=_params("arbitrary"),
    )(proj, proj, proj, proj, proj, proj, proj, proj, proj, proj, dy, probs, outs, psinks, *tables, ln_g, ln_b,
      w_sp, w_sp_t, b_sp_t)


def _dh_call(dproj, w_bf, x, dx2, scale, norm_g):
    s = x.shape[0]
    tm = min(s, 512)
    tk = W_IN_SHARD
    nk = D_IN // tk

    def body(dp_ref, w_ref, x_ref, dx2_ref, sc_ref, g_ref, gx_ref, st_ref, acc_ref):
        i = pl.program_id(0)
        k = pl.program_id(1)

        @pl.when((i == 0) & (k == 0))
        def _():
            st_ref[...] = jnp.zeros((8, D), F32)

        @pl.when(k == 0)
        def _():
            acc_ref[...] = jnp.zeros((tm, D), F32)

        acc_ref[...] += lax.dot_general(dp_ref[...], w_ref[...], NT, preferred_element_type=F32)

        @pl.when(k == nk - 1)
        def _():
            g = g_ref[...]
            one_sc = 1.0 + sc_ref[...]

            def chunk(n, carry):
                rows = pl.ds(pl.multiple_of(n * BLK, BLK), BLK)
                dh = acc_ref[rows, :]
                xv = x_ref[rows, :]
                r = lax.rsqrt(jnp.mean(xv * xv, axis=-1, keepdims=True) + EPS)
                xn = xv * r
                dhn = dh * one_sc
                dxn = dhn * g
                gx_ref[rows, :] = dx2_ref[rows, :] + r * (dxn - xn * jnp.mean(dxn * xn, axis=-1, keepdims=True))
                st_ref[0:1, :] += jnp.sum(dh, axis=0, keepdims=True)
                st_ref[1:2, :] += jnp.sum(dh * (xn * g), axis=0, keepdims=True)
                st_ref[2:3, :] += jnp.sum(dhn * xn, axis=0, keepdims=True)
                return carry

            lax.fori_loop(0, tm // BLK, chunk, 0)

    vec = pl.BlockSpec((1, D), lambda i, k: (0, 0))
    rows = lambda: pl.BlockSpec((tm, D), lambda i, k: (i, 0))
    return pl.pallas_call(
        body, name="dh", grid=(s // tm, nk),
        in_specs=[pl.BlockSpec((tm, tk), lambda i, k: (i, k)), pl.BlockSpec((D, tk), lambda i, k: (0, k)), rows(), rows(), vec, vec],
        out_specs=[rows(), pl.BlockSpec((8, D), lambda i, k: (0, 0))],
        out_shape=[jax.ShapeDtypeStruct((s, D), F32), jax.ShapeDtypeStruct((8, D), F32)],
        scratch_shapes=[pltpu.VMEM((tm, D), F32)],
        compiler_params=_params("arbitrary", "arbitrary"),
    )(dproj, w_bf, x, dx2, scale, norm_g)


def _adam_math(w, g, m, v):
    m_new = ADAM_B1 * m + (1.0 - ADAM_B1) * g
    v_new = ADAM_B2 * v + (1.0 - ADAM_B2) * (g * g)
    m_hat = m_new / ADAM_C1
    v_hat = v_new / ADAM_C2
    delta = -ADAM_LR * (m_hat / (jnp.sqrt(v_hat) + ADAM_EPS) + ADAM_WD * w)
    return delta, m_new, v_new


def _adam_small_call(tensors):
    n = len(tensors)

    def body(*refs):
        ins, outs = refs[:4 * n], refs[4 * n:]
        for t in range(n):
            w_ref, g_ref, m_ref, v_ref = ins[4 * t:4 * t + 4]
            d, mo, vo = _adam_math(w_ref[...], g_ref[...], m_ref[...], v_ref[...])
            outs[3 * t][...], outs[3 * t + 1][...], outs[3 * t + 2][...] = d, mo, vo

    vm = pl.BlockSpec(memory_space=pltpu.VMEM)
    flat = [a for t in tensors for a in t]
    out = pl.pallas_call(
        body, name="adam_small", in_specs=[vm] * (4 * n), out_specs=[vm] * (3 * n),
        out_shape=[jax.ShapeDtypeStruct(t[0].shape, F32) for t in tensors for _ in range(3)],
        compiler_params=pltpu.CompilerParams(vmem_limit_bytes=VMEM_LIMIT),
    )(*flat)
    return [tuple(out[3 * t:3 * t + 3]) for t in range(n)]


def _adam_halves_call(pos, w, mine, theirs, m, v, name):
    r, n = w.shape
    half = r // 2
    tr = ADAM_ROWS
    nh = half // tr

    def body(pos_ref, w_ref, mine_ref, theirs_ref, m_ref, v_ref, g_ref, d_ref, mo_ref, vo_ref):
        is_mine = (pl.program_id(0) // nh) == pos_ref[1]
        g = jnp.where(is_mine, mine_ref[...], theirs_ref[...])
        g_ref[...] = g
        d_ref[...], mo_ref[...], vo_ref[...] = _adam_math(w_ref[...], g, m_ref[...], v_ref[...])

    spec = lambda: pl.BlockSpec((tr, n), lambda i, pos: (i, 0))

    def half_spec(core_of_half):
        def index(i, pos):
            first = core_of_half(pos) == 0
            active = (i // nh == 0) == first
            return jnp.where(active, i % nh, jnp.where(first, nh - 1, 0)), 0
        return pl.BlockSpec((tr, n), index)

    return pl.pallas_call(
        body, name=name,
        grid_spec=pltpu.PrefetchScalarGridSpec(
            num_scalar_prefetch=1, grid=(r // tr,),
            in_specs=[spec(), half_spec(lambda pos: pos[1]), half_spec(lambda pos: 1 - pos[1]), spec(), spec()],
            out_specs=[spec() for _ in range(4)]),
        out_shape=[jax.ShapeDtypeStruct((r, n), F32)] * 4, compiler_params=_params("arbitrary"),
    )(pos, w, mine, theirs, m, v)


def _adam_outer_call(w, ct, dm, m, v, name):
    r, n = w.shape
    tr = ADAM_ROWS

    def body(w_ref, ct_ref, dm_ref, m_ref, v_ref, g_ref, d_ref, mo_ref, vo_ref):
        g = ct_ref[:, 0:1] * dm_ref[0:1, :]
        for b in range(1, N_DEV):
            g = g + ct_ref[:, b:b + 1] * dm_ref[b:b + 1, :]
        g_ref[...] = g
        d_ref[...], mo_ref[...], vo_ref[...] = _adam_math(w_ref[...], g, m_ref[...], v_ref[...])

    spec = lambda: pl.BlockSpec((tr, n), lambda i: (i, 0))
    return pl.pallas_call(
        body, name=name, grid=(r // tr,),
        in_specs=[spec(), pl.BlockSpec((tr, N_DEV), lambda i: (i, 0)), pl.BlockSpec((N_DEV, n), lambda i: (0, 0)), spec(), spec()],
        out_specs=[spec() for _ in range(4)],
        out_shape=[jax.ShapeDtypeStruct((r, n), F32)] * 4, compiler_params=_params("parallel"),
    )(w, ct, dm, m, v)


def _sum_pieces_call(pos, part, part_block, recvs, name):
    r, n = recvs[0].shape[1:]
    tr = min(r, 256)
    nrb = r // tr

    def body(pos_ref, p_ref, *refs):
        acc = p_ref[...].astype(F32)
        for r_ref in refs[:-1]:
            for d in range(r_ref.shape[0]):
                acc = acc + r_ref[d].astype(F32)
        refs[-1][...] = acc

    return pl.pallas_call(
        body, name=name,
        grid_spec=pltpu.PrefetchScalarGridSpec(
            num_scalar_prefetch=1, grid=(nrb,),
            in_specs=[pl.BlockSpec((tr, n), lambda i, pos: part_block(i, pos, nrb))] + [
                pl.BlockSpec((rv.shape[0], tr, n), lambda i, pos: (0, i, 0)) for rv in recvs],
            out_specs=pl.BlockSpec((tr, n), lambda i, pos: (i, 0))),
        out_shape=jax.ShapeDtypeStruct((r, n), F32), compiler_params=_params("parallel"),
    )(pos, part, *recvs)


def _coords():
    return lax.axis_index("x"), lax.axis_index("y"), lax.axis_index("c")


CAST_ROWS = 256


def _allgather_sum_call(blk, name, with_sum, cast=None):
    m_per, n = blk.shape
    n_out = 2 if with_sum else 1
    if cast is not None:
        w, full_shape = cast
        wr, wn = w.shape
        by_cols = full_shape[0] == wr
        tr = min(wr, CAST_ROWS)
        n_chunk = wr // tr

    def body(*refs):
        x_ref = refs[0]
        out_ref = refs[1 + (cast is not None)]
        rest = refs[1 + (cast is not None) + n_out + (cast is not None):]
        send_sems, recv_sems, local_sem = rest[:3]
        x, y, c = _coords()
        me, sibling = (x, y, c), (x, y, 1 - c)
        chips = [(1 - x, y), (x, 1 - y), (1 - x, 1 - y)]

        def rows(px, py, pc):
            return out_ref.at[pl.ds((4 * px + 2 * py + pc) * m_per, m_per), :]

        def copy(k, block, to, src=None):
            return pltpu.make_async_remote_copy(
                src_ref=rows(*block) if src is None else src, dst_ref=rows(*block),
                send_sem=send_sems.at[k], recv_sem=recv_sems.at[k], device_id=to, device_id_type=MESH)

        mine = pltpu.make_async_copy(x_ref, rows(*me), local_sem)
        mine.start()
        first = [copy(0, me, sibling, src=x_ref)]
        first += [copy(1 + j, me, (*chip, c), src=x_ref) for j, chip in enumerate(chips)]
        for cp in first:
            cp.start()

        if cast is not None:
            w_ref, full_ref = refs[1], refs[1 + 1 + n_out]
            f32_buf, bf16_buf, in_sems, out_sems = rest[3:]
            chip_no = 2 * x + y

            def fetch(i):
                return pltpu.make_async_copy(w_ref.at[pl.ds(i * tr, tr), :], f32_buf.at[i % 2], in_sems.at[i % 2])

            def store(i):
                if by_cols:
                    dst = full_ref.at[pl.ds(i * tr, tr), pl.ds(chip_no * wn, wn)]
                else:
                    dst = full_ref.at[pl.ds(chip_no * wr + i * tr, tr), :]
                return pltpu.make_async_copy(bf16_buf.at[i % 2], dst, out_sems.at[i % 2])

            fetch(0).start()
            for i in range(n_chunk):
                if i + 1 < n_chunk:
                    fetch(i + 1).start()
                fetch(i).wait()
                if i >= 2:
                    store(i - 2).wait()
                bf16_buf[i % 2] = f32_buf[i % 2].astype(BF16)
                store(i).start()
            for i in range(max(n_chunk - 2, 0), n_chunk):
                store(i).wait()

        passed = [copy(4 + j, (*chip, c), sibling) for j, chip in enumerate(chips)]
        for j, chip in enumerate(chips):
            copy(1 + j, (*chip, c), me).wait_recv()
            passed[j].start()
        copy(0, sibling, me).wait_recv()
        for j, chip in enumerate(chips):
            copy(4 + j, (*chip, 1 - c), me).wait_recv()
        for cp in first + passed:
            cp.wait_send()
        mine.wait()
        if with_sum:
            sum_ref = refs[1 + (cast is not None) + 1]
            acc = out_ref[0:m_per, :]
            for d in range(1, N_DEV):
                acc = acc + out_ref[d * m_per:(d + 1) * m_per, :]
            sum_ref[...] = acc

    vm = pl.BlockSpec(memory_space=pltpu.VMEM)
    anyspec = pl.BlockSpec(memory_space=pl.ANY)
    out_shape = [jax.ShapeDtypeStruct((N_DEV * m_per, n), F32)]
    if with_sum:
        out_shape.append(jax.ShapeDtypeStruct((m_per, n), F32))
    in_specs, out_specs, operands = [vm], [vm] * n_out, [blk]
    scratch = [pltpu.SemaphoreType.DMA((7,)), pltpu.SemaphoreType.DMA((7,)), pltpu.SemaphoreType.DMA]
    if cast is not None:
        in_specs.append(anyspec)
        operands.append(w)
        out_shape.append(jax.ShapeDtypeStruct(full_shape, BF16))
        out_specs.append(anyspec)
        scratch += [pltpu.VMEM((2, tr, wn), F32), pltpu.VMEM((2, tr, wn), BF16), pltpu.SemaphoreType.DMA((2,)),
                    pltpu.SemaphoreType.DMA((2,))]
    return pl.pallas_call(
        body, name=name, out_shape=out_shape, in_specs=in_specs, out_specs=out_specs, scratch_shapes=scratch,
        compiler_params=pltpu.CompilerParams(vmem_limit_bytes=VMEM_LIMIT),
    )(*operands)


HBM_SPEC = pl.BlockSpec(memory_space=pltpu.HBM)
SEM_SPEC = pl.BlockSpec(memory_space=pltpu.SEMAPHORE)
SIDE_EFFECT = pltpu.SideEffectType.DATAFLOW_SIDE_EFFECTING


def _peer(x, y, c, q, cb):
    return (1 - x if q & 2 else x, 1 - y if q & 1 else y, 1 - c if cb else c)


def _w_in_piece(slots):
    def piece(part_ref, k, to):
        return part_ref.at[pl.ds(to[2] * (D // 2), D // 2), pl.ds(slots[k] * W_IN_SHARD, W_IN_SHARD)]
    return piece


def _w_out_piece(part_ref, k, to):
    ho = W_OUT_SHARD // 2
    return part_ref.at[pl.ds((2 * to[0] + to[1]) * W_OUT_SHARD + to[2] * ho, ho), :]


def _group_piece(part_ref, k, to):
    return part_ref.at[4 * to[0] + 2 * to[1] + to[2]]


def _whole_piece(part_ref, k, to):
    return part_ref


def _exchange_start_call(groups, name):
    ng = len(groups)
    lands = [lax.empty((len(rels),) + slot_shape, part.dtype) for part, rels, _, slot_shape in groups]

    def body(*refs):
        ins, outs = refs[:2 * ng], refs[2 * ng:]
        x, y, c = _coords()
        for g, (_, rels, piece, _) in enumerate(groups):
            part_ref, land_ref = ins[2 * g], ins[2 * g + 1]
            send_sems, recv_sems = outs[4 * g], outs[4 * g + 1]
            for k, (q, cb) in enumerate(rels):
                to = _peer(x, y, c, q, cb)
                pltpu.make_async_remote_copy(src_ref=piece(part_ref, k, to), dst_ref=land_ref.at[k], send_sem=send_sems.at[k],
                                             recv_sem=recv_sems.at[k], device_id=to, device_id_type=MESH).start()
        outs[-1][...] = jnp.zeros_like(outs[-1])

    out_shape, out_specs, operands = [], [], []
    for (part, rels, _, _), land in zip(groups, lands):
        n = len(rels)
        out_shape += [pltpu.SemaphoreType.DMA((n,)), pltpu.SemaphoreType.DMA((n,)), pltpu.HBM(part.shape, part.dtype),
                      pltpu.HBM(land.shape, land.dtype)]
        out_specs += [SEM_SPEC, SEM_SPEC, HBM_SPEC, HBM_SPEC]
        operands += [pltpu.with_memory_space_constraint(part, pltpu.HBM), pltpu.with_memory_space_constraint(land, pltpu.HBM)]
    out = pl.pallas_call(
        body, name=name,
        out_shape=tuple(out_shape) + (jax.ShapeDtypeStruct((1, 1), F32),),
        in_specs=(HBM_SPEC,) * (2 * ng), out_specs=tuple(out_specs) + (pl.BlockSpec(memory_space=pltpu.VMEM),),
        input_output_aliases={j: 4 * (j // 2) + 2 + j % 2 for j in range(2 * ng)},
        compiler_params=pltpu.CompilerParams(has_side_effects=SIDE_EFFECT),
    )(*operands)
    return [tuple(out[4 * g:4 * g + 4]) for g in range(ng)], out[-1]


def _exchange_wait_call(started, groups, after, name):
    ng = len(groups)

    def body(*refs):
        ins = refs[:4 * ng]
        x, y, c = _coords()
        for g, (_, rels, piece, _) in enumerate(groups):
            part_ref, land_ref, send_sems, recv_sems = ins[4 * g:4 * g + 4]
            for k, (q, cb) in enumerate(rels):
                to = _peer(x, y, c, q, cb)
                cp = pltpu.make_async_remote_copy(src_ref=piece(part_ref, k, to), dst_ref=land_ref.at[k], send_sem=send_sems.at[k],
                                                  recv_sem=recv_sems.at[k], device_id=to, device_id_type=MESH)
                cp.wait_send()
                cp.wait_recv()

    operands, in_specs, out_shape = [], [], []
    for send_sems, recv_sems, part_thru, land_thru in started:
        operands += [part_thru, land_thru, send_sems, recv_sems]
        in_specs += [HBM_SPEC, HBM_SPEC, SEM_SPEC, SEM_SPEC]
        out_shape += [pltpu.HBM(part_thru.shape, part_thru.dtype), pltpu.HBM(land_thru.shape, land_thru.dtype)]
    out = pl.pallas_call(
        body, name=name, out_shape=tuple(out_shape),
        in_specs=tuple(in_specs) + (pl.BlockSpec(memory_space=pl.ANY),), out_specs=(HBM_SPEC,) * (2 * ng),
        input_output_aliases={4 * g + j: 2 * g + j for g in range(ng) for j in range(2)},
        compiler_params=pltpu.CompilerParams(has_side_effects=SIDE_EFFECT),
    )(*operands, after)
    return [tuple(out[2 * g:2 * g + 2]) for g in range(ng)]


def _rope_tables(s):
    inv_freq = np.float32(10000.0) ** (-np.arange(0, HEAD, 2, dtype=np.float32) / np.float32(HEAD))
    ang = np.arange(s, dtype=np.float32)[:, None] * inv_freq[None, :]
    cos = np.tile(np.cos(ang), (1, LANE // (HEAD // 2))).astype(np.float32)
    sin = np.tile(np.sin(ang), (1, LANE // (HEAD // 2))).astype(np.float32)
    first_half = (np.arange(LANE) % HEAD) < (HEAD // 2)
    sin = np.where(first_half[None, :], -sin, sin)
    behind = lambda t: np.concatenate([t[:BLK], t[:-BLK]], axis=0)
    return tuple(jnp.asarray(t) for t in (cos, sin, behind(cos), behind(sin)))


def kernel(x, c, w_ada, b_ada, norm_g, w_in, ln_v_g, ln_v_b, w_spatial, b_spatial, sinks, w_out, w_ada_final, b_ada_final, final_norm_g, loss_target, m_w_ada, m_b_ada, m_norm_g, m_w_in, m_ln_v_g, m_ln_v_b, m_w_spatial, m_b_spatial, m_sinks, m_w_out, m_w_ada_final, m_b_ada_final, m_final_norm_g, v_w_ada, v_b_ada, v_norm_g, v_w_in, v_ln_v_g, v_ln_v_b, v_w_spatial, v_b_spatial, v_sinks, v_w_out, v_w_ada_final, v_b_ada_final, v_final_norm_g):
    s = x.shape[1]
    ax, ay, ac = _coords()
    chip = 2 * ax + ay
    me = 4 * ax + 2 * ay + ac
    n_ada = w_ada.shape[2]
    n_adaf = w_ada_final.shape[1]

    x2d = x.reshape(s, D)
    tgt = loss_target.reshape(s, D)
    w_ada2, w_in2, w_out2 = w_ada[0], w_in[0], w_out[0]
    b_ada_f2 = b_ada_final.reshape(1, 2 * D)
    gf = final_norm_g.reshape(1, D)

    c_all, w_in_own = _allgather_sum_call(jnp.pad(c, ((0, 7), (0, 0))), "gather_c", False, cast=(w_in2, (D, D_IN)))
    c_all = c_all[::8]
    mod_p, c_act = _rowmat_call(c_all, w_ada2, lax.dynamic_slice(b_ada, (0, chip * n_ada), (1, n_ada)), "mod")
    modf_p, _ = _rowmat_call(c_all, w_ada_final, lax.dynamic_slice(b_ada_f2, (0, chip * n_adaf), (1, n_adaf)), "mod_final")
    mods, w_out_own = _allgather_sum_call(jnp.concatenate([mod_p, modf_p], axis=1), "gather_mod", False, cast=(w_out2, (D, D)))
    my_rows = [lax.dynamic_slice(mods, (16 * j + me, 0), (1, n_ada + n_adaf)) for j in range(N_CHIP)]
    mod = jnp.concatenate([r[:, :n_ada] for r in my_rows], axis=1)
    mod_f = jnp.concatenate([r[:, n_ada:] for r in my_rows], axis=1)
    shift, scale, gate = mod[:, :D], mod[:, D:2 * D], mod[:, 2 * D:]
    shift_f, scale_f = mod_f[:, :D], mod_f[:, D:]

    pos = jnp.stack([chip, ac]).astype(jnp.int32)

    tables = _rope_tables(s)
    cos, sin = tables[:2]
    b_sp_t = b_spatial[0].T
    sinks1 = sinks.reshape(N_Q)
    h, proj, w_in_bf, w_out_bf = _proj_gather_call(pos, x2d, shift, scale, norm_g, w_in_own, w_out_own)
    y, probs, attn_out, psinks = _mix_fwd_call(proj, cos, sin, ln_v_g, ln_v_b, w_spatial[0], b_sp_t, sinks1)
    dx2, do, dy, st_tail = _tail_call(y, w_out_bf, x2d, tgt, gate, shift_f, scale_f, gf)

    rel_o = [(0, 1), (1, 0), (1, 1), (2, 0), (2, 1), (3, 0), (3, 1)]
    rel_a = [(1, 0), (1, 1), (2, 0), (2, 1)]
    rel_b = [(3, 0), (3, 1), (0, 1)]
    piece_a, piece_b = _w_in_piece([0, 0, 1, 1]), _w_in_piece([0, 0, 1])
    half_in, half_out = (D // 2, W_IN_SHARD), (W_OUT_SHARD // 2, D)

    g_w_out_p = _tn_call(y, do, "grad_w_out")
    grp_o = [(g_w_out_p, rel_o, _w_out_piece, half_out)]
    st_o, tok_o = _exchange_start_call(grp_o, "send_w_out")
    dproj, st_ln, d_wsp, d_bsp_t, d_sink = _mix_bwd_call(
        proj, dy, probs, attn_out, psinks, tables, ln_v_g + tok_o, ln_v_b, w_spatial[0], jnp.swapaxes(w_spatial[0], 1, 2),
        b_sp_t)
    g_w_in_a = _tn_shards_call(pos, h, dproj, (1, 2), "grad_w_in_a")
    grp_a = [(g_w_in_a, rel_a, piece_a, half_in), (d_wsp, rel_o, _group_piece, (BLK, BLK))]
    st_a, tok_a = _exchange_start_call(grp_a, "send_w_in_a")
    g_w_in_b = _tn_shards_call(pos, h, dproj, (3, 0), "grad_w_in_b")
    grp_b = [(g_w_in_b, rel_b, piece_b, half_in)]
    st_b, tok_b = _exchange_start_call(grp_b, "send_w_in_b")
    grad_x, st_dh = _dh_call(dproj, w_in_bf, x2d, dx2, scale + (tok_a + tok_b), norm_g)

    ((g_w_out_p, recv_o),) = _exchange_wait_call(st_o, grp_o, st_dh, "wait_w_out")
    (_, recv_a), (d_wsp, recv_s) = _exchange_wait_call(st_a, grp_a, st_dh, "wait_w_in_a")
    ((g_w_in_b, recv_b),) = _exchange_wait_call(st_b, grp_b, st_dh, "wait_w_in_b")
    mine_in = _sum_pieces_call(pos, g_w_in_b, lambda i, p, nrb: (p[1] * nrb + i, 1), [recv_a, recv_b], "sum_w_in")
    mine_out = _sum_pieces_call(pos, g_w_out_p, lambda i, p, nrb: ((2 * p[0] + p[1]) * nrb + i, 0), [recv_o], "sum_w_out")
    wsp_group = _sum_pieces_call(pos, d_wsp.reshape(GROUPS * BLK, BLK), lambda i, p, nrb: (2 * p[0] + p[1], 0), [recv_s],
                                 "sum_w_spatial")
    to_sibling = [(0, 1)]
    grp_p = [(mine_in, to_sibling, _whole_piece, half_in), (mine_out, to_sibling, _whole_piece, half_out)]
    st_p, tok_p = _exchange_start_call(grp_p, "swap_halves")

    misc = jnp.concatenate([st_ln, d_bsp_t[:, :GROUPS].T, d_sink, jnp.zeros((8, D - D_A - 2 * LANE), F32)], axis=1)
    pack = jnp.concatenate([wsp_group.reshape(8, D) + tok_p, st_tail, st_dh, misc], axis=0)
    rows = pack.shape[0]
    packs, tot = _allgather_sum_call(pack, "gather_small", True)
    packs = packs.reshape(N_DEV, rows, D)
    dmod_all = jnp.concatenate([packs[:, 16, :], packs[:, 17, :], packs[:, 11, :]], axis=1)
    dmodf_all = jnp.concatenate([packs[:, 8, :], packs[:, 9, :]], axis=1)
    loss = tot[13, 0]
    (mine_in, theirs_in), (mine_out, theirs_out) = _exchange_wait_call(st_p, grp_p, tot, "swapped_halves")
    small = {
        "b_ada": jnp.concatenate([tot[16:17], tot[17:18], tot[11:12]], axis=1),
        "norm_g": tot[18:19],
        "ln_v_g": tot[24:25, :D_A],
        "ln_v_b": tot[25:26, :D_A],
        "w_spatial": packs[:, 0:8, :].reshape(GROUPS * BLK, BLK),
        "b_spatial": tot[24:32, D_A:D_A + BLK],
        "sinks": tot[24:25, D_A + LANE:D_A + LANE + N_Q],
        "b_ada_final": jnp.concatenate([tot[8:9], tot[9:10]], axis=1),
        "final_norm_g": tot[10:11],
    }

    weights = dict(w_ada=w_ada, b_ada=b_ada, norm_g=norm_g, w_in=w_in, ln_v_g=ln_v_g, ln_v_b=ln_v_b, w_spatial=w_spatial,
                   b_spatial=b_spatial, sinks=sinks, w_out=w_out, w_ada_final=w_ada_final, b_ada_final=b_ada_final,
                   final_norm_g=final_norm_g)
    m_in = dict(w_ada=m_w_ada, b_ada=m_b_ada, norm_g=m_norm_g, w_in=m_w_in, ln_v_g=m_ln_v_g, ln_v_b=m_ln_v_b,
                w_spatial=m_w_spatial, b_spatial=m_b_spatial, sinks=m_sinks, w_out=m_w_out, w_ada_final=m_w_ada_final,
                b_ada_final=m_b_ada_final, final_norm_g=m_final_norm_g)
    v_in = dict(w_ada=v_w_ada, b_ada=v_b_ada, norm_g=v_norm_g, w_in=v_w_in, ln_v_g=v_ln_v_g, ln_v_b=v_ln_v_b,
                w_spatial=v_w_spatial, b_spatial=v_b_spatial, sinks=v_sinks, w_out=v_w_out, w_ada_final=v_w_ada_final,
                b_ada_final=v_b_ada_final, final_norm_g=v_final_norm_g)
    c_act_t = c_act.T
    outer = {"w_ada": lax.dynamic_slice(dmod_all, (0, chip * n_ada), (N_DEV, n_ada)),
             "w_ada_final": lax.dynamic_slice(dmodf_all, (0, chip * n_adaf), (N_DEV, n_adaf))}
    halves = {"w_in": (mine_in, theirs_in[0]), "w_out": (mine_out, theirs_out[0])}
    done = {}
    for name, (mine, theirs) in halves.items():
        shape2 = (2 * mine.shape[0], mine.shape[1])
        done[name] = _adam_halves_call(pos, weights[name].reshape(shape2), mine, theirs, m_in[name].reshape(shape2),
                                       v_in[name].reshape(shape2), "adam_" + name)
    for name, dm in outer.items():
        shape2 = (D, dm.shape[1])
        done[name] = _adam_outer_call(weights[name].reshape(shape2), c_act_t, dm, m_in[name].reshape(shape2),
                                      v_in[name].reshape(shape2), "adam_" + name)
    updates = _adam_small_call([(weights[name].reshape(g.shape), g, m_in[name].reshape(g.shape), v_in[name].reshape(g.shape))
                                for name, g in small.items()])
    for (name, g), upd in zip(small.items(), updates):
        done[name] = (g, *upd)
    outs = [[done[name][k].reshape(w.shape) for name, w in weights.items()] for k in range(4)]
    return (loss, grad_x.reshape(x.shape), *outs[0], *outs[1], *outs[2], *outs[3])
```

```python
import numpy as np
import jax
import jax.numpy as jnp
from jax import lax
from jax.experimental import pallas as pl
from jax.experimental.pallas import tpu as pltpu

F32 = jnp.float32
BF16 = jnp.bfloat16
MESH = pl.DeviceIdType.MESH

D = 2048
D_A = 1024
D_B = 1024
D_KV = 256
HEAD = 64
N_Q = 16
N_KV = 4
Q_PER_KV = N_Q // N_KV
BLK = 128
GROUPS = 8
D_IN = 5632
OFF_Q, OFF_K, OFF_V, OFF_ZB = 3072, 4096, 4352, 4608
N_CHIP = 4
N_DEV = 8
W_IN_SHARD = D_IN // N_CHIP
W_OUT_SHARD = D // N_CHIP
EPS = 1e-5
SCALE = HEAD ** -0.5
NEG = -1e30
LANE = 128
VMEM_LIMIT = 56 * 1024 * 1024

ADAM_LR, ADAM_B1, ADAM_B2, ADAM_EPS, ADAM_WD, ADAM_STEP = 0.001, 0.9, 0.999, 1e-08, 0.01, 10
ADAM_C1 = 1.0 - ADAM_B1 ** ADAM_STEP
ADAM_C2 = 1.0 - ADAM_B2 ** ADAM_STEP
ADAM_ROWS = 256

NT = (((1,), (1,)), ((), ()))
TN = (((0,), (0,)), ((), ()))


def _params(*sem):
    return pltpu.CompilerParams(dimension_semantics=sem, vmem_limit_bytes=VMEM_LIMIT)


def _silu_parts(z):
    sig = 1.0 / (1.0 + jnp.exp(-z))
    return z * sig, sig


def _swap_halves(v, first_half):
    return jnp.where(first_half, pltpu.roll(v, 96, 1), pltpu.roll(v, 32, 1))


def _rope(v, cos_t, sin_s, first_half):
    return v * cos_t + _swap_halves(v, first_half) * sin_s


def _unrope(dv, cos_t, sin_s, first_half):
    return dv * cos_t - _swap_halves(dv, first_half) * sin_s


def _lane_masks():
    lane = lax.broadcasted_iota(jnp.int32, (BLK, LANE), 1)
    return (lane % HEAD) < (HEAD // 2), lane < HEAD


def _band_valid(first_block_bound, rows=BLK):
    rr = lax.broadcasted_iota(jnp.int32, (rows, 2 * BLK), 0) & (BLK - 1)
    jj = lax.broadcasted_iota(jnp.int32, (rows, 2 * BLK), 1)
    return (jj > rr) & (jj <= rr + BLK) & (jj >= first_block_bound)


def _dup_kv(slab, lo):
    rolled = pltpu.roll(slab, HEAD, 1)
    return jnp.where(lo, slab, rolled).astype(BF16), jnp.where(lo, rolled, slab).astype(BF16)


def _stack_heads(ref, sb, slab, lo, dtype):
    kh, base = sb // 2, 2 * (sb % 2) * BLK
    zero = jnp.zeros_like(slab)
    ref[kh, base:base + BLK, :] = jnp.where(lo, slab, zero).astype(dtype)
    ref[kh, base + BLK:base + 2 * BLK, :] = jnp.where(lo, zero, slab).astype(dtype)


def _unstack_heads(ref, sb, lo):
    kh, base = sb // 2, 2 * (sb % 2) * BLK
    return jnp.where(lo, ref[kh, base:base + BLK, :], ref[kh, base + BLK:base + 2 * BLK, :])


def _sink_column(sinks_ref, kh):
    row = lax.broadcasted_iota(jnp.int32, (Q_PER_KV * BLK, 1), 0)
    col = jnp.full(row.shape, sinks_ref[Q_PER_KV * kh + Q_PER_KV - 1], F32)
    for n in range(Q_PER_KV - 2, -1, -1):
        col = jnp.where(row < (n + 1) * BLK, sinks_ref[Q_PER_KV * kh + n], col)
    return col


def _tril():
    t = lax.broadcasted_iota(jnp.int32, (BLK, BLK), 0)
    s = lax.broadcasted_iota(jnp.int32, (BLK, BLK), 1)
    return s <= t


def _layer_norm_fwd(va, lg, lb):
    mu = jnp.mean(va, axis=-1, keepdims=True)
    xc = va - mu
    rstd = lax.rsqrt(jnp.mean(xc * xc, axis=-1, keepdims=True) + EPS)
    vhat = xc * rstd
    return vhat, rstd, vhat * lg + lb


def _softmax_sink(qm, kdup, bias, sink):
    s = lax.dot_general(qm, kdup, NT, preferred_element_type=F32) + bias
    m = jnp.maximum(jnp.max(s, axis=-1, keepdims=True), sink)
    p = jnp.exp(s - m)
    esink = jnp.exp(sink - m)
    inv = 1.0 / (jnp.sum(p, axis=-1, keepdims=True) + esink)
    return p * inv, esink * inv


def _band_bias(bias_ref):
    rows = bias_ref.shape[1]
    bias_ref[0] = jnp.where(_band_valid(BLK, rows), 0.0, NEG)
    bias_ref[1] = jnp.where(_band_valid(0, rows), 0.0, NEG)


def _rowmat_call(c_all, w, b, name):
    n = w.shape[1]
    tn = 512

    def body(c_ref, w_ref, b_ref, o_ref, ca_ref):
        ca, _ = _silu_parts(c_ref[...])
        ca_ref[...] = ca
        o_ref[...] = jnp.dot(ca.astype(BF16), w_ref[...].astype(BF16), preferred_element_type=F32) + b_ref[...]

    return pl.pallas_call(
        body, name=name, grid=(n // tn,),
        in_specs=[pl.BlockSpec((N_DEV, D), lambda j: (0, 0)), pl.BlockSpec((D, tn), lambda j: (0, j)),
                  pl.BlockSpec((1, tn), lambda j: (0, j))],
        out_specs=[pl.BlockSpec((N_DEV, tn), lambda j: (0, j)), pl.BlockSpec((N_DEV, D), lambda j: (0, 0))],
        out_shape=[jax.ShapeDtypeStruct((N_DEV, n), F32), jax.ShapeDtypeStruct((N_DEV, D), F32)],
        compiler_params=_params("arbitrary"),
    )(c_all, w, b)


W_IN_PARTS = ((0, 768), (768, 640))
OUT_STREAMS = 4
X_STREAMS = 4


def _proj_gather_call(pos, x, shift, scale, norm_g, wi_full, wo_full):
    s = x.shape[0]
    tm = min(s, 512)
    nrow = s // tm
    hi = D // 2
    ho = W_OUT_SHARD // 2
    phases = [(0, None), (1, 0), (2, 0), (1, 1), (2, 1), (3, 0), (3, 1)]

    def body(pos_ref, *refs):
        x_refs = refs[:X_STREAMS]
        (sh_ref, sc_ref, g_ref, _, _, h_ref, proj_ref, fi_ref, fo_ref,
         h_all, wbuf, obuf, send_sems, recv_sems, load_sems, out_sems) = refs[X_STREAMS:]
        p = pl.program_id(0)
        i = pl.program_id(1)
        x_, y_, c_ = _coords()
        me, sibling = (x_, y_, c_), (x_, y_, 1 - c_)

        def shard_of(q):
            px, py, _ = _peer(x_, y_, c_, q, 0)
            return 2 * px + py

        def cols_of(q, cp):
            off, w = (0, W_IN_SHARD) if cp is None else W_IN_PARTS[cp]
            return shard_of(q) * W_IN_SHARD + off, w

        def part(which, q, pc, sub, cp):
            n = hi if which == 0 else ho
            base = pc * n
            if sub is not None:
                n //= 2
                base = base + sub * n
            if which == 0:
                c0, w = cols_of(q, cp)
                return fi_ref.at[pl.ds(base, n), pl.ds(c0, w)]
            return fo_ref.at[pl.ds(shard_of(q) * W_OUT_SHARD + base, n), :]

        def copy(k, ref, to):
            return pltpu.make_async_remote_copy(src_ref=ref, dst_ref=ref, send_sem=send_sems.at[k], recv_sem=recv_sems.at[k],
                                                device_id=to, device_id_type=MESH)

        def sem(which, kind, j, cp):
            return 4 * kind + 2 * cp + j if which == 0 else 16 + 2 * kind + j

        def to_neighbour(which, q, cp=None):
            return copy(sem(which, 0, q - 1, cp), part(which, 0, c_, None, cp), _peer(x_, y_, c_, q, 0))

        def from_neighbour(which, q, cp=None):
            return copy(sem(which, 0, q - 1, cp), part(which, q, c_, None, cp), me)

        def relay(which, q, cp=None):
            return copy(sem(which, 1, q - 1, cp), part(which, q, c_, q - 1, cp), _peer(x_, y_, c_, 3 - q, 0))

        def relayed(which, sub, cp=None):
            return copy(sem(which, 1, sub, cp), part(which, 3, c_, sub, cp), me)

        def to_sibling(which, q, cp=None):
            return copy(sem(which, 2, q - 1, cp), part(which, q, c_, None, cp), sibling)

        def from_sibling(which, q, cp=None):
            return copy(sem(which, 2, q - 1, cp), part(which, q, 1 - c_, None, cp), me)

        def relayed_to_sibling(which, sub, cp=None):
            return copy(sem(which, 3, sub, cp), part(which, 3, c_, sub, cp), sibling)

        def relayed_from_sibling(which, sub, cp=None):
            return copy(sem(which, 3, sub, cp), part(which, 3, 1 - c_, sub, cp), me)

        def pass_on_neighbours(which, cp=None):
            for q in (1, 2):
                from_neighbour(which, q, cp).wait_recv()
                to_sibling(which, q, cp).start()
                relay(which, q, cp).start()

        def pass_on_relayed(which, cp=None):
            for sub in range(2):
                relayed(which, sub, cp).wait_recv()
                relayed_to_sibling(which, sub, cp).start()

        def shard_load(k):
            c0, w = cols_of(*phases[k])
            return pltpu.make_async_copy(fi_ref.at[:, pl.ds(c0, w)], wbuf.at[k % 2, :, 0:w], load_sems.at[k % 2])

        class OutCopies:
            def __init__(self, k, slot, row0):
                c0, w = cols_of(*phases[k])
                strip = tm // OUT_STREAMS
                self.copies = [pltpu.make_async_copy(obuf.at[slot, n * strip:(n + 1) * strip, 0:w],
                                                     proj_ref.at[pl.ds(row0 + n * strip, strip), pl.ds(c0, w)],
                                                     out_sems.at[slot, n]) for n in range(OUT_STREAMS)]

            def start(self):
                for cp in self.copies:
                    cp.start()

            def wait(self):
                for cp in self.copies:
                    cp.wait()

        out_copy = OutCopies

        def drain(k):
            for j in range(min(2, nrow)):
                out_copy(k, (nrow - 1 - j) % 2, 0).wait()

        def arrivals(k):
            q, cp = phases[k]
            if k == 0:
                for cp_ in range(2):
                    for q_ in (1, 2):
                        to_neighbour(0, q_, cp_).start()
            elif q < 3 and k in (1, 3):
                pass_on_neighbours(0, cp)
                if k == 1:
                    for q_ in (1, 2):
                        to_neighbour(1, q_).start()
            elif k == 5:
                for cp_ in range(2):
                    pass_on_relayed(0, cp_)
                pass_on_neighbours(1)
            if q in (1, 2):
                from_sibling(0, q, cp).wait_recv()
            elif q == 3:
                for sub in range(2):
                    relayed_from_sibling(0, sub, cp).wait_recv()

        rows = pl.ds(pl.multiple_of(i * tm, tm), tm)
        slot = i % 2
        for k, (q, cp) in enumerate(phases):
            @pl.when(p == k)
            def _(k=k, q=q, cp=cp):
                @pl.when(i == 0)
                def _():
                    if k == 0:
                        arrivals(0)
                        shard_load(0).start()
                    else:
                        drain(k - 1)
                    shard_load(k).wait()

                if k + 1 < len(phases):
                    @pl.when(i == max(nrow - 2, 0))
                    def _():
                        arrivals(k + 1)
                        shard_load(k + 1).start()

                if k == 0:
                    wx = D // X_STREAMS
                    ssq = sum(jnp.sum(xr[...] * xr[...], axis=-1, keepdims=True) for xr in x_refs)
                    r = lax.rsqrt(ssq * (1.0 / D) + EPS)
                    for n, xr in enumerate(x_refs):
                        cols = slice(n * wx, (n + 1) * wx)
                        hv = ((xr[...] * r * g_ref[:, cols]) * (1.0 + sc_ref[:, cols]) + sh_ref[:, cols]).astype(BF16)
                        h_ref[:, cols] = hv
                        h_all[rows, cols] = hv

                @pl.when(i >= 2)
                def _():
                    out_copy(k, slot, 0).wait()

                w = cols_of(q, cp)[1]
                obuf[slot, :, 0:w] = jnp.dot(h_all[rows, :], wbuf[k % 2, :, 0:w], preferred_element_type=F32)
                out_copy(k, slot, pl.multiple_of(i * tm, tm)).start()

        @pl.when((p == len(phases) - 1) & (i == nrow - 1))
        def _():
            drain(len(phases) - 1)
            pass_on_relayed(1)
            for q in (1, 2):
                from_sibling(1, q).wait_recv()
            for sub in range(2):
                relayed_from_sibling(1, sub).wait_recv()
            for which, cps in ((0, (0, 1)), (1, (None,))):
                for cp in cps:
                    for q in (1, 2):
                        to_neighbour(which, q, cp).wait_send()
                        relay(which, q, cp).wait_send()
                        to_sibling(which, q, cp).wait_send()
                        relayed_to_sibling(which, q - 1, cp).wait_send()

    vec = pl.BlockSpec((1, D), lambda p, i, pos: (0, 0))
    first_phase_rows = lambda p, i, pos: (jnp.where(p == 0, i, nrow - 1), 0)
    anyspec = pl.BlockSpec(memory_space=pl.ANY)
    x_spec = lambda n: pl.BlockSpec((tm, D // X_STREAMS), lambda p, i, pos: (jnp.where(p == 0, i, nrow - 1), n))
    return pl.pallas_call(
        body, name="proj_gather",
        grid_spec=pltpu.PrefetchScalarGridSpec(
            num_scalar_prefetch=1, grid=(len(phases), nrow),
            in_specs=[x_spec(n) for n in range(X_STREAMS)] + [vec, vec, vec, anyspec, anyspec],
            out_specs=[pl.BlockSpec((tm, D), first_phase_rows), anyspec, anyspec, anyspec],
            scratch_shapes=[pltpu.VMEM((s, D), BF16), pltpu.VMEM((2, D, W_IN_SHARD), BF16), pltpu.VMEM((2, tm, W_IN_SHARD), F32),
                            pltpu.SemaphoreType.DMA((24,)), pltpu.SemaphoreType.DMA((24,)), pltpu.SemaphoreType.DMA((2,)),
                            pltpu.SemaphoreType.DMA((2, OUT_STREAMS))]),
        out_shape=[jax.ShapeDtypeStruct((s, D), BF16), jax.ShapeDtypeStruct((s, D_IN), F32),
                   jax.ShapeDtypeStruct((D, D_IN), BF16), jax.ShapeDtypeStruct((D, D), BF16)],
        input_output_aliases={X_STREAMS + 4: 2, X_STREAMS + 5: 3},
        compiler_params=_params("arbitrary", "arbitrary"),
    )(pos, *([x] * X_STREAMS), shift, scale, norm_g, wi_full, wo_full)


def _proj_specs(rev_nb=None):
    if rev_nb is None:
        row = lambda i: i
    else:
        row = lambda i: rev_nb - 1 - i
    wide = lambda col: pl.BlockSpec((BLK, D_A), lambda i: (row(i), col))
    kv = lambda col: pl.BlockSpec((BLK, D_KV), lambda i: (row(i), col))
    half = lambda col: pl.BlockSpec((BLK, 512), lambda i: (row(i), col))
    return [wide(0), wide(1), wide(2), wide(3), kv(OFF_K // D_KV), kv(OFF_V // D_KV), half(OFF_ZB // 512), half(OFF_ZB // 512 + 1)]


def _mix_fwd_call(proj, cos, sin, ln_g, ln_b, w_sp, b_sp_t, sinks):
    s = proj.shape[0]
    nb = s // BLK

    def body(ua_ref, va_ref, za_ref, q_ref, k_ref, v_ref, zb0_ref, zb1_ref, cos_ref, sin_ref, lg_ref, lb_ref,
             w_ref, bt_ref, sinks_ref, y_ref, probs_ref, ost_ref, psink_ref, kdup_ref, vdup_ref, qm_ref, bias_ref):
        i = pl.program_id(0)
        first_half, lo = _lane_masks()
        cos_t = cos_ref[...]
        sin_t = sin_ref[...]

        _, _, vln = _layer_norm_fwd(va_ref[...], lg_ref[...], lb_ref[...])
        tril = _tril()
        for g in range(GROUPS):
            cols = slice(g * BLK, (g + 1) * BLK)
            wg = jnp.where(tril, w_ref[g], 0.0).astype(BF16)
            sg = jnp.dot(wg, vln[:, cols].astype(BF16), preferred_element_type=F32) + bt_ref[:, g:g + 1]
            gate, _ = _silu_parts(za_ref[:, cols])
            y_ref[:, cols] = (ua_ref[:, cols] * sg * gate).astype(BF16)

        @pl.when(i == 0)
        def _():
            kdup_ref[:, 0:BLK, :] = jnp.zeros((N_KV, BLK, LANE), BF16)
            vdup_ref[:, 0:BLK, :] = jnp.zeros((N_KV, BLK, LANE), BF16)
            _band_bias(bias_ref)

        @pl.when(i > 0)
        def _():
            kdup_ref[:, 0:BLK, :] = kdup_ref[:, BLK:2 * BLK, :]
            vdup_ref[:, 0:BLK, :] = vdup_ref[:, BLK:2 * BLK, :]

        for ks in range(2):
            cols = slice(ks * LANE, (ks + 1) * LANE)
            kr = _rope(k_ref[:, cols], cos_t, sin_t, first_half)
            for n, (kd, vd) in enumerate(zip(_dup_kv(kr, lo), _dup_kv(v_ref[:, cols], lo))):
                kdup_ref[2 * ks + n, BLK:2 * BLK, :] = kd
                vdup_ref[2 * ks + n, BLK:2 * BLK, :] = vd
        for sb in range(8):
            _stack_heads(qm_ref, sb, _rope(q_ref[:, sb * LANE:(sb + 1) * LANE], cos_t, sin_t, first_half) * SCALE, lo, BF16)

        block_kind = jnp.where(i > 0, 1, 0)

        psink_ref[...] = jnp.zeros((Q_PER_KV * BLK, LANE), F32)
        lane_q = lax.broadcasted_iota(jnp.int32, (Q_PER_KV * BLK, LANE), 1)

        def kv_head(kh, carry):
            probs, psink = _softmax_sink(qm_ref[kh], kdup_ref[kh], bias_ref[block_kind], _sink_column(sinks_ref, kh))
            probs_ref[kh] = probs
            psink_ref[...] = jnp.where(lane_q == kh, psink, psink_ref[...])
            ost_ref[kh] = jnp.dot(probs.astype(BF16), vdup_ref[kh], preferred_element_type=F32)
            return carry

        lax.fori_loop(0, N_KV, kv_head, 0, unroll=2)
        for sb in range(8):
            cols = slice(sb * LANE, (sb + 1) * LANE)
            zb = zb0_ref[:, cols] if sb < 4 else zb1_ref[:, (sb - 4) * LANE:(sb - 3) * LANE]
            gate, _ = _silu_parts(zb)
            y_ref[:, D_A + sb * LANE:D_A + (sb + 1) * LANE] = (_unstack_heads(ost_ref, sb, lo) * gate).astype(BF16)

    tab = pl.BlockSpec((BLK, LANE), lambda i: (i, 0))
    return pl.pallas_call(
        body, name="mix_fwd", grid=(nb,),
        in_specs=_proj_specs() + [
            tab, tab, pl.BlockSpec((1, D_A), lambda i: (0, 0)), pl.BlockSpec((1, D_A), lambda i: (0, 0)),
            pl.BlockSpec((GROUPS, BLK, BLK), lambda i: (0, 0, 0)), pl.BlockSpec((BLK, GROUPS), lambda i: (0, 0)),
            pl.BlockSpec(memory_space=pltpu.SMEM)],
        out_specs=[pl.BlockSpec((BLK, 2 * D_A), lambda i: (i, 0)),
                   pl.BlockSpec((None, N_KV, Q_PER_KV * BLK, 2 * BLK), lambda i: (i, 0, 0, 0)),
                   pl.BlockSpec((None, N_KV, Q_PER_KV * BLK, LANE), lambda i: (i, 0, 0, 0)),
                   pl.BlockSpec((None, Q_PER_KV * BLK, LANE), lambda i: (i, 0, 0))],
        out_shape=[jax.ShapeDtypeStruct((s, 2 * D_A), BF16), jax.ShapeDtypeStruct((nb, N_KV, Q_PER_KV * BLK, 2 * BLK), F32),
                   jax.ShapeDtypeStruct((nb, N_KV, Q_PER_KV * BLK, LANE), F32), jax.ShapeDtypeStruct((nb, Q_PER_KV * BLK, LANE), F32)],
        scratch_shapes=[pltpu.VMEM((N_KV, 2 * BLK, LANE), BF16), pltpu.VMEM((N_KV, 2 * BLK, LANE), BF16),
                        pltpu.VMEM((N_KV, Q_PER_KV * BLK, LANE), BF16), pltpu.VMEM((2, Q_PER_KV * BLK, 2 * BLK), F32)],
        compiler_params=_params("arbitrary"),
    )(proj, proj, proj, proj, proj, proj, proj, proj, cos, sin, ln_g, ln_b, w_sp, b_sp_t, sinks)


def _tail_call(y, w_out_bf, x, target, gate, shift_f, scale_f, gf):
    s = x.shape[0]
    tm = min(s, 256)
    nsteps = s // tm

    def body(y_ref, w_ref, x_ref, t_ref, gate_ref, shf_ref, scf_ref, gf_ref, dx2_ref, do_ref, dy_ref, st_ref):
        i = pl.program_id(0)

        @pl.when(i == 0)
        def _():
            st_ref[...] = jnp.zeros((8, D), F32)

        o = jnp.dot(y_ref[...], w_ref[...], preferred_element_type=F32)
        gate_v = gate_ref[...]
        x2 = x_ref[...] + gate_v * o
        r2 = lax.rsqrt(jnp.mean(x2 * x2, axis=-1, keepdims=True) + EPS)
        xn2 = x2 * r2
        hn2 = xn2 * gf_ref[...]
        one_sc = 1.0 + scf_ref[...]
        err = hn2 * one_sc + shf_ref[...] - t_ref[...]
        dout = err * (1.0 / D)
        dhn2 = dout * one_sc
        dxn2 = dhn2 * gf_ref[...]
        dx2 = r2 * (dxn2 - xn2 * jnp.mean(dxn2 * xn2, axis=-1, keepdims=True))
        dx2_ref[...] = dx2
        do = (dx2 * gate_v).astype(BF16)
        do_ref[...] = do
        dy_ref[...] = lax.dot_general(do, w_ref[...], NT, preferred_element_type=F32)
        st_ref[0:1, :] += jnp.sum(dout, axis=0, keepdims=True)
        st_ref[1:2, :] += jnp.sum(dout * hn2, axis=0, keepdims=True)
        st_ref[2:3, :] += jnp.sum(dhn2 * xn2, axis=0, keepdims=True)
        st_ref[3:4, :] += jnp.sum(dx2 * o, axis=0, keepdims=True)
        st_ref[4:5, :] += jnp.sum(err * err, axis=0, keepdims=True)

        @pl.when(i == nsteps - 1)
        def _():
            st_ref[5:6, :] = jnp.full((1, D), 0.5 / D, F32) * jnp.sum(st_ref[4:5, :])

    vec = pl.BlockSpec((1, D), lambda i: (0, 0))
    rows = lambda: pl.BlockSpec((tm, D), lambda i: (i, 0))
    return pl.pallas_call(
        body, name="tail", grid=(nsteps,),
        in_specs=[rows(), pl.BlockSpec((D, D), lambda i: (0, 0)), rows(), rows(), vec, vec, vec, vec],
        out_specs=[rows(), rows(), rows(), pl.BlockSpec((8, D), lambda i: (0, 0))],
        out_shape=[jax.ShapeDtypeStruct((s, D), F32), jax.ShapeDtypeStruct((s, D), BF16), jax.ShapeDtypeStruct((s, D), F32),
                   jax.ShapeDtypeStruct((8, D), F32)],
        compiler_params=_params("arbitrary"),
    )(y, w_out_bf, x, target, gate, shift_f, scale_f, gf)


def _tn_call(a, b, name):
    s, m = a.shape
    n = b.shape[1]
    tn = 1024
    ts = min(s, 1024)
    nk = s // ts

    def body(a_ref, b_ref, o_ref, acc_ref):
        k = pl.program_id(1)

        @pl.when(k == 0)
        def _():
            acc_ref[...] = jnp.zeros((m, tn), F32)

        acc_ref[...] += lax.dot_general(a_ref[...], b_ref[...], TN, preferred_element_type=F32)

        @pl.when(k == nk - 1)
        def _():
            o_ref[...] = acc_ref[...].astype(BF16)

    return pl.pallas_call(
        body, name=name, grid=(n // tn, nk),
        in_specs=[pl.BlockSpec((ts, m), lambda j, k: (k, 0)), pl.BlockSpec((ts, tn), lambda j, k: (k, j))],
        out_specs=pl.BlockSpec((m, tn), lambda j, k: (0, j)),
        out_shape=jax.ShapeDtypeStruct((m, n), BF16),
        scratch_shapes=[pltpu.VMEM((m, tn), F32)],
        compiler_params=_params("parallel", "arbitrary"),
    )(a, b)


def _tn_shards_call(pos, a, b, qs, name):
    s, m = a.shape
    ts = min(s, 1024)
    nk = s // ts

    def body(pos_ref, a_ref, b_ref, o_ref, acc_ref):
        k = pl.program_id(1)

        @pl.when(k == 0)
        def _():
            acc_ref[...] = jnp.zeros((m, W_IN_SHARD), F32)

        acc_ref[...] += lax.dot_general(a_ref[...], b_ref[...], TN, preferred_element_type=F32)

        @pl.when(k == nk - 1)
        def _():
            o_ref[...] = acc_ref[...].astype(BF16)

    def shard(j, pos):
        q = qs[0]
        for n in range(1, len(qs)):
            q = jnp.where(j == n, qs[n], q)
        return jnp.bitwise_xor(pos[0], q)

    return pl.pallas_call(
        body, name=name,
        grid_spec=pltpu.PrefetchScalarGridSpec(
            num_scalar_prefetch=1, grid=(len(qs), nk),
            in_specs=[pl.BlockSpec((ts, m), lambda j, k, pos: (k, 0)),
                      pl.BlockSpec((ts, W_IN_SHARD), lambda j, k, pos: (k, shard(j, pos)))],
            out_specs=pl.BlockSpec((m, W_IN_SHARD), lambda j, k, pos: (0, j)),
            scratch_shapes=[pltpu.VMEM((m, W_IN_SHARD), F32)]),
        out_shape=jax.ShapeDtypeStruct((m, len(qs) * W_IN_SHARD), BF16),
        compiler_params=_params("parallel", "arbitrary"),
    )(pos, a, b)


def _mix_bwd_call(proj, dy, probs, outs, psinks, tables, ln_g, ln_b, w_sp, w_sp_t, b_sp_t):
    s = proj.shape[0]
    nb = s // BLK
    rev = lambda i: nb - 1 - i
    prev = lambda i: jnp.maximum(nb - 2 - i, 0)

    def body(ua_ref, va_ref, za_ref, q_ref, k_ref, v_ref, zb0_ref, zb1_ref, kp_ref, vp_ref, dy_ref,
             probs_ref, ost_ref, psink_ref, cos_ref, sin_ref, cosp_ref, sinp_ref, lg_ref, lb_ref, w_ref, wt_ref, bt_ref,
             dp_ref, lnst_ref, dw_ref, dbt_ref, dsink_ref,
             kdup_ref, vdup_ref, dvln_ref, qm_ref, dom_ref, dqst_ref, dkdup_ref, dvdup_ref, kcar_ref, vcar_ref, sigb_ref):
        i = pl.program_id(0)
        first_half, lo = _lane_masks()
        lane8 = lax.broadcasted_iota(jnp.int32, (8, LANE), 1)
        cos_t = cos_ref[...]
        sin_t = sin_ref[...]

        @pl.when(i == 0)
        def _():
            lnst_ref[...] = jnp.zeros((8, D_A), F32)
            dw_ref[...] = jnp.zeros((GROUPS, BLK, BLK), F32)
            dbt_ref[...] = jnp.zeros((BLK, LANE), F32)
            dsink_ref[...] = jnp.zeros((8, LANE), F32)
            kcar_ref[...] = jnp.zeros((BLK, D_KV), F32)
            vcar_ref[...] = jnp.zeros((BLK, D_KV), F32)

        vhat, rstd, vln = _layer_norm_fwd(va_ref[...], lg_ref[...], lb_ref[...])
        tril = _tril()
        triu = jnp.logical_not(tril) | (lax.broadcasted_iota(jnp.int32, (BLK, BLK), 0) == lax.broadcasted_iota(jnp.int32, (BLK, BLK), 1))
        lane_b = lax.broadcasted_iota(jnp.int32, (BLK, LANE), 1)
        db_acc = jnp.zeros((BLK, LANE), F32)
        for g in range(GROUPS):
            cols = slice(g * BLK, (g + 1) * BLK)
            vln_g = vln[:, cols].astype(BF16)
            wg = jnp.where(tril, w_ref[g], 0.0).astype(BF16)
            sg = jnp.dot(wg, vln_g, preferred_element_type=F32) + bt_ref[:, g:g + 1]
            za = za_ref[:, cols]
            gate, sig = _silu_parts(za)
            ua = ua_ref[:, cols]
            dya_g = dy_ref[:, cols]
            dya = dya_g * gate
            dp_ref[:, cols] = (dya * sg).astype(BF16)
            dp_ref[:, 2 * D_A + g * BLK:2 * D_A + (g + 1) * BLK] = (
                dya_g * (ua * sg) * (sig * (1.0 + za * (1.0 - sig)))).astype(BF16)
            ds = dya * ua
            ds_b = ds.astype(BF16)
            wtg = jnp.where(triu, wt_ref[g], 0.0).astype(BF16)
            dvln_ref[:, cols] = jnp.dot(wtg, ds_b, preferred_element_type=F32)
            dw_ref[g] += jnp.where(tril, lax.dot_general(ds_b, vln_g, NT, preferred_element_type=F32), 0.0)
            db_acc = db_acc + jnp.where(lane_b == g, jnp.sum(ds, axis=-1, keepdims=True), 0.0)
        dbt_ref[...] += db_acc
        dvln = dvln_ref[...]
        lnst_ref[0:1, :] += jnp.sum(dvln * vhat, axis=0, keepdims=True)
        lnst_ref[1:2, :] += jnp.sum(dvln, axis=0, keepdims=True)
        dvhat = dvln * lg_ref[...]
        m1 = jnp.mean(dvhat, axis=-1, keepdims=True)
        m2 = jnp.mean(dvhat * vhat, axis=-1, keepdims=True)
        dp_ref[:, D_A:2 * D_A] = (rstd * (dvhat - m1 - vhat * m2)).astype(BF16)

        cosp = cosp_ref[...]
        sinp = sinp_ref[...]
        for ks in range(2):
            cols = slice(ks * LANE, (ks + 1) * LANE)
            kr = _rope(k_ref[:, cols], cos_t, sin_t, first_half)
            kpr = _rope(kp_ref[:, cols], cosp, sinp, first_half)
            for n, (kc, vc, kp, vp) in enumerate(zip(_dup_kv(kr, lo), _dup_kv(v_ref[:, cols], lo),
                                                     _dup_kv(kpr, lo), _dup_kv(vp_ref[:, cols], lo))):
                kdup_ref[2 * ks + n, BLK:2 * BLK, :] = kc
                vdup_ref[2 * ks + n, BLK:2 * BLK, :] = vc
                kdup_ref[2 * ks + n, 0:BLK, :] = kp
                vdup_ref[2 * ks + n, 0:BLK, :] = vp
        for sb in range(8):
            cols = slice(sb * LANE, (sb + 1) * LANE)
            _stack_heads(qm_ref, sb, _rope(q_ref[:, cols], cos_t, sin_t, first_half) * SCALE, lo, BF16)
            zb = zb0_ref[:, cols] if sb < 4 else zb1_ref[:, (sb - 4) * LANE:(sb - 3) * LANE]
            gate, sig = _silu_parts(zb)
            sigb_ref[:, cols] = sig
            _stack_heads(dom_ref, sb, dy_ref[:, D_A + sb * LANE:D_A + (sb + 1) * LANE] * gate, lo, F32)

        lane_q = lax.broadcasted_iota(jnp.int32, (Q_PER_KV * BLK, LANE), 1)

        def kv_head(kh, sink_acc):
            qm = qm_ref[kh]
            kd = kdup_ref[kh]
            vd = vdup_ref[kh]
            probs = probs_ref[kh]
            probs_b = probs.astype(BF16)
            o = ost_ref[kh]
            dom = dom_ref[kh]
            dom_b = dom.astype(BF16)
            delta = jnp.sum(dom * o, axis=-1, keepdims=True)
            dpr = lax.dot_general(dom_b, vd, NT, preferred_element_type=F32)
            dss = (probs * (dpr - delta)).astype(BF16)
            sink_acc = sink_acc + jnp.where(lane_q == kh, psink_ref[...] * delta, 0.0)
            dqst_ref[kh] = jnp.dot(dss, kd, preferred_element_type=F32)
            dkdup_ref[kh] = lax.dot_general(dss, qm, TN, preferred_element_type=F32)
            dvdup_ref[kh] = lax.dot_general(probs_b, dom_b, TN, preferred_element_type=F32)
            return sink_acc

        sink_acc = jnp.zeros((Q_PER_KV * BLK, LANE), F32)
        for kh in range(N_KV):
            sink_acc = kv_head(kh, sink_acc)
        lane1 = lax.broadcasted_iota(jnp.int32, (1, LANE), 1)
        dsink_acc = jnp.zeros((8, LANE), F32)
        for n in range(Q_PER_KV):
            col = jnp.sum(sink_acc[n * BLK:(n + 1) * BLK], axis=0, keepdims=True)
            for kh in range(N_KV):
                dsink_acc = dsink_acc + jnp.where(lane8 == Q_PER_KV * kh + n, -jnp.sum(jnp.where(lane1 == kh, col, 0.0)), 0.0)
        row0 = lax.broadcasted_iota(jnp.int32, (8, LANE), 0) == 0
        dsink_ref[...] += jnp.where(row0, dsink_acc, 0.0)

        for sb in range(8):
            cols = slice(sb * LANE, (sb + 1) * LANE)
            zb = zb0_ref[:, cols] if sb < 4 else zb1_ref[:, (sb - 4) * LANE:(sb - 3) * LANE]
            sig = sigb_ref[:, cols]
            dyb = dy_ref[:, D_A + sb * LANE:D_A + (sb + 1) * LANE]
            dp_ref[:, OFF_ZB + sb * LANE:OFF_ZB + (sb + 1) * LANE] = (
                dyb * _unstack_heads(ost_ref, sb, lo) * (sig * (1.0 + zb * (1.0 - sig)))).astype(BF16)
            dq_r = _unstack_heads(dqst_ref, sb, lo) * SCALE
            dp_ref[:, OFF_Q + sb * LANE:OFF_Q + (sb + 1) * LANE] = _unrope(dq_r, cos_t, sin_t, first_half).astype(BF16)

        lo2 = lax.broadcasted_iota(jnp.int32, (2 * BLK, LANE), 1) < HEAD
        for ks in range(2):
            cols = slice(ks * LANE, (ks + 1) * LANE)
            ka = dkdup_ref[2 * ks]
            kb = dkdup_ref[2 * ks + 1]
            dk_band = jnp.where(lo2, ka + pltpu.roll(ka, HEAD, 1), kb + pltpu.roll(kb, HEAD, 1))
            va_ = dvdup_ref[2 * ks]
            vb_ = dvdup_ref[2 * ks + 1]
            dv_band = jnp.where(lo2, va_ + pltpu.roll(va_, HEAD, 1), vb_ + pltpu.roll(vb_, HEAD, 1))
            dkr = dk_band[BLK:2 * BLK, :] + kcar_ref[:, cols]
            dp_ref[:, OFF_K + ks * LANE:OFF_K + (ks + 1) * LANE] = _unrope(dkr, cos_t, sin_t, first_half).astype(BF16)
            dp_ref[:, OFF_V + ks * LANE:OFF_V + (ks + 1) * LANE] = (
                dv_band[BLK:2 * BLK, :] + vcar_ref[:, cols]).astype(BF16)
            kcar_ref[:, cols] = dk_band[0:BLK, :]
            vcar_ref[:, cols] = dv_band[0:BLK, :]

    tab = pl.BlockSpec((BLK, LANE), lambda i: (rev(i), 0))
    kvp = lambda col: pl.BlockSpec((BLK, D_KV), lambda i: (prev(i), col))
    vec = pl.BlockSpec((1, D_A), lambda i: (0, 0))
    w3 = pl.BlockSpec((GROUPS, BLK, BLK), lambda i: (0, 0, 0))
    return pl.pallas_call(
        body, name="mix_bwd", grid=(nb,),
        in_specs=_proj_specs(nb) + [
            kvp(OFF_K // D_KV), kvp(OFF_V // D_KV), pl.BlockSpec((BLK, 2 * D_A), lambda i: (rev(i), 0)),
            pl.BlockSpec((None, N_KV, Q_PER_KV * BLK, 2 * BLK), lambda i: (rev(i), 0, 0, 0)),
            pl.BlockSpec((None, N_KV, Q_PER_KV * BLK, LANE), lambda i: (rev(i), 0, 0, 0)),
            pl.BlockSpec((None, Q_PER_KV * BLK, LANE), lambda i: (rev(i), 0, 0)),
            tab, tab, tab, tab, vec, vec, w3, w3, pl.BlockSpec((BLK, GROUPS), lambda i: (0, 0))],
        out_specs=[pl.BlockSpec((BLK, D_IN), lambda i: (rev(i), 0)), pl.BlockSpec((8, D_A), lambda i: (0, 0)), w3,
                   pl.BlockSpec((BLK, LANE), lambda i: (0, 0)), pl.BlockSpec((8, LANE), lambda i: (0, 0))],
        out_shape=[jax.ShapeDtypeStruct((s, D_IN), BF16), jax.ShapeDtypeStruct((8, D_A), F32),
                   jax.ShapeDtypeStruct((GROUPS, BLK, BLK), F32), jax.ShapeDtypeStruct((BLK, LANE), F32),
                   jax.ShapeDtypeStruct((8, LANE), F32)],
        scratch_shapes=[pltpu.VMEM((N_KV, 2 * BLK, LANE), BF16), pltpu.VMEM((N_KV, 2 * BLK, LANE), BF16),
                        pltpu.VMEM((BLK, D_A), F32), pltpu.VMEM((N_KV, Q_PER_KV * BLK, LANE), BF16),
                        pltpu.VMEM((N_KV, Q_PER_KV * BLK, LANE), F32), pltpu.VMEM((N_KV, Q_PER_KV * BLK, LANE), F32),
                        pltpu.VMEM((N_KV, 2 * BLK, LANE), F32), pltpu.VMEM((N_KV, 2 * BLK, LANE), F32),
                        pltpu.VMEM((BLK, D_KV), F32), pltpu.VMEM((BLK, D_KV), F32), pltpu.VMEM((BLK, D_B), F32)],
        compiler_params=_params("arbitrary"),
    )(proj, proj, proj, proj, proj, proj, proj, proj, proj, proj, dy, probs, outs, psinks, *tables, ln_g, ln_b,
      w_sp, w_sp_t, b_sp_t)


def _dh_call(dproj, w_bf, x, dx2, scale, norm_g):
    s = x.shape[0]
    tm = min(s, 512)
    tk = W_IN_SHARD
    nk = D_IN // tk

    def body(dp_ref, w_ref, x_ref, dx2_ref, sc_ref, g_ref, gx_ref, st_ref, acc_ref):
        i = pl.program_id(0)
        k = pl.program_id(1)

        @pl.when((i == 0) & (k == 0))
        def _():
            st_ref[...] = jnp.zeros((8, D), F32)

        @pl.when(k == 0)
        def _():
            acc_ref[...] = jnp.zeros((tm, D), F32)

        acc_ref[...] += lax.dot_general(dp_ref[...], w_ref[...], NT, preferred_element_type=F32)

        @pl.when(k == nk - 1)
        def _():
            g = g_ref[...]
            one_sc = 1.0 + sc_ref[...]

            def chunk(n, carry):
                rows = pl.ds(pl.multiple_of(n * BLK, BLK), BLK)
                dh = acc_ref[rows, :]
                xv = x_ref[rows, :]
                r = lax.rsqrt(jnp.mean(xv * xv, axis=-1, keepdims=True) + EPS)
                xn = xv * r
                dhn = dh * one_sc
                dxn = dhn * g
                gx_ref[rows, :] = dx2_ref[rows, :] + r * (dxn - xn * jnp.mean(dxn * xn, axis=-1, keepdims=True))
                st_ref[0:1, :] += jnp.sum(dh, axis=0, keepdims=True)
                st_ref[1:2, :] += jnp.sum(dh * (xn * g), axis=0, keepdims=True)
                st_ref[2:3, :] += jnp.sum(dhn * xn, axis=0, keepdims=True)
                return carry

            lax.fori_loop(0, tm // BLK, chunk, 0)

    vec = pl.BlockSpec((1, D), lambda i, k: (0, 0))
    rows = lambda: pl.BlockSpec((tm, D), lambda i, k: (i, 0))
    return pl.pallas_call(
        body, name="dh", grid=(s // tm, nk),
        in_specs=[pl.BlockSpec((tm, tk), lambda i, k: (i, k)), pl.BlockSpec((D, tk), lambda i, k: (0, k)), rows(), rows(), vec, vec],
        out_specs=[rows(), pl.BlockSpec((8, D), lambda i, k: (0, 0))],
        out_shape=[jax.ShapeDtypeStruct((s, D), F32), jax.ShapeDtypeStruct((8, D), F32)],
        scratch_shapes=[pltpu.VMEM((tm, D), F32)],
        compiler_params=_params("arbitrary", "arbitrary"),
    )(dproj, w_bf, x, dx2, scale, norm_g)


def _adam_math(w, g, m, v):
    m_new = ADAM_B1 * m + (1.0 - ADAM_B1) * g
    v_new = ADAM_B2 * v + (1.0 - ADAM_B2) * (g * g)
    m_hat = m_new / ADAM_C1
    v_hat = v_new / ADAM_C2
    delta = -ADAM_LR * (m_hat / (jnp.sqrt(v_hat) + ADAM_EPS) + ADAM_WD * w)
    return delta, m_new, v_new


def _adam_small_call(tensors):
    n = len(tensors)

    def body(*refs):
        ins, outs = refs[:4 * n], refs[4 * n:]
        for t in range(n):
            w_ref, g_ref, m_ref, v_ref = ins[4 * t:4 * t + 4]
            d, mo, vo = _adam_math(w_ref[...], g_ref[...], m_ref[...], v_ref[...])
            outs[3 * t][...], outs[3 * t + 1][...], outs[3 * t + 2][...] = d, mo, vo

    vm = pl.BlockSpec(memory_space=pltpu.VMEM)
    flat = [a for t in tensors for a in t]
    out = pl.pallas_call(
        body, name="adam_small", in_specs=[vm] * (4 * n), out_specs=[vm] * (3 * n),
        out_shape=[jax.ShapeDtypeStruct(t[0].shape, F32) for t in tensors for _ in range(3)],
        compiler_params=pltpu.CompilerParams(vmem_limit_bytes=VMEM_LIMIT),
    )(*flat)
    return [tuple(out[3 * t:3 * t + 3]) for t in range(n)]


def _adam_halves_call(pos, w, mine, theirs, m, v, name):
    r, n = w.shape
    half = r // 2
    tr = ADAM_ROWS
    nh = half // tr

    def body(pos_ref, w_ref, mine_ref, theirs_ref, m_ref, v_ref, g_ref, d_ref, mo_ref, vo_ref):
        is_mine = (pl.program_id(0) // nh) == pos_ref[1]
        g = jnp.where(is_mine, mine_ref[...], theirs_ref[...])
        g_ref[...] = g
        d_ref[...], mo_ref[...], vo_ref[...] = _adam_math(w_ref[...], g, m_ref[...], v_ref[...])

    spec = lambda: pl.BlockSpec((tr, n), lambda i, pos: (i, 0))

    def half_spec(core_of_half):
        def index(i, pos):
            first = core_of_half(pos) == 0
            active = (i // nh == 0) == first
            return jnp.where(active, i % nh, jnp.where(first, nh - 1, 0)), 0
        return pl.BlockSpec((tr, n), index)

    return pl.pallas_call(
        body, name=name,
        grid_spec=pltpu.PrefetchScalarGridSpec(
            num_scalar_prefetch=1, grid=(r // tr,),
            in_specs=[spec(), half_spec(lambda pos: pos[1]), half_spec(lambda pos: 1 - pos[1]), spec(), spec()],
            out_specs=[spec() for _ in range(4)]),
        out_shape=[jax.ShapeDtypeStruct((r, n), F32)] * 4, compiler_params=_params("arbitrary"),
    )(pos, w, mine, theirs, m, v)


def _adam_outer_call(w, ct, dm, m, v, name):
    r, n = w.shape
    tr = ADAM_ROWS

    def body(w_ref, ct_ref, dm_ref, m_ref, v_ref, g_ref, d_ref, mo_ref, vo_ref):
        g = ct_ref[:, 0:1] * dm_ref[0:1, :]
        for b in range(1, N_DEV):
            g = g + ct_ref[:, b:b + 1] * dm_ref[b:b + 1, :]
        g_ref[...] = g
        d_ref[...], mo_ref[...], vo_ref[...] = _adam_math(w_ref[...], g, m_ref[...], v_ref[...])

    spec = lambda: pl.BlockSpec((tr, n), lambda i: (i, 0))
    return pl.pallas_call(
        body, name=name, grid=(r // tr,),
        in_specs=[spec(), pl.BlockSpec((tr, N_DEV), lambda i: (i, 0)), pl.BlockSpec((N_DEV, n), lambda i: (0, 0)), spec(), spec()],
        out_specs=[spec() for _ in range(4)],
        out_shape=[jax.ShapeDtypeStruct((r, n), F32)] * 4, compiler_params=_params("parallel"),
    )(w, ct, dm, m, v)


def _sum_pieces_call(pos, part, part_block, recvs, name):
    r, n = recvs[0].shape[1:]
    tr = min(r, 256)
    nrb = r // tr

    def body(pos_ref, p_ref, *refs):
        acc = p_ref[...].astype(F32)
        for r_ref in refs[:-1]:
            for d in range(r_ref.shape[0]):
                acc = acc + r_ref[d].astype(F32)
        refs[-1][...] = acc

    return pl.pallas_call(
        body, name=name,
        grid_spec=pltpu.PrefetchScalarGridSpec(
            num_scalar_prefetch=1, grid=(nrb,),
            in_specs=[pl.BlockSpec((tr, n), lambda i, pos: part_block(i, pos, nrb))] + [
                pl.BlockSpec((rv.shape[0], tr, n), lambda i, pos: (0, i, 0)) for rv in recvs],
            out_specs=pl.BlockSpec((tr, n), lambda i, pos: (i, 0))),
        out_shape=jax.ShapeDtypeStruct((r, n), F32), compiler_params=_params("parallel"),
    )(pos, part, *recvs)


def _coords():
    return lax.axis_index("x"), lax.axis_index("y"), lax.axis_index("c")


CAST_ROWS = 256


def _allgather_sum_call(blk, name, with_sum, cast=None):
    m_per, n = blk.shape
    n_out = 2 if with_sum else 1
    if cast is not None:
        w, full_shape = cast
        wr, wn = w.shape
        by_cols = full_shape[0] == wr
        tr = min(wr, CAST_ROWS)
        n_chunk = wr // tr

    def body(*refs):
        x_ref = refs[0]
        out_ref = refs[1 + (cast is not None)]
        rest = refs[1 + (cast is not None) + n_out + (cast is not None):]
        send_sems, recv_sems, local_sem = rest[:3]
        x, y, c = _coords()
        me, sibling = (x, y, c), (x, y, 1 - c)
        chips = [(1 - x, y), (x, 1 - y), (1 - x, 1 - y)]

        def rows(px, py, pc):
            return out_ref.at[pl.ds((4 * px + 2 * py + pc) * m_per, m_per), :]

        def copy(k, block, to, src=None):
            return pltpu.make_async_remote_copy(
                src_ref=rows(*block) if src is None else src, dst_ref=rows(*block),
                send_sem=send_sems.at[k], recv_sem=recv_sems.at[k], device_id=to, device_id_type=MESH)

        mine = pltpu.make_async_copy(x_ref, rows(*me), local_sem)
        mine.start()
        first = [copy(0, me, sibling, src=x_ref)]
        first += [copy(1 + j, me, (*chip, c), src=x_ref) for j, chip in enumerate(chips)]
        for cp in first:
            cp.start()

        if cast is not None:
            w_ref, full_ref = refs[1], refs[1 + 1 + n_out]
            f32_buf, bf16_buf, in_sems, out_sems = rest[3:]
            chip_no = 2 * x + y

            def fetch(i):
                return pltpu.make_async_copy(w_ref.at[pl.ds(i * tr, tr), :], f32_buf.at[i % 2], in_sems.at[i % 2])

            def store(i):
                if by_cols:
                    dst = full_ref.at[pl.ds(i * tr, tr), pl.ds(chip_no * wn, wn)]
                else:
                    dst = full_ref.at[pl.ds(chip_no * wr + i * tr, tr), :]
                return pltpu.make_async_copy(bf16_buf.at[i % 2], dst, out_sems.at[i % 2])

            fetch(0).start()
            for i in range(n_chunk):
                if i + 1 < n_chunk:
                    fetch(i + 1).start()
                fetch(i).wait()
                if i >= 2:
                    store(i - 2).wait()
                bf16_buf[i % 2] = f32_buf[i % 2].astype(BF16)
                store(i).start()
            for i in range(max(n_chunk - 2, 0), n_chunk):
                store(i).wait()

        passed = [copy(4 + j, (*chip, c), sibling) for j, chip in enumerate(chips)]
        for j, chip in enumerate(chips):
            copy(1 + j, (*chip, c), me).wait_recv()
            passed[j].start()
        copy(0, sibling, me).wait_recv()
        for j, chip in enumerate(chips):
            copy(4 + j, (*chip, 1 - c), me).wait_recv()
        for cp in first + passed:
            cp.wait_send()
        mine.wait()
        if with_sum:
            sum_ref = refs[1 + (cast is not None) + 1]
            acc = out_ref[0:m_per, :]
            for d in range(1, N_DEV):
                acc = acc + out_ref[d * m_per:(d + 1) * m_per, :]
            sum_ref[...] = acc

    vm = pl.BlockSpec(memory_space=pltpu.VMEM)
    anyspec = pl.BlockSpec(memory_space=pl.ANY)
    out_shape = [jax.ShapeDtypeStruct((N_DEV * m_per, n), F32)]
    if with_sum:
        out_shape.append(jax.ShapeDtypeStruct((m_per, n), F32))
    in_specs, out_specs, operands = [vm], [vm] * n_out, [blk]
    scratch = [pltpu.SemaphoreType.DMA((7,)), pltpu.SemaphoreType.DMA((7,)), pltpu.SemaphoreType.DMA]
    if cast is not None:
        in_specs.append(anyspec)
        operands.append(w)
        out_shape.append(jax.ShapeDtypeStruct(full_shape, BF16))
        out_specs.append(anyspec)
        scratch += [pltpu.VMEM((2, tr, wn), F32), pltpu.VMEM((2, tr, wn), BF16), pltpu.SemaphoreType.DMA((2,)),
                    pltpu.SemaphoreType.DMA((2,))]
    return pl.pallas_call(
        body, name=name, out_shape=out_shape, in_specs=in_specs, out_specs=out_specs, scratch_shapes=scratch,
        compiler_params=pltpu.CompilerParams(vmem_limit_bytes=VMEM_LIMIT),
    )(*operands)


HBM_SPEC = pl.BlockSpec(memory_space=pltpu.HBM)
SEM_SPEC = pl.BlockSpec(memory_space=pltpu.SEMAPHORE)
SIDE_EFFECT = pltpu.SideEffectType.DATAFLOW_SIDE_EFFECTING


def _peer(x, y, c, q, cb):
    return (1 - x if q & 2 else x, 1 - y if q & 1 else y, 1 - c if cb else c)


def _w_in_piece(slots):
    def piece(part_ref, k, to):
        return part_ref.at[pl.ds(to[2] * (D // 2), D // 2), pl.ds(slots[k] * W_IN_SHARD, W_IN_SHARD)]
    return piece


def _w_out_piece(part_ref, k, to):
    ho = W_OUT_SHARD // 2
    return part_ref.at[pl.ds((2 * to[0] + to[1]) * W_OUT_SHARD + to[2] * ho, ho), :]


def _group_piece(part_ref, k, to):
    return part_ref.at[4 * to[0] + 2 * to[1] + to[2]]


def _whole_piece(part_ref, k, to):
    return part_ref


def _exchange_start_call(groups, name):
    ng = len(groups)
    lands = [lax.empty((len(rels),) + slot_shape, part.dtype) for part, rels, _, slot_shape in groups]

    def body(*refs):
        ins, outs = refs[:2 * ng], refs[2 * ng:]
        x, y, c = _coords()
        for g, (_, rels, piece, _) in enumerate(groups):
            part_ref, land_ref = ins[2 * g], ins[2 * g + 1]
            send_sems, recv_sems = outs[4 * g], outs[4 * g + 1]
            for k, (q, cb) in enumerate(rels):
                to = _peer(x, y, c, q, cb)
                pltpu.make_async_remote_copy(src_ref=piece(part_ref, k, to), dst_ref=land_ref.at[k], send_sem=send_sems.at[k],
                                             recv_sem=recv_sems.at[k], device_id=to, device_id_type=MESH).start()
        outs[-1][...] = jnp.zeros_like(outs[-1])

    out_shape, out_specs, operands = [], [], []
    for (part, rels, _, _), land in zip(groups, lands):
        n = len(rels)
        out_shape += [pltpu.SemaphoreType.DMA((n,)), pltpu.SemaphoreType.DMA((n,)), pltpu.HBM(part.shape, part.dtype),
                      pltpu.HBM(land.shape, land.dtype)]
        out_specs += [SEM_SPEC, SEM_SPEC, HBM_SPEC, HBM_SPEC]
        operands += [pltpu.with_memory_space_constraint(part, pltpu.HBM), pltpu.with_memory_space_constraint(land, pltpu.HBM)]
    out = pl.pallas_call(
        body, name=name,
        out_shape=tuple(out_shape) + (jax.ShapeDtypeStruct((1, 1), F32),),
        in_specs=(HBM_SPEC,) * (2 * ng), out_specs=tuple(out_specs) + (pl.BlockSpec(memory_space=pltpu.VMEM),),
        input_output_aliases={j: 4 * (j // 2) + 2 + j % 2 for j in range(2 * ng)},
        compiler_params=pltpu.CompilerParams(has_side_effects=SIDE_EFFECT),
    )(*operands)
    return [tuple(out[4 * g:4 * g + 4]) for g in range(ng)], out[-1]


def _exchange_wait_call(started, groups, after, name):
    ng = len(groups)

    def body(*refs):
        ins = refs[:4 * ng]
        x, y, c = _coords()
        for g, (_, rels, piece, _) in enumerate(groups):
            part_ref, land_ref, send_sems, recv_sems = ins[4 * g:4 * g + 4]
            for k, (q, cb) in enumerate(rels):
                to = _peer(x, y, c, q, cb)
                cp = pltpu.make_async_remote_copy(src_ref=piece(part_ref, k, to), dst_ref=land_ref.at[k], send_sem=send_sems.at[k],
                                                  recv_sem=recv_sems.at[k], device_id=to, device_id_type=MESH)
                cp.wait_send()
                cp.wait_recv()

    operands, in_specs, out_shape = [], [], []
    for send_sems, recv_sems, part_thru, land_thru in started:
        operands += [part_thru, land_thru, send_sems, recv_sems]
        in_specs += [HBM_SPEC, HBM_SPEC, SEM_SPEC, SEM_SPEC]
        out_shape += [pltpu.HBM(part_thru.shape, part_thru.dtype), pltpu.HBM(land_thru.shape, land_thru.dtype)]
    out = pl.pallas_call(
        body, name=name, out_shape=tuple(out_shape),
        in_specs=tuple(in_specs) + (pl.BlockSpec(memory_space=pl.ANY),), out_specs=(HBM_SPEC,) * (2 * ng),
        input_output_aliases={4 * g + j: 2 * g + j for g in range(ng) for j in range(2)},
        compiler_params=pltpu.CompilerParams(has_side_effects=SIDE_EFFECT),
    )(*operands, after)
    return [tuple(out[2 * g:2 * g + 2]) for g in range(ng)]


def _rope_tables(s):
    inv_freq = np.float32(10000.0) ** (-np.arange(0, HEAD, 2, dtype=np.float32) / np.float32(HEAD))
    ang = np.arange(s, dtype=np.float32)[:, None] * inv_freq[None, :]
    cos = np.tile(np.cos(ang), (1, LANE // (HEAD // 2))).astype(np.float32)
    sin = np.tile(np.sin(ang), (1, LANE // (HEAD // 2))).astype(np.float32)
    first_half = (np.arange(LANE) % HEAD) < (HEAD // 2)
    sin = np.where(first_half[None, :], -sin, sin)
    behind = lambda t: np.concatenate([t[:BLK], t[:-BLK]], axis=0)
    return tuple(jnp.asarray(t) for t in (cos, sin, behind(cos), behind(sin)))


def kernel(x, c, w_ada, b_ada, norm_g, w_in, ln_v_g, ln_v_b, w_spatial, b_spatial, sinks, w_out, w_ada_final, b_ada_final, final_norm_g, loss_target, m_w_ada, m_b_ada, m_norm_g, m_w_in, m_ln_v_g, m_ln_v_b, m_w_spatial, m_b_spatial, m_sinks, m_w_out, m_w_ada_final, m_b_ada_final, m_final_norm_g, v_w_ada, v_b_ada, v_norm_g, v_w_in, v_ln_v_g, v_ln_v_b, v_w_spatial, v_b_spatial, v_sinks, v_w_out, v_w_ada_final, v_b_ada_final, v_final_norm_g):
    s = x.shape[1]
    ax, ay, ac = _coords()
    chip = 2 * ax + ay
    me = 4 * ax + 2 * ay + ac
    n_ada = w_ada.shape[2]
    n_adaf = w_ada_final.shape[1]

    x2d = x.reshape(s, D)
    tgt = loss_target.reshape(s, D)
    w_ada2, w_in2, w_out2 = w_ada[0], w_in[0], w_out[0]
    b_ada_f2 = b_ada_final.reshape(1, 2 * D)
    gf = final_norm_g.reshape(1, D)

    c_all, w_in_own = _allgather_sum_call(jnp.pad(c, ((0, 7), (0, 0))), "gather_c", False, cast=(w_in2, (D, D_IN)))
    c_all = c_all[::8]
    mod_p, c_act = _rowmat_call(c_all, w_ada2, lax.dynamic_slice(b_ada, (0, chip * n_ada), (1, n_ada)), "mod")
    modf_p, _ = _rowmat_call(c_all, w_ada_final, lax.dynamic_slice(b_ada_f2, (0, chip * n_adaf), (1, n_adaf)), "mod_final")
    mods, w_out_own = _allgather_sum_call(jnp.concatenate([mod_p, modf_p], axis=1), "gather_mod", False, cast=(w_out2, (D, D)))
    my_rows = [lax.dynamic_slice(mods, (16 * j + me, 0), (1, n_ada + n_adaf)) for j in range(N_CHIP)]
    mod = jnp.concatenate([r[:, :n_ada] for r in my_rows], axis=1)
    mod_f = jnp.concatenate([r[:, n_ada:] for r in my_rows], axis=1)
    shift, scale, gate = mod[:, :D], mod[:, D:2 * D], mod[:, 2 * D:]
    shift_f, scale_f = mod_f[:, :D], mod_f[:, D:]

    pos = jnp.stack([chip, ac]).astype(jnp.int32)

    tables = _rope_tables(s)
    cos, sin = tables[:2]
    b_sp_t = b_spatial[0].T
    sinks1 = sinks.reshape(N_Q)
    h, proj, w_in_bf, w_out_bf = _proj_gather_call(pos, x2d, shift, scale, norm_g, w_in_own, w_out_own)
    y, probs, attn_out, psinks = _mix_fwd_call(proj, cos, sin, ln_v_g, ln_v_b, w_spatial[0], b_sp_t, sinks1)
    dx2, do, dy, st_tail = _tail_call(y, w_out_bf, x2d, tgt, gate, shift_f, scale_f, gf)

    rel_o = [(0, 1), (1, 0), (1, 1), (2, 0), (2, 1), (3, 0), (3, 1)]
    rel_a = [(1, 0), (1, 1), (2, 0), (2, 1)]
    rel_b = [(3, 0), (3, 1), (0, 1)]
    piece_a, piece_b = _w_in_piece([0, 0, 1, 1]), _w_in_piece([0, 0, 1])
    half_in, half_out = (D // 2, W_IN_SHARD), (W_OUT_SHARD // 2, D)

    g_w_out_p = _tn_call(y, do, "grad_w_out")
    grp_o = [(g_w_out_p, rel_o, _w_out_piece, half_out)]
    st_o, tok_o = _exchange_start_call(grp_o, "send_w_out")
    dproj, st_ln, d_wsp, d_bsp_t, d_sink = _mix_bwd_call(
        proj, dy, probs, attn_out, psinks, tables, ln_v_g + tok_o, ln_v_b, w_spatial[0], jnp.swapaxes(w_spatial[0], 1, 2),
        b_sp_t)
    g_w_in_a = _tn_shards_call(pos, h, dproj, (1, 2), "grad_w_in_a")
    grp_a = [(g_w_in_a, rel_a, piece_a, half_in), (d_wsp, rel_o, _group_piece, (BLK, BLK))]
    st_a, tok_a = _exchange_start_call(grp_a, "send_w_in_a")
    g_w_in_b = _tn_shards_call(pos, h, dproj, (3, 0), "grad_w_in_b")
    grp_b = [(g_w_in_b, rel_b, piece_b, half_in)]
    st_b, tok_b = _exchange_start_call(grp_b, "send_w_in_b")
    grad_x, st_dh = _dh_call(dproj, w_in_bf, x2d, dx2, scale + (tok_a + tok_b), norm_g)

    ((g_w_out_p, recv_o),) = _exchange_wait_call(st_o, grp_o, st_dh, "wait_w_out")
    (_, recv_a), (d_wsp, recv_s) = _exchange_wait_call(st_a, grp_a, st_dh, "wait_w_in_a")
    ((g_w_in_b, recv_b),) = _exchange_wait_call(st_b, grp_b, st_dh, "wait_w_in_b")
    mine_in = _sum_pieces_call(pos, g_w_in_b, lambda i, p, nrb: (p[1] * nrb + i, 1), [recv_a, recv_b], "sum_w_in")
    mine_out = _sum_pieces_call(pos, g_w_out_p, lambda i, p, nrb: ((2 * p[0] + p[1]) * nrb + i, 0), [recv_o], "sum_w_out")
    wsp_group = _sum_pieces_call(pos, d_wsp.reshape(GROUPS * BLK, BLK), lambda i, p, nrb: (2 * p[0] + p[1], 0), [recv_s],
                                 "sum_w_spatial")
    to_sibling = [(0, 1)]
    grp_p = [(mine_in, to_sibling, _whole_piece, half_in), (mine_out, to_sibling, _whole_piece, half_out)]
    st_p, tok_p = _exchange_start_call(grp_p, "swap_halves")

    misc = jnp.concatenate([st_ln, d_bsp_t[:, :GROUPS].T, d_sink, jnp.zeros((8, D - D_A - 2 * LANE), F32)], axis=1)
    pack = jnp.concatenate([wsp_group.reshape(8, D) + tok_p, st_tail, st_dh, misc], axis=0)
    rows = pack.shape[0]
    packs, tot = _allgather_sum_call(pack, "gather_small", True)
    packs = packs.reshape(N_DEV, rows, D)
    dmod_all = jnp.concatenate([packs[:, 16, :], packs[:, 17, :], packs[:, 11, :]], axis=1)
    dmodf_all = jnp.concatenate([packs[:, 8, :], packs[:, 9, :]], axis=1)
    loss = tot[13, 0]
    (mine_in, theirs_in), (mine_out, theirs_out) = _exchange_wait_call(st_p, grp_p, tot, "swapped_halves")
    small = {
        "b_ada": jnp.concatenate([tot[16:17], tot[17:18], tot[11:12]], axis=1),
        "norm_g": tot[18:19],
        "ln_v_g": tot[24:25, :D_A],
        "ln_v_b": tot[25:26, :D_A],
        "w_spatial": packs[:, 0:8, :].reshape(GROUPS * BLK, BLK),
        "b_spatial": tot[24:32, D_A:D_A + BLK],
        "sinks": tot[24:25, D_A + LANE:D_A + LANE + N_Q],
        "b_ada_final": jnp.concatenate([tot[8:9], tot[9:10]], axis=1),
        "final_norm_g": tot[10:11],
    }

    weights = dict(w_ada=w_ada, b_ada=b_ada, norm_g=norm_g, w_in=w_in, ln_v_g=ln_v_g, ln_v_b=ln_v_b, w_spatial=w_spatial,
                   b_spatial=b_spatial, sinks=sinks, w_out=w_out, w_ada_final=w_ada_final, b_ada_final=b_ada_final,
                   final_norm_g=final_norm_g)
    m_in = dict(w_ada=m_w_ada, b_ada=m_b_ada, norm_g=m_norm_g, w_in=m_w_in, ln_v_g=m_ln_v_g, ln_v_b=m_ln_v_b,
                w_spatial=m_w_spatial, b_spatial=m_b_spatial, sinks=m_sinks, w_out=m_w_out, w_ada_final=m_w_ada_final,
                b_ada_final=m_b_ada_final, final_norm_g=m_final_norm_g)
    v_in = dict(w_ada=v_w_ada, b_ada=v_b_ada, norm_g=v_norm_g, w_in=v_w_in, ln_v_g=v_ln_v_g, ln_v_b=v_ln_v_b,
                w_spatial=v_w_spatial, b_spatial=v_b_spatial, sinks=v_sinks, w_out=v_w_out, w_ada_final=v_w_ada_final,
                b_ada_final=v_b_ada_final, final_norm_g=v_final_norm_g)
    c_act_t = c_act.T
    outer = {"w_ada": lax.dynamic_slice(dmod_all, (0, chip * n_ada), (N_DEV, n_ada)),
             "w_ada_final": lax.dynamic_slice(dmodf_all, (0, chip * n_adaf), (N_DEV, n_adaf))}
    halves = {"w_in": (mine_in, theirs_in[0]), "w_out": (mine_out, theirs_out[0])}
    done = {}
    for name, (mine, theirs) in halves.items():
        shape2 = (2 * mine.shape[0], mine.shape[1])
        done[name] = _adam_halves_call(pos, weights[name].reshape(shape2), mine, theirs, m_in[name].reshape(shape2),
                                       v_in[name].reshape(shape2), "adam_" + name)
    for name, dm in outer.items():
        shape2 = (D, dm.shape[1])
        done[name] = _adam_outer_call(weights[name].reshape(shape2), c_act_t, dm, m_in[name].reshape(shape2),
                                      v_in[name].reshape(shape2), "adam_" + name)
    updates = _adam_small_call([(weights[name].reshape(g.shape), g, m_in[name].reshape(g.shape), v_in[name].reshape(g.shape))
                                for name, g in small.items()])
    for (name, g), upd in zip(small.items(), updates):
        done[name] = (g, *upd)
    outs = [[done[name][k].reshape(w.shape) for name, w in weights.items()] for k in range(4)]
    return (loss, grad_x.reshape(x.shape), *outs[0], *outs[1], *outs[2], *outs[3])
```

```python
import numpy as np
import jax
import jax.numpy as jnp
from jax import lax
from jax.experimental import pallas as pl
from jax.experimental.pallas import tpu as pltpu

F32 = jnp.float32
BF16 = jnp.bfloat16
MESH = pl.DeviceIdType.MESH

D = 2048
D_A = 1024
D_B = 1024
D_KV = 256
HEAD = 64
N_Q = 16
N_KV = 4
Q_PER_KV = N_Q // N_KV
BLK = 128
GROUPS = 8
D_IN = 5632
OFF_Q, OFF_K, OFF_V, OFF_ZB = 3072, 4096, 4352, 4608
N_CHIP = 4
N_DEV = 8
W_IN_SHARD = D_IN // N_CHIP
W_OUT_SHARD = D // N_CHIP
EPS = 1e-5
SCALE = HEAD ** -0.5
NEG = -1e30
LANE = 128
VMEM_LIMIT = 56 * 1024 * 1024

ADAM_LR, ADAM_B1, ADAM_B2, ADAM_EPS, ADAM_WD, ADAM_STEP = 0.001, 0.9, 0.999, 1e-08, 0.01, 10
ADAM_C1 = 1.0 - ADAM_B1 ** ADAM_STEP
ADAM_C2 = 1.0 - ADAM_B2 ** ADAM_STEP
ADAM_ROWS = 256

NT = (((1,), (1,)), ((), ()))
TN = (((0,), (0,)), ((), ()))


def _params(*sem):
    return pltpu.CompilerParams(dimension_semantics=sem, vmem_limit_bytes=VMEM_LIMIT)


def _silu_parts(z):
    sig = 1.0 / (1.0 + jnp.exp(-z))
    return z * sig, sig


def _swap_halves(v, first_half):
    return jnp.where(first_half, pltpu.roll(v, 96, 1), pltpu.roll(v, 32, 1))


def _rope(v, cos_t, sin_s, first_half):
    return v * cos_t + _swap_halves(v, first_half) * sin_s


def _unrope(dv, cos_t, sin_s, first_half):
    return dv * cos_t - _swap_halves(dv, first_half) * sin_s


def _lane_masks():
    lane = lax.broadcasted_iota(jnp.int32, (BLK, LANE), 1)
    return (lane % HEAD) < (HEAD // 2), lane < HEAD


def _band_valid(first_block_bound, rows=BLK):
    rr = lax.broadcasted_iota(jnp.int32, (rows, 2 * BLK), 0) & (BLK - 1)
    jj = lax.broadcasted_iota(jnp.int32, (rows, 2 * BLK), 1)
    return (jj > rr) & (jj <= rr + BLK) & (jj >= first_block_bound)


def _dup_kv(slab, lo):
    rolled = pltpu.roll(slab, HEAD, 1)
    return jnp.where(lo, slab, rolled).astype(BF16), jnp.where(lo, rolled, slab).astype(BF16)


def _fold_halves(a, b, lo):
    return jnp.where(lo, a, b) + pltpu.roll(jnp.where(lo, b, a), HEAD, 1)


def _stack_heads(ref, sb, slab, lo, dtype):
    kh, base = sb // 2, 2 * (sb % 2) * BLK
    zero = jnp.zeros_like(slab)
    ref[kh, base:base + BLK, :] = jnp.where(lo, slab, zero).astype(dtype)
    ref[kh, base + BLK:base + 2 * BLK, :] = jnp.where(lo, zero, slab).astype(dtype)


def _unstack_heads(ref, sb, lo):
    kh, base = sb // 2, 2 * (sb % 2) * BLK
    return jnp.where(lo, ref[kh, base:base + BLK, :], ref[kh, base + BLK:base + 2 * BLK, :])


def _sink_column(sinks_ref, kh):
    row = lax.broadcasted_iota(jnp.int32, (Q_PER_KV * BLK, 1), 0)
    col = jnp.full(row.shape, sinks_ref[Q_PER_KV * kh + Q_PER_KV - 1], F32)
    for n in range(Q_PER_KV - 2, -1, -1):
        col = jnp.where(row < (n + 1) * BLK, sinks_ref[Q_PER_KV * kh + n], col)
    return col


def _tril():
    t = lax.broadcasted_iota(jnp.int32, (BLK, BLK), 0)
    s = lax.broadcasted_iota(jnp.int32, (BLK, BLK), 1)
    return s <= t


def _layer_norm_fwd(va, lg, lb):
    mu = jnp.mean(va, axis=-1, keepdims=True)
    xc = va - mu
    rstd = lax.rsqrt(jnp.mean(xc * xc, axis=-1, keepdims=True) + EPS)
    vhat = xc * rstd
    return vhat, rstd, vhat * lg + lb


def _softmax_sink(qm, kdup, bias, sink):
    s = lax.dot_general(qm, kdup, NT, preferred_element_type=F32) + bias
    m = jnp.maximum(jnp.max(s, axis=-1, keepdims=True), sink)
    p = jnp.exp(s - m)
    esink = jnp.exp(sink - m)
    inv = 1.0 / (jnp.sum(p, axis=-1, keepdims=True) + esink)
    return p * inv, esink * inv


def _band_bias(bias_ref):
    rows = bias_ref.shape[1]
    bias_ref[0] = jnp.where(_band_valid(BLK, rows), 0.0, NEG)
    bias_ref[1] = jnp.where(_band_valid(0, rows), 0.0, NEG)


def _rowmat_call(c_all, w, b, name):
    n = w.shape[1]
    tn = 512

    def body(c_ref, w_ref, b_ref, o_ref, ca_ref):
        ca, _ = _silu_parts(c_ref[...])
        ca_ref[...] = ca
        o_ref[...] = jnp.dot(ca.astype(BF16), w_ref[...].astype(BF16), preferred_element_type=F32) + b_ref[...]

    return pl.pallas_call(
        body, name=name, grid=(n // tn,),
        in_specs=[pl.BlockSpec((N_DEV, D), lambda j: (0, 0)), pl.BlockSpec((D, tn), lambda j: (0, j)),
                  pl.BlockSpec((1, tn), lambda j: (0, j))],
        out_specs=[pl.BlockSpec((N_DEV, tn), lambda j: (0, j)), pl.BlockSpec((N_DEV, D), lambda j: (0, 0))],
        out_shape=[jax.ShapeDtypeStruct((N_DEV, n), F32), jax.ShapeDtypeStruct((N_DEV, D), F32)],
        compiler_params=_params("arbitrary"),
    )(c_all, w, b)


W_IN_PARTS = ((0, 768), (768, 640))
OUT_STREAMS = 4
X_STREAMS = 4


def _proj_gather_call(pos, x, shift, scale, norm_g, wi_full, wo_full):
    s = x.shape[0]
    tm = min(s, 512)
    nrow = s // tm
    hi = D // 2
    ho = W_OUT_SHARD // 2
    phases = [(0, None), (1, 0), (2, 0), (1, 1), (2, 1), (3, 0), (3, 1)]

    def body(pos_ref, *refs):
        x_refs = refs[:X_STREAMS]
        (sh_ref, sc_ref, g_ref, _, _, h_ref, proj_ref, fi_ref, fo_ref,
         h_all, wbuf, obuf, send_sems, recv_sems, load_sems, out_sems) = refs[X_STREAMS:]
        p = pl.program_id(0)
        i = pl.program_id(1)
        x_, y_, c_ = _coords()
        me, sibling = (x_, y_, c_), (x_, y_, 1 - c_)

        def shard_of(q):
            px, py, _ = _peer(x_, y_, c_, q, 0)
            return 2 * px + py

        def cols_of(q, cp):
            off, w = (0, W_IN_SHARD) if cp is None else W_IN_PARTS[cp]
            return shard_of(q) * W_IN_SHARD + off, w

        def part(which, q, pc, sub, cp):
            n = hi if which == 0 else ho
            base = pc * n
            if sub is not None:
                n //= 2
                base = base + sub * n
            if which == 0:
                c0, w = cols_of(q, cp)
                return fi_ref.at[pl.ds(base, n), pl.ds(c0, w)]
            return fo_ref.at[pl.ds(shard_of(q) * W_OUT_SHARD + base, n), :]

        def copy(k, ref, to):
            return pltpu.make_async_remote_copy(src_ref=ref, dst_ref=ref, send_sem=send_sems.at[k], recv_sem=recv_sems.at[k],
                                                device_id=to, device_id_type=MESH)

        def sem(which, kind, j, cp):
            return 4 * kind + 2 * cp + j if which == 0 else 16 + 2 * kind + j

        def to_neighbour(which, q, cp=None):
            return copy(sem(which, 0, q - 1, cp), part(which, 0, c_, None, cp), _peer(x_, y_, c_, q, 0))

        def from_neighbour(which, q, cp=None):
            return copy(sem(which, 0, q - 1, cp), part(which, q, c_, None, cp), me)

        def relay(which, q, cp=None):
            return copy(sem(which, 1, q - 1, cp), part(which, q, c_, q - 1, cp), _peer(x_, y_, c_, 3 - q, 0))

        def relayed(which, sub, cp=None):
            return copy(sem(which, 1, sub, cp), part(which, 3, c_, sub, cp), me)

        def to_sibling(which, q, cp=None):
            return copy(sem(which, 2, q - 1, cp), part(which, q, c_, None, cp), sibling)

        def from_sibling(which, q, cp=None):
            return copy(sem(which, 2, q - 1, cp), part(which, q, 1 - c_, None, cp), me)

        def relayed_to_sibling(which, sub, cp=None):
            return copy(sem(which, 3, sub, cp), part(which, 3, c_, sub, cp), sibling)

        def relayed_from_sibling(which, sub, cp=None):
            return copy(sem(which, 3, sub, cp), part(which, 3, 1 - c_, sub, cp), me)

        def pass_on_neighbours(which, cp=None):
            for q in (1, 2):
                from_neighbour(which, q, cp).wait_recv()
                to_sibling(which, q, cp).start()
                relay(which, q, cp).start()

        def pass_on_relayed(which, cp=None):
            for sub in range(2):
                relayed(which, sub, cp).wait_recv()
                relayed_to_sibling(which, sub, cp).start()

        def shard_load(k):
            c0, w = cols_of(*phases[k])
            return pltpu.make_async_copy(fi_ref.at[:, pl.ds(c0, w)], wbuf.at[k % 2, :, 0:w], load_sems.at[k % 2])

        class OutCopies:
            def __init__(self, k, slot, row0):
                c0, w = cols_of(*phases[k])
                strip = tm // OUT_STREAMS
                self.copies = [pltpu.make_async_copy(obuf.at[slot, n * strip:(n + 1) * strip, 0:w],
                                                     proj_ref.at[pl.ds(row0 + n * strip, strip), pl.ds(c0, w)],
                                                     out_sems.at[slot, n]) for n in range(OUT_STREAMS)]

            def start(self):
                for cp in self.copies:
                    cp.start()

            def wait(self):
                for cp in self.copies:
                    cp.wait()

        out_copy = OutCopies

        def drain(k):
            for j in range(min(2, nrow)):
                out_copy(k, (nrow - 1 - j) % 2, 0).wait()

        def arrivals(k):
            q, cp = phases[k]
            if k == 0:
                for cp_ in range(2):
                    for q_ in (1, 2):
                        to_neighbour(0, q_, cp_).start()
            elif q < 3 and k in (1, 3):
                pass_on_neighbours(0, cp)
                if k == 1:
                    for q_ in (1, 2):
                        to_neighbour(1, q_).start()
            elif k == 5:
                for cp_ in range(2):
                    pass_on_relayed(0, cp_)
                pass_on_neighbours(1)
            if q in (1, 2):
                from_sibling(0, q, cp).wait_recv()
            elif q == 3:
                for sub in range(2):
                    relayed_from_sibling(0, sub, cp).wait_recv()

        rows = pl.ds(pl.multiple_of(i * tm, tm), tm)
        slot = i % 2
        for k, (q, cp) in enumerate(phases):
            @pl.when(p == k)
            def _(k=k, q=q, cp=cp):
                @pl.when(i == 0)
                def _():
                    if k == 0:
                        arrivals(0)
                        shard_load(0).start()
                    else:
                        drain(k - 1)
                    shard_load(k).wait()

                if k + 1 < len(phases):
                    @pl.when(i == max(nrow - 2, 0))
                    def _():
                        arrivals(k + 1)
                        shard_load(k + 1).start()

                if k == 0:
                    wx = D // X_STREAMS
                    ssq = sum(jnp.sum(xr[...] * xr[...], axis=-1, keepdims=True) for xr in x_refs)
                    r = lax.rsqrt(ssq * (1.0 / D) + EPS)
                    for n, xr in enumerate(x_refs):
                        cols = slice(n * wx, (n + 1) * wx)
                        hv = ((xr[...] * r * g_ref[:, cols]) * (1.0 + sc_ref[:, cols]) + sh_ref[:, cols]).astype(BF16)
                        h_ref[:, cols] = hv
                        h_all[rows, cols] = hv

                @pl.when(i >= 2)
                def _():
                    out_copy(k, slot, 0).wait()

                w = cols_of(q, cp)[1]
                obuf[slot, :, 0:w] = jnp.dot(h_all[rows, :], wbuf[k % 2, :, 0:w], preferred_element_type=F32)
                out_copy(k, slot, pl.multiple_of(i * tm, tm)).start()

        @pl.when((p == len(phases) - 1) & (i == nrow - 1))
        def _():
            drain(len(phases) - 1)
            pass_on_relayed(1)
            for q in (1, 2):
                from_sibling(1, q).wait_recv()
            for sub in range(2):
                relayed_from_sibling(1, sub).wait_recv()
            for which, cps in ((0, (0, 1)), (1, (None,))):
                for cp in cps:
                    for q in (1, 2):
                        to_neighbour(which, q, cp).wait_send()
                        relay(which, q, cp).wait_send()
                        to_sibling(which, q, cp).wait_send()
                        relayed_to_sibling(which, q - 1, cp).wait_send()

    vec = pl.BlockSpec((1, D), lambda p, i, pos: (0, 0))
    first_phase_rows = lambda p, i, pos: (jnp.where(p == 0, i, nrow - 1), 0)
    anyspec = pl.BlockSpec(memory_space=pl.ANY)
    x_spec = lambda n: pl.BlockSpec((tm, D // X_STREAMS), lambda p, i, pos: (jnp.where(p == 0, i, nrow - 1), n))
    return pl.pallas_call(
        body, name="proj_gather",
        grid_spec=pltpu.PrefetchScalarGridSpec(
            num_scalar_prefetch=1, grid=(len(phases), nrow),
            in_specs=[x_spec(n) for n in range(X_STREAMS)] + [vec, vec, vec, anyspec, anyspec],
            out_specs=[pl.BlockSpec((tm, D), first_phase_rows), anyspec, anyspec, anyspec],
            scratch_shapes=[pltpu.VMEM((s, D), BF16), pltpu.VMEM((2, D, W_IN_SHARD), BF16), pltpu.VMEM((2, tm, W_IN_SHARD), F32),
                            pltpu.SemaphoreType.DMA((24,)), pltpu.SemaphoreType.DMA((24,)), pltpu.SemaphoreType.DMA((2,)),
                            pltpu.SemaphoreType.DMA((2, OUT_STREAMS))]),
        out_shape=[jax.ShapeDtypeStruct((s, D), BF16), jax.ShapeDtypeStruct((s, D_IN), F32),
                   jax.ShapeDtypeStruct((D, D_IN), BF16), jax.ShapeDtypeStruct((D, D), BF16)],
        input_output_aliases={X_STREAMS + 4: 2, X_STREAMS + 5: 3},
        compiler_params=_params("arbitrary", "arbitrary"),
    )(pos, *([x] * X_STREAMS), shift, scale, norm_g, wi_full, wo_full)


def _proj_specs(rev_nb=None):
    if rev_nb is None:
        row = lambda i: i
    else:
        row = lambda i: rev_nb - 1 - i
    wide = lambda col: pl.BlockSpec((BLK, D_A), lambda i: (row(i), col))
    kv = lambda col: pl.BlockSpec((BLK, D_KV), lambda i: (row(i), col))
    half = lambda col: pl.BlockSpec((BLK, 512), lambda i: (row(i), col))
    return [wide(0), wide(1), wide(2), wide(3), kv(OFF_K // D_KV), kv(OFF_V // D_KV), half(OFF_ZB // 512), half(OFF_ZB // 512 + 1)]


def _mix_fwd_call(proj, cos, sin, ln_g, ln_b, w_sp, b_sp_t, sinks):
    s = proj.shape[0]
    nb = s // BLK

    def body(ua_ref, va_ref, za_ref, q_ref, k_ref, v_ref, zb0_ref, zb1_ref, cos_ref, sin_ref, lg_ref, lb_ref,
             w_ref, bt_ref, sinks_ref, y_ref, probs_ref, ost_ref, psink_ref, kdup_ref, vdup_ref, qm_ref, bias_ref):
        i = pl.program_id(0)
        first_half, lo = _lane_masks()
        cos_t = cos_ref[...]
        sin_t = sin_ref[...]

        _, _, vln = _layer_norm_fwd(va_ref[...], lg_ref[...], lb_ref[...])
        tril = _tril()
        for g in range(GROUPS):
            cols = slice(g * BLK, (g + 1) * BLK)
            wg = jnp.where(tril, w_ref[g], 0.0).astype(BF16)
            sg = jnp.dot(wg, vln[:, cols].astype(BF16), preferred_element_type=F32) + bt_ref[:, g:g + 1]
            gate, _ = _silu_parts(za_ref[:, cols])
            y_ref[:, cols] = (ua_ref[:, cols] * sg * gate).astype(BF16)

        @pl.when(i == 0)
        def _():
            kdup_ref[:, 0:BLK, :] = jnp.zeros((N_KV, BLK, LANE), BF16)
            vdup_ref[:, 0:BLK, :] = jnp.zeros((N_KV, BLK, LANE), BF16)
            _band_bias(bias_ref)

        @pl.when(i > 0)
        def _():
            kdup_ref[:, 0:BLK, :] = kdup_ref[:, BLK:2 * BLK, :]
            vdup_ref[:, 0:BLK, :] = vdup_ref[:, BLK:2 * BLK, :]

        for ks in range(2):
            cols = slice(ks * LANE, (ks + 1) * LANE)
            kr = _rope(k_ref[:, cols], cos_t, sin_t, first_half)
            for n, (kd, vd) in enumerate(zip(_dup_kv(kr, lo), _dup_kv(v_ref[:, cols], lo))):
                kdup_ref[2 * ks + n, BLK:2 * BLK, :] = kd
                vdup_ref[2 * ks + n, BLK:2 * BLK, :] = vd
        for sb in range(8):
            _stack_heads(qm_ref, sb, _rope(q_ref[:, sb * LANE:(sb + 1) * LANE], cos_t, sin_t, first_half) * SCALE, lo, BF16)

        block_kind = jnp.where(i > 0, 1, 0)

        psink_ref[...] = jnp.zeros((Q_PER_KV * BLK, LANE), F32)
        lane_q = lax.broadcasted_iota(jnp.int32, (Q_PER_KV * BLK, LANE), 1)

        def kv_head(kh, carry):
            probs, psink = _softmax_sink(qm_ref[kh], kdup_ref[kh], bias_ref[block_kind], _sink_column(sinks_ref, kh))
            probs_ref[kh] = probs
            psink_ref[...] = jnp.where(lane_q == kh, psink, psink_ref[...])
            ost_ref[kh] = jnp.dot(probs.astype(BF16), vdup_ref[kh], preferred_element_type=F32)
            return carry

        lax.fori_loop(0, N_KV, kv_head, 0, unroll=2)
        for sb in range(8):
            cols = slice(sb * LANE, (sb + 1) * LANE)
            zb = zb0_ref[:, cols] if sb < 4 else zb1_ref[:, (sb - 4) * LANE:(sb - 3) * LANE]
            gate, _ = _silu_parts(zb)
            y_ref[:, D_A + sb * LANE:D_A + (sb + 1) * LANE] = (_unstack_heads(ost_ref, sb, lo) * gate).astype(BF16)

    tab = pl.BlockSpec((BLK, LANE), lambda i: (i, 0))
    return pl.pallas_call(
        body, name="mix_fwd", grid=(nb,),
        in_specs=_proj_specs() + [
            tab, tab, pl.BlockSpec((1, D_A), lambda i: (0, 0)), pl.BlockSpec((1, D_A), lambda i: (0, 0)),
            pl.BlockSpec((GROUPS, BLK, BLK), lambda i: (0, 0, 0)), pl.BlockSpec((BLK, GROUPS), lambda i: (0, 0)),
            pl.BlockSpec(memory_space=pltpu.SMEM)],
        out_specs=[pl.BlockSpec((BLK, 2 * D_A), lambda i: (i, 0)),
                   pl.BlockSpec((None, N_KV, Q_PER_KV * BLK, 2 * BLK), lambda i: (i, 0, 0, 0)),
                   pl.BlockSpec((None, N_KV, Q_PER_KV * BLK, LANE), lambda i: (i, 0, 0, 0)),
                   pl.BlockSpec((None, Q_PER_KV * BLK, LANE), lambda i: (i, 0, 0))],
        out_shape=[jax.ShapeDtypeStruct((s, 2 * D_A), BF16), jax.ShapeDtypeStruct((nb, N_KV, Q_PER_KV * BLK, 2 * BLK), F32),
                   jax.ShapeDtypeStruct((nb, N_KV, Q_PER_KV * BLK, LANE), F32), jax.ShapeDtypeStruct((nb, Q_PER_KV * BLK, LANE), F32)],
        scratch_shapes=[pltpu.VMEM((N_KV, 2 * BLK, LANE), BF16), pltpu.VMEM((N_KV, 2 * BLK, LANE), BF16),
                        pltpu.VMEM((N_KV, Q_PER_KV * BLK, LANE), BF16), pltpu.VMEM((2, Q_PER_KV * BLK, 2 * BLK), F32)],
        compiler_params=_params("arbitrary"),
    )(proj, proj, proj, proj, proj, proj, proj, proj, cos, sin, ln_g, ln_b, w_sp, b_sp_t, sinks)


def _tail_call(y, w_out_bf, x, target, gate, shift_f, scale_f, gf):
    s = x.shape[0]
    tm = min(s, 256)
    nsteps = s // tm

    def body(y_ref, w_ref, x_ref, t_ref, gate_ref, shf_ref, scf_ref, gf_ref, dx2_ref, do_ref, dy_ref, st_ref):
        i = pl.program_id(0)

        @pl.when(i == 0)
        def _():
            st_ref[...] = jnp.zeros((8, D), F32)

        o = jnp.dot(y_ref[...], w_ref[...], preferred_element_type=F32)
        gate_v = gate_ref[...]
        x2 = x_ref[...] + gate_v * o
        r2 = lax.rsqrt(jnp.mean(x2 * x2, axis=-1, keepdims=True) + EPS)
        xn2 = x2 * r2
        hn2 = xn2 * gf_ref[...]
        one_sc = 1.0 + scf_ref[...]
        err = hn2 * one_sc + shf_ref[...] - t_ref[...]
        dout = err * (1.0 / D)
        dhn2 = dout * one_sc
        dxn2 = dhn2 * gf_ref[...]
        dx2 = r2 * (dxn2 - xn2 * jnp.mean(dxn2 * xn2, axis=-1, keepdims=True))
        dx2_ref[...] = dx2
        do = (dx2 * gate_v).astype(BF16)
        do_ref[...] = do
        dy_ref[...] = lax.dot_general(do, w_ref[...], NT, preferred_element_type=F32)
        st_ref[0:1, :] += jnp.sum(dout, axis=0, keepdims=True)
        st_ref[1:2, :] += jnp.sum(dout * hn2, axis=0, keepdims=True)
        st_ref[2:3, :] += jnp.sum(dhn2 * xn2, axis=0, keepdims=True)
        st_ref[3:4, :] += jnp.sum(dx2 * o, axis=0, keepdims=True)
        st_ref[4:5, :] += jnp.sum(err * err, axis=0, keepdims=True)

        @pl.when(i == nsteps - 1)
        def _():
            st_ref[5:6, :] = jnp.full((1, D), 0.5 / D, F32) * jnp.sum(st_ref[4:5, :])

    vec = pl.BlockSpec((1, D), lambda i: (0, 0))
    rows = lambda: pl.BlockSpec((tm, D), lambda i: (i, 0))
    return pl.pallas_call(
        body, name="tail", grid=(nsteps,),
        in_specs=[rows(), pl.BlockSpec((D, D), lambda i: (0, 0)), rows(), rows(), vec, vec, vec, vec],
        out_specs=[rows(), rows(), rows(), pl.BlockSpec((8, D), lambda i: (0, 0))],
        out_shape=[jax.ShapeDtypeStruct((s, D), F32), jax.ShapeDtypeStruct((s, D), BF16), jax.ShapeDtypeStruct((s, D), F32),
                   jax.ShapeDtypeStruct((8, D), F32)],
        compiler_params=_params("arbitrary"),
    )(y, w_out_bf, x, target, gate, shift_f, scale_f, gf)


def _tn_call(a, b, name):
    s, m = a.shape
    n = b.shape[1]
    tn = 1024
    ts = min(s, 1024)
    nk = s // ts

    def body(a_ref, b_ref, o_ref, acc_ref):
        k = pl.program_id(1)

        @pl.when(k == 0)
        def _():
            acc_ref[...] = jnp.zeros((m, tn), F32)

        acc_ref[...] += lax.dot_general(a_ref[...], b_ref[...], TN, preferred_element_type=F32)

        @pl.when(k == nk - 1)
        def _():
            o_ref[...] = acc_ref[...].astype(BF16)

    return pl.pallas_call(
        body, name=name, grid=(n // tn, nk),
        in_specs=[pl.BlockSpec((ts, m), lambda j, k: (k, 0)), pl.BlockSpec((ts, tn), lambda j, k: (k, j))],
        out_specs=pl.BlockSpec((m, tn), lambda j, k: (0, j)),
        out_shape=jax.ShapeDtypeStruct((m, n), BF16),
        scratch_shapes=[pltpu.VMEM((m, tn), F32)],
        compiler_params=_params("parallel", "arbitrary"),
    )(a, b)


def _tn_shards_call(pos, a, b, qs, name):
    s, m = a.shape
    ts = min(s, 1024)
    nk = s // ts

    def body(pos_ref, a_ref, b_ref, o_ref, acc_ref):
        k = pl.program_id(1)

        @pl.when(k == 0)
        def _():
            acc_ref[...] = jnp.zeros((m, W_IN_SHARD), F32)

        acc_ref[...] += lax.dot_general(a_ref[...], b_ref[...], TN, preferred_element_type=F32)

        @pl.when(k == nk - 1)
        def _():
            o_ref[...] = acc_ref[...].astype(BF16)

    def shard(j, pos):
        q = qs[0]
        for n in range(1, len(qs)):
            q = jnp.where(j == n, qs[n], q)
        return jnp.bitwise_xor(pos[0], q)

    return pl.pallas_call(
        body, name=name,
        grid_spec=pltpu.PrefetchScalarGridSpec(
            num_scalar_prefetch=1, grid=(len(qs), nk),
            in_specs=[pl.BlockSpec((ts, m), lambda j, k, pos: (k, 0)),
                      pl.BlockSpec((ts, W_IN_SHARD), lambda j, k, pos: (k, shard(j, pos)))],
            out_specs=pl.BlockSpec((m, W_IN_SHARD), lambda j, k, pos: (0, j)),
            scratch_shapes=[pltpu.VMEM((m, W_IN_SHARD), F32)]),
        out_shape=jax.ShapeDtypeStruct((m, len(qs) * W_IN_SHARD), BF16),
        compiler_params=_params("parallel", "arbitrary"),
    )(pos, a, b)


def _mix_bwd_call(proj, dy, probs, outs, psinks, tables, ln_g, ln_b, w_sp, w_sp_t, b_sp_t):
    s = proj.shape[0]
    nb = s // BLK
    rev = lambda i: nb - 1 - i
    prev = lambda i: jnp.maximum(nb - 2 - i, 0)

    def body(ua_ref, va_ref, za_ref, q_ref, k_ref, v_ref, zb0_ref, zb1_ref, kp_ref, vp_ref, dy_ref,
             probs_ref, ost_ref, psink_ref, cos_ref, sin_ref, cosp_ref, sinp_ref, lg_ref, lb_ref, w_ref, wt_ref, bt_ref,
             dp_ref, lnst_ref, dw_ref, dbt_ref, dsink_ref,
             kdup_ref, vdup_ref, dvln_ref, qm_ref, dom_ref, dqst_ref, dkdup_ref, dvdup_ref, kcar_ref, vcar_ref, sigb_ref):
        i = pl.program_id(0)
        first_half, lo = _lane_masks()
        lane8 = lax.broadcasted_iota(jnp.int32, (8, LANE), 1)
        cos_t = cos_ref[...]
        sin_t = sin_ref[...]

        @pl.when(i == 0)
        def _():
            lnst_ref[...] = jnp.zeros((8, D_A), F32)
            dw_ref[...] = jnp.zeros((GROUPS, BLK, BLK), F32)
            dbt_ref[...] = jnp.zeros((BLK, LANE), F32)
            dsink_ref[...] = jnp.zeros((8, LANE), F32)
            kcar_ref[...] = jnp.zeros((BLK, D_KV), F32)
            vcar_ref[...] = jnp.zeros((BLK, D_KV), F32)

        vhat, rstd, vln = _layer_norm_fwd(va_ref[...], lg_ref[...], lb_ref[...])
        tril = _tril()
        triu = jnp.logical_not(tril) | (lax.broadcasted_iota(jnp.int32, (BLK, BLK), 0) == lax.broadcasted_iota(jnp.int32, (BLK, BLK), 1))
        lane_b = lax.broadcasted_iota(jnp.int32, (BLK, LANE), 1)
        db_acc = jnp.zeros((BLK, LANE), F32)
        for g in range(GROUPS):
            cols = slice(g * BLK, (g + 1) * BLK)
            vln_g = vln[:, cols].astype(BF16)
            wg = jnp.where(tril, w_ref[g], 0.0).astype(BF16)
            sg = jnp.dot(wg, vln_g, preferred_element_type=F32) + bt_ref[:, g:g + 1]
            za = za_ref[:, cols]
            gate, sig = _silu_parts(za)
            ua = ua_ref[:, cols]
            dya_g = dy_ref[:, cols]
            dya = dya_g * gate
            dp_ref[:, cols] = (dya * sg).astype(BF16)
            dp_ref[:, 2 * D_A + g * BLK:2 * D_A + (g + 1) * BLK] = (
                dya_g * (ua * sg) * (sig * (1.0 + za * (1.0 - sig)))).astype(BF16)
            ds = dya * ua
            ds_b = ds.astype(BF16)
            wtg = jnp.where(triu, wt_ref[g], 0.0).astype(BF16)
            dvln_ref[:, cols] = jnp.dot(wtg, ds_b, preferred_element_type=F32)
            dw_ref[g] += jnp.where(tril, lax.dot_general(ds_b, vln_g, NT, preferred_element_type=F32), 0.0)
            db_acc = db_acc + jnp.where(lane_b == g, jnp.sum(ds, axis=-1, keepdims=True), 0.0)
        dbt_ref[...] += db_acc
        dvln = dvln_ref[...]
        lnst_ref[0:1, :] += jnp.sum(dvln * vhat, axis=0, keepdims=True)
        lnst_ref[1:2, :] += jnp.sum(dvln, axis=0, keepdims=True)
        dvhat = dvln * lg_ref[...]
        m1 = jnp.mean(dvhat, axis=-1, keepdims=True)
        m2 = jnp.mean(dvhat * vhat, axis=-1, keepdims=True)
        dp_ref[:, D_A:2 * D_A] = (rstd * (dvhat - m1 - vhat * m2)).astype(BF16)

        cosp = cosp_ref[...]
        sinp = sinp_ref[...]
        for ks in range(2):
            cols = slice(ks * LANE, (ks + 1) * LANE)
            kr = _rope(k_ref[:, cols], cos_t, sin_t, first_half)
            kpr = _rope(kp_ref[:, cols], cosp, sinp, first_half)
            for n, (kc, vc, kp, vp) in enumerate(zip(_dup_kv(kr, lo), _dup_kv(v_ref[:, cols], lo),
                                                     _dup_kv(kpr, lo), _dup_kv(vp_ref[:, cols], lo))):
                kdup_ref[2 * ks + n, BLK:2 * BLK, :] = kc
                vdup_ref[2 * ks + n, BLK:2 * BLK, :] = vc
                kdup_ref[2 * ks + n, 0:BLK, :] = kp
                vdup_ref[2 * ks + n, 0:BLK, :] = vp
        for sb in range(8):
            cols = slice(sb * LANE, (sb + 1) * LANE)
            _stack_heads(qm_ref, sb, _rope(q_ref[:, cols], cos_t, sin_t, first_half) * SCALE, lo, BF16)
            zb = zb0_ref[:, cols] if sb < 4 else zb1_ref[:, (sb - 4) * LANE:(sb - 3) * LANE]
            gate, sig = _silu_parts(zb)
            sigb_ref[:, cols] = sig
            _stack_heads(dom_ref, sb, dy_ref[:, D_A + sb * LANE:D_A + (sb + 1) * LANE] * gate, lo, F32)

        lane_q = lax.broadcasted_iota(jnp.int32, (Q_PER_KV * BLK, LANE), 1)

        def kv_head(kh, sink_acc):
            qm = qm_ref[kh]
            kd = kdup_ref[kh]
            vd = vdup_ref[kh]
            probs = probs_ref[kh]
            probs_b = probs.astype(BF16)
            o = ost_ref[kh]
            dom = dom_ref[kh]
            dom_b = dom.astype(BF16)
            delta = jnp.sum(dom * o, axis=-1, keepdims=True)
            dpr = lax.dot_general(dom_b, vd, NT, preferred_element_type=F32)
            dss = (probs * (dpr - delta)).astype(BF16)
            sink_acc = sink_acc + jnp.where(lane_q == kh, psink_ref[...] * delta, 0.0)
            dqst_ref[kh] = jnp.dot(dss, kd, preferred_element_type=F32)
            dkdup_ref[kh] = lax.dot_general(dss, qm, TN, preferred_element_type=F32)
            dvdup_ref[kh] = lax.dot_general(probs_b, dom_b, TN, preferred_element_type=F32)
            return sink_acc

        sink_acc = jnp.zeros((Q_PER_KV * BLK, LANE), F32)
        for kh in range(N_KV):
            sink_acc = kv_head(kh, sink_acc)
        lane1 = lax.broadcasted_iota(jnp.int32, (1, LANE), 1)
        dsink_acc = jnp.zeros((8, LANE), F32)
        for n in range(Q_PER_KV):
            col = jnp.sum(sink_acc[n * BLK:(n + 1) * BLK], axis=0, keepdims=True)
            for kh in range(N_KV):
                dsink_acc = dsink_acc + jnp.where(lane8 == Q_PER_KV * kh + n, -jnp.sum(jnp.where(lane1 == kh, col, 0.0)), 0.0)
        row0 = lax.broadcasted_iota(jnp.int32, (8, LANE), 0) == 0
        dsink_ref[...] += jnp.where(row0, dsink_acc, 0.0)

        for sb in range(8):
            cols = slice(sb * LANE, (sb + 1) * LANE)
            zb = zb0_ref[:, cols] if sb < 4 else zb1_ref[:, (sb - 4) * LANE:(sb - 3) * LANE]
            sig = sigb_ref[:, cols]
            dyb = dy_ref[:, D_A + sb * LANE:D_A + (sb + 1) * LANE]
            dp_ref[:, OFF_ZB + sb * LANE:OFF_ZB + (sb + 1) * LANE] = (
                dyb * _unstack_heads(ost_ref, sb, lo) * (sig * (1.0 + zb * (1.0 - sig)))).astype(BF16)
            dq_r = _unstack_heads(dqst_ref, sb, lo) * SCALE
            dp_ref[:, OFF_Q + sb * LANE:OFF_Q + (sb + 1) * LANE] = _unrope(dq_r, cos_t, sin_t, first_half).astype(BF16)

        lo2 = lax.broadcasted_iota(jnp.int32, (2 * BLK, LANE), 1) < HEAD
        for ks in range(2):
            cols = slice(ks * LANE, (ks + 1) * LANE)
            dk_band = _fold_halves(dkdup_ref[2 * ks], dkdup_ref[2 * ks + 1], lo2)
            dv_band = _fold_halves(dvdup_ref[2 * ks], dvdup_ref[2 * ks + 1], lo2)
            dkr = dk_band[BLK:2 * BLK, :] + kcar_ref[:, cols]
            dp_ref[:, OFF_K + ks * LANE:OFF_K + (ks + 1) * LANE] = _unrope(dkr, cos_t, sin_t, first_half).astype(BF16)
            dp_ref[:, OFF_V + ks * LANE:OFF_V + (ks + 1) * LANE] = (
                dv_band[BLK:2 * BLK, :] + vcar_ref[:, cols]).astype(BF16)
            kcar_ref[:, cols] = dk_band[0:BLK, :]
            vcar_ref[:, cols] = dv_band[0:BLK, :]

    tab = pl.BlockSpec((BLK, LANE), lambda i: (rev(i), 0))
    kvp = lambda col: pl.BlockSpec((BLK, D_KV), lambda i: (prev(i), col))
    vec = pl.BlockSpec((1, D_A), lambda i: (0, 0))
    w3 = pl.BlockSpec((GROUPS, BLK, BLK), lambda i: (0, 0, 0))
    return pl.pallas_call(
        body, name="mix_bwd", grid=(nb,),
        in_specs=_proj_specs(nb) + [
            kvp(OFF_K // D_KV), kvp(OFF_V // D_KV), pl.BlockSpec((BLK, 2 * D_A), lambda i: (rev(i), 0)),
            pl.BlockSpec((None, N_KV, Q_PER_KV * BLK, 2 * BLK), lambda i: (rev(i), 0, 0, 0)),
            pl.BlockSpec((None, N_KV, Q_PER_KV * BLK, LANE), lambda i: (rev(i), 0, 0, 0)),
            pl.BlockSpec((None, Q_PER_KV * BLK, LANE), lambda i: (rev(i), 0, 0)),
            tab, tab, tab, tab, vec, vec, w3, w3, pl.BlockSpec((BLK, GROUPS), lambda i: (0, 0))],
        out_specs=[pl.BlockSpec((BLK, D_IN), lambda i: (rev(i), 0)), pl.BlockSpec((8, D_A), lambda i: (0, 0)), w3,
                   pl.BlockSpec((BLK, LANE), lambda i: (0, 0)), pl.BlockSpec((8, LANE), lambda i: (0, 0))],
        out_shape=[jax.ShapeDtypeStruct((s, D_IN), BF16), jax.ShapeDtypeStruct((8, D_A), F32),
                   jax.ShapeDtypeStruct((GROUPS, BLK, BLK), F32), jax.ShapeDtypeStruct((BLK, LANE), F32),
                   jax.ShapeDtypeStruct((8, LANE), F32)],
        scratch_shapes=[pltpu.VMEM((N_KV, 2 * BLK, LANE), BF16), pltpu.VMEM((N_KV, 2 * BLK, LANE), BF16),
                        pltpu.VMEM((BLK, D_A), F32), pltpu.VMEM((N_KV, Q_PER_KV * BLK, LANE), BF16),
                        pltpu.VMEM((N_KV, Q_PER_KV * BLK, LANE), F32), pltpu.VMEM((N_KV, Q_PER_KV * BLK, LANE), F32),
                        pltpu.VMEM((N_KV, 2 * BLK, LANE), F32), pltpu.VMEM((N_KV, 2 * BLK, LANE), F32),
                        pltpu.VMEM((BLK, D_KV), F32), pltpu.VMEM((BLK, D_KV), F32), pltpu.VMEM((BLK, D_B), F32)],
        compiler_params=_params("arbitrary"),
    )(proj, proj, proj, proj, proj, proj, proj, proj, proj, proj, dy, probs, outs, psinks, *tables, ln_g, ln_b,
      w_sp, w_sp_t, b_sp_t)


def _dh_call(dproj, w_bf, x, dx2, scale, norm_g):
    s = x.shape[0]
    tm = min(s, 512)
    tk = W_IN_SHARD
    nk = D_IN // tk

    def body(dp_ref, w_ref, x_ref, dx2_ref, sc_ref, g_ref, gx_ref, st_ref, acc_ref):
        i = pl.program_id(0)
        k = pl.program_id(1)

        @pl.when((i == 0) & (k == 0))
        def _():
            st_ref[...] = jnp.zeros((8, D), F32)

        @pl.when(k == 0)
        def _():
            acc_ref[...] = jnp.zeros((tm, D), F32)

        acc_ref[...] += lax.dot_general(dp_ref[...], w_ref[...], NT, preferred_element_type=F32)

        @pl.when(k == nk - 1)
        def _():
            g = g_ref[...]
            one_sc = 1.0 + sc_ref[...]

            def chunk(n, carry):
                rows = pl.ds(pl.multiple_of(n * BLK, BLK), BLK)
                dh = acc_ref[rows, :]
                xv = x_ref[rows, :]
                r = lax.rsqrt(jnp.mean(xv * xv, axis=-1, keepdims=True) + EPS)
                xn = xv * r
                dhn = dh * one_sc
                dxn = dhn * g
                gx_ref[rows, :] = dx2_ref[rows, :] + r * (dxn - xn * jnp.mean(dxn * xn, axis=-1, keepdims=True))
                st_ref[0:1, :] += jnp.sum(dh, axis=0, keepdims=True)
                st_ref[1:2, :] += jnp.sum(dh * (xn * g), axis=0, keepdims=True)
                st_ref[2:3, :] += jnp.sum(dhn * xn, axis=0, keepdims=True)
                return carry

            lax.fori_loop(0, tm // BLK, chunk, 0)

    vec = pl.BlockSpec((1, D), lambda i, k: (0, 0))
    rows = lambda: pl.BlockSpec((tm, D), lambda i, k: (i, 0))
    return pl.pallas_call(
        body, name="dh", grid=(s // tm, nk),
        in_specs=[pl.BlockSpec((tm, tk), lambda i, k: (i, k)), pl.BlockSpec((D, tk), lambda i, k: (0, k)), rows(), rows(), vec, vec],
        out_specs=[rows(), pl.BlockSpec((8, D), lambda i, k: (0, 0))],
        out_shape=[jax.ShapeDtypeStruct((s, D), F32), jax.ShapeDtypeStruct((8, D), F32)],
        scratch_shapes=[pltpu.VMEM((tm, D), F32)],
        compiler_params=_params("arbitrary", "arbitrary"),
    )(dproj, w_bf, x, dx2, scale, norm_g)


def _adam_math(w, g, m, v):
    m_new = ADAM_B1 * m + (1.0 - ADAM_B1) * g
    v_new = ADAM_B2 * v + (1.0 - ADAM_B2) * (g * g)
    m_hat = m_new / ADAM_C1
    v_hat = v_new / ADAM_C2
    delta = -ADAM_LR * (m_hat / (jnp.sqrt(v_hat) + ADAM_EPS) + ADAM_WD * w)
    return delta, m_new, v_new


def _adam_small_call(tensors):
    n = len(tensors)

    def body(*refs):
        ins, outs = refs[:4 * n], refs[4 * n:]
        for t in range(n):
            w_ref, g_ref, m_ref, v_ref = ins[4 * t:4 * t + 4]
            d, mo, vo = _adam_math(w_ref[...], g_ref[...], m_ref[...], v_ref[...])
            outs[3 * t][...], outs[3 * t + 1][...], outs[3 * t + 2][...] = d, mo, vo

    vm = pl.BlockSpec(memory_space=pltpu.VMEM)
    flat = [a for t in tensors for a in t]
    out = pl.pallas_call(
        body, name="adam_small", in_specs=[vm] * (4 * n), out_specs=[vm] * (3 * n),
        out_shape=[jax.ShapeDtypeStruct(t[0].shape, F32) for t in tensors for _ in range(3)],
        compiler_params=pltpu.CompilerParams(vmem_limit_bytes=VMEM_LIMIT),
    )(*flat)
    return [tuple(out[3 * t:3 * t + 3]) for t in range(n)]


def _adam_halves_call(pos, w, mine, theirs, m, v, name):
    r, n = w.shape
    half = r // 2
    tr = ADAM_ROWS
    nh = half // tr

    def body(pos_ref, w_ref, mine_ref, theirs_ref, m_ref, v_ref, g_ref, d_ref, mo_ref, vo_ref):
        is_mine = (pl.program_id(0) // nh) == pos_ref[1]
        g = jnp.where(is_mine, mine_ref[...], theirs_ref[...])
        g_ref[...] = g
        d_ref[...], mo_ref[...], vo_ref[...] = _adam_math(w_ref[...], g, m_ref[...], v_ref[...])

    spec = lambda: pl.BlockSpec((tr, n), lambda i, pos: (i, 0))

    def half_spec(core_of_half):
        def index(i, pos):
            first = core_of_half(pos) == 0
            active = (i // nh == 0) == first
            return jnp.where(active, i % nh, jnp.where(first, nh - 1, 0)), 0
        return pl.BlockSpec((tr, n), index)

    return pl.pallas_call(
        body, name=name,
        grid_spec=pltpu.PrefetchScalarGridSpec(
            num_scalar_prefetch=1, grid=(r // tr,),
            in_specs=[spec(), half_spec(lambda pos: pos[1]), half_spec(lambda pos: 1 - pos[1]), spec(), spec()],
            out_specs=[spec() for _ in range(4)]),
        out_shape=[jax.ShapeDtypeStruct((r, n), F32)] * 4, compiler_params=_params("arbitrary"),
    )(pos, w, mine, theirs, m, v)


def _adam_outer_call(w, ct, dm, m, v, name):
    r, n = w.shape
    tr = ADAM_ROWS

    def body(w_ref, ct_ref, dm_ref, m_ref, v_ref, g_ref, d_ref, mo_ref, vo_ref):
        g = ct_ref[:, 0:1] * dm_ref[0:1, :]
        for b in range(1, N_DEV):
            g = g + ct_ref[:, b:b + 1] * dm_ref[b:b + 1, :]
        g_ref[...] = g
        d_ref[...], mo_ref[...], vo_ref[...] = _adam_math(w_ref[...], g, m_ref[...], v_ref[...])

    spec = lambda: pl.BlockSpec((tr, n), lambda i: (i, 0))
    return pl.pallas_call(
        body, name=name, grid=(r // tr,),
        in_specs=[spec(), pl.BlockSpec((tr, N_DEV), lambda i: (i, 0)), pl.BlockSpec((N_DEV, n), lambda i: (0, 0)), spec(), spec()],
        out_specs=[spec() for _ in range(4)],
        out_shape=[jax.ShapeDtypeStruct((r, n), F32)] * 4, compiler_params=_params("parallel"),
    )(w, ct, dm, m, v)


def _sum_pieces_call(pos, part, part_block, recvs, name):
    r, n = recvs[0].shape[1:]
    tr = min(r, 256)
    nrb = r // tr

    def body(pos_ref, p_ref, *refs):
        acc = p_ref[...].astype(F32)
        for r_ref in refs[:-1]:
            for d in range(r_ref.shape[0]):
                acc = acc + r_ref[d].astype(F32)
        refs[-1][...] = acc

    return pl.pallas_call(
        body, name=name,
        grid_spec=pltpu.PrefetchScalarGridSpec(
            num_scalar_prefetch=1, grid=(nrb,),
            in_specs=[pl.BlockSpec((tr, n), lambda i, pos: part_block(i, pos, nrb))] + [
                pl.BlockSpec((rv.shape[0], tr, n), lambda i, pos: (0, i, 0)) for rv in recvs],
            out_specs=pl.BlockSpec((tr, n), lambda i, pos: (i, 0))),
        out_shape=jax.ShapeDtypeStruct((r, n), F32), compiler_params=_params("parallel"),
    )(pos, part, *recvs)


def _coords():
    return lax.axis_index("x"), lax.axis_index("y"), lax.axis_index("c")


CAST_ROWS = 256


def _allgather_sum_call(blk, name, with_sum, cast=None):
    m_per, n = blk.shape
    n_out = 2 if with_sum else 1
    if cast is not None:
        w, full_shape = cast
        wr, wn = w.shape
        by_cols = full_shape[0] == wr
        tr = min(wr, CAST_ROWS)
        n_chunk = wr // tr

    def body(*refs):
        x_ref = refs[0]
        out_ref = refs[1 + (cast is not None)]
        rest = refs[1 + (cast is not None) + n_out + (cast is not None):]
        send_sems, recv_sems, local_sem = rest[:3]
        x, y, c = _coords()
        me, sibling = (x, y, c), (x, y, 1 - c)
        chips = [(1 - x, y), (x, 1 - y), (1 - x, 1 - y)]

        def rows(px, py, pc):
            return out_ref.at[pl.ds((4 * px + 2 * py + pc) * m_per, m_per), :]

        def copy(k, block, to, src=None):
            return pltpu.make_async_remote_copy(
                src_ref=rows(*block) if src is None else src, dst_ref=rows(*block),
                send_sem=send_sems.at[k], recv_sem=recv_sems.at[k], device_id=to, device_id_type=MESH)

        mine = pltpu.make_async_copy(x_ref, rows(*me), local_sem)
        mine.start()
        first = [copy(0, me, sibling, src=x_ref)]
        first += [copy(1 + j, me, (*chip, c), src=x_ref) for j, chip in enumerate(chips)]
        for cp in first:
            cp.start()

        if cast is not None:
            w_ref, full_ref = refs[1], refs[1 + 1 + n_out]
            f32_buf, bf16_buf, in_sems, out_sems = rest[3:]
            chip_no = 2 * x + y

            def fetch(i):
                return pltpu.make_async_copy(w_ref.at[pl.ds(i * tr, tr), :], f32_buf.at[i % 2], in_sems.at[i % 2])

            def store(i):
                if by_cols:
                    dst = full_ref.at[pl.ds(i * tr, tr), pl.ds(chip_no * wn, wn)]
                else:
                    dst = full_ref.at[pl.ds(chip_no * wr + i * tr, tr), :]
                return pltpu.make_async_copy(bf16_buf.at[i % 2], dst, out_sems.at[i % 2])

            fetch(0).start()
            for i in range(n_chunk):
                if i + 1 < n_chunk:
                    fetch(i + 1).start()
                fetch(i).wait()
                if i >= 2:
                    store(i - 2).wait()
                bf16_buf[i % 2] = f32_buf[i % 2].astype(BF16)
                store(i).start()
            for i in range(max(n_chunk - 2, 0), n_chunk):
                store(i).wait()

        passed = [copy(4 + j, (*chip, c), sibling) for j, chip in enumerate(chips)]
        for j, chip in enumerate(chips):
            copy(1 + j, (*chip, c), me).wait_recv()
            passed[j].start()
        copy(0, sibling, me).wait_recv()
        for j, chip in enumerate(chips):
            copy(4 + j, (*chip, 1 - c), me).wait_recv()
        for cp in first + passed:
            cp.wait_send()
        mine.wait()
        if with_sum:
            sum_ref = refs[1 + (cast is not None) + 1]
            acc = out_ref[0:m_per, :]
            for d in range(1, N_DEV):
                acc = acc + out_ref[d * m_per:(d + 1) * m_per, :]
            sum_ref[...] = acc

    vm = pl.BlockSpec(memory_space=pltpu.VMEM)
    anyspec = pl.BlockSpec(memory_space=pl.ANY)
    out_shape = [jax.ShapeDtypeStruct((N_DEV * m_per, n), F32)]
    if with_sum:
        out_shape.append(jax.ShapeDtypeStruct((m_per, n), F32))
    in_specs, out_specs, operands = [vm], [vm] * n_out, [blk]
    scratch = [pltpu.SemaphoreType.DMA((7,)), pltpu.SemaphoreType.DMA((7,)), pltpu.SemaphoreType.DMA]
    if cast is not None:
        in_specs.append(anyspec)
        operands.append(w)
        out_shape.append(jax.ShapeDtypeStruct(full_shape, BF16))
        out_specs.append(anyspec)
        scratch += [pltpu.VMEM((2, tr, wn), F32), pltpu.VMEM((2, tr, wn), BF16), pltpu.SemaphoreType.DMA((2,)),
                    pltpu.SemaphoreType.DMA((2,))]
    return pl.pallas_call(
        body, name=name, out_shape=out_shape, in_specs=in_specs, out_specs=out_specs, scratch_shapes=scratch,
        compiler_params=pltpu.CompilerParams(vmem_limit_bytes=VMEM_LIMIT),
    )(*operands)


HBM_SPEC = pl.BlockSpec(memory_space=pltpu.HBM)
SEM_SPEC = pl.BlockSpec(memory_space=pltpu.SEMAPHORE)
SIDE_EFFECT = pltpu.SideEffectType.DATAFLOW_SIDE_EFFECTING


def _peer(x, y, c, q, cb):
    return (1 - x if q & 2 else x, 1 - y if q & 1 else y, 1 - c if cb else c)


def _w_in_piece(slots):
    def piece(part_ref, k, to):
        return part_ref.at[pl.ds(to[2] * (D // 2), D // 2), pl.ds(slots[k] * W_IN_SHARD, W_IN_SHARD)]
    return piece


def _w_out_piece(part_ref, k, to):
    ho = W_OUT_SHARD // 2
    return part_ref.at[pl.ds((2 * to[0] + to[1]) * W_OUT_SHARD + to[2] * ho, ho), :]


def _group_piece(part_ref, k, to):
    return part_ref.at[4 * to[0] + 2 * to[1] + to[2]]


def _whole_piece(part_ref, k, to):
    return part_ref


def _exchange_start_call(groups, name):
    ng = len(groups)
    lands = [lax.empty((len(rels),) + slot_shape, part.dtype) for part, rels, _, slot_shape in groups]

    def body(*refs):
        ins, outs = refs[:2 * ng], refs[2 * ng:]
        x, y, c = _coords()
        for g, (_, rels, piece, _) in enumerate(groups):
            part_ref, land_ref = ins[2 * g], ins[2 * g + 1]
            send_sems, recv_sems = outs[4 * g], outs[4 * g + 1]
            for k, (q, cb) in enumerate(rels):
                to = _peer(x, y, c, q, cb)
                pltpu.make_async_remote_copy(src_ref=piece(part_ref, k, to), dst_ref=land_ref.at[k], send_sem=send_sems.at[k],
                                             recv_sem=recv_sems.at[k], device_id=to, device_id_type=MESH).start()
        outs[-1][...] = jnp.zeros_like(outs[-1])

    out_shape, out_specs, operands = [], [], []
    for (part, rels, _, _), land in zip(groups, lands):
        n = len(rels)
        out_shape += [pltpu.SemaphoreType.DMA((n,)), pltpu.SemaphoreType.DMA((n,)), pltpu.HBM(part.shape, part.dtype),
                      pltpu.HBM(land.shape, land.dtype)]
        out_specs += [SEM_SPEC, SEM_SPEC, HBM_SPEC, HBM_SPEC]
        operands += [pltpu.with_memory_space_constraint(part, pltpu.HBM), pltpu.with_memory_space_constraint(land, pltpu.HBM)]
    out = pl.pallas_call(
        body, name=name,
        out_shape=tuple(out_shape) + (jax.ShapeDtypeStruct((1, 1), F32),),
        in_specs=(HBM_SPEC,) * (2 * ng), out_specs=tuple(out_specs) + (pl.BlockSpec(memory_space=pltpu.VMEM),),
        input_output_aliases={j: 4 * (j // 2) + 2 + j % 2 for j in range(2 * ng)},
        compiler_params=pltpu.CompilerParams(has_side_effects=SIDE_EFFECT),
    )(*operands)
    return [tuple(out[4 * g:4 * g + 4]) for g in range(ng)], out[-1]


def _exchange_wait_call(started, groups, after, name):
    ng = len(groups)

    def body(*refs):
        ins = refs[:4 * ng]
        x, y, c = _coords()
        for g, (_, rels, piece, _) in enumerate(groups):
            part_ref, land_ref, send_sems, recv_sems = ins[4 * g:4 * g + 4]
            for k, (q, cb) in enumerate(rels):
                to = _peer(x, y, c, q, cb)
                cp = pltpu.make_async_remote_copy(src_ref=piece(part_ref, k, to), dst_ref=land_ref.at[k], send_sem=send_sems.at[k],
                                                  recv_sem=recv_sems.at[k], device_id=to, device_id_type=MESH)
                cp.wait_send()
                cp.wait_recv()

    operands, in_specs, out_shape = [], [], []
    for send_sems, recv_sems, part_thru, land_thru in started:
        operands += [part_thru, land_thru, send_sems, recv_sems]
        in_specs += [HBM_SPEC, HBM_SPEC, SEM_SPEC, SEM_SPEC]
        out_shape += [pltpu.HBM(part_thru.shape, part_thru.dtype), pltpu.HBM(land_thru.shape, land_thru.dtype)]
    out = pl.pallas_call(
        body, name=name, out_shape=tuple(out_shape),
        in_specs=tuple(in_specs) + (pl.BlockSpec(memory_space=pl.ANY),), out_specs=(HBM_SPEC,) * (2 * ng),
        input_output_aliases={4 * g + j: 2 * g + j for g in range(ng) for j in range(2)},
        compiler_params=pltpu.CompilerParams(has_side_effects=SIDE_EFFECT),
    )(*operands, after)
    return [tuple(out[2 * g:2 * g + 2]) for g in range(ng)]


def _rope_tables(s):
    inv_freq = np.float32(10000.0) ** (-np.arange(0, HEAD, 2, dtype=np.float32) / np.float32(HEAD))
    ang = np.arange(s, dtype=np.float32)[:, None] * inv_freq[None, :]
    cos = np.tile(np.cos(ang), (1, LANE // (HEAD // 2))).astype(np.float32)
    sin = np.tile(np.sin(ang), (1, LANE // (HEAD // 2))).astype(np.float32)
    first_half = (np.arange(LANE) % HEAD) < (HEAD // 2)
    sin = np.where(first_half[None, :], -sin, sin)
    behind = lambda t: np.concatenate([t[:BLK], t[:-BLK]], axis=0)
    return tuple(jnp.asarray(t) for t in (cos, sin, behind(cos), behind(sin)))


def kernel(x, c, w_ada, b_ada, norm_g, w_in, ln_v_g, ln_v_b, w_spatial, b_spatial, sinks, w_out, w_ada_final, b_ada_final, final_norm_g, loss_target, m_w_ada, m_b_ada, m_norm_g, m_w_in, m_ln_v_g, m_ln_v_b, m_w_spatial, m_b_spatial, m_sinks, m_w_out, m_w_ada_final, m_b_ada_final, m_final_norm_g, v_w_ada, v_b_ada, v_norm_g, v_w_in, v_ln_v_g, v_ln_v_b, v_w_spatial, v_b_spatial, v_sinks, v_w_out, v_w_ada_final, v_b_ada_final, v_final_norm_g):
    s = x.shape[1]
    ax, ay, ac = _coords()
    chip = 2 * ax + ay
    me = 4 * ax + 2 * ay + ac
    n_ada = w_ada.shape[2]
    n_adaf = w_ada_final.shape[1]

    x2d = x.reshape(s, D)
    tgt = loss_target.reshape(s, D)
    w_ada2, w_in2, w_out2 = w_ada[0], w_in[0], w_out[0]
    b_ada_f2 = b_ada_final.reshape(1, 2 * D)
    gf = final_norm_g.reshape(1, D)

    c_all, w_in_own = _allgather_sum_call(jnp.pad(c, ((0, 7), (0, 0))), "gather_c", False, cast=(w_in2, (D, D_IN)))
    c_all = c_all[::8]
    mod_p, c_act = _rowmat_call(c_all, w_ada2, lax.dynamic_slice(b_ada, (0, chip * n_ada), (1, n_ada)), "mod")
    modf_p, _ = _rowmat_call(c_all, w_ada_final, lax.dynamic_slice(b_ada_f2, (0, chip * n_adaf), (1, n_adaf)), "mod_final")
    mods, w_out_own = _allgather_sum_call(jnp.concatenate([mod_p, modf_p], axis=1), "gather_mod", False, cast=(w_out2, (D, D)))
    my_rows = [lax.dynamic_slice(mods, (16 * j + me, 0), (1, n_ada + n_adaf)) for j in range(N_CHIP)]
    mod = jnp.concatenate([r[:, :n_ada] for r in my_rows], axis=1)
    mod_f = jnp.concatenate([r[:, n_ada:] for r in my_rows], axis=1)
    shift, scale, gate = mod[:, :D], mod[:, D:2 * D], mod[:, 2 * D:]
    shift_f, scale_f = mod_f[:, :D], mod_f[:, D:]

    pos = jnp.stack([chip, ac]).astype(jnp.int32)

    tables = _rope_tables(s)
    cos, sin = tables[:2]
    b_sp_t = b_spatial[0].T
    sinks1 = sinks.reshape(N_Q)
    h, proj, w_in_bf, w_out_bf = _proj_gather_call(pos, x2d, shift, scale, norm_g, w_in_own, w_out_own)
    y, probs, attn_out, psinks = _mix_fwd_call(proj, cos, sin, ln_v_g, ln_v_b, w_spatial[0], b_sp_t, sinks1)
    dx2, do, dy, st_tail = _tail_call(y, w_out_bf, x2d, tgt, gate, shift_f, scale_f, gf)

    rel_o = [(0, 1), (1, 0), (1, 1), (2, 0), (2, 1), (3, 0), (3, 1)]
    rel_a = [(1, 0), (1, 1), (2, 0), (2, 1)]
    rel_b = [(3, 0), (3, 1), (0, 1)]
    piece_a, piece_b = _w_in_piece([0, 0, 1, 1]), _w_in_piece([0, 0, 1])
    half_in, half_out = (D // 2, W_IN_SHARD), (W_OUT_SHARD // 2, D)

    g_w_out_p = _tn_call(y, do, "grad_w_out")
    grp_o = [(g_w_out_p, rel_o, _w_out_piece, half_out)]
    st_o, tok_o = _exchange_start_call(grp_o, "send_w_out")
    dproj, st_ln, d_wsp, d_bsp_t, d_sink = _mix_bwd_call(
        proj, dy, probs, attn_out, psinks, tables, ln_v_g + tok_o, ln_v_b, w_spatial[0], jnp.swapaxes(w_spatial[0], 1, 2),
        b_sp_t)
    g_w_in_a = _tn_shards_call(pos, h, dproj, (1, 2), "grad_w_in_a")
    grp_a = [(g_w_in_a, rel_a, piece_a, half_in), (d_wsp, rel_o, _group_piece, (BLK, BLK))]
    st_a, tok_a = _exchange_start_call(grp_a, "send_w_in_a")
    g_w_in_b = _tn_shards_call(pos, h, dproj, (3, 0), "grad_w_in_b")
    grp_b = [(g_w_in_b, rel_b, piece_b, half_in)]
    st_b, tok_b = _exchange_start_call(grp_b, "send_w_in_b")
    grad_x, st_dh = _dh_call(dproj, w_in_bf, x2d, dx2, scale + (tok_a + tok_b), norm_g)

    ((g_w_out_p, recv_o),) = _exchange_wait_call(st_o, grp_o, st_dh, "wait_w_out")
    (_, recv_a), (d_wsp, recv_s) = _exchange_wait_call(st_a, grp_a, st_dh, "wait_w_in_a")
    ((g_w_in_b, recv_b),) = _exchange_wait_call(st_b, grp_b, st_dh, "wait_w_in_b")
    mine_in = _sum_pieces_call(pos, g_w_in_b, lambda i, p, nrb: (p[1] * nrb + i, 1), [recv_a, recv_b], "sum_w_in")
    mine_out = _sum_pieces_call(pos, g_w_out_p, lambda i, p, nrb: ((2 * p[0] + p[1]) * nrb + i, 0), [recv_o], "sum_w_out")
    wsp_group = _sum_pieces_call(pos, d_wsp.reshape(GROUPS * BLK, BLK), lambda i, p, nrb: (2 * p[0] + p[1], 0), [recv_s],
                                 "sum_w_spatial")
    to_sibling = [(0, 1)]
    grp_p = [(mine_in, to_sibling, _whole_piece, half_in), (mine_out, to_sibling, _whole_piece, half_out)]
    st_p, tok_p = _exchange_start_call(grp_p, "swap_halves")

    misc = jnp.concatenate([st_ln, d_bsp_t[:, :GROUPS].T, d_sink, jnp.zeros((8, D - D_A - 2 * LANE), F32)], axis=1)
    pack = jnp.concatenate([wsp_group.reshape(8, D) + tok_p, st_tail, st_dh, misc], axis=0)
    rows = pack.shape[0]
    packs, tot = _allgather_sum_call(pack, "gather_small", True)
    packs = packs.reshape(N_DEV, rows, D)
    dmod_all = jnp.concatenate([packs[:, 16, :], packs[:, 17, :], packs[:, 11, :]], axis=1)
    dmodf_all = jnp.concatenate([packs[:, 8, :], packs[:, 9, :]], axis=1)
    loss = tot[13, 0]
    (mine_in, theirs_in), (mine_out, theirs_out) = _exchange_wait_call(st_p, grp_p, tot, "swapped_halves")
    small = {
        "b_ada": jnp.concatenate([tot[16:17], tot[17:18], tot[11:12]], axis=1),
        "norm_g": tot[18:19],
        "ln_v_g": tot[24:25, :D_A],
        "ln_v_b": tot[25:26, :D_A],
        "w_spatial": packs[:, 0:8, :].reshape(GROUPS * BLK, BLK),
        "b_spatial": tot[24:32, D_A:D_A + BLK],
        "sinks": tot[24:25, D_A + LANE:D_A + LANE + N_Q],
        "b_ada_final": jnp.concatenate([tot[8:9], tot[9:10]], axis=1),
        "final_norm_g": tot[10:11],
    }

    weights = dict(w_ada=w_ada, b_ada=b_ada, norm_g=norm_g, w_in=w_in, ln_v_g=ln_v_g, ln_v_b=ln_v_b, w_spatial=w_spatial,
                   b_spatial=b_spatial, sinks=sinks, w_out=w_out, w_ada_final=w_ada_final, b_ada_final=b_ada_final,
                   final_norm_g=final_norm_g)
    m_in = dict(w_ada=m_w_ada, b_ada=m_b_ada, norm_g=m_norm_g, w_in=m_w_in, ln_v_g=m_ln_v_g, ln_v_b=m_ln_v_b,
                w_spatial=m_w_spatial, b_spatial=m_b_spatial, sinks=m_sinks, w_out=m_w_out, w_ada_final=m_w_ada_final,
                b_ada_final=m_b_ada_final, final_norm_g=m_final_norm_g)
    v_in = dict(w_ada=v_w_ada, b_ada=v_b_ada, norm_g=v_norm_g, w_in=v_w_in, ln_v_g=v_ln_v_g, ln_v_b=v_ln_v_b,
                w_spatial=v_w_spatial, b_spatial=v_b_spatial, sinks=v_sinks, w_out=v_w_out, w_ada_final=v_w_ada_final,
                b_ada_final=v_b_ada_final, final_norm_g=v_final_norm_g)
    c_act_t = c_act.T
    outer = {"w_ada": lax.dynamic_slice(dmod_all, (0, chip * n_ada), (N_DEV, n_ada)),
             "w_ada_final": lax.dynamic_slice(dmodf_all, (0, chip * n_adaf), (N_DEV, n_adaf))}
    halves = {"w_in": (mine_in, theirs_in[0]), "w_out": (mine_out, theirs_out[0])}
    done = {}
    for name, (mine, theirs) in halves.items():
        shape2 = (2 * mine.shape[0], mine.shape[1])
        done[name] = _adam_halves_call(pos, weights[name].reshape(shape2), mine, theirs, m_in[name].reshape(shape2),
                                       v_in[name].reshape(shape2), "adam_" + name)
    for name, dm in outer.items():
        shape2 = (D, dm.shape[1])
        done[name] = _adam_outer_call(weights[name].reshape(shape2), c_act_t, dm, m_in[name].reshape(shape2),
                                      v_in[name].reshape(shape2), "adam_" + name)
    updates = _adam_small_call([(weights[name].reshape(g.shape), g, m_in[name].reshape(g.shape), v_in[name].reshape(g.shape))
                                for name, g in small.items()])
    for (name, g), upd in zip(small.items(), updates):
        done[name] = (g, *upd)
    outs = [[done[name][k].reshape(w.shape) for name, w in weights.items()] for k in range(4)]
    return (loss, grad_x.reshape(x.shape), *outs[0], *outs[1], *outs[2], *outs[3])
```

```python
import numpy as np
import jax
import jax.numpy as jnp
from jax import lax
from jax.experimental import pallas as pl
from jax.experimental.pallas import tpu as pltpu

F32 = jnp.float32
BF16 = jnp.bfloat16
MESH = pl.DeviceIdType.MESH

D = 2048
D_A = 1024
D_B = 1024
D_KV = 256
HEAD = 64
N_Q = 16
N_KV = 4
Q_PER_KV = N_Q // N_KV
BLK = 128
GROUPS = 8
D_IN = 5632
OFF_Q, OFF_K, OFF_V, OFF_ZB = 3072, 4096, 4352, 4608
N_CHIP = 4
N_DEV = 8
W_IN_SHARD = D_IN // N_CHIP
W_OUT_SHARD = D // N_CHIP
EPS = 1e-5
SCALE = HEAD ** -0.5
NEG = -1e30
LANE = 128
VMEM_LIMIT = 56 * 1024 * 1024

ADAM_LR, ADAM_B1, ADAM_B2, ADAM_EPS, ADAM_WD, ADAM_STEP = 0.001, 0.9, 0.999, 1e-08, 0.01, 10
ADAM_C1 = 1.0 - ADAM_B1 ** ADAM_STEP
ADAM_C2 = 1.0 - ADAM_B2 ** ADAM_STEP
ADAM_ROWS = 256

NT = (((1,), (1,)), ((), ()))
TN = (((0,), (0,)), ((), ()))


def _params(*sem):
    return pltpu.CompilerParams(dimension_semantics=sem, vmem_limit_bytes=VMEM_LIMIT)


def _silu_parts(z):
    sig = 1.0 / (1.0 + jnp.exp(-z))
    return z * sig, sig


def _swap_halves(v, first_half):
    return jnp.where(first_half, pltpu.roll(v, 96, 1), pltpu.roll(v, 32, 1))


def _rope(v, cos_t, sin_s, first_half):
    return v * cos_t + _swap_halves(v, first_half) * sin_s


def _unrope(dv, cos_t, sin_s, first_half):
    return dv * cos_t - _swap_halves(dv, first_half) * sin_s


def _lane_masks():
    lane = lax.broadcasted_iota(jnp.int32, (BLK, LANE), 1)
    return (lane % HEAD) < (HEAD // 2), lane < HEAD


def _band_valid(first_block_bound, rows=BLK):
    rr = lax.broadcasted_iota(jnp.int32, (rows, 2 * BLK), 0) & (BLK - 1)
    jj = lax.broadcasted_iota(jnp.int32, (rows, 2 * BLK), 1)
    return (jj > rr) & (jj <= rr + BLK) & (jj >= first_block_bound)


def _dup_kv(slab, lo):
    rolled = pltpu.roll(slab, HEAD, 1)
    return jnp.where(lo, slab, rolled).astype(BF16), jnp.where(lo, rolled, slab).astype(BF16)


def _fold_halves(a, b, lo):
    return jnp.where(lo, a, b) + pltpu.roll(jnp.where(lo, b, a), HEAD, 1)


def _stack_heads(ref, sb, slab, lo, dtype):
    kh, base = sb // 2, 2 * (sb % 2) * BLK
    zero = jnp.zeros_like(slab)
    ref[kh, base:base + BLK, :] = jnp.where(lo, slab, zero).astype(dtype)
    ref[kh, base + BLK:base + 2 * BLK, :] = jnp.where(lo, zero, slab).astype(dtype)


def _unstack_heads(ref, sb, lo):
    kh, base = sb // 2, 2 * (sb % 2) * BLK
    return jnp.where(lo, ref[kh, base:base + BLK, :], ref[kh, base + BLK:base + 2 * BLK, :])


def _sink_column(sinks_ref, kh):
    row = lax.broadcasted_iota(jnp.int32, (Q_PER_KV * BLK, 1), 0)
    col = jnp.full(row.shape, sinks_ref[Q_PER_KV * kh + Q_PER_KV - 1], F32)
    for n in range(Q_PER_KV - 2, -1, -1):
        col = jnp.where(row < (n + 1) * BLK, sinks_ref[Q_PER_KV * kh + n], col)
    return col


def _tril():
    t = lax.broadcasted_iota(jnp.int32, (BLK, BLK), 0)
    s = lax.broadcasted_iota(jnp.int32, (BLK, BLK), 1)
    return s <= t


def _layer_norm_fwd(va, lg, lb):
    mu = jnp.mean(va, axis=-1, keepdims=True)
    xc = va - mu
    rstd = lax.rsqrt(jnp.mean(xc * xc, axis=-1, keepdims=True) + EPS)
    vhat = xc * rstd
    return vhat, rstd, vhat * lg + lb


def _softmax_sink(qm, kdup, bias, sink):
    s = lax.dot_general(qm, kdup, NT, preferred_element_type=F32) + bias
    m = jnp.maximum(jnp.max(s, axis=-1, keepdims=True), sink)
    p = jnp.exp(s - m)
    esink = jnp.exp(sink - m)
    inv = 1.0 / (jnp.sum(p, axis=-1, keepdims=True) + esink)
    return p * inv, esink * inv


def _band_bias(bias_ref):
    rows = bias_ref.shape[1]
    bias_ref[0] = jnp.where(_band_valid(BLK, rows), 0.0, NEG)
    bias_ref[1] = jnp.where(_band_valid(0, rows), 0.0, NEG)


def _rowmat_call(c_all, w, b, name):
    n = w.shape[1]
    tn = 512

    def body(c_ref, w_ref, b_ref, o_ref, ca_ref):
        ca, _ = _silu_parts(c_ref[...])
        ca_ref[...] = ca
        o_ref[...] = jnp.dot(ca.astype(BF16), w_ref[...].astype(BF16), preferred_element_type=F32) + b_ref[...]

    return pl.pallas_call(
        body, name=name, grid=(n // tn,),
        in_specs=[pl.BlockSpec((N_DEV, D), lambda j: (0, 0)), pl.BlockSpec((D, tn), lambda j: (0, j)),
                  pl.BlockSpec((1, tn), lambda j: (0, j))],
        out_specs=[pl.BlockSpec((N_DEV, tn), lambda j: (0, j)), pl.BlockSpec((N_DEV, D), lambda j: (0, 0))],
        out_shape=[jax.ShapeDtypeStruct((N_DEV, n), F32), jax.ShapeDtypeStruct((N_DEV, D), F32)],
        compiler_params=_params("arbitrary"),
    )(c_all, w, b)


W_IN_PARTS = ((0, 768), (768, 640))
OUT_STREAMS = 4
X_STREAMS = 4


def _proj_gather_call(pos, x, shift, scale, norm_g, wi_full, wo_full):
    s = x.shape[0]
    tm = min(s, 512)
    nrow = s // tm
    hi = D // 2
    ho = W_OUT_SHARD // 2
    phases = [(0, None), (1, 0), (2, 0), (1, 1), (2, 1), (3, 0), (3, 1)]

    def body(pos_ref, *refs):
        x_refs = refs[:X_STREAMS]
        (sh_ref, sc_ref, g_ref, _, _, h_ref, proj_ref, fi_ref, fo_ref,
         h_all, wbuf, obuf, send_sems, recv_sems, load_sems, out_sems) = refs[X_STREAMS:]
        p = pl.program_id(0)
        i = pl.program_id(1)
        x_, y_, c_ = _coords()
        me, sibling = (x_, y_, c_), (x_, y_, 1 - c_)

        def shard_of(q):
            px, py, _ = _peer(x_, y_, c_, q, 0)
            return 2 * px + py

        def cols_of(q, cp):
            off, w = (0, W_IN_SHARD) if cp is None else W_IN_PARTS[cp]
            return shard_of(q) * W_IN_SHARD + off, w

        def part(which, q, pc, sub, cp):
            n = hi if which == 0 else ho
            base = pc * n
            if sub is not None:
                n //= 2
                base = base + sub * n
            if which == 0:
                c0, w = cols_of(q, cp)
                return fi_ref.at[pl.ds(base, n), pl.ds(c0, w)]
            return fo_ref.at[pl.ds(shard_of(q) * W_OUT_SHARD + base, n), :]

        def copy(k, ref, to):
            return pltpu.make_async_remote_copy(src_ref=ref, dst_ref=ref, send_sem=send_sems.at[k], recv_sem=recv_sems.at[k],
                                                device_id=to, device_id_type=MESH)

        def sem(which, kind, j, cp):
            return 4 * kind + 2 * cp + j if which == 0 else 16 + 2 * kind + j

        def to_neighbour(which, q, cp=None):
            return copy(sem(which, 0, q - 1, cp), part(which, 0, c_, None, cp), _peer(x_, y_, c_, q, 0))

        def from_neighbour(which, q, cp=None):
            return copy(sem(which, 0, q - 1, cp), part(which, q, c_, None, cp), me)

        def relay(which, q, cp=None):
            return copy(sem(which, 1, q - 1, cp), part(which, q, c_, q - 1, cp), _peer(x_, y_, c_, 3 - q, 0))

        def relayed(which, sub, cp=None):
            return copy(sem(which, 1, sub, cp), part(which, 3, c_, sub, cp), me)

        def to_sibling(which, q, cp=None):
            return copy(sem(which, 2, q - 1, cp), part(which, q, c_, None, cp), sibling)

        def from_sibling(which, q, cp=None):
            return copy(sem(which, 2, q - 1, cp), part(which, q, 1 - c_, None, cp), me)

        def relayed_to_sibling(which, sub, cp=None):
            return copy(sem(which, 3, sub, cp), part(which, 3, c_, sub, cp), sibling)

        def relayed_from_sibling(which, sub, cp=None):
            return copy(sem(which, 3, sub, cp), part(which, 3, 1 - c_, sub, cp), me)

        def pass_on_neighbours(which, cp=None):
            for q in (1, 2):
                from_neighbour(which, q, cp).wait_recv()
                to_sibling(which, q, cp).start()
                relay(which, q, cp).start()

        def pass_on_relayed(which, cp=None):
            for sub in range(2):
                relayed(which, sub, cp).wait_recv()
                relayed_to_sibling(which, sub, cp).start()

        def shard_load(k):
            c0, w = cols_of(*phases[k])
            return pltpu.make_async_copy(fi_ref.at[:, pl.ds(c0, w)], wbuf.at[k % 2, :, 0:w], load_sems.at[k % 2])

        class OutCopies:
            def __init__(self, k, slot, row0):
                c0, w = cols_of(*phases[k])
                strip = tm // OUT_STREAMS
                self.copies = [pltpu.make_async_copy(obuf.at[slot, n * strip:(n + 1) * strip, 0:w],
                                                     proj_ref.at[pl.ds(row0 + n * strip, strip), pl.ds(c0, w)],
                                                     out_sems.at[slot, n]) for n in range(OUT_STREAMS)]

            def start(self):
                for cp in self.copies:
                    cp.start()

            def wait(self):
                for cp in self.copies:
                    cp.wait()

        out_copy = OutCopies

        def drain(k):
            for j in range(min(2, nrow)):
                out_copy(k, (nrow - 1 - j) % 2, 0).wait()

        def arrivals(k):
            q, cp = phases[k]
            if k == 0:
                for cp_ in range(2):
                    for q_ in (1, 2):
                        to_neighbour(0, q_, cp_).start()
            elif q < 3 and k in (1, 3):
                pass_on_neighbours(0, cp)
                if k == 1:
                    for q_ in (1, 2):
                        to_neighbour(1, q_).start()
            elif k == 5:
                for cp_ in range(2):
                    pass_on_relayed(0, cp_)
                pass_on_neighbours(1)
            if q in (1, 2):
                from_sibling(0, q, cp).wait_recv()
            elif q == 3:
                for sub in range(2):
                    relayed_from_sibling(0, sub, cp).wait_recv()

        rows = pl.ds(pl.multiple_of(i * tm, tm), tm)
        slot = i % 2
        for k, (q, cp) in enumerate(phases):
            @pl.when(p == k)
            def _(k=k, q=q, cp=cp):
                @pl.when(i == 0)
                def _():
                    if k == 0:
                        arrivals(0)
                        shard_load(0).start()
                    else:
                        drain(k - 1)
                    shard_load(k).wait()

                if k + 1 < len(phases):
                    @pl.when(i == max(nrow - 2, 0))
                    def _():
                        arrivals(k + 1)
                        shard_load(k + 1).start()

                if k == 0:
                    wx = D // X_STREAMS
                    ssq = sum(jnp.sum(xr[...] * xr[...], axis=-1, keepdims=True) for xr in x_refs)
                    r = lax.rsqrt(ssq * (1.0 / D) + EPS)
                    for n, xr in enumerate(x_refs):
                        cols = slice(n * wx, (n + 1) * wx)
                        hv = ((xr[...] * r * g_ref[:, cols]) * (1.0 + sc_ref[:, cols]) + sh_ref[:, cols]).astype(BF16)
                        h_ref[:, cols] = hv
                        h_all[rows, cols] = hv

                @pl.when(i >= 2)
                def _():
                    out_copy(k, slot, 0).wait()

                w = cols_of(q, cp)[1]
                obuf[slot, :, 0:w] = jnp.dot(h_all[rows, :], wbuf[k % 2, :, 0:w], preferred_element_type=F32)
                out_copy(k, slot, pl.multiple_of(i * tm, tm)).start()

        @pl.when((p == len(phases) - 1) & (i == nrow - 1))
        def _():
            drain(len(phases) - 1)
            pass_on_relayed(1)
            for q in (1, 2):
                from_sibling(1, q).wait_recv()
            for sub in range(2):
                relayed_from_sibling(1, sub).wait_recv()
            for which, cps in ((0, (0, 1)), (1, (None,))):
                for cp in cps:
                    for q in (1, 2):
                        to_neighbour(which, q, cp).wait_send()
                        relay(which, q, cp).wait_send()
                        to_sibling(which, q, cp).wait_send()
                        relayed_to_sibling(which, q - 1, cp).wait_send()

    vec = pl.BlockSpec((1, D), lambda p, i, pos: (0, 0))
    first_phase_rows = lambda p, i, pos: (jnp.where(p == 0, i, nrow - 1), 0)
    anyspec = pl.BlockSpec(memory_space=pl.ANY)
    x_spec = lambda n: pl.BlockSpec((tm, D // X_STREAMS), lambda p, i, pos: (jnp.where(p == 0, i, nrow - 1), n))
    return pl.pallas_call(
        body, name="proj_gather",
        grid_spec=pltpu.PrefetchScalarGridSpec(
            num_scalar_prefetch=1, grid=(len(phases), nrow),
            in_specs=[x_spec(n) for n in range(X_STREAMS)] + [vec, vec, vec, anyspec, anyspec],
            out_specs=[pl.BlockSpec((tm, D), first_phase_rows), anyspec, anyspec, anyspec],
            scratch_shapes=[pltpu.VMEM((s, D), BF16), pltpu.VMEM((2, D, W_IN_SHARD), BF16), pltpu.VMEM((2, tm, W_IN_SHARD), F32),
                            pltpu.SemaphoreType.DMA((24,)), pltpu.SemaphoreType.DMA((24,)), pltpu.SemaphoreType.DMA((2,)),
                            pltpu.SemaphoreType.DMA((2, OUT_STREAMS))]),
        out_shape=[jax.ShapeDtypeStruct((s, D), BF16), jax.ShapeDtypeStruct((s, D_IN), F32),
                   jax.ShapeDtypeStruct((D, D_IN), BF16), jax.ShapeDtypeStruct((D, D), BF16)],
        input_output_aliases={X_STREAMS + 4: 2, X_STREAMS + 5: 3},
        compiler_params=_params("arbitrary", "arbitrary"),
    )(pos, *([x] * X_STREAMS), shift, scale, norm_g, wi_full, wo_full)


def _proj_specs(rev_nb=None, with_q=True):
    if rev_nb is None:
        row = lambda i: i
    else:
        row = lambda i: rev_nb - 1 - i
    wide = lambda col: pl.BlockSpec((BLK, D_A), lambda i: (row(i), col))
    kv = lambda col: pl.BlockSpec((BLK, D_KV), lambda i: (row(i), col))
    half = lambda col: pl.BlockSpec((BLK, 512), lambda i: (row(i), col))
    return ([wide(0), wide(1), wide(2)] + ([wide(3)] if with_q else [])
            + [kv(OFF_K // D_KV), kv(OFF_V // D_KV), half(OFF_ZB // 512), half(OFF_ZB // 512 + 1)])


def _mix_fwd_call(proj, cos, sin, ln_g, ln_b, w_sp, b_sp_t, sinks):
    s = proj.shape[0]
    nb = s // BLK

    def body(ua_ref, va_ref, za_ref, q_ref, k_ref, v_ref, zb0_ref, zb1_ref, cos_ref, sin_ref, lg_ref, lb_ref,
             w_ref, bt_ref, sinks_ref, y_ref, probs_ref, ost_ref, psink_ref, qm_ref, kdup_ref, vdup_ref, bias_ref):
        i = pl.program_id(0)
        first_half, lo = _lane_masks()
        cos_t = cos_ref[...]
        sin_t = sin_ref[...]

        _, _, vln = _layer_norm_fwd(va_ref[...], lg_ref[...], lb_ref[...])
        tril = _tril()
        for g in range(GROUPS):
            cols = slice(g * BLK, (g + 1) * BLK)
            wg = jnp.where(tril, w_ref[g], 0.0).astype(BF16)
            sg = jnp.dot(wg, vln[:, cols].astype(BF16), preferred_element_type=F32) + bt_ref[:, g:g + 1]
            gate, _ = _silu_parts(za_ref[:, cols])
            y_ref[:, cols] = (ua_ref[:, cols] * sg * gate).astype(BF16)

        @pl.when(i == 0)
        def _():
            kdup_ref[:, 0:BLK, :] = jnp.zeros((N_KV, BLK, LANE), BF16)
            vdup_ref[:, 0:BLK, :] = jnp.zeros((N_KV, BLK, LANE), BF16)
            _band_bias(bias_ref)

        @pl.when(i > 0)
        def _():
            kdup_ref[:, 0:BLK, :] = kdup_ref[:, BLK:2 * BLK, :]
            vdup_ref[:, 0:BLK, :] = vdup_ref[:, BLK:2 * BLK, :]

        for ks in range(2):
            cols = slice(ks * LANE, (ks + 1) * LANE)
            kr = _rope(k_ref[:, cols], cos_t, sin_t, first_half)
            for n, (kd, vd) in enumerate(zip(_dup_kv(kr, lo), _dup_kv(v_ref[:, cols], lo))):
                kdup_ref[2 * ks + n, BLK:2 * BLK, :] = kd
                vdup_ref[2 * ks + n, BLK:2 * BLK, :] = vd
        for sb in range(8):
            _stack_heads(qm_ref, sb, _rope(q_ref[:, sb * LANE:(sb + 1) * LANE], cos_t, sin_t, first_half) * SCALE, lo, BF16)

        block_kind = jnp.where(i > 0, 1, 0)

        psink_ref[...] = jnp.zeros((Q_PER_KV * BLK, LANE), F32)
        lane_q = lax.broadcasted_iota(jnp.int32, (Q_PER_KV * BLK, LANE), 1)

        def kv_head(kh, carry):
            probs, psink = _softmax_sink(qm_ref[kh], kdup_ref[kh], bias_ref[block_kind], _sink_column(sinks_ref, kh))
            probs_ref[kh] = probs
            psink_ref[...] = jnp.where(lane_q == kh, psink, psink_ref[...])
            ost_ref[kh] = jnp.dot(probs.astype(BF16), vdup_ref[kh], preferred_element_type=F32)
            return carry

        lax.fori_loop(0, N_KV, kv_head, 0, unroll=2)
        for sb in range(8):
            cols = slice(sb * LANE, (sb + 1) * LANE)
            zb = zb0_ref[:, cols] if sb < 4 else zb1_ref[:, (sb - 4) * LANE:(sb - 3) * LANE]
            gate, _ = _silu_parts(zb)
            y_ref[:, D_A + sb * LANE:D_A + (sb + 1) * LANE] = (_unstack_heads(ost_ref, sb, lo) * gate).astype(BF16)

    tab = pl.BlockSpec((BLK, LANE), lambda i: (i, 0))
    return pl.pallas_call(
        body, name="mix_fwd", grid=(nb,),
        in_specs=_proj_specs() + [
            tab, tab, pl.BlockSpec((1, D_A), lambda i: (0, 0)), pl.BlockSpec((1, D_A), lambda i: (0, 0)),
            pl.BlockSpec((GROUPS, BLK, BLK), lambda i: (0, 0, 0)), pl.BlockSpec((BLK, GROUPS), lambda i: (0, 0)),
            pl.BlockSpec(memory_space=pltpu.SMEM)],
        out_specs=[pl.BlockSpec((BLK, 2 * D_A), lambda i: (i, 0)),
                   pl.BlockSpec((None, N_KV, Q_PER_KV * BLK, 2 * BLK), lambda i: (i, 0, 0, 0)),
                   pl.BlockSpec((None, N_KV, Q_PER_KV * BLK, LANE), lambda i: (i, 0, 0, 0)),
                   pl.BlockSpec((None, Q_PER_KV * BLK, LANE), lambda i: (i, 0, 0)),
                   pl.BlockSpec((None, N_KV, Q_PER_KV * BLK, LANE), lambda i: (i, 0, 0, 0))],
        out_shape=[jax.ShapeDtypeStruct((s, 2 * D_A), BF16), jax.ShapeDtypeStruct((nb, N_KV, Q_PER_KV * BLK, 2 * BLK), F32),
                   jax.ShapeDtypeStruct((nb, N_KV, Q_PER_KV * BLK, LANE), F32), jax.ShapeDtypeStruct((nb, Q_PER_KV * BLK, LANE), F32),
                   jax.ShapeDtypeStruct((nb, N_KV, Q_PER_KV * BLK, LANE), BF16)],
        scratch_shapes=[pltpu.VMEM((N_KV, 2 * BLK, LANE), BF16), pltpu.VMEM((N_KV, 2 * BLK, LANE), BF16),
                        pltpu.VMEM((2, Q_PER_KV * BLK, 2 * BLK), F32)],
        compiler_params=_params("arbitrary"),
    )(proj, proj, proj, proj, proj, proj, proj, proj, cos, sin, ln_g, ln_b, w_sp, b_sp_t, sinks)


def _tail_call(y, w_out_bf, x, target, gate, shift_f, scale_f, gf):
    s = x.shape[0]
    tm = min(s, 256)
    nsteps = s // tm

    def body(y_ref, w_ref, x_ref, t_ref, gate_ref, shf_ref, scf_ref, gf_ref, dx2_ref, do_ref, dy_ref, st_ref):
        i = pl.program_id(0)

        @pl.when(i == 0)
        def _():
            st_ref[...] = jnp.zeros((8, D), F32)

        o = jnp.dot(y_ref[...], w_ref[...], preferred_element_type=F32)
        gate_v = gate_ref[...]
        x2 = x_ref[...] + gate_v * o
        r2 = lax.rsqrt(jnp.mean(x2 * x2, axis=-1, keepdims=True) + EPS)
        xn2 = x2 * r2
        hn2 = xn2 * gf_ref[...]
        one_sc = 1.0 + scf_ref[...]
        err = hn2 * one_sc + shf_ref[...] - t_ref[...]
        dout = err * (1.0 / D)
        dhn2 = dout * one_sc
        dxn2 = dhn2 * gf_ref[...]
        dx2 = r2 * (dxn2 - xn2 * jnp.mean(dxn2 * xn2, axis=-1, keepdims=True))
        dx2_ref[...] = dx2
        do = (dx2 * gate_v).astype(BF16)
        do_ref[...] = do
        dy_ref[...] = lax.dot_general(do, w_ref[...], NT, preferred_element_type=F32)
        st_ref[0:1, :] += jnp.sum(dout, axis=0, keepdims=True)
        st_ref[1:2, :] += jnp.sum(dout * hn2, axis=0, keepdims=True)
        st_ref[2:3, :] += jnp.sum(dhn2 * xn2, axis=0, keepdims=True)
        st_ref[3:4, :] += jnp.sum(dx2 * o, axis=0, keepdims=True)
        st_ref[4:5, :] += jnp.sum(err * err, axis=0, keepdims=True)

        @pl.when(i == nsteps - 1)
        def _():
            st_ref[5:6, :] = jnp.full((1, D), 0.5 / D, F32) * jnp.sum(st_ref[4:5, :])

    vec = pl.BlockSpec((1, D), lambda i: (0, 0))
    rows = lambda: pl.BlockSpec((tm, D), lambda i: (i, 0))
    return pl.pallas_call(
        body, name="tail", grid=(nsteps,),
        in_specs=[rows(), pl.BlockSpec((D, D), lambda i: (0, 0)), rows(), rows(), vec, vec, vec, vec],
        out_specs=[rows(), rows(), rows(), pl.BlockSpec((8, D), lambda i: (0, 0))],
        out_shape=[jax.ShapeDtypeStruct((s, D), F32), jax.ShapeDtypeStruct((s, D), BF16), jax.ShapeDtypeStruct((s, D), F32),
                   jax.ShapeDtypeStruct((8, D), F32)],
        compiler_params=_params("arbitrary"),
    )(y, w_out_bf, x, target, gate, shift_f, scale_f, gf)


def _tn_call(a, b, name):
    s, m = a.shape
    n = b.shape[1]
    tn = 1024
    ts = min(s, 1024)
    nk = s // ts

    def body(a_ref, b_ref, o_ref, acc_ref):
        k = pl.program_id(1)

        @pl.when(k == 0)
        def _():
            acc_ref[...] = jnp.zeros((m, tn), F32)

        acc_ref[...] += lax.dot_general(a_ref[...], b_ref[...], TN, preferred_element_type=F32)

        @pl.when(k == nk - 1)
        def _():
            o_ref[...] = acc_ref[...].astype(BF16)

    return pl.pallas_call(
        body, name=name, grid=(n // tn, nk),
        in_specs=[pl.BlockSpec((ts, m), lambda j, k: (k, 0)), pl.BlockSpec((ts, tn), lambda j, k: (k, j))],
        out_specs=pl.BlockSpec((m, tn), lambda j, k: (0, j)),
        out_shape=jax.ShapeDtypeStruct((m, n), BF16),
        scratch_shapes=[pltpu.VMEM((m, tn), F32)],
        compiler_params=_params("parallel", "arbitrary"),
    )(a, b)


def _tn_shards_call(pos, a, b, qs, name):
    s, m = a.shape
    ts = min(s, 1024)
    nk = s // ts

    def body(pos_ref, a_ref, b_ref, o_ref, acc_ref):
        k = pl.program_id(1)

        @pl.when(k == 0)
        def _():
            acc_ref[...] = jnp.zeros((m, W_IN_SHARD), F32)

        acc_ref[...] += lax.dot_general(a_ref[...], b_ref[...], TN, preferred_element_type=F32)

        @pl.when(k == nk - 1)
        def _():
            o_ref[...] = acc_ref[...].astype(BF16)

    def shard(j, pos):
        q = qs[0]
        for n in range(1, len(qs)):
            q = jnp.where(j == n, qs[n], q)
        return jnp.bitwise_xor(pos[0], q)

    return pl.pallas_call(
        body, name=name,
        grid_spec=pltpu.PrefetchScalarGridSpec(
            num_scalar_prefetch=1, grid=(len(qs), nk),
            in_specs=[pl.BlockSpec((ts, m), lambda j, k, pos: (k, 0)),
                      pl.BlockSpec((ts, W_IN_SHARD), lambda j, k, pos: (k, shard(j, pos)))],
            out_specs=pl.BlockSpec((m, W_IN_SHARD), lambda j, k, pos: (0, j)),
            scratch_shapes=[pltpu.VMEM((m, W_IN_SHARD), F32)]),
        out_shape=jax.ShapeDtypeStruct((m, len(qs) * W_IN_SHARD), BF16),
        compiler_params=_params("parallel", "arbitrary"),
    )(pos, a, b)


def _mix_bwd_call(proj, dy, probs, outs, psinks, qms, tables, ln_g, ln_b, w_sp, w_sp_t, b_sp_t):
    s = proj.shape[0]
    nb = s // BLK
    rev = lambda i: nb - 1 - i
    prev = lambda i: jnp.maximum(nb - 2 - i, 0)

    def body(ua_ref, va_ref, za_ref, k_ref, v_ref, zb0_ref, zb1_ref, kp_ref, vp_ref, dy_ref,
             probs_ref, ost_ref, psink_ref, qm_ref, cos_ref, sin_ref, cosp_ref, sinp_ref, lg_ref, lb_ref, w_ref, wt_ref, bt_ref,
             dp_ref, lnst_ref, dw_ref, dbt_ref, dsink_ref,
             kdup_ref, vdup_ref, dvln_ref, dom_ref, dqst_ref, dkdup_ref, dvdup_ref, kcar_ref, vcar_ref, sigb_ref, dbs_ref):
        i = pl.program_id(0)
        first_half, lo = _lane_masks()
        lane8 = lax.broadcasted_iota(jnp.int32, (8, LANE), 1)
        cos_t = cos_ref[...]
        sin_t = sin_ref[...]

        @pl.when(i == 0)
        def _():
            lnst_ref[...] = jnp.zeros((8, D_A), F32)
            dw_ref[...] = jnp.zeros((GROUPS, BLK, BLK), F32)
            dbs_ref[...] = jnp.zeros((GROUPS, BLK, LANE), F32)
            dsink_ref[...] = jnp.zeros((8, LANE), F32)
            kcar_ref[...] = jnp.zeros((BLK, D_KV), F32)
            vcar_ref[...] = jnp.zeros((BLK, D_KV), F32)

        vhat, rstd, vln = _layer_norm_fwd(va_ref[...], lg_ref[...], lb_ref[...])
        tril = _tril()
        triu = jnp.logical_not(tril) | (lax.broadcasted_iota(jnp.int32, (BLK, BLK), 0) == lax.broadcasted_iota(jnp.int32, (BLK, BLK), 1))
        for g in range(GROUPS):
            cols = slice(g * BLK, (g + 1) * BLK)
            vln_g = vln[:, cols].astype(BF16)
            wg = jnp.where(tril, w_ref[g], 0.0).astype(BF16)
            sg = jnp.dot(wg, vln_g, preferred_element_type=F32) + bt_ref[:, g:g + 1]
            za = za_ref[:, cols]
            gate, sig = _silu_parts(za)
            ua = ua_ref[:, cols]
            dya_g = dy_ref[:, cols]
            dya = dya_g * gate
            dp_ref[:, cols] = (dya * sg).astype(BF16)
            dp_ref[:, 2 * D_A + g * BLK:2 * D_A + (g + 1) * BLK] = (
                dya_g * (ua * sg) * (sig * (1.0 + za * (1.0 - sig)))).astype(BF16)
            ds = dya * ua
            ds_b = ds.astype(BF16)
            wtg = jnp.where(triu, wt_ref[g], 0.0).astype(BF16)
            dvln_ref[:, cols] = jnp.dot(wtg, ds_b, preferred_element_type=F32)
            dw_ref[g] += jnp.where(tril, lax.dot_general(ds_b, vln_g, NT, preferred_element_type=F32), 0.0)
            dbs_ref[g] += ds

        @pl.when(i == nb - 1)
        def _():
            lane_b = lax.broadcasted_iota(jnp.int32, (BLK, LANE), 1)
            db_acc = jnp.zeros((BLK, LANE), F32)
            for g in range(GROUPS):
                db_acc = db_acc + jnp.where(lane_b == g, jnp.sum(dbs_ref[g], axis=-1, keepdims=True), 0.0)
            dbt_ref[...] = db_acc

        dvln = dvln_ref[...]
        lnst_ref[0:1, :] += jnp.sum(dvln * vhat, axis=0, keepdims=True)
        lnst_ref[1:2, :] += jnp.sum(dvln, axis=0, keepdims=True)
        dvhat = dvln * lg_ref[...]
        m1 = jnp.mean(dvhat, axis=-1, keepdims=True)
        m2 = jnp.mean(dvhat * vhat, axis=-1, keepdims=True)
        dp_ref[:, D_A:2 * D_A] = (rstd * (dvhat - m1 - vhat * m2)).astype(BF16)

        cosp = cosp_ref[...]
        sinp = sinp_ref[...]
        for ks in range(2):
            cols = slice(ks * LANE, (ks + 1) * LANE)
            kr = _rope(k_ref[:, cols], cos_t, sin_t, first_half)
            kpr = _rope(kp_ref[:, cols], cosp, sinp, first_half)
            for n, (kc, vc, kp, vp) in enumerate(zip(_dup_kv(kr, lo), _dup_kv(v_ref[:, cols], lo),
                                                     _dup_kv(kpr, lo), _dup_kv(vp_ref[:, cols], lo))):
                kdup_ref[2 * ks + n, BLK:2 * BLK, :] = kc
                vdup_ref[2 * ks + n, BLK:2 * BLK, :] = vc
                kdup_ref[2 * ks + n, 0:BLK, :] = kp
                vdup_ref[2 * ks + n, 0:BLK, :] = vp
        for sb in range(8):
            cols = slice(sb * LANE, (sb + 1) * LANE)
            zb = zb0_ref[:, cols] if sb < 4 else zb1_ref[:, (sb - 4) * LANE:(sb - 3) * LANE]
            gate, sig = _silu_parts(zb)
            sigb_ref[:, cols] = sig
            _stack_heads(dom_ref, sb, dy_ref[:, D_A + sb * LANE:D_A + (sb + 1) * LANE] * gate, lo, F32)

        lane_q = lax.broadcasted_iota(jnp.int32, (Q_PER_KV * BLK, LANE), 1)

        def kv_head(kh, sink_acc):
            qm = qm_ref[kh]
            kd = kdup_ref[kh]
            vd = vdup_ref[kh]
            probs = probs_ref[kh]
            probs_b = probs.astype(BF16)
            o = ost_ref[kh]
            dom = dom_ref[kh]
            dom_b = dom.astype(BF16)
            delta = jnp.sum(dom * o, axis=-1, keepdims=True)
            dpr = lax.dot_general(dom_b, vd, NT, preferred_element_type=F32)
            dss = (probs * (dpr - delta)).astype(BF16)
            sink_acc = sink_acc + jnp.where(lane_q == kh, psink_ref[...] * delta, 0.0)
            dqst_ref[kh] = jnp.dot(dss, kd, preferred_element_type=F32)
            dkdup_ref[kh] = lax.dot_general(dss, qm, TN, preferred_element_type=F32)
            dvdup_ref[kh] = lax.dot_general(probs_b, dom_b, TN, preferred_element_type=F32)
            return sink_acc

        sink_acc = jnp.zeros((Q_PER_KV * BLK, LANE), F32)
        for kh in range(N_KV):
            sink_acc = kv_head(kh, sink_acc)
        lane1 = lax.broadcasted_iota(jnp.int32, (1, LANE), 1)
        dsink_acc = jnp.zeros((8, LANE), F32)
        for n in range(Q_PER_KV):
            col = jnp.sum(sink_acc[n * BLK:(n + 1) * BLK], axis=0, keepdims=True)
            for kh in range(N_KV):
                dsink_acc = dsink_acc + jnp.where(lane8 == Q_PER_KV * kh + n, -jnp.sum(jnp.where(lane1 == kh, col, 0.0)), 0.0)
        row0 = lax.broadcasted_iota(jnp.int32, (8, LANE), 0) == 0
        dsink_ref[...] += jnp.where(row0, dsink_acc, 0.0)

        for sb in range(8):
            cols = slice(sb * LANE, (sb + 1) * LANE)
            zb = zb0_ref[:, cols] if sb < 4 else zb1_ref[:, (sb - 4) * LANE:(sb - 3) * LANE]
            sig = sigb_ref[:, cols]
            dyb = dy_ref[:, D_A + sb * LANE:D_A + (sb + 1) * LANE]
            dp_ref[:, OFF_ZB + sb * LANE:OFF_ZB + (sb + 1) * LANE] = (
                dyb * _unstack_heads(ost_ref, sb, lo) * (sig * (1.0 + zb * (1.0 - sig)))).astype(BF16)
            dq_r = _unstack_heads(dqst_ref, sb, lo) * SCALE
            dp_ref[:, OFF_Q + sb * LANE:OFF_Q + (sb + 1) * LANE] = _unrope(dq_r, cos_t, sin_t, first_half).astype(BF16)

        lo2 = lax.broadcasted_iota(jnp.int32, (2 * BLK, LANE), 1) < HEAD
        for ks in range(2):
            cols = slice(ks * LANE, (ks + 1) * LANE)
            dk_band = _fold_halves(dkdup_ref[2 * ks], dkdup_ref[2 * ks + 1], lo2)
            dv_band = _fold_halves(dvdup_ref[2 * ks], dvdup_ref[2 * ks + 1], lo2)
            dkr = dk_band[BLK:2 * BLK, :] + kcar_ref[:, cols]
            dp_ref[:, OFF_K + ks * LANE:OFF_K + (ks + 1) * LANE] = _unrope(dkr, cos_t, sin_t, first_half).astype(BF16)
            dp_ref[:, OFF_V + ks * LANE:OFF_V + (ks + 1) * LANE] = (
                dv_band[BLK:2 * BLK, :] + vcar_ref[:, cols]).astype(BF16)
            kcar_ref[:, cols] = dk_band[0:BLK, :]
            vcar_ref[:, cols] = dv_band[0:BLK, :]

    tab = pl.BlockSpec((BLK, LANE), lambda i: (rev(i), 0))
    kvp = lambda col: pl.BlockSpec((BLK, D_KV), lambda i: (prev(i), col))
    vec = pl.BlockSpec((1, D_A), lambda i: (0, 0))
    w3 = pl.BlockSpec((GROUPS, BLK, BLK), lambda i: (0, 0, 0))
    return pl.pallas_call(
        body, name="mix_bwd", grid=(nb,),
        in_specs=_proj_specs(nb, with_q=False) + [
            kvp(OFF_K // D_KV), kvp(OFF_V // D_KV), pl.BlockSpec((BLK, 2 * D_A), lambda i: (rev(i), 0)),
            pl.BlockSpec((None, N_KV, Q_PER_KV * BLK, 2 * BLK), lambda i: (rev(i), 0, 0, 0)),
            pl.BlockSpec((None, N_KV, Q_PER_KV * BLK, LANE), lambda i: (rev(i), 0, 0, 0)),
            pl.BlockSpec((None, Q_PER_KV * BLK, LANE), lambda i: (rev(i), 0, 0)),
            pl.BlockSpec((None, N_KV, Q_PER_KV * BLK, LANE), lambda i: (rev(i), 0, 0, 0)),
            tab, tab, tab, tab, vec, vec, w3, w3, pl.BlockSpec((BLK, GROUPS), lambda i: (0, 0))],
        out_specs=[pl.BlockSpec((BLK, D_IN), lambda i: (rev(i), 0)), pl.BlockSpec((8, D_A), lambda i: (0, 0)), w3,
                   pl.BlockSpec((BLK, LANE), lambda i: (0, 0)), pl.BlockSpec((8, LANE), lambda i: (0, 0))],
        out_shape=[jax.ShapeDtypeStruct((s, D_IN), BF16), jax.ShapeDtypeStruct((8, D_A), F32),
                   jax.ShapeDtypeStruct((GROUPS, BLK, BLK), F32), jax.ShapeDtypeStruct((BLK, LANE), F32),
                   jax.ShapeDtypeStruct((8, LANE), F32)],
        scratch_shapes=[pltpu.VMEM((N_KV, 2 * BLK, LANE), BF16), pltpu.VMEM((N_KV, 2 * BLK, LANE), BF16),
                        pltpu.VMEM((BLK, D_A), F32),
                        pltpu.VMEM((N_KV, Q_PER_KV * BLK, LANE), F32), pltpu.VMEM((N_KV, Q_PER_KV * BLK, LANE), F32),
                        pltpu.VMEM((N_KV, 2 * BLK, LANE), F32), pltpu.VMEM((N_KV, 2 * BLK, LANE), F32),
                        pltpu.VMEM((BLK, D_KV), F32), pltpu.VMEM((BLK, D_KV), F32), pltpu.VMEM((BLK, D_B), F32),
                        pltpu.VMEM((GROUPS, BLK, LANE), F32)],
        compiler_params=_params("arbitrary"),
    )(proj, proj, proj, proj, proj, proj, proj, proj, proj, dy, probs, outs, psinks, qms, *tables, ln_g, ln_b,
      w_sp, w_sp_t, b_sp_t)


def _dh_call(dproj, w_bf, x, dx2, scale, norm_g):
    s = x.shape[0]
    tm = min(s, 512)
    tk = W_IN_SHARD
    nk = D_IN // tk

    def body(dp_ref, w_ref, x_ref, dx2_ref, sc_ref, g_ref, gx_ref, st_ref, acc_ref):
        i = pl.program_id(0)
        k = pl.program_id(1)

        @pl.when((i == 0) & (k == 0))
        def _():
            st_ref[...] = jnp.zeros((8, D), F32)

        @pl.when(k == 0)
        def _():
            acc_ref[...] = jnp.zeros((tm, D), F32)

        acc_ref[...] += lax.dot_general(dp_ref[...], w_ref[...], NT, preferred_element_type=F32)

        @pl.when(k == nk - 1)
        def _():
            g = g_ref[...]
            one_sc = 1.0 + sc_ref[...]

            def chunk(n, carry):
                rows = pl.ds(pl.multiple_of(n * BLK, BLK), BLK)
                dh = acc_ref[rows, :]
                xv = x_ref[rows, :]
                r = lax.rsqrt(jnp.mean(xv * xv, axis=-1, keepdims=True) + EPS)
                xn = xv * r
                dhn = dh * one_sc
                dxn = dhn * g
                gx_ref[rows, :] = dx2_ref[rows, :] + r * (dxn - xn * jnp.mean(dxn * xn, axis=-1, keepdims=True))
                st_ref[0:1, :] += jnp.sum(dh, axis=0, keepdims=True)
                st_ref[1:2, :] += jnp.sum(dh * (xn * g), axis=0, keepdims=True)
                st_ref[2:3, :] += jnp.sum(dhn * xn, axis=0, keepdims=True)
                return carry

            lax.fori_loop(0, tm // BLK, chunk, 0)

    vec = pl.BlockSpec((1, D), lambda i, k: (0, 0))
    rows = lambda: pl.BlockSpec((tm, D), lambda i, k: (i, 0))
    return pl.pallas_call(
        body, name="dh", grid=(s // tm, nk),
        in_specs=[pl.BlockSpec((tm, tk), lambda i, k: (i, k)), pl.BlockSpec((D, tk), lambda i, k: (0, k)), rows(), rows(), vec, vec],
        out_specs=[rows(), pl.BlockSpec((8, D), lambda i, k: (0, 0))],
        out_shape=[jax.ShapeDtypeStruct((s, D), F32), jax.ShapeDtypeStruct((8, D), F32)],
        scratch_shapes=[pltpu.VMEM((tm, D), F32)],
        compiler_params=_params("arbitrary", "arbitrary"),
    )(dproj, w_bf, x, dx2, scale, norm_g)


def _adam_math(w, g, m, v):
    m_new = ADAM_B1 * m + (1.0 - ADAM_B1) * g
    v_new = ADAM_B2 * v + (1.0 - ADAM_B2) * (g * g)
    m_hat = m_new / ADAM_C1
    v_hat = v_new / ADAM_C2
    delta = -ADAM_LR * (m_hat / (jnp.sqrt(v_hat) + ADAM_EPS) + ADAM_WD * w)
    return delta, m_new, v_new


def _adam_small_call(tensors):
    n = len(tensors)

    def body(*refs):
        ins, outs = refs[:4 * n], refs[4 * n:]
        for t in range(n):
            w_ref, g_ref, m_ref, v_ref = ins[4 * t:4 * t + 4]
            d, mo, vo = _adam_math(w_ref[...], g_ref[...], m_ref[...], v_ref[...])
            outs[3 * t][...], outs[3 * t + 1][...], outs[3 * t + 2][...] = d, mo, vo

    vm = pl.BlockSpec(memory_space=pltpu.VMEM)
    flat = [a for t in tensors for a in t]
    out = pl.pallas_call(
        body, name="adam_small", in_specs=[vm] * (4 * n), out_specs=[vm] * (3 * n),
        out_shape=[jax.ShapeDtypeStruct(t[0].shape, F32) for t in tensors for _ in range(3)],
        compiler_params=pltpu.CompilerParams(vmem_limit_bytes=VMEM_LIMIT),
    )(*flat)
    return [tuple(out[3 * t:3 * t + 3]) for t in range(n)]


def _adam_halves_call(pos, w, mine, theirs, m, v, name):
    r, n = w.shape
    half = r // 2
    tr = ADAM_ROWS
    nh = half // tr

    def body(pos_ref, w_ref, mine_ref, theirs_ref, m_ref, v_ref, g_ref, d_ref, mo_ref, vo_ref):
        is_mine = (pl.program_id(0) // nh) == pos_ref[1]
        g = jnp.where(is_mine, mine_ref[...], theirs_ref[...])
        g_ref[...] = g
        d_ref[...], mo_ref[...], vo_ref[...] = _adam_math(w_ref[...], g, m_ref[...], v_ref[...])

    spec = lambda: pl.BlockSpec((tr, n), lambda i, pos: (i, 0))

    def half_spec(core_of_half):
        def index(i, pos):
            first = core_of_half(pos) == 0
            active = (i // nh == 0) == first
            return jnp.where(active, i % nh, jnp.where(first, nh - 1, 0)), 0
        return pl.BlockSpec((tr, n), index)

    return pl.pallas_call(
        body, name=name,
        grid_spec=pltpu.PrefetchScalarGridSpec(
            num_scalar_prefetch=1, grid=(r // tr,),
            in_specs=[spec(), half_spec(lambda pos: pos[1]), half_spec(lambda pos: 1 - pos[1]), spec(), spec()],
            out_specs=[spec() for _ in range(4)]),
        out_shape=[jax.ShapeDtypeStruct((r, n), F32)] * 4, compiler_params=_params("arbitrary"),
    )(pos, w, mine, theirs, m, v)


def _adam_outer_call(w, ct, dm, m, v, name):
    r, n = w.shape
    tr = ADAM_ROWS

    def body(w_ref, ct_ref, dm_ref, m_ref, v_ref, g_ref, d_ref, mo_ref, vo_ref):
        g = ct_ref[:, 0:1] * dm_ref[0:1, :]
        for b in range(1, N_DEV):
            g = g + ct_ref[:, b:b + 1] * dm_ref[b:b + 1, :]
        g_ref[...] = g
        d_ref[...], mo_ref[...], vo_ref[...] = _adam_math(w_ref[...], g, m_ref[...], v_ref[...])

    spec = lambda: pl.BlockSpec((tr, n), lambda i: (i, 0))
    return pl.pallas_call(
        body, name=name, grid=(r // tr,),
        in_specs=[spec(), pl.BlockSpec((tr, N_DEV), lambda i: (i, 0)), pl.BlockSpec((N_DEV, n), lambda i: (0, 0)), spec(), spec()],
        out_specs=[spec() for _ in range(4)],
        out_shape=[jax.ShapeDtypeStruct((r, n), F32)] * 4, compiler_params=_params("parallel"),
    )(w, ct, dm, m, v)


def _sum_pieces_call(pos, part, part_block, recvs, name):
    r, n = recvs[0].shape[1:]
    tr = min(r, 256)
    nrb = r // tr

    def body(pos_ref, p_ref, *refs):
        acc = p_ref[...].astype(F32)
        for r_ref in refs[:-1]:
            for d in range(r_ref.shape[0]):
                acc = acc + r_ref[d].astype(F32)
        refs[-1][...] = acc

    return pl.pallas_call(
        body, name=name,
        grid_spec=pltpu.PrefetchScalarGridSpec(
            num_scalar_prefetch=1, grid=(nrb,),
            in_specs=[pl.BlockSpec((tr, n), lambda i, pos: part_block(i, pos, nrb))] + [
                pl.BlockSpec((rv.shape[0], tr, n), lambda i, pos: (0, i, 0)) for rv in recvs],
            out_specs=pl.BlockSpec((tr, n), lambda i, pos: (i, 0))),
        out_shape=jax.ShapeDtypeStruct((r, n), F32), compiler_params=_params("parallel"),
    )(pos, part, *recvs)


def _coords():
    return lax.axis_index("x"), lax.axis_index("y"), lax.axis_index("c")


CAST_ROWS = 256


def _allgather_sum_call(blk, name, with_sum, cast=None):
    m_per, n = blk.shape
    n_out = 2 if with_sum else 1
    if cast is not None:
        w, full_shape = cast
        wr, wn = w.shape
        by_cols = full_shape[0] == wr
        tr = min(wr, CAST_ROWS)
        n_chunk = wr // tr

    def body(*refs):
        x_ref = refs[0]
        out_ref = refs[1 + (cast is not None)]
        rest = refs[1 + (cast is not None) + n_out + (cast is not None):]
        send_sems, recv_sems, local_sem = rest[:3]
        x, y, c = _coords()
        me, sibling = (x, y, c), (x, y, 1 - c)
        chips = [(1 - x, y), (x, 1 - y), (1 - x, 1 - y)]

        def rows(px, py, pc):
            return out_ref.at[pl.ds((4 * px + 2 * py + pc) * m_per, m_per), :]

        def copy(k, block, to, src=None):
            return pltpu.make_async_remote_copy(
                src_ref=rows(*block) if src is None else src, dst_ref=rows(*block),
                send_sem=send_sems.at[k], recv_sem=recv_sems.at[k], device_id=to, device_id_type=MESH)

        mine = pltpu.make_async_copy(x_ref, rows(*me), local_sem)
        mine.start()
        first = [copy(0, me, sibling, src=x_ref)]
        first += [copy(1 + j, me, (*chip, c), src=x_ref) for j, chip in enumerate(chips)]
        for cp in first:
            cp.start()

        if cast is not None:
            w_ref, full_ref = refs[1], refs[1 + 1 + n_out]
            f32_buf, bf16_buf, in_sems, out_sems = rest[3:]
            chip_no = 2 * x + y

            def fetch(i):
                return pltpu.make_async_copy(w_ref.at[pl.ds(i * tr, tr), :], f32_buf.at[i % 2], in_sems.at[i % 2])

            def store(i):
                if by_cols:
                    dst = full_ref.at[pl.ds(i * tr, tr), pl.ds(chip_no * wn, wn)]
                else:
                    dst = full_ref.at[pl.ds(chip_no * wr + i * tr, tr), :]
                return pltpu.make_async_copy(bf16_buf.at[i % 2], dst, out_sems.at[i % 2])

            fetch(0).start()
            for i in range(n_chunk):
                if i + 1 < n_chunk:
                    fetch(i + 1).start()
                fetch(i).wait()
                if i >= 2:
                    store(i - 2).wait()
                bf16_buf[i % 2] = f32_buf[i % 2].astype(BF16)
                store(i).start()
            for i in range(max(n_chunk - 2, 0), n_chunk):
                store(i).wait()

        passed = [copy(4 + j, (*chip, c), sibling) for j, chip in enumerate(chips)]
        for j, chip in enumerate(chips):
            copy(1 + j, (*chip, c), me).wait_recv()
            passed[j].start()
        copy(0, sibling, me).wait_recv()
        for j, chip in enumerate(chips):
            copy(4 + j, (*chip, 1 - c), me).wait_recv()
        for cp in first + passed:
            cp.wait_send()
        mine.wait()
        if with_sum:
            sum_ref = refs[1 + (cast is not None) + 1]
            acc = out_ref[0:m_per, :]
            for d in range(1, N_DEV):
                acc = acc + out_ref[d * m_per:(d + 1) * m_per, :]
            sum_ref[...] = acc

    vm = pl.BlockSpec(memory_space=pltpu.VMEM)
    anyspec = pl.BlockSpec(memory_space=pl.ANY)
    out_shape = [jax.ShapeDtypeStruct((N_DEV * m_per, n), F32)]
    if with_sum:
        out_shape.append(jax.ShapeDtypeStruct((m_per, n), F32))
    in_specs, out_specs, operands = [vm], [vm] * n_out, [blk]
    scratch = [pltpu.SemaphoreType.DMA((7,)), pltpu.SemaphoreType.DMA((7,)), pltpu.SemaphoreType.DMA]
    if cast is not None:
        in_specs.append(anyspec)
        operands.append(w)
        out_shape.append(jax.ShapeDtypeStruct(full_shape, BF16))
        out_specs.append(anyspec)
        scratch += [pltpu.VMEM((2, tr, wn), F32), pltpu.VMEM((2, tr, wn), BF16), pltpu.SemaphoreType.DMA((2,)),
                    pltpu.SemaphoreType.DMA((2,))]
    return pl.pallas_call(
        body, name=name, out_shape=out_shape, in_specs=in_specs, out_specs=out_specs, scratch_shapes=scratch,
        compiler_params=pltpu.CompilerParams(vmem_limit_bytes=VMEM_LIMIT),
    )(*operands)


HBM_SPEC = pl.BlockSpec(memory_space=pltpu.HBM)
SEM_SPEC = pl.BlockSpec(memory_space=pltpu.SEMAPHORE)
SIDE_EFFECT = pltpu.SideEffectType.DATAFLOW_SIDE_EFFECTING


def _peer(x, y, c, q, cb):
    return (1 - x if q & 2 else x, 1 - y if q & 1 else y, 1 - c if cb else c)


def _w_in_piece(slots):
    def piece(part_ref, k, to):
        return part_ref.at[pl.ds(to[2] * (D // 2), D // 2), pl.ds(slots[k] * W_IN_SHARD, W_IN_SHARD)]
    return piece


def _w_out_piece(part_ref, k, to):
    ho = W_OUT_SHARD // 2
    return part_ref.at[pl.ds((2 * to[0] + to[1]) * W_OUT_SHARD + to[2] * ho, ho), :]


def _group_piece(part_ref, k, to):
    return part_ref.at[4 * to[0] + 2 * to[1] + to[2]]


def _whole_piece(part_ref, k, to):
    return part_ref


def _exchange_start_call(groups, name):
    ng = len(groups)
    lands = [lax.empty((len(rels),) + slot_shape, part.dtype) for part, rels, _, slot_shape in groups]

    def body(*refs):
        ins, outs = refs[:2 * ng], refs[2 * ng:]
        x, y, c = _coords()
        for g, (_, rels, piece, _) in enumerate(groups):
            part_ref, land_ref = ins[2 * g], ins[2 * g + 1]
            send_sems, recv_sems = outs[4 * g], outs[4 * g + 1]
            for k, (q, cb) in enumerate(rels):
                to = _peer(x, y, c, q, cb)
                pltpu.make_async_remote_copy(src_ref=piece(part_ref, k, to), dst_ref=land_ref.at[k], send_sem=send_sems.at[k],
                                             recv_sem=recv_sems.at[k], device_id=to, device_id_type=MESH).start()
        outs[-1][...] = jnp.zeros_like(outs[-1])

    out_shape, out_specs, operands = [], [], []
    for (part, rels, _, _), land in zip(groups, lands):
        n = len(rels)
        out_shape += [pltpu.SemaphoreType.DMA((n,)), pltpu.SemaphoreType.DMA((n,)), pltpu.HBM(part.shape, part.dtype),
                      pltpu.HBM(land.shape, land.dtype)]
        out_specs += [SEM_SPEC, SEM_SPEC, HBM_SPEC, HBM_SPEC]
        operands += [pltpu.with_memory_space_constraint(part, pltpu.HBM), pltpu.with_memory_space_constraint(land, pltpu.HBM)]
    out = pl.pallas_call(
        body, name=name,
        out_shape=tuple(out_shape) + (jax.ShapeDtypeStruct((1, 1), F32),),
        in_specs=(HBM_SPEC,) * (2 * ng), out_specs=tuple(out_specs) + (pl.BlockSpec(memory_space=pltpu.VMEM),),
        input_output_aliases={j: 4 * (j // 2) + 2 + j % 2 for j in range(2 * ng)},
        compiler_params=pltpu.CompilerParams(has_side_effects=SIDE_EFFECT),
    )(*operands)
    return [tuple(out[4 * g:4 * g + 4]) for g in range(ng)], out[-1]


def _exchange_wait_call(started, groups, after, name):
    ng = len(groups)

    def body(*refs):
        ins = refs[:4 * ng]
        x, y, c = _coords()
        for g, (_, rels, piece, _) in enumerate(groups):
            part_ref, land_ref, send_sems, recv_sems = ins[4 * g:4 * g + 4]
            for k, (q, cb) in enumerate(rels):
                to = _peer(x, y, c, q, cb)
                cp = pltpu.make_async_remote_copy(src_ref=piece(part_ref, k, to), dst_ref=land_ref.at[k], send_sem=send_sems.at[k],
                                                  recv_sem=recv_sems.at[k], device_id=to, device_id_type=MESH)
                cp.wait_send()
                cp.wait_recv()

    operands, in_specs, out_shape = [], [], []
    for send_sems, recv_sems, part_thru, land_thru in started:
        operands += [part_thru, land_thru, send_sems, recv_sems]
        in_specs += [HBM_SPEC, HBM_SPEC, SEM_SPEC, SEM_SPEC]
        out_shape += [pltpu.HBM(part_thru.shape, part_thru.dtype), pltpu.HBM(land_thru.shape, land_thru.dtype)]
    out = pl.pallas_call(
        body, name=name, out_shape=tuple(out_shape),
        in_specs=tuple(in_specs) + (pl.BlockSpec(memory_space=pl.ANY),), out_specs=(HBM_SPEC,) * (2 * ng),
        input_output_aliases={4 * g + j: 2 * g + j for g in range(ng) for j in range(2)},
        compiler_params=pltpu.CompilerParams(has_side_effects=SIDE_EFFECT),
    )(*operands, after)
    return [tuple(out[2 * g:2 * g + 2]) for g in range(ng)]


def _rope_tables(s):
    inv_freq = np.float32(10000.0) ** (-np.arange(0, HEAD, 2, dtype=np.float32) / np.float32(HEAD))
    ang = np.arange(s, dtype=np.float32)[:, None] * inv_freq[None, :]
    cos = np.tile(np.cos(ang), (1, LANE // (HEAD // 2))).astype(np.float32)
    sin = np.tile(np.sin(ang), (1, LANE // (HEAD // 2))).astype(np.float32)
    first_half = (np.arange(LANE) % HEAD) < (HEAD // 2)
    sin = np.where(first_half[None, :], -sin, sin)
    behind = lambda t: np.concatenate([t[:BLK], t[:-BLK]], axis=0)
    return tuple(jnp.asarray(t) for t in (cos, sin, behind(cos), behind(sin)))


def kernel(x, c, w_ada, b_ada, norm_g, w_in, ln_v_g, ln_v_b, w_spatial, b_spatial, sinks, w_out, w_ada_final, b_ada_final, final_norm_g, loss_target, m_w_ada, m_b_ada, m_norm_g, m_w_in, m_ln_v_g, m_ln_v_b, m_w_spatial, m_b_spatial, m_sinks, m_w_out, m_w_ada_final, m_b_ada_final, m_final_norm_g, v_w_ada, v_b_ada, v_norm_g, v_w_in, v_ln_v_g, v_ln_v_b, v_w_spatial, v_b_spatial, v_sinks, v_w_out, v_w_ada_final, v_b_ada_final, v_final_norm_g):
    s = x.shape[1]
    ax, ay, ac = _coords()
    chip = 2 * ax + ay
    me = 4 * ax + 2 * ay + ac
    n_ada = w_ada.shape[2]
    n_adaf = w_ada_final.shape[1]

    x2d = x.reshape(s, D)
    tgt = loss_target.reshape(s, D)
    w_ada2, w_in2, w_out2 = w_ada[0], w_in[0], w_out[0]
    b_ada_f2 = b_ada_final.reshape(1, 2 * D)
    gf = final_norm_g.reshape(1, D)

    c_all, w_in_own = _allgather_sum_call(jnp.pad(c, ((0, 7), (0, 0))), "gather_c", False, cast=(w_in2, (D, D_IN)))
    c_all = c_all[::8]
    mod_p, c_act = _rowmat_call(c_all, w_ada2, lax.dynamic_slice(b_ada, (0, chip * n_ada), (1, n_ada)), "mod")
    modf_p, _ = _rowmat_call(c_all, w_ada_final, lax.dynamic_slice(b_ada_f2, (0, chip * n_adaf), (1, n_adaf)), "mod_final")
    mods, w_out_own = _allgather_sum_call(jnp.concatenate([mod_p, modf_p], axis=1), "gather_mod", False, cast=(w_out2, (D, D)))
    my_rows = [lax.dynamic_slice(mods, (16 * j + me, 0), (1, n_ada + n_adaf)) for j in range(N_CHIP)]
    mod = jnp.concatenate([r[:, :n_ada] for r in my_rows], axis=1)
    mod_f = jnp.concatenate([r[:, n_ada:] for r in my_rows], axis=1)
    shift, scale, gate = mod[:, :D], mod[:, D:2 * D], mod[:, 2 * D:]
    shift_f, scale_f = mod_f[:, :D], mod_f[:, D:]

    pos = jnp.stack([chip, ac]).astype(jnp.int32)

    tables = _rope_tables(s)
    cos, sin = tables[:2]
    b_sp_t = b_spatial[0].T
    sinks1 = sinks.reshape(N_Q)
    h, proj, w_in_bf, w_out_bf = _proj_gather_call(pos, x2d, shift, scale, norm_g, w_in_own, w_out_own)
    y, probs, attn_out, psinks, qms = _mix_fwd_call(proj, cos, sin, ln_v_g, ln_v_b, w_spatial[0], b_sp_t, sinks1)
    dx2, do, dy, st_tail = _tail_call(y, w_out_bf, x2d, tgt, gate, shift_f, scale_f, gf)

    rel_o = [(0, 1), (1, 0), (1, 1), (2, 0), (2, 1), (3, 0), (3, 1)]
    rel_a = [(1, 0), (1, 1), (2, 0), (2, 1)]
    rel_b = [(3, 0), (3, 1), (0, 1)]
    piece_a, piece_b = _w_in_piece([0, 0, 1, 1]), _w_in_piece([0, 0, 1])
    half_in, half_out = (D // 2, W_IN_SHARD), (W_OUT_SHARD // 2, D)

    g_w_out_p = _tn_call(y, do, "grad_w_out")
    grp_o = [(g_w_out_p, rel_o, _w_out_piece, half_out)]
    st_o, tok_o = _exchange_start_call(grp_o, "send_w_out")
    dproj, st_ln, d_wsp, d_bsp_t, d_sink = _mix_bwd_call(
        proj, dy, probs, attn_out, psinks, qms, tables, ln_v_g + tok_o, ln_v_b, w_spatial[0], jnp.swapaxes(w_spatial[0], 1, 2),
        b_sp_t)
    g_w_in_a = _tn_shards_call(pos, h, dproj, (1, 2), "grad_w_in_a")
    grp_a = [(g_w_in_a, rel_a, piece_a, half_in), (d_wsp, rel_o, _group_piece, (BLK, BLK))]
    st_a, tok_a = _exchange_start_call(grp_a, "send_w_in_a")
    g_w_in_b = _tn_shards_call(pos, h, dproj, (3, 0), "grad_w_in_b")
    grp_b = [(g_w_in_b, rel_b, piece_b, half_in)]
    st_b, tok_b = _exchange_start_call(grp_b, "send_w_in_b")
    grad_x, st_dh = _dh_call(dproj, w_in_bf, x2d, dx2, scale + (tok_a + tok_b), norm_g)

    ((g_w_out_p, recv_o),) = _exchange_wait_call(st_o, grp_o, st_dh, "wait_w_out")
    (_, recv_a), (d_wsp, recv_s) = _exchange_wait_call(st_a, grp_a, st_dh, "wait_w_in_a")
    ((g_w_in_b, recv_b),) = _exchange_wait_call(st_b, grp_b, st_dh, "wait_w_in_b")
    mine_in = _sum_pieces_call(pos, g_w_in_b, lambda i, p, nrb: (p[1] * nrb + i, 1), [recv_a, recv_b], "sum_w_in")
    mine_out = _sum_pieces_call(pos, g_w_out_p, lambda i, p, nrb: ((2 * p[0] + p[1]) * nrb + i, 0), [recv_o], "sum_w_out")
    wsp_group = _sum_pieces_call(pos, d_wsp.reshape(GROUPS * BLK, BLK), lambda i, p, nrb: (2 * p[0] + p[1], 0), [recv_s],
                                 "sum_w_spatial")
    to_sibling = [(0, 1)]
    grp_p = [(mine_in, to_sibling, _whole_piece, half_in), (mine_out, to_sibling, _whole_piece, half_out)]
    st_p, tok_p = _exchange_start_call(grp_p, "swap_halves")

    misc = jnp.concatenate([st_ln, d_bsp_t[:, :GROUPS].T, d_sink, jnp.zeros((8, D - D_A - 2 * LANE), F32)], axis=1)
    pack = jnp.concatenate([wsp_group.reshape(8, D) + tok_p, st_tail, st_dh, misc], axis=0)
    rows = pack.shape[0]
    packs, tot = _allgather_sum_call(pack, "gather_small", True)
    packs = packs.reshape(N_DEV, rows, D)
    dmod_all = jnp.concatenate([packs[:, 16, :], packs[:, 17, :], packs[:, 11, :]], axis=1)
    dmodf_all = jnp.concatenate([packs[:, 8, :], packs[:, 9, :]], axis=1)
    loss = tot[13, 0]
    (mine_in, theirs_in), (mine_out, theirs_out) = _exchange_wait_call(st_p, grp_p, tot, "swapped_halves")
    small = {
        "b_ada": jnp.concatenate([tot[16:17], tot[17:18], tot[11:12]], axis=1),
        "norm_g": tot[18:19],
        "ln_v_g": tot[24:25, :D_A],
        "ln_v_b": tot[25:26, :D_A],
        "w_spatial": packs[:, 0:8, :].reshape(GROUPS * BLK, BLK),
        "b_spatial": tot[24:32, D_A:D_A + BLK],
        "sinks": tot[24:25, D_A + LANE:D_A + LANE + N_Q],
        "b_ada_final": jnp.concatenate([tot[8:9], tot[9:10]], axis=1),
        "final_norm_g": tot[10:11],
    }

    weights = dict(w_ada=w_ada, b_ada=b_ada, norm_g=norm_g, w_in=w_in, ln_v_g=ln_v_g, ln_v_b=ln_v_b, w_spatial=w_spatial,
                   b_spatial=b_spatial, sinks=sinks, w_out=w_out, w_ada_final=w_ada_final, b_ada_final=b_ada_final,
                   final_norm_g=final_norm_g)
    m_in = dict(w_ada=m_w_ada, b_ada=m_b_ada, norm_g=m_norm_g, w_in=m_w_in, ln_v_g=m_ln_v_g, ln_v_b=m_ln_v_b,
                w_spatial=m_w_spatial, b_spatial=m_b_spatial, sinks=m_sinks, w_out=m_w_out, w_ada_final=m_w_ada_final,
                b_ada_final=m_b_ada_final, final_norm_g=m_final_norm_g)
    v_in = dict(w_ada=v_w_ada, b_ada=v_b_ada, norm_g=v_norm_g, w_in=v_w_in, ln_v_g=v_ln_v_g, ln_v_b=v_ln_v_b,
                w_spatial=v_w_spatial, b_spatial=v_b_spatial, sinks=v_sinks, w_out=v_w_out, w_ada_final=v_w_ada_final,
                b_ada_final=v_b_ada_final, final_norm_g=v_final_norm_g)
    c_act_t = c_act.T
    outer = {"w_ada": lax.dynamic_slice(dmod_all, (0, chip * n_ada), (N_DEV, n_ada)),
             "w_ada_final": lax.dynamic_slice(dmodf_all, (0, chip * n_adaf), (N_DEV, n_adaf))}
    halves = {"w_in": (mine_in, theirs_in[0]), "w_out": (mine_out, theirs_out[0])}
    done = {}
    for name, (mine, theirs) in halves.items():
        shape2 = (2 * mine.shape[0], mine.shape[1])
        done[name] = _adam_halves_call(pos, weights[name].reshape(shape2), mine, theirs, m_in[name].reshape(shape2),
                                       v_in[name].reshape(shape2), "adam_" + name)
    for name, dm in outer.items():
        shape2 = (D, dm.shape[1])
        done[name] = _adam_outer_call(weights[name].reshape(shape2), c_act_t, dm, m_in[name].reshape(shape2),
                                      v_in[name].reshape(shape2), "adam_" + name)
    updates = _adam_small_call([(weights[name].reshape(g.shape), g, m_in[name].reshape(g.shape), v_in[name].reshape(g.shape))
                                for name, g in small.items()])
    for (name, g), upd in zip(small.items(), updates):
        done[name] = (g, *upd)
    outs = [[done[name][k].reshape(w.shape) for name, w in weights.items()] for k in range(4)]
    return (loss, grad_x.reshape(x.shape), *outs[0], *outs[1], *outs[2], *outs[3])
```

```python
import numpy as np
import jax
import jax.numpy as jnp
from jax import lax
from jax.experimental import pallas as pl
from jax.experimental.pallas import tpu as pltpu

F32 = jnp.float32
BF16 = jnp.bfloat16
MESH = pl.DeviceIdType.MESH

D = 2048
D_A = 1024
D_B = 1024
D_KV = 256
HEAD = 64
N_Q = 16
N_KV = 4
Q_PER_KV = N_Q // N_KV
BLK = 128
GROUPS = 8
D_IN = 5632
OFF_Q, OFF_K, OFF_V, OFF_ZB = 3072, 4096, 4352, 4608
N_CHIP = 4
N_DEV = 8
W_IN_SHARD = D_IN // N_CHIP
W_OUT_SHARD = D // N_CHIP
EPS = 1e-5
SCALE = HEAD ** -0.5
NEG = -1e30
LANE = 128
VMEM_LIMIT = 56 * 1024 * 1024

ADAM_LR, ADAM_B1, ADAM_B2, ADAM_EPS, ADAM_WD, ADAM_STEP = 0.001, 0.9, 0.999, 1e-08, 0.01, 10
ADAM_C1 = 1.0 - ADAM_B1 ** ADAM_STEP
ADAM_C2 = 1.0 - ADAM_B2 ** ADAM_STEP
ADAM_ROWS = 256

NT = (((1,), (1,)), ((), ()))
TN = (((0,), (0,)), ((), ()))


def _params(*sem):
    return pltpu.CompilerParams(dimension_semantics=sem, vmem_limit_bytes=VMEM_LIMIT)


def _silu_parts(z):
    sig = 1.0 / (1.0 + jnp.exp(-z))
    return z * sig, sig


def _swap_halves(v, first_half):
    return jnp.where(first_half, pltpu.roll(v, 96, 1), pltpu.roll(v, 32, 1))


def _rope(v, cos_t, sin_s, first_half):
    return v * cos_t + _swap_halves(v, first_half) * sin_s


def _unrope(dv, cos_t, sin_s, first_half):
    return dv * cos_t - _swap_halves(dv, first_half) * sin_s


def _lane_masks():
    lane = lax.broadcasted_iota(jnp.int32, (BLK, LANE), 1)
    return (lane % HEAD) < (HEAD // 2), lane < HEAD


def _band_valid(first_block_bound, rows=BLK):
    rr = lax.broadcasted_iota(jnp.int32, (rows, 2 * BLK), 0) & (BLK - 1)
    jj = lax.broadcasted_iota(jnp.int32, (rows, 2 * BLK), 1)
    return (jj > rr) & (jj <= rr + BLK) & (jj >= first_block_bound)


def _dup_kv(slab, lo):
    rolled = pltpu.roll(slab, HEAD, 1)
    return jnp.where(lo, slab, rolled).astype(BF16), jnp.where(lo, rolled, slab).astype(BF16)


def _fold_halves(a, b, lo):
    return jnp.where(lo, a, b) + pltpu.roll(jnp.where(lo, b, a), HEAD, 1)


def _stack_heads(ref, sb, slab, lo, dtype):
    kh, base = sb // 2, 2 * (sb % 2) * BLK
    zero = jnp.zeros_like(slab)
    ref[kh, base:base + BLK, :] = jnp.where(lo, slab, zero).astype(dtype)
    ref[kh, base + BLK:base + 2 * BLK, :] = jnp.where(lo, zero, slab).astype(dtype)


def _unstack_heads(ref, sb, lo):
    kh, base = sb // 2, 2 * (sb % 2) * BLK
    return jnp.where(lo, ref[kh, base:base + BLK, :], ref[kh, base + BLK:base + 2 * BLK, :])


def _sink_column(sinks_ref, kh):
    row = lax.broadcasted_iota(jnp.int32, (Q_PER_KV * BLK, 1), 0)
    col = jnp.full(row.shape, sinks_ref[Q_PER_KV * kh + Q_PER_KV - 1], F32)
    for n in range(Q_PER_KV - 2, -1, -1):
        col = jnp.where(row < (n + 1) * BLK, sinks_ref[Q_PER_KV * kh + n], col)
    return col


def _tril():
    t = lax.broadcasted_iota(jnp.int32, (BLK, BLK), 0)
    s = lax.broadcasted_iota(jnp.int32, (BLK, BLK), 1)
    return s <= t


def _layer_norm_fwd(va, lg, lb):
    mu = jnp.mean(va, axis=-1, keepdims=True)
    xc = va - mu
    rstd = lax.rsqrt(jnp.mean(xc * xc, axis=-1, keepdims=True) + EPS)
    vhat = xc * rstd
    return vhat, rstd, vhat * lg + lb


def _softmax_sink(qm, kdup, bias, sink):
    s = lax.dot_general(qm, kdup, NT, preferred_element_type=F32) + bias
    m = jnp.maximum(jnp.max(s, axis=-1, keepdims=True), sink)
    p = jnp.exp(s - m)
    esink = jnp.exp(sink - m)
    inv = 1.0 / (jnp.sum(p, axis=-1, keepdims=True) + esink)
    return p * inv, esink * inv


def _band_bias(bias_ref):
    rows = bias_ref.shape[1]
    bias_ref[0] = jnp.where(_band_valid(BLK, rows), 0.0, NEG)
    bias_ref[1] = jnp.where(_band_valid(0, rows), 0.0, NEG)


def _rowmat_call(c_all, w, b, name):
    n = w.shape[1]
    tn = 512

    def body(c_ref, w_ref, b_ref, o_ref, ca_ref):
        ca, _ = _silu_parts(c_ref[...])
        ca_ref[...] = ca
        o_ref[...] = jnp.dot(ca.astype(BF16), w_ref[...].astype(BF16), preferred_element_type=F32) + b_ref[...]

    return pl.pallas_call(
        body, name=name, grid=(n // tn,),
        in_specs=[pl.BlockSpec((N_DEV, D), lambda j: (0, 0)), pl.BlockSpec((D, tn), lambda j: (0, j)),
                  pl.BlockSpec((1, tn), lambda j: (0, j))],
        out_specs=[pl.BlockSpec((N_DEV, tn), lambda j: (0, j)), pl.BlockSpec((N_DEV, D), lambda j: (0, 0))],
        out_shape=[jax.ShapeDtypeStruct((N_DEV, n), F32), jax.ShapeDtypeStruct((N_DEV, D), F32)],
        compiler_params=_params("arbitrary"),
    )(c_all, w, b)


W_IN_PARTS = ((0, 768), (768, 640))
OUT_STREAMS = 4
X_STREAMS = 4


def _proj_gather_call(pos, x, shift, scale, norm_g, wi_full, wo_full):
    s = x.shape[0]
    tm = min(s, 512)
    nrow = s // tm
    hi = D // 2
    ho = W_OUT_SHARD // 2
    phases = [(0, None), (1, 0), (2, 0), (1, 1), (2, 1), (3, 0), (3, 1)]

    def body(pos_ref, *refs):
        x_refs = refs[:X_STREAMS]
        (sh_ref, sc_ref, g_ref, _, _, h_ref, proj_ref, fi_ref, fo_ref,
         h_all, wbuf, obuf, send_sems, recv_sems, load_sems, out_sems) = refs[X_STREAMS:]
        p = pl.program_id(0)
        i = pl.program_id(1)
        x_, y_, c_ = _coords()
        me, sibling = (x_, y_, c_), (x_, y_, 1 - c_)

        def shard_of(q):
            px, py, _ = _peer(x_, y_, c_, q, 0)
            return 2 * px + py

        def cols_of(q, cp):
            off, w = (0, W_IN_SHARD) if cp is None else W_IN_PARTS[cp]
            return shard_of(q) * W_IN_SHARD + off, w

        def part(which, q, pc, sub, cp):
            n = hi if which == 0 else ho
            base = pc * n
            if sub is not None:
                n //= 2
                base = base + sub * n
            if which == 0:
                c0, w = cols_of(q, cp)
                return fi_ref.at[pl.ds(base, n), pl.ds(c0, w)]
            return fo_ref.at[pl.ds(shard_of(q) * W_OUT_SHARD + base, n), :]

        def copy(k, ref, to):
            return pltpu.make_async_remote_copy(src_ref=ref, dst_ref=ref, send_sem=send_sems.at[k], recv_sem=recv_sems.at[k],
                                                device_id=to, device_id_type=MESH)

        def sem(which, kind, j, cp):
            return 4 * kind + 2 * cp + j if which == 0 else 16 + 2 * kind + j

        def to_neighbour(which, q, cp=None):
            return copy(sem(which, 0, q - 1, cp), part(which, 0, c_, None, cp), _peer(x_, y_, c_, q, 0))

        def from_neighbour(which, q, cp=None):
            return copy(sem(which, 0, q - 1, cp), part(which, q, c_, None, cp), me)

        def relay(which, q, cp=None):
            return copy(sem(which, 1, q - 1, cp), part(which, q, c_, q - 1, cp), _peer(x_, y_, c_, 3 - q, 0))

        def relayed(which, sub, cp=None):
            return copy(sem(which, 1, sub, cp), part(which, 3, c_, sub, cp), me)

        def to_sibling(which, q, cp=None):
            return copy(sem(which, 2, q - 1, cp), part(which, q, c_, None, cp), sibling)

        def from_sibling(which, q, cp=None):
            return copy(sem(which, 2, q - 1, cp), part(which, q, 1 - c_, None, cp), me)

        def relayed_to_sibling(which, sub, cp=None):
            return copy(sem(which, 3, sub, cp), part(which, 3, c_, sub, cp), sibling)

        def relayed_from_sibling(which, sub, cp=None):
            return copy(sem(which, 3, sub, cp), part(which, 3, 1 - c_, sub, cp), me)

        def pass_on_neighbours(which, cp=None):
            for q in (1, 2):
                from_neighbour(which, q, cp).wait_recv()
                to_sibling(which, q, cp).start()
                relay(which, q, cp).start()

        def pass_on_relayed(which, cp=None):
            for sub in range(2):
                relayed(which, sub, cp).wait_recv()
                relayed_to_sibling(which, sub, cp).start()

        def shard_load(k):
            c0, w = cols_of(*phases[k])
            return pltpu.make_async_copy(fi_ref.at[:, pl.ds(c0, w)], wbuf.at[k % 2, :, 0:w], load_sems.at[k % 2])

        class OutCopies:
            def __init__(self, k, slot, row0):
                c0, w = cols_of(*phases[k])
                strip = tm // OUT_STREAMS
                self.copies = [pltpu.make_async_copy(obuf.at[slot, n * strip:(n + 1) * strip, 0:w],
                                                     proj_ref.at[pl.ds(row0 + n * strip, strip), pl.ds(c0, w)],
                                                     out_sems.at[slot, n]) for n in range(OUT_STREAMS)]

            def start(self):
                for cp in self.copies:
                    cp.start()

            def wait(self):
                for cp in self.copies:
                    cp.wait()

        out_copy = OutCopies

        def drain(k):
            for j in range(min(2, nrow)):
                out_copy(k, (nrow - 1 - j) % 2, 0).wait()

        def arrivals(k):
            q, cp = phases[k]
            if k == 0:
                for cp_ in range(2):
                    for q_ in (1, 2):
                        to_neighbour(0, q_, cp_).start()
            elif q < 3 and k in (1, 3):
                pass_on_neighbours(0, cp)
                if k == 1:
                    for q_ in (1, 2):
                        to_neighbour(1, q_).start()
            elif k == 5:
                for cp_ in range(2):
                    pass_on_relayed(0, cp_)
                pass_on_neighbours(1)
            if q in (1, 2):
                from_sibling(0, q, cp).wait_recv()
            elif q == 3:
                for sub in range(2):
                    relayed_from_sibling(0, sub, cp).wait_recv()

        rows = pl.ds(pl.multiple_of(i * tm, tm), tm)
        slot = i % 2
        for k, (q, cp) in enumerate(phases):
            @pl.when(p == k)
            def _(k=k, q=q, cp=cp):
                @pl.when(i == 0)
                def _():
                    if k == 0:
                        arrivals(0)
                        shard_load(0).start()
                    else:
                        drain(k - 1)
                    shard_load(k).wait()

                if k + 1 < len(phases):
                    @pl.when(i == max(nrow - 2, 0))
                    def _():
                        arrivals(k + 1)
                        shard_load(k + 1).start()

                if k == 0:
                    wx = D // X_STREAMS
                    ssq = sum(jnp.sum(xr[...] * xr[...], axis=-1, keepdims=True) for xr in x_refs)
                    r = lax.rsqrt(ssq * (1.0 / D) + EPS)
                    for n, xr in enumerate(x_refs):
                        cols = slice(n * wx, (n + 1) * wx)
                        hv = ((xr[...] * r * g_ref[:, cols]) * (1.0 + sc_ref[:, cols]) + sh_ref[:, cols]).astype(BF16)
                        h_ref[:, cols] = hv
                        h_all[rows, cols] = hv

                @pl.when(i >= 2)
                def _():
                    out_copy(k, slot, 0).wait()

                w = cols_of(q, cp)[1]
                obuf[slot, :, 0:w] = jnp.dot(h_all[rows, :], wbuf[k % 2, :, 0:w], preferred_element_type=F32)
                out_copy(k, slot, pl.multiple_of(i * tm, tm)).start()

        @pl.when((p == len(phases) - 1) & (i == nrow - 1))
        def _():
            drain(len(phases) - 1)
            pass_on_relayed(1)
            for q in (1, 2):
                from_sibling(1, q).wait_recv()
            for sub in range(2):
                relayed_from_sibling(1, sub).wait_recv()
            for which, cps in ((0, (0, 1)), (1, (None,))):
                for cp in cps:
                    for q in (1, 2):
                        to_neighbour(which, q, cp).wait_send()
                        relay(which, q, cp).wait_send()
                        to_sibling(which, q, cp).wait_send()
                        relayed_to_sibling(which, q - 1, cp).wait_send()

    vec = pl.BlockSpec((1, D), lambda p, i, pos: (0, 0))
    first_phase_rows = lambda p, i, pos: (jnp.where(p == 0, i, nrow - 1), 0)
    anyspec = pl.BlockSpec(memory_space=pl.ANY)
    x_spec = lambda n: pl.BlockSpec((tm, D // X_STREAMS), lambda p, i, pos: (jnp.where(p == 0, i, nrow - 1), n))
    return pl.pallas_call(
        body, name="proj_gather",
        grid_spec=pltpu.PrefetchScalarGridSpec(
            num_scalar_prefetch=1, grid=(len(phases), nrow),
            in_specs=[x_spec(n) for n in range(X_STREAMS)] + [vec, vec, vec, anyspec, anyspec],
            out_specs=[pl.BlockSpec((tm, D), first_phase_rows), anyspec, anyspec, anyspec],
            scratch_shapes=[pltpu.VMEM((s, D), BF16), pltpu.VMEM((2, D, W_IN_SHARD), BF16), pltpu.VMEM((2, tm, W_IN_SHARD), F32),
                            pltpu.SemaphoreType.DMA((24,)), pltpu.SemaphoreType.DMA((24,)), pltpu.SemaphoreType.DMA((2,)),
                            pltpu.SemaphoreType.DMA((2, OUT_STREAMS))]),
        out_shape=[jax.ShapeDtypeStruct((s, D), BF16), jax.ShapeDtypeStruct((s, D_IN), F32),
                   jax.ShapeDtypeStruct((D, D_IN), BF16), jax.ShapeDtypeStruct((D, D), BF16)],
        input_output_aliases={X_STREAMS + 4: 2, X_STREAMS + 5: 3},
        compiler_params=_params("arbitrary", "arbitrary"),
    )(pos, *([x] * X_STREAMS), shift, scale, norm_g, wi_full, wo_full)


def _proj_specs(rev_nb=None, with_q=True):
    if rev_nb is None:
        row = lambda i: i
    else:
        row = lambda i: rev_nb - 1 - i
    wide = lambda col: pl.BlockSpec((BLK, D_A), lambda i: (row(i), col))
    kv = lambda col: pl.BlockSpec((BLK, D_KV), lambda i: (row(i), col))
    half = lambda col: pl.BlockSpec((BLK, 512), lambda i: (row(i), col))
    return ([wide(0), wide(1), wide(2)] + ([wide(3)] if with_q else [])
            + [kv(OFF_K // D_KV), kv(OFF_V // D_KV), half(OFF_ZB // 512), half(OFF_ZB // 512 + 1)])


def _mix_fwd_call(proj, cos, sin, ln_g, ln_b, w_sp, b_sp_t, sinks):
    s = proj.shape[0]
    nb = s // BLK

    def body(ua_ref, va_ref, za_ref, q_ref, k_ref, v_ref, zb0_ref, zb1_ref, cos_ref, sin_ref, lg_ref, lb_ref,
             w_ref, bt_ref, sinks_ref, y_ref, probs_ref, ost_ref, psink_ref, qm_ref, kdup_ref, vdup_ref, bias_ref):
        i = pl.program_id(0)
        first_half, lo = _lane_masks()
        cos_t = cos_ref[...]
        sin_t = sin_ref[...]

        _, _, vln = _layer_norm_fwd(va_ref[...], lg_ref[...], lb_ref[...])
        tril = _tril()
        for g in range(GROUPS):
            cols = slice(g * BLK, (g + 1) * BLK)
            wg = jnp.where(tril, w_ref[g], 0.0).astype(BF16)
            sg = jnp.dot(wg, vln[:, cols].astype(BF16), preferred_element_type=F32) + bt_ref[:, g:g + 1]
            gate, _ = _silu_parts(za_ref[:, cols])
            y_ref[:, cols] = (ua_ref[:, cols] * sg * gate).astype(BF16)

        @pl.when(i == 0)
        def _():
            kdup_ref[:, 0:BLK, :] = jnp.zeros((N_KV, BLK, LANE), BF16)
            vdup_ref[:, 0:BLK, :] = jnp.zeros((N_KV, BLK, LANE), BF16)
            _band_bias(bias_ref)

        @pl.when(i > 0)
        def _():
            kdup_ref[:, 0:BLK, :] = kdup_ref[:, BLK:2 * BLK, :]
            vdup_ref[:, 0:BLK, :] = vdup_ref[:, BLK:2 * BLK, :]

        for ks in range(2):
            cols = slice(ks * LANE, (ks + 1) * LANE)
            kr = _rope(k_ref[:, cols], cos_t, sin_t, first_half)
            for n, (kd, vd) in enumerate(zip(_dup_kv(kr, lo), _dup_kv(v_ref[:, cols], lo))):
                kdup_ref[2 * ks + n, BLK:2 * BLK, :] = kd
                vdup_ref[2 * ks + n, BLK:2 * BLK, :] = vd
        for sb in range(8):
            _stack_heads(qm_ref, sb, _rope(q_ref[:, sb * LANE:(sb + 1) * LANE], cos_t, sin_t, first_half) * SCALE, lo, BF16)

        block_kind = jnp.where(i > 0, 1, 0)

        psink_ref[...] = jnp.zeros((Q_PER_KV * BLK, LANE), F32)
        lane_q = lax.broadcasted_iota(jnp.int32, (Q_PER_KV * BLK, LANE), 1)

        def kv_head(kh, carry):
            probs, psink = _softmax_sink(qm_ref[kh], kdup_ref[kh], bias_ref[block_kind], _sink_column(sinks_ref, kh))
            probs_ref[kh] = probs
            psink_ref[...] = jnp.where(lane_q == kh, psink, psink_ref[...])
            ost_ref[kh] = jnp.dot(probs.astype(BF16), vdup_ref[kh], preferred_element_type=F32)
            return carry

        lax.fori_loop(0, N_KV, kv_head, 0, unroll=2)
        for sb in range(8):
            cols = slice(sb * LANE, (sb + 1) * LANE)
            zb = zb0_ref[:, cols] if sb < 4 else zb1_ref[:, (sb - 4) * LANE:(sb - 3) * LANE]
            gate, _ = _silu_parts(zb)
            y_ref[:, D_A + sb * LANE:D_A + (sb + 1) * LANE] = (_unstack_heads(ost_ref, sb, lo) * gate).astype(BF16)

    tab = pl.BlockSpec((BLK, LANE), lambda i: (i, 0))
    return pl.pallas_call(
        body, name="mix_fwd", grid=(nb,),
        in_specs=_proj_specs() + [
            tab, tab, pl.BlockSpec((1, D_A), lambda i: (0, 0)), pl.BlockSpec((1, D_A), lambda i: (0, 0)),
            pl.BlockSpec((GROUPS, BLK, BLK), lambda i: (0, 0, 0)), pl.BlockSpec((BLK, GROUPS), lambda i: (0, 0)),
            pl.BlockSpec(memory_space=pltpu.SMEM)],
        out_specs=[pl.BlockSpec((BLK, 2 * D_A), lambda i: (i, 0)),
                   pl.BlockSpec((None, N_KV, Q_PER_KV * BLK, 2 * BLK), lambda i: (i, 0, 0, 0)),
                   pl.BlockSpec((None, N_KV, Q_PER_KV * BLK, LANE), lambda i: (i, 0, 0, 0)),
                   pl.BlockSpec((None, Q_PER_KV * BLK, LANE), lambda i: (i, 0, 0)),
                   pl.BlockSpec((None, N_KV, Q_PER_KV * BLK, LANE), lambda i: (i, 0, 0, 0))],
        out_shape=[jax.ShapeDtypeStruct((s, 2 * D_A), BF16), jax.ShapeDtypeStruct((nb, N_KV, Q_PER_KV * BLK, 2 * BLK), F32),
                   jax.ShapeDtypeStruct((nb, N_KV, Q_PER_KV * BLK, LANE), F32), jax.ShapeDtypeStruct((nb, Q_PER_KV * BLK, LANE), F32),
                   jax.ShapeDtypeStruct((nb, N_KV, Q_PER_KV * BLK, LANE), BF16)],
        scratch_shapes=[pltpu.VMEM((N_KV, 2 * BLK, LANE), BF16), pltpu.VMEM((N_KV, 2 * BLK, LANE), BF16),
                        pltpu.VMEM((2, Q_PER_KV * BLK, 2 * BLK), F32)],
        compiler_params=_params("arbitrary"),
    )(proj, proj, proj, proj, proj, proj, proj, proj, cos, sin, ln_g, ln_b, w_sp, b_sp_t, sinks)


def _tail_call(y, w_out_bf, x, target, gate, shift_f, scale_f, gf):
    s = x.shape[0]
    tm = min(s, 256)
    nsteps = s // tm

    def body(y_ref, w_ref, x_ref, t_ref, gate_ref, shf_ref, scf_ref, gf_ref, dx2_ref, do_ref, dy_ref, st_ref):
        i = pl.program_id(0)

        @pl.when(i == 0)
        def _():
            st_ref[...] = jnp.zeros((8, D), F32)

        o = jnp.dot(y_ref[...], w_ref[...], preferred_element_type=F32)
        gate_v = gate_ref[...]
        x2 = x_ref[...] + gate_v * o
        r2 = lax.rsqrt(jnp.mean(x2 * x2, axis=-1, keepdims=True) + EPS)
        xn2 = x2 * r2
        hn2 = xn2 * gf_ref[...]
        one_sc = 1.0 + scf_ref[...]
        err = hn2 * one_sc + shf_ref[...] - t_ref[...]
        dout = err * (1.0 / D)
        dhn2 = dout * one_sc
        dxn2 = dhn2 * gf_ref[...]
        dx2 = r2 * (dxn2 - xn2 * jnp.mean(dxn2 * xn2, axis=-1, keepdims=True))
        dx2_ref[...] = dx2
        do = (dx2 * gate_v).astype(BF16)
        do_ref[...] = do
        dy_ref[...] = lax.dot_general(do, w_ref[...], NT, preferred_element_type=F32)
        st_ref[0:1, :] += jnp.sum(dout, axis=0, keepdims=True)
        st_ref[1:2, :] += jnp.sum(dout * hn2, axis=0, keepdims=True)
        st_ref[2:3, :] += jnp.sum(dhn2 * xn2, axis=0, keepdims=True)
        st_ref[3:4, :] += jnp.sum(dx2 * o, axis=0, keepdims=True)
        st_ref[4:5, :] += jnp.sum(err * err, axis=0, keepdims=True)

        @pl.when(i == nsteps - 1)
        def _():
            st_ref[5:6, :] = jnp.full((1, D), 0.5 / D, F32) * jnp.sum(st_ref[4:5, :])

    vec = pl.BlockSpec((1, D), lambda i: (0, 0))
    rows = lambda: pl.BlockSpec((tm, D), lambda i: (i, 0))
    return pl.pallas_call(
        body, name="tail", grid=(nsteps,),
        in_specs=[rows(), pl.BlockSpec((D, D), lambda i: (0, 0)), rows(), rows(), vec, vec, vec, vec],
        out_specs=[rows(), rows(), rows(), pl.BlockSpec((8, D), lambda i: (0, 0))],
        out_shape=[jax.ShapeDtypeStruct((s, D), F32), jax.ShapeDtypeStruct((s, D), BF16), jax.ShapeDtypeStruct((s, D), F32),
                   jax.ShapeDtypeStruct((8, D), F32)],
        compiler_params=_params("arbitrary"),
    )(y, w_out_bf, x, target, gate, shift_f, scale_f, gf)


def _tn_call(a, b, name):
    s, m = a.shape
    n = b.shape[1]
    tn = 1024
    ts = min(s, 1024)
    nk = s // ts

    def body(a_ref, b_ref, o_ref, acc_ref):
        k = pl.program_id(1)

        @pl.when(k == 0)
        def _():
            acc_ref[...] = jnp.zeros((m, tn), F32)

        acc_ref[...] += lax.dot_general(a_ref[...], b_ref[...], TN, preferred_element_type=F32)

        @pl.when(k == nk - 1)
        def _():
            o_ref[...] = acc_ref[...].astype(BF16)

    return pl.pallas_call(
        body, name=name, grid=(n // tn, nk),
        in_specs=[pl.BlockSpec((ts, m), lambda j, k: (k, 0)), pl.BlockSpec((ts, tn), lambda j, k: (k, j))],
        out_specs=pl.BlockSpec((m, tn), lambda j, k: (0, j)),
        out_shape=jax.ShapeDtypeStruct((m, n), BF16),
        scratch_shapes=[pltpu.VMEM((m, tn), F32)],
        compiler_params=_params("parallel", "arbitrary"),
    )(a, b)


def _tn_shards_call(pos, a, b, qs, name):
    s, m = a.shape
    ts = min(s, 1024)
    nk = s // ts

    def body(pos_ref, a_ref, b_ref, o_ref, acc_ref):
        k = pl.program_id(1)

        @pl.when(k == 0)
        def _():
            acc_ref[...] = jnp.zeros((m, W_IN_SHARD), F32)

        acc_ref[...] += lax.dot_general(a_ref[...], b_ref[...], TN, preferred_element_type=F32)

        @pl.when(k == nk - 1)
        def _():
            o_ref[...] = acc_ref[...].astype(BF16)

    def shard(j, pos):
        q = qs[0]
        for n in range(1, len(qs)):
            q = jnp.where(j == n, qs[n], q)
        return jnp.bitwise_xor(pos[0], q)

    return pl.pallas_call(
        body, name=name,
        grid_spec=pltpu.PrefetchScalarGridSpec(
            num_scalar_prefetch=1, grid=(len(qs), nk),
            in_specs=[pl.BlockSpec((ts, m), lambda j, k, pos: (k, 0)),
                      pl.BlockSpec((ts, W_IN_SHARD), lambda j, k, pos: (k, shard(j, pos)))],
            out_specs=pl.BlockSpec((m, W_IN_SHARD), lambda j, k, pos: (0, j)),
            scratch_shapes=[pltpu.VMEM((m, W_IN_SHARD), F32)]),
        out_shape=jax.ShapeDtypeStruct((m, len(qs) * W_IN_SHARD), BF16),
        compiler_params=_params("parallel", "arbitrary"),
    )(pos, a, b)


def _mix_bwd_call(proj, dy, probs, outs, psinks, qms, tables, ln_g, ln_b, w_sp, w_sp_t, b_sp_t):
    s = proj.shape[0]
    nb = s // BLK
    rev = lambda i: nb - 1 - i
    prev = lambda i: jnp.maximum(nb - 2 - i, 0)

    def body(ua_ref, va_ref, za_ref, k_ref, v_ref, zb0_ref, zb1_ref, kp_ref, vp_ref, dy_ref,
             probs_ref, ost_ref, psink_ref, qm_ref, cos_ref, sin_ref, cosp_ref, sinp_ref, lg_ref, lb_ref, w_ref, wt_ref, bt_ref,
             dp_ref, lnst_ref, dw_ref, dbt_ref, dsink_ref,
             kdup_ref, vdup_ref, dvln_ref, dom_ref, dqst_ref, dkdup_ref, dvdup_ref, kcar_ref, vcar_ref, sigb_ref):
        i = pl.program_id(0)
        first_half, lo = _lane_masks()
        lane8 = lax.broadcasted_iota(jnp.int32, (8, LANE), 1)
        cos_t = cos_ref[...]
        sin_t = sin_ref[...]

        @pl.when(i == 0)
        def _():
            lnst_ref[...] = jnp.zeros((8, D_A), F32)
            dw_ref[...] = jnp.zeros((GROUPS, BLK, BLK), F32)
            dbt_ref[...] = jnp.zeros((BLK, LANE), F32)
            dsink_ref[...] = jnp.zeros((8, LANE), F32)
            kcar_ref[...] = jnp.zeros((BLK, D_KV), F32)
            vcar_ref[...] = jnp.zeros((BLK, D_KV), F32)

        vhat, rstd, vln = _layer_norm_fwd(va_ref[...], lg_ref[...], lb_ref[...])
        tril = _tril()
        triu = jnp.logical_not(tril) | (lax.broadcasted_iota(jnp.int32, (BLK, BLK), 0) == lax.broadcasted_iota(jnp.int32, (BLK, BLK), 1))
        lane_b = lax.broadcasted_iota(jnp.int32, (BLK, LANE), 1)
        db_acc = jnp.zeros((BLK, LANE), F32)
        for g in range(GROUPS):
            cols = slice(g * BLK, (g + 1) * BLK)
            vln_g = vln[:, cols].astype(BF16)
            wg = jnp.where(tril, w_ref[g], 0.0).astype(BF16)
            sg = jnp.dot(wg, vln_g, preferred_element_type=F32) + bt_ref[:, g:g + 1]
            za = za_ref[:, cols]
            gate, sig = _silu_parts(za)
            ua = ua_ref[:, cols]
            dya_g = dy_ref[:, cols]
            dya = dya_g * gate
            dp_ref[:, cols] = (dya * sg).astype(BF16)
            dp_ref[:, 2 * D_A + g * BLK:2 * D_A + (g + 1) * BLK] = (
                dya_g * (ua * sg) * (sig * (1.0 + za * (1.0 - sig)))).astype(BF16)
            ds = dya * ua
            ds_b = ds.astype(BF16)
            wtg = jnp.where(triu, wt_ref[g], 0.0).astype(BF16)
            dvln_ref[:, cols] = jnp.dot(wtg, ds_b, preferred_element_type=F32)
            dw_ref[g] += jnp.where(tril, lax.dot_general(ds_b, vln_g, NT, preferred_element_type=F32), 0.0)
            db_acc = db_acc + jnp.where(lane_b == g, jnp.sum(ds, axis=-1, keepdims=True), 0.0)
        dbt_ref[...] += db_acc
        dvln = dvln_ref[...]
        lnst_ref[0:1, :] += jnp.sum(dvln * vhat, axis=0, keepdims=True)
        lnst_ref[1:2, :] += jnp.sum(dvln, axis=0, keepdims=True)
        dvhat = dvln * lg_ref[...]
        m1 = jnp.mean(dvhat, axis=-1, keepdims=True)
        m2 = jnp.mean(dvhat * vhat, axis=-1, keepdims=True)
        dp_ref[:, D_A:2 * D_A] = (rstd * (dvhat - m1 - vhat * m2)).astype(BF16)

        cosp = cosp_ref[...]
        sinp = sinp_ref[...]
        for ks in range(2):
            cols = slice(ks * LANE, (ks + 1) * LANE)
            kr = _rope(k_ref[:, cols], cos_t, sin_t, first_half)
            kpr = _rope(kp_ref[:, cols], cosp, sinp, first_half)
            for n, (kc, vc, kp, vp) in enumerate(zip(_dup_kv(kr, lo), _dup_kv(v_ref[:, cols], lo),
                                                     _dup_kv(kpr, lo), _dup_kv(vp_ref[:, cols], lo))):
                kdup_ref[2 * ks + n, BLK:2 * BLK, :] = kc
                vdup_ref[2 * ks + n, BLK:2 * BLK, :] = vc
                kdup_ref[2 * ks + n, 0:BLK, :] = kp
                vdup_ref[2 * ks + n, 0:BLK, :] = vp
        for sb in range(8):
            cols = slice(sb * LANE, (sb + 1) * LANE)
            zb = zb0_ref[:, cols] if sb < 4 else zb1_ref[:, (sb - 4) * LANE:(sb - 3) * LANE]
            gate, sig = _silu_parts(zb)
            sigb_ref[:, cols] = sig
            _stack_heads(dom_ref, sb, dy_ref[:, D_A + sb * LANE:D_A + (sb + 1) * LANE] * gate, lo, F32)

        lane_q = lax.broadcasted_iota(jnp.int32, (Q_PER_KV * BLK, LANE), 1)

        def kv_head(kh, sink_acc):
            qm = qm_ref[kh]
            kd = kdup_ref[kh]
            vd = vdup_ref[kh]
            probs = probs_ref[kh]
            probs_b = probs.astype(BF16)
            o = ost_ref[kh]
            dom = dom_ref[kh]
            dom_b = dom.astype(BF16)
            delta = jnp.sum(dom * o, axis=-1, keepdims=True)
            dpr = lax.dot_general(dom_b, vd, NT, preferred_element_type=F32)
            dss = (probs * (dpr - delta)).astype(BF16)
            sink_acc = sink_acc + jnp.where(lane_q == kh, psink_ref[...] * delta, 0.0)
            dqst_ref[kh] = jnp.dot(dss, kd, preferred_element_type=F32)
            dkdup_ref[kh] = lax.dot_general(dss, qm, TN, preferred_element_type=F32)
            dvdup_ref[kh] = lax.dot_general(probs_b, dom_b, TN, preferred_element_type=F32)
            return sink_acc

        sink_acc = jnp.zeros((Q_PER_KV * BLK, LANE), F32)
        for kh in range(N_KV):
            sink_acc = kv_head(kh, sink_acc)
        lane1 = lax.broadcasted_iota(jnp.int32, (1, LANE), 1)
        dsink_acc = jnp.zeros((8, LANE), F32)
        for n in range(Q_PER_KV):
            col = jnp.sum(sink_acc[n * BLK:(n + 1) * BLK], axis=0, keepdims=True)
            for kh in range(N_KV):
                dsink_acc = dsink_acc + jnp.where(lane8 == Q_PER_KV * kh + n, -jnp.sum(jnp.where(lane1 == kh, col, 0.0)), 0.0)
        row0 = lax.broadcasted_iota(jnp.int32, (8, LANE), 0) == 0
        dsink_ref[...] += jnp.where(row0, dsink_acc, 0.0)

        for sb in range(8):
            cols = slice(sb * LANE, (sb + 1) * LANE)
            zb = zb0_ref[:, cols] if sb < 4 else zb1_ref[:, (sb - 4) * LANE:(sb - 3) * LANE]
            sig = sigb_ref[:, cols]
            dyb = dy_ref[:, D_A + sb * LANE:D_A + (sb + 1) * LANE]
            dp_ref[:, OFF_ZB + sb * LANE:OFF_ZB + (sb + 1) * LANE] = (
                dyb * _unstack_heads(ost_ref, sb, lo) * (sig * (1.0 + zb * (1.0 - sig)))).astype(BF16)
            dq_r = _unstack_heads(dqst_ref, sb, lo) * SCALE
            dp_ref[:, OFF_Q + sb * LANE:OFF_Q + (sb + 1) * LANE] = _unrope(dq_r, cos_t, sin_t, first_half).astype(BF16)

        lo2 = lax.broadcasted_iota(jnp.int32, (2 * BLK, LANE), 1) < HEAD
        for ks in range(2):
            cols = slice(ks * LANE, (ks + 1) * LANE)
            dk_band = _fold_halves(dkdup_ref[2 * ks], dkdup_ref[2 * ks + 1], lo2)
            dv_band = _fold_halves(dvdup_ref[2 * ks], dvdup_ref[2 * ks + 1], lo2)
            dkr = dk_band[BLK:2 * BLK, :] + kcar_ref[:, cols]
            dp_ref[:, OFF_K + ks * LANE:OFF_K + (ks + 1) * LANE] = _unrope(dkr, cos_t, sin_t, first_half).astype(BF16)
            dp_ref[:, OFF_V + ks * LANE:OFF_V + (ks + 1) * LANE] = (
                dv_band[BLK:2 * BLK, :] + vcar_ref[:, cols]).astype(BF16)
            kcar_ref[:, cols] = dk_band[0:BLK, :]
            vcar_ref[:, cols] = dv_band[0:BLK, :]

    tab = pl.BlockSpec((BLK, LANE), lambda i: (rev(i), 0))
    kvp = lambda col: pl.BlockSpec((BLK, D_KV), lambda i: (prev(i), col))
    vec = pl.BlockSpec((1, D_A), lambda i: (0, 0))
    w3 = pl.BlockSpec((GROUPS, BLK, BLK), lambda i: (0, 0, 0))
    return pl.pallas_call(
        body, name="mix_bwd", grid=(nb,),
        in_specs=_proj_specs(nb, with_q=False) + [
            kvp(OFF_K // D_KV), kvp(OFF_V // D_KV), pl.BlockSpec((BLK, 2 * D_A), lambda i: (rev(i), 0)),
            pl.BlockSpec((None, N_KV, Q_PER_KV * BLK, 2 * BLK), lambda i: (rev(i), 0, 0, 0)),
            pl.BlockSpec((None, N_KV, Q_PER_KV * BLK, LANE), lambda i: (rev(i), 0, 0, 0)),
            pl.BlockSpec((None, Q_PER_KV * BLK, LANE), lambda i: (rev(i), 0, 0)),
            pl.BlockSpec((None, N_KV, Q_PER_KV * BLK, LANE), lambda i: (rev(i), 0, 0, 0)),
            tab, tab, tab, tab, vec, vec, w3, w3, pl.BlockSpec((BLK, GROUPS), lambda i: (0, 0))],
        out_specs=[pl.BlockSpec((BLK, D_IN), lambda i: (rev(i), 0)), pl.BlockSpec((8, D_A), lambda i: (0, 0)), w3,
                   pl.BlockSpec((BLK, LANE), lambda i: (0, 0)), pl.BlockSpec((8, LANE), lambda i: (0, 0))],
        out_shape=[jax.ShapeDtypeStruct((s, D_IN), BF16), jax.ShapeDtypeStruct((8, D_A), F32),
                   jax.ShapeDtypeStruct((GROUPS, BLK, BLK), F32), jax.ShapeDtypeStruct((BLK, LANE), F32),
                   jax.ShapeDtypeStruct((8, LANE), F32)],
        scratch_shapes=[pltpu.VMEM((N_KV, 2 * BLK, LANE), BF16), pltpu.VMEM((N_KV, 2 * BLK, LANE), BF16),
                        pltpu.VMEM((BLK, D_A), F32),
                        pltpu.VMEM((N_KV, Q_PER_KV * BLK, LANE), F32), pltpu.VMEM((N_KV, Q_PER_KV * BLK, LANE), F32),
                        pltpu.VMEM((N_KV, 2 * BLK, LANE), F32), pltpu.VMEM((N_KV, 2 * BLK, LANE), F32),
                        pltpu.VMEM((BLK, D_KV), F32), pltpu.VMEM((BLK, D_KV), F32), pltpu.VMEM((BLK, D_B), F32)],
        compiler_params=_params("arbitrary"),
    )(proj, proj, proj, proj, proj, proj, proj, proj, proj, dy, probs, outs, psinks, qms, *tables, ln_g, ln_b,
      w_sp, w_sp_t, b_sp_t)


def _dh_call(dproj, w_bf, x, dx2, scale, norm_g):
    s = x.shape[0]
    tm = min(s, 512)
    tk = W_IN_SHARD
    nk = D_IN // tk

    def body(dp_ref, w_ref, x_ref, dx2_ref, sc_ref, g_ref, gx_ref, st_ref, acc_ref):
        i = pl.program_id(0)
        k = pl.program_id(1)

        @pl.when((i == 0) & (k == 0))
        def _():
            st_ref[...] = jnp.zeros((8, D), F32)

        @pl.when(k == 0)
        def _():
            acc_ref[...] = jnp.zeros((tm, D), F32)

        acc_ref[...] += lax.dot_general(dp_ref[...], w_ref[...], NT, preferred_element_type=F32)

        @pl.when(k == nk - 1)
        def _():
            g = g_ref[...]
            one_sc = 1.0 + sc_ref[...]

            def chunk(n, carry):
                rows = pl.ds(pl.multiple_of(n * BLK, BLK), BLK)
                dh = acc_ref[rows, :]
                xv = x_ref[rows, :]
                r = lax.rsqrt(jnp.mean(xv * xv, axis=-1, keepdims=True) + EPS)
                xn = xv * r
                dhn = dh * one_sc
                dxn = dhn * g
                gx_ref[rows, :] = dx2_ref[rows, :] + r * (dxn - xn * jnp.mean(dxn * xn, axis=-1, keepdims=True))
                st_ref[0:1, :] += jnp.sum(dh, axis=0, keepdims=True)
                st_ref[1:2, :] += jnp.sum(dh * (xn * g), axis=0, keepdims=True)
                st_ref[2:3, :] += jnp.sum(dhn * xn, axis=0, keepdims=True)
                return carry

            lax.fori_loop(0, tm // BLK, chunk, 0)

    vec = pl.BlockSpec((1, D), lambda i, k: (0, 0))
    rows = lambda: pl.BlockSpec((tm, D), lambda i, k: (i, 0))
    return pl.pallas_call(
        body, name="dh", grid=(s // tm, nk),
        in_specs=[pl.BlockSpec((tm, tk), lambda i, k: (i, k)), pl.BlockSpec((D, tk), lambda i, k: (0, k)), rows(), rows(), vec, vec],
        out_specs=[rows(), pl.BlockSpec((8, D), lambda i, k: (0, 0))],
        out_shape=[jax.ShapeDtypeStruct((s, D), F32), jax.ShapeDtypeStruct((8, D), F32)],
        scratch_shapes=[pltpu.VMEM((tm, D), F32)],
        compiler_params=_params("arbitrary", "arbitrary"),
    )(dproj, w_bf, x, dx2, scale, norm_g)


def _adam_math(w, g, m, v):
    m_new = ADAM_B1 * m + (1.0 - ADAM_B1) * g
    v_new = ADAM_B2 * v + (1.0 - ADAM_B2) * (g * g)
    m_hat = m_new / ADAM_C1
    v_hat = v_new / ADAM_C2
    delta = -ADAM_LR * (m_hat / (jnp.sqrt(v_hat) + ADAM_EPS) + ADAM_WD * w)
    return delta, m_new, v_new


def _adam_small_call(tensors):
    n = len(tensors)

    def body(*refs):
        ins, outs = refs[:4 * n], refs[4 * n:]
        for t in range(n):
            w_ref, g_ref, m_ref, v_ref = ins[4 * t:4 * t + 4]
            d, mo, vo = _adam_math(w_ref[...], g_ref[...], m_ref[...], v_ref[...])
            outs[3 * t][...], outs[3 * t + 1][...], outs[3 * t + 2][...] = d, mo, vo

    vm = pl.BlockSpec(memory_space=pltpu.VMEM)
    flat = [a for t in tensors for a in t]
    out = pl.pallas_call(
        body, name="adam_small", in_specs=[vm] * (4 * n), out_specs=[vm] * (3 * n),
        out_shape=[jax.ShapeDtypeStruct(t[0].shape, F32) for t in tensors for _ in range(3)],
        compiler_params=pltpu.CompilerParams(vmem_limit_bytes=VMEM_LIMIT),
    )(*flat)
    return [tuple(out[3 * t:3 * t + 3]) for t in range(n)]


def _adam_halves_call(pos, w, mine, theirs, m, v, name):
    r, n = w.shape
    half = r // 2
    tr = ADAM_ROWS
    nh = half // tr

    def body(pos_ref, w_ref, mine_ref, theirs_ref, m_ref, v_ref, g_ref, d_ref, mo_ref, vo_ref):
        is_mine = (pl.program_id(0) // nh) == pos_ref[1]
        g = jnp.where(is_mine, mine_ref[...], theirs_ref[...])
        g_ref[...] = g
        d_ref[...], mo_ref[...], vo_ref[...] = _adam_math(w_ref[...], g, m_ref[...], v_ref[...])

    spec = lambda: pl.BlockSpec((tr, n), lambda i, pos: (i, 0))

    def half_spec(core_of_half):
        def index(i, pos):
            first = core_of_half(pos) == 0
            active = (i // nh == 0) == first
            return jnp.where(active, i % nh, jnp.where(first, nh - 1, 0)), 0
        return pl.BlockSpec((tr, n), index)

    return pl.pallas_call(
        body, name=name,
        grid_spec=pltpu.PrefetchScalarGridSpec(
            num_scalar_prefetch=1, grid=(r // tr,),
            in_specs=[spec(), half_spec(lambda pos: pos[1]), half_spec(lambda pos: 1 - pos[1]), spec(), spec()],
            out_specs=[spec() for _ in range(4)]),
        out_shape=[jax.ShapeDtypeStruct((r, n), F32)] * 4, compiler_params=_params("arbitrary"),
    )(pos, w, mine, theirs, m, v)


def _adam_outer_call(w, ct, dm, m, v, name):
    r, n = w.shape
    tr = ADAM_ROWS

    def body(w_ref, ct_ref, dm_ref, m_ref, v_ref, g_ref, d_ref, mo_ref, vo_ref):
        g = ct_ref[:, 0:1] * dm_ref[0:1, :]
        for b in range(1, N_DEV):
            g = g + ct_ref[:, b:b + 1] * dm_ref[b:b + 1, :]
        g_ref[...] = g
        d_ref[...], mo_ref[...], vo_ref[...] = _adam_math(w_ref[...], g, m_ref[...], v_ref[...])

    spec = lambda: pl.BlockSpec((tr, n), lambda i: (i, 0))
    return pl.pallas_call(
        body, name=name, grid=(r // tr,),
        in_specs=[spec(), pl.BlockSpec((tr, N_DEV), lambda i: (i, 0)), pl.BlockSpec((N_DEV, n), lambda i: (0, 0)), spec(), spec()],
        out_specs=[spec() for _ in range(4)],
        out_shape=[jax.ShapeDtypeStruct((r, n), F32)] * 4, compiler_params=_params("parallel"),
    )(w, ct, dm, m, v)


def _sum_pieces_call(pos, part, part_block, recvs, name):
    r, n = recvs[0].shape[1:]
    tr = min(r, 256)
    nrb = r // tr

    def body(pos_ref, p_ref, *refs):
        acc = p_ref[...].astype(F32)
        for r_ref in refs[:-1]:
            for d in range(r_ref.shape[0]):
                acc = acc + r_ref[d].astype(F32)
        refs[-1][...] = acc

    return pl.pallas_call(
        body, name=name,
        grid_spec=pltpu.PrefetchScalarGridSpec(
            num_scalar_prefetch=1, grid=(nrb,),
            in_specs=[pl.BlockSpec((tr, n), lambda i, pos: part_block(i, pos, nrb))] + [
                pl.BlockSpec((rv.shape[0], tr, n), lambda i, pos: (0, i, 0)) for rv in recvs],
            out_specs=pl.BlockSpec((tr, n), lambda i, pos: (i, 0))),
        out_shape=jax.ShapeDtypeStruct((r, n), F32), compiler_params=_params("parallel"),
    )(pos, part, *recvs)


def _coords():
    return lax.axis_index("x"), lax.axis_index("y"), lax.axis_index("c")


CAST_ROWS = 256


def _allgather_sum_call(blk, name, with_sum, cast=None):
    m_per, n = blk.shape
    n_out = 2 if with_sum else 1
    if cast is not None:
        w, full_shape = cast
        wr, wn = w.shape
        by_cols = full_shape[0] == wr
        tr = min(wr, CAST_ROWS)
        n_chunk = wr // tr

    def body(*refs):
        x_ref = refs[0]
        out_ref = refs[1 + (cast is not None)]
        rest = refs[1 + (cast is not None) + n_out + (cast is not None):]
        send_sems, recv_sems, local_sem = rest[:3]
        x, y, c = _coords()
        me, sibling = (x, y, c), (x, y, 1 - c)
        chips = [(1 - x, y), (x, 1 - y), (1 - x, 1 - y)]

        def rows(px, py, pc):
            return out_ref.at[pl.ds((4 * px + 2 * py + pc) * m_per, m_per), :]

        def copy(k, block, to, src=None):
            return pltpu.make_async_remote_copy(
                src_ref=rows(*block) if src is None else src, dst_ref=rows(*block),
                send_sem=send_sems.at[k], recv_sem=recv_sems.at[k], device_id=to, device_id_type=MESH)

        mine = pltpu.make_async_copy(x_ref, rows(*me), local_sem)
        mine.start()
        first = [copy(0, me, sibling, src=x_ref)]
        first += [copy(1 + j, me, (*chip, c), src=x_ref) for j, chip in enumerate(chips)]
        for cp in first:
            cp.start()

        if cast is not None:
            w_ref, full_ref = refs[1], refs[1 + 1 + n_out]
            f32_buf, bf16_buf, in_sems, out_sems = rest[3:]
            chip_no = 2 * x + y

            def fetch(i):
                return pltpu.make_async_copy(w_ref.at[pl.ds(i * tr, tr), :], f32_buf.at[i % 2], in_sems.at[i % 2])

            def store(i):
                if by_cols:
                    dst = full_ref.at[pl.ds(i * tr, tr), pl.ds(chip_no * wn, wn)]
                else:
                    dst = full_ref.at[pl.ds(chip_no * wr + i * tr, tr), :]
                return pltpu.make_async_copy(bf16_buf.at[i % 2], dst, out_sems.at[i % 2])

            fetch(0).start()
            for i in range(n_chunk):
                if i + 1 < n_chunk:
                    fetch(i + 1).start()
                fetch(i).wait()
                if i >= 2:
                    store(i - 2).wait()
                bf16_buf[i % 2] = f32_buf[i % 2].astype(BF16)
                store(i).start()
            for i in range(max(n_chunk - 2, 0), n_chunk):
                store(i).wait()

        passed = [copy(4 + j, (*chip, c), sibling) for j, chip in enumerate(chips)]
        for j, chip in enumerate(chips):
            copy(1 + j, (*chip, c), me).wait_recv()
            passed[j].start()
        copy(0, sibling, me).wait_recv()
        for j, chip in enumerate(chips):
            copy(4 + j, (*chip, 1 - c), me).wait_recv()
        for cp in first + passed:
            cp.wait_send()
        mine.wait()
        if with_sum:
            sum_ref = refs[1 + (cast is not None) + 1]
            acc = out_ref[0:m_per, :]
            for d in range(1, N_DEV):
                acc = acc + out_ref[d * m_per:(d + 1) * m_per, :]
            sum_ref[...] = acc

    vm = pl.BlockSpec(memory_space=pltpu.VMEM)
    anyspec = pl.BlockSpec(memory_space=pl.ANY)
    out_shape = [jax.ShapeDtypeStruct((N_DEV * m_per, n), F32)]
    if with_sum:
        out_shape.append(jax.ShapeDtypeStruct((m_per, n), F32))
    in_specs, out_specs, operands = [vm], [vm] * n_out, [blk]
    scratch = [pltpu.SemaphoreType.DMA((7,)), pltpu.SemaphoreType.DMA((7,)), pltpu.SemaphoreType.DMA]
    if cast is not None:
        in_specs.append(anyspec)
        operands.append(w)
        out_shape.append(jax.ShapeDtypeStruct(full_shape, BF16))
        out_specs.append(anyspec)
        scratch += [pltpu.VMEM((2, tr, wn), F32), pltpu.VMEM((2, tr, wn), BF16), pltpu.SemaphoreType.DMA((2,)),
                    pltpu.SemaphoreType.DMA((2,))]
    return pl.pallas_call(
        body, name=name, out_shape=out_shape, in_specs=in_specs, out_specs=out_specs, scratch_shapes=scratch,
        compiler_params=pltpu.CompilerParams(vmem_limit_bytes=VMEM_LIMIT),
    )(*operands)


HBM_SPEC = pl.BlockSpec(memory_space=pltpu.HBM)
SEM_SPEC = pl.BlockSpec(memory_space=pltpu.SEMAPHORE)
SIDE_EFFECT = pltpu.SideEffectType.DATAFLOW_SIDE_EFFECTING


def _peer(x, y, c, q, cb):
    return (1 - x if q & 2 else x, 1 - y if q & 1 else y, 1 - c if cb else c)


def _w_in_piece(slots):
    def piece(part_ref, k, to):
        return part_ref.at[pl.ds(to[2] * (D // 2), D // 2), pl.ds(slots[k] * W_IN_SHARD, W_IN_SHARD)]
    return piece


def _w_out_piece(part_ref, k, to):
    ho = W_OUT_SHARD // 2
    return part_ref.at[pl.ds((2 * to[0] + to[1]) * W_OUT_SHARD + to[2] * ho, ho), :]


def _group_piece(part_ref, k, to):
    return part_ref.at[4 * to[0] + 2 * to[1] + to[2]]


def _whole_piece(part_ref, k, to):
    return part_ref


def _exchange_start_call(groups, name):
    ng = len(groups)
    lands = [lax.empty((len(rels),) + slot_shape, part.dtype) for part, rels, _, slot_shape in groups]

    def body(*refs):
        ins, outs = refs[:2 * ng], refs[2 * ng:]
        x, y, c = _coords()
        for g, (_, rels, piece, _) in enumerate(groups):
            part_ref, land_ref = ins[2 * g], ins[2 * g + 1]
            send_sems, recv_sems = outs[4 * g], outs[4 * g + 1]
            for k, (q, cb) in enumerate(rels):
                to = _peer(x, y, c, q, cb)
                pltpu.make_async_remote_copy(src_ref=piece(part_ref, k, to), dst_ref=land_ref.at[k], send_sem=send_sems.at[k],
                                             recv_sem=recv_sems.at[k], device_id=to, device_id_type=MESH).start()
        outs[-1][...] = jnp.zeros_like(outs[-1])

    out_shape, out_specs, operands = [], [], []
    for (part, rels, _, _), land in zip(groups, lands):
        n = len(rels)
        out_shape += [pltpu.SemaphoreType.DMA((n,)), pltpu.SemaphoreType.DMA((n,)), pltpu.HBM(part.shape, part.dtype),
                      pltpu.HBM(land.shape, land.dtype)]
        out_specs += [SEM_SPEC, SEM_SPEC, HBM_SPEC, HBM_SPEC]
        operands += [pltpu.with_memory_space_constraint(part, pltpu.HBM), pltpu.with_memory_space_constraint(land, pltpu.HBM)]
    out = pl.pallas_call(
        body, name=name,
        out_shape=tuple(out_shape) + (jax.ShapeDtypeStruct((1, 1), F32),),
        in_specs=(HBM_SPEC,) * (2 * ng), out_specs=tuple(out_specs) + (pl.BlockSpec(memory_space=pltpu.VMEM),),
        input_output_aliases={j: 4 * (j // 2) + 2 + j % 2 for j in range(2 * ng)},
        compiler_params=pltpu.CompilerParams(has_side_effects=SIDE_EFFECT),
    )(*operands)
    return [tuple(out[4 * g:4 * g + 4]) for g in range(ng)], out[-1]


def _exchange_wait_call(started, groups, after, name):
    ng = len(groups)

    def body(*refs):
        ins = refs[:4 * ng]
        x, y, c = _coords()
        for g, (_, rels, piece, _) in enumerate(groups):
            part_ref, land_ref, send_sems, recv_sems = ins[4 * g:4 * g + 4]
            for k, (q, cb) in enumerate(rels):
                to = _peer(x, y, c, q, cb)
                cp = pltpu.make_async_remote_copy(src_ref=piece(part_ref, k, to), dst_ref=land_ref.at[k], send_sem=send_sems.at[k],
                                                  recv_sem=recv_sems.at[k], device_id=to, device_id_type=MESH)
                cp.wait_send()
                cp.wait_recv()

    operands, in_specs, out_shape = [], [], []
    for send_sems, recv_sems, part_thru, land_thru in started:
        operands += [part_thru, land_thru, send_sems, recv_sems]
        in_specs += [HBM_SPEC, HBM_SPEC, SEM_SPEC, SEM_SPEC]
        out_shape += [pltpu.HBM(part_thru.shape, part_thru.dtype), pltpu.HBM(land_thru.shape, land_thru.dtype)]
    out = pl.pallas_call(
        body, name=name, out_shape=tuple(out_shape),
        in_specs=tuple(in_specs) + (pl.BlockSpec(memory_space=pl.ANY),), out_specs=(HBM_SPEC,) * (2 * ng),
        input_output_aliases={4 * g + j: 2 * g + j for g in range(ng) for j in range(2)},
        compiler_params=pltpu.CompilerParams(has_side_effects=SIDE_EFFECT),
    )(*operands, after)
    return [tuple(out[2 * g:2 * g + 2]) for g in range(ng)]


def _rope_tables(s):
    inv_freq = np.float32(10000.0) ** (-np.arange(0, HEAD, 2, dtype=np.float32) / np.float32(HEAD))
    ang = np.arange(s, dtype=np.float32)[:, None] * inv_freq[None, :]
    cos = np.tile(np.cos(ang), (1, LANE // (HEAD // 2))).astype(np.float32)
    sin = np.tile(np.sin(ang), (1, LANE // (HEAD // 2))).astype(np.float32)
    first_half = (np.arange(LANE) % HEAD) < (HEAD // 2)
    sin = np.where(first_half[None, :], -sin, sin)
    behind = lambda t: np.concatenate([t[:BLK], t[:-BLK]], axis=0)
    return tuple(jnp.asarray(t) for t in (cos, sin, behind(cos), behind(sin)))


def kernel(x, c, w_ada, b_ada, norm_g, w_in, ln_v_g, ln_v_b, w_spatial, b_spatial, sinks, w_out, w_ada_final, b_ada_final, final_norm_g, loss_target, m_w_ada, m_b_ada, m_norm_g, m_w_in, m_ln_v_g, m_ln_v_b, m_w_spatial, m_b_spatial, m_sinks, m_w_out, m_w_ada_final, m_b_ada_final, m_final_norm_g, v_w_ada, v_b_ada, v_norm_g, v_w_in, v_ln_v_g, v_ln_v_b, v_w_spatial, v_b_spatial, v_sinks, v_w_out, v_w_ada_final, v_b_ada_final, v_final_norm_g):
    s = x.shape[1]
    ax, ay, ac = _coords()
    chip = 2 * ax + ay
    me = 4 * ax + 2 * ay + ac
    n_ada = w_ada.shape[2]
    n_adaf = w_ada_final.shape[1]

    x2d = x.reshape(s, D)
    tgt = loss_target.reshape(s, D)
    w_ada2, w_in2, w_out2 = w_ada[0], w_in[0], w_out[0]
    b_ada_f2 = b_ada_final.reshape(1, 2 * D)
    gf = final_norm_g.reshape(1, D)

    c_all, w_in_own = _allgather_sum_call(jnp.pad(c, ((0, 7), (0, 0))), "gather_c", False, cast=(w_in2, (D, D_IN)))
    c_all = c_all[::8]
    mod_p, c_act = _rowmat_call(c_all, w_ada2, lax.dynamic_slice(b_ada, (0, chip * n_ada), (1, n_ada)), "mod")
    modf_p, _ = _rowmat_call(c_all, w_ada_final, lax.dynamic_slice(b_ada_f2, (0, chip * n_adaf), (1, n_adaf)), "mod_final")
    mods, w_out_own = _allgather_sum_call(jnp.concatenate([mod_p, modf_p], axis=1), "gather_mod", False, cast=(w_out2, (D, D)))
    my_rows = [lax.dynamic_slice(mods, (16 * j + me, 0), (1, n_ada + n_adaf)) for j in range(N_CHIP)]
    mod = jnp.concatenate([r[:, :n_ada] for r in my_rows], axis=1)
    mod_f = jnp.concatenate([r[:, n_ada:] for r in my_rows], axis=1)
    shift, scale, gate = mod[:, :D], mod[:, D:2 * D], mod[:, 2 * D:]
    shift_f, scale_f = mod_f[:, :D], mod_f[:, D:]

    pos = jnp.stack([chip, ac]).astype(jnp.int32)

    tables = _rope_tables(s)
    cos, sin = tables[:2]
    b_sp_t = b_spatial[0].T
    sinks1 = sinks.reshape(N_Q)
    h, proj, w_in_bf, w_out_bf = _proj_gather_call(pos, x2d, shift, scale, norm_g, w_in_own, w_out_own)
    y, probs, attn_out, psinks, qms = _mix_fwd_call(proj, cos, sin, ln_v_g, ln_v_b, w_spatial[0], b_sp_t, sinks1)
    dx2, do, dy, st_tail = _tail_call(y, w_out_bf, x2d, tgt, gate, shift_f, scale_f, gf)

    rel_o = [(0, 1), (1, 0), (1, 1), (2, 0), (2, 1), (3, 0), (3, 1)]
    rel_a = [(1, 0), (1, 1), (2, 0), (2, 1)]
    rel_b = [(3, 0), (3, 1), (0, 1)]
    piece_a, piece_b = _w_in_piece([0, 0, 1, 1]), _w_in_piece([0, 0, 1])
    half_in, half_out = (D // 2, W_IN_SHARD), (W_OUT_SHARD // 2, D)

    g_w_out_p = _tn_call(y, do, "grad_w_out")
    grp_o = [(g_w_out_p, rel_o, _w_out_piece, half_out)]
    st_o, tok_o = _exchange_start_call(grp_o, "send_w_out")
    dproj, st_ln, d_wsp, d_bsp_t, d_sink = _mix_bwd_call(
        proj, dy, probs, attn_out, psinks, qms, tables, ln_v_g + tok_o, ln_v_b, w_spatial[0], jnp.swapaxes(w_spatial[0], 1, 2),
        b_sp_t)
    g_w_in_a = _tn_shards_call(pos, h, dproj, (1, 2), "grad_w_in_a")
    grp_a = [(g_w_in_a, rel_a, piece_a, half_in), (d_wsp, rel_o, _group_piece, (BLK, BLK))]
    st_a, tok_a = _exchange_start_call(grp_a, "send_w_in_a")
    g_w_in_b = _tn_shards_call(pos, h, dproj, (3, 0), "grad_w_in_b")
    grp_b = [(g_w_in_b, rel_b, piece_b, half_in)]
    st_b, tok_b = _exchange_start_call(grp_b, "send_w_in_b")
    grad_x, st_dh = _dh_call(dproj, w_in_bf, x2d, dx2, scale + (tok_a + tok_b), norm_g)

    ((g_w_out_p, recv_o),) = _exchange_wait_call(st_o, grp_o, st_dh, "wait_w_out")
    (_, recv_a), (d_wsp, recv_s) = _exchange_wait_call(st_a, grp_a, st_dh, "wait_w_in_a")
    ((g_w_in_b, recv_b),) = _exchange_wait_call(st_b, grp_b, st_dh, "wait_w_in_b")
    mine_in = _sum_pieces_call(pos, g_w_in_b, lambda i, p, nrb: (p[1] * nrb + i, 1), [recv_a, recv_b], "sum_w_in")
    mine_out = _sum_pieces_call(pos, g_w_out_p, lambda i, p, nrb: ((2 * p[0] + p[1]) * nrb + i, 0), [recv_o], "sum_w_out")
    wsp_group = _sum_pieces_call(pos, d_wsp.reshape(GROUPS * BLK, BLK), lambda i, p, nrb: (2 * p[0] + p[1], 0), [recv_s],
                                 "sum_w_spatial")
    to_sibling = [(0, 1)]
    grp_p = [(mine_in, to_sibling, _whole_piece, half_in), (mine_out, to_sibling, _whole_piece, half_out)]
    st_p, tok_p = _exchange_start_call(grp_p, "swap_halves")

    misc = jnp.concatenate([st_ln, d_bsp_t[:, :GROUPS].T, d_sink, jnp.zeros((8, D - D_A - 2 * LANE), F32)], axis=1)
    pack = jnp.concatenate([wsp_group.reshape(8, D) + tok_p, st_tail, st_dh, misc], axis=0)
    rows = pack.shape[0]
    packs, tot = _allgather_sum_call(pack, "gather_small", True)
    packs = packs.reshape(N_DEV, rows, D)
    dmod_all = jnp.concatenate([packs[:, 16, :], packs[:, 17, :], packs[:, 11, :]], axis=1)
    dmodf_all = jnp.concatenate([packs[:, 8, :], packs[:, 9, :]], axis=1)
    loss = tot[13, 0]
    (mine_in, theirs_in), (mine_out, theirs_out) = _exchange_wait_call(st_p, grp_p, tot, "swapped_halves")
    small = {
        "b_ada": jnp.concatenate([tot[16:17], tot[17:18], tot[11:12]], axis=1),
        "norm_g": tot[18:19],
        "ln_v_g": tot[24:25, :D_A],
        "ln_v_b": tot[25:26, :D_A],
        "w_spatial": packs[:, 0:8, :].reshape(GROUPS * BLK, BLK),
        "b_spatial": tot[24:32, D_A:D_A + BLK],
        "sinks": tot[24:25, D_A + LANE:D_A + LANE + N_Q],
        "b_ada_final": jnp.concatenate([tot[8:9], tot[9:10]], axis=1),
        "final_norm_g": tot[10:11],
    }

    weights = dict(w_ada=w_ada, b_ada=b_ada, norm_g=norm_g, w_in=w_in, ln_v_g=ln_v_g, ln_v_b=ln_v_b, w_spatial=w_spatial,
                   b_spatial=b_spatial, sinks=sinks, w_out=w_out, w_ada_final=w_ada_final, b_ada_final=b_ada_final,
                   final_norm_g=final_norm_g)
    m_in = dict(w_ada=m_w_ada, b_ada=m_b_ada, norm_g=m_norm_g, w_in=m_w_in, ln_v_g=m_ln_v_g, ln_v_b=m_ln_v_b,
                w_spatial=m_w_spatial, b_spatial=m_b_spatial, sinks=m_sinks, w_out=m_w_out, w_ada_final=m_w_ada_final,
                b_ada_final=m_b_ada_final, final_norm_g=m_final_norm_g)
    v_in = dict(w_ada=v_w_ada, b_ada=v_b_ada, norm_g=v_norm_g, w_in=v_w_in, ln_v_g=v_ln_v_g, ln_v_b=v_ln_v_b,
                w_spatial=v_w_spatial, b_spatial=v_b_spatial, sinks=v_sinks, w_out=v_w_out, w_ada_final=v_w_ada_final,
                b_ada_final=v_b_ada_final, final_norm_g=v_final_norm_g)
    c_act_t = c_act.T
    outer = {"w_ada": lax.dynamic_slice(dmod_all, (0, chip * n_ada), (N_DEV, n_ada)),
             "w_ada_final": lax.dynamic_slice(dmodf_all, (0, chip * n_adaf), (N_DEV, n_adaf))}
    halves = {"w_in": (mine_in, theirs_in[0]), "w_out": (mine_out, theirs_out[0])}
    done = {}
    for name, (mine, theirs) in halves.items():
        shape2 = (2 * mine.shape[0], mine.shape[1])
        done[name] = _adam_halves_call(pos, weights[name].reshape(shape2), mine, theirs, m_in[name].reshape(shape2),
                                       v_in[name].reshape(shape2), "adam_" + name)
    for name, dm in outer.items():
        shape2 = (D, dm.shape[1])
        done[name] = _adam_outer_call(weights[name].reshape(shape2), c_act_t, dm, m_in[name].reshape(shape2),
                                      v_in[name].reshape(shape2), "adam_" + name)
    updates = _adam_small_call([(weights[name].reshape(g.shape), g, m_in[name].reshape(g.shape), v_in[name].reshape(g.shape))
                                for name, g in small.items()])
    for (name, g), upd in zip(small.items(), updates):
        done[name] = (g, *upd)
    outs = [[done[name][k].reshape(w.shape) for name, w in weights.items()] for k in range(4)]
    return (loss, grad_x.reshape(x.shape), *outs[0], *outs[1], *outs[2], *outs[3])
```

```python
import numpy as np
import jax
import jax.numpy as jnp
from jax import lax
from jax.experimental import pallas as pl
from jax.experimental.pallas import tpu as pltpu

F32 = jnp.float32
BF16 = jnp.bfloat16
MESH = pl.DeviceIdType.MESH

D = 2048
D_A = 1024
D_B = 1024
D_KV = 256
HEAD = 64
N_Q = 16
N_KV = 4
Q_PER_KV = N_Q // N_KV
BLK = 128
GROUPS = 8
D_IN = 5632
OFF_Q, OFF_K, OFF_V, OFF_ZB = 3072, 4096, 4352, 4608
N_CHIP = 4
N_DEV = 8
W_IN_SHARD = D_IN // N_CHIP
W_OUT_SHARD = D // N_CHIP
EPS = 1e-5
SCALE = HEAD ** -0.5
NEG = -1e30
LANE = 128
VMEM_LIMIT = 56 * 1024 * 1024

ADAM_LR, ADAM_B1, ADAM_B2, ADAM_EPS, ADAM_WD, ADAM_STEP = 0.001, 0.9, 0.999, 1e-08, 0.01, 10
ADAM_C1 = 1.0 - ADAM_B1 ** ADAM_STEP
ADAM_C2 = 1.0 - ADAM_B2 ** ADAM_STEP
ADAM_ROWS = 256

NT = (((1,), (1,)), ((), ()))
TN = (((0,), (0,)), ((), ()))


def _params(*sem):
    return pltpu.CompilerParams(dimension_semantics=sem, vmem_limit_bytes=VMEM_LIMIT)


def _silu_parts(z):
    sig = 1.0 / (1.0 + jnp.exp(-z))
    return z * sig, sig


def _swap_halves(v, first_half):
    return jnp.where(first_half, pltpu.roll(v, 96, 1), pltpu.roll(v, 32, 1))


def _rope(v, cos_t, sin_s, first_half):
    return v * cos_t + _swap_halves(v, first_half) * sin_s


def _unrope(dv, cos_t, sin_s, first_half):
    return dv * cos_t - _swap_halves(dv, first_half) * sin_s


def _lane_masks():
    lane = lax.broadcasted_iota(jnp.int32, (BLK, LANE), 1)
    return (lane % HEAD) < (HEAD // 2), lane < HEAD


def _band_valid(first_block_bound, rows=BLK):
    rr = lax.broadcasted_iota(jnp.int32, (rows, 2 * BLK), 0) & (BLK - 1)
    jj = lax.broadcasted_iota(jnp.int32, (rows, 2 * BLK), 1)
    return (jj > rr) & (jj <= rr + BLK) & (jj >= first_block_bound)


def _dup_kv(slab, lo):
    rolled = pltpu.roll(slab, HEAD, 1)
    return jnp.where(lo, slab, rolled).astype(BF16), jnp.where(lo, rolled, slab).astype(BF16)


def _fold_halves(a, b, lo):
    return jnp.where(lo, a, b) + pltpu.roll(jnp.where(lo, b, a), HEAD, 1)


def _stack_heads(ref, sb, slab, lo, dtype):
    kh, base = sb // 2, 2 * (sb % 2) * BLK
    zero = jnp.zeros_like(slab)
    ref[kh, base:base + BLK, :] = jnp.where(lo, slab, zero).astype(dtype)
    ref[kh, base + BLK:base + 2 * BLK, :] = jnp.where(lo, zero, slab).astype(dtype)


def _unstack_heads(ref, sb, lo):
    kh, base = sb // 2, 2 * (sb % 2) * BLK
    return jnp.where(lo, ref[kh, base:base + BLK, :], ref[kh, base + BLK:base + 2 * BLK, :])


def _sink_column(sinks_ref, kh):
    row = lax.broadcasted_iota(jnp.int32, (Q_PER_KV * BLK, 1), 0)
    col = jnp.full(row.shape, sinks_ref[Q_PER_KV * kh + Q_PER_KV - 1], F32)
    for n in range(Q_PER_KV - 2, -1, -1):
        col = jnp.where(row < (n + 1) * BLK, sinks_ref[Q_PER_KV * kh + n], col)
    return col


def _tril():
    t = lax.broadcasted_iota(jnp.int32, (BLK, BLK), 0)
    s = lax.broadcasted_iota(jnp.int32, (BLK, BLK), 1)
    return s <= t


def _layer_norm_fwd(va, lg, lb):
    mu = jnp.mean(va, axis=-1, keepdims=True)
    xc = va - mu
    rstd = lax.rsqrt(jnp.mean(xc * xc, axis=-1, keepdims=True) + EPS)
    vhat = xc * rstd
    return vhat, rstd, vhat * lg + lb


def _softmax_sink(qm, kdup, bias, sink):
    s = lax.dot_general(qm, kdup, NT, preferred_element_type=F32) + bias
    m = jnp.maximum(jnp.max(s, axis=-1, keepdims=True), sink)
    p = jnp.exp(s - m)
    esink = jnp.exp(sink - m)
    inv = 1.0 / (jnp.sum(p, axis=-1, keepdims=True) + esink)
    return p * inv, esink * inv


def _band_bias(bias_ref):
    rows = bias_ref.shape[1]
    bias_ref[0] = jnp.where(_band_valid(BLK, rows), 0.0, NEG)
    bias_ref[1] = jnp.where(_band_valid(0, rows), 0.0, NEG)


def _rowmat_call(c_all, w, b, name):
    n = w.shape[1]
    tn = 512

    def body(c_ref, w_ref, b_ref, o_ref, ca_ref):
        ca, _ = _silu_parts(c_ref[...])
        ca_ref[...] = ca
        o_ref[...] = jnp.dot(ca.astype(BF16), w_ref[...].astype(BF16), preferred_element_type=F32) + b_ref[...]

    return pl.pallas_call(
        body, name=name, grid=(n // tn,),
        in_specs=[pl.BlockSpec((N_DEV, D), lambda j: (0, 0)), pl.BlockSpec((D, tn), lambda j: (0, j)),
                  pl.BlockSpec((1, tn), lambda j: (0, j))],
        out_specs=[pl.BlockSpec((N_DEV, tn), lambda j: (0, j)), pl.BlockSpec((N_DEV, D), lambda j: (0, 0))],
        out_shape=[jax.ShapeDtypeStruct((N_DEV, n), F32), jax.ShapeDtypeStruct((N_DEV, D), F32)],
        compiler_params=_params("arbitrary"),
    )(c_all, w, b)


W_IN_PARTS = ((0, 768), (768, 640))
OUT_STREAMS = 4
X_STREAMS = 4


def _proj_gather_call(pos, x, shift, scale, norm_g, wi_full, wo_full):
    s = x.shape[0]
    tm = min(s, 512)
    nrow = s // tm
    hi = D // 2
    ho = W_OUT_SHARD // 2
    phases = [(0, None), (1, 0), (2, 0), (1, 1), (2, 1), (3, 0), (3, 1)]

    def body(pos_ref, *refs):
        x_refs = refs[:X_STREAMS]
        (sh_ref, sc_ref, g_ref, _, _, h_ref, proj_ref, fi_ref, fo_ref,
         h_all, wbuf, obuf, send_sems, recv_sems, load_sems, out_sems) = refs[X_STREAMS:]
        p = pl.program_id(0)
        i = pl.program_id(1)
        x_, y_, c_ = _coords()
        me, sibling = (x_, y_, c_), (x_, y_, 1 - c_)

        def shard_of(q):
            px, py, _ = _peer(x_, y_, c_, q, 0)
            return 2 * px + py

        def cols_of(q, cp):
            off, w = (0, W_IN_SHARD) if cp is None else W_IN_PARTS[cp]
            return shard_of(q) * W_IN_SHARD + off, w

        def part(which, q, pc, sub, cp):
            n = hi if which == 0 else ho
            base = pc * n
            if sub is not None:
                n //= 2
                base = base + sub * n
            if which == 0:
                c0, w = cols_of(q, cp)
                return fi_ref.at[pl.ds(base, n), pl.ds(c0, w)]
            return fo_ref.at[pl.ds(shard_of(q) * W_OUT_SHARD + base, n), :]

        def copy(k, ref, to):
            return pltpu.make_async_remote_copy(src_ref=ref, dst_ref=ref, send_sem=send_sems.at[k], recv_sem=recv_sems.at[k],
                                                device_id=to, device_id_type=MESH)

        def sem(which, kind, j, cp):
            return 4 * kind + 2 * cp + j if which == 0 else 16 + 2 * kind + j

        def to_neighbour(which, q, cp=None):
            return copy(sem(which, 0, q - 1, cp), part(which, 0, c_, None, cp), _peer(x_, y_, c_, q, 0))

        def from_neighbour(which, q, cp=None):
            return copy(sem(which, 0, q - 1, cp), part(which, q, c_, None, cp), me)

        def relay(which, q, cp=None):
            return copy(sem(which, 1, q - 1, cp), part(which, q, c_, q - 1, cp), _peer(x_, y_, c_, 3 - q, 0))

        def relayed(which, sub, cp=None):
            return copy(sem(which, 1, sub, cp), part(which, 3, c_, sub, cp), me)

        def to_sibling(which, q, cp=None):
            return copy(sem(which, 2, q - 1, cp), part(which, q, c_, None, cp), sibling)

        def from_sibling(which, q, cp=None):
            return copy(sem(which, 2, q - 1, cp), part(which, q, 1 - c_, None, cp), me)

        def relayed_to_sibling(which, sub, cp=None):
            return copy(sem(which, 3, sub, cp), part(which, 3, c_, sub, cp), sibling)

        def relayed_from_sibling(which, sub, cp=None):
            return copy(sem(which, 3, sub, cp), part(which, 3, 1 - c_, sub, cp), me)

        def pass_on_neighbours(which, cp=None):
            for q in (1, 2):
                from_neighbour(which, q, cp).wait_recv()
                to_sibling(which, q, cp).start()
                relay(which, q, cp).start()

        def pass_on_relayed(which, cp=None):
            for sub in range(2):
                relayed(which, sub, cp).wait_recv()
                relayed_to_sibling(which, sub, cp).start()

        def shard_load(k):
            c0, w = cols_of(*phases[k])
            return pltpu.make_async_copy(fi_ref.at[:, pl.ds(c0, w)], wbuf.at[k % 2, :, 0:w], load_sems.at[k % 2])

        class OutCopies:
            def __init__(self, k, slot, row0):
                c0, w = cols_of(*phases[k])
                strip = tm // OUT_STREAMS
                self.copies = [pltpu.make_async_copy(obuf.at[slot, n * strip:(n + 1) * strip, 0:w],
                                                     proj_ref.at[pl.ds(row0 + n * strip, strip), pl.ds(c0, w)],
                                                     out_sems.at[slot, n]) for n in range(OUT_STREAMS)]

            def start(self):
                for cp in self.copies:
                    cp.start()

            def wait(self):
                for cp in self.copies:
                    cp.wait()

        out_copy = OutCopies

        def drain(k):
            for j in range(min(2, nrow)):
                out_copy(k, (nrow - 1 - j) % 2, 0).wait()

        def arrivals(k):
            q, cp = phases[k]
            if k == 0:
                for cp_ in range(2):
                    for q_ in (1, 2):
                        to_neighbour(0, q_, cp_).start()
            elif q < 3 and k in (1, 3):
                pass_on_neighbours(0, cp)
                if k == 1:
                    for q_ in (1, 2):
                        to_neighbour(1, q_).start()
            elif k == 5:
                for cp_ in range(2):
                    pass_on_relayed(0, cp_)
                pass_on_neighbours(1)
            if q in (1, 2):
                from_sibling(0, q, cp).wait_recv()
            elif q == 3:
                for sub in range(2):
                    relayed_from_sibling(0, sub, cp).wait_recv()

        rows = pl.ds(pl.multiple_of(i * tm, tm), tm)
        slot = i % 2
        for k, (q, cp) in enumerate(phases):
            @pl.when(p == k)
            def _(k=k, q=q, cp=cp):
                @pl.when(i == 0)
                def _():
                    if k == 0:
                        arrivals(0)
                        shard_load(0).start()
                    else:
                        drain(k - 1)
                    shard_load(k).wait()

                if k + 1 < len(phases):
                    @pl.when(i == max(nrow - 2, 0))
                    def _():
                        arrivals(k + 1)
                        shard_load(k + 1).start()

                if k == 0:
                    wx = D // X_STREAMS
                    ssq = sum(jnp.sum(xr[...] * xr[...], axis=-1, keepdims=True) for xr in x_refs)
                    r = lax.rsqrt(ssq * (1.0 / D) + EPS)
                    for n, xr in enumerate(x_refs):
                        cols = slice(n * wx, (n + 1) * wx)
                        hv = ((xr[...] * r * g_ref[:, cols]) * (1.0 + sc_ref[:, cols]) + sh_ref[:, cols]).astype(BF16)
                        h_ref[:, cols] = hv
                        h_all[rows, cols] = hv

                @pl.when(i >= 2)
                def _():
                    out_copy(k, slot, 0).wait()

                w = cols_of(q, cp)[1]
                obuf[slot, :, 0:w] = jnp.dot(h_all[rows, :], wbuf[k % 2, :, 0:w], preferred_element_type=F32)
                out_copy(k, slot, pl.multiple_of(i * tm, tm)).start()

        @pl.when((p == len(phases) - 1) & (i == nrow - 1))
        def _():
            drain(len(phases) - 1)
            pass_on_relayed(1)
            for q in (1, 2):
                from_sibling(1, q).wait_recv()
            for sub in range(2):
                relayed_from_sibling(1, sub).wait_recv()
            for which, cps in ((0, (0, 1)), (1, (None,))):
                for cp in cps:
                    for q in (1, 2):
                        to_neighbour(which, q, cp).wait_send()
                        relay(which, q, cp).wait_send()
                        to_sibling(which, q, cp).wait_send()
                        relayed_to_sibling(which, q - 1, cp).wait_send()

    vec = pl.BlockSpec((1, D), lambda p, i, pos: (0, 0))
    first_phase_rows = lambda p, i, pos: (jnp.where(p == 0, i, nrow - 1), 0)
    anyspec = pl.BlockSpec(memory_space=pl.ANY)
    x_spec = lambda n: pl.BlockSpec((tm, D // X_STREAMS), lambda p, i, pos: (jnp.where(p == 0, i, nrow - 1), n))
    return pl.pallas_call(
        body, name="proj_gather",
        grid_spec=pltpu.PrefetchScalarGridSpec(
            num_scalar_prefetch=1, grid=(len(phases), nrow),
            in_specs=[x_spec(n) for n in range(X_STREAMS)] + [vec, vec, vec, anyspec, anyspec],
            out_specs=[pl.BlockSpec((tm, D), first_phase_rows), anyspec, anyspec, anyspec],
            scratch_shapes=[pltpu.VMEM((s, D), BF16), pltpu.VMEM((2, D, W_IN_SHARD), BF16), pltpu.VMEM((2, tm, W_IN_SHARD), F32),
                            pltpu.SemaphoreType.DMA((24,)), pltpu.SemaphoreType.DMA((24,)), pltpu.SemaphoreType.DMA((2,)),
                            pltpu.SemaphoreType.DMA((2, OUT_STREAMS))]),
        out_shape=[jax.ShapeDtypeStruct((s, D), BF16), jax.ShapeDtypeStruct((s, D_IN), F32),
                   jax.ShapeDtypeStruct((D, D_IN), BF16), jax.ShapeDtypeStruct((D, D), BF16)],
        input_output_aliases={X_STREAMS + 4: 2, X_STREAMS + 5: 3},
        compiler_params=_params("arbitrary", "arbitrary"),
    )(pos, *([x] * X_STREAMS), shift, scale, norm_g, wi_full, wo_full)


def _proj_specs(rev_nb=None, with_q=True):
    if rev_nb is None:
        row = lambda i: i
    else:
        row = lambda i: rev_nb - 1 - i
    wide = lambda col: pl.BlockSpec((BLK, D_A), lambda i: (row(i), col))
    kv = lambda col: pl.BlockSpec((BLK, D_KV), lambda i: (row(i), col))
    half = lambda col: pl.BlockSpec((BLK, 512), lambda i: (row(i), col))
    return ([wide(0), wide(1), wide(2)] + ([wide(3)] if with_q else [])
            + [kv(OFF_K // D_KV), kv(OFF_V // D_KV), half(OFF_ZB // 512), half(OFF_ZB // 512 + 1)])


def _mix_fwd_call(proj, cos, sin, ln_g, ln_b, w_sp, b_sp_t, sinks):
    s = proj.shape[0]
    nb = s // BLK

    def body(ua_ref, va_ref, za_ref, q_ref, k_ref, v_ref, zb0_ref, zb1_ref, cos_ref, sin_ref, lg_ref, lb_ref,
             w_ref, bt_ref, sinks_ref, y_ref, probs_ref, ost_ref, psink_ref, qm_ref, kdup_ref, vdup_ref, bias_ref):
        i = pl.program_id(0)
        first_half, lo = _lane_masks()
        cos_t = cos_ref[...]
        sin_t = sin_ref[...]

        @pl.when(i == 0)
        def _():
            kdup_ref[:, 0:BLK, :] = jnp.zeros((N_KV, BLK, LANE), BF16)
            vdup_ref[:, 0:BLK, :] = jnp.zeros((N_KV, BLK, LANE), BF16)
            _band_bias(bias_ref)

        @pl.when(i > 0)
        def _():
            kdup_ref[:, 0:BLK, :] = kdup_ref[:, BLK:2 * BLK, :]
            vdup_ref[:, 0:BLK, :] = vdup_ref[:, BLK:2 * BLK, :]

        _, _, vln = _layer_norm_fwd(va_ref[...], lg_ref[...], lb_ref[...])
        tril = _tril()
        for g in range(GROUPS):
            cols = slice(g * BLK, (g + 1) * BLK)
            wg = jnp.where(tril, w_ref[g], 0.0).astype(BF16)
            sg = jnp.dot(wg, vln[:, cols].astype(BF16), preferred_element_type=F32) + bt_ref[:, g:g + 1]
            gate, _ = _silu_parts(za_ref[:, cols])
            y_ref[:, cols] = (ua_ref[:, cols] * sg * gate).astype(BF16)

        for ks in range(2):
            cols = slice(ks * LANE, (ks + 1) * LANE)
            kr = _rope(k_ref[:, cols], cos_t, sin_t, first_half)
            for n, (kd, vd) in enumerate(zip(_dup_kv(kr, lo), _dup_kv(v_ref[:, cols], lo))):
                kdup_ref[2 * ks + n, BLK:2 * BLK, :] = kd
                vdup_ref[2 * ks + n, BLK:2 * BLK, :] = vd
        for sb in range(8):
            _stack_heads(qm_ref, sb, _rope(q_ref[:, sb * LANE:(sb + 1) * LANE], cos_t, sin_t, first_half) * SCALE, lo, BF16)

        block_kind = jnp.where(i > 0, 1, 0)

        psink_ref[...] = jnp.zeros((Q_PER_KV * BLK, LANE), F32)
        lane_q = lax.broadcasted_iota(jnp.int32, (Q_PER_KV * BLK, LANE), 1)

        def kv_head(kh, carry):
            probs, psink = _softmax_sink(qm_ref[kh], kdup_ref[kh], bias_ref[block_kind], _sink_column(sinks_ref, kh))
            probs_ref[kh] = probs
            psink_ref[...] = jnp.where(lane_q == kh, psink, psink_ref[...])
            ost_ref[kh] = jnp.dot(probs.astype(BF16), vdup_ref[kh], preferred_element_type=F32)
            return carry

        lax.fori_loop(0, N_KV, kv_head, 0, unroll=2)
        for sb in range(8):
            cols = slice(sb * LANE, (sb + 1) * LANE)
            zb = zb0_ref[:, cols] if sb < 4 else zb1_ref[:, (sb - 4) * LANE:(sb - 3) * LANE]
            gate, _ = _silu_parts(zb)
            y_ref[:, D_A + sb * LANE:D_A + (sb + 1) * LANE] = (_unstack_heads(ost_ref, sb, lo) * gate).astype(BF16)

    tab = pl.BlockSpec((BLK, LANE), lambda i: (i, 0))
    return pl.pallas_call(
        body, name="mix_fwd", grid=(nb,),
        in_specs=_proj_specs() + [
            tab, tab, pl.BlockSpec((1, D_A), lambda i: (0, 0)), pl.BlockSpec((1, D_A), lambda i: (0, 0)),
            pl.BlockSpec((GROUPS, BLK, BLK), lambda i: (0, 0, 0)), pl.BlockSpec((BLK, GROUPS), lambda i: (0, 0)),
            pl.BlockSpec(memory_space=pltpu.SMEM)],
        out_specs=[pl.BlockSpec((BLK, 2 * D_A), lambda i: (i, 0)),
                   pl.BlockSpec((None, N_KV, Q_PER_KV * BLK, 2 * BLK), lambda i: (i, 0, 0, 0)),
                   pl.BlockSpec((None, N_KV, Q_PER_KV * BLK, LANE), lambda i: (i, 0, 0, 0)),
                   pl.BlockSpec((None, Q_PER_KV * BLK, LANE), lambda i: (i, 0, 0)),
                   pl.BlockSpec((None, N_KV, Q_PER_KV * BLK, LANE), lambda i: (i, 0, 0, 0))],
        out_shape=[jax.ShapeDtypeStruct((s, 2 * D_A), BF16), jax.ShapeDtypeStruct((nb, N_KV, Q_PER_KV * BLK, 2 * BLK), F32),
                   jax.ShapeDtypeStruct((nb, N_KV, Q_PER_KV * BLK, LANE), F32), jax.ShapeDtypeStruct((nb, Q_PER_KV * BLK, LANE), F32),
                   jax.ShapeDtypeStruct((nb, N_KV, Q_PER_KV * BLK, LANE), BF16)],
        scratch_shapes=[pltpu.VMEM((N_KV, 2 * BLK, LANE), BF16), pltpu.VMEM((N_KV, 2 * BLK, LANE), BF16),
                        pltpu.VMEM((2, Q_PER_KV * BLK, 2 * BLK), F32)],
        compiler_params=_params("arbitrary"),
    )(proj, proj, proj, proj, proj, proj, proj, proj, cos, sin, ln_g, ln_b, w_sp, b_sp_t, sinks)


def _tail_call(y, w_out_bf, x, target, gate, shift_f, scale_f, gf):
    s = x.shape[0]
    tm = min(s, 256)
    nsteps = s // tm

    def body(y_ref, w_ref, x_ref, t_ref, gate_ref, shf_ref, scf_ref, gf_ref, dx2_ref, do_ref, dy_ref, st_ref):
        i = pl.program_id(0)

        @pl.when(i == 0)
        def _():
            st_ref[...] = jnp.zeros((8, D), F32)

        o = jnp.dot(y_ref[...], w_ref[...], preferred_element_type=F32)
        gate_v = gate_ref[...]
        x2 = x_ref[...] + gate_v * o
        r2 = lax.rsqrt(jnp.mean(x2 * x2, axis=-1, keepdims=True) + EPS)
        xn2 = x2 * r2
        hn2 = xn2 * gf_ref[...]
        one_sc = 1.0 + scf_ref[...]
        err = hn2 * one_sc + shf_ref[...] - t_ref[...]
        dout = err * (1.0 / D)
        dhn2 = dout * one_sc
        dxn2 = dhn2 * gf_ref[...]
        dx2 = r2 * (dxn2 - xn2 * jnp.mean(dxn2 * xn2, axis=-1, keepdims=True))
        dx2_ref[...] = dx2
        do = (dx2 * gate_v).astype(BF16)
        do_ref[...] = do
        dy_ref[...] = lax.dot_general(do, w_ref[...], NT, preferred_element_type=F32)
        st_ref[0:1, :] += jnp.sum(dout, axis=0, keepdims=True)
        st_ref[1:2, :] += jnp.sum(dout * hn2, axis=0, keepdims=True)
        st_ref[2:3, :] += jnp.sum(dhn2 * xn2, axis=0, keepdims=True)
        st_ref[3:4, :] += jnp.sum(dx2 * o, axis=0, keepdims=True)
        st_ref[4:5, :] += jnp.sum(err * err, axis=0, keepdims=True)

        @pl.when(i == nsteps - 1)
        def _():
            st_ref[5:6, :] = jnp.full((1, D), 0.5 / D, F32) * jnp.sum(st_ref[4:5, :])

    vec = pl.BlockSpec((1, D), lambda i: (0, 0))
    rows = lambda: pl.BlockSpec((tm, D), lambda i: (i, 0))
    return pl.pallas_call(
        body, name="tail", grid=(nsteps,),
        in_specs=[rows(), pl.BlockSpec((D, D), lambda i: (0, 0)), rows(), rows(), vec, vec, vec, vec],
        out_specs=[rows(), rows(), rows(), pl.BlockSpec((8, D), lambda i: (0, 0))],
        out_shape=[jax.ShapeDtypeStruct((s, D), F32), jax.ShapeDtypeStruct((s, D), BF16), jax.ShapeDtypeStruct((s, D), F32),
                   jax.ShapeDtypeStruct((8, D), F32)],
        compiler_params=_params("arbitrary"),
    )(y, w_out_bf, x, target, gate, shift_f, scale_f, gf)


def _tn_call(a, b, name):
    s, m = a.shape
    n = b.shape[1]
    tn = 1024
    ts = min(s, 1024)
    nk = s // ts

    def body(a_ref, b_ref, o_ref, acc_ref):
        k = pl.program_id(1)

        @pl.when(k == 0)
        def _():
            acc_ref[...] = jnp.zeros((m, tn), F32)

        acc_ref[...] += lax.dot_general(a_ref[...], b_ref[...], TN, preferred_element_type=F32)

        @pl.when(k == nk - 1)
        def _():
            o_ref[...] = acc_ref[...].astype(BF16)

    return pl.pallas_call(
        body, name=name, grid=(n // tn, nk),
        in_specs=[pl.BlockSpec((ts, m), lambda j, k: (k, 0)), pl.BlockSpec((ts, tn), lambda j, k: (k, j))],
        out_specs=pl.BlockSpec((m, tn), lambda j, k: (0, j)),
        out_shape=jax.ShapeDtypeStruct((m, n), BF16),
        scratch_shapes=[pltpu.VMEM((m, tn), F32)],
        compiler_params=_params("parallel", "arbitrary"),
    )(a, b)


def _tn_shards_call(pos, a, b, qs, name):
    s, m = a.shape
    ts = min(s, 1024)
    nk = s // ts

    def body(pos_ref, a_ref, b_ref, o_ref, acc_ref):
        k = pl.program_id(1)

        @pl.when(k == 0)
        def _():
            acc_ref[...] = jnp.zeros((m, W_IN_SHARD), F32)

        acc_ref[...] += lax.dot_general(a_ref[...], b_ref[...], TN, preferred_element_type=F32)

        @pl.when(k == nk - 1)
        def _():
            o_ref[...] = acc_ref[...].astype(BF16)

    def shard(j, pos):
        q = qs[0]
        for n in range(1, len(qs)):
            q = jnp.where(j == n, qs[n], q)
        return jnp.bitwise_xor(pos[0], q)

    return pl.pallas_call(
        body, name=name,
        grid_spec=pltpu.PrefetchScalarGridSpec(
            num_scalar_prefetch=1, grid=(len(qs), nk),
            in_specs=[pl.BlockSpec((ts, m), lambda j, k, pos: (k, 0)),
                      pl.BlockSpec((ts, W_IN_SHARD), lambda j, k, pos: (k, shard(j, pos)))],
            out_specs=pl.BlockSpec((m, W_IN_SHARD), lambda j, k, pos: (0, j)),
            scratch_shapes=[pltpu.VMEM((m, W_IN_SHARD), F32)]),
        out_shape=jax.ShapeDtypeStruct((m, len(qs) * W_IN_SHARD), BF16),
        compiler_params=_params("parallel", "arbitrary"),
    )(pos, a, b)


def _mix_bwd_call(proj, dy, probs, outs, psinks, qms, tables, ln_g, ln_b, w_sp, w_sp_t, b_sp_t):
    s = proj.shape[0]
    nb = s // BLK
    rev = lambda i: nb - 1 - i
    prev = lambda i: jnp.maximum(nb - 2 - i, 0)

    def body(ua_ref, va_ref, za_ref, k_ref, v_ref, zb0_ref, zb1_ref, kp_ref, vp_ref, dy_ref,
             probs_ref, ost_ref, psink_ref, qm_ref, cos_ref, sin_ref, cosp_ref, sinp_ref, lg_ref, lb_ref, w_ref, wt_ref, bt_ref,
             dp_ref, lnst_ref, dw_ref, dbt_ref, dsink_ref,
             kdup_ref, vdup_ref, dvln_ref, dom_ref, dqst_ref, dkdup_ref, dvdup_ref, kcar_ref, vcar_ref, sigb_ref):
        i = pl.program_id(0)
        first_half, lo = _lane_masks()
        lane8 = lax.broadcasted_iota(jnp.int32, (8, LANE), 1)
        cos_t = cos_ref[...]
        sin_t = sin_ref[...]

        @pl.when(i == 0)
        def _():
            lnst_ref[...] = jnp.zeros((8, D_A), F32)
            dw_ref[...] = jnp.zeros((GROUPS, BLK, BLK), F32)
            dbt_ref[...] = jnp.zeros((BLK, LANE), F32)
            dsink_ref[...] = jnp.zeros((8, LANE), F32)
            kcar_ref[...] = jnp.zeros((BLK, D_KV), F32)
            vcar_ref[...] = jnp.zeros((BLK, D_KV), F32)

        vhat, rstd, vln = _layer_norm_fwd(va_ref[...], lg_ref[...], lb_ref[...])
        tril = _tril()
        triu = jnp.logical_not(tril) | (lax.broadcasted_iota(jnp.int32, (BLK, BLK), 0) == lax.broadcasted_iota(jnp.int32, (BLK, BLK), 1))
        lane_b = lax.broadcasted_iota(jnp.int32, (BLK, LANE), 1)
        db_acc = jnp.zeros((BLK, LANE), F32)
        for g in range(GROUPS):
            cols = slice(g * BLK, (g + 1) * BLK)
            vln_g = vln[:, cols].astype(BF16)
            wg = jnp.where(tril, w_ref[g], 0.0).astype(BF16)
            sg = jnp.dot(wg, vln_g, preferred_element_type=F32) + bt_ref[:, g:g + 1]
            za = za_ref[:, cols]
            gate, sig = _silu_parts(za)
            ua = ua_ref[:, cols]
            dya_g = dy_ref[:, cols]
            dya = dya_g * gate
            dp_ref[:, cols] = (dya * sg).astype(BF16)
            dp_ref[:, 2 * D_A + g * BLK:2 * D_A + (g + 1) * BLK] = (
                dya_g * (ua * sg) * (sig * (1.0 + za * (1.0 - sig)))).astype(BF16)
            ds = dya * ua
            ds_b = ds.astype(BF16)
            wtg = jnp.where(triu, wt_ref[g], 0.0).astype(BF16)
            dvln_ref[:, cols] = jnp.dot(wtg, ds_b, preferred_element_type=F32)
            dw_ref[g] += jnp.where(tril, lax.dot_general(ds_b, vln_g, NT, preferred_element_type=F32), 0.0)
            db_acc = db_acc + jnp.where(lane_b == g, jnp.sum(ds, axis=-1, keepdims=True), 0.0)
        dbt_ref[...] += db_acc
        dvln = dvln_ref[...]
        lnst_ref[0:1, :] += jnp.sum(dvln * vhat, axis=0, keepdims=True)
        lnst_ref[1:2, :] += jnp.sum(dvln, axis=0, keepdims=True)
        dvhat = dvln * lg_ref[...]
        m1 = jnp.mean(dvhat, axis=-1, keepdims=True)
        m2 = jnp.mean(dvhat * vhat, axis=-1, keepdims=True)
        dp_ref[:, D_A:2 * D_A] = (rstd * (dvhat - m1 - vhat * m2)).astype(BF16)

        cosp = cosp_ref[...]
        sinp = sinp_ref[...]
        for ks in range(2):
            cols = slice(ks * LANE, (ks + 1) * LANE)
            kr = _rope(k_ref[:, cols], cos_t, sin_t, first_half)
            kpr = _rope(kp_ref[:, cols], cosp, sinp, first_half)
            for n, (kc, vc, kp, vp) in enumerate(zip(_dup_kv(kr, lo), _dup_kv(v_ref[:, cols], lo),
                                                     _dup_kv(kpr, lo), _dup_kv(vp_ref[:, cols], lo))):
                kdup_ref[2 * ks + n, BLK:2 * BLK, :] = kc
                vdup_ref[2 * ks + n, BLK:2 * BLK, :] = vc
                kdup_ref[2 * ks + n, 0:BLK, :] = kp
                vdup_ref[2 * ks + n, 0:BLK, :] = vp
        for sb in range(8):
            cols = slice(sb * LANE, (sb + 1) * LANE)
            zb = zb0_ref[:, cols] if sb < 4 else zb1_ref[:, (sb - 4) * LANE:(sb - 3) * LANE]
            gate, sig = _silu_parts(zb)
            sigb_ref[:, cols] = sig
            _stack_heads(dom_ref, sb, dy_ref[:, D_A + sb * LANE:D_A + (sb + 1) * LANE] * gate, lo, F32)

        lane_q = lax.broadcasted_iota(jnp.int32, (Q_PER_KV * BLK, LANE), 1)

        def kv_head(kh, sink_acc):
            qm = qm_ref[kh]
            kd = kdup_ref[kh]
            vd = vdup_ref[kh]
            probs = probs_ref[kh]
            probs_b = probs.astype(BF16)
            o = ost_ref[kh]
            dom = dom_ref[kh]
            dom_b = dom.astype(BF16)
            delta = jnp.sum(dom * o, axis=-1, keepdims=True)
            dpr = lax.dot_general(dom_b, vd, NT, preferred_element_type=F32)
            dss = (probs * (dpr - delta)).astype(BF16)
            sink_acc = sink_acc + jnp.where(lane_q == kh, psink_ref[...] * delta, 0.0)
            dqst_ref[kh] = jnp.dot(dss, kd, preferred_element_type=F32)
            dkdup_ref[kh] = lax.dot_general(dss, qm, TN, preferred_element_type=F32)
            dvdup_ref[kh] = lax.dot_general(probs_b, dom_b, TN, preferred_element_type=F32)
            return sink_acc

        sink_acc = jnp.zeros((Q_PER_KV * BLK, LANE), F32)
        for kh in range(N_KV):
            sink_acc = kv_head(kh, sink_acc)
        lane1 = lax.broadcasted_iota(jnp.int32, (1, LANE), 1)
        dsink_acc = jnp.zeros((8, LANE), F32)
        for n in range(Q_PER_KV):
            col = jnp.sum(sink_acc[n * BLK:(n + 1) * BLK], axis=0, keepdims=True)
            for kh in range(N_KV):
                dsink_acc = dsink_acc + jnp.where(lane8 == Q_PER_KV * kh + n, -jnp.sum(jnp.where(lane1 == kh, col, 0.0)), 0.0)
        row0 = lax.broadcasted_iota(jnp.int32, (8, LANE), 0) == 0
        dsink_ref[...] += jnp.where(row0, dsink_acc, 0.0)

        for sb in range(8):
            cols = slice(sb * LANE, (sb + 1) * LANE)
            zb = zb0_ref[:, cols] if sb < 4 else zb1_ref[:, (sb - 4) * LANE:(sb - 3) * LANE]
            sig = sigb_ref[:, cols]
            dyb = dy_ref[:, D_A + sb * LANE:D_A + (sb + 1) * LANE]
            dp_ref[:, OFF_ZB + sb * LANE:OFF_ZB + (sb + 1) * LANE] = (
                dyb * _unstack_heads(ost_ref, sb, lo) * (sig * (1.0 + zb * (1.0 - sig)))).astype(BF16)
            dq_r = _unstack_heads(dqst_ref, sb, lo) * SCALE
            dp_ref[:, OFF_Q + sb * LANE:OFF_Q + (sb + 1) * LANE] = _unrope(dq_r, cos_t, sin_t, first_half).astype(BF16)

        lo2 = lax.broadcasted_iota(jnp.int32, (2 * BLK, LANE), 1) < HEAD
        for ks in range(2):
            cols = slice(ks * LANE, (ks + 1) * LANE)
            dk_band = _fold_halves(dkdup_ref[2 * ks], dkdup_ref[2 * ks + 1], lo2)
            dv_band = _fold_halves(dvdup_ref[2 * ks], dvdup_ref[2 * ks + 1], lo2)
            dkr = dk_band[BLK:2 * BLK, :] + kcar_ref[:, cols]
            dp_ref[:, OFF_K + ks * LANE:OFF_K + (ks + 1) * LANE] = _unrope(dkr, cos_t, sin_t, first_half).astype(BF16)
            dp_ref[:, OFF_V + ks * LANE:OFF_V + (ks + 1) * LANE] = (
                dv_band[BLK:2 * BLK, :] + vcar_ref[:, cols]).astype(BF16)
            kcar_ref[:, cols] = dk_band[0:BLK, :]
            vcar_ref[:, cols] = dv_band[0:BLK, :]

    tab = pl.BlockSpec((BLK, LANE), lambda i: (rev(i), 0))
    kvp = lambda col: pl.BlockSpec((BLK, D_KV), lambda i: (prev(i), col))
    vec = pl.BlockSpec((1, D_A), lambda i: (0, 0))
    w3 = pl.BlockSpec((GROUPS, BLK, BLK), lambda i: (0, 0, 0))
    return pl.pallas_call(
        body, name="mix_bwd", grid=(nb,),
        in_specs=_proj_specs(nb, with_q=False) + [
            kvp(OFF_K // D_KV), kvp(OFF_V // D_KV), pl.BlockSpec((BLK, 2 * D_A), lambda i: (rev(i), 0)),
            pl.BlockSpec((None, N_KV, Q_PER_KV * BLK, 2 * BLK), lambda i: (rev(i), 0, 0, 0)),
            pl.BlockSpec((None, N_KV, Q_PER_KV * BLK, LANE), lambda i: (rev(i), 0, 0, 0)),
            pl.BlockSpec((None, Q_PER_KV * BLK, LANE), lambda i: (rev(i), 0, 0)),
            pl.BlockSpec((None, N_KV, Q_PER_KV * BLK, LANE), lambda i: (rev(i), 0, 0, 0)),
            tab, tab, tab, tab, vec, vec, w3, w3, pl.BlockSpec((BLK, GROUPS), lambda i: (0, 0))],
        out_specs=[pl.BlockSpec((BLK, D_IN), lambda i: (rev(i), 0)), pl.BlockSpec((8, D_A), lambda i: (0, 0)), w3,
                   pl.BlockSpec((BLK, LANE), lambda i: (0, 0)), pl.BlockSpec((8, LANE), lambda i: (0, 0))],
        out_shape=[jax.ShapeDtypeStruct((s, D_IN), BF16), jax.ShapeDtypeStruct((8, D_A), F32),
                   jax.ShapeDtypeStruct((GROUPS, BLK, BLK), F32), jax.ShapeDtypeStruct((BLK, LANE), F32),
                   jax.ShapeDtypeStruct((8, LANE), F32)],
        scratch_shapes=[pltpu.VMEM((N_KV, 2 * BLK, LANE), BF16), pltpu.VMEM((N_KV, 2 * BLK, LANE), BF16),
                        pltpu.VMEM((BLK, D_A), F32),
                        pltpu.VMEM((N_KV, Q_PER_KV * BLK, LANE), F32), pltpu.VMEM((N_KV, Q_PER_KV * BLK, LANE), F32),
                        pltpu.VMEM((N_KV, 2 * BLK, LANE), F32), pltpu.VMEM((N_KV, 2 * BLK, LANE), F32),
                        pltpu.VMEM((BLK, D_KV), F32), pltpu.VMEM((BLK, D_KV), F32), pltpu.VMEM((BLK, D_B), F32)],
        compiler_params=_params("arbitrary"),
    )(proj, proj, proj, proj, proj, proj, proj, proj, proj, dy, probs, outs, psinks, qms, *tables, ln_g, ln_b,
      w_sp, w_sp_t, b_sp_t)


def _dh_call(dproj, w_bf, x, dx2, scale, norm_g):
    s = x.shape[0]
    tm = min(s, 512)
    tk = W_IN_SHARD
    nk = D_IN // tk

    def body(dp_ref, w_ref, x_ref, dx2_ref, sc_ref, g_ref, gx_ref, st_ref, acc_ref):
        i = pl.program_id(0)
        k = pl.program_id(1)

        @pl.when((i == 0) & (k == 0))
        def _():
            st_ref[...] = jnp.zeros((8, D), F32)

        @pl.when(k == 0)
        def _():
            acc_ref[...] = jnp.zeros((tm, D), F32)

        acc_ref[...] += lax.dot_general(dp_ref[...], w_ref[...], NT, preferred_element_type=F32)

        @pl.when(k == nk - 1)
        def _():
            g = g_ref[...]
            one_sc = 1.0 + sc_ref[...]

            def chunk(n, carry):
                rows = pl.ds(pl.multiple_of(n * BLK, BLK), BLK)
                dh = acc_ref[rows, :]
                xv = x_ref[rows, :]
                r = lax.rsqrt(jnp.mean(xv * xv, axis=-1, keepdims=True) + EPS)
                xn = xv * r
                dhn = dh * one_sc
                dxn = dhn * g
                gx_ref[rows, :] = dx2_ref[rows, :] + r * (dxn - xn * jnp.mean(dxn * xn, axis=-1, keepdims=True))
                st_ref[0:1, :] += jnp.sum(dh, axis=0, keepdims=True)
                st_ref[1:2, :] += jnp.sum(dh * (xn * g), axis=0, keepdims=True)
                st_ref[2:3, :] += jnp.sum(dhn * xn, axis=0, keepdims=True)
                return carry

            lax.fori_loop(0, tm // BLK, chunk, 0)

    vec = pl.BlockSpec((1, D), lambda i, k: (0, 0))
    rows = lambda: pl.BlockSpec((tm, D), lambda i, k: (i, 0))
    return pl.pallas_call(
        body, name="dh", grid=(s // tm, nk),
        in_specs=[pl.BlockSpec((tm, tk), lambda i, k: (i, k)), pl.BlockSpec((D, tk), lambda i, k: (0, k)), rows(), rows(), vec, vec],
        out_specs=[rows(), pl.BlockSpec((8, D), lambda i, k: (0, 0))],
        out_shape=[jax.ShapeDtypeStruct((s, D), F32), jax.ShapeDtypeStruct((8, D), F32)],
        scratch_shapes=[pltpu.VMEM((tm, D), F32)],
        compiler_params=_params("arbitrary", "arbitrary"),
    )(dproj, w_bf, x, dx2, scale, norm_g)


def _adam_math(w, g, m, v):
    m_new = ADAM_B1 * m + (1.0 - ADAM_B1) * g
    v_new = ADAM_B2 * v + (1.0 - ADAM_B2) * (g * g)
    m_hat = m_new / ADAM_C1
    v_hat = v_new / ADAM_C2
    delta = -ADAM_LR * (m_hat / (jnp.sqrt(v_hat) + ADAM_EPS) + ADAM_WD * w)
    return delta, m_new, v_new


def _adam_small_call(tensors):
    n = len(tensors)

    def body(*refs):
        ins, outs = refs[:4 * n], refs[4 * n:]
        for t in range(n):
            w_ref, g_ref, m_ref, v_ref = ins[4 * t:4 * t + 4]
            d, mo, vo = _adam_math(w_ref[...], g_ref[...], m_ref[...], v_ref[...])
            outs[3 * t][...], outs[3 * t + 1][...], outs[3 * t + 2][...] = d, mo, vo

    vm = pl.BlockSpec(memory_space=pltpu.VMEM)
    flat = [a for t in tensors for a in t]
    out = pl.pallas_call(
        body, name="adam_small", in_specs=[vm] * (4 * n), out_specs=[vm] * (3 * n),
        out_shape=[jax.ShapeDtypeStruct(t[0].shape, F32) for t in tensors for _ in range(3)],
        compiler_params=pltpu.CompilerParams(vmem_limit_bytes=VMEM_LIMIT),
    )(*flat)
    return [tuple(out[3 * t:3 * t + 3]) for t in range(n)]


def _adam_halves_call(pos, w, mine, theirs, m, v, name):
    r, n = w.shape
    half = r // 2
    tr = ADAM_ROWS
    nh = half // tr

    def body(pos_ref, w_ref, mine_ref, theirs_ref, m_ref, v_ref, g_ref, d_ref, mo_ref, vo_ref):
        is_mine = (pl.program_id(0) // nh) == pos_ref[1]
        g = jnp.where(is_mine, mine_ref[...], theirs_ref[...])
        g_ref[...] = g
        d_ref[...], mo_ref[...], vo_ref[...] = _adam_math(w_ref[...], g, m_ref[...], v_ref[...])

    spec = lambda: pl.BlockSpec((tr, n), lambda i, pos: (i, 0))

    def half_spec(core_of_half):
        def index(i, pos):
            first = core_of_half(pos) == 0
            active = (i // nh == 0) == first
            return jnp.where(active, i % nh, jnp.where(first, nh - 1, 0)), 0
        return pl.BlockSpec((tr, n), index)

    return pl.pallas_call(
        body, name=name,
        grid_spec=pltpu.PrefetchScalarGridSpec(
            num_scalar_prefetch=1, grid=(r // tr,),
            in_specs=[spec(), half_spec(lambda pos: pos[1]), half_spec(lambda pos: 1 - pos[1]), spec(), spec()],
            out_specs=[spec() for _ in range(4)]),
        out_shape=[jax.ShapeDtypeStruct((r, n), F32)] * 4, compiler_params=_params("arbitrary"),
    )(pos, w, mine, theirs, m, v)


def _adam_outer_call(w, ct, dm, m, v, name):
    r, n = w.shape
    tr = ADAM_ROWS

    def body(w_ref, ct_ref, dm_ref, m_ref, v_ref, g_ref, d_ref, mo_ref, vo_ref):
        g = ct_ref[:, 0:1] * dm_ref[0:1, :]
        for b in range(1, N_DEV):
            g = g + ct_ref[:, b:b + 1] * dm_ref[b:b + 1, :]
        g_ref[...] = g
        d_ref[...], mo_ref[...], vo_ref[...] = _adam_math(w_ref[...], g, m_ref[...], v_ref[...])

    spec = lambda: pl.BlockSpec((tr, n), lambda i: (i, 0))
    return pl.pallas_call(
        body, name=name, grid=(r // tr,),
        in_specs=[spec(), pl.BlockSpec((tr, N_DEV), lambda i: (i, 0)), pl.BlockSpec((N_DEV, n), lambda i: (0, 0)), spec(), spec()],
        out_specs=[spec() for _ in range(4)],
        out_shape=[jax.ShapeDtypeStruct((r, n), F32)] * 4, compiler_params=_params("parallel"),
    )(w, ct, dm, m, v)


def _sum_pieces_call(pos, part, part_block, recvs, name):
    r, n = recvs[0].shape[1:]
    tr = min(r, 256)
    nrb = r // tr

    def body(pos_ref, p_ref, *refs):
        acc = p_ref[...].astype(F32)
        for r_ref in refs[:-1]:
            for d in range(r_ref.shape[0]):
                acc = acc + r_ref[d].astype(F32)
        refs[-1][...] = acc

    return pl.pallas_call(
        body, name=name,
        grid_spec=pltpu.PrefetchScalarGridSpec(
            num_scalar_prefetch=1, grid=(nrb,),
            in_specs=[pl.BlockSpec((tr, n), lambda i, pos: part_block(i, pos, nrb))] + [
                pl.BlockSpec((rv.shape[0], tr, n), lambda i, pos: (0, i, 0)) for rv in recvs],
            out_specs=pl.BlockSpec((tr, n), lambda i, pos: (i, 0))),
        out_shape=jax.ShapeDtypeStruct((r, n), F32), compiler_params=_params("parallel"),
    )(pos, part, *recvs)


def _coords():
    return lax.axis_index("x"), lax.axis_index("y"), lax.axis_index("c")


CAST_ROWS = 256


def _allgather_sum_call(blk, name, with_sum, cast=None):
    m_per, n = blk.shape
    n_out = 2 if with_sum else 1
    if cast is not None:
        w, full_shape = cast
        wr, wn = w.shape
        by_cols = full_shape[0] == wr
        tr = min(wr, CAST_ROWS)
        n_chunk = wr // tr

    def body(*refs):
        x_ref = refs[0]
        out_ref = refs[1 + (cast is not None)]
        rest = refs[1 + (cast is not None) + n_out + (cast is not None):]
        send_sems, recv_sems, local_sem = rest[:3]
        x, y, c = _coords()
        me, sibling = (x, y, c), (x, y, 1 - c)
        chips = [(1 - x, y), (x, 1 - y), (1 - x, 1 - y)]

        def rows(px, py, pc):
            return out_ref.at[pl.ds((4 * px + 2 * py + pc) * m_per, m_per), :]

        def copy(k, block, to, src=None):
            return pltpu.make_async_remote_copy(
                src_ref=rows(*block) if src is None else src, dst_ref=rows(*block),
                send_sem=send_sems.at[k], recv_sem=recv_sems.at[k], device_id=to, device_id_type=MESH)

        mine = pltpu.make_async_copy(x_ref, rows(*me), local_sem)
        mine.start()
        first = [copy(0, me, sibling, src=x_ref)]
        first += [copy(1 + j, me, (*chip, c), src=x_ref) for j, chip in enumerate(chips)]
        for cp in first:
            cp.start()

        if cast is not None:
            w_ref, full_ref = refs[1], refs[1 + 1 + n_out]
            f32_buf, bf16_buf, in_sems, out_sems = rest[3:]
            chip_no = 2 * x + y

            def fetch(i):
                return pltpu.make_async_copy(w_ref.at[pl.ds(i * tr, tr), :], f32_buf.at[i % 2], in_sems.at[i % 2])

            def store(i):
                if by_cols:
                    dst = full_ref.at[pl.ds(i * tr, tr), pl.ds(chip_no * wn, wn)]
                else:
                    dst = full_ref.at[pl.ds(chip_no * wr + i * tr, tr), :]
                return pltpu.make_async_copy(bf16_buf.at[i % 2], dst, out_sems.at[i % 2])

            fetch(0).start()
            for i in range(n_chunk):
                if i + 1 < n_chunk:
                    fetch(i + 1).start()
                fetch(i).wait()
                if i >= 2:
                    store(i - 2).wait()
                bf16_buf[i % 2] = f32_buf[i % 2].astype(BF16)
                store(i).start()
            for i in range(max(n_chunk - 2, 0), n_chunk):
                store(i).wait()

        passed = [copy(4 + j, (*chip, c), sibling) for j, chip in enumerate(chips)]
        for j, chip in enumerate(chips):
            copy(1 + j, (*chip, c), me).wait_recv()
            passed[j].start()
        copy(0, sibling, me).wait_recv()
        for j, chip in enumerate(chips):
            copy(4 + j, (*chip, 1 - c), me).wait_recv()
        for cp in first + passed:
            cp.wait_send()
        mine.wait()
        if with_sum:
            sum_ref = refs[1 + (cast is not None) + 1]
            acc = out_ref[0:m_per, :]
            for d in range(1, N_DEV):
                acc = acc + out_ref[d * m_per:(d + 1) * m_per, :]
            sum_ref[...] = acc

    vm = pl.BlockSpec(memory_space=pltpu.VMEM)
    anyspec = pl.BlockSpec(memory_space=pl.ANY)
    out_shape = [jax.ShapeDtypeStruct((N_DEV * m_per, n), F32)]
    if with_sum:
        out_shape.append(jax.ShapeDtypeStruct((m_per, n), F32))
    in_specs, out_specs, operands = [vm], [vm] * n_out, [blk]
    scratch = [pltpu.SemaphoreType.DMA((7,)), pltpu.SemaphoreType.DMA((7,)), pltpu.SemaphoreType.DMA]
    if cast is not None:
        in_specs.append(anyspec)
        operands.append(w)
        out_shape.append(jax.ShapeDtypeStruct(full_shape, BF16))
        out_specs.append(anyspec)
        scratch += [pltpu.VMEM((2, tr, wn), F32), pltpu.VMEM((2, tr, wn), BF16), pltpu.SemaphoreType.DMA((2,)),
                    pltpu.SemaphoreType.DMA((2,))]
    return pl.pallas_call(
        body, name=name, out_shape=out_shape, in_specs=in_specs, out_specs=out_specs, scratch_shapes=scratch,
        compiler_params=pltpu.CompilerParams(vmem_limit_bytes=VMEM_LIMIT),
    )(*operands)


HBM_SPEC = pl.BlockSpec(memory_space=pltpu.HBM)
SEM_SPEC = pl.BlockSpec(memory_space=pltpu.SEMAPHORE)
SIDE_EFFECT = pltpu.SideEffectType.DATAFLOW_SIDE_EFFECTING


def _peer(x, y, c, q, cb):
    return (1 - x if q & 2 else x, 1 - y if q & 1 else y, 1 - c if cb else c)


def _w_in_piece(slots):
    def piece(part_ref, k, to):
        return part_ref.at[pl.ds(to[2] * (D // 2), D // 2), pl.ds(slots[k] * W_IN_SHARD, W_IN_SHARD)]
    return piece


def _w_out_piece(part_ref, k, to):
    ho = W_OUT_SHARD // 2
    return part_ref.at[pl.ds((2 * to[0] + to[1]) * W_OUT_SHARD + to[2] * ho, ho), :]


def _group_piece(part_ref, k, to):
    return part_ref.at[4 * to[0] + 2 * to[1] + to[2]]


def _whole_piece(part_ref, k, to):
    return part_ref


def _exchange_start_call(groups, name):
    ng = len(groups)
    lands = [lax.empty((len(rels),) + slot_shape, part.dtype) for part, rels, _, slot_shape in groups]

    def body(*refs):
        ins, outs = refs[:2 * ng], refs[2 * ng:]
        x, y, c = _coords()
        for g, (_, rels, piece, _) in enumerate(groups):
            part_ref, land_ref = ins[2 * g], ins[2 * g + 1]
            send_sems, recv_sems = outs[4 * g], outs[4 * g + 1]
            for k, (q, cb) in enumerate(rels):
                to = _peer(x, y, c, q, cb)
                pltpu.make_async_remote_copy(src_ref=piece(part_ref, k, to), dst_ref=land_ref.at[k], send_sem=send_sems.at[k],
                                             recv_sem=recv_sems.at[k], device_id=to, device_id_type=MESH).start()
        outs[-1][...] = jnp.zeros_like(outs[-1])

    out_shape, out_specs, operands = [], [], []
    for (part, rels, _, _), land in zip(groups, lands):
        n = len(rels)
        out_shape += [pltpu.SemaphoreType.DMA((n,)), pltpu.SemaphoreType.DMA((n,)), pltpu.HBM(part.shape, part.dtype),
                      pltpu.HBM(land.shape, land.dtype)]
        out_specs += [SEM_SPEC, SEM_SPEC, HBM_SPEC, HBM_SPEC]
        operands += [pltpu.with_memory_space_constraint(part, pltpu.HBM), pltpu.with_memory_space_constraint(land, pltpu.HBM)]
    out = pl.pallas_call(
        body, name=name,
        out_shape=tuple(out_shape) + (jax.ShapeDtypeStruct((1, 1), F32),),
        in_specs=(HBM_SPEC,) * (2 * ng), out_specs=tuple(out_specs) + (pl.BlockSpec(memory_space=pltpu.VMEM),),
        input_output_aliases={j: 4 * (j // 2) + 2 + j % 2 for j in range(2 * ng)},
        compiler_params=pltpu.CompilerParams(has_side_effects=SIDE_EFFECT),
    )(*operands)
    return [tuple(out[4 * g:4 * g + 4]) for g in range(ng)], out[-1]


def _exchange_wait_call(started, groups, after, name):
    ng = len(groups)

    def body(*refs):
        ins = refs[:4 * ng]
        x, y, c = _coords()
        for g, (_, rels, piece, _) in enumerate(groups):
            part_ref, land_ref, send_sems, recv_sems = ins[4 * g:4 * g + 4]
            for k, (q, cb) in enumerate(rels):
                to = _peer(x, y, c, q, cb)
                cp = pltpu.make_async_remote_copy(src_ref=piece(part_ref, k, to), dst_ref=land_ref.at[k], send_sem=send_sems.at[k],
                                                  recv_sem=recv_sems.at[k], device_id=to, device_id_type=MESH)
                cp.wait_send()
                cp.wait_recv()

    operands, in_specs, out_shape = [], [], []
    for send_sems, recv_sems, part_thru, land_thru in started:
        operands += [part_thru, land_thru, send_sems, recv_sems]
        in_specs += [HBM_SPEC, HBM_SPEC, SEM_SPEC, SEM_SPEC]
        out_shape += [pltpu.HBM(part_thru.shape, part_thru.dtype), pltpu.HBM(land_thru.shape, land_thru.dtype)]
    out = pl.pallas_call(
        body, name=name, out_shape=tuple(out_shape),
        in_specs=tuple(in_specs) + (pl.BlockSpec(memory_space=pl.ANY),), out_specs=(HBM_SPEC,) * (2 * ng),
        input_output_aliases={4 * g + j: 2 * g + j for g in range(ng) for j in range(2)},
        compiler_params=pltpu.CompilerParams(has_side_effects=SIDE_EFFECT),
    )(*operands, after)
    return [tuple(out[2 * g:2 * g + 2]) for g in range(ng)]


def _rope_tables(s):
    inv_freq = np.float32(10000.0) ** (-np.arange(0, HEAD, 2, dtype=np.float32) / np.float32(HEAD))
    ang = np.arange(s, dtype=np.float32)[:, None] * inv_freq[None, :]
    cos = np.tile(np.cos(ang), (1, LANE // (HEAD // 2))).astype(np.float32)
    sin = np.tile(np.sin(ang), (1, LANE // (HEAD // 2))).astype(np.float32)
    first_half = (np.arange(LANE) % HEAD) < (HEAD // 2)
    sin = np.where(first_half[None, :], -sin, sin)
    behind = lambda t: np.concatenate([t[:BLK], t[:-BLK]], axis=0)
    return tuple(jnp.asarray(t) for t in (cos, sin, behind(cos), behind(sin)))


def kernel(x, c, w_ada, b_ada, norm_g, w_in, ln_v_g, ln_v_b, w_spatial, b_spatial, sinks, w_out, w_ada_final, b_ada_final, final_norm_g, loss_target, m_w_ada, m_b_ada, m_norm_g, m_w_in, m_ln_v_g, m_ln_v_b, m_w_spatial, m_b_spatial, m_sinks, m_w_out, m_w_ada_final, m_b_ada_final, m_final_norm_g, v_w_ada, v_b_ada, v_norm_g, v_w_in, v_ln_v_g, v_ln_v_b, v_w_spatial, v_b_spatial, v_sinks, v_w_out, v_w_ada_final, v_b_ada_final, v_final_norm_g):
    s = x.shape[1]
    ax, ay, ac = _coords()
    chip = 2 * ax + ay
    me = 4 * ax + 2 * ay + ac
    n_ada = w_ada.shape[2]
    n_adaf = w_ada_final.shape[1]

    x2d = x.reshape(s, D)
    tgt = loss_target.reshape(s, D)
    w_ada2, w_in2, w_out2 = w_ada[0], w_in[0], w_out[0]
    b_ada_f2 = b_ada_final.reshape(1, 2 * D)
    gf = final_norm_g.reshape(1, D)

    c_all, w_in_own = _allgather_sum_call(jnp.pad(c, ((0, 7), (0, 0))), "gather_c", False, cast=(w_in2, (D, D_IN)))
    c_all = c_all[::8]
    mod_p, c_act = _rowmat_call(c_all, w_ada2, lax.dynamic_slice(b_ada, (0, chip * n_ada), (1, n_ada)), "mod")
    modf_p, _ = _rowmat_call(c_all, w_ada_final, lax.dynamic_slice(b_ada_f2, (0, chip * n_adaf), (1, n_adaf)), "mod_final")
    mods, w_out_own = _allgather_sum_call(jnp.concatenate([mod_p, modf_p], axis=1), "gather_mod", False, cast=(w_out2, (D, D)))
    my_rows = [lax.dynamic_slice(mods, (16 * j + me, 0), (1, n_ada + n_adaf)) for j in range(N_CHIP)]
    mod = jnp.concatenate([r[:, :n_ada] for r in my_rows], axis=1)
    mod_f = jnp.concatenate([r[:, n_ada:] for r in my_rows], axis=1)
    shift, scale, gate = mod[:, :D], mod[:, D:2 * D], mod[:, 2 * D:]
    shift_f, scale_f = mod_f[:, :D], mod_f[:, D:]

    pos = jnp.stack([chip, ac]).astype(jnp.int32)

    tables = _rope_tables(s)
    cos, sin = tables[:2]
    b_sp_t = b_spatial[0].T
    sinks1 = sinks.reshape(N_Q)
    h, proj, w_in_bf, w_out_bf = _proj_gather_call(pos, x2d, shift, scale, norm_g, w_in_own, w_out_own)
    y, probs, attn_out, psinks, qms = _mix_fwd_call(proj, cos, sin, ln_v_g, ln_v_b, w_spatial[0], b_sp_t, sinks1)
    dx2, do, dy, st_tail = _tail_call(y, w_out_bf, x2d, tgt, gate, shift_f, scale_f, gf)

    rel_o = [(0, 1), (1, 0), (1, 1), (2, 0), (2, 1), (3, 0), (3, 1)]
    rel_a = [(1, 0), (1, 1), (2, 0), (2, 1)]
    rel_b = [(3, 0), (3, 1), (0, 1)]
    piece_a, piece_b = _w_in_piece([0, 0, 1, 1]), _w_in_piece([0, 0, 1])
    half_in, half_out = (D // 2, W_IN_SHARD), (W_OUT_SHARD // 2, D)

    g_w_out_p = _tn_call(y, do, "grad_w_out")
    grp_o = [(g_w_out_p, rel_o, _w_out_piece, half_out)]
    st_o, tok_o = _exchange_start_call(grp_o, "send_w_out")
    dproj, st_ln, d_wsp, d_bsp_t, d_sink = _mix_bwd_call(
        proj, dy, probs, attn_out, psinks, qms, tables, ln_v_g + tok_o, ln_v_b, w_spatial[0], jnp.swapaxes(w_spatial[0], 1, 2),
        b_sp_t)
    g_w_in_a = _tn_shards_call(pos, h, dproj, (1, 2), "grad_w_in_a")
    grp_a = [(g_w_in_a, rel_a, piece_a, half_in), (d_wsp, rel_o, _group_piece, (BLK, BLK))]
    st_a, tok_a = _exchange_start_call(grp_a, "send_w_in_a")
    g_w_in_b = _tn_shards_call(pos, h, dproj, (3, 0), "grad_w_in_b")
    grp_b = [(g_w_in_b, rel_b, piece_b, half_in)]
    st_b, tok_b = _exchange_start_call(grp_b, "send_w_in_b")
    grad_x, st_dh = _dh_call(dproj, w_in_bf, x2d, dx2, scale + (tok_a + tok_b), norm_g)

    ((g_w_out_p, recv_o),) = _exchange_wait_call(st_o, grp_o, st_dh, "wait_w_out")
    (_, recv_a), (d_wsp, recv_s) = _exchange_wait_call(st_a, grp_a, st_dh, "wait_w_in_a")
    ((g_w_in_b, recv_b),) = _exchange_wait_call(st_b, grp_b, st_dh, "wait_w_in_b")
    mine_in = _sum_pieces_call(pos, g_w_in_b, lambda i, p, nrb: (p[1] * nrb + i, 1), [recv_a, recv_b], "sum_w_in")
    mine_out = _sum_pieces_call(pos, g_w_out_p, lambda i, p, nrb: ((2 * p[0] + p[1]) * nrb + i, 0), [recv_o], "sum_w_out")
    wsp_group = _sum_pieces_call(pos, d_wsp.reshape(GROUPS * BLK, BLK), lambda i, p, nrb: (2 * p[0] + p[1], 0), [recv_s],
                                 "sum_w_spatial")
    to_sibling = [(0, 1)]
    grp_p = [(mine_in, to_sibling, _whole_piece, half_in), (mine_out, to_sibling, _whole_piece, half_out)]
    st_p, tok_p = _exchange_start_call(grp_p, "swap_halves")

    misc = jnp.concatenate([st_ln, d_bsp_t[:, :GROUPS].T, d_sink, jnp.zeros((8, D - D_A - 2 * LANE), F32)], axis=1)
    pack = jnp.concatenate([wsp_group.reshape(8, D) + tok_p, st_tail, st_dh, misc], axis=0)
    rows = pack.shape[0]
    packs, tot = _allgather_sum_call(pack, "gather_small", True)
    packs = packs.reshape(N_DEV, rows, D)
    dmod_all = jnp.concatenate([packs[:, 16, :], packs[:, 17, :], packs[:, 11, :]], axis=1)
    dmodf_all = jnp.concatenate([packs[:, 8, :], packs[:, 9, :]], axis=1)
    loss = tot[13, 0]
    (mine_in, theirs_in), (mine_out, theirs_out) = _exchange_wait_call(st_p, grp_p, tot, "swapped_halves")
    small = {
        "b_ada": jnp.concatenate([tot[16:17], tot[17:18], tot[11:12]], axis=1),
        "norm_g": tot[18:19],
        "ln_v_g": tot[24:25, :D_A],
        "ln_v_b": tot[25:26, :D_A],
        "w_spatial": packs[:, 0:8, :].reshape(GROUPS * BLK, BLK),
        "b_spatial": tot[24:32, D_A:D_A + BLK],
        "sinks": tot[24:25, D_A + LANE:D_A + LANE + N_Q],
        "b_ada_final": jnp.concatenate([tot[8:9], tot[9:10]], axis=1),
        "final_norm_g": tot[10:11],
    }

    weights = dict(w_ada=w_ada, b_ada=b_ada, norm_g=norm_g, w_in=w_in, ln_v_g=ln_v_g, ln_v_b=ln_v_b, w_spatial=w_spatial,
                   b_spatial=b_spatial, sinks=sinks, w_out=w_out, w_ada_final=w_ada_final, b_ada_final=b_ada_final,
                   final_norm_g=final_norm_g)
    m_in = dict(w_ada=m_w_ada, b_ada=m_b_ada, norm_g=m_norm_g, w_in=m_w_in, ln_v_g=m_ln_v_g, ln_v_b=m_ln_v_b,
                w_spatial=m_w_spatial, b_spatial=m_b_spatial, sinks=m_sinks, w_out=m_w_out, w_ada_final=m_w_ada_final,
                b_ada_final=m_b_ada_final, final_norm_g=m_final_norm_g)
    v_in = dict(w_ada=v_w_ada, b_ada=v_b_ada, norm_g=v_norm_g, w_in=v_w_in, ln_v_g=v_ln_v_g, ln_v_b=v_ln_v_b,
                w_spatial=v_w_spatial, b_spatial=v_b_spatial, sinks=v_sinks, w_out=v_w_out, w_ada_final=v_w_ada_final,
                b_ada_final=v_b_ada_final, final_norm_g=v_final_norm_g)
    c_act_t = c_act.T
    outer = {"w_ada": lax.dynamic_slice(dmod_all, (0, chip * n_ada), (N_DEV, n_ada)),
             "w_ada_final": lax.dynamic_slice(dmodf_all, (0, chip * n_adaf), (N_DEV, n_adaf))}
    halves = {"w_in": (mine_in, theirs_in[0]), "w_out": (mine_out, theirs_out[0])}
    done = {}
    for name, (mine, theirs) in halves.items():
        shape2 = (2 * mine.shape[0], mine.shape[1])
        done[name] = _adam_halves_call(pos, weights[name].reshape(shape2), mine, theirs, m_in[name].reshape(shape2),
                                       v_in[name].reshape(shape2), "adam_" + name)
    for name, dm in outer.items():
        shape2 = (D, dm.shape[1])
        done[name] = _adam_outer_call(weights[name].reshape(shape2), c_act_t, dm, m_in[name].reshape(shape2),
                                      v_in[name].reshape(shape2), "adam_" + name)
    updates = _adam_small_call([(weights[name].reshape(g.shape), g, m_in[name].reshape(g.shape), v_in[name].reshape(g.shape))
                                for name, g in small.items()])
    for (name, g), upd in zip(small.items(), updates):
        done[name] = (g, *upd)
    outs = [[done[name][k].reshape(w.shape) for name, w in weights.items()] for k in range(4)]
    return (loss, grad_x.reshape(x.shape), *outs[0], *outs[1], *outs[2], *outs[3])
```

```python
import numpy as np
import jax
import jax.numpy as jnp
from jax import lax
from jax.experimental import pallas as pl
from jax.experimental.pallas import tpu as pltpu

F32 = jnp.float32
BF16 = jnp.bfloat16
MESH = pl.DeviceIdType.MESH

D = 2048
D_A = 1024
D_B = 1024
D_KV = 256
HEAD = 64
N_Q = 16
N_KV = 4
Q_PER_KV = N_Q // N_KV
BLK = 128
GROUPS = 8
D_IN = 5632
OFF_Q, OFF_K, OFF_V, OFF_ZB = 3072, 4096, 4352, 4608
N_CHIP = 4
N_DEV = 8
W_IN_SHARD = D_IN // N_CHIP
W_OUT_SHARD = D // N_CHIP
EPS = 1e-5
SCALE = HEAD ** -0.5
NEG = -1e30
LANE = 128
VMEM_LIMIT = 56 * 1024 * 1024

ADAM_LR, ADAM_B1, ADAM_B2, ADAM_EPS, ADAM_WD, ADAM_STEP = 0.001, 0.9, 0.999, 1e-08, 0.01, 10
ADAM_C1 = 1.0 - ADAM_B1 ** ADAM_STEP
ADAM_C2 = 1.0 - ADAM_B2 ** ADAM_STEP
ADAM_ROWS = 256

NT = (((1,), (1,)), ((), ()))
TN = (((0,), (0,)), ((), ()))


def _params(*sem):
    return pltpu.CompilerParams(dimension_semantics=sem, vmem_limit_bytes=VMEM_LIMIT)


def _silu_parts(z):
    sig = 1.0 / (1.0 + jnp.exp(-z))
    return z * sig, sig


def _swap_halves(v, first_half):
    return jnp.where(first_half, pltpu.roll(v, 96, 1), pltpu.roll(v, 32, 1))


def _rope(v, cos_t, sin_s, first_half):
    return v * cos_t + _swap_halves(v, first_half) * sin_s


def _unrope(dv, cos_t, sin_s, first_half):
    return dv * cos_t - _swap_halves(dv, first_half) * sin_s


def _lane_masks():
    lane = lax.broadcasted_iota(jnp.int32, (BLK, LANE), 1)
    return (lane % HEAD) < (HEAD // 2), lane < HEAD


def _band_valid(first_block_bound, rows=BLK):
    rr = lax.broadcasted_iota(jnp.int32, (rows, 2 * BLK), 0) & (BLK - 1)
    jj = lax.broadcasted_iota(jnp.int32, (rows, 2 * BLK), 1)
    return (jj > rr) & (jj <= rr + BLK) & (jj >= first_block_bound)


def _dup_kv(slab, lo):
    rolled = pltpu.roll(slab, HEAD, 1)
    return jnp.where(lo, slab, rolled).astype(BF16), jnp.where(lo, rolled, slab).astype(BF16)


def _fold_halves(a, b, lo):
    return jnp.where(lo, a, b) + pltpu.roll(jnp.where(lo, b, a), HEAD, 1)


def _stack_heads(ref, sb, slab, lo, dtype):
    kh, base = sb // 2, 2 * (sb % 2) * BLK
    zero = jnp.zeros_like(slab)
    ref[kh, base:base + BLK, :] = jnp.where(lo, slab, zero).astype(dtype)
    ref[kh, base + BLK:base + 2 * BLK, :] = jnp.where(lo, zero, slab).astype(dtype)


def _unstack_heads(ref, sb, lo):
    kh, base = sb // 2, 2 * (sb % 2) * BLK
    return jnp.where(lo, ref[kh, base:base + BLK, :], ref[kh, base + BLK:base + 2 * BLK, :])


def _sink_column(sinks_ref, kh):
    row = lax.broadcasted_iota(jnp.int32, (Q_PER_KV * BLK, 1), 0)
    col = jnp.full(row.shape, sinks_ref[Q_PER_KV * kh + Q_PER_KV - 1], F32)
    for n in range(Q_PER_KV - 2, -1, -1):
        col = jnp.where(row < (n + 1) * BLK, sinks_ref[Q_PER_KV * kh + n], col)
    return col


def _tril():
    t = lax.broadcasted_iota(jnp.int32, (BLK, BLK), 0)
    s = lax.broadcasted_iota(jnp.int32, (BLK, BLK), 1)
    return s <= t


def _layer_norm_fwd(va, lg, lb):
    mu = jnp.mean(va, axis=-1, keepdims=True)
    xc = va - mu
    rstd = lax.rsqrt(jnp.mean(xc * xc, axis=-1, keepdims=True) + EPS)
    vhat = xc * rstd
    return vhat, rstd, vhat * lg + lb


def _softmax_sink(qm, kdup, bias, sink):
    s = lax.dot_general(qm, kdup, NT, preferred_element_type=F32) + bias
    m = jnp.maximum(jnp.max(s, axis=-1, keepdims=True), sink)
    p = jnp.exp(s - m)
    esink = jnp.exp(sink - m)
    inv = 1.0 / (jnp.sum(p, axis=-1, keepdims=True) + esink)
    return p * inv, esink * inv


def _band_bias(bias_ref):
    rows = bias_ref.shape[1]
    bias_ref[0] = jnp.where(_band_valid(BLK, rows), 0.0, NEG)
    bias_ref[1] = jnp.where(_band_valid(0, rows), 0.0, NEG)


def _rowmat_call(c_all, w, b, name):
    n = w.shape[1]
    tn = 512

    def body(c_ref, w_ref, b_ref, o_ref, ca_ref):
        ca, _ = _silu_parts(c_ref[...])
        ca_ref[...] = ca
        o_ref[...] = jnp.dot(ca.astype(BF16), w_ref[...].astype(BF16), preferred_element_type=F32) + b_ref[...]

    return pl.pallas_call(
        body, name=name, grid=(n // tn,),
        in_specs=[pl.BlockSpec((N_DEV, D), lambda j: (0, 0)), pl.BlockSpec((D, tn), lambda j: (0, j)),
                  pl.BlockSpec((1, tn), lambda j: (0, j))],
        out_specs=[pl.BlockSpec((N_DEV, tn), lambda j: (0, j)), pl.BlockSpec((N_DEV, D), lambda j: (0, 0))],
        out_shape=[jax.ShapeDtypeStruct((N_DEV, n), F32), jax.ShapeDtypeStruct((N_DEV, D), F32)],
        compiler_params=_params("arbitrary"),
    )(c_all, w, b)


W_IN_PARTS = ((0, 768), (768, 640))
OUT_STREAMS = 4
X_STREAMS = 4


def _proj_gather_call(pos, x, shift, scale, norm_g, wi_full, wo_full):
    s = x.shape[0]
    tm = min(s, 512)
    nrow = s // tm
    hi = D // 2
    ho = W_OUT_SHARD // 2
    phases = [(0, None), (1, 0), (2, 0), (1, 1), (2, 1), (3, 0), (3, 1)]

    def body(pos_ref, *refs):
        x_refs = refs[:X_STREAMS]
        (sh_ref, sc_ref, g_ref, _, _, h_ref, proj_ref, fi_ref, fo_ref,
         h_all, wbuf, obuf, send_sems, recv_sems, load_sems, out_sems) = refs[X_STREAMS:]
        p = pl.program_id(0)
        i = pl.program_id(1)
        x_, y_, c_ = _coords()
        me, sibling = (x_, y_, c_), (x_, y_, 1 - c_)

        def shard_of(q):
            px, py, _ = _peer(x_, y_, c_, q, 0)
            return 2 * px + py

        def cols_of(q, cp):
            off, w = (0, W_IN_SHARD) if cp is None else W_IN_PARTS[cp]
            return shard_of(q) * W_IN_SHARD + off, w

        def part(which, q, pc, sub, cp):
            n = hi if which == 0 else ho
            base = pc * n
            if sub is not None:
                n //= 2
                base = base + sub * n
            if which == 0:
                c0, w = cols_of(q, cp)
                return fi_ref.at[pl.ds(base, n), pl.ds(c0, w)]
            return fo_ref.at[pl.ds(shard_of(q) * W_OUT_SHARD + base, n), :]

        def copy(k, ref, to):
            return pltpu.make_async_remote_copy(src_ref=ref, dst_ref=ref, send_sem=send_sems.at[k], recv_sem=recv_sems.at[k],
                                                device_id=to, device_id_type=MESH)

        def sem(which, kind, j, cp):
            return 4 * kind + 2 * cp + j if which == 0 else 16 + 2 * kind + j

        def to_neighbour(which, q, cp=None):
            return copy(sem(which, 0, q - 1, cp), part(which, 0, c_, None, cp), _peer(x_, y_, c_, q, 0))

        def from_neighbour(which, q, cp=None):
            return copy(sem(which, 0, q - 1, cp), part(which, q, c_, None, cp), me)

        def relay(which, q, cp=None):
            return copy(sem(which, 1, q - 1, cp), part(which, q, c_, q - 1, cp), _peer(x_, y_, c_, 3 - q, 0))

        def relayed(which, sub, cp=None):
            return copy(sem(which, 1, sub, cp), part(which, 3, c_, sub, cp), me)

        def to_sibling(which, q, cp=None):
            return copy(sem(which, 2, q - 1, cp), part(which, q, c_, None, cp), sibling)

        def from_sibling(which, q, cp=None):
            return copy(sem(which, 2, q - 1, cp), part(which, q, 1 - c_, None, cp), me)

        def relayed_to_sibling(which, sub, cp=None):
            return copy(sem(which, 3, sub, cp), part(which, 3, c_, sub, cp), sibling)

        def relayed_from_sibling(which, sub, cp=None):
            return copy(sem(which, 3, sub, cp), part(which, 3, 1 - c_, sub, cp), me)

        def pass_on_neighbours(which, cp=None):
            for q in (1, 2):
                from_neighbour(which, q, cp).wait_recv()
                to_sibling(which, q, cp).start()
                relay(which, q, cp).start()

        def pass_on_relayed(which, cp=None):
            for sub in range(2):
                relayed(which, sub, cp).wait_recv()
                relayed_to_sibling(which, sub, cp).start()

        def shard_load(k):
            c0, w = cols_of(*phases[k])
            return pltpu.make_async_copy(fi_ref.at[:, pl.ds(c0, w)], wbuf.at[k % 2, :, 0:w], load_sems.at[k % 2])

        class OutCopies:
            def __init__(self, k, slot, row0):
                c0, w = cols_of(*phases[k])
                strip = tm // OUT_STREAMS
                self.copies = [pltpu.make_async_copy(obuf.at[slot, n * strip:(n + 1) * strip, 0:w],
                                                     proj_ref.at[pl.ds(row0 + n * strip, strip), pl.ds(c0, w)],
                                                     out_sems.at[slot, n]) for n in range(OUT_STREAMS)]

            def start(self):
                for cp in self.copies:
                    cp.start()

            def wait(self):
                for cp in self.copies:
                    cp.wait()

        out_copy = OutCopies

        def drain(k):
            for j in range(min(2, nrow)):
                out_copy(k, (nrow - 1 - j) % 2, 0).wait()

        def arrivals(k):
            q, cp = phases[k]
            if k == 0:
                for cp_ in range(2):
                    for q_ in (1, 2):
                        to_neighbour(0, q_, cp_).start()
            elif q < 3 and k in (1, 3):
                pass_on_neighbours(0, cp)
                if k == 1:
                    for q_ in (1, 2):
                        to_neighbour(1, q_).start()
            elif k == 5:
                for cp_ in range(2):
                    pass_on_relayed(0, cp_)
                pass_on_neighbours(1)
            if q in (1, 2):
                from_sibling(0, q, cp).wait_recv()
            elif q == 3:
                for sub in range(2):
                    relayed_from_sibling(0, sub, cp).wait_recv()

        rows = pl.ds(pl.multiple_of(i * tm, tm), tm)
        slot = i % 2
        for k, (q, cp) in enumerate(phases):
            @pl.when(p == k)
            def _(k=k, q=q, cp=cp):
                @pl.when(i == 0)
                def _():
                    if k == 0:
                        arrivals(0)
                        shard_load(0).start()
                    else:
                        drain(k - 1)
                    shard_load(k).wait()

                if k + 1 < len(phases):
                    @pl.when(i == max(nrow - 2, 0))
                    def _():
                        arrivals(k + 1)
                        shard_load(k + 1).start()

                if k == 0:
                    wx = D // X_STREAMS
                    ssq = sum(jnp.sum(xr[...] * xr[...], axis=-1, keepdims=True) for xr in x_refs)
                    r = lax.rsqrt(ssq * (1.0 / D) + EPS)
                    for n, xr in enumerate(x_refs):
                        cols = slice(n * wx, (n + 1) * wx)
                        hv = ((xr[...] * r * g_ref[:, cols]) * (1.0 + sc_ref[:, cols]) + sh_ref[:, cols]).astype(BF16)
                        h_ref[:, cols] = hv
                        h_all[rows, cols] = hv

                @pl.when(i >= 2)
                def _():
                    out_copy(k, slot, 0).wait()

                w = cols_of(q, cp)[1]
                obuf[slot, :, 0:w] = jnp.dot(h_all[rows, :], wbuf[k % 2, :, 0:w], preferred_element_type=F32)
                out_copy(k, slot, pl.multiple_of(i * tm, tm)).start()

        @pl.when((p == len(phases) - 1) & (i == nrow - 1))
        def _():
            drain(len(phases) - 1)
            pass_on_relayed(1)
            for q in (1, 2):
                from_sibling(1, q).wait_recv()
            for sub in range(2):
                relayed_from_sibling(1, sub).wait_recv()
            for which, cps in ((0, (0, 1)), (1, (None,))):
                for cp in cps:
                    for q in (1, 2):
                        to_neighbour(which, q, cp).wait_send()
                        relay(which, q, cp).wait_send()
                        to_sibling(which, q, cp).wait_send()
                        relayed_to_sibling(which, q - 1, cp).wait_send()

    vec = pl.BlockSpec((1, D), lambda p, i, pos: (0, 0))
    first_phase_rows = lambda p, i, pos: (jnp.where(p == 0, i, nrow - 1), 0)
    anyspec = pl.BlockSpec(memory_space=pl.ANY)
    x_spec = lambda n: pl.BlockSpec((tm, D // X_STREAMS), lambda p, i, pos: (jnp.where(p == 0, i, nrow - 1), n))
    return pl.pallas_call(
        body, name="proj_gather",
        grid_spec=pltpu.PrefetchScalarGridSpec(
            num_scalar_prefetch=1, grid=(len(phases), nrow),
            in_specs=[x_spec(n) for n in range(X_STREAMS)] + [vec, vec, vec, anyspec, anyspec],
            out_specs=[pl.BlockSpec((tm, D), first_phase_rows), anyspec, anyspec, anyspec],
            scratch_shapes=[pltpu.VMEM((s, D), BF16), pltpu.VMEM((2, D, W_IN_SHARD), BF16), pltpu.VMEM((2, tm, W_IN_SHARD), F32),
                            pltpu.SemaphoreType.DMA((24,)), pltpu.SemaphoreType.DMA((24,)), pltpu.SemaphoreType.DMA((2,)),
                            pltpu.SemaphoreType.DMA((2, OUT_STREAMS))]),
        out_shape=[jax.ShapeDtypeStruct((s, D), BF16), jax.ShapeDtypeStruct((s, D_IN), F32),
                   jax.ShapeDtypeStruct((D, D_IN), BF16), jax.ShapeDtypeStruct((D, D), BF16)],
        input_output_aliases={X_STREAMS + 4: 2, X_STREAMS + 5: 3},
        compiler_params=_params("arbitrary", "arbitrary"),
    )(pos, *([x] * X_STREAMS), shift, scale, norm_g, wi_full, wo_full)


def _proj_specs(rev_nb=None, with_q=True):
    if rev_nb is None:
        row = lambda i: i
    else:
        row = lambda i: rev_nb - 1 - i
    wide = lambda col: pl.BlockSpec((BLK, D_A), lambda i: (row(i), col))
    kv = lambda col: pl.BlockSpec((BLK, D_KV), lambda i: (row(i), col))
    half = lambda col: pl.BlockSpec((BLK, 512), lambda i: (row(i), col))
    return ([wide(0), wide(1), wide(2)] + ([wide(3)] if with_q else [])
            + [kv(OFF_K // D_KV), kv(OFF_V // D_KV), half(OFF_ZB // 512), half(OFF_ZB // 512 + 1)])


def _mix_fwd_call(proj, cos, sin, ln_g, ln_b, w_sp, b_sp_t, sinks):
    s = proj.shape[0]
    nb = s // BLK

    def body(ua_ref, va_ref, za_ref, q_ref, k_ref, v_ref, zb0_ref, zb1_ref, cos_ref, sin_ref, lg_ref, lb_ref,
             w_ref, bt_ref, sinks_ref, y_ref, probs_ref, ost_ref, psink_ref, qm_ref, kdup_ref, vdup_ref, bias_ref):
        i = pl.program_id(0)
        first_half, lo = _lane_masks()
        cos_t = cos_ref[...]
        sin_t = sin_ref[...]

        @pl.when(i == 0)
        def _():
            kdup_ref[:, 0:BLK, :] = jnp.zeros((N_KV, BLK, LANE), BF16)
            vdup_ref[:, 0:BLK, :] = jnp.zeros((N_KV, BLK, LANE), BF16)
            _band_bias(bias_ref)

        @pl.when(i > 0)
        def _():
            kdup_ref[:, 0:BLK, :] = kdup_ref[:, BLK:2 * BLK, :]
            vdup_ref[:, 0:BLK, :] = vdup_ref[:, BLK:2 * BLK, :]

        _, _, vln = _layer_norm_fwd(va_ref[...], lg_ref[...], lb_ref[...])
        tril = _tril()
        for g in range(GROUPS):
            cols = slice(g * BLK, (g + 1) * BLK)
            wg = jnp.where(tril, w_ref[g], 0.0).astype(BF16)
            sg = jnp.dot(wg, vln[:, cols].astype(BF16), preferred_element_type=F32) + bt_ref[:, g:g + 1]
            gate, _ = _silu_parts(za_ref[:, cols])
            y_ref[:, cols] = (ua_ref[:, cols] * sg * gate).astype(BF16)

        for ks in range(2):
            cols = slice(ks * LANE, (ks + 1) * LANE)
            kr = _rope(k_ref[:, cols], cos_t, sin_t, first_half)
            for n, (kd, vd) in enumerate(zip(_dup_kv(kr, lo), _dup_kv(v_ref[:, cols], lo))):
                kdup_ref[2 * ks + n, BLK:2 * BLK, :] = kd
                vdup_ref[2 * ks + n, BLK:2 * BLK, :] = vd
        for sb in range(8):
            _stack_heads(qm_ref, sb, _rope(q_ref[:, sb * LANE:(sb + 1) * LANE], cos_t, sin_t, first_half) * SCALE, lo, BF16)

        block_kind = jnp.where(i > 0, 1, 0)

        psink_ref[...] = jnp.zeros((Q_PER_KV * BLK, LANE), F32)
        lane_q = lax.broadcasted_iota(jnp.int32, (Q_PER_KV * BLK, LANE), 1)

        def kv_head(kh, carry):
            probs, psink = _softmax_sink(qm_ref[kh], kdup_ref[kh], bias_ref[block_kind], _sink_column(sinks_ref, kh))
            probs_ref[kh] = probs
            psink_ref[...] = jnp.where(lane_q == kh, psink, psink_ref[...])
            ost_ref[kh] = jnp.dot(probs.astype(BF16), vdup_ref[kh], preferred_element_type=F32)
            return carry

        for kh in range(N_KV):
            kv_head(kh, 0)
        for sb in range(8):
            cols = slice(sb * LANE, (sb + 1) * LANE)
            zb = zb0_ref[:, cols] if sb < 4 else zb1_ref[:, (sb - 4) * LANE:(sb - 3) * LANE]
            gate, _ = _silu_parts(zb)
            y_ref[:, D_A + sb * LANE:D_A + (sb + 1) * LANE] = (_unstack_heads(ost_ref, sb, lo) * gate).astype(BF16)

    tab = pl.BlockSpec((BLK, LANE), lambda i: (i, 0))
    return pl.pallas_call(
        body, name="mix_fwd", grid=(nb,),
        in_specs=_proj_specs() + [
            tab, tab, pl.BlockSpec((1, D_A), lambda i: (0, 0)), pl.BlockSpec((1, D_A), lambda i: (0, 0)),
            pl.BlockSpec((GROUPS, BLK, BLK), lambda i: (0, 0, 0)), pl.BlockSpec((BLK, GROUPS), lambda i: (0, 0)),
            pl.BlockSpec(memory_space=pltpu.SMEM)],
        out_specs=[pl.BlockSpec((BLK, 2 * D_A), lambda i: (i, 0)),
                   pl.BlockSpec((None, N_KV, Q_PER_KV * BLK, 2 * BLK), lambda i: (i, 0, 0, 0)),
                   pl.BlockSpec((None, N_KV, Q_PER_KV * BLK, LANE), lambda i: (i, 0, 0, 0)),
                   pl.BlockSpec((None, Q_PER_KV * BLK, LANE), lambda i: (i, 0, 0)),
                   pl.BlockSpec((None, N_KV, Q_PER_KV * BLK, LANE), lambda i: (i, 0, 0, 0))],
        out_shape=[jax.ShapeDtypeStruct((s, 2 * D_A), BF16), jax.ShapeDtypeStruct((nb, N_KV, Q_PER_KV * BLK, 2 * BLK), F32),
                   jax.ShapeDtypeStruct((nb, N_KV, Q_PER_KV * BLK, LANE), F32), jax.ShapeDtypeStruct((nb, Q_PER_KV * BLK, LANE), F32),
                   jax.ShapeDtypeStruct((nb, N_KV, Q_PER_KV * BLK, LANE), BF16)],
        scratch_shapes=[pltpu.VMEM((N_KV, 2 * BLK, LANE), BF16), pltpu.VMEM((N_KV, 2 * BLK, LANE), BF16),
                        pltpu.VMEM((2, Q_PER_KV * BLK, 2 * BLK), F32)],
        compiler_params=_params("arbitrary"),
    )(proj, proj, proj, proj, proj, proj, proj, proj, cos, sin, ln_g, ln_b, w_sp, b_sp_t, sinks)


def _tail_call(y, w_out_bf, x, target, gate, shift_f, scale_f, gf):
    s = x.shape[0]
    tm = min(s, 256)
    nsteps = s // tm

    def body(y_ref, w_ref, x_ref, t_ref, gate_ref, shf_ref, scf_ref, gf_ref, dx2_ref, do_ref, dy_ref, st_ref):
        i = pl.program_id(0)

        @pl.when(i == 0)
        def _():
            st_ref[...] = jnp.zeros((8, D), F32)

        o = jnp.dot(y_ref[...], w_ref[...], preferred_element_type=F32)
        gate_v = gate_ref[...]
        x2 = x_ref[...] + gate_v * o
        r2 = lax.rsqrt(jnp.mean(x2 * x2, axis=-1, keepdims=True) + EPS)
        xn2 = x2 * r2
        hn2 = xn2 * gf_ref[...]
        one_sc = 1.0 + scf_ref[...]
        err = hn2 * one_sc + shf_ref[...] - t_ref[...]
        dout = err * (1.0 / D)
        dhn2 = dout * one_sc
        dxn2 = dhn2 * gf_ref[...]
        dx2 = r2 * (dxn2 - xn2 * jnp.mean(dxn2 * xn2, axis=-1, keepdims=True))
        dx2_ref[...] = dx2
        do = (dx2 * gate_v).astype(BF16)
        do_ref[...] = do
        dy_ref[...] = lax.dot_general(do, w_ref[...], NT, preferred_element_type=F32)
        st_ref[0:1, :] += jnp.sum(dout, axis=0, keepdims=True)
        st_ref[1:2, :] += jnp.sum(dout * hn2, axis=0, keepdims=True)
        st_ref[2:3, :] += jnp.sum(dhn2 * xn2, axis=0, keepdims=True)
        st_ref[3:4, :] += jnp.sum(dx2 * o, axis=0, keepdims=True)
        st_ref[4:5, :] += jnp.sum(err * err, axis=0, keepdims=True)

        @pl.when(i == nsteps - 1)
        def _():
            st_ref[5:6, :] = jnp.full((1, D), 0.5 / D, F32) * jnp.sum(st_ref[4:5, :])

    vec = pl.BlockSpec((1, D), lambda i: (0, 0))
    rows = lambda: pl.BlockSpec((tm, D), lambda i: (i, 0))
    return pl.pallas_call(
        body, name="tail", grid=(nsteps,),
        in_specs=[rows(), pl.BlockSpec((D, D), lambda i: (0, 0)), rows(), rows(), vec, vec, vec, vec],
        out_specs=[rows(), rows(), rows(), pl.BlockSpec((8, D), lambda i: (0, 0))],
        out_shape=[jax.ShapeDtypeStruct((s, D), F32), jax.ShapeDtypeStruct((s, D), BF16), jax.ShapeDtypeStruct((s, D), F32),
                   jax.ShapeDtypeStruct((8, D), F32)],
        compiler_params=_params("arbitrary"),
    )(y, w_out_bf, x, target, gate, shift_f, scale_f, gf)


def _tn_call(a, b, name):
    s, m = a.shape
    n = b.shape[1]
    tn = 1024
    ts = min(s, 1024)
    nk = s // ts

    def body(a_ref, b_ref, o_ref, acc_ref):
        k = pl.program_id(1)

        @pl.when(k == 0)
        def _():
            acc_ref[...] = jnp.zeros((m, tn), F32)

        acc_ref[...] += lax.dot_general(a_ref[...], b_ref[...], TN, preferred_element_type=F32)

        @pl.when(k == nk - 1)
        def _():
            o_ref[...] = acc_ref[...].astype(BF16)

    return pl.pallas_call(
        body, name=name, grid=(n // tn, nk),
        in_specs=[pl.BlockSpec((ts, m), lambda j, k: (k, 0)), pl.BlockSpec((ts, tn), lambda j, k: (k, j))],
        out_specs=pl.BlockSpec((m, tn), lambda j, k: (0, j)),
        out_shape=jax.ShapeDtypeStruct((m, n), BF16),
        scratch_shapes=[pltpu.VMEM((m, tn), F32)],
        compiler_params=_params("parallel", "arbitrary"),
    )(a, b)


def _tn_shards_call(pos, a, b, qs, name):
    s, m = a.shape
    ts = min(s, 1024)
    nk = s // ts

    def body(pos_ref, a_ref, b_ref, o_ref, acc_ref):
        k = pl.program_id(1)

        @pl.when(k == 0)
        def _():
            acc_ref[...] = jnp.zeros((m, W_IN_SHARD), F32)

        acc_ref[...] += lax.dot_general(a_ref[...], b_ref[...], TN, preferred_element_type=F32)

        @pl.when(k == nk - 1)
        def _():
            o_ref[...] = acc_ref[...].astype(BF16)

    def shard(j, pos):
        q = qs[0]
        for n in range(1, len(qs)):
            q = jnp.where(j == n, qs[n], q)
        return jnp.bitwise_xor(pos[0], q)

    return pl.pallas_call(
        body, name=name,
        grid_spec=pltpu.PrefetchScalarGridSpec(
            num_scalar_prefetch=1, grid=(len(qs), nk),
            in_specs=[pl.BlockSpec((ts, m), lambda j, k, pos: (k, 0)),
                      pl.BlockSpec((ts, W_IN_SHARD), lambda j, k, pos: (k, shard(j, pos)))],
            out_specs=pl.BlockSpec((m, W_IN_SHARD), lambda j, k, pos: (0, j)),
            scratch_shapes=[pltpu.VMEM((m, W_IN_SHARD), F32)]),
        out_shape=jax.ShapeDtypeStruct((m, len(qs) * W_IN_SHARD), BF16),
        compiler_params=_params("parallel", "arbitrary"),
    )(pos, a, b)


def _mix_bwd_call(proj, dy, probs, outs, psinks, qms, tables, ln_g, ln_b, w_sp, w_sp_t, b_sp_t):
    s = proj.shape[0]
    nb = s // BLK
    rev = lambda i: nb - 1 - i
    prev = lambda i: jnp.maximum(nb - 2 - i, 0)

    def body(ua_ref, va_ref, za_ref, k_ref, v_ref, zb0_ref, zb1_ref, kp_ref, vp_ref, dy_ref,
             probs_ref, ost_ref, psink_ref, qm_ref, cos_ref, sin_ref, cosp_ref, sinp_ref, lg_ref, lb_ref, w_ref, wt_ref, bt_ref,
             dp_ref, lnst_ref, dw_ref, dbt_ref, dsink_ref,
             kdup_ref, vdup_ref, dvln_ref, dom_ref, dqst_ref, dkdup_ref, dvdup_ref, kcar_ref, vcar_ref, sigb_ref):
        i = pl.program_id(0)
        first_half, lo = _lane_masks()
        lane8 = lax.broadcasted_iota(jnp.int32, (8, LANE), 1)
        cos_t = cos_ref[...]
        sin_t = sin_ref[...]

        @pl.when(i == 0)
        def _():
            lnst_ref[...] = jnp.zeros((8, D_A), F32)
            dw_ref[...] = jnp.zeros((GROUPS, BLK, BLK), F32)
            dbt_ref[...] = jnp.zeros((BLK, LANE), F32)
            dsink_ref[...] = jnp.zeros((8, LANE), F32)
            kcar_ref[...] = jnp.zeros((BLK, D_KV), F32)
            vcar_ref[...] = jnp.zeros((BLK, D_KV), F32)

        vhat, rstd, vln = _layer_norm_fwd(va_ref[...], lg_ref[...], lb_ref[...])
        tril = _tril()
        triu = jnp.logical_not(tril) | (lax.broadcasted_iota(jnp.int32, (BLK, BLK), 0) == lax.broadcasted_iota(jnp.int32, (BLK, BLK), 1))
        lane_b = lax.broadcasted_iota(jnp.int32, (BLK, LANE), 1)
        db_acc = jnp.zeros((BLK, LANE), F32)
        for g in range(GROUPS):
            cols = slice(g * BLK, (g + 1) * BLK)
            vln_g = vln[:, cols].astype(BF16)
            wg = jnp.where(tril, w_ref[g], 0.0).astype(BF16)
            sg = jnp.dot(wg, vln_g, preferred_element_type=F32) + bt_ref[:, g:g + 1]
            za = za_ref[:, cols]
            gate, sig = _silu_parts(za)
            ua = ua_ref[:, cols]
            dya_g = dy_ref[:, cols]
            dya = dya_g * gate
            dp_ref[:, cols] = (dya * sg).astype(BF16)
            dp_ref[:, 2 * D_A + g * BLK:2 * D_A + (g + 1) * BLK] = (
                dya_g * (ua * sg) * (sig * (1.0 + za * (1.0 - sig)))).astype(BF16)
            ds = dya * ua
            ds_b = ds.astype(BF16)
            wtg = jnp.where(triu, wt_ref[g], 0.0).astype(BF16)
            dvln_ref[:, cols] = jnp.dot(wtg, ds_b, preferred_element_type=F32)
            dw_ref[g] += jnp.where(tril, lax.dot_general(ds_b, vln_g, NT, preferred_element_type=F32), 0.0)
            db_acc = db_acc + jnp.where(lane_b == g, jnp.sum(ds, axis=-1, keepdims=True), 0.0)
        dbt_ref[...] += db_acc
        dvln = dvln_ref[...]
        lnst_ref[0:1, :] += jnp.sum(dvln * vhat, axis=0, keepdims=True)
        lnst_ref[1:2, :] += jnp.sum(dvln, axis=0, keepdims=True)
        dvhat = dvln * lg_ref[...]
        m1 = jnp.mean(dvhat, axis=-1, keepdims=True)
        m2 = jnp.mean(dvhat * vhat, axis=-1, keepdims=True)
        dp_ref[:, D_A:2 * D_A] = (rstd * (dvhat - m1 - vhat * m2)).astype(BF16)

        cosp = cosp_ref[...]
        sinp = sinp_ref[...]
        for ks in range(2):
            cols = slice(ks * LANE, (ks + 1) * LANE)
            kr = _rope(k_ref[:, cols], cos_t, sin_t, first_half)
            kpr = _rope(kp_ref[:, cols], cosp, sinp, first_half)
            for n, (kc, vc, kp, vp) in enumerate(zip(_dup_kv(kr, lo), _dup_kv(v_ref[:, cols], lo),
                                                     _dup_kv(kpr, lo), _dup_kv(vp_ref[:, cols], lo))):
                kdup_ref[2 * ks + n, BLK:2 * BLK, :] = kc
                vdup_ref[2 * ks + n, BLK:2 * BLK, :] = vc
                kdup_ref[2 * ks + n, 0:BLK, :] = kp
                vdup_ref[2 * ks + n, 0:BLK, :] = vp
        for sb in range(8):
            cols = slice(sb * LANE, (sb + 1) * LANE)
            zb = zb0_ref[:, cols] if sb < 4 else zb1_ref[:, (sb - 4) * LANE:(sb - 3) * LANE]
            gate, sig = _silu_parts(zb)
            sigb_ref[:, cols] = sig
            _stack_heads(dom_ref, sb, dy_ref[:, D_A + sb * LANE:D_A + (sb + 1) * LANE] * gate, lo, F32)

        lane_q = lax.broadcasted_iota(jnp.int32, (Q_PER_KV * BLK, LANE), 1)

        def kv_head(kh, sink_acc):
            qm = qm_ref[kh]
            kd = kdup_ref[kh]
            vd = vdup_ref[kh]
            probs = probs_ref[kh]
            probs_b = probs.astype(BF16)
            o = ost_ref[kh]
            dom = dom_ref[kh]
            dom_b = dom.astype(BF16)
            delta = jnp.sum(dom * o, axis=-1, keepdims=True)
            dpr = lax.dot_general(dom_b, vd, NT, preferred_element_type=F32)
            dss = (probs * (dpr - delta)).astype(BF16)
            sink_acc = sink_acc + jnp.where(lane_q == kh, psink_ref[...] * delta, 0.0)
            dqst_ref[kh] = jnp.dot(dss, kd, preferred_element_type=F32)
            dkdup_ref[kh] = lax.dot_general(dss, qm, TN, preferred_element_type=F32)
            dvdup_ref[kh] = lax.dot_general(probs_b, dom_b, TN, preferred_element_type=F32)
            return sink_acc

        sink_acc = jnp.zeros((Q_PER_KV * BLK, LANE), F32)
        for kh in range(N_KV):
            sink_acc = kv_head(kh, sink_acc)
        lane1 = lax.broadcasted_iota(jnp.int32, (1, LANE), 1)
        dsink_acc = jnp.zeros((8, LANE), F32)
        for n in range(Q_PER_KV):
            col = jnp.sum(sink_acc[n * BLK:(n + 1) * BLK], axis=0, keepdims=True)
            for kh in range(N_KV):
                dsink_acc = dsink_acc + jnp.where(lane8 == Q_PER_KV * kh + n, -jnp.sum(jnp.where(lane1 == kh, col, 0.0)), 0.0)
        row0 = lax.broadcasted_iota(jnp.int32, (8, LANE), 0) == 0
        dsink_ref[...] += jnp.where(row0, dsink_acc, 0.0)

        for sb in range(8):
            cols = slice(sb * LANE, (sb + 1) * LANE)
            zb = zb0_ref[:, cols] if sb < 4 else zb1_ref[:, (sb - 4) * LANE:(sb - 3) * LANE]
            sig = sigb_ref[:, cols]
            dyb = dy_ref[:, D_A + sb * LANE:D_A + (sb + 1) * LANE]
            dp_ref[:, OFF_ZB + sb * LANE:OFF_ZB + (sb + 1) * LANE] = (
                dyb * _unstack_heads(ost_ref, sb, lo) * (sig * (1.0 + zb * (1.0 - sig)))).astype(BF16)
            dq_r = _unstack_heads(dqst_ref, sb, lo) * SCALE
            dp_ref[:, OFF_Q + sb * LANE:OFF_Q + (sb + 1) * LANE] = _unrope(dq_r, cos_t, sin_t, first_half).astype(BF16)

        lo2 = lax.broadcasted_iota(jnp.int32, (2 * BLK, LANE), 1) < HEAD
        for ks in range(2):
            cols = slice(ks * LANE, (ks + 1) * LANE)
            dk_band = _fold_halves(dkdup_ref[2 * ks], dkdup_ref[2 * ks + 1], lo2)
            dv_band = _fold_halves(dvdup_ref[2 * ks], dvdup_ref[2 * ks + 1], lo2)
            dkr = dk_band[BLK:2 * BLK, :] + kcar_ref[:, cols]
            dp_ref[:, OFF_K + ks * LANE:OFF_K + (ks + 1) * LANE] = _unrope(dkr, cos_t, sin_t, first_half).astype(BF16)
            dp_ref[:, OFF_V + ks * LANE:OFF_V + (ks + 1) * LANE] = (
                dv_band[BLK:2 * BLK, :] + vcar_ref[:, cols]).astype(BF16)
            kcar_ref[:, cols] = dk_band[0:BLK, :]
            vcar_ref[:, cols] = dv_band[0:BLK, :]

    tab = pl.BlockSpec((BLK, LANE), lambda i: (rev(i), 0))
    kvp = lambda col: pl.BlockSpec((BLK, D_KV), lambda i: (prev(i), col))
    vec = pl.BlockSpec((1, D_A), lambda i: (0, 0))
    w3 = pl.BlockSpec((GROUPS, BLK, BLK), lambda i: (0, 0, 0))
    return pl.pallas_call(
        body, name="mix_bwd", grid=(nb,),
        in_specs=_proj_specs(nb, with_q=False) + [
            kvp(OFF_K // D_KV), kvp(OFF_V // D_KV), pl.BlockSpec((BLK, 2 * D_A), lambda i: (rev(i), 0)),
            pl.BlockSpec((None, N_KV, Q_PER_KV * BLK, 2 * BLK), lambda i: (rev(i), 0, 0, 0)),
            pl.BlockSpec((None, N_KV, Q_PER_KV * BLK, LANE), lambda i: (rev(i), 0, 0, 0)),
            pl.BlockSpec((None, Q_PER_KV * BLK, LANE), lambda i: (rev(i), 0, 0)),
            pl.BlockSpec((None, N_KV, Q_PER_KV * BLK, LANE), lambda i: (rev(i), 0, 0, 0)),
            tab, tab, tab, tab, vec, vec, w3, w3, pl.BlockSpec((BLK, GROUPS), lambda i: (0, 0))],
        out_specs=[pl.BlockSpec((BLK, D_IN), lambda i: (rev(i), 0)), pl.BlockSpec((8, D_A), lambda i: (0, 0)), w3,
                   pl.BlockSpec((BLK, LANE), lambda i: (0, 0)), pl.BlockSpec((8, LANE), lambda i: (0, 0))],
        out_shape=[jax.ShapeDtypeStruct((s, D_IN), BF16), jax.ShapeDtypeStruct((8, D_A), F32),
                   jax.ShapeDtypeStruct((GROUPS, BLK, BLK), F32), jax.ShapeDtypeStruct((BLK, LANE), F32),
                   jax.ShapeDtypeStruct((8, LANE), F32)],
        scratch_shapes=[pltpu.VMEM((N_KV, 2 * BLK, LANE), BF16), pltpu.VMEM((N_KV, 2 * BLK, LANE), BF16),
                        pltpu.VMEM((BLK, D_A), F32),
                        pltpu.VMEM((N_KV, Q_PER_KV * BLK, LANE), F32), pltpu.VMEM((N_KV, Q_PER_KV * BLK, LANE), F32),
                        pltpu.VMEM((N_KV, 2 * BLK, LANE), F32), pltpu.VMEM((N_KV, 2 * BLK, LANE), F32),
                        pltpu.VMEM((BLK, D_KV), F32), pltpu.VMEM((BLK, D_KV), F32), pltpu.VMEM((BLK, D_B), F32)],
        compiler_params=_params("arbitrary"),
    )(proj, proj, proj, proj, proj, proj, proj, proj, proj, dy, probs, outs, psinks, qms, *tables, ln_g, ln_b,
      w_sp, w_sp_t, b_sp_t)


def _dh_call(dproj, w_bf, x, dx2, scale, norm_g):
    s = x.shape[0]
    tm = min(s, 512)
    tk = W_IN_SHARD
    nk = D_IN // tk

    def body(dp_ref, w_ref, x_ref, dx2_ref, sc_ref, g_ref, gx_ref, st_ref, acc_ref):
        i = pl.program_id(0)
        k = pl.program_id(1)

        @pl.when((i == 0) & (k == 0))
        def _():
            st_ref[...] = jnp.zeros((8, D), F32)

        @pl.when(k == 0)
        def _():
            acc_ref[...] = jnp.zeros((tm, D), F32)

        acc_ref[...] += lax.dot_general(dp_ref[...], w_ref[...], NT, preferred_element_type=F32)

        @pl.when(k == nk - 1)
        def _():
            g = g_ref[...]
            one_sc = 1.0 + sc_ref[...]

            def chunk(n, carry):
                rows = pl.ds(pl.multiple_of(n * BLK, BLK), BLK)
                dh = acc_ref[rows, :]
                xv = x_ref[rows, :]
                r = lax.rsqrt(jnp.mean(xv * xv, axis=-1, keepdims=True) + EPS)
                xn = xv * r
                dhn = dh * one_sc
                dxn = dhn * g
                gx_ref[rows, :] = dx2_ref[rows, :] + r * (dxn - xn * jnp.mean(dxn * xn, axis=-1, keepdims=True))
                st_ref[0:1, :] += jnp.sum(dh, axis=0, keepdims=True)
                st_ref[1:2, :] += jnp.sum(dh * (xn * g), axis=0, keepdims=True)
                st_ref[2:3, :] += jnp.sum(dhn * xn, axis=0, keepdims=True)
                return carry

            lax.fori_loop(0, tm // BLK, chunk, 0)

    vec = pl.BlockSpec((1, D), lambda i, k: (0, 0))
    rows = lambda: pl.BlockSpec((tm, D), lambda i, k: (i, 0))
    return pl.pallas_call(
        body, name="dh", grid=(s // tm, nk),
        in_specs=[pl.BlockSpec((tm, tk), lambda i, k: (i, k)), pl.BlockSpec((D, tk), lambda i, k: (0, k)), rows(), rows(), vec, vec],
        out_specs=[rows(), pl.BlockSpec((8, D), lambda i, k: (0, 0))],
        out_shape=[jax.ShapeDtypeStruct((s, D), F32), jax.ShapeDtypeStruct((8, D), F32)],
        scratch_shapes=[pltpu.VMEM((tm, D), F32)],
        compiler_params=_params("arbitrary", "arbitrary"),
    )(dproj, w_bf, x, dx2, scale, norm_g)


def _adam_math(w, g, m, v):
    m_new = ADAM_B1 * m + (1.0 - ADAM_B1) * g
    v_new = ADAM_B2 * v + (1.0 - ADAM_B2) * (g * g)
    m_hat = m_new / ADAM_C1
    v_hat = v_new / ADAM_C2
    delta = -ADAM_LR * (m_hat / (jnp.sqrt(v_hat) + ADAM_EPS) + ADAM_WD * w)
    return delta, m_new, v_new


def _adam_small_call(tensors):
    n = len(tensors)

    def body(*refs):
        ins, outs = refs[:4 * n], refs[4 * n:]
        for t in range(n):
            w_ref, g_ref, m_ref, v_ref = ins[4 * t:4 * t + 4]
            d, mo, vo = _adam_math(w_ref[...], g_ref[...], m_ref[...], v_ref[...])
            outs[3 * t][...], outs[3 * t + 1][...], outs[3 * t + 2][...] = d, mo, vo

    vm = pl.BlockSpec(memory_space=pltpu.VMEM)
    flat = [a for t in tensors for a in t]
    out = pl.pallas_call(
        body, name="adam_small", in_specs=[vm] * (4 * n), out_specs=[vm] * (3 * n),
        out_shape=[jax.ShapeDtypeStruct(t[0].shape, F32) for t in tensors for _ in range(3)],
        compiler_params=pltpu.CompilerParams(vmem_limit_bytes=VMEM_LIMIT),
    )(*flat)
    return [tuple(out[3 * t:3 * t + 3]) for t in range(n)]


def _adam_halves_call(pos, w, mine, theirs, m, v, name):
    r, n = w.shape
    half = r // 2
    tr = ADAM_ROWS
    nh = half // tr

    def body(pos_ref, w_ref, mine_ref, theirs_ref, m_ref, v_ref, g_ref, d_ref, mo_ref, vo_ref):
        is_mine = (pl.program_id(0) // nh) == pos_ref[1]
        g = jnp.where(is_mine, mine_ref[...], theirs_ref[...])
        g_ref[...] = g
        d_ref[...], mo_ref[...], vo_ref[...] = _adam_math(w_ref[...], g, m_ref[...], v_ref[...])

    spec = lambda: pl.BlockSpec((tr, n), lambda i, pos: (i, 0))

    def half_spec(core_of_half):
        def index(i, pos):
            first = core_of_half(pos) == 0
            active = (i // nh == 0) == first
            return jnp.where(active, i % nh, jnp.where(first, nh - 1, 0)), 0
        return pl.BlockSpec((tr, n), index)

    return pl.pallas_call(
        body, name=name,
        grid_spec=pltpu.PrefetchScalarGridSpec(
            num_scalar_prefetch=1, grid=(r // tr,),
            in_specs=[spec(), half_spec(lambda pos: pos[1]), half_spec(lambda pos: 1 - pos[1]), spec(), spec()],
            out_specs=[spec() for _ in range(4)]),
        out_shape=[jax.ShapeDtypeStruct((r, n), F32)] * 4, compiler_params=_params("arbitrary"),
    )(pos, w, mine, theirs, m, v)


def _adam_outer_call(w, ct, dm, m, v, name):
    r, n = w.shape
    tr = ADAM_ROWS

    def body(w_ref, ct_ref, dm_ref, m_ref, v_ref, g_ref, d_ref, mo_ref, vo_ref):
        g = ct_ref[:, 0:1] * dm_ref[0:1, :]
        for b in range(1, N_DEV):
            g = g + ct_ref[:, b:b + 1] * dm_ref[b:b + 1, :]
        g_ref[...] = g
        d_ref[...], mo_ref[...], vo_ref[...] = _adam_math(w_ref[...], g, m_ref[...], v_ref[...])

    spec = lambda: pl.BlockSpec((tr, n), lambda i: (i, 0))
    return pl.pallas_call(
        body, name=name, grid=(r // tr,),
        in_specs=[spec(), pl.BlockSpec((tr, N_DEV), lambda i: (i, 0)), pl.BlockSpec((N_DEV, n), lambda i: (0, 0)), spec(), spec()],
        out_specs=[spec() for _ in range(4)],
        out_shape=[jax.ShapeDtypeStruct((r, n), F32)] * 4, compiler_params=_params("parallel"),
    )(w, ct, dm, m, v)


def _sum_pieces_call(pos, part, part_block, recvs, name):
    r, n = recvs[0].shape[1:]
    tr = min(r, 256)
    nrb = r // tr

    def body(pos_ref, p_ref, *refs):
        acc = p_ref[...].astype(F32)
        for r_ref in refs[:-1]:
            for d in range(r_ref.shape[0]):
                acc = acc + r_ref[d].astype(F32)
        refs[-1][...] = acc

    return pl.pallas_call(
        body, name=name,
        grid_spec=pltpu.PrefetchScalarGridSpec(
            num_scalar_prefetch=1, grid=(nrb,),
            in_specs=[pl.BlockSpec((tr, n), lambda i, pos: part_block(i, pos, nrb))] + [
                pl.BlockSpec((rv.shape[0], tr, n), lambda i, pos: (0, i, 0)) for rv in recvs],
            out_specs=pl.BlockSpec((tr, n), lambda i, pos: (i, 0))),
        out_shape=jax.ShapeDtypeStruct((r, n), F32), compiler_params=_params("parallel"),
    )(pos, part, *recvs)


def _coords():
    return lax.axis_index("x"), lax.axis_index("y"), lax.axis_index("c")


CAST_ROWS = 256


def _allgather_sum_call(blk, name, with_sum, cast=None):
    m_per, n = blk.shape
    n_out = 2 if with_sum else 1
    if cast is not None:
        w, full_shape = cast
        wr, wn = w.shape
        by_cols = full_shape[0] == wr
        tr = min(wr, CAST_ROWS)
        n_chunk = wr // tr

    def body(*refs):
        x_ref = refs[0]
        out_ref = refs[1 + (cast is not None)]
        rest = refs[1 + (cast is not None) + n_out + (cast is not None):]
        send_sems, recv_sems, local_sem = rest[:3]
        x, y, c = _coords()
        me, sibling = (x, y, c), (x, y, 1 - c)
        chips = [(1 - x, y), (x, 1 - y), (1 - x, 1 - y)]

        def rows(px, py, pc):
            return out_ref.at[pl.ds((4 * px + 2 * py + pc) * m_per, m_per), :]

        def copy(k, block, to, src=None):
            return pltpu.make_async_remote_copy(
                src_ref=rows(*block) if src is None else src, dst_ref=rows(*block),
                send_sem=send_sems.at[k], recv_sem=recv_sems.at[k], device_id=to, device_id_type=MESH)

        mine = pltpu.make_async_copy(x_ref, rows(*me), local_sem)
        mine.start()
        first = [copy(0, me, sibling, src=x_ref)]
        first += [copy(1 + j, me, (*chip, c), src=x_ref) for j, chip in enumerate(chips)]
        for cp in first:
            cp.start()

        if cast is not None:
            w_ref, full_ref = refs[1], refs[1 + 1 + n_out]
            f32_buf, bf16_buf, in_sems, out_sems = rest[3:]
            chip_no = 2 * x + y

            def fetch(i):
                return pltpu.make_async_copy(w_ref.at[pl.ds(i * tr, tr), :], f32_buf.at[i % 2], in_sems.at[i % 2])

            def store(i):
                if by_cols:
                    dst = full_ref.at[pl.ds(i * tr, tr), pl.ds(chip_no * wn, wn)]
                else:
                    dst = full_ref.at[pl.ds(chip_no * wr + i * tr, tr), :]
                return pltpu.make_async_copy(bf16_buf.at[i % 2], dst, out_sems.at[i % 2])

            fetch(0).start()
            for i in range(n_chunk):
                if i + 1 < n_chunk:
                    fetch(i + 1).start()
                fetch(i).wait()
                if i >= 2:
                    store(i - 2).wait()
                bf16_buf[i % 2] = f32_buf[i % 2].astype(BF16)
                store(i).start()
            for i in range(max(n_chunk - 2, 0), n_chunk):
                store(i).wait()

        passed = [copy(4 + j, (*chip, c), sibling) for j, chip in enumerate(chips)]
        for j, chip in enumerate(chips):
            copy(1 + j, (*chip, c), me).wait_recv()
            passed[j].start()
        copy(0, sibling, me).wait_recv()
        for j, chip in enumerate(chips):
            copy(4 + j, (*chip, 1 - c), me).wait_recv()
        for cp in first + passed:
            cp.wait_send()
        mine.wait()
        if with_sum:
            sum_ref = refs[1 + (cast is not None) + 1]
            acc = out_ref[0:m_per, :]
            for d in range(1, N_DEV):
                acc = acc + out_ref[d * m_per:(d + 1) * m_per, :]
            sum_ref[...] = acc

    vm = pl.BlockSpec(memory_space=pltpu.VMEM)
    anyspec = pl.BlockSpec(memory_space=pl.ANY)
    out_shape = [jax.ShapeDtypeStruct((N_DEV * m_per, n), F32)]
    if with_sum:
        out_shape.append(jax.ShapeDtypeStruct((m_per, n), F32))
    in_specs, out_specs, operands = [vm], [vm] * n_out, [blk]
    scratch = [pltpu.SemaphoreType.DMA((7,)), pltpu.SemaphoreType.DMA((7,)), pltpu.SemaphoreType.DMA]
    if cast is not None:
        in_specs.append(anyspec)
        operands.append(w)
        out_shape.append(jax.ShapeDtypeStruct(full_shape, BF16))
        out_specs.append(anyspec)
        scratch += [pltpu.VMEM((2, tr, wn), F32), pltpu.VMEM((2, tr, wn), BF16), pltpu.SemaphoreType.DMA((2,)),
                    pltpu.SemaphoreType.DMA((2,))]
    return pl.pallas_call(
        body, name=name, out_shape=out_shape, in_specs=in_specs, out_specs=out_specs, scratch_shapes=scratch,
        compiler_params=pltpu.CompilerParams(vmem_limit_bytes=VMEM_LIMIT),
    )(*operands)


HBM_SPEC = pl.BlockSpec(memory_space=pltpu.HBM)
SEM_SPEC = pl.BlockSpec(memory_space=pltpu.SEMAPHORE)
SIDE_EFFECT = pltpu.SideEffectType.DATAFLOW_SIDE_EFFECTING


def _peer(x, y, c, q, cb):
    return (1 - x if q & 2 else x, 1 - y if q & 1 else y, 1 - c if cb else c)


def _w_in_piece(slots):
    def piece(part_ref, k, to):
        return part_ref.at[pl.ds(to[2] * (D // 2), D // 2), pl.ds(slots[k] * W_IN_SHARD, W_IN_SHARD)]
    return piece


def _w_out_piece(part_ref, k, to):
    ho = W_OUT_SHARD // 2
    return part_ref.at[pl.ds((2 * to[0] + to[1]) * W_OUT_SHARD + to[2] * ho, ho), :]


def _group_piece(part_ref, k, to):
    return part_ref.at[4 * to[0] + 2 * to[1] + to[2]]


def _whole_piece(part_ref, k, to):
    return part_ref


def _exchange_start_call(groups, name):
    ng = len(groups)
    lands = [lax.empty((len(rels),) + slot_shape, part.dtype) for part, rels, _, slot_shape in groups]

    def body(*refs):
        ins, outs = refs[:2 * ng], refs[2 * ng:]
        x, y, c = _coords()
        for g, (_, rels, piece, _) in enumerate(groups):
            part_ref, land_ref = ins[2 * g], ins[2 * g + 1]
            send_sems, recv_sems = outs[4 * g], outs[4 * g + 1]
            for k, (q, cb) in enumerate(rels):
                to = _peer(x, y, c, q, cb)
                pltpu.make_async_remote_copy(src_ref=piece(part_ref, k, to), dst_ref=land_ref.at[k], send_sem=send_sems.at[k],
                                             recv_sem=recv_sems.at[k], device_id=to, device_id_type=MESH).start()
        outs[-1][...] = jnp.zeros_like(outs[-1])

    out_shape, out_specs, operands = [], [], []
    for (part, rels, _, _), land in zip(groups, lands):
        n = len(rels)
        out_shape += [pltpu.SemaphoreType.DMA((n,)), pltpu.SemaphoreType.DMA((n,)), pltpu.HBM(part.shape, part.dtype),
                      pltpu.HBM(land.shape, land.dtype)]
        out_specs += [SEM_SPEC, SEM_SPEC, HBM_SPEC, HBM_SPEC]
        operands += [pltpu.with_memory_space_constraint(part, pltpu.HBM), pltpu.with_memory_space_constraint(land, pltpu.HBM)]
    out = pl.pallas_call(
        body, name=name,
        out_shape=tuple(out_shape) + (jax.ShapeDtypeStruct((1, 1), F32),),
        in_specs=(HBM_SPEC,) * (2 * ng), out_specs=tuple(out_specs) + (pl.BlockSpec(memory_space=pltpu.VMEM),),
        input_output_aliases={j: 4 * (j // 2) + 2 + j % 2 for j in range(2 * ng)},
        compiler_params=pltpu.CompilerParams(has_side_effects=SIDE_EFFECT),
    )(*operands)
    return [tuple(out[4 * g:4 * g + 4]) for g in range(ng)], out[-1]


def _exchange_wait_call(started, groups, after, name):
    ng = len(groups)

    def body(*refs):
        ins = refs[:4 * ng]
        x, y, c = _coords()
        for g, (_, rels, piece, _) in enumerate(groups):
            part_ref, land_ref, send_sems, recv_sems = ins[4 * g:4 * g + 4]
            for k, (q, cb) in enumerate(rels):
                to = _peer(x, y, c, q, cb)
                cp = pltpu.make_async_remote_copy(src_ref=piece(part_ref, k, to), dst_ref=land_ref.at[k], send_sem=send_sems.at[k],
                                                  recv_sem=recv_sems.at[k], device_id=to, device_id_type=MESH)
                cp.wait_send()
                cp.wait_recv()

    operands, in_specs, out_shape = [], [], []
    for send_sems, recv_sems, part_thru, land_thru in started:
        operands += [part_thru, land_thru, send_sems, recv_sems]
        in_specs += [HBM_SPEC, HBM_SPEC, SEM_SPEC, SEM_SPEC]
        out_shape += [pltpu.HBM(part_thru.shape, part_thru.dtype), pltpu.HBM(land_thru.shape, land_thru.dtype)]
    out = pl.pallas_call(
        body, name=name, out_shape=tuple(out_shape),
        in_specs=tuple(in_specs) + (pl.BlockSpec(memory_space=pl.ANY),), out_specs=(HBM_SPEC,) * (2 * ng),
        input_output_aliases={4 * g + j: 2 * g + j for g in range(ng) for j in range(2)},
        compiler_params=pltpu.CompilerParams(has_side_effects=SIDE_EFFECT),
    )(*operands, after)
    return [tuple(out[2 * g:2 * g + 2]) for g in range(ng)]


def _rope_tables(s):
    inv_freq = np.float32(10000.0) ** (-np.arange(0, HEAD, 2, dtype=np.float32) / np.float32(HEAD))
    ang = np.arange(s, dtype=np.float32)[:, None] * inv_freq[None, :]
    cos = np.tile(np.cos(ang), (1, LANE // (HEAD // 2))).astype(np.float32)
    sin = np.tile(np.sin(ang), (1, LANE // (HEAD // 2))).astype(np.float32)
    first_half = (np.arange(LANE) % HEAD) < (HEAD // 2)
    sin = np.where(first_half[None, :], -sin, sin)
    behind = lambda t: np.concatenate([t[:BLK], t[:-BLK]], axis=0)
    return tuple(jnp.asarray(t) for t in (cos, sin, behind(cos), behind(sin)))


def kernel(x, c, w_ada, b_ada, norm_g, w_in, ln_v_g, ln_v_b, w_spatial, b_spatial, sinks, w_out, w_ada_final, b_ada_final, final_norm_g, loss_target, m_w_ada, m_b_ada, m_norm_g, m_w_in, m_ln_v_g, m_ln_v_b, m_w_spatial, m_b_spatial, m_sinks, m_w_out, m_w_ada_final, m_b_ada_final, m_final_norm_g, v_w_ada, v_b_ada, v_norm_g, v_w_in, v_ln_v_g, v_ln_v_b, v_w_spatial, v_b_spatial, v_sinks, v_w_out, v_w_ada_final, v_b_ada_final, v_final_norm_g):
    s = x.shape[1]
    ax, ay, ac = _coords()
    chip = 2 * ax + ay
    me = 4 * ax + 2 * ay + ac
    n_ada = w_ada.shape[2]
    n_adaf = w_ada_final.shape[1]

    x2d = x.reshape(s, D)
    tgt = loss_target.reshape(s, D)
    w_ada2, w_in2, w_out2 = w_ada[0], w_in[0], w_out[0]
    b_ada_f2 = b_ada_final.reshape(1, 2 * D)
    gf = final_norm_g.reshape(1, D)

    c_all, w_in_own = _allgather_sum_call(jnp.pad(c, ((0, 7), (0, 0))), "gather_c", False, cast=(w_in2, (D, D_IN)))
    c_all = c_all[::8]
    mod_p, c_act = _rowmat_call(c_all, w_ada2, lax.dynamic_slice(b_ada, (0, chip * n_ada), (1, n_ada)), "mod")
    modf_p, _ = _rowmat_call(c_all, w_ada_final, lax.dynamic_slice(b_ada_f2, (0, chip * n_adaf), (1, n_adaf)), "mod_final")
    mods, w_out_own = _allgather_sum_call(jnp.concatenate([mod_p, modf_p], axis=1), "gather_mod", False, cast=(w_out2, (D, D)))
    my_rows = [lax.dynamic_slice(mods, (16 * j + me, 0), (1, n_ada + n_adaf)) for j in range(N_CHIP)]
    mod = jnp.concatenate([r[:, :n_ada] for r in my_rows], axis=1)
    mod_f = jnp.concatenate([r[:, n_ada:] for r in my_rows], axis=1)
    shift, scale, gate = mod[:, :D], mod[:, D:2 * D], mod[:, 2 * D:]
    shift_f, scale_f = mod_f[:, :D], mod_f[:, D:]

    pos = jnp.stack([chip, ac]).astype(jnp.int32)

    tables = _rope_tables(s)
    cos, sin = tables[:2]
    b_sp_t = b_spatial[0].T
    sinks1 = sinks.reshape(N_Q)
    h, proj, w_in_bf, w_out_bf = _proj_gather_call(pos, x2d, shift, scale, norm_g, w_in_own, w_out_own)
    y, probs, attn_out, psinks, qms = _mix_fwd_call(proj, cos, sin, ln_v_g, ln_v_b, w_spatial[0], b_sp_t, sinks1)
    dx2, do, dy, st_tail = _tail_call(y, w_out_bf, x2d, tgt, gate, shift_f, scale_f, gf)

    rel_o = [(0, 1), (1, 0), (1, 1), (2, 0), (2, 1), (3, 0), (3, 1)]
    rel_a = [(1, 0), (1, 1), (2, 0), (2, 1)]
    rel_b = [(3, 0), (3, 1), (0, 1)]
    piece_a, piece_b = _w_in_piece([0, 0, 1, 1]), _w_in_piece([0, 0, 1])
    half_in, half_out = (D // 2, W_IN_SHARD), (W_OUT_SHARD // 2, D)

    g_w_out_p = _tn_call(y, do, "grad_w_out")
    grp_o = [(g_w_out_p, rel_o, _w_out_piece, half_out)]
    st_o, tok_o = _exchange_start_call(grp_o, "send_w_out")
    dproj, st_ln, d_wsp, d_bsp_t, d_sink = _mix_bwd_call(
        proj, dy, probs, attn_out, psinks, qms, tables, ln_v_g + tok_o, ln_v_b, w_spatial[0], jnp.swapaxes(w_spatial[0], 1, 2),
        b_sp_t)
    g_w_in_a = _tn_shards_call(pos, h, dproj, (1, 2), "grad_w_in_a")
    grp_a = [(g_w_in_a, rel_a, piece_a, half_in), (d_wsp, rel_o, _group_piece, (BLK, BLK))]
    st_a, tok_a = _exchange_start_call(grp_a, "send_w_in_a")
    g_w_in_b = _tn_shards_call(pos, h, dproj, (3, 0), "grad_w_in_b")
    grp_b = [(g_w_in_b, rel_b, piece_b, half_in)]
    st_b, tok_b = _exchange_start_call(grp_b, "send_w_in_b")
    grad_x, st_dh = _dh_call(dproj, w_in_bf, x2d, dx2, scale + (tok_a + tok_b), norm_g)

    ((g_w_out_p, recv_o),) = _exchange_wait_call(st_o, grp_o, st_dh, "wait_w_out")
    (_, recv_a), (d_wsp, recv_s) = _exchange_wait_call(st_a, grp_a, st_dh, "wait_w_in_a")
    ((g_w_in_b, recv_b),) = _exchange_wait_call(st_b, grp_b, st_dh, "wait_w_in_b")
    mine_in = _sum_pieces_call(pos, g_w_in_b, lambda i, p, nrb: (p[1] * nrb + i, 1), [recv_a, recv_b], "sum_w_in")
    mine_out = _sum_pieces_call(pos, g_w_out_p, lambda i, p, nrb: ((2 * p[0] + p[1]) * nrb + i, 0), [recv_o], "sum_w_out")
    wsp_group = _sum_pieces_call(pos, d_wsp.reshape(GROUPS * BLK, BLK), lambda i, p, nrb: (2 * p[0] + p[1], 0), [recv_s],
                                 "sum_w_spatial")
    to_sibling = [(0, 1)]
    grp_p = [(mine_in, to_sibling, _whole_piece, half_in), (mine_out, to_sibling, _whole_piece, half_out)]
    st_p, tok_p = _exchange_start_call(grp_p, "swap_halves")

    misc = jnp.concatenate([st_ln, d_bsp_t[:, :GROUPS].T, d_sink, jnp.zeros((8, D - D_A - 2 * LANE), F32)], axis=1)
    pack = jnp.concatenate([wsp_group.reshape(8, D) + tok_p, st_tail, st_dh, misc], axis=0)
    rows = pack.shape[0]
    packs, tot = _allgather_sum_call(pack, "gather_small", True)
    packs = packs.reshape(N_DEV, rows, D)
    dmod_all = jnp.concatenate([packs[:, 16, :], packs[:, 17, :], packs[:, 11, :]], axis=1)
    dmodf_all = jnp.concatenate([packs[:, 8, :], packs[:, 9, :]], axis=1)
    loss = tot[13, 0]
    (mine_in, theirs_in), (mine_out, theirs_out) = _exchange_wait_call(st_p, grp_p, tot, "swapped_halves")
    small = {
        "b_ada": jnp.concatenate([tot[16:17], tot[17:18], tot[11:12]], axis=1),
        "norm_g": tot[18:19],
        "ln_v_g": tot[24:25, :D_A],
        "ln_v_b": tot[25:26, :D_A],
        "w_spatial": packs[:, 0:8, :].reshape(GROUPS * BLK, BLK),
        "b_spatial": tot[24:32, D_A:D_A + BLK],
        "sinks": tot[24:25, D_A + LANE:D_A + LANE + N_Q],
        "b_ada_final": jnp.concatenate([tot[8:9], tot[9:10]], axis=1),
        "final_norm_g": tot[10:11],
    }

    weights = dict(w_ada=w_ada, b_ada=b_ada, norm_g=norm_g, w_in=w_in, ln_v_g=ln_v_g, ln_v_b=ln_v_b, w_spatial=w_spatial,
                   b_spatial=b_spatial, sinks=sinks, w_out=w_out, w_ada_final=w_ada_final, b_ada_final=b_ada_final,
                   final_norm_g=final_norm_g)
    m_in = dict(w_ada=m_w_ada, b_ada=m_b_ada, norm_g=m_norm_g, w_in=m_w_in, ln_v_g=m_ln_v_g, ln_v_b=m_ln_v_b,
                w_spatial=m_w_spatial, b_spatial=m_b_spatial, sinks=m_sinks, w_out=m_w_out, w_ada_final=m_w_ada_final,
                b_ada_final=m_b_ada_final, final_norm_g=m_final_norm_g)
    v_in = dict(w_ada=v_w_ada, b_ada=v_b_ada, norm_g=v_norm_g, w_in=v_w_in, ln_v_g=v_ln_v_g, ln_v_b=v_ln_v_b,
                w_spatial=v_w_spatial, b_spatial=v_b_spatial, sinks=v_sinks, w_out=v_w_out, w_ada_final=v_w_ada_final,
                b_ada_final=v_b_ada_final, final_norm_g=v_final_norm_g)
    c_act_t = c_act.T
    outer = {"w_ada": lax.dynamic_slice(dmod_all, (0, chip * n_ada), (N_DEV, n_ada)),
             "w_ada_final": lax.dynamic_slice(dmodf_all, (0, chip * n_adaf), (N_DEV, n_adaf))}
    halves = {"w_in": (mine_in, theirs_in[0]), "w_out": (mine_out, theirs_out[0])}
    done = {}
    for name, (mine, theirs) in halves.items():
        shape2 = (2 * mine.shape[0], mine.shape[1])
        done[name] = _adam_halves_call(pos, weights[name].reshape(shape2), mine, theirs, m_in[name].reshape(shape2),
                                       v_in[name].reshape(shape2), "adam_" + name)
    for name, dm in outer.items():
        shape2 = (D, dm.shape[1])
        done[name] = _adam_outer_call(weights[name].reshape(shape2), c_act_t, dm, m_in[name].reshape(shape2),
                                      v_in[name].reshape(shape2), "adam_" + name)
    updates = _adam_small_call([(weights[name].reshape(g.shape), g, m_in[name].reshape(g.shape), v_in[name].reshape(g.shape))
                                for name, g in small.items()])
    for (name, g), upd in zip(small.items(), updates):
        done[name] = (g, *upd)
    outs = [[done[name][k].reshape(w.shape) for name, w in weights.items()] for k in range(4)]
    return (loss, grad_x.reshape(x.shape), *outs[0], *outs[1], *outs[2], *outs[3])
```
